```python
import math
import jax, jax.numpy as jnp
from jax import lax
import numpy as np

D_MODEL = 1024
BATCH = 8
SEQ = 4096
DEPTH = 2

S5_WIDTH = D_MODEL // 2
S5_GROUP = 16
S5_GROUPS = S5_WIDTH // S5_GROUP
S5_STATE = 64
FOX_HEAD_DIM = 64
FOX_HEADS = (D_MODEL - S5_WIDTH) // FOX_HEAD_DIM
FOX_WIDTH = FOX_HEADS * FOX_HEAD_DIM
Q_BLOCK = 128
EVEN_IN = S5_WIDTH + 3 * FOX_WIDTH + FOX_HEADS
EVEN_MIX = S5_WIDTH + FOX_WIDTH
POOL_WIDTH = D_MODEL // 2
POOL_WINDOWS = (2, 4, 8, 16)
POOL_GROUPS = len(POOL_WINDOWS)
POOL_GROUP_DIM = POOL_WIDTH // POOL_GROUPS
SGU_WIDTH = D_MODEL // 2
SGU_GROUPS = 4
SGU_GROUP_DIM = SGU_WIDTH // SGU_GROUPS
CHUNK = 128
ODD_IN = POOL_WIDTH + 2 * SGU_WIDTH
ODD_MIX = POOL_WIDTH + SGU_WIDTH
D_FF = 4 * D_MODEL
N_EVEN = (DEPTH + 1) // 2
N_ODD = DEPTH // 2
EPS = 1e-6

kernel_name = 'hybrid_s5_fox_pool_sgu'


def rms_norm(x, g):
    xf = x.astype(jnp.float32)
    y = xf * lax.rsqrt(jnp.mean(xf * xf, axis=-1, keepdims=True) + EPS)
    return (y * g.astype(jnp.float32)).astype(x.dtype)


def layer_norm(x, g, b):
    xf = x.astype(jnp.float32)
    mu = jnp.mean(xf, axis=-1, keepdims=True)
    xc = xf - mu
    y = xc * lax.rsqrt(jnp.mean(xc * xc, axis=-1, keepdims=True) + EPS)
    return (y * g.astype(jnp.float32) + b.astype(jnp.float32)).astype(x.dtype)


def _complex_scan_combine(e_i, e_j):
    ar_i, ai_i, br_i, bi_i = e_i
    ar_j, ai_j, br_j, bi_j = e_j
    ar = ar_j * ar_i - ai_j * ai_i
    ai = ar_j * ai_i + ai_j * ar_i
    br = ar_j * br_i - ai_j * bi_i + br_j
    bi = ar_j * bi_i + ai_j * br_i + bi_j
    return (ar, ai, br, bi)


def s5_mixer(u, lam_re, lam_im, log_dt, b_re, b_im, c_re, c_im, d, w_glu):
    f32 = jnp.float32
    bsz, L, _ = u.shape
    uf = u.astype(f32)
    dt = jnp.exp(log_dt.astype(f32))[:, None]
    lr = lam_re.astype(f32)
    li = lam_im.astype(f32)
    mag = jnp.exp(lr * dt)
    ab_re = mag * jnp.cos(li * dt)
    ab_im = mag * jnp.sin(li * dt)
    den = lr * lr + li * li
    nr = ab_re - 1.0
    ni = ab_im
    q_re = (nr * lr + ni * li) / den
    q_im = (ni * lr - nr * li) / den
    br = b_re.astype(f32)
    bi = b_im.astype(f32)
    bb_re = q_re[..., None] * br - q_im[..., None] * bi
    bb_im = q_re[..., None] * bi + q_im[..., None] * br
    ut = jnp.swapaxes(uf.reshape(bsz, L, S5_GROUPS, S5_GROUP), 0, 1)
    bu_re = jnp.einsum('lbgh,gph->lbgp', ut, bb_re)
    bu_im = jnp.einsum('lbgh,gph->lbgp', ut, bb_im)
    a_re = jnp.broadcast_to(ab_re[None, None], (L, 1, S5_GROUPS, S5_STATE))
    a_im = jnp.broadcast_to(ab_im[None, None], (L, 1, S5_GROUPS, S5_STATE))
    _, _, x_re, x_im = lax.associative_scan(_complex_scan_combine, (a_re, a_im, bu_re, bu_im), axis=0)
    y = (jnp.einsum('lbgp,ghp->lbgh', x_re, c_re.astype(f32))
         - jnp.einsum('lbgp,ghp->lbgh', x_im, c_im.astype(f32)))
    y = jnp.swapaxes(y, 0, 1).reshape(bsz, L, S5_WIDTH) + d.astype(f32) * uf
    y = jax.nn.gelu(y)
    y = y * jax.nn.sigmoid(y @ w_glu.astype(f32))
    return y.astype(u.dtype)


def fox_attention(q, k, v, f_logit, b_f):
    f32 = jnp.float32
    bsz, L, H, Dh = q.shape
    log_f = jax.nn.log_sigmoid(f_logit.astype(f32) + b_f.astype(f32))
    F = jnp.cumsum(log_f, axis=1)
    F_k = jnp.transpose(F, (0, 2, 1))
    n_blk = L // Q_BLOCK
    q_blocks = jnp.swapaxes(q.reshape(bsz, n_blk, Q_BLOCK, H, Dh), 0, 1)
    F_blocks = jnp.swapaxes(F_k.reshape(bsz, H, n_blk, Q_BLOCK), 0, 2).swapaxes(1, 2)
    k_pos = jnp.arange(L)
    scale = Dh ** -0.5

    def block(args):
        i, qi, Fi = args
        s = jnp.einsum('bqhd,bkhd->bhqk', qi, k).astype(f32) * scale
        s = s + (Fi[..., None] - F_k[:, :, None, :])
        q_pos = i * Q_BLOCK + jnp.arange(Q_BLOCK)
        mask = k_pos[None, :] <= q_pos[:, None]
        s = jnp.where(mask[None, None], s, -jnp.inf)
        p = jax.nn.softmax(s, axis=-1).astype(v.dtype)
        return jnp.einsum('bhqk,bkhd->bqhd', p, v)

    out = lax.map(block, (jnp.arange(n_blk), q_blocks, F_blocks))
    return jnp.swapaxes(out, 0, 1).reshape(bsz, L, H * Dh)


def pool_mixer(xc, pool_w, pool_scale):
    f32 = jnp.float32
    bsz, L, _ = xc.shape
    xg = xc.astype(f32).reshape(bsz, L, POOL_GROUPS, POOL_GROUP_DIM)
    csum = jnp.cumsum(xg, axis=1)
    t = jnp.arange(L, dtype=f32)
    outs = []
    for g, w in enumerate(POOL_WINDOWS):
        cg = csum[:, :, g]
        lagged = jnp.pad(cg, ((0, 0), (w, 0), (0, 0)))[:, :L]
        cnt = jnp.minimum(t + 1.0, float(w))[None, :, None]
        outs.append((cg - lagged) / cnt - xg[:, :, g])
    pooled = jnp.stack(outs, axis=2)
    y = jnp.einsum('blgc,gcd->blgd', pooled, pool_w.astype(f32)).reshape(bsz, L, POOL_WIDTH)
    return (y * pool_scale.astype(f32)).astype(xc.dtype)


def sgu_mixer(u, v, ln_g, ln_b, w_s, b_s):
    bsz, L, _ = u.shape
    u = jax.nn.gelu(u)
    v = layer_norm(jax.nn.gelu(v), ln_g, ln_b)
    n_chunk = L // CHUNK
    vg = v.reshape(bsz, n_chunk, CHUNK, SGU_GROUPS, SGU_GROUP_DIM)
    causal = jnp.tril(jnp.ones((CHUNK, CHUNK), dtype=bool))
    ws = jnp.where(causal[None], w_s, jnp.zeros_like(w_s))
    mixed = jnp.einsum('gts,bnsgc->bntgc', ws, vg) + jnp.transpose(b_s)[None, None, :, :, None]
    return u * mixed.reshape(bsz, L, SGU_WIDTH)


def even_mixer(h, w_in, lam_re, lam_im, log_dt, b_re, b_im, c_re, c_im, d, w_glu, b_f, w_out):
    bsz, L, _ = h.shape
    z = h @ w_in
    s1 = S5_WIDTH
    s2 = s1 + FOX_WIDTH
    s3 = s2 + FOX_WIDTH
    s4 = s3 + FOX_WIDTH
    u, q, k, v, fl = z[..., :s1], z[..., s1:s2], z[..., s2:s3], z[..., s3:s4], z[..., s4:]
    y_a = s5_mixer(u, lam_re, lam_im, log_dt, b_re, b_im, c_re, c_im, d, w_glu)
    shp = (bsz, L, FOX_HEADS, FOX_HEAD_DIM)
    y_b = fox_attention(q.reshape(shp), k.reshape(shp), v.reshape(shp), fl, b_f)
    return jnp.concatenate([y_a, y_b], axis=-1) @ w_out


def odd_mixer(h, w_in, pool_w, pool_scale, ln_g, ln_b, w_s, b_s, w_out):
    z = h @ w_in
    s1 = POOL_WIDTH
    s2 = s1 + SGU_WIDTH
    xc, u, v = z[..., :s1], z[..., s1:s2], z[..., s2:]
    y_c = pool_mixer(xc, pool_w, pool_scale)
    y_d = sgu_mixer(u, v, ln_g, ln_b, w_s, b_s)
    return jnp.concatenate([y_c, y_d], axis=-1) @ w_out


def sq_relu_mlp(h, w1, w2):
    return jnp.square(jax.nn.relu(h @ w1)) @ w2


def _fwd_setup_inputs(seed: int = 0) -> dict:
    key = jax.random.key(seed)
    ks = jax.random.split(key, 32)
    f32 = jnp.float32

    def nrm(k, shape, scale):
        return scale * jax.random.normal(k, shape, f32)

    G, P, H = S5_GROUPS, S5_STATE, S5_GROUP
    return {
        'x': nrm(ks[0], (BATCH, SEQ, D_MODEL), 1.0),
        'mix_pre_g': 1.0 + nrm(ks[1], (DEPTH, D_MODEL), 0.02),
        'mix_post_g': 1.0 + nrm(ks[2], (DEPTH, D_MODEL), 0.02),
        'mlp_pre_g': 1.0 + nrm(ks[3], (DEPTH, D_MODEL), 0.02),
        'mlp_post_g': 1.0 + nrm(ks[4], (DEPTH, D_MODEL), 0.02),
        'w_in_even': nrm(ks[5], (N_EVEN, D_MODEL, EVEN_IN), D_MODEL ** -0.5),
        's5_lam_re': -0.5 + nrm(ks[6], (N_EVEN, G, P), 0.01),
        's5_lam_im': jnp.pi * jnp.arange(P, dtype=f32) + nrm(ks[7], (N_EVEN, G, P), 0.01),
        's5_log_dt': jax.random.uniform(ks[8], (N_EVEN, G), f32, math.log(1e-3), math.log(1e-1)),
        's5_b_re': nrm(ks[9], (N_EVEN, G, P, H), (2 * H) ** -0.5),
        's5_b_im': nrm(ks[10], (N_EVEN, G, P, H), (2 * H) ** -0.5),
        's5_c_re': nrm(ks[11], (N_EVEN, G, H, P), P ** -0.5),
        's5_c_im': nrm(ks[12], (N_EVEN, G, H, P), P ** -0.5),
        's5_d': nrm(ks[13], (N_EVEN, S5_WIDTH), 1.0),
        's5_w_glu': nrm(ks[14], (N_EVEN, S5_WIDTH, S5_WIDTH), S5_WIDTH ** -0.5),
        'fox_b_f': jax.random.uniform(ks[15], (N_EVEN, FOX_HEADS), f32, 0.0, 3.0),
        'w_out_even': nrm(ks[16], (N_EVEN, EVEN_MIX, D_MODEL), EVEN_MIX ** -0.5),
        'w_in_odd': nrm(ks[17], (N_ODD, D_MODEL, ODD_IN), D_MODEL ** -0.5),
        'pool_w': nrm(ks[18], (N_ODD, POOL_GROUPS, POOL_GROUP_DIM, POOL_GROUP_DIM), POOL_GROUP_DIM ** -0.5),
        'pool_scale': 1.0 + nrm(ks[19], (N_ODD, POOL_WIDTH), 0.02),
        'sgu_ln_g': 1.0 + nrm(ks[20], (N_ODD, SGU_WIDTH), 0.02),
        'sgu_ln_b': nrm(ks[21], (N_ODD, SGU_WIDTH), 0.02),
        'sgu_w_s': nrm(ks[22], (N_ODD, SGU_GROUPS, CHUNK, CHUNK), CHUNK ** -0.5),
        'sgu_b_s': 1.0 + nrm(ks[23], (N_ODD, SGU_GROUPS, CHUNK), 0.1),
        'w_out_odd': nrm(ks[24], (N_ODD, ODD_MIX, D_MODEL), ODD_MIX ** -0.5),
        'mlp_w1': nrm(ks[25], (DEPTH, D_MODEL, D_FF), D_MODEL ** -0.5),
        'mlp_w2': nrm(ks[26], (DEPTH, D_FF, D_MODEL), D_FF ** -0.5),
    }


def _fwd_reference(x, mix_pre_g, mix_post_g, mlp_pre_g, mlp_post_g,
              w_in_even, s5_lam_re, s5_lam_im, s5_log_dt, s5_b_re, s5_b_im, s5_c_re, s5_c_im,
              s5_d, s5_w_glu, fox_b_f, w_out_even,
              w_in_odd, pool_w, pool_scale, sgu_ln_g, sgu_ln_b, sgu_w_s, sgu_b_s, w_out_odd,
              mlp_w1, mlp_w2):
    for l in range(DEPTH):
        h = rms_norm(x, mix_pre_g[l])
        if l % 2 == 0:
            e = l // 2
            y = even_mixer(h, w_in_even[e], s5_lam_re[e], s5_lam_im[e], s5_log_dt[e],
                           s5_b_re[e], s5_b_im[e], s5_c_re[e], s5_c_im[e], s5_d[e], s5_w_glu[e],
                           fox_b_f[e], w_out_even[e])
        else:
            o = l // 2
            y = odd_mixer(h, w_in_odd[o], pool_w[o], pool_scale[o], sgu_ln_g[o], sgu_ln_b[o],
                          sgu_w_s[o], sgu_b_s[o], w_out_odd[o])
        x = x + rms_norm(y, mix_post_g[l])
        h = rms_norm(x, mlp_pre_g[l])
        x = x + rms_norm(sq_relu_mlp(h, mlp_w1[l], mlp_w2[l]), mlp_post_g[l])
    return x


import jax as _jax
import jax.numpy as _jnp

TWIN_FORMAT = 'train_step'
FWD_PARAMS = ['x', 'mix_pre_g', 'mix_post_g', 'mlp_pre_g', 'mlp_post_g', 'w_in_even', 's5_lam_re', 's5_lam_im', 's5_log_dt', 's5_b_re', 's5_b_im', 's5_c_re', 's5_c_im', 's5_d', 's5_w_glu', 'fox_b_f', 'w_out_even', 'w_in_odd', 'pool_w', 'pool_scale', 'sgu_ln_g', 'sgu_ln_b', 'sgu_w_s', 'sgu_b_s', 'w_out_odd', 'mlp_w1', 'mlp_w2']
TWIN_WEIGHTS = ['mix_pre_g', 'mix_post_g', 'mlp_pre_g', 'mlp_post_g', 'w_in_even', 's5_lam_re', 's5_lam_im', 's5_log_dt', 's5_b_re', 's5_b_im', 's5_c_re', 's5_c_im', 's5_d', 's5_w_glu', 'fox_b_f', 'w_out_even', 'w_in_odd', 'pool_w', 'pool_scale', 'sgu_ln_g', 'sgu_ln_b', 'sgu_w_s', 'sgu_b_s', 'w_out_odd', 'mlp_w1', 'mlp_w2']
TWIN_DIFF_INPUT = 'x'
TWIN_INPUTS = ['x', 'mix_pre_g', 'mix_post_g', 'mlp_pre_g', 'mlp_post_g', 'w_in_even', 's5_lam_re', 's5_lam_im', 's5_log_dt', 's5_b_re', 's5_b_im', 's5_c_re', 's5_c_im', 's5_d', 's5_w_glu', 'fox_b_f', 'w_out_even', 'w_in_odd', 'pool_w', 'pool_scale', 'sgu_ln_g', 'sgu_ln_b', 'sgu_w_s', 'sgu_b_s', 'w_out_odd', 'mlp_w1', 'mlp_w2', 'loss_target', 'm_mix_pre_g', 'm_mix_post_g', 'm_mlp_pre_g', 'm_mlp_post_g', 'm_w_in_even', 'm_s5_lam_re', 'm_s5_lam_im', 'm_s5_log_dt', 'm_s5_b_re', 'm_s5_b_im', 'm_s5_c_re', 'm_s5_c_im', 'm_s5_d', 'm_s5_w_glu', 'm_fox_b_f', 'm_w_out_even', 'm_w_in_odd', 'm_pool_w', 'm_pool_scale', 'm_sgu_ln_g', 'm_sgu_ln_b', 'm_sgu_w_s', 'm_sgu_b_s', 'm_w_out_odd', 'm_mlp_w1', 'm_mlp_w2', 'v_mix_pre_g', 'v_mix_post_g', 'v_mlp_pre_g', 'v_mlp_post_g', 'v_w_in_even', 'v_s5_lam_re', 'v_s5_lam_im', 'v_s5_log_dt', 'v_s5_b_re', 'v_s5_b_im', 'v_s5_c_re', 'v_s5_c_im', 'v_s5_d', 'v_s5_w_glu', 'v_fox_b_f', 'v_w_out_even', 'v_w_in_odd', 'v_pool_w', 'v_pool_scale', 'v_sgu_ln_g', 'v_sgu_ln_b', 'v_sgu_w_s', 'v_sgu_b_s', 'v_w_out_odd', 'v_mlp_w1', 'v_mlp_w2']
TWIN_OUTPUTS = ['loss', 'grad_x', 'grad_mix_pre_g', 'grad_mix_post_g', 'grad_mlp_pre_g', 'grad_mlp_post_g', 'grad_w_in_even', 'grad_s5_lam_re', 'grad_s5_lam_im', 'grad_s5_log_dt', 'grad_s5_b_re', 'grad_s5_b_im', 'grad_s5_c_re', 'grad_s5_c_im', 'grad_s5_d', 'grad_s5_w_glu', 'grad_fox_b_f', 'grad_w_out_even', 'grad_w_in_odd', 'grad_pool_w', 'grad_pool_scale', 'grad_sgu_ln_g', 'grad_sgu_ln_b', 'grad_sgu_w_s', 'grad_sgu_b_s', 'grad_w_out_odd', 'grad_mlp_w1', 'grad_mlp_w2', 'delta_mix_pre_g', 'delta_mix_post_g', 'delta_mlp_pre_g', 'delta_mlp_post_g', 'delta_w_in_even', 'delta_s5_lam_re', 'delta_s5_lam_im', 'delta_s5_log_dt', 'delta_s5_b_re', 'delta_s5_b_im', 'delta_s5_c_re', 'delta_s5_c_im', 'delta_s5_d', 'delta_s5_w_glu', 'delta_fox_b_f', 'delta_w_out_even', 'delta_w_in_odd', 'delta_pool_w', 'delta_pool_scale', 'delta_sgu_ln_g', 'delta_sgu_ln_b', 'delta_sgu_w_s', 'delta_sgu_b_s', 'delta_w_out_odd', 'delta_mlp_w1', 'delta_mlp_w2', 'new_m_mix_pre_g', 'new_m_mix_post_g', 'new_m_mlp_pre_g', 'new_m_mlp_post_g', 'new_m_w_in_even', 'new_m_s5_lam_re', 'new_m_s5_lam_im', 'new_m_s5_log_dt', 'new_m_s5_b_re', 'new_m_s5_b_im', 'new_m_s5_c_re', 'new_m_s5_c_im', 'new_m_s5_d', 'new_m_s5_w_glu', 'new_m_fox_b_f', 'new_m_w_out_even', 'new_m_w_in_odd', 'new_m_pool_w', 'new_m_pool_scale', 'new_m_sgu_ln_g', 'new_m_sgu_ln_b', 'new_m_sgu_w_s', 'new_m_sgu_b_s', 'new_m_w_out_odd', 'new_m_mlp_w1', 'new_m_mlp_w2', 'new_v_mix_pre_g', 'new_v_mix_post_g', 'new_v_mlp_pre_g', 'new_v_mlp_post_g', 'new_v_w_in_even', 'new_v_s5_lam_re', 'new_v_s5_lam_im', 'new_v_s5_log_dt', 'new_v_s5_b_re', 'new_v_s5_b_im', 'new_v_s5_c_re', 'new_v_s5_c_im', 'new_v_s5_d', 'new_v_s5_w_glu', 'new_v_fox_b_f', 'new_v_w_out_even', 'new_v_w_in_odd', 'new_v_pool_w', 'new_v_pool_scale', 'new_v_sgu_ln_g', 'new_v_sgu_ln_b', 'new_v_sgu_w_s', 'new_v_sgu_b_s', 'new_v_w_out_odd', 'new_v_mlp_w1', 'new_v_mlp_w2']
TWIN_LEAF_KINDS = {'loss': 'loss', 'grad_x': 'grad_x', 'grad_mix_pre_g': 'grad_w', 'grad_mix_post_g': 'grad_w', 'grad_mlp_pre_g': 'grad_w', 'grad_mlp_post_g': 'grad_w', 'grad_w_in_even': 'grad_w', 'grad_s5_lam_re': 'grad_w', 'grad_s5_lam_im': 'grad_w', 'grad_s5_log_dt': 'grad_w', 'grad_s5_b_re': 'grad_w', 'grad_s5_b_im': 'grad_w', 'grad_s5_c_re': 'grad_w', 'grad_s5_c_im': 'grad_w', 'grad_s5_d': 'grad_w', 'grad_s5_w_glu': 'grad_w', 'grad_fox_b_f': 'grad_w', 'grad_w_out_even': 'grad_w', 'grad_w_in_odd': 'grad_w', 'grad_pool_w': 'grad_w', 'grad_pool_scale': 'grad_w', 'grad_sgu_ln_g': 'grad_w', 'grad_sgu_ln_b': 'grad_w', 'grad_sgu_w_s': 'grad_w', 'grad_sgu_b_s': 'grad_w', 'grad_w_out_odd': 'grad_w', 'grad_mlp_w1': 'grad_w', 'grad_mlp_w2': 'grad_w', 'delta_mix_pre_g': 'delta_w', 'delta_mix_post_g': 'delta_w', 'delta_mlp_pre_g': 'delta_w', 'delta_mlp_post_g': 'delta_w', 'delta_w_in_even': 'delta_w', 'delta_s5_lam_re': 'delta_w', 'delta_s5_lam_im': 'delta_w', 'delta_s5_log_dt': 'delta_w', 'delta_s5_b_re': 'delta_w', 'delta_s5_b_im': 'delta_w', 'delta_s5_c_re': 'delta_w', 'delta_s5_c_im': 'delta_w', 'delta_s5_d': 'delta_w', 'delta_s5_w_glu': 'delta_w', 'delta_fox_b_f': 'delta_w', 'delta_w_out_even': 'delta_w', 'delta_w_in_odd': 'delta_w', 'delta_pool_w': 'delta_w', 'delta_pool_scale': 'delta_w', 'delta_sgu_ln_g': 'delta_w', 'delta_sgu_ln_b': 'delta_w', 'delta_sgu_w_s': 'delta_w', 'delta_sgu_b_s': 'delta_w', 'delta_w_out_odd': 'delta_w', 'delta_mlp_w1': 'delta_w', 'delta_mlp_w2': 'delta_w', 'new_m_mix_pre_g': 'new_m', 'new_m_mix_post_g': 'new_m', 'new_m_mlp_pre_g': 'new_m', 'new_m_mlp_post_g': 'new_m', 'new_m_w_in_even': 'new_m', 'new_m_s5_lam_re': 'new_m', 'new_m_s5_lam_im': 'new_m', 'new_m_s5_log_dt': 'new_m', 'new_m_s5_b_re': 'new_m', 'new_m_s5_b_im': 'new_m', 'new_m_s5_c_re': 'new_m', 'new_m_s5_c_im': 'new_m', 'new_m_s5_d': 'new_m', 'new_m_s5_w_glu': 'new_m', 'new_m_fox_b_f': 'new_m', 'new_m_w_out_even': 'new_m', 'new_m_w_in_odd': 'new_m', 'new_m_pool_w': 'new_m', 'new_m_pool_scale': 'new_m', 'new_m_sgu_ln_g': 'new_m', 'new_m_sgu_ln_b': 'new_m', 'new_m_sgu_w_s': 'new_m', 'new_m_sgu_b_s': 'new_m', 'new_m_w_out_odd': 'new_m', 'new_m_mlp_w1': 'new_m', 'new_m_mlp_w2': 'new_m', 'new_v_mix_pre_g': 'new_v', 'new_v_mix_post_g': 'new_v', 'new_v_mlp_pre_g': 'new_v', 'new_v_mlp_post_g': 'new_v', 'new_v_w_in_even': 'new_v', 'new_v_s5_lam_re': 'new_v', 'new_v_s5_lam_im': 'new_v', 'new_v_s5_log_dt': 'new_v', 'new_v_s5_b_re': 'new_v', 'new_v_s5_b_im': 'new_v', 'new_v_s5_c_re': 'new_v', 'new_v_s5_c_im': 'new_v', 'new_v_s5_d': 'new_v', 'new_v_s5_w_glu': 'new_v', 'new_v_fox_b_f': 'new_v', 'new_v_w_out_even': 'new_v', 'new_v_w_in_odd': 'new_v', 'new_v_pool_w': 'new_v', 'new_v_pool_scale': 'new_v', 'new_v_sgu_ln_g': 'new_v', 'new_v_sgu_ln_b': 'new_v', 'new_v_sgu_w_s': 'new_v', 'new_v_sgu_b_s': 'new_v', 'new_v_w_out_odd': 'new_v', 'new_v_mlp_w1': 'new_v', 'new_v_mlp_w2': 'new_v'}


def _forward(args):
    return _fwd_reference(*[args[k] for k in FWD_PARAMS])


def _output_shape():
    def fwd():
        inp = _fwd_setup_inputs(0)
        return _fwd_reference(*[inp[k] for k in FWD_PARAMS])
    out = _jax.eval_shape(fwd)
    return out.shape, out.dtype

N_MICROBATCH = 1
ADAM_LR = 0.001
ADAM_B1 = 0.9
ADAM_B2 = 0.999
ADAM_EPS = 1e-08
ADAM_WD = 0.01
ADAM_STEP = 10
PER_EXAMPLE_BATCH_AXIS = {'x': 0, 'loss_target': 0}
SHARED_INPUTS = []
_WEIGHT_DTYPES = {'mix_pre_g': _jnp.float32, 'mix_post_g': _jnp.float32, 'mlp_pre_g': _jnp.float32, 'mlp_post_g': _jnp.float32, 'w_in_even': _jnp.float32, 's5_lam_re': _jnp.float32, 's5_lam_im': _jnp.float32, 's5_log_dt': _jnp.float32, 's5_b_re': _jnp.float32, 's5_b_im': _jnp.float32, 's5_c_re': _jnp.float32, 's5_c_im': _jnp.float32, 's5_d': _jnp.float32, 's5_w_glu': _jnp.float32, 'fox_b_f': _jnp.float32, 'w_out_even': _jnp.float32, 'w_in_odd': _jnp.float32, 'pool_w': _jnp.float32, 'pool_scale': _jnp.float32, 'sgu_ln_g': _jnp.float32, 'sgu_ln_b': _jnp.float32, 'sgu_w_s': _jnp.float32, 'sgu_b_s': _jnp.float32, 'w_out_odd': _jnp.float32, 'mlp_w1': _jnp.float32, 'mlp_w2': _jnp.float32}
MOMENT_SCALE = {'mix_pre_g': 2.493807e+00, 'mix_post_g': 3.327302e+01, 'mlp_pre_g': 3.983127e+00, 'mlp_post_g': 3.462774e+01, 'w_in_even': 9.045492e-01, 's5_lam_re': 5.226216e-02, 's5_lam_im': 3.911706e-02, 's5_log_dt': 2.746889e+01, 's5_b_re': 3.452918e-02, 's5_b_im': 3.402524e-02, 's5_c_re': 4.633525e-02, 's5_c_im': 5.090986e-02, 's5_d': 1.066771e+01, 's5_w_glu': 1.701570e+00, 'fox_b_f': 6.802268e+00, 'w_out_even': 7.274733e+00, 'w_in_odd': 2.859024e+00, 'pool_w': 2.019965e+00, 'pool_scale': 1.984818e+00, 'sgu_ln_g': 3.516057e-01, 'sgu_ln_b': 4.763705e-01, 'sgu_w_s': 2.617883e-01, 'sgu_b_s': 6.489219e-01, 'w_out_odd': 6.636540e+00, 'mlp_w1': 1.858760e+00, 'mlp_w2': 8.929107e+00}


def _to_microbatches(a, axis):
    t = _jnp.moveaxis(a, axis, 0)
    t = t.reshape((N_MICROBATCH, t.shape[0] // N_MICROBATCH) + t.shape[1:])
    return _jnp.moveaxis(t, 1, axis + 1)


def setup_inputs(seed: int = 0) -> dict:
    inp = _fwd_setup_inputs(seed)
    key = _jax.random.fold_in(_jax.random.key(seed), 7919)
    shape, _ = _output_shape()
    out = dict(inp)
    out["loss_target"] = _jax.random.normal(_jax.random.fold_in(key, 0), shape, _jnp.float32)
    for i, name in enumerate(TWIN_WEIGHTS):
        w = inp[name].astype(_jnp.float32)
        if MOMENT_SCALE is None:
            s = _jnp.sqrt(_jnp.mean(_jnp.square(w)) + 1e-30)
        else:
            s = MOMENT_SCALE[name]
        km, kv = _jax.random.split(_jax.random.fold_in(key, i + 1))
        out[name] = w
        out["m_" + name] = s * _jax.random.normal(km, w.shape, _jnp.float32)
        out["v_" + name] = (s * s) * _jax.random.uniform(kv, w.shape, _jnp.float32, 0.5, 1.5)
    if N_MICROBATCH > 1:
        for name, axis in PER_EXAMPLE_BATCH_AXIS.items():
            out[name] = _to_microbatches(out[name], axis)
    return {'x': out['x'], 'mix_pre_g': out['mix_pre_g'], 'mix_post_g': out['mix_post_g'], 'mlp_pre_g': out['mlp_pre_g'], 'mlp_post_g': out['mlp_post_g'], 'w_in_even': out['w_in_even'], 's5_lam_re': out['s5_lam_re'], 's5_lam_im': out['s5_lam_im'], 's5_log_dt': out['s5_log_dt'], 's5_b_re': out['s5_b_re'], 's5_b_im': out['s5_b_im'], 's5_c_re': out['s5_c_re'], 's5_c_im': out['s5_c_im'], 's5_d': out['s5_d'], 's5_w_glu': out['s5_w_glu'], 'fox_b_f': out['fox_b_f'], 'w_out_even': out['w_out_even'], 'w_in_odd': out['w_in_odd'], 'pool_w': out['pool_w'], 'pool_scale': out['pool_scale'], 'sgu_ln_g': out['sgu_ln_g'], 'sgu_ln_b': out['sgu_ln_b'], 'sgu_w_s': out['sgu_w_s'], 'sgu_b_s': out['sgu_b_s'], 'w_out_odd': out['w_out_odd'], 'mlp_w1': out['mlp_w1'], 'mlp_w2': out['mlp_w2'], 'loss_target': out['loss_target'], 'm_mix_pre_g': out['m_mix_pre_g'], 'm_mix_post_g': out['m_mix_post_g'], 'm_mlp_pre_g': out['m_mlp_pre_g'], 'm_mlp_post_g': out['m_mlp_post_g'], 'm_w_in_even': out['m_w_in_even'], 'm_s5_lam_re': out['m_s5_lam_re'], 'm_s5_lam_im': out['m_s5_lam_im'], 'm_s5_log_dt': out['m_s5_log_dt'], 'm_s5_b_re': out['m_s5_b_re'], 'm_s5_b_im': out['m_s5_b_im'], 'm_s5_c_re': out['m_s5_c_re'], 'm_s5_c_im': out['m_s5_c_im'], 'm_s5_d': out['m_s5_d'], 'm_s5_w_glu': out['m_s5_w_glu'], 'm_fox_b_f': out['m_fox_b_f'], 'm_w_out_even': out['m_w_out_even'], 'm_w_in_odd': out['m_w_in_odd'], 'm_pool_w': out['m_pool_w'], 'm_pool_scale': out['m_pool_scale'], 'm_sgu_ln_g': out['m_sgu_ln_g'], 'm_sgu_ln_b': out['m_sgu_ln_b'], 'm_sgu_w_s': out['m_sgu_w_s'], 'm_sgu_b_s': out['m_sgu_b_s'], 'm_w_out_odd': out['m_w_out_odd'], 'm_mlp_w1': out['m_mlp_w1'], 'm_mlp_w2': out['m_mlp_w2'], 'v_mix_pre_g': out['v_mix_pre_g'], 'v_mix_post_g': out['v_mix_post_g'], 'v_mlp_pre_g': out['v_mlp_pre_g'], 'v_mlp_post_g': out['v_mlp_post_g'], 'v_w_in_even': out['v_w_in_even'], 'v_s5_lam_re': out['v_s5_lam_re'], 'v_s5_lam_im': out['v_s5_lam_im'], 'v_s5_log_dt': out['v_s5_log_dt'], 'v_s5_b_re': out['v_s5_b_re'], 'v_s5_b_im': out['v_s5_b_im'], 'v_s5_c_re': out['v_s5_c_re'], 'v_s5_c_im': out['v_s5_c_im'], 'v_s5_d': out['v_s5_d'], 'v_s5_w_glu': out['v_s5_w_glu'], 'v_fox_b_f': out['v_fox_b_f'], 'v_w_out_even': out['v_w_out_even'], 'v_w_in_odd': out['v_w_in_odd'], 'v_pool_w': out['v_pool_w'], 'v_pool_scale': out['v_pool_scale'], 'v_sgu_ln_g': out['v_sgu_ln_g'], 'v_sgu_ln_b': out['v_sgu_ln_b'], 'v_sgu_w_s': out['v_sgu_w_s'], 'v_sgu_b_s': out['v_sgu_b_s'], 'v_w_out_odd': out['v_w_out_odd'], 'v_mlp_w1': out['v_mlp_w1'], 'v_mlp_w2': out['v_mlp_w2']}


def _loss(weights, diff, rest, loss_target):
    with _jax.named_scope("forward"):
        args = {**rest, TWIN_DIFF_INPUT: diff, **{k: w.astype(_WEIGHT_DTYPES[k]) for k, w in weights.items()}}
        y = _forward(args)
    with _jax.named_scope("loss_head"):
        err = _jnp.square(y.astype(_jnp.float32) - loss_target)
        return 0.5 * _jnp.sum(_jnp.mean(err, axis=-1)) if err.ndim else 0.5 * err


def _adamw(w, g, m, v):
    m = ADAM_B1 * m + (1.0 - ADAM_B1) * g
    v = ADAM_B2 * v + (1.0 - ADAM_B2) * _jnp.square(g)
    m_hat = m / (1.0 - ADAM_B1 ** ADAM_STEP)
    v_hat = v / (1.0 - ADAM_B2 ** ADAM_STEP)
    delta = -ADAM_LR * (m_hat / (_jnp.sqrt(v_hat) + ADAM_EPS) + ADAM_WD * w)
    return delta, m, v


def reference(x, mix_pre_g, mix_post_g, mlp_pre_g, mlp_post_g, w_in_even, s5_lam_re, s5_lam_im, s5_log_dt, s5_b_re, s5_b_im, s5_c_re, s5_c_im, s5_d, s5_w_glu, fox_b_f, w_out_even, w_in_odd, pool_w, pool_scale, sgu_ln_g, sgu_ln_b, sgu_w_s, sgu_b_s, w_out_odd, mlp_w1, mlp_w2, loss_target, m_mix_pre_g, m_mix_post_g, m_mlp_pre_g, m_mlp_post_g, m_w_in_even, m_s5_lam_re, m_s5_lam_im, m_s5_log_dt, m_s5_b_re, m_s5_b_im, m_s5_c_re, m_s5_c_im, m_s5_d, m_s5_w_glu, m_fox_b_f, m_w_out_even, m_w_in_odd, m_pool_w, m_pool_scale, m_sgu_ln_g, m_sgu_ln_b, m_sgu_w_s, m_sgu_b_s, m_w_out_odd, m_mlp_w1, m_mlp_w2, v_mix_pre_g, v_mix_post_g, v_mlp_pre_g, v_mlp_post_g, v_w_in_even, v_s5_lam_re, v_s5_lam_im, v_s5_log_dt, v_s5_b_re, v_s5_b_im, v_s5_c_re, v_s5_c_im, v_s5_d, v_s5_w_glu, v_fox_b_f, v_w_out_even, v_w_in_odd, v_pool_w, v_pool_scale, v_sgu_ln_g, v_sgu_ln_b, v_sgu_w_s, v_sgu_b_s, v_w_out_odd, v_mlp_w1, v_mlp_w2):
    given = dict(x=x, mix_pre_g=mix_pre_g, mix_post_g=mix_post_g, mlp_pre_g=mlp_pre_g, mlp_post_g=mlp_post_g, w_in_even=w_in_even, s5_lam_re=s5_lam_re, s5_lam_im=s5_lam_im, s5_log_dt=s5_log_dt, s5_b_re=s5_b_re, s5_b_im=s5_b_im, s5_c_re=s5_c_re, s5_c_im=s5_c_im, s5_d=s5_d, s5_w_glu=s5_w_glu, fox_b_f=fox_b_f, w_out_even=w_out_even, w_in_odd=w_in_odd, pool_w=pool_w, pool_scale=pool_scale, sgu_ln_g=sgu_ln_g, sgu_ln_b=sgu_ln_b, sgu_w_s=sgu_w_s, sgu_b_s=sgu_b_s, w_out_odd=w_out_odd, mlp_w1=mlp_w1, mlp_w2=mlp_w2, loss_target=loss_target, m_mix_pre_g=m_mix_pre_g, m_mix_post_g=m_mix_post_g, m_mlp_pre_g=m_mlp_pre_g, m_mlp_post_g=m_mlp_post_g, m_w_in_even=m_w_in_even, m_s5_lam_re=m_s5_lam_re, m_s5_lam_im=m_s5_lam_im, m_s5_log_dt=m_s5_log_dt, m_s5_b_re=m_s5_b_re, m_s5_b_im=m_s5_b_im, m_s5_c_re=m_s5_c_re, m_s5_c_im=m_s5_c_im, m_s5_d=m_s5_d, m_s5_w_glu=m_s5_w_glu, m_fox_b_f=m_fox_b_f, m_w_out_even=m_w_out_even, m_w_in_odd=m_w_in_odd, m_pool_w=m_pool_w, m_pool_scale=m_pool_scale, m_sgu_ln_g=m_sgu_ln_g, m_sgu_ln_b=m_sgu_ln_b, m_sgu_w_s=m_sgu_w_s, m_sgu_b_s=m_sgu_b_s, m_w_out_odd=m_w_out_odd, m_mlp_w1=m_mlp_w1, m_mlp_w2=m_mlp_w2, v_mix_pre_g=v_mix_pre_g, v_mix_post_g=v_mix_post_g, v_mlp_pre_g=v_mlp_pre_g, v_mlp_post_g=v_mlp_post_g, v_w_in_even=v_w_in_even, v_s5_lam_re=v_s5_lam_re, v_s5_lam_im=v_s5_lam_im, v_s5_log_dt=v_s5_log_dt, v_s5_b_re=v_s5_b_re, v_s5_b_im=v_s5_b_im, v_s5_c_re=v_s5_c_re, v_s5_c_im=v_s5_c_im, v_s5_d=v_s5_d, v_s5_w_glu=v_s5_w_glu, v_fox_b_f=v_fox_b_f, v_w_out_even=v_w_out_even, v_w_in_odd=v_w_in_odd, v_pool_w=v_pool_w, v_pool_scale=v_pool_scale, v_sgu_ln_g=v_sgu_ln_g, v_sgu_ln_b=v_sgu_ln_b, v_sgu_w_s=v_sgu_w_s, v_sgu_b_s=v_sgu_b_s, v_w_out_odd=v_w_out_odd, v_mlp_w1=v_mlp_w1, v_mlp_w2=v_mlp_w2)
    weights = {n: given[n] for n in TWIN_WEIGHTS}
    shared = {n: given[n] for n in SHARED_INPUTS}
    per_example = {n: given[n] for n in ['x']}
    grad_fn = _jax.value_and_grad(_loss, argnums=(0, 1))

    def one_microbatch(ex, loss_target):
        ex = dict(ex)
        diff = ex.pop(TWIN_DIFF_INPUT)
        return grad_fn(weights, diff, {**shared, **ex}, loss_target)

    if N_MICROBATCH == 1:
        loss, (grad_w, grad_x) = one_microbatch(per_example, given["loss_target"])
    else:
        def body(carry, xs):
            loss_sum, grad_sum = carry
            l_k, (gw_k, gx_k) = one_microbatch(xs[0], xs[1])
            with _jax.named_scope("update"):
                return (loss_sum + l_k, _jax.tree.map(_jnp.add, grad_sum, gw_k)), gx_k

        init = (_jnp.zeros((), _jnp.float32), _jax.tree.map(_jnp.zeros_like, weights))
        (loss, grad_w), grad_x = _jax.lax.scan(body, init, (per_example, given["loss_target"]))
    with _jax.named_scope("update"):
        delta_w, new_m, new_v = {}, {}, {}
        for n in TWIN_WEIGHTS:
            delta_w[n], new_m[n], new_v[n] = _adamw(weights[n], grad_w[n], given["m_" + n], given["v_" + n])
    return (loss, grad_x, *[grad_w[n] for n in TWIN_WEIGHTS], *[delta_w[n] for n in TWIN_WEIGHTS],
            *[new_m[n] for n in TWIN_WEIGHTS], *[new_v[n] for n in TWIN_WEIGHTS])
```

```python
import functools
import math

import jax
import jax.numpy as jnp
from jax import lax
from jax.experimental import pallas as pl
from jax.experimental.pallas import tpu as pltpu

F32 = jnp.float32
MXU_DTYPE = jnp.bfloat16
WIRE_DTYPE = jnp.bfloat16
EPS = 1e-6
VMEM_LIMIT_BYTES = 48 * 1024 * 1024
LANES = 128
SUBLANES = 8

D_MODEL = 1024
S5_WIDTH = 512
S5_GROUP = 16
S5_GROUPS = 32
S5_STATE = 64
S5_LANES = S5_GROUPS * S5_STATE
FOX_HEADS = 8
FOX_HEAD_DIM = 64
FOX_WIDTH = 512
EVEN_IN = S5_WIDTH + 3 * FOX_WIDTH + FOX_HEADS
EVEN_IN_PAD = 2176
POOL_WINDOWS = (2, 4, 8, 16)
POOL_HALO = 16
POOL_GROUP_DIM = 128
SGU_GROUPS = 4
SGU_GROUP_DIM = 128
CHUNK = 128
D_FF = 4096

ADAM_LR = 0.001
ADAM_B1 = 0.9
ADAM_B2 = 0.999
ADAM_EPS = 1e-08
ADAM_WD = 0.01
ADAM_STEP = 10

MESH_AXES = ("x", "y", "c")
MESH = pl.DeviceIdType.MESH
N_CHIPS = 4
N_DEV = 8

SDS = jax.ShapeDtypeStruct


def _cp(*sem):
    return pltpu.CompilerParams(dimension_semantics=sem, vmem_limit_bytes=VMEM_LIMIT_BYTES)


def _pick(dim, pref):
    if dim <= pref:
        return dim
    t = pref
    while t >= 256:
        if dim % t == 0:
            return t
        t //= 2
    return dim


def _row(tr, c):
    return pl.BlockSpec((tr, c), lambda i: (i, 0))


def _full(shape):
    nd = len(shape)
    return pl.BlockSpec(shape, lambda *_: (0,) * nd)


def _gelu_grad(x):
    c = math.sqrt(2.0 / math.pi)
    t = jnp.tanh(c * (x + 0.044715 * x * x * x))
    return 0.5 * (1.0 + t) + 0.5 * x * (1.0 - t * t) * c * (1.0 + 3.0 * 0.044715 * x * x)


def matmul(a, b, *, name, ta=False, tb=False, out_dtype=F32, tm=512, tn=512, tk=1024):
    M, K = (a.shape[1], a.shape[0]) if ta else a.shape
    K2, N = (b.shape[1], b.shape[0]) if tb else b.shape
    assert K == K2, (a.shape, b.shape, ta, tb)
    tm, tn, tk = _pick(M, tm), _pick(N, tn), _pick(K, tk)
    nk = K // tk
    dn = (((0 if ta else 1,), (1 if tb else 0,)), ((), ()))

    def body(a_ref, b_ref, o_ref, acc_ref):
        k = pl.program_id(2)

        @pl.when(k == 0)
        def _():
            acc_ref[...] = jnp.zeros_like(acc_ref)

        acc_ref[...] += lax.dot_general(a_ref[...].astype(MXU_DTYPE), b_ref[...].astype(MXU_DTYPE), dn,
                                        preferred_element_type=F32)

        @pl.when(k == nk - 1)
        def _():
            o_ref[...] = acc_ref[...].astype(o_ref.dtype)

    a_spec = pl.BlockSpec((tk, tm), lambda i, j, k: (k, i)) if ta else pl.BlockSpec((tm, tk), lambda i, j, k: (i, k))
    b_spec = pl.BlockSpec((tn, tk), lambda i, j, k: (j, k)) if tb else pl.BlockSpec((tk, tn), lambda i, j, k: (k, j))
    return pl.pallas_call(
        body, name=name, grid=(M // tm, N // tn, nk),
        in_specs=[a_spec, b_spec], out_specs=pl.BlockSpec((tm, tn), lambda i, j, k: (i, j)),
        out_shape=SDS((M, N), out_dtype), scratch_shapes=[pltpu.VMEM((tm, tn), F32)],
        compiler_params=_cp("parallel", "parallel", "arbitrary"),
    )(a, b)


def _rms_hat(x):
    return x * lax.rsqrt(jnp.mean(x * x, axis=-1, keepdims=True) + EPS)


def rms_fwd(x, g, *, name):
    T, D = x.shape
    tr = _pick(T, 512)

    def body(x_ref, g_ref, o_ref):
        o_ref[...] = (_rms_hat(x_ref[...]) * g_ref[...]).astype(o_ref.dtype)

    return pl.pallas_call(body, name=name, grid=(T // tr,), in_specs=[_row(tr, D), _full((1, D))],
                          out_specs=_row(tr, D), out_shape=SDS((T, D), MXU_DTYPE), compiler_params=_cp("parallel"))(x, g)


def rms_res_fwd(x, y, g, *, name):
    T, D = x.shape
    tr = _pick(T, 512)

    def body(x_ref, y_ref, g_ref, o_ref):
        o_ref[...] = x_ref[...] + _rms_hat(y_ref[...]) * g_ref[...]

    return pl.pallas_call(body, name=name, grid=(T // tr,), in_specs=[_row(tr, D), _row(tr, D), _full((1, D))],
                          out_specs=_row(tr, D), out_shape=SDS((T, D), F32), compiler_params=_cp("parallel"))(x, y, g)


def rms_bwd(x, g, dy, res, *, name):
    T, D = x.shape
    tr = _pick(T, 512)
    has_res = res is not None

    def body(*refs):
        if has_res:
            x_ref, g_ref, dy_ref, res_ref, dx_ref, dg_ref = refs
        else:
            x_ref, g_ref, dy_ref, dx_ref, dg_ref = refs
        xv = x_ref[...]
        r = lax.rsqrt(jnp.mean(xv * xv, axis=-1, keepdims=True) + EPS)
        xh = xv * r
        dyv = dy_ref[...]
        dxh = dyv * g_ref[...]
        dx = r * (dxh - xh * jnp.mean(dxh * xh, axis=-1, keepdims=True))
        if has_res:
            dx = dx + res_ref[...]
        dx_ref[...] = dx

        @pl.when(pl.program_id(0) == 0)
        def _():
            dg_ref[...] = jnp.zeros_like(dg_ref)

        dg_ref[...] += jnp.sum(dyv * xh, axis=0, keepdims=True)

    ins = [x, g, dy] + ([res] if has_res else [])
    in_specs = [_row(tr, D), _full((1, D)), _row(tr, D)] + ([_row(tr, D)] if has_res else [])
    return pl.pallas_call(body, name=name, grid=(T // tr,), in_specs=in_specs,
                          out_specs=[_row(tr, D), _full((1, D))],
                          out_shape=[SDS((T, D), F32), SDS((1, D), F32)], compiler_params=_cp("arbitrary"))(*ins)


def sqrelu_fwd(a, *, name):
    T, N = a.shape
    tr, tc = _pick(T, 512), _pick(N, 2048)

    def body(a_ref, o_ref):
        r = jnp.maximum(a_ref[...], 0.0)
        o_ref[...] = (r * r).astype(o_ref.dtype)

    spec = pl.BlockSpec((tr, tc), lambda i, j: (i, j))
    return pl.pallas_call(body, name=name, grid=(T // tr, N // tc), in_specs=[spec], out_specs=spec,
                          out_shape=SDS((T, N), MXU_DTYPE), compiler_params=_cp("parallel", "parallel"))(a)


def sqrelu_bwd(a, ds, *, name):
    T, N = a.shape
    tr, tc = _pick(T, 512), _pick(N, 2048)

    def body(a_ref, ds_ref, o_ref):
        o_ref[...] = (ds_ref[...] * (2.0 * jnp.maximum(a_ref[...], 0.0))).astype(o_ref.dtype)

    spec = pl.BlockSpec((tr, tc), lambda i, j: (i, j))
    return pl.pallas_call(body, name=name, grid=(T // tr, N // tc), in_specs=[spec, spec], out_specs=spec,
                          out_shape=SDS((T, N), MXU_DTYPE), compiler_params=_cp("parallel", "parallel"))(a, ds)


def loss_fwd_bwd(y, target, *, name):
    T, D = y.shape
    tr = _pick(T, 512)

    def body(y_ref, t_ref, l_ref, dy_ref):
        err = y_ref[...] - t_ref[...]
        dy_ref[...] = err * (1.0 / D)

        @pl.when(pl.program_id(0) == 0)
        def _():
            l_ref[...] = jnp.zeros_like(l_ref)

        l_ref[...] += 0.5 * jnp.sum(jnp.mean(err * err, axis=-1, keepdims=True))

    return pl.pallas_call(body, name=name, grid=(T // tr,), in_specs=[_row(tr, D), _row(tr, D)],
                          out_specs=[_full((SUBLANES, LANES)), _row(tr, D)],
                          out_shape=[SDS((SUBLANES, LANES), F32), SDS((T, D), F32)],
                          compiler_params=_cp("arbitrary"))(y, target)


def _s5_disc(lr, li, ldt, btr, bti):
    dt = jnp.exp(ldt)
    k = lax.broadcasted_iota(jnp.int32, (SUBLANES, S5_LANES), 0).astype(F32)
    kf = k + 1.0
    kb = 8.0 - k
    ph = li * dt
    lm = lr * dt
    tf_re = jnp.exp(kf * lm) * jnp.cos(kf * ph)
    tf_im = jnp.exp(kf * lm) * jnp.sin(kf * ph)
    tb_re = jnp.exp(kb * lm) * jnp.cos(kb * ph)
    tb_im = -jnp.exp(kb * lm) * jnp.sin(kb * ph)
    mag = jnp.exp(lm)
    ab_re = mag * jnp.cos(ph)
    ab_im = mag * jnp.sin(ph)
    den = lr * lr + li * li
    nr = ab_re - 1.0
    ni = ab_im
    q_re = (nr * lr + ni * li) / den
    q_im = (ni * lr - nr * li) / den
    bbt_re = q_re * btr - q_im * bti
    bbt_im = q_re * bti + q_im * btr
    return tf_re, tf_im, tb_re, tb_im, bbt_re, bbt_im


def _s5_disc_core(lr, li, ldt, btr, bti):
    dt = jnp.exp(ldt)
    mag = jnp.exp(lr * dt)
    ab_re = mag * jnp.cos(li * dt)
    ab_im = mag * jnp.sin(li * dt)
    den = lr * lr + li * li
    nr = ab_re - 1.0
    ni = ab_im
    q_re = (nr * lr + ni * li) / den
    q_im = (ni * lr - nr * li) / den
    return ab_re, ab_im, q_re * btr - q_im * bti, q_re * bti + q_im * btr


def s5_disc_fwd(lr, li, ldt, btr, bti, *, name):
    def body(lr_ref, li_ref, ldt_ref, btr_ref, bti_ref, *outs):
        vals = _s5_disc(lr_ref[...], li_ref[...], ldt_ref[...], btr_ref[...], bti_ref[...])
        for o, v in zip(outs, vals):
            o[...] = v

    tab = SDS((SUBLANES, S5_LANES), F32)
    bb = SDS((S5_GROUP, S5_LANES), F32)
    return pl.pallas_call(body, name=name, out_shape=[tab, tab, tab, tab, bb, bb])(lr, li, ldt, btr, bti)


def s5_disc_bwd(lr, li, ldt, btr, bti, dab_re, dab_im, dbbt_re, dbbt_im, *, name):
    def body(lr_ref, li_ref, ldt_ref, btr_ref, bti_ref, dar_ref, dai_ref, dbr_ref, dbi_ref,
             dlr_ref, dli_ref, dldt_ref, dbtr_ref, dbti_ref):
        _, vjp = jax.vjp(_s5_disc_core, lr_ref[...], li_ref[...], ldt_ref[...], btr_ref[...], bti_ref[...])
        dlr, dli, dldt, dbtr, dbti = vjp((dar_ref[...], dai_ref[...], dbr_ref[...], dbi_ref[...]))
        dlr_ref[...] = dlr
        dli_ref[...] = dli
        dbtr_ref[...] = dbtr
        dbti_ref[...] = dbti
        lane_group = lax.broadcasted_iota(jnp.int32, (S5_LANES, LANES), 0) // S5_STATE
        col = lax.broadcasted_iota(jnp.int32, (S5_LANES, LANES), 1)
        ind = (lane_group == col).astype(F32)
        dldt_ref[...] = jnp.dot(jnp.broadcast_to(dldt, (SUBLANES, S5_LANES)), ind,
                                precision=lax.Precision.HIGHEST, preferred_element_type=F32)

    row = SDS((1, S5_LANES), F32)
    bb = SDS((S5_GROUP, S5_LANES), F32)
    return pl.pallas_call(body, name=name, out_shape=[row, row, SDS((SUBLANES, LANES), F32), bb, bb])(
        lr, li, ldt, btr, bti, dab_re, dab_im, dbbt_re, dbbt_im)


S5_NB = 1024


def s5_interleave(re, im, axis):
    parts = []
    for n in range(S5_LANES // S5_NB):
        sl = [slice(None)] * re.ndim
        sl[axis] = slice(n * S5_NB, (n + 1) * S5_NB)
        parts += [re[tuple(sl)], im[tuple(sl)]]
    return jnp.concatenate(parts, axis=axis)


def s5_deinterleave(a, axis):
    re, im = [], []
    for n in range(S5_LANES // S5_NB):
        sl = [slice(None)] * a.ndim
        sl[axis] = slice(2 * n * S5_NB, (2 * n + 1) * S5_NB)
        re.append(a[tuple(sl)])
        sl[axis] = slice((2 * n + 1) * S5_NB, (2 * n + 2) * S5_NB)
        im.append(a[tuple(sl)])
    return jnp.concatenate(re, axis=axis), jnp.concatenate(im, axis=axis)


def s5_scan(bu, tab_re, tab_im, *, reverse, name):
    T = bu.shape[0]
    nb = S5_NB
    tc = _pick(T, 256)
    nl = S5_LANES // nb
    nt = T // tc
    ntile = tc // SUBLANES
    step_rows = ((1, 7), (2, 6), (4, 4)) if reverse else ((1, 0), (2, 1), (4, 3))

    def body(br_ref, bi_ref, tr_ref, ti_ref, xo_ref, cr_ref, ci_ref):
        @pl.when(pl.program_id(1) == 0)
        def _():
            cr_ref[...] = jnp.zeros_like(cr_ref)
            ci_ref[...] = jnp.zeros_like(ci_ref)

        tr = tr_ref[...]
        ti = ti_ref[...]
        io = lax.broadcasted_iota(jnp.int32, (SUBLANES, nb), 0)
        steps = [(d, tr_ref[r:r + 1, :], ti_ref[r:r + 1, :]) for d, r in step_rows]

        def tile(i, carry):
            cr, ci = carry
            j = (ntile - 1 - i) if reverse else i
            r0 = pl.multiple_of(j * SUBLANES, SUBLANES)
            xr = br_ref[pl.ds(r0, SUBLANES), :]
            xi = bi_ref[pl.ds(r0, SUBLANES), :]
            for d, pr, pi in steps:
                if reverse:
                    keep = io < SUBLANES - d
                    sh = SUBLANES - d
                else:
                    keep = io >= d
                    sh = d
                sr = jnp.where(keep, pltpu.roll(xr, sh, 0), 0.0)
                si = jnp.where(keep, pltpu.roll(xi, sh, 0), 0.0)
                xr, xi = xr + pr * sr - pi * si, xi + pr * si + pi * sr
            xr, xi = xr + tr * cr - ti * ci, xi + tr * ci + ti * cr
            xo_ref[pl.ds(r0, SUBLANES), 0:nb] = xr
            xo_ref[pl.ds(r0, SUBLANES), nb:2 * nb] = xi
            if reverse:
                return xr[0:1, :], xi[0:1, :]
            return xr[SUBLANES - 1:SUBLANES, :], xi[SUBLANES - 1:SUBLANES, :]

        cr, ci = lax.fori_loop(0, ntile, tile, (cr_ref[0:1, :], ci_ref[0:1, :]))
        cr_ref[0:1, :] = cr
        ci_ref[0:1, :] = ci

    def tmap(t):
        return (nt - 1 - t) if reverse else t

    re_spec = pl.BlockSpec((tc, nb), lambda n, t: (tmap(t), 2 * n))
    im_spec = pl.BlockSpec((tc, nb), lambda n, t: (tmap(t), 2 * n + 1))
    tab_spec = pl.BlockSpec((SUBLANES, nb), lambda n, t: (0, n))
    return pl.pallas_call(
        body, name=name, grid=(nl, nt), in_specs=[re_spec, im_spec, tab_spec, tab_spec],
        out_specs=pl.BlockSpec((tc, 2 * nb), lambda n, t: (tmap(t), n)), out_shape=SDS((T, 2 * S5_LANES), F32),
        scratch_shapes=[pltpu.VMEM((SUBLANES, nb), F32), pltpu.VMEM((SUBLANES, nb), F32)],
        compiler_params=_cp("parallel", "arbitrary"),
    )(bu, bu, tab_re, tab_im)


def s5_da(lam, xs, *, name):
    T = xs.shape[0]
    nb = S5_NB
    tc = _pick(T, 256)
    nl, nt = S5_LANES // nb, T // tc
    hb = tc // SUBLANES

    def body(lr_ref, li_ref, xr_ref, xi_ref, hr_ref, hi_ref, dar_ref, dai_ref):
        t = pl.program_id(1)

        @pl.when(t == 0)
        def _():
            dar_ref[...] = jnp.zeros_like(dar_ref)
            dai_ref[...] = jnp.zeros_like(dai_ref)

        io = lax.broadcasted_iota(jnp.int32, (tc, nb), 0)
        first = jnp.where(t > 0, 1.0, 0.0)
        pr = jnp.where(io >= 1, pltpu.roll(xr_ref[...], 1, 0), hr_ref[SUBLANES - 1:SUBLANES, :] * first)
        pi = jnp.where(io >= 1, pltpu.roll(xi_ref[...], 1, 0), hi_ref[SUBLANES - 1:SUBLANES, :] * first)
        lr = lr_ref[...]
        li = li_ref[...]
        dar_ref[...] += jnp.sum(lr * pr + li * pi, axis=0, keepdims=True)
        dai_ref[...] += jnp.sum(li * pr - lr * pi, axis=0, keepdims=True)

    re_blk = pl.BlockSpec((tc, nb), lambda n, t: (t, 2 * n))
    im_blk = pl.BlockSpec((tc, nb), lambda n, t: (t, 2 * n + 1))
    re_halo = pl.BlockSpec((SUBLANES, nb), lambda n, t: (jnp.maximum(t * hb - 1, 0), 2 * n))
    im_halo = pl.BlockSpec((SUBLANES, nb), lambda n, t: (jnp.maximum(t * hb - 1, 0), 2 * n + 1))
    acc = pl.BlockSpec((1, nb), lambda n, t: (0, n))
    row = SDS((1, S5_LANES), F32)
    return pl.pallas_call(body, name=name, grid=(nl, nt), in_specs=[re_blk, im_blk, re_blk, im_blk, re_halo, im_halo],
                          out_specs=[acc, acc], out_shape=[row, row],
                          compiler_params=_cp("parallel", "arbitrary"))(lam, lam, xs, xs, xs, xs)


def s5_out_fwd(yc, u, d, *, name):
    T, C = u.shape
    tr = _pick(T, 512)

    def body(yc_ref, u_ref, d_ref, yl_ref, yg_ref):
        yl = yc_ref[...] + d_ref[...] * u_ref[...]
        yl_ref[...] = yl
        yg_ref[...] = jax.nn.gelu(yl)

    return pl.pallas_call(body, name=name, grid=(T // tr,), in_specs=[_row(tr, C), _row(tr, C), _full((1, C))],
                          out_specs=[_row(tr, C)] * 2, out_shape=[SDS((T, C), F32)] * 2,
                          compiler_params=_cp("parallel"))(yc, u, d)


def glu_fwd(yg, gl, *, name):
    T, C = yg.shape
    tr = _pick(T, 512)

    def body(yg_ref, gl_ref, o_ref):
        o_ref[...] = yg_ref[...] * jax.nn.sigmoid(gl_ref[...])

    return pl.pallas_call(body, name=name, grid=(T // tr,), in_specs=[_row(tr, C)] * 2, out_specs=_row(tr, C),
                          out_shape=SDS((T, C), F32), compiler_params=_cp("parallel"))(yg, gl)


def glu_bwd(yg, gl, dy, *, name):
    T, C = yg.shape
    tr = _pick(T, 512)

    def body(yg_ref, gl_ref, dy_ref, dyg_ref, dgl_ref):
        s = jax.nn.sigmoid(gl_ref[...])
        dyv = dy_ref[...]
        dyg_ref[...] = dyv * s
        dgl_ref[...] = dyv * yg_ref[...] * s * (1.0 - s)

    return pl.pallas_call(body, name=name, grid=(T // tr,), in_specs=[_row(tr, C)] * 3, out_specs=[_row(tr, C)] * 2,
                          out_shape=[SDS((T, C), F32)] * 2, compiler_params=_cp("parallel"))(yg, gl, dy)


def s5_out_bwd(yl, u, d, dyg_a, dyg_b, *, name):
    T, C = u.shape
    tr = _pick(T, 512)

    def body(yl_ref, u_ref, d_ref, da_ref, db_ref, dyl_ref, du_ref, dd_ref):
        dyl = (da_ref[...] + db_ref[...]) * _gelu_grad(yl_ref[...])
        dyl_ref[...] = dyl
        du_ref[...] = dyl * d_ref[...]

        @pl.when(pl.program_id(0) == 0)
        def _():
            dd_ref[...] = jnp.zeros_like(dd_ref)

        dd_ref[...] += jnp.sum(dyl * u_ref[...], axis=0, keepdims=True)

    return pl.pallas_call(body, name=name, grid=(T // tr,),
                          in_specs=[_row(tr, C), _row(tr, C), _full((1, C)), _row(tr, C), _row(tr, C)],
                          out_specs=[_row(tr, C), _row(tr, C), _full((1, C))],
                          out_shape=[SDS((T, C), F32), SDS((T, C), F32), SDS((1, C), F32)],
                          compiler_params=_cp("arbitrary"))(yl, u, d, dyg_a, dyg_b)


def add2(a, b, *, name):
    T, C = a.shape
    tr = _pick(T, 512)

    def body(a_ref, b_ref, o_ref):
        o_ref[...] = a_ref[...] + b_ref[...]

    return pl.pallas_call(body, name=name, grid=(T // tr,), in_specs=[_row(tr, C)] * 2, out_specs=_row(tr, C),
                          out_shape=SDS((T, C), F32), compiler_params=_cp("parallel"))(a, b)


def _tri(n, upper):
    r = lax.broadcasted_iota(jnp.int32, (n, n), 0)
    c = lax.broadcasted_iota(jnp.int32, (n, n), 1)
    return ((c >= r) if upper else (c <= r)).astype(F32)


def fox_gate_fwd(fl, bf, *, name):
    T = fl.shape[0]
    tb = _pick(T, 256)

    def body(fl_ref, bf_ref, f_ref, c_ref):
        @pl.when(pl.program_id(0) == 0)
        def _():
            c_ref[...] = jnp.zeros_like(c_ref)

        lf = jax.nn.log_sigmoid(fl_ref[...] + bf_ref[...])
        f = jnp.dot(_tri(tb, False), lf, precision=lax.Precision.HIGHEST, preferred_element_type=F32) + c_ref[0:1, :]
        f_ref[...] = f
        c_ref[0:1, :] = f[tb - 1:tb, :]

    return pl.pallas_call(body, name=name, grid=(T // tb,), in_specs=[_row(tb, LANES), _full((1, LANES))],
                          out_specs=_row(tb, LANES), out_shape=SDS((T, LANES), F32),
                          scratch_shapes=[pltpu.VMEM((SUBLANES, LANES), F32)], compiler_params=_cp("arbitrary"))(fl, bf)


def fox_gate_bwd(fl, bf, df, *, name):
    T = fl.shape[0]
    tb = _pick(T, 256)
    nt = T // tb

    def body(fl_ref, bf_ref, df_ref, dfl_ref, dbf_ref, c_ref):
        @pl.when(pl.program_id(0) == 0)
        def _():
            c_ref[...] = jnp.zeros_like(c_ref)
            dbf_ref[...] = jnp.zeros_like(dbf_ref)

        dlf = jnp.dot(_tri(tb, True), df_ref[...], precision=lax.Precision.HIGHEST, preferred_element_type=F32) + c_ref[0:1, :]
        c_ref[0:1, :] = dlf[0:1, :]
        dfl = dlf * jax.nn.sigmoid(-(fl_ref[...] + bf_ref[...]))
        dfl_ref[...] = dfl
        dbf_ref[...] += jnp.sum(dfl, axis=0, keepdims=True)

    rev = pl.BlockSpec((tb, LANES), lambda i: (nt - 1 - i, 0))
    return pl.pallas_call(body, name=name, grid=(nt,), in_specs=[rev, _full((1, LANES)), rev],
                          out_specs=[rev, _full((1, LANES))], out_shape=[SDS((T, LANES), F32), SDS((1, LANES), F32)],
                          scratch_shapes=[pltpu.VMEM((SUBLANES, LANES), F32)], compiler_params=_cp("arbitrary"))(fl, bf, df)


FOX_BLOCK = 512
FOX_PAIRS = FOX_HEADS // 2
_NT = (((1,), (1,)), ((), ()))


def _fox_block(T):
    return _pick(T, FOX_BLOCK)


def fox_fwd(q, k, v, f_col, f_row, *, name):
    T = q.shape[0]
    blk = _fox_block(T)
    nb = T // blk
    scale = FOX_HEAD_DIM ** -0.5

    def body(q_ref, k_ref, v_ref, fc_ref, fr_ref, o_ref, l_ref):
        i = pl.program_id(1)
        row = lax.broadcasted_iota(jnp.int32, (blk, blk), 0)
        col = lax.broadcasted_iota(jnp.int32, (blk, blk), 1)
        for hh in range(2):
            ls = slice(hh * FOX_HEAD_DIM, (hh + 1) * FOX_HEAD_DIM)
            qh = (q_ref[:, ls] * scale).astype(MXU_DTYPE)
            fi = fc_ref[0, :, hh:hh + 1]

            def step(j, carry, masked):
                m, l, acc = carry
                r0 = pl.multiple_of(j * blk, blk)
                kj = k_ref[pl.ds(r0, blk), ls].astype(MXU_DTYPE)
                vj = v_ref[pl.ds(r0, blk), ls].astype(MXU_DTYPE)
                s = lax.dot_general(qh, kj, _NT, preferred_element_type=F32) + (fi - fr_ref[0, j, hh:hh + 1, :])
                if masked:
                    s = jnp.where(col <= row, s, -jnp.inf)
                m_new = jnp.maximum(m, jnp.max(s, axis=-1, keepdims=True))
                p = jnp.exp(s - m_new)
                alpha = jnp.exp(m - m_new)
                l = alpha * l + jnp.sum(p, axis=-1, keepdims=True)
                acc = alpha * acc + jnp.dot(p.astype(MXU_DTYPE), vj, preferred_element_type=F32)
                return m_new, l, acc

            init = (jnp.full((blk, 1), -jnp.inf, F32), jnp.zeros((blk, 1), F32), jnp.zeros((blk, FOX_HEAD_DIM), F32))
            carry = lax.fori_loop(0, i, lambda j, c: step(j, c, False), init)
            m, l, acc = step(i, carry, True)
            o_ref[:, ls] = acc / l
            l_ref[0, :, hh:hh + 1] = m + jnp.log(l)

    qspec = pl.BlockSpec((blk, LANES), lambda h, i: (i, h))
    kvspec = pl.BlockSpec((T, LANES), lambda h, i: (0, h))
    cspec = pl.BlockSpec((1, blk, 2), lambda h, i: (h, i, 0))
    rspec = pl.BlockSpec((1, nb, 2, blk), lambda h, i: (h, 0, 0, 0))
    return pl.pallas_call(body, name=name, grid=(FOX_PAIRS, nb), in_specs=[qspec, kvspec, kvspec, cspec, rspec],
                          out_specs=[qspec, cspec], out_shape=[SDS((T, FOX_WIDTH), F32), SDS((FOX_PAIRS, T, 2), F32)],
                          compiler_params=_cp("parallel", "parallel"))(q, k, v, f_col, f_row)


def fox_bwd_q(q, k, v, o, do, f_col, f_row, lse_col, *, name):
    T = q.shape[0]
    blk = _fox_block(T)
    nb = T // blk
    scale = FOX_HEAD_DIM ** -0.5

    def body(q_ref, k_ref, v_ref, o_ref, do_ref, fc_ref, fr_ref, lc_ref, dq_ref, dd_ref, df_ref):
        i = pl.program_id(1)
        row = lax.broadcasted_iota(jnp.int32, (blk, blk), 0)
        col = lax.broadcasted_iota(jnp.int32, (blk, blk), 1)
        for hh in range(2):
            ls = slice(hh * FOX_HEAD_DIM, (hh + 1) * FOX_HEAD_DIM)
            qh = (q_ref[:, ls] * scale).astype(MXU_DTYPE)
            doh = do_ref[:, ls]
            dd = jnp.sum(doh * o_ref[:, ls], axis=-1, keepdims=True)
            dd_ref[0, :, hh:hh + 1] = dd
            dob = doh.astype(MXU_DTYPE)
            fi = fc_ref[0, :, hh:hh + 1]
            lse = lc_ref[0, :, hh:hh + 1]

            def step(j, carry, masked):
                dq, df = carry
                r0 = pl.multiple_of(j * blk, blk)
                kj = k_ref[pl.ds(r0, blk), ls].astype(MXU_DTYPE)
                vj = v_ref[pl.ds(r0, blk), ls].astype(MXU_DTYPE)
                s = lax.dot_general(qh, kj, _NT, preferred_element_type=F32) + (fi - fr_ref[0, j, hh:hh + 1, :])
                p = jnp.exp(s - lse)
                if masked:
                    p = jnp.where(col <= row, p, 0.0)
                dp = lax.dot_general(dob, vj, _NT, preferred_element_type=F32)
                ds = p * (dp - dd)
                return (dq + jnp.dot(ds.astype(MXU_DTYPE), kj, preferred_element_type=F32),
                        df + jnp.sum(ds, axis=-1, keepdims=True))

            init = (jnp.zeros((blk, FOX_HEAD_DIM), F32), jnp.zeros((blk, 1), F32))
            carry = lax.fori_loop(0, i, lambda j, c: step(j, c, False), init)
            dq, df = step(i, carry, True)
            dq_ref[:, ls] = dq * scale
            df_ref[0, :, hh:hh + 1] = df

    qspec = pl.BlockSpec((blk, LANES), lambda h, i: (i, h))
    kvspec = pl.BlockSpec((T, LANES), lambda h, i: (0, h))
    cspec = pl.BlockSpec((1, blk, 2), lambda h, i: (h, i, 0))
    rspec = pl.BlockSpec((1, nb, 2, blk), lambda h, i: (h, 0, 0, 0))
    stat = SDS((FOX_PAIRS, T, 2), F32)
    return pl.pallas_call(body, name=name, grid=(FOX_PAIRS, nb),
                          in_specs=[qspec, kvspec, kvspec, qspec, qspec, cspec, rspec, cspec],
                          out_specs=[qspec, cspec, cspec], out_shape=[SDS((T, FOX_WIDTH), F32), stat, stat],
                          compiler_params=_cp("parallel", "parallel"))(q, k, v, o, do, f_col, f_row, lse_col)


def fox_bwd_kv(q, k, v, do, f_col, f_row, lse_row, dd_row, dfq_col, *, name):
    T = q.shape[0]
    blk = _fox_block(T)
    nb = T // blk
    scale = FOX_HEAD_DIM ** -0.5

    def body(q_ref, k_ref, v_ref, do_ref, fc_ref, fr_ref, lr_ref, dr_ref, dfq_ref, dk_ref, dv_ref, df_ref):
        j = pl.program_id(1)
        row = lax.broadcasted_iota(jnp.int32, (blk, blk), 0)
        col = lax.broadcasted_iota(jnp.int32, (blk, blk), 1)
        for hh in range(2):
            ls = slice(hh * FOX_HEAD_DIM, (hh + 1) * FOX_HEAD_DIM)
            kh = k_ref[:, ls].astype(MXU_DTYPE)
            vh = v_ref[:, ls].astype(MXU_DTYPE)
            fj = fc_ref[0, :, hh:hh + 1]

            def step(i, carry, masked):
                dk, dv, df = carry
                r0 = pl.multiple_of(i * blk, blk)
                qi = (q_ref[pl.ds(r0, blk), ls] * scale).astype(MXU_DTYPE)
                doi = do_ref[pl.ds(r0, blk), ls].astype(MXU_DTYPE)
                st = lax.dot_general(kh, qi, _NT, preferred_element_type=F32) + (fr_ref[0, i, hh:hh + 1, :] - fj)
                pt = jnp.exp(st - lr_ref[0, i, hh:hh + 1, :])
                if masked:
                    pt = jnp.where(col >= row, pt, 0.0)
                dv = dv + jnp.dot(pt.astype(MXU_DTYPE), doi, preferred_element_type=F32)
                dpt = lax.dot_general(vh, doi, _NT, preferred_element_type=F32)
                dst = pt * (dpt - dr_ref[0, i, hh:hh + 1, :])
                dk = dk + jnp.dot(dst.astype(MXU_DTYPE), qi, preferred_element_type=F32)
                df = df - jnp.sum(dst, axis=-1, keepdims=True)
                return dk, dv, df

            init = (jnp.zeros((blk, FOX_HEAD_DIM), F32), jnp.zeros((blk, FOX_HEAD_DIM), F32), dfq_ref[0, :, hh:hh + 1])
            carry = step(j, init, True)
            dk, dv, df = lax.fori_loop(j + 1, nb, lambda i, c: step(i, c, False), carry)
            dk_ref[:, ls] = dk
            dv_ref[:, ls] = dv
            df_ref[0, :, hh:hh + 1] = df

    bspec = pl.BlockSpec((blk, LANES), lambda h, j: (j, h))
    allspec = pl.BlockSpec((T, LANES), lambda h, j: (0, h))
    cspec = pl.BlockSpec((1, blk, 2), lambda h, j: (h, j, 0))
    rspec = pl.BlockSpec((1, nb, 2, blk), lambda h, j: (h, 0, 0, 0))
    return pl.pallas_call(body, name=name, grid=(FOX_PAIRS, nb),
                          in_specs=[allspec, bspec, bspec, allspec, cspec, rspec, rspec, rspec, cspec],
                          out_specs=[bspec, bspec, cspec],
                          out_shape=[SDS((T, FOX_WIDTH), F32), SDS((T, FOX_WIDTH), F32), SDS((FOX_PAIRS, T, 2), F32)],
                          compiler_params=_cp("parallel", "parallel"))(q, k, v, do, f_col, f_row, lse_row, dd_row, dfq_col)


def _pairs_col(a, T):
    return jnp.transpose(a[:, :FOX_HEADS].reshape(T, FOX_PAIRS, 2), (1, 0, 2))


def _col_to_row(a, T):
    blk = _fox_block(T)
    return jnp.transpose(a.reshape(FOX_PAIRS, T // blk, blk, 2), (0, 1, 3, 2))


def _pairs_to_lanes(a, T):
    flat = jnp.transpose(a, (1, 0, 2)).reshape(T, FOX_HEADS)
    return jnp.pad(flat, ((0, 0), (0, LANES - FOX_HEADS)))


def _pool_counts(t0, n, w):
    t = (t0 + lax.broadcasted_iota(jnp.int32, (n, 1), 0)).astype(F32)
    return jnp.minimum(t + 1.0, float(w))


def pool_window(x, *, adjoint, name):
    T, C = x.shape
    tr = _pick(T, 512)
    nt = T // tr
    hb = tr // POOL_HALO
    n = tr + POOL_HALO

    def body(x_ref, h_ref, o_ref):
        i = pl.program_id(0)
        cur = x_ref[...]
        if adjoint:
            halo = h_ref[...] * jnp.where(i < nt - 1, 1.0, 0.0)
            ext = jnp.concatenate([cur, halo], axis=0)
            t0 = i * tr
        else:
            halo = h_ref[...] * jnp.where(i > 0, 1.0, 0.0)
            ext = jnp.concatenate([halo, cur], axis=0)
            t0 = i * tr - POOL_HALO
        sums = {}
        for g, w in enumerate(POOL_WINDOWS):
            ls = slice(g * POOL_GROUP_DIM, (g + 1) * POOL_GROUP_DIM)
            s = ext[:, ls]
            if adjoint:
                s = s / _pool_counts(t0, n, w)
            d = 1
            while d < w:
                s = s + pltpu.roll(s, (n - d) if adjoint else d, 0)
                d *= 2
            if adjoint:
                o_ref[:, ls] = s[0:tr, :] - cur[:, ls]
            else:
                o_ref[:, ls] = s[POOL_HALO:n, :] / _pool_counts(i * tr, tr, w) - cur[:, ls]

    if adjoint:
        halo_spec = pl.BlockSpec((POOL_HALO, C), lambda i: (jnp.minimum((i + 1) * hb, T // POOL_HALO - 1), 0))
    else:
        halo_spec = pl.BlockSpec((POOL_HALO, C), lambda i: (jnp.maximum(i * hb - 1, 0), 0))
    return pl.pallas_call(body, name=name, grid=(nt,), in_specs=[_row(tr, C), halo_spec], out_specs=_row(tr, C),
                          out_shape=SDS((T, C), F32), compiler_params=_cp("parallel"))(x, x)


def colscale_fwd(a, s, *, name):
    T, C = a.shape
    tr = _pick(T, 512)

    def body(a_ref, s_ref, o_ref):
        o_ref[...] = a_ref[...] * s_ref[...]

    return pl.pallas_call(body, name=name, grid=(T // tr,), in_specs=[_row(tr, C), _full((1, C))], out_specs=_row(tr, C),
                          out_shape=SDS((T, C), F32), compiler_params=_cp("parallel"))(a, s)


def colscale_bwd(a, s, dy, *, name):
    T, C = a.shape
    tr = _pick(T, 512)

    def body(a_ref, s_ref, dy_ref, da_ref, ds_ref):
        dyv = dy_ref[...]
        da_ref[...] = dyv * s_ref[...]

        @pl.when(pl.program_id(0) == 0)
        def _():
            ds_ref[...] = jnp.zeros_like(ds_ref)

        ds_ref[...] += jnp.sum(dyv * a_ref[...], axis=0, keepdims=True)

    return pl.pallas_call(body, name=name, grid=(T // tr,), in_specs=[_row(tr, C), _full((1, C)), _row(tr, C)],
                          out_specs=[_row(tr, C), _full((1, C))], out_shape=[SDS((T, C), F32), SDS((1, C), F32)],
                          compiler_params=_cp("arbitrary"))(a, s, dy)


SGU_ROWS = 512


def _sgu_norm(v, ln_g, ln_b):
    vg = jax.nn.gelu(v)
    xc = vg - jnp.mean(vg, axis=-1, keepdims=True)
    r = lax.rsqrt(jnp.mean(xc * xc, axis=-1, keepdims=True) + EPS)
    xh = xc * r
    return xh * ln_g + ln_b, xh, r


def sgu_fwd(u, v, ln_g, ln_b, ws, bst, *, name):
    T, C = u.shape
    tr = _pick(T, SGU_ROWS)

    def body(u_ref, v_ref, g_ref, b_ref, ws_ref, bst_ref, o_ref):
        vn, _, _ = _sgu_norm(v_ref[...], g_ref[...], b_ref[...])
        vn = vn.astype(MXU_DTYPE)
        ug = jax.nn.gelu(u_ref[...])
        for g in range(SGU_GROUPS):
            w = ws_ref[g].astype(MXU_DTYPE)
            bias = bst_ref[:, g:g + 1]
            for c in range(tr // CHUNK):
                rs = slice(c * CHUNK, (c + 1) * CHUNK)
                ls = slice(g * SGU_GROUP_DIM, (g + 1) * SGU_GROUP_DIM)
                mixed = jnp.dot(w, vn[rs, ls], preferred_element_type=F32) + bias
                o_ref[rs, ls] = ug[rs, ls] * mixed

    return pl.pallas_call(body, name=name, grid=(T // tr,),
                          in_specs=[_row(tr, C), _row(tr, C), _full((1, C)), _full((1, C)),
                                    _full((SGU_GROUPS, CHUNK, CHUNK)), _full((CHUNK, SGU_GROUPS))],
                          out_specs=_row(tr, C), out_shape=SDS((T, C), F32),
                          compiler_params=_cp("parallel"))(u, v, ln_g, ln_b, ws, bst)


def sgu_bwd(u, v, ln_g, ln_b, ws, wst, bst, dy, *, name):
    T, C = u.shape
    tr = _pick(T, SGU_ROWS)

    def body(u_ref, v_ref, g_ref, b_ref, ws_ref, wst_ref, bst_ref, dy_ref,
             du_ref, dv_ref, dg_ref, db_ref, dws_ref, dbst_ref, dvn_ref):
        @pl.when(pl.program_id(0) == 0)
        def _():
            dg_ref[...] = jnp.zeros_like(dg_ref)
            db_ref[...] = jnp.zeros_like(db_ref)
            dws_ref[...] = jnp.zeros_like(dws_ref)
            dbst_ref[...] = jnp.zeros_like(dbst_ref)

        uv = u_ref[...]
        vv = v_ref[...]
        vn, xh, r = _sgu_norm(vv, g_ref[...], b_ref[...])
        vn = vn.astype(MXU_DTYPE)
        ug = jax.nn.gelu(uv)
        dyv = dy_ref[...]
        for g in range(SGU_GROUPS):
            w = ws_ref[g].astype(MXU_DTYPE)
            wt = wst_ref[g].astype(MXU_DTYPE)
            bias = bst_ref[:, g:g + 1]
            dw = jnp.zeros((CHUNK, CHUNK), F32)
            dbias = jnp.zeros((CHUNK, 1), F32)
            for c in range(tr // CHUNK):
                rs = slice(c * CHUNK, (c + 1) * CHUNK)
                ls = slice(g * SGU_GROUP_DIM, (g + 1) * SGU_GROUP_DIM)
                vblk = vn[rs, ls]
                mixed = jnp.dot(w, vblk, preferred_element_type=F32) + bias
                dyb = dyv[rs, ls]
                du_ref[rs, ls] = dyb * mixed * _gelu_grad(uv[rs, ls])
                dmixed = dyb * ug[rs, ls]
                dbias = dbias + jnp.sum(dmixed, axis=-1, keepdims=True)
                dmb = dmixed.astype(MXU_DTYPE)
                dw = dw + lax.dot_general(dmb, vblk, _NT, preferred_element_type=F32)
                dvn_ref[rs, ls] = jnp.dot(wt, dmb, preferred_element_type=F32)
            dws_ref[g] += dw
            dbst_ref[:, g:g + 1] += dbias
        dvn = dvn_ref[...]
        dg_ref[...] += jnp.sum(dvn * xh, axis=0, keepdims=True)
        db_ref[...] += jnp.sum(dvn, axis=0, keepdims=True)
        dxh = dvn * g_ref[...]
        dvg = r * (dxh - jnp.mean(dxh, axis=-1, keepdims=True) - xh * jnp.mean(dxh * xh, axis=-1, keepdims=True))
        dv_ref[...] = dvg * _gelu_grad(vv)

    wspec = _full((SGU_GROUPS, CHUNK, CHUNK))
    return pl.pallas_call(body, name=name, grid=(T // tr,),
                          in_specs=[_row(tr, C), _row(tr, C), _full((1, C)), _full((1, C)), wspec, wspec,
                                    _full((CHUNK, SGU_GROUPS)), _row(tr, C)],
                          out_specs=[_row(tr, C), _row(tr, C), _full((1, C)), _full((1, C)), wspec,
                                     _full((CHUNK, SGU_GROUPS))],
                          out_shape=[SDS((T, C), F32), SDS((T, C), F32), SDS((1, C), F32), SDS((1, C), F32),
                                     SDS((SGU_GROUPS, CHUNK, CHUNK), F32), SDS((CHUNK, SGU_GROUPS), F32)],
                          scratch_shapes=[pltpu.VMEM((tr, C), F32)],
                          compiler_params=_cp("arbitrary"))(u, v, ln_g, ln_b, ws, wst, bst, dy)


def adamw(w, g, m, v, *, name):
    R, C = w.shape
    tr = _pick(R, 512)
    c1 = 1.0 - ADAM_B1 ** ADAM_STEP
    c2 = 1.0 - ADAM_B2 ** ADAM_STEP

    def body(w_ref, g_ref, m_ref, v_ref, d_ref, nm_ref, nv_ref):
        gv = g_ref[...]
        nm = ADAM_B1 * m_ref[...] + (1.0 - ADAM_B1) * gv
        nv = ADAM_B2 * v_ref[...] + (1.0 - ADAM_B2) * (gv * gv)
        nm_ref[...] = nm
        nv_ref[...] = nv
        d_ref[...] = -ADAM_LR * ((nm / c1) / (jnp.sqrt(nv / c2) + ADAM_EPS) + ADAM_WD * w_ref[...])

    spec = _row(tr, C)
    return pl.pallas_call(body, name=name, grid=(R // tr,), in_specs=[spec] * 4, out_specs=[spec] * 3,
                          out_shape=[SDS((R, C), F32)] * 3, compiler_params=_cp("parallel"))(w, g, m, v)


ANY = pl.BlockSpec(memory_space=pl.ANY)


def _coords():
    return lax.axis_index("x"), lax.axis_index("y"), lax.axis_index("c")


def _other_chips(x, y):
    return [(1 - x, y), (x, 1 - y), (1 - x, 1 - y)]


def _remote(src, dst, send_sems, recv_sems, k, dev):
    return pltpu.make_async_remote_copy(src_ref=src, dst_ref=dst, send_sem=send_sems.at[k], recv_sem=recv_sems.at[k],
                                        device_id=dev, device_id_type=MESH)


def allgather_chip_shards(shard, *, name):
    R, C = shard.shape
    half = R // 2

    def body(s_ref, o_ref, send_sems, recv_sems, local_sem):
        x, y, c = _coords()
        j = 2 * x + y
        sibling = (x, y, 1 - c)
        mine = pl.ds(c * half, half)
        theirs = pl.ds((1 - c) * half, half)
        chips = _other_chips(x, y)
        local = pltpu.make_async_copy(s_ref, o_ref.at[j], local_sem)
        local.start()
        first = [_remote(s_ref.at[mine], o_ref.at[j, mine], send_sems, recv_sems, k, (px, py, c))
                 for k, (px, py) in enumerate(chips)]
        for cp in first:
            cp.start()
        passed = []
        for k, (px, py) in enumerate(chips):
            rows = o_ref.at[2 * px + py, mine]
            _remote(rows, rows, send_sems, recv_sems, k, (px, py, c)).wait_recv()
            fw = _remote(rows, rows, send_sems, recv_sems, 3 + k, sibling)
            fw.start()
            passed.append(fw)
        for k, (px, py) in enumerate(chips):
            rows = o_ref.at[2 * px + py, theirs]
            _remote(rows, rows, send_sems, recv_sems, 3 + k, sibling).wait_recv()
        for cp in first + passed:
            cp.wait_send()
        local.wait()

    return pl.pallas_call(body, name=name, in_specs=[ANY], out_specs=ANY, out_shape=SDS((N_CHIPS, R, C), shard.dtype),
                          scratch_shapes=[pltpu.SemaphoreType.DMA((6,)), pltpu.SemaphoreType.DMA((6,)),
                                          pltpu.SemaphoreType.DMA(())])(shard)


def swap_sibling_halves(g, *, name):
    n, R, C = g.shape
    half = R // 2

    def body(g_ref, o_ref, send_sems, recv_sems):
        x, y, c = _coords()
        cp = _remote(g_ref.at[:, pl.ds((1 - c) * half, half), :], o_ref, send_sems, recv_sems, 0, (x, y, 1 - c))
        cp.start()
        cp.wait()

    return pl.pallas_call(body, name=name, in_specs=[ANY], out_specs=ANY, out_shape=SDS((n, half, C), g.dtype),
                          scratch_shapes=[pltpu.SemaphoreType.DMA((1,)), pltpu.SemaphoreType.DMA((1,))])(g)


def add_sibling_half(g, land, c_idx, *, name):
    n, R, C = g.shape
    half = R // 2
    tr = _pick(half, 256)
    nt = half // tr

    def body(c_ref, g_ref, l_ref, of_ref, ob_ref):
        s = g_ref[...] + l_ref[...].astype(F32)
        of_ref[...] = s
        ob_ref[...] = s.astype(ob_ref.dtype)

    blk = pl.BlockSpec((1, tr, C), lambda s, i, c_ref: (s, i, 0))
    gblk = pl.BlockSpec((1, tr, C), lambda s, i, c_ref: (s, c_ref[0] * nt + i, 0))
    return pl.pallas_call(
        body, name=name,
        grid_spec=pltpu.PrefetchScalarGridSpec(num_scalar_prefetch=1, grid=(n, nt), in_specs=[gblk, blk],
                                               out_specs=[blk, blk]),
        out_shape=[SDS((n, half, C), F32), SDS((n, half, C), WIRE_DTYPE)],
        compiler_params=_cp("parallel", "parallel"))(c_idx, g, land)


def send_chip_partials(pb, *, name):
    n, H, C = pb.shape

    def body(p_ref, o_ref, send_sems, recv_sems):
        x, y, c = _coords()
        cps = [_remote(p_ref.at[2 * px + py], o_ref.at[k], send_sems, recv_sems, k, (px, py, c))
               for k, (px, py) in enumerate(_other_chips(x, y))]
        for cp in cps:
            cp.start()
        for cp in cps:
            cp.wait()

    return pl.pallas_call(body, name=name, in_specs=[ANY], out_specs=ANY, out_shape=SDS((3, H, C), pb.dtype),
                          scratch_shapes=[pltpu.SemaphoreType.DMA((3,)), pltpu.SemaphoreType.DMA((3,))])(pb)


def add_chip_partials(pf, rb, j_idx, *, name):
    n, H, C = pf.shape
    tr = _pick(H, 256)

    def body(j_ref, p_ref, r_ref, o_ref):
        s = p_ref[0]
        for k in range(3):
            s = s + r_ref[k].astype(F32)
        o_ref[...] = s

    pblk = pl.BlockSpec((1, tr, C), lambda i, j_ref: (j_ref[0], i, 0))
    rblk = pl.BlockSpec((3, tr, C), lambda i, j_ref: (0, i, 0))
    oblk = pl.BlockSpec((tr, C), lambda i, j_ref: (i, 0))
    return pl.pallas_call(
        body, name=name,
        grid_spec=pltpu.PrefetchScalarGridSpec(num_scalar_prefetch=1, grid=(H // tr,), in_specs=[pblk, rblk],
                                               out_specs=oblk),
        out_shape=SDS((H, C), F32), compiler_params=_cp("parallel"))(j_idx, pf, rb)


def join_sibling_halves(sf, *, name):
    H, C = sf.shape

    def body(s_ref, o_ref, send_sems, recv_sems, local_sem):
        x, y, c = _coords()
        local = pltpu.make_async_copy(s_ref, o_ref.at[c], local_sem)
        local.start()
        cp = _remote(s_ref, o_ref.at[c], send_sems, recv_sems, 0, (x, y, 1 - c))
        cp.start()
        cp.wait_send()
        _remote(s_ref, o_ref.at[1 - c], send_sems, recv_sems, 0, (x, y, 1 - c)).wait_recv()
        local.wait()

    return pl.pallas_call(body, name=name, in_specs=[ANY], out_specs=ANY, out_shape=SDS((2, H, C), sf.dtype),
                          scratch_shapes=[pltpu.SemaphoreType.DMA((1,)), pltpu.SemaphoreType.DMA((1,)),
                                          pltpu.SemaphoreType.DMA(())])(sf)


def exchange_pieces(v, *, scatter, name):
    P, C = v.shape[-2:]

    def body(v_ref, o_ref, send_sems, recv_sems, local_sem):
        x, y, c = _coords()
        me = 4 * x + 2 * y + c
        local = pltpu.make_async_copy(v_ref.at[me] if scatter else v_ref, o_ref.at[me], local_sem)
        local.start()
        cps = []
        for m in range(1, N_DEV):
            px = (1 - x) if m & 4 else x
            py = (1 - y) if m & 2 else y
            pc = (1 - c) if m & 1 else c
            src = v_ref.at[4 * px + 2 * py + pc] if scatter else v_ref
            cps.append(_remote(src, o_ref.at[me], send_sems, recv_sems, m - 1, (px, py, pc)))
        for cp in cps:
            cp.start()
        for cp in cps:
            cp.wait_send()
        for m in range(1, N_DEV):
            px = (1 - x) if m & 4 else x
            py = (1 - y) if m & 2 else y
            pc = (1 - c) if m & 1 else c
            slot = o_ref.at[4 * px + 2 * py + pc]
            _remote(slot, slot, send_sems, recv_sems, m - 1, (px, py, pc)).wait_recv()
        local.wait()

    return pl.pallas_call(body, name=name, in_specs=[ANY], out_specs=ANY, out_shape=SDS((N_DEV, P, C), v.dtype),
                          scratch_shapes=[pltpu.SemaphoreType.DMA((N_DEV - 1,)), pltpu.SemaphoreType.DMA((N_DEV - 1,)),
                                          pltpu.SemaphoreType.DMA(())])(v)


def sum_pieces(land, *, name):
    n, P, C = land.shape

    def body(l_ref, o_ref):
        s = l_ref[0]
        for d in range(1, n):
            s = s + l_ref[d]
        o_ref[...] = s

    return pl.pallas_call(body, name=name, out_shape=SDS((P, C), F32))(land)


FLAT_C = 1024
BIG_SEGS = (
    ("w_in_even", (1024, 514), 1),
    ("s5_w_glu", (128, 512), 0),
    ("w_out_even", (256, 1024), 0),
    ("w_in_odd", (1024, 384), 1),
    ("w_out_odd", (256, 1024), 0),
    ("mlp_w1", (2, 1024, 1024), 2),
    ("mlp_w2", (2, 1024, 1024), 1),
)
BIG_ROWS = sum(math.prod(s) // FLAT_C for _, s, _ in BIG_SEGS)
FLAT_ROWS = 5632
SHARDED_SMALL = ("pool_scale", "sgu_ln_g", "sgu_ln_b")
SMALL_SEGS = (
    ("mix_pre_g", (2, 1024)), ("mix_post_g", (2, 1024)), ("mlp_pre_g", (2, 1024)), ("mlp_post_g", (2, 1024)),
    ("s5_lam_re", (1, 32, 64)), ("s5_lam_im", (1, 32, 64)), ("s5_log_dt", (1, 32)),
    ("s5_b_re", (1, 32, 64, 16)), ("s5_b_im", (1, 32, 64, 16)), ("s5_c_re", (1, 32, 16, 64)), ("s5_c_im", (1, 32, 16, 64)),
    ("s5_d", (1, 512)), ("fox_b_f", (1, 8)), ("pool_w", (1, 4, 128, 128)), ("sgu_w_s", (1, 4, 128, 128)),
    ("sgu_b_s", (1, 4, 128)),
)
REDUCED_SEGS = SMALL_SEGS + tuple((n, (1, 512)) for n in SHARDED_SMALL)


def _pack_rows(arrays, rows):
    flat = jnp.concatenate([a.reshape(-1, FLAT_C) for a in arrays], axis=0)
    return jnp.pad(flat, ((0, rows - flat.shape[0]), (0, 0)))


def _unpack_slab(slab):
    out, r = {}, 0
    for name, shape, _ in BIG_SEGS:
        n = math.prod(shape) // FLAT_C
        out[name] = slab[r:r + n].reshape(shape)
        r += n
    return out


def _full_from_slabs(slabs):
    parts = [_unpack_slab(slabs[j]) for j in range(N_CHIPS)]
    return {name: jnp.concatenate([p[name] for p in parts], axis=ax) for name, _, ax in BIG_SEGS}


def _slabs_from_full(full):
    slabs = []
    for j in range(N_CHIPS):
        shards = []
        for name, shape, ax in BIG_SEGS:
            shards.append(lax.slice_in_dim(full[name], j * shape[ax], (j + 1) * shape[ax], axis=ax))
        slabs.append(_pack_rows(shards, FLAT_ROWS))
    return jnp.stack(slabs)


def _pack_vec(d, segs, rows_multiple):
    flat = jnp.concatenate([d[n].reshape(-1) for n, _ in segs])
    rows = -(-flat.shape[0] // LANES)
    rows = -(-rows // rows_multiple) * rows_multiple
    return jnp.pad(flat, (0, rows * LANES - flat.shape[0])).reshape(rows, LANES)


def _unpack_vec(v, segs):
    flat, out, r = v.reshape(-1), {}, 0
    for n, shape in segs:
        k = math.prod(shape)
        out[n] = flat[r:r + k].reshape(shape)
        r += k
    return out


def _block_diag(blocks):
    G, a, b = blocks.shape
    eye = jnp.eye(G, dtype=blocks.dtype)
    return (eye[:, None, :, None] * blocks[:, :, None, :]).reshape(G * a, G * b)


def _diag_blocks(m, G):
    a, b = m.shape[0] // G, m.shape[1] // G
    return jnp.stack([m[g * a:(g + 1) * a, g * b:(g + 1) * b] for g in range(G)])


def _mlp_fwd(x, w1, w2, g_pre, g_post, tag):
    h = rms_fwd(x, g_pre, name=f"{tag}_pre_norm")
    a = matmul(h, w1, name=f"{tag}_up")
    s = sqrelu_fwd(a, name=f"{tag}_act")
    m = matmul(s, w2, name=f"{tag}_down")
    return rms_res_fwd(x, m, g_post, name=f"{tag}_post_norm"), (x, h, a, s, m)


def _mlp_bwd(saved, w1, w2, g_pre, g_post, dxo, tag):
    x, h, a, s, m = saved
    dm, dg_post = rms_bwd(m, g_post, dxo, None, name=f"{tag}_post_norm_bwd")
    ds = matmul(dm, w2, tb=True, name=f"{tag}_down_dx")
    dw2 = matmul(s, dm, ta=True, name=f"{tag}_down_dw")
    da = sqrelu_bwd(a, ds, name=f"{tag}_act_bwd")
    dh = matmul(da, w1, tb=True, name=f"{tag}_up_dx")
    dw1 = matmul(h, da, ta=True, name=f"{tag}_up_dw")
    dx, dg_pre = rms_bwd(x, g_pre, dh, dxo, name=f"{tag}_pre_norm_bwd")
    return dx, dw1, dw2, dg_pre, dg_post


def kernel(x, mix_pre_g, mix_post_g, mlp_pre_g, mlp_post_g, w_in_even, s5_lam_re, s5_lam_im, s5_log_dt, s5_b_re, s5_b_im, s5_c_re, s5_c_im, s5_d, s5_w_glu, fox_b_f, w_out_even, w_in_odd, pool_w, pool_scale, sgu_ln_g, sgu_ln_b, sgu_w_s, sgu_b_s, w_out_odd, mlp_w1, mlp_w2, loss_target, m_mix_pre_g, m_mix_post_g, m_mlp_pre_g, m_mlp_post_g, m_w_in_even, m_s5_lam_re, m_s5_lam_im, m_s5_log_dt, m_s5_b_re, m_s5_b_im, m_s5_c_re, m_s5_c_im, m_s5_d, m_s5_w_glu, m_fox_b_f, m_w_out_even, m_w_in_odd, m_pool_w, m_pool_scale, m_sgu_ln_g, m_sgu_ln_b, m_sgu_w_s, m_sgu_b_s, m_w_out_odd, m_mlp_w1, m_mlp_w2, v_mix_pre_g, v_mix_post_g, v_mlp_pre_g, v_mlp_post_g, v_w_in_even, v_s5_lam_re, v_s5_lam_im, v_s5_log_dt, v_s5_b_re, v_s5_b_im, v_s5_c_re, v_s5_c_im, v_s5_d, v_s5_w_glu, v_fox_b_f, v_w_out_even, v_w_in_odd, v_pool_w, v_pool_scale, v_sgu_ln_g, v_sgu_ln_b, v_sgu_w_s, v_sgu_b_s, v_w_out_odd, v_mlp_w1, v_mlp_w2):
    names = [n for n, _ in SMALL_SEGS] + [n for n, _, _ in BIG_SEGS] + list(SHARDED_SMALL)
    env = dict(locals())
    W = {n: env[n] for n in names}
    M = {n: env["m_" + n] for n in names}
    V = {n: env["v_" + n] for n in names}

    big_shard = {n: W[n][0] if W[n].shape[0] == 1 else W[n] for n, _, _ in BIG_SEGS}
    small_f32 = jnp.concatenate([W[n][0] for n in SHARDED_SMALL])
    small_bits = lax.bitcast_convert_type(small_f32, WIRE_DTYPE).reshape(-1)
    small_row = jnp.pad(small_bits, (0, FLAT_C - small_bits.shape[0]))[None, :]
    flat = _pack_rows([big_shard[n].astype(WIRE_DTYPE) for n, _, _ in BIG_SEGS] + [small_row], FLAT_ROWS)
    slabs = allgather_chip_shards(flat, name="allgather_weights")
    Wf = _full_from_slabs(slabs)
    words = jnp.dtype(F32).itemsize // jnp.dtype(WIRE_DTYPE).itemsize
    small_all = lax.bitcast_convert_type(
        slabs[:, BIG_ROWS, :small_bits.shape[0]].reshape(N_CHIPS, -1, words) if words > 1
        else slabs[:, BIG_ROWS, :small_bits.shape[0]], F32).reshape(N_CHIPS, 3, LANES)
    for i, n in enumerate(SHARDED_SMALL):
        Wf[n] = small_all[:, i, :].reshape(1, N_CHIPS * LANES)
    for n, _ in SMALL_SEGS:
        Wf[n] = W[n]

    loss8, dx0, full_grads, local_small = _local_step(x[0], loss_target[0], Wf)
    loss = lax.psum(loss8[0, 0], MESH_AXES)
    return _reduce_and_update(W, M, V, loss, dx0, full_grads, local_small)


def _local_step(x0, target, P):
    T = x0.shape[0]
    mix_pre_g, mix_post_g, mlp_pre_g, mlp_post_g = P["mix_pre_g"], P["mix_post_g"], P["mlp_pre_g"], P["mlp_post_g"]
    s5_lam_re, s5_lam_im, s5_log_dt = P["s5_lam_re"], P["s5_lam_im"], P["s5_log_dt"]
    s5_b_re, s5_b_im, s5_c_re, s5_c_im, s5_d = P["s5_b_re"], P["s5_b_im"], P["s5_c_re"], P["s5_c_im"], P["s5_d"]
    fox_b_f, pool_w, sgu_w_s, sgu_b_s = P["fox_b_f"], P["pool_w"], P["sgu_w_s"], P["sgu_b_s"]
    pool_scale_f, ln_g_f, ln_b_f = P["pool_scale"], P["sgu_ln_g"], P["sgu_ln_b"]
    w_in_e = jnp.pad(P["w_in_even"], ((0, 0), (0, EVEN_IN_PAD - EVEN_IN)))
    w_glu, w_out_e, w_in_o, w_out_o = P["s5_w_glu"], P["w_out_even"], P["w_in_odd"], P["w_out_odd"]
    w1, w2 = P["mlp_w1"], P["mlp_w2"]

    def gain(a, l):
        return a[l][None, :]

    lr = s5_lam_re[0].reshape(1, S5_LANES)
    li = s5_lam_im[0].reshape(1, S5_LANES)
    ldt = jnp.repeat(s5_log_dt[0], S5_STATE).reshape(1, S5_LANES)
    btr = s5_b_re[0].reshape(S5_LANES, S5_GROUP).T
    bti = s5_b_im[0].reshape(S5_LANES, S5_GROUP).T
    tf_re, tf_im, tb_re, tb_im, bbt_re, bbt_im = s5_disc_fwd(lr, li, ldt, btr, bti, name="s5_disc")
    same_group = (jnp.arange(S5_WIDTH)[:, None] // S5_GROUP) == (jnp.arange(S5_LANES)[None, :] // S5_STATE)
    b_bd = s5_interleave(jnp.where(same_group, jnp.tile(bbt_re, (S5_GROUPS, 1)), 0.0),
                         jnp.where(same_group, jnp.tile(bbt_im, (S5_GROUPS, 1)), 0.0), axis=1)
    cr2 = jnp.transpose(s5_c_re[0], (0, 2, 1)).reshape(S5_LANES, S5_GROUP)
    ci2 = jnp.transpose(s5_c_im[0], (0, 2, 1)).reshape(S5_LANES, S5_GROUP)
    c_bd = s5_interleave(jnp.where(same_group.T, jnp.tile(cr2, (1, S5_GROUPS)), 0.0),
                         -jnp.where(same_group.T, jnp.tile(ci2, (1, S5_GROUPS)), 0.0), axis=0)
    bf_pad = jnp.pad(fox_b_f, ((0, 0), (0, LANES - FOX_HEADS)))

    h1 = rms_fwd(x0, gain(mix_pre_g, 0), name="l0_pre_norm")
    z = matmul(h1, w_in_e, name="l0_in_proj")
    u, q, k, v, fl = z[:, 0:512], z[:, 512:1024], z[:, 1024:1536], z[:, 1536:2048], z[:, 2048:EVEN_IN_PAD]
    bu = matmul(u, b_bd, name="s5_bu")
    xs = s5_scan(bu, tf_re, tf_im, reverse=False, name="s5_scan_fwd")
    yc = matmul(xs, c_bd, name="s5_cx")
    yl, yg = s5_out_fwd(yc, u, s5_d, name="s5_out")
    gl = matmul(yg, w_glu, name="s5_glu_proj")
    ya = glu_fwd(yg, gl, name="s5_glu")
    fgate = fox_gate_fwd(fl, bf_pad, name="fox_gate")
    f_col = _pairs_col(fgate, T)
    f_row = _col_to_row(f_col, T)
    o, lse_col = fox_fwd(q, k, v, f_col, f_row, name="fox_fwd")
    ycat = jnp.concatenate([ya, o], axis=1)
    mo = matmul(ycat, w_out_e, name="l0_out_proj")
    x1 = rms_res_fwd(x0, mo, gain(mix_post_g, 0), name="l0_post_norm")
    x2, mlp0 = _mlp_fwd(x1, w1[0], w2[0], gain(mlp_pre_g, 0), gain(mlp_post_g, 0), "mlp0")

    h3 = rms_fwd(x2, gain(mix_pre_g, 1), name="l1_pre_norm")
    z2 = matmul(h3, w_in_o, name="l1_in_proj")
    xc, us, vs = z2[:, 0:512], z2[:, 512:1024], z2[:, 1024:1536]
    pooled = pool_window(xc, adjoint=False, name="pool_fwd")
    pw_bd = _block_diag(pool_w[0])
    pw = matmul(pooled, pw_bd, name="pool_proj")
    yc2 = colscale_fwd(pw, pool_scale_f, name="pool_scale")
    causal = jnp.tril(jnp.ones((CHUNK, CHUNK), dtype=bool))
    wsm = jnp.where(causal[None], sgu_w_s[0], 0.0)
    wsmt = jnp.transpose(wsm, (0, 2, 1))
    bst = sgu_b_s[0].T
    yd = sgu_fwd(us, vs, ln_g_f, ln_b_f, wsm, bst, name="sgu_fwd")
    ycat2 = jnp.concatenate([yc2, yd], axis=1)
    mo2 = matmul(ycat2, w_out_o, name="l1_out_proj")
    x3 = rms_res_fwd(x2, mo2, gain(mix_post_g, 1), name="l1_post_norm")
    x4, mlp1 = _mlp_fwd(x3, w1[1], w2[1], gain(mlp_pre_g, 1), gain(mlp_post_g, 1), "mlp1")

    loss8, dx4 = loss_fwd_bwd(x4, target, name="loss")

    dx3, dw1_1, dw2_1, dg_mlp_pre1, dg_mlp_post1 = _mlp_bwd(mlp1, w1[1], w2[1], gain(mlp_pre_g, 1), gain(mlp_post_g, 1), dx4, "mlp1")
    dmo2, dg_mix_post1 = rms_bwd(mo2, gain(mix_post_g, 1), dx3, None, name="l1_post_norm_bwd")
    dycat2 = matmul(dmo2, w_out_o, tb=True, name="l1_out_proj_dx")
    dw_out_o = matmul(ycat2, dmo2, ta=True, name="l1_out_proj_dw")
    dyc2, dyd = dycat2[:, 0:512], dycat2[:, 512:1024]
    dpw, dpool_scale = colscale_bwd(pw, pool_scale_f, dyc2, name="pool_scale_bwd")
    dpooled = matmul(dpw, pw_bd, tb=True, name="pool_proj_dx")
    dpw_bd = matmul(pooled, dpw, ta=True, name="pool_proj_dw")
    dxc = pool_window(dpooled, adjoint=True, name="pool_bwd")
    dus, dvs, dln_g, dln_b, dws, dbst = sgu_bwd(us, vs, ln_g_f, ln_b_f, wsm, wsmt, bst, dyd, name="sgu_bwd")
    dz2 = jnp.concatenate([dxc, dus, dvs], axis=1)
    dh3 = matmul(dz2, w_in_o, tb=True, name="l1_in_proj_dx")
    dw_in_o = matmul(h3, dz2, ta=True, name="l1_in_proj_dw")
    dx2, dg_mix_pre1 = rms_bwd(x2, gain(mix_pre_g, 1), dh3, dx3, name="l1_pre_norm_bwd")

    dx1, dw1_0, dw2_0, dg_mlp_pre0, dg_mlp_post0 = _mlp_bwd(mlp0, w1[0], w2[0], gain(mlp_pre_g, 0), gain(mlp_post_g, 0), dx2, "mlp0")
    dmo, dg_mix_post0 = rms_bwd(mo, gain(mix_post_g, 0), dx1, None, name="l0_post_norm_bwd")
    dycat = matmul(dmo, w_out_e, tb=True, name="l0_out_proj_dx")
    dw_out_e = matmul(ycat, dmo, ta=True, name="l0_out_proj_dw")
    dya, do = dycat[:, 0:512], dycat[:, 512:1024]
    dyg_a, dgl = glu_bwd(yg, gl, dya, name="s5_glu_bwd")
    dyg_b = matmul(dgl, w_glu, tb=True, name="s5_glu_proj_dx")
    dw_glu = matmul(yg, dgl, ta=True, name="s5_glu_proj_dw")
    dyl, du_skip, dd = s5_out_bwd(yl, u, s5_d, dyg_a, dyg_b, name="s5_out_bwd")
    dxs = matmul(dyl, c_bd, tb=True, name="s5_cx_dx")
    dc_bd = matmul(xs, dyl, ta=True, name="s5_cx_dw")
    lam = s5_scan(dxs, tb_re, tb_im, reverse=True, name="s5_scan_bwd")
    dab_re, dab_im = s5_da(lam, xs, name="s5_da")
    db_bd = matmul(u, lam, ta=True, name="s5_bu_dw")
    du_b = matmul(lam, b_bd, tb=True, name="s5_bu_dx")
    du = add2(du_skip, du_b, name="s5_du")
    dq, dd_col, dfq_col = fox_bwd_q(q, k, v, o, do, f_col, f_row, lse_col, name="fox_bwd_q")
    dk, dv, df_col = fox_bwd_kv(q, k, v, do, f_col, f_row, _col_to_row(lse_col, T), _col_to_row(dd_col, T), dfq_col,
                                name="fox_bwd_kv")
    dfl, dbf = fox_gate_bwd(fl, bf_pad, _pairs_to_lanes(df_col, T), name="fox_gate_bwd")
    dz = jnp.concatenate([du, dq, dk, dv, dfl], axis=1)
    dh1 = matmul(dz, w_in_e, tb=True, name="l0_in_proj_dx")
    dw_in_e = matmul(h1, dz, ta=True, name="l0_in_proj_dw")[:, :EVEN_IN]
    dx0, dg_mix_pre0 = rms_bwd(x0, gain(mix_pre_g, 0), dh1, dx1, name="l0_pre_norm_bwd")

    db_re_bd, db_im_bd = s5_deinterleave(db_bd, axis=1)
    dbbt_re = jnp.where(same_group, db_re_bd, 0.0).reshape(S5_GROUPS, S5_GROUP, S5_LANES).sum(0)
    dbbt_im = jnp.where(same_group, db_im_bd, 0.0).reshape(S5_GROUPS, S5_GROUP, S5_LANES).sum(0)
    dlr, dli, dldt8, dbtr, dbti = s5_disc_bwd(lr, li, ldt, btr, bti, dab_re, dab_im, dbbt_re, dbbt_im, name="s5_disc_bwd")
    dc_re_bd, dc_im_bd = s5_deinterleave(dc_bd, axis=0)
    dcr2 = jnp.where(same_group.T, dc_re_bd, 0.0).reshape(S5_LANES, S5_GROUPS, S5_GROUP).sum(1)
    dci2 = -jnp.where(same_group.T, dc_im_bd, 0.0).reshape(S5_LANES, S5_GROUPS, S5_GROUP).sum(1)

    def c_layout(a):
        return jnp.transpose(a.reshape(S5_GROUPS, S5_STATE, S5_GROUP), (0, 2, 1))[None]

    def b_layout(a):
        return a.T.reshape(1, S5_GROUPS, S5_STATE, S5_GROUP)

    local_small = {
        "mix_pre_g": jnp.concatenate([dg_mix_pre0, dg_mix_pre1]), "mix_post_g": jnp.concatenate([dg_mix_post0, dg_mix_post1]),
        "mlp_pre_g": jnp.concatenate([dg_mlp_pre0, dg_mlp_pre1]), "mlp_post_g": jnp.concatenate([dg_mlp_post0, dg_mlp_post1]),
        "s5_lam_re": dlr.reshape(1, S5_GROUPS, S5_STATE), "s5_lam_im": dli.reshape(1, S5_GROUPS, S5_STATE),
        "s5_log_dt": dldt8[0:1, 0:S5_GROUPS],
        "s5_b_re": b_layout(dbtr), "s5_b_im": b_layout(dbti), "s5_c_re": c_layout(dcr2), "s5_c_im": c_layout(dci2),
        "s5_d": dd, "fox_b_f": dbf[:, 0:FOX_HEADS],
        "pool_w": _diag_blocks(dpw_bd, len(POOL_WINDOWS))[None],
        "sgu_w_s": jnp.where(causal[None], dws, 0.0)[None], "sgu_b_s": dbst.T[None],
        "pool_scale": dpool_scale, "sgu_ln_g": dln_g, "sgu_ln_b": dln_b,
    }
    full_grads = {"w_in_even": dw_in_e, "s5_w_glu": dw_glu, "w_out_even": dw_out_e, "w_in_odd": dw_in_o,
                  "w_out_odd": dw_out_o, "mlp_w1": jnp.stack([dw1_0, dw1_1]), "mlp_w2": jnp.stack([dw2_0, dw2_1])}
    return loss8, dx0, full_grads, local_small


def _reduce_and_update(W, M, V, loss, dx0, full_grads, local_small):
    cx, cy, cc = _coords()
    chip = 2 * cx + cy

    vec = _pack_vec(local_small, REDUCED_SEGS, N_DEV * SUBLANES)
    piece = vec.shape[0] // N_DEV
    landed = exchange_pieces(vec.reshape(N_DEV, piece, LANES), scatter=True, name="small_grads_scatter")
    mine = sum_pieces(landed, name="small_grads_sum")
    everyone = exchange_pieces(mine, scatter=False, name="small_grads_gather")
    G = _unpack_vec(everyone, REDUCED_SEGS)
    for n in SHARDED_SMALL:
        G[n] = lax.dynamic_slice_in_dim(G[n], chip * LANES, LANES, axis=1)

    gslabs = _slabs_from_full(full_grads)
    from_sibling = swap_sibling_halves(gslabs.astype(WIRE_DTYPE), name="big_grads_to_sibling")
    pf, pb = add_sibling_half(gslabs, from_sibling, cc.reshape(1).astype(jnp.int32), name="big_grads_chip_sum")
    from_chips = send_chip_partials(pb, name="big_grads_to_chips")
    my_half = add_chip_partials(pf, from_chips, chip.reshape(1).astype(jnp.int32), name="big_grads_sum")
    reduced = join_sibling_halves(my_half, name="big_grads_join").reshape(FLAT_ROWS, FLAT_C)
    G.update(_unpack_slab(reduced))

    def two_d(a):
        return a.reshape(-1, a.shape[-1])

    delta, new_m, new_v = {}, {}, {}
    for n, shape, _ in BIG_SEGS:
        d_, m_, v_ = adamw(two_d(W[n]), two_d(G[n]), two_d(M[n]), two_d(V[n]), name=f"adamw_{n}")
        delta[n], new_m[n], new_v[n] = (t.reshape(W[n].shape) for t in (d_, m_, v_))
        G[n] = G[n].reshape(W[n].shape)
    packed = [_pack_vec(src, SMALL_SEGS, SUBLANES) for src in (W, G, M, V)]
    outs = adamw(*packed, name="adamw_replicated")
    for dst, t in zip((delta, new_m, new_v), outs):
        dst.update(_unpack_vec(t, SMALL_SEGS))
    sharded_segs = tuple((n, (1, LANES)) for n in SHARDED_SMALL)
    packed = [_pack_vec(src, sharded_segs, 1) for src in (W, G, M, V)]
    outs = adamw(*packed, name="adamw_sharded_vectors")
    for dst, t in zip((delta, new_m, new_v), outs):
        dst.update(_unpack_vec(t, sharded_segs))

    order = ["mix_pre_g", "mix_post_g", "mlp_pre_g", "mlp_post_g", "w_in_even", "s5_lam_re", "s5_lam_im", "s5_log_dt",
             "s5_b_re", "s5_b_im", "s5_c_re", "s5_c_im", "s5_d", "s5_w_glu", "fox_b_f", "w_out_even", "w_in_odd",
             "pool_w", "pool_scale", "sgu_ln_g", "sgu_ln_b", "sgu_w_s", "sgu_b_s", "w_out_odd", "mlp_w1", "mlp_w2"]
    return (loss, dx0[None], *[G[n] for n in order], *[delta[n] for n in order],
            *[new_m[n] for n in order], *[new_v[n] for n in order])
```

```python
import functools
import math

import jax
import jax.numpy as jnp
from jax import lax
from jax.experimental import pallas as pl
from jax.experimental.pallas import tpu as pltpu

F32 = jnp.float32
MXU_DTYPE = jnp.bfloat16
WIRE_DTYPE = jnp.bfloat16
EPS = 1e-6
VMEM_LIMIT_BYTES = 48 * 1024 * 1024
LANES = 128
SUBLANES = 8

D_MODEL = 1024
S5_WIDTH = 512
S5_GROUP = 16
S5_GROUPS = 32
S5_STATE = 64
S5_LANES = S5_GROUPS * S5_STATE
FOX_HEADS = 8
FOX_HEAD_DIM = 64
FOX_WIDTH = 512
EVEN_IN = S5_WIDTH + 3 * FOX_WIDTH + FOX_HEADS
EVEN_IN_PAD = 2176
POOL_WINDOWS = (2, 4, 8, 16)
POOL_HALO = 16
POOL_GROUP_DIM = 128
SGU_GROUPS = 4
SGU_GROUP_DIM = 128
CHUNK = 128
D_FF = 4096

ADAM_LR = 0.001
ADAM_B1 = 0.9
ADAM_B2 = 0.999
ADAM_EPS = 1e-08
ADAM_WD = 0.01
ADAM_STEP = 10

MESH_AXES = ("x", "y", "c")
MESH = pl.DeviceIdType.MESH
N_CHIPS = 4
N_DEV = 8

SDS = jax.ShapeDtypeStruct


def _cp(*sem):
    return pltpu.CompilerParams(dimension_semantics=sem, vmem_limit_bytes=VMEM_LIMIT_BYTES)


def _pick(dim, pref):
    if dim <= pref:
        return dim
    t = pref
    while t >= 256:
        if dim % t == 0:
            return t
        t //= 2
    return dim


def _row(tr, c):
    return pl.BlockSpec((tr, c), lambda i: (i, 0))


def _full(shape):
    nd = len(shape)
    return pl.BlockSpec(shape, lambda *_: (0,) * nd)


def _gelu_grad(x):
    c = math.sqrt(2.0 / math.pi)
    t = jnp.tanh(c * (x + 0.044715 * x * x * x))
    return 0.5 * (1.0 + t) + 0.5 * x * (1.0 - t * t) * c * (1.0 + 3.0 * 0.044715 * x * x)


def matmul(a, b, *, name, ta=False, tb=False, out_dtype=F32, tm=1024, tn=512, tk=1024, mnk=None, a_koff=0,
           b_spec=None, o_spec=None, o_shape=None, prev=None, epi=None, epi_in=(), out_dtypes=None):
    if mnk is None:
        M, K = (a.shape[1], a.shape[0]) if ta else a.shape
        K2, N = (b.shape[1], b.shape[0]) if tb else b.shape
        assert K == K2, (a.shape, b.shape, ta, tb)
    else:
        M, N, K = mnk
    tm, tn, tk = _pick(M, tm), _pick(N, tn), _pick(K, tk)
    nk = K // tk
    assert a_koff % tk == 0 and not (ta and a_koff)
    ko = a_koff // tk
    dn = (((0 if ta else 1,), (1 if tb else 0,)), ((), ()))
    out_dtypes = tuple(out_dtypes) if out_dtypes is not None else (out_dtype,)
    n_out, n_epi = len(out_dtypes), len(epi_in)

    def body(*refs):
        a_ref, b_ref = refs[0], refs[1]
        epi_refs = refs[2:2 + n_epi]
        o_refs = refs[len(refs) - 1 - n_out:len(refs) - 1]
        acc_ref = refs[-1]
        k = pl.program_id(2)

        @pl.when(k == 0)
        def _():
            acc_ref[...] = jnp.zeros_like(acc_ref)

        acc_ref[...] += lax.dot_general(a_ref[...].astype(MXU_DTYPE), b_ref[...].astype(MXU_DTYPE), dn,
                                        preferred_element_type=F32)

        @pl.when(k == nk - 1)
        def _():
            res = (acc_ref[...],) if epi is None else epi(acc_ref[...], *[r[...] for r in epi_refs])
            for o_ref, r in zip(o_refs, res):
                o_ref[...] = r.astype(o_ref.dtype)

    a_spec = pl.BlockSpec((tk, tm), lambda i, j, k: (k, i)) if ta else pl.BlockSpec((tm, tk), lambda i, j, k: (i, k + ko))
    if b_spec is None:
        bs = pl.BlockSpec((tn, tk), lambda i, j, k: (j, k)) if tb else pl.BlockSpec((tk, tn), lambda i, j, k: (k, j))
    else:
        bs = b_spec(tm, tn, tk)
    tile = pl.BlockSpec((tm, tn), lambda i, j, k: (i, j))
    os_ = tile if o_spec is None else o_spec(tm, tn, tk)
    ins, in_specs, aliases = [a, b, *epi_in], [a_spec, bs] + [tile] * n_epi, {}
    if prev is not None:
        aliases = {len(ins): 0}
        ins.append(prev)
        in_specs.append(pl.BlockSpec(memory_space=pl.ANY))
    shapes = [SDS((M, N) if o_shape is None else o_shape, dt) for dt in out_dtypes]
    outs = pl.pallas_call(
        body, name=name, grid=(M // tm, N // tn, nk),
        in_specs=in_specs, out_specs=[os_] * n_out, out_shape=shapes, input_output_aliases=aliases,
        scratch_shapes=[pltpu.VMEM((tm, tn), F32)],
        compiler_params=_cp("parallel", "parallel", "arbitrary"),
    )(*ins)
    return outs[0] if n_out == 1 else outs


def _rms_hat(x):
    return x * lax.rsqrt(jnp.mean(x * x, axis=-1, keepdims=True) + EPS)


def rms_fwd(x, g, *, name):
    T, D = x.shape
    tr = _pick(T, 512)

    def body(x_ref, g_ref, o_ref):
        o_ref[...] = (_rms_hat(x_ref[...]) * g_ref[...]).astype(o_ref.dtype)

    return pl.pallas_call(body, name=name, grid=(T // tr,), in_specs=[_row(tr, D), _full((1, D))],
                          out_specs=_row(tr, D), out_shape=SDS((T, D), MXU_DTYPE), compiler_params=_cp("parallel"))(x, g)


def rms_res_fwd(x, y, g, *, name):
    T, D = x.shape
    tr = _pick(T, 512)

    def body(x_ref, y_ref, g_ref, o_ref):
        o_ref[...] = x_ref[...] + _rms_hat(y_ref[...]) * g_ref[...]

    return pl.pallas_call(body, name=name, grid=(T // tr,), in_specs=[_row(tr, D), _row(tr, D), _full((1, D))],
                          out_specs=_row(tr, D), out_shape=SDS((T, D), F32), compiler_params=_cp("parallel"))(x, y, g)


def rms_bwd(x, g, dy, res, *, name):
    T, D = x.shape
    tr = _pick(T, 512)
    has_res = res is not None

    def body(*refs):
        if has_res:
            x_ref, g_ref, dy_ref, res_ref, dx_ref, dg_ref = refs
        else:
            x_ref, g_ref, dy_ref, dx_ref, dg_ref = refs
        xv = x_ref[...]
        r = lax.rsqrt(jnp.mean(xv * xv, axis=-1, keepdims=True) + EPS)
        xh = xv * r
        dyv = dy_ref[...]
        dxh = dyv * g_ref[...]
        dx = r * (dxh - xh * jnp.mean(dxh * xh, axis=-1, keepdims=True))
        if has_res:
            dx = dx + res_ref[...]
        dx_ref[...] = dx

        @pl.when(pl.program_id(0) == 0)
        def _():
            dg_ref[...] = jnp.zeros_like(dg_ref)

        dg_ref[...] += jnp.sum(dyv * xh, axis=0, keepdims=True)

    ins = [x, g, dy] + ([res] if has_res else [])
    in_specs = [_row(tr, D), _full((1, D)), _row(tr, D)] + ([_row(tr, D)] if has_res else [])
    return pl.pallas_call(body, name=name, grid=(T // tr,), in_specs=in_specs,
                          out_specs=[_row(tr, D), _full((1, D))],
                          out_shape=[SDS((T, D), F32), SDS((1, D), F32)], compiler_params=_cp("arbitrary"))(*ins)


def loss_fwd_bwd(y, target, *, name):
    T, D = y.shape
    tr = _pick(T, 512)

    def body(y_ref, t_ref, l_ref, dy_ref):
        err = y_ref[...] - t_ref[...]
        dy_ref[...] = err * (1.0 / D)

        @pl.when(pl.program_id(0) == 0)
        def _():
            l_ref[...] = jnp.zeros_like(l_ref)

        l_ref[...] += 0.5 * jnp.sum(jnp.mean(err * err, axis=-1, keepdims=True))

    return pl.pallas_call(body, name=name, grid=(T // tr,), in_specs=[_row(tr, D), _row(tr, D)],
                          out_specs=[_full((SUBLANES, LANES)), _row(tr, D)],
                          out_shape=[SDS((SUBLANES, LANES), F32), SDS((T, D), F32)],
                          compiler_params=_cp("arbitrary"))(y, target)


def _s5_disc(lr, li, ldt, btr, bti):
    dt = jnp.exp(ldt)
    k = lax.broadcasted_iota(jnp.int32, (SUBLANES, S5_LANES), 0).astype(F32)
    kf = k + 1.0
    kb = 8.0 - k
    ph = li * dt
    lm = lr * dt
    tf_re = jnp.exp(kf * lm) * jnp.cos(kf * ph)
    tf_im = jnp.exp(kf * lm) * jnp.sin(kf * ph)
    tb_re = jnp.exp(kb * lm) * jnp.cos(kb * ph)
    tb_im = -jnp.exp(kb * lm) * jnp.sin(kb * ph)
    mag = jnp.exp(lm)
    ab_re = mag * jnp.cos(ph)
    ab_im = mag * jnp.sin(ph)
    den = lr * lr + li * li
    nr = ab_re - 1.0
    ni = ab_im
    q_re = (nr * lr + ni * li) / den
    q_im = (ni * lr - nr * li) / den
    bbt_re = q_re * btr - q_im * bti
    bbt_im = q_re * bti + q_im * btr
    return tf_re, tf_im, tb_re, tb_im, bbt_re, bbt_im


def _s5_disc_core(lr, li, ldt, btr, bti):
    dt = jnp.exp(ldt)
    mag = jnp.exp(lr * dt)
    ab_re = mag * jnp.cos(li * dt)
    ab_im = mag * jnp.sin(li * dt)
    den = lr * lr + li * li
    nr = ab_re - 1.0
    ni = ab_im
    q_re = (nr * lr + ni * li) / den
    q_im = (ni * lr - nr * li) / den
    return ab_re, ab_im, q_re * btr - q_im * bti, q_re * bti + q_im * btr


def s5_disc_fwd(lr, li, ldt, btr, bti, *, name):
    def body(lr_ref, li_ref, ldt_ref, btr_ref, bti_ref, *outs):
        vals = _s5_disc(lr_ref[...], li_ref[...], ldt_ref[...], btr_ref[...], bti_ref[...])
        for o, v in zip(outs, vals):
            o[...] = v

    tab = SDS((SUBLANES, S5_LANES), F32)
    bb = SDS((S5_GROUP, S5_LANES), F32)
    return pl.pallas_call(body, name=name, out_shape=[tab, tab, tab, tab, bb, bb])(lr, li, ldt, btr, bti)


def s5_disc_bwd(lr, li, ldt, btr, bti, dab_re, dab_im, dbbt_re, dbbt_im, *, name):
    def body(lr_ref, li_ref, ldt_ref, btr_ref, bti_ref, dar_ref, dai_ref, dbr_ref, dbi_ref,
             dlr_ref, dli_ref, dldt_ref, dbtr_ref, dbti_ref):
        _, vjp = jax.vjp(_s5_disc_core, lr_ref[...], li_ref[...], ldt_ref[...], btr_ref[...], bti_ref[...])
        dlr, dli, dldt, dbtr, dbti = vjp((dar_ref[...], dai_ref[...], dbr_ref[...], dbi_ref[...]))
        dlr_ref[...] = dlr
        dli_ref[...] = dli
        dbtr_ref[...] = dbtr
        dbti_ref[...] = dbti
        lane_group = lax.broadcasted_iota(jnp.int32, (S5_LANES, LANES), 0) // S5_STATE
        col = lax.broadcasted_iota(jnp.int32, (S5_LANES, LANES), 1)
        ind = (lane_group == col).astype(F32)
        dldt_ref[...] = jnp.dot(jnp.broadcast_to(dldt, (SUBLANES, S5_LANES)), ind,
                                precision=lax.Precision.HIGHEST, preferred_element_type=F32)

    row = SDS((1, S5_LANES), F32)
    bb = SDS((S5_GROUP, S5_LANES), F32)
    return pl.pallas_call(body, name=name, out_shape=[row, row, SDS((SUBLANES, LANES), F32), bb, bb])(
        lr, li, ldt, btr, bti, dab_re, dab_im, dbbt_re, dbbt_im)


S5_NB = 1024


def s5_interleave(re, im, axis):
    parts = []
    for n in range(S5_LANES // S5_NB):
        sl = [slice(None)] * re.ndim
        sl[axis] = slice(n * S5_NB, (n + 1) * S5_NB)
        parts += [re[tuple(sl)], im[tuple(sl)]]
    return jnp.concatenate(parts, axis=axis)


def s5_deinterleave(a, axis):
    re, im = [], []
    for n in range(S5_LANES // S5_NB):
        sl = [slice(None)] * a.ndim
        sl[axis] = slice(2 * n * S5_NB, (2 * n + 1) * S5_NB)
        re.append(a[tuple(sl)])
        sl[axis] = slice((2 * n + 1) * S5_NB, (2 * n + 2) * S5_NB)
        im.append(a[tuple(sl)])
    return jnp.concatenate(re, axis=axis), jnp.concatenate(im, axis=axis)


def s5_scan(bu, tab_re, tab_im, *, reverse, name):
    T = bu.shape[0]
    nb = S5_NB
    tc = _pick(T, 256)
    nl = S5_LANES // nb
    nt = T // tc
    ntile = tc // SUBLANES
    step_rows = ((1, 7), (2, 6), (4, 4)) if reverse else ((1, 0), (2, 1), (4, 3))

    def body(br_ref, bi_ref, tr_ref, ti_ref, xo_ref, cr_ref, ci_ref):
        @pl.when(pl.program_id(1) == 0)
        def _():
            cr_ref[...] = jnp.zeros_like(cr_ref)
            ci_ref[...] = jnp.zeros_like(ci_ref)

        tr = tr_ref[...]
        ti = ti_ref[...]
        io = lax.broadcasted_iota(jnp.int32, (SUBLANES, nb), 0)
        steps = [(d, tr_ref[r:r + 1, :], ti_ref[r:r + 1, :]) for d, r in step_rows]

        def tile(i, carry):
            cr, ci = carry
            j = (ntile - 1 - i) if reverse else i
            r0 = pl.multiple_of(j * SUBLANES, SUBLANES)
            xr = br_ref[pl.ds(r0, SUBLANES), :]
            xi = bi_ref[pl.ds(r0, SUBLANES), :]
            for d, pr, pi in steps:
                if reverse:
                    keep = io < SUBLANES - d
                    sh = SUBLANES - d
                else:
                    keep = io >= d
                    sh = d
                sr = jnp.where(keep, pltpu.roll(xr, sh, 0), 0.0)
                si = jnp.where(keep, pltpu.roll(xi, sh, 0), 0.0)
                xr, xi = xr + pr * sr - pi * si, xi + pr * si + pi * sr
            xr, xi = xr + tr * cr - ti * ci, xi + tr * ci + ti * cr
            xo_ref[pl.ds(r0, SUBLANES), 0:nb] = xr
            xo_ref[pl.ds(r0, SUBLANES), nb:2 * nb] = xi
            if reverse:
                return xr[0:1, :], xi[0:1, :]
            return xr[SUBLANES - 1:SUBLANES, :], xi[SUBLANES - 1:SUBLANES, :]

        cr, ci = lax.fori_loop(0, ntile, tile, (cr_ref[0:1, :], ci_ref[0:1, :]))
        cr_ref[0:1, :] = cr
        ci_ref[0:1, :] = ci

    def tmap(t):
        return (nt - 1 - t) if reverse else t

    re_spec = pl.BlockSpec((tc, nb), lambda n, t: (tmap(t), 2 * n))
    im_spec = pl.BlockSpec((tc, nb), lambda n, t: (tmap(t), 2 * n + 1))
    tab_spec = pl.BlockSpec((SUBLANES, nb), lambda n, t: (0, n))
    return pl.pallas_call(
        body, name=name, grid=(nl, nt), in_specs=[re_spec, im_spec, tab_spec, tab_spec],
        out_specs=pl.BlockSpec((tc, 2 * nb), lambda n, t: (tmap(t), n)), out_shape=SDS((T, 2 * S5_LANES), F32),
        scratch_shapes=[pltpu.VMEM((SUBLANES, nb), F32), pltpu.VMEM((SUBLANES, nb), F32)],
        compiler_params=_cp("parallel", "arbitrary"),
    )(bu, bu, tab_re, tab_im)


def s5_da(lam, xs, *, name):
    T = xs.shape[0]
    nb = S5_NB
    tc = _pick(T, 256)
    nl, nt = S5_LANES // nb, T // tc
    hb = tc // SUBLANES

    def body(lr_ref, li_ref, xr_ref, xi_ref, hr_ref, hi_ref, dar_ref, dai_ref):
        t = pl.program_id(1)

        @pl.when(t == 0)
        def _():
            dar_ref[...] = jnp.zeros_like(dar_ref)
            dai_ref[...] = jnp.zeros_like(dai_ref)

        io = lax.broadcasted_iota(jnp.int32, (tc, nb), 0)
        first = jnp.where(t > 0, 1.0, 0.0)
        pr = jnp.where(io >= 1, pltpu.roll(xr_ref[...], 1, 0), hr_ref[SUBLANES - 1:SUBLANES, :] * first)
        pi = jnp.where(io >= 1, pltpu.roll(xi_ref[...], 1, 0), hi_ref[SUBLANES - 1:SUBLANES, :] * first)
        lr = lr_ref[...]
        li = li_ref[...]
        dar_ref[...] += jnp.sum(lr * pr + li * pi, axis=0, keepdims=True)
        dai_ref[...] += jnp.sum(li * pr - lr * pi, axis=0, keepdims=True)

    re_blk = pl.BlockSpec((tc, nb), lambda n, t: (t, 2 * n))
    im_blk = pl.BlockSpec((tc, nb), lambda n, t: (t, 2 * n + 1))
    re_halo = pl.BlockSpec((SUBLANES, nb), lambda n, t: (jnp.maximum(t * hb - 1, 0), 2 * n))
    im_halo = pl.BlockSpec((SUBLANES, nb), lambda n, t: (jnp.maximum(t * hb - 1, 0), 2 * n + 1))
    acc = pl.BlockSpec((1, nb), lambda n, t: (0, n))
    row = SDS((1, S5_LANES), F32)
    return pl.pallas_call(body, name=name, grid=(nl, nt), in_specs=[re_blk, im_blk, re_blk, im_blk, re_halo, im_halo],
                          out_specs=[acc, acc], out_shape=[row, row],
                          compiler_params=_cp("parallel", "arbitrary"))(lam, lam, xs, xs, xs, xs)


def s5_out_fwd(yc, u, d, *, name):
    T, C = yc.shape
    tr = _pick(T, 512)

    def body(yc_ref, u_ref, d_ref, yl_ref, yg_ref):
        yl = yc_ref[...] + d_ref[...] * u_ref[...]
        yl_ref[...] = yl
        yg_ref[...] = jax.nn.gelu(yl)

    return pl.pallas_call(body, name=name, grid=(T // tr,), in_specs=[_row(tr, C), _row(tr, C), _full((1, C))],
                          out_specs=[_row(tr, C)] * 2, out_shape=[SDS((T, C), F32)] * 2,
                          compiler_params=_cp("parallel"))(yc, u, d)


def glu_fwd(yg, gl, *, out_cols, name):
    T, C = yg.shape
    tr = _pick(T, 512)

    def body(yg_ref, gl_ref, o_ref):
        o_ref[...] = yg_ref[...] * jax.nn.sigmoid(gl_ref[...])

    return pl.pallas_call(body, name=name, grid=(T // tr,), in_specs=[_row(tr, C)] * 2, out_specs=_row(tr, C),
                          out_shape=SDS((T, out_cols), F32), compiler_params=_cp("parallel"))(yg, gl)


def glu_bwd(yg, gl, dy, *, name):
    T, C = yg.shape
    tr = _pick(T, 512)

    def body(yg_ref, gl_ref, dy_ref, dyg_ref, dgl_ref):
        s = jax.nn.sigmoid(gl_ref[...])
        dyv = dy_ref[...]
        dyg_ref[...] = dyv * s
        dgl_ref[...] = dyv * yg_ref[...] * s * (1.0 - s)

    return pl.pallas_call(body, name=name, grid=(T // tr,), in_specs=[_row(tr, C)] * 3, out_specs=[_row(tr, C)] * 2,
                          out_shape=[SDS((T, C), F32)] * 2, compiler_params=_cp("parallel"))(yg, gl, dy)


def s5_out_bwd(yl, u, d, dyg_a, dyg_b, *, name):
    T, C = yl.shape
    tr = _pick(T, 512)

    def body(yl_ref, u_ref, d_ref, da_ref, db_ref, dyl_ref, du_ref, dd_ref):
        dyl = (da_ref[...] + db_ref[...]) * _gelu_grad(yl_ref[...])
        dyl_ref[...] = dyl
        du_ref[...] = dyl * d_ref[...]

        @pl.when(pl.program_id(0) == 0)
        def _():
            dd_ref[...] = jnp.zeros_like(dd_ref)

        dd_ref[...] += jnp.sum(dyl * u_ref[...], axis=0, keepdims=True)

    return pl.pallas_call(body, name=name, grid=(T // tr,),
                          in_specs=[_row(tr, C), _row(tr, C), _full((1, C)), _row(tr, C), _row(tr, C)],
                          out_specs=[_row(tr, C), _row(tr, C), _full((1, C))],
                          out_shape=[SDS((T, C), F32), SDS((T, C), F32), SDS((1, C), F32)],
                          compiler_params=_cp("arbitrary"))(yl, u, d, dyg_a, dyg_b)


def add2(a, b, *, name):
    T, C = a.shape
    tr = _pick(T, 512)

    def body(a_ref, b_ref, o_ref):
        o_ref[...] = a_ref[...] + b_ref[...]

    return pl.pallas_call(body, name=name, grid=(T // tr,), in_specs=[_row(tr, C)] * 2, out_specs=_row(tr, C),
                          out_shape=SDS((T, C), F32), compiler_params=_cp("parallel"))(a, b)


def _tri(n, upper):
    r = lax.broadcasted_iota(jnp.int32, (n, n), 0)
    c = lax.broadcasted_iota(jnp.int32, (n, n), 1)
    return ((c >= r) if upper else (c <= r)).astype(F32)


def fox_gate_fwd(fl, bf, *, fl_col, name):
    T = fl.shape[0]
    tb = _pick(T, 256)

    def body(fl_ref, bf_ref, f_ref, c_ref):
        @pl.when(pl.program_id(0) == 0)
        def _():
            c_ref[...] = jnp.zeros_like(c_ref)

        lf = jax.nn.log_sigmoid(fl_ref[...] + bf_ref[...])
        f = jnp.dot(_tri(tb, False), lf, precision=lax.Precision.HIGHEST, preferred_element_type=F32) + c_ref[0:1, :]
        f_ref[...] = f
        c_ref[0:1, :] = f[tb - 1:tb, :]

    fl_spec = pl.BlockSpec((tb, LANES), lambda i: (i, fl_col))
    return pl.pallas_call(body, name=name, grid=(T // tb,), in_specs=[fl_spec, _full((1, LANES))],
                          out_specs=_row(tb, LANES), out_shape=SDS((T, LANES), F32),
                          scratch_shapes=[pltpu.VMEM((SUBLANES, LANES), F32)], compiler_params=_cp("arbitrary"))(fl, bf)


def fox_gate_bwd(fl, bf, df, *, fl_col, name):
    T = fl.shape[0]
    tb = _pick(T, 256)
    nt = T // tb

    def body(fl_ref, bf_ref, df_ref, dfl_ref, dbf_ref, c_ref):
        @pl.when(pl.program_id(0) == 0)
        def _():
            c_ref[...] = jnp.zeros_like(c_ref)
            dbf_ref[...] = jnp.zeros_like(dbf_ref)

        dlf = jnp.dot(_tri(tb, True), df_ref[...], precision=lax.Precision.HIGHEST, preferred_element_type=F32) + c_ref[0:1, :]
        c_ref[0:1, :] = dlf[0:1, :]
        dfl = dlf * jax.nn.sigmoid(-(fl_ref[...] + bf_ref[...]))
        dfl_ref[...] = dfl
        dbf_ref[...] += jnp.sum(dfl, axis=0, keepdims=True)

    rev = pl.BlockSpec((tb, LANES), lambda i: (nt - 1 - i, 0))
    fl_rev = pl.BlockSpec((tb, LANES), lambda i: (nt - 1 - i, fl_col))
    return pl.pallas_call(body, name=name, grid=(nt,), in_specs=[fl_rev, _full((1, LANES)), rev],
                          out_specs=[rev, _full((1, LANES))], out_shape=[SDS((T, LANES), F32), SDS((1, LANES), F32)],
                          scratch_shapes=[pltpu.VMEM((SUBLANES, LANES), F32)], compiler_params=_cp("arbitrary"))(fl, bf, df)


FOX_BLOCK = 512
FOX_PAIRS = FOX_HEADS // 2
_NT = (((1,), (1,)), ((), ()))


def _fox_block(T):
    return _pick(T, FOX_BLOCK)


Q_TILE0, K_TILE0, V_TILE0, O_TILE0 = 4, 8, 12, 4
FL_TILE = 16
POOL_COL = 2


def fox_fwd(z, f_col, f_row, ycat, *, name):
    T = z.shape[0]
    blk = _fox_block(T)
    nb = T // blk
    scale = FOX_HEAD_DIM ** -0.5

    def body(q_ref, k_ref, v_ref, fc_ref, fr_ref, prev_ref, o_ref, l_ref):
        i = pl.program_id(1)
        row = lax.broadcasted_iota(jnp.int32, (blk, blk), 0)
        col = lax.broadcasted_iota(jnp.int32, (blk, blk), 1)
        for hh in range(2):
            ls = slice(hh * FOX_HEAD_DIM, (hh + 1) * FOX_HEAD_DIM)
            qh = (q_ref[:, ls] * scale).astype(MXU_DTYPE)
            fi = fc_ref[0, :, hh:hh + 1]

            def step(j, carry, masked):
                m, l, acc = carry
                r0 = pl.multiple_of(j * blk, blk)
                kj = k_ref[pl.ds(r0, blk), ls].astype(MXU_DTYPE)
                vj = v_ref[pl.ds(r0, blk), ls].astype(MXU_DTYPE)
                s = lax.dot_general(qh, kj, _NT, preferred_element_type=F32) + (fi - fr_ref[0, j, hh:hh + 1, :])
                if masked:
                    s = jnp.where(col <= row, s, -jnp.inf)
                m_new = jnp.maximum(m, jnp.max(s, axis=-1, keepdims=True))
                p = jnp.exp(s - m_new)
                alpha = jnp.exp(m - m_new)
                l = alpha * l + jnp.sum(p, axis=-1, keepdims=True)
                acc = alpha * acc + jnp.dot(p.astype(MXU_DTYPE), vj, preferred_element_type=F32)
                return m_new, l, acc

            init = (jnp.full((blk, 1), -jnp.inf, F32), jnp.zeros((blk, 1), F32), jnp.zeros((blk, FOX_HEAD_DIM), F32))
            carry = lax.fori_loop(0, i, lambda j, c: step(j, c, False), init)
            m, l, acc = step(i, carry, True)
            o_ref[:, ls] = acc / l
            l_ref[0, :, hh:hh + 1] = m + jnp.log(l)

    qspec = pl.BlockSpec((blk, LANES), lambda h, i: (i, Q_TILE0 + h))
    kspec = pl.BlockSpec((T, LANES), lambda h, i: (0, K_TILE0 + h))
    vspec = pl.BlockSpec((T, LANES), lambda h, i: (0, V_TILE0 + h))
    ospec = pl.BlockSpec((blk, LANES), lambda h, i: (i, O_TILE0 + h))
    cspec = pl.BlockSpec((1, blk, 2), lambda h, i: (h, i, 0))
    rspec = pl.BlockSpec((1, nb, 2, blk), lambda h, i: (h, 0, 0, 0))
    return pl.pallas_call(body, name=name, grid=(FOX_PAIRS, nb), in_specs=[qspec, kspec, vspec, cspec, rspec, ANY],
                          out_specs=[ospec, cspec], out_shape=[SDS(ycat.shape, F32), SDS((FOX_PAIRS, T, 2), F32)],
                          input_output_aliases={5: 0},
                          compiler_params=_cp("parallel", "parallel"))(z, z, z, f_col, f_row, ycat)


def fox_bwd_q(z, ycat, dycat, f_col, f_row, lse_col, *, name):
    T = z.shape[0]
    blk = _fox_block(T)
    nb = T // blk
    scale = FOX_HEAD_DIM ** -0.5

    def body(q_ref, k_ref, v_ref, o_ref, do_ref, fc_ref, fr_ref, lc_ref, dq_ref, dd_ref, df_ref):
        i = pl.program_id(1)
        row = lax.broadcasted_iota(jnp.int32, (blk, blk), 0)
        col = lax.broadcasted_iota(jnp.int32, (blk, blk), 1)
        for hh in range(2):
            ls = slice(hh * FOX_HEAD_DIM, (hh + 1) * FOX_HEAD_DIM)
            qh = (q_ref[:, ls] * scale).astype(MXU_DTYPE)
            doh = do_ref[:, ls]
            dd = jnp.sum(doh * o_ref[:, ls], axis=-1, keepdims=True)
            dd_ref[0, :, hh:hh + 1] = dd
            dob = doh.astype(MXU_DTYPE)
            fi = fc_ref[0, :, hh:hh + 1]
            lse = lc_ref[0, :, hh:hh + 1]

            def step(j, carry, masked):
                dq, df = carry
                r0 = pl.multiple_of(j * blk, blk)
                kj = k_ref[pl.ds(r0, blk), ls].astype(MXU_DTYPE)
                vj = v_ref[pl.ds(r0, blk), ls].astype(MXU_DTYPE)
                s = lax.dot_general(qh, kj, _NT, preferred_element_type=F32) + (fi - fr_ref[0, j, hh:hh + 1, :])
                p = jnp.exp(s - lse)
                if masked:
                    p = jnp.where(col <= row, p, 0.0)
                dp = lax.dot_general(dob, vj, _NT, preferred_element_type=F32)
                ds = p * (dp - dd)
                return (dq + jnp.dot(ds.astype(MXU_DTYPE), kj, preferred_element_type=F32),
                        df + jnp.sum(ds, axis=-1, keepdims=True))

            init = (jnp.zeros((blk, FOX_HEAD_DIM), F32), jnp.zeros((blk, 1), F32))
            carry = lax.fori_loop(0, i, lambda j, c: step(j, c, False), init)
            dq, df = step(i, carry, True)
            dq_ref[:, ls] = dq * scale
            df_ref[0, :, hh:hh + 1] = df

    qspec = pl.BlockSpec((blk, LANES), lambda h, i: (i, Q_TILE0 + h))
    kspec = pl.BlockSpec((T, LANES), lambda h, i: (0, K_TILE0 + h))
    vspec = pl.BlockSpec((T, LANES), lambda h, i: (0, V_TILE0 + h))
    ospec = pl.BlockSpec((blk, LANES), lambda h, i: (i, O_TILE0 + h))
    dqspec = pl.BlockSpec((blk, LANES), lambda h, i: (i, h))
    cspec = pl.BlockSpec((1, blk, 2), lambda h, i: (h, i, 0))
    rspec = pl.BlockSpec((1, nb, 2, blk), lambda h, i: (h, 0, 0, 0))
    stat = SDS((FOX_PAIRS, T, 2), F32)
    return pl.pallas_call(body, name=name, grid=(FOX_PAIRS, nb),
                          in_specs=[qspec, kspec, vspec, ospec, ospec, cspec, rspec, cspec],
                          out_specs=[dqspec, cspec, cspec], out_shape=[SDS((T, FOX_WIDTH), F32), stat, stat],
                          compiler_params=_cp("parallel", "parallel"))(z, z, z, ycat, dycat, f_col, f_row, lse_col)


def fox_bwd_kv(z, dycat, f_col, f_row, lse_row, dd_row, dfq_col, *, name):
    T = z.shape[0]
    blk = _fox_block(T)
    nb = T // blk
    scale = FOX_HEAD_DIM ** -0.5

    def body(q_ref, k_ref, v_ref, do_ref, fc_ref, fr_ref, lr_ref, dr_ref, dfq_ref, dk_ref, dv_ref, df_ref):
        j = pl.program_id(1)
        row = lax.broadcasted_iota(jnp.int32, (blk, blk), 0)
        col = lax.broadcasted_iota(jnp.int32, (blk, blk), 1)
        for hh in range(2):
            ls = slice(hh * FOX_HEAD_DIM, (hh + 1) * FOX_HEAD_DIM)
            kh = k_ref[:, ls].astype(MXU_DTYPE)
            vh = v_ref[:, ls].astype(MXU_DTYPE)
            fj = fc_ref[0, :, hh:hh + 1]

            def step(i, carry, masked):
                dk, dv, df = carry
                r0 = pl.multiple_of(i * blk, blk)
                qi = (q_ref[pl.ds(r0, blk), ls] * scale).astype(MXU_DTYPE)
                doi = do_ref[pl.ds(r0, blk), ls].astype(MXU_DTYPE)
                st = lax.dot_general(kh, qi, _NT, preferred_element_type=F32) + (fr_ref[0, i, hh:hh + 1, :] - fj)
                pt = jnp.exp(st - lr_ref[0, i, hh:hh + 1, :])
                if masked:
                    pt = jnp.where(col >= row, pt, 0.0)
                dv = dv + jnp.dot(pt.astype(MXU_DTYPE), doi, preferred_element_type=F32)
                dpt = lax.dot_general(vh, doi, _NT, preferred_element_type=F32)
                dst = pt * (dpt - dr_ref[0, i, hh:hh + 1, :])
                dk = dk + jnp.dot(dst.astype(MXU_DTYPE), qi, preferred_element_type=F32)
                df = df - jnp.sum(dst, axis=-1, keepdims=True)
                return dk, dv, df

            init = (jnp.zeros((blk, FOX_HEAD_DIM), F32), jnp.zeros((blk, FOX_HEAD_DIM), F32), dfq_ref[0, :, hh:hh + 1])
            carry = step(j, init, True)
            dk, dv, df = lax.fori_loop(j + 1, nb, lambda i, c: step(i, c, False), carry)
            dk_ref[:, ls] = dk
            dv_ref[:, ls] = dv
            df_ref[0, :, hh:hh + 1] = df

    bspec = pl.BlockSpec((blk, LANES), lambda h, j: (j, h))
    qspec = pl.BlockSpec((T, LANES), lambda h, j: (0, Q_TILE0 + h))
    kspec = pl.BlockSpec((blk, LANES), lambda h, j: (j, K_TILE0 + h))
    vspec = pl.BlockSpec((blk, LANES), lambda h, j: (j, V_TILE0 + h))
    dospec = pl.BlockSpec((T, LANES), lambda h, j: (0, O_TILE0 + h))
    cspec = pl.BlockSpec((1, blk, 2), lambda h, j: (h, j, 0))
    rspec = pl.BlockSpec((1, nb, 2, blk), lambda h, j: (h, 0, 0, 0))
    return pl.pallas_call(body, name=name, grid=(FOX_PAIRS, nb),
                          in_specs=[qspec, kspec, vspec, dospec, cspec, rspec, rspec, rspec, cspec],
                          out_specs=[bspec, bspec, cspec],
                          out_shape=[SDS((T, FOX_WIDTH), F32), SDS((T, FOX_WIDTH), F32), SDS((FOX_PAIRS, T, 2), F32)],
                          compiler_params=_cp("parallel", "parallel"))(z, z, z, dycat, f_col, f_row, lse_row, dd_row, dfq_col)


def _pairs_col(a, T):
    return jnp.transpose(a[:, :FOX_HEADS].reshape(T, FOX_PAIRS, 2), (1, 0, 2))


def _col_to_row(a, T):
    blk = _fox_block(T)
    return jnp.transpose(a.reshape(FOX_PAIRS, T // blk, blk, 2), (0, 1, 3, 2))


def _pairs_to_lanes(a, T):
    flat = jnp.transpose(a, (1, 0, 2)).reshape(T, FOX_HEADS)
    return jnp.pad(flat, ((0, 0), (0, LANES - FOX_HEADS)))


def _pool_counts(t0, n, w):
    t = (t0 + lax.broadcasted_iota(jnp.int32, (n, 1), 0)).astype(F32)
    return jnp.minimum(t + 1.0, float(w))


def pool_window(x, *, adjoint, name, in_col=0, into=None, out_col=0):
    T, C = x.shape[0], len(POOL_WINDOWS) * POOL_GROUP_DIM
    tr = _pick(T, 512)
    nt = T // tr
    hb = tr // POOL_HALO
    n = tr + POOL_HALO

    def body(x_ref, h_ref, *rest):
        o_ref = rest[-1]
        i = pl.program_id(0)
        cur = x_ref[...]
        if adjoint:
            halo = h_ref[...] * jnp.where(i < nt - 1, 1.0, 0.0)
            ext = jnp.concatenate([cur, halo], axis=0)
            t0 = i * tr
        else:
            halo = h_ref[...] * jnp.where(i > 0, 1.0, 0.0)
            ext = jnp.concatenate([halo, cur], axis=0)
            t0 = i * tr - POOL_HALO
        sums = {}
        for g, w in enumerate(POOL_WINDOWS):
            ls = slice(g * POOL_GROUP_DIM, (g + 1) * POOL_GROUP_DIM)
            s = ext[:, ls]
            if adjoint:
                s = s / _pool_counts(t0, n, w)
            d = 1
            while d < w:
                s = s + pltpu.roll(s, (n - d) if adjoint else d, 0)
                d *= 2
            if adjoint:
                o_ref[:, ls] = s[0:tr, :] - cur[:, ls]
            else:
                o_ref[:, ls] = s[POOL_HALO:n, :] / _pool_counts(i * tr, tr, w) - cur[:, ls]

    if adjoint:
        halo_spec = pl.BlockSpec((POOL_HALO, C), lambda i: (jnp.minimum((i + 1) * hb, T // POOL_HALO - 1), in_col))
    else:
        halo_spec = pl.BlockSpec((POOL_HALO, C), lambda i: (jnp.maximum(i * hb - 1, 0), in_col))
    x_spec = pl.BlockSpec((tr, C), lambda i: (i, in_col))
    if into is None:
        return pl.pallas_call(body, name=name, grid=(nt,), in_specs=[x_spec, halo_spec], out_specs=_row(tr, C),
                              out_shape=SDS((T, C), F32), compiler_params=_cp("parallel"))(x, x)
    return pl.pallas_call(body, name=name, grid=(nt,), in_specs=[x_spec, halo_spec, ANY],
                          out_specs=pl.BlockSpec((tr, C), lambda i: (i, out_col)), out_shape=SDS(into.shape, F32),
                          input_output_aliases={2: 0}, compiler_params=_cp("parallel"))(x, x, into)


def colscale_fwd(a, s, *, out_cols, name):
    T, C = a.shape
    tr = _pick(T, 512)

    def body(a_ref, s_ref, o_ref):
        o_ref[...] = a_ref[...] * s_ref[...]

    return pl.pallas_call(body, name=name, grid=(T // tr,), in_specs=[_row(tr, C), _full((1, C))], out_specs=_row(tr, C),
                          out_shape=SDS((T, out_cols), F32), compiler_params=_cp("parallel"))(a, s)


def colscale_bwd(a, s, dy, *, name):
    T, C = a.shape
    tr = _pick(T, 512)

    def body(a_ref, s_ref, dy_ref, da_ref, ds_ref):
        dyv = dy_ref[...]
        da_ref[...] = dyv * s_ref[...]

        @pl.when(pl.program_id(0) == 0)
        def _():
            ds_ref[...] = jnp.zeros_like(ds_ref)

        ds_ref[...] += jnp.sum(dyv * a_ref[...], axis=0, keepdims=True)

    return pl.pallas_call(body, name=name, grid=(T // tr,), in_specs=[_row(tr, C), _full((1, C)), _row(tr, C)],
                          out_specs=[_row(tr, C), _full((1, C))], out_shape=[SDS((T, C), F32), SDS((1, C), F32)],
                          compiler_params=_cp("arbitrary"))(a, s, dy)


SGU_ROWS = 512


def _sgu_norm(v, ln_g, ln_b):
    vg = jax.nn.gelu(v)
    xc = vg - jnp.mean(vg, axis=-1, keepdims=True)
    r = lax.rsqrt(jnp.mean(xc * xc, axis=-1, keepdims=True) + EPS)
    xh = xc * r
    return xh * ln_g + ln_b, xh, r


def _rowc(tr, c, cb):
    return pl.BlockSpec((tr, c), lambda i: (i, cb))


def sgu_fwd(z, ln_g, ln_b, ws, bst, ycat, *, name):
    T, C = z.shape[0], SGU_GROUPS * SGU_GROUP_DIM
    tr = _pick(T, SGU_ROWS)

    def body(u_ref, v_ref, g_ref, b_ref, ws_ref, bst_ref, prev_ref, o_ref):
        vn, _, _ = _sgu_norm(v_ref[...], g_ref[...], b_ref[...])
        vn = vn.astype(MXU_DTYPE)
        ug = jax.nn.gelu(u_ref[...])
        for g in range(SGU_GROUPS):
            w = ws_ref[g].astype(MXU_DTYPE)
            bias = bst_ref[:, g:g + 1]
            for c in range(tr // CHUNK):
                rs = slice(c * CHUNK, (c + 1) * CHUNK)
                ls = slice(g * SGU_GROUP_DIM, (g + 1) * SGU_GROUP_DIM)
                mixed = jnp.dot(w, vn[rs, ls], preferred_element_type=F32) + bias
                o_ref[rs, ls] = ug[rs, ls] * mixed

    return pl.pallas_call(body, name=name, grid=(T // tr,),
                          in_specs=[_rowc(tr, C, 0), _rowc(tr, C, 1), _full((1, C)), _full((1, C)),
                                    _full((SGU_GROUPS, CHUNK, CHUNK)), _full((CHUNK, SGU_GROUPS)), ANY],
                          out_specs=_rowc(tr, C, 1), out_shape=SDS(ycat.shape, F32), input_output_aliases={6: 0},
                          compiler_params=_cp("parallel"))(z, z, ln_g, ln_b, ws, bst, ycat)


def sgu_bwd(z, ln_g, ln_b, ws, wst, bst, dycat, *, out_cols, name):
    T, C = z.shape[0], SGU_GROUPS * SGU_GROUP_DIM
    tr = _pick(T, SGU_ROWS)

    def body(u_ref, v_ref, g_ref, b_ref, ws_ref, wst_ref, bst_ref, dy_ref,
             duv_ref, dg_ref, db_ref, dws_ref, dbst_ref, dvn_ref):
        du_ref = duv_ref.at[:, 0:C]
        dv_ref = duv_ref.at[:, C:2 * C]
        @pl.when(pl.program_id(0) == 0)
        def _():
            dg_ref[...] = jnp.zeros_like(dg_ref)
            db_ref[...] = jnp.zeros_like(db_ref)
            dws_ref[...] = jnp.zeros_like(dws_ref)
            dbst_ref[...] = jnp.zeros_like(dbst_ref)

        uv = u_ref[...]
        vv = v_ref[...]
        vn, xh, r = _sgu_norm(vv, g_ref[...], b_ref[...])
        vn = vn.astype(MXU_DTYPE)
        ug = jax.nn.gelu(uv)
        dyv = dy_ref[...]
        for g in range(SGU_GROUPS):
            w = ws_ref[g].astype(MXU_DTYPE)
            wt = wst_ref[g].astype(MXU_DTYPE)
            bias = bst_ref[:, g:g + 1]
            dw = jnp.zeros((CHUNK, CHUNK), F32)
            dbias = jnp.zeros((CHUNK, 1), F32)
            for c in range(tr // CHUNK):
                rs = slice(c * CHUNK, (c + 1) * CHUNK)
                ls = slice(g * SGU_GROUP_DIM, (g + 1) * SGU_GROUP_DIM)
                vblk = vn[rs, ls]
                mixed = jnp.dot(w, vblk, preferred_element_type=F32) + bias
                dyb = dyv[rs, ls]
                du_ref[rs, ls] = dyb * mixed * _gelu_grad(uv[rs, ls])
                dmixed = dyb * ug[rs, ls]
                dbias = dbias + jnp.sum(dmixed, axis=-1, keepdims=True)
                dmb = dmixed.astype(MXU_DTYPE)
                dw = dw + lax.dot_general(dmb, vblk, _NT, preferred_element_type=F32)
                dvn_ref[rs, ls] = jnp.dot(wt, dmb, preferred_element_type=F32)
            dws_ref[g] += dw
            dbst_ref[:, g:g + 1] += dbias
        dvn = dvn_ref[...]
        dg_ref[...] += jnp.sum(dvn * xh, axis=0, keepdims=True)
        db_ref[...] += jnp.sum(dvn, axis=0, keepdims=True)
        dxh = dvn * g_ref[...]
        dvg = r * (dxh - jnp.mean(dxh, axis=-1, keepdims=True) - xh * jnp.mean(dxh * xh, axis=-1, keepdims=True))
        dv_ref[...] = dvg * _gelu_grad(vv)

    wspec = _full((SGU_GROUPS, CHUNK, CHUNK))
    return pl.pallas_call(body, name=name, grid=(T // tr,),
                          in_specs=[_rowc(tr, C, 0), _rowc(tr, C, 1), _full((1, C)), _full((1, C)), wspec, wspec,
                                    _full((CHUNK, SGU_GROUPS)), _rowc(tr, C, 1)],
                          out_specs=[_rowc(tr, 2 * C, 0), _full((1, C)), _full((1, C)), wspec,
                                     _full((CHUNK, SGU_GROUPS))],
                          out_shape=[SDS((T, out_cols), F32), SDS((1, C), F32), SDS((1, C), F32),
                                     SDS((SGU_GROUPS, CHUNK, CHUNK), F32), SDS((CHUNK, SGU_GROUPS), F32)],
                          scratch_shapes=[pltpu.VMEM((tr, C), F32)],
                          compiler_params=_cp("arbitrary"))(z, z, ln_g, ln_b, ws, wst, bst, dycat)


def adamw(w, g, m, v, *, name):
    R, C = w.shape
    tr = _pick(R, 512)
    c1 = 1.0 - ADAM_B1 ** ADAM_STEP
    c2 = 1.0 - ADAM_B2 ** ADAM_STEP

    def body(w_ref, g_ref, m_ref, v_ref, d_ref, nm_ref, nv_ref):
        gv = g_ref[...]
        nm = ADAM_B1 * m_ref[...] + (1.0 - ADAM_B1) * gv
        nv = ADAM_B2 * v_ref[...] + (1.0 - ADAM_B2) * (gv * gv)
        nm_ref[...] = nm
        nv_ref[...] = nv
        d_ref[...] = -ADAM_LR * ((nm / c1) / (jnp.sqrt(nv / c2) + ADAM_EPS) + ADAM_WD * w_ref[...])

    spec = _row(tr, C)
    return pl.pallas_call(body, name=name, grid=(R // tr,), in_specs=[spec] * 4, out_specs=[spec] * 3,
                          out_shape=[SDS((R, C), F32)] * 3, compiler_params=_cp("parallel"))(w, g, m, v)


ANY = pl.BlockSpec(memory_space=pl.ANY)


def _coords():
    return lax.axis_index("x"), lax.axis_index("y"), lax.axis_index("c")


def _other_chips(x, y):
    return [(1 - x, y), (x, 1 - y), (1 - x, 1 - y)]


def _remote(src, dst, send_sems, recv_sems, k, dev):
    return pltpu.make_async_remote_copy(src_ref=src, dst_ref=dst, send_sem=send_sems.at[k], recv_sem=recv_sems.at[k],
                                        device_id=dev, device_id_type=MESH)


LOCAL_CHUNKS = 8


def allgather_chip_shards(shards, small, *, name):
    na = len(shards)

    def body(*refs):
        s_refs, sm_ref = refs[:na], refs[na]
        o_refs, smo_ref = refs[na + 1:2 * na + 1], refs[2 * na + 1]
        send_sems, recv_sems, local_sems = refs[2 * na + 2:]
        x, y, c = _coords()
        j = 2 * x + y
        sibling = (x, y, 1 - c)
        chips = _other_chips(x, y)
        for a in range(na):
            chunk = shards[a].shape[0] // LOCAL_CHUNKS
            for q in range(LOCAL_CHUNKS):
                rows = pl.ds(q * chunk, chunk)
                pltpu.make_async_copy(s_refs[a].at[rows], o_refs[a].at[j, rows], local_sems.at[a]).start()
        pltpu.make_async_copy(sm_ref, smo_ref.at[j], local_sems.at[na]).start()
        sends = []
        for a in range(na):
            half = shards[a].shape[0] // 2
            mine = pl.ds(c * half, half)
            for k, (px, py) in enumerate(chips):
                sends.append(_remote(s_refs[a].at[mine], o_refs[a].at[j, mine], send_sems, recv_sems, 6 * a + k, (px, py, c)))
        for k, (px, py) in enumerate(chips):
            sends.append(_remote(sm_ref, smo_ref.at[j], send_sems, recv_sems, 6 * na + k, (px, py, c)))
        for cp in sends:
            cp.start()
        for a in range(na):
            half = shards[a].shape[0] // 2
            mine = pl.ds(c * half, half)
            for k, (px, py) in enumerate(chips):
                rows = o_refs[a].at[2 * px + py, mine]
                _remote(rows, rows, send_sems, recv_sems, 6 * a + k, (px, py, c)).wait_recv()
                fw = _remote(rows, rows, send_sems, recv_sems, 6 * a + 3 + k, sibling)
                fw.start()
                sends.append(fw)
        for a in range(na):
            half = shards[a].shape[0] // 2
            theirs = pl.ds((1 - c) * half, half)
            for k, (px, py) in enumerate(chips):
                rows = o_refs[a].at[2 * px + py, theirs]
                _remote(rows, rows, send_sems, recv_sems, 6 * a + 3 + k, sibling).wait_recv()
        for k, (px, py) in enumerate(chips):
            slot = smo_ref.at[2 * px + py]
            _remote(slot, slot, send_sems, recv_sems, 6 * na + k, (px, py, c)).wait_recv()
        for cp in sends:
            cp.wait_send()
        for a in range(na):
            pltpu.make_async_copy(s_refs[a], o_refs[a].at[j], local_sems.at[a]).wait()
        pltpu.make_async_copy(sm_ref, smo_ref.at[j], local_sems.at[na]).wait()

    nsem = 6 * na + 3
    outs = pl.pallas_call(
        body, name=name, in_specs=[ANY] * (na + 1), out_specs=[ANY] * (na + 1),
        out_shape=[SDS((N_CHIPS,) + s.shape, s.dtype) for s in shards] + [SDS((N_CHIPS,) + small.shape, small.dtype)],
        scratch_shapes=[pltpu.SemaphoreType.DMA((nsem,)), pltpu.SemaphoreType.DMA((nsem,)),
                        pltpu.SemaphoreType.DMA((na + 1,))])(*shards, small)
    return outs[:na], outs[na]


def swap_sibling_halves(gs, *, name):
    na = len(gs)

    def body(*refs):
        g_refs, o_refs = refs[:na], refs[na:2 * na]
        send_sems, recv_sems = refs[2 * na:]
        x, y, c = _coords()
        cps = []
        for a in range(na):
            half = gs[a].shape[1] // 2
            cps.append(_remote(g_refs[a].at[:, pl.ds((1 - c) * half, half), :], o_refs[a], send_sems, recv_sems, a,
                               (x, y, 1 - c)))
        for cp in cps:
            cp.start()
        for cp in cps:
            cp.wait()

    return pl.pallas_call(body, name=name, in_specs=[ANY] * na, out_specs=[ANY] * na,
                          out_shape=[SDS((g.shape[0], g.shape[1] // 2, g.shape[2]), g.dtype) for g in gs],
                          scratch_shapes=[pltpu.SemaphoreType.DMA((na,)), pltpu.SemaphoreType.DMA((na,))])(*gs)


def add_sibling_half(g, land, c_idx, *, name):
    n, R, C = g.shape
    half = R // 2
    tr = _pick(half, 256)
    nt = half // tr

    def body(c_ref, g_ref, l_ref, of_ref, ob_ref):
        s = g_ref[...] + l_ref[...].astype(F32)
        of_ref[...] = s
        ob_ref[...] = s.astype(ob_ref.dtype)

    blk = pl.BlockSpec((1, tr, C), lambda s, i, c_ref: (s, i, 0))
    gblk = pl.BlockSpec((1, tr, C), lambda s, i, c_ref: (s, c_ref[0] * nt + i, 0))
    return pl.pallas_call(
        body, name=name,
        grid_spec=pltpu.PrefetchScalarGridSpec(num_scalar_prefetch=1, grid=(n, nt), in_specs=[gblk, blk],
                                               out_specs=[blk, blk]),
        out_shape=[SDS((n, half, C), F32), SDS((n, half, C), WIRE_DTYPE)],
        compiler_params=_cp("parallel", "parallel"))(c_idx, g, land)


def send_chip_partials(pbs, *, name):
    na = len(pbs)

    def body(*refs):
        p_refs, o_refs = refs[:na], refs[na:2 * na]
        send_sems, recv_sems = refs[2 * na:]
        x, y, c = _coords()
        cps = []
        for a in range(na):
            for k, (px, py) in enumerate(_other_chips(x, y)):
                cps.append(_remote(p_refs[a].at[2 * px + py], o_refs[a].at[k], send_sems, recv_sems, 3 * a + k, (px, py, c)))
        for cp in cps:
            cp.start()
        for cp in cps:
            cp.wait()

    return pl.pallas_call(body, name=name, in_specs=[ANY] * na, out_specs=[ANY] * na,
                          out_shape=[SDS((3,) + p.shape[1:], p.dtype) for p in pbs],
                          scratch_shapes=[pltpu.SemaphoreType.DMA((3 * na,)), pltpu.SemaphoreType.DMA((3 * na,))])(*pbs)


def add_chip_partials(pf, rb, jc_idx, *, name):
    n, H, C = pf.shape
    tr = _pick(H, 256)

    def body(jc_ref, p_ref, r_ref, o_ref):
        s = p_ref[0]
        for k in range(3):
            s = s + r_ref[k].astype(F32)
        o_ref[...] = s

    pblk = pl.BlockSpec((1, tr, C), lambda i, jc_ref: (jc_ref[0], i, 0))
    rblk = pl.BlockSpec((3, tr, C), lambda i, jc_ref: (0, i, 0))
    oblk = pl.BlockSpec((None, tr, C), lambda i, jc_ref: (jc_ref[1], i, 0))
    return pl.pallas_call(
        body, name=name,
        grid_spec=pltpu.PrefetchScalarGridSpec(num_scalar_prefetch=1, grid=(H // tr,), in_specs=[pblk, rblk],
                                               out_specs=oblk),
        out_shape=SDS((2, H, C), F32), compiler_params=_cp("parallel"))(jc_idx, pf, rb)


def join_sibling_halves(bufs, *, name):
    na = len(bufs)

    def body(*refs):
        o_refs = refs[na:2 * na]
        send_sems, recv_sems = refs[2 * na:]
        x, y, c = _coords()
        cps = [_remote(o_refs[a].at[c], o_refs[a].at[c], send_sems, recv_sems, a, (x, y, 1 - c)) for a in range(na)]
        for cp in cps:
            cp.start()
        for a in range(na):
            cps[a].wait_send()
            _remote(o_refs[a].at[1 - c], o_refs[a].at[1 - c], send_sems, recv_sems, a, (x, y, 1 - c)).wait_recv()

    return pl.pallas_call(body, name=name, in_specs=[ANY] * na, out_specs=[ANY] * na,
                          out_shape=[SDS(b.shape, b.dtype) for b in bufs],
                          input_output_aliases={a: a for a in range(na)},
                          scratch_shapes=[pltpu.SemaphoreType.DMA((na,)), pltpu.SemaphoreType.DMA((na,))])(*bufs)


def exchange_pieces(v, *, scatter, name):
    P, C = v.shape[-2:]

    def body(v_ref, o_ref, send_sems, recv_sems, local_sem):
        x, y, c = _coords()
        me = 4 * x + 2 * y + c
        local = pltpu.make_async_copy(v_ref.at[me] if scatter else v_ref, o_ref.at[me], local_sem)
        local.start()
        cps = []
        for m in range(1, N_DEV):
            px = (1 - x) if m & 4 else x
            py = (1 - y) if m & 2 else y
            pc = (1 - c) if m & 1 else c
            src = v_ref.at[4 * px + 2 * py + pc] if scatter else v_ref
            cps.append(_remote(src, o_ref.at[me], send_sems, recv_sems, m - 1, (px, py, pc)))
        for cp in cps:
            cp.start()
        for cp in cps:
            cp.wait_send()
        for m in range(1, N_DEV):
            px = (1 - x) if m & 4 else x
            py = (1 - y) if m & 2 else y
            pc = (1 - c) if m & 1 else c
            slot = o_ref.at[4 * px + 2 * py + pc]
            _remote(slot, slot, send_sems, recv_sems, m - 1, (px, py, pc)).wait_recv()
        local.wait()

    return pl.pallas_call(body, name=name, in_specs=[ANY], out_specs=ANY, out_shape=SDS((N_DEV, P, C), v.dtype),
                          scratch_shapes=[pltpu.SemaphoreType.DMA((N_DEV - 1,)), pltpu.SemaphoreType.DMA((N_DEV - 1,)),
                                          pltpu.SemaphoreType.DMA(())])(v)


def sum_pieces(land, *, name):
    n, P, C = land.shape

    def body(l_ref, o_ref):
        s = l_ref[0]
        for d in range(1, n):
            s = s + l_ref[d]
        o_ref[...] = s

    return pl.pallas_call(body, name=name, out_shape=SDS((P, C), F32))(land)


BIG_SEGS = (
    ("w_in_even", (1024, 514), 1),
    ("s5_w_glu", (128, 512), 0),
    ("w_out_even", (256, 1024), 0),
    ("w_in_odd", (1024, 384), 1),
    ("w_out_odd", (256, 1024), 0),
    ("mlp_w1", (2, 1024, 1024), 2),
    ("mlp_w2", (2, 1024, 1024), 1),
)
BIG_NAMES = tuple(n for n, _, _ in BIG_SEGS)
SHARDED_SMALL = ("pool_scale", "sgu_ln_g", "sgu_ln_b")
SMALL_SEGS = (
    ("mix_pre_g", (2, 1024)), ("mix_post_g", (2, 1024)), ("mlp_pre_g", (2, 1024)), ("mlp_post_g", (2, 1024)),
    ("s5_lam_re", (1, 32, 64)), ("s5_lam_im", (1, 32, 64)), ("s5_log_dt", (1, 32)),
    ("s5_b_re", (1, 32, 64, 16)), ("s5_b_im", (1, 32, 64, 16)), ("s5_c_re", (1, 32, 16, 64)), ("s5_c_im", (1, 32, 16, 64)),
    ("s5_d", (1, 512)), ("fox_b_f", (1, 8)), ("pool_w", (1, 4, 128, 128)), ("sgu_w_s", (1, 4, 128, 128)),
    ("sgu_b_s", (1, 4, 128)),
)
REDUCED_SEGS = SMALL_SEGS + tuple((n, (1, 512)) for n in SHARDED_SMALL)


def _cols_from_chips(g):
    n, R, C = g.shape
    return jnp.transpose(g, (1, 0, 2)).reshape(R, n * C)


def _chips_from_cols(m):
    R, C4 = m.shape
    return jnp.transpose(m.reshape(R, N_CHIPS, C4 // N_CHIPS), (1, 0, 2))


MLP_SHARD = 1024


def _w1_cols(l):
    def spec(tm, tn, tk):
        per = MLP_SHARD // tn
        return pl.BlockSpec((None, tk, tn), lambda i, j, k: (j // per, l * (MLP_SHARD // tk) + k, j % per))
    return spec


def _w1_rows_t(l):
    def spec(tm, tn, tk):
        per = MLP_SHARD // tk
        return pl.BlockSpec((None, tn, tk), lambda i, j, k: (k // per, l * (MLP_SHARD // tn) + j, k % per))
    return spec


def _w2_rows(l):
    def spec(tm, tn, tk):
        per = MLP_SHARD // tk
        return pl.BlockSpec((None, tk, tn), lambda i, j, k: (k // per, l * per + k % per, j))
    return spec


def _w2_rows_t(l):
    def spec(tm, tn, tk):
        per = MLP_SHARD // tn
        return pl.BlockSpec((None, tn, tk), lambda i, j, k: (j // per, l * per + j % per, k))
    return spec


def _dw1_out(l):
    def spec(tm, tn, tk):
        per = MLP_SHARD // tn
        return pl.BlockSpec((None, tm, tn), lambda i, j, k: (j // per, l * (MLP_SHARD // tm) + i, j % per))
    return spec


def _dw2_out(l):
    def spec(tm, tn, tk):
        per = MLP_SHARD // tm
        return pl.BlockSpec((None, tm, tn), lambda i, j, k: (i // per, l * per + i % per, j))
    return spec


def _pack_vec(d, segs, rows_multiple):
    flat = jnp.concatenate([d[n].reshape(-1) for n, _ in segs])
    rows = -(-flat.shape[0] // LANES)
    rows = -(-rows // rows_multiple) * rows_multiple
    return jnp.pad(flat, (0, rows * LANES - flat.shape[0])).reshape(rows, LANES)


def _unpack_vec(v, segs):
    flat, out, r = v.reshape(-1), {}, 0
    for n, shape in segs:
        k = math.prod(shape)
        out[n] = flat[r:r + k].reshape(shape)
        r += k
    return out


def _block_diag(blocks):
    G, a, b = blocks.shape
    eye = jnp.eye(G, dtype=blocks.dtype)
    return (eye[:, None, :, None] * blocks[:, :, None, :]).reshape(G * a, G * b)


def _diag_blocks(m, G):
    a, b = m.shape[0] // G, m.shape[1] // G
    return jnp.stack([m[g * a:(g + 1) * a, g * b:(g + 1) * b] for g in range(G)])


def _sqrelu_epi(acc):
    r = jnp.maximum(acc, 0.0)
    return acc, r * r


def _sqrelu_bwd_epi(acc, a):
    return (acc * (2.0 * jnp.maximum(a.astype(F32), 0.0)),)


def _mlp_fwd(x, g1, g2, l, g_pre, g_post, tag):
    T, D = x.shape
    h = rms_fwd(x, g_pre, name=f"{tag}_pre_norm")
    a, s = matmul(h, g1, name=f"{tag}_up", mnk=(T, D_FF, D), b_spec=_w1_cols(l), epi=_sqrelu_epi,
                  out_dtypes=(MXU_DTYPE, MXU_DTYPE))
    m = matmul(s, g2, name=f"{tag}_down", mnk=(T, D, D_FF), b_spec=_w2_rows(l))
    return rms_res_fwd(x, m, g_post, name=f"{tag}_post_norm"), (x, h, a, s, m)


def _mlp_bwd(saved, g1, g2, l, g_pre, g_post, dxo, dg1, dg2, tag):
    x, h, a, s, m = saved
    T, D = x.shape
    gshape = (N_CHIPS, 2 * MLP_SHARD, MLP_SHARD)
    dm, dg_post = rms_bwd(m, g_post, dxo, None, name=f"{tag}_post_norm_bwd")
    da = matmul(dm, g2, tb=True, name=f"{tag}_down_dx", mnk=(T, D_FF, D), b_spec=_w2_rows_t(l),
                epi=_sqrelu_bwd_epi, epi_in=(a,), out_dtype=MXU_DTYPE)
    dg2 = matmul(s, dm, ta=True, name=f"{tag}_down_dw", o_spec=_dw2_out(l), o_shape=gshape, prev=dg2)
    dh = matmul(da, g1, tb=True, name=f"{tag}_up_dx", mnk=(T, D, D_FF), b_spec=_w1_rows_t(l))
    dg1 = matmul(h, da, ta=True, name=f"{tag}_up_dw", o_spec=_dw1_out(l), o_shape=gshape, prev=dg1)
    dx, dg_pre = rms_bwd(x, g_pre, dh, dxo, name=f"{tag}_pre_norm_bwd")
    return dx, dg1, dg2, dg_pre, dg_post


def kernel(x, mix_pre_g, mix_post_g, mlp_pre_g, mlp_post_g, w_in_even, s5_lam_re, s5_lam_im, s5_log_dt, s5_b_re, s5_b_im, s5_c_re, s5_c_im, s5_d, s5_w_glu, fox_b_f, w_out_even, w_in_odd, pool_w, pool_scale, sgu_ln_g, sgu_ln_b, sgu_w_s, sgu_b_s, w_out_odd, mlp_w1, mlp_w2, loss_target, m_mix_pre_g, m_mix_post_g, m_mlp_pre_g, m_mlp_post_g, m_w_in_even, m_s5_lam_re, m_s5_lam_im, m_s5_log_dt, m_s5_b_re, m_s5_b_im, m_s5_c_re, m_s5_c_im, m_s5_d, m_s5_w_glu, m_fox_b_f, m_w_out_even, m_w_in_odd, m_pool_w, m_pool_scale, m_sgu_ln_g, m_sgu_ln_b, m_sgu_w_s, m_sgu_b_s, m_w_out_odd, m_mlp_w1, m_mlp_w2, v_mix_pre_g, v_mix_post_g, v_mlp_pre_g, v_mlp_post_g, v_w_in_even, v_s5_lam_re, v_s5_lam_im, v_s5_log_dt, v_s5_b_re, v_s5_b_im, v_s5_c_re, v_s5_c_im, v_s5_d, v_s5_w_glu, v_fox_b_f, v_w_out_even, v_w_in_odd, v_pool_w, v_pool_scale, v_sgu_ln_g, v_sgu_ln_b, v_sgu_w_s, v_sgu_b_s, v_w_out_odd, v_mlp_w1, v_mlp_w2):
    names = [n for n, _ in SMALL_SEGS] + [n for n, _, _ in BIG_SEGS] + list(SHARDED_SMALL)
    env = dict(locals())
    W = {n: env[n] for n in names}
    M = {n: env["m_" + n] for n in names}
    V = {n: env["v_" + n] for n in names}

    shards = [W[n].reshape(-1, W[n].shape[-1]).astype(WIRE_DTYPE) for n in BIG_NAMES]
    small = jnp.pad(jnp.concatenate([W[n] for n in SHARDED_SMALL]), ((0, SUBLANES - len(SHARDED_SMALL)), (0, 0)))
    gathered, small_all = allgather_chip_shards(shards, small, name="allgather_weights")
    Wf = dict(zip(BIG_NAMES, gathered))
    for i, n in enumerate(SHARDED_SMALL):
        Wf[n] = small_all[:, i, :].reshape(1, N_CHIPS * LANES)
    for n, _ in SMALL_SEGS:
        Wf[n] = W[n]

    loss8, dx0, full_grads, local_small = _local_step(x[0], loss_target[0], Wf)
    loss = lax.psum(loss8[0, 0], MESH_AXES)
    return _reduce_and_update(W, M, V, loss, dx0, full_grads, local_small)


def _local_step(x0, target, P):
    T = x0.shape[0]
    mix_pre_g, mix_post_g, mlp_pre_g, mlp_post_g = P["mix_pre_g"], P["mix_post_g"], P["mlp_pre_g"], P["mlp_post_g"]
    s5_lam_re, s5_lam_im, s5_log_dt = P["s5_lam_re"], P["s5_lam_im"], P["s5_log_dt"]
    s5_b_re, s5_b_im, s5_c_re, s5_c_im, s5_d = P["s5_b_re"], P["s5_b_im"], P["s5_c_re"], P["s5_c_im"], P["s5_d"]
    fox_b_f, pool_w, sgu_w_s, sgu_b_s = P["fox_b_f"], P["pool_w"], P["sgu_w_s"], P["sgu_b_s"]
    pool_scale_f, ln_g_f, ln_b_f = P["pool_scale"], P["sgu_ln_g"], P["sgu_ln_b"]
    w_in_e = jnp.pad(_cols_from_chips(P["w_in_even"]), ((0, 0), (0, EVEN_IN_PAD - EVEN_IN)))
    w_in_o = _cols_from_chips(P["w_in_odd"])
    w_in_o = jnp.concatenate([w_in_o[:, S5_WIDTH:], w_in_o[:, :S5_WIDTH]], axis=1)
    w_glu = P["s5_w_glu"].reshape(S5_WIDTH, S5_WIDTH)
    w_out_e = P["w_out_even"].reshape(D_MODEL, D_MODEL)
    w_out_o = P["w_out_odd"].reshape(D_MODEL, D_MODEL)
    g1, g2 = P["mlp_w1"], P["mlp_w2"]

    def gain(a, l):
        return a[l][None, :]

    lr = s5_lam_re[0].reshape(1, S5_LANES)
    li = s5_lam_im[0].reshape(1, S5_LANES)
    ldt = jnp.repeat(s5_log_dt[0], S5_STATE).reshape(1, S5_LANES)
    btr = s5_b_re[0].reshape(S5_LANES, S5_GROUP).T
    bti = s5_b_im[0].reshape(S5_LANES, S5_GROUP).T
    tf_re, tf_im, tb_re, tb_im, bbt_re, bbt_im = s5_disc_fwd(lr, li, ldt, btr, bti, name="s5_disc")
    same_group = (jnp.arange(S5_WIDTH)[:, None] // S5_GROUP) == (jnp.arange(S5_LANES)[None, :] // S5_STATE)
    b_bd = s5_interleave(jnp.where(same_group, jnp.tile(bbt_re, (S5_GROUPS, 1)), 0.0),
                         jnp.where(same_group, jnp.tile(bbt_im, (S5_GROUPS, 1)), 0.0), axis=1)
    cr2 = jnp.transpose(s5_c_re[0], (0, 2, 1)).reshape(S5_LANES, S5_GROUP)
    ci2 = jnp.transpose(s5_c_im[0], (0, 2, 1)).reshape(S5_LANES, S5_GROUP)
    c_bd = s5_interleave(jnp.where(same_group.T, jnp.tile(cr2, (1, S5_GROUPS)), 0.0),
                         -jnp.where(same_group.T, jnp.tile(ci2, (1, S5_GROUPS)), 0.0), axis=0)
    bf_pad = jnp.pad(fox_b_f, ((0, 0), (0, LANES - FOX_HEADS)))

    h1 = rms_fwd(x0, gain(mix_pre_g, 0), name="l0_pre_norm")
    z = matmul(h1, w_in_e, name="l0_in_proj")
    bu = matmul(z, b_bd, mnk=(T, 2 * S5_LANES, S5_WIDTH), name="s5_bu")
    xs = s5_scan(bu, tf_re, tf_im, reverse=False, name="s5_scan_fwd")
    yc = matmul(xs, c_bd, name="s5_cx")
    yl, yg = s5_out_fwd(yc, z, s5_d, name="s5_out")
    gl = matmul(yg, w_glu, name="s5_glu_proj")
    ycat = glu_fwd(yg, gl, out_cols=D_MODEL, name="s5_glu")
    fgate = fox_gate_fwd(z, bf_pad, fl_col=FL_TILE, name="fox_gate")
    f_col = _pairs_col(fgate, T)
    f_row = _col_to_row(f_col, T)
    ycat, lse_col = fox_fwd(z, f_col, f_row, ycat, name="fox_fwd")
    mo = matmul(ycat, w_out_e, name="l0_out_proj")
    x1 = rms_res_fwd(x0, mo, gain(mix_post_g, 0), name="l0_post_norm")
    x2, mlp0 = _mlp_fwd(x1, g1, g2, 0, gain(mlp_pre_g, 0), gain(mlp_post_g, 0), "mlp0")

    h3 = rms_fwd(x2, gain(mix_pre_g, 1), name="l1_pre_norm")
    z2 = matmul(h3, w_in_o, name="l1_in_proj")
    pooled = pool_window(z2, adjoint=False, in_col=POOL_COL, name="pool_fwd")
    pw_bd = _block_diag(pool_w[0])
    pw = matmul(pooled, pw_bd, name="pool_proj")
    ycat2 = colscale_fwd(pw, pool_scale_f, out_cols=D_MODEL, name="pool_scale")
    causal = jnp.tril(jnp.ones((CHUNK, CHUNK), dtype=bool))
    wsm = jnp.where(causal[None], sgu_w_s[0], 0.0)
    wsmt = jnp.transpose(wsm, (0, 2, 1))
    bst = sgu_b_s[0].T
    ycat2 = sgu_fwd(z2, ln_g_f, ln_b_f, wsm, bst, ycat2, name="sgu_fwd")
    mo2 = matmul(ycat2, w_out_o, name="l1_out_proj")
    x3 = rms_res_fwd(x2, mo2, gain(mix_post_g, 1), name="l1_post_norm")
    x4, mlp1 = _mlp_fwd(x3, g1, g2, 1, gain(mlp_pre_g, 1), gain(mlp_post_g, 1), "mlp1")

    loss8, dx4 = loss_fwd_bwd(x4, target, name="loss")

    dx3, dg1, dg2, dg_mlp_pre1, dg_mlp_post1 = _mlp_bwd(mlp1, g1, g2, 1, gain(mlp_pre_g, 1), gain(mlp_post_g, 1), dx4,
                                                        None, None, "mlp1")
    dmo2, dg_mix_post1 = rms_bwd(mo2, gain(mix_post_g, 1), dx3, None, name="l1_post_norm_bwd")
    dycat2 = matmul(dmo2, w_out_o, tb=True, name="l1_out_proj_dx")
    dw_out_o = matmul(ycat2, dmo2, ta=True, name="l1_out_proj_dw")
    dpw, dpool_scale = colscale_bwd(pw, pool_scale_f, dycat2, name="pool_scale_bwd")
    dpooled = matmul(dpw, pw_bd, tb=True, name="pool_proj_dx")
    dpw_bd = matmul(pooled, dpw, ta=True, name="pool_proj_dw")
    dz2, dln_g, dln_b, dws, dbst = sgu_bwd(z2, ln_g_f, ln_b_f, wsm, wsmt, bst, dycat2, out_cols=3 * S5_WIDTH,
                                           name="sgu_bwd")
    dz2 = pool_window(dpooled, adjoint=True, into=dz2, out_col=POOL_COL, name="pool_bwd")
    dh3 = matmul(dz2, w_in_o, tb=True, name="l1_in_proj_dx")
    dw_in_o = matmul(h3, dz2, ta=True, name="l1_in_proj_dw")
    dw_in_o = jnp.concatenate([dw_in_o[:, 2 * S5_WIDTH:], dw_in_o[:, :2 * S5_WIDTH]], axis=1)
    dx2, dg_mix_pre1 = rms_bwd(x2, gain(mix_pre_g, 1), dh3, dx3, name="l1_pre_norm_bwd")

    dx1, dg1, dg2, dg_mlp_pre0, dg_mlp_post0 = _mlp_bwd(mlp0, g1, g2, 0, gain(mlp_pre_g, 0), gain(mlp_post_g, 0), dx2,
                                                        dg1, dg2, "mlp0")
    dmo, dg_mix_post0 = rms_bwd(mo, gain(mix_post_g, 0), dx1, None, name="l0_post_norm_bwd")
    dycat = matmul(dmo, w_out_e, tb=True, name="l0_out_proj_dx")
    dw_out_e = matmul(ycat, dmo, ta=True, name="l0_out_proj_dw")
    dyg_a, dgl = glu_bwd(yg, gl, dycat, name="s5_glu_bwd")
    dyg_b = matmul(dgl, w_glu, tb=True, name="s5_glu_proj_dx")
    dw_glu = matmul(yg, dgl, ta=True, name="s5_glu_proj_dw")
    dyl, du_skip, dd = s5_out_bwd(yl, z, s5_d, dyg_a, dyg_b, name="s5_out_bwd")
    dxs = matmul(dyl, c_bd, tb=True, name="s5_cx_dx")
    dc_bd = matmul(xs, dyl, ta=True, name="s5_cx_dw")
    lam = s5_scan(dxs, tb_re, tb_im, reverse=True, name="s5_scan_bwd")
    dab_re, dab_im = s5_da(lam, xs, name="s5_da")
    db_bd = matmul(z, lam, ta=True, mnk=(S5_WIDTH, 2 * S5_LANES, T), name="s5_bu_dw")
    du_b = matmul(lam, b_bd, tb=True, name="s5_bu_dx")
    du = add2(du_skip, du_b, name="s5_du")
    dq, dd_col, dfq_col = fox_bwd_q(z, ycat, dycat, f_col, f_row, lse_col, name="fox_bwd_q")
    dk, dv, df_col = fox_bwd_kv(z, dycat, f_col, f_row, _col_to_row(lse_col, T), _col_to_row(dd_col, T), dfq_col,
                                name="fox_bwd_kv")
    dfl, dbf = fox_gate_bwd(z, bf_pad, _pairs_to_lanes(df_col, T), fl_col=FL_TILE, name="fox_gate_bwd")
    dz = jnp.concatenate([du, dq, dk, dv, dfl], axis=1)
    dh1 = matmul(dz, w_in_e, tb=True, name="l0_in_proj_dx")
    dw_in_e = matmul(h1, dz, ta=True, name="l0_in_proj_dw")[:, :EVEN_IN]
    dx0, dg_mix_pre0 = rms_bwd(x0, gain(mix_pre_g, 0), dh1, dx1, name="l0_pre_norm_bwd")

    db_re_bd, db_im_bd = s5_deinterleave(db_bd, axis=1)
    dbbt_re = jnp.where(same_group, db_re_bd, 0.0).reshape(S5_GROUPS, S5_GROUP, S5_LANES).sum(0)
    dbbt_im = jnp.where(same_group, db_im_bd, 0.0).reshape(S5_GROUPS, S5_GROUP, S5_LANES).sum(0)
    dlr, dli, dldt8, dbtr, dbti = s5_disc_bwd(lr, li, ldt, btr, bti, dab_re, dab_im, dbbt_re, dbbt_im, name="s5_disc_bwd")
    dc_re_bd, dc_im_bd = s5_deinterleave(dc_bd, axis=0)
    dcr2 = jnp.where(same_group.T, dc_re_bd, 0.0).reshape(S5_LANES, S5_GROUPS, S5_GROUP).sum(1)
    dci2 = -jnp.where(same_group.T, dc_im_bd, 0.0).reshape(S5_LANES, S5_GROUPS, S5_GROUP).sum(1)

    def c_layout(a):
        return jnp.transpose(a.reshape(S5_GROUPS, S5_STATE, S5_GROUP), (0, 2, 1))[None]

    def b_layout(a):
        return a.T.reshape(1, S5_GROUPS, S5_STATE, S5_GROUP)

    local_small = {
        "mix_pre_g": jnp.concatenate([dg_mix_pre0, dg_mix_pre1]), "mix_post_g": jnp.concatenate([dg_mix_post0, dg_mix_post1]),
        "mlp_pre_g": jnp.concatenate([dg_mlp_pre0, dg_mlp_pre1]), "mlp_post_g": jnp.concatenate([dg_mlp_post0, dg_mlp_post1]),
        "s5_lam_re": dlr.reshape(1, S5_GROUPS, S5_STATE), "s5_lam_im": dli.reshape(1, S5_GROUPS, S5_STATE),
        "s5_log_dt": dldt8[0:1, 0:S5_GROUPS],
        "s5_b_re": b_layout(dbtr), "s5_b_im": b_layout(dbti), "s5_c_re": c_layout(dcr2), "s5_c_im": c_layout(dci2),
        "s5_d": dd, "fox_b_f": dbf[:, 0:FOX_HEADS],
        "pool_w": _diag_blocks(dpw_bd, len(POOL_WINDOWS))[None],
        "sgu_w_s": jnp.where(causal[None], dws, 0.0)[None], "sgu_b_s": dbst.T[None],
        "pool_scale": dpool_scale, "sgu_ln_g": dln_g, "sgu_ln_b": dln_b,
    }
    full_grads = {"w_in_even": _chips_from_cols(dw_in_e), "s5_w_glu": dw_glu.reshape(N_CHIPS, -1, S5_WIDTH),
                  "w_out_even": dw_out_e.reshape(N_CHIPS, -1, D_MODEL), "w_in_odd": _chips_from_cols(dw_in_o),
                  "w_out_odd": dw_out_o.reshape(N_CHIPS, -1, D_MODEL), "mlp_w1": dg1, "mlp_w2": dg2}
    return loss8, dx0, full_grads, local_small


def _reduce_and_update(W, M, V, loss, dx0, full_grads, local_small):
    cx, cy, cc = _coords()
    chip = 2 * cx + cy

    vec = _pack_vec(local_small, REDUCED_SEGS, N_DEV * SUBLANES)
    piece = vec.shape[0] // N_DEV
    landed = exchange_pieces(vec.reshape(N_DEV, piece, LANES), scatter=True, name="small_grads_scatter")
    mine = sum_pieces(landed, name="small_grads_sum")
    everyone = exchange_pieces(mine, scatter=False, name="small_grads_gather")
    G = _unpack_vec(everyone, REDUCED_SEGS)
    for n in SHARDED_SMALL:
        G[n] = lax.dynamic_slice_in_dim(G[n], chip * LANES, LANES, axis=1)

    gs = [full_grads[n] for n in BIG_NAMES]
    c_idx = cc.reshape(1).astype(jnp.int32)
    jc_idx = jnp.stack([chip, cc]).astype(jnp.int32)
    from_sibling = swap_sibling_halves(gs, name="big_grads_to_sibling")
    sums = [add_sibling_half(g, l, c_idx, name=f"big_grads_chip_sum_{n}") for n, g, l in zip(BIG_NAMES, gs, from_sibling)]
    from_chips = send_chip_partials([pb for _, pb in sums], name="big_grads_to_chips")
    halves = [add_chip_partials(pf, r, jc_idx, name=f"big_grads_sum_{n}") for n, (pf, _), r in zip(BIG_NAMES, sums, from_chips)]
    reduced = join_sibling_halves(halves, name="big_grads_join")
    for n, r in zip(BIG_NAMES, reduced):
        G[n] = r.reshape(W[n].shape)

    def two_d(a):
        return a.reshape(-1, a.shape[-1])

    delta, new_m, new_v = {}, {}, {}
    for n in BIG_NAMES:
        d_, m_, v_ = adamw(two_d(W[n]), two_d(G[n]), two_d(M[n]), two_d(V[n]), name=f"adamw_{n}")
        delta[n], new_m[n], new_v[n] = (t.reshape(W[n].shape) for t in (d_, m_, v_))
    packed = [_pack_vec(src, SMALL_SEGS, SUBLANES) for src in (W, G, M, V)]
    outs = adamw(*packed, name="adamw_replicated")
    for dst, t in zip((delta, new_m, new_v), outs):
        dst.update(_unpack_vec(t, SMALL_SEGS))
    sharded_segs = tuple((n, (1, LANES)) for n in SHARDED_SMALL)
    packed = [_pack_vec(src, sharded_segs, 1) for src in (W, G, M, V)]
    outs = adamw(*packed, name="adamw_sharded_vectors")
    for dst, t in zip((delta, new_m, new_v), outs):
        dst.update(_unpack_vec(t, sharded_segs))

    order = ["mix_pre_g", "mix_post_g", "mlp_pre_g", "mlp_post_g", "w_in_even", "s5_lam_re", "s5_lam_im", "s5_log_dt",
             "s5_b_re", "s5_b_im", "s5_c_re", "s5_c_im", "s5_d", "s5_w_glu", "fox_b_f", "w_out_even", "w_in_odd",
             "pool_w", "pool_scale", "sgu_ln_g", "sgu_ln_b", "sgu_w_s", "sgu_b_s", "w_out_odd", "mlp_w1", "mlp_w2"]
    return (loss, dx0[None], *[G[n] for n in order], *[delta[n] for n in order],
            *[new_m[n] for n in order], *[new_v[n] for n in order])
```

```python
import functools
import math

import jax
import jax.numpy as jnp
from jax import lax
from jax.experimental import pallas as pl
from jax.experimental.pallas import tpu as pltpu

F32 = jnp.float32
MXU_DTYPE = jnp.bfloat16
WIRE_DTYPE = jnp.bfloat16
EPS = 1e-6
VMEM_LIMIT_BYTES = 48 * 1024 * 1024
LANES = 128
SUBLANES = 8

D_MODEL = 1024
S5_WIDTH = 512
S5_GROUP = 16
S5_GROUPS = 32
S5_STATE = 64
S5_LANES = S5_GROUPS * S5_STATE
FOX_HEADS = 8
FOX_HEAD_DIM = 64
FOX_WIDTH = 512
EVEN_IN = S5_WIDTH + 3 * FOX_WIDTH + FOX_HEADS
EVEN_IN_PAD = 2176
POOL_WINDOWS = (2, 4, 8, 16)
POOL_HALO = 16
POOL_GROUP_DIM = 128
SGU_GROUPS = 4
SGU_GROUP_DIM = 128
CHUNK = 128
D_FF = 4096

ADAM_LR = 0.001
ADAM_B1 = 0.9
ADAM_B2 = 0.999
ADAM_EPS = 1e-08
ADAM_WD = 0.01
ADAM_STEP = 10

MESH_AXES = ("x", "y", "c")
MESH = pl.DeviceIdType.MESH
N_CHIPS = 4
N_DEV = 8

SDS = jax.ShapeDtypeStruct


def _cp(*sem):
    return pltpu.CompilerParams(dimension_semantics=sem, vmem_limit_bytes=VMEM_LIMIT_BYTES)


def _pick(dim, pref):
    if dim <= pref:
        return dim
    t = pref
    while t >= 256:
        if dim % t == 0:
            return t
        t //= 2
    return dim


def _row(tr, c):
    return pl.BlockSpec((tr, c), lambda i: (i, 0))


def _full(shape):
    nd = len(shape)
    return pl.BlockSpec(shape, lambda *_: (0,) * nd)


def _gelu_grad(x):
    c = math.sqrt(2.0 / math.pi)
    t = jnp.tanh(c * (x + 0.044715 * x * x * x))
    return 0.5 * (1.0 + t) + 0.5 * x * (1.0 - t * t) * c * (1.0 + 3.0 * 0.044715 * x * x)


def matmul(a, b, *, name, ta=False, tb=False, out_dtype=F32, tm=1024, tn=512, tk=1024, mnk=None, a_koff=0,
           b_spec=None, o_spec=None, o_shape=None, prev=None, epi=None, epi_in=(), out_dtypes=None):
    if mnk is None:
        M, K = (a.shape[1], a.shape[0]) if ta else a.shape
        K2, N = (b.shape[1], b.shape[0]) if tb else b.shape
        assert K == K2, (a.shape, b.shape, ta, tb)
    else:
        M, N, K = mnk
    tm, tn, tk = _pick(M, tm), _pick(N, tn), _pick(K, tk)
    nk = K // tk
    assert a_koff % tk == 0 and not (ta and a_koff)
    ko = a_koff // tk
    dn = (((0 if ta else 1,), (1 if tb else 0,)), ((), ()))
    out_dtypes = tuple(out_dtypes) if out_dtypes is not None else (out_dtype,)
    n_out, n_epi = len(out_dtypes), len(epi_in)

    def body(*refs):
        a_ref, b_ref = refs[0], refs[1]
        epi_refs = refs[2:2 + n_epi]
        o_refs = refs[len(refs) - 1 - n_out:len(refs) - 1]
        acc_ref = refs[-1]
        k = pl.program_id(2)

        @pl.when(k == 0)
        def _():
            acc_ref[...] = jnp.zeros_like(acc_ref)

        acc_ref[...] += lax.dot_general(a_ref[...].astype(MXU_DTYPE), b_ref[...].astype(MXU_DTYPE), dn,
                                        preferred_element_type=F32)

        @pl.when(k == nk - 1)
        def _():
            res = (acc_ref[...],) if epi is None else epi(acc_ref[...], *[r[...] for r in epi_refs])
            for o_ref, r in zip(o_refs, res):
                o_ref[...] = r.astype(o_ref.dtype)

    a_spec = pl.BlockSpec((tk, tm), lambda i, j, k: (k, i)) if ta else pl.BlockSpec((tm, tk), lambda i, j, k: (i, k + ko))
    if b_spec is None:
        bs = pl.BlockSpec((tn, tk), lambda i, j, k: (j, k)) if tb else pl.BlockSpec((tk, tn), lambda i, j, k: (k, j))
    else:
        bs = b_spec(tm, tn, tk)
    tile = pl.BlockSpec((tm, tn), lambda i, j, k: (i, j))
    os_ = tile if o_spec is None else o_spec(tm, tn, tk)
    ins, in_specs, aliases = [a, b, *epi_in], [a_spec, bs] + [tile] * n_epi, {}
    if prev is not None:
        aliases = {len(ins): 0}
        ins.append(prev)
        in_specs.append(pl.BlockSpec(memory_space=pl.ANY))
    shapes = [SDS((M, N) if o_shape is None else o_shape, dt) for dt in out_dtypes]
    outs = pl.pallas_call(
        body, name=name, grid=(M // tm, N // tn, nk),
        in_specs=in_specs, out_specs=[os_] * n_out, out_shape=shapes, input_output_aliases=aliases,
        scratch_shapes=[pltpu.VMEM((tm, tn), F32)],
        compiler_params=_cp("parallel", "parallel", "arbitrary"),
    )(*ins)
    return outs[0] if n_out == 1 else outs


def _rms_hat(x):
    return x * lax.rsqrt(jnp.mean(x * x, axis=-1, keepdims=True) + EPS)


def rms_fwd(x, g, *, name):
    T, D = x.shape
    tr = _pick(T, 512)

    def body(x_ref, g_ref, o_ref):
        o_ref[...] = (_rms_hat(x_ref[...]) * g_ref[...]).astype(o_ref.dtype)

    return pl.pallas_call(body, name=name, grid=(T // tr,), in_specs=[_row(tr, D), _full((1, D))],
                          out_specs=_row(tr, D), out_shape=SDS((T, D), MXU_DTYPE), compiler_params=_cp("parallel"))(x, g)


def rms_res_fwd(x, y, g, *, name):
    T, D = x.shape
    tr = _pick(T, 512)

    def body(x_ref, y_ref, g_ref, o_ref):
        o_ref[...] = x_ref[...] + _rms_hat(y_ref[...]) * g_ref[...]

    return pl.pallas_call(body, name=name, grid=(T // tr,), in_specs=[_row(tr, D), _row(tr, D), _full((1, D))],
                          out_specs=_row(tr, D), out_shape=SDS((T, D), F32), compiler_params=_cp("parallel"))(x, y, g)


def rms_bwd(x, g, dy, res, *, name):
    T, D = x.shape
    tr = _pick(T, 512)
    has_res = res is not None

    def body(*refs):
        if has_res:
            x_ref, g_ref, dy_ref, res_ref, dx_ref, dg_ref = refs
        else:
            x_ref, g_ref, dy_ref, dx_ref, dg_ref = refs
        xv = x_ref[...]
        r = lax.rsqrt(jnp.mean(xv * xv, axis=-1, keepdims=True) + EPS)
        xh = xv * r
        dyv = dy_ref[...]
        dxh = dyv * g_ref[...]
        dx = r * (dxh - xh * jnp.mean(dxh * xh, axis=-1, keepdims=True))
        if has_res:
            dx = dx + res_ref[...]
        dx_ref[...] = dx

        @pl.when(pl.program_id(0) == 0)
        def _():
            dg_ref[...] = jnp.zeros_like(dg_ref)

        dg_ref[...] += jnp.sum(dyv * xh, axis=0, keepdims=True)

    ins = [x, g, dy] + ([res] if has_res else [])
    in_specs = [_row(tr, D), _full((1, D)), _row(tr, D)] + ([_row(tr, D)] if has_res else [])
    return pl.pallas_call(body, name=name, grid=(T // tr,), in_specs=in_specs,
                          out_specs=[_row(tr, D), _full((1, D))],
                          out_shape=[SDS((T, D), F32), SDS((1, D), F32)], compiler_params=_cp("arbitrary"))(*ins)


def loss_fwd_bwd(y, target, *, name):
    T, D = y.shape
    tr = _pick(T, 512)

    def body(y_ref, t_ref, l_ref, dy_ref):
        err = y_ref[...] - t_ref[...]
        dy_ref[...] = err * (1.0 / D)

        @pl.when(pl.program_id(0) == 0)
        def _():
            l_ref[...] = jnp.zeros_like(l_ref)

        l_ref[...] += 0.5 * jnp.sum(jnp.mean(err * err, axis=-1, keepdims=True))

    return pl.pallas_call(body, name=name, grid=(T // tr,), in_specs=[_row(tr, D), _row(tr, D)],
                          out_specs=[_full((SUBLANES, LANES)), _row(tr, D)],
                          out_shape=[SDS((SUBLANES, LANES), F32), SDS((T, D), F32)],
                          compiler_params=_cp("arbitrary"))(y, target)


def _s5_disc(lr, li, ldt, btr, bti):
    dt = jnp.exp(ldt)
    k = lax.broadcasted_iota(jnp.int32, (SUBLANES, S5_LANES), 0).astype(F32)
    kf = k + 1.0
    kb = 8.0 - k
    ph = li * dt
    lm = lr * dt
    tf_re = jnp.exp(kf * lm) * jnp.cos(kf * ph)
    tf_im = jnp.exp(kf * lm) * jnp.sin(kf * ph)
    tb_re = jnp.exp(kb * lm) * jnp.cos(kb * ph)
    tb_im = -jnp.exp(kb * lm) * jnp.sin(kb * ph)
    mag = jnp.exp(lm)
    ab_re = mag * jnp.cos(ph)
    ab_im = mag * jnp.sin(ph)
    den = lr * lr + li * li
    nr = ab_re - 1.0
    ni = ab_im
    q_re = (nr * lr + ni * li) / den
    q_im = (ni * lr - nr * li) / den
    bbt_re = q_re * btr - q_im * bti
    bbt_im = q_re * bti + q_im * btr
    return tf_re, tf_im, tb_re, tb_im, bbt_re, bbt_im


def _s5_disc_core(lr, li, ldt, btr, bti):
    dt = jnp.exp(ldt)
    mag = jnp.exp(lr * dt)
    ab_re = mag * jnp.cos(li * dt)
    ab_im = mag * jnp.sin(li * dt)
    den = lr * lr + li * li
    nr = ab_re - 1.0
    ni = ab_im
    q_re = (nr * lr + ni * li) / den
    q_im = (ni * lr - nr * li) / den
    return ab_re, ab_im, q_re * btr - q_im * bti, q_re * bti + q_im * btr


def s5_disc_fwd(lr, li, ldt, btr, bti, *, name):
    def body(lr_ref, li_ref, ldt_ref, btr_ref, bti_ref, *outs):
        vals = _s5_disc(lr_ref[...], li_ref[...], ldt_ref[...], btr_ref[...], bti_ref[...])
        for o, v in zip(outs, vals):
            o[...] = v

    tab = SDS((SUBLANES, S5_LANES), F32)
    bb = SDS((S5_GROUP, S5_LANES), F32)
    return pl.pallas_call(body, name=name, out_shape=[tab, tab, tab, tab, bb, bb])(lr, li, ldt, btr, bti)


def s5_disc_bwd(lr, li, ldt, btr, bti, dab_re, dab_im, dbbt_re, dbbt_im, *, name):
    def body(lr_ref, li_ref, ldt_ref, btr_ref, bti_ref, dar_ref, dai_ref, dbr_ref, dbi_ref,
             dlr_ref, dli_ref, dldt_ref, dbtr_ref, dbti_ref):
        _, vjp = jax.vjp(_s5_disc_core, lr_ref[...], li_ref[...], ldt_ref[...], btr_ref[...], bti_ref[...])
        dlr, dli, dldt, dbtr, dbti = vjp((dar_ref[...], dai_ref[...], dbr_ref[...], dbi_ref[...]))
        dlr_ref[...] = dlr
        dli_ref[...] = dli
        dbtr_ref[...] = dbtr
        dbti_ref[...] = dbti
        lane_group = lax.broadcasted_iota(jnp.int32, (S5_LANES, LANES), 0) // S5_STATE
        col = lax.broadcasted_iota(jnp.int32, (S5_LANES, LANES), 1)
        ind = (lane_group == col).astype(F32)
        dldt_ref[...] = jnp.dot(jnp.broadcast_to(dldt, (SUBLANES, S5_LANES)), ind,
                                precision=lax.Precision.HIGHEST, preferred_element_type=F32)

    row = SDS((1, S5_LANES), F32)
    bb = SDS((S5_GROUP, S5_LANES), F32)
    return pl.pallas_call(body, name=name, out_shape=[row, row, SDS((SUBLANES, LANES), F32), bb, bb])(
        lr, li, ldt, btr, bti, dab_re, dab_im, dbbt_re, dbbt_im)


S5_NB = 1024


def s5_interleave(re, im, axis):
    parts = []
    for n in range(S5_LANES // S5_NB):
        sl = [slice(None)] * re.ndim
        sl[axis] = slice(n * S5_NB, (n + 1) * S5_NB)
        parts += [re[tuple(sl)], im[tuple(sl)]]
    return jnp.concatenate(parts, axis=axis)


def s5_deinterleave(a, axis):
    re, im = [], []
    for n in range(S5_LANES // S5_NB):
        sl = [slice(None)] * a.ndim
        sl[axis] = slice(2 * n * S5_NB, (2 * n + 1) * S5_NB)
        re.append(a[tuple(sl)])
        sl[axis] = slice((2 * n + 1) * S5_NB, (2 * n + 2) * S5_NB)
        im.append(a[tuple(sl)])
    return jnp.concatenate(re, axis=axis), jnp.concatenate(im, axis=axis)


def s5_scan(bu, tab_re, tab_im, *, reverse, name):
    T = bu.shape[0]
    nb = S5_NB
    tc = _pick(T, 256)
    nl = S5_LANES // nb
    nt = T // tc
    ntile = tc // SUBLANES
    step_rows = ((1, 7), (2, 6), (4, 4)) if reverse else ((1, 0), (2, 1), (4, 3))

    def body(br_ref, bi_ref, tr_ref, ti_ref, xo_ref, cr_ref, ci_ref):
        @pl.when(pl.program_id(1) == 0)
        def _():
            cr_ref[...] = jnp.zeros_like(cr_ref)
            ci_ref[...] = jnp.zeros_like(ci_ref)

        tr = tr_ref[...]
        ti = ti_ref[...]
        io = lax.broadcasted_iota(jnp.int32, (SUBLANES, nb), 0)
        steps = [(d, tr_ref[r:r + 1, :], ti_ref[r:r + 1, :]) for d, r in step_rows]

        def tile(i, carry):
            cr, ci = carry
            j = (ntile - 1 - i) if reverse else i
            r0 = pl.multiple_of(j * SUBLANES, SUBLANES)
            xr = br_ref[pl.ds(r0, SUBLANES), :]
            xi = bi_ref[pl.ds(r0, SUBLANES), :]
            for d, pr, pi in steps:
                if reverse:
                    keep = io < SUBLANES - d
                    sh = SUBLANES - d
                else:
                    keep = io >= d
                    sh = d
                sr = jnp.where(keep, pltpu.roll(xr, sh, 0), 0.0)
                si = jnp.where(keep, pltpu.roll(xi, sh, 0), 0.0)
                xr, xi = xr + pr * sr - pi * si, xi + pr * si + pi * sr
            xr, xi = xr + tr * cr - ti * ci, xi + tr * ci + ti * cr
            xo_ref[pl.ds(r0, SUBLANES), 0:nb] = xr
            xo_ref[pl.ds(r0, SUBLANES), nb:2 * nb] = xi
            if reverse:
                return xr[0:1, :], xi[0:1, :]
            return xr[SUBLANES - 1:SUBLANES, :], xi[SUBLANES - 1:SUBLANES, :]

        cr, ci = lax.fori_loop(0, ntile, tile, (cr_ref[0:1, :], ci_ref[0:1, :]))
        cr_ref[0:1, :] = cr
        ci_ref[0:1, :] = ci

    def tmap(t):
        return (nt - 1 - t) if reverse else t

    re_spec = pl.BlockSpec((tc, nb), lambda n, t: (tmap(t), 2 * n))
    im_spec = pl.BlockSpec((tc, nb), lambda n, t: (tmap(t), 2 * n + 1))
    tab_spec = pl.BlockSpec((SUBLANES, nb), lambda n, t: (0, n))
    return pl.pallas_call(
        body, name=name, grid=(nl, nt), in_specs=[re_spec, im_spec, tab_spec, tab_spec],
        out_specs=pl.BlockSpec((tc, 2 * nb), lambda n, t: (tmap(t), n)), out_shape=SDS((T, 2 * S5_LANES), F32),
        scratch_shapes=[pltpu.VMEM((SUBLANES, nb), F32), pltpu.VMEM((SUBLANES, nb), F32)],
        compiler_params=_cp("parallel", "arbitrary"),
    )(bu, bu, tab_re, tab_im)


def s5_da(lam, xs, *, name):
    T = xs.shape[0]
    nb = S5_NB
    tc = _pick(T, 256)
    nl, nt = S5_LANES // nb, T // tc
    hb = tc // SUBLANES

    def body(lr_ref, li_ref, xr_ref, xi_ref, hr_ref, hi_ref, dar_ref, dai_ref):
        t = pl.program_id(1)

        @pl.when(t == 0)
        def _():
            dar_ref[...] = jnp.zeros_like(dar_ref)
            dai_ref[...] = jnp.zeros_like(dai_ref)

        io = lax.broadcasted_iota(jnp.int32, (tc, nb), 0)
        first = jnp.where(t > 0, 1.0, 0.0)
        pr = jnp.where(io >= 1, pltpu.roll(xr_ref[...], 1, 0), hr_ref[SUBLANES - 1:SUBLANES, :] * first)
        pi = jnp.where(io >= 1, pltpu.roll(xi_ref[...], 1, 0), hi_ref[SUBLANES - 1:SUBLANES, :] * first)
        lr = lr_ref[...]
        li = li_ref[...]
        dar_ref[...] += jnp.sum(lr * pr + li * pi, axis=0, keepdims=True)
        dai_ref[...] += jnp.sum(li * pr - lr * pi, axis=0, keepdims=True)

    re_blk = pl.BlockSpec((tc, nb), lambda n, t: (t, 2 * n))
    im_blk = pl.BlockSpec((tc, nb), lambda n, t: (t, 2 * n + 1))
    re_halo = pl.BlockSpec((SUBLANES, nb), lambda n, t: (jnp.maximum(t * hb - 1, 0), 2 * n))
    im_halo = pl.BlockSpec((SUBLANES, nb), lambda n, t: (jnp.maximum(t * hb - 1, 0), 2 * n + 1))
    acc = pl.BlockSpec((1, nb), lambda n, t: (0, n))
    row = SDS((1, S5_LANES), F32)
    return pl.pallas_call(body, name=name, grid=(nl, nt), in_specs=[re_blk, im_blk, re_blk, im_blk, re_halo, im_halo],
                          out_specs=[acc, acc], out_shape=[row, row],
                          compiler_params=_cp("parallel", "arbitrary"))(lam, lam, xs, xs, xs, xs)


def s5_out_fwd(yc, u, d, *, name):
    T, C = yc.shape
    tr = _pick(T, 512)

    def body(yc_ref, u_ref, d_ref, yl_ref, yg_ref):
        yl = yc_ref[...] + d_ref[...] * u_ref[...]
        yl_ref[...] = yl
        yg_ref[...] = jax.nn.gelu(yl)

    return pl.pallas_call(body, name=name, grid=(T // tr,), in_specs=[_row(tr, C), _row(tr, C), _full((1, C))],
                          out_specs=[_row(tr, C)] * 2, out_shape=[SDS((T, C), F32)] * 2,
                          compiler_params=_cp("parallel"))(yc, u, d)


def glu_fwd(yg, gl, *, out_cols, name):
    T, C = yg.shape
    tr = _pick(T, 512)

    def body(yg_ref, gl_ref, o_ref):
        o_ref[...] = yg_ref[...] * jax.nn.sigmoid(gl_ref[...])

    return pl.pallas_call(body, name=name, grid=(T // tr,), in_specs=[_row(tr, C)] * 2, out_specs=_row(tr, C),
                          out_shape=SDS((T, out_cols), F32), compiler_params=_cp("parallel"))(yg, gl)


def glu_bwd(yg, gl, dy, *, name):
    T, C = yg.shape
    tr = _pick(T, 512)

    def body(yg_ref, gl_ref, dy_ref, dyg_ref, dgl_ref):
        s = jax.nn.sigmoid(gl_ref[...])
        dyv = dy_ref[...]
        dyg_ref[...] = dyv * s
        dgl_ref[...] = dyv * yg_ref[...] * s * (1.0 - s)

    return pl.pallas_call(body, name=name, grid=(T // tr,), in_specs=[_row(tr, C)] * 3, out_specs=[_row(tr, C)] * 2,
                          out_shape=[SDS((T, C), F32)] * 2, compiler_params=_cp("parallel"))(yg, gl, dy)


def s5_out_bwd(yl, u, d, dyg_a, dyg_b, *, name):
    T, C = yl.shape
    tr = _pick(T, 512)

    def body(yl_ref, u_ref, d_ref, da_ref, db_ref, dyl_ref, du_ref, dd_ref):
        dyl = (da_ref[...] + db_ref[...]) * _gelu_grad(yl_ref[...])
        dyl_ref[...] = dyl
        du_ref[...] = dyl * d_ref[...]

        @pl.when(pl.program_id(0) == 0)
        def _():
            dd_ref[...] = jnp.zeros_like(dd_ref)

        dd_ref[...] += jnp.sum(dyl * u_ref[...], axis=0, keepdims=True)

    return pl.pallas_call(body, name=name, grid=(T // tr,),
                          in_specs=[_row(tr, C), _row(tr, C), _full((1, C)), _row(tr, C), _row(tr, C)],
                          out_specs=[_row(tr, C), _row(tr, C), _full((1, C))],
                          out_shape=[SDS((T, C), F32), SDS((T, C), F32), SDS((1, C), F32)],
                          compiler_params=_cp("arbitrary"))(yl, u, d, dyg_a, dyg_b)


def add2(a, b, *, name):
    T, C = a.shape
    tr = _pick(T, 512)

    def body(a_ref, b_ref, o_ref):
        o_ref[...] = a_ref[...] + b_ref[...]

    return pl.pallas_call(body, name=name, grid=(T // tr,), in_specs=[_row(tr, C)] * 2, out_specs=_row(tr, C),
                          out_shape=SDS((T, C), F32), compiler_params=_cp("parallel"))(a, b)


def _tri(n, upper):
    r = lax.broadcasted_iota(jnp.int32, (n, n), 0)
    c = lax.broadcasted_iota(jnp.int32, (n, n), 1)
    return ((c >= r) if upper else (c <= r)).astype(F32)


def fox_gate_fwd(fl, bf, *, fl_col, name):
    T = fl.shape[0]
    tb = _pick(T, 256)

    def body(fl_ref, bf_ref, f_ref, c_ref):
        @pl.when(pl.program_id(0) == 0)
        def _():
            c_ref[...] = jnp.zeros_like(c_ref)

        lf = jax.nn.log_sigmoid(fl_ref[...] + bf_ref[...])
        f = jnp.dot(_tri(tb, False), lf, precision=lax.Precision.HIGHEST, preferred_element_type=F32) + c_ref[0:1, :]
        f_ref[...] = f * LOG2E
        c_ref[0:1, :] = f[tb - 1:tb, :]

    fl_spec = pl.BlockSpec((tb, LANES), lambda i: (i, fl_col))
    return pl.pallas_call(body, name=name, grid=(T // tb,), in_specs=[fl_spec, _full((1, LANES))],
                          out_specs=_row(tb, LANES), out_shape=SDS((T, LANES), F32),
                          scratch_shapes=[pltpu.VMEM((SUBLANES, LANES), F32)], compiler_params=_cp("arbitrary"))(fl, bf)


def fox_gate_bwd(fl, bf, df, *, fl_col, name):
    T = fl.shape[0]
    tb = _pick(T, 256)
    nt = T // tb

    def body(fl_ref, bf_ref, df_ref, dfl_ref, dbf_ref, c_ref):
        @pl.when(pl.program_id(0) == 0)
        def _():
            c_ref[...] = jnp.zeros_like(c_ref)
            dbf_ref[...] = jnp.zeros_like(dbf_ref)

        dlf = jnp.dot(_tri(tb, True), df_ref[...], precision=lax.Precision.HIGHEST, preferred_element_type=F32) + c_ref[0:1, :]
        c_ref[0:1, :] = dlf[0:1, :]
        dfl = dlf * jax.nn.sigmoid(-(fl_ref[...] + bf_ref[...]))
        dfl_ref[...] = dfl
        dbf_ref[...] += jnp.sum(dfl, axis=0, keepdims=True)

    rev = pl.BlockSpec((tb, LANES), lambda i: (nt - 1 - i, 0))
    fl_rev = pl.BlockSpec((tb, LANES), lambda i: (nt - 1 - i, fl_col))
    return pl.pallas_call(body, name=name, grid=(nt,), in_specs=[fl_rev, _full((1, LANES)), rev],
                          out_specs=[rev, _full((1, LANES))], out_shape=[SDS((T, LANES), F32), SDS((1, LANES), F32)],
                          scratch_shapes=[pltpu.VMEM((SUBLANES, LANES), F32)], compiler_params=_cp("arbitrary"))(fl, bf, df)


FOX_BLOCK = 512
FOX_PAIRS = FOX_HEADS // 2
_NT = (((1,), (1,)), ((), ()))


LOG2E = 1.4426950408889634
FOX_FWD_UNROLL = 4
FOX_BWD_UNROLL = 2


def _fox_block(T):
    return _pick(T, FOX_BLOCK)


def _own_lanes(lane, hh):
    return (lane < FOX_HEAD_DIM) if hh == 0 else (lane >= FOX_HEAD_DIM)


def _grouped_steps(step, lo, n, unroll, init):
    def trip(t, c):
        for u in range(unroll):
            c = step(lo + t * unroll + u, c)
        return c

    carry = lax.fori_loop(0, n // unroll, trip, init)
    for u in range(unroll - 1):
        carry = lax.cond(n % unroll > u, lambda c: step(lo + (n // unroll) * unroll + u, c), lambda c: c, carry)
    return carry


Q_TILE0, K_TILE0, V_TILE0, O_TILE0 = 4, 8, 12, 4
FL_TILE = 16
POOL_COL = 2


def fox_fwd(z, f_col, f_row, ycat, *, name):
    T = z.shape[0]
    blk = _fox_block(T)
    nb = T // blk
    scale = FOX_HEAD_DIM ** -0.5

    def body(q_ref, k_ref, v_ref, fc_ref, fr_ref, prev_ref, o_ref, l_ref):
        i = pl.program_id(1)
        row = lax.broadcasted_iota(jnp.int32, (blk, blk), 0)
        col = lax.broadcasted_iota(jnp.int32, (blk, blk), 1)
        lane = lax.broadcasted_iota(jnp.int32, (blk, LANES), 1)
        qt = q_ref[...] * (scale * LOG2E)
        outs = []
        for hh in range(2):
            qh = jnp.where(_own_lanes(lane, hh), qt, 0.0).astype(MXU_DTYPE)
            fi = fc_ref[0, :, hh:hh + 1]

            def step(j, carry, masked=False):
                m, l, acc = carry
                r0 = pl.multiple_of(j * blk, blk)
                kj = k_ref[pl.ds(r0, blk), :].astype(MXU_DTYPE)
                vj = v_ref[pl.ds(r0, blk), :].astype(MXU_DTYPE)
                s = lax.dot_general(qh, kj, _NT, preferred_element_type=F32) + (fi - fr_ref[0, j, hh:hh + 1, :])
                if masked:
                    s = jnp.where(col <= row, s, -jnp.inf)
                m_new = jnp.maximum(m, jnp.max(s, axis=-1, keepdims=True))
                p = jnp.exp2(s - m_new)
                alpha = jnp.exp2(m - m_new)
                l = alpha * l + jnp.sum(p, axis=-1, keepdims=True)
                acc = alpha * acc + jnp.dot(p.astype(MXU_DTYPE), vj, preferred_element_type=F32)
                return m_new, l, acc

            init = (jnp.full((blk, 1), -jnp.inf, F32), jnp.zeros((blk, 1), F32), jnp.zeros((blk, LANES), F32))
            m, l, acc = step(i, _grouped_steps(step, 0, i, FOX_FWD_UNROLL, init), True)
            outs.append(acc / l)
            l_ref[0, :, hh:hh + 1] = m + jnp.log2(l)
        o_ref[...] = jnp.where(_own_lanes(lane, 0), outs[0], outs[1])

    qspec = pl.BlockSpec((blk, LANES), lambda h, i: (i, Q_TILE0 + h))
    kspec = pl.BlockSpec((T, LANES), lambda h, i: (0, K_TILE0 + h))
    vspec = pl.BlockSpec((T, LANES), lambda h, i: (0, V_TILE0 + h))
    ospec = pl.BlockSpec((blk, LANES), lambda h, i: (i, O_TILE0 + h))
    cspec = pl.BlockSpec((1, blk, 2), lambda h, i: (h, i, 0))
    rspec = pl.BlockSpec((1, nb, 2, blk), lambda h, i: (h, 0, 0, 0))
    return pl.pallas_call(body, name=name, grid=(FOX_PAIRS, nb), in_specs=[qspec, kspec, vspec, cspec, rspec, ANY],
                          out_specs=[ospec, cspec], out_shape=[SDS(ycat.shape, F32), SDS((FOX_PAIRS, T, 2), F32)],
                          input_output_aliases={5: 0},
                          compiler_params=_cp("parallel", "parallel"))(z, z, z, f_col, f_row, ycat)


def fox_bwd_q(z, ycat, dycat, f_col, f_row, lse_col, *, name):
    T = z.shape[0]
    blk = _fox_block(T)
    nb = T // blk
    scale = FOX_HEAD_DIM ** -0.5

    def body(q_ref, k_ref, v_ref, o_ref, do_ref, fc_ref, fr_ref, lc_ref, dq_ref, dd_ref, df_ref):
        i = pl.program_id(1)
        row = lax.broadcasted_iota(jnp.int32, (blk, blk), 0)
        col = lax.broadcasted_iota(jnp.int32, (blk, blk), 1)
        lane = lax.broadcasted_iota(jnp.int32, (blk, LANES), 1)
        qt = q_ref[...] * (scale * LOG2E)
        dot = do_ref[...]
        prod = dot * o_ref[...]
        outs = []
        for hh in range(2):
            own = _own_lanes(lane, hh)
            qh = jnp.where(own, qt, 0.0).astype(MXU_DTYPE)
            dob = jnp.where(own, dot, 0.0).astype(MXU_DTYPE)
            dd = jnp.sum(jnp.where(own, prod, 0.0), axis=-1, keepdims=True)
            dd_ref[0, :, hh:hh + 1] = dd
            fi = fc_ref[0, :, hh:hh + 1]
            lse = lc_ref[0, :, hh:hh + 1]

            def step(j, carry, masked=False):
                dq, df = carry
                r0 = pl.multiple_of(j * blk, blk)
                kj = k_ref[pl.ds(r0, blk), :].astype(MXU_DTYPE)
                vj = v_ref[pl.ds(r0, blk), :].astype(MXU_DTYPE)
                s = lax.dot_general(qh, kj, _NT, preferred_element_type=F32) + (fi - fr_ref[0, j, hh:hh + 1, :])
                p = jnp.exp2(s - lse)
                if masked:
                    p = jnp.where(col <= row, p, 0.0)
                dp = lax.dot_general(dob, vj, _NT, preferred_element_type=F32)
                ds = p * (dp - dd)
                return (dq + jnp.dot(ds.astype(MXU_DTYPE), kj, preferred_element_type=F32),
                        df + jnp.sum(ds, axis=-1, keepdims=True))

            init = (jnp.zeros((blk, LANES), F32), jnp.zeros((blk, 1), F32))
            dq, df = step(i, _grouped_steps(step, 0, i, FOX_BWD_UNROLL, init), True)
            outs.append(dq * scale)
            df_ref[0, :, hh:hh + 1] = df
        dq_ref[...] = jnp.where(_own_lanes(lane, 0), outs[0], outs[1])

    qspec = pl.BlockSpec((blk, LANES), lambda h, i: (i, Q_TILE0 + h))
    kspec = pl.BlockSpec((T, LANES), lambda h, i: (0, K_TILE0 + h))
    vspec = pl.BlockSpec((T, LANES), lambda h, i: (0, V_TILE0 + h))
    ospec = pl.BlockSpec((blk, LANES), lambda h, i: (i, O_TILE0 + h))
    dqspec = pl.BlockSpec((blk, LANES), lambda h, i: (i, h))
    cspec = pl.BlockSpec((1, blk, 2), lambda h, i: (h, i, 0))
    rspec = pl.BlockSpec((1, nb, 2, blk), lambda h, i: (h, 0, 0, 0))
    stat = SDS((FOX_PAIRS, T, 2), F32)
    return pl.pallas_call(body, name=name, grid=(FOX_PAIRS, nb),
                          in_specs=[qspec, kspec, vspec, ospec, ospec, cspec, rspec, cspec],
                          out_specs=[dqspec, cspec, cspec], out_shape=[SDS((T, FOX_WIDTH), F32), stat, stat],
                          compiler_params=_cp("parallel", "parallel"))(z, z, z, ycat, dycat, f_col, f_row, lse_col)


def fox_bwd_kv(z, dycat, f_col, f_row, lse_row, dd_row, dfq_col, *, name):
    T = z.shape[0]
    blk = _fox_block(T)
    nb = T // blk
    scale = FOX_HEAD_DIM ** -0.5

    def body(q_ref, k_ref, v_ref, do_ref, fc_ref, fr_ref, lr_ref, dr_ref, dfq_ref, dk_ref, dv_ref, df_ref):
        j = pl.program_id(1)
        row = lax.broadcasted_iota(jnp.int32, (blk, blk), 0)
        col = lax.broadcasted_iota(jnp.int32, (blk, blk), 1)
        lane = lax.broadcasted_iota(jnp.int32, (blk, LANES), 1)
        kt = k_ref[...]
        vt = v_ref[...]
        dks, dvs = [], []
        for hh in range(2):
            own = _own_lanes(lane, hh)
            kh = jnp.where(own, kt, 0.0).astype(MXU_DTYPE)
            vh = jnp.where(own, vt, 0.0).astype(MXU_DTYPE)
            fj = fc_ref[0, :, hh:hh + 1]

            def step(i, carry, masked=False):
                dk, dv, df = carry
                r0 = pl.multiple_of(i * blk, blk)
                qi = (q_ref[pl.ds(r0, blk), :] * (scale * LOG2E)).astype(MXU_DTYPE)
                doi = do_ref[pl.ds(r0, blk), :].astype(MXU_DTYPE)
                st = lax.dot_general(kh, qi, _NT, preferred_element_type=F32) + (fr_ref[0, i, hh:hh + 1, :] - fj)
                pt = jnp.exp2(st - lr_ref[0, i, hh:hh + 1, :])
                if masked:
                    pt = jnp.where(col >= row, pt, 0.0)
                dv = dv + jnp.dot(pt.astype(MXU_DTYPE), doi, preferred_element_type=F32)
                dpt = lax.dot_general(vh, doi, _NT, preferred_element_type=F32)
                dst = pt * (dpt - dr_ref[0, i, hh:hh + 1, :])
                dk = dk + jnp.dot(dst.astype(MXU_DTYPE), qi, preferred_element_type=F32)
                df = df - jnp.sum(dst, axis=-1, keepdims=True)
                return dk, dv, df

            init = (jnp.zeros((blk, LANES), F32), jnp.zeros((blk, LANES), F32), dfq_ref[0, :, hh:hh + 1])
            dk, dv, df = _grouped_steps(step, j + 1, nb - 1 - j, FOX_BWD_UNROLL, step(j, init, True))
            dks.append(dk * (1.0 / LOG2E))
            dvs.append(dv)
            df_ref[0, :, hh:hh + 1] = df
        dk_ref[...] = jnp.where(_own_lanes(lane, 0), dks[0], dks[1])
        dv_ref[...] = jnp.where(_own_lanes(lane, 0), dvs[0], dvs[1])

    bspec = pl.BlockSpec((blk, LANES), lambda h, j: (j, h))
    qspec = pl.BlockSpec((T, LANES), lambda h, j: (0, Q_TILE0 + h))
    kspec = pl.BlockSpec((blk, LANES), lambda h, j: (j, K_TILE0 + h))
    vspec = pl.BlockSpec((blk, LANES), lambda h, j: (j, V_TILE0 + h))
    dospec = pl.BlockSpec((T, LANES), lambda h, j: (0, O_TILE0 + h))
    cspec = pl.BlockSpec((1, blk, 2), lambda h, j: (h, j, 0))
    rspec = pl.BlockSpec((1, nb, 2, blk), lambda h, j: (h, 0, 0, 0))
    return pl.pallas_call(body, name=name, grid=(FOX_PAIRS, nb),
                          in_specs=[qspec, kspec, vspec, dospec, cspec, rspec, rspec, rspec, cspec],
                          out_specs=[bspec, bspec, cspec],
                          out_shape=[SDS((T, FOX_WIDTH), F32), SDS((T, FOX_WIDTH), F32), SDS((FOX_PAIRS, T, 2), F32)],
                          compiler_params=_cp("parallel", "parallel"))(z, z, z, dycat, f_col, f_row, lse_row, dd_row, dfq_col)


def _pairs_col(a, T):
    return jnp.transpose(a[:, :FOX_HEADS].reshape(T, FOX_PAIRS, 2), (1, 0, 2))


def _col_to_row(a, T):
    blk = _fox_block(T)
    return jnp.transpose(a.reshape(FOX_PAIRS, T // blk, blk, 2), (0, 1, 3, 2))


def _pairs_to_lanes(a, T):
    flat = jnp.transpose(a, (1, 0, 2)).reshape(T, FOX_HEADS)
    return jnp.pad(flat, ((0, 0), (0, LANES - FOX_HEADS)))


def _pool_counts(t0, n, w):
    t = (t0 + lax.broadcasted_iota(jnp.int32, (n, 1), 0)).astype(F32)
    return jnp.minimum(t + 1.0, float(w))


def pool_window(x, *, adjoint, name, in_col=0, into=None, out_col=0):
    T, C = x.shape[0], len(POOL_WINDOWS) * POOL_GROUP_DIM
    tr = _pick(T, 512)
    nt = T // tr
    hb = tr // POOL_HALO
    n = tr + POOL_HALO

    def body(x_ref, h_ref, *rest):
        o_ref = rest[-1]
        i = pl.program_id(0)
        cur = x_ref[...]
        if adjoint:
            halo = h_ref[...] * jnp.where(i < nt - 1, 1.0, 0.0)
            ext = jnp.concatenate([cur, halo], axis=0)
            t0 = i * tr
        else:
            halo = h_ref[...] * jnp.where(i > 0, 1.0, 0.0)
            ext = jnp.concatenate([halo, cur], axis=0)
            t0 = i * tr - POOL_HALO
        sums = {}
        for g, w in enumerate(POOL_WINDOWS):
            ls = slice(g * POOL_GROUP_DIM, (g + 1) * POOL_GROUP_DIM)
            s = ext[:, ls]
            if adjoint:
                s = s / _pool_counts(t0, n, w)
            d = 1
            while d < w:
                s = s + pltpu.roll(s, (n - d) if adjoint else d, 0)
                d *= 2
            if adjoint:
                o_ref[:, ls] = s[0:tr, :] - cur[:, ls]
            else:
                o_ref[:, ls] = s[POOL_HALO:n, :] / _pool_counts(i * tr, tr, w) - cur[:, ls]

    if adjoint:
        halo_spec = pl.BlockSpec((POOL_HALO, C), lambda i: (jnp.minimum((i + 1) * hb, T // POOL_HALO - 1), in_col))
    else:
        halo_spec = pl.BlockSpec((POOL_HALO, C), lambda i: (jnp.maximum(i * hb - 1, 0), in_col))
    x_spec = pl.BlockSpec((tr, C), lambda i: (i, in_col))
    if into is None:
        return pl.pallas_call(body, name=name, grid=(nt,), in_specs=[x_spec, halo_spec], out_specs=_row(tr, C),
                              out_shape=SDS((T, C), F32), compiler_params=_cp("parallel"))(x, x)
    return pl.pallas_call(body, name=name, grid=(nt,), in_specs=[x_spec, halo_spec, ANY],
                          out_specs=pl.BlockSpec((tr, C), lambda i: (i, out_col)), out_shape=SDS(into.shape, F32),
                          input_output_aliases={2: 0}, compiler_params=_cp("parallel"))(x, x, into)


def colscale_fwd(a, s, *, out_cols, name):
    T, C = a.shape
    tr = _pick(T, 512)

    def body(a_ref, s_ref, o_ref):
        o_ref[...] = a_ref[...] * s_ref[...]

    return pl.pallas_call(body, name=name, grid=(T // tr,), in_specs=[_row(tr, C), _full((1, C))], out_specs=_row(tr, C),
                          out_shape=SDS((T, out_cols), F32), compiler_params=_cp("parallel"))(a, s)


def colscale_bwd(a, s, dy, *, name):
    T, C = a.shape
    tr = _pick(T, 512)

    def body(a_ref, s_ref, dy_ref, da_ref, ds_ref):
        dyv = dy_ref[...]
        da_ref[...] = dyv * s_ref[...]

        @pl.when(pl.program_id(0) == 0)
        def _():
            ds_ref[...] = jnp.zeros_like(ds_ref)

        ds_ref[...] += jnp.sum(dyv * a_ref[...], axis=0, keepdims=True)

    return pl.pallas_call(body, name=name, grid=(T // tr,), in_specs=[_row(tr, C), _full((1, C)), _row(tr, C)],
                          out_specs=[_row(tr, C), _full((1, C))], out_shape=[SDS((T, C), F32), SDS((1, C), F32)],
                          compiler_params=_cp("arbitrary"))(a, s, dy)


SGU_ROWS = 512


def _sgu_norm(v, ln_g, ln_b):
    vg = jax.nn.gelu(v)
    xc = vg - jnp.mean(vg, axis=-1, keepdims=True)
    r = lax.rsqrt(jnp.mean(xc * xc, axis=-1, keepdims=True) + EPS)
    xh = xc * r
    return xh * ln_g + ln_b, xh, r


def _rowc(tr, c, cb):
    return pl.BlockSpec((tr, c), lambda i: (i, cb))


def sgu_fwd(z, ln_g, ln_b, ws, bst, ycat, *, name):
    T, C = z.shape[0], SGU_GROUPS * SGU_GROUP_DIM
    tr = _pick(T, SGU_ROWS)

    def body(u_ref, v_ref, g_ref, b_ref, ws_ref, bst_ref, prev_ref, o_ref):
        vn, _, _ = _sgu_norm(v_ref[...], g_ref[...], b_ref[...])
        vn = vn.astype(MXU_DTYPE)
        ug = jax.nn.gelu(u_ref[...])
        for g in range(SGU_GROUPS):
            w = ws_ref[g].astype(MXU_DTYPE)
            bias = bst_ref[:, g:g + 1]
            for c in range(tr // CHUNK):
                rs = slice(c * CHUNK, (c + 1) * CHUNK)
                ls = slice(g * SGU_GROUP_DIM, (g + 1) * SGU_GROUP_DIM)
                mixed = jnp.dot(w, vn[rs, ls], preferred_element_type=F32) + bias
                o_ref[rs, ls] = ug[rs, ls] * mixed

    return pl.pallas_call(body, name=name, grid=(T // tr,),
                          in_specs=[_rowc(tr, C, 0), _rowc(tr, C, 1), _full((1, C)), _full((1, C)),
                                    _full((SGU_GROUPS, CHUNK, CHUNK)), _full((CHUNK, SGU_GROUPS)), ANY],
                          out_specs=_rowc(tr, C, 1), out_shape=SDS(ycat.shape, F32), input_output_aliases={6: 0},
                          compiler_params=_cp("parallel"))(z, z, ln_g, ln_b, ws, bst, ycat)


def sgu_bwd(z, ln_g, ln_b, ws, wst, bst, dycat, *, out_cols, name):
    T, C = z.shape[0], SGU_GROUPS * SGU_GROUP_DIM
    tr = _pick(T, SGU_ROWS)

    def body(u_ref, v_ref, g_ref, b_ref, ws_ref, wst_ref, bst_ref, dy_ref,
             duv_ref, dg_ref, db_ref, dws_ref, dbst_ref, dvn_ref):
        du_ref = duv_ref.at[:, 0:C]
        dv_ref = duv_ref.at[:, C:2 * C]
        @pl.when(pl.program_id(0) == 0)
        def _():
            dg_ref[...] = jnp.zeros_like(dg_ref)
            db_ref[...] = jnp.zeros_like(db_ref)
            dws_ref[...] = jnp.zeros_like(dws_ref)
            dbst_ref[...] = jnp.zeros_like(dbst_ref)

        uv = u_ref[...]
        vv = v_ref[...]
        vn, xh, r = _sgu_norm(vv, g_ref[...], b_ref[...])
        vn = vn.astype(MXU_DTYPE)
        ug = jax.nn.gelu(uv)
        dyv = dy_ref[...]
        for g in range(SGU_GROUPS):
            w = ws_ref[g].astype(MXU_DTYPE)
            wt = wst_ref[g].astype(MXU_DTYPE)
            bias = bst_ref[:, g:g + 1]
            dw = jnp.zeros((CHUNK, CHUNK), F32)
            dbias = jnp.zeros((CHUNK, 1), F32)
            for c in range(tr // CHUNK):
                rs = slice(c * CHUNK, (c + 1) * CHUNK)
                ls = slice(g * SGU_GROUP_DIM, (g + 1) * SGU_GROUP_DIM)
                vblk = vn[rs, ls]
                mixed = jnp.dot(w, vblk, preferred_element_type=F32) + bias
                dyb = dyv[rs, ls]
                du_ref[rs, ls] = dyb * mixed * _gelu_grad(uv[rs, ls])
                dmixed = dyb * ug[rs, ls]
                dbias = dbias + jnp.sum(dmixed, axis=-1, keepdims=True)
                dmb = dmixed.astype(MXU_DTYPE)
                dw = dw + lax.dot_general(dmb, vblk, _NT, preferred_element_type=F32)
                dvn_ref[rs, ls] = jnp.dot(wt, dmb, preferred_element_type=F32)
            dws_ref[g] += dw
            dbst_ref[:, g:g + 1] += dbias
        dvn = dvn_ref[...]
        dg_ref[...] += jnp.sum(dvn * xh, axis=0, keepdims=True)
        db_ref[...] += jnp.sum(dvn, axis=0, keepdims=True)
        dxh = dvn * g_ref[...]
        dvg = r * (dxh - jnp.mean(dxh, axis=-1, keepdims=True) - xh * jnp.mean(dxh * xh, axis=-1, keepdims=True))
        dv_ref[...] = dvg * _gelu_grad(vv)

    wspec = _full((SGU_GROUPS, CHUNK, CHUNK))
    return pl.pallas_call(body, name=name, grid=(T // tr,),
                          in_specs=[_rowc(tr, C, 0), _rowc(tr, C, 1), _full((1, C)), _full((1, C)), wspec, wspec,
                                    _full((CHUNK, SGU_GROUPS)), _rowc(tr, C, 1)],
                          out_specs=[_rowc(tr, 2 * C, 0), _full((1, C)), _full((1, C)), wspec,
                                     _full((CHUNK, SGU_GROUPS))],
                          out_shape=[SDS((T, out_cols), F32), SDS((1, C), F32), SDS((1, C), F32),
                                     SDS((SGU_GROUPS, CHUNK, CHUNK), F32), SDS((CHUNK, SGU_GROUPS), F32)],
                          scratch_shapes=[pltpu.VMEM((tr, C), F32)],
                          compiler_params=_cp("arbitrary"))(z, z, ln_g, ln_b, ws, wst, bst, dycat)


def adamw(w, g, m, v, *, name):
    R, C = w.shape
    tr = _pick(R, 512)
    c1 = 1.0 - ADAM_B1 ** ADAM_STEP
    c2 = 1.0 - ADAM_B2 ** ADAM_STEP

    def body(w_ref, g_ref, m_ref, v_ref, d_ref, nm_ref, nv_ref):
        gv = g_ref[...]
        nm = ADAM_B1 * m_ref[...] + (1.0 - ADAM_B1) * gv
        nv = ADAM_B2 * v_ref[...] + (1.0 - ADAM_B2) * (gv * gv)
        nm_ref[...] = nm
        nv_ref[...] = nv
        d_ref[...] = -ADAM_LR * ((nm / c1) / (jnp.sqrt(nv / c2) + ADAM_EPS) + ADAM_WD * w_ref[...])

    spec = _row(tr, C)
    return pl.pallas_call(body, name=name, grid=(R // tr,), in_specs=[spec] * 4, out_specs=[spec] * 3,
                          out_shape=[SDS((R, C), F32)] * 3, compiler_params=_cp("parallel"))(w, g, m, v)


ANY = pl.BlockSpec(memory_space=pl.ANY)


def _coords():
    return lax.axis_index("x"), lax.axis_index("y"), lax.axis_index("c")


def _other_chips(x, y):
    return [(1 - x, y), (x, 1 - y), (1 - x, 1 - y)]


def _remote(src, dst, send_sems, recv_sems, k, dev):
    return pltpu.make_async_remote_copy(src_ref=src, dst_ref=dst, send_sem=send_sems.at[k], recv_sem=recv_sems.at[k],
                                        device_id=dev, device_id_type=MESH)


LOCAL_CHUNKS = 8


def allgather_chip_shards(shards, small, *, name):
    na = len(shards)

    def body(*refs):
        s_refs, sm_ref = refs[:na], refs[na]
        o_refs, smo_ref = refs[na + 1:2 * na + 1], refs[2 * na + 1]
        send_sems, recv_sems, local_sems = refs[2 * na + 2:]
        x, y, c = _coords()
        j = 2 * x + y
        sibling = (x, y, 1 - c)
        chips = _other_chips(x, y)
        for a in range(na):
            chunk = shards[a].shape[0] // LOCAL_CHUNKS
            for q in range(LOCAL_CHUNKS):
                rows = pl.ds(q * chunk, chunk)
                pltpu.make_async_copy(s_refs[a].at[rows], o_refs[a].at[j, rows], local_sems.at[a]).start()
        pltpu.make_async_copy(sm_ref, smo_ref.at[j], local_sems.at[na]).start()
        sends = []
        for a in range(na):
            half = shards[a].shape[0] // 2
            mine = pl.ds(c * half, half)
            for k, (px, py) in enumerate(chips):
                sends.append(_remote(s_refs[a].at[mine], o_refs[a].at[j, mine], send_sems, recv_sems, 6 * a + k, (px, py, c)))
        for k, (px, py) in enumerate(chips):
            sends.append(_remote(sm_ref, smo_ref.at[j], send_sems, recv_sems, 6 * na + k, (px, py, c)))
        for cp in sends:
            cp.start()
        for a in range(na):
            half = shards[a].shape[0] // 2
            mine = pl.ds(c * half, half)
            for k, (px, py) in enumerate(chips):
                rows = o_refs[a].at[2 * px + py, mine]
                _remote(rows, rows, send_sems, recv_sems, 6 * a + k, (px, py, c)).wait_recv()
                fw = _remote(rows, rows, send_sems, recv_sems, 6 * a + 3 + k, sibling)
                fw.start()
                sends.append(fw)
        for a in range(na):
            half = shards[a].shape[0] // 2
            theirs = pl.ds((1 - c) * half, half)
            for k, (px, py) in enumerate(chips):
                rows = o_refs[a].at[2 * px + py, theirs]
                _remote(rows, rows, send_sems, recv_sems, 6 * a + 3 + k, sibling).wait_recv()
        for k, (px, py) in enumerate(chips):
            slot = smo_ref.at[2 * px + py]
            _remote(slot, slot, send_sems, recv_sems, 6 * na + k, (px, py, c)).wait_recv()
        for cp in sends:
            cp.wait_send()
        for a in range(na):
            pltpu.make_async_copy(s_refs[a], o_refs[a].at[j], local_sems.at[a]).wait()
        pltpu.make_async_copy(sm_ref, smo_ref.at[j], local_sems.at[na]).wait()

    nsem = 6 * na + 3
    outs = pl.pallas_call(
        body, name=name, in_specs=[ANY] * (na + 1), out_specs=[ANY] * (na + 1),
        out_shape=[SDS((N_CHIPS,) + s.shape, s.dtype) for s in shards] + [SDS((N_CHIPS,) + small.shape, small.dtype)],
        scratch_shapes=[pltpu.SemaphoreType.DMA((nsem,)), pltpu.SemaphoreType.DMA((nsem,)),
                        pltpu.SemaphoreType.DMA((na + 1,))])(*shards, small)
    return outs[:na], outs[na]


def swap_sibling_halves(gs, *, name):
    na = len(gs)

    def body(*refs):
        g_refs, o_refs = refs[:na], refs[na:2 * na]
        send_sems, recv_sems = refs[2 * na:]
        x, y, c = _coords()
        cps = []
        for a in range(na):
            half = gs[a].shape[1] // 2
            cps.append(_remote(g_refs[a].at[:, pl.ds((1 - c) * half, half), :], o_refs[a], send_sems, recv_sems, a,
                               (x, y, 1 - c)))
        for cp in cps:
            cp.start()
        for cp in cps:
            cp.wait()

    return pl.pallas_call(body, name=name, in_specs=[ANY] * na, out_specs=[ANY] * na,
                          out_shape=[SDS((g.shape[0], g.shape[1] // 2, g.shape[2]), g.dtype) for g in gs],
                          scratch_shapes=[pltpu.SemaphoreType.DMA((na,)), pltpu.SemaphoreType.DMA((na,))])(*gs)


def add_sibling_half(g, land, c_idx, *, name):
    n, R, C = g.shape
    half = R // 2
    tr = _pick(half, 256)
    nt = half // tr

    def body(c_ref, g_ref, l_ref, of_ref, ob_ref):
        s = g_ref[...] + l_ref[...].astype(F32)
        of_ref[...] = s
        ob_ref[...] = s.astype(ob_ref.dtype)

    blk = pl.BlockSpec((1, tr, C), lambda s, i, c_ref: (s, i, 0))
    gblk = pl.BlockSpec((1, tr, C), lambda s, i, c_ref: (s, c_ref[0] * nt + i, 0))
    return pl.pallas_call(
        body, name=name,
        grid_spec=pltpu.PrefetchScalarGridSpec(num_scalar_prefetch=1, grid=(n, nt), in_specs=[gblk, blk],
                                               out_specs=[blk, blk]),
        out_shape=[SDS((n, half, C), F32), SDS((n, half, C), WIRE_DTYPE)],
        compiler_params=_cp("parallel", "parallel"))(c_idx, g, land)


def send_chip_partials(pbs, *, name):
    na = len(pbs)

    def body(*refs):
        p_refs, o_refs = refs[:na], refs[na:2 * na]
        send_sems, recv_sems = refs[2 * na:]
        x, y, c = _coords()
        cps = []
        for a in range(na):
            for k, (px, py) in enumerate(_other_chips(x, y)):
                cps.append(_remote(p_refs[a].at[2 * px + py], o_refs[a].at[k], send_sems, recv_sems, 3 * a + k, (px, py, c)))
        for cp in cps:
            cp.start()
        for cp in cps:
            cp.wait()

    return pl.pallas_call(body, name=name, in_specs=[ANY] * na, out_specs=[ANY] * na,
                          out_shape=[SDS((3,) + p.shape[1:], p.dtype) for p in pbs],
                          scratch_shapes=[pltpu.SemaphoreType.DMA((3 * na,)), pltpu.SemaphoreType.DMA((3 * na,))])(*pbs)


def add_chip_partials(pf, rb, jc_idx, *, name):
    n, H, C = pf.shape
    tr = _pick(H, 256)

    def body(jc_ref, p_ref, r_ref, o_ref):
        s = p_ref[0]
        for k in range(3):
            s = s + r_ref[k].astype(F32)
        o_ref[...] = s

    pblk = pl.BlockSpec((1, tr, C), lambda i, jc_ref: (jc_ref[0], i, 0))
    rblk = pl.BlockSpec((3, tr, C), lambda i, jc_ref: (0, i, 0))
    oblk = pl.BlockSpec((None, tr, C), lambda i, jc_ref: (jc_ref[1], i, 0))
    return pl.pallas_call(
        body, name=name,
        grid_spec=pltpu.PrefetchScalarGridSpec(num_scalar_prefetch=1, grid=(H // tr,), in_specs=[pblk, rblk],
                                               out_specs=oblk),
        out_shape=SDS((2, H, C), F32), compiler_params=_cp("parallel"))(jc_idx, pf, rb)


def join_sibling_halves(bufs, *, name):
    na = len(bufs)

    def body(*refs):
        o_refs = refs[na:2 * na]
        send_sems, recv_sems = refs[2 * na:]
        x, y, c = _coords()
        cps = [_remote(o_refs[a].at[c], o_refs[a].at[c], send_sems, recv_sems, a, (x, y, 1 - c)) for a in range(na)]
        for cp in cps:
            cp.start()
        for a in range(na):
            cps[a].wait_send()
            _remote(o_refs[a].at[1 - c], o_refs[a].at[1 - c], send_sems, recv_sems, a, (x, y, 1 - c)).wait_recv()

    return pl.pallas_call(body, name=name, in_specs=[ANY] * na, out_specs=[ANY] * na,
                          out_shape=[SDS(b.shape, b.dtype) for b in bufs],
                          input_output_aliases={a: a for a in range(na)},
                          scratch_shapes=[pltpu.SemaphoreType.DMA((na,)), pltpu.SemaphoreType.DMA((na,))])(*bufs)


def exchange_pieces(v, *, scatter, name):
    P, C = v.shape[-2:]

    def body(v_ref, o_ref, send_sems, recv_sems, local_sem):
        x, y, c = _coords()
        me = 4 * x + 2 * y + c
        local = pltpu.make_async_copy(v_ref.at[me] if scatter else v_ref, o_ref.at[me], local_sem)
        local.start()
        cps = []
        for m in range(1, N_DEV):
            px = (1 - x) if m & 4 else x
            py = (1 - y) if m & 2 else y
            pc = (1 - c) if m & 1 else c
            src = v_ref.at[4 * px + 2 * py + pc] if scatter else v_ref
            cps.append(_remote(src, o_ref.at[me], send_sems, recv_sems, m - 1, (px, py, pc)))
        for cp in cps:
            cp.start()
        for cp in cps:
            cp.wait_send()
        for m in range(1, N_DEV):
            px = (1 - x) if m & 4 else x
            py = (1 - y) if m & 2 else y
            pc = (1 - c) if m & 1 else c
            slot = o_ref.at[4 * px + 2 * py + pc]
            _remote(slot, slot, send_sems, recv_sems, m - 1, (px, py, pc)).wait_recv()
        local.wait()

    return pl.pallas_call(body, name=name, in_specs=[ANY], out_specs=ANY, out_shape=SDS((N_DEV, P, C), v.dtype),
                          scratch_shapes=[pltpu.SemaphoreType.DMA((N_DEV - 1,)), pltpu.SemaphoreType.DMA((N_DEV - 1,)),
                                          pltpu.SemaphoreType.DMA(())])(v)


def sum_pieces(land, *, name):
    n, P, C = land.shape

    def body(l_ref, o_ref):
        s = l_ref[0]
        for d in range(1, n):
            s = s + l_ref[d]
        o_ref[...] = s

    return pl.pallas_call(body, name=name, out_shape=SDS((P, C), F32))(land)


BIG_SEGS = (
    ("w_in_even", (1024, 514), 1),
    ("s5_w_glu", (128, 512), 0),
    ("w_out_even", (256, 1024), 0),
    ("w_in_odd", (1024, 384), 1),
    ("w_out_odd", (256, 1024), 0),
    ("mlp_w1", (2, 1024, 1024), 2),
    ("mlp_w2", (2, 1024, 1024), 1),
)
BIG_NAMES = tuple(n for n, _, _ in BIG_SEGS)
SHARDED_SMALL = ("pool_scale", "sgu_ln_g", "sgu_ln_b")
SMALL_SEGS = (
    ("mix_pre_g", (2, 1024)), ("mix_post_g", (2, 1024)), ("mlp_pre_g", (2, 1024)), ("mlp_post_g", (2, 1024)),
    ("s5_lam_re", (1, 32, 64)), ("s5_lam_im", (1, 32, 64)), ("s5_log_dt", (1, 32)),
    ("s5_b_re", (1, 32, 64, 16)), ("s5_b_im", (1, 32, 64, 16)), ("s5_c_re", (1, 32, 16, 64)), ("s5_c_im", (1, 32, 16, 64)),
    ("s5_d", (1, 512)), ("fox_b_f", (1, 8)), ("pool_w", (1, 4, 128, 128)), ("sgu_w_s", (1, 4, 128, 128)),
    ("sgu_b_s", (1, 4, 128)),
)
REDUCED_SEGS = SMALL_SEGS + tuple((n, (1, 512)) for n in SHARDED_SMALL)


def _cols_from_chips(g):
    n, R, C = g.shape
    return jnp.transpose(g, (1, 0, 2)).reshape(R, n * C)


def _chips_from_cols(m):
    R, C4 = m.shape
    return jnp.transpose(m.reshape(R, N_CHIPS, C4 // N_CHIPS), (1, 0, 2))


MLP_SHARD = 1024


def _w1_cols(l):
    def spec(tm, tn, tk):
        per = MLP_SHARD // tn
        return pl.BlockSpec((None, tk, tn), lambda i, j, k: (j // per, l * (MLP_SHARD // tk) + k, j % per))
    return spec


def _w1_rows_t(l):
    def spec(tm, tn, tk):
        per = MLP_SHARD // tk
        return pl.BlockSpec((None, tn, tk), lambda i, j, k: (k // per, l * (MLP_SHARD // tn) + j, k % per))
    return spec


def _w2_rows(l):
    def spec(tm, tn, tk):
        per = MLP_SHARD // tk
        return pl.BlockSpec((None, tk, tn), lambda i, j, k: (k // per, l * per + k % per, j))
    return spec


def _w2_rows_t(l):
    def spec(tm, tn, tk):
        per = MLP_SHARD // tn
        return pl.BlockSpec((None, tn, tk), lambda i, j, k: (j // per, l * per + j % per, k))
    return spec


def _dw1_out(l):
    def spec(tm, tn, tk):
        per = MLP_SHARD // tn
        return pl.BlockSpec((None, tm, tn), lambda i, j, k: (j // per, l * (MLP_SHARD // tm) + i, j % per))
    return spec


def _dw2_out(l):
    def spec(tm, tn, tk):
        per = MLP_SHARD // tm
        return pl.BlockSpec((None, tm, tn), lambda i, j, k: (i // per, l * per + i % per, j))
    return spec


def _pack_vec(d, segs, rows_multiple):
    flat = jnp.concatenate([d[n].reshape(-1) for n, _ in segs])
    rows = -(-flat.shape[0] // LANES)
    rows = -(-rows // rows_multiple) * rows_multiple
    return jnp.pad(flat, (0, rows * LANES - flat.shape[0])).reshape(rows, LANES)


def _unpack_vec(v, segs):
    flat, out, r = v.reshape(-1), {}, 0
    for n, shape in segs:
        k = math.prod(shape)
        out[n] = flat[r:r + k].reshape(shape)
        r += k
    return out


def _block_diag(blocks):
    G, a, b = blocks.shape
    eye = jnp.eye(G, dtype=blocks.dtype)
    return (eye[:, None, :, None] * blocks[:, :, None, :]).reshape(G * a, G * b)


def _diag_blocks(m, G):
    a, b = m.shape[0] // G, m.shape[1] // G
    return jnp.stack([m[g * a:(g + 1) * a, g * b:(g + 1) * b] for g in range(G)])


def _sqrelu_epi(acc):
    r = jnp.maximum(acc, 0.0)
    return acc, r * r


def _sqrelu_bwd_epi(acc, a):
    return (acc * (2.0 * jnp.maximum(a.astype(F32), 0.0)),)


def _mlp_fwd(x, g1, g2, l, g_pre, g_post, tag):
    T, D = x.shape
    h = rms_fwd(x, g_pre, name=f"{tag}_pre_norm")
    a, s = matmul(h, g1, name=f"{tag}_up", mnk=(T, D_FF, D), b_spec=_w1_cols(l), epi=_sqrelu_epi,
                  out_dtypes=(MXU_DTYPE, MXU_DTYPE))
    m = matmul(s, g2, name=f"{tag}_down", mnk=(T, D, D_FF), b_spec=_w2_rows(l))
    return rms_res_fwd(x, m, g_post, name=f"{tag}_post_norm"), (x, h, a, s, m)


def _mlp_bwd(saved, g1, g2, l, g_pre, g_post, dxo, dg1, dg2, tag):
    x, h, a, s, m = saved
    T, D = x.shape
    gshape = (N_CHIPS, 2 * MLP_SHARD, MLP_SHARD)
    dm, dg_post = rms_bwd(m, g_post, dxo, None, name=f"{tag}_post_norm_bwd")
    da = matmul(dm, g2, tb=True, name=f"{tag}_down_dx", mnk=(T, D_FF, D), b_spec=_w2_rows_t(l),
                epi=_sqrelu_bwd_epi, epi_in=(a,), out_dtype=MXU_DTYPE)
    dg2 = matmul(s, dm, ta=True, name=f"{tag}_down_dw", o_spec=_dw2_out(l), o_shape=gshape, prev=dg2)
    dh = matmul(da, g1, tb=True, name=f"{tag}_up_dx", mnk=(T, D, D_FF), b_spec=_w1_rows_t(l))
    dg1 = matmul(h, da, ta=True, name=f"{tag}_up_dw", o_spec=_dw1_out(l), o_shape=gshape, prev=dg1)
    dx, dg_pre = rms_bwd(x, g_pre, dh, dxo, name=f"{tag}_pre_norm_bwd")
    return dx, dg1, dg2, dg_pre, dg_post


def kernel(x, mix_pre_g, mix_post_g, mlp_pre_g, mlp_post_g, w_in_even, s5_lam_re, s5_lam_im, s5_log_dt, s5_b_re, s5_b_im, s5_c_re, s5_c_im, s5_d, s5_w_glu, fox_b_f, w_out_even, w_in_odd, pool_w, pool_scale, sgu_ln_g, sgu_ln_b, sgu_w_s, sgu_b_s, w_out_odd, mlp_w1, mlp_w2, loss_target, m_mix_pre_g, m_mix_post_g, m_mlp_pre_g, m_mlp_post_g, m_w_in_even, m_s5_lam_re, m_s5_lam_im, m_s5_log_dt, m_s5_b_re, m_s5_b_im, m_s5_c_re, m_s5_c_im, m_s5_d, m_s5_w_glu, m_fox_b_f, m_w_out_even, m_w_in_odd, m_pool_w, m_pool_scale, m_sgu_ln_g, m_sgu_ln_b, m_sgu_w_s, m_sgu_b_s, m_w_out_odd, m_mlp_w1, m_mlp_w2, v_mix_pre_g, v_mix_post_g, v_mlp_pre_g, v_mlp_post_g, v_w_in_even, v_s5_lam_re, v_s5_lam_im, v_s5_log_dt, v_s5_b_re, v_s5_b_im, v_s5_c_re, v_s5_c_im, v_s5_d, v_s5_w_glu, v_fox_b_f, v_w_out_even, v_w_in_odd, v_pool_w, v_pool_scale, v_sgu_ln_g, v_sgu_ln_b, v_sgu_w_s, v_sgu_b_s, v_w_out_odd, v_mlp_w1, v_mlp_w2):
    names = [n for n, _ in SMALL_SEGS] + [n for n, _, _ in BIG_SEGS] + list(SHARDED_SMALL)
    env = dict(locals())
    W = {n: env[n] for n in names}
    M = {n: env["m_" + n] for n in names}
    V = {n: env["v_" + n] for n in names}

    shards = [W[n].reshape(-1, W[n].shape[-1]).astype(WIRE_DTYPE) for n in BIG_NAMES]
    small = jnp.pad(jnp.concatenate([W[n] for n in SHARDED_SMALL]), ((0, SUBLANES - len(SHARDED_SMALL)), (0, 0)))
    gathered, small_all = allgather_chip_shards(shards, small, name="allgather_weights")
    Wf = dict(zip(BIG_NAMES, gathered))
    for i, n in enumerate(SHARDED_SMALL):
        Wf[n] = small_all[:, i, :].reshape(1, N_CHIPS * LANES)
    for n, _ in SMALL_SEGS:
        Wf[n] = W[n]

    loss8, dx0, full_grads, local_small = _local_step(x[0], loss_target[0], Wf)
    loss = lax.psum(loss8[0, 0], MESH_AXES)
    return _reduce_and_update(W, M, V, loss, dx0, full_grads, local_small)


def _local_step(x0, target, P):
    T = x0.shape[0]
    mix_pre_g, mix_post_g, mlp_pre_g, mlp_post_g = P["mix_pre_g"], P["mix_post_g"], P["mlp_pre_g"], P["mlp_post_g"]
    s5_lam_re, s5_lam_im, s5_log_dt = P["s5_lam_re"], P["s5_lam_im"], P["s5_log_dt"]
    s5_b_re, s5_b_im, s5_c_re, s5_c_im, s5_d = P["s5_b_re"], P["s5_b_im"], P["s5_c_re"], P["s5_c_im"], P["s5_d"]
    fox_b_f, pool_w, sgu_w_s, sgu_b_s = P["fox_b_f"], P["pool_w"], P["sgu_w_s"], P["sgu_b_s"]
    pool_scale_f, ln_g_f, ln_b_f = P["pool_scale"], P["sgu_ln_g"], P["sgu_ln_b"]
    w_in_e = jnp.pad(_cols_from_chips(P["w_in_even"]), ((0, 0), (0, EVEN_IN_PAD - EVEN_IN)))
    w_in_o = _cols_from_chips(P["w_in_odd"])
    w_in_o = jnp.concatenate([w_in_o[:, S5_WIDTH:], w_in_o[:, :S5_WIDTH]], axis=1)
    w_glu = P["s5_w_glu"].reshape(S5_WIDTH, S5_WIDTH)
    w_out_e = P["w_out_even"].reshape(D_MODEL, D_MODEL)
    w_out_o = P["w_out_odd"].reshape(D_MODEL, D_MODEL)
    g1, g2 = P["mlp_w1"], P["mlp_w2"]

    def gain(a, l):
        return a[l][None, :]

    lr = s5_lam_re[0].reshape(1, S5_LANES)
    li = s5_lam_im[0].reshape(1, S5_LANES)
    ldt = jnp.repeat(s5_log_dt[0], S5_STATE).reshape(1, S5_LANES)
    btr = s5_b_re[0].reshape(S5_LANES, S5_GROUP).T
    bti = s5_b_im[0].reshape(S5_LANES, S5_GROUP).T
    tf_re, tf_im, tb_re, tb_im, bbt_re, bbt_im = s5_disc_fwd(lr, li, ldt, btr, bti, name="s5_disc")
    same_group = (jnp.arange(S5_WIDTH)[:, None] // S5_GROUP) == (jnp.arange(S5_LANES)[None, :] // S5_STATE)
    b_bd = s5_interleave(jnp.where(same_group, jnp.tile(bbt_re, (S5_GROUPS, 1)), 0.0),
                         jnp.where(same_group, jnp.tile(bbt_im, (S5_GROUPS, 1)), 0.0), axis=1)
    cr2 = jnp.transpose(s5_c_re[0], (0, 2, 1)).reshape(S5_LANES, S5_GROUP)
    ci2 = jnp.transpose(s5_c_im[0], (0, 2, 1)).reshape(S5_LANES, S5_GROUP)
    c_bd = s5_interleave(jnp.where(same_group.T, jnp.tile(cr2, (1, S5_GROUPS)), 0.0),
                         -jnp.where(same_group.T, jnp.tile(ci2, (1, S5_GROUPS)), 0.0), axis=0)
    bf_pad = jnp.pad(fox_b_f, ((0, 0), (0, LANES - FOX_HEADS)))

    h1 = rms_fwd(x0, gain(mix_pre_g, 0), name="l0_pre_norm")
    z = matmul(h1, w_in_e, name="l0_in_proj")
    bu = matmul(z, b_bd, mnk=(T, 2 * S5_LANES, S5_WIDTH), name="s5_bu")
    xs = s5_scan(bu, tf_re, tf_im, reverse=False, name="s5_scan_fwd")
    yc = matmul(xs, c_bd, name="s5_cx")
    yl, yg = s5_out_fwd(yc, z, s5_d, name="s5_out")
    gl = matmul(yg, w_glu, name="s5_glu_proj")
    ycat = glu_fwd(yg, gl, out_cols=D_MODEL, name="s5_glu")
    fgate = fox_gate_fwd(z, bf_pad, fl_col=FL_TILE, name="fox_gate")
    f_col = _pairs_col(fgate, T)
    f_row = _col_to_row(f_col, T)
    ycat, lse_col = fox_fwd(z, f_col, f_row, ycat, name="fox_fwd")
    mo = matmul(ycat, w_out_e, name="l0_out_proj")
    x1 = rms_res_fwd(x0, mo, gain(mix_post_g, 0), name="l0_post_norm")
    x2, mlp0 = _mlp_fwd(x1, g1, g2, 0, gain(mlp_pre_g, 0), gain(mlp_post_g, 0), "mlp0")

    h3 = rms_fwd(x2, gain(mix_pre_g, 1), name="l1_pre_norm")
    z2 = matmul(h3, w_in_o, name="l1_in_proj")
    pooled = pool_window(z2, adjoint=False, in_col=POOL_COL, name="pool_fwd")
    pw_bd = _block_diag(pool_w[0])
    pw = matmul(pooled, pw_bd, name="pool_proj")
    ycat2 = colscale_fwd(pw, pool_scale_f, out_cols=D_MODEL, name="pool_scale")
    causal = jnp.tril(jnp.ones((CHUNK, CHUNK), dtype=bool))
    wsm = jnp.where(causal[None], sgu_w_s[0], 0.0)
    wsmt = jnp.transpose(wsm, (0, 2, 1))
    bst = sgu_b_s[0].T
    ycat2 = sgu_fwd(z2, ln_g_f, ln_b_f, wsm, bst, ycat2, name="sgu_fwd")
    mo2 = matmul(ycat2, w_out_o, name="l1_out_proj")
    x3 = rms_res_fwd(x2, mo2, gain(mix_post_g, 1), name="l1_post_norm")
    x4, mlp1 = _mlp_fwd(x3, g1, g2, 1, gain(mlp_pre_g, 1), gain(mlp_post_g, 1), "mlp1")

    loss8, dx4 = loss_fwd_bwd(x4, target, name="loss")

    dx3, dg1, dg2, dg_mlp_pre1, dg_mlp_post1 = _mlp_bwd(mlp1, g1, g2, 1, gain(mlp_pre_g, 1), gain(mlp_post_g, 1), dx4,
                                                        None, None, "mlp1")
    dmo2, dg_mix_post1 = rms_bwd(mo2, gain(mix_post_g, 1), dx3, None, name="l1_post_norm_bwd")
    dycat2 = matmul(dmo2, w_out_o, tb=True, name="l1_out_proj_dx")
    dw_out_o = matmul(ycat2, dmo2, ta=True, name="l1_out_proj_dw")
    dpw, dpool_scale = colscale_bwd(pw, pool_scale_f, dycat2, name="pool_scale_bwd")
    dpooled = matmul(dpw, pw_bd, tb=True, name="pool_proj_dx")
    dpw_bd = matmul(pooled, dpw, ta=True, name="pool_proj_dw")
    dz2, dln_g, dln_b, dws, dbst = sgu_bwd(z2, ln_g_f, ln_b_f, wsm, wsmt, bst, dycat2, out_cols=3 * S5_WIDTH,
                                           name="sgu_bwd")
    dz2 = pool_window(dpooled, adjoint=True, into=dz2, out_col=POOL_COL, name="pool_bwd")
    dh3 = matmul(dz2, w_in_o, tb=True, name="l1_in_proj_dx")
    dw_in_o = matmul(h3, dz2, ta=True, name="l1_in_proj_dw")
    dw_in_o = jnp.concatenate([dw_in_o[:, 2 * S5_WIDTH:], dw_in_o[:, :2 * S5_WIDTH]], axis=1)
    dx2, dg_mix_pre1 = rms_bwd(x2, gain(mix_pre_g, 1), dh3, dx3, name="l1_pre_norm_bwd")

    dx1, dg1, dg2, dg_mlp_pre0, dg_mlp_post0 = _mlp_bwd(mlp0, g1, g2, 0, gain(mlp_pre_g, 0), gain(mlp_post_g, 0), dx2,
                                                        dg1, dg2, "mlp0")
    dmo, dg_mix_post0 = rms_bwd(mo, gain(mix_post_g, 0), dx1, None, name="l0_post_norm_bwd")
    dycat = matmul(dmo, w_out_e, tb=True, name="l0_out_proj_dx")
    dw_out_e = matmul(ycat, dmo, ta=True, name="l0_out_proj_dw")
    dyg_a, dgl = glu_bwd(yg, gl, dycat, name="s5_glu_bwd")
    dyg_b = matmul(dgl, w_glu, tb=True, name="s5_glu_proj_dx")
    dw_glu = matmul(yg, dgl, ta=True, name="s5_glu_proj_dw")
    dyl, du_skip, dd = s5_out_bwd(yl, z, s5_d, dyg_a, dyg_b, name="s5_out_bwd")
    dxs = matmul(dyl, c_bd, tb=True, name="s5_cx_dx")
    dc_bd = matmul(xs, dyl, ta=True, name="s5_cx_dw")
    lam = s5_scan(dxs, tb_re, tb_im, reverse=True, name="s5_scan_bwd")
    dab_re, dab_im = s5_da(lam, xs, name="s5_da")
    db_bd = matmul(z, lam, ta=True, mnk=(S5_WIDTH, 2 * S5_LANES, T), name="s5_bu_dw")
    du_b = matmul(lam, b_bd, tb=True, name="s5_bu_dx")
    du = add2(du_skip, du_b, name="s5_du")
    dq, dd_col, dfq_col = fox_bwd_q(z, ycat, dycat, f_col, f_row, lse_col, name="fox_bwd_q")
    dk, dv, df_col = fox_bwd_kv(z, dycat, f_col, f_row, _col_to_row(lse_col, T), _col_to_row(dd_col, T), dfq_col,
                                name="fox_bwd_kv")
    dfl, dbf = fox_gate_bwd(z, bf_pad, _pairs_to_lanes(df_col, T), fl_col=FL_TILE, name="fox_gate_bwd")
    dz = jnp.concatenate([du, dq, dk, dv, dfl], axis=1)
    dh1 = matmul(dz, w_in_e, tb=True, name="l0_in_proj_dx")
    dw_in_e = matmul(h1, dz, ta=True, name="l0_in_proj_dw")[:, :EVEN_IN]
    dx0, dg_mix_pre0 = rms_bwd(x0, gain(mix_pre_g, 0), dh1, dx1, name="l0_pre_norm_bwd")

    db_re_bd, db_im_bd = s5_deinterleave(db_bd, axis=1)
    dbbt_re = jnp.where(same_group, db_re_bd, 0.0).reshape(S5_GROUPS, S5_GROUP, S5_LANES).sum(0)
    dbbt_im = jnp.where(same_group, db_im_bd, 0.0).reshape(S5_GROUPS, S5_GROUP, S5_LANES).sum(0)
    dlr, dli, dldt8, dbtr, dbti = s5_disc_bwd(lr, li, ldt, btr, bti, dab_re, dab_im, dbbt_re, dbbt_im, name="s5_disc_bwd")
    dc_re_bd, dc_im_bd = s5_deinterleave(dc_bd, axis=0)
    dcr2 = jnp.where(same_group.T, dc_re_bd, 0.0).reshape(S5_LANES, S5_GROUPS, S5_GROUP).sum(1)
    dci2 = -jnp.where(same_group.T, dc_im_bd, 0.0).reshape(S5_LANES, S5_GROUPS, S5_GROUP).sum(1)

    def c_layout(a):
        return jnp.transpose(a.reshape(S5_GROUPS, S5_STATE, S5_GROUP), (0, 2, 1))[None]

    def b_layout(a):
        return a.T.reshape(1, S5_GROUPS, S5_STATE, S5_GROUP)

    local_small = {
        "mix_pre_g": jnp.concatenate([dg_mix_pre0, dg_mix_pre1]), "mix_post_g": jnp.concatenate([dg_mix_post0, dg_mix_post1]),
        "mlp_pre_g": jnp.concatenate([dg_mlp_pre0, dg_mlp_pre1]), "mlp_post_g": jnp.concatenate([dg_mlp_post0, dg_mlp_post1]),
        "s5_lam_re": dlr.reshape(1, S5_GROUPS, S5_STATE), "s5_lam_im": dli.reshape(1, S5_GROUPS, S5_STATE),
        "s5_log_dt": dldt8[0:1, 0:S5_GROUPS],
        "s5_b_re": b_layout(dbtr), "s5_b_im": b_layout(dbti), "s5_c_re": c_layout(dcr2), "s5_c_im": c_layout(dci2),
        "s5_d": dd, "fox_b_f": dbf[:, 0:FOX_HEADS],
        "pool_w": _diag_blocks(dpw_bd, len(POOL_WINDOWS))[None],
        "sgu_w_s": jnp.where(causal[None], dws, 0.0)[None], "sgu_b_s": dbst.T[None],
        "pool_scale": dpool_scale, "sgu_ln_g": dln_g, "sgu_ln_b": dln_b,
    }
    full_grads = {"w_in_even": _chips_from_cols(dw_in_e), "s5_w_glu": dw_glu.reshape(N_CHIPS, -1, S5_WIDTH),
                  "w_out_even": dw_out_e.reshape(N_CHIPS, -1, D_MODEL), "w_in_odd": _chips_from_cols(dw_in_o),
                  "w_out_odd": dw_out_o.reshape(N_CHIPS, -1, D_MODEL), "mlp_w1": dg1, "mlp_w2": dg2}
    return loss8, dx0, full_grads, local_small


def _reduce_and_update(W, M, V, loss, dx0, full_grads, local_small):
    cx, cy, cc = _coords()
    chip = 2 * cx + cy

    vec = _pack_vec(local_small, REDUCED_SEGS, N_DEV * SUBLANES)
    piece = vec.shape[0] // N_DEV
    landed = exchange_pieces(vec.reshape(N_DEV, piece, LANES), scatter=True, name="small_grads_scatter")
    mine = sum_pieces(landed, name="small_grads_sum")
    everyone = exchange_pieces(mine, scatter=False, name="small_grads_gather")
    G = _unpack_vec(everyone, REDUCED_SEGS)
    for n in SHARDED_SMALL:
        G[n] = lax.dynamic_slice_in_dim(G[n], chip * LANES, LANES, axis=1)

    gs = [full_grads[n] for n in BIG_NAMES]
    c_idx = cc.reshape(1).astype(jnp.int32)
    jc_idx = jnp.stack([chip, cc]).astype(jnp.int32)
    from_sibling = swap_sibling_halves(gs, name="big_grads_to_sibling")
    sums = [add_sibling_half(g, l, c_idx, name=f"big_grads_chip_sum_{n}") for n, g, l in zip(BIG_NAMES, gs, from_sibling)]
    from_chips = send_chip_partials([pb for _, pb in sums], name="big_grads_to_chips")
    halves = [add_chip_partials(pf, r, jc_idx, name=f"big_grads_sum_{n}") for n, (pf, _), r in zip(BIG_NAMES, sums, from_chips)]
    reduced = join_sibling_halves(halves, name="big_grads_join")
    for n, r in zip(BIG_NAMES, reduced):
        G[n] = r.reshape(W[n].shape)

    def two_d(a):
        return a.reshape(-1, a.shape[-1])

    delta, new_m, new_v = {}, {}, {}
    for n in BIG_NAMES:
        d_, m_, v_ = adamw(two_d(W[n]), two_d(G[n]), two_d(M[n]), two_d(V[n]), name=f"adamw_{n}")
        delta[n], new_m[n], new_v[n] = (t.reshape(W[n].shape) for t in (d_, m_, v_))
    packed = [_pack_vec(src, SMALL_SEGS, SUBLANES) for src in (W, G, M, V)]
    outs = adamw(*packed, name="adamw_replicated")
    for dst, t in zip((delta, new_m, new_v), outs):
        dst.update(_unpack_vec(t, SMALL_SEGS))
    sharded_segs = tuple((n, (1, LANES)) for n in SHARDED_SMALL)
    packed = [_pack_vec(src, sharded_segs, 1) for src in (W, G, M, V)]
    outs = adamw(*packed, name="adamw_sharded_vectors")
    for dst, t in zip((delta, new_m, new_v), outs):
        dst.update(_unpack_vec(t, sharded_segs))

    order = ["mix_pre_g", "mix_post_g", "mlp_pre_g", "mlp_post_g", "w_in_even", "s5_lam_re", "s5_lam_im", "s5_log_dt",
             "s5_b_re", "s5_b_im", "s5_c_re", "s5_c_im", "s5_d", "s5_w_glu", "fox_b_f", "w_out_even", "w_in_odd",
             "pool_w", "pool_scale", "sgu_ln_g", "sgu_ln_b", "sgu_w_s", "sgu_b_s", "w_out_odd", "mlp_w1", "mlp_w2"]
    return (loss, dx0[None], *[G[n] for n in order], *[delta[n] for n in order],
            *[new_m[n] for n in order], *[new_v[n] for n in order])
```

```python
import functools
import math

import jax
import jax.numpy as jnp
from jax import lax
from jax.experimental import pallas as pl
from jax.experimental.pallas import tpu as pltpu

F32 = jnp.float32
MXU_DTYPE = jnp.bfloat16
WIRE_DTYPE = jnp.bfloat16
EPS = 1e-6
VMEM_LIMIT_BYTES = 48 * 1024 * 1024
LANES = 128
SUBLANES = 8

D_MODEL = 1024
S5_WIDTH = 512
S5_GROUP = 16
S5_GROUPS = 32
S5_STATE = 64
S5_LANES = S5_GROUPS * S5_STATE
FOX_HEADS = 8
FOX_HEAD_DIM = 64
FOX_WIDTH = 512
EVEN_IN = S5_WIDTH + 3 * FOX_WIDTH + FOX_HEADS
EVEN_IN_PAD = 2176
POOL_WINDOWS = (2, 4, 8, 16)
POOL_HALO = 16
POOL_GROUP_DIM = 128
SGU_GROUPS = 4
SGU_GROUP_DIM = 128
CHUNK = 128
D_FF = 4096

ADAM_LR = 0.001
ADAM_B1 = 0.9
ADAM_B2 = 0.999
ADAM_EPS = 1e-08
ADAM_WD = 0.01
ADAM_STEP = 10

MESH_AXES = ("x", "y", "c")
MESH = pl.DeviceIdType.MESH
N_CHIPS = 4
N_DEV = 8

SDS = jax.ShapeDtypeStruct


def _cp(*sem):
    return pltpu.CompilerParams(dimension_semantics=sem, vmem_limit_bytes=VMEM_LIMIT_BYTES)


def _pick(dim, pref):
    if dim <= pref:
        return dim
    t = pref
    while t >= 256:
        if dim % t == 0:
            return t
        t //= 2
    return dim


def _row(tr, c):
    return pl.BlockSpec((tr, c), lambda i: (i, 0))


def _full(shape):
    nd = len(shape)
    return pl.BlockSpec(shape, lambda *_: (0,) * nd)


def _gelu_grad(x):
    c = math.sqrt(2.0 / math.pi)
    t = jnp.tanh(c * (x + 0.044715 * x * x * x))
    return 0.5 * (1.0 + t) + 0.5 * x * (1.0 - t * t) * c * (1.0 + 3.0 * 0.044715 * x * x)


def matmul(a, b, *, name, ta=False, tb=False, out_dtype=F32, tm=1024, tn=512, tk=1024, mnk=None, a_koff=0,
           b_spec=None, o_spec=None, o_shape=None, prev=None, epi=None, epi_in=(), out_dtypes=None):
    if mnk is None:
        M, K = (a.shape[1], a.shape[0]) if ta else a.shape
        K2, N = (b.shape[1], b.shape[0]) if tb else b.shape
        assert K == K2, (a.shape, b.shape, ta, tb)
    else:
        M, N, K = mnk
    tm, tn, tk = _pick(M, tm), _pick(N, tn), _pick(K, tk)
    nk = K // tk
    assert a_koff % tk == 0 and not (ta and a_koff)
    ko = a_koff // tk
    dn = (((0 if ta else 1,), (1 if tb else 0,)), ((), ()))
    out_dtypes = tuple(out_dtypes) if out_dtypes is not None else (out_dtype,)
    n_out, n_epi = len(out_dtypes), len(epi_in)

    def body(*refs):
        a_ref, b_ref = refs[0], refs[1]
        epi_refs = refs[2:2 + n_epi]
        o_refs = refs[len(refs) - 1 - n_out:len(refs) - 1]
        acc_ref = refs[-1]
        k = pl.program_id(2)

        @pl.when(k == 0)
        def _():
            acc_ref[...] = jnp.zeros_like(acc_ref)

        acc_ref[...] += lax.dot_general(a_ref[...].astype(MXU_DTYPE), b_ref[...].astype(MXU_DTYPE), dn,
                                        preferred_element_type=F32)

        @pl.when(k == nk - 1)
        def _():
            res = (acc_ref[...],) if epi is None else epi(acc_ref[...], *[r[...] for r in epi_refs])
            for o_ref, r in zip(o_refs, res):
                o_ref[...] = r.astype(o_ref.dtype)

    a_spec = pl.BlockSpec((tk, tm), lambda i, j, k: (k, i)) if ta else pl.BlockSpec((tm, tk), lambda i, j, k: (i, k + ko))
    if b_spec is None:
        bs = pl.BlockSpec((tn, tk), lambda i, j, k: (j, k)) if tb else pl.BlockSpec((tk, tn), lambda i, j, k: (k, j))
    else:
        bs = b_spec(tm, tn, tk)
    tile = pl.BlockSpec((tm, tn), lambda i, j, k: (i, j))
    os_ = tile if o_spec is None else o_spec(tm, tn, tk)
    ins, in_specs, aliases = [a, b, *epi_in], [a_spec, bs] + [tile] * n_epi, {}
    if prev is not None:
        aliases = {len(ins): 0}
        ins.append(prev)
        in_specs.append(pl.BlockSpec(memory_space=pl.ANY))
    shapes = [SDS((M, N) if o_shape is None else o_shape, dt) for dt in out_dtypes]
    outs = pl.pallas_call(
        body, name=name, grid=(M // tm, N // tn, nk),
        in_specs=in_specs, out_specs=[os_] * n_out, out_shape=shapes, input_output_aliases=aliases,
        scratch_shapes=[pltpu.VMEM((tm, tn), F32)],
        compiler_params=_cp("parallel", "parallel", "arbitrary"),
    )(*ins)
    return outs[0] if n_out == 1 else outs


def _rms_hat(x):
    return x * lax.rsqrt(jnp.mean(x * x, axis=-1, keepdims=True) + EPS)


def rms_fwd(x, g, *, name):
    T, D = x.shape
    tr = _pick(T, 512)

    def body(x_ref, g_ref, o_ref):
        o_ref[...] = (_rms_hat(x_ref[...]) * g_ref[...]).astype(o_ref.dtype)

    return pl.pallas_call(body, name=name, grid=(T // tr,), in_specs=[_row(tr, D), _full((1, D))],
                          out_specs=_row(tr, D), out_shape=SDS((T, D), MXU_DTYPE), compiler_params=_cp("parallel"))(x, g)


def rms_res_fwd(x, y, g, *, name):
    T, D = x.shape
    tr = _pick(T, 512)

    def body(x_ref, y_ref, g_ref, o_ref):
        o_ref[...] = x_ref[...] + _rms_hat(y_ref[...]) * g_ref[...]

    return pl.pallas_call(body, name=name, grid=(T // tr,), in_specs=[_row(tr, D), _row(tr, D), _full((1, D))],
                          out_specs=_row(tr, D), out_shape=SDS((T, D), F32), compiler_params=_cp("parallel"))(x, y, g)


def rms_bwd(x, g, dy, res, *, name):
    T, D = x.shape
    tr = _pick(T, 512)
    has_res = res is not None

    def body(*refs):
        if has_res:
            x_ref, g_ref, dy_ref, res_ref, dx_ref, dg_ref = refs
        else:
            x_ref, g_ref, dy_ref, dx_ref, dg_ref = refs
        xv = x_ref[...]
        r = lax.rsqrt(jnp.mean(xv * xv, axis=-1, keepdims=True) + EPS)
        xh = xv * r
        dyv = dy_ref[...]
        dxh = dyv * g_ref[...]
        dx = r * (dxh - xh * jnp.mean(dxh * xh, axis=-1, keepdims=True))
        if has_res:
            dx = dx + res_ref[...]
        dx_ref[...] = dx

        @pl.when(pl.program_id(0) == 0)
        def _():
            dg_ref[...] = jnp.zeros_like(dg_ref)

        dg_ref[...] += jnp.sum(dyv * xh, axis=0, keepdims=True)

    ins = [x, g, dy] + ([res] if has_res else [])
    in_specs = [_row(tr, D), _full((1, D)), _row(tr, D)] + ([_row(tr, D)] if has_res else [])
    return pl.pallas_call(body, name=name, grid=(T // tr,), in_specs=in_specs,
                          out_specs=[_row(tr, D), _full((1, D))],
                          out_shape=[SDS((T, D), F32), SDS((1, D), F32)], compiler_params=_cp("arbitrary"))(*ins)


def loss_fwd_bwd(y, target, *, name):
    T, D = y.shape
    tr = _pick(T, 512)

    def body(y_ref, t_ref, l_ref, dy_ref):
        err = y_ref[...] - t_ref[...]
        dy_ref[...] = err * (1.0 / D)

        @pl.when(pl.program_id(0) == 0)
        def _():
            l_ref[...] = jnp.zeros_like(l_ref)

        l_ref[...] += 0.5 * jnp.sum(jnp.mean(err * err, axis=-1, keepdims=True))

    return pl.pallas_call(body, name=name, grid=(T // tr,), in_specs=[_row(tr, D), _row(tr, D)],
                          out_specs=[_full((SUBLANES, LANES)), _row(tr, D)],
                          out_shape=[SDS((SUBLANES, LANES), F32), SDS((T, D), F32)],
                          compiler_params=_cp("arbitrary"))(y, target)


def _s5_disc(lr, li, ldt, btr, bti):
    dt = jnp.exp(ldt)
    k = lax.broadcasted_iota(jnp.int32, (SUBLANES, S5_LANES), 0).astype(F32)
    kf = k + 1.0
    kb = 8.0 - k
    ph = li * dt
    lm = lr * dt
    tf_re = jnp.exp(kf * lm) * jnp.cos(kf * ph)
    tf_im = jnp.exp(kf * lm) * jnp.sin(kf * ph)
    tb_re = jnp.exp(kb * lm) * jnp.cos(kb * ph)
    tb_im = -jnp.exp(kb * lm) * jnp.sin(kb * ph)
    mag = jnp.exp(lm)
    ab_re = mag * jnp.cos(ph)
    ab_im = mag * jnp.sin(ph)
    den = lr * lr + li * li
    nr = ab_re - 1.0
    ni = ab_im
    q_re = (nr * lr + ni * li) / den
    q_im = (ni * lr - nr * li) / den
    bbt_re = q_re * btr - q_im * bti
    bbt_im = q_re * bti + q_im * btr
    return tf_re, tf_im, tb_re, tb_im, bbt_re, bbt_im


def _s5_disc_core(lr, li, ldt, btr, bti):
    dt = jnp.exp(ldt)
    mag = jnp.exp(lr * dt)
    ab_re = mag * jnp.cos(li * dt)
    ab_im = mag * jnp.sin(li * dt)
    den = lr * lr + li * li
    nr = ab_re - 1.0
    ni = ab_im
    q_re = (nr * lr + ni * li) / den
    q_im = (ni * lr - nr * li) / den
    return ab_re, ab_im, q_re * btr - q_im * bti, q_re * bti + q_im * btr


def s5_disc_fwd(lr, li, ldt, btr, bti, *, name):
    def body(lr_ref, li_ref, ldt_ref, btr_ref, bti_ref, *outs):
        vals = _s5_disc(lr_ref[...], li_ref[...], ldt_ref[...], btr_ref[...], bti_ref[...])
        for o, v in zip(outs, vals):
            o[...] = v

    tab = SDS((SUBLANES, S5_LANES), F32)
    bb = SDS((S5_GROUP, S5_LANES), F32)
    return pl.pallas_call(body, name=name, out_shape=[tab, tab, tab, tab, bb, bb])(lr, li, ldt, btr, bti)


def s5_disc_bwd(lr, li, ldt, btr, bti, dab_re, dab_im, dbbt_re, dbbt_im, *, name):
    def body(lr_ref, li_ref, ldt_ref, btr_ref, bti_ref, dar_ref, dai_ref, dbr_ref, dbi_ref,
             dlr_ref, dli_ref, dldt_ref, dbtr_ref, dbti_ref):
        _, vjp = jax.vjp(_s5_disc_core, lr_ref[...], li_ref[...], ldt_ref[...], btr_ref[...], bti_ref[...])
        dlr, dli, dldt, dbtr, dbti = vjp((dar_ref[...], dai_ref[...], dbr_ref[...], dbi_ref[...]))
        dlr_ref[...] = dlr
        dli_ref[...] = dli
        dbtr_ref[...] = dbtr
        dbti_ref[...] = dbti
        lane_group = lax.broadcasted_iota(jnp.int32, (S5_LANES, LANES), 0) // S5_STATE
        col = lax.broadcasted_iota(jnp.int32, (S5_LANES, LANES), 1)
        ind = (lane_group == col).astype(F32)
        dldt_ref[...] = jnp.dot(jnp.broadcast_to(dldt, (SUBLANES, S5_LANES)), ind,
                                precision=lax.Precision.HIGHEST, preferred_element_type=F32)

    row = SDS((1, S5_LANES), F32)
    bb = SDS((S5_GROUP, S5_LANES), F32)
    return pl.pallas_call(body, name=name, out_shape=[row, row, SDS((SUBLANES, LANES), F32), bb, bb])(
        lr, li, ldt, btr, bti, dab_re, dab_im, dbbt_re, dbbt_im)


S5_NB = 1024


def s5_interleave(re, im, axis):
    parts = []
    for n in range(S5_LANES // S5_NB):
        sl = [slice(None)] * re.ndim
        sl[axis] = slice(n * S5_NB, (n + 1) * S5_NB)
        parts += [re[tuple(sl)], im[tuple(sl)]]
    return jnp.concatenate(parts, axis=axis)


def s5_deinterleave(a, axis):
    re, im = [], []
    for n in range(S5_LANES // S5_NB):
        sl = [slice(None)] * a.ndim
        sl[axis] = slice(2 * n * S5_NB, (2 * n + 1) * S5_NB)
        re.append(a[tuple(sl)])
        sl[axis] = slice((2 * n + 1) * S5_NB, (2 * n + 2) * S5_NB)
        im.append(a[tuple(sl)])
    return jnp.concatenate(re, axis=axis), jnp.concatenate(im, axis=axis)


def s5_scan(bu, tab_re, tab_im, *, reverse, name):
    T = bu.shape[0]
    nb = S5_NB
    tc = _pick(T, 256)
    nl = S5_LANES // nb
    nt = T // tc
    ntile = tc // SUBLANES
    step_rows = ((1, 7), (2, 6), (4, 4)) if reverse else ((1, 0), (2, 1), (4, 3))

    def body(br_ref, bi_ref, tr_ref, ti_ref, xo_ref, cr_ref, ci_ref):
        @pl.when(pl.program_id(1) == 0)
        def _():
            cr_ref[...] = jnp.zeros_like(cr_ref)
            ci_ref[...] = jnp.zeros_like(ci_ref)

        tr = tr_ref[...]
        ti = ti_ref[...]
        io = lax.broadcasted_iota(jnp.int32, (SUBLANES, nb), 0)
        steps = [(d, tr_ref[r:r + 1, :], ti_ref[r:r + 1, :]) for d, r in step_rows]

        def tile(i, carry):
            cr, ci = carry
            j = (ntile - 1 - i) if reverse else i
            r0 = pl.multiple_of(j * SUBLANES, SUBLANES)
            xr = br_ref[pl.ds(r0, SUBLANES), :]
            xi = bi_ref[pl.ds(r0, SUBLANES), :]
            for d, pr, pi in steps:
                if reverse:
                    keep = io < SUBLANES - d
                    sh = SUBLANES - d
                else:
                    keep = io >= d
                    sh = d
                sr = jnp.where(keep, pltpu.roll(xr, sh, 0), 0.0)
                si = jnp.where(keep, pltpu.roll(xi, sh, 0), 0.0)
                xr, xi = xr + pr * sr - pi * si, xi + pr * si + pi * sr
            xr, xi = xr + tr * cr - ti * ci, xi + tr * ci + ti * cr
            xo_ref[pl.ds(r0, SUBLANES), 0:nb] = xr
            xo_ref[pl.ds(r0, SUBLANES), nb:2 * nb] = xi
            if reverse:
                return xr[0:1, :], xi[0:1, :]
            return xr[SUBLANES - 1:SUBLANES, :], xi[SUBLANES - 1:SUBLANES, :]

        cr, ci = lax.fori_loop(0, ntile, tile, (cr_ref[0:1, :], ci_ref[0:1, :]))
        cr_ref[0:1, :] = cr
        ci_ref[0:1, :] = ci

    def tmap(t):
        return (nt - 1 - t) if reverse else t

    re_spec = pl.BlockSpec((tc, nb), lambda n, t: (tmap(t), 2 * n))
    im_spec = pl.BlockSpec((tc, nb), lambda n, t: (tmap(t), 2 * n + 1))
    tab_spec = pl.BlockSpec((SUBLANES, nb), lambda n, t: (0, n))
    return pl.pallas_call(
        body, name=name, grid=(nl, nt), in_specs=[re_spec, im_spec, tab_spec, tab_spec],
        out_specs=pl.BlockSpec((tc, 2 * nb), lambda n, t: (tmap(t), n)), out_shape=SDS((T, 2 * S5_LANES), F32),
        scratch_shapes=[pltpu.VMEM((SUBLANES, nb), F32), pltpu.VMEM((SUBLANES, nb), F32)],
        compiler_params=_cp("parallel", "arbitrary"),
    )(bu, bu, tab_re, tab_im)


def s5_da(lam, xs, *, name):
    T = xs.shape[0]
    nb = S5_NB
    tc = _pick(T, 256)
    nl, nt = S5_LANES // nb, T // tc
    hb = tc // SUBLANES

    def body(lr_ref, li_ref, xr_ref, xi_ref, hr_ref, hi_ref, dar_ref, dai_ref):
        t = pl.program_id(1)

        @pl.when(t == 0)
        def _():
            dar_ref[...] = jnp.zeros_like(dar_ref)
            dai_ref[...] = jnp.zeros_like(dai_ref)

        io = lax.broadcasted_iota(jnp.int32, (tc, nb), 0)
        first = jnp.where(t > 0, 1.0, 0.0)
        pr = jnp.where(io >= 1, pltpu.roll(xr_ref[...], 1, 0), hr_ref[SUBLANES - 1:SUBLANES, :] * first)
        pi = jnp.where(io >= 1, pltpu.roll(xi_ref[...], 1, 0), hi_ref[SUBLANES - 1:SUBLANES, :] * first)
        lr = lr_ref[...]
        li = li_ref[...]
        dar_ref[...] += jnp.sum(lr * pr + li * pi, axis=0, keepdims=True)
        dai_ref[...] += jnp.sum(li * pr - lr * pi, axis=0, keepdims=True)

    re_blk = pl.BlockSpec((tc, nb), lambda n, t: (t, 2 * n))
    im_blk = pl.BlockSpec((tc, nb), lambda n, t: (t, 2 * n + 1))
    re_halo = pl.BlockSpec((SUBLANES, nb), lambda n, t: (jnp.maximum(t * hb - 1, 0), 2 * n))
    im_halo = pl.BlockSpec((SUBLANES, nb), lambda n, t: (jnp.maximum(t * hb - 1, 0), 2 * n + 1))
    acc = pl.BlockSpec((1, nb), lambda n, t: (0, n))
    row = SDS((1, S5_LANES), F32)
    return pl.pallas_call(body, name=name, grid=(nl, nt), in_specs=[re_blk, im_blk, re_blk, im_blk, re_halo, im_halo],
                          out_specs=[acc, acc], out_shape=[row, row],
                          compiler_params=_cp("parallel", "arbitrary"))(lam, lam, xs, xs, xs, xs)


def s5_out_fwd(yc, u, d, *, name):
    T, C = yc.shape
    tr = _pick(T, 512)

    def body(yc_ref, u_ref, d_ref, yl_ref, yg_ref):
        yl = yc_ref[...] + d_ref[...] * u_ref[...]
        yl_ref[...] = yl
        yg_ref[...] = jax.nn.gelu(yl)

    return pl.pallas_call(body, name=name, grid=(T // tr,), in_specs=[_row(tr, C), _row(tr, C), _full((1, C))],
                          out_specs=[_row(tr, C)] * 2, out_shape=[SDS((T, C), F32)] * 2,
                          compiler_params=_cp("parallel"))(yc, u, d)


def glu_fwd(yg, gl, *, out_cols, name):
    T, C = yg.shape
    tr = _pick(T, 512)

    def body(yg_ref, gl_ref, o_ref):
        o_ref[...] = yg_ref[...] * jax.nn.sigmoid(gl_ref[...])

    return pl.pallas_call(body, name=name, grid=(T // tr,), in_specs=[_row(tr, C)] * 2, out_specs=_row(tr, C),
                          out_shape=SDS((T, out_cols), F32), compiler_params=_cp("parallel"))(yg, gl)


def glu_bwd(yg, gl, dy, *, name):
    T, C = yg.shape
    tr = _pick(T, 512)

    def body(yg_ref, gl_ref, dy_ref, dyg_ref, dgl_ref):
        s = jax.nn.sigmoid(gl_ref[...])
        dyv = dy_ref[...]
        dyg_ref[...] = dyv * s
        dgl_ref[...] = dyv * yg_ref[...] * s * (1.0 - s)

    return pl.pallas_call(body, name=name, grid=(T // tr,), in_specs=[_row(tr, C)] * 3, out_specs=[_row(tr, C)] * 2,
                          out_shape=[SDS((T, C), F32)] * 2, compiler_params=_cp("parallel"))(yg, gl, dy)


def s5_out_bwd(yl, u, d, dyg_a, dyg_b, *, name):
    T, C = yl.shape
    tr = _pick(T, 512)

    def body(yl_ref, u_ref, d_ref, da_ref, db_ref, dyl_ref, du_ref, dd_ref):
        dyl = (da_ref[...] + db_ref[...]) * _gelu_grad(yl_ref[...])
        dyl_ref[...] = dyl
        du_ref[...] = dyl * d_ref[...]

        @pl.when(pl.program_id(0) == 0)
        def _():
            dd_ref[...] = jnp.zeros_like(dd_ref)

        dd_ref[...] += jnp.sum(dyl * u_ref[...], axis=0, keepdims=True)

    return pl.pallas_call(body, name=name, grid=(T // tr,),
                          in_specs=[_row(tr, C), _row(tr, C), _full((1, C)), _row(tr, C), _row(tr, C)],
                          out_specs=[_row(tr, C), _row(tr, C), _full((1, C))],
                          out_shape=[SDS((T, C), F32), SDS((T, C), F32), SDS((1, C), F32)],
                          compiler_params=_cp("arbitrary"))(yl, u, d, dyg_a, dyg_b)


def add2(a, b, *, name):
    T, C = a.shape
    tr = _pick(T, 512)

    def body(a_ref, b_ref, o_ref):
        o_ref[...] = a_ref[...] + b_ref[...]

    return pl.pallas_call(body, name=name, grid=(T // tr,), in_specs=[_row(tr, C)] * 2, out_specs=_row(tr, C),
                          out_shape=SDS((T, C), F32), compiler_params=_cp("parallel"))(a, b)


def _tri(n, upper):
    r = lax.broadcasted_iota(jnp.int32, (n, n), 0)
    c = lax.broadcasted_iota(jnp.int32, (n, n), 1)
    return ((c >= r) if upper else (c <= r)).astype(F32)


def fox_gate_fwd(fl, bf, *, fl_col, name):
    T = fl.shape[0]
    tb = _pick(T, 256)

    def body(fl_ref, bf_ref, f_ref, c_ref):
        @pl.when(pl.program_id(0) == 0)
        def _():
            c_ref[...] = jnp.zeros_like(c_ref)

        lf = jax.nn.log_sigmoid(fl_ref[...] + bf_ref[...])
        f = jnp.dot(_tri(tb, False), lf, precision=lax.Precision.HIGHEST, preferred_element_type=F32) + c_ref[0:1, :]
        f_ref[...] = f * LOG2E
        c_ref[0:1, :] = f[tb - 1:tb, :]

    fl_spec = pl.BlockSpec((tb, LANES), lambda i: (i, fl_col))
    return pl.pallas_call(body, name=name, grid=(T // tb,), in_specs=[fl_spec, _full((1, LANES))],
                          out_specs=_row(tb, LANES), out_shape=SDS((T, LANES), F32),
                          scratch_shapes=[pltpu.VMEM((SUBLANES, LANES), F32)], compiler_params=_cp("arbitrary"))(fl, bf)


def fox_gate_bwd(fl, bf, df, *, fl_col, name):
    T = fl.shape[0]
    tb = _pick(T, 256)
    nt = T // tb

    def body(fl_ref, bf_ref, df_ref, dfl_ref, dbf_ref, c_ref):
        @pl.when(pl.program_id(0) == 0)
        def _():
            c_ref[...] = jnp.zeros_like(c_ref)
            dbf_ref[...] = jnp.zeros_like(dbf_ref)

        dlf = jnp.dot(_tri(tb, True), df_ref[...], precision=lax.Precision.HIGHEST, preferred_element_type=F32) + c_ref[0:1, :]
        c_ref[0:1, :] = dlf[0:1, :]
        dfl = dlf * jax.nn.sigmoid(-(fl_ref[...] + bf_ref[...]))
        dfl_ref[...] = dfl
        dbf_ref[...] += jnp.sum(dfl, axis=0, keepdims=True)

    rev = pl.BlockSpec((tb, LANES), lambda i: (nt - 1 - i, 0))
    fl_rev = pl.BlockSpec((tb, LANES), lambda i: (nt - 1 - i, fl_col))
    return pl.pallas_call(body, name=name, grid=(nt,), in_specs=[fl_rev, _full((1, LANES)), rev],
                          out_specs=[rev, _full((1, LANES))], out_shape=[SDS((T, LANES), F32), SDS((1, LANES), F32)],
                          scratch_shapes=[pltpu.VMEM((SUBLANES, LANES), F32)], compiler_params=_cp("arbitrary"))(fl, bf, df)


FOX_BLOCK = 512
FOX_PAIRS = FOX_HEADS // 2
_NT = (((1,), (1,)), ((), ()))


LOG2E = 1.4426950408889634
FOX_FWD_UNROLL = 4
FOX_BWD_UNROLL = 2


def _fox_block(T):
    return _pick(T, FOX_BLOCK)


def _own_lanes(lane, hh):
    return (lane < FOX_HEAD_DIM) if hh == 0 else (lane >= FOX_HEAD_DIM)


def _grouped_steps(step, lo, n, unroll, init):
    def trip(t, c):
        for u in range(unroll):
            c = step(lo + t * unroll + u, c)
        return c

    carry = lax.fori_loop(0, n // unroll, trip, init)
    for u in range(unroll - 1):
        carry = lax.cond(n % unroll > u, lambda c: step(lo + (n // unroll) * unroll + u, c), lambda c: c, carry)
    return carry


Q_TILE0, K_TILE0, V_TILE0, O_TILE0 = 4, 8, 12, 4
FL_TILE = 16
POOL_COL = 2


def fox_fwd(z, f_col, f_row, ycat, hosted, *, name):
    T = z.shape[0]
    blk = _fox_block(T)
    nb = T // blk
    scale = FOX_HEAD_DIM ** -0.5

    def body(q_ref, k_ref, v_ref, fc_ref, fr_ref, prev_ref, o_ref, l_ref):
        i = pl.program_id(1)
        row = lax.broadcasted_iota(jnp.int32, (blk, blk), 0)
        col = lax.broadcasted_iota(jnp.int32, (blk, blk), 1)
        lane = lax.broadcasted_iota(jnp.int32, (blk, LANES), 1)
        qt = q_ref[...] * (scale * LOG2E)
        outs = []
        for hh in range(2):
            qh = jnp.where(_own_lanes(lane, hh), qt, 0.0).astype(MXU_DTYPE)
            fi = fc_ref[0, :, hh:hh + 1]

            def step(j, carry, masked=False):
                m, l, acc = carry
                r0 = pl.multiple_of(j * blk, blk)
                kj = k_ref[pl.ds(r0, blk), :].astype(MXU_DTYPE)
                vj = v_ref[pl.ds(r0, blk), :].astype(MXU_DTYPE)
                s = lax.dot_general(qh, kj, _NT, preferred_element_type=F32) + (fi - fr_ref[0, j, hh:hh + 1, :])
                if masked:
                    s = jnp.where(col <= row, s, -jnp.inf)
                m_new = jnp.maximum(m, jnp.max(s, axis=-1, keepdims=True))
                p = jnp.exp2(s - m_new)
                alpha = jnp.exp2(m - m_new)
                l = alpha * l + jnp.sum(p, axis=-1, keepdims=True)
                acc = alpha * acc + jnp.dot(p.astype(MXU_DTYPE), vj, preferred_element_type=F32)
                return m_new, l, acc

            init = (jnp.full((blk, 1), -jnp.inf, F32), jnp.zeros((blk, 1), F32), jnp.zeros((blk, LANES), F32))
            m, l, acc = step(i, _grouped_steps(step, 0, i, FOX_FWD_UNROLL, init), True)
            outs.append(acc / l)
            l_ref[0, :, hh:hh + 1] = m + jnp.log2(l)
        o_ref[...] = jnp.where(_own_lanes(lane, 0), outs[0], outs[1])

    qspec = pl.BlockSpec((blk, LANES), lambda h, i: (i, Q_TILE0 + h))
    kspec = pl.BlockSpec((T, LANES), lambda h, i: (0, K_TILE0 + h))
    vspec = pl.BlockSpec((T, LANES), lambda h, i: (0, V_TILE0 + h))
    ospec = pl.BlockSpec((blk, LANES), lambda h, i: (i, O_TILE0 + h))
    cspec = pl.BlockSpec((1, blk, 2), lambda h, i: (h, i, 0))
    rspec = pl.BlockSpec((1, nb, 2, blk), lambda h, i: (h, 0, 0, 0))
    return call_hosting(body, hosted, name=name, grid=(FOX_PAIRS, nb),
                        in_specs=[qspec, kspec, vspec, cspec, rspec, ANY], out_specs=[ospec, cspec],
                        out_shape=[SDS(ycat.shape, F32), SDS((FOX_PAIRS, T, 2), F32)],
                        inputs=[z, z, z, f_col, f_row, ycat], aliases={5: 0})


def fox_bwd_q(z, ycat, dycat, f_col, f_row, lse_col, hosted, *, name):
    T = z.shape[0]
    blk = _fox_block(T)
    nb = T // blk
    scale = FOX_HEAD_DIM ** -0.5

    def body(q_ref, k_ref, v_ref, o_ref, do_ref, fc_ref, fr_ref, lc_ref, dq_ref, dd_ref, df_ref):
        i = pl.program_id(1)
        row = lax.broadcasted_iota(jnp.int32, (blk, blk), 0)
        col = lax.broadcasted_iota(jnp.int32, (blk, blk), 1)
        lane = lax.broadcasted_iota(jnp.int32, (blk, LANES), 1)
        qt = q_ref[...] * (scale * LOG2E)
        dot = do_ref[...]
        prod = dot * o_ref[...]
        outs = []
        for hh in range(2):
            own = _own_lanes(lane, hh)
            qh = jnp.where(own, qt, 0.0).astype(MXU_DTYPE)
            dob = jnp.where(own, dot, 0.0).astype(MXU_DTYPE)
            dd = jnp.sum(jnp.where(own, prod, 0.0), axis=-1, keepdims=True)
            dd_ref[0, :, hh:hh + 1] = dd
            fi = fc_ref[0, :, hh:hh + 1]
            lse = lc_ref[0, :, hh:hh + 1]

            def step(j, carry, masked=False):
                dq, df = carry
                r0 = pl.multiple_of(j * blk, blk)
                kj = k_ref[pl.ds(r0, blk), :].astype(MXU_DTYPE)
                vj = v_ref[pl.ds(r0, blk), :].astype(MXU_DTYPE)
                s = lax.dot_general(qh, kj, _NT, preferred_element_type=F32) + (fi - fr_ref[0, j, hh:hh + 1, :])
                p = jnp.exp2(s - lse)
                if masked:
                    p = jnp.where(col <= row, p, 0.0)
                dp = lax.dot_general(dob, vj, _NT, preferred_element_type=F32)
                ds = p * (dp - dd)
                return (dq + jnp.dot(ds.astype(MXU_DTYPE), kj, preferred_element_type=F32),
                        df + jnp.sum(ds, axis=-1, keepdims=True))

            init = (jnp.zeros((blk, LANES), F32), jnp.zeros((blk, 1), F32))
            dq, df = step(i, _grouped_steps(step, 0, i, FOX_BWD_UNROLL, init), True)
            outs.append(dq * scale)
            df_ref[0, :, hh:hh + 1] = df
        dq_ref[...] = jnp.where(_own_lanes(lane, 0), outs[0], outs[1])

    qspec = pl.BlockSpec((blk, LANES), lambda h, i: (i, Q_TILE0 + h))
    kspec = pl.BlockSpec((T, LANES), lambda h, i: (0, K_TILE0 + h))
    vspec = pl.BlockSpec((T, LANES), lambda h, i: (0, V_TILE0 + h))
    ospec = pl.BlockSpec((blk, LANES), lambda h, i: (i, O_TILE0 + h))
    dqspec = pl.BlockSpec((blk, LANES), lambda h, i: (i, h))
    cspec = pl.BlockSpec((1, blk, 2), lambda h, i: (h, i, 0))
    rspec = pl.BlockSpec((1, nb, 2, blk), lambda h, i: (h, 0, 0, 0))
    stat = SDS((FOX_PAIRS, T, 2), F32)
    return call_hosting(body, hosted, name=name, grid=(FOX_PAIRS, nb),
                        in_specs=[qspec, kspec, vspec, ospec, ospec, cspec, rspec, cspec],
                        out_specs=[dqspec, cspec, cspec], out_shape=[SDS((T, FOX_WIDTH), F32), stat, stat],
                        inputs=[z, z, z, ycat, dycat, f_col, f_row, lse_col], aliases={})


def fox_bwd_kv(z, dycat, f_col, f_row, lse_row, dd_row, dfq_col, hosted, *, name):
    T = z.shape[0]
    blk = _fox_block(T)
    nb = T // blk
    scale = FOX_HEAD_DIM ** -0.5

    def body(q_ref, k_ref, v_ref, do_ref, fc_ref, fr_ref, lr_ref, dr_ref, dfq_ref, dk_ref, dv_ref, df_ref):
        j = pl.program_id(1)
        row = lax.broadcasted_iota(jnp.int32, (blk, blk), 0)
        col = lax.broadcasted_iota(jnp.int32, (blk, blk), 1)
        lane = lax.broadcasted_iota(jnp.int32, (blk, LANES), 1)
        kt = k_ref[...]
        vt = v_ref[...]
        dks, dvs = [], []
        for hh in range(2):
            own = _own_lanes(lane, hh)
            kh = jnp.where(own, kt, 0.0).astype(MXU_DTYPE)
            vh = jnp.where(own, vt, 0.0).astype(MXU_DTYPE)
            fj = fc_ref[0, :, hh:hh + 1]

            def step(i, carry, masked=False):
                dk, dv, df = carry
                r0 = pl.multiple_of(i * blk, blk)
                qi = (q_ref[pl.ds(r0, blk), :] * (scale * LOG2E)).astype(MXU_DTYPE)
                doi = do_ref[pl.ds(r0, blk), :].astype(MXU_DTYPE)
                st = lax.dot_general(kh, qi, _NT, preferred_element_type=F32) + (fr_ref[0, i, hh:hh + 1, :] - fj)
                pt = jnp.exp2(st - lr_ref[0, i, hh:hh + 1, :])
                if masked:
                    pt = jnp.where(col >= row, pt, 0.0)
                dv = dv + jnp.dot(pt.astype(MXU_DTYPE), doi, preferred_element_type=F32)
                dpt = lax.dot_general(vh, doi, _NT, preferred_element_type=F32)
                dst = pt * (dpt - dr_ref[0, i, hh:hh + 1, :])
                dk = dk + jnp.dot(dst.astype(MXU_DTYPE), qi, preferred_element_type=F32)
                df = df - jnp.sum(dst, axis=-1, keepdims=True)
                return dk, dv, df

            init = (jnp.zeros((blk, LANES), F32), jnp.zeros((blk, LANES), F32), dfq_ref[0, :, hh:hh + 1])
            dk, dv, df = _grouped_steps(step, j + 1, nb - 1 - j, FOX_BWD_UNROLL, step(j, init, True))
            dks.append(dk * (1.0 / LOG2E))
            dvs.append(dv)
            df_ref[0, :, hh:hh + 1] = df
        dk_ref[...] = jnp.where(_own_lanes(lane, 0), dks[0], dks[1])
        dv_ref[...] = jnp.where(_own_lanes(lane, 0), dvs[0], dvs[1])

    bspec = pl.BlockSpec((blk, LANES), lambda h, j: (j, h))
    qspec = pl.BlockSpec((T, LANES), lambda h, j: (0, Q_TILE0 + h))
    kspec = pl.BlockSpec((blk, LANES), lambda h, j: (j, K_TILE0 + h))
    vspec = pl.BlockSpec((blk, LANES), lambda h, j: (j, V_TILE0 + h))
    dospec = pl.BlockSpec((T, LANES), lambda h, j: (0, O_TILE0 + h))
    cspec = pl.BlockSpec((1, blk, 2), lambda h, j: (h, j, 0))
    rspec = pl.BlockSpec((1, nb, 2, blk), lambda h, j: (h, 0, 0, 0))
    return call_hosting(body, hosted, name=name, grid=(FOX_PAIRS, nb),
                        in_specs=[qspec, kspec, vspec, dospec, cspec, rspec, rspec, rspec, cspec],
                        out_specs=[bspec, bspec, cspec],
                        out_shape=[SDS((T, FOX_WIDTH), F32), SDS((T, FOX_WIDTH), F32), SDS((FOX_PAIRS, T, 2), F32)],
                        inputs=[z, z, z, dycat, f_col, f_row, lse_row, dd_row, dfq_col], aliases={})


def _pairs_col(a, T):
    return jnp.transpose(a[:, :FOX_HEADS].reshape(T, FOX_PAIRS, 2), (1, 0, 2))


def _col_to_row(a, T):
    blk = _fox_block(T)
    return jnp.transpose(a.reshape(FOX_PAIRS, T // blk, blk, 2), (0, 1, 3, 2))


def _pairs_to_lanes(a, T):
    flat = jnp.transpose(a, (1, 0, 2)).reshape(T, FOX_HEADS)
    return jnp.pad(flat, ((0, 0), (0, LANES - FOX_HEADS)))


def _pool_counts(t0, n, w):
    t = (t0 + lax.broadcasted_iota(jnp.int32, (n, 1), 0)).astype(F32)
    return jnp.minimum(t + 1.0, float(w))


def pool_window(x, *, adjoint, name, in_col=0, into=None, out_col=0):
    T, C = x.shape[0], len(POOL_WINDOWS) * POOL_GROUP_DIM
    tr = _pick(T, 512)
    nt = T // tr
    hb = tr // POOL_HALO
    n = tr + POOL_HALO

    def body(x_ref, h_ref, *rest):
        o_ref = rest[-1]
        i = pl.program_id(0)
        cur = x_ref[...]
        if adjoint:
            halo = h_ref[...] * jnp.where(i < nt - 1, 1.0, 0.0)
            ext = jnp.concatenate([cur, halo], axis=0)
            t0 = i * tr
        else:
            halo = h_ref[...] * jnp.where(i > 0, 1.0, 0.0)
            ext = jnp.concatenate([halo, cur], axis=0)
            t0 = i * tr - POOL_HALO
        sums = {}
        for g, w in enumerate(POOL_WINDOWS):
            ls = slice(g * POOL_GROUP_DIM, (g + 1) * POOL_GROUP_DIM)
            s = ext[:, ls]
            if adjoint:
                s = s / _pool_counts(t0, n, w)
            d = 1
            while d < w:
                s = s + pltpu.roll(s, (n - d) if adjoint else d, 0)
                d *= 2
            if adjoint:
                o_ref[:, ls] = s[0:tr, :] - cur[:, ls]
            else:
                o_ref[:, ls] = s[POOL_HALO:n, :] / _pool_counts(i * tr, tr, w) - cur[:, ls]

    if adjoint:
        halo_spec = pl.BlockSpec((POOL_HALO, C), lambda i: (jnp.minimum((i + 1) * hb, T // POOL_HALO - 1), in_col))
    else:
        halo_spec = pl.BlockSpec((POOL_HALO, C), lambda i: (jnp.maximum(i * hb - 1, 0), in_col))
    x_spec = pl.BlockSpec((tr, C), lambda i: (i, in_col))
    if into is None:
        return pl.pallas_call(body, name=name, grid=(nt,), in_specs=[x_spec, halo_spec], out_specs=_row(tr, C),
                              out_shape=SDS((T, C), F32), compiler_params=_cp("parallel"))(x, x)
    return pl.pallas_call(body, name=name, grid=(nt,), in_specs=[x_spec, halo_spec, ANY],
                          out_specs=pl.BlockSpec((tr, C), lambda i: (i, out_col)), out_shape=SDS(into.shape, F32),
                          input_output_aliases={2: 0}, compiler_params=_cp("parallel"))(x, x, into)


def colscale_fwd(a, s, *, out_cols, name):
    T, C = a.shape
    tr = _pick(T, 512)

    def body(a_ref, s_ref, o_ref):
        o_ref[...] = a_ref[...] * s_ref[...]

    return pl.pallas_call(body, name=name, grid=(T // tr,), in_specs=[_row(tr, C), _full((1, C))], out_specs=_row(tr, C),
                          out_shape=SDS((T, out_cols), F32), compiler_params=_cp("parallel"))(a, s)


def colscale_bwd(a, s, dy, *, name):
    T, C = a.shape
    tr = _pick(T, 512)

    def body(a_ref, s_ref, dy_ref, da_ref, ds_ref):
        dyv = dy_ref[...]
        da_ref[...] = dyv * s_ref[...]

        @pl.when(pl.program_id(0) == 0)
        def _():
            ds_ref[...] = jnp.zeros_like(ds_ref)

        ds_ref[...] += jnp.sum(dyv * a_ref[...], axis=0, keepdims=True)

    return pl.pallas_call(body, name=name, grid=(T // tr,), in_specs=[_row(tr, C), _full((1, C)), _row(tr, C)],
                          out_specs=[_row(tr, C), _full((1, C))], out_shape=[SDS((T, C), F32), SDS((1, C), F32)],
                          compiler_params=_cp("arbitrary"))(a, s, dy)


SGU_ROWS = 512


def _sgu_norm(v, ln_g, ln_b):
    vg = jax.nn.gelu(v)
    xc = vg - jnp.mean(vg, axis=-1, keepdims=True)
    r = lax.rsqrt(jnp.mean(xc * xc, axis=-1, keepdims=True) + EPS)
    xh = xc * r
    return xh * ln_g + ln_b, xh, r


def _rowc(tr, c, cb):
    return pl.BlockSpec((tr, c), lambda i: (i, cb))


def sgu_fwd(z, ln_g, ln_b, ws, bst, ycat, *, name):
    T, C = z.shape[0], SGU_GROUPS * SGU_GROUP_DIM
    tr = _pick(T, SGU_ROWS)

    def body(u_ref, v_ref, g_ref, b_ref, ws_ref, bst_ref, prev_ref, o_ref):
        vn, _, _ = _sgu_norm(v_ref[...], g_ref[...], b_ref[...])
        vn = vn.astype(MXU_DTYPE)
        ug = jax.nn.gelu(u_ref[...])
        for g in range(SGU_GROUPS):
            w = ws_ref[g].astype(MXU_DTYPE)
            bias = bst_ref[:, g:g + 1]
            for c in range(tr // CHUNK):
                rs = slice(c * CHUNK, (c + 1) * CHUNK)
                ls = slice(g * SGU_GROUP_DIM, (g + 1) * SGU_GROUP_DIM)
                mixed = jnp.dot(w, vn[rs, ls], preferred_element_type=F32) + bias
                o_ref[rs, ls] = ug[rs, ls] * mixed

    return pl.pallas_call(body, name=name, grid=(T // tr,),
                          in_specs=[_rowc(tr, C, 0), _rowc(tr, C, 1), _full((1, C)), _full((1, C)),
                                    _full((SGU_GROUPS, CHUNK, CHUNK)), _full((CHUNK, SGU_GROUPS)), ANY],
                          out_specs=_rowc(tr, C, 1), out_shape=SDS(ycat.shape, F32), input_output_aliases={6: 0},
                          compiler_params=_cp("parallel"))(z, z, ln_g, ln_b, ws, bst, ycat)


def sgu_bwd(z, ln_g, ln_b, ws, wst, bst, dycat, *, out_cols, name):
    T, C = z.shape[0], SGU_GROUPS * SGU_GROUP_DIM
    tr = _pick(T, SGU_ROWS)

    def body(u_ref, v_ref, g_ref, b_ref, ws_ref, wst_ref, bst_ref, dy_ref,
             duv_ref, dg_ref, db_ref, dws_ref, dbst_ref, dvn_ref):
        du_ref = duv_ref.at[:, 0:C]
        dv_ref = duv_ref.at[:, C:2 * C]
        @pl.when(pl.program_id(0) == 0)
        def _():
            dg_ref[...] = jnp.zeros_like(dg_ref)
            db_ref[...] = jnp.zeros_like(db_ref)
            dws_ref[...] = jnp.zeros_like(dws_ref)
            dbst_ref[...] = jnp.zeros_like(dbst_ref)

        uv = u_ref[...]
        vv = v_ref[...]
        vn, xh, r = _sgu_norm(vv, g_ref[...], b_ref[...])
        vn = vn.astype(MXU_DTYPE)
        ug = jax.nn.gelu(uv)
        dyv = dy_ref[...]
        for g in range(SGU_GROUPS):
            w = ws_ref[g].astype(MXU_DTYPE)
            wt = wst_ref[g].astype(MXU_DTYPE)
            bias = bst_ref[:, g:g + 1]
            dw = jnp.zeros((CHUNK, CHUNK), F32)
            dbias = jnp.zeros((CHUNK, 1), F32)
            for c in range(tr // CHUNK):
                rs = slice(c * CHUNK, (c + 1) * CHUNK)
                ls = slice(g * SGU_GROUP_DIM, (g + 1) * SGU_GROUP_DIM)
                vblk = vn[rs, ls]
                mixed = jnp.dot(w, vblk, preferred_element_type=F32) + bias
                dyb = dyv[rs, ls]
                du_ref[rs, ls] = dyb * mixed * _gelu_grad(uv[rs, ls])
                dmixed = dyb * ug[rs, ls]
                dbias = dbias + jnp.sum(dmixed, axis=-1, keepdims=True)
                dmb = dmixed.astype(MXU_DTYPE)
                dw = dw + lax.dot_general(dmb, vblk, _NT, preferred_element_type=F32)
                dvn_ref[rs, ls] = jnp.dot(wt, dmb, preferred_element_type=F32)
            dws_ref[g] += dw
            dbst_ref[:, g:g + 1] += dbias
        dvn = dvn_ref[...]
        dg_ref[...] += jnp.sum(dvn * xh, axis=0, keepdims=True)
        db_ref[...] += jnp.sum(dvn, axis=0, keepdims=True)
        dxh = dvn * g_ref[...]
        dvg = r * (dxh - jnp.mean(dxh, axis=-1, keepdims=True) - xh * jnp.mean(dxh * xh, axis=-1, keepdims=True))
        dv_ref[...] = dvg * _gelu_grad(vv)

    wspec = _full((SGU_GROUPS, CHUNK, CHUNK))
    return pl.pallas_call(body, name=name, grid=(T // tr,),
                          in_specs=[_rowc(tr, C, 0), _rowc(tr, C, 1), _full((1, C)), _full((1, C)), wspec, wspec,
                                    _full((CHUNK, SGU_GROUPS)), _rowc(tr, C, 1)],
                          out_specs=[_rowc(tr, 2 * C, 0), _full((1, C)), _full((1, C)), wspec,
                                     _full((CHUNK, SGU_GROUPS))],
                          out_shape=[SDS((T, out_cols), F32), SDS((1, C), F32), SDS((1, C), F32),
                                     SDS((SGU_GROUPS, CHUNK, CHUNK), F32), SDS((CHUNK, SGU_GROUPS), F32)],
                          scratch_shapes=[pltpu.VMEM((tr, C), F32)],
                          compiler_params=_cp("arbitrary"))(z, z, ln_g, ln_b, ws, wst, bst, dycat)


def adamw(w, g, m, v, *, name):
    R, C = w.shape
    tr = _pick(R, 512)
    c1 = 1.0 - ADAM_B1 ** ADAM_STEP
    c2 = 1.0 - ADAM_B2 ** ADAM_STEP

    def body(w_ref, g_ref, m_ref, v_ref, d_ref, nm_ref, nv_ref):
        gv = g_ref[...]
        nm = ADAM_B1 * m_ref[...] + (1.0 - ADAM_B1) * gv
        nv = ADAM_B2 * v_ref[...] + (1.0 - ADAM_B2) * (gv * gv)
        nm_ref[...] = nm
        nv_ref[...] = nv
        d_ref[...] = -ADAM_LR * ((nm / c1) / (jnp.sqrt(nv / c2) + ADAM_EPS) + ADAM_WD * w_ref[...])

    spec = _row(tr, C)
    return pl.pallas_call(body, name=name, grid=(R // tr,), in_specs=[spec] * 4, out_specs=[spec] * 3,
                          out_shape=[SDS((R, C), F32)] * 3, compiler_params=_cp("parallel"))(w, g, m, v)


ANY = pl.BlockSpec(memory_space=pl.ANY)


def _coords():
    return lax.axis_index("x"), lax.axis_index("y"), lax.axis_index("c")


def _other_chips(x, y):
    return [(1 - x, y), (x, 1 - y), (1 - x, 1 - y)]


def _remote(src, dst, send_sems, recv_sems, k, dev):
    return pltpu.make_async_remote_copy(src_ref=src, dst_ref=dst, send_sem=send_sems.at[k], recv_sem=recv_sems.at[k],
                                        device_id=dev, device_id_type=MESH)


LOCAL_CHUNKS = 8


def allgather_chip_shards(shards, small, *, name):
    na = len(shards)

    def body(*refs):
        s_refs, sm_ref = refs[:na], refs[na]
        o_refs, smo_ref = refs[na + 1:2 * na + 1], refs[2 * na + 1]
        send_sems, recv_sems, local_sems = refs[2 * na + 2:]
        x, y, c = _coords()
        j = 2 * x + y
        sibling = (x, y, 1 - c)
        chips = _other_chips(x, y)
        for a in range(na):
            chunk = shards[a].shape[0] // LOCAL_CHUNKS
            for q in range(LOCAL_CHUNKS):
                rows = pl.ds(q * chunk, chunk)
                pltpu.make_async_copy(s_refs[a].at[rows], o_refs[a].at[j, rows], local_sems.at[a]).start()
        pltpu.make_async_copy(sm_ref, smo_ref.at[j], local_sems.at[na]).start()
        sends = []
        for a in range(na):
            half = shards[a].shape[0] // 2
            mine = pl.ds(c * half, half)
            for k, (px, py) in enumerate(chips):
                sends.append(_remote(s_refs[a].at[mine], o_refs[a].at[j, mine], send_sems, recv_sems, 6 * a + k, (px, py, c)))
        for k, (px, py) in enumerate(chips):
            sends.append(_remote(sm_ref, smo_ref.at[j], send_sems, recv_sems, 6 * na + k, (px, py, c)))
        for cp in sends:
            cp.start()
        for a in range(na):
            half = shards[a].shape[0] // 2
            mine = pl.ds(c * half, half)
            for k, (px, py) in enumerate(chips):
                rows = o_refs[a].at[2 * px + py, mine]
                _remote(rows, rows, send_sems, recv_sems, 6 * a + k, (px, py, c)).wait_recv()
                fw = _remote(rows, rows, send_sems, recv_sems, 6 * a + 3 + k, sibling)
                fw.start()
                sends.append(fw)
        for a in range(na):
            half = shards[a].shape[0] // 2
            theirs = pl.ds((1 - c) * half, half)
            for k, (px, py) in enumerate(chips):
                rows = o_refs[a].at[2 * px + py, theirs]
                _remote(rows, rows, send_sems, recv_sems, 6 * a + 3 + k, sibling).wait_recv()
        for k, (px, py) in enumerate(chips):
            slot = smo_ref.at[2 * px + py]
            _remote(slot, slot, send_sems, recv_sems, 6 * na + k, (px, py, c)).wait_recv()
        for cp in sends:
            cp.wait_send()
        for a in range(na):
            pltpu.make_async_copy(s_refs[a], o_refs[a].at[j], local_sems.at[a]).wait()
        pltpu.make_async_copy(sm_ref, smo_ref.at[j], local_sems.at[na]).wait()

    nsem = 6 * na + 3
    outs = pl.pallas_call(
        body, name=name, in_specs=[ANY] * (na + 1), out_specs=[ANY] * (na + 1),
        out_shape=[SDS((N_CHIPS,) + s.shape, s.dtype) for s in shards] + [SDS((N_CHIPS,) + small.shape, small.dtype)],
        scratch_shapes=[pltpu.SemaphoreType.DMA((nsem,)), pltpu.SemaphoreType.DMA((nsem,)),
                        pltpu.SemaphoreType.DMA((na + 1,))])(*shards, small)
    return outs[:na], outs[na]


class Exchange:
    def __init__(self, ins, out_shapes, scratch, start, wait):
        self.ins, self.out_shapes, self.scratch, self.start, self.wait = list(ins), list(out_shapes), list(scratch), start, wait


def run_exchange(ex, *, name):
    ni, no = len(ex.ins), len(ex.out_shapes)

    def body(*refs):
        parts = refs[:ni], refs[ni:ni + no], refs[ni + no:]
        ex.start(*parts)
        ex.wait(*parts)

    return pl.pallas_call(body, name=name, in_specs=[ANY] * ni, out_specs=[ANY] * no, out_shape=ex.out_shapes,
                          scratch_shapes=ex.scratch)(*ex.ins)


def call_hosting(body, ex, *, name, grid, in_specs, out_specs, out_shape, inputs, aliases):
    n_in, n_out, ni, no = len(inputs), len(out_shape), len(ex.ins), len(ex.out_shapes)

    def wrapped(*refs):
        own = refs[:n_in] + refs[n_in + ni:n_in + ni + n_out]
        parts = refs[n_in:n_in + ni], refs[n_in + ni + n_out:n_in + ni + n_out + no], refs[n_in + ni + n_out + no:]
        ids = [pl.program_id(d) for d in range(len(grid))]
        first = functools.reduce(jnp.logical_and, [i == 0 for i in ids])
        last = functools.reduce(jnp.logical_and, [i == g - 1 for i, g in zip(ids, grid)])

        @pl.when(first)
        def _():
            ex.start(*parts)

        body(*own)

        @pl.when(last)
        def _():
            ex.wait(*parts)

    outs = pl.pallas_call(
        wrapped, name=name, grid=grid, in_specs=list(in_specs) + [ANY] * ni, out_specs=list(out_specs) + [ANY] * no,
        out_shape=list(out_shape) + ex.out_shapes, input_output_aliases=aliases, scratch_shapes=ex.scratch,
        compiler_params=_cp(*["arbitrary"] * len(grid)))(*inputs, *ex.ins)
    return outs[:n_out], outs[n_out:]


def allgather_ici_exchange(shards):
    na = len(shards)

    def copies(s_refs, o_refs, sems):
        send_sems, recv_sems, _ = sems
        x, y, c = _coords()
        j = 2 * x + y
        out = []
        for a in range(na):
            half = shards[a].shape[0] // 2
            mine = pl.ds(c * half, half)
            for k, (px, py) in enumerate(_other_chips(x, y)):
                send = _remote(s_refs[a].at[mine], o_refs[a].at[j, mine], send_sems, recv_sems, 3 * a + k, (px, py, c))
                rows = o_refs[a].at[2 * px + py, mine]
                out.append((send, _remote(rows, rows, send_sems, recv_sems, 3 * a + k, (px, py, c))))
        return out

    def start(s_refs, o_refs, sems):
        x, y, c = _coords()
        j = 2 * x + y
        for a in range(na):
            chunk = shards[a].shape[0] // LOCAL_CHUNKS
            for q in range(LOCAL_CHUNKS):
                rows = pl.ds(q * chunk, chunk)
                pltpu.make_async_copy(s_refs[a].at[rows], o_refs[a].at[j, rows], sems[2].at[a]).start()
        for send, _ in copies(s_refs, o_refs, sems):
            send.start()

    def wait(s_refs, o_refs, sems):
        x, y, c = _coords()
        j = 2 * x + y
        for send, arrival in copies(s_refs, o_refs, sems):
            arrival.wait_recv()
            send.wait_send()
        for a in range(na):
            pltpu.make_async_copy(s_refs[a], o_refs[a].at[j], sems[2].at[a]).wait()

    return Exchange(shards, [SDS((N_CHIPS,) + s.shape, s.dtype) for s in shards],
                    [pltpu.SemaphoreType.DMA((3 * na,)), pltpu.SemaphoreType.DMA((3 * na,)), pltpu.SemaphoreType.DMA((na,))],
                    start, wait)


def allgather_forward(gathered, *, name):
    na = len(gathered)

    def body(*refs):
        o_refs = refs[na:2 * na]
        send_sems, recv_sems = refs[2 * na:]
        x, y, c = _coords()
        sibling = (x, y, 1 - c)
        cps = []
        for a in range(na):
            half = gathered[a].shape[1] // 2
            for k, (px, py) in enumerate(_other_chips(x, y)):
                mine = o_refs[a].at[2 * px + py, pl.ds(c * half, half)]
                theirs = o_refs[a].at[2 * px + py, pl.ds((1 - c) * half, half)]
                cps.append((_remote(mine, mine, send_sems, recv_sems, 3 * a + k, sibling),
                            _remote(theirs, theirs, send_sems, recv_sems, 3 * a + k, sibling)))
        for send, _ in cps:
            send.start()
        for send, arrival in cps:
            send.wait_send()
            arrival.wait_recv()

    return pl.pallas_call(body, name=name, in_specs=[ANY] * na, out_specs=[ANY] * na,
                          out_shape=[SDS(g.shape, g.dtype) for g in gathered],
                          input_output_aliases={a: a for a in range(na)},
                          scratch_shapes=[pltpu.SemaphoreType.DMA((3 * na,)), pltpu.SemaphoreType.DMA((3 * na,))])(*gathered)


def swap_halves_exchange(gs):
    na = len(gs)

    def copies(g_refs, o_refs, sems):
        x, y, c = _coords()
        out = []
        for a in range(na):
            half = gs[a].shape[1] // 2
            out.append(_remote(g_refs[a].at[:, pl.ds((1 - c) * half, half), :], o_refs[a], sems[0], sems[1], a,
                               (x, y, 1 - c)))
        return out

    def start(g_refs, o_refs, sems):
        for cp in copies(g_refs, o_refs, sems):
            cp.start()

    def wait(g_refs, o_refs, sems):
        for cp in copies(g_refs, o_refs, sems):
            cp.wait()

    return Exchange(gs, [SDS((g.shape[0], g.shape[1] // 2, g.shape[2]), g.dtype) for g in gs],
                    [pltpu.SemaphoreType.DMA((na,)), pltpu.SemaphoreType.DMA((na,))], start, wait)


def chip_partials_exchange(pbs):
    na = len(pbs)

    def copies(p_refs, o_refs, sems):
        x, y, c = _coords()
        out = []
        for a in range(na):
            for k, (px, py) in enumerate(_other_chips(x, y)):
                out.append(_remote(p_refs[a].at[2 * px + py], o_refs[a].at[k], sems[0], sems[1], 3 * a + k, (px, py, c)))
        return out

    def start(p_refs, o_refs, sems):
        for cp in copies(p_refs, o_refs, sems):
            cp.start()

    def wait(p_refs, o_refs, sems):
        for cp in copies(p_refs, o_refs, sems):
            cp.wait()

    return Exchange(pbs, [SDS((3,) + p.shape[1:], p.dtype) for p in pbs],
                    [pltpu.SemaphoreType.DMA((3 * na,)), pltpu.SemaphoreType.DMA((3 * na,))], start, wait)


def add_sibling_half(g, land, c_idx, *, name):
    n, R, C = g.shape
    half = R // 2
    tr = _pick(half, 256)
    nt = half // tr

    def body(c_ref, g_ref, l_ref, of_ref, ob_ref):
        s = g_ref[...] + l_ref[...].astype(F32)
        of_ref[...] = s
        ob_ref[...] = s.astype(ob_ref.dtype)

    blk = pl.BlockSpec((1, tr, C), lambda s, i, c_ref: (s, i, 0))
    gblk = pl.BlockSpec((1, tr, C), lambda s, i, c_ref: (s, c_ref[0] * nt + i, 0))
    return pl.pallas_call(
        body, name=name,
        grid_spec=pltpu.PrefetchScalarGridSpec(num_scalar_prefetch=1, grid=(n, nt), in_specs=[gblk, blk],
                                               out_specs=[blk, blk]),
        out_shape=[SDS((n, half, C), F32), SDS((n, half, C), WIRE_DTYPE)],
        compiler_params=_cp("parallel", "parallel"))(c_idx, g, land)


def add_chip_partials(pf, rb, jc_idx, *, name):
    n, H, C = pf.shape
    tr = _pick(H, 256)

    def body(jc_ref, p_ref, r_ref, o_ref):
        s = p_ref[0]
        for k in range(3):
            s = s + r_ref[k].astype(F32)
        o_ref[...] = s

    pblk = pl.BlockSpec((1, tr, C), lambda i, jc_ref: (jc_ref[0], i, 0))
    rblk = pl.BlockSpec((3, tr, C), lambda i, jc_ref: (0, i, 0))
    oblk = pl.BlockSpec((None, tr, C), lambda i, jc_ref: (jc_ref[1], i, 0))
    return pl.pallas_call(
        body, name=name,
        grid_spec=pltpu.PrefetchScalarGridSpec(num_scalar_prefetch=1, grid=(H // tr,), in_specs=[pblk, rblk],
                                               out_specs=oblk),
        out_shape=SDS((2, H, C), F32), compiler_params=_cp("parallel"))(jc_idx, pf, rb)


def join_sibling_halves(bufs, *, name):
    na = len(bufs)

    def body(*refs):
        o_refs = refs[na:2 * na]
        send_sems, recv_sems = refs[2 * na:]
        x, y, c = _coords()
        cps = [_remote(o_refs[a].at[c], o_refs[a].at[c], send_sems, recv_sems, a, (x, y, 1 - c)) for a in range(na)]
        for cp in cps:
            cp.start()
        for a in range(na):
            cps[a].wait_send()
            _remote(o_refs[a].at[1 - c], o_refs[a].at[1 - c], send_sems, recv_sems, a, (x, y, 1 - c)).wait_recv()

    return pl.pallas_call(body, name=name, in_specs=[ANY] * na, out_specs=[ANY] * na,
                          out_shape=[SDS(b.shape, b.dtype) for b in bufs],
                          input_output_aliases={a: a for a in range(na)},
                          scratch_shapes=[pltpu.SemaphoreType.DMA((na,)), pltpu.SemaphoreType.DMA((na,))])(*bufs)


def exchange_pieces(v, *, scatter, name):
    P, C = v.shape[-2:]

    def body(v_ref, o_ref, send_sems, recv_sems, local_sem):
        x, y, c = _coords()
        me = 4 * x + 2 * y + c
        local = pltpu.make_async_copy(v_ref.at[me] if scatter else v_ref, o_ref.at[me], local_sem)
        local.start()
        cps = []
        for m in range(1, N_DEV):
            px = (1 - x) if m & 4 else x
            py = (1 - y) if m & 2 else y
            pc = (1 - c) if m & 1 else c
            src = v_ref.at[4 * px + 2 * py + pc] if scatter else v_ref
            cps.append(_remote(src, o_ref.at[me], send_sems, recv_sems, m - 1, (px, py, pc)))
        for cp in cps:
            cp.start()
        for cp in cps:
            cp.wait_send()
        for m in range(1, N_DEV):
            px = (1 - x) if m & 4 else x
            py = (1 - y) if m & 2 else y
            pc = (1 - c) if m & 1 else c
            slot = o_ref.at[4 * px + 2 * py + pc]
            _remote(slot, slot, send_sems, recv_sems, m - 1, (px, py, pc)).wait_recv()
        local.wait()

    return pl.pallas_call(body, name=name, in_specs=[ANY], out_specs=ANY, out_shape=SDS((N_DEV, P, C), v.dtype),
                          scratch_shapes=[pltpu.SemaphoreType.DMA((N_DEV - 1,)), pltpu.SemaphoreType.DMA((N_DEV - 1,)),
                                          pltpu.SemaphoreType.DMA(())])(v)


def sum_pieces(land, *, name):
    n, P, C = land.shape

    def body(l_ref, o_ref):
        s = l_ref[0]
        for d in range(1, n):
            s = s + l_ref[d]
        o_ref[...] = s

    return pl.pallas_call(body, name=name, out_shape=SDS((P, C), F32))(land)


BIG_SEGS = (
    ("w_in_even", (1024, 514), 1),
    ("s5_w_glu", (128, 512), 0),
    ("w_out_even", (256, 1024), 0),
    ("w_in_odd", (1024, 384), 1),
    ("w_out_odd", (256, 1024), 0),
    ("mlp_w1", (2, 1024, 1024), 2),
    ("mlp_w2", (2, 1024, 1024), 1),
)
BIG_NAMES = tuple(n for n, _, _ in BIG_SEGS)
EARLY_NAMES = ("w_in_even", "s5_w_glu", "w_out_even")
LATE_NAMES = ("w_in_odd", "w_out_odd", "mlp_w1", "mlp_w2")
REDUCED_EARLY = ("s5_w_glu", "w_out_even", "w_in_odd", "w_out_odd", "mlp_w1", "mlp_w2")
SHARDED_SMALL = ("pool_scale", "sgu_ln_g", "sgu_ln_b")
SMALL_SEGS = (
    ("mix_pre_g", (2, 1024)), ("mix_post_g", (2, 1024)), ("mlp_pre_g", (2, 1024)), ("mlp_post_g", (2, 1024)),
    ("s5_lam_re", (1, 32, 64)), ("s5_lam_im", (1, 32, 64)), ("s5_log_dt", (1, 32)),
    ("s5_b_re", (1, 32, 64, 16)), ("s5_b_im", (1, 32, 64, 16)), ("s5_c_re", (1, 32, 16, 64)), ("s5_c_im", (1, 32, 16, 64)),
    ("s5_d", (1, 512)), ("fox_b_f", (1, 8)), ("pool_w", (1, 4, 128, 128)), ("sgu_w_s", (1, 4, 128, 128)),
    ("sgu_b_s", (1, 4, 128)),
)
REDUCED_SEGS = SMALL_SEGS + tuple((n, (1, 512)) for n in SHARDED_SMALL)


def _cols_from_chips(g):
    n, R, C = g.shape
    return jnp.transpose(g, (1, 0, 2)).reshape(R, n * C)


def _chips_from_cols(m):
    R, C4 = m.shape
    return jnp.transpose(m.reshape(R, N_CHIPS, C4 // N_CHIPS), (1, 0, 2))


MLP_SHARD = 1024


def _w1_cols(l):
    def spec(tm, tn, tk):
        per = MLP_SHARD // tn
        return pl.BlockSpec((None, tk, tn), lambda i, j, k: (j // per, l * (MLP_SHARD // tk) + k, j % per))
    return spec


def _w1_rows_t(l):
    def spec(tm, tn, tk):
        per = MLP_SHARD // tk
        return pl.BlockSpec((None, tn, tk), lambda i, j, k: (k // per, l * (MLP_SHARD // tn) + j, k % per))
    return spec


def _w2_rows(l):
    def spec(tm, tn, tk):
        per = MLP_SHARD // tk
        return pl.BlockSpec((None, tk, tn), lambda i, j, k: (k // per, l * per + k % per, j))
    return spec


def _w2_rows_t(l):
    def spec(tm, tn, tk):
        per = MLP_SHARD // tn
        return pl.BlockSpec((None, tn, tk), lambda i, j, k: (j // per, l * per + j % per, k))
    return spec


def _dw1_out(l):
    def spec(tm, tn, tk):
        per = MLP_SHARD // tn
        return pl.BlockSpec((None, tm, tn), lambda i, j, k: (j // per, l * (MLP_SHARD // tm) + i, j % per))
    return spec


def _dw2_out(l):
    def spec(tm, tn, tk):
        per = MLP_SHARD // tm
        return pl.BlockSpec((None, tm, tn), lambda i, j, k: (i // per, l * per + i % per, j))
    return spec


def _pack_vec(d, segs, rows_multiple):
    flat = jnp.concatenate([d[n].reshape(-1) for n, _ in segs])
    rows = -(-flat.shape[0] // LANES)
    rows = -(-rows // rows_multiple) * rows_multiple
    return jnp.pad(flat, (0, rows * LANES - flat.shape[0])).reshape(rows, LANES)


def _unpack_vec(v, segs):
    flat, out, r = v.reshape(-1), {}, 0
    for n, shape in segs:
        k = math.prod(shape)
        out[n] = flat[r:r + k].reshape(shape)
        r += k
    return out


def _block_diag(blocks):
    G, a, b = blocks.shape
    eye = jnp.eye(G, dtype=blocks.dtype)
    return (eye[:, None, :, None] * blocks[:, :, None, :]).reshape(G * a, G * b)


def _diag_blocks(m, G):
    a, b = m.shape[0] // G, m.shape[1] // G
    return jnp.stack([m[g * a:(g + 1) * a, g * b:(g + 1) * b] for g in range(G)])


def _sqrelu_epi(acc):
    r = jnp.maximum(acc, 0.0)
    return acc, r * r


def _sqrelu_bwd_epi(acc, a):
    return (acc * (2.0 * jnp.maximum(a.astype(F32), 0.0)),)


def _mlp_fwd(x, g1, g2, l, g_pre, g_post, tag):
    T, D = x.shape
    h = rms_fwd(x, g_pre, name=f"{tag}_pre_norm")
    a, s = matmul(h, g1, name=f"{tag}_up", mnk=(T, D_FF, D), b_spec=_w1_cols(l), epi=_sqrelu_epi,
                  out_dtypes=(MXU_DTYPE, MXU_DTYPE))
    m = matmul(s, g2, name=f"{tag}_down", mnk=(T, D, D_FF), b_spec=_w2_rows(l))
    return rms_res_fwd(x, m, g_post, name=f"{tag}_post_norm"), (x, h, a, s, m)


def _mlp_bwd(saved, g1, g2, l, g_pre, g_post, dxo, dg1, dg2, tag):
    x, h, a, s, m = saved
    T, D = x.shape
    gshape = (N_CHIPS, 2 * MLP_SHARD, MLP_SHARD)
    dm, dg_post = rms_bwd(m, g_post, dxo, None, name=f"{tag}_post_norm_bwd")
    da = matmul(dm, g2, tb=True, name=f"{tag}_down_dx", mnk=(T, D_FF, D), b_spec=_w2_rows_t(l),
                epi=_sqrelu_bwd_epi, epi_in=(a,), out_dtype=MXU_DTYPE)
    dg2 = matmul(s, dm, ta=True, name=f"{tag}_down_dw", o_spec=_dw2_out(l), o_shape=gshape, prev=dg2)
    dh = matmul(da, g1, tb=True, name=f"{tag}_up_dx", mnk=(T, D, D_FF), b_spec=_w1_rows_t(l))
    dg1 = matmul(h, da, ta=True, name=f"{tag}_up_dw", o_spec=_dw1_out(l), o_shape=gshape, prev=dg1)
    dx, dg_pre = rms_bwd(x, g_pre, dh, dxo, name=f"{tag}_pre_norm_bwd")
    return dx, dg1, dg2, dg_pre, dg_post


def kernel(x, mix_pre_g, mix_post_g, mlp_pre_g, mlp_post_g, w_in_even, s5_lam_re, s5_lam_im, s5_log_dt, s5_b_re, s5_b_im, s5_c_re, s5_c_im, s5_d, s5_w_glu, fox_b_f, w_out_even, w_in_odd, pool_w, pool_scale, sgu_ln_g, sgu_ln_b, sgu_w_s, sgu_b_s, w_out_odd, mlp_w1, mlp_w2, loss_target, m_mix_pre_g, m_mix_post_g, m_mlp_pre_g, m_mlp_post_g, m_w_in_even, m_s5_lam_re, m_s5_lam_im, m_s5_log_dt, m_s5_b_re, m_s5_b_im, m_s5_c_re, m_s5_c_im, m_s5_d, m_s5_w_glu, m_fox_b_f, m_w_out_even, m_w_in_odd, m_pool_w, m_pool_scale, m_sgu_ln_g, m_sgu_ln_b, m_sgu_w_s, m_sgu_b_s, m_w_out_odd, m_mlp_w1, m_mlp_w2, v_mix_pre_g, v_mix_post_g, v_mlp_pre_g, v_mlp_post_g, v_w_in_even, v_s5_lam_re, v_s5_lam_im, v_s5_log_dt, v_s5_b_re, v_s5_b_im, v_s5_c_re, v_s5_c_im, v_s5_d, v_s5_w_glu, v_fox_b_f, v_w_out_even, v_w_in_odd, v_pool_w, v_pool_scale, v_sgu_ln_g, v_sgu_ln_b, v_sgu_w_s, v_sgu_b_s, v_w_out_odd, v_mlp_w1, v_mlp_w2):
    names = [n for n, _ in SMALL_SEGS] + [n for n, _, _ in BIG_SEGS] + list(SHARDED_SMALL)
    env = dict(locals())
    W = {n: env[n] for n in names}
    M = {n: env["m_" + n] for n in names}
    V = {n: env["v_" + n] for n in names}

    def shard(n):
        return W[n].reshape(-1, W[n].shape[-1]).astype(WIRE_DTYPE)

    small = jnp.pad(jnp.concatenate([W[n] for n in SHARDED_SMALL]), ((0, SUBLANES - len(SHARDED_SMALL)), (0, 0)))
    gathered, small_all = allgather_chip_shards([shard(n) for n in EARLY_NAMES], small, name="allgather_weights")
    Wf = dict(zip(EARLY_NAMES, gathered))
    for i, n in enumerate(SHARDED_SMALL):
        Wf[n] = small_all[:, i, :].reshape(1, N_CHIPS * LANES)
    for n, _ in SMALL_SEGS:
        Wf[n] = W[n]

    loss8, dx0, halves, dw_in_e, local_small = _local_step(x[0], loss_target[0], Wf, [shard(n) for n in LATE_NAMES])
    loss = lax.psum(loss8[0, 0], MESH_AXES)
    return _reduce_and_update(W, M, V, loss, dx0, halves, dw_in_e, local_small)


def _reduce_to_my_half(gs, names, tag, carry_swap=None, carry_ici=None):
    cx, cy, cc = _coords()
    c_idx = cc.reshape(1).astype(jnp.int32)
    jc_idx = jnp.stack([2 * cx + cy, cc]).astype(jnp.int32)
    swap = swap_halves_exchange(gs)
    from_sibling = carry_swap(swap) if carry_swap else run_exchange(swap, name=f"{tag}_to_sibling")
    sums = [add_sibling_half(g, l, c_idx, name=f"{tag}_chip_sum_{n}") for n, g, l in zip(names, gs, from_sibling)]
    send = chip_partials_exchange([pb for _, pb in sums])
    from_chips = carry_ici(send) if carry_ici else run_exchange(send, name=f"{tag}_to_chips")
    return [add_chip_partials(pf, r, jc_idx, name=f"{tag}_sum_{n}") for n, (pf, _), r in zip(names, sums, from_chips)]


def _local_step(x0, target, P, late_shards):
    T = x0.shape[0]
    mix_pre_g, mix_post_g, mlp_pre_g, mlp_post_g = P["mix_pre_g"], P["mix_post_g"], P["mlp_pre_g"], P["mlp_post_g"]
    s5_lam_re, s5_lam_im, s5_log_dt = P["s5_lam_re"], P["s5_lam_im"], P["s5_log_dt"]
    s5_b_re, s5_b_im, s5_c_re, s5_c_im, s5_d = P["s5_b_re"], P["s5_b_im"], P["s5_c_re"], P["s5_c_im"], P["s5_d"]
    fox_b_f, pool_w, sgu_w_s, sgu_b_s = P["fox_b_f"], P["pool_w"], P["sgu_w_s"], P["sgu_b_s"]
    pool_scale_f, ln_g_f, ln_b_f = P["pool_scale"], P["sgu_ln_g"], P["sgu_ln_b"]
    w_in_e = jnp.pad(_cols_from_chips(P["w_in_even"]), ((0, 0), (0, EVEN_IN_PAD - EVEN_IN)))
    w_glu = P["s5_w_glu"].reshape(S5_WIDTH, S5_WIDTH)
    w_out_e = P["w_out_even"].reshape(D_MODEL, D_MODEL)

    def gain(a, l):
        return a[l][None, :]

    lr = s5_lam_re[0].reshape(1, S5_LANES)
    li = s5_lam_im[0].reshape(1, S5_LANES)
    ldt = jnp.repeat(s5_log_dt[0], S5_STATE).reshape(1, S5_LANES)
    btr = s5_b_re[0].reshape(S5_LANES, S5_GROUP).T
    bti = s5_b_im[0].reshape(S5_LANES, S5_GROUP).T
    tf_re, tf_im, tb_re, tb_im, bbt_re, bbt_im = s5_disc_fwd(lr, li, ldt, btr, bti, name="s5_disc")
    same_group = (jnp.arange(S5_WIDTH)[:, None] // S5_GROUP) == (jnp.arange(S5_LANES)[None, :] // S5_STATE)
    b_bd = s5_interleave(jnp.where(same_group, jnp.tile(bbt_re, (S5_GROUPS, 1)), 0.0),
                         jnp.where(same_group, jnp.tile(bbt_im, (S5_GROUPS, 1)), 0.0), axis=1)
    cr2 = jnp.transpose(s5_c_re[0], (0, 2, 1)).reshape(S5_LANES, S5_GROUP)
    ci2 = jnp.transpose(s5_c_im[0], (0, 2, 1)).reshape(S5_LANES, S5_GROUP)
    c_bd = s5_interleave(jnp.where(same_group.T, jnp.tile(cr2, (1, S5_GROUPS)), 0.0),
                         -jnp.where(same_group.T, jnp.tile(ci2, (1, S5_GROUPS)), 0.0), axis=0)
    bf_pad = jnp.pad(fox_b_f, ((0, 0), (0, LANES - FOX_HEADS)))

    h1 = rms_fwd(x0, gain(mix_pre_g, 0), name="l0_pre_norm")
    z = matmul(h1, w_in_e, name="l0_in_proj")
    bu = matmul(z, b_bd, mnk=(T, 2 * S5_LANES, S5_WIDTH), name="s5_bu")
    xs = s5_scan(bu, tf_re, tf_im, reverse=False, name="s5_scan_fwd")
    yc = matmul(xs, c_bd, name="s5_cx")
    yl, yg = s5_out_fwd(yc, z, s5_d, name="s5_out")
    gl = matmul(yg, w_glu, name="s5_glu_proj")
    ycat = glu_fwd(yg, gl, out_cols=D_MODEL, name="s5_glu")
    fgate = fox_gate_fwd(z, bf_pad, fl_col=FL_TILE, name="fox_gate")
    f_col = _pairs_col(fgate, T)
    f_row = _col_to_row(f_col, T)
    (ycat, lse_col), late = fox_fwd(z, f_col, f_row, ycat, allgather_ici_exchange(late_shards), name="fox_fwd")
    late = dict(zip(LATE_NAMES, allgather_forward(late, name="allgather_late_weights")))
    w_in_o = _cols_from_chips(late["w_in_odd"])
    w_in_o = jnp.concatenate([w_in_o[:, S5_WIDTH:], w_in_o[:, :S5_WIDTH]], axis=1)
    w_out_o = late["w_out_odd"].reshape(D_MODEL, D_MODEL)
    g1, g2 = late["mlp_w1"], late["mlp_w2"]
    mo = matmul(ycat, w_out_e, name="l0_out_proj")
    x1 = rms_res_fwd(x0, mo, gain(mix_post_g, 0), name="l0_post_norm")
    x2, mlp0 = _mlp_fwd(x1, g1, g2, 0, gain(mlp_pre_g, 0), gain(mlp_post_g, 0), "mlp0")

    h3 = rms_fwd(x2, gain(mix_pre_g, 1), name="l1_pre_norm")
    z2 = matmul(h3, w_in_o, name="l1_in_proj")
    pooled = pool_window(z2, adjoint=False, in_col=POOL_COL, name="pool_fwd")
    pw_bd = _block_diag(pool_w[0])
    pw = matmul(pooled, pw_bd, name="pool_proj")
    ycat2 = colscale_fwd(pw, pool_scale_f, out_cols=D_MODEL, name="pool_scale")
    causal = jnp.tril(jnp.ones((CHUNK, CHUNK), dtype=bool))
    wsm = jnp.where(causal[None], sgu_w_s[0], 0.0)
    wsmt = jnp.transpose(wsm, (0, 2, 1))
    bst = sgu_b_s[0].T
    ycat2 = sgu_fwd(z2, ln_g_f, ln_b_f, wsm, bst, ycat2, name="sgu_fwd")
    mo2 = matmul(ycat2, w_out_o, name="l1_out_proj")
    x3 = rms_res_fwd(x2, mo2, gain(mix_post_g, 1), name="l1_post_norm")
    x4, mlp1 = _mlp_fwd(x3, g1, g2, 1, gain(mlp_pre_g, 1), gain(mlp_post_g, 1), "mlp1")

    loss8, dx4 = loss_fwd_bwd(x4, target, name="loss")

    dx3, dg1, dg2, dg_mlp_pre1, dg_mlp_post1 = _mlp_bwd(mlp1, g1, g2, 1, gain(mlp_pre_g, 1), gain(mlp_post_g, 1), dx4,
                                                        None, None, "mlp1")
    dmo2, dg_mix_post1 = rms_bwd(mo2, gain(mix_post_g, 1), dx3, None, name="l1_post_norm_bwd")
    dycat2 = matmul(dmo2, w_out_o, tb=True, name="l1_out_proj_dx")
    dw_out_o = matmul(ycat2, dmo2, ta=True, name="l1_out_proj_dw")
    dpw, dpool_scale = colscale_bwd(pw, pool_scale_f, dycat2, name="pool_scale_bwd")
    dpooled = matmul(dpw, pw_bd, tb=True, name="pool_proj_dx")
    dpw_bd = matmul(pooled, dpw, ta=True, name="pool_proj_dw")
    dz2, dln_g, dln_b, dws, dbst = sgu_bwd(z2, ln_g_f, ln_b_f, wsm, wsmt, bst, dycat2, out_cols=3 * S5_WIDTH,
                                           name="sgu_bwd")
    dz2 = pool_window(dpooled, adjoint=True, into=dz2, out_col=POOL_COL, name="pool_bwd")
    dh3 = matmul(dz2, w_in_o, tb=True, name="l1_in_proj_dx")
    dw_in_o = matmul(h3, dz2, ta=True, name="l1_in_proj_dw")
    dw_in_o = jnp.concatenate([dw_in_o[:, 2 * S5_WIDTH:], dw_in_o[:, :2 * S5_WIDTH]], axis=1)
    dx2, dg_mix_pre1 = rms_bwd(x2, gain(mix_pre_g, 1), dh3, dx3, name="l1_pre_norm_bwd")

    dx1, dg1, dg2, dg_mlp_pre0, dg_mlp_post0 = _mlp_bwd(mlp0, g1, g2, 0, gain(mlp_pre_g, 0), gain(mlp_post_g, 0), dx2,
                                                        dg1, dg2, "mlp0")
    dmo, dg_mix_post0 = rms_bwd(mo, gain(mix_post_g, 0), dx1, None, name="l0_post_norm_bwd")
    dycat = matmul(dmo, w_out_e, tb=True, name="l0_out_proj_dx")
    dw_out_e = matmul(ycat, dmo, ta=True, name="l0_out_proj_dw")
    dyg_a, dgl = glu_bwd(yg, gl, dycat, name="s5_glu_bwd")
    dyg_b = matmul(dgl, w_glu, tb=True, name="s5_glu_proj_dx")
    dw_glu = matmul(yg, dgl, ta=True, name="s5_glu_proj_dw")
    dyl, du_skip, dd = s5_out_bwd(yl, z, s5_d, dyg_a, dyg_b, name="s5_out_bwd")
    dxs = matmul(dyl, c_bd, tb=True, name="s5_cx_dx")
    dc_bd = matmul(xs, dyl, ta=True, name="s5_cx_dw")
    lam = s5_scan(dxs, tb_re, tb_im, reverse=True, name="s5_scan_bwd")
    dab_re, dab_im = s5_da(lam, xs, name="s5_da")
    db_bd = matmul(z, lam, ta=True, mnk=(S5_WIDTH, 2 * S5_LANES, T), name="s5_bu_dw")
    du_b = matmul(lam, b_bd, tb=True, name="s5_bu_dx")
    du = add2(du_skip, du_b, name="s5_du")
    early_grads = {"s5_w_glu": dw_glu.reshape(N_CHIPS, -1, S5_WIDTH), "w_out_even": dw_out_e.reshape(N_CHIPS, -1, D_MODEL),
                   "w_in_odd": _chips_from_cols(dw_in_o), "w_out_odd": dw_out_o.reshape(N_CHIPS, -1, D_MODEL),
                   "mlp_w1": dg1, "mlp_w2": dg2}
    fox = {}

    def attention_bwd_q(exchange):
        (fox["dq"], fox["dd"], fox["dfq"]), bufs = fox_bwd_q(z, ycat, dycat, f_col, f_row, lse_col, exchange, name="fox_bwd_q")
        return bufs

    def attention_bwd_kv(exchange):
        (fox["dk"], fox["dv"], fox["df"]), bufs = fox_bwd_kv(z, dycat, f_col, f_row, _col_to_row(lse_col, T),
                                                             _col_to_row(fox["dd"], T), fox["dfq"], exchange, name="fox_bwd_kv")
        return bufs

    halves = _reduce_to_my_half([early_grads[n] for n in REDUCED_EARLY], REDUCED_EARLY, "early_grads",
                                attention_bwd_q, attention_bwd_kv)
    dq, dk, dv = fox["dq"], fox["dk"], fox["dv"]
    dfl, dbf = fox_gate_bwd(z, bf_pad, _pairs_to_lanes(fox["df"], T), fl_col=FL_TILE, name="fox_gate_bwd")
    dz = jnp.concatenate([du, dq, dk, dv, dfl], axis=1)
    dh1 = matmul(dz, w_in_e, tb=True, name="l0_in_proj_dx")
    dw_in_e = matmul(h1, dz, ta=True, name="l0_in_proj_dw")[:, :EVEN_IN]
    dx0, dg_mix_pre0 = rms_bwd(x0, gain(mix_pre_g, 0), dh1, dx1, name="l0_pre_norm_bwd")

    db_re_bd, db_im_bd = s5_deinterleave(db_bd, axis=1)
    dbbt_re = jnp.where(same_group, db_re_bd, 0.0).reshape(S5_GROUPS, S5_GROUP, S5_LANES).sum(0)
    dbbt_im = jnp.where(same_group, db_im_bd, 0.0).reshape(S5_GROUPS, S5_GROUP, S5_LANES).sum(0)
    dlr, dli, dldt8, dbtr, dbti = s5_disc_bwd(lr, li, ldt, btr, bti, dab_re, dab_im, dbbt_re, dbbt_im, name="s5_disc_bwd")
    dc_re_bd, dc_im_bd = s5_deinterleave(dc_bd, axis=0)
    dcr2 = jnp.where(same_group.T, dc_re_bd, 0.0).reshape(S5_LANES, S5_GROUPS, S5_GROUP).sum(1)
    dci2 = -jnp.where(same_group.T, dc_im_bd, 0.0).reshape(S5_LANES, S5_GROUPS, S5_GROUP).sum(1)

    def c_layout(a):
        return jnp.transpose(a.reshape(S5_GROUPS, S5_STATE, S5_GROUP), (0, 2, 1))[None]

    def b_layout(a):
        return a.T.reshape(1, S5_GROUPS, S5_STATE, S5_GROUP)

    local_small = {
        "mix_pre_g": jnp.concatenate([dg_mix_pre0, dg_mix_pre1]), "mix_post_g": jnp.concatenate([dg_mix_post0, dg_mix_post1]),
        "mlp_pre_g": jnp.concatenate([dg_mlp_pre0, dg_mlp_pre1]), "mlp_post_g": jnp.concatenate([dg_mlp_post0, dg_mlp_post1]),
        "s5_lam_re": dlr.reshape(1, S5_GROUPS, S5_STATE), "s5_lam_im": dli.reshape(1, S5_GROUPS, S5_STATE),
        "s5_log_dt": dldt8[0:1, 0:S5_GROUPS],
        "s5_b_re": b_layout(dbtr), "s5_b_im": b_layout(dbti), "s5_c_re": c_layout(dcr2), "s5_c_im": c_layout(dci2),
        "s5_d": dd, "fox_b_f": dbf[:, 0:FOX_HEADS],
        "pool_w": _diag_blocks(dpw_bd, len(POOL_WINDOWS))[None],
        "sgu_w_s": jnp.where(causal[None], dws, 0.0)[None], "sgu_b_s": dbst.T[None],
        "pool_scale": dpool_scale, "sgu_ln_g": dln_g, "sgu_ln_b": dln_b,
    }
    return loss8, dx0, dict(zip(REDUCED_EARLY, halves)), _chips_from_cols(dw_in_e), local_small


def _reduce_and_update(W, M, V, loss, dx0, halves, dw_in_e, local_small):
    cx, cy, cc = _coords()
    chip = 2 * cx + cy

    vec = _pack_vec(local_small, REDUCED_SEGS, N_DEV * SUBLANES)
    piece = vec.shape[0] // N_DEV
    landed = exchange_pieces(vec.reshape(N_DEV, piece, LANES), scatter=True, name="small_grads_scatter")
    mine = sum_pieces(landed, name="small_grads_sum")
    everyone = exchange_pieces(mine, scatter=False, name="small_grads_gather")
    G = _unpack_vec(everyone, REDUCED_SEGS)
    for n in SHARDED_SMALL:
        G[n] = lax.dynamic_slice_in_dim(G[n], chip * LANES, LANES, axis=1)

    halves = dict(halves)
    halves["w_in_even"] = _reduce_to_my_half([dw_in_e], ["w_in_even"], "late_grads")[0]
    reduced = join_sibling_halves([halves[n] for n in BIG_NAMES], name="big_grads_join")
    for n, r in zip(BIG_NAMES, reduced):
        G[n] = r.reshape(W[n].shape)

    def two_d(a):
        return a.reshape(-1, a.shape[-1])

    delta, new_m, new_v = {}, {}, {}
    for n in BIG_NAMES:
        d_, m_, v_ = adamw(two_d(W[n]), two_d(G[n]), two_d(M[n]), two_d(V[n]), name=f"adamw_{n}")
        delta[n], new_m[n], new_v[n] = (t.reshape(W[n].shape) for t in (d_, m_, v_))
    packed = [_pack_vec(src, SMALL_SEGS, SUBLANES) for src in (W, G, M, V)]
    outs = adamw(*packed, name="adamw_replicated")
    for dst, t in zip((delta, new_m, new_v), outs):
        dst.update(_unpack_vec(t, SMALL_SEGS))
    sharded_segs = tuple((n, (1, LANES)) for n in SHARDED_SMALL)
    packed = [_pack_vec(src, sharded_segs, 1) for src in (W, G, M, V)]
    outs = adamw(*packed, name="adamw_sharded_vectors")
    for dst, t in zip((delta, new_m, new_v), outs):
        dst.update(_unpack_vec(t, sharded_segs))

    order = ["mix_pre_g", "mix_post_g", "mlp_pre_g", "mlp_post_g", "w_in_even", "s5_lam_re", "s5_lam_im", "s5_log_dt",
             "s5_b_re", "s5_b_im", "s5_c_re", "s5_c_im", "s5_d", "s5_w_glu", "fox_b_f", "w_out_even", "w_in_odd",
             "pool_w", "pool_scale", "sgu_ln_g", "sgu_ln_b", "sgu_w_s", "sgu_b_s", "w_out_odd", "mlp_w1", "mlp_w2"]
    return (loss, dx0[None], *[G[n] for n in order], *[delta[n] for n in order],
            *[new_m[n] for n in order], *[new_v[n] for n in order])
```

```python
import functools
import math

import jax
import jax.numpy as jnp
from jax import lax
from jax.experimental import pallas as pl
from jax.experimental.pallas import tpu as pltpu

F32 = jnp.float32
MXU_DTYPE = jnp.bfloat16
WIRE_DTYPE = jnp.bfloat16
EPS = 1e-6
VMEM_LIMIT_BYTES = 48 * 1024 * 1024
LANES = 128
SUBLANES = 8

D_MODEL = 1024
S5_WIDTH = 512
S5_GROUP = 16
S5_GROUPS = 32
S5_STATE = 64
S5_LANES = S5_GROUPS * S5_STATE
FOX_HEADS = 8
FOX_HEAD_DIM = 64
FOX_WIDTH = 512
EVEN_IN = S5_WIDTH + 3 * FOX_WIDTH + FOX_HEADS
EVEN_IN_PAD = 2176
POOL_WINDOWS = (2, 4, 8, 16)
POOL_HALO = 16
POOL_GROUP_DIM = 128
SGU_GROUPS = 4
SGU_GROUP_DIM = 128
CHUNK = 128
D_FF = 4096

ADAM_LR = 0.001
ADAM_B1 = 0.9
ADAM_B2 = 0.999
ADAM_EPS = 1e-08
ADAM_WD = 0.01
ADAM_STEP = 10

MESH_AXES = ("x", "y", "c")
MESH = pl.DeviceIdType.MESH
N_CHIPS = 4
N_DEV = 8

SDS = jax.ShapeDtypeStruct


def _cp(*sem):
    return pltpu.CompilerParams(dimension_semantics=sem, vmem_limit_bytes=VMEM_LIMIT_BYTES)


def _pick(dim, pref):
    if dim <= pref:
        return dim
    t = pref
    while t >= 256:
        if dim % t == 0:
            return t
        t //= 2
    return dim


def _row(tr, c):
    return pl.BlockSpec((tr, c), lambda i: (i, 0))


def _full(shape):
    nd = len(shape)
    return pl.BlockSpec(shape, lambda *_: (0,) * nd)


def _gelu_grad(x):
    c = math.sqrt(2.0 / math.pi)
    t = jnp.tanh(c * (x + 0.044715 * x * x * x))
    return 0.5 * (1.0 + t) + 0.5 * x * (1.0 - t * t) * c * (1.0 + 3.0 * 0.044715 * x * x)


MATMUL_VMEM_BYTES = 40 * 1024 * 1024


def matmul(a, b, *, name, ta=False, tb=False, out_dtype=F32, tm=1024, tn=512, tk=4096, mnk=None, a_koff=0,
           a_spec=None, b_spec=None, o_spec=None, o_shape=None, prev=None, epi=None, epi_in=(), out_dtypes=None):
    if mnk is None:
        M, K = (a.shape[1], a.shape[0]) if ta else a.shape
        K2, N = (b.shape[1], b.shape[0]) if tb else b.shape
        assert K == K2, (a.shape, b.shape, ta, tb)
    else:
        M, N, K = mnk
    out_dtypes = tuple(out_dtypes) if out_dtypes is not None else (out_dtype,)
    n_out, n_epi = len(out_dtypes), len(epi_in)
    tm, tn, tk = _pick(M, tm), _pick(N, tn), _pick(K, tk)

    def vmem_bytes(tk_):
        tiles = tm * tk_ * a.dtype.itemsize + tk_ * tn * b.dtype.itemsize
        tiles += tm * tn * (sum(jnp.dtype(d).itemsize for d in out_dtypes) + sum(e.dtype.itemsize for e in epi_in))
        return 2 * tiles + tm * tn * 4

    while vmem_bytes(tk) > MATMUL_VMEM_BYTES and tk % 2 == 0 and tk > 512:
        tk //= 2
    nk = K // tk
    assert a_koff % tk == 0 and not (ta and a_koff)
    ko = a_koff // tk
    dn = (((0 if ta else 1,), (1 if tb else 0,)), ((), ()))

    def body(*refs):
        a_ref, b_ref = refs[0], refs[1]
        epi_refs = refs[2:2 + n_epi]
        o_refs = refs[len(refs) - n_out - (nk > 1):len(refs) - (nk > 1)]
        k = pl.program_id(2)
        bv = b_ref[...]
        if bv.ndim == 3:
            bv = bv.reshape(-1, bv.shape[-1])
        prod = lax.dot_general(a_ref[...].astype(MXU_DTYPE), bv.astype(MXU_DTYPE), dn, preferred_element_type=F32)

        def finish(acc):
            res = (acc,) if epi is None else epi(acc, *[r[...] for r in epi_refs])
            for o_ref, r in zip(o_refs, res):
                o_ref[...] = r.astype(o_ref.dtype)

        if nk == 1:
            finish(prod)
            return
        acc_ref = refs[-1]

        @pl.when(k == 0)
        def _():
            acc_ref[...] = prod

        @pl.when(jnp.logical_and(k > 0, k < nk - 1))
        def _():
            acc_ref[...] += prod

        @pl.when(k == nk - 1)
        def _():
            finish(acc_ref[...] + prod)

    if a_spec is None:
        a_spec = pl.BlockSpec((tk, tm), lambda i, j, k: (k, i)) if ta else pl.BlockSpec((tm, tk), lambda i, j, k: (i, k + ko))
    else:
        a_spec = a_spec(tm, tn, tk)
    if b_spec is None:
        bs = pl.BlockSpec((tn, tk), lambda i, j, k: (j, k)) if tb else pl.BlockSpec((tk, tn), lambda i, j, k: (k, j))
    else:
        bs = b_spec(tm, tn, tk)
    tile = pl.BlockSpec((tm, tn), lambda i, j, k: (i, j))
    os_ = tile if o_spec is None else o_spec(tm, tn, tk)
    ins, in_specs, aliases = [a, b, *epi_in], [a_spec, bs] + [tile] * n_epi, {}
    if prev is not None:
        aliases = {len(ins): 0}
        ins.append(prev)
        in_specs.append(pl.BlockSpec(memory_space=pl.ANY))
    shapes = [SDS((M, N) if o_shape is None else o_shape, dt) for dt in out_dtypes]
    outs = pl.pallas_call(
        body, name=name, grid=(M // tm, N // tn, nk),
        in_specs=in_specs, out_specs=[os_] * n_out, out_shape=shapes, input_output_aliases=aliases,
        scratch_shapes=[pltpu.VMEM((tm, tn), F32)] if nk > 1 else [],
        compiler_params=_cp("parallel", "parallel", "arbitrary"),
    )(*ins)
    return outs[0] if n_out == 1 else outs


def _rms_hat(x):
    return x * lax.rsqrt(jnp.mean(x * x, axis=-1, keepdims=True) + EPS)


def rms_fwd(x, g, *, name):
    T, D = x.shape
    tr = _pick(T, 512)

    def body(x_ref, g_ref, o_ref):
        o_ref[...] = (_rms_hat(x_ref[...]) * g_ref[...]).astype(o_ref.dtype)

    return pl.pallas_call(body, name=name, grid=(T // tr,), in_specs=[_row(tr, D), _full((1, D))],
                          out_specs=_row(tr, D), out_shape=SDS((T, D), MXU_DTYPE), compiler_params=_cp("parallel"))(x, g)


def rms_res_fwd(x, y, g, *, name):
    T, D = x.shape
    tr = _pick(T, 512)

    def body(x_ref, y_ref, g_ref, o_ref):
        o_ref[...] = x_ref[...] + _rms_hat(y_ref[...]) * g_ref[...]

    return pl.pallas_call(body, name=name, grid=(T // tr,), in_specs=[_row(tr, D), _row(tr, D), _full((1, D))],
                          out_specs=_row(tr, D), out_shape=SDS((T, D), F32), compiler_params=_cp("parallel"))(x, y, g)


def rms_bwd(x, g, dy, res, *, name):
    T, D = x.shape
    tr = _pick(T, 512)
    has_res = res is not None

    def body(*refs):
        if has_res:
            x_ref, g_ref, dy_ref, res_ref, dx_ref, dg_ref = refs
        else:
            x_ref, g_ref, dy_ref, dx_ref, dg_ref = refs
        xv = x_ref[...]
        r = lax.rsqrt(jnp.mean(xv * xv, axis=-1, keepdims=True) + EPS)
        xh = xv * r
        dyv = dy_ref[...]
        dxh = dyv * g_ref[...]
        dx = r * (dxh - xh * jnp.mean(dxh * xh, axis=-1, keepdims=True))
        if has_res:
            dx = dx + res_ref[...]
        dx_ref[...] = dx.astype(dx_ref.dtype)

        @pl.when(pl.program_id(0) == 0)
        def _():
            dg_ref[...] = jnp.zeros_like(dg_ref)

        dg_ref[...] += jnp.sum(dyv * xh, axis=0, keepdims=True)

    ins = [x, g, dy] + ([res] if has_res else [])
    in_specs = [_row(tr, D), _full((1, D)), _row(tr, D)] + ([_row(tr, D)] if has_res else [])
    return pl.pallas_call(body, name=name, grid=(T // tr,), in_specs=in_specs,
                          out_specs=[_row(tr, D), _full((1, D))],
                          out_shape=[SDS((T, D), F32 if has_res else MXU_DTYPE), SDS((1, D), F32)],
                          compiler_params=_cp("arbitrary"))(*ins)


def loss_fwd_bwd(y, target, *, name):
    T, D = y.shape
    tr = _pick(T, 512)

    def body(y_ref, t_ref, l_ref, dy_ref):
        err = y_ref[...] - t_ref[...]
        dy_ref[...] = err * (1.0 / D)

        @pl.when(pl.program_id(0) == 0)
        def _():
            l_ref[...] = jnp.zeros_like(l_ref)

        l_ref[...] += 0.5 * jnp.sum(jnp.mean(err * err, axis=-1, keepdims=True))

    return pl.pallas_call(body, name=name, grid=(T // tr,), in_specs=[_row(tr, D), _row(tr, D)],
                          out_specs=[_full((SUBLANES, LANES)), _row(tr, D)],
                          out_shape=[SDS((SUBLANES, LANES), F32), SDS((T, D), F32)],
                          compiler_params=_cp("arbitrary"))(y, target)


def _s5_disc(lr, li, ldt, btr, bti):
    dt = jnp.exp(ldt)
    k = lax.broadcasted_iota(jnp.int32, (SUBLANES, S5_LANES), 0).astype(F32)
    kf = k + 1.0
    kb = 8.0 - k
    ph = li * dt
    lm = lr * dt
    tf_re = jnp.exp(kf * lm) * jnp.cos(kf * ph)
    tf_im = jnp.exp(kf * lm) * jnp.sin(kf * ph)
    tb_re = jnp.exp(kb * lm) * jnp.cos(kb * ph)
    tb_im = -jnp.exp(kb * lm) * jnp.sin(kb * ph)
    mag = jnp.exp(lm)
    ab_re = mag * jnp.cos(ph)
    ab_im = mag * jnp.sin(ph)
    den = lr * lr + li * li
    nr = ab_re - 1.0
    ni = ab_im
    q_re = (nr * lr + ni * li) / den
    q_im = (ni * lr - nr * li) / den
    bbt_re = q_re * btr - q_im * bti
    bbt_im = q_re * bti + q_im * btr
    return tf_re, tf_im, tb_re, tb_im, bbt_re, bbt_im


def _s5_disc_core(lr, li, ldt, btr, bti):
    dt = jnp.exp(ldt)
    mag = jnp.exp(lr * dt)
    ab_re = mag * jnp.cos(li * dt)
    ab_im = mag * jnp.sin(li * dt)
    den = lr * lr + li * li
    nr = ab_re - 1.0
    ni = ab_im
    q_re = (nr * lr + ni * li) / den
    q_im = (ni * lr - nr * li) / den
    return ab_re, ab_im, q_re * btr - q_im * bti, q_re * bti + q_im * btr


def s5_disc_fwd(lr, li, ldt, btr, bti, *, name):
    def body(lr_ref, li_ref, ldt_ref, btr_ref, bti_ref, *outs):
        vals = _s5_disc(lr_ref[...], li_ref[...], ldt_ref[...], btr_ref[...], bti_ref[...])
        for o, v in zip(outs, vals):
            o[...] = v

    tab = SDS((SUBLANES, S5_LANES), F32)
    bb = SDS((S5_GROUP, S5_LANES), F32)
    return pl.pallas_call(body, name=name, out_shape=[tab, tab, tab, tab, bb, bb])(lr, li, ldt, btr, bti)


def s5_disc_bwd(lr, li, ldt, btr, bti, dab_re, dab_im, dbbt_re, dbbt_im, *, name):
    def body(lr_ref, li_ref, ldt_ref, btr_ref, bti_ref, dar_ref, dai_ref, dbr_ref, dbi_ref,
             dlr_ref, dli_ref, dldt_ref, dbtr_ref, dbti_ref):
        _, vjp = jax.vjp(_s5_disc_core, lr_ref[...], li_ref[...], ldt_ref[...], btr_ref[...], bti_ref[...])
        dlr, dli, dldt, dbtr, dbti = vjp((dar_ref[...], dai_ref[...], dbr_ref[...], dbi_ref[...]))
        dlr_ref[...] = dlr
        dli_ref[...] = dli
        dbtr_ref[...] = dbtr
        dbti_ref[...] = dbti
        lane_group = lax.broadcasted_iota(jnp.int32, (S5_LANES, LANES), 0) // S5_STATE
        col = lax.broadcasted_iota(jnp.int32, (S5_LANES, LANES), 1)
        ind = (lane_group == col).astype(F32)
        dldt_ref[...] = jnp.dot(jnp.broadcast_to(dldt, (SUBLANES, S5_LANES)), ind,
                                precision=lax.Precision.HIGHEST, preferred_element_type=F32)

    row = SDS((1, S5_LANES), F32)
    bb = SDS((S5_GROUP, S5_LANES), F32)
    return pl.pallas_call(body, name=name, out_shape=[row, row, SDS((SUBLANES, LANES), F32), bb, bb])(
        lr, li, ldt, btr, bti, dab_re, dab_im, dbbt_re, dbbt_im)


S5_NB = 1024


S5_CB = S5_WIDTH * S5_NB // S5_LANES


def _chan_rows(tm, tn, tk):
    return pl.BlockSpec((tm, S5_CB), lambda i, j, k: (i, j // 2))


def _chan_rows_t(tm, tn, tk):
    return pl.BlockSpec((tk, S5_CB), lambda i, j, k: (k, j // 2))


def _chan_cols_of_i(tm, tn, tk):
    return pl.BlockSpec((tk, S5_CB), lambda i, j, k: (k, i // 2))


def _s5_b_block(tm, tn, tk):
    return pl.BlockSpec((S5_CB, S5_NB), lambda i, j, k: (j // 2, j))


def _s5_c_block_t(tm, tn, tk):
    return pl.BlockSpec((S5_NB, S5_CB), lambda i, j, k: (j, j // 2))


def _lanes_of_chan(tm, tn, tk):
    return pl.BlockSpec((tm, 2 * S5_NB), lambda i, j, k: (i, j))


def _s5_b_block_t(tm, tn, tk):
    return pl.BlockSpec((S5_CB, 2 * S5_NB), lambda i, j, k: (j, j))


def _s5_c_block(tm, tn, tk):
    return pl.BlockSpec((2 * S5_NB, S5_CB), lambda i, j, k: (j, j))


def s5_interleave(re, im, axis):
    parts = []
    for n in range(S5_LANES // S5_NB):
        sl = [slice(None)] * re.ndim
        sl[axis] = slice(n * S5_NB, (n + 1) * S5_NB)
        parts += [re[tuple(sl)], im[tuple(sl)]]
    return jnp.concatenate(parts, axis=axis)


def s5_deinterleave(a, axis):
    re, im = [], []
    for n in range(S5_LANES // S5_NB):
        sl = [slice(None)] * a.ndim
        sl[axis] = slice(2 * n * S5_NB, (2 * n + 1) * S5_NB)
        re.append(a[tuple(sl)])
        sl[axis] = slice((2 * n + 1) * S5_NB, (2 * n + 2) * S5_NB)
        im.append(a[tuple(sl)])
    return jnp.concatenate(re, axis=axis), jnp.concatenate(im, axis=axis)


def s5_scan(bu, tab_re, tab_im, *, reverse, name):
    T = bu.shape[0]
    nb = S5_NB
    tc = _pick(T, 256)
    nl = S5_LANES // nb
    nt = T // tc
    ntile = tc // SUBLANES
    step_rows = ((1, 7), (2, 6), (4, 4)) if reverse else ((1, 0), (2, 1), (4, 3))

    def body(br_ref, bi_ref, tr_ref, ti_ref, xo_ref, cr_ref, ci_ref, mr_ref, mi_ref):
        @pl.when(pl.program_id(1) == 0)
        def _():
            cr_ref[...] = jnp.zeros_like(cr_ref)
            ci_ref[...] = jnp.zeros_like(ci_ref)

        io = lax.broadcasted_iota(jnp.int32, (SUBLANES, nb), 0)
        for s_, (d, r) in enumerate(step_rows):
            keep = (io < SUBLANES - d) if reverse else (io >= d)
            mr_ref[s_] = jnp.where(keep, tr_ref[r:r + 1, :], 0.0)
            mi_ref[s_] = jnp.where(keep, ti_ref[r:r + 1, :], 0.0)

        def tile(i, carry):
            cr, ci = carry
            j = (ntile - 1 - i) if reverse else i
            r0 = pl.multiple_of(j * SUBLANES, SUBLANES)
            xr = br_ref[pl.ds(r0, SUBLANES), :]
            xi = bi_ref[pl.ds(r0, SUBLANES), :]
            for s_, (d, _) in enumerate(step_rows):
                sh = (SUBLANES - d) if reverse else d
                sr = pltpu.roll(xr, sh, 0)
                si = pltpu.roll(xi, sh, 0)
                pr, pi = mr_ref[s_], mi_ref[s_]
                xr, xi = xr + pr * sr - pi * si, xi + pr * si + pi * sr
            tr, ti = tr_ref[...], ti_ref[...]
            xr, xi = xr + tr * cr - ti * ci, xi + tr * ci + ti * cr
            xo_ref[pl.ds(r0, SUBLANES), 0:nb] = xr
            xo_ref[pl.ds(r0, SUBLANES), nb:2 * nb] = xi
            if reverse:
                return xr[0:1, :], xi[0:1, :]
            return xr[SUBLANES - 1:SUBLANES, :], xi[SUBLANES - 1:SUBLANES, :]

        cr, ci = lax.fori_loop(0, ntile, tile, (cr_ref[0:1, :], ci_ref[0:1, :]))
        cr_ref[0:1, :] = cr
        ci_ref[0:1, :] = ci

    def tmap(t):
        return (nt - 1 - t) if reverse else t

    re_spec = pl.BlockSpec((tc, nb), lambda n, t: (tmap(t), 2 * n))
    im_spec = pl.BlockSpec((tc, nb), lambda n, t: (tmap(t), 2 * n + 1))
    tab_spec = pl.BlockSpec((SUBLANES, nb), lambda n, t: (0, n))
    return pl.pallas_call(
        body, name=name, grid=(nl, nt), in_specs=[re_spec, im_spec, tab_spec, tab_spec],
        out_specs=pl.BlockSpec((tc, 2 * nb), lambda n, t: (tmap(t), n)), out_shape=SDS((T, 2 * S5_LANES), F32),
        scratch_shapes=[pltpu.VMEM((SUBLANES, nb), F32), pltpu.VMEM((SUBLANES, nb), F32),
                        pltpu.VMEM((len(step_rows), SUBLANES, nb), F32), pltpu.VMEM((len(step_rows), SUBLANES, nb), F32)],
        compiler_params=_cp("parallel", "arbitrary"),
    )(bu, bu, tab_re, tab_im)


def s5_da(lam, xs, *, name):
    T = xs.shape[0]
    nb = S5_NB
    tc = _pick(T, 256)
    nl, nt = S5_LANES // nb, T // tc
    hb = tc // SUBLANES

    def body(lr_ref, li_ref, xr_ref, xi_ref, hr_ref, hi_ref, dar_ref, dai_ref):
        t = pl.program_id(1)

        @pl.when(t == 0)
        def _():
            dar_ref[...] = jnp.zeros_like(dar_ref)
            dai_ref[...] = jnp.zeros_like(dai_ref)

        io = lax.broadcasted_iota(jnp.int32, (tc, nb), 0)
        first = jnp.where(t > 0, 1.0, 0.0)
        pr = jnp.where(io >= 1, pltpu.roll(xr_ref[...], 1, 0), hr_ref[SUBLANES - 1:SUBLANES, :] * first)
        pi = jnp.where(io >= 1, pltpu.roll(xi_ref[...], 1, 0), hi_ref[SUBLANES - 1:SUBLANES, :] * first)
        lr = lr_ref[...]
        li = li_ref[...]
        dar_ref[...] += jnp.sum(lr * pr + li * pi, axis=0, keepdims=True)
        dai_ref[...] += jnp.sum(li * pr - lr * pi, axis=0, keepdims=True)

    re_blk = pl.BlockSpec((tc, nb), lambda n, t: (t, 2 * n))
    im_blk = pl.BlockSpec((tc, nb), lambda n, t: (t, 2 * n + 1))
    re_halo = pl.BlockSpec((SUBLANES, nb), lambda n, t: (jnp.maximum(t * hb - 1, 0), 2 * n))
    im_halo = pl.BlockSpec((SUBLANES, nb), lambda n, t: (jnp.maximum(t * hb - 1, 0), 2 * n + 1))
    acc = pl.BlockSpec((1, nb), lambda n, t: (0, n))
    row = SDS((1, S5_LANES), F32)
    return pl.pallas_call(body, name=name, grid=(nl, nt), in_specs=[re_blk, im_blk, re_blk, im_blk, re_halo, im_halo],
                          out_specs=[acc, acc], out_shape=[row, row],
                          compiler_params=_cp("parallel", "arbitrary"))(lam, lam, xs, xs, xs, xs)


def s5_out_fwd(yc, u, d, *, name):
    T, C = yc.shape
    tr = _pick(T, 512)

    def body(yc_ref, u_ref, d_ref, yl_ref, yg_ref):
        yl = yc_ref[...] + d_ref[...] * u_ref[...]
        yl_ref[...] = yl
        yg_ref[...] = jax.nn.gelu(yl)

    return pl.pallas_call(body, name=name, grid=(T // tr,), in_specs=[_row(tr, C), _row(tr, C), _full((1, C))],
                          out_specs=[_row(tr, C)] * 2, out_shape=[SDS((T, C), F32)] * 2,
                          compiler_params=_cp("parallel"))(yc, u, d)


def glu_fwd(yg, gl, *, out_cols, name):
    T, C = yg.shape
    tr = _pick(T, 512)

    def body(yg_ref, gl_ref, o_ref):
        o_ref[...] = yg_ref[...] * jax.nn.sigmoid(gl_ref[...])

    return pl.pallas_call(body, name=name, grid=(T // tr,), in_specs=[_row(tr, C)] * 2, out_specs=_row(tr, C),
                          out_shape=SDS((T, out_cols), F32), compiler_params=_cp("parallel"))(yg, gl)


def glu_bwd(yg, gl, dy, *, name):
    T, C = yg.shape
    tr = _pick(T, 512)

    def body(yg_ref, gl_ref, dy_ref, dyg_ref, dgl_ref):
        s = jax.nn.sigmoid(gl_ref[...])
        dyv = dy_ref[...]
        dyg_ref[...] = dyv * s
        dgl_ref[...] = dyv * yg_ref[...] * s * (1.0 - s)

    return pl.pallas_call(body, name=name, grid=(T // tr,), in_specs=[_row(tr, C)] * 3, out_specs=[_row(tr, C)] * 2,
                          out_shape=[SDS((T, C), F32)] * 2, compiler_params=_cp("parallel"))(yg, gl, dy)


def s5_out_bwd(yl, u, d, dyg_a, dyg_b, *, name):
    T, C = yl.shape
    tr = _pick(T, 512)

    def body(yl_ref, u_ref, d_ref, da_ref, db_ref, dyl_ref, du_ref, dd_ref):
        dyl = (da_ref[...] + db_ref[...]) * _gelu_grad(yl_ref[...])
        dyl_ref[...] = dyl
        du_ref[...] = dyl * d_ref[...]

        @pl.when(pl.program_id(0) == 0)
        def _():
            dd_ref[...] = jnp.zeros_like(dd_ref)

        dd_ref[...] += jnp.sum(dyl * u_ref[...], axis=0, keepdims=True)

    return pl.pallas_call(body, name=name, grid=(T // tr,),
                          in_specs=[_row(tr, C), _row(tr, C), _full((1, C)), _row(tr, C), _row(tr, C)],
                          out_specs=[_row(tr, C), _row(tr, C), _full((1, C))],
                          out_shape=[SDS((T, C), F32), SDS((T, C), F32), SDS((1, C), F32)],
                          compiler_params=_cp("arbitrary"))(yl, u, d, dyg_a, dyg_b)


def add2(a, b, *, name):
    T, C = a.shape
    tr = _pick(T, 512)

    def body(a_ref, b_ref, o_ref):
        o_ref[...] = a_ref[...] + b_ref[...]

    return pl.pallas_call(body, name=name, grid=(T // tr,), in_specs=[_row(tr, C)] * 2, out_specs=_row(tr, C),
                          out_shape=SDS((T, C), F32), compiler_params=_cp("parallel"))(a, b)


def _tri(n, upper):
    r = lax.broadcasted_iota(jnp.int32, (n, n), 0)
    c = lax.broadcasted_iota(jnp.int32, (n, n), 1)
    return ((c >= r) if upper else (c <= r)).astype(F32)


def fox_gate_fwd(fl, bf, *, fl_col, name):
    T = fl.shape[0]
    tb = _pick(T, 256)

    def body(fl_ref, bf_ref, f_ref, c_ref):
        @pl.when(pl.program_id(0) == 0)
        def _():
            c_ref[...] = jnp.zeros_like(c_ref)

        lf = jax.nn.log_sigmoid(fl_ref[...] + bf_ref[...])
        f = jnp.dot(_tri(tb, False), lf, precision=lax.Precision.HIGHEST, preferred_element_type=F32) + c_ref[0:1, :]
        f_ref[...] = f * LOG2E
        c_ref[0:1, :] = f[tb - 1:tb, :]

    fl_spec = pl.BlockSpec((tb, LANES), lambda i: (i, fl_col))
    return pl.pallas_call(body, name=name, grid=(T // tb,), in_specs=[fl_spec, _full((1, LANES))],
                          out_specs=_row(tb, LANES), out_shape=SDS((T, LANES), F32),
                          scratch_shapes=[pltpu.VMEM((SUBLANES, LANES), F32)], compiler_params=_cp("arbitrary"))(fl, bf)


def fox_gate_bwd(fl, bf, df, *, fl_col, name):
    T = fl.shape[0]
    tb = _pick(T, 256)
    nt = T // tb

    def body(fl_ref, bf_ref, df_ref, dfl_ref, dbf_ref, c_ref):
        @pl.when(pl.program_id(0) == 0)
        def _():
            c_ref[...] = jnp.zeros_like(c_ref)
            dbf_ref[...] = jnp.zeros_like(dbf_ref)

        dlf = jnp.dot(_tri(tb, True), df_ref[...], precision=lax.Precision.HIGHEST, preferred_element_type=F32) + c_ref[0:1, :]
        c_ref[0:1, :] = dlf[0:1, :]
        dfl = dlf * jax.nn.sigmoid(-(fl_ref[...] + bf_ref[...]))
        dfl_ref[...] = dfl
        dbf_ref[...] += jnp.sum(dfl, axis=0, keepdims=True)

    rev = pl.BlockSpec((tb, LANES), lambda i: (nt - 1 - i, 0))
    fl_rev = pl.BlockSpec((tb, LANES), lambda i: (nt - 1 - i, fl_col))
    return pl.pallas_call(body, name=name, grid=(nt,), in_specs=[fl_rev, _full((1, LANES)), rev],
                          out_specs=[rev, _full((1, LANES))], out_shape=[SDS((T, LANES), F32), SDS((1, LANES), F32)],
                          scratch_shapes=[pltpu.VMEM((SUBLANES, LANES), F32)], compiler_params=_cp("arbitrary"))(fl, bf, df)


FOX_BLOCK = 512
FOX_PAIRS = FOX_HEADS // 2
_NT = (((1,), (1,)), ((), ()))


LOG2E = 1.4426950408889634
FOX_FWD_UNROLL = 4
FOX_BWD_UNROLL = 2


def _fox_block(T):
    return _pick(T, FOX_BLOCK)


def _own_lanes(lane, hh):
    return (lane < FOX_HEAD_DIM) if hh == 0 else (lane >= FOX_HEAD_DIM)


def _grouped_steps(step, lo, n, unroll, init):
    def trip(t, c):
        for u in range(unroll):
            c = step(lo + t * unroll + u, c)
        return c

    carry = lax.fori_loop(0, n // unroll, trip, init)
    for u in range(unroll - 1):
        carry = lax.cond(n % unroll > u, lambda c: step(lo + (n // unroll) * unroll + u, c), lambda c: c, carry)
    return carry


Q_TILE0, K_TILE0, V_TILE0, O_TILE0 = 4, 8, 12, 4
FL_TILE = 16
POOL_COL = 2


def fox_fwd(z, f_col, f_row, ycat, hosted, *, name):
    T = z.shape[0]
    blk = _fox_block(T)
    nb = T // blk
    scale = FOX_HEAD_DIM ** -0.5

    def body(q_ref, k_ref, v_ref, fc_ref, fr_ref, prev_ref, o_ref, l_ref):
        i = pl.program_id(1)
        row = lax.broadcasted_iota(jnp.int32, (blk, blk), 0)
        col = lax.broadcasted_iota(jnp.int32, (blk, blk), 1)
        lane = lax.broadcasted_iota(jnp.int32, (blk, LANES), 1)
        qt = q_ref[...] * (scale * LOG2E)
        outs = []
        for hh in range(2):
            qh = jnp.where(_own_lanes(lane, hh), qt, 0.0).astype(MXU_DTYPE)
            fi = fc_ref[0, :, hh:hh + 1]

            def step(j, carry, masked=False):
                m, l, acc = carry
                r0 = pl.multiple_of(j * blk, blk)
                kj = k_ref[pl.ds(r0, blk), :].astype(MXU_DTYPE)
                vj = v_ref[pl.ds(r0, blk), :].astype(MXU_DTYPE)
                s = lax.dot_general(qh, kj, _NT, preferred_element_type=F32) + (fi - fr_ref[0, j, hh:hh + 1, :])
                if masked:
                    s = jnp.where(col <= row, s, -jnp.inf)
                m_new = jnp.maximum(m, jnp.max(s, axis=-1, keepdims=True))
                p = jnp.exp2(s - m_new)
                alpha = jnp.exp2(m - m_new)
                l = alpha * l + jnp.sum(p, axis=-1, keepdims=True)
                acc = alpha * acc + jnp.dot(p.astype(MXU_DTYPE), vj, preferred_element_type=F32)
                return m_new, l, acc

            init = (jnp.full((blk, 1), -jnp.inf, F32), jnp.zeros((blk, 1), F32), jnp.zeros((blk, LANES), F32))
            m, l, acc = step(i, _grouped_steps(step, 0, i, FOX_FWD_UNROLL, init), True)
            outs.append(acc / l)
            l_ref[0, :, hh:hh + 1] = m + jnp.log2(l)
        o_ref[...] = jnp.where(_own_lanes(lane, 0), outs[0], outs[1])

    qspec = pl.BlockSpec((blk, LANES), lambda h, i: (i, Q_TILE0 + h))
    kspec = pl.BlockSpec((T, LANES), lambda h, i: (0, K_TILE0 + h))
    vspec = pl.BlockSpec((T, LANES), lambda h, i: (0, V_TILE0 + h))
    ospec = pl.BlockSpec((blk, LANES), lambda h, i: (i, O_TILE0 + h))
    cspec = pl.BlockSpec((1, blk, 2), lambda h, i: (h, i, 0))
    rspec = pl.BlockSpec((1, nb, 2, blk), lambda h, i: (h, 0, 0, 0))
    return call_hosting(body, hosted, name=name, grid=(FOX_PAIRS, nb),
                        in_specs=[qspec, kspec, vspec, cspec, rspec, ANY], out_specs=[ospec, cspec],
                        out_shape=[SDS(ycat.shape, F32), SDS((FOX_PAIRS, T, 2), F32)],
                        inputs=[z, z, z, f_col, f_row, ycat], aliases={5: 0})


def fox_bwd_q(z, ycat, dycat, f_col, f_row, lse_col, hosted, *, name):
    T = z.shape[0]
    blk = _fox_block(T)
    nb = T // blk
    scale = FOX_HEAD_DIM ** -0.5

    def body(q_ref, k_ref, v_ref, o_ref, do_ref, fc_ref, fr_ref, lc_ref, dq_ref, dd_ref, df_ref):
        i = pl.program_id(1)
        row = lax.broadcasted_iota(jnp.int32, (blk, blk), 0)
        col = lax.broadcasted_iota(jnp.int32, (blk, blk), 1)
        lane = lax.broadcasted_iota(jnp.int32, (blk, LANES), 1)
        qt = q_ref[...] * (scale * LOG2E)
        dot = do_ref[...]
        prod = dot * o_ref[...]
        outs = []
        for hh in range(2):
            own = _own_lanes(lane, hh)
            qh = jnp.where(own, qt, 0.0).astype(MXU_DTYPE)
            dob = jnp.where(own, dot, 0.0).astype(MXU_DTYPE)
            dd = jnp.sum(jnp.where(own, prod, 0.0), axis=-1, keepdims=True)
            dd_ref[0, :, hh:hh + 1] = dd
            fi = fc_ref[0, :, hh:hh + 1]
            lse = lc_ref[0, :, hh:hh + 1]

            def step(j, carry, masked=False):
                dq, df = carry
                r0 = pl.multiple_of(j * blk, blk)
                kj = k_ref[pl.ds(r0, blk), :].astype(MXU_DTYPE)
                vj = v_ref[pl.ds(r0, blk), :].astype(MXU_DTYPE)
                s = lax.dot_general(qh, kj, _NT, preferred_element_type=F32) + (fi - fr_ref[0, j, hh:hh + 1, :])
                p = jnp.exp2(s - lse)
                if masked:
                    p = jnp.where(col <= row, p, 0.0)
                dp = lax.dot_general(dob, vj, _NT, preferred_element_type=F32)
                ds = p * (dp - dd)
                return (dq + jnp.dot(ds.astype(MXU_DTYPE), kj, preferred_element_type=F32),
                        df + jnp.sum(ds, axis=-1, keepdims=True))

            init = (jnp.zeros((blk, LANES), F32), jnp.zeros((blk, 1), F32))
            dq, df = step(i, _grouped_steps(step, 0, i, FOX_BWD_UNROLL, init), True)
            outs.append(dq * scale)
            df_ref[0, :, hh:hh + 1] = df
        dq_ref[...] = jnp.where(_own_lanes(lane, 0), outs[0], outs[1])

    qspec = pl.BlockSpec((blk, LANES), lambda h, i: (i, Q_TILE0 + h))
    kspec = pl.BlockSpec((T, LANES), lambda h, i: (0, K_TILE0 + h))
    vspec = pl.BlockSpec((T, LANES), lambda h, i: (0, V_TILE0 + h))
    ospec = pl.BlockSpec((blk, LANES), lambda h, i: (i, O_TILE0 + h))
    dqspec = pl.BlockSpec((blk, LANES), lambda h, i: (i, h))
    cspec = pl.BlockSpec((1, blk, 2), lambda h, i: (h, i, 0))
    rspec = pl.BlockSpec((1, nb, 2, blk), lambda h, i: (h, 0, 0, 0))
    stat = SDS((FOX_PAIRS, T, 2), F32)
    return call_hosting(body, hosted, name=name, grid=(FOX_PAIRS, nb),
                        in_specs=[qspec, kspec, vspec, ospec, ospec, cspec, rspec, cspec],
                        out_specs=[dqspec, cspec, cspec], out_shape=[SDS((T, FOX_WIDTH), F32), stat, stat],
                        inputs=[z, z, z, ycat, dycat, f_col, f_row, lse_col], aliases={})


def fox_bwd_kv(z, dycat, f_col, f_row, lse_row, dd_row, dfq_col, hosted, *, name):
    T = z.shape[0]
    blk = _fox_block(T)
    nb = T // blk
    scale = FOX_HEAD_DIM ** -0.5

    def body(q_ref, k_ref, v_ref, do_ref, fc_ref, fr_ref, lr_ref, dr_ref, dfq_ref, dk_ref, dv_ref, df_ref):
        j = pl.program_id(1)
        row = lax.broadcasted_iota(jnp.int32, (blk, blk), 0)
        col = lax.broadcasted_iota(jnp.int32, (blk, blk), 1)
        lane = lax.broadcasted_iota(jnp.int32, (blk, LANES), 1)
        kt = k_ref[...]
        vt = v_ref[...]
        dks, dvs = [], []
        for hh in range(2):
            own = _own_lanes(lane, hh)
            kh = jnp.where(own, kt, 0.0).astype(MXU_DTYPE)
            vh = jnp.where(own, vt, 0.0).astype(MXU_DTYPE)
            fj = fc_ref[0, :, hh:hh + 1]

            def step(i, carry, masked=False):
                dk, dv, df = carry
                r0 = pl.multiple_of(i * blk, blk)
                qi = (q_ref[pl.ds(r0, blk), :] * (scale * LOG2E)).astype(MXU_DTYPE)
                doi = do_ref[pl.ds(r0, blk), :].astype(MXU_DTYPE)
                st = lax.dot_general(kh, qi, _NT, preferred_element_type=F32) + (fr_ref[0, i, hh:hh + 1, :] - fj)
                pt = jnp.exp2(st - lr_ref[0, i, hh:hh + 1, :])
                if masked:
                    pt = jnp.where(col >= row, pt, 0.0)
                dv = dv + jnp.dot(pt.astype(MXU_DTYPE), doi, preferred_element_type=F32)
                dpt = lax.dot_general(vh, doi, _NT, preferred_element_type=F32)
                dst = pt * (dpt - dr_ref[0, i, hh:hh + 1, :])
                dk = dk + jnp.dot(dst.astype(MXU_DTYPE), qi, preferred_element_type=F32)
                df = df - jnp.sum(dst, axis=-1, keepdims=True)
                return dk, dv, df

            init = (jnp.zeros((blk, LANES), F32), jnp.zeros((blk, LANES), F32), dfq_ref[0, :, hh:hh + 1])
            dk, dv, df = _grouped_steps(step, j + 1, nb - 1 - j, FOX_BWD_UNROLL, step(j, init, True))
            dks.append(dk * (1.0 / LOG2E))
            dvs.append(dv)
            df_ref[0, :, hh:hh + 1] = df
        dk_ref[...] = jnp.where(_own_lanes(lane, 0), dks[0], dks[1])
        dv_ref[...] = jnp.where(_own_lanes(lane, 0), dvs[0], dvs[1])

    bspec = pl.BlockSpec((blk, LANES), lambda h, j: (j, h))
    qspec = pl.BlockSpec((T, LANES), lambda h, j: (0, Q_TILE0 + h))
    kspec = pl.BlockSpec((blk, LANES), lambda h, j: (j, K_TILE0 + h))
    vspec = pl.BlockSpec((blk, LANES), lambda h, j: (j, V_TILE0 + h))
    dospec = pl.BlockSpec((T, LANES), lambda h, j: (0, O_TILE0 + h))
    cspec = pl.BlockSpec((1, blk, 2), lambda h, j: (h, j, 0))
    rspec = pl.BlockSpec((1, nb, 2, blk), lambda h, j: (h, 0, 0, 0))
    return call_hosting(body, hosted, name=name, grid=(FOX_PAIRS, nb),
                        in_specs=[qspec, kspec, vspec, dospec, cspec, rspec, rspec, rspec, cspec],
                        out_specs=[bspec, bspec, cspec],
                        out_shape=[SDS((T, FOX_WIDTH), F32), SDS((T, FOX_WIDTH), F32), SDS((FOX_PAIRS, T, 2), F32)],
                        inputs=[z, z, z, dycat, f_col, f_row, lse_row, dd_row, dfq_col], aliases={})


def _pairs_col(a, T):
    return jnp.transpose(a[:, :FOX_HEADS].reshape(T, FOX_PAIRS, 2), (1, 0, 2))


def _col_to_row(a, T):
    blk = _fox_block(T)
    return jnp.transpose(a.reshape(FOX_PAIRS, T // blk, blk, 2), (0, 1, 3, 2))


def _pairs_to_lanes(a, T):
    flat = jnp.transpose(a, (1, 0, 2)).reshape(T, FOX_HEADS)
    return jnp.pad(flat, ((0, 0), (0, LANES - FOX_HEADS)))


def _pool_counts(t0, n, w):
    t = (t0 + lax.broadcasted_iota(jnp.int32, (n, 1), 0)).astype(F32)
    return jnp.minimum(t + 1.0, float(w))


def pool_window(x, *, adjoint, name, in_col=0, into=None, out_col=0):
    T, C = x.shape[0], len(POOL_WINDOWS) * POOL_GROUP_DIM
    tr = _pick(T, 512)
    nt = T // tr
    hb = tr // POOL_HALO
    n = tr + POOL_HALO

    def body(x_ref, h_ref, *rest):
        o_ref = rest[-1]
        i = pl.program_id(0)
        cur = x_ref[...]
        if adjoint:
            halo = h_ref[...] * jnp.where(i < nt - 1, 1.0, 0.0)
            ext = jnp.concatenate([cur, halo], axis=0)
            t0 = i * tr
        else:
            halo = h_ref[...] * jnp.where(i > 0, 1.0, 0.0)
            ext = jnp.concatenate([halo, cur], axis=0)
            t0 = i * tr - POOL_HALO
        sums = {}
        for g, w in enumerate(POOL_WINDOWS):
            ls = slice(g * POOL_GROUP_DIM, (g + 1) * POOL_GROUP_DIM)
            s = ext[:, ls]
            if adjoint:
                s = s / _pool_counts(t0, n, w)
            d = 1
            while d < w:
                s = s + pltpu.roll(s, (n - d) if adjoint else d, 0)
                d *= 2
            if adjoint:
                o_ref[:, ls] = s[0:tr, :] - cur[:, ls]
            else:
                o_ref[:, ls] = s[POOL_HALO:n, :] / _pool_counts(i * tr, tr, w) - cur[:, ls]

    if adjoint:
        halo_spec = pl.BlockSpec((POOL_HALO, C), lambda i: (jnp.minimum((i + 1) * hb, T // POOL_HALO - 1), in_col))
    else:
        halo_spec = pl.BlockSpec((POOL_HALO, C), lambda i: (jnp.maximum(i * hb - 1, 0), in_col))
    x_spec = pl.BlockSpec((tr, C), lambda i: (i, in_col))
    if into is None:
        return pl.pallas_call(body, name=name, grid=(nt,), in_specs=[x_spec, halo_spec], out_specs=_row(tr, C),
                              out_shape=SDS((T, C), F32), compiler_params=_cp("parallel"))(x, x)
    return pl.pallas_call(body, name=name, grid=(nt,), in_specs=[x_spec, halo_spec, ANY],
                          out_specs=pl.BlockSpec((tr, C), lambda i: (i, out_col)), out_shape=SDS(into.shape, F32),
                          input_output_aliases={2: 0}, compiler_params=_cp("parallel"))(x, x, into)


def colscale_fwd(a, s, *, out_cols, name):
    T, C = a.shape
    tr = _pick(T, 512)

    def body(a_ref, s_ref, o_ref):
        o_ref[...] = a_ref[...] * s_ref[...]

    return pl.pallas_call(body, name=name, grid=(T // tr,), in_specs=[_row(tr, C), _full((1, C))], out_specs=_row(tr, C),
                          out_shape=SDS((T, out_cols), F32), compiler_params=_cp("parallel"))(a, s)


def colscale_bwd(a, s, dy, *, name):
    T, C = a.shape
    tr = _pick(T, 512)

    def body(a_ref, s_ref, dy_ref, da_ref, ds_ref):
        dyv = dy_ref[...]
        da_ref[...] = dyv * s_ref[...]

        @pl.when(pl.program_id(0) == 0)
        def _():
            ds_ref[...] = jnp.zeros_like(ds_ref)

        ds_ref[...] += jnp.sum(dyv * a_ref[...], axis=0, keepdims=True)

    return pl.pallas_call(body, name=name, grid=(T // tr,), in_specs=[_row(tr, C), _full((1, C)), _row(tr, C)],
                          out_specs=[_row(tr, C), _full((1, C))], out_shape=[SDS((T, C), F32), SDS((1, C), F32)],
                          compiler_params=_cp("arbitrary"))(a, s, dy)


SGU_ROWS = 512


def _sgu_norm(v, ln_g, ln_b):
    vg = jax.nn.gelu(v)
    xc = vg - jnp.mean(vg, axis=-1, keepdims=True)
    r = lax.rsqrt(jnp.mean(xc * xc, axis=-1, keepdims=True) + EPS)
    xh = xc * r
    return xh * ln_g + ln_b, xh, r


def _rowc(tr, c, cb):
    return pl.BlockSpec((tr, c), lambda i: (i, cb))


def sgu_fwd(z, ln_g, ln_b, ws, bst, ycat, *, name):
    T, C = z.shape[0], SGU_GROUPS * SGU_GROUP_DIM
    tr = _pick(T, SGU_ROWS)

    def body(u_ref, v_ref, g_ref, b_ref, ws_ref, bst_ref, prev_ref, o_ref):
        vn, _, _ = _sgu_norm(v_ref[...], g_ref[...], b_ref[...])
        vn = vn.astype(MXU_DTYPE)
        ug = jax.nn.gelu(u_ref[...])
        for g in range(SGU_GROUPS):
            w = ws_ref[g].astype(MXU_DTYPE)
            bias = bst_ref[:, g:g + 1]
            for c in range(tr // CHUNK):
                rs = slice(c * CHUNK, (c + 1) * CHUNK)
                ls = slice(g * SGU_GROUP_DIM, (g + 1) * SGU_GROUP_DIM)
                mixed = jnp.dot(w, vn[rs, ls], preferred_element_type=F32) + bias
                o_ref[rs, ls] = ug[rs, ls] * mixed

    return pl.pallas_call(body, name=name, grid=(T // tr,),
                          in_specs=[_rowc(tr, C, 0), _rowc(tr, C, 1), _full((1, C)), _full((1, C)),
                                    _full((SGU_GROUPS, CHUNK, CHUNK)), _full((CHUNK, SGU_GROUPS)), ANY],
                          out_specs=_rowc(tr, C, 1), out_shape=SDS(ycat.shape, F32), input_output_aliases={6: 0},
                          compiler_params=_cp("parallel"))(z, z, ln_g, ln_b, ws, bst, ycat)


def sgu_bwd(z, ln_g, ln_b, ws, wst, bst, dycat, *, out_cols, name):
    T, C = z.shape[0], SGU_GROUPS * SGU_GROUP_DIM
    tr = _pick(T, SGU_ROWS)

    def body(u_ref, v_ref, g_ref, b_ref, ws_ref, wst_ref, bst_ref, dy_ref,
             duv_ref, dg_ref, db_ref, dws_ref, dbst_ref, dvn_ref):
        du_ref = duv_ref.at[:, 0:C]
        dv_ref = duv_ref.at[:, C:2 * C]
        @pl.when(pl.program_id(0) == 0)
        def _():
            dg_ref[...] = jnp.zeros_like(dg_ref)
            db_ref[...] = jnp.zeros_like(db_ref)
            dws_ref[...] = jnp.zeros_like(dws_ref)
            dbst_ref[...] = jnp.zeros_like(dbst_ref)

        uv = u_ref[...]
        vv = v_ref[...]
        vn, xh, r = _sgu_norm(vv, g_ref[...], b_ref[...])
        vn = vn.astype(MXU_DTYPE)
        ug = jax.nn.gelu(uv)
        dyv = dy_ref[...]
        for g in range(SGU_GROUPS):
            w = ws_ref[g].astype(MXU_DTYPE)
            wt = wst_ref[g].astype(MXU_DTYPE)
            bias = bst_ref[:, g:g + 1]
            dw = jnp.zeros((CHUNK, CHUNK), F32)
            dbias = jnp.zeros((CHUNK, 1), F32)
            for c in range(tr // CHUNK):
                rs = slice(c * CHUNK, (c + 1) * CHUNK)
                ls = slice(g * SGU_GROUP_DIM, (g + 1) * SGU_GROUP_DIM)
                vblk = vn[rs, ls]
                mixed = jnp.dot(w, vblk, preferred_element_type=F32) + bias
                dyb = dyv[rs, ls]
                du_ref[rs, ls] = dyb * mixed * _gelu_grad(uv[rs, ls])
                dmixed = dyb * ug[rs, ls]
                dbias = dbias + jnp.sum(dmixed, axis=-1, keepdims=True)
                dmb = dmixed.astype(MXU_DTYPE)
                dw = dw + lax.dot_general(dmb, vblk, _NT, preferred_element_type=F32)
                dvn_ref[rs, ls] = jnp.dot(wt, dmb, preferred_element_type=F32)
            dws_ref[g] += dw
            dbst_ref[:, g:g + 1] += dbias
        dvn = dvn_ref[...]
        dg_ref[...] += jnp.sum(dvn * xh, axis=0, keepdims=True)
        db_ref[...] += jnp.sum(dvn, axis=0, keepdims=True)
        dxh = dvn * g_ref[...]
        dvg = r * (dxh - jnp.mean(dxh, axis=-1, keepdims=True) - xh * jnp.mean(dxh * xh, axis=-1, keepdims=True))
        dv_ref[...] = dvg * _gelu_grad(vv)

    wspec = _full((SGU_GROUPS, CHUNK, CHUNK))
    return pl.pallas_call(body, name=name, grid=(T // tr,),
                          in_specs=[_rowc(tr, C, 0), _rowc(tr, C, 1), _full((1, C)), _full((1, C)), wspec, wspec,
                                    _full((CHUNK, SGU_GROUPS)), _rowc(tr, C, 1)],
                          out_specs=[_rowc(tr, 2 * C, 0), _full((1, C)), _full((1, C)), wspec,
                                     _full((CHUNK, SGU_GROUPS))],
                          out_shape=[SDS((T, out_cols), F32), SDS((1, C), F32), SDS((1, C), F32),
                                     SDS((SGU_GROUPS, CHUNK, CHUNK), F32), SDS((CHUNK, SGU_GROUPS), F32)],
                          scratch_shapes=[pltpu.VMEM((tr, C), F32)],
                          compiler_params=_cp("arbitrary"))(z, z, ln_g, ln_b, ws, wst, bst, dycat)


def adamw(w, g, m, v, *, name):
    R, C = w.shape
    tr = _pick(R, 512)
    c1 = 1.0 - ADAM_B1 ** ADAM_STEP
    c2 = 1.0 - ADAM_B2 ** ADAM_STEP

    def body(w_ref, g_ref, m_ref, v_ref, d_ref, nm_ref, nv_ref):
        gv = g_ref[...]
        nm = ADAM_B1 * m_ref[...] + (1.0 - ADAM_B1) * gv
        nv = ADAM_B2 * v_ref[...] + (1.0 - ADAM_B2) * (gv * gv)
        nm_ref[...] = nm
        nv_ref[...] = nv
        d_ref[...] = -ADAM_LR * ((nm / c1) / (jnp.sqrt(nv / c2) + ADAM_EPS) + ADAM_WD * w_ref[...])

    spec = _row(tr, C)
    return pl.pallas_call(body, name=name, grid=(R // tr,), in_specs=[spec] * 4, out_specs=[spec] * 3,
                          out_shape=[SDS((R, C), F32)] * 3, compiler_params=_cp("parallel"))(w, g, m, v)


ANY = pl.BlockSpec(memory_space=pl.ANY)


def _coords():
    return lax.axis_index("x"), lax.axis_index("y"), lax.axis_index("c")


def _other_chips(x, y):
    return [(1 - x, y), (x, 1 - y), (1 - x, 1 - y)]


def _remote(src, dst, send_sems, recv_sems, k, dev):
    return pltpu.make_async_remote_copy(src_ref=src, dst_ref=dst, send_sem=send_sems.at[k], recv_sem=recv_sems.at[k],
                                        device_id=dev, device_id_type=MESH)


LOCAL_CHUNKS = 8


def allgather_chip_shards(shards, small, *, name):
    na = len(shards)

    def body(*refs):
        s_refs, sm_ref = refs[:na], refs[na]
        o_refs, smo_ref = refs[na + 1:2 * na + 1], refs[2 * na + 1]
        send_sems, recv_sems, local_sems = refs[2 * na + 2:]
        x, y, c = _coords()
        j = 2 * x + y
        sibling = (x, y, 1 - c)
        chips = _other_chips(x, y)
        for a in range(na):
            chunk = shards[a].shape[0] // LOCAL_CHUNKS
            for q in range(LOCAL_CHUNKS):
                rows = pl.ds(q * chunk, chunk)
                pltpu.make_async_copy(s_refs[a].at[rows], o_refs[a].at[j, rows], local_sems.at[a]).start()
        pltpu.make_async_copy(sm_ref, smo_ref.at[j], local_sems.at[na]).start()
        sends = []
        for a in range(na):
            half = shards[a].shape[0] // 2
            mine = pl.ds(c * half, half)
            for k, (px, py) in enumerate(chips):
                sends.append(_remote(s_refs[a].at[mine], o_refs[a].at[j, mine], send_sems, recv_sems, 6 * a + k, (px, py, c)))
        for k, (px, py) in enumerate(chips):
            sends.append(_remote(sm_ref, smo_ref.at[j], send_sems, recv_sems, 6 * na + k, (px, py, c)))
        for cp in sends:
            cp.start()
        for a in range(na):
            half = shards[a].shape[0] // 2
            mine = pl.ds(c * half, half)
            for k, (px, py) in enumerate(chips):
                rows = o_refs[a].at[2 * px + py, mine]
                _remote(rows, rows, send_sems, recv_sems, 6 * a + k, (px, py, c)).wait_recv()
                fw = _remote(rows, rows, send_sems, recv_sems, 6 * a + 3 + k, sibling)
                fw.start()
                sends.append(fw)
        for a in range(na):
            half = shards[a].shape[0] // 2
            theirs = pl.ds((1 - c) * half, half)
            for k, (px, py) in enumerate(chips):
                rows = o_refs[a].at[2 * px + py, theirs]
                _remote(rows, rows, send_sems, recv_sems, 6 * a + 3 + k, sibling).wait_recv()
        for k, (px, py) in enumerate(chips):
            slot = smo_ref.at[2 * px + py]
            _remote(slot, slot, send_sems, recv_sems, 6 * na + k, (px, py, c)).wait_recv()
        for cp in sends:
            cp.wait_send()
        for a in range(na):
            pltpu.make_async_copy(s_refs[a], o_refs[a].at[j], local_sems.at[a]).wait()
        pltpu.make_async_copy(sm_ref, smo_ref.at[j], local_sems.at[na]).wait()

    nsem = 6 * na + 3
    outs = pl.pallas_call(
        body, name=name, in_specs=[ANY] * (na + 1), out_specs=[ANY] * (na + 1),
        out_shape=[SDS((N_CHIPS,) + s.shape, s.dtype) for s in shards] + [SDS((N_CHIPS,) + small.shape, small.dtype)],
        scratch_shapes=[pltpu.SemaphoreType.DMA((nsem,)), pltpu.SemaphoreType.DMA((nsem,)),
                        pltpu.SemaphoreType.DMA((na + 1,))])(*shards, small)
    return outs[:na], outs[na]


class Exchange:
    def __init__(self, ins, out_shapes, scratch, start, wait):
        self.ins, self.out_shapes, self.scratch, self.start, self.wait = list(ins), list(out_shapes), list(scratch), start, wait


def run_exchange(ex, *, name):
    ni, no = len(ex.ins), len(ex.out_shapes)

    def body(*refs):
        parts = refs[:ni], refs[ni:ni + no], refs[ni + no:]
        ex.start(*parts)
        ex.wait(*parts)

    return pl.pallas_call(body, name=name, in_specs=[ANY] * ni, out_specs=[ANY] * no, out_shape=ex.out_shapes,
                          scratch_shapes=ex.scratch)(*ex.ins)


def call_hosting(body, ex, *, name, grid, in_specs, out_specs, out_shape, inputs, aliases):
    n_in, n_out, ni, no = len(inputs), len(out_shape), len(ex.ins), len(ex.out_shapes)

    def wrapped(*refs):
        own = refs[:n_in] + refs[n_in + ni:n_in + ni + n_out]
        parts = refs[n_in:n_in + ni], refs[n_in + ni + n_out:n_in + ni + n_out + no], refs[n_in + ni + n_out + no:]
        ids = [pl.program_id(d) for d in range(len(grid))]
        first = functools.reduce(jnp.logical_and, [i == 0 for i in ids])
        last = functools.reduce(jnp.logical_and, [i == g - 1 for i, g in zip(ids, grid)])

        @pl.when(first)
        def _():
            ex.start(*parts)

        body(*own)

        @pl.when(last)
        def _():
            ex.wait(*parts)

    outs = pl.pallas_call(
        wrapped, name=name, grid=grid, in_specs=list(in_specs) + [ANY] * ni, out_specs=list(out_specs) + [ANY] * no,
        out_shape=list(out_shape) + ex.out_shapes, input_output_aliases=aliases, scratch_shapes=ex.scratch,
        compiler_params=_cp(*["arbitrary"] * len(grid)))(*inputs, *ex.ins)
    return outs[:n_out], outs[n_out:]


def allgather_ici_exchange(shards):
    na = len(shards)

    def copies(s_refs, o_refs, sems):
        send_sems, recv_sems, _ = sems
        x, y, c = _coords()
        j = 2 * x + y
        out = []
        for a in range(na):
            half = shards[a].shape[0] // 2
            mine = pl.ds(c * half, half)
            for k, (px, py) in enumerate(_other_chips(x, y)):
                send = _remote(s_refs[a].at[mine], o_refs[a].at[j, mine], send_sems, recv_sems, 3 * a + k, (px, py, c))
                rows = o_refs[a].at[2 * px + py, mine]
                out.append((send, _remote(rows, rows, send_sems, recv_sems, 3 * a + k, (px, py, c))))
        return out

    def start(s_refs, o_refs, sems):
        x, y, c = _coords()
        j = 2 * x + y
        for a in range(na):
            chunk = shards[a].shape[0] // LOCAL_CHUNKS
            for q in range(LOCAL_CHUNKS):
                rows = pl.ds(q * chunk, chunk)
                pltpu.make_async_copy(s_refs[a].at[rows], o_refs[a].at[j, rows], sems[2].at[a]).start()
        for send, _ in copies(s_refs, o_refs, sems):
            send.start()

    def wait(s_refs, o_refs, sems):
        x, y, c = _coords()
        j = 2 * x + y
        for send, arrival in copies(s_refs, o_refs, sems):
            arrival.wait_recv()
            send.wait_send()
        for a in range(na):
            pltpu.make_async_copy(s_refs[a], o_refs[a].at[j], sems[2].at[a]).wait()

    return Exchange(shards, [SDS((N_CHIPS,) + s.shape, s.dtype) for s in shards],
                    [pltpu.SemaphoreType.DMA((3 * na,)), pltpu.SemaphoreType.DMA((3 * na,)), pltpu.SemaphoreType.DMA((na,))],
                    start, wait)


def allgather_forward(gathered, *, name):
    na = len(gathered)

    def body(*refs):
        o_refs = refs[na:2 * na]
        send_sems, recv_sems = refs[2 * na:]
        x, y, c = _coords()
        sibling = (x, y, 1 - c)
        cps = []
        for a in range(na):
            half = gathered[a].shape[1] // 2
            for k, (px, py) in enumerate(_other_chips(x, y)):
                mine = o_refs[a].at[2 * px + py, pl.ds(c * half, half)]
                theirs = o_refs[a].at[2 * px + py, pl.ds((1 - c) * half, half)]
                cps.append((_remote(mine, mine, send_sems, recv_sems, 3 * a + k, sibling),
                            _remote(theirs, theirs, send_sems, recv_sems, 3 * a + k, sibling)))
        for send, _ in cps:
            send.start()
        for send, arrival in cps:
            send.wait_send()
            arrival.wait_recv()

    return pl.pallas_call(body, name=name, in_specs=[ANY] * na, out_specs=[ANY] * na,
                          out_shape=[SDS(g.shape, g.dtype) for g in gathered],
                          input_output_aliases={a: a for a in range(na)},
                          scratch_shapes=[pltpu.SemaphoreType.DMA((3 * na,)), pltpu.SemaphoreType.DMA((3 * na,))])(*gathered)


def swap_halves_exchange(gs):
    na = len(gs)

    def copies(g_refs, o_refs, sems):
        x, y, c = _coords()
        out = []
        for a in range(na):
            half = gs[a].shape[1] // 2
            out.append(_remote(g_refs[a].at[:, pl.ds((1 - c) * half, half), :], o_refs[a], sems[0], sems[1], a,
                               (x, y, 1 - c)))
        return out

    def start(g_refs, o_refs, sems):
        for cp in copies(g_refs, o_refs, sems):
            cp.start()

    def wait(g_refs, o_refs, sems):
        for cp in copies(g_refs, o_refs, sems):
            cp.wait()

    return Exchange(gs, [SDS((g.shape[0], g.shape[1] // 2, g.shape[2]), g.dtype) for g in gs],
                    [pltpu.SemaphoreType.DMA((na,)), pltpu.SemaphoreType.DMA((na,))], start, wait)


def chip_partials_exchange(pbs):
    na = len(pbs)

    def copies(p_refs, o_refs, sems):
        x, y, c = _coords()
        out = []
        for a in range(na):
            for k, (px, py) in enumerate(_other_chips(x, y)):
                out.append(_remote(p_refs[a].at[2 * px + py], o_refs[a].at[k], sems[0], sems[1], 3 * a + k, (px, py, c)))
        return out

    def start(p_refs, o_refs, sems):
        for cp in copies(p_refs, o_refs, sems):
            cp.start()

    def wait(p_refs, o_refs, sems):
        for cp in copies(p_refs, o_refs, sems):
            cp.wait()

    return Exchange(pbs, [SDS((3,) + p.shape[1:], p.dtype) for p in pbs],
                    [pltpu.SemaphoreType.DMA((3 * na,)), pltpu.SemaphoreType.DMA((3 * na,))], start, wait)


def add_sibling_half(g, land, c_idx, *, name):
    n, R, C = g.shape
    half = R // 2
    tr = _pick(half, 256)
    nt = half // tr

    def body(c_ref, g_ref, l_ref, of_ref, ob_ref):
        s = g_ref[...] + l_ref[...].astype(F32)
        of_ref[...] = s
        ob_ref[...] = s.astype(ob_ref.dtype)

    blk = pl.BlockSpec((1, tr, C), lambda s, i, c_ref: (s, i, 0))
    gblk = pl.BlockSpec((1, tr, C), lambda s, i, c_ref: (s, c_ref[0] * nt + i, 0))
    return pl.pallas_call(
        body, name=name,
        grid_spec=pltpu.PrefetchScalarGridSpec(num_scalar_prefetch=1, grid=(n, nt), in_specs=[gblk, blk],
                                               out_specs=[blk, blk]),
        out_shape=[SDS((n, half, C), F32), SDS((n, half, C), WIRE_DTYPE)],
        compiler_params=_cp("parallel", "parallel"))(c_idx, g, land)


def add_chip_partials(pf, rb, jc_idx, *, name):
    n, H, C = pf.shape
    tr = _pick(H, 256)

    def body(jc_ref, p_ref, r_ref, o_ref):
        s = p_ref[0]
        for k in range(3):
            s = s + r_ref[k].astype(F32)
        o_ref[...] = s

    pblk = pl.BlockSpec((1, tr, C), lambda i, jc_ref: (jc_ref[0], i, 0))
    rblk = pl.BlockSpec((3, tr, C), lambda i, jc_ref: (0, i, 0))
    oblk = pl.BlockSpec((None, tr, C), lambda i, jc_ref: (jc_ref[1], i, 0))
    return pl.pallas_call(
        body, name=name,
        grid_spec=pltpu.PrefetchScalarGridSpec(num_scalar_prefetch=1, grid=(H // tr,), in_specs=[pblk, rblk],
                                               out_specs=oblk),
        out_shape=SDS((2, H, C), F32), compiler_params=_cp("parallel"))(jc_idx, pf, rb)


def join_sibling_halves(bufs, *, name):
    na = len(bufs)

    def body(*refs):
        o_refs = refs[na:2 * na]
        send_sems, recv_sems = refs[2 * na:]
        x, y, c = _coords()
        cps = [_remote(o_refs[a].at[c], o_refs[a].at[c], send_sems, recv_sems, a, (x, y, 1 - c)) for a in range(na)]
        for cp in cps:
            cp.start()
        for a in range(na):
            cps[a].wait_send()
            _remote(o_refs[a].at[1 - c], o_refs[a].at[1 - c], send_sems, recv_sems, a, (x, y, 1 - c)).wait_recv()

    return pl.pallas_call(body, name=name, in_specs=[ANY] * na, out_specs=[ANY] * na,
                          out_shape=[SDS(b.shape, b.dtype) for b in bufs],
                          input_output_aliases={a: a for a in range(na)},
                          scratch_shapes=[pltpu.SemaphoreType.DMA((na,)), pltpu.SemaphoreType.DMA((na,))])(*bufs)


def exchange_pieces(v, *, scatter, name):
    P, C = v.shape[-2:]

    def body(v_ref, o_ref, send_sems, recv_sems, local_sem):
        x, y, c = _coords()
        me = 4 * x + 2 * y + c
        local = pltpu.make_async_copy(v_ref.at[me] if scatter else v_ref, o_ref.at[me], local_sem)
        local.start()
        cps = []
        for m in range(1, N_DEV):
            px = (1 - x) if m & 4 else x
            py = (1 - y) if m & 2 else y
            pc = (1 - c) if m & 1 else c
            src = v_ref.at[4 * px + 2 * py + pc] if scatter else v_ref
            cps.append(_remote(src, o_ref.at[me], send_sems, recv_sems, m - 1, (px, py, pc)))
        for cp in cps:
            cp.start()
        for cp in cps:
            cp.wait_send()
        for m in range(1, N_DEV):
            px = (1 - x) if m & 4 else x
            py = (1 - y) if m & 2 else y
            pc = (1 - c) if m & 1 else c
            slot = o_ref.at[4 * px + 2 * py + pc]
            _remote(slot, slot, send_sems, recv_sems, m - 1, (px, py, pc)).wait_recv()
        local.wait()

    return pl.pallas_call(body, name=name, in_specs=[ANY], out_specs=ANY, out_shape=SDS((N_DEV, P, C), v.dtype),
                          scratch_shapes=[pltpu.SemaphoreType.DMA((N_DEV - 1,)), pltpu.SemaphoreType.DMA((N_DEV - 1,)),
                                          pltpu.SemaphoreType.DMA(())])(v)


def sum_pieces(land, *, name):
    n, P, C = land.shape

    def body(l_ref, o_ref):
        s = l_ref[0]
        for d in range(1, n):
            s = s + l_ref[d]
        o_ref[...] = s

    return pl.pallas_call(body, name=name, out_shape=SDS((P, C), F32))(land)


BIG_SEGS = (
    ("w_in_even", (1024, 514), 1),
    ("s5_w_glu", (128, 512), 0),
    ("w_out_even", (256, 1024), 0),
    ("w_in_odd", (1024, 384), 1),
    ("w_out_odd", (256, 1024), 0),
    ("mlp_w1", (2, 1024, 1024), 2),
    ("mlp_w2", (2, 1024, 1024), 1),
)
BIG_NAMES = tuple(n for n, _, _ in BIG_SEGS)
EARLY_NAMES = ("w_in_even", "s5_w_glu", "w_out_even")
LATE_NAMES = ("w_in_odd", "w_out_odd", "mlp_w1", "mlp_w2")
REDUCED_EARLY = ("s5_w_glu", "w_out_even", "w_in_odd", "w_out_odd", "mlp_w1", "mlp_w2")
SHARDED_SMALL = ("pool_scale", "sgu_ln_g", "sgu_ln_b")
SMALL_SEGS = (
    ("mix_pre_g", (2, 1024)), ("mix_post_g", (2, 1024)), ("mlp_pre_g", (2, 1024)), ("mlp_post_g", (2, 1024)),
    ("s5_lam_re", (1, 32, 64)), ("s5_lam_im", (1, 32, 64)), ("s5_log_dt", (1, 32)),
    ("s5_b_re", (1, 32, 64, 16)), ("s5_b_im", (1, 32, 64, 16)), ("s5_c_re", (1, 32, 16, 64)), ("s5_c_im", (1, 32, 16, 64)),
    ("s5_d", (1, 512)), ("fox_b_f", (1, 8)), ("pool_w", (1, 4, 128, 128)), ("sgu_w_s", (1, 4, 128, 128)),
    ("sgu_b_s", (1, 4, 128)),
)
REDUCED_SEGS = SMALL_SEGS + tuple((n, (1, 512)) for n in SHARDED_SMALL)


def _cols_from_chips(g):
    n, R, C = g.shape
    return jnp.transpose(g, (1, 0, 2)).reshape(R, n * C)


def _chips_from_cols(m):
    R, C4 = m.shape
    return jnp.transpose(m.reshape(R, N_CHIPS, C4 // N_CHIPS), (1, 0, 2))


MLP_SHARD = 1024


def _w1_cols(l):
    def spec(tm, tn, tk):
        per = MLP_SHARD // tn
        return pl.BlockSpec((None, tk, tn), lambda i, j, k: (j // per, l * (MLP_SHARD // tk) + k, j % per))
    return spec


def _w1_rows_t(l):
    def spec(tm, tn, tk):
        per = MLP_SHARD // tk
        return pl.BlockSpec((None, tn, tk), lambda i, j, k: (k // per, l * (MLP_SHARD // tn) + j, k % per))
    return spec


def _w2_rows(l):
    def spec(tm, tn, tk):
        if tk == N_CHIPS * MLP_SHARD:
            return pl.BlockSpec((N_CHIPS, MLP_SHARD, tn), lambda i, j, k: (0, l, j))
        per = MLP_SHARD // tk
        return pl.BlockSpec((None, tk, tn), lambda i, j, k: (k // per, l * per + k % per, j))
    return spec


def _w2_rows_t(l):
    def spec(tm, tn, tk):
        per = MLP_SHARD // tn
        return pl.BlockSpec((None, tn, tk), lambda i, j, k: (j // per, l * per + j % per, k))
    return spec


def _dw1_out(l):
    def spec(tm, tn, tk):
        per = MLP_SHARD // tn
        return pl.BlockSpec((None, tm, tn), lambda i, j, k: (j // per, l * (MLP_SHARD // tm) + i, j % per))
    return spec


def _dw2_out(l):
    def spec(tm, tn, tk):
        per = MLP_SHARD // tm
        return pl.BlockSpec((None, tm, tn), lambda i, j, k: (i // per, l * per + i % per, j))
    return spec


def _pack_vec(d, segs, rows_multiple):
    flat = jnp.concatenate([d[n].reshape(-1) for n, _ in segs])
    rows = -(-flat.shape[0] // LANES)
    rows = -(-rows // rows_multiple) * rows_multiple
    return jnp.pad(flat, (0, rows * LANES - flat.shape[0])).reshape(rows, LANES)


def _unpack_vec(v, segs):
    flat, out, r = v.reshape(-1), {}, 0
    for n, shape in segs:
        k = math.prod(shape)
        out[n] = flat[r:r + k].reshape(shape)
        r += k
    return out


def _block_diag(blocks):
    G, a, b = blocks.shape
    eye = jnp.eye(G, dtype=blocks.dtype)
    return (eye[:, None, :, None] * blocks[:, :, None, :]).reshape(G * a, G * b)


def _diag_blocks(m, G):
    a, b = m.shape[0] // G, m.shape[1] // G
    return jnp.stack([m[g * a:(g + 1) * a, g * b:(g + 1) * b] for g in range(G)])


def _sqrelu_epi(acc):
    r = jnp.maximum(acc, 0.0)
    return acc, r * r


def _sqrelu_bwd_epi(acc, a):
    return (acc * (2.0 * jnp.maximum(a.astype(F32), 0.0)),)


def _mlp_fwd(x, g1, g2, l, g_pre, g_post, tag):
    T, D = x.shape
    h = rms_fwd(x, g_pre, name=f"{tag}_pre_norm")
    a, s = matmul(h, g1, name=f"{tag}_up", mnk=(T, D_FF, D), b_spec=_w1_cols(l), epi=_sqrelu_epi,
                  out_dtypes=(MXU_DTYPE, MXU_DTYPE))
    m = matmul(s, g2, name=f"{tag}_down", mnk=(T, D, D_FF), b_spec=_w2_rows(l))
    return rms_res_fwd(x, m, g_post, name=f"{tag}_post_norm"), (x, h, a, s, m)


def _mlp_bwd(saved, g1, g2, l, g_pre, g_post, dxo, dg1, dg2, tag):
    x, h, a, s, m = saved
    T, D = x.shape
    gshape = (N_CHIPS, 2 * MLP_SHARD, MLP_SHARD)
    dm, dg_post = rms_bwd(m, g_post, dxo, None, name=f"{tag}_post_norm_bwd")
    da = matmul(dm, g2, tb=True, name=f"{tag}_down_dx", mnk=(T, D_FF, D), b_spec=_w2_rows_t(l),
                epi=_sqrelu_bwd_epi, epi_in=(a,), out_dtype=MXU_DTYPE)
    dg2 = matmul(s, dm, ta=True, name=f"{tag}_down_dw", o_spec=_dw2_out(l), o_shape=gshape, prev=dg2)
    dh = matmul(da, g1, tb=True, name=f"{tag}_up_dx", mnk=(T, D, D_FF), b_spec=_w1_rows_t(l), tk=MLP_SHARD)
    dg1 = matmul(h, da, ta=True, name=f"{tag}_up_dw", o_spec=_dw1_out(l), o_shape=gshape, prev=dg1)
    dx, dg_pre = rms_bwd(x, g_pre, dh, dxo, name=f"{tag}_pre_norm_bwd")
    return dx, dg1, dg2, dg_pre, dg_post


def kernel(x, mix_pre_g, mix_post_g, mlp_pre_g, mlp_post_g, w_in_even, s5_lam_re, s5_lam_im, s5_log_dt, s5_b_re, s5_b_im, s5_c_re, s5_c_im, s5_d, s5_w_glu, fox_b_f, w_out_even, w_in_odd, pool_w, pool_scale, sgu_ln_g, sgu_ln_b, sgu_w_s, sgu_b_s, w_out_odd, mlp_w1, mlp_w2, loss_target, m_mix_pre_g, m_mix_post_g, m_mlp_pre_g, m_mlp_post_g, m_w_in_even, m_s5_lam_re, m_s5_lam_im, m_s5_log_dt, m_s5_b_re, m_s5_b_im, m_s5_c_re, m_s5_c_im, m_s5_d, m_s5_w_glu, m_fox_b_f, m_w_out_even, m_w_in_odd, m_pool_w, m_pool_scale, m_sgu_ln_g, m_sgu_ln_b, m_sgu_w_s, m_sgu_b_s, m_w_out_odd, m_mlp_w1, m_mlp_w2, v_mix_pre_g, v_mix_post_g, v_mlp_pre_g, v_mlp_post_g, v_w_in_even, v_s5_lam_re, v_s5_lam_im, v_s5_log_dt, v_s5_b_re, v_s5_b_im, v_s5_c_re, v_s5_c_im, v_s5_d, v_s5_w_glu, v_fox_b_f, v_w_out_even, v_w_in_odd, v_pool_w, v_pool_scale, v_sgu_ln_g, v_sgu_ln_b, v_sgu_w_s, v_sgu_b_s, v_w_out_odd, v_mlp_w1, v_mlp_w2):
    names = [n for n, _ in SMALL_SEGS] + [n for n, _, _ in BIG_SEGS] + list(SHARDED_SMALL)
    env = dict(locals())
    W = {n: env[n] for n in names}
    M = {n: env["m_" + n] for n in names}
    V = {n: env["v_" + n] for n in names}

    def shard(n):
        return W[n].reshape(-1, W[n].shape[-1]).astype(WIRE_DTYPE)

    small = jnp.pad(jnp.concatenate([W[n] for n in SHARDED_SMALL]), ((0, SUBLANES - len(SHARDED_SMALL)), (0, 0)))
    gathered, small_all = allgather_chip_shards([shard(n) for n in EARLY_NAMES], small, name="allgather_weights")
    Wf = dict(zip(EARLY_NAMES, gathered))
    for i, n in enumerate(SHARDED_SMALL):
        Wf[n] = small_all[:, i, :].reshape(1, N_CHIPS * LANES)
    for n, _ in SMALL_SEGS:
        Wf[n] = W[n]

    loss8, dx0, halves, dw_in_e, local_small = _local_step(x[0], loss_target[0], Wf, [shard(n) for n in LATE_NAMES])
    loss = lax.psum(loss8[0, 0], MESH_AXES)
    return _reduce_and_update(W, M, V, loss, dx0, halves, dw_in_e, local_small)


def _reduce_to_my_half(gs, names, tag, carry_swap=None, carry_ici=None):
    cx, cy, cc = _coords()
    c_idx = cc.reshape(1).astype(jnp.int32)
    jc_idx = jnp.stack([2 * cx + cy, cc]).astype(jnp.int32)
    swap = swap_halves_exchange(gs)
    from_sibling = carry_swap(swap) if carry_swap else run_exchange(swap, name=f"{tag}_to_sibling")
    sums = [add_sibling_half(g, l, c_idx, name=f"{tag}_chip_sum_{n}") for n, g, l in zip(names, gs, from_sibling)]
    send = chip_partials_exchange([pb for _, pb in sums])
    from_chips = carry_ici(send) if carry_ici else run_exchange(send, name=f"{tag}_to_chips")
    return [add_chip_partials(pf, r, jc_idx, name=f"{tag}_sum_{n}") for n, (pf, _), r in zip(names, sums, from_chips)]


def _local_step(x0, target, P, late_shards):
    T = x0.shape[0]
    mix_pre_g, mix_post_g, mlp_pre_g, mlp_post_g = P["mix_pre_g"], P["mix_post_g"], P["mlp_pre_g"], P["mlp_post_g"]
    s5_lam_re, s5_lam_im, s5_log_dt = P["s5_lam_re"], P["s5_lam_im"], P["s5_log_dt"]
    s5_b_re, s5_b_im, s5_c_re, s5_c_im, s5_d = P["s5_b_re"], P["s5_b_im"], P["s5_c_re"], P["s5_c_im"], P["s5_d"]
    fox_b_f, pool_w, sgu_w_s, sgu_b_s = P["fox_b_f"], P["pool_w"], P["sgu_w_s"], P["sgu_b_s"]
    pool_scale_f, ln_g_f, ln_b_f = P["pool_scale"], P["sgu_ln_g"], P["sgu_ln_b"]
    w_in_e = jnp.pad(_cols_from_chips(P["w_in_even"]), ((0, 0), (0, EVEN_IN_PAD - EVEN_IN)))
    w_glu = P["s5_w_glu"].reshape(S5_WIDTH, S5_WIDTH)
    w_out_e = P["w_out_even"].reshape(D_MODEL, D_MODEL)

    def gain(a, l):
        return a[l][None, :]

    lr = s5_lam_re[0].reshape(1, S5_LANES)
    li = s5_lam_im[0].reshape(1, S5_LANES)
    ldt = jnp.repeat(s5_log_dt[0], S5_STATE).reshape(1, S5_LANES)
    btr = s5_b_re[0].reshape(S5_LANES, S5_GROUP).T
    bti = s5_b_im[0].reshape(S5_LANES, S5_GROUP).T
    tf_re, tf_im, tb_re, tb_im, bbt_re, bbt_im = s5_disc_fwd(lr, li, ldt, btr, bti, name="s5_disc")
    same_group = (jnp.arange(S5_WIDTH)[:, None] // S5_GROUP) == (jnp.arange(S5_LANES)[None, :] // S5_STATE)
    b_bd = s5_interleave(jnp.where(same_group, jnp.tile(bbt_re, (S5_GROUPS, 1)), 0.0),
                         jnp.where(same_group, jnp.tile(bbt_im, (S5_GROUPS, 1)), 0.0), axis=1)
    cr2 = jnp.transpose(s5_c_re[0], (0, 2, 1)).reshape(S5_LANES, S5_GROUP)
    ci2 = jnp.transpose(s5_c_im[0], (0, 2, 1)).reshape(S5_LANES, S5_GROUP)
    c_bd = s5_interleave(jnp.where(same_group.T, jnp.tile(cr2, (1, S5_GROUPS)), 0.0),
                         -jnp.where(same_group.T, jnp.tile(ci2, (1, S5_GROUPS)), 0.0), axis=0)
    bf_pad = jnp.pad(fox_b_f, ((0, 0), (0, LANES - FOX_HEADS)))

    h1 = rms_fwd(x0, gain(mix_pre_g, 0), name="l0_pre_norm")
    z = matmul(h1, w_in_e, name="l0_in_proj")
    bu = matmul(z, b_bd, mnk=(T, 2 * S5_LANES, S5_CB), tn=S5_NB, a_spec=_chan_rows, b_spec=_s5_b_block,
                name="s5_bu")
    xs = s5_scan(bu, tf_re, tf_im, reverse=False, name="s5_scan_fwd")
    yc = matmul(xs, c_bd, mnk=(T, S5_WIDTH, 2 * S5_NB), tn=S5_CB, a_spec=_lanes_of_chan, b_spec=_s5_c_block,
                name="s5_cx")
    yl, yg = s5_out_fwd(yc, z, s5_d, name="s5_out")
    gl = matmul(yg, w_glu, name="s5_glu_proj")
    ycat = glu_fwd(yg, gl, out_cols=D_MODEL, name="s5_glu")
    fgate = fox_gate_fwd(z, bf_pad, fl_col=FL_TILE, name="fox_gate")
    f_col = _pairs_col(fgate, T)
    f_row = _col_to_row(f_col, T)
    (ycat, lse_col), late = fox_fwd(z, f_col, f_row, ycat, allgather_ici_exchange(late_shards), name="fox_fwd")
    late = dict(zip(LATE_NAMES, allgather_forward(late, name="allgather_late_weights")))
    w_in_o = _cols_from_chips(late["w_in_odd"])
    w_in_o = jnp.concatenate([w_in_o[:, S5_WIDTH:], w_in_o[:, :S5_WIDTH]], axis=1)
    w_out_o = late["w_out_odd"].reshape(D_MODEL, D_MODEL)
    g1, g2 = late["mlp_w1"], late["mlp_w2"]
    mo = matmul(ycat, w_out_e, name="l0_out_proj")
    x1 = rms_res_fwd(x0, mo, gain(mix_post_g, 0), name="l0_post_norm")
    x2, mlp0 = _mlp_fwd(x1, g1, g2, 0, gain(mlp_pre_g, 0), gain(mlp_post_g, 0), "mlp0")

    h3 = rms_fwd(x2, gain(mix_pre_g, 1), name="l1_pre_norm")
    z2 = matmul(h3, w_in_o, name="l1_in_proj")
    pooled = pool_window(z2, adjoint=False, in_col=POOL_COL, name="pool_fwd")
    pw_bd = _block_diag(pool_w[0])
    pw = matmul(pooled, pw_bd, name="pool_proj")
    ycat2 = colscale_fwd(pw, pool_scale_f, out_cols=D_MODEL, name="pool_scale")
    causal = jnp.tril(jnp.ones((CHUNK, CHUNK), dtype=bool))
    wsm = jnp.where(causal[None], sgu_w_s[0], 0.0)
    wsmt = jnp.transpose(wsm, (0, 2, 1))
    bst = sgu_b_s[0].T
    ycat2 = sgu_fwd(z2, ln_g_f, ln_b_f, wsm, bst, ycat2, name="sgu_fwd")
    mo2 = matmul(ycat2, w_out_o, name="l1_out_proj")
    x3 = rms_res_fwd(x2, mo2, gain(mix_post_g, 1), name="l1_post_norm")
    x4, mlp1 = _mlp_fwd(x3, g1, g2, 1, gain(mlp_pre_g, 1), gain(mlp_post_g, 1), "mlp1")

    loss8, dx4 = loss_fwd_bwd(x4, target, name="loss")

    dx3, dg1, dg2, dg_mlp_pre1, dg_mlp_post1 = _mlp_bwd(mlp1, g1, g2, 1, gain(mlp_pre_g, 1), gain(mlp_post_g, 1), dx4,
                                                        None, None, "mlp1")
    dmo2, dg_mix_post1 = rms_bwd(mo2, gain(mix_post_g, 1), dx3, None, name="l1_post_norm_bwd")
    dycat2 = matmul(dmo2, w_out_o, tb=True, name="l1_out_proj_dx")
    dw_out_o = matmul(ycat2, dmo2, ta=True, name="l1_out_proj_dw")
    dpw, dpool_scale = colscale_bwd(pw, pool_scale_f, dycat2, name="pool_scale_bwd")
    dpooled = matmul(dpw, pw_bd, tb=True, name="pool_proj_dx")
    dpw_bd = matmul(pooled, dpw, ta=True, name="pool_proj_dw")
    dz2, dln_g, dln_b, dws, dbst = sgu_bwd(z2, ln_g_f, ln_b_f, wsm, wsmt, bst, dycat2, out_cols=3 * S5_WIDTH,
                                           name="sgu_bwd")
    dz2 = pool_window(dpooled, adjoint=True, into=dz2, out_col=POOL_COL, name="pool_bwd")
    dh3 = matmul(dz2, w_in_o, tb=True, name="l1_in_proj_dx")
    dw_in_o = matmul(h3, dz2, ta=True, name="l1_in_proj_dw")
    dw_in_o = jnp.concatenate([dw_in_o[:, 2 * S5_WIDTH:], dw_in_o[:, :2 * S5_WIDTH]], axis=1)
    dx2, dg_mix_pre1 = rms_bwd(x2, gain(mix_pre_g, 1), dh3, dx3, name="l1_pre_norm_bwd")

    dx1, dg1, dg2, dg_mlp_pre0, dg_mlp_post0 = _mlp_bwd(mlp0, g1, g2, 0, gain(mlp_pre_g, 0), gain(mlp_post_g, 0), dx2,
                                                        dg1, dg2, "mlp0")
    dmo, dg_mix_post0 = rms_bwd(mo, gain(mix_post_g, 0), dx1, None, name="l0_post_norm_bwd")
    dycat = matmul(dmo, w_out_e, tb=True, name="l0_out_proj_dx")
    dw_out_e = matmul(ycat, dmo, ta=True, name="l0_out_proj_dw")
    dyg_a, dgl = glu_bwd(yg, gl, dycat, name="s5_glu_bwd")
    dyg_b = matmul(dgl, w_glu, tb=True, name="s5_glu_proj_dx")
    dw_glu = matmul(yg, dgl, ta=True, name="s5_glu_proj_dw")
    dyl, du_skip, dd = s5_out_bwd(yl, z, s5_d, dyg_a, dyg_b, name="s5_out_bwd")
    dxs = matmul(dyl, c_bd, tb=True, mnk=(T, 2 * S5_LANES, S5_CB), tn=S5_NB, a_spec=_chan_rows, b_spec=_s5_c_block_t,
                 name="s5_cx_dx")
    dc_blocks = matmul(xs, dyl, ta=True, mnk=(2 * S5_LANES, S5_CB, T), tn=S5_CB, b_spec=_chan_cols_of_i,
                       name="s5_cx_dw")
    lam = s5_scan(dxs, tb_re, tb_im, reverse=True, name="s5_scan_bwd")
    dab_re, dab_im = s5_da(lam, xs, name="s5_da")
    db_blocks = matmul(z, lam, ta=True, mnk=(S5_CB, 2 * S5_LANES, T), tm=S5_CB, tn=S5_NB, a_spec=_chan_rows_t,
                       name="s5_bu_dw")
    du_b = matmul(lam, b_bd, tb=True, mnk=(T, S5_WIDTH, 2 * S5_NB), tn=S5_CB, a_spec=_lanes_of_chan,
                  b_spec=_s5_b_block_t, name="s5_bu_dx")
    du = add2(du_skip, du_b, name="s5_du")
    early_grads = {"s5_w_glu": dw_glu.reshape(N_CHIPS, -1, S5_WIDTH), "w_out_even": dw_out_e.reshape(N_CHIPS, -1, D_MODEL),
                   "w_in_odd": _chips_from_cols(dw_in_o), "w_out_odd": dw_out_o.reshape(N_CHIPS, -1, D_MODEL),
                   "mlp_w1": dg1, "mlp_w2": dg2}
    fox = {}

    def attention_bwd_q(exchange):
        (fox["dq"], fox["dd"], fox["dfq"]), bufs = fox_bwd_q(z, ycat, dycat, f_col, f_row, lse_col, exchange, name="fox_bwd_q")
        return bufs

    def attention_bwd_kv(exchange):
        (fox["dk"], fox["dv"], fox["df"]), bufs = fox_bwd_kv(z, dycat, f_col, f_row, _col_to_row(lse_col, T),
                                                             _col_to_row(fox["dd"], T), fox["dfq"], exchange, name="fox_bwd_kv")
        return bufs

    halves = _reduce_to_my_half([early_grads[n] for n in REDUCED_EARLY], REDUCED_EARLY, "early_grads",
                                attention_bwd_q, attention_bwd_kv)
    dq, dk, dv = fox["dq"], fox["dk"], fox["dv"]
    dfl, dbf = fox_gate_bwd(z, bf_pad, _pairs_to_lanes(fox["df"], T), fl_col=FL_TILE, name="fox_gate_bwd")
    dz = jnp.concatenate([du, dq, dk, dv, dfl], axis=1)
    dh1 = matmul(dz, w_in_e, tb=True, name="l0_in_proj_dx")
    dw_in_e = matmul(h1, dz, ta=True, name="l0_in_proj_dw")[:, :EVEN_IN]
    dx0, dg_mix_pre0 = rms_bwd(x0, gain(mix_pre_g, 0), dh1, dx1, name="l0_pre_norm_bwd")

    groups_per_block = S5_CB // S5_GROUP
    own_group = (jnp.arange(S5_CB)[:, None] // S5_GROUP) == ((jnp.arange(S5_LANES)[None, :] // S5_STATE) % groups_per_block)
    db_re, db_im = s5_deinterleave(db_blocks, axis=1)
    dbbt_re = jnp.where(own_group, db_re, 0.0).reshape(groups_per_block, S5_GROUP, S5_LANES).sum(0)
    dbbt_im = jnp.where(own_group, db_im, 0.0).reshape(groups_per_block, S5_GROUP, S5_LANES).sum(0)
    dlr, dli, dldt8, dbtr, dbti = s5_disc_bwd(lr, li, ldt, btr, bti, dab_re, dab_im, dbbt_re, dbbt_im, name="s5_disc_bwd")
    dc_re, dc_im = s5_deinterleave(dc_blocks, axis=0)
    dcr2 = jnp.where(own_group.T, dc_re, 0.0).reshape(S5_LANES, groups_per_block, S5_GROUP).sum(1)
    dci2 = -jnp.where(own_group.T, dc_im, 0.0).reshape(S5_LANES, groups_per_block, S5_GROUP).sum(1)

    def c_layout(a):
        return jnp.transpose(a.reshape(S5_GROUPS, S5_STATE, S5_GROUP), (0, 2, 1))[None]

    def b_layout(a):
        return a.T.reshape(1, S5_GROUPS, S5_STATE, S5_GROUP)

    local_small = {
        "mix_pre_g": jnp.concatenate([dg_mix_pre0, dg_mix_pre1]), "mix_post_g": jnp.concatenate([dg_mix_post0, dg_mix_post1]),
        "mlp_pre_g": jnp.concatenate([dg_mlp_pre0, dg_mlp_pre1]), "mlp_post_g": jnp.concatenate([dg_mlp_post0, dg_mlp_post1]),
        "s5_lam_re": dlr.reshape(1, S5_GROUPS, S5_STATE), "s5_lam_im": dli.reshape(1, S5_GROUPS, S5_STATE),
        "s5_log_dt": dldt8[0:1, 0:S5_GROUPS],
        "s5_b_re": b_layout(dbtr), "s5_b_im": b_layout(dbti), "s5_c_re": c_layout(dcr2), "s5_c_im": c_layout(dci2),
        "s5_d": dd, "fox_b_f": dbf[:, 0:FOX_HEADS],
        "pool_w": _diag_blocks(dpw_bd, len(POOL_WINDOWS))[None],
        "sgu_w_s": jnp.where(causal[None], dws, 0.0)[None], "sgu_b_s": dbst.T[None],
        "pool_scale": dpool_scale, "sgu_ln_g": dln_g, "sgu_ln_b": dln_b,
    }
    return loss8, dx0, dict(zip(REDUCED_EARLY, halves)), _chips_from_cols(dw_in_e), local_small


def _reduce_and_update(W, M, V, loss, dx0, halves, dw_in_e, local_small):
    cx, cy, cc = _coords()
    chip = 2 * cx + cy

    vec = _pack_vec(local_small, REDUCED_SEGS, N_DEV * SUBLANES)
    piece = vec.shape[0] // N_DEV
    landed = exchange_pieces(vec.reshape(N_DEV, piece, LANES), scatter=True, name="small_grads_scatter")
    mine = sum_pieces(landed, name="small_grads_sum")
    everyone = exchange_pieces(mine, scatter=False, name="small_grads_gather")
    G = _unpack_vec(everyone, REDUCED_SEGS)
    for n in SHARDED_SMALL:
        G[n] = lax.dynamic_slice_in_dim(G[n], chip * LANES, LANES, axis=1)

    halves = dict(halves)
    halves["w_in_even"] = _reduce_to_my_half([dw_in_e], ["w_in_even"], "late_grads")[0]
    reduced = join_sibling_halves([halves[n] for n in BIG_NAMES], name="big_grads_join")
    for n, r in zip(BIG_NAMES, reduced):
        G[n] = r.reshape(W[n].shape)

    def two_d(a):
        return a.reshape(-1, a.shape[-1])

    delta, new_m, new_v = {}, {}, {}
    for n in BIG_NAMES:
        d_, m_, v_ = adamw(two_d(W[n]), two_d(G[n]), two_d(M[n]), two_d(V[n]), name=f"adamw_{n}")
        delta[n], new_m[n], new_v[n] = (t.reshape(W[n].shape) for t in (d_, m_, v_))
    packed = [_pack_vec(src, SMALL_SEGS, SUBLANES) for src in (W, G, M, V)]
    outs = adamw(*packed, name="adamw_replicated")
    for dst, t in zip((delta, new_m, new_v), outs):
        dst.update(_unpack_vec(t, SMALL_SEGS))
    sharded_segs = tuple((n, (1, LANES)) for n in SHARDED_SMALL)
    packed = [_pack_vec(src, sharded_segs, 1) for src in (W, G, M, V)]
    outs = adamw(*packed, name="adamw_sharded_vectors")
    for dst, t in zip((delta, new_m, new_v), outs):
        dst.update(_unpack_vec(t, sharded_segs))

    order = ["mix_pre_g", "mix_post_g", "mlp_pre_g", "mlp_post_g", "w_in_even", "s5_lam_re", "s5_lam_im", "s5_log_dt",
             "s5_b_re", "s5_b_im", "s5_c_re", "s5_c_im", "s5_d", "s5_w_glu", "fox_b_f", "w_out_even", "w_in_odd",
             "pool_w", "pool_scale", "sgu_ln_g", "sgu_ln_b", "sgu_w_s", "sgu_b_s", "w_out_odd", "mlp_w1", "mlp_w2"]
    return (loss, dx0[None], *[G[n] for n in order], *[delta[n] for n in order],
            *[new_m[n] for n in order], *[new_v[n] for n in order])
```

```python
import functools
import math

import jax
import jax.numpy as jnp
from jax import lax
from jax.experimental import pallas as pl
from jax.experimental.pallas import tpu as pltpu

F32 = jnp.float32
MXU_DTYPE = jnp.bfloat16
WIRE_DTYPE = jnp.bfloat16
EPS = 1e-6
VMEM_LIMIT_BYTES = 48 * 1024 * 1024
LANES = 128
SUBLANES = 8

D_MODEL = 1024
S5_WIDTH = 512
S5_GROUP = 16
S5_GROUPS = 32
S5_STATE = 64
S5_LANES = S5_GROUPS * S5_STATE
FOX_HEADS = 8
FOX_HEAD_DIM = 64
FOX_WIDTH = 512
EVEN_IN = S5_WIDTH + 3 * FOX_WIDTH + FOX_HEADS
EVEN_IN_PAD = 2176
POOL_WINDOWS = (2, 4, 8, 16)
POOL_HALO = 16
POOL_GROUP_DIM = 128
SGU_GROUPS = 4
SGU_GROUP_DIM = 128
CHUNK = 128
D_FF = 4096

ADAM_LR = 0.001
ADAM_B1 = 0.9
ADAM_B2 = 0.999
ADAM_EPS = 1e-08
ADAM_WD = 0.01
ADAM_STEP = 10

MESH_AXES = ("x", "y", "c")
MESH = pl.DeviceIdType.MESH
N_CHIPS = 4
N_DEV = 8

SDS = jax.ShapeDtypeStruct


def _cp(*sem):
    return pltpu.CompilerParams(dimension_semantics=sem, vmem_limit_bytes=VMEM_LIMIT_BYTES)


def _pick(dim, pref):
    if dim <= pref:
        return dim
    t = pref
    while t >= 256:
        if dim % t == 0:
            return t
        t //= 2
    return dim


def _row(tr, c):
    return pl.BlockSpec((tr, c), lambda i: (i, 0))


def _full(shape):
    nd = len(shape)
    return pl.BlockSpec(shape, lambda *_: (0,) * nd)


def _gelu_grad(x):
    c = math.sqrt(2.0 / math.pi)
    t = jnp.tanh(c * (x + 0.044715 * x * x * x))
    return 0.5 * (1.0 + t) + 0.5 * x * (1.0 - t * t) * c * (1.0 + 3.0 * 0.044715 * x * x)


MATMUL_VMEM_BYTES = 36 * 1024 * 1024


def matmul(a, b, *, name, ta=False, tb=False, out_dtype=F32, tm=2048, tn=1024, tk=4096, mnk=None, a_koff=0,
           a_spec=None, b_spec=None, o_spec=None, o_shape=None, prev=None, epi=None, epi_in=(), out_dtypes=None,
           exact_tiles=False):
    if mnk is None:
        M, K = (a.shape[1], a.shape[0]) if ta else a.shape
        K2, N = (b.shape[1], b.shape[0]) if tb else b.shape
        assert K == K2, (a.shape, b.shape, ta, tb)
    else:
        M, N, K = mnk
    out_dtypes = tuple(out_dtypes) if out_dtypes is not None else (out_dtype,)
    n_out, n_epi = len(out_dtypes), len(epi_in)
    tm, tn, tk = _pick(M, tm), _pick(N, tn), _pick(K, tk)

    def vmem_bytes(tm_, tn_, tk_):
        tiles = tm_ * tk_ * a.dtype.itemsize + tk_ * tn_ * b.dtype.itemsize
        tiles += tm_ * tn_ * (sum(jnp.dtype(d).itemsize for d in out_dtypes) + sum(e.dtype.itemsize for e in epi_in))
        return 2 * tiles + tm_ * tn_ * 4 * (tk_ < K)

    def halves(t, dim):
        return [t] + ([t // 2] if t % (2 * LANES) == 0 and t // 2 >= 512 and dim % (t // 2) == 0 else [])

    if exact_tiles:
        halves = lambda t, dim: [t]
    fits = [(m_, n_) for m_ in halves(tm, M) for n_ in halves(tn, N) if vmem_bytes(m_, n_, tk) <= MATMUL_VMEM_BYTES]
    if fits:
        tm, tn = max(fits, key=lambda t: (t[0] * t[1], t[0]))
    else:
        tm, tn = halves(tm, M)[-1], halves(tn, N)[-1]
        while vmem_bytes(tm, tn, tk) > MATMUL_VMEM_BYTES and tk % 2 == 0 and tk > 512:
            tk //= 2
    nk = K // tk
    assert a_koff % tk == 0 and not (ta and a_koff)
    ko = a_koff // tk
    dn = (((0 if ta else 1,), (1 if tb else 0,)), ((), ()))

    def body(*refs):
        a_ref, b_ref = refs[0], refs[1]
        epi_refs = refs[2:2 + n_epi]
        o_refs = refs[len(refs) - n_out - (nk > 1):len(refs) - (nk > 1)]
        k = pl.program_id(2)
        bv = b_ref[...]
        if bv.ndim == 3 and tb:
            cw = bv.shape[-1]
            prod = sum(lax.dot_general(a_ref[:, c * cw:(c + 1) * cw].astype(MXU_DTYPE), bv[c].astype(MXU_DTYPE), dn,
                                       preferred_element_type=F32) for c in range(bv.shape[0]))
        else:
            if bv.ndim == 3:
                bv = bv.reshape(-1, bv.shape[-1])
            prod = lax.dot_general(a_ref[...].astype(MXU_DTYPE), bv.astype(MXU_DTYPE), dn, preferred_element_type=F32)

        def finish(acc):
            res = (acc,) if epi is None else epi(acc, *[r[...] for r in epi_refs])
            for o_ref, r in zip(o_refs, res):
                o_ref[...] = r.astype(o_ref.dtype)

        if nk == 1:
            finish(prod)
            return
        acc_ref = refs[-1]

        @pl.when(k == 0)
        def _():
            acc_ref[...] = prod

        @pl.when(jnp.logical_and(k > 0, k < nk - 1))
        def _():
            acc_ref[...] += prod

        @pl.when(k == nk - 1)
        def _():
            finish(acc_ref[...] + prod)

    if a_spec is None:
        a_spec = pl.BlockSpec((tk, tm), lambda i, j, k: (k, i)) if ta else pl.BlockSpec((tm, tk), lambda i, j, k: (i, k + ko))
    else:
        a_spec = a_spec(tm, tn, tk)
    if b_spec is None:
        bs = pl.BlockSpec((tn, tk), lambda i, j, k: (j, k)) if tb else pl.BlockSpec((tk, tn), lambda i, j, k: (k, j))
    else:
        bs = b_spec(tm, tn, tk)
    tile = pl.BlockSpec((tm, tn), lambda i, j, k: (i, j))
    os_ = tile if o_spec is None else o_spec(tm, tn, tk)
    ins, in_specs, aliases = [a, b, *epi_in], [a_spec, bs] + [tile] * n_epi, {}
    if prev is not None:
        aliases = {len(ins): 0}
        ins.append(prev)
        in_specs.append(pl.BlockSpec(memory_space=pl.ANY))
    shapes = [SDS((M, N) if o_shape is None else o_shape, dt) for dt in out_dtypes]
    outs = pl.pallas_call(
        body, name=name, grid=(M // tm, N // tn, nk),
        in_specs=in_specs, out_specs=[os_] * n_out, out_shape=shapes, input_output_aliases=aliases,
        scratch_shapes=[pltpu.VMEM((tm, tn), F32)] if nk > 1 else [],
        compiler_params=_cp("parallel", "parallel", "arbitrary"),
    )(*ins)
    return outs[0] if n_out == 1 else outs


def _rms_hat(x):
    return x * lax.rsqrt(jnp.mean(x * x, axis=-1, keepdims=True) + EPS)


def rms_fwd(x, g, *, name):
    T, D = x.shape
    tr = _pick(T, 512)

    def body(x_ref, g_ref, o_ref):
        o_ref[...] = (_rms_hat(x_ref[...]) * g_ref[...]).astype(o_ref.dtype)

    return pl.pallas_call(body, name=name, grid=(T // tr,), in_specs=[_row(tr, D), _full((1, D))],
                          out_specs=_row(tr, D), out_shape=SDS((T, D), MXU_DTYPE), compiler_params=_cp("parallel"))(x, g)


def rms_res_fwd(x, y, g, *, name):
    T, D = x.shape
    tr = _pick(T, 512)

    def body(x_ref, y_ref, g_ref, o_ref):
        o_ref[...] = x_ref[...] + _rms_hat(y_ref[...]) * g_ref[...]

    return pl.pallas_call(body, name=name, grid=(T // tr,), in_specs=[_row(tr, D), _row(tr, D), _full((1, D))],
                          out_specs=_row(tr, D), out_shape=SDS((T, D), F32), compiler_params=_cp("parallel"))(x, y, g)


def rms_bwd(x, g, dy, res, *, name):
    T, D = x.shape
    tr = _pick(T, 512)
    has_res = res is not None

    def body(*refs):
        if has_res:
            x_ref, g_ref, dy_ref, res_ref, dx_ref, dg_ref = refs
        else:
            x_ref, g_ref, dy_ref, dx_ref, dg_ref = refs
        xv = x_ref[...]
        r = lax.rsqrt(jnp.mean(xv * xv, axis=-1, keepdims=True) + EPS)
        xh = xv * r
        dyv = dy_ref[...]
        dxh = dyv * g_ref[...]
        dx = r * (dxh - xh * jnp.mean(dxh * xh, axis=-1, keepdims=True))
        if has_res:
            dx = dx + res_ref[...]
        dx_ref[...] = dx.astype(dx_ref.dtype)

        @pl.when(pl.program_id(0) == 0)
        def _():
            dg_ref[...] = jnp.zeros_like(dg_ref)

        dg_ref[...] += jnp.sum(dyv * xh, axis=0, keepdims=True)

    ins = [x, g, dy] + ([res] if has_res else [])
    in_specs = [_row(tr, D), _full((1, D)), _row(tr, D)] + ([_row(tr, D)] if has_res else [])
    return pl.pallas_call(body, name=name, grid=(T // tr,), in_specs=in_specs,
                          out_specs=[_row(tr, D), _full((1, D))],
                          out_shape=[SDS((T, D), F32 if has_res else MXU_DTYPE), SDS((1, D), F32)],
                          compiler_params=_cp("arbitrary"))(*ins)


def loss_fwd_bwd(y, target, *, name):
    T, D = y.shape
    tr = _pick(T, 512)

    def body(y_ref, t_ref, l_ref, dy_ref):
        err = y_ref[...] - t_ref[...]
        dy_ref[...] = err * (1.0 / D)

        @pl.when(pl.program_id(0) == 0)
        def _():
            l_ref[...] = jnp.zeros_like(l_ref)

        l_ref[...] += 0.5 * jnp.sum(jnp.mean(err * err, axis=-1, keepdims=True))

    return pl.pallas_call(body, name=name, grid=(T // tr,), in_specs=[_row(tr, D), _row(tr, D)],
                          out_specs=[_full((SUBLANES, LANES)), _row(tr, D)],
                          out_shape=[SDS((SUBLANES, LANES), F32), SDS((T, D), F32)],
                          compiler_params=_cp("arbitrary"))(y, target)


def _s5_disc(lr, li, ldt, btr, bti):
    dt = jnp.exp(ldt)
    k = lax.broadcasted_iota(jnp.int32, (SUBLANES, S5_LANES), 0).astype(F32)
    kf = k + 1.0
    kb = 8.0 - k
    ph = li * dt
    lm = lr * dt
    tf_re = jnp.exp(kf * lm) * jnp.cos(kf * ph)
    tf_im = jnp.exp(kf * lm) * jnp.sin(kf * ph)
    tb_re = jnp.exp(kb * lm) * jnp.cos(kb * ph)
    tb_im = -jnp.exp(kb * lm) * jnp.sin(kb * ph)
    mag = jnp.exp(lm)
    ab_re = mag * jnp.cos(ph)
    ab_im = mag * jnp.sin(ph)
    den = lr * lr + li * li
    nr = ab_re - 1.0
    ni = ab_im
    q_re = (nr * lr + ni * li) / den
    q_im = (ni * lr - nr * li) / den
    bbt_re = q_re * btr - q_im * bti
    bbt_im = q_re * bti + q_im * btr
    return tf_re, tf_im, tb_re, tb_im, bbt_re, bbt_im


def _s5_disc_core(lr, li, ldt, btr, bti):
    dt = jnp.exp(ldt)
    mag = jnp.exp(lr * dt)
    ab_re = mag * jnp.cos(li * dt)
    ab_im = mag * jnp.sin(li * dt)
    den = lr * lr + li * li
    nr = ab_re - 1.0
    ni = ab_im
    q_re = (nr * lr + ni * li) / den
    q_im = (ni * lr - nr * li) / den
    return ab_re, ab_im, q_re * btr - q_im * bti, q_re * bti + q_im * btr


def s5_disc_fwd(lr, li, ldt, btr, bti, *, name):
    def body(lr_ref, li_ref, ldt_ref, btr_ref, bti_ref, *outs):
        vals = _s5_disc(lr_ref[...], li_ref[...], ldt_ref[...], btr_ref[...], bti_ref[...])
        for o, v in zip(outs, vals):
            o[...] = v

    tab = SDS((SUBLANES, S5_LANES), F32)
    bb = SDS((S5_GROUP, S5_LANES), F32)
    return pl.pallas_call(body, name=name, out_shape=[tab, tab, tab, tab, bb, bb])(lr, li, ldt, btr, bti)


def s5_disc_bwd(lr, li, ldt, btr, bti, dab_re, dab_im, dbbt_re, dbbt_im, *, name):
    def body(lr_ref, li_ref, ldt_ref, btr_ref, bti_ref, dar_ref, dai_ref, dbr_ref, dbi_ref,
             dlr_ref, dli_ref, dldt_ref, dbtr_ref, dbti_ref):
        _, vjp = jax.vjp(_s5_disc_core, lr_ref[...], li_ref[...], ldt_ref[...], btr_ref[...], bti_ref[...])
        dlr, dli, dldt, dbtr, dbti = vjp((dar_ref[...], dai_ref[...], dbr_ref[...], dbi_ref[...]))
        dlr_ref[...] = dlr
        dli_ref[...] = dli
        dbtr_ref[...] = dbtr
        dbti_ref[...] = dbti
        lane_group = lax.broadcasted_iota(jnp.int32, (S5_LANES, LANES), 0) // S5_STATE
        col = lax.broadcasted_iota(jnp.int32, (S5_LANES, LANES), 1)
        ind = (lane_group == col).astype(F32)
        dldt_ref[...] = jnp.dot(jnp.broadcast_to(dldt, (SUBLANES, S5_LANES)), ind,
                                precision=lax.Precision.HIGHEST, preferred_element_type=F32)

    row = SDS((1, S5_LANES), F32)
    bb = SDS((S5_GROUP, S5_LANES), F32)
    return pl.pallas_call(body, name=name, out_shape=[row, row, SDS((SUBLANES, LANES), F32), bb, bb])(
        lr, li, ldt, btr, bti, dab_re, dab_im, dbbt_re, dbbt_im)


S5_NB = 1024


S5_CB = S5_WIDTH * S5_NB // S5_LANES


def _chan_rows(tm, tn, tk):
    return pl.BlockSpec((tm, S5_CB), lambda i, j, k: (i, j // 2))


def _chan_rows_t(tm, tn, tk):
    return pl.BlockSpec((tk, S5_CB), lambda i, j, k: (k, j // 2))


def _chan_cols_of_i(tm, tn, tk):
    return pl.BlockSpec((tk, S5_CB), lambda i, j, k: (k, i // 2))


def _s5_b_block(tm, tn, tk):
    return pl.BlockSpec((S5_CB, S5_NB), lambda i, j, k: (j // 2, j))


def _s5_c_block_t(tm, tn, tk):
    return pl.BlockSpec((S5_NB, S5_CB), lambda i, j, k: (j, j // 2))


def _lanes_of_chan(tm, tn, tk):
    return pl.BlockSpec((tm, 2 * S5_NB), lambda i, j, k: (i, j))


def _s5_b_block_t(tm, tn, tk):
    return pl.BlockSpec((S5_CB, 2 * S5_NB), lambda i, j, k: (j, j))


def _s5_c_block(tm, tn, tk):
    return pl.BlockSpec((2 * S5_NB, S5_CB), lambda i, j, k: (j, j))


def s5_interleave(re, im, axis):
    parts = []
    for n in range(S5_LANES // S5_NB):
        sl = [slice(None)] * re.ndim
        sl[axis] = slice(n * S5_NB, (n + 1) * S5_NB)
        parts += [re[tuple(sl)], im[tuple(sl)]]
    return jnp.concatenate(parts, axis=axis)


def s5_deinterleave(a, axis):
    re, im = [], []
    for n in range(S5_LANES // S5_NB):
        sl = [slice(None)] * a.ndim
        sl[axis] = slice(2 * n * S5_NB, (2 * n + 1) * S5_NB)
        re.append(a[tuple(sl)])
        sl[axis] = slice((2 * n + 1) * S5_NB, (2 * n + 2) * S5_NB)
        im.append(a[tuple(sl)])
    return jnp.concatenate(re, axis=axis), jnp.concatenate(im, axis=axis)


def s5_scan(bu, tab_re, tab_im, *, reverse, name):
    T = bu.shape[0]
    nb = S5_NB
    tc = _pick(T, 256)
    nl = S5_LANES // nb
    nt = T // tc
    ntile = tc // SUBLANES
    step_rows = ((1, 7), (2, 6), (4, 4)) if reverse else ((1, 0), (2, 1), (4, 3))

    def body(br_ref, bi_ref, tr_ref, ti_ref, xo_ref, cr_ref, ci_ref, mr_ref, mi_ref):
        @pl.when(pl.program_id(1) == 0)
        def _():
            cr_ref[...] = jnp.zeros_like(cr_ref)
            ci_ref[...] = jnp.zeros_like(ci_ref)

        io = lax.broadcasted_iota(jnp.int32, (SUBLANES, nb), 0)
        for s_, (d, r) in enumerate(step_rows):
            keep = (io < SUBLANES - d) if reverse else (io >= d)
            mr_ref[s_] = jnp.where(keep, tr_ref[r:r + 1, :], 0.0)
            mi_ref[s_] = jnp.where(keep, ti_ref[r:r + 1, :], 0.0)

        def tile(i, carry):
            cr, ci = carry
            j = (ntile - 1 - i) if reverse else i
            r0 = pl.multiple_of(j * SUBLANES, SUBLANES)
            xr = br_ref[pl.ds(r0, SUBLANES), :]
            xi = bi_ref[pl.ds(r0, SUBLANES), :]
            for s_, (d, _) in enumerate(step_rows):
                sh = (SUBLANES - d) if reverse else d
                sr = pltpu.roll(xr, sh, 0)
                si = pltpu.roll(xi, sh, 0)
                pr, pi = mr_ref[s_], mi_ref[s_]
                xr, xi = xr + pr * sr - pi * si, xi + pr * si + pi * sr
            tr, ti = tr_ref[...], ti_ref[...]
            xr, xi = xr + tr * cr - ti * ci, xi + tr * ci + ti * cr
            xo_ref[pl.ds(r0, SUBLANES), 0:nb] = xr
            xo_ref[pl.ds(r0, SUBLANES), nb:2 * nb] = xi
            if reverse:
                return xr[0:1, :], xi[0:1, :]
            return xr[SUBLANES - 1:SUBLANES, :], xi[SUBLANES - 1:SUBLANES, :]

        cr, ci = lax.fori_loop(0, ntile, tile, (cr_ref[0:1, :], ci_ref[0:1, :]))
        cr_ref[0:1, :] = cr
        ci_ref[0:1, :] = ci

    def tmap(t):
        return (nt - 1 - t) if reverse else t

    re_spec = pl.BlockSpec((tc, nb), lambda n, t: (tmap(t), 2 * n))
    im_spec = pl.BlockSpec((tc, nb), lambda n, t: (tmap(t), 2 * n + 1))
    tab_spec = pl.BlockSpec((SUBLANES, nb), lambda n, t: (0, n))
    return pl.pallas_call(
        body, name=name, grid=(nl, nt), in_specs=[re_spec, im_spec, tab_spec, tab_spec],
        out_specs=pl.BlockSpec((tc, 2 * nb), lambda n, t: (tmap(t), n)), out_shape=SDS((T, 2 * S5_LANES), F32),
        scratch_shapes=[pltpu.VMEM((SUBLANES, nb), F32), pltpu.VMEM((SUBLANES, nb), F32),
                        pltpu.VMEM((len(step_rows), SUBLANES, nb), F32), pltpu.VMEM((len(step_rows), SUBLANES, nb), F32)],
        compiler_params=_cp("parallel", "arbitrary"),
    )(bu, bu, tab_re, tab_im)


def s5_da(lam, xs, *, name):
    T = xs.shape[0]
    nb = S5_NB
    tc = _pick(T, 256)
    nl, nt = S5_LANES // nb, T // tc
    hb = tc // SUBLANES

    def body(lr_ref, li_ref, xr_ref, xi_ref, hr_ref, hi_ref, dar_ref, dai_ref):
        t = pl.program_id(1)

        @pl.when(t == 0)
        def _():
            dar_ref[...] = jnp.zeros_like(dar_ref)
            dai_ref[...] = jnp.zeros_like(dai_ref)

        io = lax.broadcasted_iota(jnp.int32, (tc, nb), 0)
        first = jnp.where(t > 0, 1.0, 0.0)
        pr = jnp.where(io >= 1, pltpu.roll(xr_ref[...], 1, 0), hr_ref[SUBLANES - 1:SUBLANES, :] * first)
        pi = jnp.where(io >= 1, pltpu.roll(xi_ref[...], 1, 0), hi_ref[SUBLANES - 1:SUBLANES, :] * first)
        lr = lr_ref[...]
        li = li_ref[...]
        dar_ref[...] += jnp.sum(lr * pr + li * pi, axis=0, keepdims=True)
        dai_ref[...] += jnp.sum(li * pr - lr * pi, axis=0, keepdims=True)

    re_blk = pl.BlockSpec((tc, nb), lambda n, t: (t, 2 * n))
    im_blk = pl.BlockSpec((tc, nb), lambda n, t: (t, 2 * n + 1))
    re_halo = pl.BlockSpec((SUBLANES, nb), lambda n, t: (jnp.maximum(t * hb - 1, 0), 2 * n))
    im_halo = pl.BlockSpec((SUBLANES, nb), lambda n, t: (jnp.maximum(t * hb - 1, 0), 2 * n + 1))
    acc = pl.BlockSpec((1, nb), lambda n, t: (0, n))
    row = SDS((1, S5_LANES), F32)
    return pl.pallas_call(body, name=name, grid=(nl, nt), in_specs=[re_blk, im_blk, re_blk, im_blk, re_halo, im_halo],
                          out_specs=[acc, acc], out_shape=[row, row],
                          compiler_params=_cp("parallel", "arbitrary"))(lam, lam, xs, xs, xs, xs)


def s5_out_fwd(yc, u, d, *, name):
    T, C = yc.shape
    tr = _pick(T, 512)

    def body(yc_ref, u_ref, d_ref, yl_ref, yg_ref):
        yl = yc_ref[...] + d_ref[...] * u_ref[...]
        yl_ref[...] = yl
        yg_ref[...] = jax.nn.gelu(yl)

    return pl.pallas_call(body, name=name, grid=(T // tr,), in_specs=[_row(tr, C), _row(tr, C), _full((1, C))],
                          out_specs=[_row(tr, C)] * 2, out_shape=[SDS((T, C), F32)] * 2,
                          compiler_params=_cp("parallel"))(yc, u, d)


def glu_fwd(yg, gl, *, out_cols, name):
    T, C = yg.shape
    tr = _pick(T, 512)

    def body(yg_ref, gl_ref, o_ref):
        o_ref[...] = yg_ref[...] * jax.nn.sigmoid(gl_ref[...])

    return pl.pallas_call(body, name=name, grid=(T // tr,), in_specs=[_row(tr, C)] * 2, out_specs=_row(tr, C),
                          out_shape=SDS((T, out_cols), F32), compiler_params=_cp("parallel"))(yg, gl)


def glu_bwd(yg, gl, dy, *, name):
    T, C = yg.shape
    tr = _pick(T, 512)

    def body(yg_ref, gl_ref, dy_ref, dyg_ref, dgl_ref):
        s = jax.nn.sigmoid(gl_ref[...])
        dyv = dy_ref[...]
        dyg_ref[...] = dyv * s
        dgl_ref[...] = dyv * yg_ref[...] * s * (1.0 - s)

    return pl.pallas_call(body, name=name, grid=(T // tr,), in_specs=[_row(tr, C)] * 3, out_specs=[_row(tr, C)] * 2,
                          out_shape=[SDS((T, C), F32)] * 2, compiler_params=_cp("parallel"))(yg, gl, dy)


def s5_out_bwd(yl, u, d, dyg_a, dyg_b, *, name):
    T, C = yl.shape
    tr = _pick(T, 512)

    def body(yl_ref, u_ref, d_ref, da_ref, db_ref, dyl_ref, du_ref, dd_ref):
        dyl = (da_ref[...] + db_ref[...]) * _gelu_grad(yl_ref[...])
        dyl_ref[...] = dyl
        du_ref[...] = dyl * d_ref[...]

        @pl.when(pl.program_id(0) == 0)
        def _():
            dd_ref[...] = jnp.zeros_like(dd_ref)

        dd_ref[...] += jnp.sum(dyl * u_ref[...], axis=0, keepdims=True)

    return pl.pallas_call(body, name=name, grid=(T // tr,),
                          in_specs=[_row(tr, C), _row(tr, C), _full((1, C)), _row(tr, C), _row(tr, C)],
                          out_specs=[_row(tr, C), _row(tr, C), _full((1, C))],
                          out_shape=[SDS((T, C), F32), SDS((T, C), F32), SDS((1, C), F32)],
                          compiler_params=_cp("arbitrary"))(yl, u, d, dyg_a, dyg_b)


def add2(a, b, *, name):
    T, C = a.shape
    tr = _pick(T, 512)

    def body(a_ref, b_ref, o_ref):
        o_ref[...] = a_ref[...] + b_ref[...]

    return pl.pallas_call(body, name=name, grid=(T // tr,), in_specs=[_row(tr, C)] * 2, out_specs=_row(tr, C),
                          out_shape=SDS((T, C), F32), compiler_params=_cp("parallel"))(a, b)


def _tri(n, upper):
    r = lax.broadcasted_iota(jnp.int32, (n, n), 0)
    c = lax.broadcasted_iota(jnp.int32, (n, n), 1)
    return ((c >= r) if upper else (c <= r)).astype(F32)


def fox_gate_fwd(fl, bf, *, fl_col, name):
    T = fl.shape[0]
    tb = _pick(T, 256)

    def body(fl_ref, bf_ref, f_ref, c_ref):
        @pl.when(pl.program_id(0) == 0)
        def _():
            c_ref[...] = jnp.zeros_like(c_ref)

        lf = jax.nn.log_sigmoid(fl_ref[...] + bf_ref[...])
        f = jnp.dot(_tri(tb, False), lf, precision=lax.Precision.HIGHEST, preferred_element_type=F32) + c_ref[0:1, :]
        f_ref[...] = f * LOG2E
        c_ref[0:1, :] = f[tb - 1:tb, :]

    fl_spec = pl.BlockSpec((tb, LANES), lambda i: (i, fl_col))
    return pl.pallas_call(body, name=name, grid=(T // tb,), in_specs=[fl_spec, _full((1, LANES))],
                          out_specs=_row(tb, LANES), out_shape=SDS((T, LANES), F32),
                          scratch_shapes=[pltpu.VMEM((SUBLANES, LANES), F32)], compiler_params=_cp("arbitrary"))(fl, bf)


def fox_gate_bwd(fl, bf, df, *, fl_col, name):
    T = fl.shape[0]
    tb = _pick(T, 256)
    nt = T // tb

    def body(fl_ref, bf_ref, df_ref, dfl_ref, dbf_ref, c_ref):
        @pl.when(pl.program_id(0) == 0)
        def _():
            c_ref[...] = jnp.zeros_like(c_ref)
            dbf_ref[...] = jnp.zeros_like(dbf_ref)

        dlf = jnp.dot(_tri(tb, True), df_ref[...], precision=lax.Precision.HIGHEST, preferred_element_type=F32) + c_ref[0:1, :]
        c_ref[0:1, :] = dlf[0:1, :]
        dfl = dlf * jax.nn.sigmoid(-(fl_ref[...] + bf_ref[...]))
        dfl_ref[...] = dfl
        dbf_ref[...] += jnp.sum(dfl, axis=0, keepdims=True)

    rev = pl.BlockSpec((tb, LANES), lambda i: (nt - 1 - i, 0))
    fl_rev = pl.BlockSpec((tb, LANES), lambda i: (nt - 1 - i, fl_col))
    return pl.pallas_call(body, name=name, grid=(nt,), in_specs=[fl_rev, _full((1, LANES)), rev],
                          out_specs=[rev, _full((1, LANES))], out_shape=[SDS((T, LANES), F32), SDS((1, LANES), F32)],
                          scratch_shapes=[pltpu.VMEM((SUBLANES, LANES), F32)], compiler_params=_cp("arbitrary"))(fl, bf, df)


FOX_BLOCK = 512
FOX_PAIRS = FOX_HEADS // 2
_NT = (((1,), (1,)), ((), ()))


LOG2E = 1.4426950408889634
FOX_FWD_UNROLL = 4
FOX_BWD_UNROLL = 2


def _fox_block(T):
    return _pick(T, FOX_BLOCK)


def _own_lanes(lane, hh):
    return (lane < FOX_HEAD_DIM) if hh == 0 else (lane >= FOX_HEAD_DIM)


def _grouped_steps(step, lo, n, unroll, init):
    def trip(t, c):
        for u in range(unroll):
            c = step(lo + t * unroll + u, c)
        return c

    carry = lax.fori_loop(0, n // unroll, trip, init)
    for u in range(unroll - 1):
        carry = lax.cond(n % unroll > u, lambda c: step(lo + (n // unroll) * unroll + u, c), lambda c: c, carry)
    return carry


Q_TILE0, K_TILE0, V_TILE0, O_TILE0 = 4, 8, 12, 4
FL_TILE = 16
POOL_COL = 2


def fox_fwd(z, f_col, f_row, ycat, hosted, *, name):
    T = z.shape[0]
    blk = _fox_block(T)
    nb = T // blk
    scale = FOX_HEAD_DIM ** -0.5

    def body(q_ref, k_ref, v_ref, fc_ref, fr_ref, prev_ref, o_ref, l_ref):
        i = pl.program_id(1)
        row = lax.broadcasted_iota(jnp.int32, (blk, blk), 0)
        col = lax.broadcasted_iota(jnp.int32, (blk, blk), 1)
        lane = lax.broadcasted_iota(jnp.int32, (blk, LANES), 1)
        qt = q_ref[...] * (scale * LOG2E)
        outs = []
        for hh in range(2):
            qh = jnp.where(_own_lanes(lane, hh), qt, 0.0).astype(MXU_DTYPE)
            fi = fc_ref[0, :, hh:hh + 1]

            def step(j, carry, masked=False):
                m, l, acc = carry
                r0 = pl.multiple_of(j * blk, blk)
                kj = k_ref[pl.ds(r0, blk), :].astype(MXU_DTYPE)
                vj = v_ref[pl.ds(r0, blk), :].astype(MXU_DTYPE)
                s = lax.dot_general(qh, kj, _NT, preferred_element_type=F32) + (fi - fr_ref[0, j, hh:hh + 1, :])
                if masked:
                    s = jnp.where(col <= row, s, -jnp.inf)
                m_new = jnp.maximum(m, jnp.max(s, axis=-1, keepdims=True))
                p = jnp.exp2(s - m_new)
                alpha = jnp.exp2(m - m_new)
                l = alpha * l + jnp.sum(p, axis=-1, keepdims=True)
                acc = alpha * acc + jnp.dot(p.astype(MXU_DTYPE), vj, preferred_element_type=F32)
                return m_new, l, acc

            init = (jnp.full((blk, 1), -jnp.inf, F32), jnp.zeros((blk, 1), F32), jnp.zeros((blk, LANES), F32))
            m, l, acc = step(i, _grouped_steps(step, 0, i, FOX_FWD_UNROLL, init), True)
            outs.append(acc / l)
            l_ref[0, :, hh:hh + 1] = m + jnp.log2(l)
        o_ref[...] = jnp.where(_own_lanes(lane, 0), outs[0], outs[1])

    qspec = pl.BlockSpec((blk, LANES), lambda h, i: (i, Q_TILE0 + h))
    kspec = pl.BlockSpec((T, LANES), lambda h, i: (0, K_TILE0 + h))
    vspec = pl.BlockSpec((T, LANES), lambda h, i: (0, V_TILE0 + h))
    ospec = pl.BlockSpec((blk, LANES), lambda h, i: (i, O_TILE0 + h))
    cspec = pl.BlockSpec((1, blk, 2), lambda h, i: (h, i, 0))
    rspec = pl.BlockSpec((1, nb, 2, blk), lambda h, i: (h, 0, 0, 0))
    return call_hosting(body, hosted, name=name, grid=(FOX_PAIRS, nb),
                        in_specs=[qspec, kspec, vspec, cspec, rspec, ANY], out_specs=[ospec, cspec],
                        out_shape=[SDS(ycat.shape, F32), SDS((FOX_PAIRS, T, 2), F32)],
                        inputs=[z, z, z, f_col, f_row, ycat], aliases={5: 0})


def fox_bwd_q(z, ycat, dycat, f_col, f_row, lse_col, hosted, *, name):
    T = z.shape[0]
    blk = _fox_block(T)
    nb = T // blk
    scale = FOX_HEAD_DIM ** -0.5

    def body(q_ref, k_ref, v_ref, o_ref, do_ref, fc_ref, fr_ref, lc_ref, dq_ref, dd_ref, df_ref):
        i = pl.program_id(1)
        row = lax.broadcasted_iota(jnp.int32, (blk, blk), 0)
        col = lax.broadcasted_iota(jnp.int32, (blk, blk), 1)
        lane = lax.broadcasted_iota(jnp.int32, (blk, LANES), 1)
        qt = q_ref[...] * (scale * LOG2E)
        dot = do_ref[...]
        prod = dot * o_ref[...]
        outs = []
        for hh in range(2):
            own = _own_lanes(lane, hh)
            qh = jnp.where(own, qt, 0.0).astype(MXU_DTYPE)
            dob = jnp.where(own, dot, 0.0).astype(MXU_DTYPE)
            dd = jnp.sum(jnp.where(own, prod, 0.0), axis=-1, keepdims=True)
            dd_ref[0, :, hh:hh + 1] = dd
            fi = fc_ref[0, :, hh:hh + 1]
            lse = lc_ref[0, :, hh:hh + 1]

            def step(j, carry, masked=False):
                dq, df = carry
                r0 = pl.multiple_of(j * blk, blk)
                kj = k_ref[pl.ds(r0, blk), :].astype(MXU_DTYPE)
                vj = v_ref[pl.ds(r0, blk), :].astype(MXU_DTYPE)
                s = lax.dot_general(qh, kj, _NT, preferred_element_type=F32) + (fi - fr_ref[0, j, hh:hh + 1, :])
                p = jnp.exp2(s - lse)
                if masked:
                    p = jnp.where(col <= row, p, 0.0)
                dp = lax.dot_general(dob, vj, _NT, preferred_element_type=F32)
                ds = p * (dp - dd)
                return (dq + jnp.dot(ds.astype(MXU_DTYPE), kj, preferred_element_type=F32),
                        df + jnp.sum(ds, axis=-1, keepdims=True))

            init = (jnp.zeros((blk, LANES), F32), jnp.zeros((blk, 1), F32))
            dq, df = step(i, _grouped_steps(step, 0, i, FOX_BWD_UNROLL, init), True)
            outs.append(dq * scale)
            df_ref[0, :, hh:hh + 1] = df
        dq_ref[...] = jnp.where(_own_lanes(lane, 0), outs[0], outs[1])

    qspec = pl.BlockSpec((blk, LANES), lambda h, i: (i, Q_TILE0 + h))
    kspec = pl.BlockSpec((T, LANES), lambda h, i: (0, K_TILE0 + h))
    vspec = pl.BlockSpec((T, LANES), lambda h, i: (0, V_TILE0 + h))
    ospec = pl.BlockSpec((blk, LANES), lambda h, i: (i, O_TILE0 + h))
    dqspec = pl.BlockSpec((blk, LANES), lambda h, i: (i, h))
    cspec = pl.BlockSpec((1, blk, 2), lambda h, i: (h, i, 0))
    rspec = pl.BlockSpec((1, nb, 2, blk), lambda h, i: (h, 0, 0, 0))
    stat = SDS((FOX_PAIRS, T, 2), F32)
    return call_hosting(body, hosted, name=name, grid=(FOX_PAIRS, nb),
                        in_specs=[qspec, kspec, vspec, ospec, ospec, cspec, rspec, cspec],
                        out_specs=[dqspec, cspec, cspec], out_shape=[SDS((T, FOX_WIDTH), F32), stat, stat],
                        inputs=[z, z, z, ycat, dycat, f_col, f_row, lse_col], aliases={})


def fox_bwd_kv(z, dycat, f_col, f_row, lse_row, dd_row, dfq_col, hosted, *, name):
    T = z.shape[0]
    blk = _fox_block(T)
    nb = T // blk
    scale = FOX_HEAD_DIM ** -0.5

    def body(q_ref, k_ref, v_ref, do_ref, fc_ref, fr_ref, lr_ref, dr_ref, dfq_ref, dk_ref, dv_ref, df_ref):
        j = pl.program_id(1)
        row = lax.broadcasted_iota(jnp.int32, (blk, blk), 0)
        col = lax.broadcasted_iota(jnp.int32, (blk, blk), 1)
        lane = lax.broadcasted_iota(jnp.int32, (blk, LANES), 1)
        kt = k_ref[...]
        vt = v_ref[...]
        dks, dvs = [], []
        for hh in range(2):
            own = _own_lanes(lane, hh)
            kh = jnp.where(own, kt, 0.0).astype(MXU_DTYPE)
            vh = jnp.where(own, vt, 0.0).astype(MXU_DTYPE)
            fj = fc_ref[0, :, hh:hh + 1]

            def step(i, carry, masked=False):
                dk, dv, df = carry
                r0 = pl.multiple_of(i * blk, blk)
                qi = (q_ref[pl.ds(r0, blk), :] * (scale * LOG2E)).astype(MXU_DTYPE)
                doi = do_ref[pl.ds(r0, blk), :].astype(MXU_DTYPE)
                st = lax.dot_general(kh, qi, _NT, preferred_element_type=F32) + (fr_ref[0, i, hh:hh + 1, :] - fj)
                pt = jnp.exp2(st - lr_ref[0, i, hh:hh + 1, :])
                if masked:
                    pt = jnp.where(col >= row, pt, 0.0)
                dv = dv + jnp.dot(pt.astype(MXU_DTYPE), doi, preferred_element_type=F32)
                dpt = lax.dot_general(vh, doi, _NT, preferred_element_type=F32)
                dst = pt * (dpt - dr_ref[0, i, hh:hh + 1, :])
                dk = dk + jnp.dot(dst.astype(MXU_DTYPE), qi, preferred_element_type=F32)
                df = df - jnp.sum(dst, axis=-1, keepdims=True)
                return dk, dv, df

            init = (jnp.zeros((blk, LANES), F32), jnp.zeros((blk, LANES), F32), dfq_ref[0, :, hh:hh + 1])
            dk, dv, df = _grouped_steps(step, j + 1, nb - 1 - j, FOX_BWD_UNROLL, step(j, init, True))
            dks.append(dk * (1.0 / LOG2E))
            dvs.append(dv)
            df_ref[0, :, hh:hh + 1] = df
        dk_ref[...] = jnp.where(_own_lanes(lane, 0), dks[0], dks[1])
        dv_ref[...] = jnp.where(_own_lanes(lane, 0), dvs[0], dvs[1])

    bspec = pl.BlockSpec((blk, LANES), lambda h, j: (j, h))
    qspec = pl.BlockSpec((T, LANES), lambda h, j: (0, Q_TILE0 + h))
    kspec = pl.BlockSpec((blk, LANES), lambda h, j: (j, K_TILE0 + h))
    vspec = pl.BlockSpec((blk, LANES), lambda h, j: (j, V_TILE0 + h))
    dospec = pl.BlockSpec((T, LANES), lambda h, j: (0, O_TILE0 + h))
    cspec = pl.BlockSpec((1, blk, 2), lambda h, j: (h, j, 0))
    rspec = pl.BlockSpec((1, nb, 2, blk), lambda h, j: (h, 0, 0, 0))
    return call_hosting(body, hosted, name=name, grid=(FOX_PAIRS, nb),
                        in_specs=[qspec, kspec, vspec, dospec, cspec, rspec, rspec, rspec, cspec],
                        out_specs=[bspec, bspec, cspec],
                        out_shape=[SDS((T, FOX_WIDTH), F32), SDS((T, FOX_WIDTH), F32), SDS((FOX_PAIRS, T, 2), F32)],
                        inputs=[z, z, z, dycat, f_col, f_row, lse_row, dd_row, dfq_col], aliases={})


def _pairs_col(a, T):
    return jnp.transpose(a[:, :FOX_HEADS].reshape(T, FOX_PAIRS, 2), (1, 0, 2))


def _col_to_row(a, T):
    blk = _fox_block(T)
    return jnp.transpose(a.reshape(FOX_PAIRS, T // blk, blk, 2), (0, 1, 3, 2))


def _pairs_to_lanes(a, T):
    flat = jnp.transpose(a, (1, 0, 2)).reshape(T, FOX_HEADS)
    return jnp.pad(flat, ((0, 0), (0, LANES - FOX_HEADS)))


def _pool_counts(t0, n, w):
    t = (t0 + lax.broadcasted_iota(jnp.int32, (n, 1), 0)).astype(F32)
    return jnp.minimum(t + 1.0, float(w))


def pool_window(x, *, adjoint, name, in_col=0, into=None, out_col=0):
    T, C = x.shape[0], len(POOL_WINDOWS) * POOL_GROUP_DIM
    tr = _pick(T, 512)
    nt = T // tr
    hb = tr // POOL_HALO
    n = tr + POOL_HALO

    def body(x_ref, h_ref, *rest):
        o_ref = rest[-1]
        i = pl.program_id(0)
        cur = x_ref[...]
        if adjoint:
            halo = h_ref[...] * jnp.where(i < nt - 1, 1.0, 0.0)
            ext = jnp.concatenate([cur, halo], axis=0)
            t0 = i * tr
        else:
            halo = h_ref[...] * jnp.where(i > 0, 1.0, 0.0)
            ext = jnp.concatenate([halo, cur], axis=0)
            t0 = i * tr - POOL_HALO
        sums = {}
        for g, w in enumerate(POOL_WINDOWS):
            ls = slice(g * POOL_GROUP_DIM, (g + 1) * POOL_GROUP_DIM)
            s = ext[:, ls]
            if adjoint:
                s = s / _pool_counts(t0, n, w)
            d = 1
            while d < w:
                s = s + pltpu.roll(s, (n - d) if adjoint else d, 0)
                d *= 2
            if adjoint:
                o_ref[:, ls] = s[0:tr, :] - cur[:, ls]
            else:
                o_ref[:, ls] = s[POOL_HALO:n, :] / _pool_counts(i * tr, tr, w) - cur[:, ls]

    if adjoint:
        halo_spec = pl.BlockSpec((POOL_HALO, C), lambda i: (jnp.minimum((i + 1) * hb, T // POOL_HALO - 1), in_col))
    else:
        halo_spec = pl.BlockSpec((POOL_HALO, C), lambda i: (jnp.maximum(i * hb - 1, 0), in_col))
    x_spec = pl.BlockSpec((tr, C), lambda i: (i, in_col))
    if into is None:
        return pl.pallas_call(body, name=name, grid=(nt,), in_specs=[x_spec, halo_spec], out_specs=_row(tr, C),
                              out_shape=SDS((T, C), F32), compiler_params=_cp("parallel"))(x, x)
    return pl.pallas_call(body, name=name, grid=(nt,), in_specs=[x_spec, halo_spec, ANY],
                          out_specs=pl.BlockSpec((tr, C), lambda i: (i, out_col)), out_shape=SDS(into.shape, F32),
                          input_output_aliases={2: 0}, compiler_params=_cp("parallel"))(x, x, into)


def colscale_fwd(a, s, *, out_cols, name):
    T, C = a.shape
    tr = _pick(T, 512)

    def body(a_ref, s_ref, o_ref):
        o_ref[...] = a_ref[...] * s_ref[...]

    return pl.pallas_call(body, name=name, grid=(T // tr,), in_specs=[_row(tr, C), _full((1, C))], out_specs=_row(tr, C),
                          out_shape=SDS((T, out_cols), F32), compiler_params=_cp("parallel"))(a, s)


def colscale_bwd(a, s, dy, *, name):
    T, C = a.shape
    tr = _pick(T, 512)

    def body(a_ref, s_ref, dy_ref, da_ref, ds_ref):
        dyv = dy_ref[...]
        da_ref[...] = dyv * s_ref[...]

        @pl.when(pl.program_id(0) == 0)
        def _():
            ds_ref[...] = jnp.zeros_like(ds_ref)

        ds_ref[...] += jnp.sum(dyv * a_ref[...], axis=0, keepdims=True)

    return pl.pallas_call(body, name=name, grid=(T // tr,), in_specs=[_row(tr, C), _full((1, C)), _row(tr, C)],
                          out_specs=[_row(tr, C), _full((1, C))], out_shape=[SDS((T, C), F32), SDS((1, C), F32)],
                          compiler_params=_cp("arbitrary"))(a, s, dy)


SGU_ROWS = 512


def _sgu_norm(v, ln_g, ln_b):
    vg = jax.nn.gelu(v)
    xc = vg - jnp.mean(vg, axis=-1, keepdims=True)
    r = lax.rsqrt(jnp.mean(xc * xc, axis=-1, keepdims=True) + EPS)
    xh = xc * r
    return xh * ln_g + ln_b, xh, r


def _rowc(tr, c, cb):
    return pl.BlockSpec((tr, c), lambda i: (i, cb))


def sgu_fwd(z, ln_g, ln_b, ws, bst, ycat, *, name):
    T, C = z.shape[0], SGU_GROUPS * SGU_GROUP_DIM
    tr = _pick(T, SGU_ROWS)

    def body(u_ref, v_ref, g_ref, b_ref, ws_ref, bst_ref, prev_ref, o_ref):
        vn, _, _ = _sgu_norm(v_ref[...], g_ref[...], b_ref[...])
        vn = vn.astype(MXU_DTYPE)
        ug = jax.nn.gelu(u_ref[...])
        for g in range(SGU_GROUPS):
            w = ws_ref[g].astype(MXU_DTYPE)
            bias = bst_ref[:, g:g + 1]
            for c in range(tr // CHUNK):
                rs = slice(c * CHUNK, (c + 1) * CHUNK)
                ls = slice(g * SGU_GROUP_DIM, (g + 1) * SGU_GROUP_DIM)
                mixed = jnp.dot(w, vn[rs, ls], preferred_element_type=F32) + bias
                o_ref[rs, ls] = ug[rs, ls] * mixed

    return pl.pallas_call(body, name=name, grid=(T // tr,),
                          in_specs=[_rowc(tr, C, 0), _rowc(tr, C, 1), _full((1, C)), _full((1, C)),
                                    _full((SGU_GROUPS, CHUNK, CHUNK)), _full((CHUNK, SGU_GROUPS)), ANY],
                          out_specs=_rowc(tr, C, 1), out_shape=SDS(ycat.shape, F32), input_output_aliases={6: 0},
                          compiler_params=_cp("parallel"))(z, z, ln_g, ln_b, ws, bst, ycat)


def sgu_bwd(z, ln_g, ln_b, ws, wst, bst, dycat, *, out_cols, name):
    T, C = z.shape[0], SGU_GROUPS * SGU_GROUP_DIM
    tr = _pick(T, SGU_ROWS)

    def body(u_ref, v_ref, g_ref, b_ref, ws_ref, wst_ref, bst_ref, dy_ref,
             duv_ref, dg_ref, db_ref, dws_ref, dbst_ref, dvn_ref):
        du_ref = duv_ref.at[:, 0:C]
        dv_ref = duv_ref.at[:, C:2 * C]
        @pl.when(pl.program_id(0) == 0)
        def _():
            dg_ref[...] = jnp.zeros_like(dg_ref)
            db_ref[...] = jnp.zeros_like(db_ref)
            dws_ref[...] = jnp.zeros_like(dws_ref)
            dbst_ref[...] = jnp.zeros_like(dbst_ref)

        uv = u_ref[...]
        vv = v_ref[...]
        vn, xh, r = _sgu_norm(vv, g_ref[...], b_ref[...])
        vn = vn.astype(MXU_DTYPE)
        ug = jax.nn.gelu(uv)
        dyv = dy_ref[...]
        for g in range(SGU_GROUPS):
            w = ws_ref[g].astype(MXU_DTYPE)
            wt = wst_ref[g].astype(MXU_DTYPE)
            bias = bst_ref[:, g:g + 1]
            dw = jnp.zeros((CHUNK, CHUNK), F32)
            dbias = jnp.zeros((CHUNK, 1), F32)
            for c in range(tr // CHUNK):
                rs = slice(c * CHUNK, (c + 1) * CHUNK)
                ls = slice(g * SGU_GROUP_DIM, (g + 1) * SGU_GROUP_DIM)
                vblk = vn[rs, ls]
                mixed = jnp.dot(w, vblk, preferred_element_type=F32) + bias
                dyb = dyv[rs, ls]
                du_ref[rs, ls] = dyb * mixed * _gelu_grad(uv[rs, ls])
                dmixed = dyb * ug[rs, ls]
                dbias = dbias + jnp.sum(dmixed, axis=-1, keepdims=True)
                dmb = dmixed.astype(MXU_DTYPE)
                dw = dw + lax.dot_general(dmb, vblk, _NT, preferred_element_type=F32)
                dvn_ref[rs, ls] = jnp.dot(wt, dmb, preferred_element_type=F32)
            dws_ref[g] += dw
            dbst_ref[:, g:g + 1] += dbias
        dvn = dvn_ref[...]
        dg_ref[...] += jnp.sum(dvn * xh, axis=0, keepdims=True)
        db_ref[...] += jnp.sum(dvn, axis=0, keepdims=True)
        dxh = dvn * g_ref[...]
        dvg = r * (dxh - jnp.mean(dxh, axis=-1, keepdims=True) - xh * jnp.mean(dxh * xh, axis=-1, keepdims=True))
        dv_ref[...] = dvg * _gelu_grad(vv)

    wspec = _full((SGU_GROUPS, CHUNK, CHUNK))
    return pl.pallas_call(body, name=name, grid=(T // tr,),
                          in_specs=[_rowc(tr, C, 0), _rowc(tr, C, 1), _full((1, C)), _full((1, C)), wspec, wspec,
                                    _full((CHUNK, SGU_GROUPS)), _rowc(tr, C, 1)],
                          out_specs=[_rowc(tr, 2 * C, 0), _full((1, C)), _full((1, C)), wspec,
                                     _full((CHUNK, SGU_GROUPS))],
                          out_shape=[SDS((T, out_cols), F32), SDS((1, C), F32), SDS((1, C), F32),
                                     SDS((SGU_GROUPS, CHUNK, CHUNK), F32), SDS((CHUNK, SGU_GROUPS), F32)],
                          scratch_shapes=[pltpu.VMEM((tr, C), F32)],
                          compiler_params=_cp("arbitrary"))(z, z, ln_g, ln_b, ws, wst, bst, dycat)


def adamw(w, g, m, v, *, name):
    R, C = w.shape
    tr = _pick(R, 512)
    c1 = 1.0 - ADAM_B1 ** ADAM_STEP
    c2 = 1.0 - ADAM_B2 ** ADAM_STEP

    def body(w_ref, g_ref, m_ref, v_ref, d_ref, nm_ref, nv_ref):
        gv = g_ref[...]
        nm = ADAM_B1 * m_ref[...] + (1.0 - ADAM_B1) * gv
        nv = ADAM_B2 * v_ref[...] + (1.0 - ADAM_B2) * (gv * gv)
        nm_ref[...] = nm
        nv_ref[...] = nv
        d_ref[...] = -ADAM_LR * ((nm / c1) / (jnp.sqrt(nv / c2) + ADAM_EPS) + ADAM_WD * w_ref[...])

    spec = _row(tr, C)
    return pl.pallas_call(body, name=name, grid=(R // tr,), in_specs=[spec] * 4, out_specs=[spec] * 3,
                          out_shape=[SDS((R, C), F32)] * 3, compiler_params=_cp("parallel"))(w, g, m, v)


ANY = pl.BlockSpec(memory_space=pl.ANY)


def _coords():
    return lax.axis_index("x"), lax.axis_index("y"), lax.axis_index("c")


def _other_chips(x, y):
    return [(1 - x, y), (x, 1 - y), (1 - x, 1 - y)]


def _remote(src, dst, send_sems, recv_sems, k, dev):
    return pltpu.make_async_remote_copy(src_ref=src, dst_ref=dst, send_sem=send_sems.at[k], recv_sem=recv_sems.at[k],
                                        device_id=dev, device_id_type=MESH)


LOCAL_CHUNKS = 8


def allgather_chip_shards(shards, small, *, name):
    na = len(shards)

    def body(*refs):
        s_refs, sm_ref = refs[:na], refs[na]
        o_refs, smo_ref = refs[na + 1:2 * na + 1], refs[2 * na + 1]
        send_sems, recv_sems, local_sems = refs[2 * na + 2:]
        x, y, c = _coords()
        j = 2 * x + y
        sibling = (x, y, 1 - c)
        chips = _other_chips(x, y)
        for a in range(na):
            chunk = shards[a].shape[0] // LOCAL_CHUNKS
            for q in range(LOCAL_CHUNKS):
                rows = pl.ds(q * chunk, chunk)
                pltpu.make_async_copy(s_refs[a].at[rows], o_refs[a].at[j, rows], local_sems.at[a]).start()
        pltpu.make_async_copy(sm_ref, smo_ref.at[j], local_sems.at[na]).start()
        sends = []
        for a in range(na):
            half = shards[a].shape[0] // 2
            mine = pl.ds(c * half, half)
            for k, (px, py) in enumerate(chips):
                sends.append(_remote(s_refs[a].at[mine], o_refs[a].at[j, mine], send_sems, recv_sems, 6 * a + k, (px, py, c)))
        for k, (px, py) in enumerate(chips):
            sends.append(_remote(sm_ref, smo_ref.at[j], send_sems, recv_sems, 6 * na + k, (px, py, c)))
        for cp in sends:
            cp.start()
        for a in range(na):
            half = shards[a].shape[0] // 2
            mine = pl.ds(c * half, half)
            for k, (px, py) in enumerate(chips):
                rows = o_refs[a].at[2 * px + py, mine]
                _remote(rows, rows, send_sems, recv_sems, 6 * a + k, (px, py, c)).wait_recv()
                fw = _remote(rows, rows, send_sems, recv_sems, 6 * a + 3 + k, sibling)
                fw.start()
                sends.append(fw)
        for a in range(na):
            half = shards[a].shape[0] // 2
            theirs = pl.ds((1 - c) * half, half)
            for k, (px, py) in enumerate(chips):
                rows = o_refs[a].at[2 * px + py, theirs]
                _remote(rows, rows, send_sems, recv_sems, 6 * a + 3 + k, sibling).wait_recv()
        for k, (px, py) in enumerate(chips):
            slot = smo_ref.at[2 * px + py]
            _remote(slot, slot, send_sems, recv_sems, 6 * na + k, (px, py, c)).wait_recv()
        for cp in sends:
            cp.wait_send()
        for a in range(na):
            pltpu.make_async_copy(s_refs[a], o_refs[a].at[j], local_sems.at[a]).wait()
        pltpu.make_async_copy(sm_ref, smo_ref.at[j], local_sems.at[na]).wait()

    nsem = 6 * na + 3
    outs = pl.pallas_call(
        body, name=name, in_specs=[ANY] * (na + 1), out_specs=[ANY] * (na + 1),
        out_shape=[SDS((N_CHIPS,) + s.shape, s.dtype) for s in shards] + [SDS((N_CHIPS,) + small.shape, small.dtype)],
        scratch_shapes=[pltpu.SemaphoreType.DMA((nsem,)), pltpu.SemaphoreType.DMA((nsem,)),
                        pltpu.SemaphoreType.DMA((na + 1,))])(*shards, small)
    return outs[:na], outs[na]


class Exchange:
    def __init__(self, ins, out_shapes, scratch, start, wait):
        self.ins, self.out_shapes, self.scratch, self.start, self.wait = list(ins), list(out_shapes), list(scratch), start, wait


def run_exchange(ex, *, name):
    ni, no = len(ex.ins), len(ex.out_shapes)

    def body(*refs):
        parts = refs[:ni], refs[ni:ni + no], refs[ni + no:]
        ex.start(*parts)
        ex.wait(*parts)

    return pl.pallas_call(body, name=name, in_specs=[ANY] * ni, out_specs=[ANY] * no, out_shape=ex.out_shapes,
                          scratch_shapes=ex.scratch)(*ex.ins)


def call_hosting(body, ex, *, name, grid, in_specs, out_specs, out_shape, inputs, aliases):
    n_in, n_out, ni, no = len(inputs), len(out_shape), len(ex.ins), len(ex.out_shapes)

    def wrapped(*refs):
        own = refs[:n_in] + refs[n_in + ni:n_in + ni + n_out]
        parts = refs[n_in:n_in + ni], refs[n_in + ni + n_out:n_in + ni + n_out + no], refs[n_in + ni + n_out + no:]
        ids = [pl.program_id(d) for d in range(len(grid))]
        first = functools.reduce(jnp.logical_and, [i == 0 for i in ids])
        last = functools.reduce(jnp.logical_and, [i == g - 1 for i, g in zip(ids, grid)])

        @pl.when(first)
        def _():
            ex.start(*parts)

        body(*own)

        @pl.when(last)
        def _():
            ex.wait(*parts)

    outs = pl.pallas_call(
        wrapped, name=name, grid=grid, in_specs=list(in_specs) + [ANY] * ni, out_specs=list(out_specs) + [ANY] * no,
        out_shape=list(out_shape) + ex.out_shapes, input_output_aliases=aliases, scratch_shapes=ex.scratch,
        compiler_params=_cp(*["arbitrary"] * len(grid)))(*inputs, *ex.ins)
    return outs[:n_out], outs[n_out:]


def allgather_ici_exchange(shards):
    na = len(shards)

    def copies(s_refs, o_refs, sems):
        send_sems, recv_sems, _ = sems
        x, y, c = _coords()
        j = 2 * x + y
        out = []
        for a in range(na):
            half = shards[a].shape[0] // 2
            mine = pl.ds(c * half, half)
            for k, (px, py) in enumerate(_other_chips(x, y)):
                send = _remote(s_refs[a].at[mine], o_refs[a].at[j, mine], send_sems, recv_sems, 3 * a + k, (px, py, c))
                rows = o_refs[a].at[2 * px + py, mine]
                out.append((send, _remote(rows, rows, send_sems, recv_sems, 3 * a + k, (px, py, c))))
        return out

    def start(s_refs, o_refs, sems):
        x, y, c = _coords()
        j = 2 * x + y
        for a in range(na):
            chunk = shards[a].shape[0] // LOCAL_CHUNKS
            for q in range(LOCAL_CHUNKS):
                rows = pl.ds(q * chunk, chunk)
                pltpu.make_async_copy(s_refs[a].at[rows], o_refs[a].at[j, rows], sems[2].at[a]).start()
        for send, _ in copies(s_refs, o_refs, sems):
            send.start()

    def wait(s_refs, o_refs, sems):
        x, y, c = _coords()
        j = 2 * x + y
        for send, arrival in copies(s_refs, o_refs, sems):
            arrival.wait_recv()
            send.wait_send()
        for a in range(na):
            pltpu.make_async_copy(s_refs[a], o_refs[a].at[j], sems[2].at[a]).wait()

    return Exchange(shards, [SDS((N_CHIPS,) + s.shape, s.dtype) for s in shards],
                    [pltpu.SemaphoreType.DMA((3 * na,)), pltpu.SemaphoreType.DMA((3 * na,)), pltpu.SemaphoreType.DMA((na,))],
                    start, wait)


def allgather_forward(gathered, *, name):
    na = len(gathered)

    def body(*refs):
        o_refs = refs[na:2 * na]
        send_sems, recv_sems = refs[2 * na:]
        x, y, c = _coords()
        sibling = (x, y, 1 - c)
        cps = []
        for a in range(na):
            half = gathered[a].shape[1] // 2
            for k, (px, py) in enumerate(_other_chips(x, y)):
                mine = o_refs[a].at[2 * px + py, pl.ds(c * half, half)]
                theirs = o_refs[a].at[2 * px + py, pl.ds((1 - c) * half, half)]
                cps.append((_remote(mine, mine, send_sems, recv_sems, 3 * a + k, sibling),
                            _remote(theirs, theirs, send_sems, recv_sems, 3 * a + k, sibling)))
        for send, _ in cps:
            send.start()
        for send, arrival in cps:
            send.wait_send()
            arrival.wait_recv()

    return pl.pallas_call(body, name=name, in_specs=[ANY] * na, out_specs=[ANY] * na,
                          out_shape=[SDS(g.shape, g.dtype) for g in gathered],
                          input_output_aliases={a: a for a in range(na)},
                          scratch_shapes=[pltpu.SemaphoreType.DMA((3 * na,)), pltpu.SemaphoreType.DMA((3 * na,))])(*gathered)


def swap_halves_exchange(gs):
    na = len(gs)

    def copies(g_refs, o_refs, sems):
        x, y, c = _coords()
        out = []
        for a in range(na):
            half = gs[a].shape[1] // 2
            out.append(_remote(g_refs[a].at[:, pl.ds((1 - c) * half, half), :], o_refs[a], sems[0], sems[1], a,
                               (x, y, 1 - c)))
        return out

    def start(g_refs, o_refs, sems):
        for cp in copies(g_refs, o_refs, sems):
            cp.start()

    def wait(g_refs, o_refs, sems):
        for cp in copies(g_refs, o_refs, sems):
            cp.wait()

    return Exchange(gs, [SDS((g.shape[0], g.shape[1] // 2, g.shape[2]), g.dtype) for g in gs],
                    [pltpu.SemaphoreType.DMA((na,)), pltpu.SemaphoreType.DMA((na,))], start, wait)


def chip_partials_exchange(pbs):
    na = len(pbs)

    def copies(p_refs, o_refs, sems):
        x, y, c = _coords()
        out = []
        for a in range(na):
            for k, (px, py) in enumerate(_other_chips(x, y)):
                out.append(_remote(p_refs[a].at[2 * px + py], o_refs[a].at[k], sems[0], sems[1], 3 * a + k, (px, py, c)))
        return out

    def start(p_refs, o_refs, sems):
        for cp in copies(p_refs, o_refs, sems):
            cp.start()

    def wait(p_refs, o_refs, sems):
        for cp in copies(p_refs, o_refs, sems):
            cp.wait()

    return Exchange(pbs, [SDS((3,) + p.shape[1:], p.dtype) for p in pbs],
                    [pltpu.SemaphoreType.DMA((3 * na,)), pltpu.SemaphoreType.DMA((3 * na,))], start, wait)


def add_sibling_half(g, land, c_idx, *, name):
    n, R, C = g.shape
    half = R // 2
    tr = _pick(half, 256)
    nt = half // tr

    def body(c_ref, g_ref, l_ref, of_ref, ob_ref):
        s = g_ref[...] + l_ref[...].astype(F32)
        of_ref[...] = s
        ob_ref[...] = s.astype(ob_ref.dtype)

    blk = pl.BlockSpec((1, tr, C), lambda s, i, c_ref: (s, i, 0))
    gblk = pl.BlockSpec((1, tr, C), lambda s, i, c_ref: (s, c_ref[0] * nt + i, 0))
    return pl.pallas_call(
        body, name=name,
        grid_spec=pltpu.PrefetchScalarGridSpec(num_scalar_prefetch=1, grid=(n, nt), in_specs=[gblk, blk],
                                               out_specs=[blk, blk]),
        out_shape=[SDS((n, half, C), F32), SDS((n, half, C), WIRE_DTYPE)],
        compiler_params=_cp("parallel", "parallel"))(c_idx, g, land)


def add_chip_partials(pf, rb, jc_idx, *, name):
    n, H, C = pf.shape
    tr = _pick(H, 256)

    def body(jc_ref, p_ref, r_ref, o_ref):
        s = p_ref[0]
        for k in range(3):
            s = s + r_ref[k].astype(F32)
        o_ref[...] = s

    pblk = pl.BlockSpec((1, tr, C), lambda i, jc_ref: (jc_ref[0], i, 0))
    rblk = pl.BlockSpec((3, tr, C), lambda i, jc_ref: (0, i, 0))
    oblk = pl.BlockSpec((None, tr, C), lambda i, jc_ref: (jc_ref[1], i, 0))
    return pl.pallas_call(
        body, name=name,
        grid_spec=pltpu.PrefetchScalarGridSpec(num_scalar_prefetch=1, grid=(H // tr,), in_specs=[pblk, rblk],
                                               out_specs=oblk),
        out_shape=SDS((2, H, C), F32), compiler_params=_cp("parallel"))(jc_idx, pf, rb)


def join_sibling_halves(bufs, *, name):
    na = len(bufs)

    def body(*refs):
        o_refs = refs[na:2 * na]
        send_sems, recv_sems = refs[2 * na:]
        x, y, c = _coords()
        cps = [_remote(o_refs[a].at[c], o_refs[a].at[c], send_sems, recv_sems, a, (x, y, 1 - c)) for a in range(na)]
        for cp in cps:
            cp.start()
        for a in range(na):
            cps[a].wait_send()
            _remote(o_refs[a].at[1 - c], o_refs[a].at[1 - c], send_sems, recv_sems, a, (x, y, 1 - c)).wait_recv()

    return pl.pallas_call(body, name=name, in_specs=[ANY] * na, out_specs=[ANY] * na,
                          out_shape=[SDS(b.shape, b.dtype) for b in bufs],
                          input_output_aliases={a: a for a in range(na)},
                          scratch_shapes=[pltpu.SemaphoreType.DMA((na,)), pltpu.SemaphoreType.DMA((na,))])(*bufs)


def exchange_pieces(v, *, scatter, name):
    P, C = v.shape[-2:]

    def body(v_ref, o_ref, send_sems, recv_sems, local_sem):
        x, y, c = _coords()
        me = 4 * x + 2 * y + c
        local = pltpu.make_async_copy(v_ref.at[me] if scatter else v_ref, o_ref.at[me], local_sem)
        local.start()
        cps = []
        for m in range(1, N_DEV):
            px = (1 - x) if m & 4 else x
            py = (1 - y) if m & 2 else y
            pc = (1 - c) if m & 1 else c
            src = v_ref.at[4 * px + 2 * py + pc] if scatter else v_ref
            cps.append(_remote(src, o_ref.at[me], send_sems, recv_sems, m - 1, (px, py, pc)))
        for cp in cps:
            cp.start()
        for cp in cps:
            cp.wait_send()
        for m in range(1, N_DEV):
            px = (1 - x) if m & 4 else x
            py = (1 - y) if m & 2 else y
            pc = (1 - c) if m & 1 else c
            slot = o_ref.at[4 * px + 2 * py + pc]
            _remote(slot, slot, send_sems, recv_sems, m - 1, (px, py, pc)).wait_recv()
        local.wait()

    return pl.pallas_call(body, name=name, in_specs=[ANY], out_specs=ANY, out_shape=SDS((N_DEV, P, C), v.dtype),
                          scratch_shapes=[pltpu.SemaphoreType.DMA((N_DEV - 1,)), pltpu.SemaphoreType.DMA((N_DEV - 1,)),
                                          pltpu.SemaphoreType.DMA(())])(v)


def sum_pieces(land, *, name):
    n, P, C = land.shape

    def body(l_ref, o_ref):
        s = l_ref[0]
        for d in range(1, n):
            s = s + l_ref[d]
        o_ref[...] = s

    return pl.pallas_call(body, name=name, out_shape=SDS((P, C), F32))(land)


BIG_SEGS = (
    ("w_in_even", (1024, 514), 1),
    ("s5_w_glu", (128, 512), 0),
    ("w_out_even", (256, 1024), 0),
    ("w_in_odd", (1024, 384), 1),
    ("w_out_odd", (256, 1024), 0),
    ("mlp_w1", (2, 1024, 1024), 2),
    ("mlp_w2", (2, 1024, 1024), 1),
)
BIG_NAMES = tuple(n for n, _, _ in BIG_SEGS)
EARLY_NAMES = ("w_in_even", "s5_w_glu", "w_out_even")
LATE_NAMES = ("w_in_odd", "w_out_odd", "mlp_w1", "mlp_w2")
REDUCED_EARLY = ("s5_w_glu", "w_out_even", "w_in_odd", "w_out_odd", "mlp_w1", "mlp_w2")
SHARDED_SMALL = ("pool_scale", "sgu_ln_g", "sgu_ln_b")
SMALL_SEGS = (
    ("mix_pre_g", (2, 1024)), ("mix_post_g", (2, 1024)), ("mlp_pre_g", (2, 1024)), ("mlp_post_g", (2, 1024)),
    ("s5_lam_re", (1, 32, 64)), ("s5_lam_im", (1, 32, 64)), ("s5_log_dt", (1, 32)),
    ("s5_b_re", (1, 32, 64, 16)), ("s5_b_im", (1, 32, 64, 16)), ("s5_c_re", (1, 32, 16, 64)), ("s5_c_im", (1, 32, 16, 64)),
    ("s5_d", (1, 512)), ("fox_b_f", (1, 8)), ("pool_w", (1, 4, 128, 128)), ("sgu_w_s", (1, 4, 128, 128)),
    ("sgu_b_s", (1, 4, 128)),
)
REDUCED_SEGS = SMALL_SEGS + tuple((n, (1, 512)) for n in SHARDED_SMALL)


def _cols_from_chips(g):
    n, R, C = g.shape
    return jnp.transpose(g, (1, 0, 2)).reshape(R, n * C)


def _chips_from_cols(m):
    R, C4 = m.shape
    return jnp.transpose(m.reshape(R, N_CHIPS, C4 // N_CHIPS), (1, 0, 2))


MLP_SHARD = 1024


def _w1_cols(l):
    def spec(tm, tn, tk):
        per = MLP_SHARD // tn
        return pl.BlockSpec((None, tk, tn), lambda i, j, k: (j // per, l * (MLP_SHARD // tk) + k, j % per))
    return spec


def _w1_rows_t(l):
    def spec(tm, tn, tk):
        if tk == N_CHIPS * MLP_SHARD:
            return pl.BlockSpec((N_CHIPS, tn, MLP_SHARD), lambda i, j, k: (0, l * (MLP_SHARD // tn) + j, 0))
        per = MLP_SHARD // tk
        return pl.BlockSpec((None, tn, tk), lambda i, j, k: (k // per, l * (MLP_SHARD // tn) + j, k % per))
    return spec


def _w2_rows(l):
    def spec(tm, tn, tk):
        if tk == N_CHIPS * MLP_SHARD:
            return pl.BlockSpec((N_CHIPS, MLP_SHARD, tn), lambda i, j, k: (0, l, j))
        per = MLP_SHARD // tk
        return pl.BlockSpec((None, tk, tn), lambda i, j, k: (k // per, l * per + k % per, j))
    return spec


def _w2_rows_t(l):
    def spec(tm, tn, tk):
        per = MLP_SHARD // tn
        return pl.BlockSpec((None, tn, tk), lambda i, j, k: (j // per, l * per + j % per, k))
    return spec


def _dw1_out(l):
    def spec(tm, tn, tk):
        per = MLP_SHARD // tn
        return pl.BlockSpec((None, tm, tn), lambda i, j, k: (j // per, l * (MLP_SHARD // tm) + i, j % per))
    return spec


def _dw2_out(l):
    def spec(tm, tn, tk):
        per = MLP_SHARD // tm
        return pl.BlockSpec((None, tm, tn), lambda i, j, k: (i // per, l * per + i % per, j))
    return spec


def _pack_vec(d, segs, rows_multiple):
    flat = jnp.concatenate([d[n].reshape(-1) for n, _ in segs])
    rows = -(-flat.shape[0] // LANES)
    rows = -(-rows // rows_multiple) * rows_multiple
    return jnp.pad(flat, (0, rows * LANES - flat.shape[0])).reshape(rows, LANES)


def _unpack_vec(v, segs):
    flat, out, r = v.reshape(-1), {}, 0
    for n, shape in segs:
        k = math.prod(shape)
        out[n] = flat[r:r + k].reshape(shape)
        r += k
    return out


def _block_diag(blocks):
    G, a, b = blocks.shape
    eye = jnp.eye(G, dtype=blocks.dtype)
    return (eye[:, None, :, None] * blocks[:, :, None, :]).reshape(G * a, G * b)


def _diag_blocks(m, G):
    a, b = m.shape[0] // G, m.shape[1] // G
    return jnp.stack([m[g * a:(g + 1) * a, g * b:(g + 1) * b] for g in range(G)])


def _sqrelu_epi(acc):
    r = jnp.maximum(acc, 0.0)
    return acc, r * r


def _sqrelu_bwd_epi(acc, a):
    return (acc * (2.0 * jnp.maximum(a.astype(F32), 0.0)),)


def _mlp_fwd(x, g1, g2, l, g_pre, g_post, tag):
    T, D = x.shape
    h = rms_fwd(x, g_pre, name=f"{tag}_pre_norm")
    a, s = matmul(h, g1, name=f"{tag}_up", mnk=(T, D_FF, D), b_spec=_w1_cols(l), epi=_sqrelu_epi,
                  out_dtypes=(MXU_DTYPE, MXU_DTYPE))
    m = matmul(s, g2, name=f"{tag}_down", mnk=(T, D, D_FF), b_spec=_w2_rows(l))
    return rms_res_fwd(x, m, g_post, name=f"{tag}_post_norm"), (x, h, a, s, m)


def _mlp_bwd(saved, g1, g2, l, g_pre, g_post, dxo, dg1, dg2, tag):
    x, h, a, s, m = saved
    T, D = x.shape
    gshape = (N_CHIPS, 2 * MLP_SHARD, MLP_SHARD)
    dm, dg_post = rms_bwd(m, g_post, dxo, None, name=f"{tag}_post_norm_bwd")
    da = matmul(dm, g2, tb=True, name=f"{tag}_down_dx", mnk=(T, D_FF, D), b_spec=_w2_rows_t(l),
                epi=_sqrelu_bwd_epi, epi_in=(a,), out_dtype=MXU_DTYPE)
    dg2 = matmul(s, dm, ta=True, name=f"{tag}_down_dw", tm=MLP_SHARD, o_spec=_dw2_out(l), o_shape=gshape, prev=dg2)
    dh = matmul(da, g1, tb=True, name=f"{tag}_up_dx", mnk=(T, D, D_FF), b_spec=_w1_rows_t(l))
    dg1 = matmul(h, da, ta=True, name=f"{tag}_up_dw", o_spec=_dw1_out(l), o_shape=gshape, prev=dg1)
    dx, dg_pre = rms_bwd(x, g_pre, dh, dxo, name=f"{tag}_pre_norm_bwd")
    return dx, dg1, dg2, dg_pre, dg_post


def kernel(x, mix_pre_g, mix_post_g, mlp_pre_g, mlp_post_g, w_in_even, s5_lam_re, s5_lam_im, s5_log_dt, s5_b_re, s5_b_im, s5_c_re, s5_c_im, s5_d, s5_w_glu, fox_b_f, w_out_even, w_in_odd, pool_w, pool_scale, sgu_ln_g, sgu_ln_b, sgu_w_s, sgu_b_s, w_out_odd, mlp_w1, mlp_w2, loss_target, m_mix_pre_g, m_mix_post_g, m_mlp_pre_g, m_mlp_post_g, m_w_in_even, m_s5_lam_re, m_s5_lam_im, m_s5_log_dt, m_s5_b_re, m_s5_b_im, m_s5_c_re, m_s5_c_im, m_s5_d, m_s5_w_glu, m_fox_b_f, m_w_out_even, m_w_in_odd, m_pool_w, m_pool_scale, m_sgu_ln_g, m_sgu_ln_b, m_sgu_w_s, m_sgu_b_s, m_w_out_odd, m_mlp_w1, m_mlp_w2, v_mix_pre_g, v_mix_post_g, v_mlp_pre_g, v_mlp_post_g, v_w_in_even, v_s5_lam_re, v_s5_lam_im, v_s5_log_dt, v_s5_b_re, v_s5_b_im, v_s5_c_re, v_s5_c_im, v_s5_d, v_s5_w_glu, v_fox_b_f, v_w_out_even, v_w_in_odd, v_pool_w, v_pool_scale, v_sgu_ln_g, v_sgu_ln_b, v_sgu_w_s, v_sgu_b_s, v_w_out_odd, v_mlp_w1, v_mlp_w2):
    names = [n for n, _ in SMALL_SEGS] + [n for n, _, _ in BIG_SEGS] + list(SHARDED_SMALL)
    env = dict(locals())
    W = {n: env[n] for n in names}
    M = {n: env["m_" + n] for n in names}
    V = {n: env["v_" + n] for n in names}

    def shard(n):
        return W[n].reshape(-1, W[n].shape[-1]).astype(WIRE_DTYPE)

    small = jnp.pad(jnp.concatenate([W[n] for n in SHARDED_SMALL]), ((0, SUBLANES - len(SHARDED_SMALL)), (0, 0)))
    gathered, small_all = allgather_chip_shards([shard(n) for n in EARLY_NAMES], small, name="allgather_weights")
    Wf = dict(zip(EARLY_NAMES, gathered))
    for i, n in enumerate(SHARDED_SMALL):
        Wf[n] = small_all[:, i, :].reshape(1, N_CHIPS * LANES)
    for n, _ in SMALL_SEGS:
        Wf[n] = W[n]

    loss8, dx0, halves, dw_in_e, local_small = _local_step(x[0], loss_target[0], Wf, [shard(n) for n in LATE_NAMES])
    loss = lax.psum(loss8[0, 0], MESH_AXES)
    return _reduce_and_update(W, M, V, loss, dx0, halves, dw_in_e, local_small)


def _reduce_to_my_half(gs, names, tag, carry_swap=None, carry_ici=None):
    cx, cy, cc = _coords()
    c_idx = cc.reshape(1).astype(jnp.int32)
    jc_idx = jnp.stack([2 * cx + cy, cc]).astype(jnp.int32)
    swap = swap_halves_exchange(gs)
    from_sibling = carry_swap(swap) if carry_swap else run_exchange(swap, name=f"{tag}_to_sibling")
    sums = [add_sibling_half(g, l, c_idx, name=f"{tag}_chip_sum_{n}") for n, g, l in zip(names, gs, from_sibling)]
    send = chip_partials_exchange([pb for _, pb in sums])
    from_chips = carry_ici(send) if carry_ici else run_exchange(send, name=f"{tag}_to_chips")
    return [add_chip_partials(pf, r, jc_idx, name=f"{tag}_sum_{n}") for n, (pf, _), r in zip(names, sums, from_chips)]


def _local_step(x0, target, P, late_shards):
    T = x0.shape[0]
    mix_pre_g, mix_post_g, mlp_pre_g, mlp_post_g = P["mix_pre_g"], P["mix_post_g"], P["mlp_pre_g"], P["mlp_post_g"]
    s5_lam_re, s5_lam_im, s5_log_dt = P["s5_lam_re"], P["s5_lam_im"], P["s5_log_dt"]
    s5_b_re, s5_b_im, s5_c_re, s5_c_im, s5_d = P["s5_b_re"], P["s5_b_im"], P["s5_c_re"], P["s5_c_im"], P["s5_d"]
    fox_b_f, pool_w, sgu_w_s, sgu_b_s = P["fox_b_f"], P["pool_w"], P["sgu_w_s"], P["sgu_b_s"]
    pool_scale_f, ln_g_f, ln_b_f = P["pool_scale"], P["sgu_ln_g"], P["sgu_ln_b"]
    w_in_e = jnp.pad(_cols_from_chips(P["w_in_even"]), ((0, 0), (0, EVEN_IN_PAD - EVEN_IN)))
    w_glu = P["s5_w_glu"].reshape(S5_WIDTH, S5_WIDTH)
    w_out_e = P["w_out_even"].reshape(D_MODEL, D_MODEL)

    def gain(a, l):
        return a[l][None, :]

    lr = s5_lam_re[0].reshape(1, S5_LANES)
    li = s5_lam_im[0].reshape(1, S5_LANES)
    ldt = jnp.repeat(s5_log_dt[0], S5_STATE).reshape(1, S5_LANES)
    btr = s5_b_re[0].reshape(S5_LANES, S5_GROUP).T
    bti = s5_b_im[0].reshape(S5_LANES, S5_GROUP).T
    tf_re, tf_im, tb_re, tb_im, bbt_re, bbt_im = s5_disc_fwd(lr, li, ldt, btr, bti, name="s5_disc")
    same_group = (jnp.arange(S5_WIDTH)[:, None] // S5_GROUP) == (jnp.arange(S5_LANES)[None, :] // S5_STATE)
    b_bd = s5_interleave(jnp.where(same_group, jnp.tile(bbt_re, (S5_GROUPS, 1)), 0.0),
                         jnp.where(same_group, jnp.tile(bbt_im, (S5_GROUPS, 1)), 0.0), axis=1)
    cr2 = jnp.transpose(s5_c_re[0], (0, 2, 1)).reshape(S5_LANES, S5_GROUP)
    ci2 = jnp.transpose(s5_c_im[0], (0, 2, 1)).reshape(S5_LANES, S5_GROUP)
    c_bd = s5_interleave(jnp.where(same_group.T, jnp.tile(cr2, (1, S5_GROUPS)), 0.0),
                         -jnp.where(same_group.T, jnp.tile(ci2, (1, S5_GROUPS)), 0.0), axis=0)
    bf_pad = jnp.pad(fox_b_f, ((0, 0), (0, LANES - FOX_HEADS)))

    h1 = rms_fwd(x0, gain(mix_pre_g, 0), name="l0_pre_norm")
    z = matmul(h1, w_in_e, name="l0_in_proj")
    s5_tiles = dict(tm=_pick(T, S5_NB), exact_tiles=True)
    bu = matmul(z, b_bd, mnk=(T, 2 * S5_LANES, S5_CB), tn=S5_NB, a_spec=_chan_rows, b_spec=_s5_b_block,
                name="s5_bu", **s5_tiles)
    xs = s5_scan(bu, tf_re, tf_im, reverse=False, name="s5_scan_fwd")
    yc = matmul(xs, c_bd, mnk=(T, S5_WIDTH, 2 * S5_NB), tn=S5_CB, a_spec=_lanes_of_chan, b_spec=_s5_c_block,
                name="s5_cx", **s5_tiles)
    yl, yg = s5_out_fwd(yc, z, s5_d, name="s5_out")
    gl = matmul(yg, w_glu, name="s5_glu_proj")
    ycat = glu_fwd(yg, gl, out_cols=D_MODEL, name="s5_glu")
    fgate = fox_gate_fwd(z, bf_pad, fl_col=FL_TILE, name="fox_gate")
    f_col = _pairs_col(fgate, T)
    f_row = _col_to_row(f_col, T)
    (ycat, lse_col), late = fox_fwd(z, f_col, f_row, ycat, allgather_ici_exchange(late_shards), name="fox_fwd")
    late = dict(zip(LATE_NAMES, allgather_forward(late, name="allgather_late_weights")))
    w_in_o = _cols_from_chips(late["w_in_odd"])
    w_in_o = jnp.concatenate([w_in_o[:, S5_WIDTH:], w_in_o[:, :S5_WIDTH]], axis=1)
    w_out_o = late["w_out_odd"].reshape(D_MODEL, D_MODEL)
    g1, g2 = late["mlp_w1"], late["mlp_w2"]
    mo = matmul(ycat, w_out_e, name="l0_out_proj")
    x1 = rms_res_fwd(x0, mo, gain(mix_post_g, 0), name="l0_post_norm")
    x2, mlp0 = _mlp_fwd(x1, g1, g2, 0, gain(mlp_pre_g, 0), gain(mlp_post_g, 0), "mlp0")

    h3 = rms_fwd(x2, gain(mix_pre_g, 1), name="l1_pre_norm")
    z2 = matmul(h3, w_in_o, name="l1_in_proj")
    pooled = pool_window(z2, adjoint=False, in_col=POOL_COL, name="pool_fwd")
    pw_bd = _block_diag(pool_w[0])
    pw = matmul(pooled, pw_bd, name="pool_proj")
    ycat2 = colscale_fwd(pw, pool_scale_f, out_cols=D_MODEL, name="pool_scale")
    causal = jnp.tril(jnp.ones((CHUNK, CHUNK), dtype=bool))
    wsm = jnp.where(causal[None], sgu_w_s[0], 0.0)
    wsmt = jnp.transpose(wsm, (0, 2, 1))
    bst = sgu_b_s[0].T
    ycat2 = sgu_fwd(z2, ln_g_f, ln_b_f, wsm, bst, ycat2, name="sgu_fwd")
    mo2 = matmul(ycat2, w_out_o, name="l1_out_proj")
    x3 = rms_res_fwd(x2, mo2, gain(mix_post_g, 1), name="l1_post_norm")
    x4, mlp1 = _mlp_fwd(x3, g1, g2, 1, gain(mlp_pre_g, 1), gain(mlp_post_g, 1), "mlp1")

    loss8, dx4 = loss_fwd_bwd(x4, target, name="loss")

    dx3, dg1, dg2, dg_mlp_pre1, dg_mlp_post1 = _mlp_bwd(mlp1, g1, g2, 1, gain(mlp_pre_g, 1), gain(mlp_post_g, 1), dx4,
                                                        None, None, "mlp1")
    dmo2, dg_mix_post1 = rms_bwd(mo2, gain(mix_post_g, 1), dx3, None, name="l1_post_norm_bwd")
    dycat2 = matmul(dmo2, w_out_o, tb=True, name="l1_out_proj_dx")
    dw_out_o = matmul(ycat2, dmo2, ta=True, name="l1_out_proj_dw")
    dpw, dpool_scale = colscale_bwd(pw, pool_scale_f, dycat2, name="pool_scale_bwd")
    dpooled = matmul(dpw, pw_bd, tb=True, name="pool_proj_dx")
    dpw_bd = matmul(pooled, dpw, ta=True, name="pool_proj_dw")
    dz2, dln_g, dln_b, dws, dbst = sgu_bwd(z2, ln_g_f, ln_b_f, wsm, wsmt, bst, dycat2, out_cols=3 * S5_WIDTH,
                                           name="sgu_bwd")
    dz2 = pool_window(dpooled, adjoint=True, into=dz2, out_col=POOL_COL, name="pool_bwd")
    dh3 = matmul(dz2, w_in_o, tb=True, name="l1_in_proj_dx")
    dw_in_o = matmul(h3, dz2, ta=True, name="l1_in_proj_dw")
    dw_in_o = jnp.concatenate([dw_in_o[:, 2 * S5_WIDTH:], dw_in_o[:, :2 * S5_WIDTH]], axis=1)
    dx2, dg_mix_pre1 = rms_bwd(x2, gain(mix_pre_g, 1), dh3, dx3, name="l1_pre_norm_bwd")

    dx1, dg1, dg2, dg_mlp_pre0, dg_mlp_post0 = _mlp_bwd(mlp0, g1, g2, 0, gain(mlp_pre_g, 0), gain(mlp_post_g, 0), dx2,
                                                        dg1, dg2, "mlp0")
    dmo, dg_mix_post0 = rms_bwd(mo, gain(mix_post_g, 0), dx1, None, name="l0_post_norm_bwd")
    dycat = matmul(dmo, w_out_e, tb=True, name="l0_out_proj_dx")
    dw_out_e = matmul(ycat, dmo, ta=True, name="l0_out_proj_dw")
    dyg_a, dgl = glu_bwd(yg, gl, dycat, name="s5_glu_bwd")
    dyg_b = matmul(dgl, w_glu, tb=True, name="s5_glu_proj_dx")
    dw_glu = matmul(yg, dgl, ta=True, name="s5_glu_proj_dw")
    dyl, du_skip, dd = s5_out_bwd(yl, z, s5_d, dyg_a, dyg_b, name="s5_out_bwd")
    dxs = matmul(dyl, c_bd, tb=True, mnk=(T, 2 * S5_LANES, S5_CB), tn=S5_NB, a_spec=_chan_rows, b_spec=_s5_c_block_t,
                 name="s5_cx_dx", **s5_tiles)
    dc_blocks = matmul(xs, dyl, ta=True, mnk=(2 * S5_LANES, S5_CB, T), tm=S5_NB, tn=S5_CB, b_spec=_chan_cols_of_i,
                       exact_tiles=True, name="s5_cx_dw")
    lam = s5_scan(dxs, tb_re, tb_im, reverse=True, name="s5_scan_bwd")
    dab_re, dab_im = s5_da(lam, xs, name="s5_da")
    db_blocks = matmul(z, lam, ta=True, mnk=(S5_CB, 2 * S5_LANES, T), tm=S5_CB, tn=S5_NB, a_spec=_chan_rows_t,
                       exact_tiles=True, name="s5_bu_dw")
    du_b = matmul(lam, b_bd, tb=True, mnk=(T, S5_WIDTH, 2 * S5_NB), tn=S5_CB, a_spec=_lanes_of_chan,
                  b_spec=_s5_b_block_t, name="s5_bu_dx", **s5_tiles)
    du = add2(du_skip, du_b, name="s5_du")
    early_grads = {"s5_w_glu": dw_glu.reshape(N_CHIPS, -1, S5_WIDTH), "w_out_even": dw_out_e.reshape(N_CHIPS, -1, D_MODEL),
                   "w_in_odd": _chips_from_cols(dw_in_o), "w_out_odd": dw_out_o.reshape(N_CHIPS, -1, D_MODEL),
                   "mlp_w1": dg1, "mlp_w2": dg2}
    fox = {}

    def attention_bwd_q(exchange):
        (fox["dq"], fox["dd"], fox["dfq"]), bufs = fox_bwd_q(z, ycat, dycat, f_col, f_row, lse_col, exchange, name="fox_bwd_q")
        return bufs

    def attention_bwd_kv(exchange):
        (fox["dk"], fox["dv"], fox["df"]), bufs = fox_bwd_kv(z, dycat, f_col, f_row, _col_to_row(lse_col, T),
                                                             _col_to_row(fox["dd"], T), fox["dfq"], exchange, name="fox_bwd_kv")
        return bufs

    halves = _reduce_to_my_half([early_grads[n] for n in REDUCED_EARLY], REDUCED_EARLY, "early_grads",
                                attention_bwd_q, attention_bwd_kv)
    dq, dk, dv = fox["dq"], fox["dk"], fox["dv"]
    dfl, dbf = fox_gate_bwd(z, bf_pad, _pairs_to_lanes(fox["df"], T), fl_col=FL_TILE, name="fox_gate_bwd")
    dz = jnp.concatenate([du, dq, dk, dv, dfl], axis=1)
    dh1 = matmul(dz, w_in_e, tb=True, name="l0_in_proj_dx")
    dw_in_e = matmul(h1, dz, ta=True, name="l0_in_proj_dw")[:, :EVEN_IN]
    dx0, dg_mix_pre0 = rms_bwd(x0, gain(mix_pre_g, 0), dh1, dx1, name="l0_pre_norm_bwd")

    groups_per_block = S5_CB // S5_GROUP
    own_group = (jnp.arange(S5_CB)[:, None] // S5_GROUP) == ((jnp.arange(S5_LANES)[None, :] // S5_STATE) % groups_per_block)
    db_re, db_im = s5_deinterleave(db_blocks, axis=1)
    dbbt_re = jnp.where(own_group, db_re, 0.0).reshape(groups_per_block, S5_GROUP, S5_LANES).sum(0)
    dbbt_im = jnp.where(own_group, db_im, 0.0).reshape(groups_per_block, S5_GROUP, S5_LANES).sum(0)
    dlr, dli, dldt8, dbtr, dbti = s5_disc_bwd(lr, li, ldt, btr, bti, dab_re, dab_im, dbbt_re, dbbt_im, name="s5_disc_bwd")
    dc_re, dc_im = s5_deinterleave(dc_blocks, axis=0)
    dcr2 = jnp.where(own_group.T, dc_re, 0.0).reshape(S5_LANES, groups_per_block, S5_GROUP).sum(1)
    dci2 = -jnp.where(own_group.T, dc_im, 0.0).reshape(S5_LANES, groups_per_block, S5_GROUP).sum(1)

    def c_layout(a):
        return jnp.transpose(a.reshape(S5_GROUPS, S5_STATE, S5_GROUP), (0, 2, 1))[None]

    def b_layout(a):
        return a.T.reshape(1, S5_GROUPS, S5_STATE, S5_GROUP)

    local_small = {
        "mix_pre_g": jnp.concatenate([dg_mix_pre0, dg_mix_pre1]), "mix_post_g": jnp.concatenate([dg_mix_post0, dg_mix_post1]),
        "mlp_pre_g": jnp.concatenate([dg_mlp_pre0, dg_mlp_pre1]), "mlp_post_g": jnp.concatenate([dg_mlp_post0, dg_mlp_post1]),
        "s5_lam_re": dlr.reshape(1, S5_GROUPS, S5_STATE), "s5_lam_im": dli.reshape(1, S5_GROUPS, S5_STATE),
        "s5_log_dt": dldt8[0:1, 0:S5_GROUPS],
        "s5_b_re": b_layout(dbtr), "s5_b_im": b_layout(dbti), "s5_c_re": c_layout(dcr2), "s5_c_im": c_layout(dci2),
        "s5_d": dd, "fox_b_f": dbf[:, 0:FOX_HEADS],
        "pool_w": _diag_blocks(dpw_bd, len(POOL_WINDOWS))[None],
        "sgu_w_s": jnp.where(causal[None], dws, 0.0)[None], "sgu_b_s": dbst.T[None],
        "pool_scale": dpool_scale, "sgu_ln_g": dln_g, "sgu_ln_b": dln_b,
    }
    return loss8, dx0, dict(zip(REDUCED_EARLY, halves)), _chips_from_cols(dw_in_e), local_small


def _reduce_and_update(W, M, V, loss, dx0, halves, dw_in_e, local_small):
    cx, cy, cc = _coords()
    chip = 2 * cx + cy

    vec = _pack_vec(local_small, REDUCED_SEGS, N_DEV * SUBLANES)
    piece = vec.shape[0] // N_DEV
    landed = exchange_pieces(vec.reshape(N_DEV, piece, LANES), scatter=True, name="small_grads_scatter")
    mine = sum_pieces(landed, name="small_grads_sum")
    everyone = exchange_pieces(mine, scatter=False, name="small_grads_gather")
    G = _unpack_vec(everyone, REDUCED_SEGS)
    for n in SHARDED_SMALL:
        G[n] = lax.dynamic_slice_in_dim(G[n], chip * LANES, LANES, axis=1)

    halves = dict(halves)
    halves["w_in_even"] = _reduce_to_my_half([dw_in_e], ["w_in_even"], "late_grads")[0]
    reduced = join_sibling_halves([halves[n] for n in BIG_NAMES], name="big_grads_join")
    for n, r in zip(BIG_NAMES, reduced):
        G[n] = r.reshape(W[n].shape)

    def two_d(a):
        return a.reshape(-1, a.shape[-1])

    delta, new_m, new_v = {}, {}, {}
    for n in BIG_NAMES:
        d_, m_, v_ = adamw(two_d(W[n]), two_d(G[n]), two_d(M[n]), two_d(V[n]), name=f"adamw_{n}")
        delta[n], new_m[n], new_v[n] = (t.reshape(W[n].shape) for t in (d_, m_, v_))
    packed = [_pack_vec(src, SMALL_SEGS, SUBLANES) for src in (W, G, M, V)]
    outs = adamw(*packed, name="adamw_replicated")
    for dst, t in zip((delta, new_m, new_v), outs):
        dst.update(_unpack_vec(t, SMALL_SEGS))
    sharded_segs = tuple((n, (1, LANES)) for n in SHARDED_SMALL)
    packed = [_pack_vec(src, sharded_segs, 1) for src in (W, G, M, V)]
    outs = adamw(*packed, name="adamw_sharded_vectors")
    for dst, t in zip((delta, new_m, new_v), outs):
        dst.update(_unpack_vec(t, sharded_segs))

    order = ["mix_pre_g", "mix_post_g", "mlp_pre_g", "mlp_post_g", "w_in_even", "s5_lam_re", "s5_lam_im", "s5_log_dt",
             "s5_b_re", "s5_b_im", "s5_c_re", "s5_c_im", "s5_d", "s5_w_glu", "fox_b_f", "w_out_even", "w_in_odd",
             "pool_w", "pool_scale", "sgu_ln_g", "sgu_ln_b", "sgu_w_s", "sgu_b_s", "w_out_odd", "mlp_w1", "mlp_w2"]
    return (loss, dx0[None], *[G[n] for n in order], *[delta[n] for n in order],
            *[new_m[n] for n in order], *[new_v[n] for n in order])
```

```python
import functools
import math

import jax
import jax.numpy as jnp
from jax import lax
from jax.experimental import pallas as pl
from jax.experimental.pallas import tpu as pltpu

F32 = jnp.float32
MXU_DTYPE = jnp.bfloat16
WIRE_DTYPE = jnp.bfloat16
EPS = 1e-6
VMEM_LIMIT_BYTES = 48 * 1024 * 1024
LANES = 128
SUBLANES = 8

D_MODEL = 1024
S5_WIDTH = 512
S5_GROUP = 16
S5_GROUPS = 32
S5_STATE = 64
S5_LANES = S5_GROUPS * S5_STATE
FOX_HEADS = 8
FOX_HEAD_DIM = 64
FOX_WIDTH = 512
EVEN_IN = S5_WIDTH + 3 * FOX_WIDTH + FOX_HEADS
EVEN_IN_PAD = 2176
POOL_WINDOWS = (2, 4, 8, 16)
POOL_HALO = 16
POOL_GROUP_DIM = 128
SGU_GROUPS = 4
SGU_GROUP_DIM = 128
CHUNK = 128
D_FF = 4096

ADAM_LR = 0.001
ADAM_B1 = 0.9
ADAM_B2 = 0.999
ADAM_EPS = 1e-08
ADAM_WD = 0.01
ADAM_STEP = 10

MESH_AXES = ("x", "y", "c")
MESH = pl.DeviceIdType.MESH
N_CHIPS = 4
N_DEV = 8

SDS = jax.ShapeDtypeStruct


def _cp(*sem):
    return pltpu.CompilerParams(dimension_semantics=sem, vmem_limit_bytes=VMEM_LIMIT_BYTES)


def _pick(dim, pref):
    if dim <= pref:
        return dim
    t = pref
    while t >= 256:
        if dim % t == 0:
            return t
        t //= 2
    return dim


def _row(tr, c):
    return pl.BlockSpec((tr, c), lambda i: (i, 0))


def _full(shape):
    nd = len(shape)
    return pl.BlockSpec(shape, lambda *_: (0,) * nd)


def _gelu_grad(x):
    c = math.sqrt(2.0 / math.pi)
    t = jnp.tanh(c * (x + 0.044715 * x * x * x))
    return 0.5 * (1.0 + t) + 0.5 * x * (1.0 - t * t) * c * (1.0 + 3.0 * 0.044715 * x * x)


MATMUL_VMEM_BYTES = 36 * 1024 * 1024


def matmul(a, b, *, name, ta=False, tb=False, out_dtype=F32, tm=2048, tn=1024, tk=4096, mnk=None, a_koff=0,
           a_spec=None, b_spec=None, o_spec=None, o_shape=None, prev=None, epi=None, epi_in=(), out_dtypes=None,
           exact_tiles=False):
    if mnk is None:
        M, K = (a.shape[1], a.shape[0]) if ta else a.shape
        K2, N = (b.shape[1], b.shape[0]) if tb else b.shape
        assert K == K2, (a.shape, b.shape, ta, tb)
    else:
        M, N, K = mnk
    out_dtypes = tuple(out_dtypes) if out_dtypes is not None else (out_dtype,)
    n_out, n_epi = len(out_dtypes), len(epi_in)
    tm, tn, tk = _pick(M, tm), _pick(N, tn), _pick(K, tk)

    def vmem_bytes(tm_, tn_, tk_):
        tiles = tm_ * tk_ * a.dtype.itemsize + tk_ * tn_ * b.dtype.itemsize
        tiles += tm_ * tn_ * (sum(jnp.dtype(d).itemsize for d in out_dtypes) + sum(e.dtype.itemsize for e in epi_in))
        return 2 * tiles + tm_ * tn_ * 4 * (tk_ < K)

    def halves(t, dim):
        return [t] + ([t // 2] if t % (2 * LANES) == 0 and t // 2 >= 512 and dim % (t // 2) == 0 else [])

    if exact_tiles:
        halves = lambda t, dim: [t]
    fits = [(m_, n_) for m_ in halves(tm, M) for n_ in halves(tn, N) if vmem_bytes(m_, n_, tk) <= MATMUL_VMEM_BYTES]
    if fits:
        tm, tn = max(fits, key=lambda t: (t[0] * t[1], t[0]))
    else:
        tm, tn = halves(tm, M)[-1], halves(tn, N)[-1]
        while vmem_bytes(tm, tn, tk) > MATMUL_VMEM_BYTES and tk % 2 == 0 and tk > 512:
            tk //= 2
    nk = K // tk
    assert a_koff % tk == 0 and not (ta and a_koff)
    ko = a_koff // tk
    dn = (((0 if ta else 1,), (1 if tb else 0,)), ((), ()))

    def body(*refs):
        a_ref, b_ref = refs[0], refs[1]
        epi_refs = refs[2:2 + n_epi]
        o_refs = refs[len(refs) - n_out - (nk > 1):len(refs) - (nk > 1)]
        k = pl.program_id(2)
        bv = b_ref[...]
        if bv.ndim == 3 and tb:
            cw = bv.shape[-1]
            prod = sum(lax.dot_general(a_ref[:, c * cw:(c + 1) * cw].astype(MXU_DTYPE), bv[c].astype(MXU_DTYPE), dn,
                                       preferred_element_type=F32) for c in range(bv.shape[0]))
        else:
            if bv.ndim == 3:
                bv = bv.reshape(-1, bv.shape[-1])
            prod = lax.dot_general(a_ref[...].astype(MXU_DTYPE), bv.astype(MXU_DTYPE), dn, preferred_element_type=F32)

        def finish(acc):
            res = (acc,) if epi is None else epi(acc, *[r[...] for r in epi_refs])
            for o_ref, r in zip(o_refs, res):
                o_ref[...] = r.astype(o_ref.dtype)

        if nk == 1:
            finish(prod)
            return
        acc_ref = refs[-1]

        @pl.when(k == 0)
        def _():
            acc_ref[...] = prod

        @pl.when(jnp.logical_and(k > 0, k < nk - 1))
        def _():
            acc_ref[...] += prod

        @pl.when(k == nk - 1)
        def _():
            finish(acc_ref[...] + prod)

    if a_spec is None:
        a_spec = pl.BlockSpec((tk, tm), lambda i, j, k: (k, i)) if ta else pl.BlockSpec((tm, tk), lambda i, j, k: (i, k + ko))
    else:
        a_spec = a_spec(tm, tn, tk)
    if b_spec is None:
        bs = pl.BlockSpec((tn, tk), lambda i, j, k: (j, k)) if tb else pl.BlockSpec((tk, tn), lambda i, j, k: (k, j))
    else:
        bs = b_spec(tm, tn, tk)
    tile = pl.BlockSpec((tm, tn), lambda i, j, k: (i, j))
    os_ = tile if o_spec is None else o_spec(tm, tn, tk)
    ins, in_specs, aliases = [a, b, *epi_in], [a_spec, bs] + [tile] * n_epi, {}
    if prev is not None:
        aliases = {len(ins): 0}
        ins.append(prev)
        in_specs.append(pl.BlockSpec(memory_space=pl.ANY))
    shapes = [SDS((M, N) if o_shape is None else o_shape, dt) for dt in out_dtypes]
    outs = pl.pallas_call(
        body, name=name, grid=(M // tm, N // tn, nk),
        in_specs=in_specs, out_specs=[os_] * n_out, out_shape=shapes, input_output_aliases=aliases,
        scratch_shapes=[pltpu.VMEM((tm, tn), F32)] if nk > 1 else [],
        compiler_params=_cp("parallel", "parallel", "arbitrary"),
    )(*ins)
    return outs[0] if n_out == 1 else outs


def _rms_hat(x):
    return x * lax.rsqrt(jnp.mean(x * x, axis=-1, keepdims=True) + EPS)


def rms_fwd(x, g, *, name):
    T, D = x.shape
    tr = _pick(T, 512)

    def body(x_ref, g_ref, o_ref):
        o_ref[...] = (_rms_hat(x_ref[...]) * g_ref[...]).astype(o_ref.dtype)

    return pl.pallas_call(body, name=name, grid=(T // tr,), in_specs=[_row(tr, D), _full((1, D))],
                          out_specs=_row(tr, D), out_shape=SDS((T, D), MXU_DTYPE), compiler_params=_cp("parallel"))(x, g)


def res_norm_fwd(x, y, g_post, g_next, *, name):
    T, D = x.shape
    tr = _pick(T, 512)

    def body(x_ref, y_ref, gp_ref, gn_ref, o_ref, h_ref):
        xn = x_ref[...] + _rms_hat(y_ref[...]) * gp_ref[...]
        o_ref[...] = xn
        h_ref[...] = (_rms_hat(xn) * gn_ref[...]).astype(h_ref.dtype)

    return pl.pallas_call(body, name=name, grid=(T // tr,),
                          in_specs=[_row(tr, D), _row(tr, D), _full((1, D)), _full((1, D))],
                          out_specs=[_row(tr, D), _row(tr, D)], out_shape=[SDS((T, D), F32), SDS((T, D), MXU_DTYPE)],
                          compiler_params=_cp("parallel"))(x, y, g_post, g_next)


def res_norm_loss(x, y, g_post, target, *, name):
    T, D = x.shape
    tr = _pick(T, 512)

    def body(x_ref, y_ref, g_ref, t_ref, l_ref, d_ref):
        err = x_ref[...] + _rms_hat(y_ref[...]) * g_ref[...] - t_ref[...]
        d_ref[...] = err * (1.0 / D)

        @pl.when(pl.program_id(0) == 0)
        def _():
            l_ref[...] = jnp.zeros_like(l_ref)

        l_ref[...] += 0.5 * jnp.sum(jnp.mean(err * err, axis=-1, keepdims=True))

    return pl.pallas_call(body, name=name, grid=(T // tr,),
                          in_specs=[_row(tr, D), _row(tr, D), _full((1, D)), _row(tr, D)],
                          out_specs=[_full((SUBLANES, LANES)), _row(tr, D)],
                          out_shape=[SDS((SUBLANES, LANES), F32), SDS((T, D), F32)],
                          compiler_params=_cp("arbitrary"))(x, y, g_post, target)


def _rms_bwd_rows(x, g, dy):
    r = lax.rsqrt(jnp.mean(x * x, axis=-1, keepdims=True) + EPS)
    xh = x * r
    dxh = dy * g
    return r * (dxh - xh * jnp.mean(dxh * xh, axis=-1, keepdims=True)), jnp.sum(dy * xh, axis=0, keepdims=True)


def norm_res_bwd(x, g_pre, dh, res, y, g_post, *, name):
    T, D = x.shape
    tr = _pick(T, 512)

    def body(x_ref, gp_ref, dh_ref, res_ref, y_ref, gy_ref, dx_ref, dy_ref, dgp_ref, dgy_ref):
        dx, dgp = _rms_bwd_rows(x_ref[...], gp_ref[...], dh_ref[...])
        dx = dx + res_ref[...]
        dx_ref[...] = dx
        dy, dgy = _rms_bwd_rows(y_ref[...], gy_ref[...], dx)
        dy_ref[...] = dy.astype(dy_ref.dtype)

        @pl.when(pl.program_id(0) == 0)
        def _():
            dgp_ref[...] = jnp.zeros_like(dgp_ref)
            dgy_ref[...] = jnp.zeros_like(dgy_ref)

        dgp_ref[...] += dgp
        dgy_ref[...] += dgy

    row, vec = _row(tr, D), _full((1, D))
    return pl.pallas_call(body, name=name, grid=(T // tr,), in_specs=[row, vec, row, row, row, vec],
                          out_specs=[row, row, vec, vec],
                          out_shape=[SDS((T, D), F32), SDS((T, D), MXU_DTYPE), SDS((1, D), F32), SDS((1, D), F32)],
                          compiler_params=_cp("arbitrary"))(x, g_pre, dh, res, y, g_post)


def rms_bwd(x, g, dy, res, *, name):
    T, D = x.shape
    tr = _pick(T, 512)
    has_res = res is not None

    def body(*refs):
        if has_res:
            x_ref, g_ref, dy_ref, res_ref, dx_ref, dg_ref = refs
        else:
            x_ref, g_ref, dy_ref, dx_ref, dg_ref = refs
        dx, dg = _rms_bwd_rows(x_ref[...], g_ref[...], dy_ref[...])
        if has_res:
            dx = dx + res_ref[...]
        dx_ref[...] = dx.astype(dx_ref.dtype)

        @pl.when(pl.program_id(0) == 0)
        def _():
            dg_ref[...] = jnp.zeros_like(dg_ref)

        dg_ref[...] += dg

    ins = [x, g, dy] + ([res] if has_res else [])
    in_specs = [_row(tr, D), _full((1, D)), _row(tr, D)] + ([_row(tr, D)] if has_res else [])
    return pl.pallas_call(body, name=name, grid=(T // tr,), in_specs=in_specs,
                          out_specs=[_row(tr, D), _full((1, D))],
                          out_shape=[SDS((T, D), F32 if has_res else MXU_DTYPE), SDS((1, D), F32)],
                          compiler_params=_cp("arbitrary"))(*ins)


def _s5_disc(lr, li, ldt, btr, bti):
    dt = jnp.exp(ldt)
    k = lax.broadcasted_iota(jnp.int32, (SUBLANES, S5_LANES), 0).astype(F32)
    kf = k + 1.0
    kb = 8.0 - k
    ph = li * dt
    lm = lr * dt
    tf_re = jnp.exp(kf * lm) * jnp.cos(kf * ph)
    tf_im = jnp.exp(kf * lm) * jnp.sin(kf * ph)
    tb_re = jnp.exp(kb * lm) * jnp.cos(kb * ph)
    tb_im = -jnp.exp(kb * lm) * jnp.sin(kb * ph)
    mag = jnp.exp(lm)
    ab_re = mag * jnp.cos(ph)
    ab_im = mag * jnp.sin(ph)
    den = lr * lr + li * li
    nr = ab_re - 1.0
    ni = ab_im
    q_re = (nr * lr + ni * li) / den
    q_im = (ni * lr - nr * li) / den
    bbt_re = q_re * btr - q_im * bti
    bbt_im = q_re * bti + q_im * btr
    return tf_re, tf_im, tb_re, tb_im, bbt_re, bbt_im


def _s5_disc_core(lr, li, ldt, btr, bti):
    dt = jnp.exp(ldt)
    mag = jnp.exp(lr * dt)
    ab_re = mag * jnp.cos(li * dt)
    ab_im = mag * jnp.sin(li * dt)
    den = lr * lr + li * li
    nr = ab_re - 1.0
    ni = ab_im
    q_re = (nr * lr + ni * li) / den
    q_im = (ni * lr - nr * li) / den
    return ab_re, ab_im, q_re * btr - q_im * bti, q_re * bti + q_im * btr


def s5_disc_fwd(lr, li, ldt, btr, bti, *, name):
    def body(lr_ref, li_ref, ldt_ref, btr_ref, bti_ref, *outs):
        vals = _s5_disc(lr_ref[...], li_ref[...], ldt_ref[...], btr_ref[...], bti_ref[...])
        for o, v in zip(outs, vals):
            o[...] = v

    tab = SDS((SUBLANES, S5_LANES), F32)
    bb = SDS((S5_GROUP, S5_LANES), F32)
    return pl.pallas_call(body, name=name, out_shape=[tab, tab, tab, tab, bb, bb])(lr, li, ldt, btr, bti)


def s5_disc_bwd(lr, li, ldt, btr, bti, dab_re, dab_im, dbbt_re, dbbt_im, *, name):
    def body(lr_ref, li_ref, ldt_ref, btr_ref, bti_ref, dar_ref, dai_ref, dbr_ref, dbi_ref,
             dlr_ref, dli_ref, dldt_ref, dbtr_ref, dbti_ref):
        _, vjp = jax.vjp(_s5_disc_core, lr_ref[...], li_ref[...], ldt_ref[...], btr_ref[...], bti_ref[...])
        dlr, dli, dldt, dbtr, dbti = vjp((dar_ref[...], dai_ref[...], dbr_ref[...], dbi_ref[...]))
        dlr_ref[...] = dlr
        dli_ref[...] = dli
        dbtr_ref[...] = dbtr
        dbti_ref[...] = dbti
        lane_group = lax.broadcasted_iota(jnp.int32, (S5_LANES, LANES), 0) // S5_STATE
        col = lax.broadcasted_iota(jnp.int32, (S5_LANES, LANES), 1)
        ind = (lane_group == col).astype(F32)
        dldt_ref[...] = jnp.dot(jnp.broadcast_to(dldt, (SUBLANES, S5_LANES)), ind,
                                precision=lax.Precision.HIGHEST, preferred_element_type=F32)

    row = SDS((1, S5_LANES), F32)
    bb = SDS((S5_GROUP, S5_LANES), F32)
    return pl.pallas_call(body, name=name, out_shape=[row, row, SDS((SUBLANES, LANES), F32), bb, bb])(
        lr, li, ldt, btr, bti, dab_re, dab_im, dbbt_re, dbbt_im)


S5_NB = 1024


S5_CB = S5_WIDTH * S5_NB // S5_LANES


def _chan_rows(tm, tn, tk):
    return pl.BlockSpec((tm, S5_CB), lambda i, j, k: (i, j // 2))


def _chan_rows_t(tm, tn, tk):
    return pl.BlockSpec((tk, S5_CB), lambda i, j, k: (k, j // 2))


def _chan_cols_of_i(tm, tn, tk):
    return pl.BlockSpec((tk, S5_CB), lambda i, j, k: (k, i // 2))


def _s5_b_block(tm, tn, tk):
    return pl.BlockSpec((S5_CB, S5_NB), lambda i, j, k: (j // 2, j))


def _s5_c_block_t(tm, tn, tk):
    return pl.BlockSpec((S5_NB, S5_CB), lambda i, j, k: (j, j // 2))


def _lanes_of_chan(tm, tn, tk):
    return pl.BlockSpec((tm, 2 * S5_NB), lambda i, j, k: (i, j))


def _s5_b_block_t(tm, tn, tk):
    return pl.BlockSpec((S5_CB, 2 * S5_NB), lambda i, j, k: (j, j))


def _s5_c_block(tm, tn, tk):
    return pl.BlockSpec((2 * S5_NB, S5_CB), lambda i, j, k: (j, j))


def s5_interleave(re, im, axis):
    parts = []
    for n in range(S5_LANES // S5_NB):
        sl = [slice(None)] * re.ndim
        sl[axis] = slice(n * S5_NB, (n + 1) * S5_NB)
        parts += [re[tuple(sl)], im[tuple(sl)]]
    return jnp.concatenate(parts, axis=axis)


def s5_deinterleave(a, axis):
    re, im = [], []
    for n in range(S5_LANES // S5_NB):
        sl = [slice(None)] * a.ndim
        sl[axis] = slice(2 * n * S5_NB, (2 * n + 1) * S5_NB)
        re.append(a[tuple(sl)])
        sl[axis] = slice((2 * n + 1) * S5_NB, (2 * n + 2) * S5_NB)
        im.append(a[tuple(sl)])
    return jnp.concatenate(re, axis=axis), jnp.concatenate(im, axis=axis)


def s5_scan(bu, tab_re, tab_im, *, reverse, name, states=None):
    T = bu.shape[0]
    nb = S5_NB
    tc = _pick(T, 256)
    nl = S5_LANES // nb
    nt = T // tc
    ntile = tc // SUBLANES
    with_da = states is not None
    assert reverse or not with_da
    step_rows = ((1, 7), (2, 6), (4, 4)) if reverse else ((1, 0), (2, 1), (4, 3))

    def body(*refs):
        if with_da:
            (br_ref, bi_ref, tr_ref, ti_ref, sr_ref, si_ref, hr_ref, hi_ref, xo_ref, dar_ref, dai_ref,
             cr_ref, ci_ref, mr_ref, mi_ref, ar_ref, ai_ref) = refs
        else:
            br_ref, bi_ref, tr_ref, ti_ref, xo_ref, cr_ref, ci_ref, mr_ref, mi_ref = refs

        @pl.when(pl.program_id(1) == 0)
        def _():
            cr_ref[...] = jnp.zeros_like(cr_ref)
            ci_ref[...] = jnp.zeros_like(ci_ref)
            if with_da:
                ar_ref[...] = jnp.zeros_like(ar_ref)
                ai_ref[...] = jnp.zeros_like(ai_ref)

        seen = jnp.where(pl.program_id(1) < nt - 1, 1.0, 0.0)

        def add_da(lr, li, r0, last_r, last_i):
            first = lax.broadcasted_iota(jnp.int32, (SUBLANES, nb), 0) == 0
            pr = jnp.where(first, last_r, pltpu.roll(sr_ref[pl.ds(r0, SUBLANES), :], 1, 0))
            pi = jnp.where(first, last_i, pltpu.roll(si_ref[pl.ds(r0, SUBLANES), :], 1, 0))
            ar_ref[...] += lr * pr + li * pi
            ai_ref[...] += li * pr - lr * pi

        io = lax.broadcasted_iota(jnp.int32, (SUBLANES, nb), 0)
        for s_, (d, r) in enumerate(step_rows):
            keep = (io < SUBLANES - d) if reverse else (io >= d)
            mr_ref[s_] = jnp.where(keep, tr_ref[r:r + 1, :], 0.0)
            mi_ref[s_] = jnp.where(keep, ti_ref[r:r + 1, :], 0.0)

        def tile(i, carry):
            cr, ci = carry
            j = (ntile - 1 - i) if reverse else i
            r0 = pl.multiple_of(j * SUBLANES, SUBLANES)
            xr = br_ref[pl.ds(r0, SUBLANES), :]
            xi = bi_ref[pl.ds(r0, SUBLANES), :]
            for s_, (d, _) in enumerate(step_rows):
                sh = (SUBLANES - d) if reverse else d
                sr = pltpu.roll(xr, sh, 0)
                si = pltpu.roll(xi, sh, 0)
                pr, pi = mr_ref[s_], mi_ref[s_]
                xr, xi = xr + pr * sr - pi * si, xi + pr * si + pi * sr
            tr, ti = tr_ref[...], ti_ref[...]
            xr, xi = xr + tr * cr - ti * ci, xi + tr * ci + ti * cr
            xo_ref[pl.ds(r0, SUBLANES), 0:nb] = xr
            xo_ref[pl.ds(r0, SUBLANES), nb:2 * nb] = xi
            if with_da:
                @pl.when(j > 0)
                def _():
                    p0 = pl.multiple_of(r0 - SUBLANES, SUBLANES)
                    add_da(xr, xi, r0, sr_ref[pl.ds(p0, SUBLANES), :][SUBLANES - 1:SUBLANES, :],
                           si_ref[pl.ds(p0, SUBLANES), :][SUBLANES - 1:SUBLANES, :])

                @pl.when(j == 0)
                def _():
                    add_da(xr, xi, r0, hr_ref[SUBLANES - 1:SUBLANES, :] * seen, hi_ref[SUBLANES - 1:SUBLANES, :] * seen)
            if reverse:
                return xr[0:1, :], xi[0:1, :]
            return xr[SUBLANES - 1:SUBLANES, :], xi[SUBLANES - 1:SUBLANES, :]

        cr, ci = lax.fori_loop(0, ntile, tile, (cr_ref[0:1, :], ci_ref[0:1, :]))
        cr_ref[0:1, :] = cr
        ci_ref[0:1, :] = ci
        if with_da:
            @pl.when(pl.program_id(1) == nt - 1)
            def _():
                dar_ref[...] = jnp.sum(ar_ref[...], axis=0, keepdims=True)
                dai_ref[...] = jnp.sum(ai_ref[...], axis=0, keepdims=True)

    def tmap(t):
        return (nt - 1 - t) if reverse else t

    hb = tc // SUBLANES
    re_spec = pl.BlockSpec((tc, nb), lambda n, t: (tmap(t), 2 * n))
    im_spec = pl.BlockSpec((tc, nb), lambda n, t: (tmap(t), 2 * n + 1))
    tab_spec = pl.BlockSpec((SUBLANES, nb), lambda n, t: (0, n))
    out_spec = pl.BlockSpec((tc, 2 * nb), lambda n, t: (tmap(t), n))
    out_shape = SDS((T, 2 * S5_LANES), F32)
    scratch = [pltpu.VMEM((SUBLANES, nb), F32), pltpu.VMEM((SUBLANES, nb), F32),
               pltpu.VMEM((len(step_rows), SUBLANES, nb), F32), pltpu.VMEM((len(step_rows), SUBLANES, nb), F32)]
    if not with_da:
        return pl.pallas_call(body, name=name, grid=(nl, nt), in_specs=[re_spec, im_spec, tab_spec, tab_spec],
                              out_specs=out_spec, out_shape=out_shape, scratch_shapes=scratch,
                              compiler_params=_cp("parallel", "arbitrary"))(bu, bu, tab_re, tab_im)
    re_halo = pl.BlockSpec((SUBLANES, nb), lambda n, t: (jnp.maximum(tmap(t) * hb - 1, 0), 2 * n))
    im_halo = pl.BlockSpec((SUBLANES, nb), lambda n, t: (jnp.maximum(tmap(t) * hb - 1, 0), 2 * n + 1))
    acc = pl.BlockSpec((1, nb), lambda n, t: (0, n))
    row = SDS((1, S5_LANES), F32)
    return pl.pallas_call(
        body, name=name, grid=(nl, nt),
        in_specs=[re_spec, im_spec, tab_spec, tab_spec, re_spec, im_spec, re_halo, im_halo],
        out_specs=[out_spec, acc, acc], out_shape=[out_shape, row, row],
        scratch_shapes=scratch + [pltpu.VMEM((SUBLANES, nb), F32), pltpu.VMEM((SUBLANES, nb), F32)],
        compiler_params=_cp("parallel", "arbitrary"))(bu, bu, tab_re, tab_im, states, states, states, states)


def s5_out_fwd(yc, u, d, *, name):
    T, C = yc.shape
    tr = _pick(T, 512)

    def body(yc_ref, u_ref, d_ref, yl_ref, yg_ref):
        yl = yc_ref[...] + d_ref[...] * u_ref[...]
        yl_ref[...] = yl
        yg_ref[...] = jax.nn.gelu(yl)

    return pl.pallas_call(body, name=name, grid=(T // tr,), in_specs=[_row(tr, C), _row(tr, C), _full((1, C))],
                          out_specs=[_row(tr, C)] * 2, out_shape=[SDS((T, C), F32)] * 2,
                          compiler_params=_cp("parallel"))(yc, u, d)


def glu_fwd(yg, gl, *, out_cols, name):
    T, C = yg.shape
    tr = _pick(T, 512)

    def body(yg_ref, gl_ref, o_ref):
        o_ref[...] = yg_ref[...] * jax.nn.sigmoid(gl_ref[...])

    return pl.pallas_call(body, name=name, grid=(T // tr,), in_specs=[_row(tr, C)] * 2, out_specs=_row(tr, C),
                          out_shape=SDS((T, out_cols), F32), compiler_params=_cp("parallel"))(yg, gl)


def glu_bwd(yg, gl, dy, *, name):
    T, C = yg.shape
    tr = _pick(T, 512)

    def body(yg_ref, gl_ref, dy_ref, dyg_ref, dgl_ref):
        s = jax.nn.sigmoid(gl_ref[...])
        dyv = dy_ref[...]
        dyg_ref[...] = dyv * s
        dgl_ref[...] = dyv * yg_ref[...] * s * (1.0 - s)

    return pl.pallas_call(body, name=name, grid=(T // tr,), in_specs=[_row(tr, C)] * 3, out_specs=[_row(tr, C)] * 2,
                          out_shape=[SDS((T, C), F32)] * 2, compiler_params=_cp("parallel"))(yg, gl, dy)


def s5_out_bwd(yl, u, d, dyg_a, dyg_b, *, name):
    T, C = yl.shape
    tr = _pick(T, 512)

    def body(yl_ref, u_ref, d_ref, da_ref, db_ref, dyl_ref, du_ref, dd_ref):
        dyl = (da_ref[...] + db_ref[...]) * _gelu_grad(yl_ref[...])
        dyl_ref[...] = dyl
        du_ref[...] = dyl * d_ref[...]

        @pl.when(pl.program_id(0) == 0)
        def _():
            dd_ref[...] = jnp.zeros_like(dd_ref)

        dd_ref[...] += jnp.sum(dyl * u_ref[...], axis=0, keepdims=True)

    return pl.pallas_call(body, name=name, grid=(T // tr,),
                          in_specs=[_row(tr, C), _row(tr, C), _full((1, C)), _row(tr, C), _row(tr, C)],
                          out_specs=[_row(tr, C), _row(tr, C), _full((1, C))],
                          out_shape=[SDS((T, C), F32), SDS((T, C), F32), SDS((1, C), F32)],
                          compiler_params=_cp("arbitrary"))(yl, u, d, dyg_a, dyg_b)


def add2(a, b, *, name):
    T, C = a.shape
    tr = _pick(T, 512)

    def body(a_ref, b_ref, o_ref):
        o_ref[...] = a_ref[...] + b_ref[...]

    return pl.pallas_call(body, name=name, grid=(T // tr,), in_specs=[_row(tr, C)] * 2, out_specs=_row(tr, C),
                          out_shape=SDS((T, C), F32), compiler_params=_cp("parallel"))(a, b)


def _tri(n, upper):
    r = lax.broadcasted_iota(jnp.int32, (n, n), 0)
    c = lax.broadcasted_iota(jnp.int32, (n, n), 1)
    return ((c >= r) if upper else (c <= r)).astype(F32)


def fox_gate_fwd(fl, bf, *, fl_col, name):
    T = fl.shape[0]
    tb = _pick(T, 256)

    def body(fl_ref, bf_ref, f_ref, c_ref):
        @pl.when(pl.program_id(0) == 0)
        def _():
            c_ref[...] = jnp.zeros_like(c_ref)

        lf = jax.nn.log_sigmoid(fl_ref[...] + bf_ref[...])
        f = jnp.dot(_tri(tb, False), lf, precision=lax.Precision.HIGHEST, preferred_element_type=F32) + c_ref[0:1, :]
        f_ref[...] = f * LOG2E
        c_ref[0:1, :] = f[tb - 1:tb, :]

    fl_spec = pl.BlockSpec((tb, LANES), lambda i: (i, fl_col))
    return pl.pallas_call(body, name=name, grid=(T // tb,), in_specs=[fl_spec, _full((1, LANES))],
                          out_specs=_row(tb, LANES), out_shape=SDS((T, LANES), F32),
                          scratch_shapes=[pltpu.VMEM((SUBLANES, LANES), F32)], compiler_params=_cp("arbitrary"))(fl, bf)


def fox_gate_bwd(fl, bf, df, *, fl_col, name):
    T = fl.shape[0]
    tb = _pick(T, 256)
    nt = T // tb

    def body(fl_ref, bf_ref, df_ref, dfl_ref, dbf_ref, c_ref):
        @pl.when(pl.program_id(0) == 0)
        def _():
            c_ref[...] = jnp.zeros_like(c_ref)
            dbf_ref[...] = jnp.zeros_like(dbf_ref)

        dlf = jnp.dot(_tri(tb, True), df_ref[...], precision=lax.Precision.HIGHEST, preferred_element_type=F32) + c_ref[0:1, :]
        c_ref[0:1, :] = dlf[0:1, :]
        dfl = dlf * jax.nn.sigmoid(-(fl_ref[...] + bf_ref[...]))
        dfl_ref[...] = dfl
        dbf_ref[...] += jnp.sum(dfl, axis=0, keepdims=True)

    rev = pl.BlockSpec((tb, LANES), lambda i: (nt - 1 - i, 0))
    fl_rev = pl.BlockSpec((tb, LANES), lambda i: (nt - 1 - i, fl_col))
    return pl.pallas_call(body, name=name, grid=(nt,), in_specs=[fl_rev, _full((1, LANES)), rev],
                          out_specs=[rev, _full((1, LANES))], out_shape=[SDS((T, LANES), F32), SDS((1, LANES), F32)],
                          scratch_shapes=[pltpu.VMEM((SUBLANES, LANES), F32)], compiler_params=_cp("arbitrary"))(fl, bf, df)


FOX_BLOCK = 512
FOX_PAIRS = FOX_HEADS // 2
_NT = (((1,), (1,)), ((), ()))


LOG2E = 1.4426950408889634
FOX_FWD_UNROLL = 4
FOX_BWD_UNROLL = 2


def _fox_block(T):
    return _pick(T, FOX_BLOCK)


def _own_lanes(lane, hh):
    return (lane < FOX_HEAD_DIM) if hh == 0 else (lane >= FOX_HEAD_DIM)


def _grouped_steps(step, lo, n, unroll, init):
    def trip(t, c):
        for u in range(unroll):
            c = step(lo + t * unroll + u, c)
        return c

    carry = lax.fori_loop(0, n // unroll, trip, init)
    for u in range(unroll - 1):
        carry = lax.cond(n % unroll > u, lambda c: step(lo + (n // unroll) * unroll + u, c), lambda c: c, carry)
    return carry


Q_TILE0, K_TILE0, V_TILE0, O_TILE0 = 4, 8, 12, 4
FL_TILE = 16
POOL_COL = 2


def fox_fwd(z, f_col, f_row, ycat, hosted, *, name):
    T = z.shape[0]
    blk = _fox_block(T)
    nb = T // blk
    scale = FOX_HEAD_DIM ** -0.5

    def body(q_ref, k_ref, v_ref, fc_ref, fr_ref, prev_ref, o_ref, l_ref):
        i = pl.program_id(1)
        row = lax.broadcasted_iota(jnp.int32, (blk, blk), 0)
        col = lax.broadcasted_iota(jnp.int32, (blk, blk), 1)
        lane = lax.broadcasted_iota(jnp.int32, (blk, LANES), 1)
        qt = q_ref[...] * (scale * LOG2E)
        outs = []
        for hh in range(2):
            qh = jnp.where(_own_lanes(lane, hh), qt, 0.0).astype(MXU_DTYPE)
            fi = fc_ref[0, :, hh:hh + 1]

            def step(j, carry, masked=False):
                m, l, acc = carry
                r0 = pl.multiple_of(j * blk, blk)
                kj = k_ref[pl.ds(r0, blk), :].astype(MXU_DTYPE)
                vj = v_ref[pl.ds(r0, blk), :].astype(MXU_DTYPE)
                s = lax.dot_general(qh, kj, _NT, preferred_element_type=F32) + (fi - fr_ref[0, j, hh:hh + 1, :])
                if masked:
                    s = jnp.where(col <= row, s, -jnp.inf)
                m_new = jnp.maximum(m, jnp.max(s, axis=-1, keepdims=True))
                p = jnp.exp2(s - m_new)
                alpha = jnp.exp2(m - m_new)
                l = alpha * l + jnp.sum(p, axis=-1, keepdims=True)
                acc = alpha * acc + jnp.dot(p.astype(MXU_DTYPE), vj, preferred_element_type=F32)
                return m_new, l, acc

            init = (jnp.full((blk, 1), -jnp.inf, F32), jnp.zeros((blk, 1), F32), jnp.zeros((blk, LANES), F32))
            m, l, acc = step(i, _grouped_steps(step, 0, i, FOX_FWD_UNROLL, init), True)
            outs.append(acc / l)
            l_ref[0, :, hh:hh + 1] = m + jnp.log2(l)
        o_ref[...] = jnp.where(_own_lanes(lane, 0), outs[0], outs[1])

    qspec = pl.BlockSpec((blk, LANES), lambda h, i: (i, Q_TILE0 + h))
    kspec = pl.BlockSpec((T, LANES), lambda h, i: (0, K_TILE0 + h))
    vspec = pl.BlockSpec((T, LANES), lambda h, i: (0, V_TILE0 + h))
    ospec = pl.BlockSpec((blk, LANES), lambda h, i: (i, O_TILE0 + h))
    cspec = pl.BlockSpec((1, blk, 2), lambda h, i: (h, i, 0))
    rspec = pl.BlockSpec((1, nb, 2, blk), lambda h, i: (h, 0, 0, 0))
    return call_hosting(body, hosted, name=name, grid=(FOX_PAIRS, nb),
                        in_specs=[qspec, kspec, vspec, cspec, rspec, ANY], out_specs=[ospec, cspec],
                        out_shape=[SDS(ycat.shape, F32), SDS((FOX_PAIRS, T, 2), F32)],
                        inputs=[z, z, z, f_col, f_row, ycat], aliases={5: 0})


def fox_bwd_q(z, ycat, dycat, f_col, f_row, lse_col, hosted, *, name):
    T = z.shape[0]
    blk = _fox_block(T)
    nb = T // blk
    scale = FOX_HEAD_DIM ** -0.5

    def body(q_ref, k_ref, v_ref, o_ref, do_ref, fc_ref, fr_ref, lc_ref, dq_ref, dd_ref, df_ref):
        i = pl.program_id(1)
        row = lax.broadcasted_iota(jnp.int32, (blk, blk), 0)
        col = lax.broadcasted_iota(jnp.int32, (blk, blk), 1)
        lane = lax.broadcasted_iota(jnp.int32, (blk, LANES), 1)
        qt = q_ref[...] * (scale * LOG2E)
        dot = do_ref[...]
        prod = dot * o_ref[...]
        outs = []
        for hh in range(2):
            own = _own_lanes(lane, hh)
            qh = jnp.where(own, qt, 0.0).astype(MXU_DTYPE)
            dob = jnp.where(own, dot, 0.0).astype(MXU_DTYPE)
            dd = jnp.sum(jnp.where(own, prod, 0.0), axis=-1, keepdims=True)
            dd_ref[0, :, hh:hh + 1] = dd
            fi = fc_ref[0, :, hh:hh + 1]
            lse = lc_ref[0, :, hh:hh + 1]

            def step(j, carry, masked=False):
                dq, df = carry
                r0 = pl.multiple_of(j * blk, blk)
                kj = k_ref[pl.ds(r0, blk), :].astype(MXU_DTYPE)
                vj = v_ref[pl.ds(r0, blk), :].astype(MXU_DTYPE)
                s = lax.dot_general(qh, kj, _NT, preferred_element_type=F32) + (fi - fr_ref[0, j, hh:hh + 1, :])
                p = jnp.exp2(s - lse)
                if masked:
                    p = jnp.where(col <= row, p, 0.0)
                dp = lax.dot_general(dob, vj, _NT, preferred_element_type=F32)
                ds = p * (dp - dd)
                return (dq + jnp.dot(ds.astype(MXU_DTYPE), kj, preferred_element_type=F32),
                        df + jnp.sum(ds, axis=-1, keepdims=True))

            init = (jnp.zeros((blk, LANES), F32), jnp.zeros((blk, 1), F32))
            dq, df = step(i, _grouped_steps(step, 0, i, FOX_BWD_UNROLL, init), True)
            outs.append(dq * scale)
            df_ref[0, :, hh:hh + 1] = df
        dq_ref[...] = jnp.where(_own_lanes(lane, 0), outs[0], outs[1])

    qspec = pl.BlockSpec((blk, LANES), lambda h, i: (i, Q_TILE0 + h))
    kspec = pl.BlockSpec((T, LANES), lambda h, i: (0, K_TILE0 + h))
    vspec = pl.BlockSpec((T, LANES), lambda h, i: (0, V_TILE0 + h))
    ospec = pl.BlockSpec((blk, LANES), lambda h, i: (i, O_TILE0 + h))
    dqspec = pl.BlockSpec((blk, LANES), lambda h, i: (i, h))
    cspec = pl.BlockSpec((1, blk, 2), lambda h, i: (h, i, 0))
    rspec = pl.BlockSpec((1, nb, 2, blk), lambda h, i: (h, 0, 0, 0))
    stat = SDS((FOX_PAIRS, T, 2), F32)
    return call_hosting(body, hosted, name=name, grid=(FOX_PAIRS, nb),
                        in_specs=[qspec, kspec, vspec, ospec, ospec, cspec, rspec, cspec],
                        out_specs=[dqspec, cspec, cspec], out_shape=[SDS((T, FOX_WIDTH), F32), stat, stat],
                        inputs=[z, z, z, ycat, dycat, f_col, f_row, lse_col], aliases={})


def fox_bwd_kv(z, dycat, f_col, f_row, lse_row, dd_row, dfq_col, hosted, *, name):
    T = z.shape[0]
    blk = _fox_block(T)
    nb = T // blk
    scale = FOX_HEAD_DIM ** -0.5

    def body(q_ref, k_ref, v_ref, do_ref, fc_ref, fr_ref, lr_ref, dr_ref, dfq_ref, dk_ref, dv_ref, df_ref):
        j = pl.program_id(1)
        row = lax.broadcasted_iota(jnp.int32, (blk, blk), 0)
        col = lax.broadcasted_iota(jnp.int32, (blk, blk), 1)
        lane = lax.broadcasted_iota(jnp.int32, (blk, LANES), 1)
        kt = k_ref[...]
        vt = v_ref[...]
        dks, dvs = [], []
        for hh in range(2):
            own = _own_lanes(lane, hh)
            kh = jnp.where(own, kt, 0.0).astype(MXU_DTYPE)
            vh = jnp.where(own, vt, 0.0).astype(MXU_DTYPE)
            fj = fc_ref[0, :, hh:hh + 1]

            def step(i, carry, masked=False):
                dk, dv, df = carry
                r0 = pl.multiple_of(i * blk, blk)
                qi = (q_ref[pl.ds(r0, blk), :] * (scale * LOG2E)).astype(MXU_DTYPE)
                doi = do_ref[pl.ds(r0, blk), :].astype(MXU_DTYPE)
                st = lax.dot_general(kh, qi, _NT, preferred_element_type=F32) + (fr_ref[0, i, hh:hh + 1, :] - fj)
                pt = jnp.exp2(st - lr_ref[0, i, hh:hh + 1, :])
                if masked:
                    pt = jnp.where(col >= row, pt, 0.0)
                dv = dv + jnp.dot(pt.astype(MXU_DTYPE), doi, preferred_element_type=F32)
                dpt = lax.dot_general(vh, doi, _NT, preferred_element_type=F32)
                dst = pt * (dpt - dr_ref[0, i, hh:hh + 1, :])
                dk = dk + jnp.dot(dst.astype(MXU_DTYPE), qi, preferred_element_type=F32)
                df = df - jnp.sum(dst, axis=-1, keepdims=True)
                return dk, dv, df

            init = (jnp.zeros((blk, LANES), F32), jnp.zeros((blk, LANES), F32), dfq_ref[0, :, hh:hh + 1])
            dk, dv, df = _grouped_steps(step, j + 1, nb - 1 - j, FOX_BWD_UNROLL, step(j, init, True))
            dks.append(dk * (1.0 / LOG2E))
            dvs.append(dv)
            df_ref[0, :, hh:hh + 1] = df
        dk_ref[...] = jnp.where(_own_lanes(lane, 0), dks[0], dks[1])
        dv_ref[...] = jnp.where(_own_lanes(lane, 0), dvs[0], dvs[1])

    bspec = pl.BlockSpec((blk, LANES), lambda h, j: (j, h))
    qspec = pl.BlockSpec((T, LANES), lambda h, j: (0, Q_TILE0 + h))
    kspec = pl.BlockSpec((blk, LANES), lambda h, j: (j, K_TILE0 + h))
    vspec = pl.BlockSpec((blk, LANES), lambda h, j: (j, V_TILE0 + h))
    dospec = pl.BlockSpec((T, LANES), lambda h, j: (0, O_TILE0 + h))
    cspec = pl.BlockSpec((1, blk, 2), lambda h, j: (h, j, 0))
    rspec = pl.BlockSpec((1, nb, 2, blk), lambda h, j: (h, 0, 0, 0))
    return call_hosting(body, hosted, name=name, grid=(FOX_PAIRS, nb),
                        in_specs=[qspec, kspec, vspec, dospec, cspec, rspec, rspec, rspec, cspec],
                        out_specs=[bspec, bspec, cspec],
                        out_shape=[SDS((T, FOX_WIDTH), F32), SDS((T, FOX_WIDTH), F32), SDS((FOX_PAIRS, T, 2), F32)],
                        inputs=[z, z, z, dycat, f_col, f_row, lse_row, dd_row, dfq_col], aliases={})


def _pairs_col(a, T):
    return jnp.transpose(a[:, :FOX_HEADS].reshape(T, FOX_PAIRS, 2), (1, 0, 2))


def _col_to_row(a, T):
    blk = _fox_block(T)
    return jnp.transpose(a.reshape(FOX_PAIRS, T // blk, blk, 2), (0, 1, 3, 2))


def _pairs_to_lanes(a, T):
    flat = jnp.transpose(a, (1, 0, 2)).reshape(T, FOX_HEADS)
    return jnp.pad(flat, ((0, 0), (0, LANES - FOX_HEADS)))


def _pool_counts(t0, n, w):
    t = (t0 + lax.broadcasted_iota(jnp.int32, (n, 1), 0)).astype(F32)
    return jnp.minimum(t + 1.0, float(w))


def pool_window(x, *, adjoint, name, in_col=0, into=None, out_col=0):
    T, C = x.shape[0], len(POOL_WINDOWS) * POOL_GROUP_DIM
    tr = _pick(T, 512)
    nt = T // tr
    hb = tr // POOL_HALO
    n = tr + POOL_HALO

    def body(x_ref, h_ref, *rest):
        o_ref = rest[-1]
        i = pl.program_id(0)
        cur = x_ref[...]
        if adjoint:
            halo = h_ref[...] * jnp.where(i < nt - 1, 1.0, 0.0)
            ext = jnp.concatenate([cur, halo], axis=0)
            t0 = i * tr
        else:
            halo = h_ref[...] * jnp.where(i > 0, 1.0, 0.0)
            ext = jnp.concatenate([halo, cur], axis=0)
            t0 = i * tr - POOL_HALO
        sums = {}
        for g, w in enumerate(POOL_WINDOWS):
            ls = slice(g * POOL_GROUP_DIM, (g + 1) * POOL_GROUP_DIM)
            s = ext[:, ls]
            if adjoint:
                s = s / _pool_counts(t0, n, w)
            d = 1
            while d < w:
                s = s + pltpu.roll(s, (n - d) if adjoint else d, 0)
                d *= 2
            if adjoint:
                o_ref[:, ls] = s[0:tr, :] - cur[:, ls]
            else:
                o_ref[:, ls] = s[POOL_HALO:n, :] / _pool_counts(i * tr, tr, w) - cur[:, ls]

    if adjoint:
        halo_spec = pl.BlockSpec((POOL_HALO, C), lambda i: (jnp.minimum((i + 1) * hb, T // POOL_HALO - 1), in_col))
    else:
        halo_spec = pl.BlockSpec((POOL_HALO, C), lambda i: (jnp.maximum(i * hb - 1, 0), in_col))
    x_spec = pl.BlockSpec((tr, C), lambda i: (i, in_col))
    if into is None:
        return pl.pallas_call(body, name=name, grid=(nt,), in_specs=[x_spec, halo_spec], out_specs=_row(tr, C),
                              out_shape=SDS((T, C), F32), compiler_params=_cp("parallel"))(x, x)
    return pl.pallas_call(body, name=name, grid=(nt,), in_specs=[x_spec, halo_spec, ANY],
                          out_specs=pl.BlockSpec((tr, C), lambda i: (i, out_col)), out_shape=SDS(into.shape, F32),
                          input_output_aliases={2: 0}, compiler_params=_cp("parallel"))(x, x, into)


def colscale_fwd(a, s, *, out_cols, name):
    T, C = a.shape
    tr = _pick(T, 512)

    def body(a_ref, s_ref, o_ref):
        o_ref[...] = a_ref[...] * s_ref[...]

    return pl.pallas_call(body, name=name, grid=(T // tr,), in_specs=[_row(tr, C), _full((1, C))], out_specs=_row(tr, C),
                          out_shape=SDS((T, out_cols), F32), compiler_params=_cp("parallel"))(a, s)


def colscale_bwd(a, s, dy, *, name):
    T, C = a.shape
    tr = _pick(T, 512)

    def body(a_ref, s_ref, dy_ref, da_ref, ds_ref):
        dyv = dy_ref[...]
        da_ref[...] = dyv * s_ref[...]

        @pl.when(pl.program_id(0) == 0)
        def _():
            ds_ref[...] = jnp.zeros_like(ds_ref)

        ds_ref[...] += jnp.sum(dyv * a_ref[...], axis=0, keepdims=True)

    return pl.pallas_call(body, name=name, grid=(T // tr,), in_specs=[_row(tr, C), _full((1, C)), _row(tr, C)],
                          out_specs=[_row(tr, C), _full((1, C))], out_shape=[SDS((T, C), F32), SDS((1, C), F32)],
                          compiler_params=_cp("arbitrary"))(a, s, dy)


SGU_ROWS = 512


def _sgu_norm(v, ln_g, ln_b):
    vg = jax.nn.gelu(v)
    xc = vg - jnp.mean(vg, axis=-1, keepdims=True)
    r = lax.rsqrt(jnp.mean(xc * xc, axis=-1, keepdims=True) + EPS)
    xh = xc * r
    return xh * ln_g + ln_b, xh, r


def _rowc(tr, c, cb):
    return pl.BlockSpec((tr, c), lambda i: (i, cb))


def sgu_fwd(z, ln_g, ln_b, ws, bst, ycat, *, name):
    T, C = z.shape[0], SGU_GROUPS * SGU_GROUP_DIM
    tr = _pick(T, SGU_ROWS)

    def body(u_ref, v_ref, g_ref, b_ref, ws_ref, bst_ref, prev_ref, o_ref):
        vn, _, _ = _sgu_norm(v_ref[...], g_ref[...], b_ref[...])
        vn = vn.astype(MXU_DTYPE)
        ug = jax.nn.gelu(u_ref[...])
        for g in range(SGU_GROUPS):
            w = ws_ref[g].astype(MXU_DTYPE)
            bias = bst_ref[:, g:g + 1]
            for c in range(tr // CHUNK):
                rs = slice(c * CHUNK, (c + 1) * CHUNK)
                ls = slice(g * SGU_GROUP_DIM, (g + 1) * SGU_GROUP_DIM)
                mixed = jnp.dot(w, vn[rs, ls], preferred_element_type=F32) + bias
                o_ref[rs, ls] = ug[rs, ls] * mixed

    return pl.pallas_call(body, name=name, grid=(T // tr,),
                          in_specs=[_rowc(tr, C, 0), _rowc(tr, C, 1), _full((1, C)), _full((1, C)),
                                    _full((SGU_GROUPS, CHUNK, CHUNK)), _full((CHUNK, SGU_GROUPS)), ANY],
                          out_specs=_rowc(tr, C, 1), out_shape=SDS(ycat.shape, F32), input_output_aliases={6: 0},
                          compiler_params=_cp("parallel"))(z, z, ln_g, ln_b, ws, bst, ycat)


def sgu_bwd(z, ln_g, ln_b, ws, wst, bst, dycat, *, out_cols, name):
    T, C = z.shape[0], SGU_GROUPS * SGU_GROUP_DIM
    tr = _pick(T, SGU_ROWS)

    def body(u_ref, v_ref, g_ref, b_ref, ws_ref, wst_ref, bst_ref, dy_ref,
             duv_ref, dg_ref, db_ref, dws_ref, dbst_ref, dvn_ref):
        du_ref = duv_ref.at[:, 0:C]
        dv_ref = duv_ref.at[:, C:2 * C]
        @pl.when(pl.program_id(0) == 0)
        def _():
            dg_ref[...] = jnp.zeros_like(dg_ref)
            db_ref[...] = jnp.zeros_like(db_ref)
            dws_ref[...] = jnp.zeros_like(dws_ref)
            dbst_ref[...] = jnp.zeros_like(dbst_ref)

        uv = u_ref[...]
        vv = v_ref[...]
        vn, xh, r = _sgu_norm(vv, g_ref[...], b_ref[...])
        vn = vn.astype(MXU_DTYPE)
        ug = jax.nn.gelu(uv)
        dyv = dy_ref[...]
        for g in range(SGU_GROUPS):
            w = ws_ref[g].astype(MXU_DTYPE)
            wt = wst_ref[g].astype(MXU_DTYPE)
            bias = bst_ref[:, g:g + 1]
            dw = jnp.zeros((CHUNK, CHUNK), F32)
            dbias = jnp.zeros((CHUNK, 1), F32)
            for c in range(tr // CHUNK):
                rs = slice(c * CHUNK, (c + 1) * CHUNK)
                ls = slice(g * SGU_GROUP_DIM, (g + 1) * SGU_GROUP_DIM)
                vblk = vn[rs, ls]
                mixed = jnp.dot(w, vblk, preferred_element_type=F32) + bias
                dyb = dyv[rs, ls]
                du_ref[rs, ls] = dyb * mixed * _gelu_grad(uv[rs, ls])
                dmixed = dyb * ug[rs, ls]
                dbias = dbias + jnp.sum(dmixed, axis=-1, keepdims=True)
                dmb = dmixed.astype(MXU_DTYPE)
                dw = dw + lax.dot_general(dmb, vblk, _NT, preferred_element_type=F32)
                dvn_ref[rs, ls] = jnp.dot(wt, dmb, preferred_element_type=F32)
            dws_ref[g] += dw
            dbst_ref[:, g:g + 1] += dbias
        dvn = dvn_ref[...]
        dg_ref[...] += jnp.sum(dvn * xh, axis=0, keepdims=True)
        db_ref[...] += jnp.sum(dvn, axis=0, keepdims=True)
        dxh = dvn * g_ref[...]
        dvg = r * (dxh - jnp.mean(dxh, axis=-1, keepdims=True) - xh * jnp.mean(dxh * xh, axis=-1, keepdims=True))
        dv_ref[...] = dvg * _gelu_grad(vv)

    wspec = _full((SGU_GROUPS, CHUNK, CHUNK))
    return pl.pallas_call(body, name=name, grid=(T // tr,),
                          in_specs=[_rowc(tr, C, 0), _rowc(tr, C, 1), _full((1, C)), _full((1, C)), wspec, wspec,
                                    _full((CHUNK, SGU_GROUPS)), _rowc(tr, C, 1)],
                          out_specs=[_rowc(tr, 2 * C, 0), _full((1, C)), _full((1, C)), wspec,
                                     _full((CHUNK, SGU_GROUPS))],
                          out_shape=[SDS((T, out_cols), F32), SDS((1, C), F32), SDS((1, C), F32),
                                     SDS((SGU_GROUPS, CHUNK, CHUNK), F32), SDS((CHUNK, SGU_GROUPS), F32)],
                          scratch_shapes=[pltpu.VMEM((tr, C), F32)],
                          compiler_params=_cp("arbitrary"))(z, z, ln_g, ln_b, ws, wst, bst, dycat)


def adamw(w, g, m, v, *, name):
    R, C = w.shape
    tr = _pick(R, 512)
    c1 = 1.0 - ADAM_B1 ** ADAM_STEP
    c2 = 1.0 - ADAM_B2 ** ADAM_STEP

    def body(w_ref, g_ref, m_ref, v_ref, d_ref, nm_ref, nv_ref):
        gv = g_ref[...]
        nm = ADAM_B1 * m_ref[...] + (1.0 - ADAM_B1) * gv
        nv = ADAM_B2 * v_ref[...] + (1.0 - ADAM_B2) * (gv * gv)
        nm_ref[...] = nm
        nv_ref[...] = nv
        d_ref[...] = -ADAM_LR * ((nm / c1) / (jnp.sqrt(nv / c2) + ADAM_EPS) + ADAM_WD * w_ref[...])

    spec = _row(tr, C)
    return pl.pallas_call(body, name=name, grid=(R // tr,), in_specs=[spec] * 4, out_specs=[spec] * 3,
                          out_shape=[SDS((R, C), F32)] * 3, compiler_params=_cp("parallel"))(w, g, m, v)


ANY = pl.BlockSpec(memory_space=pl.ANY)


def _coords():
    return lax.axis_index("x"), lax.axis_index("y"), lax.axis_index("c")


def _other_chips(x, y):
    return [(1 - x, y), (x, 1 - y), (1 - x, 1 - y)]


def _remote(src, dst, send_sems, recv_sems, k, dev):
    return pltpu.make_async_remote_copy(src_ref=src, dst_ref=dst, send_sem=send_sems.at[k], recv_sem=recv_sems.at[k],
                                        device_id=dev, device_id_type=MESH)


LOCAL_CHUNKS = 8


def allgather_chip_shards(shards, small, *, name):
    na = len(shards)

    def body(*refs):
        s_refs, sm_ref = refs[:na], refs[na]
        o_refs, smo_ref = refs[na + 1:2 * na + 1], refs[2 * na + 1]
        send_sems, recv_sems, local_sems = refs[2 * na + 2:]
        x, y, c = _coords()
        j = 2 * x + y
        sibling = (x, y, 1 - c)
        chips = _other_chips(x, y)
        for a in range(na):
            chunk = shards[a].shape[0] // LOCAL_CHUNKS
            for q in range(LOCAL_CHUNKS):
                rows = pl.ds(q * chunk, chunk)
                pltpu.make_async_copy(s_refs[a].at[rows], o_refs[a].at[j, rows], local_sems.at[a]).start()
        pltpu.make_async_copy(sm_ref, smo_ref.at[j], local_sems.at[na]).start()
        sends = []
        for a in range(na):
            half = shards[a].shape[0] // 2
            mine = pl.ds(c * half, half)
            for k, (px, py) in enumerate(chips):
                sends.append(_remote(s_refs[a].at[mine], o_refs[a].at[j, mine], send_sems, recv_sems, 6 * a + k, (px, py, c)))
        for k, (px, py) in enumerate(chips):
            sends.append(_remote(sm_ref, smo_ref.at[j], send_sems, recv_sems, 6 * na + k, (px, py, c)))
        for cp in sends:
            cp.start()
        for a in range(na):
            half = shards[a].shape[0] // 2
            mine = pl.ds(c * half, half)
            for k, (px, py) in enumerate(chips):
                rows = o_refs[a].at[2 * px + py, mine]
                _remote(rows, rows, send_sems, recv_sems, 6 * a + k, (px, py, c)).wait_recv()
                fw = _remote(rows, rows, send_sems, recv_sems, 6 * a + 3 + k, sibling)
                fw.start()
                sends.append(fw)
        for a in range(na):
            half = shards[a].shape[0] // 2
            theirs = pl.ds((1 - c) * half, half)
            for k, (px, py) in enumerate(chips):
                rows = o_refs[a].at[2 * px + py, theirs]
                _remote(rows, rows, send_sems, recv_sems, 6 * a + 3 + k, sibling).wait_recv()
        for k, (px, py) in enumerate(chips):
            slot = smo_ref.at[2 * px + py]
            _remote(slot, slot, send_sems, recv_sems, 6 * na + k, (px, py, c)).wait_recv()
        for cp in sends:
            cp.wait_send()
        for a in range(na):
            pltpu.make_async_copy(s_refs[a], o_refs[a].at[j], local_sems.at[a]).wait()
        pltpu.make_async_copy(sm_ref, smo_ref.at[j], local_sems.at[na]).wait()

    nsem = 6 * na + 3
    outs = pl.pallas_call(
        body, name=name, in_specs=[ANY] * (na + 1), out_specs=[ANY] * (na + 1),
        out_shape=[SDS((N_CHIPS,) + s.shape, s.dtype) for s in shards] + [SDS((N_CHIPS,) + small.shape, small.dtype)],
        scratch_shapes=[pltpu.SemaphoreType.DMA((nsem,)), pltpu.SemaphoreType.DMA((nsem,)),
                        pltpu.SemaphoreType.DMA((na + 1,))])(*shards, small)
    return outs[:na], outs[na]


class Exchange:
    def __init__(self, ins, out_shapes, scratch, start, wait):
        self.ins, self.out_shapes, self.scratch, self.start, self.wait = list(ins), list(out_shapes), list(scratch), start, wait


def run_exchange(ex, *, name):
    ni, no = len(ex.ins), len(ex.out_shapes)

    def body(*refs):
        parts = refs[:ni], refs[ni:ni + no], refs[ni + no:]
        ex.start(*parts)
        ex.wait(*parts)

    return pl.pallas_call(body, name=name, in_specs=[ANY] * ni, out_specs=[ANY] * no, out_shape=ex.out_shapes,
                          scratch_shapes=ex.scratch)(*ex.ins)


def call_hosting(body, ex, *, name, grid, in_specs, out_specs, out_shape, inputs, aliases):
    n_in, n_out, ni, no = len(inputs), len(out_shape), len(ex.ins), len(ex.out_shapes)

    def wrapped(*refs):
        own = refs[:n_in] + refs[n_in + ni:n_in + ni + n_out]
        parts = refs[n_in:n_in + ni], refs[n_in + ni + n_out:n_in + ni + n_out + no], refs[n_in + ni + n_out + no:]
        ids = [pl.program_id(d) for d in range(len(grid))]
        first = functools.reduce(jnp.logical_and, [i == 0 for i in ids])
        last = functools.reduce(jnp.logical_and, [i == g - 1 for i, g in zip(ids, grid)])

        @pl.when(first)
        def _():
            ex.start(*parts)

        body(*own)

        @pl.when(last)
        def _():
            ex.wait(*parts)

    outs = pl.pallas_call(
        wrapped, name=name, grid=grid, in_specs=list(in_specs) + [ANY] * ni, out_specs=list(out_specs) + [ANY] * no,
        out_shape=list(out_shape) + ex.out_shapes, input_output_aliases=aliases, scratch_shapes=ex.scratch,
        compiler_params=_cp(*["arbitrary"] * len(grid)))(*inputs, *ex.ins)
    return outs[:n_out], outs[n_out:]


def allgather_ici_exchange(shards):
    na = len(shards)

    def copies(s_refs, o_refs, sems):
        send_sems, recv_sems, _ = sems
        x, y, c = _coords()
        j = 2 * x + y
        out = []
        for a in range(na):
            half = shards[a].shape[0] // 2
            mine = pl.ds(c * half, half)
            for k, (px, py) in enumerate(_other_chips(x, y)):
                send = _remote(s_refs[a].at[mine], o_refs[a].at[j, mine], send_sems, recv_sems, 3 * a + k, (px, py, c))
                rows = o_refs[a].at[2 * px + py, mine]
                out.append((send, _remote(rows, rows, send_sems, recv_sems, 3 * a + k, (px, py, c))))
        return out

    def start(s_refs, o_refs, sems):
        x, y, c = _coords()
        j = 2 * x + y
        for a in range(na):
            chunk = shards[a].shape[0] // LOCAL_CHUNKS
            for q in range(LOCAL_CHUNKS):
                rows = pl.ds(q * chunk, chunk)
                pltpu.make_async_copy(s_refs[a].at[rows], o_refs[a].at[j, rows], sems[2].at[a]).start()
        for send, _ in copies(s_refs, o_refs, sems):
            send.start()

    def wait(s_refs, o_refs, sems):
        x, y, c = _coords()
        j = 2 * x + y
        for send, arrival in copies(s_refs, o_refs, sems):
            arrival.wait_recv()
            send.wait_send()
        for a in range(na):
            pltpu.make_async_copy(s_refs[a], o_refs[a].at[j], sems[2].at[a]).wait()

    return Exchange(shards, [SDS((N_CHIPS,) + s.shape, s.dtype) for s in shards],
                    [pltpu.SemaphoreType.DMA((3 * na,)), pltpu.SemaphoreType.DMA((3 * na,)), pltpu.SemaphoreType.DMA((na,))],
                    start, wait)


def allgather_forward(gathered, *, name):
    na = len(gathered)

    def body(*refs):
        o_refs = refs[na:2 * na]
        send_sems, recv_sems = refs[2 * na:]
        x, y, c = _coords()
        sibling = (x, y, 1 - c)
        cps = []
        for a in range(na):
            half = gathered[a].shape[1] // 2
            for k, (px, py) in enumerate(_other_chips(x, y)):
                mine = o_refs[a].at[2 * px + py, pl.ds(c * half, half)]
                theirs = o_refs[a].at[2 * px + py, pl.ds((1 - c) * half, half)]
                cps.append((_remote(mine, mine, send_sems, recv_sems, 3 * a + k, sibling),
                            _remote(theirs, theirs, send_sems, recv_sems, 3 * a + k, sibling)))
        for send, _ in cps:
            send.start()
        for send, arrival in cps:
            send.wait_send()
            arrival.wait_recv()

    return pl.pallas_call(body, name=name, in_specs=[ANY] * na, out_specs=[ANY] * na,
                          out_shape=[SDS(g.shape, g.dtype) for g in gathered],
                          input_output_aliases={a: a for a in range(na)},
                          scratch_shapes=[pltpu.SemaphoreType.DMA((3 * na,)), pltpu.SemaphoreType.DMA((3 * na,))])(*gathered)


def swap_halves_exchange(gs):
    na = len(gs)

    def copies(g_refs, o_refs, sems):
        x, y, c = _coords()
        out = []
        for a in range(na):
            half = gs[a].shape[1] // 2
            out.append(_remote(g_refs[a].at[:, pl.ds((1 - c) * half, half), :], o_refs[a], sems[0], sems[1], a,
                               (x, y, 1 - c)))
        return out

    def start(g_refs, o_refs, sems):
        for cp in copies(g_refs, o_refs, sems):
            cp.start()

    def wait(g_refs, o_refs, sems):
        for cp in copies(g_refs, o_refs, sems):
            cp.wait()

    return Exchange(gs, [SDS((g.shape[0], g.shape[1] // 2, g.shape[2]), g.dtype) for g in gs],
                    [pltpu.SemaphoreType.DMA((na,)), pltpu.SemaphoreType.DMA((na,))], start, wait)


def chip_partials_exchange(pbs):
    na = len(pbs)

    def copies(p_refs, o_refs, sems):
        x, y, c = _coords()
        out = []
        for a in range(na):
            for k, (px, py) in enumerate(_other_chips(x, y)):
                out.append(_remote(p_refs[a].at[2 * px + py], o_refs[a].at[k], sems[0], sems[1], 3 * a + k, (px, py, c)))
        return out

    def start(p_refs, o_refs, sems):
        for cp in copies(p_refs, o_refs, sems):
            cp.start()

    def wait(p_refs, o_refs, sems):
        for cp in copies(p_refs, o_refs, sems):
            cp.wait()

    return Exchange(pbs, [SDS((3,) + p.shape[1:], p.dtype) for p in pbs],
                    [pltpu.SemaphoreType.DMA((3 * na,)), pltpu.SemaphoreType.DMA((3 * na,))], start, wait)


def add_sibling_half(g, land, c_idx, *, name):
    n, R, C = g.shape
    half = R // 2
    tr = _pick(half, 256)
    nt = half // tr

    def body(c_ref, g_ref, l_ref, of_ref, ob_ref):
        s = g_ref[...] + l_ref[...].astype(F32)
        of_ref[...] = s
        ob_ref[...] = s.astype(ob_ref.dtype)

    blk = pl.BlockSpec((1, tr, C), lambda s, i, c_ref: (s, i, 0))
    gblk = pl.BlockSpec((1, tr, C), lambda s, i, c_ref: (s, c_ref[0] * nt + i, 0))
    return pl.pallas_call(
        body, name=name,
        grid_spec=pltpu.PrefetchScalarGridSpec(num_scalar_prefetch=1, grid=(n, nt), in_specs=[gblk, blk],
                                               out_specs=[blk, blk]),
        out_shape=[SDS((n, half, C), F32), SDS((n, half, C), WIRE_DTYPE)],
        compiler_params=_cp("parallel", "parallel"))(c_idx, g, land)


def add_chip_partials(pf, rb, jc_idx, *, name):
    n, H, C = pf.shape
    tr = _pick(H, 256)

    def body(jc_ref, p_ref, r_ref, o_ref):
        s = p_ref[0]
        for k in range(3):
            s = s + r_ref[k].astype(F32)
        o_ref[...] = s

    pblk = pl.BlockSpec((1, tr, C), lambda i, jc_ref: (jc_ref[0], i, 0))
    rblk = pl.BlockSpec((3, tr, C), lambda i, jc_ref: (0, i, 0))
    oblk = pl.BlockSpec((None, tr, C), lambda i, jc_ref: (jc_ref[1], i, 0))
    return pl.pallas_call(
        body, name=name,
        grid_spec=pltpu.PrefetchScalarGridSpec(num_scalar_prefetch=1, grid=(H // tr,), in_specs=[pblk, rblk],
                                               out_specs=oblk),
        out_shape=SDS((2, H, C), F32), compiler_params=_cp("parallel"))(jc_idx, pf, rb)


def join_sibling_halves(bufs, *, name):
    na = len(bufs)

    def body(*refs):
        o_refs = refs[na:2 * na]
        send_sems, recv_sems = refs[2 * na:]
        x, y, c = _coords()
        cps = [_remote(o_refs[a].at[c], o_refs[a].at[c], send_sems, recv_sems, a, (x, y, 1 - c)) for a in range(na)]
        for cp in cps:
            cp.start()
        for a in range(na):
            cps[a].wait_send()
            _remote(o_refs[a].at[1 - c], o_refs[a].at[1 - c], send_sems, recv_sems, a, (x, y, 1 - c)).wait_recv()

    return pl.pallas_call(body, name=name, in_specs=[ANY] * na, out_specs=[ANY] * na,
                          out_shape=[SDS(b.shape, b.dtype) for b in bufs],
                          input_output_aliases={a: a for a in range(na)},
                          scratch_shapes=[pltpu.SemaphoreType.DMA((na,)), pltpu.SemaphoreType.DMA((na,))])(*bufs)


def exchange_pieces(v, *, scatter, name):
    P, C = v.shape[-2:]

    def body(v_ref, o_ref, send_sems, recv_sems, local_sem):
        x, y, c = _coords()
        me = 4 * x + 2 * y + c
        local = pltpu.make_async_copy(v_ref.at[me] if scatter else v_ref, o_ref.at[me], local_sem)
        local.start()
        cps = []
        for m in range(1, N_DEV):
            px = (1 - x) if m & 4 else x
            py = (1 - y) if m & 2 else y
            pc = (1 - c) if m & 1 else c
            src = v_ref.at[4 * px + 2 * py + pc] if scatter else v_ref
            cps.append(_remote(src, o_ref.at[me], send_sems, recv_sems, m - 1, (px, py, pc)))
        for cp in cps:
            cp.start()
        for cp in cps:
            cp.wait_send()
        for m in range(1, N_DEV):
            px = (1 - x) if m & 4 else x
            py = (1 - y) if m & 2 else y
            pc = (1 - c) if m & 1 else c
            slot = o_ref.at[4 * px + 2 * py + pc]
            _remote(slot, slot, send_sems, recv_sems, m - 1, (px, py, pc)).wait_recv()
        local.wait()

    return pl.pallas_call(body, name=name, in_specs=[ANY], out_specs=ANY, out_shape=SDS((N_DEV, P, C), v.dtype),
                          scratch_shapes=[pltpu.SemaphoreType.DMA((N_DEV - 1,)), pltpu.SemaphoreType.DMA((N_DEV - 1,)),
                                          pltpu.SemaphoreType.DMA(())])(v)


def sum_pieces(land, *, name):
    n, P, C = land.shape

    def body(l_ref, o_ref):
        s = l_ref[0]
        for d in range(1, n):
            s = s + l_ref[d]
        o_ref[...] = s

    return pl.pallas_call(body, name=name, out_shape=SDS((P, C), F32))(land)


BIG_SEGS = (
    ("w_in_even", (1024, 514), 1),
    ("s5_w_glu", (128, 512), 0),
    ("w_out_even", (256, 1024), 0),
    ("w_in_odd", (1024, 384), 1),
    ("w_out_odd", (256, 1024), 0),
    ("mlp_w1", (2, 1024, 1024), 2),
    ("mlp_w2", (2, 1024, 1024), 1),
)
BIG_NAMES = tuple(n for n, _, _ in BIG_SEGS)
EARLY_NAMES = ("w_in_even", "s5_w_glu")
LATE_NAMES = ("w_out_even", "w_in_odd", "w_out_odd", "mlp_w1", "mlp_w2")
REDUCED_EARLY = ("s5_w_glu", "w_out_even", "w_in_odd", "w_out_odd", "mlp_w1", "mlp_w2")
SHARDED_SMALL = ("pool_scale", "sgu_ln_g", "sgu_ln_b")
SMALL_SEGS = (
    ("mix_pre_g", (2, 1024)), ("mix_post_g", (2, 1024)), ("mlp_pre_g", (2, 1024)), ("mlp_post_g", (2, 1024)),
    ("s5_lam_re", (1, 32, 64)), ("s5_lam_im", (1, 32, 64)), ("s5_log_dt", (1, 32)),
    ("s5_b_re", (1, 32, 64, 16)), ("s5_b_im", (1, 32, 64, 16)), ("s5_c_re", (1, 32, 16, 64)), ("s5_c_im", (1, 32, 16, 64)),
    ("s5_d", (1, 512)), ("fox_b_f", (1, 8)), ("pool_w", (1, 4, 128, 128)), ("sgu_w_s", (1, 4, 128, 128)),
    ("sgu_b_s", (1, 4, 128)),
)
REDUCED_SEGS = SMALL_SEGS + tuple((n, (1, 512)) for n in SHARDED_SMALL)


def _cols_from_chips(g):
    n, R, C = g.shape
    return jnp.transpose(g, (1, 0, 2)).reshape(R, n * C)


def _chips_from_cols(m):
    R, C4 = m.shape
    return jnp.transpose(m.reshape(R, N_CHIPS, C4 // N_CHIPS), (1, 0, 2))


MLP_SHARD = 1024


def _w1_cols(l):
    def spec(tm, tn, tk):
        per = MLP_SHARD // tn
        return pl.BlockSpec((None, tk, tn), lambda i, j, k: (j // per, l * (MLP_SHARD // tk) + k, j % per))
    return spec


def _w1_rows_t(l):
    def spec(tm, tn, tk):
        if tk == N_CHIPS * MLP_SHARD:
            return pl.BlockSpec((N_CHIPS, tn, MLP_SHARD), lambda i, j, k: (0, l * (MLP_SHARD // tn) + j, 0))
        per = MLP_SHARD // tk
        return pl.BlockSpec((None, tn, tk), lambda i, j, k: (k // per, l * (MLP_SHARD // tn) + j, k % per))
    return spec


def _w2_rows(l):
    def spec(tm, tn, tk):
        if tk == N_CHIPS * MLP_SHARD:
            return pl.BlockSpec((N_CHIPS, MLP_SHARD, tn), lambda i, j, k: (0, l, j))
        per = MLP_SHARD // tk
        return pl.BlockSpec((None, tk, tn), lambda i, j, k: (k // per, l * per + k % per, j))
    return spec


def _w2_rows_t(l):
    def spec(tm, tn, tk):
        per = MLP_SHARD // tn
        return pl.BlockSpec((None, tn, tk), lambda i, j, k: (j // per, l * per + j % per, k))
    return spec


def _dw1_out(l):
    def spec(tm, tn, tk):
        per = MLP_SHARD // tn
        return pl.BlockSpec((None, tm, tn), lambda i, j, k: (j // per, l * (MLP_SHARD // tm) + i, j % per))
    return spec


def _dw2_out(l):
    def spec(tm, tn, tk):
        per = MLP_SHARD // tm
        return pl.BlockSpec((None, tm, tn), lambda i, j, k: (i // per, l * per + i % per, j))
    return spec


def _pack_vec(d, segs, rows_multiple):
    flat = jnp.concatenate([d[n].reshape(-1) for n, _ in segs])
    rows = -(-flat.shape[0] // LANES)
    rows = -(-rows // rows_multiple) * rows_multiple
    return jnp.pad(flat, (0, rows * LANES - flat.shape[0])).reshape(rows, LANES)


def _unpack_vec(v, segs):
    flat, out, r = v.reshape(-1), {}, 0
    for n, shape in segs:
        k = math.prod(shape)
        out[n] = flat[r:r + k].reshape(shape)
        r += k
    return out


def _block_diag(blocks):
    G, a, b = blocks.shape
    eye = jnp.eye(G, dtype=blocks.dtype)
    return (eye[:, None, :, None] * blocks[:, :, None, :]).reshape(G * a, G * b)


def _diag_blocks(m, G):
    a, b = m.shape[0] // G, m.shape[1] // G
    return jnp.stack([m[g * a:(g + 1) * a, g * b:(g + 1) * b] for g in range(G)])


def _sqrelu_epi(acc):
    r = jnp.maximum(acc, 0.0)
    return acc, r * r


def _sqrelu_bwd_epi(acc, a):
    return (acc * (2.0 * jnp.maximum(a.astype(F32), 0.0)),)


def _mlp_fwd(h, g1, g2, l, tag):
    T, D = h.shape
    a, s = matmul(h, g1, name=f"{tag}_up", mnk=(T, D_FF, D), b_spec=_w1_cols(l), epi=_sqrelu_epi,
                  out_dtypes=(MXU_DTYPE, MXU_DTYPE))
    m = matmul(s, g2, name=f"{tag}_down", mnk=(T, D, D_FF), b_spec=_w2_rows(l))
    return m, (h, a, s)


def _mlp_bwd(saved, dm, g1, g2, l, dg1, dg2, tag):
    h, a, s = saved
    T, D = h.shape
    gshape = (N_CHIPS, 2 * MLP_SHARD, MLP_SHARD)
    da = matmul(dm, g2, tb=True, name=f"{tag}_down_dx", mnk=(T, D_FF, D), b_spec=_w2_rows_t(l),
                epi=_sqrelu_bwd_epi, epi_in=(a,), out_dtype=MXU_DTYPE)
    dg2 = matmul(s, dm, ta=True, name=f"{tag}_down_dw", tm=MLP_SHARD, o_spec=_dw2_out(l), o_shape=gshape, prev=dg2)
    dh = matmul(da, g1, tb=True, name=f"{tag}_up_dx", mnk=(T, D, D_FF), b_spec=_w1_rows_t(l))
    dg1 = matmul(h, da, ta=True, name=f"{tag}_up_dw", o_spec=_dw1_out(l), o_shape=gshape, prev=dg1)
    return dh, dg1, dg2


def kernel(x, mix_pre_g, mix_post_g, mlp_pre_g, mlp_post_g, w_in_even, s5_lam_re, s5_lam_im, s5_log_dt, s5_b_re, s5_b_im, s5_c_re, s5_c_im, s5_d, s5_w_glu, fox_b_f, w_out_even, w_in_odd, pool_w, pool_scale, sgu_ln_g, sgu_ln_b, sgu_w_s, sgu_b_s, w_out_odd, mlp_w1, mlp_w2, loss_target, m_mix_pre_g, m_mix_post_g, m_mlp_pre_g, m_mlp_post_g, m_w_in_even, m_s5_lam_re, m_s5_lam_im, m_s5_log_dt, m_s5_b_re, m_s5_b_im, m_s5_c_re, m_s5_c_im, m_s5_d, m_s5_w_glu, m_fox_b_f, m_w_out_even, m_w_in_odd, m_pool_w, m_pool_scale, m_sgu_ln_g, m_sgu_ln_b, m_sgu_w_s, m_sgu_b_s, m_w_out_odd, m_mlp_w1, m_mlp_w2, v_mix_pre_g, v_mix_post_g, v_mlp_pre_g, v_mlp_post_g, v_w_in_even, v_s5_lam_re, v_s5_lam_im, v_s5_log_dt, v_s5_b_re, v_s5_b_im, v_s5_c_re, v_s5_c_im, v_s5_d, v_s5_w_glu, v_fox_b_f, v_w_out_even, v_w_in_odd, v_pool_w, v_pool_scale, v_sgu_ln_g, v_sgu_ln_b, v_sgu_w_s, v_sgu_b_s, v_w_out_odd, v_mlp_w1, v_mlp_w2):
    names = [n for n, _ in SMALL_SEGS] + [n for n, _, _ in BIG_SEGS] + list(SHARDED_SMALL)
    env = dict(locals())
    W = {n: env[n] for n in names}
    M = {n: env["m_" + n] for n in names}
    V = {n: env["v_" + n] for n in names}

    def shard(n):
        return W[n].reshape(-1, W[n].shape[-1]).astype(WIRE_DTYPE)

    small = jnp.pad(jnp.concatenate([W[n] for n in SHARDED_SMALL]), ((0, SUBLANES - len(SHARDED_SMALL)), (0, 0)))
    gathered, small_all = allgather_chip_shards([shard(n) for n in EARLY_NAMES], small, name="allgather_weights")
    Wf = dict(zip(EARLY_NAMES, gathered))
    for i, n in enumerate(SHARDED_SMALL):
        Wf[n] = small_all[:, i, :].reshape(1, N_CHIPS * LANES)
    for n, _ in SMALL_SEGS:
        Wf[n] = W[n]

    loss8, dx0, halves, dw_in_e, local_small = _local_step(x[0], loss_target[0], Wf, [shard(n) for n in LATE_NAMES])
    loss = lax.psum(loss8[0, 0], MESH_AXES)
    return _reduce_and_update(W, M, V, loss, dx0, halves, dw_in_e, local_small)


def _reduce_to_my_half(gs, names, tag, carry_swap=None, carry_ici=None):
    cx, cy, cc = _coords()
    c_idx = cc.reshape(1).astype(jnp.int32)
    jc_idx = jnp.stack([2 * cx + cy, cc]).astype(jnp.int32)
    swap = swap_halves_exchange(gs)
    from_sibling = carry_swap(swap) if carry_swap else run_exchange(swap, name=f"{tag}_to_sibling")
    sums = [add_sibling_half(g, l, c_idx, name=f"{tag}_chip_sum_{n}") for n, g, l in zip(names, gs, from_sibling)]
    send = chip_partials_exchange([pb for _, pb in sums])
    from_chips = carry_ici(send) if carry_ici else run_exchange(send, name=f"{tag}_to_chips")
    return [add_chip_partials(pf, r, jc_idx, name=f"{tag}_sum_{n}") for n, (pf, _), r in zip(names, sums, from_chips)]


def _local_step(x0, target, P, late_shards):
    T = x0.shape[0]
    mix_pre_g, mix_post_g, mlp_pre_g, mlp_post_g = P["mix_pre_g"], P["mix_post_g"], P["mlp_pre_g"], P["mlp_post_g"]
    s5_lam_re, s5_lam_im, s5_log_dt = P["s5_lam_re"], P["s5_lam_im"], P["s5_log_dt"]
    s5_b_re, s5_b_im, s5_c_re, s5_c_im, s5_d = P["s5_b_re"], P["s5_b_im"], P["s5_c_re"], P["s5_c_im"], P["s5_d"]
    fox_b_f, pool_w, sgu_w_s, sgu_b_s = P["fox_b_f"], P["pool_w"], P["sgu_w_s"], P["sgu_b_s"]
    pool_scale_f, ln_g_f, ln_b_f = P["pool_scale"], P["sgu_ln_g"], P["sgu_ln_b"]
    w_in_e = jnp.pad(_cols_from_chips(P["w_in_even"]), ((0, 0), (0, EVEN_IN_PAD - EVEN_IN)))
    w_glu = P["s5_w_glu"].reshape(S5_WIDTH, S5_WIDTH)

    def gain(a, l):
        return a[l][None, :]

    lr = s5_lam_re[0].reshape(1, S5_LANES)
    li = s5_lam_im[0].reshape(1, S5_LANES)
    ldt = jnp.repeat(s5_log_dt[0], S5_STATE).reshape(1, S5_LANES)
    btr = s5_b_re[0].reshape(S5_LANES, S5_GROUP).T
    bti = s5_b_im[0].reshape(S5_LANES, S5_GROUP).T
    tf_re, tf_im, tb_re, tb_im, bbt_re, bbt_im = s5_disc_fwd(lr, li, ldt, btr, bti, name="s5_disc")
    same_group = (jnp.arange(S5_WIDTH)[:, None] // S5_GROUP) == (jnp.arange(S5_LANES)[None, :] // S5_STATE)
    b_bd = s5_interleave(jnp.where(same_group, jnp.tile(bbt_re, (S5_GROUPS, 1)), 0.0),
                         jnp.where(same_group, jnp.tile(bbt_im, (S5_GROUPS, 1)), 0.0), axis=1)
    cr2 = jnp.transpose(s5_c_re[0], (0, 2, 1)).reshape(S5_LANES, S5_GROUP)
    ci2 = jnp.transpose(s5_c_im[0], (0, 2, 1)).reshape(S5_LANES, S5_GROUP)
    c_bd = s5_interleave(jnp.where(same_group.T, jnp.tile(cr2, (1, S5_GROUPS)), 0.0),
                         -jnp.where(same_group.T, jnp.tile(ci2, (1, S5_GROUPS)), 0.0), axis=0)
    bf_pad = jnp.pad(fox_b_f, ((0, 0), (0, LANES - FOX_HEADS)))

    h1 = rms_fwd(x0, gain(mix_pre_g, 0), name="l0_pre_norm")
    z = matmul(h1, w_in_e, name="l0_in_proj")
    s5_tiles = dict(tm=_pick(T, S5_NB), exact_tiles=True)
    bu = matmul(z, b_bd, mnk=(T, 2 * S5_LANES, S5_CB), tn=S5_NB, a_spec=_chan_rows, b_spec=_s5_b_block,
                name="s5_bu", **s5_tiles)
    xs = s5_scan(bu, tf_re, tf_im, reverse=False, name="s5_scan_fwd")
    yc = matmul(xs, c_bd, mnk=(T, S5_WIDTH, 2 * S5_NB), tn=S5_CB, a_spec=_lanes_of_chan, b_spec=_s5_c_block,
                name="s5_cx", **s5_tiles)
    yl, yg = s5_out_fwd(yc, z, s5_d, name="s5_out")
    gl = matmul(yg, w_glu, name="s5_glu_proj")
    ycat = glu_fwd(yg, gl, out_cols=D_MODEL, name="s5_glu")
    fgate = fox_gate_fwd(z, bf_pad, fl_col=FL_TILE, name="fox_gate")
    f_col = _pairs_col(fgate, T)
    f_row = _col_to_row(f_col, T)
    (ycat, lse_col), late = fox_fwd(z, f_col, f_row, ycat, allgather_ici_exchange(late_shards), name="fox_fwd")
    late = dict(zip(LATE_NAMES, allgather_forward(late, name="allgather_late_weights")))
    w_in_o = _cols_from_chips(late["w_in_odd"])
    w_in_o = jnp.concatenate([w_in_o[:, S5_WIDTH:], w_in_o[:, :S5_WIDTH]], axis=1)
    w_out_e = late["w_out_even"].reshape(D_MODEL, D_MODEL)
    w_out_o = late["w_out_odd"].reshape(D_MODEL, D_MODEL)
    g1, g2 = late["mlp_w1"], late["mlp_w2"]
    mo = matmul(ycat, w_out_e, name="l0_out_proj")
    x1, h2 = res_norm_fwd(x0, mo, gain(mix_post_g, 0), gain(mlp_pre_g, 0), name="l0_post_mlp0_pre_norm")
    m0, mlp0 = _mlp_fwd(h2, g1, g2, 0, "mlp0")

    x2, h3 = res_norm_fwd(x1, m0, gain(mlp_post_g, 0), gain(mix_pre_g, 1), name="mlp0_post_l1_pre_norm")
    z2 = matmul(h3, w_in_o, name="l1_in_proj")
    pooled = pool_window(z2, adjoint=False, in_col=POOL_COL, name="pool_fwd")
    pw_bd = _block_diag(pool_w[0])
    pw = matmul(pooled, pw_bd, name="pool_proj")
    ycat2 = colscale_fwd(pw, pool_scale_f, out_cols=D_MODEL, name="pool_scale")
    causal = jnp.tril(jnp.ones((CHUNK, CHUNK), dtype=bool))
    wsm = jnp.where(causal[None], sgu_w_s[0], 0.0)
    wsmt = jnp.transpose(wsm, (0, 2, 1))
    bst = sgu_b_s[0].T
    ycat2 = sgu_fwd(z2, ln_g_f, ln_b_f, wsm, bst, ycat2, name="sgu_fwd")
    mo2 = matmul(ycat2, w_out_o, name="l1_out_proj")
    x3, h4 = res_norm_fwd(x2, mo2, gain(mix_post_g, 1), gain(mlp_pre_g, 1), name="l1_post_mlp1_pre_norm")
    m1, mlp1 = _mlp_fwd(h4, g1, g2, 1, "mlp1")
    loss8, dx4 = res_norm_loss(x3, m1, gain(mlp_post_g, 1), target, name="mlp1_post_norm_loss")

    dm1, dg_mlp_post1 = rms_bwd(m1, gain(mlp_post_g, 1), dx4, None, name="mlp1_post_norm_bwd")
    dh4, dg1, dg2 = _mlp_bwd(mlp1, dm1, g1, g2, 1, None, None, "mlp1")
    dx3, dmo2, dg_mlp_pre1, dg_mix_post1 = norm_res_bwd(x3, gain(mlp_pre_g, 1), dh4, dx4, mo2, gain(mix_post_g, 1),
                                                        name="mlp1_pre_l1_post_norm_bwd")
    dycat2 = matmul(dmo2, w_out_o, tb=True, name="l1_out_proj_dx")
    dw_out_o = matmul(ycat2, dmo2, ta=True, name="l1_out_proj_dw")
    dpw, dpool_scale = colscale_bwd(pw, pool_scale_f, dycat2, name="pool_scale_bwd")
    dpooled = matmul(dpw, pw_bd, tb=True, name="pool_proj_dx")
    dpw_bd = matmul(pooled, dpw, ta=True, name="pool_proj_dw")
    dz2, dln_g, dln_b, dws, dbst = sgu_bwd(z2, ln_g_f, ln_b_f, wsm, wsmt, bst, dycat2, out_cols=3 * S5_WIDTH,
                                           name="sgu_bwd")
    dz2 = pool_window(dpooled, adjoint=True, into=dz2, out_col=POOL_COL, name="pool_bwd")
    dh3 = matmul(dz2, w_in_o, tb=True, name="l1_in_proj_dx")
    dw_in_o = matmul(h3, dz2, ta=True, name="l1_in_proj_dw")
    dw_in_o = jnp.concatenate([dw_in_o[:, 2 * S5_WIDTH:], dw_in_o[:, :2 * S5_WIDTH]], axis=1)
    dx2, dm0, dg_mix_pre1, dg_mlp_post0 = norm_res_bwd(x2, gain(mix_pre_g, 1), dh3, dx3, m0, gain(mlp_post_g, 0),
                                                       name="l1_pre_mlp0_post_norm_bwd")

    dh2, dg1, dg2 = _mlp_bwd(mlp0, dm0, g1, g2, 0, dg1, dg2, "mlp0")
    dx1, dmo, dg_mlp_pre0, dg_mix_post0 = norm_res_bwd(x1, gain(mlp_pre_g, 0), dh2, dx2, mo, gain(mix_post_g, 0),
                                                       name="mlp0_pre_l0_post_norm_bwd")
    dycat = matmul(dmo, w_out_e, tb=True, name="l0_out_proj_dx")
    dw_out_e = matmul(ycat, dmo, ta=True, name="l0_out_proj_dw")
    dyg_a, dgl = glu_bwd(yg, gl, dycat, name="s5_glu_bwd")
    dyg_b = matmul(dgl, w_glu, tb=True, name="s5_glu_proj_dx")
    dw_glu = matmul(yg, dgl, ta=True, name="s5_glu_proj_dw")
    dyl, du_skip, dd = s5_out_bwd(yl, z, s5_d, dyg_a, dyg_b, name="s5_out_bwd")
    dxs = matmul(dyl, c_bd, tb=True, mnk=(T, 2 * S5_LANES, S5_CB), tn=S5_NB, a_spec=_chan_rows, b_spec=_s5_c_block_t,
                 name="s5_cx_dx", **s5_tiles)
    dc_blocks = matmul(xs, dyl, ta=True, mnk=(2 * S5_LANES, S5_CB, T), tm=S5_NB, tn=S5_CB, b_spec=_chan_cols_of_i,
                       exact_tiles=True, name="s5_cx_dw")
    lam, dab_re, dab_im = s5_scan(dxs, tb_re, tb_im, reverse=True, states=xs, name="s5_scan_bwd")
    db_blocks = matmul(z, lam, ta=True, mnk=(S5_CB, 2 * S5_LANES, T), tm=S5_CB, tn=S5_NB, a_spec=_chan_rows_t,
                       exact_tiles=True, name="s5_bu_dw")
    du_b = matmul(lam, b_bd, tb=True, mnk=(T, S5_WIDTH, 2 * S5_NB), tn=S5_CB, a_spec=_lanes_of_chan,
                  b_spec=_s5_b_block_t, name="s5_bu_dx", **s5_tiles)
    du = add2(du_skip, du_b, name="s5_du")
    early_grads = {"s5_w_glu": dw_glu.reshape(N_CHIPS, -1, S5_WIDTH), "w_out_even": dw_out_e.reshape(N_CHIPS, -1, D_MODEL),
                   "w_in_odd": _chips_from_cols(dw_in_o), "w_out_odd": dw_out_o.reshape(N_CHIPS, -1, D_MODEL),
                   "mlp_w1": dg1, "mlp_w2": dg2}
    fox = {}

    def attention_bwd_q(exchange):
        (fox["dq"], fox["dd"], fox["dfq"]), bufs = fox_bwd_q(z, ycat, dycat, f_col, f_row, lse_col, exchange, name="fox_bwd_q")
        return bufs

    def attention_bwd_kv(exchange):
        (fox["dk"], fox["dv"], fox["df"]), bufs = fox_bwd_kv(z, dycat, f_col, f_row, _col_to_row(lse_col, T),
                                                             _col_to_row(fox["dd"], T), fox["dfq"], exchange, name="fox_bwd_kv")
        return bufs

    halves = _reduce_to_my_half([early_grads[n] for n in REDUCED_EARLY], REDUCED_EARLY, "early_grads",
                                attention_bwd_q, attention_bwd_kv)
    dq, dk, dv = fox["dq"], fox["dk"], fox["dv"]
    dfl, dbf = fox_gate_bwd(z, bf_pad, _pairs_to_lanes(fox["df"], T), fl_col=FL_TILE, name="fox_gate_bwd")
    dz = jnp.concatenate([du, dq, dk, dv, dfl], axis=1)
    dh1 = matmul(dz, w_in_e, tb=True, name="l0_in_proj_dx")
    dw_in_e = matmul(h1, dz, ta=True, name="l0_in_proj_dw")[:, :EVEN_IN]
    dx0, dg_mix_pre0 = rms_bwd(x0, gain(mix_pre_g, 0), dh1, dx1, name="l0_pre_norm_bwd")

    groups_per_block = S5_CB // S5_GROUP
    own_group = (jnp.arange(S5_CB)[:, None] // S5_GROUP) == ((jnp.arange(S5_LANES)[None, :] // S5_STATE) % groups_per_block)
    db_re, db_im = s5_deinterleave(db_blocks, axis=1)
    dbbt_re = jnp.where(own_group, db_re, 0.0).reshape(groups_per_block, S5_GROUP, S5_LANES).sum(0)
    dbbt_im = jnp.where(own_group, db_im, 0.0).reshape(groups_per_block, S5_GROUP, S5_LANES).sum(0)
    dlr, dli, dldt8, dbtr, dbti = s5_disc_bwd(lr, li, ldt, btr, bti, dab_re, dab_im, dbbt_re, dbbt_im, name="s5_disc_bwd")
    dc_re, dc_im = s5_deinterleave(dc_blocks, axis=0)
    dcr2 = jnp.where(own_group.T, dc_re, 0.0).reshape(S5_LANES, groups_per_block, S5_GROUP).sum(1)
    dci2 = -jnp.where(own_group.T, dc_im, 0.0).reshape(S5_LANES, groups_per_block, S5_GROUP).sum(1)

    def c_layout(a):
        return jnp.transpose(a.reshape(S5_GROUPS, S5_STATE, S5_GROUP), (0, 2, 1))[None]

    def b_layout(a):
        return a.T.reshape(1, S5_GROUPS, S5_STATE, S5_GROUP)

    local_small = {
        "mix_pre_g": jnp.concatenate([dg_mix_pre0, dg_mix_pre1]), "mix_post_g": jnp.concatenate([dg_mix_post0, dg_mix_post1]),
        "mlp_pre_g": jnp.concatenate([dg_mlp_pre0, dg_mlp_pre1]), "mlp_post_g": jnp.concatenate([dg_mlp_post0, dg_mlp_post1]),
        "s5_lam_re": dlr.reshape(1, S5_GROUPS, S5_STATE), "s5_lam_im": dli.reshape(1, S5_GROUPS, S5_STATE),
        "s5_log_dt": dldt8[0:1, 0:S5_GROUPS],
        "s5_b_re": b_layout(dbtr), "s5_b_im": b_layout(dbti), "s5_c_re": c_layout(dcr2), "s5_c_im": c_layout(dci2),
        "s5_d": dd, "fox_b_f": dbf[:, 0:FOX_HEADS],
        "pool_w": _diag_blocks(dpw_bd, len(POOL_WINDOWS))[None],
        "sgu_w_s": jnp.where(causal[None], dws, 0.0)[None], "sgu_b_s": dbst.T[None],
        "pool_scale": dpool_scale, "sgu_ln_g": dln_g, "sgu_ln_b": dln_b,
    }
    return loss8, dx0, dict(zip(REDUCED_EARLY, halves)), _chips_from_cols(dw_in_e), local_small


def _reduce_and_update(W, M, V, loss, dx0, halves, dw_in_e, local_small):
    cx, cy, cc = _coords()
    chip = 2 * cx + cy

    vec = _pack_vec(local_small, REDUCED_SEGS, N_DEV * SUBLANES)
    piece = vec.shape[0] // N_DEV
    landed = exchange_pieces(vec.reshape(N_DEV, piece, LANES), scatter=True, name="small_grads_scatter")
    mine = sum_pieces(landed, name="small_grads_sum")
    everyone = exchange_pieces(mine, scatter=False, name="small_grads_gather")
    G = _unpack_vec(everyone, REDUCED_SEGS)
    for n in SHARDED_SMALL:
        G[n] = lax.dynamic_slice_in_dim(G[n], chip * LANES, LANES, axis=1)

    halves = dict(halves)
    halves["w_in_even"] = _reduce_to_my_half([dw_in_e], ["w_in_even"], "late_grads")[0]
    reduced = join_sibling_halves([halves[n] for n in BIG_NAMES], name="big_grads_join")
    for n, r in zip(BIG_NAMES, reduced):
        G[n] = r.reshape(W[n].shape)

    def two_d(a):
        return a.reshape(-1, a.shape[-1])

    delta, new_m, new_v = {}, {}, {}
    for n in BIG_NAMES:
        d_, m_, v_ = adamw(two_d(W[n]), two_d(G[n]), two_d(M[n]), two_d(V[n]), name=f"adamw_{n}")
        delta[n], new_m[n], new_v[n] = (t.reshape(W[n].shape) for t in (d_, m_, v_))
    packed = [_pack_vec(src, SMALL_SEGS, SUBLANES) for src in (W, G, M, V)]
    outs = adamw(*packed, name="adamw_replicated")
    for dst, t in zip((delta, new_m, new_v), outs):
        dst.update(_unpack_vec(t, SMALL_SEGS))
    sharded_segs = tuple((n, (1, LANES)) for n in SHARDED_SMALL)
    packed = [_pack_vec(src, sharded_segs, 1) for src in (W, G, M, V)]
    outs = adamw(*packed, name="adamw_sharded_vectors")
    for dst, t in zip((delta, new_m, new_v), outs):
        dst.update(_unpack_vec(t, sharded_segs))

    order = ["mix_pre_g", "mix_post_g", "mlp_pre_g", "mlp_post_g", "w_in_even", "s5_lam_re", "s5_lam_im", "s5_log_dt",
             "s5_b_re", "s5_b_im", "s5_c_re", "s5_c_im", "s5_d", "s5_w_glu", "fox_b_f", "w_out_even", "w_in_odd",
             "pool_w", "pool_scale", "sgu_ln_g", "sgu_ln_b", "sgu_w_s", "sgu_b_s", "w_out_odd", "mlp_w1", "mlp_w2"]
    return (loss, dx0[None], *[G[n] for n in order], *[delta[n] for n in order],
            *[new_m[n] for n in order], *[new_v[n] for n in order])
```

```python
import functools
import math

import jax
import jax.numpy as jnp
from jax import lax
from jax.experimental import pallas as pl
from jax.experimental.pallas import tpu as pltpu

F32 = jnp.float32
MXU_DTYPE = jnp.bfloat16
WIRE_DTYPE = jnp.bfloat16
EPS = 1e-6
VMEM_LIMIT_BYTES = 48 * 1024 * 1024
LANES = 128
SUBLANES = 8

D_MODEL = 1024
S5_WIDTH = 512
S5_GROUP = 16
S5_GROUPS = 32
S5_STATE = 64
S5_LANES = S5_GROUPS * S5_STATE
FOX_HEADS = 8
FOX_HEAD_DIM = 64
FOX_WIDTH = 512
EVEN_IN = S5_WIDTH + 3 * FOX_WIDTH + FOX_HEADS
EVEN_IN_PAD = 2176
POOL_WINDOWS = (2, 4, 8, 16)
POOL_HALO = 16
POOL_GROUP_DIM = 128
SGU_GROUPS = 4
SGU_GROUP_DIM = 128
CHUNK = 128
D_FF = 4096

ADAM_LR = 0.001
ADAM_B1 = 0.9
ADAM_B2 = 0.999
ADAM_EPS = 1e-08
ADAM_WD = 0.01
ADAM_STEP = 10

MESH_AXES = ("x", "y", "c")
MESH = pl.DeviceIdType.MESH
N_CHIPS = 4
N_DEV = 8

SDS = jax.ShapeDtypeStruct


def _cp(*sem):
    return pltpu.CompilerParams(dimension_semantics=sem, vmem_limit_bytes=VMEM_LIMIT_BYTES)


def _pick(dim, pref):
    if dim <= pref:
        return dim
    t = pref
    while t >= 256:
        if dim % t == 0:
            return t
        t //= 2
    return dim


def _row(tr, c):
    return pl.BlockSpec((tr, c), lambda i: (i, 0))


def _full(shape):
    nd = len(shape)
    return pl.BlockSpec(shape, lambda *_: (0,) * nd)


def _gelu_grad(x):
    c = math.sqrt(2.0 / math.pi)
    t = jnp.tanh(c * (x + 0.044715 * x * x * x))
    return 0.5 * (1.0 + t) + 0.5 * x * (1.0 - t * t) * c * (1.0 + 3.0 * 0.044715 * x * x)


MATMUL_VMEM_BYTES = 36 * 1024 * 1024


def matmul(a, b, *, name, ta=False, tb=False, out_dtype=F32, tm=2048, tn=1024, tk=4096, mnk=None, a_koff=0,
           a_spec=None, b_spec=None, o_spec=None, o_shape=None, prev=None, epi=None, epi_in=(), out_dtypes=None,
           exact_tiles=False):
    if mnk is None:
        M, K = (a.shape[1], a.shape[0]) if ta else a.shape
        K2, N = (b.shape[1], b.shape[0]) if tb else b.shape
        assert K == K2, (a.shape, b.shape, ta, tb)
    else:
        M, N, K = mnk
    out_dtypes = tuple(out_dtypes) if out_dtypes is not None else (out_dtype,)
    n_out, n_epi = len(out_dtypes), len(epi_in)
    tm, tn, tk = _pick(M, tm), _pick(N, tn), _pick(K, tk)

    def vmem_bytes(tm_, tn_, tk_):
        tiles = tm_ * tk_ * a.dtype.itemsize + tk_ * tn_ * b.dtype.itemsize
        tiles += tm_ * tn_ * (sum(jnp.dtype(d).itemsize for d in out_dtypes) + sum(e.dtype.itemsize for e in epi_in))
        return 2 * tiles + tm_ * tn_ * 4 * (tk_ < K)

    def halves(t, dim):
        return [t] + ([t // 2] if t % (2 * LANES) == 0 and t // 2 >= 512 and dim % (t // 2) == 0 else [])

    if exact_tiles:
        halves = lambda t, dim: [t]
    fits = [(m_, n_) for m_ in halves(tm, M) for n_ in halves(tn, N) if vmem_bytes(m_, n_, tk) <= MATMUL_VMEM_BYTES]
    if fits:
        tm, tn = max(fits, key=lambda t: (t[0] * t[1], t[0]))
    else:
        tm, tn = halves(tm, M)[-1], halves(tn, N)[-1]
        while vmem_bytes(tm, tn, tk) > MATMUL_VMEM_BYTES and tk % 2 == 0 and tk > 512:
            tk //= 2
    nk = K // tk
    assert a_koff % tk == 0 and not (ta and a_koff)
    ko = a_koff // tk
    dn = (((0 if ta else 1,), (1 if tb else 0,)), ((), ()))

    def body(*refs):
        a_ref, b_ref = refs[0], refs[1]
        epi_refs = refs[2:2 + n_epi]
        o_refs = refs[len(refs) - n_out - (nk > 1):len(refs) - (nk > 1)]
        k = pl.program_id(2)
        bv = b_ref[...]
        if bv.ndim == 3 and tb:
            cw = bv.shape[-1]
            prod = sum(lax.dot_general(a_ref[:, c * cw:(c + 1) * cw].astype(MXU_DTYPE), bv[c].astype(MXU_DTYPE), dn,
                                       preferred_element_type=F32) for c in range(bv.shape[0]))
        else:
            if bv.ndim == 3:
                bv = bv.reshape(-1, bv.shape[-1])
            prod = lax.dot_general(a_ref[...].astype(MXU_DTYPE), bv.astype(MXU_DTYPE), dn, preferred_element_type=F32)

        def finish(acc):
            res = (acc,) if epi is None else epi(acc, *[r[...] for r in epi_refs])
            for o_ref, r in zip(o_refs, res):
                o_ref[...] = r.astype(o_ref.dtype)

        if nk == 1:
            finish(prod)
            return
        acc_ref = refs[-1]

        @pl.when(k == 0)
        def _():
            acc_ref[...] = prod

        @pl.when(jnp.logical_and(k > 0, k < nk - 1))
        def _():
            acc_ref[...] += prod

        @pl.when(k == nk - 1)
        def _():
            finish(acc_ref[...] + prod)

    if a_spec is None:
        a_spec = pl.BlockSpec((tk, tm), lambda i, j, k: (k, i)) if ta else pl.BlockSpec((tm, tk), lambda i, j, k: (i, k + ko))
    else:
        a_spec = a_spec(tm, tn, tk)
    if b_spec is None:
        bs = pl.BlockSpec((tn, tk), lambda i, j, k: (j, k)) if tb else pl.BlockSpec((tk, tn), lambda i, j, k: (k, j))
    else:
        bs = b_spec(tm, tn, tk)
    tile = pl.BlockSpec((tm, tn), lambda i, j, k: (i, j))
    os_ = tile if o_spec is None else o_spec(tm, tn, tk)
    ins, in_specs, aliases = [a, b, *epi_in], [a_spec, bs] + [tile] * n_epi, {}
    if prev is not None:
        aliases = {len(ins): 0}
        ins.append(prev)
        in_specs.append(pl.BlockSpec(memory_space=pl.ANY))
    shapes = [SDS((M, N) if o_shape is None else o_shape, dt) for dt in out_dtypes]
    outs = pl.pallas_call(
        body, name=name, grid=(M // tm, N // tn, nk),
        in_specs=in_specs, out_specs=[os_] * n_out, out_shape=shapes, input_output_aliases=aliases,
        scratch_shapes=[pltpu.VMEM((tm, tn), F32)] if nk > 1 else [],
        compiler_params=_cp("parallel", "parallel", "arbitrary"),
    )(*ins)
    return outs[0] if n_out == 1 else outs


def _rms_hat(x):
    return x * lax.rsqrt(jnp.mean(x * x, axis=-1, keepdims=True) + EPS)


def rms_fwd(x, g, *, name):
    T, D = x.shape
    tr = _pick(T, 512)

    def body(x_ref, g_ref, o_ref):
        o_ref[...] = (_rms_hat(x_ref[...]) * g_ref[...]).astype(o_ref.dtype)

    return pl.pallas_call(body, name=name, grid=(T // tr,), in_specs=[_row(tr, D), _full((1, D))],
                          out_specs=_row(tr, D), out_shape=SDS((T, D), MXU_DTYPE), compiler_params=_cp("parallel"))(x, g)


def res_norm_fwd(x, y, g_post, g_next, *, name):
    T, D = x.shape
    tr = _pick(T, 512)

    def body(x_ref, y_ref, gp_ref, gn_ref, o_ref, h_ref):
        xn = x_ref[...] + _rms_hat(y_ref[...]) * gp_ref[...]
        o_ref[...] = xn
        h_ref[...] = (_rms_hat(xn) * gn_ref[...]).astype(h_ref.dtype)

    return pl.pallas_call(body, name=name, grid=(T // tr,),
                          in_specs=[_row(tr, D), _row(tr, D), _full((1, D)), _full((1, D))],
                          out_specs=[_row(tr, D), _row(tr, D)], out_shape=[SDS((T, D), F32), SDS((T, D), MXU_DTYPE)],
                          compiler_params=_cp("parallel"))(x, y, g_post, g_next)


def res_norm_loss(x, y, g_post, target, *, name):
    T, D = x.shape
    tr = _pick(T, 512)

    def body(x_ref, y_ref, g_ref, t_ref, l_ref, d_ref):
        err = x_ref[...] + _rms_hat(y_ref[...]) * g_ref[...] - t_ref[...]
        d_ref[...] = err * (1.0 / D)

        @pl.when(pl.program_id(0) == 0)
        def _():
            l_ref[...] = jnp.zeros_like(l_ref)

        l_ref[...] += 0.5 * jnp.sum(jnp.mean(err * err, axis=-1, keepdims=True))

    return pl.pallas_call(body, name=name, grid=(T // tr,),
                          in_specs=[_row(tr, D), _row(tr, D), _full((1, D)), _row(tr, D)],
                          out_specs=[_full((SUBLANES, LANES)), _row(tr, D)],
                          out_shape=[SDS((SUBLANES, LANES), F32), SDS((T, D), F32)],
                          compiler_params=_cp("arbitrary"))(x, y, g_post, target)


def _rms_bwd_rows(x, g, dy):
    r = lax.rsqrt(jnp.mean(x * x, axis=-1, keepdims=True) + EPS)
    xh = x * r
    dxh = dy * g
    return r * (dxh - xh * jnp.mean(dxh * xh, axis=-1, keepdims=True)), jnp.sum(dy * xh, axis=0, keepdims=True)


def norm_res_bwd(x, g_pre, dh, res, y, g_post, *, name):
    T, D = x.shape
    tr = _pick(T, 512)

    def body(x_ref, gp_ref, dh_ref, res_ref, y_ref, gy_ref, dx_ref, dy_ref, dgp_ref, dgy_ref):
        dx, dgp = _rms_bwd_rows(x_ref[...], gp_ref[...], dh_ref[...])
        dx = dx + res_ref[...]
        dx_ref[...] = dx
        dy, dgy = _rms_bwd_rows(y_ref[...], gy_ref[...], dx)
        dy_ref[...] = dy.astype(dy_ref.dtype)

        @pl.when(pl.program_id(0) == 0)
        def _():
            dgp_ref[...] = jnp.zeros_like(dgp_ref)
            dgy_ref[...] = jnp.zeros_like(dgy_ref)

        dgp_ref[...] += dgp
        dgy_ref[...] += dgy

    row, vec = _row(tr, D), _full((1, D))
    return pl.pallas_call(body, name=name, grid=(T // tr,), in_specs=[row, vec, row, row, row, vec],
                          out_specs=[row, row, vec, vec],
                          out_shape=[SDS((T, D), F32), SDS((T, D), MXU_DTYPE), SDS((1, D), F32), SDS((1, D), F32)],
                          compiler_params=_cp("arbitrary"))(x, g_pre, dh, res, y, g_post)


def rms_bwd(x, g, dy, res, *, name):
    T, D = x.shape
    tr = _pick(T, 512)
    has_res = res is not None

    def body(*refs):
        if has_res:
            x_ref, g_ref, dy_ref, res_ref, dx_ref, dg_ref = refs
        else:
            x_ref, g_ref, dy_ref, dx_ref, dg_ref = refs
        dx, dg = _rms_bwd_rows(x_ref[...], g_ref[...], dy_ref[...])
        if has_res:
            dx = dx + res_ref[...]
        dx_ref[...] = dx.astype(dx_ref.dtype)

        @pl.when(pl.program_id(0) == 0)
        def _():
            dg_ref[...] = jnp.zeros_like(dg_ref)

        dg_ref[...] += dg

    ins = [x, g, dy] + ([res] if has_res else [])
    in_specs = [_row(tr, D), _full((1, D)), _row(tr, D)] + ([_row(tr, D)] if has_res else [])
    return pl.pallas_call(body, name=name, grid=(T // tr,), in_specs=in_specs,
                          out_specs=[_row(tr, D), _full((1, D))],
                          out_shape=[SDS((T, D), F32 if has_res else MXU_DTYPE), SDS((1, D), F32)],
                          compiler_params=_cp("arbitrary"))(*ins)


def _s5_disc(lr, li, ldt, btr, bti):
    dt = jnp.exp(ldt)
    k = lax.broadcasted_iota(jnp.int32, (SUBLANES, S5_LANES), 0).astype(F32)
    kf = k + 1.0
    kb = 8.0 - k
    ph = li * dt
    lm = lr * dt
    tf_re = jnp.exp(kf * lm) * jnp.cos(kf * ph)
    tf_im = jnp.exp(kf * lm) * jnp.sin(kf * ph)
    tb_re = jnp.exp(kb * lm) * jnp.cos(kb * ph)
    tb_im = -jnp.exp(kb * lm) * jnp.sin(kb * ph)
    mag = jnp.exp(lm)
    ab_re = mag * jnp.cos(ph)
    ab_im = mag * jnp.sin(ph)
    den = lr * lr + li * li
    nr = ab_re - 1.0
    ni = ab_im
    q_re = (nr * lr + ni * li) / den
    q_im = (ni * lr - nr * li) / den
    bbt_re = q_re * btr - q_im * bti
    bbt_im = q_re * bti + q_im * btr
    return tf_re, tf_im, tb_re, tb_im, bbt_re, bbt_im


def _s5_disc_core(lr, li, ldt, btr, bti):
    dt = jnp.exp(ldt)
    mag = jnp.exp(lr * dt)
    ab_re = mag * jnp.cos(li * dt)
    ab_im = mag * jnp.sin(li * dt)
    den = lr * lr + li * li
    nr = ab_re - 1.0
    ni = ab_im
    q_re = (nr * lr + ni * li) / den
    q_im = (ni * lr - nr * li) / den
    return ab_re, ab_im, q_re * btr - q_im * bti, q_re * bti + q_im * btr


def s5_disc_fwd(lr, li, ldt, btr, bti, *, name):
    def body(lr_ref, li_ref, ldt_ref, btr_ref, bti_ref, *outs):
        vals = _s5_disc(lr_ref[...], li_ref[...], ldt_ref[...], btr_ref[...], bti_ref[...])
        for o, v in zip(outs, vals):
            o[...] = v

    tab = SDS((SUBLANES, S5_LANES), F32)
    bb = SDS((S5_GROUP, S5_LANES), F32)
    return pl.pallas_call(body, name=name, out_shape=[tab, tab, tab, tab, bb, bb])(lr, li, ldt, btr, bti)


def s5_disc_bwd(lr, li, ldt, btr, bti, dab_re, dab_im, dbbt_re, dbbt_im, *, name):
    def body(lr_ref, li_ref, ldt_ref, btr_ref, bti_ref, dar_ref, dai_ref, dbr_ref, dbi_ref,
             dlr_ref, dli_ref, dldt_ref, dbtr_ref, dbti_ref):
        _, vjp = jax.vjp(_s5_disc_core, lr_ref[...], li_ref[...], ldt_ref[...], btr_ref[...], bti_ref[...])
        dlr, dli, dldt, dbtr, dbti = vjp((dar_ref[...], dai_ref[...], dbr_ref[...], dbi_ref[...]))
        dlr_ref[...] = dlr
        dli_ref[...] = dli
        dbtr_ref[...] = dbtr
        dbti_ref[...] = dbti
        lane_group = lax.broadcasted_iota(jnp.int32, (S5_LANES, LANES), 0) // S5_STATE
        col = lax.broadcasted_iota(jnp.int32, (S5_LANES, LANES), 1)
        ind = (lane_group == col).astype(F32)
        dldt_ref[...] = jnp.dot(jnp.broadcast_to(dldt, (SUBLANES, S5_LANES)), ind,
                                precision=lax.Precision.HIGHEST, preferred_element_type=F32)

    row = SDS((1, S5_LANES), F32)
    bb = SDS((S5_GROUP, S5_LANES), F32)
    return pl.pallas_call(body, name=name, out_shape=[row, row, SDS((SUBLANES, LANES), F32), bb, bb])(
        lr, li, ldt, btr, bti, dab_re, dab_im, dbbt_re, dbbt_im)


S5_NB = 1024


S5_CB = S5_WIDTH * S5_NB // S5_LANES


def _chan_rows(tm, tn, tk):
    return pl.BlockSpec((tm, S5_CB), lambda i, j, k: (i, j // 2))


def _chan_rows_t(tm, tn, tk):
    return pl.BlockSpec((tk, S5_CB), lambda i, j, k: (k, j // 2))


def _chan_cols_of_i(tm, tn, tk):
    return pl.BlockSpec((tk, S5_CB), lambda i, j, k: (k, i // 2))


def _s5_b_block(tm, tn, tk):
    return pl.BlockSpec((S5_CB, S5_NB), lambda i, j, k: (j // 2, j))


def _s5_c_block_t(tm, tn, tk):
    return pl.BlockSpec((S5_NB, S5_CB), lambda i, j, k: (j, j // 2))


def _lanes_of_chan(tm, tn, tk):
    return pl.BlockSpec((tm, 2 * S5_NB), lambda i, j, k: (i, j))


def _s5_b_block_t(tm, tn, tk):
    return pl.BlockSpec((S5_CB, 2 * S5_NB), lambda i, j, k: (j, j))


def _s5_c_block(tm, tn, tk):
    return pl.BlockSpec((2 * S5_NB, S5_CB), lambda i, j, k: (j, j))


def s5_interleave(re, im, axis):
    parts = []
    for n in range(S5_LANES // S5_NB):
        sl = [slice(None)] * re.ndim
        sl[axis] = slice(n * S5_NB, (n + 1) * S5_NB)
        parts += [re[tuple(sl)], im[tuple(sl)]]
    return jnp.concatenate(parts, axis=axis)


def s5_deinterleave(a, axis):
    re, im = [], []
    for n in range(S5_LANES // S5_NB):
        sl = [slice(None)] * a.ndim
        sl[axis] = slice(2 * n * S5_NB, (2 * n + 1) * S5_NB)
        re.append(a[tuple(sl)])
        sl[axis] = slice((2 * n + 1) * S5_NB, (2 * n + 2) * S5_NB)
        im.append(a[tuple(sl)])
    return jnp.concatenate(re, axis=axis), jnp.concatenate(im, axis=axis)


def s5_scan(bu, tab_re, tab_im, *, reverse, name, states=None, hosted=None):
    T = bu.shape[0]
    nb = S5_NB
    tc = _pick(T, 256)
    nl = S5_LANES // nb
    nt = T // tc
    ntile = tc // SUBLANES
    with_da = states is not None
    assert reverse or not with_da
    step_rows = ((1, 7), (2, 6), (4, 4)) if reverse else ((1, 0), (2, 1), (4, 3))

    def body(*refs):
        if with_da:
            (br_ref, bi_ref, tr_ref, ti_ref, sr_ref, si_ref, hr_ref, hi_ref, xo_ref, dar_ref, dai_ref,
             cr_ref, ci_ref, mr_ref, mi_ref, ar_ref, ai_ref) = refs
        else:
            br_ref, bi_ref, tr_ref, ti_ref, xo_ref, cr_ref, ci_ref, mr_ref, mi_ref = refs

        @pl.when(pl.program_id(1) == 0)
        def _():
            cr_ref[...] = jnp.zeros_like(cr_ref)
            ci_ref[...] = jnp.zeros_like(ci_ref)
            if with_da:
                ar_ref[...] = jnp.zeros_like(ar_ref)
                ai_ref[...] = jnp.zeros_like(ai_ref)

        seen = jnp.where(pl.program_id(1) < nt - 1, 1.0, 0.0)

        def add_da(lr, li, r0, last_r, last_i):
            first = lax.broadcasted_iota(jnp.int32, (SUBLANES, nb), 0) == 0
            pr = jnp.where(first, last_r, pltpu.roll(sr_ref[pl.ds(r0, SUBLANES), :], 1, 0))
            pi = jnp.where(first, last_i, pltpu.roll(si_ref[pl.ds(r0, SUBLANES), :], 1, 0))
            ar_ref[...] += lr * pr + li * pi
            ai_ref[...] += li * pr - lr * pi

        io = lax.broadcasted_iota(jnp.int32, (SUBLANES, nb), 0)
        for s_, (d, r) in enumerate(step_rows):
            keep = (io < SUBLANES - d) if reverse else (io >= d)
            mr_ref[s_] = jnp.where(keep, tr_ref[r:r + 1, :], 0.0)
            mi_ref[s_] = jnp.where(keep, ti_ref[r:r + 1, :], 0.0)

        def tile(i, carry):
            cr, ci = carry
            j = (ntile - 1 - i) if reverse else i
            r0 = pl.multiple_of(j * SUBLANES, SUBLANES)
            xr = br_ref[pl.ds(r0, SUBLANES), :]
            xi = bi_ref[pl.ds(r0, SUBLANES), :]
            for s_, (d, _) in enumerate(step_rows):
                sh = (SUBLANES - d) if reverse else d
                sr = pltpu.roll(xr, sh, 0)
                si = pltpu.roll(xi, sh, 0)
                pr, pi = mr_ref[s_], mi_ref[s_]
                xr, xi = xr + pr * sr - pi * si, xi + pr * si + pi * sr
            tr, ti = tr_ref[...], ti_ref[...]
            xr, xi = xr + tr * cr - ti * ci, xi + tr * ci + ti * cr
            xo_ref[pl.ds(r0, SUBLANES), 0:nb] = xr
            xo_ref[pl.ds(r0, SUBLANES), nb:2 * nb] = xi
            if with_da:
                @pl.when(j > 0)
                def _():
                    p0 = pl.multiple_of(r0 - SUBLANES, SUBLANES)
                    add_da(xr, xi, r0, sr_ref[pl.ds(p0, SUBLANES), :][SUBLANES - 1:SUBLANES, :],
                           si_ref[pl.ds(p0, SUBLANES), :][SUBLANES - 1:SUBLANES, :])

                @pl.when(j == 0)
                def _():
                    add_da(xr, xi, r0, hr_ref[SUBLANES - 1:SUBLANES, :] * seen, hi_ref[SUBLANES - 1:SUBLANES, :] * seen)
            if reverse:
                return xr[0:1, :], xi[0:1, :]
            return xr[SUBLANES - 1:SUBLANES, :], xi[SUBLANES - 1:SUBLANES, :]

        cr, ci = lax.fori_loop(0, ntile, tile, (cr_ref[0:1, :], ci_ref[0:1, :]))
        cr_ref[0:1, :] = cr
        ci_ref[0:1, :] = ci
        if with_da:
            @pl.when(pl.program_id(1) == nt - 1)
            def _():
                dar_ref[...] = jnp.sum(ar_ref[...], axis=0, keepdims=True)
                dai_ref[...] = jnp.sum(ai_ref[...], axis=0, keepdims=True)

    def tmap(t):
        return (nt - 1 - t) if reverse else t

    hb = tc // SUBLANES
    re_spec = pl.BlockSpec((tc, nb), lambda n, t: (tmap(t), 2 * n))
    im_spec = pl.BlockSpec((tc, nb), lambda n, t: (tmap(t), 2 * n + 1))
    tab_spec = pl.BlockSpec((SUBLANES, nb), lambda n, t: (0, n))
    out_spec = pl.BlockSpec((tc, 2 * nb), lambda n, t: (tmap(t), n))
    out_shape = SDS((T, 2 * S5_LANES), F32)
    scratch = [pltpu.VMEM((SUBLANES, nb), F32), pltpu.VMEM((SUBLANES, nb), F32),
               pltpu.VMEM((len(step_rows), SUBLANES, nb), F32), pltpu.VMEM((len(step_rows), SUBLANES, nb), F32)]
    if not with_da:
        return pl.pallas_call(body, name=name, grid=(nl, nt), in_specs=[re_spec, im_spec, tab_spec, tab_spec],
                              out_specs=out_spec, out_shape=out_shape, scratch_shapes=scratch,
                              compiler_params=_cp("parallel", "arbitrary"))(bu, bu, tab_re, tab_im)
    re_halo = pl.BlockSpec((SUBLANES, nb), lambda n, t: (jnp.maximum(tmap(t) * hb - 1, 0), 2 * n))
    im_halo = pl.BlockSpec((SUBLANES, nb), lambda n, t: (jnp.maximum(tmap(t) * hb - 1, 0), 2 * n + 1))
    acc = pl.BlockSpec((1, nb), lambda n, t: (0, n))
    row = SDS((1, S5_LANES), F32)
    return call_hosting(
        body, hosted, name=name, grid=(nl, nt),
        in_specs=[re_spec, im_spec, tab_spec, tab_spec, re_spec, im_spec, re_halo, im_halo],
        out_specs=[out_spec, acc, acc], out_shape=[out_shape, row, row],
        inputs=[bu, bu, tab_re, tab_im, states, states, states, states], aliases={},
        scratch=scratch + [pltpu.VMEM((SUBLANES, nb), F32), pltpu.VMEM((SUBLANES, nb), F32)])


def s5_out_fwd(yc, u, d, *, name):
    T, C = yc.shape
    tr = _pick(T, 512)

    def body(yc_ref, u_ref, d_ref, yl_ref, yg_ref):
        yl = yc_ref[...] + d_ref[...] * u_ref[...]
        yl_ref[...] = yl
        yg_ref[...] = jax.nn.gelu(yl)

    return pl.pallas_call(body, name=name, grid=(T // tr,), in_specs=[_row(tr, C), _row(tr, C), _full((1, C))],
                          out_specs=[_row(tr, C)] * 2, out_shape=[SDS((T, C), F32)] * 2,
                          compiler_params=_cp("parallel"))(yc, u, d)


def glu_fwd(yg, gl, *, out_cols, name):
    T, C = yg.shape
    tr = _pick(T, 512)

    def body(yg_ref, gl_ref, o_ref):
        o_ref[...] = yg_ref[...] * jax.nn.sigmoid(gl_ref[...])

    return pl.pallas_call(body, name=name, grid=(T // tr,), in_specs=[_row(tr, C)] * 2, out_specs=_row(tr, C),
                          out_shape=SDS((T, out_cols), F32), compiler_params=_cp("parallel"))(yg, gl)


def glu_bwd(yg, gl, dy, *, name):
    T, C = yg.shape
    tr = _pick(T, 512)

    def body(yg_ref, gl_ref, dy_ref, dyg_ref, dgl_ref):
        s = jax.nn.sigmoid(gl_ref[...])
        dyv = dy_ref[...]
        dyg_ref[...] = dyv * s
        dgl_ref[...] = dyv * yg_ref[...] * s * (1.0 - s)

    return pl.pallas_call(body, name=name, grid=(T // tr,), in_specs=[_row(tr, C)] * 3, out_specs=[_row(tr, C)] * 2,
                          out_shape=[SDS((T, C), F32)] * 2, compiler_params=_cp("parallel"))(yg, gl, dy)


def s5_out_bwd(yl, u, d, dyg_a, dyg_b, *, name):
    T, C = yl.shape
    tr = _pick(T, 512)

    def body(yl_ref, u_ref, d_ref, da_ref, db_ref, dyl_ref, du_ref, dd_ref):
        dyl = (da_ref[...] + db_ref[...]) * _gelu_grad(yl_ref[...])
        dyl_ref[...] = dyl
        du_ref[...] = dyl * d_ref[...]

        @pl.when(pl.program_id(0) == 0)
        def _():
            dd_ref[...] = jnp.zeros_like(dd_ref)

        dd_ref[...] += jnp.sum(dyl * u_ref[...], axis=0, keepdims=True)

    return pl.pallas_call(body, name=name, grid=(T // tr,),
                          in_specs=[_row(tr, C), _row(tr, C), _full((1, C)), _row(tr, C), _row(tr, C)],
                          out_specs=[_row(tr, C), _row(tr, C), _full((1, C))],
                          out_shape=[SDS((T, C), F32), SDS((T, C), F32), SDS((1, C), F32)],
                          compiler_params=_cp("arbitrary"))(yl, u, d, dyg_a, dyg_b)


def add2(a, b, *, name):
    T, C = a.shape
    tr = _pick(T, 512)

    def body(a_ref, b_ref, o_ref):
        o_ref[...] = a_ref[...] + b_ref[...]

    return pl.pallas_call(body, name=name, grid=(T // tr,), in_specs=[_row(tr, C)] * 2, out_specs=_row(tr, C),
                          out_shape=SDS((T, C), F32), compiler_params=_cp("parallel"))(a, b)


def _tri(n, upper):
    r = lax.broadcasted_iota(jnp.int32, (n, n), 0)
    c = lax.broadcasted_iota(jnp.int32, (n, n), 1)
    return ((c >= r) if upper else (c <= r)).astype(F32)


def fox_gate_fwd(fl, bf, *, fl_col, name):
    T = fl.shape[0]
    tb = _pick(T, 256)

    def body(fl_ref, bf_ref, f_ref, c_ref):
        @pl.when(pl.program_id(0) == 0)
        def _():
            c_ref[...] = jnp.zeros_like(c_ref)

        lf = jax.nn.log_sigmoid(fl_ref[...] + bf_ref[...])
        f = jnp.dot(_tri(tb, False), lf, precision=lax.Precision.HIGHEST, preferred_element_type=F32) + c_ref[0:1, :]
        f_ref[...] = f * LOG2E
        c_ref[0:1, :] = f[tb - 1:tb, :]

    fl_spec = pl.BlockSpec((tb, LANES), lambda i: (i, fl_col))
    return pl.pallas_call(body, name=name, grid=(T // tb,), in_specs=[fl_spec, _full((1, LANES))],
                          out_specs=_row(tb, LANES), out_shape=SDS((T, LANES), F32),
                          scratch_shapes=[pltpu.VMEM((SUBLANES, LANES), F32)], compiler_params=_cp("arbitrary"))(fl, bf)


def fox_gate_bwd(fl, bf, df_keys, df_queries, *, fl_col, name):
    T = fl.shape[0]
    tb = _pick(T, 256)
    nt = T // tb

    def body(fl_ref, bf_ref, dfk_ref, dfq_ref, dfl_ref, dbf_ref, c_ref):
        @pl.when(pl.program_id(0) == 0)
        def _():
            c_ref[...] = jnp.zeros_like(c_ref)
            dbf_ref[...] = jnp.zeros_like(dbf_ref)

        dlf = jnp.dot(_tri(tb, True), dfk_ref[...] + dfq_ref[...], precision=lax.Precision.HIGHEST,
                      preferred_element_type=F32) + c_ref[0:1, :]
        c_ref[0:1, :] = dlf[0:1, :]
        dfl = dlf * jax.nn.sigmoid(-(fl_ref[...] + bf_ref[...]))
        dfl_ref[...] = dfl
        dbf_ref[...] += jnp.sum(dfl, axis=0, keepdims=True)

    rev = pl.BlockSpec((tb, LANES), lambda i: (nt - 1 - i, 0))
    fl_rev = pl.BlockSpec((tb, LANES), lambda i: (nt - 1 - i, fl_col))
    return pl.pallas_call(body, name=name, grid=(nt,), in_specs=[fl_rev, _full((1, LANES)), rev, rev],
                          out_specs=[rev, _full((1, LANES))], out_shape=[SDS((T, LANES), F32), SDS((1, LANES), F32)],
                          scratch_shapes=[pltpu.VMEM((SUBLANES, LANES), F32)],
                          compiler_params=_cp("arbitrary"))(fl, bf, df_keys, df_queries)


FOX_BLOCK = 512
FOX_PAIRS = FOX_HEADS // 2
_NT = (((1,), (1,)), ((), ()))


LOG2E = 1.4426950408889634
FOX_FWD_UNROLL = 4
FOX_BWD_UNROLL = 2


def _fox_block(T):
    return _pick(T, FOX_BLOCK)


def _own_lanes(lane, hh):
    return (lane < FOX_HEAD_DIM) if hh == 0 else (lane >= FOX_HEAD_DIM)


def _grouped_steps(step, lo, n, unroll, init):
    def trip(t, c):
        for u in range(unroll):
            c = step(lo + t * unroll + u, c)
        return c

    carry = lax.fori_loop(0, n // unroll, trip, init)
    for u in range(unroll - 1):
        carry = lax.cond(n % unroll > u, lambda c: step(lo + (n // unroll) * unroll + u, c), lambda c: c, carry)
    return carry


Q_TILE0, K_TILE0, V_TILE0, O_TILE0 = 4, 8, 12, 4
FL_TILE = 16
POOL_COL = 2


def fox_fwd(z, f_col, f_row, ycat, hosted, *, name):
    T = z.shape[0]
    blk = _fox_block(T)
    nb = T // blk
    scale = FOX_HEAD_DIM ** -0.5

    def body(q_ref, k_ref, v_ref, fc_ref, fr_ref, prev_ref, o_ref, l_ref):
        i = pl.program_id(1)
        row = lax.broadcasted_iota(jnp.int32, (blk, blk), 0)
        col = lax.broadcasted_iota(jnp.int32, (blk, blk), 1)
        lane = lax.broadcasted_iota(jnp.int32, (blk, LANES), 1)
        qt = q_ref[...] * (scale * LOG2E)
        outs = []
        for hh in range(2):
            qh = jnp.where(_own_lanes(lane, hh), qt, 0.0).astype(MXU_DTYPE)
            fi = fc_ref[0, :, hh:hh + 1]

            def step(j, carry, masked=False):
                m, l, acc = carry
                r0 = pl.multiple_of(j * blk, blk)
                kj = k_ref[pl.ds(r0, blk), :].astype(MXU_DTYPE)
                vj = v_ref[pl.ds(r0, blk), :].astype(MXU_DTYPE)
                s = lax.dot_general(qh, kj, _NT, preferred_element_type=F32) + (fi - fr_ref[0, j, hh:hh + 1, :])
                if masked:
                    s = jnp.where(col <= row, s, -jnp.inf)
                m_new = jnp.maximum(m, jnp.max(s, axis=-1, keepdims=True))
                p = jnp.exp2(s - m_new)
                alpha = jnp.exp2(m - m_new)
                l = alpha * l + jnp.sum(p, axis=-1, keepdims=True)
                acc = alpha * acc + jnp.dot(p.astype(MXU_DTYPE), vj, preferred_element_type=F32)
                return m_new, l, acc

            init = (jnp.full((blk, 1), -jnp.inf, F32), jnp.zeros((blk, 1), F32), jnp.zeros((blk, LANES), F32))
            m, l, acc = step(i, _grouped_steps(step, 0, i, FOX_FWD_UNROLL, init), True)
            outs.append(acc / l)
            l_ref[0, :, hh:hh + 1] = m + jnp.log2(l)
        o_ref[...] = jnp.where(_own_lanes(lane, 0), outs[0], outs[1])

    qspec = pl.BlockSpec((blk, LANES), lambda h, i: (i, Q_TILE0 + h))
    kspec = pl.BlockSpec((T, LANES), lambda h, i: (0, K_TILE0 + h))
    vspec = pl.BlockSpec((T, LANES), lambda h, i: (0, V_TILE0 + h))
    ospec = pl.BlockSpec((blk, LANES), lambda h, i: (i, O_TILE0 + h))
    cspec = pl.BlockSpec((1, blk, 2), lambda h, i: (h, i, 0))
    rspec = pl.BlockSpec((1, nb, 2, blk), lambda h, i: (h, 0, 0, 0))
    return call_hosting(body, hosted, name=name, grid=(FOX_PAIRS, nb),
                        in_specs=[qspec, kspec, vspec, cspec, rspec, ANY], out_specs=[ospec, cspec],
                        out_shape=[SDS(ycat.shape, F32), SDS((FOX_PAIRS, T, 2), F32)],
                        inputs=[z, z, z, f_col, f_row, ycat], aliases={5: 0})


def fox_dd(ycat, dycat, *, name):
    T = ycat.shape[0]
    blk = _fox_block(T)

    def body(o_ref, do_ref, dd_ref):
        lane = lax.broadcasted_iota(jnp.int32, (blk, LANES), 1)
        prod = do_ref[...] * o_ref[...]
        for hh in range(2):
            dd_ref[0, :, hh:hh + 1] = jnp.sum(jnp.where(_own_lanes(lane, hh), prod, 0.0), axis=-1, keepdims=True)

    ospec = pl.BlockSpec((blk, LANES), lambda h, i: (i, O_TILE0 + h))
    return pl.pallas_call(body, name=name, grid=(FOX_PAIRS, T // blk), in_specs=[ospec, ospec],
                          out_specs=pl.BlockSpec((1, blk, 2), lambda h, i: (h, i, 0)),
                          out_shape=SDS((FOX_PAIRS, T, 2), F32), compiler_params=_cp("parallel", "parallel"))(ycat, dycat)


def fox_bwd(z, dycat, f_col, f_row, lse_row, dd_row, hosted, *, name):
    T = z.shape[0]
    blk = _fox_block(T)
    nb = T // blk
    scale = FOX_HEAD_DIM ** -0.5

    def body(q_ref, k_ref, v_ref, do_ref, fc_ref, fr_ref, lr_ref, dr_ref, dk_ref, dv_ref, df_ref, dqt_ref, dfq_ref):
        j = pl.program_id(1)

        @pl.when(j == 0)
        def _():
            dqt_ref[...] = jnp.zeros_like(dqt_ref)
            dfq_ref[...] = jnp.zeros_like(dfq_ref)

        row = lax.broadcasted_iota(jnp.int32, (blk, blk), 0)
        col = lax.broadcasted_iota(jnp.int32, (blk, blk), 1)
        lane = lax.broadcasted_iota(jnp.int32, (blk, LANES), 1)
        kt = k_ref[...]
        vt = v_ref[...]
        dks, dvs = [], []
        for hh in range(2):
            own = _own_lanes(lane, hh)
            kh = jnp.where(own, kt, 0.0).astype(MXU_DTYPE)
            vh = jnp.where(own, vt, 0.0).astype(MXU_DTYPE)
            kht = kh.T
            fj = fc_ref[0, :, hh:hh + 1]

            def step(i, carry, masked=False):
                dk, dv, df = carry
                r0 = pl.multiple_of(i * blk, blk)
                qi = (q_ref[pl.ds(r0, blk), :] * (scale * LOG2E)).astype(MXU_DTYPE)
                doi = do_ref[pl.ds(r0, blk), :].astype(MXU_DTYPE)
                st = lax.dot_general(kh, qi, _NT, preferred_element_type=F32) + (fr_ref[0, i, hh:hh + 1, :] - fj)
                pt = jnp.exp2(st - lr_ref[0, i, hh:hh + 1, :])
                if masked:
                    pt = jnp.where(col >= row, pt, 0.0)
                dv = dv + jnp.dot(pt.astype(MXU_DTYPE), doi, preferred_element_type=F32)
                dpt = lax.dot_general(vh, doi, _NT, preferred_element_type=F32)
                dst = pt * (dpt - dr_ref[0, i, hh:hh + 1, :])
                dsb = dst.astype(MXU_DTYPE)
                dk = dk + jnp.dot(dsb, qi, preferred_element_type=F32)
                df = df - jnp.sum(dst, axis=-1, keepdims=True)
                dqt_ref[0, i] += jnp.dot(kht, dsb, preferred_element_type=F32)
                dfq_ref[0, i, hh:hh + 1, :] += jnp.sum(dst, axis=0, keepdims=True)
                return dk, dv, df

            init = (jnp.zeros((blk, LANES), F32), jnp.zeros((blk, LANES), F32), jnp.zeros((blk, 1), F32))
            dk, dv, df = _grouped_steps(step, j + 1, nb - 1 - j, FOX_BWD_UNROLL, step(j, init, True))
            dks.append(dk * (1.0 / LOG2E))
            dvs.append(dv)
            df_ref[0, :, hh:hh + 1] = df
        dk_ref[...] = jnp.where(_own_lanes(lane, 0), dks[0], dks[1])
        dv_ref[...] = jnp.where(_own_lanes(lane, 0), dvs[0], dvs[1])

    bspec = pl.BlockSpec((blk, LANES), lambda h, j: (j, h))
    qspec = pl.BlockSpec((T, LANES), lambda h, j: (0, Q_TILE0 + h))
    kspec = pl.BlockSpec((blk, LANES), lambda h, j: (j, K_TILE0 + h))
    vspec = pl.BlockSpec((blk, LANES), lambda h, j: (j, V_TILE0 + h))
    dospec = pl.BlockSpec((T, LANES), lambda h, j: (0, O_TILE0 + h))
    cspec = pl.BlockSpec((1, blk, 2), lambda h, j: (h, j, 0))
    rspec = pl.BlockSpec((1, nb, 2, blk), lambda h, j: (h, 0, 0, 0))
    dqspec = pl.BlockSpec((1, nb, LANES, blk), lambda h, j: (h, 0, 0, 0))
    return call_hosting(body, hosted, name=name, grid=(FOX_PAIRS, nb),
                        in_specs=[qspec, kspec, vspec, dospec, cspec, rspec, rspec, rspec],
                        out_specs=[bspec, bspec, cspec, dqspec, rspec],
                        out_shape=[SDS((T, FOX_WIDTH), F32), SDS((T, FOX_WIDTH), F32), SDS((FOX_PAIRS, T, 2), F32),
                                   SDS((FOX_PAIRS, nb, LANES, blk), F32), SDS((FOX_PAIRS, nb, 2, blk), F32)],
                        inputs=[z, z, z, dycat, f_col, f_row, lse_row, dd_row], aliases={})


def _pairs_col(a, T):
    return jnp.transpose(a[:, :FOX_HEADS].reshape(T, FOX_PAIRS, 2), (1, 0, 2))


def _col_to_row(a, T):
    blk = _fox_block(T)
    return jnp.transpose(a.reshape(FOX_PAIRS, T // blk, blk, 2), (0, 1, 3, 2))


def _row_to_col(a, T):
    return jnp.transpose(a, (0, 1, 3, 2)).reshape(FOX_PAIRS, T, 2)


def _pairs_to_lanes(a, T):
    flat = jnp.transpose(a, (1, 0, 2)).reshape(T, FOX_HEADS)
    return jnp.pad(flat, ((0, 0), (0, LANES - FOX_HEADS)))


def _pool_counts(t0, n, w):
    t = (t0 + lax.broadcasted_iota(jnp.int32, (n, 1), 0)).astype(F32)
    return jnp.minimum(t + 1.0, float(w))


def pool_window(x, *, adjoint, name, in_col=0, into=None, out_col=0):
    T, C = x.shape[0], len(POOL_WINDOWS) * POOL_GROUP_DIM
    tr = _pick(T, 512)
    nt = T // tr
    hb = tr // POOL_HALO
    n = tr + POOL_HALO

    def body(x_ref, h_ref, *rest):
        o_ref = rest[-1]
        i = pl.program_id(0)
        cur = x_ref[...]
        if adjoint:
            halo = h_ref[...] * jnp.where(i < nt - 1, 1.0, 0.0)
            ext = jnp.concatenate([cur, halo], axis=0)
            t0 = i * tr
        else:
            halo = h_ref[...] * jnp.where(i > 0, 1.0, 0.0)
            ext = jnp.concatenate([halo, cur], axis=0)
            t0 = i * tr - POOL_HALO
        sums = {}
        for g, w in enumerate(POOL_WINDOWS):
            ls = slice(g * POOL_GROUP_DIM, (g + 1) * POOL_GROUP_DIM)
            s = ext[:, ls]
            if adjoint:
                s = s / _pool_counts(t0, n, w)
            d = 1
            while d < w:
                s = s + pltpu.roll(s, (n - d) if adjoint else d, 0)
                d *= 2
            if adjoint:
                o_ref[:, ls] = s[0:tr, :] - cur[:, ls]
            else:
                o_ref[:, ls] = s[POOL_HALO:n, :] / _pool_counts(i * tr, tr, w) - cur[:, ls]

    if adjoint:
        halo_spec = pl.BlockSpec((POOL_HALO, C), lambda i: (jnp.minimum((i + 1) * hb, T // POOL_HALO - 1), in_col))
    else:
        halo_spec = pl.BlockSpec((POOL_HALO, C), lambda i: (jnp.maximum(i * hb - 1, 0), in_col))
    x_spec = pl.BlockSpec((tr, C), lambda i: (i, in_col))
    if into is None:
        return pl.pallas_call(body, name=name, grid=(nt,), in_specs=[x_spec, halo_spec], out_specs=_row(tr, C),
                              out_shape=SDS((T, C), F32), compiler_params=_cp("parallel"))(x, x)
    return pl.pallas_call(body, name=name, grid=(nt,), in_specs=[x_spec, halo_spec, ANY],
                          out_specs=pl.BlockSpec((tr, C), lambda i: (i, out_col)), out_shape=SDS(into.shape, F32),
                          input_output_aliases={2: 0}, compiler_params=_cp("parallel"))(x, x, into)


def colscale_fwd(a, s, *, out_cols, name):
    T, C = a.shape
    tr = _pick(T, 512)

    def body(a_ref, s_ref, o_ref):
        o_ref[...] = a_ref[...] * s_ref[...]

    return pl.pallas_call(body, name=name, grid=(T // tr,), in_specs=[_row(tr, C), _full((1, C))], out_specs=_row(tr, C),
                          out_shape=SDS((T, out_cols), F32), compiler_params=_cp("parallel"))(a, s)


def colscale_bwd(a, s, dy, *, name):
    T, C = a.shape
    tr = _pick(T, 512)

    def body(a_ref, s_ref, dy_ref, da_ref, ds_ref):
        dyv = dy_ref[...]
        da_ref[...] = dyv * s_ref[...]

        @pl.when(pl.program_id(0) == 0)
        def _():
            ds_ref[...] = jnp.zeros_like(ds_ref)

        ds_ref[...] += jnp.sum(dyv * a_ref[...], axis=0, keepdims=True)

    return pl.pallas_call(body, name=name, grid=(T // tr,), in_specs=[_row(tr, C), _full((1, C)), _row(tr, C)],
                          out_specs=[_row(tr, C), _full((1, C))], out_shape=[SDS((T, C), F32), SDS((1, C), F32)],
                          compiler_params=_cp("arbitrary"))(a, s, dy)


SGU_ROWS = 512


def _sgu_norm(v, ln_g, ln_b):
    vg = jax.nn.gelu(v)
    xc = vg - jnp.mean(vg, axis=-1, keepdims=True)
    r = lax.rsqrt(jnp.mean(xc * xc, axis=-1, keepdims=True) + EPS)
    xh = xc * r
    return xh * ln_g + ln_b, xh, r


def _rowc(tr, c, cb):
    return pl.BlockSpec((tr, c), lambda i: (i, cb))


def sgu_fwd(z, ln_g, ln_b, ws, bst, ycat, *, name):
    T, C = z.shape[0], SGU_GROUPS * SGU_GROUP_DIM
    tr = _pick(T, SGU_ROWS)

    def body(u_ref, v_ref, g_ref, b_ref, ws_ref, bst_ref, prev_ref, o_ref):
        vn, _, _ = _sgu_norm(v_ref[...], g_ref[...], b_ref[...])
        vn = vn.astype(MXU_DTYPE)
        ug = jax.nn.gelu(u_ref[...])
        for g in range(SGU_GROUPS):
            w = ws_ref[g].astype(MXU_DTYPE)
            bias = bst_ref[:, g:g + 1]
            for c in range(tr // CHUNK):
                rs = slice(c * CHUNK, (c + 1) * CHUNK)
                ls = slice(g * SGU_GROUP_DIM, (g + 1) * SGU_GROUP_DIM)
                mixed = jnp.dot(w, vn[rs, ls], preferred_element_type=F32) + bias
                o_ref[rs, ls] = ug[rs, ls] * mixed

    return pl.pallas_call(body, name=name, grid=(T // tr,),
                          in_specs=[_rowc(tr, C, 0), _rowc(tr, C, 1), _full((1, C)), _full((1, C)),
                                    _full((SGU_GROUPS, CHUNK, CHUNK)), _full((CHUNK, SGU_GROUPS)), ANY],
                          out_specs=_rowc(tr, C, 1), out_shape=SDS(ycat.shape, F32), input_output_aliases={6: 0},
                          compiler_params=_cp("parallel"))(z, z, ln_g, ln_b, ws, bst, ycat)


def sgu_bwd(z, ln_g, ln_b, ws, wst, bst, dycat, *, out_cols, name):
    T, C = z.shape[0], SGU_GROUPS * SGU_GROUP_DIM
    tr = _pick(T, SGU_ROWS)

    def body(u_ref, v_ref, g_ref, b_ref, ws_ref, wst_ref, bst_ref, dy_ref,
             duv_ref, dg_ref, db_ref, dws_ref, dbst_ref, dvn_ref):
        du_ref = duv_ref.at[:, 0:C]
        dv_ref = duv_ref.at[:, C:2 * C]
        @pl.when(pl.program_id(0) == 0)
        def _():
            dg_ref[...] = jnp.zeros_like(dg_ref)
            db_ref[...] = jnp.zeros_like(db_ref)
            dws_ref[...] = jnp.zeros_like(dws_ref)
            dbst_ref[...] = jnp.zeros_like(dbst_ref)

        uv = u_ref[...]
        vv = v_ref[...]
        vn, xh, r = _sgu_norm(vv, g_ref[...], b_ref[...])
        vn = vn.astype(MXU_DTYPE)
        ug = jax.nn.gelu(uv)
        dyv = dy_ref[...]
        for g in range(SGU_GROUPS):
            w = ws_ref[g].astype(MXU_DTYPE)
            wt = wst_ref[g].astype(MXU_DTYPE)
            bias = bst_ref[:, g:g + 1]
            dw = jnp.zeros((CHUNK, CHUNK), F32)
            dbias = jnp.zeros((CHUNK, 1), F32)
            for c in range(tr // CHUNK):
                rs = slice(c * CHUNK, (c + 1) * CHUNK)
                ls = slice(g * SGU_GROUP_DIM, (g + 1) * SGU_GROUP_DIM)
                vblk = vn[rs, ls]
                mixed = jnp.dot(w, vblk, preferred_element_type=F32) + bias
                dyb = dyv[rs, ls]
                du_ref[rs, ls] = dyb * mixed * _gelu_grad(uv[rs, ls])
                dmixed = dyb * ug[rs, ls]
                dbias = dbias + jnp.sum(dmixed, axis=-1, keepdims=True)
                dmb = dmixed.astype(MXU_DTYPE)
                dw = dw + lax.dot_general(dmb, vblk, _NT, preferred_element_type=F32)
                dvn_ref[rs, ls] = jnp.dot(wt, dmb, preferred_element_type=F32)
            dws_ref[g] += dw
            dbst_ref[:, g:g + 1] += dbias
        dvn = dvn_ref[...]
        dg_ref[...] += jnp.sum(dvn * xh, axis=0, keepdims=True)
        db_ref[...] += jnp.sum(dvn, axis=0, keepdims=True)
        dxh = dvn * g_ref[...]
        dvg = r * (dxh - jnp.mean(dxh, axis=-1, keepdims=True) - xh * jnp.mean(dxh * xh, axis=-1, keepdims=True))
        dv_ref[...] = dvg * _gelu_grad(vv)

    wspec = _full((SGU_GROUPS, CHUNK, CHUNK))
    return pl.pallas_call(body, name=name, grid=(T // tr,),
                          in_specs=[_rowc(tr, C, 0), _rowc(tr, C, 1), _full((1, C)), _full((1, C)), wspec, wspec,
                                    _full((CHUNK, SGU_GROUPS)), _rowc(tr, C, 1)],
                          out_specs=[_rowc(tr, 2 * C, 0), _full((1, C)), _full((1, C)), wspec,
                                     _full((CHUNK, SGU_GROUPS))],
                          out_shape=[SDS((T, out_cols), F32), SDS((1, C), F32), SDS((1, C), F32),
                                     SDS((SGU_GROUPS, CHUNK, CHUNK), F32), SDS((CHUNK, SGU_GROUPS), F32)],
                          scratch_shapes=[pltpu.VMEM((tr, C), F32)],
                          compiler_params=_cp("arbitrary"))(z, z, ln_g, ln_b, ws, wst, bst, dycat)


def adamw(w, g, m, v, *, name):
    R, C = w.shape
    tr = _pick(R, 512)
    c1 = 1.0 - ADAM_B1 ** ADAM_STEP
    c2 = 1.0 - ADAM_B2 ** ADAM_STEP

    def body(w_ref, g_ref, m_ref, v_ref, d_ref, nm_ref, nv_ref):
        gv = g_ref[...]
        nm = ADAM_B1 * m_ref[...] + (1.0 - ADAM_B1) * gv
        nv = ADAM_B2 * v_ref[...] + (1.0 - ADAM_B2) * (gv * gv)
        nm_ref[...] = nm
        nv_ref[...] = nv
        d_ref[...] = -ADAM_LR * ((nm / c1) / (jnp.sqrt(nv / c2) + ADAM_EPS) + ADAM_WD * w_ref[...])

    spec = _row(tr, C)
    return pl.pallas_call(body, name=name, grid=(R // tr,), in_specs=[spec] * 4, out_specs=[spec] * 3,
                          out_shape=[SDS((R, C), F32)] * 3, compiler_params=_cp("parallel"))(w, g, m, v)


ANY = pl.BlockSpec(memory_space=pl.ANY)


def _coords():
    return lax.axis_index("x"), lax.axis_index("y"), lax.axis_index("c")


def _other_chips(x, y):
    return [(1 - x, y), (x, 1 - y), (1 - x, 1 - y)]


def _remote(src, dst, send_sems, recv_sems, k, dev):
    return pltpu.make_async_remote_copy(src_ref=src, dst_ref=dst, send_sem=send_sems.at[k], recv_sem=recv_sems.at[k],
                                        device_id=dev, device_id_type=MESH)


LOCAL_CHUNKS = 8


def allgather_chip_shards(shards, small, *, name):
    na = len(shards)

    def body(*refs):
        s_refs, sm_ref = refs[:na], refs[na]
        o_refs, smo_ref = refs[na + 1:2 * na + 1], refs[2 * na + 1]
        send_sems, recv_sems, local_sems = refs[2 * na + 2:]
        x, y, c = _coords()
        j = 2 * x + y
        sibling = (x, y, 1 - c)
        chips = _other_chips(x, y)
        for a in range(na):
            chunk = shards[a].shape[0] // LOCAL_CHUNKS
            for q in range(LOCAL_CHUNKS):
                rows = pl.ds(q * chunk, chunk)
                pltpu.make_async_copy(s_refs[a].at[rows], o_refs[a].at[j, rows], local_sems.at[a]).start()
        pltpu.make_async_copy(sm_ref, smo_ref.at[j], local_sems.at[na]).start()
        sends = []
        for a in range(na):
            half = shards[a].shape[0] // 2
            mine = pl.ds(c * half, half)
            for k, (px, py) in enumerate(chips):
                sends.append(_remote(s_refs[a].at[mine], o_refs[a].at[j, mine], send_sems, recv_sems, 6 * a + k, (px, py, c)))
        for k, (px, py) in enumerate(chips):
            sends.append(_remote(sm_ref, smo_ref.at[j], send_sems, recv_sems, 6 * na + k, (px, py, c)))
        for cp in sends:
            cp.start()
        for a in range(na):
            half = shards[a].shape[0] // 2
            mine = pl.ds(c * half, half)
            for k, (px, py) in enumerate(chips):
                rows = o_refs[a].at[2 * px + py, mine]
                _remote(rows, rows, send_sems, recv_sems, 6 * a + k, (px, py, c)).wait_recv()
                fw = _remote(rows, rows, send_sems, recv_sems, 6 * a + 3 + k, sibling)
                fw.start()
                sends.append(fw)
        for a in range(na):
            half = shards[a].shape[0] // 2
            theirs = pl.ds((1 - c) * half, half)
            for k, (px, py) in enumerate(chips):
                rows = o_refs[a].at[2 * px + py, theirs]
                _remote(rows, rows, send_sems, recv_sems, 6 * a + 3 + k, sibling).wait_recv()
        for k, (px, py) in enumerate(chips):
            slot = smo_ref.at[2 * px + py]
            _remote(slot, slot, send_sems, recv_sems, 6 * na + k, (px, py, c)).wait_recv()
        for cp in sends:
            cp.wait_send()
        for a in range(na):
            pltpu.make_async_copy(s_refs[a], o_refs[a].at[j], local_sems.at[a]).wait()
        pltpu.make_async_copy(sm_ref, smo_ref.at[j], local_sems.at[na]).wait()

    nsem = 6 * na + 3
    outs = pl.pallas_call(
        body, name=name, in_specs=[ANY] * (na + 1), out_specs=[ANY] * (na + 1),
        out_shape=[SDS((N_CHIPS,) + s.shape, s.dtype) for s in shards] + [SDS((N_CHIPS,) + small.shape, small.dtype)],
        scratch_shapes=[pltpu.SemaphoreType.DMA((nsem,)), pltpu.SemaphoreType.DMA((nsem,)),
                        pltpu.SemaphoreType.DMA((na + 1,))])(*shards, small)
    return outs[:na], outs[na]


class Exchange:
    def __init__(self, ins, out_shapes, scratch, start, wait):
        self.ins, self.out_shapes, self.scratch, self.start, self.wait = list(ins), list(out_shapes), list(scratch), start, wait


def run_exchange(ex, *, name):
    ni, no = len(ex.ins), len(ex.out_shapes)

    def body(*refs):
        parts = refs[:ni], refs[ni:ni + no], refs[ni + no:]
        ex.start(*parts)
        ex.wait(*parts)

    return pl.pallas_call(body, name=name, in_specs=[ANY] * ni, out_specs=[ANY] * no, out_shape=ex.out_shapes,
                          scratch_shapes=ex.scratch)(*ex.ins)


def call_hosting(body, ex, *, name, grid, in_specs, out_specs, out_shape, inputs, aliases, scratch=()):
    n_in, n_out, ni, no, ns = len(inputs), len(out_shape), len(ex.ins), len(ex.out_shapes), len(scratch)
    outs_at = n_in + ni
    scr_at = outs_at + n_out + no

    def wrapped(*refs):
        own = refs[:n_in] + refs[outs_at:outs_at + n_out] + refs[scr_at:scr_at + ns]
        parts = refs[n_in:outs_at], refs[outs_at + n_out:scr_at], refs[scr_at + ns:]
        ids = [pl.program_id(d) for d in range(len(grid))]
        first = functools.reduce(jnp.logical_and, [i == 0 for i in ids])
        last = functools.reduce(jnp.logical_and, [i == g - 1 for i, g in zip(ids, grid)])

        @pl.when(first)
        def _():
            ex.start(*parts)

        body(*own)

        @pl.when(last)
        def _():
            ex.wait(*parts)

    outs = pl.pallas_call(
        wrapped, name=name, grid=grid, in_specs=list(in_specs) + [ANY] * ni, out_specs=list(out_specs) + [ANY] * no,
        out_shape=list(out_shape) + ex.out_shapes, input_output_aliases=aliases,
        scratch_shapes=list(scratch) + ex.scratch,
        compiler_params=_cp(*["arbitrary"] * len(grid)))(*inputs, *ex.ins)
    return outs[:n_out], outs[n_out:]


def allgather_ici_exchange(shards):
    na = len(shards)

    def copies(s_refs, o_refs, sems):
        send_sems, recv_sems, _ = sems
        x, y, c = _coords()
        j = 2 * x + y
        out = []
        for a in range(na):
            half = shards[a].shape[0] // 2
            mine = pl.ds(c * half, half)
            for k, (px, py) in enumerate(_other_chips(x, y)):
                send = _remote(s_refs[a].at[mine], o_refs[a].at[j, mine], send_sems, recv_sems, 3 * a + k, (px, py, c))
                rows = o_refs[a].at[2 * px + py, mine]
                out.append((send, _remote(rows, rows, send_sems, recv_sems, 3 * a + k, (px, py, c))))
        return out

    def start(s_refs, o_refs, sems):
        x, y, c = _coords()
        j = 2 * x + y
        for a in range(na):
            chunk = shards[a].shape[0] // LOCAL_CHUNKS
            for q in range(LOCAL_CHUNKS):
                rows = pl.ds(q * chunk, chunk)
                pltpu.make_async_copy(s_refs[a].at[rows], o_refs[a].at[j, rows], sems[2].at[a]).start()
        for send, _ in copies(s_refs, o_refs, sems):
            send.start()

    def wait(s_refs, o_refs, sems):
        x, y, c = _coords()
        j = 2 * x + y
        for send, arrival in copies(s_refs, o_refs, sems):
            arrival.wait_recv()
            send.wait_send()
        for a in range(na):
            pltpu.make_async_copy(s_refs[a], o_refs[a].at[j], sems[2].at[a]).wait()

    return Exchange(shards, [SDS((N_CHIPS,) + s.shape, s.dtype) for s in shards],
                    [pltpu.SemaphoreType.DMA((3 * na,)), pltpu.SemaphoreType.DMA((3 * na,)), pltpu.SemaphoreType.DMA((na,))],
                    start, wait)


def allgather_forward(gathered, *, name):
    na = len(gathered)

    def body(*refs):
        o_refs = refs[na:2 * na]
        send_sems, recv_sems = refs[2 * na:]
        x, y, c = _coords()
        sibling = (x, y, 1 - c)
        cps = []
        for a in range(na):
            half = gathered[a].shape[1] // 2
            for k, (px, py) in enumerate(_other_chips(x, y)):
                mine = o_refs[a].at[2 * px + py, pl.ds(c * half, half)]
                theirs = o_refs[a].at[2 * px + py, pl.ds((1 - c) * half, half)]
                cps.append((_remote(mine, mine, send_sems, recv_sems, 3 * a + k, sibling),
                            _remote(theirs, theirs, send_sems, recv_sems, 3 * a + k, sibling)))
        for send, _ in cps:
            send.start()
        for send, arrival in cps:
            send.wait_send()
            arrival.wait_recv()

    return pl.pallas_call(body, name=name, in_specs=[ANY] * na, out_specs=[ANY] * na,
                          out_shape=[SDS(g.shape, g.dtype) for g in gathered],
                          input_output_aliases={a: a for a in range(na)},
                          scratch_shapes=[pltpu.SemaphoreType.DMA((3 * na,)), pltpu.SemaphoreType.DMA((3 * na,))])(*gathered)


def swap_halves_exchange(gs):
    na = len(gs)

    def copies(g_refs, o_refs, sems):
        x, y, c = _coords()
        out = []
        for a in range(na):
            half = gs[a].shape[1] // 2
            out.append(_remote(g_refs[a].at[:, pl.ds((1 - c) * half, half), :], o_refs[a], sems[0], sems[1], a,
                               (x, y, 1 - c)))
        return out

    def start(g_refs, o_refs, sems):
        for cp in copies(g_refs, o_refs, sems):
            cp.start()

    def wait(g_refs, o_refs, sems):
        for cp in copies(g_refs, o_refs, sems):
            cp.wait()

    return Exchange(gs, [SDS((g.shape[0], g.shape[1] // 2, g.shape[2]), g.dtype) for g in gs],
                    [pltpu.SemaphoreType.DMA((na,)), pltpu.SemaphoreType.DMA((na,))], start, wait)


def chip_partials_exchange(pbs):
    na = len(pbs)

    def copies(p_refs, o_refs, sems):
        x, y, c = _coords()
        out = []
        for a in range(na):
            for k, (px, py) in enumerate(_other_chips(x, y)):
                out.append(_remote(p_refs[a].at[2 * px + py], o_refs[a].at[k], sems[0], sems[1], 3 * a + k, (px, py, c)))
        return out

    def start(p_refs, o_refs, sems):
        for cp in copies(p_refs, o_refs, sems):
            cp.start()

    def wait(p_refs, o_refs, sems):
        for cp in copies(p_refs, o_refs, sems):
            cp.wait()

    return Exchange(pbs, [SDS((3,) + p.shape[1:], p.dtype) for p in pbs],
                    [pltpu.SemaphoreType.DMA((3 * na,)), pltpu.SemaphoreType.DMA((3 * na,))], start, wait)


def add_sibling_half(g, land, c_idx, *, name):
    n, R, C = g.shape
    half = R // 2
    tr = _pick(half, 256)
    nt = half // tr

    def body(c_ref, g_ref, l_ref, of_ref, ob_ref):
        s = g_ref[...] + l_ref[...].astype(F32)
        of_ref[...] = s
        ob_ref[...] = s.astype(ob_ref.dtype)

    blk = pl.BlockSpec((1, tr, C), lambda s, i, c_ref: (s, i, 0))
    gblk = pl.BlockSpec((1, tr, C), lambda s, i, c_ref: (s, c_ref[0] * nt + i, 0))
    return pl.pallas_call(
        body, name=name,
        grid_spec=pltpu.PrefetchScalarGridSpec(num_scalar_prefetch=1, grid=(n, nt), in_specs=[gblk, blk],
                                               out_specs=[blk, blk]),
        out_shape=[SDS((n, half, C), F32), SDS((n, half, C), WIRE_DTYPE)],
        compiler_params=_cp("parallel", "parallel"))(c_idx, g, land)


def add_chip_partials(pf, rb, jc_idx, *, name):
    n, H, C = pf.shape
    tr = _pick(H, 256)

    def body(jc_ref, p_ref, r_ref, o_ref):
        s = p_ref[0]
        for k in range(3):
            s = s + r_ref[k].astype(F32)
        o_ref[...] = s

    pblk = pl.BlockSpec((1, tr, C), lambda i, jc_ref: (jc_ref[0], i, 0))
    rblk = pl.BlockSpec((3, tr, C), lambda i, jc_ref: (0, i, 0))
    oblk = pl.BlockSpec((None, tr, C), lambda i, jc_ref: (jc_ref[1], i, 0))
    return pl.pallas_call(
        body, name=name,
        grid_spec=pltpu.PrefetchScalarGridSpec(num_scalar_prefetch=1, grid=(H // tr,), in_specs=[pblk, rblk],
                                               out_specs=oblk),
        out_shape=SDS((2, H, C), F32), compiler_params=_cp("parallel"))(jc_idx, pf, rb)


def join_sibling_halves(bufs, *, name):
    na = len(bufs)

    def body(*refs):
        o_refs = refs[na:2 * na]
        send_sems, recv_sems = refs[2 * na:]
        x, y, c = _coords()
        cps = [_remote(o_refs[a].at[c], o_refs[a].at[c], send_sems, recv_sems, a, (x, y, 1 - c)) for a in range(na)]
        for cp in cps:
            cp.start()
        for a in range(na):
            cps[a].wait_send()
            _remote(o_refs[a].at[1 - c], o_refs[a].at[1 - c], send_sems, recv_sems, a, (x, y, 1 - c)).wait_recv()

    return pl.pallas_call(body, name=name, in_specs=[ANY] * na, out_specs=[ANY] * na,
                          out_shape=[SDS(b.shape, b.dtype) for b in bufs],
                          input_output_aliases={a: a for a in range(na)},
                          scratch_shapes=[pltpu.SemaphoreType.DMA((na,)), pltpu.SemaphoreType.DMA((na,))])(*bufs)


def exchange_pieces(v, *, scatter, name):
    P, C = v.shape[-2:]

    def body(v_ref, o_ref, send_sems, recv_sems, local_sem):
        x, y, c = _coords()
        me = 4 * x + 2 * y + c
        local = pltpu.make_async_copy(v_ref.at[me] if scatter else v_ref, o_ref.at[me], local_sem)
        local.start()
        cps = []
        for m in range(1, N_DEV):
            px = (1 - x) if m & 4 else x
            py = (1 - y) if m & 2 else y
            pc = (1 - c) if m & 1 else c
            src = v_ref.at[4 * px + 2 * py + pc] if scatter else v_ref
            cps.append(_remote(src, o_ref.at[me], send_sems, recv_sems, m - 1, (px, py, pc)))
        for cp in cps:
            cp.start()
        for cp in cps:
            cp.wait_send()
        for m in range(1, N_DEV):
            px = (1 - x) if m & 4 else x
            py = (1 - y) if m & 2 else y
            pc = (1 - c) if m & 1 else c
            slot = o_ref.at[4 * px + 2 * py + pc]
            _remote(slot, slot, send_sems, recv_sems, m - 1, (px, py, pc)).wait_recv()
        local.wait()

    return pl.pallas_call(body, name=name, in_specs=[ANY], out_specs=ANY, out_shape=SDS((N_DEV, P, C), v.dtype),
                          scratch_shapes=[pltpu.SemaphoreType.DMA((N_DEV - 1,)), pltpu.SemaphoreType.DMA((N_DEV - 1,)),
                                          pltpu.SemaphoreType.DMA(())])(v)


def sum_pieces(land, *, name):
    n, P, C = land.shape

    def body(l_ref, o_ref):
        s = l_ref[0]
        for d in range(1, n):
            s = s + l_ref[d]
        o_ref[...] = s

    return pl.pallas_call(body, name=name, out_shape=SDS((P, C), F32))(land)


BIG_SEGS = (
    ("w_in_even", (1024, 514), 1),
    ("s5_w_glu", (128, 512), 0),
    ("w_out_even", (256, 1024), 0),
    ("w_in_odd", (1024, 384), 1),
    ("w_out_odd", (256, 1024), 0),
    ("mlp_w1", (2, 1024, 1024), 2),
    ("mlp_w2", (2, 1024, 1024), 1),
)
BIG_NAMES = tuple(n for n, _, _ in BIG_SEGS)
EARLY_NAMES = ("w_in_even", "s5_w_glu")
LATE_NAMES = ("w_out_even", "w_in_odd", "w_out_odd", "mlp_w1", "mlp_w2")
REDUCED_EARLY = ("s5_w_glu", "w_out_even", "w_in_odd", "w_out_odd", "mlp_w1", "mlp_w2")
SHARDED_SMALL = ("pool_scale", "sgu_ln_g", "sgu_ln_b")
SMALL_SEGS = (
    ("mix_pre_g", (2, 1024)), ("mix_post_g", (2, 1024)), ("mlp_pre_g", (2, 1024)), ("mlp_post_g", (2, 1024)),
    ("s5_lam_re", (1, 32, 64)), ("s5_lam_im", (1, 32, 64)), ("s5_log_dt", (1, 32)),
    ("s5_b_re", (1, 32, 64, 16)), ("s5_b_im", (1, 32, 64, 16)), ("s5_c_re", (1, 32, 16, 64)), ("s5_c_im", (1, 32, 16, 64)),
    ("s5_d", (1, 512)), ("fox_b_f", (1, 8)), ("pool_w", (1, 4, 128, 128)), ("sgu_w_s", (1, 4, 128, 128)),
    ("sgu_b_s", (1, 4, 128)),
)
REDUCED_SEGS = SMALL_SEGS + tuple((n, (1, 512)) for n in SHARDED_SMALL)


def _cols_from_chips(g):
    n, R, C = g.shape
    return jnp.transpose(g, (1, 0, 2)).reshape(R, n * C)


def _chips_from_cols(m):
    R, C4 = m.shape
    return jnp.transpose(m.reshape(R, N_CHIPS, C4 // N_CHIPS), (1, 0, 2))


MLP_SHARD = 1024


def _w1_cols(l):
    def spec(tm, tn, tk):
        per = MLP_SHARD // tn
        return pl.BlockSpec((None, tk, tn), lambda i, j, k: (j // per, l * (MLP_SHARD // tk) + k, j % per))
    return spec


def _w1_rows_t(l):
    def spec(tm, tn, tk):
        if tk == N_CHIPS * MLP_SHARD:
            return pl.BlockSpec((N_CHIPS, tn, MLP_SHARD), lambda i, j, k: (0, l * (MLP_SHARD // tn) + j, 0))
        per = MLP_SHARD // tk
        return pl.BlockSpec((None, tn, tk), lambda i, j, k: (k // per, l * (MLP_SHARD // tn) + j, k % per))
    return spec


def _w2_rows(l):
    def spec(tm, tn, tk):
        if tk == N_CHIPS * MLP_SHARD:
            return pl.BlockSpec((N_CHIPS, MLP_SHARD, tn), lambda i, j, k: (0, l, j))
        per = MLP_SHARD // tk
        return pl.BlockSpec((None, tk, tn), lambda i, j, k: (k // per, l * per + k % per, j))
    return spec


def _w2_rows_t(l):
    def spec(tm, tn, tk):
        per = MLP_SHARD // tn
        return pl.BlockSpec((None, tn, tk), lambda i, j, k: (j // per, l * per + j % per, k))
    return spec


def _dw1_out(l):
    def spec(tm, tn, tk):
        per = MLP_SHARD // tn
        return pl.BlockSpec((None, tm, tn), lambda i, j, k: (j // per, l * (MLP_SHARD // tm) + i, j % per))
    return spec


def _dw2_out(l):
    def spec(tm, tn, tk):
        per = MLP_SHARD // tm
        return pl.BlockSpec((None, tm, tn), lambda i, j, k: (i // per, l * per + i % per, j))
    return spec


def _pack_vec(d, segs, rows_multiple):
    flat = jnp.concatenate([d[n].reshape(-1) for n, _ in segs])
    rows = -(-flat.shape[0] // LANES)
    rows = -(-rows // rows_multiple) * rows_multiple
    return jnp.pad(flat, (0, rows * LANES - flat.shape[0])).reshape(rows, LANES)


def _unpack_vec(v, segs):
    flat, out, r = v.reshape(-1), {}, 0
    for n, shape in segs:
        k = math.prod(shape)
        out[n] = flat[r:r + k].reshape(shape)
        r += k
    return out


def _block_diag(blocks):
    G, a, b = blocks.shape
    eye = jnp.eye(G, dtype=blocks.dtype)
    return (eye[:, None, :, None] * blocks[:, :, None, :]).reshape(G * a, G * b)


def _diag_blocks(m, G):
    a, b = m.shape[0] // G, m.shape[1] // G
    return jnp.stack([m[g * a:(g + 1) * a, g * b:(g + 1) * b] for g in range(G)])


def _sqrelu_epi(acc):
    r = jnp.maximum(acc, 0.0)
    return acc, r * r


def _sqrelu_bwd_epi(acc, a):
    return (acc * (2.0 * jnp.maximum(a.astype(F32), 0.0)),)


def _mlp_fwd(h, g1, g2, l, tag):
    T, D = h.shape
    a, s = matmul(h, g1, name=f"{tag}_up", mnk=(T, D_FF, D), b_spec=_w1_cols(l), epi=_sqrelu_epi,
                  out_dtypes=(MXU_DTYPE, MXU_DTYPE))
    m = matmul(s, g2, name=f"{tag}_down", mnk=(T, D, D_FF), b_spec=_w2_rows(l))
    return m, (h, a, s)


def _mlp_bwd(saved, dm, g1, g2, l, dg1, dg2, tag):
    h, a, s = saved
    T, D = h.shape
    gshape = (N_CHIPS, 2 * MLP_SHARD, MLP_SHARD)
    da = matmul(dm, g2, tb=True, name=f"{tag}_down_dx", mnk=(T, D_FF, D), b_spec=_w2_rows_t(l),
                epi=_sqrelu_bwd_epi, epi_in=(a,), out_dtype=MXU_DTYPE)
    dg2 = matmul(s, dm, ta=True, name=f"{tag}_down_dw", tm=MLP_SHARD, o_spec=_dw2_out(l), o_shape=gshape, prev=dg2)
    dh = matmul(da, g1, tb=True, name=f"{tag}_up_dx", mnk=(T, D, D_FF), b_spec=_w1_rows_t(l))
    dg1 = matmul(h, da, ta=True, name=f"{tag}_up_dw", o_spec=_dw1_out(l), o_shape=gshape, prev=dg1)
    return dh, dg1, dg2


def kernel(x, mix_pre_g, mix_post_g, mlp_pre_g, mlp_post_g, w_in_even, s5_lam_re, s5_lam_im, s5_log_dt, s5_b_re, s5_b_im, s5_c_re, s5_c_im, s5_d, s5_w_glu, fox_b_f, w_out_even, w_in_odd, pool_w, pool_scale, sgu_ln_g, sgu_ln_b, sgu_w_s, sgu_b_s, w_out_odd, mlp_w1, mlp_w2, loss_target, m_mix_pre_g, m_mix_post_g, m_mlp_pre_g, m_mlp_post_g, m_w_in_even, m_s5_lam_re, m_s5_lam_im, m_s5_log_dt, m_s5_b_re, m_s5_b_im, m_s5_c_re, m_s5_c_im, m_s5_d, m_s5_w_glu, m_fox_b_f, m_w_out_even, m_w_in_odd, m_pool_w, m_pool_scale, m_sgu_ln_g, m_sgu_ln_b, m_sgu_w_s, m_sgu_b_s, m_w_out_odd, m_mlp_w1, m_mlp_w2, v_mix_pre_g, v_mix_post_g, v_mlp_pre_g, v_mlp_post_g, v_w_in_even, v_s5_lam_re, v_s5_lam_im, v_s5_log_dt, v_s5_b_re, v_s5_b_im, v_s5_c_re, v_s5_c_im, v_s5_d, v_s5_w_glu, v_fox_b_f, v_w_out_even, v_w_in_odd, v_pool_w, v_pool_scale, v_sgu_ln_g, v_sgu_ln_b, v_sgu_w_s, v_sgu_b_s, v_w_out_odd, v_mlp_w1, v_mlp_w2):
    names = [n for n, _ in SMALL_SEGS] + [n for n, _, _ in BIG_SEGS] + list(SHARDED_SMALL)
    env = dict(locals())
    W = {n: env[n] for n in names}
    M = {n: env["m_" + n] for n in names}
    V = {n: env["v_" + n] for n in names}

    def shard(n):
        return W[n].reshape(-1, W[n].shape[-1]).astype(WIRE_DTYPE)

    small = jnp.pad(jnp.concatenate([W[n] for n in SHARDED_SMALL]), ((0, SUBLANES - len(SHARDED_SMALL)), (0, 0)))
    gathered, small_all = allgather_chip_shards([shard(n) for n in EARLY_NAMES], small, name="allgather_weights")
    Wf = dict(zip(EARLY_NAMES, gathered))
    for i, n in enumerate(SHARDED_SMALL):
        Wf[n] = small_all[:, i, :].reshape(1, N_CHIPS * LANES)
    for n, _ in SMALL_SEGS:
        Wf[n] = W[n]

    loss8, dx0, halves, dw_in_e, local_small = _local_step(x[0], loss_target[0], Wf, [shard(n) for n in LATE_NAMES])
    loss = lax.psum(loss8[0, 0], MESH_AXES)
    return _reduce_and_update(W, M, V, loss, dx0, halves, dw_in_e, local_small)


def _reduce_to_my_half(gs, names, tag, carry_swap=None, carry_ici=None):
    cx, cy, cc = _coords()
    c_idx = cc.reshape(1).astype(jnp.int32)
    jc_idx = jnp.stack([2 * cx + cy, cc]).astype(jnp.int32)
    swap = swap_halves_exchange(gs)
    from_sibling = carry_swap(swap) if carry_swap else run_exchange(swap, name=f"{tag}_to_sibling")
    sums = [add_sibling_half(g, l, c_idx, name=f"{tag}_chip_sum_{n}") for n, g, l in zip(names, gs, from_sibling)]
    send = chip_partials_exchange([pb for _, pb in sums])
    from_chips = carry_ici(send) if carry_ici else run_exchange(send, name=f"{tag}_to_chips")
    return [add_chip_partials(pf, r, jc_idx, name=f"{tag}_sum_{n}") for n, (pf, _), r in zip(names, sums, from_chips)]


def _local_step(x0, target, P, late_shards):
    T = x0.shape[0]
    mix_pre_g, mix_post_g, mlp_pre_g, mlp_post_g = P["mix_pre_g"], P["mix_post_g"], P["mlp_pre_g"], P["mlp_post_g"]
    s5_lam_re, s5_lam_im, s5_log_dt = P["s5_lam_re"], P["s5_lam_im"], P["s5_log_dt"]
    s5_b_re, s5_b_im, s5_c_re, s5_c_im, s5_d = P["s5_b_re"], P["s5_b_im"], P["s5_c_re"], P["s5_c_im"], P["s5_d"]
    fox_b_f, pool_w, sgu_w_s, sgu_b_s = P["fox_b_f"], P["pool_w"], P["sgu_w_s"], P["sgu_b_s"]
    pool_scale_f, ln_g_f, ln_b_f = P["pool_scale"], P["sgu_ln_g"], P["sgu_ln_b"]
    w_in_e = jnp.pad(_cols_from_chips(P["w_in_even"]), ((0, 0), (0, EVEN_IN_PAD - EVEN_IN)))
    w_glu = P["s5_w_glu"].reshape(S5_WIDTH, S5_WIDTH)

    def gain(a, l):
        return a[l][None, :]

    lr = s5_lam_re[0].reshape(1, S5_LANES)
    li = s5_lam_im[0].reshape(1, S5_LANES)
    ldt = jnp.repeat(s5_log_dt[0], S5_STATE).reshape(1, S5_LANES)
    btr = s5_b_re[0].reshape(S5_LANES, S5_GROUP).T
    bti = s5_b_im[0].reshape(S5_LANES, S5_GROUP).T
    tf_re, tf_im, tb_re, tb_im, bbt_re, bbt_im = s5_disc_fwd(lr, li, ldt, btr, bti, name="s5_disc")
    same_group = (jnp.arange(S5_WIDTH)[:, None] // S5_GROUP) == (jnp.arange(S5_LANES)[None, :] // S5_STATE)
    b_bd = s5_interleave(jnp.where(same_group, jnp.tile(bbt_re, (S5_GROUPS, 1)), 0.0),
                         jnp.where(same_group, jnp.tile(bbt_im, (S5_GROUPS, 1)), 0.0), axis=1)
    cr2 = jnp.transpose(s5_c_re[0], (0, 2, 1)).reshape(S5_LANES, S5_GROUP)
    ci2 = jnp.transpose(s5_c_im[0], (0, 2, 1)).reshape(S5_LANES, S5_GROUP)
    c_bd = s5_interleave(jnp.where(same_group.T, jnp.tile(cr2, (1, S5_GROUPS)), 0.0),
                         -jnp.where(same_group.T, jnp.tile(ci2, (1, S5_GROUPS)), 0.0), axis=0)
    bf_pad = jnp.pad(fox_b_f, ((0, 0), (0, LANES - FOX_HEADS)))

    h1 = rms_fwd(x0, gain(mix_pre_g, 0), name="l0_pre_norm")
    z = matmul(h1, w_in_e, name="l0_in_proj")
    s5_tiles = dict(tm=_pick(T, S5_NB), exact_tiles=True)
    bu = matmul(z, b_bd, mnk=(T, 2 * S5_LANES, S5_CB), tn=S5_NB, a_spec=_chan_rows, b_spec=_s5_b_block,
                name="s5_bu", **s5_tiles)
    xs = s5_scan(bu, tf_re, tf_im, reverse=False, name="s5_scan_fwd")
    yc = matmul(xs, c_bd, mnk=(T, S5_WIDTH, 2 * S5_NB), tn=S5_CB, a_spec=_lanes_of_chan, b_spec=_s5_c_block,
                name="s5_cx", **s5_tiles)
    yl, yg = s5_out_fwd(yc, z, s5_d, name="s5_out")
    gl = matmul(yg, w_glu, name="s5_glu_proj")
    ycat = glu_fwd(yg, gl, out_cols=D_MODEL, name="s5_glu")
    fgate = fox_gate_fwd(z, bf_pad, fl_col=FL_TILE, name="fox_gate")
    f_col = _pairs_col(fgate, T)
    f_row = _col_to_row(f_col, T)
    (ycat, lse_col), late = fox_fwd(z, f_col, f_row, ycat, allgather_ici_exchange(late_shards), name="fox_fwd")
    late = dict(zip(LATE_NAMES, allgather_forward(late, name="allgather_late_weights")))
    w_in_o = _cols_from_chips(late["w_in_odd"])
    w_in_o = jnp.concatenate([w_in_o[:, S5_WIDTH:], w_in_o[:, :S5_WIDTH]], axis=1)
    w_out_e = late["w_out_even"].reshape(D_MODEL, D_MODEL)
    w_out_o = late["w_out_odd"].reshape(D_MODEL, D_MODEL)
    g1, g2 = late["mlp_w1"], late["mlp_w2"]
    mo = matmul(ycat, w_out_e, name="l0_out_proj")
    x1, h2 = res_norm_fwd(x0, mo, gain(mix_post_g, 0), gain(mlp_pre_g, 0), name="l0_post_mlp0_pre_norm")
    m0, mlp0 = _mlp_fwd(h2, g1, g2, 0, "mlp0")

    x2, h3 = res_norm_fwd(x1, m0, gain(mlp_post_g, 0), gain(mix_pre_g, 1), name="mlp0_post_l1_pre_norm")
    z2 = matmul(h3, w_in_o, name="l1_in_proj")
    pooled = pool_window(z2, adjoint=False, in_col=POOL_COL, name="pool_fwd")
    pw_bd = _block_diag(pool_w[0])
    pw = matmul(pooled, pw_bd, name="pool_proj")
    ycat2 = colscale_fwd(pw, pool_scale_f, out_cols=D_MODEL, name="pool_scale")
    causal = jnp.tril(jnp.ones((CHUNK, CHUNK), dtype=bool))
    wsm = jnp.where(causal[None], sgu_w_s[0], 0.0)
    wsmt = jnp.transpose(wsm, (0, 2, 1))
    bst = sgu_b_s[0].T
    ycat2 = sgu_fwd(z2, ln_g_f, ln_b_f, wsm, bst, ycat2, name="sgu_fwd")
    mo2 = matmul(ycat2, w_out_o, name="l1_out_proj")
    x3, h4 = res_norm_fwd(x2, mo2, gain(mix_post_g, 1), gain(mlp_pre_g, 1), name="l1_post_mlp1_pre_norm")
    m1, mlp1 = _mlp_fwd(h4, g1, g2, 1, "mlp1")
    loss8, dx4 = res_norm_loss(x3, m1, gain(mlp_post_g, 1), target, name="mlp1_post_norm_loss")

    dm1, dg_mlp_post1 = rms_bwd(m1, gain(mlp_post_g, 1), dx4, None, name="mlp1_post_norm_bwd")
    dh4, dg1, dg2 = _mlp_bwd(mlp1, dm1, g1, g2, 1, None, None, "mlp1")
    dx3, dmo2, dg_mlp_pre1, dg_mix_post1 = norm_res_bwd(x3, gain(mlp_pre_g, 1), dh4, dx4, mo2, gain(mix_post_g, 1),
                                                        name="mlp1_pre_l1_post_norm_bwd")
    dycat2 = matmul(dmo2, w_out_o, tb=True, name="l1_out_proj_dx")
    dw_out_o = matmul(ycat2, dmo2, ta=True, name="l1_out_proj_dw")
    dpw, dpool_scale = colscale_bwd(pw, pool_scale_f, dycat2, name="pool_scale_bwd")
    dpooled = matmul(dpw, pw_bd, tb=True, name="pool_proj_dx")
    dpw_bd = matmul(pooled, dpw, ta=True, name="pool_proj_dw")
    dz2, dln_g, dln_b, dws, dbst = sgu_bwd(z2, ln_g_f, ln_b_f, wsm, wsmt, bst, dycat2, out_cols=3 * S5_WIDTH,
                                           name="sgu_bwd")
    dz2 = pool_window(dpooled, adjoint=True, into=dz2, out_col=POOL_COL, name="pool_bwd")
    dh3 = matmul(dz2, w_in_o, tb=True, name="l1_in_proj_dx")
    dw_in_o = matmul(h3, dz2, ta=True, name="l1_in_proj_dw")
    dw_in_o = jnp.concatenate([dw_in_o[:, 2 * S5_WIDTH:], dw_in_o[:, :2 * S5_WIDTH]], axis=1)
    dx2, dm0, dg_mix_pre1, dg_mlp_post0 = norm_res_bwd(x2, gain(mix_pre_g, 1), dh3, dx3, m0, gain(mlp_post_g, 0),
                                                       name="l1_pre_mlp0_post_norm_bwd")

    dh2, dg1, dg2 = _mlp_bwd(mlp0, dm0, g1, g2, 0, dg1, dg2, "mlp0")
    dx1, dmo, dg_mlp_pre0, dg_mix_post0 = norm_res_bwd(x1, gain(mlp_pre_g, 0), dh2, dx2, mo, gain(mix_post_g, 0),
                                                       name="mlp0_pre_l0_post_norm_bwd")
    dycat = matmul(dmo, w_out_e, tb=True, name="l0_out_proj_dx")
    dw_out_e = matmul(ycat, dmo, ta=True, name="l0_out_proj_dw")
    dyg_a, dgl = glu_bwd(yg, gl, dycat, name="s5_glu_bwd")
    dyg_b = matmul(dgl, w_glu, tb=True, name="s5_glu_proj_dx")
    dw_glu = matmul(yg, dgl, ta=True, name="s5_glu_proj_dw")
    dyl, du_skip, dd = s5_out_bwd(yl, z, s5_d, dyg_a, dyg_b, name="s5_out_bwd")
    dxs = matmul(dyl, c_bd, tb=True, mnk=(T, 2 * S5_LANES, S5_CB), tn=S5_NB, a_spec=_chan_rows, b_spec=_s5_c_block_t,
                 name="s5_cx_dx", **s5_tiles)
    dc_blocks = matmul(xs, dyl, ta=True, mnk=(2 * S5_LANES, S5_CB, T), tm=S5_NB, tn=S5_CB, b_spec=_chan_cols_of_i,
                       exact_tiles=True, name="s5_cx_dw")
    early_grads = {"s5_w_glu": dw_glu.reshape(N_CHIPS, -1, S5_WIDTH), "w_out_even": dw_out_e.reshape(N_CHIPS, -1, D_MODEL),
                   "w_in_odd": _chips_from_cols(dw_in_o), "w_out_odd": dw_out_o.reshape(N_CHIPS, -1, D_MODEL),
                   "mlp_w1": dg1, "mlp_w2": dg2}
    got = {}

    def reverse_scan(exchange):
        (got["lam"], got["dab_re"], got["dab_im"]), bufs = s5_scan(dxs, tb_re, tb_im, reverse=True, states=xs,
                                                                   hosted=exchange, name="s5_scan_bwd")
        return bufs

    def attention_bwd(exchange):
        dd_col = fox_dd(ycat, dycat, name="fox_dd")
        (got["dk"], got["dv"], got["dfk"], got["dqt"], got["dfq"]), bufs = fox_bwd(
            z, dycat, f_col, f_row, _col_to_row(lse_col, T), _col_to_row(dd_col, T), exchange, name="fox_bwd")
        return bufs

    halves = _reduce_to_my_half([early_grads[n] for n in REDUCED_EARLY], REDUCED_EARLY, "early_grads",
                                reverse_scan, attention_bwd)
    lam, dab_re, dab_im, dk, dv = got["lam"], got["dab_re"], got["dab_im"], got["dk"], got["dv"]
    db_blocks = matmul(z, lam, ta=True, mnk=(S5_CB, 2 * S5_LANES, T), tm=S5_CB, tn=S5_NB, a_spec=_chan_rows_t,
                       exact_tiles=True, name="s5_bu_dw")
    du_b = matmul(lam, b_bd, tb=True, mnk=(T, S5_WIDTH, 2 * S5_NB), tn=S5_CB, a_spec=_lanes_of_chan,
                  b_spec=_s5_b_block_t, name="s5_bu_dx", **s5_tiles)
    du = add2(du_skip, du_b, name="s5_du")
    dq = jnp.transpose(got["dqt"], (1, 3, 0, 2)).reshape(T, FOX_WIDTH) * (FOX_HEAD_DIM ** -0.5)
    dfl, dbf = fox_gate_bwd(z, bf_pad, _pairs_to_lanes(got["dfk"], T), _pairs_to_lanes(_row_to_col(got["dfq"], T), T),
                            fl_col=FL_TILE, name="fox_gate_bwd")
    dz = jnp.concatenate([du, dq, dk, dv, dfl], axis=1)
    dh1 = matmul(dz, w_in_e, tb=True, name="l0_in_proj_dx")
    dw_in_e = matmul(h1, dz, ta=True, name="l0_in_proj_dw")[:, :EVEN_IN]
    dx0, dg_mix_pre0 = rms_bwd(x0, gain(mix_pre_g, 0), dh1, dx1, name="l0_pre_norm_bwd")

    groups_per_block = S5_CB // S5_GROUP
    own_group = (jnp.arange(S5_CB)[:, None] // S5_GROUP) == ((jnp.arange(S5_LANES)[None, :] // S5_STATE) % groups_per_block)
    db_re, db_im = s5_deinterleave(db_blocks, axis=1)
    dbbt_re = jnp.where(own_group, db_re, 0.0).reshape(groups_per_block, S5_GROUP, S5_LANES).sum(0)
    dbbt_im = jnp.where(own_group, db_im, 0.0).reshape(groups_per_block, S5_GROUP, S5_LANES).sum(0)
    dlr, dli, dldt8, dbtr, dbti = s5_disc_bwd(lr, li, ldt, btr, bti, dab_re, dab_im, dbbt_re, dbbt_im, name="s5_disc_bwd")
    dc_re, dc_im = s5_deinterleave(dc_blocks, axis=0)
    dcr2 = jnp.where(own_group.T, dc_re, 0.0).reshape(S5_LANES, groups_per_block, S5_GROUP).sum(1)
    dci2 = -jnp.where(own_group.T, dc_im, 0.0).reshape(S5_LANES, groups_per_block, S5_GROUP).sum(1)

    def c_layout(a):
        return jnp.transpose(a.reshape(S5_GROUPS, S5_STATE, S5_GROUP), (0, 2, 1))[None]

    def b_layout(a):
        return a.T.reshape(1, S5_GROUPS, S5_STATE, S5_GROUP)

    local_small = {
        "mix_pre_g": jnp.concatenate([dg_mix_pre0, dg_mix_pre1]), "mix_post_g": jnp.concatenate([dg_mix_post0, dg_mix_post1]),
        "mlp_pre_g": jnp.concatenate([dg_mlp_pre0, dg_mlp_pre1]), "mlp_post_g": jnp.concatenate([dg_mlp_post0, dg_mlp_post1]),
        "s5_lam_re": dlr.reshape(1, S5_GROUPS, S5_STATE), "s5_lam_im": dli.reshape(1, S5_GROUPS, S5_STATE),
        "s5_log_dt": dldt8[0:1, 0:S5_GROUPS],
        "s5_b_re": b_layout(dbtr), "s5_b_im": b_layout(dbti), "s5_c_re": c_layout(dcr2), "s5_c_im": c_layout(dci2),
        "s5_d": dd, "fox_b_f": dbf[:, 0:FOX_HEADS],
        "pool_w": _diag_blocks(dpw_bd, len(POOL_WINDOWS))[None],
        "sgu_w_s": jnp.where(causal[None], dws, 0.0)[None], "sgu_b_s": dbst.T[None],
        "pool_scale": dpool_scale, "sgu_ln_g": dln_g, "sgu_ln_b": dln_b,
    }
    return loss8, dx0, dict(zip(REDUCED_EARLY, halves)), _chips_from_cols(dw_in_e), local_small


def _reduce_and_update(W, M, V, loss, dx0, halves, dw_in_e, local_small):
    cx, cy, cc = _coords()
    chip = 2 * cx + cy

    vec = _pack_vec(local_small, REDUCED_SEGS, N_DEV * SUBLANES)
    piece = vec.shape[0] // N_DEV
    landed = exchange_pieces(vec.reshape(N_DEV, piece, LANES), scatter=True, name="small_grads_scatter")
    mine = sum_pieces(landed, name="small_grads_sum")
    everyone = exchange_pieces(mine, scatter=False, name="small_grads_gather")
    G = _unpack_vec(everyone, REDUCED_SEGS)
    for n in SHARDED_SMALL:
        G[n] = lax.dynamic_slice_in_dim(G[n], chip * LANES, LANES, axis=1)

    halves = dict(halves)
    halves["w_in_even"] = _reduce_to_my_half([dw_in_e], ["w_in_even"], "late_grads")[0]
    reduced = join_sibling_halves([halves[n] for n in BIG_NAMES], name="big_grads_join")
    for n, r in zip(BIG_NAMES, reduced):
        G[n] = r.reshape(W[n].shape)

    def two_d(a):
        return a.reshape(-1, a.shape[-1])

    delta, new_m, new_v = {}, {}, {}
    for n in BIG_NAMES:
        d_, m_, v_ = adamw(two_d(W[n]), two_d(G[n]), two_d(M[n]), two_d(V[n]), name=f"adamw_{n}")
        delta[n], new_m[n], new_v[n] = (t.reshape(W[n].shape) for t in (d_, m_, v_))
    packed = [_pack_vec(src, SMALL_SEGS, SUBLANES) for src in (W, G, M, V)]
    outs = adamw(*packed, name="adamw_replicated")
    for dst, t in zip((delta, new_m, new_v), outs):
        dst.update(_unpack_vec(t, SMALL_SEGS))
    sharded_segs = tuple((n, (1, LANES)) for n in SHARDED_SMALL)
    packed = [_pack_vec(src, sharded_segs, 1) for src in (W, G, M, V)]
    outs = adamw(*packed, name="adamw_sharded_vectors")
    for dst, t in zip((delta, new_m, new_v), outs):
        dst.update(_unpack_vec(t, sharded_segs))

    order = ["mix_pre_g", "mix_post_g", "mlp_pre_g", "mlp_post_g", "w_in_even", "s5_lam_re", "s5_lam_im", "s5_log_dt",
             "s5_b_re", "s5_b_im", "s5_c_re", "s5_c_im", "s5_d", "s5_w_glu", "fox_b_f", "w_out_even", "w_in_odd",
             "pool_w", "pool_scale", "sgu_ln_g", "sgu_ln_b", "sgu_w_s", "sgu_b_s", "w_out_odd", "mlp_w1", "mlp_w2"]
    return (loss, dx0[None], *[G[n] for n in order], *[delta[n] for n in order],
            *[new_m[n] for n in order], *[new_v[n] for n in order])
```

```python
import functools
import math

import jax
import jax.numpy as jnp
from jax import lax
from jax.experimental import pallas as pl
from jax.experimental.pallas import tpu as pltpu

F32 = jnp.float32
MXU_DTYPE = jnp.bfloat16
WIRE_DTYPE = jnp.bfloat16
EPS = 1e-6
VMEM_LIMIT_BYTES = 48 * 1024 * 1024
LANES = 128
SUBLANES = 8

D_MODEL = 1024
S5_WIDTH = 512
S5_GROUP = 16
S5_GROUPS = 32
S5_STATE = 64
S5_LANES = S5_GROUPS * S5_STATE
FOX_HEADS = 8
FOX_HEAD_DIM = 64
FOX_WIDTH = 512
EVEN_IN = S5_WIDTH + 3 * FOX_WIDTH + FOX_HEADS
EVEN_IN_PAD = 2176
POOL_WINDOWS = (2, 4, 8, 16)
POOL_HALO = 16
POOL_GROUP_DIM = 128
SGU_GROUPS = 4
SGU_GROUP_DIM = 128
CHUNK = 128
D_FF = 4096

ADAM_LR = 0.001
ADAM_B1 = 0.9
ADAM_B2 = 0.999
ADAM_EPS = 1e-08
ADAM_WD = 0.01
ADAM_STEP = 10

MESH_AXES = ("x", "y", "c")
MESH = pl.DeviceIdType.MESH
N_CHIPS = 4
N_DEV = 8

SDS = jax.ShapeDtypeStruct


def _cp(*sem):
    return pltpu.CompilerParams(dimension_semantics=sem, vmem_limit_bytes=VMEM_LIMIT_BYTES)


def _pick(dim, pref):
    if dim <= pref:
        return dim
    t = pref
    while t >= 256:
        if dim % t == 0:
            return t
        t //= 2
    return dim


def _row(tr, c):
    return pl.BlockSpec((tr, c), lambda i: (i, 0))


def _full(shape):
    nd = len(shape)
    return pl.BlockSpec(shape, lambda *_: (0,) * nd)


def _gelu_grad(x):
    c = math.sqrt(2.0 / math.pi)
    t = jnp.tanh(c * (x + 0.044715 * x * x * x))
    return 0.5 * (1.0 + t) + 0.5 * x * (1.0 - t * t) * c * (1.0 + 3.0 * 0.044715 * x * x)


MATMUL_VMEM_BYTES = 36 * 1024 * 1024


def matmul(a, b, *, name, ta=False, tb=False, out_dtype=F32, tm=2048, tn=1024, tk=4096, mnk=None, a_koff=0,
           a_spec=None, b_spec=None, o_spec=None, o_shape=None, prev=None, epi=None, epi_in=(), out_dtypes=None,
           exact_tiles=False, hosted=None):
    if mnk is None:
        M, K = (a.shape[1], a.shape[0]) if ta else a.shape
        K2, N = (b.shape[1], b.shape[0]) if tb else b.shape
        assert K == K2, (a.shape, b.shape, ta, tb)
    else:
        M, N, K = mnk
    out_dtypes = tuple(out_dtypes) if out_dtypes is not None else (out_dtype,)
    n_out, n_epi = len(out_dtypes), len(epi_in)
    tm, tn, tk = _pick(M, tm), _pick(N, tn), _pick(K, tk)

    def vmem_bytes(tm_, tn_, tk_):
        tiles = tm_ * tk_ * a.dtype.itemsize + tk_ * tn_ * b.dtype.itemsize
        tiles += tm_ * tn_ * (sum(jnp.dtype(d).itemsize for d in out_dtypes) + sum(e.dtype.itemsize for e in epi_in))
        return 2 * tiles + tm_ * tn_ * 4 * (tk_ < K)

    def halves(t, dim):
        return [t] + ([t // 2] if t % (2 * LANES) == 0 and t // 2 >= 512 and dim % (t // 2) == 0 else [])

    if exact_tiles:
        halves = lambda t, dim: [t]
    fits = [(m_, n_) for m_ in halves(tm, M) for n_ in halves(tn, N) if vmem_bytes(m_, n_, tk) <= MATMUL_VMEM_BYTES]
    if fits:
        tm, tn = max(fits, key=lambda t: (t[0] * t[1], t[0]))
    else:
        tm, tn = halves(tm, M)[-1], halves(tn, N)[-1]
        while vmem_bytes(tm, tn, tk) > MATMUL_VMEM_BYTES and tk % 2 == 0 and tk > 512:
            tk //= 2
    nk = K // tk
    assert a_koff % tk == 0 and not (ta and a_koff)
    ko = a_koff // tk
    dn = (((0 if ta else 1,), (1 if tb else 0,)), ((), ()))

    def body(*refs):
        a_ref, b_ref = refs[0], refs[1]
        epi_refs = refs[2:2 + n_epi]
        o_refs = refs[len(refs) - n_out - (nk > 1):len(refs) - (nk > 1)]
        k = pl.program_id(2)
        bv = b_ref[...]
        if bv.ndim == 3 and tb:
            cw = bv.shape[-1]
            prod = sum(lax.dot_general(a_ref[:, c * cw:(c + 1) * cw].astype(MXU_DTYPE), bv[c].astype(MXU_DTYPE), dn,
                                       preferred_element_type=F32) for c in range(bv.shape[0]))
        else:
            if bv.ndim == 3:
                bv = bv.reshape(-1, bv.shape[-1])
            prod = lax.dot_general(a_ref[...].astype(MXU_DTYPE), bv.astype(MXU_DTYPE), dn, preferred_element_type=F32)

        def finish(acc):
            res = (acc,) if epi is None else epi(acc, *[r[...] for r in epi_refs])
            for o_ref, r in zip(o_refs, res):
                o_ref[...] = r.astype(o_ref.dtype)

        if nk == 1:
            finish(prod)
            return
        acc_ref = refs[-1]

        @pl.when(k == 0)
        def _():
            acc_ref[...] = prod

        @pl.when(jnp.logical_and(k > 0, k < nk - 1))
        def _():
            acc_ref[...] += prod

        @pl.when(k == nk - 1)
        def _():
            finish(acc_ref[...] + prod)

    if a_spec is None:
        a_spec = pl.BlockSpec((tk, tm), lambda i, j, k: (k, i)) if ta else pl.BlockSpec((tm, tk), lambda i, j, k: (i, k + ko))
    else:
        a_spec = a_spec(tm, tn, tk)
    if b_spec is None:
        bs = pl.BlockSpec((tn, tk), lambda i, j, k: (j, k)) if tb else pl.BlockSpec((tk, tn), lambda i, j, k: (k, j))
    else:
        bs = b_spec(tm, tn, tk)
    tile = pl.BlockSpec((tm, tn), lambda i, j, k: (i, j))
    os_ = tile if o_spec is None else o_spec(tm, tn, tk)
    ins, in_specs, aliases = [a, b, *epi_in], [a_spec, bs] + [tile] * n_epi, {}
    if prev is not None:
        aliases = {len(ins): 0}
        ins.append(prev)
        in_specs.append(pl.BlockSpec(memory_space=pl.ANY))
    shapes = [SDS((M, N) if o_shape is None else o_shape, dt) for dt in out_dtypes]
    scratch = [pltpu.VMEM((tm, tn), F32)] if nk > 1 else []
    if hosted is not None:
        outs, bufs = call_hosting(body, hosted, name=name, grid=(M // tm, N // tn, nk), in_specs=in_specs,
                                  out_specs=[os_] * n_out, out_shape=shapes, inputs=ins, aliases=aliases, scratch=scratch)
        return (outs[0] if n_out == 1 else outs), bufs
    outs = pl.pallas_call(
        body, name=name, grid=(M // tm, N // tn, nk),
        in_specs=in_specs, out_specs=[os_] * n_out, out_shape=shapes, input_output_aliases=aliases,
        scratch_shapes=scratch, compiler_params=_cp("parallel", "parallel", "arbitrary"),
    )(*ins)
    return outs[0] if n_out == 1 else outs


def _rms_hat(x):
    return x * lax.rsqrt(jnp.mean(x * x, axis=-1, keepdims=True) + EPS)


def rms_fwd(x, g, *, name):
    T, D = x.shape
    tr = _pick(T, 512)

    def body(x_ref, g_ref, o_ref):
        o_ref[...] = (_rms_hat(x_ref[...]) * g_ref[...]).astype(o_ref.dtype)

    return pl.pallas_call(body, name=name, grid=(T // tr,), in_specs=[_row(tr, D), _full((1, D))],
                          out_specs=_row(tr, D), out_shape=SDS((T, D), MXU_DTYPE), compiler_params=_cp("parallel"))(x, g)


def res_norm_fwd(x, y, g_post, g_next, *, name):
    T, D = x.shape
    tr = _pick(T, 512)

    def body(x_ref, y_ref, gp_ref, gn_ref, o_ref, h_ref):
        xn = x_ref[...] + _rms_hat(y_ref[...]) * gp_ref[...]
        o_ref[...] = xn
        h_ref[...] = (_rms_hat(xn) * gn_ref[...]).astype(h_ref.dtype)

    return pl.pallas_call(body, name=name, grid=(T // tr,),
                          in_specs=[_row(tr, D), _row(tr, D), _full((1, D)), _full((1, D))],
                          out_specs=[_row(tr, D), _row(tr, D)], out_shape=[SDS((T, D), F32), SDS((T, D), MXU_DTYPE)],
                          compiler_params=_cp("parallel"))(x, y, g_post, g_next)


def res_norm_loss(x, y, g_post, target, *, name):
    T, D = x.shape
    tr = _pick(T, 512)

    def body(x_ref, y_ref, g_ref, t_ref, l_ref, d_ref):
        err = x_ref[...] + _rms_hat(y_ref[...]) * g_ref[...] - t_ref[...]
        d_ref[...] = err * (1.0 / D)

        @pl.when(pl.program_id(0) == 0)
        def _():
            l_ref[...] = jnp.zeros_like(l_ref)

        l_ref[...] += 0.5 * jnp.sum(jnp.mean(err * err, axis=-1, keepdims=True))

    return pl.pallas_call(body, name=name, grid=(T // tr,),
                          in_specs=[_row(tr, D), _row(tr, D), _full((1, D)), _row(tr, D)],
                          out_specs=[_full((SUBLANES, LANES)), _row(tr, D)],
                          out_shape=[SDS((SUBLANES, LANES), F32), SDS((T, D), F32)],
                          compiler_params=_cp("arbitrary"))(x, y, g_post, target)


def _rms_bwd_rows(x, g, dy):
    r = lax.rsqrt(jnp.mean(x * x, axis=-1, keepdims=True) + EPS)
    xh = x * r
    dxh = dy * g
    return r * (dxh - xh * jnp.mean(dxh * xh, axis=-1, keepdims=True)), jnp.sum(dy * xh, axis=0, keepdims=True)


def norm_res_bwd(x, g_pre, dh, res, y, g_post, *, name):
    T, D = x.shape
    tr = _pick(T, 512)

    def body(x_ref, gp_ref, dh_ref, res_ref, y_ref, gy_ref, dx_ref, dy_ref, dgp_ref, dgy_ref):
        dx, dgp = _rms_bwd_rows(x_ref[...], gp_ref[...], dh_ref[...])
        dx = dx + res_ref[...]
        dx_ref[...] = dx
        dy, dgy = _rms_bwd_rows(y_ref[...], gy_ref[...], dx)
        dy_ref[...] = dy.astype(dy_ref.dtype)

        @pl.when(pl.program_id(0) == 0)
        def _():
            dgp_ref[...] = jnp.zeros_like(dgp_ref)
            dgy_ref[...] = jnp.zeros_like(dgy_ref)

        dgp_ref[...] += dgp
        dgy_ref[...] += dgy

    row, vec = _row(tr, D), _full((1, D))
    return pl.pallas_call(body, name=name, grid=(T // tr,), in_specs=[row, vec, row, row, row, vec],
                          out_specs=[row, row, vec, vec],
                          out_shape=[SDS((T, D), F32), SDS((T, D), MXU_DTYPE), SDS((1, D), F32), SDS((1, D), F32)],
                          compiler_params=_cp("arbitrary"))(x, g_pre, dh, res, y, g_post)


def rms_bwd(x, g, dy, res, *, name, hosted=None):
    T, D = x.shape
    tr = _pick(T, 512)
    has_res = res is not None

    def body(*refs):
        if has_res:
            x_ref, g_ref, dy_ref, res_ref, dx_ref, dg_ref = refs
        else:
            x_ref, g_ref, dy_ref, dx_ref, dg_ref = refs
        dx, dg = _rms_bwd_rows(x_ref[...], g_ref[...], dy_ref[...])
        if has_res:
            dx = dx + res_ref[...]
        dx_ref[...] = dx.astype(dx_ref.dtype)

        @pl.when(pl.program_id(0) == 0)
        def _():
            dg_ref[...] = jnp.zeros_like(dg_ref)

        dg_ref[...] += dg

    ins = [x, g, dy] + ([res] if has_res else [])
    in_specs = [_row(tr, D), _full((1, D)), _row(tr, D)] + ([_row(tr, D)] if has_res else [])
    out_shape = [SDS((T, D), F32 if has_res else MXU_DTYPE), SDS((1, D), F32)]
    out_specs = [_row(tr, D), _full((1, D))]
    if hosted is not None:
        return call_hosting(body, hosted, name=name, grid=(T // tr,), in_specs=in_specs, out_specs=out_specs,
                            out_shape=out_shape, inputs=ins, aliases={})
    return pl.pallas_call(body, name=name, grid=(T // tr,), in_specs=in_specs, out_specs=out_specs,
                          out_shape=out_shape, compiler_params=_cp("arbitrary"))(*ins)


def _s5_disc(lr, li, ldt, btr, bti):
    dt = jnp.exp(ldt)
    k = lax.broadcasted_iota(jnp.int32, (SUBLANES, S5_LANES), 0).astype(F32)
    kf = k + 1.0
    kb = 8.0 - k
    ph = li * dt
    lm = lr * dt
    tf_re = jnp.exp(kf * lm) * jnp.cos(kf * ph)
    tf_im = jnp.exp(kf * lm) * jnp.sin(kf * ph)
    tb_re = jnp.exp(kb * lm) * jnp.cos(kb * ph)
    tb_im = -jnp.exp(kb * lm) * jnp.sin(kb * ph)
    mag = jnp.exp(lm)
    ab_re = mag * jnp.cos(ph)
    ab_im = mag * jnp.sin(ph)
    den = lr * lr + li * li
    nr = ab_re - 1.0
    ni = ab_im
    q_re = (nr * lr + ni * li) / den
    q_im = (ni * lr - nr * li) / den
    bbt_re = q_re * btr - q_im * bti
    bbt_im = q_re * bti + q_im * btr
    return tf_re, tf_im, tb_re, tb_im, bbt_re, bbt_im


def _s5_disc_core(lr, li, ldt, btr, bti):
    dt = jnp.exp(ldt)
    mag = jnp.exp(lr * dt)
    ab_re = mag * jnp.cos(li * dt)
    ab_im = mag * jnp.sin(li * dt)
    den = lr * lr + li * li
    nr = ab_re - 1.0
    ni = ab_im
    q_re = (nr * lr + ni * li) / den
    q_im = (ni * lr - nr * li) / den
    return ab_re, ab_im, q_re * btr - q_im * bti, q_re * bti + q_im * btr


def s5_disc_fwd(lr, li, ldt, btr, bti, *, name):
    def body(lr_ref, li_ref, ldt_ref, btr_ref, bti_ref, *outs):
        vals = _s5_disc(lr_ref[...], li_ref[...], ldt_ref[...], btr_ref[...], bti_ref[...])
        for o, v in zip(outs, vals):
            o[...] = v

    tab = SDS((SUBLANES, S5_LANES), F32)
    bb = SDS((S5_GROUP, S5_LANES), F32)
    return pl.pallas_call(body, name=name, out_shape=[tab, tab, tab, tab, bb, bb])(lr, li, ldt, btr, bti)


def s5_disc_bwd(lr, li, ldt, btr, bti, dab_re, dab_im, dbbt_re, dbbt_im, *, name):
    def body(lr_ref, li_ref, ldt_ref, btr_ref, bti_ref, dar_ref, dai_ref, dbr_ref, dbi_ref,
             dlr_ref, dli_ref, dldt_ref, dbtr_ref, dbti_ref):
        _, vjp = jax.vjp(_s5_disc_core, lr_ref[...], li_ref[...], ldt_ref[...], btr_ref[...], bti_ref[...])
        dlr, dli, dldt, dbtr, dbti = vjp((dar_ref[...], dai_ref[...], dbr_ref[...], dbi_ref[...]))
        dlr_ref[...] = dlr
        dli_ref[...] = dli
        dbtr_ref[...] = dbtr
        dbti_ref[...] = dbti
        lane_group = lax.broadcasted_iota(jnp.int32, (S5_LANES, LANES), 0) // S5_STATE
        col = lax.broadcasted_iota(jnp.int32, (S5_LANES, LANES), 1)
        ind = (lane_group == col).astype(F32)
        dldt_ref[...] = jnp.dot(jnp.broadcast_to(dldt, (SUBLANES, S5_LANES)), ind,
                                precision=lax.Precision.HIGHEST, preferred_element_type=F32)

    row = SDS((1, S5_LANES), F32)
    bb = SDS((S5_GROUP, S5_LANES), F32)
    return pl.pallas_call(body, name=name, out_shape=[row, row, SDS((SUBLANES, LANES), F32), bb, bb])(
        lr, li, ldt, btr, bti, dab_re, dab_im, dbbt_re, dbbt_im)


S5_NB = 1024


S5_CB = S5_WIDTH * S5_NB // S5_LANES


def _chan_rows_t(tm, tn, tk):
    return pl.BlockSpec((tk, S5_CB), lambda i, j, k: (k, j // 2))


def _chan_cols_of_i(tm, tn, tk):
    return pl.BlockSpec((tk, S5_CB), lambda i, j, k: (k, i // 2))


def _lanes_of_chan(tm, tn, tk):
    return pl.BlockSpec((tm, 2 * S5_NB), lambda i, j, k: (i, j))


def _s5_b_block_t(tm, tn, tk):
    return pl.BlockSpec((S5_CB, 2 * S5_NB), lambda i, j, k: (j, j))


def _s5_c_block(tm, tn, tk):
    return pl.BlockSpec((2 * S5_NB, S5_CB), lambda i, j, k: (j, j))


def s5_interleave(re, im, axis):
    parts = []
    for n in range(S5_LANES // S5_NB):
        sl = [slice(None)] * re.ndim
        sl[axis] = slice(n * S5_NB, (n + 1) * S5_NB)
        parts += [re[tuple(sl)], im[tuple(sl)]]
    return jnp.concatenate(parts, axis=axis)


def s5_deinterleave(a, axis):
    re, im = [], []
    for n in range(S5_LANES // S5_NB):
        sl = [slice(None)] * a.ndim
        sl[axis] = slice(2 * n * S5_NB, (2 * n + 1) * S5_NB)
        re.append(a[tuple(sl)])
        sl[axis] = slice((2 * n + 1) * S5_NB, (2 * n + 2) * S5_NB)
        im.append(a[tuple(sl)])
    return jnp.concatenate(re, axis=axis), jnp.concatenate(im, axis=axis)


def s5_scan(src, mat, tab_re, tab_im, *, reverse, name, states=None, hosted=None):
    T = src.shape[0]
    nb = S5_NB
    tc = _pick(T, 256)
    nl = S5_LANES // nb
    nt = T // tc
    ntile = tc // SUBLANES
    with_da = states is not None
    assert reverse or not with_da
    step_rows = ((1, 7), (2, 6), (4, 4)) if reverse else ((1, 0), (2, 1), (4, 3))
    drive_dn = _NT if reverse else (((1,), (0,)), ((), ()))

    def body(*refs):
        if with_da:
            (src_ref, wr_ref, wi_ref, tr_ref, ti_ref, sr_ref, si_ref, hr_ref, hi_ref, xo_ref, dar_ref, dai_ref,
             cr_ref, ci_ref, mr_ref, mi_ref, br_ref, bi_ref, ar_ref, ai_ref) = refs
        else:
            src_ref, wr_ref, wi_ref, tr_ref, ti_ref, xo_ref, cr_ref, ci_ref, mr_ref, mi_ref, br_ref, bi_ref = refs

        @pl.when(pl.program_id(1) == 0)
        def _():
            cr_ref[...] = jnp.zeros_like(cr_ref)
            ci_ref[...] = jnp.zeros_like(ci_ref)
            if with_da:
                ar_ref[...] = jnp.zeros_like(ar_ref)
                ai_ref[...] = jnp.zeros_like(ai_ref)

        lhs = src_ref[...].astype(MXU_DTYPE)
        br_ref[...] = lax.dot_general(lhs, wr_ref[...].astype(MXU_DTYPE), drive_dn, preferred_element_type=F32)
        bi_ref[...] = lax.dot_general(lhs, wi_ref[...].astype(MXU_DTYPE), drive_dn, preferred_element_type=F32)

        seen = jnp.where(pl.program_id(1) < nt - 1, 1.0, 0.0)

        def add_da(lr, li, r0, last_r, last_i):
            first = lax.broadcasted_iota(jnp.int32, (SUBLANES, nb), 0) == 0
            pr = jnp.where(first, last_r, pltpu.roll(sr_ref[pl.ds(r0, SUBLANES), :], 1, 0))
            pi = jnp.where(first, last_i, pltpu.roll(si_ref[pl.ds(r0, SUBLANES), :], 1, 0))
            ar_ref[...] += lr * pr + li * pi
            ai_ref[...] += li * pr - lr * pi

        io = lax.broadcasted_iota(jnp.int32, (SUBLANES, nb), 0)
        for s_, (d, r) in enumerate(step_rows):
            keep = (io < SUBLANES - d) if reverse else (io >= d)
            mr_ref[s_] = jnp.where(keep, tr_ref[r:r + 1, :], 0.0)
            mi_ref[s_] = jnp.where(keep, ti_ref[r:r + 1, :], 0.0)

        def tile(i, carry):
            cr, ci = carry
            j = (ntile - 1 - i) if reverse else i
            r0 = pl.multiple_of(j * SUBLANES, SUBLANES)
            xr = br_ref[pl.ds(r0, SUBLANES), :]
            xi = bi_ref[pl.ds(r0, SUBLANES), :]
            for s_, (d, _) in enumerate(step_rows):
                sh = (SUBLANES - d) if reverse else d
                sr = pltpu.roll(xr, sh, 0)
                si = pltpu.roll(xi, sh, 0)
                pr, pi = mr_ref[s_], mi_ref[s_]
                xr, xi = xr + pr * sr - pi * si, xi + pr * si + pi * sr
            tr, ti = tr_ref[...], ti_ref[...]
            xr, xi = xr + tr * cr - ti * ci, xi + tr * ci + ti * cr
            xo_ref[pl.ds(r0, SUBLANES), 0:nb] = xr
            xo_ref[pl.ds(r0, SUBLANES), nb:2 * nb] = xi
            if with_da:
                @pl.when(j > 0)
                def _():
                    p0 = pl.multiple_of(r0 - SUBLANES, SUBLANES)
                    add_da(xr, xi, r0, sr_ref[pl.ds(p0, SUBLANES), :][SUBLANES - 1:SUBLANES, :],
                           si_ref[pl.ds(p0, SUBLANES), :][SUBLANES - 1:SUBLANES, :])

                @pl.when(j == 0)
                def _():
                    add_da(xr, xi, r0, hr_ref[SUBLANES - 1:SUBLANES, :] * seen, hi_ref[SUBLANES - 1:SUBLANES, :] * seen)
            if reverse:
                return xr[0:1, :], xi[0:1, :]
            return xr[SUBLANES - 1:SUBLANES, :], xi[SUBLANES - 1:SUBLANES, :]

        cr, ci = lax.fori_loop(0, ntile, tile, (cr_ref[0:1, :], ci_ref[0:1, :]))
        cr_ref[0:1, :] = cr
        ci_ref[0:1, :] = ci
        if with_da:
            @pl.when(pl.program_id(1) == nt - 1)
            def _():
                dar_ref[...] = jnp.sum(ar_ref[...], axis=0, keepdims=True)
                dai_ref[...] = jnp.sum(ai_ref[...], axis=0, keepdims=True)

    def tmap(t):
        return (nt - 1 - t) if reverse else t

    hb = tc // SUBLANES
    re_spec = pl.BlockSpec((tc, nb), lambda n, t: (tmap(t), 2 * n))
    im_spec = pl.BlockSpec((tc, nb), lambda n, t: (tmap(t), 2 * n + 1))
    tab_spec = pl.BlockSpec((SUBLANES, nb), lambda n, t: (0, n))
    out_spec = pl.BlockSpec((tc, 2 * nb), lambda n, t: (tmap(t), n))
    out_shape = SDS((T, 2 * S5_LANES), F32)
    scratch = [pltpu.VMEM((SUBLANES, nb), F32), pltpu.VMEM((SUBLANES, nb), F32),
               pltpu.VMEM((len(step_rows), SUBLANES, nb), F32), pltpu.VMEM((len(step_rows), SUBLANES, nb), F32),
               pltpu.VMEM((tc, nb), F32), pltpu.VMEM((tc, nb), F32)]
    src_spec = pl.BlockSpec((tc, S5_CB), lambda n, t: (tmap(t), n))
    if reverse:
        wr_spec = pl.BlockSpec((nb, S5_CB), lambda n, t: (2 * n, n))
        wi_spec = pl.BlockSpec((nb, S5_CB), lambda n, t: (2 * n + 1, n))
    else:
        wr_spec = pl.BlockSpec((S5_CB, nb), lambda n, t: (n, 2 * n))
        wi_spec = pl.BlockSpec((S5_CB, nb), lambda n, t: (n, 2 * n + 1))
    drive_specs = [src_spec, wr_spec, wi_spec, tab_spec, tab_spec]
    drive = [src, mat, mat, tab_re, tab_im]
    if not with_da:
        return pl.pallas_call(body, name=name, grid=(nl, nt), in_specs=drive_specs,
                              out_specs=out_spec, out_shape=out_shape, scratch_shapes=scratch,
                              compiler_params=_cp("parallel", "arbitrary"))(*drive)
    re_halo = pl.BlockSpec((SUBLANES, nb), lambda n, t: (jnp.maximum(tmap(t) * hb - 1, 0), 2 * n))
    im_halo = pl.BlockSpec((SUBLANES, nb), lambda n, t: (jnp.maximum(tmap(t) * hb - 1, 0), 2 * n + 1))
    acc = pl.BlockSpec((1, nb), lambda n, t: (0, n))
    row = SDS((1, S5_LANES), F32)
    return call_hosting(
        body, hosted, name=name, grid=(nl, nt),
        in_specs=drive_specs + [re_spec, im_spec, re_halo, im_halo],
        out_specs=[out_spec, acc, acc], out_shape=[out_shape, row, row],
        inputs=drive + [states, states, states, states], aliases={},
        scratch=scratch + [pltpu.VMEM((SUBLANES, nb), F32), pltpu.VMEM((SUBLANES, nb), F32)])


def s5_out_fwd(yc, u, d, *, name):
    T, C = yc.shape
    tr = _pick(T, 512)

    def body(yc_ref, u_ref, d_ref, yl_ref, yg_ref):
        yl = yc_ref[...] + d_ref[...] * u_ref[...]
        yl_ref[...] = yl
        yg_ref[...] = jax.nn.gelu(yl)

    return pl.pallas_call(body, name=name, grid=(T // tr,), in_specs=[_row(tr, C), _row(tr, C), _full((1, C))],
                          out_specs=[_row(tr, C)] * 2, out_shape=[SDS((T, C), F32)] * 2,
                          compiler_params=_cp("parallel"))(yc, u, d)


def glu_fwd(yg, gl, *, out_cols, name):
    T, C = yg.shape
    tr = _pick(T, 512)

    def body(yg_ref, gl_ref, o_ref):
        o_ref[...] = yg_ref[...] * jax.nn.sigmoid(gl_ref[...])

    return pl.pallas_call(body, name=name, grid=(T // tr,), in_specs=[_row(tr, C)] * 2, out_specs=_row(tr, C),
                          out_shape=SDS((T, out_cols), F32), compiler_params=_cp("parallel"))(yg, gl)


def glu_bwd(yg, gl, dy, *, name):
    T, C = yg.shape
    tr = _pick(T, 512)

    def body(yg_ref, gl_ref, dy_ref, dyg_ref, dgl_ref):
        s = jax.nn.sigmoid(gl_ref[...])
        dyv = dy_ref[...]
        dyg_ref[...] = dyv * s
        dgl_ref[...] = dyv * yg_ref[...] * s * (1.0 - s)

    return pl.pallas_call(body, name=name, grid=(T // tr,), in_specs=[_row(tr, C)] * 3, out_specs=[_row(tr, C)] * 2,
                          out_shape=[SDS((T, C), F32)] * 2, compiler_params=_cp("parallel"))(yg, gl, dy)


def s5_out_bwd(yl, u, d, dyg_a, dyg_b, *, name):
    T, C = yl.shape
    tr = _pick(T, 512)

    def body(yl_ref, u_ref, d_ref, da_ref, db_ref, dyl_ref, du_ref, dd_ref):
        dyl = (da_ref[...] + db_ref[...]) * _gelu_grad(yl_ref[...])
        dyl_ref[...] = dyl
        du_ref[...] = dyl * d_ref[...]

        @pl.when(pl.program_id(0) == 0)
        def _():
            dd_ref[...] = jnp.zeros_like(dd_ref)

        dd_ref[...] += jnp.sum(dyl * u_ref[...], axis=0, keepdims=True)

    return pl.pallas_call(body, name=name, grid=(T // tr,),
                          in_specs=[_row(tr, C), _row(tr, C), _full((1, C)), _row(tr, C), _row(tr, C)],
                          out_specs=[_row(tr, C), _row(tr, C), _full((1, C))],
                          out_shape=[SDS((T, C), F32), SDS((T, C), F32), SDS((1, C), F32)],
                          compiler_params=_cp("arbitrary"))(yl, u, d, dyg_a, dyg_b)


def add2(a, b, *, name):
    T, C = a.shape
    tr = _pick(T, 512)

    def body(a_ref, b_ref, o_ref):
        o_ref[...] = a_ref[...] + b_ref[...]

    return pl.pallas_call(body, name=name, grid=(T // tr,), in_specs=[_row(tr, C)] * 2, out_specs=_row(tr, C),
                          out_shape=SDS((T, C), F32), compiler_params=_cp("parallel"))(a, b)


def _tri(n, upper):
    r = lax.broadcasted_iota(jnp.int32, (n, n), 0)
    c = lax.broadcasted_iota(jnp.int32, (n, n), 1)
    return ((c >= r) if upper else (c <= r)).astype(F32)


def fox_gate_fwd(fl, bf, *, fl_col, name):
    T = fl.shape[0]
    tb = _pick(T, 256)

    def body(fl_ref, bf_ref, f_ref, c_ref):
        @pl.when(pl.program_id(0) == 0)
        def _():
            c_ref[...] = jnp.zeros_like(c_ref)

        lf = jax.nn.log_sigmoid(fl_ref[...] + bf_ref[...])
        f = jnp.dot(_tri(tb, False), lf, precision=lax.Precision.HIGHEST, preferred_element_type=F32) + c_ref[0:1, :]
        f_ref[...] = f * LOG2E
        c_ref[0:1, :] = f[tb - 1:tb, :]

    fl_spec = pl.BlockSpec((tb, LANES), lambda i: (i, fl_col))
    return pl.pallas_call(body, name=name, grid=(T // tb,), in_specs=[fl_spec, _full((1, LANES))],
                          out_specs=_row(tb, LANES), out_shape=SDS((T, LANES), F32),
                          scratch_shapes=[pltpu.VMEM((SUBLANES, LANES), F32)], compiler_params=_cp("arbitrary"))(fl, bf)


def fox_gate_bwd(fl, bf, df_keys, df_queries, *, fl_col, name):
    T = fl.shape[0]
    tb = _pick(T, 256)
    nt = T // tb

    def body(fl_ref, bf_ref, dfk_ref, dfq_ref, dfl_ref, dbf_ref, c_ref):
        @pl.when(pl.program_id(0) == 0)
        def _():
            c_ref[...] = jnp.zeros_like(c_ref)
            dbf_ref[...] = jnp.zeros_like(dbf_ref)

        dlf = jnp.dot(_tri(tb, True), dfk_ref[...] + dfq_ref[...], precision=lax.Precision.HIGHEST,
                      preferred_element_type=F32) + c_ref[0:1, :]
        c_ref[0:1, :] = dlf[0:1, :]
        dfl = dlf * jax.nn.sigmoid(-(fl_ref[...] + bf_ref[...]))
        dfl_ref[...] = dfl
        dbf_ref[...] += jnp.sum(dfl, axis=0, keepdims=True)

    rev = pl.BlockSpec((tb, LANES), lambda i: (nt - 1 - i, 0))
    fl_rev = pl.BlockSpec((tb, LANES), lambda i: (nt - 1 - i, fl_col))
    return pl.pallas_call(body, name=name, grid=(nt,), in_specs=[fl_rev, _full((1, LANES)), rev, rev],
                          out_specs=[rev, _full((1, LANES))], out_shape=[SDS((T, LANES), F32), SDS((1, LANES), F32)],
                          scratch_shapes=[pltpu.VMEM((SUBLANES, LANES), F32)],
                          compiler_params=_cp("arbitrary"))(fl, bf, df_keys, df_queries)


FOX_BLOCK = 512
FOX_PAIRS = FOX_HEADS // 2
_NT = (((1,), (1,)), ((), ()))


LOG2E = 1.4426950408889634
FOX_FWD_UNROLL = 4
FOX_BWD_UNROLL = 2


def _fox_block(T):
    return _pick(T, FOX_BLOCK)


def _own_lanes(lane, hh):
    return (lane < FOX_HEAD_DIM) if hh == 0 else (lane >= FOX_HEAD_DIM)


def _grouped_steps(step, lo, n, unroll, init):
    def trip(t, c):
        for u in range(unroll):
            c = step(lo + t * unroll + u, c)
        return c

    carry = lax.fori_loop(0, n // unroll, trip, init)
    for u in range(unroll - 1):
        carry = lax.cond(n % unroll > u, lambda c: step(lo + (n // unroll) * unroll + u, c), lambda c: c, carry)
    return carry


Q_TILE0, K_TILE0, V_TILE0, O_TILE0 = 4, 8, 12, 4
FL_TILE = 16
POOL_COL = 2


def fox_fwd(z, f_col, f_row, ycat, hosted, *, name):
    T = z.shape[0]
    blk = _fox_block(T)
    nb = T // blk
    scale = FOX_HEAD_DIM ** -0.5

    def body(q_ref, k_ref, v_ref, fc_ref, fr_ref, prev_ref, o_ref, l_ref):
        i = pl.program_id(1)
        row = lax.broadcasted_iota(jnp.int32, (blk, blk), 0)
        col = lax.broadcasted_iota(jnp.int32, (blk, blk), 1)
        lane = lax.broadcasted_iota(jnp.int32, (blk, LANES), 1)
        qt = q_ref[...] * (scale * LOG2E)
        outs = []
        for hh in range(2):
            qh = jnp.where(_own_lanes(lane, hh), qt, 0.0).astype(MXU_DTYPE)
            fi = fc_ref[0, :, hh:hh + 1]

            def step(j, carry, masked=False):
                m, l, acc = carry
                r0 = pl.multiple_of(j * blk, blk)
                kj = k_ref[pl.ds(r0, blk), :].astype(MXU_DTYPE)
                vj = v_ref[pl.ds(r0, blk), :].astype(MXU_DTYPE)
                s = lax.dot_general(qh, kj, _NT, preferred_element_type=F32) + (fi - fr_ref[0, j, hh:hh + 1, :])
                if masked:
                    s = jnp.where(col <= row, s, -jnp.inf)
                m_new = jnp.maximum(m, jnp.max(s, axis=-1, keepdims=True))
                p = jnp.exp2(s - m_new)
                alpha = jnp.exp2(m - m_new)
                l = alpha * l + jnp.sum(p, axis=-1, keepdims=True)
                acc = alpha * acc + jnp.dot(p.astype(MXU_DTYPE), vj, preferred_element_type=F32)
                return m_new, l, acc

            init = (jnp.full((blk, 1), -jnp.inf, F32), jnp.zeros((blk, 1), F32), jnp.zeros((blk, LANES), F32))
            m, l, acc = step(i, _grouped_steps(step, 0, i, FOX_FWD_UNROLL, init), True)
            outs.append(acc / l)
            l_ref[0, :, hh:hh + 1] = m + jnp.log2(l)
        o_ref[...] = jnp.where(_own_lanes(lane, 0), outs[0], outs[1])

    qspec = pl.BlockSpec((blk, LANES), lambda h, i: (i, Q_TILE0 + h))
    kspec = pl.BlockSpec((T, LANES), lambda h, i: (0, K_TILE0 + h))
    vspec = pl.BlockSpec((T, LANES), lambda h, i: (0, V_TILE0 + h))
    ospec = pl.BlockSpec((blk, LANES), lambda h, i: (i, O_TILE0 + h))
    cspec = pl.BlockSpec((1, blk, 2), lambda h, i: (h, i, 0))
    rspec = pl.BlockSpec((1, nb, 2, blk), lambda h, i: (h, 0, 0, 0))
    return call_hosting(body, hosted, name=name, grid=(FOX_PAIRS, nb),
                        in_specs=[qspec, kspec, vspec, cspec, rspec, ANY], out_specs=[ospec, cspec],
                        out_shape=[SDS(ycat.shape, F32), SDS((FOX_PAIRS, T, 2), F32)],
                        inputs=[z, z, z, f_col, f_row, ycat], aliases={5: 0})


def fox_dd(ycat, dycat, *, name):
    T = ycat.shape[0]
    blk = _fox_block(T)

    def body(o_ref, do_ref, dd_ref):
        lane = lax.broadcasted_iota(jnp.int32, (blk, LANES), 1)
        prod = do_ref[...] * o_ref[...]
        for hh in range(2):
            dd_ref[0, :, hh:hh + 1] = jnp.sum(jnp.where(_own_lanes(lane, hh), prod, 0.0), axis=-1, keepdims=True)

    ospec = pl.BlockSpec((blk, LANES), lambda h, i: (i, O_TILE0 + h))
    return pl.pallas_call(body, name=name, grid=(FOX_PAIRS, T // blk), in_specs=[ospec, ospec],
                          out_specs=pl.BlockSpec((1, blk, 2), lambda h, i: (h, i, 0)),
                          out_shape=SDS((FOX_PAIRS, T, 2), F32), compiler_params=_cp("parallel", "parallel"))(ycat, dycat)


def fox_bwd(z, dycat, f_col, f_row, lse_row, dd_row, hosted, *, name):
    T = z.shape[0]
    blk = _fox_block(T)
    nb = T // blk
    scale = FOX_HEAD_DIM ** -0.5

    def body(q_ref, k_ref, v_ref, do_ref, fc_ref, fr_ref, lr_ref, dr_ref, dk_ref, dv_ref, df_ref, dqt_ref, dfq_ref):
        j = pl.program_id(1)

        @pl.when(j == 0)
        def _():
            dqt_ref[...] = jnp.zeros_like(dqt_ref)
            dfq_ref[...] = jnp.zeros_like(dfq_ref)

        row = lax.broadcasted_iota(jnp.int32, (blk, blk), 0)
        col = lax.broadcasted_iota(jnp.int32, (blk, blk), 1)
        lane = lax.broadcasted_iota(jnp.int32, (blk, LANES), 1)
        kt = k_ref[...]
        vt = v_ref[...]
        dks, dvs = [], []
        for hh in range(2):
            own = _own_lanes(lane, hh)
            kh = jnp.where(own, kt, 0.0).astype(MXU_DTYPE)
            vh = jnp.where(own, vt, 0.0).astype(MXU_DTYPE)
            kht = kh.T
            fj = fc_ref[0, :, hh:hh + 1]

            def step(i, carry, masked=False):
                dk, dv, df = carry
                r0 = pl.multiple_of(i * blk, blk)
                qi = (q_ref[pl.ds(r0, blk), :] * (scale * LOG2E)).astype(MXU_DTYPE)
                doi = do_ref[pl.ds(r0, blk), :].astype(MXU_DTYPE)
                st = lax.dot_general(kh, qi, _NT, preferred_element_type=F32) + (fr_ref[0, i, hh:hh + 1, :] - fj)
                pt = jnp.exp2(st - lr_ref[0, i, hh:hh + 1, :])
                if masked:
                    pt = jnp.where(col >= row, pt, 0.0)
                dv = dv + jnp.dot(pt.astype(MXU_DTYPE), doi, preferred_element_type=F32)
                dpt = lax.dot_general(vh, doi, _NT, preferred_element_type=F32)
                dst = pt * (dpt - dr_ref[0, i, hh:hh + 1, :])
                dsb = dst.astype(MXU_DTYPE)
                dk = dk + jnp.dot(dsb, qi, preferred_element_type=F32)
                df = df - jnp.sum(dst, axis=-1, keepdims=True)
                dqt_ref[0, i] += jnp.dot(kht, dsb, preferred_element_type=F32)
                dfq_ref[0, i, hh:hh + 1, :] += jnp.sum(dst, axis=0, keepdims=True)
                return dk, dv, df

            init = (jnp.zeros((blk, LANES), F32), jnp.zeros((blk, LANES), F32), jnp.zeros((blk, 1), F32))
            dk, dv, df = _grouped_steps(step, j + 1, nb - 1 - j, FOX_BWD_UNROLL, step(j, init, True))
            dks.append(dk * (1.0 / LOG2E))
            dvs.append(dv)
            df_ref[0, :, hh:hh + 1] = df
        dk_ref[...] = jnp.where(_own_lanes(lane, 0), dks[0], dks[1])
        dv_ref[...] = jnp.where(_own_lanes(lane, 0), dvs[0], dvs[1])

    bspec = pl.BlockSpec((blk, LANES), lambda h, j: (j, h))
    qspec = pl.BlockSpec((T, LANES), lambda h, j: (0, Q_TILE0 + h))
    kspec = pl.BlockSpec((blk, LANES), lambda h, j: (j, K_TILE0 + h))
    vspec = pl.BlockSpec((blk, LANES), lambda h, j: (j, V_TILE0 + h))
    dospec = pl.BlockSpec((T, LANES), lambda h, j: (0, O_TILE0 + h))
    cspec = pl.BlockSpec((1, blk, 2), lambda h, j: (h, j, 0))
    rspec = pl.BlockSpec((1, nb, 2, blk), lambda h, j: (h, 0, 0, 0))
    dqspec = pl.BlockSpec((1, nb, LANES, blk), lambda h, j: (h, 0, 0, 0))
    return call_hosting(body, hosted, name=name, grid=(FOX_PAIRS, nb),
                        in_specs=[qspec, kspec, vspec, dospec, cspec, rspec, rspec, rspec],
                        out_specs=[bspec, bspec, cspec, dqspec, rspec],
                        out_shape=[SDS((T, FOX_WIDTH), F32), SDS((T, FOX_WIDTH), F32), SDS((FOX_PAIRS, T, 2), F32),
                                   SDS((FOX_PAIRS, nb, LANES, blk), F32), SDS((FOX_PAIRS, nb, 2, blk), F32)],
                        inputs=[z, z, z, dycat, f_col, f_row, lse_row, dd_row], aliases={})


def _pairs_col(a, T):
    return jnp.transpose(a[:, :FOX_HEADS].reshape(T, FOX_PAIRS, 2), (1, 0, 2))


def _col_to_row(a, T):
    blk = _fox_block(T)
    return jnp.transpose(a.reshape(FOX_PAIRS, T // blk, blk, 2), (0, 1, 3, 2))


def _row_to_col(a, T):
    return jnp.transpose(a, (0, 1, 3, 2)).reshape(FOX_PAIRS, T, 2)


def _pairs_to_lanes(a, T):
    flat = jnp.transpose(a, (1, 0, 2)).reshape(T, FOX_HEADS)
    return jnp.pad(flat, ((0, 0), (0, LANES - FOX_HEADS)))


def _pool_counts(t0, n, w):
    t = (t0 + lax.broadcasted_iota(jnp.int32, (n, 1), 0)).astype(F32)
    return jnp.minimum(t + 1.0, float(w))


def pool_window(x, *, adjoint, name, in_col=0, into=None, out_col=0):
    T, C = x.shape[0], len(POOL_WINDOWS) * POOL_GROUP_DIM
    tr = _pick(T, 512)
    nt = T // tr
    hb = tr // POOL_HALO
    n = tr + POOL_HALO

    def body(x_ref, h_ref, *rest):
        o_ref = rest[-1]
        i = pl.program_id(0)
        cur = x_ref[...]
        if adjoint:
            halo = h_ref[...] * jnp.where(i < nt - 1, 1.0, 0.0)
            ext = jnp.concatenate([cur, halo], axis=0)
            t0 = i * tr
        else:
            halo = h_ref[...] * jnp.where(i > 0, 1.0, 0.0)
            ext = jnp.concatenate([halo, cur], axis=0)
            t0 = i * tr - POOL_HALO
        sums = {}
        for g, w in enumerate(POOL_WINDOWS):
            ls = slice(g * POOL_GROUP_DIM, (g + 1) * POOL_GROUP_DIM)
            s = ext[:, ls]
            if adjoint:
                s = s / _pool_counts(t0, n, w)
            d = 1
            while d < w:
                s = s + pltpu.roll(s, (n - d) if adjoint else d, 0)
                d *= 2
            if adjoint:
                o_ref[:, ls] = s[0:tr, :] - cur[:, ls]
            else:
                o_ref[:, ls] = s[POOL_HALO:n, :] / _pool_counts(i * tr, tr, w) - cur[:, ls]

    if adjoint:
        halo_spec = pl.BlockSpec((POOL_HALO, C), lambda i: (jnp.minimum((i + 1) * hb, T // POOL_HALO - 1), in_col))
    else:
        halo_spec = pl.BlockSpec((POOL_HALO, C), lambda i: (jnp.maximum(i * hb - 1, 0), in_col))
    x_spec = pl.BlockSpec((tr, C), lambda i: (i, in_col))
    if into is None:
        return pl.pallas_call(body, name=name, grid=(nt,), in_specs=[x_spec, halo_spec], out_specs=_row(tr, C),
                              out_shape=SDS((T, C), F32), compiler_params=_cp("parallel"))(x, x)
    return pl.pallas_call(body, name=name, grid=(nt,), in_specs=[x_spec, halo_spec, ANY],
                          out_specs=pl.BlockSpec((tr, C), lambda i: (i, out_col)), out_shape=SDS(into.shape, F32),
                          input_output_aliases={2: 0}, compiler_params=_cp("parallel"))(x, x, into)


def colscale_fwd(a, s, *, out_cols, name):
    T, C = a.shape
    tr = _pick(T, 512)

    def body(a_ref, s_ref, o_ref):
        o_ref[...] = a_ref[...] * s_ref[...]

    return pl.pallas_call(body, name=name, grid=(T // tr,), in_specs=[_row(tr, C), _full((1, C))], out_specs=_row(tr, C),
                          out_shape=SDS((T, out_cols), F32), compiler_params=_cp("parallel"))(a, s)


def colscale_bwd(a, s, dy, *, name):
    T, C = a.shape
    tr = _pick(T, 512)

    def body(a_ref, s_ref, dy_ref, da_ref, ds_ref):
        dyv = dy_ref[...]
        da_ref[...] = dyv * s_ref[...]

        @pl.when(pl.program_id(0) == 0)
        def _():
            ds_ref[...] = jnp.zeros_like(ds_ref)

        ds_ref[...] += jnp.sum(dyv * a_ref[...], axis=0, keepdims=True)

    return pl.pallas_call(body, name=name, grid=(T // tr,), in_specs=[_row(tr, C), _full((1, C)), _row(tr, C)],
                          out_specs=[_row(tr, C), _full((1, C))], out_shape=[SDS((T, C), F32), SDS((1, C), F32)],
                          compiler_params=_cp("arbitrary"))(a, s, dy)


SGU_ROWS = 512


def _sgu_norm(v, ln_g, ln_b):
    vg = jax.nn.gelu(v)
    xc = vg - jnp.mean(vg, axis=-1, keepdims=True)
    r = lax.rsqrt(jnp.mean(xc * xc, axis=-1, keepdims=True) + EPS)
    xh = xc * r
    return xh * ln_g + ln_b, xh, r


def _rowc(tr, c, cb):
    return pl.BlockSpec((tr, c), lambda i: (i, cb))


def sgu_fwd(z, ln_g, ln_b, ws, bst, ycat, *, name):
    T, C = z.shape[0], SGU_GROUPS * SGU_GROUP_DIM
    tr = _pick(T, SGU_ROWS)

    def body(u_ref, v_ref, g_ref, b_ref, ws_ref, bst_ref, prev_ref, o_ref):
        vn, _, _ = _sgu_norm(v_ref[...], g_ref[...], b_ref[...])
        vn = vn.astype(MXU_DTYPE)
        ug = jax.nn.gelu(u_ref[...])
        for g in range(SGU_GROUPS):
            w = ws_ref[g].astype(MXU_DTYPE)
            bias = bst_ref[:, g:g + 1]
            for c in range(tr // CHUNK):
                rs = slice(c * CHUNK, (c + 1) * CHUNK)
                ls = slice(g * SGU_GROUP_DIM, (g + 1) * SGU_GROUP_DIM)
                mixed = jnp.dot(w, vn[rs, ls], preferred_element_type=F32) + bias
                o_ref[rs, ls] = ug[rs, ls] * mixed

    return pl.pallas_call(body, name=name, grid=(T // tr,),
                          in_specs=[_rowc(tr, C, 0), _rowc(tr, C, 1), _full((1, C)), _full((1, C)),
                                    _full((SGU_GROUPS, CHUNK, CHUNK)), _full((CHUNK, SGU_GROUPS)), ANY],
                          out_specs=_rowc(tr, C, 1), out_shape=SDS(ycat.shape, F32), input_output_aliases={6: 0},
                          compiler_params=_cp("parallel"))(z, z, ln_g, ln_b, ws, bst, ycat)


def sgu_bwd(z, ln_g, ln_b, ws, wst, bst, dycat, *, out_cols, name):
    T, C = z.shape[0], SGU_GROUPS * SGU_GROUP_DIM
    tr = _pick(T, SGU_ROWS)

    def body(u_ref, v_ref, g_ref, b_ref, ws_ref, wst_ref, bst_ref, dy_ref,
             duv_ref, dg_ref, db_ref, dws_ref, dbst_ref, dvn_ref):
        du_ref = duv_ref.at[:, 0:C]
        dv_ref = duv_ref.at[:, C:2 * C]
        @pl.when(pl.program_id(0) == 0)
        def _():
            dg_ref[...] = jnp.zeros_like(dg_ref)
            db_ref[...] = jnp.zeros_like(db_ref)
            dws_ref[...] = jnp.zeros_like(dws_ref)
            dbst_ref[...] = jnp.zeros_like(dbst_ref)

        uv = u_ref[...]
        vv = v_ref[...]
        vn, xh, r = _sgu_norm(vv, g_ref[...], b_ref[...])
        vn = vn.astype(MXU_DTYPE)
        ug = jax.nn.gelu(uv)
        dyv = dy_ref[...]
        for g in range(SGU_GROUPS):
            w = ws_ref[g].astype(MXU_DTYPE)
            wt = wst_ref[g].astype(MXU_DTYPE)
            bias = bst_ref[:, g:g + 1]
            dw = jnp.zeros((CHUNK, CHUNK), F32)
            dbias = jnp.zeros((CHUNK, 1), F32)
            for c in range(tr // CHUNK):
                rs = slice(c * CHUNK, (c + 1) * CHUNK)
                ls = slice(g * SGU_GROUP_DIM, (g + 1) * SGU_GROUP_DIM)
                vblk = vn[rs, ls]
                mixed = jnp.dot(w, vblk, preferred_element_type=F32) + bias
                dyb = dyv[rs, ls]
                du_ref[rs, ls] = dyb * mixed * _gelu_grad(uv[rs, ls])
                dmixed = dyb * ug[rs, ls]
                dbias = dbias + jnp.sum(dmixed, axis=-1, keepdims=True)
                dmb = dmixed.astype(MXU_DTYPE)
                dw = dw + lax.dot_general(dmb, vblk, _NT, preferred_element_type=F32)
                dvn_ref[rs, ls] = jnp.dot(wt, dmb, preferred_element_type=F32)
            dws_ref[g] += dw
            dbst_ref[:, g:g + 1] += dbias
        dvn = dvn_ref[...]
        dg_ref[...] += jnp.sum(dvn * xh, axis=0, keepdims=True)
        db_ref[...] += jnp.sum(dvn, axis=0, keepdims=True)
        dxh = dvn * g_ref[...]
        dvg = r * (dxh - jnp.mean(dxh, axis=-1, keepdims=True) - xh * jnp.mean(dxh * xh, axis=-1, keepdims=True))
        dv_ref[...] = dvg * _gelu_grad(vv)

    wspec = _full((SGU_GROUPS, CHUNK, CHUNK))
    return pl.pallas_call(body, name=name, grid=(T // tr,),
                          in_specs=[_rowc(tr, C, 0), _rowc(tr, C, 1), _full((1, C)), _full((1, C)), wspec, wspec,
                                    _full((CHUNK, SGU_GROUPS)), _rowc(tr, C, 1)],
                          out_specs=[_rowc(tr, 2 * C, 0), _full((1, C)), _full((1, C)), wspec,
                                     _full((CHUNK, SGU_GROUPS))],
                          out_shape=[SDS((T, out_cols), F32), SDS((1, C), F32), SDS((1, C), F32),
                                     SDS((SGU_GROUPS, CHUNK, CHUNK), F32), SDS((CHUNK, SGU_GROUPS), F32)],
                          scratch_shapes=[pltpu.VMEM((tr, C), F32)],
                          compiler_params=_cp("arbitrary"))(z, z, ln_g, ln_b, ws, wst, bst, dycat)


def adamw(w, g, m, v, *, name):
    R, C = w.shape
    tr = _pick(R, 512)
    c1 = 1.0 - ADAM_B1 ** ADAM_STEP
    c2 = 1.0 - ADAM_B2 ** ADAM_STEP

    def body(w_ref, g_ref, m_ref, v_ref, d_ref, nm_ref, nv_ref):
        gv = g_ref[...]
        nm = ADAM_B1 * m_ref[...] + (1.0 - ADAM_B1) * gv
        nv = ADAM_B2 * v_ref[...] + (1.0 - ADAM_B2) * (gv * gv)
        nm_ref[...] = nm
        nv_ref[...] = nv
        d_ref[...] = -ADAM_LR * ((nm / c1) / (jnp.sqrt(nv / c2) + ADAM_EPS) + ADAM_WD * w_ref[...])

    spec = _row(tr, C)
    return pl.pallas_call(body, name=name, grid=(R // tr,), in_specs=[spec] * 4, out_specs=[spec] * 3,
                          out_shape=[SDS((R, C), F32)] * 3, compiler_params=_cp("parallel"))(w, g, m, v)


ANY = pl.BlockSpec(memory_space=pl.ANY)


def _coords():
    return lax.axis_index("x"), lax.axis_index("y"), lax.axis_index("c")


def _other_chips(x, y):
    return [(1 - x, y), (x, 1 - y), (1 - x, 1 - y)]


def _remote(src, dst, send_sems, recv_sems, k, dev):
    return pltpu.make_async_remote_copy(src_ref=src, dst_ref=dst, send_sem=send_sems.at[k], recv_sem=recv_sems.at[k],
                                        device_id=dev, device_id_type=MESH)


LOCAL_CHUNKS = 8


def allgather_chip_shards(shards, small, *, name):
    na = len(shards)

    def body(*refs):
        s_refs, sm_ref = refs[:na], refs[na]
        o_refs, smo_ref = refs[na + 1:2 * na + 1], refs[2 * na + 1]
        send_sems, recv_sems, local_sems = refs[2 * na + 2:]
        x, y, c = _coords()
        j = 2 * x + y
        sibling = (x, y, 1 - c)
        chips = _other_chips(x, y)
        for a in range(na):
            chunk = shards[a].shape[0] // LOCAL_CHUNKS
            for q in range(LOCAL_CHUNKS):
                rows = pl.ds(q * chunk, chunk)
                pltpu.make_async_copy(s_refs[a].at[rows], o_refs[a].at[j, rows], local_sems.at[a]).start()
        pltpu.make_async_copy(sm_ref, smo_ref.at[j], local_sems.at[na]).start()
        sends = []
        for a in range(na):
            half = shards[a].shape[0] // 2
            mine = pl.ds(c * half, half)
            for k, (px, py) in enumerate(chips):
                sends.append(_remote(s_refs[a].at[mine], o_refs[a].at[j, mine], send_sems, recv_sems, 6 * a + k, (px, py, c)))
        for k, (px, py) in enumerate(chips):
            sends.append(_remote(sm_ref, smo_ref.at[j], send_sems, recv_sems, 6 * na + k, (px, py, c)))
        for cp in sends:
            cp.start()
        for a in range(na):
            half = shards[a].shape[0] // 2
            mine = pl.ds(c * half, half)
            for k, (px, py) in enumerate(chips):
                rows = o_refs[a].at[2 * px + py, mine]
                _remote(rows, rows, send_sems, recv_sems, 6 * a + k, (px, py, c)).wait_recv()
                fw = _remote(rows, rows, send_sems, recv_sems, 6 * a + 3 + k, sibling)
                fw.start()
                sends.append(fw)
        for a in range(na):
            half = shards[a].shape[0] // 2
            theirs = pl.ds((1 - c) * half, half)
            for k, (px, py) in enumerate(chips):
                rows = o_refs[a].at[2 * px + py, theirs]
                _remote(rows, rows, send_sems, recv_sems, 6 * a + 3 + k, sibling).wait_recv()
        for k, (px, py) in enumerate(chips):
            slot = smo_ref.at[2 * px + py]
            _remote(slot, slot, send_sems, recv_sems, 6 * na + k, (px, py, c)).wait_recv()
        for cp in sends:
            cp.wait_send()
        for a in range(na):
            pltpu.make_async_copy(s_refs[a], o_refs[a].at[j], local_sems.at[a]).wait()
        pltpu.make_async_copy(sm_ref, smo_ref.at[j], local_sems.at[na]).wait()

    nsem = 6 * na + 3
    outs = pl.pallas_call(
        body, name=name, in_specs=[ANY] * (na + 1), out_specs=[ANY] * (na + 1),
        out_shape=[SDS((N_CHIPS,) + s.shape, s.dtype) for s in shards] + [SDS((N_CHIPS,) + small.shape, small.dtype)],
        scratch_shapes=[pltpu.SemaphoreType.DMA((nsem,)), pltpu.SemaphoreType.DMA((nsem,)),
                        pltpu.SemaphoreType.DMA((na + 1,))])(*shards, small)
    return outs[:na], outs[na]


class Exchange:
    def __init__(self, ins, out_shapes, scratch, start, wait):
        self.ins, self.out_shapes, self.scratch, self.start, self.wait = list(ins), list(out_shapes), list(scratch), start, wait


def run_exchange(ex, *, name):
    ni, no = len(ex.ins), len(ex.out_shapes)

    def body(*refs):
        parts = refs[:ni], refs[ni:ni + no], refs[ni + no:]
        ex.start(*parts)
        ex.wait(*parts)

    return pl.pallas_call(body, name=name, in_specs=[ANY] * ni, out_specs=[ANY] * no, out_shape=ex.out_shapes,
                          scratch_shapes=ex.scratch)(*ex.ins)


def call_hosting(body, ex, *, name, grid, in_specs, out_specs, out_shape, inputs, aliases, scratch=()):
    n_in, n_out, ni, no, ns = len(inputs), len(out_shape), len(ex.ins), len(ex.out_shapes), len(scratch)
    outs_at = n_in + ni
    scr_at = outs_at + n_out + no

    def wrapped(*refs):
        own = refs[:n_in] + refs[outs_at:outs_at + n_out] + refs[scr_at:scr_at + ns]
        parts = refs[n_in:outs_at], refs[outs_at + n_out:scr_at], refs[scr_at + ns:]
        ids = [pl.program_id(d) for d in range(len(grid))]
        first = functools.reduce(jnp.logical_and, [i == 0 for i in ids])
        last = functools.reduce(jnp.logical_and, [i == g - 1 for i, g in zip(ids, grid)])

        @pl.when(first)
        def _():
            ex.start(*parts)

        body(*own)

        @pl.when(last)
        def _():
            ex.wait(*parts)

    outs = pl.pallas_call(
        wrapped, name=name, grid=grid, in_specs=list(in_specs) + [ANY] * ni, out_specs=list(out_specs) + [ANY] * no,
        out_shape=list(out_shape) + ex.out_shapes, input_output_aliases=aliases,
        scratch_shapes=list(scratch) + ex.scratch,
        compiler_params=_cp(*["arbitrary"] * len(grid)))(*inputs, *ex.ins)
    return outs[:n_out], outs[n_out:]


def allgather_ici_exchange(shards):
    na = len(shards)

    def copies(s_refs, o_refs, sems):
        send_sems, recv_sems, _ = sems
        x, y, c = _coords()
        j = 2 * x + y
        out = []
        for a in range(na):
            half = shards[a].shape[0] // 2
            mine = pl.ds(c * half, half)
            for k, (px, py) in enumerate(_other_chips(x, y)):
                send = _remote(s_refs[a].at[mine], o_refs[a].at[j, mine], send_sems, recv_sems, 3 * a + k, (px, py, c))
                rows = o_refs[a].at[2 * px + py, mine]
                out.append((send, _remote(rows, rows, send_sems, recv_sems, 3 * a + k, (px, py, c))))
        return out

    def start(s_refs, o_refs, sems):
        x, y, c = _coords()
        j = 2 * x + y
        for a in range(na):
            chunk = shards[a].shape[0] // LOCAL_CHUNKS
            for q in range(LOCAL_CHUNKS):
                rows = pl.ds(q * chunk, chunk)
                pltpu.make_async_copy(s_refs[a].at[rows], o_refs[a].at[j, rows], sems[2].at[a]).start()
        for send, _ in copies(s_refs, o_refs, sems):
            send.start()

    def wait(s_refs, o_refs, sems):
        x, y, c = _coords()
        j = 2 * x + y
        for send, arrival in copies(s_refs, o_refs, sems):
            arrival.wait_recv()
            send.wait_send()
        for a in range(na):
            pltpu.make_async_copy(s_refs[a], o_refs[a].at[j], sems[2].at[a]).wait()

    return Exchange(shards, [SDS((N_CHIPS,) + s.shape, s.dtype) for s in shards],
                    [pltpu.SemaphoreType.DMA((3 * na,)), pltpu.SemaphoreType.DMA((3 * na,)), pltpu.SemaphoreType.DMA((na,))],
                    start, wait)


def allgather_forward(gathered, *, name):
    na = len(gathered)

    def body(*refs):
        o_refs = refs[na:2 * na]
        send_sems, recv_sems = refs[2 * na:]
        x, y, c = _coords()
        sibling = (x, y, 1 - c)
        cps = []
        for a in range(na):
            half = gathered[a].shape[1] // 2
            for k, (px, py) in enumerate(_other_chips(x, y)):
                mine = o_refs[a].at[2 * px + py, pl.ds(c * half, half)]
                theirs = o_refs[a].at[2 * px + py, pl.ds((1 - c) * half, half)]
                cps.append((_remote(mine, mine, send_sems, recv_sems, 3 * a + k, sibling),
                            _remote(theirs, theirs, send_sems, recv_sems, 3 * a + k, sibling)))
        for send, _ in cps:
            send.start()
        for send, arrival in cps:
            send.wait_send()
            arrival.wait_recv()

    return pl.pallas_call(body, name=name, in_specs=[ANY] * na, out_specs=[ANY] * na,
                          out_shape=[SDS(g.shape, g.dtype) for g in gathered],
                          input_output_aliases={a: a for a in range(na)},
                          scratch_shapes=[pltpu.SemaphoreType.DMA((3 * na,)), pltpu.SemaphoreType.DMA((3 * na,))])(*gathered)


def swap_halves_exchange(gs):
    na = len(gs)

    def copies(g_refs, o_refs, sems):
        x, y, c = _coords()
        out = []
        for a in range(na):
            half = gs[a].shape[1] // 2
            out.append(_remote(g_refs[a].at[:, pl.ds((1 - c) * half, half), :], o_refs[a], sems[0], sems[1], a,
                               (x, y, 1 - c)))
        return out

    def start(g_refs, o_refs, sems):
        for cp in copies(g_refs, o_refs, sems):
            cp.start()

    def wait(g_refs, o_refs, sems):
        for cp in copies(g_refs, o_refs, sems):
            cp.wait()

    return Exchange(gs, [SDS((g.shape[0], g.shape[1] // 2, g.shape[2]), g.dtype) for g in gs],
                    [pltpu.SemaphoreType.DMA((na,)), pltpu.SemaphoreType.DMA((na,))], start, wait)


def chip_partials_exchange(pbs):
    na = len(pbs)

    def copies(p_refs, o_refs, sems):
        x, y, c = _coords()
        out = []
        for a in range(na):
            for k, (px, py) in enumerate(_other_chips(x, y)):
                out.append(_remote(p_refs[a].at[2 * px + py], o_refs[a].at[k], sems[0], sems[1], 3 * a + k, (px, py, c)))
        return out

    def start(p_refs, o_refs, sems):
        for cp in copies(p_refs, o_refs, sems):
            cp.start()

    def wait(p_refs, o_refs, sems):
        for cp in copies(p_refs, o_refs, sems):
            cp.wait()

    return Exchange(pbs, [SDS((3,) + p.shape[1:], p.dtype) for p in pbs],
                    [pltpu.SemaphoreType.DMA((3 * na,)), pltpu.SemaphoreType.DMA((3 * na,))], start, wait)


def add_sibling_half(g, land, c_idx, *, name):
    n, R, C = g.shape
    half = R // 2
    tr = _pick(half, 256)
    nt = half // tr

    def body(c_ref, g_ref, l_ref, of_ref, ob_ref):
        s = g_ref[...] + l_ref[...].astype(F32)
        of_ref[...] = s
        ob_ref[...] = s.astype(ob_ref.dtype)

    blk = pl.BlockSpec((1, tr, C), lambda s, i, c_ref: (s, i, 0))
    gblk = pl.BlockSpec((1, tr, C), lambda s, i, c_ref: (s, c_ref[0] * nt + i, 0))
    return pl.pallas_call(
        body, name=name,
        grid_spec=pltpu.PrefetchScalarGridSpec(num_scalar_prefetch=1, grid=(n, nt), in_specs=[gblk, blk],
                                               out_specs=[blk, blk]),
        out_shape=[SDS((n, half, C), F32), SDS((n, half, C), WIRE_DTYPE)],
        compiler_params=_cp("parallel", "parallel"))(c_idx, g, land)


def add_chip_partials(pf, rb, jc_idx, *, name):
    n, H, C = pf.shape
    tr = _pick(H, 256)

    def body(jc_ref, p_ref, r_ref, o_ref):
        s = p_ref[0]
        for k in range(3):
            s = s + r_ref[k].astype(F32)
        o_ref[...] = s

    pblk = pl.BlockSpec((1, tr, C), lambda i, jc_ref: (jc_ref[0], i, 0))
    rblk = pl.BlockSpec((3, tr, C), lambda i, jc_ref: (0, i, 0))
    oblk = pl.BlockSpec((None, tr, C), lambda i, jc_ref: (jc_ref[1], i, 0))
    return pl.pallas_call(
        body, name=name,
        grid_spec=pltpu.PrefetchScalarGridSpec(num_scalar_prefetch=1, grid=(H // tr,), in_specs=[pblk, rblk],
                                               out_specs=oblk),
        out_shape=SDS((2, H, C), F32), compiler_params=_cp("parallel"))(jc_idx, pf, rb)


def join_sibling_halves(bufs, *, name):
    na = len(bufs)

    def body(*refs):
        o_refs = refs[na:2 * na]
        send_sems, recv_sems = refs[2 * na:]
        x, y, c = _coords()
        cps = [_remote(o_refs[a].at[c], o_refs[a].at[c], send_sems, recv_sems, a, (x, y, 1 - c)) for a in range(na)]
        for cp in cps:
            cp.start()
        for a in range(na):
            cps[a].wait_send()
            _remote(o_refs[a].at[1 - c], o_refs[a].at[1 - c], send_sems, recv_sems, a, (x, y, 1 - c)).wait_recv()

    return pl.pallas_call(body, name=name, in_specs=[ANY] * na, out_specs=[ANY] * na,
                          out_shape=[SDS(b.shape, b.dtype) for b in bufs],
                          input_output_aliases={a: a for a in range(na)},
                          scratch_shapes=[pltpu.SemaphoreType.DMA((na,)), pltpu.SemaphoreType.DMA((na,))])(*bufs)


def exchange_pieces(v, *, scatter, name):
    P, C = v.shape[-2:]

    def body(v_ref, o_ref, send_sems, recv_sems, local_sem):
        x, y, c = _coords()
        me = 4 * x + 2 * y + c
        local = pltpu.make_async_copy(v_ref.at[me] if scatter else v_ref, o_ref.at[me], local_sem)
        local.start()
        cps = []
        for m in range(1, N_DEV):
            px = (1 - x) if m & 4 else x
            py = (1 - y) if m & 2 else y
            pc = (1 - c) if m & 1 else c
            src = v_ref.at[4 * px + 2 * py + pc] if scatter else v_ref
            cps.append(_remote(src, o_ref.at[me], send_sems, recv_sems, m - 1, (px, py, pc)))
        for cp in cps:
            cp.start()
        for cp in cps:
            cp.wait_send()
        for m in range(1, N_DEV):
            px = (1 - x) if m & 4 else x
            py = (1 - y) if m & 2 else y
            pc = (1 - c) if m & 1 else c
            slot = o_ref.at[4 * px + 2 * py + pc]
            _remote(slot, slot, send_sems, recv_sems, m - 1, (px, py, pc)).wait_recv()
        local.wait()

    return pl.pallas_call(body, name=name, in_specs=[ANY], out_specs=ANY, out_shape=SDS((N_DEV, P, C), v.dtype),
                          scratch_shapes=[pltpu.SemaphoreType.DMA((N_DEV - 1,)), pltpu.SemaphoreType.DMA((N_DEV - 1,)),
                                          pltpu.SemaphoreType.DMA(())])(v)


def sum_pieces(land, *, name):
    n, P, C = land.shape

    def body(l_ref, o_ref):
        s = l_ref[0]
        for d in range(1, n):
            s = s + l_ref[d]
        o_ref[...] = s

    return pl.pallas_call(body, name=name, out_shape=SDS((P, C), F32))(land)


BIG_SEGS = (
    ("w_in_even", (1024, 514), 1),
    ("s5_w_glu", (128, 512), 0),
    ("w_out_even", (256, 1024), 0),
    ("w_in_odd", (1024, 384), 1),
    ("w_out_odd", (256, 1024), 0),
    ("mlp_w1", (2, 1024, 1024), 2),
    ("mlp_w2", (2, 1024, 1024), 1),
)
BIG_NAMES = tuple(n for n, _, _ in BIG_SEGS)
EARLY_NAMES = ("w_in_even", "s5_w_glu")
LATE_NAMES = ("w_out_even", "w_in_odd", "w_out_odd", "mlp_w1", "mlp_w2")
REDUCED_EARLY = ("s5_w_glu", "w_out_even", "w_in_odd", "w_out_odd", "mlp_w1", "mlp_w2")
SHARDED_SMALL = ("pool_scale", "sgu_ln_g", "sgu_ln_b")
SMALL_SEGS = (
    ("mix_pre_g", (2, 1024)), ("mix_post_g", (2, 1024)), ("mlp_pre_g", (2, 1024)), ("mlp_post_g", (2, 1024)),
    ("s5_lam_re", (1, 32, 64)), ("s5_lam_im", (1, 32, 64)), ("s5_log_dt", (1, 32)),
    ("s5_b_re", (1, 32, 64, 16)), ("s5_b_im", (1, 32, 64, 16)), ("s5_c_re", (1, 32, 16, 64)), ("s5_c_im", (1, 32, 16, 64)),
    ("s5_d", (1, 512)), ("fox_b_f", (1, 8)), ("pool_w", (1, 4, 128, 128)), ("sgu_w_s", (1, 4, 128, 128)),
    ("sgu_b_s", (1, 4, 128)),
)
REDUCED_SEGS = SMALL_SEGS + tuple((n, (1, 512)) for n in SHARDED_SMALL) + (("loss", (1, 1)),)


def _cols_from_chips(g):
    n, R, C = g.shape
    return jnp.transpose(g, (1, 0, 2)).reshape(R, n * C)


def _chips_from_cols(m):
    R, C4 = m.shape
    return jnp.transpose(m.reshape(R, N_CHIPS, C4 // N_CHIPS), (1, 0, 2))


MLP_SHARD = 1024


def _w1_cols(l):
    def spec(tm, tn, tk):
        per = MLP_SHARD // tn
        return pl.BlockSpec((None, tk, tn), lambda i, j, k: (j // per, l * (MLP_SHARD // tk) + k, j % per))
    return spec


def _w1_rows_t(l):
    def spec(tm, tn, tk):
        if tk == N_CHIPS * MLP_SHARD:
            return pl.BlockSpec((N_CHIPS, tn, MLP_SHARD), lambda i, j, k: (0, l * (MLP_SHARD // tn) + j, 0))
        per = MLP_SHARD // tk
        return pl.BlockSpec((None, tn, tk), lambda i, j, k: (k // per, l * (MLP_SHARD // tn) + j, k % per))
    return spec


def _w2_rows(l):
    def spec(tm, tn, tk):
        if tk == N_CHIPS * MLP_SHARD:
            return pl.BlockSpec((N_CHIPS, MLP_SHARD, tn), lambda i, j, k: (0, l, j))
        per = MLP_SHARD // tk
        return pl.BlockSpec((None, tk, tn), lambda i, j, k: (k // per, l * per + k % per, j))
    return spec


def _w2_rows_t(l):
    def spec(tm, tn, tk):
        per = MLP_SHARD // tn
        return pl.BlockSpec((None, tn, tk), lambda i, j, k: (j // per, l * per + j % per, k))
    return spec


def _dw1_out(l):
    def spec(tm, tn, tk):
        per = MLP_SHARD // tn
        return pl.BlockSpec((None, tm, tn), lambda i, j, k: (j // per, l * (MLP_SHARD // tm) + i, j % per))
    return spec


def _dw2_out(l):
    def spec(tm, tn, tk):
        per = MLP_SHARD // tm
        return pl.BlockSpec((None, tm, tn), lambda i, j, k: (i // per, l * per + i % per, j))
    return spec


def _pack_vec(d, segs, rows_multiple):
    flat = jnp.concatenate([d[n].reshape(-1) for n, _ in segs])
    rows = -(-flat.shape[0] // LANES)
    rows = -(-rows // rows_multiple) * rows_multiple
    return jnp.pad(flat, (0, rows * LANES - flat.shape[0])).reshape(rows, LANES)


def _unpack_vec(v, segs):
    flat, out, r = v.reshape(-1), {}, 0
    for n, shape in segs:
        k = math.prod(shape)
        out[n] = flat[r:r + k].reshape(shape)
        r += k
    return out


def _block_diag(blocks):
    G, a, b = blocks.shape
    eye = jnp.eye(G, dtype=blocks.dtype)
    return (eye[:, None, :, None] * blocks[:, :, None, :]).reshape(G * a, G * b)


def _diag_blocks(m, G):
    a, b = m.shape[0] // G, m.shape[1] // G
    return jnp.stack([m[g * a:(g + 1) * a, g * b:(g + 1) * b] for g in range(G)])


def _sqrelu_epi(acc):
    r = jnp.maximum(acc, 0.0)
    return acc, r * r


def _sqrelu_bwd_epi(acc, a):
    return (acc * (2.0 * jnp.maximum(a.astype(F32), 0.0)),)


def _mlp_fwd(h, g1, g2, l, tag):
    T, D = h.shape
    a, s = matmul(h, g1, name=f"{tag}_up", mnk=(T, D_FF, D), b_spec=_w1_cols(l), epi=_sqrelu_epi,
                  out_dtypes=(MXU_DTYPE, MXU_DTYPE))
    m = matmul(s, g2, name=f"{tag}_down", mnk=(T, D, D_FF), b_spec=_w2_rows(l))
    return m, (h, a, s)


def _mlp_bwd(saved, dm, g1, g2, l, dg1, dg2, tag):
    h, a, s = saved
    T, D = h.shape
    gshape = (N_CHIPS, 2 * MLP_SHARD, MLP_SHARD)
    da = matmul(dm, g2, tb=True, name=f"{tag}_down_dx", mnk=(T, D_FF, D), b_spec=_w2_rows_t(l),
                epi=_sqrelu_bwd_epi, epi_in=(a,), out_dtype=MXU_DTYPE)
    dg2 = matmul(s, dm, ta=True, name=f"{tag}_down_dw", tm=MLP_SHARD, o_spec=_dw2_out(l), o_shape=gshape, prev=dg2)
    dh = matmul(da, g1, tb=True, name=f"{tag}_up_dx", mnk=(T, D, D_FF), b_spec=_w1_rows_t(l))
    dg1 = matmul(h, da, ta=True, name=f"{tag}_up_dw", o_spec=_dw1_out(l), o_shape=gshape, prev=dg1)
    return dh, dg1, dg2


def kernel(x, mix_pre_g, mix_post_g, mlp_pre_g, mlp_post_g, w_in_even, s5_lam_re, s5_lam_im, s5_log_dt, s5_b_re, s5_b_im, s5_c_re, s5_c_im, s5_d, s5_w_glu, fox_b_f, w_out_even, w_in_odd, pool_w, pool_scale, sgu_ln_g, sgu_ln_b, sgu_w_s, sgu_b_s, w_out_odd, mlp_w1, mlp_w2, loss_target, m_mix_pre_g, m_mix_post_g, m_mlp_pre_g, m_mlp_post_g, m_w_in_even, m_s5_lam_re, m_s5_lam_im, m_s5_log_dt, m_s5_b_re, m_s5_b_im, m_s5_c_re, m_s5_c_im, m_s5_d, m_s5_w_glu, m_fox_b_f, m_w_out_even, m_w_in_odd, m_pool_w, m_pool_scale, m_sgu_ln_g, m_sgu_ln_b, m_sgu_w_s, m_sgu_b_s, m_w_out_odd, m_mlp_w1, m_mlp_w2, v_mix_pre_g, v_mix_post_g, v_mlp_pre_g, v_mlp_post_g, v_w_in_even, v_s5_lam_re, v_s5_lam_im, v_s5_log_dt, v_s5_b_re, v_s5_b_im, v_s5_c_re, v_s5_c_im, v_s5_d, v_s5_w_glu, v_fox_b_f, v_w_out_even, v_w_in_odd, v_pool_w, v_pool_scale, v_sgu_ln_g, v_sgu_ln_b, v_sgu_w_s, v_sgu_b_s, v_w_out_odd, v_mlp_w1, v_mlp_w2):
    names = [n for n, _ in SMALL_SEGS] + [n for n, _, _ in BIG_SEGS] + list(SHARDED_SMALL)
    env = dict(locals())
    W = {n: env[n] for n in names}
    M = {n: env["m_" + n] for n in names}
    V = {n: env["v_" + n] for n in names}

    def shard(n):
        return W[n].reshape(-1, W[n].shape[-1]).astype(WIRE_DTYPE)

    small = jnp.pad(jnp.concatenate([W[n] for n in SHARDED_SMALL]), ((0, SUBLANES - len(SHARDED_SMALL)), (0, 0)))
    gathered, small_all = allgather_chip_shards([shard(n) for n in EARLY_NAMES], small, name="allgather_weights")
    Wf = dict(zip(EARLY_NAMES, gathered))
    for i, n in enumerate(SHARDED_SMALL):
        Wf[n] = small_all[:, i, :].reshape(1, N_CHIPS * LANES)
    for n, _ in SMALL_SEGS:
        Wf[n] = W[n]

    loss8, dx0, halves, local_small = _local_step(x[0], loss_target[0], Wf, [shard(n) for n in LATE_NAMES])
    return _reduce_and_update(W, M, V, loss8, dx0, halves, local_small)


def _reduce_to_my_half(gs, names, tag, carry_swap=None, carry_ici=None):
    cx, cy, cc = _coords()
    c_idx = cc.reshape(1).astype(jnp.int32)
    jc_idx = jnp.stack([2 * cx + cy, cc]).astype(jnp.int32)
    swap = swap_halves_exchange(gs)
    from_sibling = carry_swap(swap) if carry_swap else run_exchange(swap, name=f"{tag}_to_sibling")
    sums = [add_sibling_half(g, l, c_idx, name=f"{tag}_chip_sum_{n}") for n, g, l in zip(names, gs, from_sibling)]
    send = chip_partials_exchange([pb for _, pb in sums])
    from_chips = carry_ici(send) if carry_ici else run_exchange(send, name=f"{tag}_to_chips")
    return [add_chip_partials(pf, r, jc_idx, name=f"{tag}_sum_{n}") for n, (pf, _), r in zip(names, sums, from_chips)]


def _local_step(x0, target, P, late_shards):
    T = x0.shape[0]
    mix_pre_g, mix_post_g, mlp_pre_g, mlp_post_g = P["mix_pre_g"], P["mix_post_g"], P["mlp_pre_g"], P["mlp_post_g"]
    s5_lam_re, s5_lam_im, s5_log_dt = P["s5_lam_re"], P["s5_lam_im"], P["s5_log_dt"]
    s5_b_re, s5_b_im, s5_c_re, s5_c_im, s5_d = P["s5_b_re"], P["s5_b_im"], P["s5_c_re"], P["s5_c_im"], P["s5_d"]
    fox_b_f, pool_w, sgu_w_s, sgu_b_s = P["fox_b_f"], P["pool_w"], P["sgu_w_s"], P["sgu_b_s"]
    pool_scale_f, ln_g_f, ln_b_f = P["pool_scale"], P["sgu_ln_g"], P["sgu_ln_b"]
    w_in_e = jnp.pad(_cols_from_chips(P["w_in_even"]), ((0, 0), (0, EVEN_IN_PAD - EVEN_IN)))
    w_glu = P["s5_w_glu"].reshape(S5_WIDTH, S5_WIDTH)

    def gain(a, l):
        return a[l][None, :]

    lr = s5_lam_re[0].reshape(1, S5_LANES)
    li = s5_lam_im[0].reshape(1, S5_LANES)
    ldt = jnp.repeat(s5_log_dt[0], S5_STATE).reshape(1, S5_LANES)
    btr = s5_b_re[0].reshape(S5_LANES, S5_GROUP).T
    bti = s5_b_im[0].reshape(S5_LANES, S5_GROUP).T
    tf_re, tf_im, tb_re, tb_im, bbt_re, bbt_im = s5_disc_fwd(lr, li, ldt, btr, bti, name="s5_disc")
    same_group = (jnp.arange(S5_WIDTH)[:, None] // S5_GROUP) == (jnp.arange(S5_LANES)[None, :] // S5_STATE)
    b_bd = s5_interleave(jnp.where(same_group, jnp.tile(bbt_re, (S5_GROUPS, 1)), 0.0),
                         jnp.where(same_group, jnp.tile(bbt_im, (S5_GROUPS, 1)), 0.0), axis=1)
    cr2 = jnp.transpose(s5_c_re[0], (0, 2, 1)).reshape(S5_LANES, S5_GROUP)
    ci2 = jnp.transpose(s5_c_im[0], (0, 2, 1)).reshape(S5_LANES, S5_GROUP)
    c_bd = s5_interleave(jnp.where(same_group.T, jnp.tile(cr2, (1, S5_GROUPS)), 0.0),
                         -jnp.where(same_group.T, jnp.tile(ci2, (1, S5_GROUPS)), 0.0), axis=0)
    bf_pad = jnp.pad(fox_b_f, ((0, 0), (0, LANES - FOX_HEADS)))

    h1 = rms_fwd(x0, gain(mix_pre_g, 0), name="l0_pre_norm")
    z = matmul(h1, w_in_e, name="l0_in_proj")
    s5_tiles = dict(tm=_pick(T, S5_NB), exact_tiles=True)
    xs = s5_scan(z, b_bd, tf_re, tf_im, reverse=False, name="s5_scan_fwd")
    yc = matmul(xs, c_bd, mnk=(T, S5_WIDTH, 2 * S5_NB), tn=S5_CB, a_spec=_lanes_of_chan, b_spec=_s5_c_block,
                name="s5_cx", **s5_tiles)
    yl, yg = s5_out_fwd(yc, z, s5_d, name="s5_out")
    gl = matmul(yg, w_glu, name="s5_glu_proj")
    ycat = glu_fwd(yg, gl, out_cols=D_MODEL, name="s5_glu")
    fgate = fox_gate_fwd(z, bf_pad, fl_col=FL_TILE, name="fox_gate")
    f_col = _pairs_col(fgate, T)
    f_row = _col_to_row(f_col, T)
    (ycat, lse_col), late = fox_fwd(z, f_col, f_row, ycat, allgather_ici_exchange(late_shards), name="fox_fwd")
    late = dict(zip(LATE_NAMES, allgather_forward(late, name="allgather_late_weights")))
    w_in_o = _cols_from_chips(late["w_in_odd"])
    w_in_o = jnp.concatenate([w_in_o[:, S5_WIDTH:], w_in_o[:, :S5_WIDTH]], axis=1)
    w_out_e = late["w_out_even"].reshape(D_MODEL, D_MODEL)
    w_out_o = late["w_out_odd"].reshape(D_MODEL, D_MODEL)
    g1, g2 = late["mlp_w1"], late["mlp_w2"]
    mo = matmul(ycat, w_out_e, name="l0_out_proj")
    x1, h2 = res_norm_fwd(x0, mo, gain(mix_post_g, 0), gain(mlp_pre_g, 0), name="l0_post_mlp0_pre_norm")
    m0, mlp0 = _mlp_fwd(h2, g1, g2, 0, "mlp0")

    x2, h3 = res_norm_fwd(x1, m0, gain(mlp_post_g, 0), gain(mix_pre_g, 1), name="mlp0_post_l1_pre_norm")
    z2 = matmul(h3, w_in_o, name="l1_in_proj")
    pooled = pool_window(z2, adjoint=False, in_col=POOL_COL, name="pool_fwd")
    pw_bd = _block_diag(pool_w[0])
    pw = matmul(pooled, pw_bd, name="pool_proj")
    ycat2 = colscale_fwd(pw, pool_scale_f, out_cols=D_MODEL, name="pool_scale")
    causal = jnp.tril(jnp.ones((CHUNK, CHUNK), dtype=bool))
    wsm = jnp.where(causal[None], sgu_w_s[0], 0.0)
    wsmt = jnp.transpose(wsm, (0, 2, 1))
    bst = sgu_b_s[0].T
    ycat2 = sgu_fwd(z2, ln_g_f, ln_b_f, wsm, bst, ycat2, name="sgu_fwd")
    mo2 = matmul(ycat2, w_out_o, name="l1_out_proj")
    x3, h4 = res_norm_fwd(x2, mo2, gain(mix_post_g, 1), gain(mlp_pre_g, 1), name="l1_post_mlp1_pre_norm")
    m1, mlp1 = _mlp_fwd(h4, g1, g2, 1, "mlp1")
    loss8, dx4 = res_norm_loss(x3, m1, gain(mlp_post_g, 1), target, name="mlp1_post_norm_loss")

    dm1, dg_mlp_post1 = rms_bwd(m1, gain(mlp_post_g, 1), dx4, None, name="mlp1_post_norm_bwd")
    dh4, dg1, dg2 = _mlp_bwd(mlp1, dm1, g1, g2, 1, None, None, "mlp1")
    dx3, dmo2, dg_mlp_pre1, dg_mix_post1 = norm_res_bwd(x3, gain(mlp_pre_g, 1), dh4, dx4, mo2, gain(mix_post_g, 1),
                                                        name="mlp1_pre_l1_post_norm_bwd")
    dycat2 = matmul(dmo2, w_out_o, tb=True, name="l1_out_proj_dx")
    dw_out_o = matmul(ycat2, dmo2, ta=True, name="l1_out_proj_dw")
    dpw, dpool_scale = colscale_bwd(pw, pool_scale_f, dycat2, name="pool_scale_bwd")
    dpooled = matmul(dpw, pw_bd, tb=True, name="pool_proj_dx")
    dpw_bd = matmul(pooled, dpw, ta=True, name="pool_proj_dw")
    dz2, dln_g, dln_b, dws, dbst = sgu_bwd(z2, ln_g_f, ln_b_f, wsm, wsmt, bst, dycat2, out_cols=3 * S5_WIDTH,
                                           name="sgu_bwd")
    dz2 = pool_window(dpooled, adjoint=True, into=dz2, out_col=POOL_COL, name="pool_bwd")
    dh3 = matmul(dz2, w_in_o, tb=True, name="l1_in_proj_dx")
    dw_in_o = matmul(h3, dz2, ta=True, name="l1_in_proj_dw")
    dw_in_o = jnp.concatenate([dw_in_o[:, 2 * S5_WIDTH:], dw_in_o[:, :2 * S5_WIDTH]], axis=1)
    dx2, dm0, dg_mix_pre1, dg_mlp_post0 = norm_res_bwd(x2, gain(mix_pre_g, 1), dh3, dx3, m0, gain(mlp_post_g, 0),
                                                       name="l1_pre_mlp0_post_norm_bwd")

    dh2, dg1, dg2 = _mlp_bwd(mlp0, dm0, g1, g2, 0, dg1, dg2, "mlp0")
    dx1, dmo, dg_mlp_pre0, dg_mix_post0 = norm_res_bwd(x1, gain(mlp_pre_g, 0), dh2, dx2, mo, gain(mix_post_g, 0),
                                                       name="mlp0_pre_l0_post_norm_bwd")
    dycat = matmul(dmo, w_out_e, tb=True, name="l0_out_proj_dx")
    dw_out_e = matmul(ycat, dmo, ta=True, name="l0_out_proj_dw")
    dyg_a, dgl = glu_bwd(yg, gl, dycat, name="s5_glu_bwd")
    dyg_b = matmul(dgl, w_glu, tb=True, name="s5_glu_proj_dx")
    dw_glu = matmul(yg, dgl, ta=True, name="s5_glu_proj_dw")
    dyl, du_skip, dd = s5_out_bwd(yl, z, s5_d, dyg_a, dyg_b, name="s5_out_bwd")
    dc_blocks = matmul(xs, dyl, ta=True, mnk=(2 * S5_LANES, S5_CB, T), tm=S5_NB, tn=S5_CB, b_spec=_chan_cols_of_i,
                       exact_tiles=True, name="s5_cx_dw")
    early_grads = {"s5_w_glu": dw_glu.reshape(N_CHIPS, -1, S5_WIDTH), "w_out_even": dw_out_e.reshape(N_CHIPS, -1, D_MODEL),
                   "w_in_odd": _chips_from_cols(dw_in_o), "w_out_odd": dw_out_o.reshape(N_CHIPS, -1, D_MODEL),
                   "mlp_w1": dg1, "mlp_w2": dg2}
    got = {}

    def reverse_scan(exchange):
        (got["lam"], got["dab_re"], got["dab_im"]), bufs = s5_scan(dyl, c_bd, tb_re, tb_im, reverse=True, states=xs,
                                                                   hosted=exchange, name="s5_scan_bwd")
        return bufs

    def attention_bwd(exchange):
        dd_col = fox_dd(ycat, dycat, name="fox_dd")
        (got["dk"], got["dv"], got["dfk"], got["dqt"], got["dfq"]), bufs = fox_bwd(
            z, dycat, f_col, f_row, _col_to_row(lse_col, T), _col_to_row(dd_col, T), exchange, name="fox_bwd")
        return bufs

    halves = _reduce_to_my_half([early_grads[n] for n in REDUCED_EARLY], REDUCED_EARLY, "early_grads",
                                reverse_scan, attention_bwd)
    lam, dab_re, dab_im, dk, dv = got["lam"], got["dab_re"], got["dab_im"], got["dk"], got["dv"]
    db_blocks = matmul(z, lam, ta=True, mnk=(S5_CB, 2 * S5_LANES, T), tm=S5_CB, tn=S5_NB, a_spec=_chan_rows_t,
                       exact_tiles=True, name="s5_bu_dw")
    du_b = matmul(lam, b_bd, tb=True, mnk=(T, S5_WIDTH, 2 * S5_NB), tn=S5_CB, a_spec=_lanes_of_chan,
                  b_spec=_s5_b_block_t, name="s5_bu_dx", **s5_tiles)
    du = add2(du_skip, du_b, name="s5_du")
    dq = jnp.transpose(got["dqt"], (1, 3, 0, 2)).reshape(T, FOX_WIDTH) * (FOX_HEAD_DIM ** -0.5)
    dfl, dbf = fox_gate_bwd(z, bf_pad, _pairs_to_lanes(got["dfk"], T), _pairs_to_lanes(_row_to_col(got["dfq"], T), T),
                            fl_col=FL_TILE, name="fox_gate_bwd")
    dz = jnp.concatenate([du, dq, dk, dv, dfl], axis=1)
    dw_in_e = matmul(h1, dz, ta=True, name="l0_in_proj_dw")[:, :EVEN_IN]

    def in_proj_dx(exchange):
        got["dh1"], bufs = matmul(dz, w_in_e, tb=True, hosted=exchange, name="l0_in_proj_dx")
        return bufs

    def pre_norm_bwd(exchange):
        (got["dx0"], got["dg_mix_pre0"]), bufs = rms_bwd(x0, gain(mix_pre_g, 0), got["dh1"], dx1, hosted=exchange,
                                                         name="l0_pre_norm_bwd")
        return bufs

    halves = halves + _reduce_to_my_half([_chips_from_cols(dw_in_e)], ["w_in_even"], "late_grads", in_proj_dx, pre_norm_bwd)
    dx0, dg_mix_pre0 = got["dx0"], got["dg_mix_pre0"]

    groups_per_block = S5_CB // S5_GROUP
    own_group = (jnp.arange(S5_CB)[:, None] // S5_GROUP) == ((jnp.arange(S5_LANES)[None, :] // S5_STATE) % groups_per_block)
    db_re, db_im = s5_deinterleave(db_blocks, axis=1)
    dbbt_re = jnp.where(own_group, db_re, 0.0).reshape(groups_per_block, S5_GROUP, S5_LANES).sum(0)
    dbbt_im = jnp.where(own_group, db_im, 0.0).reshape(groups_per_block, S5_GROUP, S5_LANES).sum(0)
    dlr, dli, dldt8, dbtr, dbti = s5_disc_bwd(lr, li, ldt, btr, bti, dab_re, dab_im, dbbt_re, dbbt_im, name="s5_disc_bwd")
    dc_re, dc_im = s5_deinterleave(dc_blocks, axis=0)
    dcr2 = jnp.where(own_group.T, dc_re, 0.0).reshape(S5_LANES, groups_per_block, S5_GROUP).sum(1)
    dci2 = -jnp.where(own_group.T, dc_im, 0.0).reshape(S5_LANES, groups_per_block, S5_GROUP).sum(1)

    def c_layout(a):
        return jnp.transpose(a.reshape(S5_GROUPS, S5_STATE, S5_GROUP), (0, 2, 1))[None]

    def b_layout(a):
        return a.T.reshape(1, S5_GROUPS, S5_STATE, S5_GROUP)

    local_small = {
        "mix_pre_g": jnp.concatenate([dg_mix_pre0, dg_mix_pre1]), "mix_post_g": jnp.concatenate([dg_mix_post0, dg_mix_post1]),
        "mlp_pre_g": jnp.concatenate([dg_mlp_pre0, dg_mlp_pre1]), "mlp_post_g": jnp.concatenate([dg_mlp_post0, dg_mlp_post1]),
        "s5_lam_re": dlr.reshape(1, S5_GROUPS, S5_STATE), "s5_lam_im": dli.reshape(1, S5_GROUPS, S5_STATE),
        "s5_log_dt": dldt8[0:1, 0:S5_GROUPS],
        "s5_b_re": b_layout(dbtr), "s5_b_im": b_layout(dbti), "s5_c_re": c_layout(dcr2), "s5_c_im": c_layout(dci2),
        "s5_d": dd, "fox_b_f": dbf[:, 0:FOX_HEADS],
        "pool_w": _diag_blocks(dpw_bd, len(POOL_WINDOWS))[None],
        "sgu_w_s": jnp.where(causal[None], dws, 0.0)[None], "sgu_b_s": dbst.T[None],
        "pool_scale": dpool_scale, "sgu_ln_g": dln_g, "sgu_ln_b": dln_b,
    }
    return loss8, dx0, dict(zip(REDUCED_EARLY + ("w_in_even",), halves)), local_small


def _reduce_and_update(W, M, V, loss8, dx0, halves, local_small):
    cx, cy, cc = _coords()
    chip = 2 * cx + cy

    summed = dict(local_small, loss=loss8[0:1, 0:1])
    vec = _pack_vec(summed, REDUCED_SEGS, N_DEV * SUBLANES)
    piece = vec.shape[0] // N_DEV
    landed = exchange_pieces(vec.reshape(N_DEV, piece, LANES), scatter=True, name="small_grads_scatter")
    mine = sum_pieces(landed, name="small_grads_sum")
    everyone = exchange_pieces(mine, scatter=False, name="small_grads_gather")
    G = _unpack_vec(everyone, REDUCED_SEGS)
    loss = G["loss"].reshape(())
    for n in SHARDED_SMALL:
        G[n] = lax.dynamic_slice_in_dim(G[n], chip * LANES, LANES, axis=1)

    reduced = join_sibling_halves([halves[n] for n in BIG_NAMES], name="big_grads_join")
    for n, r in zip(BIG_NAMES, reduced):
        G[n] = r.reshape(W[n].shape)

    def two_d(a):
        return a.reshape(-1, a.shape[-1])

    delta, new_m, new_v = {}, {}, {}
    for n in BIG_NAMES:
        d_, m_, v_ = adamw(two_d(W[n]), two_d(G[n]), two_d(M[n]), two_d(V[n]), name=f"adamw_{n}")
        delta[n], new_m[n], new_v[n] = (t.reshape(W[n].shape) for t in (d_, m_, v_))
    packed = [_pack_vec(src, SMALL_SEGS, SUBLANES) for src in (W, G, M, V)]
    outs = adamw(*packed, name="adamw_replicated")
    for dst, t in zip((delta, new_m, new_v), outs):
        dst.update(_unpack_vec(t, SMALL_SEGS))
    sharded_segs = tuple((n, (1, LANES)) for n in SHARDED_SMALL)
    packed = [_pack_vec(src, sharded_segs, 1) for src in (W, G, M, V)]
    outs = adamw(*packed, name="adamw_sharded_vectors")
    for dst, t in zip((delta, new_m, new_v), outs):
        dst.update(_unpack_vec(t, sharded_segs))

    order = ["mix_pre_g", "mix_post_g", "mlp_pre_g", "mlp_post_g", "w_in_even", "s5_lam_re", "s5_lam_im", "s5_log_dt",
             "s5_b_re", "s5_b_im", "s5_c_re", "s5_c_im", "s5_d", "s5_w_glu", "fox_b_f", "w_out_even", "w_in_odd",
             "pool_w", "pool_scale", "sgu_ln_g", "sgu_ln_b", "sgu_w_s", "sgu_b_s", "w_out_odd", "mlp_w1", "mlp_w2"]
    return (loss, dx0[None], *[G[n] for n in order], *[delta[n] for n in order],
            *[new_m[n] for n in order], *[new_v[n] for n in order])
```

```python
import functools
import math

import jax
import jax.numpy as jnp
from jax import lax
from jax.experimental import pallas as pl
from jax.experimental.pallas import tpu as pltpu

F32 = jnp.float32
MXU_DTYPE = jnp.bfloat16
WIRE_DTYPE = jnp.bfloat16
EPS = 1e-6
VMEM_LIMIT_BYTES = 48 * 1024 * 1024
LANES = 128
SUBLANES = 8

D_MODEL = 1024
S5_WIDTH = 512
S5_GROUP = 16
S5_GROUPS = 32
S5_STATE = 64
S5_LANES = S5_GROUPS * S5_STATE
FOX_HEADS = 8
FOX_HEAD_DIM = 64
FOX_WIDTH = 512
EVEN_IN = S5_WIDTH + 3 * FOX_WIDTH + FOX_HEADS
EVEN_IN_PAD = 2176
POOL_WINDOWS = (2, 4, 8, 16)
POOL_HALO = 16
POOL_GROUP_DIM = 128
SGU_GROUPS = 4
SGU_GROUP_DIM = 128
CHUNK = 128
D_FF = 4096

ADAM_LR = 0.001
ADAM_B1 = 0.9
ADAM_B2 = 0.999
ADAM_EPS = 1e-08
ADAM_WD = 0.01
ADAM_STEP = 10

MESH_AXES = ("x", "y", "c")
MESH = pl.DeviceIdType.MESH
N_CHIPS = 4
N_DEV = 8

SDS = jax.ShapeDtypeStruct


def _cp(*sem):
    return pltpu.CompilerParams(dimension_semantics=sem, vmem_limit_bytes=VMEM_LIMIT_BYTES)


def _pick(dim, pref):
    if dim <= pref:
        return dim
    t = pref
    while t >= 256:
        if dim % t == 0:
            return t
        t //= 2
    return dim


def _row(tr, c):
    return pl.BlockSpec((tr, c), lambda i: (i, 0))


def _full(shape):
    nd = len(shape)
    return pl.BlockSpec(shape, lambda *_: (0,) * nd)


def _gelu_grad(x):
    c = math.sqrt(2.0 / math.pi)
    t = jnp.tanh(c * (x + 0.044715 * x * x * x))
    return 0.5 * (1.0 + t) + 0.5 * x * (1.0 - t * t) * c * (1.0 + 3.0 * 0.044715 * x * x)


MATMUL_VMEM_BYTES = 36 * 1024 * 1024


def matmul(a, b, *, name, ta=False, tb=False, out_dtype=F32, tm=2048, tn=1024, tk=4096, mnk=None, a_koff=0,
           a_spec=None, b_spec=None, o_spec=None, o_shape=None, prev=None, epi=None, epi_in=(), out_dtypes=None,
           exact_tiles=False, hosted=None):
    if mnk is None:
        M, K = (a.shape[1], a.shape[0]) if ta else a.shape
        K2, N = (b.shape[1], b.shape[0]) if tb else b.shape
        assert K == K2, (a.shape, b.shape, ta, tb)
    else:
        M, N, K = mnk
    out_dtypes = tuple(out_dtypes) if out_dtypes is not None else (out_dtype,)
    n_out, n_epi = len(out_dtypes), len(epi_in)
    tm, tn, tk = _pick(M, tm), _pick(N, tn), _pick(K, tk)

    def vmem_bytes(tm_, tn_, tk_):
        tiles = tm_ * tk_ * a.dtype.itemsize + tk_ * tn_ * b.dtype.itemsize
        tiles += tm_ * tn_ * (sum(jnp.dtype(d).itemsize for d in out_dtypes) + sum(e.dtype.itemsize for e in epi_in))
        return 2 * tiles + tm_ * tn_ * 4 * (tk_ < K)

    def halves(t, dim):
        return [t] + ([t // 2] if t % (2 * LANES) == 0 and t // 2 >= 512 and dim % (t // 2) == 0 else [])

    if exact_tiles:
        halves = lambda t, dim: [t]
    fits = [(m_, n_) for m_ in halves(tm, M) for n_ in halves(tn, N) if vmem_bytes(m_, n_, tk) <= MATMUL_VMEM_BYTES]
    if fits:
        tm, tn = max(fits, key=lambda t: (t[0] * t[1], t[0]))
    else:
        tm, tn = halves(tm, M)[-1], halves(tn, N)[-1]
        while vmem_bytes(tm, tn, tk) > MATMUL_VMEM_BYTES and tk % 2 == 0 and tk > 512:
            tk //= 2
    nk = K // tk
    assert a_koff % tk == 0 and not (ta and a_koff)
    ko = a_koff // tk
    dn = (((0 if ta else 1,), (1 if tb else 0,)), ((), ()))

    def body(*refs):
        a_ref, b_ref = refs[0], refs[1]
        epi_refs = refs[2:2 + n_epi]
        o_refs = refs[len(refs) - n_out - (nk > 1):len(refs) - (nk > 1)]
        k = pl.program_id(2)
        bv = b_ref[...]
        if bv.ndim == 3 and tb:
            cw = bv.shape[-1]
            prod = sum(lax.dot_general(a_ref[:, c * cw:(c + 1) * cw].astype(MXU_DTYPE), bv[c].astype(MXU_DTYPE), dn,
                                       preferred_element_type=F32) for c in range(bv.shape[0]))
        else:
            if bv.ndim == 3:
                bv = bv.reshape(-1, bv.shape[-1])
            prod = lax.dot_general(a_ref[...].astype(MXU_DTYPE), bv.astype(MXU_DTYPE), dn, preferred_element_type=F32)

        def finish(acc):
            res = (acc,) if epi is None else epi(acc, *[r[...] for r in epi_refs])
            for o_ref, r in zip(o_refs, res):
                o_ref[...] = r.astype(o_ref.dtype)

        if nk == 1:
            finish(prod)
            return
        acc_ref = refs[-1]

        @pl.when(k == 0)
        def _():
            acc_ref[...] = prod

        @pl.when(jnp.logical_and(k > 0, k < nk - 1))
        def _():
            acc_ref[...] += prod

        @pl.when(k == nk - 1)
        def _():
            finish(acc_ref[...] + prod)

    if a_spec is None:
        a_spec = pl.BlockSpec((tk, tm), lambda i, j, k: (k, i)) if ta else pl.BlockSpec((tm, tk), lambda i, j, k: (i, k + ko))
    else:
        a_spec = a_spec(tm, tn, tk)
    if b_spec is None:
        bs = pl.BlockSpec((tn, tk), lambda i, j, k: (j, k)) if tb else pl.BlockSpec((tk, tn), lambda i, j, k: (k, j))
    else:
        bs = b_spec(tm, tn, tk)
    tile = pl.BlockSpec((tm, tn), lambda i, j, k: (i, j))
    os_ = tile if o_spec is None else o_spec(tm, tn, tk)
    ins, in_specs, aliases = [a, b, *epi_in], [a_spec, bs] + [tile] * n_epi, {}
    if prev is not None:
        aliases = {len(ins): 0}
        ins.append(prev)
        in_specs.append(pl.BlockSpec(memory_space=pl.ANY))
    shapes = [SDS((M, N) if o_shape is None else o_shape, dt) for dt in out_dtypes]
    scratch = [pltpu.VMEM((tm, tn), F32)] if nk > 1 else []
    if hosted is not None:
        outs, bufs = call_hosting(body, hosted, name=name, grid=(M // tm, N // tn, nk), in_specs=in_specs,
                                  out_specs=[os_] * n_out, out_shape=shapes, inputs=ins, aliases=aliases, scratch=scratch)
        return (outs[0] if n_out == 1 else outs), bufs
    outs = pl.pallas_call(
        body, name=name, grid=(M // tm, N // tn, nk),
        in_specs=in_specs, out_specs=[os_] * n_out, out_shape=shapes, input_output_aliases=aliases,
        scratch_shapes=scratch, compiler_params=_cp("parallel", "parallel", "arbitrary"),
    )(*ins)
    return outs[0] if n_out == 1 else outs


def _rms_hat(x):
    return x * lax.rsqrt(jnp.mean(x * x, axis=-1, keepdims=True) + EPS)


def rms_fwd(x, g, *, name):
    T, D = x.shape
    tr = _pick(T, 512)

    def body(x_ref, g_ref, o_ref):
        o_ref[...] = (_rms_hat(x_ref[...]) * g_ref[...]).astype(o_ref.dtype)

    return pl.pallas_call(body, name=name, grid=(T // tr,), in_specs=[_row(tr, D), _full((1, D))],
                          out_specs=_row(tr, D), out_shape=SDS((T, D), MXU_DTYPE), compiler_params=_cp("parallel"))(x, g)


def res_norm_fwd(x, y, g_post, g_next, *, name):
    T, D = x.shape
    tr = _pick(T, 512)

    def body(x_ref, y_ref, gp_ref, gn_ref, o_ref, h_ref):
        xn = x_ref[...] + _rms_hat(y_ref[...]) * gp_ref[...]
        o_ref[...] = xn
        h_ref[...] = (_rms_hat(xn) * gn_ref[...]).astype(h_ref.dtype)

    return pl.pallas_call(body, name=name, grid=(T // tr,),
                          in_specs=[_row(tr, D), _row(tr, D), _full((1, D)), _full((1, D))],
                          out_specs=[_row(tr, D), _row(tr, D)], out_shape=[SDS((T, D), F32), SDS((T, D), MXU_DTYPE)],
                          compiler_params=_cp("parallel"))(x, y, g_post, g_next)


def res_norm_loss(x, y, g_post, target, *, name):
    T, D = x.shape
    tr = _pick(T, 512)

    def body(x_ref, y_ref, g_ref, t_ref, l_ref, d_ref):
        err = x_ref[...] + _rms_hat(y_ref[...]) * g_ref[...] - t_ref[...]
        d_ref[...] = err * (1.0 / D)

        @pl.when(pl.program_id(0) == 0)
        def _():
            l_ref[...] = jnp.zeros_like(l_ref)

        l_ref[...] += 0.5 * jnp.sum(jnp.mean(err * err, axis=-1, keepdims=True))

    return pl.pallas_call(body, name=name, grid=(T // tr,),
                          in_specs=[_row(tr, D), _row(tr, D), _full((1, D)), _row(tr, D)],
                          out_specs=[_full((SUBLANES, LANES)), _row(tr, D)],
                          out_shape=[SDS((SUBLANES, LANES), F32), SDS((T, D), F32)],
                          compiler_params=_cp("arbitrary"))(x, y, g_post, target)


def _rms_bwd_rows(x, g, dy):
    r = lax.rsqrt(jnp.mean(x * x, axis=-1, keepdims=True) + EPS)
    xh = x * r
    dxh = dy * g
    return r * (dxh - xh * jnp.mean(dxh * xh, axis=-1, keepdims=True)), jnp.sum(dy * xh, axis=0, keepdims=True)


def norm_res_bwd(x, g_pre, dh, res, y, g_post, *, name):
    T, D = x.shape
    tr = _pick(T, 512)

    def body(x_ref, gp_ref, dh_ref, res_ref, y_ref, gy_ref, dx_ref, dy_ref, dgp_ref, dgy_ref):
        dx, dgp = _rms_bwd_rows(x_ref[...], gp_ref[...], dh_ref[...])
        dx = dx + res_ref[...]
        dx_ref[...] = dx
        dy, dgy = _rms_bwd_rows(y_ref[...], gy_ref[...], dx)
        dy_ref[...] = dy.astype(dy_ref.dtype)

        @pl.when(pl.program_id(0) == 0)
        def _():
            dgp_ref[...] = jnp.zeros_like(dgp_ref)
            dgy_ref[...] = jnp.zeros_like(dgy_ref)

        dgp_ref[...] += dgp
        dgy_ref[...] += dgy

    row, vec = _row(tr, D), _full((1, D))
    return pl.pallas_call(body, name=name, grid=(T // tr,), in_specs=[row, vec, row, row, row, vec],
                          out_specs=[row, row, vec, vec],
                          out_shape=[SDS((T, D), F32), SDS((T, D), MXU_DTYPE), SDS((1, D), F32), SDS((1, D), F32)],
                          compiler_params=_cp("arbitrary"))(x, g_pre, dh, res, y, g_post)


def rms_bwd(x, g, dy, res, *, name, hosted=None):
    T, D = x.shape
    tr = _pick(T, 512)
    has_res = res is not None

    def body(*refs):
        if has_res:
            x_ref, g_ref, dy_ref, res_ref, dx_ref, dg_ref = refs
        else:
            x_ref, g_ref, dy_ref, dx_ref, dg_ref = refs
        dx, dg = _rms_bwd_rows(x_ref[...], g_ref[...], dy_ref[...])
        if has_res:
            dx = dx + res_ref[...]
        dx_ref[...] = dx.astype(dx_ref.dtype)

        @pl.when(pl.program_id(0) == 0)
        def _():
            dg_ref[...] = jnp.zeros_like(dg_ref)

        dg_ref[...] += dg

    ins = [x, g, dy] + ([res] if has_res else [])
    in_specs = [_row(tr, D), _full((1, D)), _row(tr, D)] + ([_row(tr, D)] if has_res else [])
    out_shape = [SDS((T, D), F32 if has_res else MXU_DTYPE), SDS((1, D), F32)]
    out_specs = [_row(tr, D), _full((1, D))]
    if hosted is not None:
        return call_hosting(body, hosted, name=name, grid=(T // tr,), in_specs=in_specs, out_specs=out_specs,
                            out_shape=out_shape, inputs=ins, aliases={})
    return pl.pallas_call(body, name=name, grid=(T // tr,), in_specs=in_specs, out_specs=out_specs,
                          out_shape=out_shape, compiler_params=_cp("arbitrary"))(*ins)


def _s5_disc(lr, li, ldt, btr, bti):
    dt = jnp.exp(ldt)
    k = lax.broadcasted_iota(jnp.int32, (SUBLANES, S5_LANES), 0).astype(F32)
    kf = k + 1.0
    kb = 8.0 - k
    ph = li * dt
    lm = lr * dt
    tf_re = jnp.exp(kf * lm) * jnp.cos(kf * ph)
    tf_im = jnp.exp(kf * lm) * jnp.sin(kf * ph)
    tb_re = jnp.exp(kb * lm) * jnp.cos(kb * ph)
    tb_im = -jnp.exp(kb * lm) * jnp.sin(kb * ph)
    mag = jnp.exp(lm)
    ab_re = mag * jnp.cos(ph)
    ab_im = mag * jnp.sin(ph)
    den = lr * lr + li * li
    nr = ab_re - 1.0
    ni = ab_im
    q_re = (nr * lr + ni * li) / den
    q_im = (ni * lr - nr * li) / den
    bbt_re = q_re * btr - q_im * bti
    bbt_im = q_re * bti + q_im * btr
    return tf_re, tf_im, tb_re, tb_im, bbt_re, bbt_im


def _s5_disc_core(lr, li, ldt, btr, bti):
    dt = jnp.exp(ldt)
    mag = jnp.exp(lr * dt)
    ab_re = mag * jnp.cos(li * dt)
    ab_im = mag * jnp.sin(li * dt)
    den = lr * lr + li * li
    nr = ab_re - 1.0
    ni = ab_im
    q_re = (nr * lr + ni * li) / den
    q_im = (ni * lr - nr * li) / den
    return ab_re, ab_im, q_re * btr - q_im * bti, q_re * bti + q_im * btr


def s5_disc_fwd(lr, li, ldt, btr, bti, *, name):
    def body(lr_ref, li_ref, ldt_ref, btr_ref, bti_ref, *outs):
        vals = _s5_disc(lr_ref[...], li_ref[...], ldt_ref[...], btr_ref[...], bti_ref[...])
        for o, v in zip(outs, vals):
            o[...] = v

    tab = SDS((SUBLANES, S5_LANES), F32)
    bb = SDS((S5_GROUP, S5_LANES), F32)
    return pl.pallas_call(body, name=name, out_shape=[tab, tab, tab, tab, bb, bb])(lr, li, ldt, btr, bti)


def s5_disc_bwd(lr, li, ldt, btr, bti, dab_re, dab_im, dbbt_re, dbbt_im, *, name):
    def body(lr_ref, li_ref, ldt_ref, btr_ref, bti_ref, dar_ref, dai_ref, dbr_ref, dbi_ref,
             dlr_ref, dli_ref, dldt_ref, dbtr_ref, dbti_ref):
        _, vjp = jax.vjp(_s5_disc_core, lr_ref[...], li_ref[...], ldt_ref[...], btr_ref[...], bti_ref[...])
        dlr, dli, dldt, dbtr, dbti = vjp((dar_ref[...], dai_ref[...], dbr_ref[...], dbi_ref[...]))
        dlr_ref[...] = dlr
        dli_ref[...] = dli
        dbtr_ref[...] = dbtr
        dbti_ref[...] = dbti
        lane_group = lax.broadcasted_iota(jnp.int32, (S5_LANES, LANES), 0) // S5_STATE
        col = lax.broadcasted_iota(jnp.int32, (S5_LANES, LANES), 1)
        ind = (lane_group == col).astype(F32)
        dldt_ref[...] = jnp.dot(jnp.broadcast_to(dldt, (SUBLANES, S5_LANES)), ind,
                                precision=lax.Precision.HIGHEST, preferred_element_type=F32)

    row = SDS((1, S5_LANES), F32)
    bb = SDS((S5_GROUP, S5_LANES), F32)
    return pl.pallas_call(body, name=name, out_shape=[row, row, SDS((SUBLANES, LANES), F32), bb, bb])(
        lr, li, ldt, btr, bti, dab_re, dab_im, dbbt_re, dbbt_im)


S5_NB = 1024


S5_CB = S5_WIDTH * S5_NB // S5_LANES


def _chan_rows_t(tm, tn, tk):
    return pl.BlockSpec((tk, S5_CB), lambda i, j, k: (k, j // 2))


def _chan_cols_of_i(tm, tn, tk):
    return pl.BlockSpec((tk, S5_CB), lambda i, j, k: (k, i // 2))


def s5_interleave(re, im, axis):
    parts = []
    for n in range(S5_LANES // S5_NB):
        sl = [slice(None)] * re.ndim
        sl[axis] = slice(n * S5_NB, (n + 1) * S5_NB)
        parts += [re[tuple(sl)], im[tuple(sl)]]
    return jnp.concatenate(parts, axis=axis)


def s5_deinterleave(a, axis):
    re, im = [], []
    for n in range(S5_LANES // S5_NB):
        sl = [slice(None)] * a.ndim
        sl[axis] = slice(2 * n * S5_NB, (2 * n + 1) * S5_NB)
        re.append(a[tuple(sl)])
        sl[axis] = slice((2 * n + 1) * S5_NB, (2 * n + 2) * S5_NB)
        im.append(a[tuple(sl)])
    return jnp.concatenate(re, axis=axis), jnp.concatenate(im, axis=axis)


def s5_scan(src, mat, tab_re, tab_im, back, *, reverse, name, states=None, hosted=None):
    T = src.shape[0]
    nb = S5_NB
    tc = _pick(T, 256)
    nl = S5_LANES // nb
    nt = T // tc
    ntile = tc // SUBLANES
    with_da = states is not None
    assert reverse or not with_da
    step_rows = ((1, 7), (2, 6), (4, 4)) if reverse else ((1, 0), (2, 1), (4, 3))
    drive_dn = _NT if reverse else (((1,), (0,)), ((), ()))

    def body(*refs):
        if with_da:
            (src_ref, wr_ref, wi_ref, tr_ref, ti_ref, wo_ref, sr_ref, si_ref, hr_ref, hi_ref, xo_ref, y_ref, dar_ref,
             dai_ref, cr_ref, ci_ref, mr_ref, mi_ref, br_ref, bi_ref, ar_ref, ai_ref) = refs
        else:
            (src_ref, wr_ref, wi_ref, tr_ref, ti_ref, wo_ref, xo_ref, y_ref,
             cr_ref, ci_ref, mr_ref, mi_ref, br_ref, bi_ref) = refs

        @pl.when(pl.program_id(1) == 0)
        def _():
            cr_ref[...] = jnp.zeros_like(cr_ref)
            ci_ref[...] = jnp.zeros_like(ci_ref)
            if with_da:
                ar_ref[...] = jnp.zeros_like(ar_ref)
                ai_ref[...] = jnp.zeros_like(ai_ref)

        lhs = src_ref[...].astype(MXU_DTYPE)
        br_ref[...] = lax.dot_general(lhs, wr_ref[...].astype(MXU_DTYPE), drive_dn, preferred_element_type=F32)
        bi_ref[...] = lax.dot_general(lhs, wi_ref[...].astype(MXU_DTYPE), drive_dn, preferred_element_type=F32)

        seen = jnp.where(pl.program_id(1) < nt - 1, 1.0, 0.0)

        def add_da(lr, li, r0, last_r, last_i):
            first = lax.broadcasted_iota(jnp.int32, (SUBLANES, nb), 0) == 0
            pr = jnp.where(first, last_r, pltpu.roll(sr_ref[pl.ds(r0, SUBLANES), :], 1, 0))
            pi = jnp.where(first, last_i, pltpu.roll(si_ref[pl.ds(r0, SUBLANES), :], 1, 0))
            ar_ref[...] += lr * pr + li * pi
            ai_ref[...] += li * pr - lr * pi

        io = lax.broadcasted_iota(jnp.int32, (SUBLANES, nb), 0)
        for s_, (d, r) in enumerate(step_rows):
            keep = (io < SUBLANES - d) if reverse else (io >= d)
            mr_ref[s_] = jnp.where(keep, tr_ref[r:r + 1, :], 0.0)
            mi_ref[s_] = jnp.where(keep, ti_ref[r:r + 1, :], 0.0)

        def tile(i, carry):
            cr, ci = carry
            j = (ntile - 1 - i) if reverse else i
            r0 = pl.multiple_of(j * SUBLANES, SUBLANES)
            xr = br_ref[pl.ds(r0, SUBLANES), :]
            xi = bi_ref[pl.ds(r0, SUBLANES), :]
            for s_, (d, _) in enumerate(step_rows):
                sh = (SUBLANES - d) if reverse else d
                sr = pltpu.roll(xr, sh, 0)
                si = pltpu.roll(xi, sh, 0)
                pr, pi = mr_ref[s_], mi_ref[s_]
                xr, xi = xr + pr * sr - pi * si, xi + pr * si + pi * sr
            tr, ti = tr_ref[...], ti_ref[...]
            xr, xi = xr + tr * cr - ti * ci, xi + tr * ci + ti * cr
            xo_ref[pl.ds(r0, SUBLANES), 0:nb] = xr
            xo_ref[pl.ds(r0, SUBLANES), nb:2 * nb] = xi
            if with_da:
                @pl.when(j > 0)
                def _():
                    p0 = pl.multiple_of(r0 - SUBLANES, SUBLANES)
                    add_da(xr, xi, r0, sr_ref[pl.ds(p0, SUBLANES), :][SUBLANES - 1:SUBLANES, :],
                           si_ref[pl.ds(p0, SUBLANES), :][SUBLANES - 1:SUBLANES, :])

                @pl.when(j == 0)
                def _():
                    add_da(xr, xi, r0, hr_ref[SUBLANES - 1:SUBLANES, :] * seen, hi_ref[SUBLANES - 1:SUBLANES, :] * seen)
            if reverse:
                return xr[0:1, :], xi[0:1, :]
            return xr[SUBLANES - 1:SUBLANES, :], xi[SUBLANES - 1:SUBLANES, :]

        cr, ci = lax.fori_loop(0, ntile, tile, (cr_ref[0:1, :], ci_ref[0:1, :]))
        cr_ref[0:1, :] = cr
        ci_ref[0:1, :] = ci
        y_ref[...] = lax.dot_general(xo_ref[...].astype(MXU_DTYPE), wo_ref[...].astype(MXU_DTYPE), drive_dn,
                                     preferred_element_type=F32)
        if with_da:
            @pl.when(pl.program_id(1) == nt - 1)
            def _():
                dar_ref[...] = jnp.sum(ar_ref[...], axis=0, keepdims=True)
                dai_ref[...] = jnp.sum(ai_ref[...], axis=0, keepdims=True)

    def tmap(t):
        return (nt - 1 - t) if reverse else t

    hb = tc // SUBLANES
    re_spec = pl.BlockSpec((tc, nb), lambda n, t: (tmap(t), 2 * n))
    im_spec = pl.BlockSpec((tc, nb), lambda n, t: (tmap(t), 2 * n + 1))
    tab_spec = pl.BlockSpec((SUBLANES, nb), lambda n, t: (0, n))
    out_spec = pl.BlockSpec((tc, 2 * nb), lambda n, t: (tmap(t), n))
    out_shape = SDS((T, 2 * S5_LANES), F32)
    scratch = [pltpu.VMEM((SUBLANES, nb), F32), pltpu.VMEM((SUBLANES, nb), F32),
               pltpu.VMEM((len(step_rows), SUBLANES, nb), F32), pltpu.VMEM((len(step_rows), SUBLANES, nb), F32),
               pltpu.VMEM((tc, nb), F32), pltpu.VMEM((tc, nb), F32)]
    src_spec = pl.BlockSpec((tc, S5_CB), lambda n, t: (tmap(t), n))
    if reverse:
        wr_spec = pl.BlockSpec((nb, S5_CB), lambda n, t: (2 * n, n))
        wi_spec = pl.BlockSpec((nb, S5_CB), lambda n, t: (2 * n + 1, n))
    else:
        wr_spec = pl.BlockSpec((S5_CB, nb), lambda n, t: (n, 2 * n))
        wi_spec = pl.BlockSpec((S5_CB, nb), lambda n, t: (n, 2 * n + 1))
    wo_spec = pl.BlockSpec((S5_CB, 2 * nb) if reverse else (2 * nb, S5_CB), lambda n, t: (n, n))
    drive_specs = [src_spec, wr_spec, wi_spec, tab_spec, tab_spec, wo_spec]
    drive = [src, mat, mat, tab_re, tab_im, back]
    y_shape = SDS((T, S5_WIDTH), F32)
    if not with_da:
        return pl.pallas_call(body, name=name, grid=(nl, nt), in_specs=drive_specs,
                              out_specs=[out_spec, src_spec], out_shape=[out_shape, y_shape], scratch_shapes=scratch,
                              compiler_params=_cp("parallel", "arbitrary"))(*drive)
    re_halo = pl.BlockSpec((SUBLANES, nb), lambda n, t: (jnp.maximum(tmap(t) * hb - 1, 0), 2 * n))
    im_halo = pl.BlockSpec((SUBLANES, nb), lambda n, t: (jnp.maximum(tmap(t) * hb - 1, 0), 2 * n + 1))
    acc = pl.BlockSpec((1, nb), lambda n, t: (0, n))
    row = SDS((1, S5_LANES), F32)
    return call_hosting(
        body, hosted, name=name, grid=(nl, nt),
        in_specs=drive_specs + [re_spec, im_spec, re_halo, im_halo],
        out_specs=[out_spec, src_spec, acc, acc], out_shape=[out_shape, y_shape, row, row],
        inputs=drive + [states, states, states, states], aliases={},
        scratch=scratch + [pltpu.VMEM((SUBLANES, nb), F32), pltpu.VMEM((SUBLANES, nb), F32)])


def s5_out_fwd(yc, u, d, *, name):
    T, C = yc.shape
    tr = _pick(T, 512)

    def body(yc_ref, u_ref, d_ref, yl_ref, yg_ref):
        yl = yc_ref[...] + d_ref[...] * u_ref[...]
        yl_ref[...] = yl
        yg_ref[...] = jax.nn.gelu(yl)

    return pl.pallas_call(body, name=name, grid=(T // tr,), in_specs=[_row(tr, C), _row(tr, C), _full((1, C))],
                          out_specs=[_row(tr, C)] * 2, out_shape=[SDS((T, C), F32)] * 2,
                          compiler_params=_cp("parallel"))(yc, u, d)


def glu_fwd(yg, gl, *, out_cols, name):
    T, C = yg.shape
    tr = _pick(T, 512)

    def body(yg_ref, gl_ref, o_ref):
        o_ref[...] = yg_ref[...] * jax.nn.sigmoid(gl_ref[...])

    return pl.pallas_call(body, name=name, grid=(T // tr,), in_specs=[_row(tr, C)] * 2, out_specs=_row(tr, C),
                          out_shape=SDS((T, out_cols), F32), compiler_params=_cp("parallel"))(yg, gl)


def glu_bwd(yg, gl, dy, *, name):
    T, C = yg.shape
    tr = _pick(T, 512)

    def body(yg_ref, gl_ref, dy_ref, dyg_ref, dgl_ref):
        s = jax.nn.sigmoid(gl_ref[...])
        dyv = dy_ref[...]
        dyg_ref[...] = dyv * s
        dgl_ref[...] = dyv * yg_ref[...] * s * (1.0 - s)

    return pl.pallas_call(body, name=name, grid=(T // tr,), in_specs=[_row(tr, C)] * 3, out_specs=[_row(tr, C)] * 2,
                          out_shape=[SDS((T, C), F32)] * 2, compiler_params=_cp("parallel"))(yg, gl, dy)


def s5_out_bwd(yl, u, d, dyg_a, dyg_b, *, name):
    T, C = yl.shape
    tr = _pick(T, 512)

    def body(yl_ref, u_ref, d_ref, da_ref, db_ref, dyl_ref, du_ref, dd_ref):
        dyl = (da_ref[...] + db_ref[...]) * _gelu_grad(yl_ref[...])
        dyl_ref[...] = dyl
        du_ref[...] = dyl * d_ref[...]

        @pl.when(pl.program_id(0) == 0)
        def _():
            dd_ref[...] = jnp.zeros_like(dd_ref)

        dd_ref[...] += jnp.sum(dyl * u_ref[...], axis=0, keepdims=True)

    return pl.pallas_call(body, name=name, grid=(T // tr,),
                          in_specs=[_row(tr, C), _row(tr, C), _full((1, C)), _row(tr, C), _row(tr, C)],
                          out_specs=[_row(tr, C), _row(tr, C), _full((1, C))],
                          out_shape=[SDS((T, C), F32), SDS((T, C), F32), SDS((1, C), F32)],
                          compiler_params=_cp("arbitrary"))(yl, u, d, dyg_a, dyg_b)


def add2(a, b, *, name):
    T, C = a.shape
    tr = _pick(T, 512)

    def body(a_ref, b_ref, o_ref):
        o_ref[...] = a_ref[...] + b_ref[...]

    return pl.pallas_call(body, name=name, grid=(T // tr,), in_specs=[_row(tr, C)] * 2, out_specs=_row(tr, C),
                          out_shape=SDS((T, C), F32), compiler_params=_cp("parallel"))(a, b)


def _tri(n, upper):
    r = lax.broadcasted_iota(jnp.int32, (n, n), 0)
    c = lax.broadcasted_iota(jnp.int32, (n, n), 1)
    return ((c >= r) if upper else (c <= r)).astype(F32)


def fox_gate_fwd(fl, bf, *, fl_col, name):
    T = fl.shape[0]
    tb = _pick(T, 256)

    def body(fl_ref, bf_ref, f_ref, c_ref):
        @pl.when(pl.program_id(0) == 0)
        def _():
            c_ref[...] = jnp.zeros_like(c_ref)

        lf = jax.nn.log_sigmoid(fl_ref[...] + bf_ref[...])
        f = jnp.dot(_tri(tb, False), lf, precision=lax.Precision.HIGHEST, preferred_element_type=F32) + c_ref[0:1, :]
        f_ref[...] = f * LOG2E
        c_ref[0:1, :] = f[tb - 1:tb, :]

    fl_spec = pl.BlockSpec((tb, LANES), lambda i: (i, fl_col))
    return pl.pallas_call(body, name=name, grid=(T // tb,), in_specs=[fl_spec, _full((1, LANES))],
                          out_specs=_row(tb, LANES), out_shape=SDS((T, LANES), F32),
                          scratch_shapes=[pltpu.VMEM((SUBLANES, LANES), F32)], compiler_params=_cp("arbitrary"))(fl, bf)


def fox_gate_bwd(fl, bf, df_keys, df_queries, *, fl_col, name):
    T = fl.shape[0]
    tb = _pick(T, 256)
    nt = T // tb

    def body(fl_ref, bf_ref, dfk_ref, dfq_ref, dfl_ref, dbf_ref, c_ref):
        @pl.when(pl.program_id(0) == 0)
        def _():
            c_ref[...] = jnp.zeros_like(c_ref)
            dbf_ref[...] = jnp.zeros_like(dbf_ref)

        dlf = jnp.dot(_tri(tb, True), dfk_ref[...] + dfq_ref[...], precision=lax.Precision.HIGHEST,
                      preferred_element_type=F32) + c_ref[0:1, :]
        c_ref[0:1, :] = dlf[0:1, :]
        dfl = dlf * jax.nn.sigmoid(-(fl_ref[...] + bf_ref[...]))
        dfl_ref[...] = dfl
        dbf_ref[...] += jnp.sum(dfl, axis=0, keepdims=True)

    rev = pl.BlockSpec((tb, LANES), lambda i: (nt - 1 - i, 0))
    fl_rev = pl.BlockSpec((tb, LANES), lambda i: (nt - 1 - i, fl_col))
    return pl.pallas_call(body, name=name, grid=(nt,), in_specs=[fl_rev, _full((1, LANES)), rev, rev],
                          out_specs=[rev, _full((1, LANES))], out_shape=[SDS((T, LANES), F32), SDS((1, LANES), F32)],
                          scratch_shapes=[pltpu.VMEM((SUBLANES, LANES), F32)],
                          compiler_params=_cp("arbitrary"))(fl, bf, df_keys, df_queries)


FOX_BLOCK = 512
FOX_PAIRS = FOX_HEADS // 2
_NT = (((1,), (1,)), ((), ()))


LOG2E = 1.4426950408889634
FOX_FWD_UNROLL = 4
FOX_BWD_UNROLL = 2


def _fox_block(T):
    return _pick(T, FOX_BLOCK)


def _own_lanes(lane, hh):
    return (lane < FOX_HEAD_DIM) if hh == 0 else (lane >= FOX_HEAD_DIM)


def _grouped_steps(step, lo, n, unroll, init):
    def trip(t, c):
        for u in range(unroll):
            c = step(lo + t * unroll + u, c)
        return c

    carry = lax.fori_loop(0, n // unroll, trip, init)
    for u in range(unroll - 1):
        carry = lax.cond(n % unroll > u, lambda c: step(lo + (n // unroll) * unroll + u, c), lambda c: c, carry)
    return carry


Q_TILE0, K_TILE0, V_TILE0, O_TILE0 = 4, 8, 12, 4
FL_TILE = 16
POOL_COL = 2


def fox_fwd(z, f_col, f_row, ycat, hosted, *, name):
    T = z.shape[0]
    blk = _fox_block(T)
    nb = T // blk
    scale = FOX_HEAD_DIM ** -0.5

    def body(q_ref, k_ref, v_ref, fc_ref, fr_ref, prev_ref, o_ref, l_ref):
        i = pl.program_id(1)
        row = lax.broadcasted_iota(jnp.int32, (blk, blk), 0)
        col = lax.broadcasted_iota(jnp.int32, (blk, blk), 1)
        lane = lax.broadcasted_iota(jnp.int32, (blk, LANES), 1)
        qt = q_ref[...] * (scale * LOG2E)
        outs = []
        for hh in range(2):
            qh = jnp.where(_own_lanes(lane, hh), qt, 0.0).astype(MXU_DTYPE)
            fi = fc_ref[0, :, hh:hh + 1]

            def step(j, carry, masked=False):
                m, l, acc = carry
                r0 = pl.multiple_of(j * blk, blk)
                kj = k_ref[pl.ds(r0, blk), :].astype(MXU_DTYPE)
                vj = v_ref[pl.ds(r0, blk), :].astype(MXU_DTYPE)
                s = lax.dot_general(qh, kj, _NT, preferred_element_type=F32) + (fi - fr_ref[0, j, hh:hh + 1, :])
                if masked:
                    s = jnp.where(col <= row, s, -jnp.inf)
                m_new = jnp.maximum(m, jnp.max(s, axis=-1, keepdims=True))
                p = jnp.exp2(s - m_new)
                alpha = jnp.exp2(m - m_new)
                l = alpha * l + jnp.sum(p, axis=-1, keepdims=True)
                acc = alpha * acc + jnp.dot(p.astype(MXU_DTYPE), vj, preferred_element_type=F32)
                return m_new, l, acc

            init = (jnp.full((blk, 1), -jnp.inf, F32), jnp.zeros((blk, 1), F32), jnp.zeros((blk, LANES), F32))
            m, l, acc = step(i, _grouped_steps(step, 0, i, FOX_FWD_UNROLL, init), True)
            outs.append(acc / l)
            l_ref[0, :, hh:hh + 1] = m + jnp.log2(l)
        o_ref[...] = jnp.where(_own_lanes(lane, 0), outs[0], outs[1])

    qspec = pl.BlockSpec((blk, LANES), lambda h, i: (i, Q_TILE0 + h))
    kspec = pl.BlockSpec((T, LANES), lambda h, i: (0, K_TILE0 + h))
    vspec = pl.BlockSpec((T, LANES), lambda h, i: (0, V_TILE0 + h))
    ospec = pl.BlockSpec((blk, LANES), lambda h, i: (i, O_TILE0 + h))
    cspec = pl.BlockSpec((1, blk, 2), lambda h, i: (h, i, 0))
    rspec = pl.BlockSpec((1, nb, 2, blk), lambda h, i: (h, 0, 0, 0))
    return call_hosting(body, hosted, name=name, grid=(FOX_PAIRS, nb),
                        in_specs=[qspec, kspec, vspec, cspec, rspec, ANY], out_specs=[ospec, cspec],
                        out_shape=[SDS(ycat.shape, F32), SDS((FOX_PAIRS, T, 2), F32)],
                        inputs=[z, z, z, f_col, f_row, ycat], aliases={5: 0})


def fox_dd(ycat, dycat, *, name):
    T = ycat.shape[0]
    blk = _fox_block(T)

    def body(o_ref, do_ref, dd_ref):
        lane = lax.broadcasted_iota(jnp.int32, (blk, LANES), 1)
        prod = do_ref[...] * o_ref[...]
        for hh in range(2):
            dd_ref[0, :, hh:hh + 1] = jnp.sum(jnp.where(_own_lanes(lane, hh), prod, 0.0), axis=-1, keepdims=True)

    ospec = pl.BlockSpec((blk, LANES), lambda h, i: (i, O_TILE0 + h))
    return pl.pallas_call(body, name=name, grid=(FOX_PAIRS, T // blk), in_specs=[ospec, ospec],
                          out_specs=pl.BlockSpec((1, blk, 2), lambda h, i: (h, i, 0)),
                          out_shape=SDS((FOX_PAIRS, T, 2), F32), compiler_params=_cp("parallel", "parallel"))(ycat, dycat)


def fox_bwd(z, dycat, f_col, f_row, lse_row, dd_row, hosted, *, name):
    T = z.shape[0]
    blk = _fox_block(T)
    nb = T // blk
    scale = FOX_HEAD_DIM ** -0.5

    def body(q_ref, k_ref, v_ref, do_ref, fc_ref, fr_ref, lr_ref, dr_ref, dk_ref, dv_ref, df_ref, dqt_ref, dfq_ref):
        j = pl.program_id(1)

        @pl.when(j == 0)
        def _():
            dqt_ref[...] = jnp.zeros_like(dqt_ref)
            dfq_ref[...] = jnp.zeros_like(dfq_ref)

        row = lax.broadcasted_iota(jnp.int32, (blk, blk), 0)
        col = lax.broadcasted_iota(jnp.int32, (blk, blk), 1)
        lane = lax.broadcasted_iota(jnp.int32, (blk, LANES), 1)
        kt = k_ref[...]
        vt = v_ref[...]
        dks, dvs = [], []
        for hh in range(2):
            own = _own_lanes(lane, hh)
            kh = jnp.where(own, kt, 0.0).astype(MXU_DTYPE)
            vh = jnp.where(own, vt, 0.0).astype(MXU_DTYPE)
            kht = kh.T
            fj = fc_ref[0, :, hh:hh + 1]

            def step(i, carry, masked=False):
                dk, dv, df = carry
                r0 = pl.multiple_of(i * blk, blk)
                qi = (q_ref[pl.ds(r0, blk), :] * (scale * LOG2E)).astype(MXU_DTYPE)
                doi = do_ref[pl.ds(r0, blk), :].astype(MXU_DTYPE)
                st = lax.dot_general(kh, qi, _NT, preferred_element_type=F32) + (fr_ref[0, i, hh:hh + 1, :] - fj)
                pt = jnp.exp2(st - lr_ref[0, i, hh:hh + 1, :])
                if masked:
                    pt = jnp.where(col >= row, pt, 0.0)
                dv = dv + jnp.dot(pt.astype(MXU_DTYPE), doi, preferred_element_type=F32)
                dpt = lax.dot_general(vh, doi, _NT, preferred_element_type=F32)
                dst = pt * (dpt - dr_ref[0, i, hh:hh + 1, :])
                dsb = dst.astype(MXU_DTYPE)
                dk = dk + jnp.dot(dsb, qi, preferred_element_type=F32)
                df = df - jnp.sum(dst, axis=-1, keepdims=True)
                dqt_ref[0, i] += jnp.dot(kht, dsb, preferred_element_type=F32)
                dfq_ref[0, i, hh:hh + 1, :] += jnp.sum(dst, axis=0, keepdims=True)
                return dk, dv, df

            init = (jnp.zeros((blk, LANES), F32), jnp.zeros((blk, LANES), F32), jnp.zeros((blk, 1), F32))
            dk, dv, df = _grouped_steps(step, j + 1, nb - 1 - j, FOX_BWD_UNROLL, step(j, init, True))
            dks.append(dk * (1.0 / LOG2E))
            dvs.append(dv)
            df_ref[0, :, hh:hh + 1] = df
        dk_ref[...] = jnp.where(_own_lanes(lane, 0), dks[0], dks[1])
        dv_ref[...] = jnp.where(_own_lanes(lane, 0), dvs[0], dvs[1])

    bspec = pl.BlockSpec((blk, LANES), lambda h, j: (j, h))
    qspec = pl.BlockSpec((T, LANES), lambda h, j: (0, Q_TILE0 + h))
    kspec = pl.BlockSpec((blk, LANES), lambda h, j: (j, K_TILE0 + h))
    vspec = pl.BlockSpec((blk, LANES), lambda h, j: (j, V_TILE0 + h))
    dospec = pl.BlockSpec((T, LANES), lambda h, j: (0, O_TILE0 + h))
    cspec = pl.BlockSpec((1, blk, 2), lambda h, j: (h, j, 0))
    rspec = pl.BlockSpec((1, nb, 2, blk), lambda h, j: (h, 0, 0, 0))
    dqspec = pl.BlockSpec((1, nb, LANES, blk), lambda h, j: (h, 0, 0, 0))
    return call_hosting(body, hosted, name=name, grid=(FOX_PAIRS, nb),
                        in_specs=[qspec, kspec, vspec, dospec, cspec, rspec, rspec, rspec],
                        out_specs=[bspec, bspec, cspec, dqspec, rspec],
                        out_shape=[SDS((T, FOX_WIDTH), F32), SDS((T, FOX_WIDTH), F32), SDS((FOX_PAIRS, T, 2), F32),
                                   SDS((FOX_PAIRS, nb, LANES, blk), F32), SDS((FOX_PAIRS, nb, 2, blk), F32)],
                        inputs=[z, z, z, dycat, f_col, f_row, lse_row, dd_row], aliases={})


def _pairs_col(a, T):
    return jnp.transpose(a[:, :FOX_HEADS].reshape(T, FOX_PAIRS, 2), (1, 0, 2))


def _col_to_row(a, T):
    blk = _fox_block(T)
    return jnp.transpose(a.reshape(FOX_PAIRS, T // blk, blk, 2), (0, 1, 3, 2))


def _row_to_col(a, T):
    return jnp.transpose(a, (0, 1, 3, 2)).reshape(FOX_PAIRS, T, 2)


def _pairs_to_lanes(a, T):
    flat = jnp.transpose(a, (1, 0, 2)).reshape(T, FOX_HEADS)
    return jnp.pad(flat, ((0, 0), (0, LANES - FOX_HEADS)))


def _pool_counts(t0, n, w):
    t = (t0 + lax.broadcasted_iota(jnp.int32, (n, 1), 0)).astype(F32)
    return jnp.minimum(t + 1.0, float(w))


def pool_window(x, *, adjoint, name, in_col=0, into=None, out_col=0):
    T, C = x.shape[0], len(POOL_WINDOWS) * POOL_GROUP_DIM
    tr = _pick(T, 512)
    nt = T // tr
    hb = tr // POOL_HALO
    n = tr + POOL_HALO

    def body(x_ref, h_ref, *rest):
        o_ref = rest[-1]
        i = pl.program_id(0)
        cur = x_ref[...]
        if adjoint:
            halo = h_ref[...] * jnp.where(i < nt - 1, 1.0, 0.0)
            ext = jnp.concatenate([cur, halo], axis=0)
            t0 = i * tr
        else:
            halo = h_ref[...] * jnp.where(i > 0, 1.0, 0.0)
            ext = jnp.concatenate([halo, cur], axis=0)
            t0 = i * tr - POOL_HALO
        sums = {}
        for g, w in enumerate(POOL_WINDOWS):
            ls = slice(g * POOL_GROUP_DIM, (g + 1) * POOL_GROUP_DIM)
            s = ext[:, ls]
            if adjoint:
                s = s / _pool_counts(t0, n, w)
            d = 1
            while d < w:
                s = s + pltpu.roll(s, (n - d) if adjoint else d, 0)
                d *= 2
            if adjoint:
                o_ref[:, ls] = s[0:tr, :] - cur[:, ls]
            else:
                o_ref[:, ls] = s[POOL_HALO:n, :] / _pool_counts(i * tr, tr, w) - cur[:, ls]

    if adjoint:
        halo_spec = pl.BlockSpec((POOL_HALO, C), lambda i: (jnp.minimum((i + 1) * hb, T // POOL_HALO - 1), in_col))
    else:
        halo_spec = pl.BlockSpec((POOL_HALO, C), lambda i: (jnp.maximum(i * hb - 1, 0), in_col))
    x_spec = pl.BlockSpec((tr, C), lambda i: (i, in_col))
    if into is None:
        return pl.pallas_call(body, name=name, grid=(nt,), in_specs=[x_spec, halo_spec], out_specs=_row(tr, C),
                              out_shape=SDS((T, C), F32), compiler_params=_cp("parallel"))(x, x)
    return pl.pallas_call(body, name=name, grid=(nt,), in_specs=[x_spec, halo_spec, ANY],
                          out_specs=pl.BlockSpec((tr, C), lambda i: (i, out_col)), out_shape=SDS(into.shape, F32),
                          input_output_aliases={2: 0}, compiler_params=_cp("parallel"))(x, x, into)


def colscale_fwd(a, s, *, out_cols, name):
    T, C = a.shape
    tr = _pick(T, 512)

    def body(a_ref, s_ref, o_ref):
        o_ref[...] = a_ref[...] * s_ref[...]

    return pl.pallas_call(body, name=name, grid=(T // tr,), in_specs=[_row(tr, C), _full((1, C))], out_specs=_row(tr, C),
                          out_shape=SDS((T, out_cols), F32), compiler_params=_cp("parallel"))(a, s)


def colscale_bwd(a, s, dy, *, name):
    T, C = a.shape
    tr = _pick(T, 512)

    def body(a_ref, s_ref, dy_ref, da_ref, ds_ref):
        dyv = dy_ref[...]
        da_ref[...] = dyv * s_ref[...]

        @pl.when(pl.program_id(0) == 0)
        def _():
            ds_ref[...] = jnp.zeros_like(ds_ref)

        ds_ref[...] += jnp.sum(dyv * a_ref[...], axis=0, keepdims=True)

    return pl.pallas_call(body, name=name, grid=(T // tr,), in_specs=[_row(tr, C), _full((1, C)), _row(tr, C)],
                          out_specs=[_row(tr, C), _full((1, C))], out_shape=[SDS((T, C), F32), SDS((1, C), F32)],
                          compiler_params=_cp("arbitrary"))(a, s, dy)


SGU_ROWS = 512


def _sgu_norm(v, ln_g, ln_b):
    vg = jax.nn.gelu(v)
    xc = vg - jnp.mean(vg, axis=-1, keepdims=True)
    r = lax.rsqrt(jnp.mean(xc * xc, axis=-1, keepdims=True) + EPS)
    xh = xc * r
    return xh * ln_g + ln_b, xh, r


def _rowc(tr, c, cb):
    return pl.BlockSpec((tr, c), lambda i: (i, cb))


def sgu_fwd(z, ln_g, ln_b, ws, bst, ycat, *, name):
    T, C = z.shape[0], SGU_GROUPS * SGU_GROUP_DIM
    tr = _pick(T, SGU_ROWS)

    def body(u_ref, v_ref, g_ref, b_ref, ws_ref, bst_ref, prev_ref, o_ref):
        vn, _, _ = _sgu_norm(v_ref[...], g_ref[...], b_ref[...])
        vn = vn.astype(MXU_DTYPE)
        ug = jax.nn.gelu(u_ref[...])
        for g in range(SGU_GROUPS):
            w = ws_ref[g].astype(MXU_DTYPE)
            bias = bst_ref[:, g:g + 1]
            for c in range(tr // CHUNK):
                rs = slice(c * CHUNK, (c + 1) * CHUNK)
                ls = slice(g * SGU_GROUP_DIM, (g + 1) * SGU_GROUP_DIM)
                mixed = jnp.dot(w, vn[rs, ls], preferred_element_type=F32) + bias
                o_ref[rs, ls] = ug[rs, ls] * mixed

    return pl.pallas_call(body, name=name, grid=(T // tr,),
                          in_specs=[_rowc(tr, C, 0), _rowc(tr, C, 1), _full((1, C)), _full((1, C)),
                                    _full((SGU_GROUPS, CHUNK, CHUNK)), _full((CHUNK, SGU_GROUPS)), ANY],
                          out_specs=_rowc(tr, C, 1), out_shape=SDS(ycat.shape, F32), input_output_aliases={6: 0},
                          compiler_params=_cp("parallel"))(z, z, ln_g, ln_b, ws, bst, ycat)


def sgu_bwd(z, ln_g, ln_b, ws, wst, bst, dycat, *, out_cols, name):
    T, C = z.shape[0], SGU_GROUPS * SGU_GROUP_DIM
    tr = _pick(T, SGU_ROWS)

    def body(u_ref, v_ref, g_ref, b_ref, ws_ref, wst_ref, bst_ref, dy_ref,
             duv_ref, dg_ref, db_ref, dws_ref, dbst_ref, dvn_ref):
        du_ref = duv_ref.at[:, 0:C]
        dv_ref = duv_ref.at[:, C:2 * C]
        @pl.when(pl.program_id(0) == 0)
        def _():
            dg_ref[...] = jnp.zeros_like(dg_ref)
            db_ref[...] = jnp.zeros_like(db_ref)
            dws_ref[...] = jnp.zeros_like(dws_ref)
            dbst_ref[...] = jnp.zeros_like(dbst_ref)

        uv = u_ref[...]
        vv = v_ref[...]
        vn, xh, r = _sgu_norm(vv, g_ref[...], b_ref[...])
        vn = vn.astype(MXU_DTYPE)
        ug = jax.nn.gelu(uv)
        dyv = dy_ref[...]
        for g in range(SGU_GROUPS):
            w = ws_ref[g].astype(MXU_DTYPE)
            wt = wst_ref[g].astype(MXU_DTYPE)
            bias = bst_ref[:, g:g + 1]
            dw = jnp.zeros((CHUNK, CHUNK), F32)
            dbias = jnp.zeros((CHUNK, 1), F32)
            for c in range(tr // CHUNK):
                rs = slice(c * CHUNK, (c + 1) * CHUNK)
                ls = slice(g * SGU_GROUP_DIM, (g + 1) * SGU_GROUP_DIM)
                vblk = vn[rs, ls]
                mixed = jnp.dot(w, vblk, preferred_element_type=F32) + bias
                dyb = dyv[rs, ls]
                du_ref[rs, ls] = dyb * mixed * _gelu_grad(uv[rs, ls])
                dmixed = dyb * ug[rs, ls]
                dbias = dbias + jnp.sum(dmixed, axis=-1, keepdims=True)
                dmb = dmixed.astype(MXU_DTYPE)
                dw = dw + lax.dot_general(dmb, vblk, _NT, preferred_element_type=F32)
                dvn_ref[rs, ls] = jnp.dot(wt, dmb, preferred_element_type=F32)
            dws_ref[g] += dw
            dbst_ref[:, g:g + 1] += dbias
        dvn = dvn_ref[...]
        dg_ref[...] += jnp.sum(dvn * xh, axis=0, keepdims=True)
        db_ref[...] += jnp.sum(dvn, axis=0, keepdims=True)
        dxh = dvn * g_ref[...]
        dvg = r * (dxh - jnp.mean(dxh, axis=-1, keepdims=True) - xh * jnp.mean(dxh * xh, axis=-1, keepdims=True))
        dv_ref[...] = dvg * _gelu_grad(vv)

    wspec = _full((SGU_GROUPS, CHUNK, CHUNK))
    return pl.pallas_call(body, name=name, grid=(T // tr,),
                          in_specs=[_rowc(tr, C, 0), _rowc(tr, C, 1), _full((1, C)), _full((1, C)), wspec, wspec,
                                    _full((CHUNK, SGU_GROUPS)), _rowc(tr, C, 1)],
                          out_specs=[_rowc(tr, 2 * C, 0), _full((1, C)), _full((1, C)), wspec,
                                     _full((CHUNK, SGU_GROUPS))],
                          out_shape=[SDS((T, out_cols), F32), SDS((1, C), F32), SDS((1, C), F32),
                                     SDS((SGU_GROUPS, CHUNK, CHUNK), F32), SDS((CHUNK, SGU_GROUPS), F32)],
                          scratch_shapes=[pltpu.VMEM((tr, C), F32)],
                          compiler_params=_cp("arbitrary"))(z, z, ln_g, ln_b, ws, wst, bst, dycat)


def adamw(w, g, m, v, *, name):
    R, C = w.shape
    tr = _pick(R, 512)
    c1 = 1.0 - ADAM_B1 ** ADAM_STEP
    c2 = 1.0 - ADAM_B2 ** ADAM_STEP

    def body(w_ref, g_ref, m_ref, v_ref, d_ref, nm_ref, nv_ref):
        gv = g_ref[...]
        nm = ADAM_B1 * m_ref[...] + (1.0 - ADAM_B1) * gv
        nv = ADAM_B2 * v_ref[...] + (1.0 - ADAM_B2) * (gv * gv)
        nm_ref[...] = nm
        nv_ref[...] = nv
        d_ref[...] = -ADAM_LR * ((nm / c1) / (jnp.sqrt(nv / c2) + ADAM_EPS) + ADAM_WD * w_ref[...])

    spec = _row(tr, C)
    return pl.pallas_call(body, name=name, grid=(R // tr,), in_specs=[spec] * 4, out_specs=[spec] * 3,
                          out_shape=[SDS((R, C), F32)] * 3, compiler_params=_cp("parallel"))(w, g, m, v)


ANY = pl.BlockSpec(memory_space=pl.ANY)


def _coords():
    return lax.axis_index("x"), lax.axis_index("y"), lax.axis_index("c")


def _other_chips(x, y):
    return [(1 - x, y), (x, 1 - y), (1 - x, 1 - y)]


def _remote(src, dst, send_sems, recv_sems, k, dev):
    return pltpu.make_async_remote_copy(src_ref=src, dst_ref=dst, send_sem=send_sems.at[k], recv_sem=recv_sems.at[k],
                                        device_id=dev, device_id_type=MESH)


LOCAL_CHUNKS = 8


def allgather_chip_shards(shards, small, *, name):
    na = len(shards)

    def body(*refs):
        s_refs, sm_ref = refs[:na], refs[na]
        o_refs, smo_ref = refs[na + 1:2 * na + 1], refs[2 * na + 1]
        send_sems, recv_sems, local_sems = refs[2 * na + 2:]
        x, y, c = _coords()
        j = 2 * x + y
        sibling = (x, y, 1 - c)
        chips = _other_chips(x, y)
        for a in range(na):
            chunk = shards[a].shape[0] // LOCAL_CHUNKS
            for q in range(LOCAL_CHUNKS):
                rows = pl.ds(q * chunk, chunk)
                pltpu.make_async_copy(s_refs[a].at[rows], o_refs[a].at[j, rows], local_sems.at[a]).start()
        pltpu.make_async_copy(sm_ref, smo_ref.at[j], local_sems.at[na]).start()
        sends = []
        for a in range(na):
            half = shards[a].shape[0] // 2
            mine = pl.ds(c * half, half)
            for k, (px, py) in enumerate(chips):
                sends.append(_remote(s_refs[a].at[mine], o_refs[a].at[j, mine], send_sems, recv_sems, 6 * a + k, (px, py, c)))
        for k, (px, py) in enumerate(chips):
            sends.append(_remote(sm_ref, smo_ref.at[j], send_sems, recv_sems, 6 * na + k, (px, py, c)))
        for cp in sends:
            cp.start()
        for a in range(na):
            half = shards[a].shape[0] // 2
            mine = pl.ds(c * half, half)
            for k, (px, py) in enumerate(chips):
                rows = o_refs[a].at[2 * px + py, mine]
                _remote(rows, rows, send_sems, recv_sems, 6 * a + k, (px, py, c)).wait_recv()
                fw = _remote(rows, rows, send_sems, recv_sems, 6 * a + 3 + k, sibling)
                fw.start()
                sends.append(fw)
        for a in range(na):
            half = shards[a].shape[0] // 2
            theirs = pl.ds((1 - c) * half, half)
            for k, (px, py) in enumerate(chips):
                rows = o_refs[a].at[2 * px + py, theirs]
                _remote(rows, rows, send_sems, recv_sems, 6 * a + 3 + k, sibling).wait_recv()
        for k, (px, py) in enumerate(chips):
            slot = smo_ref.at[2 * px + py]
            _remote(slot, slot, send_sems, recv_sems, 6 * na + k, (px, py, c)).wait_recv()
        for cp in sends:
            cp.wait_send()
        for a in range(na):
            pltpu.make_async_copy(s_refs[a], o_refs[a].at[j], local_sems.at[a]).wait()
        pltpu.make_async_copy(sm_ref, smo_ref.at[j], local_sems.at[na]).wait()

    nsem = 6 * na + 3
    outs = pl.pallas_call(
        body, name=name, in_specs=[ANY] * (na + 1), out_specs=[ANY] * (na + 1),
        out_shape=[SDS((N_CHIPS,) + s.shape, s.dtype) for s in shards] + [SDS((N_CHIPS,) + small.shape, small.dtype)],
        scratch_shapes=[pltpu.SemaphoreType.DMA((nsem,)), pltpu.SemaphoreType.DMA((nsem,)),
                        pltpu.SemaphoreType.DMA((na + 1,))])(*shards, small)
    return outs[:na], outs[na]


class Exchange:
    def __init__(self, ins, out_shapes, scratch, start, wait):
        self.ins, self.out_shapes, self.scratch, self.start, self.wait = list(ins), list(out_shapes), list(scratch), start, wait


def run_exchange(ex, *, name):
    ni, no = len(ex.ins), len(ex.out_shapes)

    def body(*refs):
        parts = refs[:ni], refs[ni:ni + no], refs[ni + no:]
        ex.start(*parts)
        ex.wait(*parts)

    return pl.pallas_call(body, name=name, in_specs=[ANY] * ni, out_specs=[ANY] * no, out_shape=ex.out_shapes,
                          scratch_shapes=ex.scratch)(*ex.ins)


def call_hosting(body, ex, *, name, grid, in_specs, out_specs, out_shape, inputs, aliases, scratch=()):
    n_in, n_out, ni, no, ns = len(inputs), len(out_shape), len(ex.ins), len(ex.out_shapes), len(scratch)
    outs_at = n_in + ni
    scr_at = outs_at + n_out + no

    def wrapped(*refs):
        own = refs[:n_in] + refs[outs_at:outs_at + n_out] + refs[scr_at:scr_at + ns]
        parts = refs[n_in:outs_at], refs[outs_at + n_out:scr_at], refs[scr_at + ns:]
        ids = [pl.program_id(d) for d in range(len(grid))]
        first = functools.reduce(jnp.logical_and, [i == 0 for i in ids])
        last = functools.reduce(jnp.logical_and, [i == g - 1 for i, g in zip(ids, grid)])

        @pl.when(first)
        def _():
            ex.start(*parts)

        body(*own)

        @pl.when(last)
        def _():
            ex.wait(*parts)

    outs = pl.pallas_call(
        wrapped, name=name, grid=grid, in_specs=list(in_specs) + [ANY] * ni, out_specs=list(out_specs) + [ANY] * no,
        out_shape=list(out_shape) + ex.out_shapes, input_output_aliases=aliases,
        scratch_shapes=list(scratch) + ex.scratch,
        compiler_params=_cp(*["arbitrary"] * len(grid)))(*inputs, *ex.ins)
    return outs[:n_out], outs[n_out:]


def allgather_ici_exchange(shards):
    na = len(shards)

    def copies(s_refs, o_refs, sems):
        send_sems, recv_sems, _ = sems
        x, y, c = _coords()
        j = 2 * x + y
        out = []
        for a in range(na):
            half = shards[a].shape[0] // 2
            mine = pl.ds(c * half, half)
            for k, (px, py) in enumerate(_other_chips(x, y)):
                send = _remote(s_refs[a].at[mine], o_refs[a].at[j, mine], send_sems, recv_sems, 3 * a + k, (px, py, c))
                rows = o_refs[a].at[2 * px + py, mine]
                out.append((send, _remote(rows, rows, send_sems, recv_sems, 3 * a + k, (px, py, c))))
        return out

    def start(s_refs, o_refs, sems):
        x, y, c = _coords()
        j = 2 * x + y
        for a in range(na):
            chunk = shards[a].shape[0] // LOCAL_CHUNKS
            for q in range(LOCAL_CHUNKS):
                rows = pl.ds(q * chunk, chunk)
                pltpu.make_async_copy(s_refs[a].at[rows], o_refs[a].at[j, rows], sems[2].at[a]).start()
        for send, _ in copies(s_refs, o_refs, sems):
            send.start()

    def wait(s_refs, o_refs, sems):
        x, y, c = _coords()
        j = 2 * x + y
        for send, arrival in copies(s_refs, o_refs, sems):
            arrival.wait_recv()
            send.wait_send()
        for a in range(na):
            pltpu.make_async_copy(s_refs[a], o_refs[a].at[j], sems[2].at[a]).wait()

    return Exchange(shards, [SDS((N_CHIPS,) + s.shape, s.dtype) for s in shards],
                    [pltpu.SemaphoreType.DMA((3 * na,)), pltpu.SemaphoreType.DMA((3 * na,)), pltpu.SemaphoreType.DMA((na,))],
                    start, wait)


def allgather_forward(gathered, *, name):
    na = len(gathered)

    def body(*refs):
        o_refs = refs[na:2 * na]
        send_sems, recv_sems = refs[2 * na:]
        x, y, c = _coords()
        sibling = (x, y, 1 - c)
        cps = []
        for a in range(na):
            half = gathered[a].shape[1] // 2
            for k, (px, py) in enumerate(_other_chips(x, y)):
                mine = o_refs[a].at[2 * px + py, pl.ds(c * half, half)]
                theirs = o_refs[a].at[2 * px + py, pl.ds((1 - c) * half, half)]
                cps.append((_remote(mine, mine, send_sems, recv_sems, 3 * a + k, sibling),
                            _remote(theirs, theirs, send_sems, recv_sems, 3 * a + k, sibling)))
        for send, _ in cps:
            send.start()
        for send, arrival in cps:
            send.wait_send()
            arrival.wait_recv()

    return pl.pallas_call(body, name=name, in_specs=[ANY] * na, out_specs=[ANY] * na,
                          out_shape=[SDS(g.shape, g.dtype) for g in gathered],
                          input_output_aliases={a: a for a in range(na)},
                          scratch_shapes=[pltpu.SemaphoreType.DMA((3 * na,)), pltpu.SemaphoreType.DMA((3 * na,))])(*gathered)


def swap_halves_exchange(gs):
    na = len(gs)

    def copies(g_refs, o_refs, sems):
        x, y, c = _coords()
        out = []
        for a in range(na):
            half = gs[a].shape[1] // 2
            out.append(_remote(g_refs[a].at[:, pl.ds((1 - c) * half, half), :], o_refs[a], sems[0], sems[1], a,
                               (x, y, 1 - c)))
        return out

    def start(g_refs, o_refs, sems):
        for cp in copies(g_refs, o_refs, sems):
            cp.start()

    def wait(g_refs, o_refs, sems):
        for cp in copies(g_refs, o_refs, sems):
            cp.wait()

    return Exchange(gs, [SDS((g.shape[0], g.shape[1] // 2, g.shape[2]), g.dtype) for g in gs],
                    [pltpu.SemaphoreType.DMA((na,)), pltpu.SemaphoreType.DMA((na,))], start, wait)


def chip_partials_exchange(pbs):
    na = len(pbs)

    def copies(p_refs, o_refs, sems):
        x, y, c = _coords()
        out = []
        for a in range(na):
            for k, (px, py) in enumerate(_other_chips(x, y)):
                out.append(_remote(p_refs[a].at[2 * px + py], o_refs[a].at[k], sems[0], sems[1], 3 * a + k, (px, py, c)))
        return out

    def start(p_refs, o_refs, sems):
        for cp in copies(p_refs, o_refs, sems):
            cp.start()

    def wait(p_refs, o_refs, sems):
        for cp in copies(p_refs, o_refs, sems):
            cp.wait()

    return Exchange(pbs, [SDS((3,) + p.shape[1:], p.dtype) for p in pbs],
                    [pltpu.SemaphoreType.DMA((3 * na,)), pltpu.SemaphoreType.DMA((3 * na,))], start, wait)


def add_sibling_half(g, land, c_idx, *, name):
    n, R, C = g.shape
    half = R // 2
    tr = _pick(half, 256)
    nt = half // tr

    def body(c_ref, g_ref, l_ref, of_ref, ob_ref):
        s = g_ref[...] + l_ref[...].astype(F32)
        of_ref[...] = s
        ob_ref[...] = s.astype(ob_ref.dtype)

    blk = pl.BlockSpec((1, tr, C), lambda s, i, c_ref: (s, i, 0))
    gblk = pl.BlockSpec((1, tr, C), lambda s, i, c_ref: (s, c_ref[0] * nt + i, 0))
    return pl.pallas_call(
        body, name=name,
        grid_spec=pltpu.PrefetchScalarGridSpec(num_scalar_prefetch=1, grid=(n, nt), in_specs=[gblk, blk],
                                               out_specs=[blk, blk]),
        out_shape=[SDS((n, half, C), F32), SDS((n, half, C), WIRE_DTYPE)],
        compiler_params=_cp("parallel", "parallel"))(c_idx, g, land)


def add_chip_partials(pf, rb, jc_idx, *, name):
    n, H, C = pf.shape
    tr = _pick(H, 256)

    def body(jc_ref, p_ref, r_ref, o_ref):
        s = p_ref[0]
        for k in range(3):
            s = s + r_ref[k].astype(F32)
        o_ref[...] = s

    pblk = pl.BlockSpec((1, tr, C), lambda i, jc_ref: (jc_ref[0], i, 0))
    rblk = pl.BlockSpec((3, tr, C), lambda i, jc_ref: (0, i, 0))
    oblk = pl.BlockSpec((None, tr, C), lambda i, jc_ref: (jc_ref[1], i, 0))
    return pl.pallas_call(
        body, name=name,
        grid_spec=pltpu.PrefetchScalarGridSpec(num_scalar_prefetch=1, grid=(H // tr,), in_specs=[pblk, rblk],
                                               out_specs=oblk),
        out_shape=SDS((2, H, C), F32), compiler_params=_cp("parallel"))(jc_idx, pf, rb)


def join_sibling_halves(bufs, *, name):
    na = len(bufs)

    def body(*refs):
        o_refs = refs[na:2 * na]
        send_sems, recv_sems = refs[2 * na:]
        x, y, c = _coords()
        cps = [_remote(o_refs[a].at[c], o_refs[a].at[c], send_sems, recv_sems, a, (x, y, 1 - c)) for a in range(na)]
        for cp in cps:
            cp.start()
        for a in range(na):
            cps[a].wait_send()
            _remote(o_refs[a].at[1 - c], o_refs[a].at[1 - c], send_sems, recv_sems, a, (x, y, 1 - c)).wait_recv()

    return pl.pallas_call(body, name=name, in_specs=[ANY] * na, out_specs=[ANY] * na,
                          out_shape=[SDS(b.shape, b.dtype) for b in bufs],
                          input_output_aliases={a: a for a in range(na)},
                          scratch_shapes=[pltpu.SemaphoreType.DMA((na,)), pltpu.SemaphoreType.DMA((na,))])(*bufs)


def exchange_pieces(v, *, scatter, name):
    P, C = v.shape[-2:]

    def body(v_ref, o_ref, send_sems, recv_sems, local_sem):
        x, y, c = _coords()
        me = 4 * x + 2 * y + c
        local = pltpu.make_async_copy(v_ref.at[me] if scatter else v_ref, o_ref.at[me], local_sem)
        local.start()
        cps = []
        for m in range(1, N_DEV):
            px = (1 - x) if m & 4 else x
            py = (1 - y) if m & 2 else y
            pc = (1 - c) if m & 1 else c
            src = v_ref.at[4 * px + 2 * py + pc] if scatter else v_ref
            cps.append(_remote(src, o_ref.at[me], send_sems, recv_sems, m - 1, (px, py, pc)))
        for cp in cps:
            cp.start()
        for cp in cps:
            cp.wait_send()
        for m in range(1, N_DEV):
            px = (1 - x) if m & 4 else x
            py = (1 - y) if m & 2 else y
            pc = (1 - c) if m & 1 else c
            slot = o_ref.at[4 * px + 2 * py + pc]
            _remote(slot, slot, send_sems, recv_sems, m - 1, (px, py, pc)).wait_recv()
        local.wait()

    return pl.pallas_call(body, name=name, in_specs=[ANY], out_specs=ANY, out_shape=SDS((N_DEV, P, C), v.dtype),
                          scratch_shapes=[pltpu.SemaphoreType.DMA((N_DEV - 1,)), pltpu.SemaphoreType.DMA((N_DEV - 1,)),
                                          pltpu.SemaphoreType.DMA(())])(v)


def sum_pieces(land, *, name):
    n, P, C = land.shape

    def body(l_ref, o_ref):
        s = l_ref[0]
        for d in range(1, n):
            s = s + l_ref[d]
        o_ref[...] = s

    return pl.pallas_call(body, name=name, out_shape=SDS((P, C), F32))(land)


BIG_SEGS = (
    ("w_in_even", (1024, 514), 1),
    ("s5_w_glu", (128, 512), 0),
    ("w_out_even", (256, 1024), 0),
    ("w_in_odd", (1024, 384), 1),
    ("w_out_odd", (256, 1024), 0),
    ("mlp_w1", (2, 1024, 1024), 2),
    ("mlp_w2", (2, 1024, 1024), 1),
)
BIG_NAMES = tuple(n for n, _, _ in BIG_SEGS)
EARLY_NAMES = ("w_in_even", "s5_w_glu")
LATE_NAMES = ("w_out_even", "w_in_odd", "w_out_odd", "mlp_w1", "mlp_w2")
REDUCED_EARLY = ("s5_w_glu", "w_out_even", "w_in_odd", "w_out_odd", "mlp_w1", "mlp_w2")
SHARDED_SMALL = ("pool_scale", "sgu_ln_g", "sgu_ln_b")
SMALL_SEGS = (
    ("mix_pre_g", (2, 1024)), ("mix_post_g", (2, 1024)), ("mlp_pre_g", (2, 1024)), ("mlp_post_g", (2, 1024)),
    ("s5_lam_re", (1, 32, 64)), ("s5_lam_im", (1, 32, 64)), ("s5_log_dt", (1, 32)),
    ("s5_b_re", (1, 32, 64, 16)), ("s5_b_im", (1, 32, 64, 16)), ("s5_c_re", (1, 32, 16, 64)), ("s5_c_im", (1, 32, 16, 64)),
    ("s5_d", (1, 512)), ("fox_b_f", (1, 8)), ("pool_w", (1, 4, 128, 128)), ("sgu_w_s", (1, 4, 128, 128)),
    ("sgu_b_s", (1, 4, 128)),
)
REDUCED_SEGS = SMALL_SEGS + tuple((n, (1, 512)) for n in SHARDED_SMALL) + (("loss", (1, 1)),)


def _cols_from_chips(g):
    n, R, C = g.shape
    return jnp.transpose(g, (1, 0, 2)).reshape(R, n * C)


def _chips_from_cols(m):
    R, C4 = m.shape
    return jnp.transpose(m.reshape(R, N_CHIPS, C4 // N_CHIPS), (1, 0, 2))


MLP_SHARD = 1024


def _w1_cols(l):
    def spec(tm, tn, tk):
        per = MLP_SHARD // tn
        return pl.BlockSpec((None, tk, tn), lambda i, j, k: (j // per, l * (MLP_SHARD // tk) + k, j % per))
    return spec


def _w1_rows_t(l):
    def spec(tm, tn, tk):
        if tk == N_CHIPS * MLP_SHARD:
            return pl.BlockSpec((N_CHIPS, tn, MLP_SHARD), lambda i, j, k: (0, l * (MLP_SHARD // tn) + j, 0))
        per = MLP_SHARD // tk
        return pl.BlockSpec((None, tn, tk), lambda i, j, k: (k // per, l * (MLP_SHARD // tn) + j, k % per))
    return spec


def _w2_rows(l):
    def spec(tm, tn, tk):
        if tk == N_CHIPS * MLP_SHARD:
            return pl.BlockSpec((N_CHIPS, MLP_SHARD, tn), lambda i, j, k: (0, l, j))
        per = MLP_SHARD // tk
        return pl.BlockSpec((None, tk, tn), lambda i, j, k: (k // per, l * per + k % per, j))
    return spec


def _w2_rows_t(l):
    def spec(tm, tn, tk):
        per = MLP_SHARD // tn
        return pl.BlockSpec((None, tn, tk), lambda i, j, k: (j // per, l * per + j % per, k))
    return spec


def _dw1_out(l):
    def spec(tm, tn, tk):
        per = MLP_SHARD // tn
        return pl.BlockSpec((None, tm, tn), lambda i, j, k: (j // per, l * (MLP_SHARD // tm) + i, j % per))
    return spec


def _dw2_out(l):
    def spec(tm, tn, tk):
        per = MLP_SHARD // tm
        return pl.BlockSpec((None, tm, tn), lambda i, j, k: (i // per, l * per + i % per, j))
    return spec


def _pack_vec(d, segs, rows_multiple):
    flat = jnp.concatenate([d[n].reshape(-1) for n, _ in segs])
    rows = -(-flat.shape[0] // LANES)
    rows = -(-rows // rows_multiple) * rows_multiple
    return jnp.pad(flat, (0, rows * LANES - flat.shape[0])).reshape(rows, LANES)


def _unpack_vec(v, segs):
    flat, out, r = v.reshape(-1), {}, 0
    for n, shape in segs:
        k = math.prod(shape)
        out[n] = flat[r:r + k].reshape(shape)
        r += k
    return out


def _block_diag(blocks):
    G, a, b = blocks.shape
    eye = jnp.eye(G, dtype=blocks.dtype)
    return (eye[:, None, :, None] * blocks[:, :, None, :]).reshape(G * a, G * b)


def _diag_blocks(m, G):
    a, b = m.shape[0] // G, m.shape[1] // G
    return jnp.stack([m[g * a:(g + 1) * a, g * b:(g + 1) * b] for g in range(G)])


def _sqrelu_epi(acc):
    r = jnp.maximum(acc, 0.0)
    return acc, r * r


def _sqrelu_bwd_epi(acc, a):
    return (acc * (2.0 * jnp.maximum(a.astype(F32), 0.0)),)


def _mlp_fwd(h, g1, g2, l, tag):
    T, D = h.shape
    a, s = matmul(h, g1, name=f"{tag}_up", mnk=(T, D_FF, D), b_spec=_w1_cols(l), epi=_sqrelu_epi,
                  out_dtypes=(MXU_DTYPE, MXU_DTYPE))
    m = matmul(s, g2, name=f"{tag}_down", mnk=(T, D, D_FF), b_spec=_w2_rows(l))
    return m, (h, a, s)


def _mlp_bwd(saved, dm, g1, g2, l, dg1, dg2, tag):
    h, a, s = saved
    T, D = h.shape
    gshape = (N_CHIPS, 2 * MLP_SHARD, MLP_SHARD)
    da = matmul(dm, g2, tb=True, name=f"{tag}_down_dx", mnk=(T, D_FF, D), b_spec=_w2_rows_t(l),
                epi=_sqrelu_bwd_epi, epi_in=(a,), out_dtype=MXU_DTYPE)
    dg2 = matmul(s, dm, ta=True, name=f"{tag}_down_dw", tm=MLP_SHARD, o_spec=_dw2_out(l), o_shape=gshape, prev=dg2)
    dh = matmul(da, g1, tb=True, name=f"{tag}_up_dx", mnk=(T, D, D_FF), b_spec=_w1_rows_t(l))
    dg1 = matmul(h, da, ta=True, name=f"{tag}_up_dw", o_spec=_dw1_out(l), o_shape=gshape, prev=dg1)
    return dh, dg1, dg2


def kernel(x, mix_pre_g, mix_post_g, mlp_pre_g, mlp_post_g, w_in_even, s5_lam_re, s5_lam_im, s5_log_dt, s5_b_re, s5_b_im, s5_c_re, s5_c_im, s5_d, s5_w_glu, fox_b_f, w_out_even, w_in_odd, pool_w, pool_scale, sgu_ln_g, sgu_ln_b, sgu_w_s, sgu_b_s, w_out_odd, mlp_w1, mlp_w2, loss_target, m_mix_pre_g, m_mix_post_g, m_mlp_pre_g, m_mlp_post_g, m_w_in_even, m_s5_lam_re, m_s5_lam_im, m_s5_log_dt, m_s5_b_re, m_s5_b_im, m_s5_c_re, m_s5_c_im, m_s5_d, m_s5_w_glu, m_fox_b_f, m_w_out_even, m_w_in_odd, m_pool_w, m_pool_scale, m_sgu_ln_g, m_sgu_ln_b, m_sgu_w_s, m_sgu_b_s, m_w_out_odd, m_mlp_w1, m_mlp_w2, v_mix_pre_g, v_mix_post_g, v_mlp_pre_g, v_mlp_post_g, v_w_in_even, v_s5_lam_re, v_s5_lam_im, v_s5_log_dt, v_s5_b_re, v_s5_b_im, v_s5_c_re, v_s5_c_im, v_s5_d, v_s5_w_glu, v_fox_b_f, v_w_out_even, v_w_in_odd, v_pool_w, v_pool_scale, v_sgu_ln_g, v_sgu_ln_b, v_sgu_w_s, v_sgu_b_s, v_w_out_odd, v_mlp_w1, v_mlp_w2):
    names = [n for n, _ in SMALL_SEGS] + [n for n, _, _ in BIG_SEGS] + list(SHARDED_SMALL)
    env = dict(locals())
    W = {n: env[n] for n in names}
    M = {n: env["m_" + n] for n in names}
    V = {n: env["v_" + n] for n in names}

    def shard(n):
        return W[n].reshape(-1, W[n].shape[-1]).astype(WIRE_DTYPE)

    small = jnp.pad(jnp.concatenate([W[n] for n in SHARDED_SMALL]), ((0, SUBLANES - len(SHARDED_SMALL)), (0, 0)))
    gathered, small_all = allgather_chip_shards([shard(n) for n in EARLY_NAMES], small, name="allgather_weights")
    Wf = dict(zip(EARLY_NAMES, gathered))
    for i, n in enumerate(SHARDED_SMALL):
        Wf[n] = small_all[:, i, :].reshape(1, N_CHIPS * LANES)
    for n, _ in SMALL_SEGS:
        Wf[n] = W[n]

    loss8, dx0, halves, local_small = _local_step(x[0], loss_target[0], Wf, [shard(n) for n in LATE_NAMES])
    return _reduce_and_update(W, M, V, loss8, dx0, halves, local_small)


def _reduce_to_my_half(gs, names, tag, carry_swap=None, carry_ici=None):
    cx, cy, cc = _coords()
    c_idx = cc.reshape(1).astype(jnp.int32)
    jc_idx = jnp.stack([2 * cx + cy, cc]).astype(jnp.int32)
    swap = swap_halves_exchange(gs)
    from_sibling = carry_swap(swap) if carry_swap else run_exchange(swap, name=f"{tag}_to_sibling")
    sums = [add_sibling_half(g, l, c_idx, name=f"{tag}_chip_sum_{n}") for n, g, l in zip(names, gs, from_sibling)]
    send = chip_partials_exchange([pb for _, pb in sums])
    from_chips = carry_ici(send) if carry_ici else run_exchange(send, name=f"{tag}_to_chips")
    return [add_chip_partials(pf, r, jc_idx, name=f"{tag}_sum_{n}") for n, (pf, _), r in zip(names, sums, from_chips)]


def _local_step(x0, target, P, late_shards):
    T = x0.shape[0]
    mix_pre_g, mix_post_g, mlp_pre_g, mlp_post_g = P["mix_pre_g"], P["mix_post_g"], P["mlp_pre_g"], P["mlp_post_g"]
    s5_lam_re, s5_lam_im, s5_log_dt = P["s5_lam_re"], P["s5_lam_im"], P["s5_log_dt"]
    s5_b_re, s5_b_im, s5_c_re, s5_c_im, s5_d = P["s5_b_re"], P["s5_b_im"], P["s5_c_re"], P["s5_c_im"], P["s5_d"]
    fox_b_f, pool_w, sgu_w_s, sgu_b_s = P["fox_b_f"], P["pool_w"], P["sgu_w_s"], P["sgu_b_s"]
    pool_scale_f, ln_g_f, ln_b_f = P["pool_scale"], P["sgu_ln_g"], P["sgu_ln_b"]
    w_in_e = jnp.pad(_cols_from_chips(P["w_in_even"]), ((0, 0), (0, EVEN_IN_PAD - EVEN_IN)))
    w_glu = P["s5_w_glu"].reshape(S5_WIDTH, S5_WIDTH)

    def gain(a, l):
        return a[l][None, :]

    lr = s5_lam_re[0].reshape(1, S5_LANES)
    li = s5_lam_im[0].reshape(1, S5_LANES)
    ldt = jnp.repeat(s5_log_dt[0], S5_STATE).reshape(1, S5_LANES)
    btr = s5_b_re[0].reshape(S5_LANES, S5_GROUP).T
    bti = s5_b_im[0].reshape(S5_LANES, S5_GROUP).T
    tf_re, tf_im, tb_re, tb_im, bbt_re, bbt_im = s5_disc_fwd(lr, li, ldt, btr, bti, name="s5_disc")
    same_group = (jnp.arange(S5_WIDTH)[:, None] // S5_GROUP) == (jnp.arange(S5_LANES)[None, :] // S5_STATE)
    b_bd = s5_interleave(jnp.where(same_group, jnp.tile(bbt_re, (S5_GROUPS, 1)), 0.0),
                         jnp.where(same_group, jnp.tile(bbt_im, (S5_GROUPS, 1)), 0.0), axis=1)
    cr2 = jnp.transpose(s5_c_re[0], (0, 2, 1)).reshape(S5_LANES, S5_GROUP)
    ci2 = jnp.transpose(s5_c_im[0], (0, 2, 1)).reshape(S5_LANES, S5_GROUP)
    c_bd = s5_interleave(jnp.where(same_group.T, jnp.tile(cr2, (1, S5_GROUPS)), 0.0),
                         -jnp.where(same_group.T, jnp.tile(ci2, (1, S5_GROUPS)), 0.0), axis=0)
    bf_pad = jnp.pad(fox_b_f, ((0, 0), (0, LANES - FOX_HEADS)))

    h1 = rms_fwd(x0, gain(mix_pre_g, 0), name="l0_pre_norm")
    z = matmul(h1, w_in_e, name="l0_in_proj")
    xs, yc = s5_scan(z, b_bd, tf_re, tf_im, c_bd, reverse=False, name="s5_scan_fwd")
    yl, yg = s5_out_fwd(yc, z, s5_d, name="s5_out")
    gl = matmul(yg, w_glu, name="s5_glu_proj")
    ycat = glu_fwd(yg, gl, out_cols=D_MODEL, name="s5_glu")
    fgate = fox_gate_fwd(z, bf_pad, fl_col=FL_TILE, name="fox_gate")
    f_col = _pairs_col(fgate, T)
    f_row = _col_to_row(f_col, T)
    (ycat, lse_col), late = fox_fwd(z, f_col, f_row, ycat, allgather_ici_exchange(late_shards), name="fox_fwd")
    late = dict(zip(LATE_NAMES, allgather_forward(late, name="allgather_late_weights")))
    w_in_o = _cols_from_chips(late["w_in_odd"])
    w_in_o = jnp.concatenate([w_in_o[:, S5_WIDTH:], w_in_o[:, :S5_WIDTH]], axis=1)
    w_out_e = late["w_out_even"].reshape(D_MODEL, D_MODEL)
    w_out_o = late["w_out_odd"].reshape(D_MODEL, D_MODEL)
    g1, g2 = late["mlp_w1"], late["mlp_w2"]
    mo = matmul(ycat, w_out_e, name="l0_out_proj")
    x1, h2 = res_norm_fwd(x0, mo, gain(mix_post_g, 0), gain(mlp_pre_g, 0), name="l0_post_mlp0_pre_norm")
    m0, mlp0 = _mlp_fwd(h2, g1, g2, 0, "mlp0")

    x2, h3 = res_norm_fwd(x1, m0, gain(mlp_post_g, 0), gain(mix_pre_g, 1), name="mlp0_post_l1_pre_norm")
    z2 = matmul(h3, w_in_o, name="l1_in_proj")
    pooled = pool_window(z2, adjoint=False, in_col=POOL_COL, name="pool_fwd")
    pw_bd = _block_diag(pool_w[0])
    pw = matmul(pooled, pw_bd, name="pool_proj")
    ycat2 = colscale_fwd(pw, pool_scale_f, out_cols=D_MODEL, name="pool_scale")
    causal = jnp.tril(jnp.ones((CHUNK, CHUNK), dtype=bool))
    wsm = jnp.where(causal[None], sgu_w_s[0], 0.0)
    wsmt = jnp.transpose(wsm, (0, 2, 1))
    bst = sgu_b_s[0].T
    ycat2 = sgu_fwd(z2, ln_g_f, ln_b_f, wsm, bst, ycat2, name="sgu_fwd")
    mo2 = matmul(ycat2, w_out_o, name="l1_out_proj")
    x3, h4 = res_norm_fwd(x2, mo2, gain(mix_post_g, 1), gain(mlp_pre_g, 1), name="l1_post_mlp1_pre_norm")
    m1, mlp1 = _mlp_fwd(h4, g1, g2, 1, "mlp1")
    loss8, dx4 = res_norm_loss(x3, m1, gain(mlp_post_g, 1), target, name="mlp1_post_norm_loss")

    dm1, dg_mlp_post1 = rms_bwd(m1, gain(mlp_post_g, 1), dx4, None, name="mlp1_post_norm_bwd")
    dh4, dg1, dg2 = _mlp_bwd(mlp1, dm1, g1, g2, 1, None, None, "mlp1")
    dx3, dmo2, dg_mlp_pre1, dg_mix_post1 = norm_res_bwd(x3, gain(mlp_pre_g, 1), dh4, dx4, mo2, gain(mix_post_g, 1),
                                                        name="mlp1_pre_l1_post_norm_bwd")
    dycat2 = matmul(dmo2, w_out_o, tb=True, name="l1_out_proj_dx")
    dw_out_o = matmul(ycat2, dmo2, ta=True, name="l1_out_proj_dw")
    dpw, dpool_scale = colscale_bwd(pw, pool_scale_f, dycat2, name="pool_scale_bwd")
    dpooled = matmul(dpw, pw_bd, tb=True, name="pool_proj_dx")
    dpw_bd = matmul(pooled, dpw, ta=True, name="pool_proj_dw")
    dz2, dln_g, dln_b, dws, dbst = sgu_bwd(z2, ln_g_f, ln_b_f, wsm, wsmt, bst, dycat2, out_cols=3 * S5_WIDTH,
                                           name="sgu_bwd")
    dz2 = pool_window(dpooled, adjoint=True, into=dz2, out_col=POOL_COL, name="pool_bwd")
    dh3 = matmul(dz2, w_in_o, tb=True, name="l1_in_proj_dx")
    dw_in_o = matmul(h3, dz2, ta=True, name="l1_in_proj_dw")
    dw_in_o = jnp.concatenate([dw_in_o[:, 2 * S5_WIDTH:], dw_in_o[:, :2 * S5_WIDTH]], axis=1)
    dx2, dm0, dg_mix_pre1, dg_mlp_post0 = norm_res_bwd(x2, gain(mix_pre_g, 1), dh3, dx3, m0, gain(mlp_post_g, 0),
                                                       name="l1_pre_mlp0_post_norm_bwd")

    dh2, dg1, dg2 = _mlp_bwd(mlp0, dm0, g1, g2, 0, dg1, dg2, "mlp0")
    dx1, dmo, dg_mlp_pre0, dg_mix_post0 = norm_res_bwd(x1, gain(mlp_pre_g, 0), dh2, dx2, mo, gain(mix_post_g, 0),
                                                       name="mlp0_pre_l0_post_norm_bwd")
    dycat = matmul(dmo, w_out_e, tb=True, name="l0_out_proj_dx")
    dw_out_e = matmul(ycat, dmo, ta=True, name="l0_out_proj_dw")
    dyg_a, dgl = glu_bwd(yg, gl, dycat, name="s5_glu_bwd")
    dyg_b = matmul(dgl, w_glu, tb=True, name="s5_glu_proj_dx")
    dw_glu = matmul(yg, dgl, ta=True, name="s5_glu_proj_dw")
    dyl, du_skip, dd = s5_out_bwd(yl, z, s5_d, dyg_a, dyg_b, name="s5_out_bwd")
    dc_blocks = matmul(xs, dyl, ta=True, mnk=(2 * S5_LANES, S5_CB, T), tm=S5_NB, tn=S5_CB, b_spec=_chan_cols_of_i,
                       exact_tiles=True, name="s5_cx_dw")
    early_grads = {"s5_w_glu": dw_glu.reshape(N_CHIPS, -1, S5_WIDTH), "w_out_even": dw_out_e.reshape(N_CHIPS, -1, D_MODEL),
                   "w_in_odd": _chips_from_cols(dw_in_o), "w_out_odd": dw_out_o.reshape(N_CHIPS, -1, D_MODEL),
                   "mlp_w1": dg1, "mlp_w2": dg2}
    got = {}

    def reverse_scan(exchange):
        (got["lam"], got["du_b"], got["dab_re"], got["dab_im"]), bufs = s5_scan(
            dyl, c_bd, tb_re, tb_im, b_bd, reverse=True, states=xs, hosted=exchange, name="s5_scan_bwd")
        return bufs

    def attention_bwd(exchange):
        dd_col = fox_dd(ycat, dycat, name="fox_dd")
        (got["dk"], got["dv"], got["dfk"], got["dqt"], got["dfq"]), bufs = fox_bwd(
            z, dycat, f_col, f_row, _col_to_row(lse_col, T), _col_to_row(dd_col, T), exchange, name="fox_bwd")
        return bufs

    halves = _reduce_to_my_half([early_grads[n] for n in REDUCED_EARLY], REDUCED_EARLY, "early_grads",
                                reverse_scan, attention_bwd)
    lam, dab_re, dab_im, dk, dv = got["lam"], got["dab_re"], got["dab_im"], got["dk"], got["dv"]
    db_blocks = matmul(z, lam, ta=True, mnk=(S5_CB, 2 * S5_LANES, T), tm=S5_CB, tn=S5_NB, a_spec=_chan_rows_t,
                       exact_tiles=True, name="s5_bu_dw")
    du = add2(du_skip, got["du_b"], name="s5_du")
    dq = jnp.transpose(got["dqt"], (1, 3, 0, 2)).reshape(T, FOX_WIDTH) * (FOX_HEAD_DIM ** -0.5)
    dfl, dbf = fox_gate_bwd(z, bf_pad, _pairs_to_lanes(got["dfk"], T), _pairs_to_lanes(_row_to_col(got["dfq"], T), T),
                            fl_col=FL_TILE, name="fox_gate_bwd")
    dz = jnp.concatenate([du, dq, dk, dv, dfl], axis=1)
    dw_in_e = matmul(h1, dz, ta=True, name="l0_in_proj_dw")[:, :EVEN_IN]

    def in_proj_dx(exchange):
        got["dh1"], bufs = matmul(dz, w_in_e, tb=True, hosted=exchange, name="l0_in_proj_dx")
        return bufs

    def pre_norm_bwd(exchange):
        (got["dx0"], got["dg_mix_pre0"]), bufs = rms_bwd(x0, gain(mix_pre_g, 0), got["dh1"], dx1, hosted=exchange,
                                                         name="l0_pre_norm_bwd")
        return bufs

    halves = halves + _reduce_to_my_half([_chips_from_cols(dw_in_e)], ["w_in_even"], "late_grads", in_proj_dx, pre_norm_bwd)
    dx0, dg_mix_pre0 = got["dx0"], got["dg_mix_pre0"]

    groups_per_block = S5_CB // S5_GROUP
    own_group = (jnp.arange(S5_CB)[:, None] // S5_GROUP) == ((jnp.arange(S5_LANES)[None, :] // S5_STATE) % groups_per_block)
    db_re, db_im = s5_deinterleave(db_blocks, axis=1)
    dbbt_re = jnp.where(own_group, db_re, 0.0).reshape(groups_per_block, S5_GROUP, S5_LANES).sum(0)
    dbbt_im = jnp.where(own_group, db_im, 0.0).reshape(groups_per_block, S5_GROUP, S5_LANES).sum(0)
    dlr, dli, dldt8, dbtr, dbti = s5_disc_bwd(lr, li, ldt, btr, bti, dab_re, dab_im, dbbt_re, dbbt_im, name="s5_disc_bwd")
    dc_re, dc_im = s5_deinterleave(dc_blocks, axis=0)
    dcr2 = jnp.where(own_group.T, dc_re, 0.0).reshape(S5_LANES, groups_per_block, S5_GROUP).sum(1)
    dci2 = -jnp.where(own_group.T, dc_im, 0.0).reshape(S5_LANES, groups_per_block, S5_GROUP).sum(1)

    def c_layout(a):
        return jnp.transpose(a.reshape(S5_GROUPS, S5_STATE, S5_GROUP), (0, 2, 1))[None]

    def b_layout(a):
        return a.T.reshape(1, S5_GROUPS, S5_STATE, S5_GROUP)

    local_small = {
        "mix_pre_g": jnp.concatenate([dg_mix_pre0, dg_mix_pre1]), "mix_post_g": jnp.concatenate([dg_mix_post0, dg_mix_post1]),
        "mlp_pre_g": jnp.concatenate([dg_mlp_pre0, dg_mlp_pre1]), "mlp_post_g": jnp.concatenate([dg_mlp_post0, dg_mlp_post1]),
        "s5_lam_re": dlr.reshape(1, S5_GROUPS, S5_STATE), "s5_lam_im": dli.reshape(1, S5_GROUPS, S5_STATE),
        "s5_log_dt": dldt8[0:1, 0:S5_GROUPS],
        "s5_b_re": b_layout(dbtr), "s5_b_im": b_layout(dbti), "s5_c_re": c_layout(dcr2), "s5_c_im": c_layout(dci2),
        "s5_d": dd, "fox_b_f": dbf[:, 0:FOX_HEADS],
        "pool_w": _diag_blocks(dpw_bd, len(POOL_WINDOWS))[None],
        "sgu_w_s": jnp.where(causal[None], dws, 0.0)[None], "sgu_b_s": dbst.T[None],
        "pool_scale": dpool_scale, "sgu_ln_g": dln_g, "sgu_ln_b": dln_b,
    }
    return loss8, dx0, dict(zip(REDUCED_EARLY + ("w_in_even",), halves)), local_small


def _reduce_and_update(W, M, V, loss8, dx0, halves, local_small):
    cx, cy, cc = _coords()
    chip = 2 * cx + cy

    summed = dict(local_small, loss=loss8[0:1, 0:1])
    vec = _pack_vec(summed, REDUCED_SEGS, N_DEV * SUBLANES)
    piece = vec.shape[0] // N_DEV
    landed = exchange_pieces(vec.reshape(N_DEV, piece, LANES), scatter=True, name="small_grads_scatter")
    mine = sum_pieces(landed, name="small_grads_sum")
    everyone = exchange_pieces(mine, scatter=False, name="small_grads_gather")
    G = _unpack_vec(everyone, REDUCED_SEGS)
    loss = G["loss"].reshape(())
    for n in SHARDED_SMALL:
        G[n] = lax.dynamic_slice_in_dim(G[n], chip * LANES, LANES, axis=1)

    reduced = join_sibling_halves([halves[n] for n in BIG_NAMES], name="big_grads_join")
    for n, r in zip(BIG_NAMES, reduced):
        G[n] = r.reshape(W[n].shape)

    def two_d(a):
        return a.reshape(-1, a.shape[-1])

    delta, new_m, new_v = {}, {}, {}
    for n in BIG_NAMES:
        d_, m_, v_ = adamw(two_d(W[n]), two_d(G[n]), two_d(M[n]), two_d(V[n]), name=f"adamw_{n}")
        delta[n], new_m[n], new_v[n] = (t.reshape(W[n].shape) for t in (d_, m_, v_))
    packed = [_pack_vec(src, SMALL_SEGS, SUBLANES) for src in (W, G, M, V)]
    outs = adamw(*packed, name="adamw_replicated")
    for dst, t in zip((delta, new_m, new_v), outs):
        dst.update(_unpack_vec(t, SMALL_SEGS))
    sharded_segs = tuple((n, (1, LANES)) for n in SHARDED_SMALL)
    packed = [_pack_vec(src, sharded_segs, 1) for src in (W, G, M, V)]
    outs = adamw(*packed, name="adamw_sharded_vectors")
    for dst, t in zip((delta, new_m, new_v), outs):
        dst.update(_unpack_vec(t, sharded_segs))

    order = ["mix_pre_g", "mix_post_g", "mlp_pre_g", "mlp_post_g", "w_in_even", "s5_lam_re", "s5_lam_im", "s5_log_dt",
             "s5_b_re", "s5_b_im", "s5_c_re", "s5_c_im", "s5_d", "s5_w_glu", "fox_b_f", "w_out_even", "w_in_odd",
             "pool_w", "pool_scale", "sgu_ln_g", "sgu_ln_b", "sgu_w_s", "sgu_b_s", "w_out_odd", "mlp_w1", "mlp_w2"]
    return (loss, dx0[None], *[G[n] for n in order], *[delta[n] for n in order],
            *[new_m[n] for n in order], *[new_v[n] for n in order])
```

```python
import functools
import math

import jax
import jax.numpy as jnp
from jax import lax
from jax.experimental import pallas as pl
from jax.experimental.pallas import tpu as pltpu

F32 = jnp.float32
MXU_DTYPE = jnp.bfloat16
WIRE_DTYPE = jnp.bfloat16
EPS = 1e-6
VMEM_LIMIT_BYTES = 48 * 1024 * 1024
LANES = 128
SUBLANES = 8

D_MODEL = 1024
S5_WIDTH = 512
S5_GROUP = 16
S5_GROUPS = 32
S5_STATE = 64
S5_LANES = S5_GROUPS * S5_STATE
FOX_HEADS = 8
FOX_HEAD_DIM = 64
FOX_WIDTH = 512
EVEN_IN = S5_WIDTH + 3 * FOX_WIDTH + FOX_HEADS
EVEN_IN_PAD = 2176
POOL_WINDOWS = (2, 4, 8, 16)
POOL_HALO = 16
POOL_GROUP_DIM = 128
SGU_GROUPS = 4
SGU_GROUP_DIM = 128
CHUNK = 128
D_FF = 4096

ADAM_LR = 0.001
ADAM_B1 = 0.9
ADAM_B2 = 0.999
ADAM_EPS = 1e-08
ADAM_WD = 0.01
ADAM_STEP = 10

MESH_AXES = ("x", "y", "c")
MESH = pl.DeviceIdType.MESH
N_CHIPS = 4
N_DEV = 8

SDS = jax.ShapeDtypeStruct


def _cp(*sem):
    return pltpu.CompilerParams(dimension_semantics=sem, vmem_limit_bytes=VMEM_LIMIT_BYTES)


def _pick(dim, pref):
    if dim <= pref:
        return dim
    t = pref
    while t >= 256:
        if dim % t == 0:
            return t
        t //= 2
    return dim


def _row(tr, c):
    return pl.BlockSpec((tr, c), lambda i: (i, 0))


def _full(shape):
    nd = len(shape)
    return pl.BlockSpec(shape, lambda *_: (0,) * nd)


def _gelu_grad(x):
    c = math.sqrt(2.0 / math.pi)
    t = jnp.tanh(c * (x + 0.044715 * x * x * x))
    return 0.5 * (1.0 + t) + 0.5 * x * (1.0 - t * t) * c * (1.0 + 3.0 * 0.044715 * x * x)


MATMUL_VMEM_BYTES = 36 * 1024 * 1024


def matmul(a, b, *, name, ta=False, tb=False, out_dtype=F32, tm=2048, tn=1024, tk=4096, mnk=None, a_koff=0,
           a_spec=None, b_spec=None, o_spec=None, o_shape=None, prev=None, epi=None, epi_in=(), out_dtypes=None,
           exact_tiles=False, hosted=None):
    if mnk is None:
        M, K = (a.shape[1], a.shape[0]) if ta else a.shape
        K2, N = (b.shape[1], b.shape[0]) if tb else b.shape
        assert K == K2, (a.shape, b.shape, ta, tb)
    else:
        M, N, K = mnk
    out_dtypes = tuple(out_dtypes) if out_dtypes is not None else (out_dtype,)
    n_out, n_epi = len(out_dtypes), len(epi_in)
    tm, tn, tk = _pick(M, tm), _pick(N, tn), _pick(K, tk)

    def vmem_bytes(tm_, tn_, tk_):
        tiles = tm_ * tk_ * a.dtype.itemsize + tk_ * tn_ * b.dtype.itemsize
        tiles += tm_ * tn_ * (sum(jnp.dtype(d).itemsize for d in out_dtypes) + sum(e.dtype.itemsize for e in epi_in))
        return 2 * tiles + tm_ * tn_ * 4 * (tk_ < K)

    def halves(t, dim):
        return [t] + ([t // 2] if t % (2 * LANES) == 0 and t // 2 >= 512 and dim % (t // 2) == 0 else [])

    if exact_tiles:
        halves = lambda t, dim: [t]
    fits = [(m_, n_) for m_ in halves(tm, M) for n_ in halves(tn, N) if vmem_bytes(m_, n_, tk) <= MATMUL_VMEM_BYTES]
    if fits:
        tm, tn = max(fits, key=lambda t: (t[0] * t[1], t[0]))
    else:
        tm, tn = halves(tm, M)[-1], halves(tn, N)[-1]
        while vmem_bytes(tm, tn, tk) > MATMUL_VMEM_BYTES and tk % 2 == 0 and tk > 512:
            tk //= 2
    nk = K // tk
    assert a_koff % tk == 0 and not (ta and a_koff)
    ko = a_koff // tk
    dn = (((0 if ta else 1,), (1 if tb else 0,)), ((), ()))

    def body(*refs):
        a_ref, b_ref = refs[0], refs[1]
        epi_refs = refs[2:2 + n_epi]
        o_refs = refs[len(refs) - n_out - (nk > 1):len(refs) - (nk > 1)]
        k = pl.program_id(2)
        bv = b_ref[...]
        if bv.ndim == 3 and tb:
            cw = bv.shape[-1]
            prod = sum(lax.dot_general(a_ref[:, c * cw:(c + 1) * cw].astype(MXU_DTYPE), bv[c].astype(MXU_DTYPE), dn,
                                       preferred_element_type=F32) for c in range(bv.shape[0]))
        else:
            if bv.ndim == 3:
                bv = bv.reshape(-1, bv.shape[-1])
            prod = lax.dot_general(a_ref[...].astype(MXU_DTYPE), bv.astype(MXU_DTYPE), dn, preferred_element_type=F32)

        def finish(acc):
            res = (acc,) if epi is None else epi(acc, *[r[...] for r in epi_refs])
            for o_ref, r in zip(o_refs, res):
                o_ref[...] = r.astype(o_ref.dtype)

        if nk == 1:
            finish(prod)
            return
        acc_ref = refs[-1]

        @pl.when(k == 0)
        def _():
            acc_ref[...] = prod

        @pl.when(jnp.logical_and(k > 0, k < nk - 1))
        def _():
            acc_ref[...] += prod

        @pl.when(k == nk - 1)
        def _():
            finish(acc_ref[...] + prod)

    if a_spec is None:
        a_spec = pl.BlockSpec((tk, tm), lambda i, j, k: (k, i)) if ta else pl.BlockSpec((tm, tk), lambda i, j, k: (i, k + ko))
    else:
        a_spec = a_spec(tm, tn, tk)
    if b_spec is None:
        bs = pl.BlockSpec((tn, tk), lambda i, j, k: (j, k)) if tb else pl.BlockSpec((tk, tn), lambda i, j, k: (k, j))
    else:
        bs = b_spec(tm, tn, tk)
    tile = pl.BlockSpec((tm, tn), lambda i, j, k: (i, j))
    os_ = tile if o_spec is None else o_spec(tm, tn, tk)
    ins, in_specs, aliases = [a, b, *epi_in], [a_spec, bs] + [tile] * n_epi, {}
    if prev is not None:
        aliases = {len(ins): 0}
        ins.append(prev)
        in_specs.append(pl.BlockSpec(memory_space=pl.ANY))
    shapes = [SDS((M, N) if o_shape is None else o_shape, dt) for dt in out_dtypes]
    scratch = [pltpu.VMEM((tm, tn), F32)] if nk > 1 else []
    if hosted is not None:
        outs, bufs = call_hosting(body, hosted, name=name, grid=(M // tm, N // tn, nk), in_specs=in_specs,
                                  out_specs=[os_] * n_out, out_shape=shapes, inputs=ins, aliases=aliases, scratch=scratch)
        return (outs[0] if n_out == 1 else outs), bufs
    outs = pl.pallas_call(
        body, name=name, grid=(M // tm, N // tn, nk),
        in_specs=in_specs, out_specs=[os_] * n_out, out_shape=shapes, input_output_aliases=aliases,
        scratch_shapes=scratch, compiler_params=_cp("parallel", "parallel", "arbitrary"),
    )(*ins)
    return outs[0] if n_out == 1 else outs


def _rms_hat(x):
    return x * lax.rsqrt(jnp.mean(x * x, axis=-1, keepdims=True) + EPS)


def rms_fwd(x, g, *, name):
    T, D = x.shape
    tr = _pick(T, 512)

    def body(x_ref, g_ref, o_ref):
        o_ref[...] = (_rms_hat(x_ref[...]) * g_ref[...]).astype(o_ref.dtype)

    return pl.pallas_call(body, name=name, grid=(T // tr,), in_specs=[_row(tr, D), _full((1, D))],
                          out_specs=_row(tr, D), out_shape=SDS((T, D), MXU_DTYPE), compiler_params=_cp("parallel"))(x, g)


def res_norm_fwd(x, y, g_post, g_next, *, name):
    T, D = x.shape
    tr = _pick(T, 512)

    def body(x_ref, y_ref, gp_ref, gn_ref, o_ref, h_ref):
        xn = x_ref[...] + _rms_hat(y_ref[...]) * gp_ref[...]
        o_ref[...] = xn
        h_ref[...] = (_rms_hat(xn) * gn_ref[...]).astype(h_ref.dtype)

    return pl.pallas_call(body, name=name, grid=(T // tr,),
                          in_specs=[_row(tr, D), _row(tr, D), _full((1, D)), _full((1, D))],
                          out_specs=[_row(tr, D), _row(tr, D)], out_shape=[SDS((T, D), F32), SDS((T, D), MXU_DTYPE)],
                          compiler_params=_cp("parallel"))(x, y, g_post, g_next)


def res_norm_loss(x, y, g_post, target, *, name):
    T, D = x.shape
    tr = _pick(T, 512)

    def body(x_ref, y_ref, g_ref, t_ref, l_ref, d_ref):
        err = x_ref[...] + _rms_hat(y_ref[...]) * g_ref[...] - t_ref[...]
        d_ref[...] = err * (1.0 / D)

        @pl.when(pl.program_id(0) == 0)
        def _():
            l_ref[...] = jnp.zeros_like(l_ref)

        l_ref[...] += 0.5 * jnp.sum(jnp.mean(err * err, axis=-1, keepdims=True))

    return pl.pallas_call(body, name=name, grid=(T // tr,),
                          in_specs=[_row(tr, D), _row(tr, D), _full((1, D)), _row(tr, D)],
                          out_specs=[_full((SUBLANES, LANES)), _row(tr, D)],
                          out_shape=[SDS((SUBLANES, LANES), F32), SDS((T, D), F32)],
                          compiler_params=_cp("arbitrary"))(x, y, g_post, target)


def _rms_bwd_rows(x, g, dy):
    r = lax.rsqrt(jnp.mean(x * x, axis=-1, keepdims=True) + EPS)
    xh = x * r
    dxh = dy * g
    return r * (dxh - xh * jnp.mean(dxh * xh, axis=-1, keepdims=True)), jnp.sum(dy * xh, axis=0, keepdims=True)


def norm_res_bwd(x, g_pre, dh, res, y, g_post, *, name):
    T, D = x.shape
    tr = _pick(T, 512)

    def body(x_ref, gp_ref, dh_ref, res_ref, y_ref, gy_ref, dx_ref, dy_ref, dgp_ref, dgy_ref):
        dx, dgp = _rms_bwd_rows(x_ref[...], gp_ref[...], dh_ref[...])
        dx = dx + res_ref[...]
        dx_ref[...] = dx
        dy, dgy = _rms_bwd_rows(y_ref[...], gy_ref[...], dx)
        dy_ref[...] = dy.astype(dy_ref.dtype)

        @pl.when(pl.program_id(0) == 0)
        def _():
            dgp_ref[...] = jnp.zeros_like(dgp_ref)
            dgy_ref[...] = jnp.zeros_like(dgy_ref)

        dgp_ref[...] += dgp
        dgy_ref[...] += dgy

    row, vec = _row(tr, D), _full((1, D))
    return pl.pallas_call(body, name=name, grid=(T // tr,), in_specs=[row, vec, row, row, row, vec],
                          out_specs=[row, row, vec, vec],
                          out_shape=[SDS((T, D), F32), SDS((T, D), MXU_DTYPE), SDS((1, D), F32), SDS((1, D), F32)],
                          compiler_params=_cp("arbitrary"))(x, g_pre, dh, res, y, g_post)


def rms_bwd(x, g, dy, res, *, name, hosted=None):
    T, D = x.shape
    tr = _pick(T, 512)
    has_res = res is not None

    def body(*refs):
        if has_res:
            x_ref, g_ref, dy_ref, res_ref, dx_ref, dg_ref = refs
        else:
            x_ref, g_ref, dy_ref, dx_ref, dg_ref = refs
        dx, dg = _rms_bwd_rows(x_ref[...], g_ref[...], dy_ref[...])
        if has_res:
            dx = dx + res_ref[...]
        dx_ref[...] = dx.astype(dx_ref.dtype)

        @pl.when(pl.program_id(0) == 0)
        def _():
            dg_ref[...] = jnp.zeros_like(dg_ref)

        dg_ref[...] += dg

    ins = [x, g, dy] + ([res] if has_res else [])
    in_specs = [_row(tr, D), _full((1, D)), _row(tr, D)] + ([_row(tr, D)] if has_res else [])
    out_shape = [SDS((T, D), F32 if has_res else MXU_DTYPE), SDS((1, D), F32)]
    out_specs = [_row(tr, D), _full((1, D))]
    if hosted is not None:
        return call_hosting(body, hosted, name=name, grid=(T // tr,), in_specs=in_specs, out_specs=out_specs,
                            out_shape=out_shape, inputs=ins, aliases={})
    return pl.pallas_call(body, name=name, grid=(T // tr,), in_specs=in_specs, out_specs=out_specs,
                          out_shape=out_shape, compiler_params=_cp("arbitrary"))(*ins)


def _s5_disc(lr, li, ldt, btr, bti):
    dt = jnp.exp(ldt)
    k = lax.broadcasted_iota(jnp.int32, (SUBLANES, S5_LANES), 0).astype(F32)
    kf = k + 1.0
    kb = 8.0 - k
    ph = li * dt
    lm = lr * dt
    tf_re = jnp.exp(kf * lm) * jnp.cos(kf * ph)
    tf_im = jnp.exp(kf * lm) * jnp.sin(kf * ph)
    tb_re = jnp.exp(kb * lm) * jnp.cos(kb * ph)
    tb_im = -jnp.exp(kb * lm) * jnp.sin(kb * ph)
    mag = jnp.exp(lm)
    ab_re = mag * jnp.cos(ph)
    ab_im = mag * jnp.sin(ph)
    den = lr * lr + li * li
    nr = ab_re - 1.0
    ni = ab_im
    q_re = (nr * lr + ni * li) / den
    q_im = (ni * lr - nr * li) / den
    bbt_re = q_re * btr - q_im * bti
    bbt_im = q_re * bti + q_im * btr
    return tf_re, tf_im, tb_re, tb_im, bbt_re, bbt_im


def _s5_disc_core(lr, li, ldt, btr, bti):
    dt = jnp.exp(ldt)
    mag = jnp.exp(lr * dt)
    ab_re = mag * jnp.cos(li * dt)
    ab_im = mag * jnp.sin(li * dt)
    den = lr * lr + li * li
    nr = ab_re - 1.0
    ni = ab_im
    q_re = (nr * lr + ni * li) / den
    q_im = (ni * lr - nr * li) / den
    return ab_re, ab_im, q_re * btr - q_im * bti, q_re * bti + q_im * btr


def s5_disc_fwd(lr, li, ldt, btr, bti, *, name):
    def body(lr_ref, li_ref, ldt_ref, btr_ref, bti_ref, *outs):
        vals = _s5_disc(lr_ref[...], li_ref[...], ldt_ref[...], btr_ref[...], bti_ref[...])
        for o, v in zip(outs, vals):
            o[...] = v

    tab = SDS((SUBLANES, S5_LANES), F32)
    bb = SDS((S5_GROUP, S5_LANES), F32)
    return pl.pallas_call(body, name=name, out_shape=[tab, tab, tab, tab, bb, bb])(lr, li, ldt, btr, bti)


def s5_disc_bwd(lr, li, ldt, btr, bti, dab_re, dab_im, dbbt_re, dbbt_im, *, name):
    def body(lr_ref, li_ref, ldt_ref, btr_ref, bti_ref, dar_ref, dai_ref, dbr_ref, dbi_ref,
             dlr_ref, dli_ref, dldt_ref, dbtr_ref, dbti_ref):
        _, vjp = jax.vjp(_s5_disc_core, lr_ref[...], li_ref[...], ldt_ref[...], btr_ref[...], bti_ref[...])
        dlr, dli, dldt, dbtr, dbti = vjp((dar_ref[...], dai_ref[...], dbr_ref[...], dbi_ref[...]))
        dlr_ref[...] = dlr
        dli_ref[...] = dli
        dbtr_ref[...] = dbtr
        dbti_ref[...] = dbti
        lane_group = lax.broadcasted_iota(jnp.int32, (S5_LANES, LANES), 0) // S5_STATE
        col = lax.broadcasted_iota(jnp.int32, (S5_LANES, LANES), 1)
        ind = (lane_group == col).astype(F32)
        dldt_ref[...] = jnp.dot(jnp.broadcast_to(dldt, (SUBLANES, S5_LANES)), ind,
                                precision=lax.Precision.HIGHEST, preferred_element_type=F32)

    row = SDS((1, S5_LANES), F32)
    bb = SDS((S5_GROUP, S5_LANES), F32)
    return pl.pallas_call(body, name=name, out_shape=[row, row, SDS((SUBLANES, LANES), F32), bb, bb])(
        lr, li, ldt, btr, bti, dab_re, dab_im, dbbt_re, dbbt_im)


S5_NB = 1024


S5_CB = S5_WIDTH * S5_NB // S5_LANES


def _chan_rows_t(tm, tn, tk):
    return pl.BlockSpec((tk, S5_CB), lambda i, j, k: (k, j // 2))


def _chan_cols_of_i(tm, tn, tk):
    return pl.BlockSpec((tk, S5_CB), lambda i, j, k: (k, i // 2))


def _lanes_of_chan(tm, tn, tk):
    return pl.BlockSpec((tm, 2 * S5_NB), lambda i, j, k: (i, j))


def _s5_b_block_t(tm, tn, tk):
    return pl.BlockSpec((S5_CB, 2 * S5_NB), lambda i, j, k: (j, j))


def _s5_c_block(tm, tn, tk):
    return pl.BlockSpec((2 * S5_NB, S5_CB), lambda i, j, k: (j, j))


def s5_interleave(re, im, axis):
    parts = []
    for n in range(S5_LANES // S5_NB):
        sl = [slice(None)] * re.ndim
        sl[axis] = slice(n * S5_NB, (n + 1) * S5_NB)
        parts += [re[tuple(sl)], im[tuple(sl)]]
    return jnp.concatenate(parts, axis=axis)


def s5_deinterleave(a, axis):
    re, im = [], []
    for n in range(S5_LANES // S5_NB):
        sl = [slice(None)] * a.ndim
        sl[axis] = slice(2 * n * S5_NB, (2 * n + 1) * S5_NB)
        re.append(a[tuple(sl)])
        sl[axis] = slice((2 * n + 1) * S5_NB, (2 * n + 2) * S5_NB)
        im.append(a[tuple(sl)])
    return jnp.concatenate(re, axis=axis), jnp.concatenate(im, axis=axis)


def s5_scan(src, mat, tab_re, tab_im, *, reverse, name, states=None, hosted=None):
    T = src.shape[0]
    nb = S5_NB
    tc = _pick(T, 256)
    nl = S5_LANES // nb
    nt = T // tc
    ntile = tc // SUBLANES
    with_da = states is not None
    assert reverse or not with_da
    step_rows = ((1, 7), (2, 6), (4, 4)) if reverse else ((1, 0), (2, 1), (4, 3))
    drive_dn = _NT if reverse else (((1,), (0,)), ((), ()))

    def body(*refs):
        if with_da:
            (src_ref, wr_ref, wi_ref, tr_ref, ti_ref, sr_ref, si_ref, hr_ref, hi_ref, xo_ref, dar_ref, dai_ref,
             cr_ref, ci_ref, mr_ref, mi_ref, br_ref, bi_ref, ar_ref, ai_ref) = refs
        else:
            src_ref, wr_ref, wi_ref, tr_ref, ti_ref, xo_ref, cr_ref, ci_ref, mr_ref, mi_ref, br_ref, bi_ref = refs

        @pl.when(pl.program_id(1) == 0)
        def _():
            cr_ref[...] = jnp.zeros_like(cr_ref)
            ci_ref[...] = jnp.zeros_like(ci_ref)
            if with_da:
                ar_ref[...] = jnp.zeros_like(ar_ref)
                ai_ref[...] = jnp.zeros_like(ai_ref)

        lhs = src_ref[...].astype(MXU_DTYPE)
        br_ref[...] = lax.dot_general(lhs, wr_ref[...].astype(MXU_DTYPE), drive_dn, preferred_element_type=F32)
        bi_ref[...] = lax.dot_general(lhs, wi_ref[...].astype(MXU_DTYPE), drive_dn, preferred_element_type=F32)

        seen = jnp.where(pl.program_id(1) < nt - 1, 1.0, 0.0)

        def add_da(lr, li, r0, last_r, last_i):
            first = lax.broadcasted_iota(jnp.int32, (SUBLANES, nb), 0) == 0
            pr = jnp.where(first, last_r, pltpu.roll(sr_ref[pl.ds(r0, SUBLANES), :], 1, 0))
            pi = jnp.where(first, last_i, pltpu.roll(si_ref[pl.ds(r0, SUBLANES), :], 1, 0))
            ar_ref[...] += lr * pr + li * pi
            ai_ref[...] += li * pr - lr * pi

        io = lax.broadcasted_iota(jnp.int32, (SUBLANES, nb), 0)
        for s_, (d, r) in enumerate(step_rows):
            keep = (io < SUBLANES - d) if reverse else (io >= d)
            mr_ref[s_] = jnp.where(keep, tr_ref[r:r + 1, :], 0.0)
            mi_ref[s_] = jnp.where(keep, ti_ref[r:r + 1, :], 0.0)

        def tile(i, carry):
            cr, ci = carry
            j = (ntile - 1 - i) if reverse else i
            r0 = pl.multiple_of(j * SUBLANES, SUBLANES)
            xr = br_ref[pl.ds(r0, SUBLANES), :]
            xi = bi_ref[pl.ds(r0, SUBLANES), :]
            for s_, (d, _) in enumerate(step_rows):
                sh = (SUBLANES - d) if reverse else d
                sr = pltpu.roll(xr, sh, 0)
                si = pltpu.roll(xi, sh, 0)
                pr, pi = mr_ref[s_], mi_ref[s_]
                xr, xi = xr + pr * sr - pi * si, xi + pr * si + pi * sr
            tr, ti = tr_ref[...], ti_ref[...]
            xr, xi = xr + tr * cr - ti * ci, xi + tr * ci + ti * cr
            xo_ref[pl.ds(r0, SUBLANES), 0:nb] = xr
            xo_ref[pl.ds(r0, SUBLANES), nb:2 * nb] = xi
            if with_da:
                @pl.when(j > 0)
                def _():
                    p0 = pl.multiple_of(r0 - SUBLANES, SUBLANES)
                    add_da(xr, xi, r0, sr_ref[pl.ds(p0, SUBLANES), :][SUBLANES - 1:SUBLANES, :],
                           si_ref[pl.ds(p0, SUBLANES), :][SUBLANES - 1:SUBLANES, :])

                @pl.when(j == 0)
                def _():
                    add_da(xr, xi, r0, hr_ref[SUBLANES - 1:SUBLANES, :] * seen, hi_ref[SUBLANES - 1:SUBLANES, :] * seen)
            if reverse:
                return xr[0:1, :], xi[0:1, :]
            return xr[SUBLANES - 1:SUBLANES, :], xi[SUBLANES - 1:SUBLANES, :]

        cr, ci = lax.fori_loop(0, ntile, tile, (cr_ref[0:1, :], ci_ref[0:1, :]))
        cr_ref[0:1, :] = cr
        ci_ref[0:1, :] = ci
        if with_da:
            @pl.when(pl.program_id(1) == nt - 1)
            def _():
                dar_ref[...] = jnp.sum(ar_ref[...], axis=0, keepdims=True)
                dai_ref[...] = jnp.sum(ai_ref[...], axis=0, keepdims=True)

    def tmap(t):
        return (nt - 1 - t) if reverse else t

    hb = tc // SUBLANES
    re_spec = pl.BlockSpec((tc, nb), lambda n, t: (tmap(t), 2 * n))
    im_spec = pl.BlockSpec((tc, nb), lambda n, t: (tmap(t), 2 * n + 1))
    tab_spec = pl.BlockSpec((SUBLANES, nb), lambda n, t: (0, n))
    out_spec = pl.BlockSpec((tc, 2 * nb), lambda n, t: (tmap(t), n))
    out_shape = SDS((T, 2 * S5_LANES), F32)
    scratch = [pltpu.VMEM((SUBLANES, nb), F32), pltpu.VMEM((SUBLANES, nb), F32),
               pltpu.VMEM((len(step_rows), SUBLANES, nb), F32), pltpu.VMEM((len(step_rows), SUBLANES, nb), F32),
               pltpu.VMEM((tc, nb), F32), pltpu.VMEM((tc, nb), F32)]
    src_spec = pl.BlockSpec((tc, S5_CB), lambda n, t: (tmap(t), n))
    if reverse:
        wr_spec = pl.BlockSpec((nb, S5_CB), lambda n, t: (2 * n, n))
        wi_spec = pl.BlockSpec((nb, S5_CB), lambda n, t: (2 * n + 1, n))
    else:
        wr_spec = pl.BlockSpec((S5_CB, nb), lambda n, t: (n, 2 * n))
        wi_spec = pl.BlockSpec((S5_CB, nb), lambda n, t: (n, 2 * n + 1))
    drive_specs = [src_spec, wr_spec, wi_spec, tab_spec, tab_spec]
    drive = [src, mat, mat, tab_re, tab_im]
    if not with_da:
        return pl.pallas_call(body, name=name, grid=(nl, nt), in_specs=drive_specs,
                              out_specs=out_spec, out_shape=out_shape, scratch_shapes=scratch,
                              compiler_params=_cp("parallel", "arbitrary"))(*drive)
    re_halo = pl.BlockSpec((SUBLANES, nb), lambda n, t: (jnp.maximum(tmap(t) * hb - 1, 0), 2 * n))
    im_halo = pl.BlockSpec((SUBLANES, nb), lambda n, t: (jnp.maximum(tmap(t) * hb - 1, 0), 2 * n + 1))
    acc = pl.BlockSpec((1, nb), lambda n, t: (0, n))
    row = SDS((1, S5_LANES), F32)
    return call_hosting(
        body, hosted, name=name, grid=(nl, nt),
        in_specs=drive_specs + [re_spec, im_spec, re_halo, im_halo],
        out_specs=[out_spec, acc, acc], out_shape=[out_shape, row, row],
        inputs=drive + [states, states, states, states], aliases={},
        scratch=scratch + [pltpu.VMEM((SUBLANES, nb), F32), pltpu.VMEM((SUBLANES, nb), F32)])


def s5_out_fwd(yc, u, d, *, name):
    T, C = yc.shape
    tr = _pick(T, 512)

    def body(yc_ref, u_ref, d_ref, yl_ref, yg_ref):
        yl = yc_ref[...] + d_ref[...] * u_ref[...]
        yl_ref[...] = yl
        yg_ref[...] = jax.nn.gelu(yl)

    return pl.pallas_call(body, name=name, grid=(T // tr,), in_specs=[_row(tr, C), _row(tr, C), _full((1, C))],
                          out_specs=[_row(tr, C)] * 2, out_shape=[SDS((T, C), F32)] * 2,
                          compiler_params=_cp("parallel"))(yc, u, d)


def glu_fwd(yg, gl, *, out_cols, name):
    T, C = yg.shape
    tr = _pick(T, 512)

    def body(yg_ref, gl_ref, o_ref):
        o_ref[...] = yg_ref[...] * jax.nn.sigmoid(gl_ref[...])

    return pl.pallas_call(body, name=name, grid=(T // tr,), in_specs=[_row(tr, C)] * 2, out_specs=_row(tr, C),
                          out_shape=SDS((T, out_cols), F32), compiler_params=_cp("parallel"))(yg, gl)


def glu_bwd(yg, gl, dy, *, name):
    T, C = yg.shape
    tr = _pick(T, 512)

    def body(yg_ref, gl_ref, dy_ref, dyg_ref, dgl_ref):
        s = jax.nn.sigmoid(gl_ref[...])
        dyv = dy_ref[...]
        dyg_ref[...] = dyv * s
        dgl_ref[...] = dyv * yg_ref[...] * s * (1.0 - s)

    return pl.pallas_call(body, name=name, grid=(T // tr,), in_specs=[_row(tr, C)] * 3, out_specs=[_row(tr, C)] * 2,
                          out_shape=[SDS((T, C), F32)] * 2, compiler_params=_cp("parallel"))(yg, gl, dy)


def s5_out_bwd(yl, u, d, dyg_a, dyg_b, *, name):
    T, C = yl.shape
    tr = _pick(T, 512)

    def body(yl_ref, u_ref, d_ref, da_ref, db_ref, dyl_ref, du_ref, dd_ref):
        dyl = (da_ref[...] + db_ref[...]) * _gelu_grad(yl_ref[...])
        dyl_ref[...] = dyl
        du_ref[...] = dyl * d_ref[...]

        @pl.when(pl.program_id(0) == 0)
        def _():
            dd_ref[...] = jnp.zeros_like(dd_ref)

        dd_ref[...] += jnp.sum(dyl * u_ref[...], axis=0, keepdims=True)

    return pl.pallas_call(body, name=name, grid=(T // tr,),
                          in_specs=[_row(tr, C), _row(tr, C), _full((1, C)), _row(tr, C), _row(tr, C)],
                          out_specs=[_row(tr, C), _row(tr, C), _full((1, C))],
                          out_shape=[SDS((T, C), F32), SDS((T, C), F32), SDS((1, C), F32)],
                          compiler_params=_cp("arbitrary"))(yl, u, d, dyg_a, dyg_b)


def add2(a, b, *, name):
    T, C = a.shape
    tr = _pick(T, 512)

    def body(a_ref, b_ref, o_ref):
        o_ref[...] = a_ref[...] + b_ref[...]

    return pl.pallas_call(body, name=name, grid=(T // tr,), in_specs=[_row(tr, C)] * 2, out_specs=_row(tr, C),
                          out_shape=SDS((T, C), F32), compiler_params=_cp("parallel"))(a, b)


def _tri(n, upper):
    r = lax.broadcasted_iota(jnp.int32, (n, n), 0)
    c = lax.broadcasted_iota(jnp.int32, (n, n), 1)
    return ((c >= r) if upper else (c <= r)).astype(F32)


def fox_gate_fwd(fl, bf, *, fl_col, name):
    T = fl.shape[0]
    tb = _pick(T, 256)

    def body(fl_ref, bf_ref, f_ref, c_ref):
        @pl.when(pl.program_id(0) == 0)
        def _():
            c_ref[...] = jnp.zeros_like(c_ref)

        lf = jax.nn.log_sigmoid(fl_ref[...] + bf_ref[...])
        f = jnp.dot(_tri(tb, False), lf, precision=lax.Precision.HIGHEST, preferred_element_type=F32) + c_ref[0:1, :]
        f_ref[...] = f * LOG2E
        c_ref[0:1, :] = f[tb - 1:tb, :]

    fl_spec = pl.BlockSpec((tb, LANES), lambda i: (i, fl_col))
    return pl.pallas_call(body, name=name, grid=(T // tb,), in_specs=[fl_spec, _full((1, LANES))],
                          out_specs=_row(tb, LANES), out_shape=SDS((T, LANES), F32),
                          scratch_shapes=[pltpu.VMEM((SUBLANES, LANES), F32)], compiler_params=_cp("arbitrary"))(fl, bf)


def fox_gate_bwd(fl, bf, df_keys, df_queries, *, fl_col, name):
    T = fl.shape[0]
    tb = _pick(T, 256)
    nt = T // tb

    def body(fl_ref, bf_ref, dfk_ref, dfq_ref, dfl_ref, dbf_ref, c_ref):
        @pl.when(pl.program_id(0) == 0)
        def _():
            c_ref[...] = jnp.zeros_like(c_ref)
            dbf_ref[...] = jnp.zeros_like(dbf_ref)

        dlf = jnp.dot(_tri(tb, True), dfk_ref[...] + dfq_ref[...], precision=lax.Precision.HIGHEST,
                      preferred_element_type=F32) + c_ref[0:1, :]
        c_ref[0:1, :] = dlf[0:1, :]
        dfl = dlf * jax.nn.sigmoid(-(fl_ref[...] + bf_ref[...]))
        dfl_ref[...] = dfl
        dbf_ref[...] += jnp.sum(dfl, axis=0, keepdims=True)

    rev = pl.BlockSpec((tb, LANES), lambda i: (nt - 1 - i, 0))
    fl_rev = pl.BlockSpec((tb, LANES), lambda i: (nt - 1 - i, fl_col))
    return pl.pallas_call(body, name=name, grid=(nt,), in_specs=[fl_rev, _full((1, LANES)), rev, rev],
                          out_specs=[rev, _full((1, LANES))], out_shape=[SDS((T, LANES), F32), SDS((1, LANES), F32)],
                          scratch_shapes=[pltpu.VMEM((SUBLANES, LANES), F32)],
                          compiler_params=_cp("arbitrary"))(fl, bf, df_keys, df_queries)


FOX_BLOCK = 512
FOX_PAIRS = FOX_HEADS // 2
_NT = (((1,), (1,)), ((), ()))


LOG2E = 1.4426950408889634
FOX_FWD_UNROLL = 4
FOX_BWD_UNROLL = 2


def _fox_block(T):
    return _pick(T, FOX_BLOCK)


def _own_lanes(lane, hh):
    return (lane < FOX_HEAD_DIM) if hh == 0 else (lane >= FOX_HEAD_DIM)


def _grouped_steps(step, lo, n, unroll, init):
    def trip(t, c):
        for u in range(unroll):
            c = step(lo + t * unroll + u, c)
        return c

    carry = lax.fori_loop(0, n // unroll, trip, init)
    for u in range(unroll - 1):
        carry = lax.cond(n % unroll > u, lambda c: step(lo + (n // unroll) * unroll + u, c), lambda c: c, carry)
    return carry


Q_TILE0, K_TILE0, V_TILE0, O_TILE0 = 4, 8, 12, 4
FL_TILE = 16
POOL_COL = 2


def fox_fwd(z, f_col, f_row, ycat, hosted, *, name):
    T = z.shape[0]
    blk = _fox_block(T)
    nb = T // blk
    scale = FOX_HEAD_DIM ** -0.5

    def body(q_ref, k_ref, v_ref, fc_ref, fr_ref, prev_ref, o_ref, l_ref):
        i = pl.program_id(1)
        row = lax.broadcasted_iota(jnp.int32, (blk, blk), 0)
        col = lax.broadcasted_iota(jnp.int32, (blk, blk), 1)
        lane = lax.broadcasted_iota(jnp.int32, (blk, LANES), 1)
        qt = q_ref[...] * (scale * LOG2E)
        outs = []
        for hh in range(2):
            qh = jnp.where(_own_lanes(lane, hh), qt, 0.0).astype(MXU_DTYPE)
            fi = fc_ref[0, :, hh:hh + 1]

            def step(j, carry, masked=False):
                m, l, acc = carry
                r0 = pl.multiple_of(j * blk, blk)
                kj = k_ref[pl.ds(r0, blk), :].astype(MXU_DTYPE)
                vj = v_ref[pl.ds(r0, blk), :].astype(MXU_DTYPE)
                s = lax.dot_general(qh, kj, _NT, preferred_element_type=F32) + (fi - fr_ref[0, j, hh:hh + 1, :])
                if masked:
                    s = jnp.where(col <= row, s, -jnp.inf)
                m_new = jnp.maximum(m, jnp.max(s, axis=-1, keepdims=True))
                p = jnp.exp2(s - m_new)
                alpha = jnp.exp2(m - m_new)
                l = alpha * l + jnp.sum(p, axis=-1, keepdims=True)
                acc = alpha * acc + jnp.dot(p.astype(MXU_DTYPE), vj, preferred_element_type=F32)
                return m_new, l, acc

            init = (jnp.full((blk, 1), -jnp.inf, F32), jnp.zeros((blk, 1), F32), jnp.zeros((blk, LANES), F32))
            m, l, acc = step(i, _grouped_steps(step, 0, i, FOX_FWD_UNROLL, init), True)
            outs.append(acc / l)
            l_ref[0, :, hh:hh + 1] = m + jnp.log2(l)
        o_ref[...] = jnp.where(_own_lanes(lane, 0), outs[0], outs[1])

    qspec = pl.BlockSpec((blk, LANES), lambda h, i: (i, Q_TILE0 + h))
    kspec = pl.BlockSpec((T, LANES), lambda h, i: (0, K_TILE0 + h))
    vspec = pl.BlockSpec((T, LANES), lambda h, i: (0, V_TILE0 + h))
    ospec = pl.BlockSpec((blk, LANES), lambda h, i: (i, O_TILE0 + h))
    cspec = pl.BlockSpec((1, blk, 2), lambda h, i: (h, i, 0))
    rspec = pl.BlockSpec((1, nb, 2, blk), lambda h, i: (h, 0, 0, 0))
    return call_hosting(body, hosted, name=name, grid=(FOX_PAIRS, nb),
                        in_specs=[qspec, kspec, vspec, cspec, rspec, ANY], out_specs=[ospec, cspec],
                        out_shape=[SDS(ycat.shape, F32), SDS((FOX_PAIRS, T, 2), F32)],
                        inputs=[z, z, z, f_col, f_row, ycat], aliases={5: 0})


def fox_dd(ycat, dycat, *, name):
    T = ycat.shape[0]
    blk = _fox_block(T)

    def body(o_ref, do_ref, dd_ref):
        lane = lax.broadcasted_iota(jnp.int32, (blk, LANES), 1)
        prod = do_ref[...] * o_ref[...]
        for hh in range(2):
            dd_ref[0, :, hh:hh + 1] = jnp.sum(jnp.where(_own_lanes(lane, hh), prod, 0.0), axis=-1, keepdims=True)

    ospec = pl.BlockSpec((blk, LANES), lambda h, i: (i, O_TILE0 + h))
    return pl.pallas_call(body, name=name, grid=(FOX_PAIRS, T // blk), in_specs=[ospec, ospec],
                          out_specs=pl.BlockSpec((1, blk, 2), lambda h, i: (h, i, 0)),
                          out_shape=SDS((FOX_PAIRS, T, 2), F32), compiler_params=_cp("parallel", "parallel"))(ycat, dycat)


def fox_bwd(z, dycat, f_col, f_row, lse_row, dd_row, hosted, *, name):
    T = z.shape[0]
    blk = _fox_block(T)
    nb = T // blk
    scale = FOX_HEAD_DIM ** -0.5

    def body(q_ref, k_ref, v_ref, do_ref, fc_ref, fr_ref, lr_ref, dr_ref, dk_ref, dv_ref, df_ref, dqt_ref, dfq_ref):
        j = pl.program_id(1)

        @pl.when(j == 0)
        def _():
            dqt_ref[...] = jnp.zeros_like(dqt_ref)
            dfq_ref[...] = jnp.zeros_like(dfq_ref)

        row = lax.broadcasted_iota(jnp.int32, (blk, blk), 0)
        col = lax.broadcasted_iota(jnp.int32, (blk, blk), 1)
        lane = lax.broadcasted_iota(jnp.int32, (blk, LANES), 1)
        kt = k_ref[...]
        vt = v_ref[...]
        dks, dvs = [], []
        for hh in range(2):
            own = _own_lanes(lane, hh)
            kh = jnp.where(own, kt, 0.0).astype(MXU_DTYPE)
            vh = jnp.where(own, vt, 0.0).astype(MXU_DTYPE)
            kht = kh.T
            fj = fc_ref[0, :, hh:hh + 1]

            def step(i, carry, masked=False):
                dk, dv, df = carry
                r0 = pl.multiple_of(i * blk, blk)
                qi = (q_ref[pl.ds(r0, blk), :] * (scale * LOG2E)).astype(MXU_DTYPE)
                doi = do_ref[pl.ds(r0, blk), :].astype(MXU_DTYPE)
                st = lax.dot_general(kh, qi, _NT, preferred_element_type=F32) + (fr_ref[0, i, hh:hh + 1, :] - fj)
                pt = jnp.exp2(st - lr_ref[0, i, hh:hh + 1, :])
                if masked:
                    pt = jnp.where(col >= row, pt, 0.0)
                dv = dv + jnp.dot(pt.astype(MXU_DTYPE), doi, preferred_element_type=F32)
                dpt = lax.dot_general(vh, doi, _NT, preferred_element_type=F32)
                dst = pt * (dpt - dr_ref[0, i, hh:hh + 1, :])
                dsb = dst.astype(MXU_DTYPE)
                dk = dk + jnp.dot(dsb, qi, preferred_element_type=F32)
                df = df - jnp.sum(dst, axis=-1, keepdims=True)
                dqt_ref[0, i] += jnp.dot(kht, dsb, preferred_element_type=F32)
                dfq_ref[0, i, hh:hh + 1, :] += jnp.sum(dst, axis=0, keepdims=True)
                return dk, dv, df

            init = (jnp.zeros((blk, LANES), F32), jnp.zeros((blk, LANES), F32), jnp.zeros((blk, 1), F32))
            dk, dv, df = _grouped_steps(step, j + 1, nb - 1 - j, FOX_BWD_UNROLL, step(j, init, True))
            dks.append(dk * (1.0 / LOG2E))
            dvs.append(dv)
            df_ref[0, :, hh:hh + 1] = df
        dk_ref[...] = jnp.where(_own_lanes(lane, 0), dks[0], dks[1])
        dv_ref[...] = jnp.where(_own_lanes(lane, 0), dvs[0], dvs[1])

    bspec = pl.BlockSpec((blk, LANES), lambda h, j: (j, h))
    qspec = pl.BlockSpec((T, LANES), lambda h, j: (0, Q_TILE0 + h))
    kspec = pl.BlockSpec((blk, LANES), lambda h, j: (j, K_TILE0 + h))
    vspec = pl.BlockSpec((blk, LANES), lambda h, j: (j, V_TILE0 + h))
    dospec = pl.BlockSpec((T, LANES), lambda h, j: (0, O_TILE0 + h))
    cspec = pl.BlockSpec((1, blk, 2), lambda h, j: (h, j, 0))
    rspec = pl.BlockSpec((1, nb, 2, blk), lambda h, j: (h, 0, 0, 0))
    dqspec = pl.BlockSpec((1, nb, LANES, blk), lambda h, j: (h, 0, 0, 0))
    return call_hosting(body, hosted, name=name, grid=(FOX_PAIRS, nb),
                        in_specs=[qspec, kspec, vspec, dospec, cspec, rspec, rspec, rspec],
                        out_specs=[bspec, bspec, cspec, dqspec, rspec],
                        out_shape=[SDS((T, FOX_WIDTH), F32), SDS((T, FOX_WIDTH), F32), SDS((FOX_PAIRS, T, 2), F32),
                                   SDS((FOX_PAIRS, nb, LANES, blk), F32), SDS((FOX_PAIRS, nb, 2, blk), F32)],
                        inputs=[z, z, z, dycat, f_col, f_row, lse_row, dd_row], aliases={})


def _pairs_col(a, T):
    return jnp.transpose(a[:, :FOX_HEADS].reshape(T, FOX_PAIRS, 2), (1, 0, 2))


def _col_to_row(a, T):
    blk = _fox_block(T)
    return jnp.transpose(a.reshape(FOX_PAIRS, T // blk, blk, 2), (0, 1, 3, 2))


def _row_to_col(a, T):
    return jnp.transpose(a, (0, 1, 3, 2)).reshape(FOX_PAIRS, T, 2)


def _pairs_to_lanes(a, T):
    flat = jnp.transpose(a, (1, 0, 2)).reshape(T, FOX_HEADS)
    return jnp.pad(flat, ((0, 0), (0, LANES - FOX_HEADS)))


def _pool_counts(t0, n, w):
    t = (t0 + lax.broadcasted_iota(jnp.int32, (n, 1), 0)).astype(F32)
    return jnp.minimum(t + 1.0, float(w))


def pool_window(x, *, adjoint, name, in_col=0, into=None, out_col=0):
    T, C = x.shape[0], len(POOL_WINDOWS) * POOL_GROUP_DIM
    tr = _pick(T, 512)
    nt = T // tr
    hb = tr // POOL_HALO
    n = tr + POOL_HALO

    def body(x_ref, h_ref, *rest):
        o_ref = rest[-1]
        i = pl.program_id(0)
        cur = x_ref[...]
        if adjoint:
            halo = h_ref[...] * jnp.where(i < nt - 1, 1.0, 0.0)
            ext = jnp.concatenate([cur, halo], axis=0)
            t0 = i * tr
        else:
            halo = h_ref[...] * jnp.where(i > 0, 1.0, 0.0)
            ext = jnp.concatenate([halo, cur], axis=0)
            t0 = i * tr - POOL_HALO
        sums = {}
        for g, w in enumerate(POOL_WINDOWS):
            ls = slice(g * POOL_GROUP_DIM, (g + 1) * POOL_GROUP_DIM)
            s = ext[:, ls]
            if adjoint:
                s = s / _pool_counts(t0, n, w)
            d = 1
            while d < w:
                s = s + pltpu.roll(s, (n - d) if adjoint else d, 0)
                d *= 2
            if adjoint:
                o_ref[:, ls] = (s[0:tr, :] - cur[:, ls]).astype(o_ref.dtype)
            else:
                o_ref[:, ls] = (s[POOL_HALO:n, :] / _pool_counts(i * tr, tr, w) - cur[:, ls]).astype(o_ref.dtype)

    if adjoint:
        halo_spec = pl.BlockSpec((POOL_HALO, C), lambda i: (jnp.minimum((i + 1) * hb, T // POOL_HALO - 1), in_col))
    else:
        halo_spec = pl.BlockSpec((POOL_HALO, C), lambda i: (jnp.maximum(i * hb - 1, 0), in_col))
    x_spec = pl.BlockSpec((tr, C), lambda i: (i, in_col))
    if into is None:
        return pl.pallas_call(body, name=name, grid=(nt,), in_specs=[x_spec, halo_spec], out_specs=_row(tr, C),
                              out_shape=SDS((T, C), F32), compiler_params=_cp("parallel"))(x, x)
    return pl.pallas_call(body, name=name, grid=(nt,), in_specs=[x_spec, halo_spec, ANY],
                          out_specs=pl.BlockSpec((tr, C), lambda i: (i, out_col)), out_shape=SDS(into.shape, into.dtype),
                          input_output_aliases={2: 0}, compiler_params=_cp("parallel"))(x, x, into)


def colscale_fwd(a, s, *, out_cols, name):
    T, C = a.shape
    tr = _pick(T, 512)

    def body(a_ref, s_ref, o_ref):
        o_ref[...] = (a_ref[...] * s_ref[...]).astype(o_ref.dtype)

    return pl.pallas_call(body, name=name, grid=(T // tr,), in_specs=[_row(tr, C), _full((1, C))], out_specs=_row(tr, C),
                          out_shape=SDS((T, out_cols), MXU_DTYPE), compiler_params=_cp("parallel"))(a, s)


def colscale_bwd(a, s, dy, *, name):
    T, C = a.shape
    tr = _pick(T, 512)

    def body(a_ref, s_ref, dy_ref, da_ref, ds_ref):
        dyv = dy_ref[...]
        da_ref[...] = dyv * s_ref[...]

        @pl.when(pl.program_id(0) == 0)
        def _():
            ds_ref[...] = jnp.zeros_like(ds_ref)

        ds_ref[...] += jnp.sum(dyv * a_ref[...], axis=0, keepdims=True)

    return pl.pallas_call(body, name=name, grid=(T // tr,), in_specs=[_row(tr, C), _full((1, C)), _row(tr, C)],
                          out_specs=[_row(tr, C), _full((1, C))], out_shape=[SDS((T, C), F32), SDS((1, C), F32)],
                          compiler_params=_cp("arbitrary"))(a, s, dy)


SGU_ROWS = 512


def _sgu_norm(v, ln_g, ln_b):
    vg = jax.nn.gelu(v)
    xc = vg - jnp.mean(vg, axis=-1, keepdims=True)
    r = lax.rsqrt(jnp.mean(xc * xc, axis=-1, keepdims=True) + EPS)
    xh = xc * r
    return xh * ln_g + ln_b, xh, r


def _rowc(tr, c, cb):
    return pl.BlockSpec((tr, c), lambda i: (i, cb))


def sgu_fwd(z, ln_g, ln_b, ws, bst, ycat, *, name):
    T, C = z.shape[0], SGU_GROUPS * SGU_GROUP_DIM
    tr = _pick(T, SGU_ROWS)

    def body(u_ref, v_ref, g_ref, b_ref, ws_ref, bst_ref, prev_ref, o_ref):
        vn, _, _ = _sgu_norm(v_ref[...], g_ref[...], b_ref[...])
        vn = vn.astype(MXU_DTYPE)
        ug = jax.nn.gelu(u_ref[...])
        for g in range(SGU_GROUPS):
            w = ws_ref[g].astype(MXU_DTYPE)
            bias = bst_ref[:, g:g + 1]
            for c in range(tr // CHUNK):
                rs = slice(c * CHUNK, (c + 1) * CHUNK)
                ls = slice(g * SGU_GROUP_DIM, (g + 1) * SGU_GROUP_DIM)
                mixed = jnp.dot(w, vn[rs, ls], preferred_element_type=F32) + bias
                o_ref[rs, ls] = (ug[rs, ls] * mixed).astype(o_ref.dtype)

    return pl.pallas_call(body, name=name, grid=(T // tr,),
                          in_specs=[_rowc(tr, C, 0), _rowc(tr, C, 1), _full((1, C)), _full((1, C)),
                                    _full((SGU_GROUPS, CHUNK, CHUNK)), _full((CHUNK, SGU_GROUPS)), ANY],
                          out_specs=_rowc(tr, C, 1), out_shape=SDS(ycat.shape, ycat.dtype), input_output_aliases={6: 0},
                          compiler_params=_cp("parallel"))(z, z, ln_g, ln_b, ws, bst, ycat)


def sgu_bwd(z, ln_g, ln_b, ws, wst, bst, dycat, *, out_cols, name):
    T, C = z.shape[0], SGU_GROUPS * SGU_GROUP_DIM
    tr = _pick(T, SGU_ROWS)

    def body(u_ref, v_ref, g_ref, b_ref, ws_ref, wst_ref, bst_ref, dy_ref,
             duv_ref, dg_ref, db_ref, dws_ref, dbst_ref, dvn_ref):
        du_ref = duv_ref.at[:, 0:C]
        dv_ref = duv_ref.at[:, C:2 * C]
        @pl.when(pl.program_id(0) == 0)
        def _():
            dg_ref[...] = jnp.zeros_like(dg_ref)
            db_ref[...] = jnp.zeros_like(db_ref)
            dws_ref[...] = jnp.zeros_like(dws_ref)
            dbst_ref[...] = jnp.zeros_like(dbst_ref)

        uv = u_ref[...]
        vv = v_ref[...]
        vn, xh, r = _sgu_norm(vv, g_ref[...], b_ref[...])
        vn = vn.astype(MXU_DTYPE)
        ug = jax.nn.gelu(uv)
        dyv = dy_ref[...]
        for g in range(SGU_GROUPS):
            w = ws_ref[g].astype(MXU_DTYPE)
            wt = wst_ref[g].astype(MXU_DTYPE)
            bias = bst_ref[:, g:g + 1]
            dw = jnp.zeros((CHUNK, CHUNK), F32)
            dbias = jnp.zeros((CHUNK, 1), F32)
            for c in range(tr // CHUNK):
                rs = slice(c * CHUNK, (c + 1) * CHUNK)
                ls = slice(g * SGU_GROUP_DIM, (g + 1) * SGU_GROUP_DIM)
                vblk = vn[rs, ls]
                mixed = jnp.dot(w, vblk, preferred_element_type=F32) + bias
                dyb = dyv[rs, ls]
                du_ref[rs, ls] = (dyb * mixed * _gelu_grad(uv[rs, ls])).astype(du_ref.dtype)
                dmixed = dyb * ug[rs, ls]
                dbias = dbias + jnp.sum(dmixed, axis=-1, keepdims=True)
                dmb = dmixed.astype(MXU_DTYPE)
                dw = dw + lax.dot_general(dmb, vblk, _NT, preferred_element_type=F32)
                dvn_ref[rs, ls] = jnp.dot(wt, dmb, preferred_element_type=F32)
            dws_ref[g] += dw
            dbst_ref[:, g:g + 1] += dbias
        dvn = dvn_ref[...]
        dg_ref[...] += jnp.sum(dvn * xh, axis=0, keepdims=True)
        db_ref[...] += jnp.sum(dvn, axis=0, keepdims=True)
        dxh = dvn * g_ref[...]
        dvg = r * (dxh - jnp.mean(dxh, axis=-1, keepdims=True) - xh * jnp.mean(dxh * xh, axis=-1, keepdims=True))
        dv_ref[...] = (dvg * _gelu_grad(vv)).astype(dv_ref.dtype)

    wspec = _full((SGU_GROUPS, CHUNK, CHUNK))
    return pl.pallas_call(body, name=name, grid=(T // tr,),
                          in_specs=[_rowc(tr, C, 0), _rowc(tr, C, 1), _full((1, C)), _full((1, C)), wspec, wspec,
                                    _full((CHUNK, SGU_GROUPS)), _rowc(tr, C, 1)],
                          out_specs=[_rowc(tr, 2 * C, 0), _full((1, C)), _full((1, C)), wspec,
                                     _full((CHUNK, SGU_GROUPS))],
                          out_shape=[SDS((T, out_cols), MXU_DTYPE), SDS((1, C), F32), SDS((1, C), F32),
                                     SDS((SGU_GROUPS, CHUNK, CHUNK), F32), SDS((CHUNK, SGU_GROUPS), F32)],
                          scratch_shapes=[pltpu.VMEM((tr, C), F32)],
                          compiler_params=_cp("arbitrary"))(z, z, ln_g, ln_b, ws, wst, bst, dycat)


def adamw(w, g, m, v, *, name):
    R, C = w.shape
    tr = _pick(R, 512)
    c1 = 1.0 - ADAM_B1 ** ADAM_STEP
    c2 = 1.0 - ADAM_B2 ** ADAM_STEP

    def body(w_ref, g_ref, m_ref, v_ref, d_ref, nm_ref, nv_ref):
        gv = g_ref[...]
        nm = ADAM_B1 * m_ref[...] + (1.0 - ADAM_B1) * gv
        nv = ADAM_B2 * v_ref[...] + (1.0 - ADAM_B2) * (gv * gv)
        nm_ref[...] = nm
        nv_ref[...] = nv
        d_ref[...] = -ADAM_LR * ((nm / c1) / (jnp.sqrt(nv / c2) + ADAM_EPS) + ADAM_WD * w_ref[...])

    spec = _row(tr, C)
    return pl.pallas_call(body, name=name, grid=(R // tr,), in_specs=[spec] * 4, out_specs=[spec] * 3,
                          out_shape=[SDS((R, C), F32)] * 3, compiler_params=_cp("parallel"))(w, g, m, v)


ANY = pl.BlockSpec(memory_space=pl.ANY)


def _coords():
    return lax.axis_index("x"), lax.axis_index("y"), lax.axis_index("c")


def _other_chips(x, y):
    return [(1 - x, y), (x, 1 - y), (1 - x, 1 - y)]


def _remote(src, dst, send_sems, recv_sems, k, dev):
    return pltpu.make_async_remote_copy(src_ref=src, dst_ref=dst, send_sem=send_sems.at[k], recv_sem=recv_sems.at[k],
                                        device_id=dev, device_id_type=MESH)


LOCAL_CHUNKS = 8


def allgather_chip_shards(shards, small, *, name):
    na = len(shards)

    def body(*refs):
        s_refs, sm_ref = refs[:na], refs[na]
        o_refs, smo_ref = refs[na + 1:2 * na + 1], refs[2 * na + 1]
        send_sems, recv_sems, local_sems = refs[2 * na + 2:]
        x, y, c = _coords()
        j = 2 * x + y
        sibling = (x, y, 1 - c)
        chips = _other_chips(x, y)
        for a in range(na):
            chunk = shards[a].shape[0] // LOCAL_CHUNKS
            for q in range(LOCAL_CHUNKS):
                rows = pl.ds(q * chunk, chunk)
                pltpu.make_async_copy(s_refs[a].at[rows], o_refs[a].at[j, rows], local_sems.at[a]).start()
        pltpu.make_async_copy(sm_ref, smo_ref.at[j], local_sems.at[na]).start()
        sends = []
        for a in range(na):
            half = shards[a].shape[0] // 2
            mine = pl.ds(c * half, half)
            for k, (px, py) in enumerate(chips):
                sends.append(_remote(s_refs[a].at[mine], o_refs[a].at[j, mine], send_sems, recv_sems, 6 * a + k, (px, py, c)))
        for k, (px, py) in enumerate(chips):
            sends.append(_remote(sm_ref, smo_ref.at[j], send_sems, recv_sems, 6 * na + k, (px, py, c)))
        for cp in sends:
            cp.start()
        for a in range(na):
            half = shards[a].shape[0] // 2
            mine = pl.ds(c * half, half)
            for k, (px, py) in enumerate(chips):
                rows = o_refs[a].at[2 * px + py, mine]
                _remote(rows, rows, send_sems, recv_sems, 6 * a + k, (px, py, c)).wait_recv()
                fw = _remote(rows, rows, send_sems, recv_sems, 6 * a + 3 + k, sibling)
                fw.start()
                sends.append(fw)
        for a in range(na):
            half = shards[a].shape[0] // 2
            theirs = pl.ds((1 - c) * half, half)
            for k, (px, py) in enumerate(chips):
                rows = o_refs[a].at[2 * px + py, theirs]
                _remote(rows, rows, send_sems, recv_sems, 6 * a + 3 + k, sibling).wait_recv()
        for k, (px, py) in enumerate(chips):
            slot = smo_ref.at[2 * px + py]
            _remote(slot, slot, send_sems, recv_sems, 6 * na + k, (px, py, c)).wait_recv()
        for cp in sends:
            cp.wait_send()
        for a in range(na):
            pltpu.make_async_copy(s_refs[a], o_refs[a].at[j], local_sems.at[a]).wait()
        pltpu.make_async_copy(sm_ref, smo_ref.at[j], local_sems.at[na]).wait()

    nsem = 6 * na + 3
    outs = pl.pallas_call(
        body, name=name, in_specs=[ANY] * (na + 1), out_specs=[ANY] * (na + 1),
        out_shape=[SDS((N_CHIPS,) + s.shape, s.dtype) for s in shards] + [SDS((N_CHIPS,) + small.shape, small.dtype)],
        scratch_shapes=[pltpu.SemaphoreType.DMA((nsem,)), pltpu.SemaphoreType.DMA((nsem,)),
                        pltpu.SemaphoreType.DMA((na + 1,))])(*shards, small)
    return outs[:na], outs[na]


class Exchange:
    def __init__(self, ins, out_shapes, scratch, start, wait):
        self.ins, self.out_shapes, self.scratch, self.start, self.wait = list(ins), list(out_shapes), list(scratch), start, wait


def run_exchange(ex, *, name):
    ni, no = len(ex.ins), len(ex.out_shapes)

    def body(*refs):
        parts = refs[:ni], refs[ni:ni + no], refs[ni + no:]
        ex.start(*parts)
        ex.wait(*parts)

    return pl.pallas_call(body, name=name, in_specs=[ANY] * ni, out_specs=[ANY] * no, out_shape=ex.out_shapes,
                          scratch_shapes=ex.scratch)(*ex.ins)


def call_hosting(body, ex, *, name, grid, in_specs, out_specs, out_shape, inputs, aliases, scratch=()):
    n_in, n_out, ni, no, ns = len(inputs), len(out_shape), len(ex.ins), len(ex.out_shapes), len(scratch)
    outs_at = n_in + ni
    scr_at = outs_at + n_out + no

    def wrapped(*refs):
        own = refs[:n_in] + refs[outs_at:outs_at + n_out] + refs[scr_at:scr_at + ns]
        parts = refs[n_in:outs_at], refs[outs_at + n_out:scr_at], refs[scr_at + ns:]
        ids = [pl.program_id(d) for d in range(len(grid))]
        first = functools.reduce(jnp.logical_and, [i == 0 for i in ids])
        last = functools.reduce(jnp.logical_and, [i == g - 1 for i, g in zip(ids, grid)])

        @pl.when(first)
        def _():
            ex.start(*parts)

        body(*own)

        @pl.when(last)
        def _():
            ex.wait(*parts)

    outs = pl.pallas_call(
        wrapped, name=name, grid=grid, in_specs=list(in_specs) + [ANY] * ni, out_specs=list(out_specs) + [ANY] * no,
        out_shape=list(out_shape) + ex.out_shapes, input_output_aliases=aliases,
        scratch_shapes=list(scratch) + ex.scratch,
        compiler_params=_cp(*["arbitrary"] * len(grid)))(*inputs, *ex.ins)
    return outs[:n_out], outs[n_out:]


def allgather_ici_exchange(shards):
    na = len(shards)

    def copies(s_refs, o_refs, sems):
        send_sems, recv_sems, _ = sems
        x, y, c = _coords()
        j = 2 * x + y
        out = []
        for a in range(na):
            half = shards[a].shape[0] // 2
            mine = pl.ds(c * half, half)
            for k, (px, py) in enumerate(_other_chips(x, y)):
                send = _remote(s_refs[a].at[mine], o_refs[a].at[j, mine], send_sems, recv_sems, 3 * a + k, (px, py, c))
                rows = o_refs[a].at[2 * px + py, mine]
                out.append((send, _remote(rows, rows, send_sems, recv_sems, 3 * a + k, (px, py, c))))
        return out

    def start(s_refs, o_refs, sems):
        x, y, c = _coords()
        j = 2 * x + y
        for a in range(na):
            chunk = shards[a].shape[0] // LOCAL_CHUNKS
            for q in range(LOCAL_CHUNKS):
                rows = pl.ds(q * chunk, chunk)
                pltpu.make_async_copy(s_refs[a].at[rows], o_refs[a].at[j, rows], sems[2].at[a]).start()
        for send, _ in copies(s_refs, o_refs, sems):
            send.start()

    def wait(s_refs, o_refs, sems):
        x, y, c = _coords()
        j = 2 * x + y
        for send, arrival in copies(s_refs, o_refs, sems):
            arrival.wait_recv()
            send.wait_send()
        for a in range(na):
            pltpu.make_async_copy(s_refs[a], o_refs[a].at[j], sems[2].at[a]).wait()

    return Exchange(shards, [SDS((N_CHIPS,) + s.shape, s.dtype) for s in shards],
                    [pltpu.SemaphoreType.DMA((3 * na,)), pltpu.SemaphoreType.DMA((3 * na,)), pltpu.SemaphoreType.DMA((na,))],
                    start, wait)


def allgather_forward(gathered, *, name):
    na = len(gathered)

    def body(*refs):
        o_refs = refs[na:2 * na]
        send_sems, recv_sems = refs[2 * na:]
        x, y, c = _coords()
        sibling = (x, y, 1 - c)
        cps = []
        for a in range(na):
            half = gathered[a].shape[1] // 2
            for k, (px, py) in enumerate(_other_chips(x, y)):
                mine = o_refs[a].at[2 * px + py, pl.ds(c * half, half)]
                theirs = o_refs[a].at[2 * px + py, pl.ds((1 - c) * half, half)]
                cps.append((_remote(mine, mine, send_sems, recv_sems, 3 * a + k, sibling),
                            _remote(theirs, theirs, send_sems, recv_sems, 3 * a + k, sibling)))
        for send, _ in cps:
            send.start()
        for send, arrival in cps:
            send.wait_send()
            arrival.wait_recv()

    return pl.pallas_call(body, name=name, in_specs=[ANY] * na, out_specs=[ANY] * na,
                          out_shape=[SDS(g.shape, g.dtype) for g in gathered],
                          input_output_aliases={a: a for a in range(na)},
                          scratch_shapes=[pltpu.SemaphoreType.DMA((3 * na,)), pltpu.SemaphoreType.DMA((3 * na,))])(*gathered)


def swap_halves_exchange(gs):
    na = len(gs)

    def copies(g_refs, o_refs, sems):
        x, y, c = _coords()
        out = []
        for a in range(na):
            half = gs[a].shape[1] // 2
            out.append(_remote(g_refs[a].at[:, pl.ds((1 - c) * half, half), :], o_refs[a], sems[0], sems[1], a,
                               (x, y, 1 - c)))
        return out

    def start(g_refs, o_refs, sems):
        for cp in copies(g_refs, o_refs, sems):
            cp.start()

    def wait(g_refs, o_refs, sems):
        for cp in copies(g_refs, o_refs, sems):
            cp.wait()

    return Exchange(gs, [SDS((g.shape[0], g.shape[1] // 2, g.shape[2]), g.dtype) for g in gs],
                    [pltpu.SemaphoreType.DMA((na,)), pltpu.SemaphoreType.DMA((na,))], start, wait)


def chip_partials_exchange(pbs):
    na = len(pbs)

    def copies(p_refs, o_refs, sems):
        x, y, c = _coords()
        out = []
        for a in range(na):
            for k, (px, py) in enumerate(_other_chips(x, y)):
                out.append(_remote(p_refs[a].at[2 * px + py], o_refs[a].at[k], sems[0], sems[1], 3 * a + k, (px, py, c)))
        return out

    def start(p_refs, o_refs, sems):
        for cp in copies(p_refs, o_refs, sems):
            cp.start()

    def wait(p_refs, o_refs, sems):
        for cp in copies(p_refs, o_refs, sems):
            cp.wait()

    return Exchange(pbs, [SDS((3,) + p.shape[1:], p.dtype) for p in pbs],
                    [pltpu.SemaphoreType.DMA((3 * na,)), pltpu.SemaphoreType.DMA((3 * na,))], start, wait)


def add_sibling_half(g, land, c_idx, *, name):
    n, R, C = g.shape
    half = R // 2
    tr = _pick(half, 256)
    nt = half // tr

    def body(c_ref, g_ref, l_ref, of_ref, ob_ref):
        s = g_ref[...] + l_ref[...].astype(F32)
        of_ref[...] = s
        ob_ref[...] = s.astype(ob_ref.dtype)

    blk = pl.BlockSpec((1, tr, C), lambda s, i, c_ref: (s, i, 0))
    gblk = pl.BlockSpec((1, tr, C), lambda s, i, c_ref: (s, c_ref[0] * nt + i, 0))
    return pl.pallas_call(
        body, name=name,
        grid_spec=pltpu.PrefetchScalarGridSpec(num_scalar_prefetch=1, grid=(n, nt), in_specs=[gblk, blk],
                                               out_specs=[blk, blk]),
        out_shape=[SDS((n, half, C), F32), SDS((n, half, C), WIRE_DTYPE)],
        compiler_params=_cp("parallel", "parallel"))(c_idx, g, land)


def add_chip_partials(pf, rb, jc_idx, *, name):
    n, H, C = pf.shape
    tr = _pick(H, 256)

    def body(jc_ref, p_ref, r_ref, o_ref):
        s = p_ref[0]
        for k in range(3):
            s = s + r_ref[k].astype(F32)
        o_ref[...] = s

    pblk = pl.BlockSpec((1, tr, C), lambda i, jc_ref: (jc_ref[0], i, 0))
    rblk = pl.BlockSpec((3, tr, C), lambda i, jc_ref: (0, i, 0))
    oblk = pl.BlockSpec((None, tr, C), lambda i, jc_ref: (jc_ref[1], i, 0))
    return pl.pallas_call(
        body, name=name,
        grid_spec=pltpu.PrefetchScalarGridSpec(num_scalar_prefetch=1, grid=(H // tr,), in_specs=[pblk, rblk],
                                               out_specs=oblk),
        out_shape=SDS((2, H, C), F32), compiler_params=_cp("parallel"))(jc_idx, pf, rb)


def join_sibling_halves(bufs, *, name):
    na = len(bufs)

    def body(*refs):
        o_refs = refs[na:2 * na]
        send_sems, recv_sems = refs[2 * na:]
        x, y, c = _coords()
        cps = [_remote(o_refs[a].at[c], o_refs[a].at[c], send_sems, recv_sems, a, (x, y, 1 - c)) for a in range(na)]
        for cp in cps:
            cp.start()
        for a in range(na):
            cps[a].wait_send()
            _remote(o_refs[a].at[1 - c], o_refs[a].at[1 - c], send_sems, recv_sems, a, (x, y, 1 - c)).wait_recv()

    return pl.pallas_call(body, name=name, in_specs=[ANY] * na, out_specs=[ANY] * na,
                          out_shape=[SDS(b.shape, b.dtype) for b in bufs],
                          input_output_aliases={a: a for a in range(na)},
                          scratch_shapes=[pltpu.SemaphoreType.DMA((na,)), pltpu.SemaphoreType.DMA((na,))])(*bufs)


def exchange_pieces(v, *, scatter, name):
    P, C = v.shape[-2:]

    def body(v_ref, o_ref, send_sems, recv_sems, local_sem):
        x, y, c = _coords()
        me = 4 * x + 2 * y + c
        local = pltpu.make_async_copy(v_ref.at[me] if scatter else v_ref, o_ref.at[me], local_sem)
        local.start()
        cps = []
        for m in range(1, N_DEV):
            px = (1 - x) if m & 4 else x
            py = (1 - y) if m & 2 else y
            pc = (1 - c) if m & 1 else c
            src = v_ref.at[4 * px + 2 * py + pc] if scatter else v_ref
            cps.append(_remote(src, o_ref.at[me], send_sems, recv_sems, m - 1, (px, py, pc)))
        for cp in cps:
            cp.start()
        for cp in cps:
            cp.wait_send()
        for m in range(1, N_DEV):
            px = (1 - x) if m & 4 else x
            py = (1 - y) if m & 2 else y
            pc = (1 - c) if m & 1 else c
            slot = o_ref.at[4 * px + 2 * py + pc]
            _remote(slot, slot, send_sems, recv_sems, m - 1, (px, py, pc)).wait_recv()
        local.wait()

    return pl.pallas_call(body, name=name, in_specs=[ANY], out_specs=ANY, out_shape=SDS((N_DEV, P, C), v.dtype),
                          scratch_shapes=[pltpu.SemaphoreType.DMA((N_DEV - 1,)), pltpu.SemaphoreType.DMA((N_DEV - 1,)),
                                          pltpu.SemaphoreType.DMA(())])(v)


def sum_pieces(land, *, name):
    n, P, C = land.shape

    def body(l_ref, o_ref):
        s = l_ref[0]
        for d in range(1, n):
            s = s + l_ref[d]
        o_ref[...] = s

    return pl.pallas_call(body, name=name, out_shape=SDS((P, C), F32))(land)


BIG_SEGS = (
    ("w_in_even", (1024, 514), 1),
    ("s5_w_glu", (128, 512), 0),
    ("w_out_even", (256, 1024), 0),
    ("w_in_odd", (1024, 384), 1),
    ("w_out_odd", (256, 1024), 0),
    ("mlp_w1", (2, 1024, 1024), 2),
    ("mlp_w2", (2, 1024, 1024), 1),
)
BIG_NAMES = tuple(n for n, _, _ in BIG_SEGS)
EARLY_NAMES = ("w_in_even", "s5_w_glu")
LATE_NAMES = ("w_out_even", "w_in_odd", "w_out_odd", "mlp_w1", "mlp_w2")
REDUCED_EARLY = ("s5_w_glu", "w_out_even", "w_in_odd", "w_out_odd", "mlp_w1", "mlp_w2")
SHARDED_SMALL = ("pool_scale", "sgu_ln_g", "sgu_ln_b")
SMALL_SEGS = (
    ("mix_pre_g", (2, 1024)), ("mix_post_g", (2, 1024)), ("mlp_pre_g", (2, 1024)), ("mlp_post_g", (2, 1024)),
    ("s5_lam_re", (1, 32, 64)), ("s5_lam_im", (1, 32, 64)), ("s5_log_dt", (1, 32)),
    ("s5_b_re", (1, 32, 64, 16)), ("s5_b_im", (1, 32, 64, 16)), ("s5_c_re", (1, 32, 16, 64)), ("s5_c_im", (1, 32, 16, 64)),
    ("s5_d", (1, 512)), ("fox_b_f", (1, 8)), ("pool_w", (1, 4, 128, 128)), ("sgu_w_s", (1, 4, 128, 128)),
    ("sgu_b_s", (1, 4, 128)),
)
REDUCED_SEGS = SMALL_SEGS + tuple((n, (1, 512)) for n in SHARDED_SMALL) + (("loss", (1, 1)),)


def _cols_from_chips(g):
    n, R, C = g.shape
    return jnp.transpose(g, (1, 0, 2)).reshape(R, n * C)


def _chips_from_cols(m):
    R, C4 = m.shape
    return jnp.transpose(m.reshape(R, N_CHIPS, C4 // N_CHIPS), (1, 0, 2))


MLP_SHARD = 1024


def _w1_cols(l):
    def spec(tm, tn, tk):
        per = MLP_SHARD // tn
        return pl.BlockSpec((None, tk, tn), lambda i, j, k: (j // per, l * (MLP_SHARD // tk) + k, j % per))
    return spec


def _w1_rows_t(l):
    def spec(tm, tn, tk):
        if tk == N_CHIPS * MLP_SHARD:
            return pl.BlockSpec((N_CHIPS, tn, MLP_SHARD), lambda i, j, k: (0, l * (MLP_SHARD // tn) + j, 0))
        per = MLP_SHARD // tk
        return pl.BlockSpec((None, tn, tk), lambda i, j, k: (k // per, l * (MLP_SHARD // tn) + j, k % per))
    return spec


def _w2_rows(l):
    def spec(tm, tn, tk):
        if tk == N_CHIPS * MLP_SHARD:
            return pl.BlockSpec((N_CHIPS, MLP_SHARD, tn), lambda i, j, k: (0, l, j))
        per = MLP_SHARD // tk
        return pl.BlockSpec((None, tk, tn), lambda i, j, k: (k // per, l * per + k % per, j))
    return spec


def _w2_rows_t(l):
    def spec(tm, tn, tk):
        per = MLP_SHARD // tn
        return pl.BlockSpec((None, tn, tk), lambda i, j, k: (j // per, l * per + j % per, k))
    return spec


def _dw1_out(l):
    def spec(tm, tn, tk):
        per = MLP_SHARD // tn
        return pl.BlockSpec((None, tm, tn), lambda i, j, k: (j // per, l * (MLP_SHARD // tm) + i, j % per))
    return spec


def _dw2_out(l):
    def spec(tm, tn, tk):
        per = MLP_SHARD // tm
        return pl.BlockSpec((None, tm, tn), lambda i, j, k: (i // per, l * per + i % per, j))
    return spec


def _pack_vec(d, segs, rows_multiple):
    flat = jnp.concatenate([d[n].reshape(-1) for n, _ in segs])
    rows = -(-flat.shape[0] // LANES)
    rows = -(-rows // rows_multiple) * rows_multiple
    return jnp.pad(flat, (0, rows * LANES - flat.shape[0])).reshape(rows, LANES)


def _unpack_vec(v, segs):
    flat, out, r = v.reshape(-1), {}, 0
    for n, shape in segs:
        k = math.prod(shape)
        out[n] = flat[r:r + k].reshape(shape)
        r += k
    return out


def _block_diag(blocks):
    G, a, b = blocks.shape
    eye = jnp.eye(G, dtype=blocks.dtype)
    return (eye[:, None, :, None] * blocks[:, :, None, :]).reshape(G * a, G * b)


def _diag_blocks(m, G):
    a, b = m.shape[0] // G, m.shape[1] // G
    return jnp.stack([m[g * a:(g + 1) * a, g * b:(g + 1) * b] for g in range(G)])


def _sqrelu_epi(acc):
    r = jnp.maximum(acc, 0.0)
    return acc, r * r


def _sqrelu_bwd_epi(acc, a):
    return (acc * (2.0 * jnp.maximum(a.astype(F32), 0.0)),)


def _mlp_fwd(h, g1, g2, l, tag):
    T, D = h.shape
    a, s = matmul(h, g1, name=f"{tag}_up", mnk=(T, D_FF, D), b_spec=_w1_cols(l), epi=_sqrelu_epi,
                  out_dtypes=(MXU_DTYPE, MXU_DTYPE))
    m = matmul(s, g2, name=f"{tag}_down", mnk=(T, D, D_FF), b_spec=_w2_rows(l))
    return m, (h, a, s)


def _mlp_bwd(saved, dm, g1, g2, l, dg1, dg2, tag):
    h, a, s = saved
    T, D = h.shape
    gshape = (N_CHIPS, 2 * MLP_SHARD, MLP_SHARD)
    da = matmul(dm, g2, tb=True, name=f"{tag}_down_dx", mnk=(T, D_FF, D), b_spec=_w2_rows_t(l),
                epi=_sqrelu_bwd_epi, epi_in=(a,), out_dtype=MXU_DTYPE)
    dg2 = matmul(s, dm, ta=True, name=f"{tag}_down_dw", tm=MLP_SHARD, o_spec=_dw2_out(l), o_shape=gshape, prev=dg2)
    dh = matmul(da, g1, tb=True, name=f"{tag}_up_dx", mnk=(T, D, D_FF), b_spec=_w1_rows_t(l))
    dg1 = matmul(h, da, ta=True, name=f"{tag}_up_dw", o_spec=_dw1_out(l), o_shape=gshape, prev=dg1)
    return dh, dg1, dg2


def kernel(x, mix_pre_g, mix_post_g, mlp_pre_g, mlp_post_g, w_in_even, s5_lam_re, s5_lam_im, s5_log_dt, s5_b_re, s5_b_im, s5_c_re, s5_c_im, s5_d, s5_w_glu, fox_b_f, w_out_even, w_in_odd, pool_w, pool_scale, sgu_ln_g, sgu_ln_b, sgu_w_s, sgu_b_s, w_out_odd, mlp_w1, mlp_w2, loss_target, m_mix_pre_g, m_mix_post_g, m_mlp_pre_g, m_mlp_post_g, m_w_in_even, m_s5_lam_re, m_s5_lam_im, m_s5_log_dt, m_s5_b_re, m_s5_b_im, m_s5_c_re, m_s5_c_im, m_s5_d, m_s5_w_glu, m_fox_b_f, m_w_out_even, m_w_in_odd, m_pool_w, m_pool_scale, m_sgu_ln_g, m_sgu_ln_b, m_sgu_w_s, m_sgu_b_s, m_w_out_odd, m_mlp_w1, m_mlp_w2, v_mix_pre_g, v_mix_post_g, v_mlp_pre_g, v_mlp_post_g, v_w_in_even, v_s5_lam_re, v_s5_lam_im, v_s5_log_dt, v_s5_b_re, v_s5_b_im, v_s5_c_re, v_s5_c_im, v_s5_d, v_s5_w_glu, v_fox_b_f, v_w_out_even, v_w_in_odd, v_pool_w, v_pool_scale, v_sgu_ln_g, v_sgu_ln_b, v_sgu_w_s, v_sgu_b_s, v_w_out_odd, v_mlp_w1, v_mlp_w2):
    names = [n for n, _ in SMALL_SEGS] + [n for n, _, _ in BIG_SEGS] + list(SHARDED_SMALL)
    env = dict(locals())
    W = {n: env[n] for n in names}
    M = {n: env["m_" + n] for n in names}
    V = {n: env["v_" + n] for n in names}

    def shard(n):
        return W[n].reshape(-1, W[n].shape[-1]).astype(WIRE_DTYPE)

    small = jnp.pad(jnp.concatenate([W[n] for n in SHARDED_SMALL]), ((0, SUBLANES - len(SHARDED_SMALL)), (0, 0)))
    gathered, small_all = allgather_chip_shards([shard(n) for n in EARLY_NAMES], small, name="allgather_weights")
    Wf = dict(zip(EARLY_NAMES, gathered))
    for i, n in enumerate(SHARDED_SMALL):
        Wf[n] = small_all[:, i, :].reshape(1, N_CHIPS * LANES)
    for n, _ in SMALL_SEGS:
        Wf[n] = W[n]

    loss8, dx0, halves, local_small = _local_step(x[0], loss_target[0], Wf, [shard(n) for n in LATE_NAMES])
    return _reduce_and_update(W, M, V, loss8, dx0, halves, local_small)


def _reduce_to_my_half(gs, names, tag, carry_swap=None, carry_ici=None):
    cx, cy, cc = _coords()
    c_idx = cc.reshape(1).astype(jnp.int32)
    jc_idx = jnp.stack([2 * cx + cy, cc]).astype(jnp.int32)
    swap = swap_halves_exchange(gs)
    from_sibling = carry_swap(swap) if carry_swap else run_exchange(swap, name=f"{tag}_to_sibling")
    sums = [add_sibling_half(g, l, c_idx, name=f"{tag}_chip_sum_{n}") for n, g, l in zip(names, gs, from_sibling)]
    send = chip_partials_exchange([pb for _, pb in sums])
    from_chips = carry_ici(send) if carry_ici else run_exchange(send, name=f"{tag}_to_chips")
    return [add_chip_partials(pf, r, jc_idx, name=f"{tag}_sum_{n}") for n, (pf, _), r in zip(names, sums, from_chips)]


def _local_step(x0, target, P, late_shards):
    T = x0.shape[0]
    mix_pre_g, mix_post_g, mlp_pre_g, mlp_post_g = P["mix_pre_g"], P["mix_post_g"], P["mlp_pre_g"], P["mlp_post_g"]
    s5_lam_re, s5_lam_im, s5_log_dt = P["s5_lam_re"], P["s5_lam_im"], P["s5_log_dt"]
    s5_b_re, s5_b_im, s5_c_re, s5_c_im, s5_d = P["s5_b_re"], P["s5_b_im"], P["s5_c_re"], P["s5_c_im"], P["s5_d"]
    fox_b_f, pool_w, sgu_w_s, sgu_b_s = P["fox_b_f"], P["pool_w"], P["sgu_w_s"], P["sgu_b_s"]
    pool_scale_f, ln_g_f, ln_b_f = P["pool_scale"], P["sgu_ln_g"], P["sgu_ln_b"]
    w_in_e = jnp.pad(_cols_from_chips(P["w_in_even"]), ((0, 0), (0, EVEN_IN_PAD - EVEN_IN)))
    w_glu = P["s5_w_glu"].reshape(S5_WIDTH, S5_WIDTH)

    def gain(a, l):
        return a[l][None, :]

    lr = s5_lam_re[0].reshape(1, S5_LANES)
    li = s5_lam_im[0].reshape(1, S5_LANES)
    ldt = jnp.repeat(s5_log_dt[0], S5_STATE).reshape(1, S5_LANES)
    btr = s5_b_re[0].reshape(S5_LANES, S5_GROUP).T
    bti = s5_b_im[0].reshape(S5_LANES, S5_GROUP).T
    tf_re, tf_im, tb_re, tb_im, bbt_re, bbt_im = s5_disc_fwd(lr, li, ldt, btr, bti, name="s5_disc")
    same_group = (jnp.arange(S5_WIDTH)[:, None] // S5_GROUP) == (jnp.arange(S5_LANES)[None, :] // S5_STATE)
    b_bd = s5_interleave(jnp.where(same_group, jnp.tile(bbt_re, (S5_GROUPS, 1)), 0.0),
                         jnp.where(same_group, jnp.tile(bbt_im, (S5_GROUPS, 1)), 0.0), axis=1)
    cr2 = jnp.transpose(s5_c_re[0], (0, 2, 1)).reshape(S5_LANES, S5_GROUP)
    ci2 = jnp.transpose(s5_c_im[0], (0, 2, 1)).reshape(S5_LANES, S5_GROUP)
    c_bd = s5_interleave(jnp.where(same_group.T, jnp.tile(cr2, (1, S5_GROUPS)), 0.0),
                         -jnp.where(same_group.T, jnp.tile(ci2, (1, S5_GROUPS)), 0.0), axis=0)
    bf_pad = jnp.pad(fox_b_f, ((0, 0), (0, LANES - FOX_HEADS)))

    h1 = rms_fwd(x0, gain(mix_pre_g, 0), name="l0_pre_norm")
    z = matmul(h1, w_in_e, name="l0_in_proj")
    s5_tiles = dict(tm=_pick(T, S5_NB), exact_tiles=True)
    xs = s5_scan(z, b_bd, tf_re, tf_im, reverse=False, name="s5_scan_fwd")
    yc = matmul(xs, c_bd, mnk=(T, S5_WIDTH, 2 * S5_NB), tn=S5_CB, a_spec=_lanes_of_chan, b_spec=_s5_c_block,
                name="s5_cx", **s5_tiles)
    yl, yg = s5_out_fwd(yc, z, s5_d, name="s5_out")
    gl = matmul(yg, w_glu, name="s5_glu_proj")
    ycat = glu_fwd(yg, gl, out_cols=D_MODEL, name="s5_glu")
    fgate = fox_gate_fwd(z, bf_pad, fl_col=FL_TILE, name="fox_gate")
    f_col = _pairs_col(fgate, T)
    f_row = _col_to_row(f_col, T)
    (ycat, lse_col), late = fox_fwd(z, f_col, f_row, ycat, allgather_ici_exchange(late_shards), name="fox_fwd")
    late = dict(zip(LATE_NAMES, allgather_forward(late, name="allgather_late_weights")))
    w_in_o = _cols_from_chips(late["w_in_odd"])
    w_in_o = jnp.concatenate([w_in_o[:, S5_WIDTH:], w_in_o[:, :S5_WIDTH]], axis=1)
    w_out_e = late["w_out_even"].reshape(D_MODEL, D_MODEL)
    w_out_o = late["w_out_odd"].reshape(D_MODEL, D_MODEL)
    g1, g2 = late["mlp_w1"], late["mlp_w2"]
    mo = matmul(ycat, w_out_e, name="l0_out_proj")
    x1, h2 = res_norm_fwd(x0, mo, gain(mix_post_g, 0), gain(mlp_pre_g, 0), name="l0_post_mlp0_pre_norm")
    m0, mlp0 = _mlp_fwd(h2, g1, g2, 0, "mlp0")

    x2, h3 = res_norm_fwd(x1, m0, gain(mlp_post_g, 0), gain(mix_pre_g, 1), name="mlp0_post_l1_pre_norm")
    z2 = matmul(h3, w_in_o, name="l1_in_proj")
    pooled = pool_window(z2, adjoint=False, in_col=POOL_COL, name="pool_fwd")
    pw_bd = _block_diag(pool_w[0])
    pw = matmul(pooled, pw_bd, name="pool_proj")
    ycat2 = colscale_fwd(pw, pool_scale_f, out_cols=D_MODEL, name="pool_scale")
    causal = jnp.tril(jnp.ones((CHUNK, CHUNK), dtype=bool))
    wsm = jnp.where(causal[None], sgu_w_s[0], 0.0)
    wsmt = jnp.transpose(wsm, (0, 2, 1))
    bst = sgu_b_s[0].T
    ycat2 = sgu_fwd(z2, ln_g_f, ln_b_f, wsm, bst, ycat2, name="sgu_fwd")
    mo2 = matmul(ycat2, w_out_o, name="l1_out_proj")
    x3, h4 = res_norm_fwd(x2, mo2, gain(mix_post_g, 1), gain(mlp_pre_g, 1), name="l1_post_mlp1_pre_norm")
    m1, mlp1 = _mlp_fwd(h4, g1, g2, 1, "mlp1")
    loss8, dx4 = res_norm_loss(x3, m1, gain(mlp_post_g, 1), target, name="mlp1_post_norm_loss")

    dm1, dg_mlp_post1 = rms_bwd(m1, gain(mlp_post_g, 1), dx4, None, name="mlp1_post_norm_bwd")
    dh4, dg1, dg2 = _mlp_bwd(mlp1, dm1, g1, g2, 1, None, None, "mlp1")
    dx3, dmo2, dg_mlp_pre1, dg_mix_post1 = norm_res_bwd(x3, gain(mlp_pre_g, 1), dh4, dx4, mo2, gain(mix_post_g, 1),
                                                        name="mlp1_pre_l1_post_norm_bwd")
    dycat2 = matmul(dmo2, w_out_o, tb=True, name="l1_out_proj_dx")
    dw_out_o = matmul(ycat2, dmo2, ta=True, name="l1_out_proj_dw")
    dpw, dpool_scale = colscale_bwd(pw, pool_scale_f, dycat2, name="pool_scale_bwd")
    dpooled = matmul(dpw, pw_bd, tb=True, name="pool_proj_dx")
    dpw_bd = matmul(pooled, dpw, ta=True, name="pool_proj_dw")
    dz2, dln_g, dln_b, dws, dbst = sgu_bwd(z2, ln_g_f, ln_b_f, wsm, wsmt, bst, dycat2, out_cols=3 * S5_WIDTH,
                                           name="sgu_bwd")
    dz2 = pool_window(dpooled, adjoint=True, into=dz2, out_col=POOL_COL, name="pool_bwd")
    dh3 = matmul(dz2, w_in_o, tb=True, name="l1_in_proj_dx")
    dw_in_o = matmul(h3, dz2, ta=True, name="l1_in_proj_dw")
    dw_in_o = jnp.concatenate([dw_in_o[:, 2 * S5_WIDTH:], dw_in_o[:, :2 * S5_WIDTH]], axis=1)
    dx2, dm0, dg_mix_pre1, dg_mlp_post0 = norm_res_bwd(x2, gain(mix_pre_g, 1), dh3, dx3, m0, gain(mlp_post_g, 0),
                                                       name="l1_pre_mlp0_post_norm_bwd")

    dh2, dg1, dg2 = _mlp_bwd(mlp0, dm0, g1, g2, 0, dg1, dg2, "mlp0")
    dx1, dmo, dg_mlp_pre0, dg_mix_post0 = norm_res_bwd(x1, gain(mlp_pre_g, 0), dh2, dx2, mo, gain(mix_post_g, 0),
                                                       name="mlp0_pre_l0_post_norm_bwd")
    dycat = matmul(dmo, w_out_e, tb=True, name="l0_out_proj_dx")
    dw_out_e = matmul(ycat, dmo, ta=True, name="l0_out_proj_dw")
    dyg_a, dgl = glu_bwd(yg, gl, dycat, name="s5_glu_bwd")
    dyg_b = matmul(dgl, w_glu, tb=True, name="s5_glu_proj_dx")
    dw_glu = matmul(yg, dgl, ta=True, name="s5_glu_proj_dw")
    dyl, du_skip, dd = s5_out_bwd(yl, z, s5_d, dyg_a, dyg_b, name="s5_out_bwd")
    dc_blocks = matmul(xs, dyl, ta=True, mnk=(2 * S5_LANES, S5_CB, T), tm=S5_NB, tn=S5_CB, b_spec=_chan_cols_of_i,
                       exact_tiles=True, name="s5_cx_dw")
    early_grads = {"s5_w_glu": dw_glu.reshape(N_CHIPS, -1, S5_WIDTH), "w_out_even": dw_out_e.reshape(N_CHIPS, -1, D_MODEL),
                   "w_in_odd": _chips_from_cols(dw_in_o), "w_out_odd": dw_out_o.reshape(N_CHIPS, -1, D_MODEL),
                   "mlp_w1": dg1, "mlp_w2": dg2}
    got = {}

    def reverse_scan(exchange):
        (got["lam"], got["dab_re"], got["dab_im"]), bufs = s5_scan(dyl, c_bd, tb_re, tb_im, reverse=True, states=xs,
                                                                   hosted=exchange, name="s5_scan_bwd")
        return bufs

    def attention_bwd(exchange):
        dd_col = fox_dd(ycat, dycat, name="fox_dd")
        (got["dk"], got["dv"], got["dfk"], got["dqt"], got["dfq"]), bufs = fox_bwd(
            z, dycat, f_col, f_row, _col_to_row(lse_col, T), _col_to_row(dd_col, T), exchange, name="fox_bwd")
        return bufs

    halves = _reduce_to_my_half([early_grads[n] for n in REDUCED_EARLY], REDUCED_EARLY, "early_grads",
                                reverse_scan, attention_bwd)
    lam, dab_re, dab_im, dk, dv = got["lam"], got["dab_re"], got["dab_im"], got["dk"], got["dv"]
    db_blocks = matmul(z, lam, ta=True, mnk=(S5_CB, 2 * S5_LANES, T), tm=S5_CB, tn=S5_NB, a_spec=_chan_rows_t,
                       exact_tiles=True, name="s5_bu_dw")
    du_b = matmul(lam, b_bd, tb=True, mnk=(T, S5_WIDTH, 2 * S5_NB), tn=S5_CB, a_spec=_lanes_of_chan,
                  b_spec=_s5_b_block_t, name="s5_bu_dx", **s5_tiles)
    du = add2(du_skip, du_b, name="s5_du")
    dq = jnp.transpose(got["dqt"], (1, 3, 0, 2)).reshape(T, FOX_WIDTH) * (FOX_HEAD_DIM ** -0.5)
    dfl, dbf = fox_gate_bwd(z, bf_pad, _pairs_to_lanes(got["dfk"], T), _pairs_to_lanes(_row_to_col(got["dfq"], T), T),
                            fl_col=FL_TILE, name="fox_gate_bwd")
    dz = jnp.concatenate([du, dq, dk, dv, dfl], axis=1).astype(MXU_DTYPE)
    dw_in_e = matmul(h1, dz, ta=True, name="l0_in_proj_dw")[:, :EVEN_IN]

    def in_proj_dx(exchange):
        got["dh1"], bufs = matmul(dz, w_in_e, tb=True, hosted=exchange, name="l0_in_proj_dx")
        return bufs

    def pre_norm_bwd(exchange):
        (got["dx0"], got["dg_mix_pre0"]), bufs = rms_bwd(x0, gain(mix_pre_g, 0), got["dh1"], dx1, hosted=exchange,
                                                         name="l0_pre_norm_bwd")
        return bufs

    halves = halves + _reduce_to_my_half([_chips_from_cols(dw_in_e)], ["w_in_even"], "late_grads", in_proj_dx, pre_norm_bwd)
    dx0, dg_mix_pre0 = got["dx0"], got["dg_mix_pre0"]

    groups_per_block = S5_CB // S5_GROUP
    own_group = (jnp.arange(S5_CB)[:, None] // S5_GROUP) == ((jnp.arange(S5_LANES)[None, :] // S5_STATE) % groups_per_block)
    db_re, db_im = s5_deinterleave(db_blocks, axis=1)
    dbbt_re = jnp.where(own_group, db_re, 0.0).reshape(groups_per_block, S5_GROUP, S5_LANES).sum(0)
    dbbt_im = jnp.where(own_group, db_im, 0.0).reshape(groups_per_block, S5_GROUP, S5_LANES).sum(0)
    dlr, dli, dldt8, dbtr, dbti = s5_disc_bwd(lr, li, ldt, btr, bti, dab_re, dab_im, dbbt_re, dbbt_im, name="s5_disc_bwd")
    dc_re, dc_im = s5_deinterleave(dc_blocks, axis=0)
    dcr2 = jnp.where(own_group.T, dc_re, 0.0).reshape(S5_LANES, groups_per_block, S5_GROUP).sum(1)
    dci2 = -jnp.where(own_group.T, dc_im, 0.0).reshape(S5_LANES, groups_per_block, S5_GROUP).sum(1)

    def c_layout(a):
        return jnp.transpose(a.reshape(S5_GROUPS, S5_STATE, S5_GROUP), (0, 2, 1))[None]

    def b_layout(a):
        return a.T.reshape(1, S5_GROUPS, S5_STATE, S5_GROUP)

    local_small = {
        "mix_pre_g": jnp.concatenate([dg_mix_pre0, dg_mix_pre1]), "mix_post_g": jnp.concatenate([dg_mix_post0, dg_mix_post1]),
        "mlp_pre_g": jnp.concatenate([dg_mlp_pre0, dg_mlp_pre1]), "mlp_post_g": jnp.concatenate([dg_mlp_post0, dg_mlp_post1]),
        "s5_lam_re": dlr.reshape(1, S5_GROUPS, S5_STATE), "s5_lam_im": dli.reshape(1, S5_GROUPS, S5_STATE),
        "s5_log_dt": dldt8[0:1, 0:S5_GROUPS],
        "s5_b_re": b_layout(dbtr), "s5_b_im": b_layout(dbti), "s5_c_re": c_layout(dcr2), "s5_c_im": c_layout(dci2),
        "s5_d": dd, "fox_b_f": dbf[:, 0:FOX_HEADS],
        "pool_w": _diag_blocks(dpw_bd, len(POOL_WINDOWS))[None],
        "sgu_w_s": jnp.where(causal[None], dws, 0.0)[None], "sgu_b_s": dbst.T[None],
        "pool_scale": dpool_scale, "sgu_ln_g": dln_g, "sgu_ln_b": dln_b,
    }
    return loss8, dx0, dict(zip(REDUCED_EARLY + ("w_in_even",), halves)), local_small


def _reduce_and_update(W, M, V, loss8, dx0, halves, local_small):
    cx, cy, cc = _coords()
    chip = 2 * cx + cy

    summed = dict(local_small, loss=loss8[0:1, 0:1])
    vec = _pack_vec(summed, REDUCED_SEGS, N_DEV * SUBLANES)
    piece = vec.shape[0] // N_DEV
    landed = exchange_pieces(vec.reshape(N_DEV, piece, LANES), scatter=True, name="small_grads_scatter")
    mine = sum_pieces(landed, name="small_grads_sum")
    everyone = exchange_pieces(mine, scatter=False, name="small_grads_gather")
    G = _unpack_vec(everyone, REDUCED_SEGS)
    loss = G["loss"].reshape(())
    for n in SHARDED_SMALL:
        G[n] = lax.dynamic_slice_in_dim(G[n], chip * LANES, LANES, axis=1)

    reduced = join_sibling_halves([halves[n] for n in BIG_NAMES], name="big_grads_join")
    for n, r in zip(BIG_NAMES, reduced):
        G[n] = r.reshape(W[n].shape)

    def two_d(a):
        return a.reshape(-1, a.shape[-1])

    delta, new_m, new_v = {}, {}, {}
    for n in BIG_NAMES:
        d_, m_, v_ = adamw(two_d(W[n]), two_d(G[n]), two_d(M[n]), two_d(V[n]), name=f"adamw_{n}")
        delta[n], new_m[n], new_v[n] = (t.reshape(W[n].shape) for t in (d_, m_, v_))
    packed = [_pack_vec(src, SMALL_SEGS, SUBLANES) for src in (W, G, M, V)]
    outs = adamw(*packed, name="adamw_replicated")
    for dst, t in zip((delta, new_m, new_v), outs):
        dst.update(_unpack_vec(t, SMALL_SEGS))
    sharded_segs = tuple((n, (1, LANES)) for n in SHARDED_SMALL)
    packed = [_pack_vec(src, sharded_segs, 1) for src in (W, G, M, V)]
    outs = adamw(*packed, name="adamw_sharded_vectors")
    for dst, t in zip((delta, new_m, new_v), outs):
        dst.update(_unpack_vec(t, sharded_segs))

    order = ["mix_pre_g", "mix_post_g", "mlp_pre_g", "mlp_post_g", "w_in_even", "s5_lam_re", "s5_lam_im", "s5_log_dt",
             "s5_b_re", "s5_b_im", "s5_c_re", "s5_c_im", "s5_d", "s5_w_glu", "fox_b_f", "w_out_even", "w_in_odd",
             "pool_w", "pool_scale", "sgu_ln_g", "sgu_ln_b", "sgu_w_s", "sgu_b_s", "w_out_odd", "mlp_w1", "mlp_w2"]
    return (loss, dx0[None], *[G[n] for n in order], *[delta[n] for n in order],
            *[new_m[n] for n in order], *[new_v[n] for n in order])
```

```python
import functools
import math

import jax
import jax.numpy as jnp
from jax import lax
from jax.experimental import pallas as pl
from jax.experimental.pallas import tpu as pltpu

F32 = jnp.float32
MXU_DTYPE = jnp.bfloat16
WIRE_DTYPE = jnp.bfloat16
EPS = 1e-6
VMEM_LIMIT_BYTES = 48 * 1024 * 1024
LANES = 128
SUBLANES = 8

D_MODEL = 1024
S5_WIDTH = 512
S5_GROUP = 16
S5_GROUPS = 32
S5_STATE = 64
S5_LANES = S5_GROUPS * S5_STATE
FOX_HEADS = 8
FOX_HEAD_DIM = 64
FOX_WIDTH = 512
EVEN_IN = S5_WIDTH + 3 * FOX_WIDTH + FOX_HEADS
EVEN_IN_PAD = 2176
POOL_WINDOWS = (2, 4, 8, 16)
POOL_HALO = 16
POOL_GROUP_DIM = 128
SGU_GROUPS = 4
SGU_GROUP_DIM = 128
CHUNK = 128
D_FF = 4096

ADAM_LR = 0.001
ADAM_B1 = 0.9
ADAM_B2 = 0.999
ADAM_EPS = 1e-08
ADAM_WD = 0.01
ADAM_STEP = 10

MESH_AXES = ("x", "y", "c")
MESH = pl.DeviceIdType.MESH
N_CHIPS = 4
N_DEV = 8

SDS = jax.ShapeDtypeStruct


def _cp(*sem):
    return pltpu.CompilerParams(dimension_semantics=sem, vmem_limit_bytes=VMEM_LIMIT_BYTES)


def _pick(dim, pref):
    if dim <= pref:
        return dim
    t = pref
    while t >= 256:
        if dim % t == 0:
            return t
        t //= 2
    return dim


def _row(tr, c):
    return pl.BlockSpec((tr, c), lambda i: (i, 0))


def _full(shape):
    nd = len(shape)
    return pl.BlockSpec(shape, lambda *_: (0,) * nd)


def _gelu_grad(x):
    c = math.sqrt(2.0 / math.pi)
    t = jnp.tanh(c * (x + 0.044715 * x * x * x))
    return 0.5 * (1.0 + t) + 0.5 * x * (1.0 - t * t) * c * (1.0 + 3.0 * 0.044715 * x * x)


MATMUL_VMEM_BYTES = 36 * 1024 * 1024


def matmul(a, b, *, name, ta=False, tb=False, out_dtype=F32, tm=2048, tn=1024, tk=4096, mnk=None, a_koff=0,
           a_spec=None, b_spec=None, o_spec=None, o_shape=None, prev=None, epi=None, epi_in=(), out_dtypes=None,
           exact_tiles=False, hosted=None):
    if mnk is None:
        M, K = (a.shape[1], a.shape[0]) if ta else a.shape
        K2, N = (b.shape[1], b.shape[0]) if tb else b.shape
        assert K == K2, (a.shape, b.shape, ta, tb)
    else:
        M, N, K = mnk
    out_dtypes = tuple(out_dtypes) if out_dtypes is not None else (out_dtype,)
    n_out, n_epi = len(out_dtypes), len(epi_in)
    tm, tn, tk = _pick(M, tm), _pick(N, tn), _pick(K, tk)

    def vmem_bytes(tm_, tn_, tk_):
        tiles = tm_ * tk_ * a.dtype.itemsize + tk_ * tn_ * b.dtype.itemsize
        tiles += tm_ * tn_ * (sum(jnp.dtype(d).itemsize for d in out_dtypes) + sum(e.dtype.itemsize for e in epi_in))
        return 2 * tiles + tm_ * tn_ * 4 * (tk_ < K)

    def halves(t, dim):
        return [t] + ([t // 2] if t % (2 * LANES) == 0 and t // 2 >= 512 and dim % (t // 2) == 0 else [])

    if exact_tiles:
        halves = lambda t, dim: [t]
    fits = [(m_, n_) for m_ in halves(tm, M) for n_ in halves(tn, N) if vmem_bytes(m_, n_, tk) <= MATMUL_VMEM_BYTES]
    if fits:
        tm, tn = max(fits, key=lambda t: (t[0] * t[1], t[0]))
    else:
        tm, tn = halves(tm, M)[-1], halves(tn, N)[-1]
        while vmem_bytes(tm, tn, tk) > MATMUL_VMEM_BYTES and tk % 2 == 0 and tk > 512:
            tk //= 2
    nk = K // tk
    assert a_koff % tk == 0 and not (ta and a_koff)
    ko = a_koff // tk
    dn = (((0 if ta else 1,), (1 if tb else 0,)), ((), ()))

    def body(*refs):
        a_ref, b_ref = refs[0], refs[1]
        epi_refs = refs[2:2 + n_epi]
        o_refs = refs[len(refs) - n_out - (nk > 1):len(refs) - (nk > 1)]
        k = pl.program_id(2)
        bv = b_ref[...]
        if bv.ndim == 3 and tb:
            cw = bv.shape[-1]
            prod = sum(lax.dot_general(a_ref[:, c * cw:(c + 1) * cw].astype(MXU_DTYPE), bv[c].astype(MXU_DTYPE), dn,
                                       preferred_element_type=F32) for c in range(bv.shape[0]))
        else:
            if bv.ndim == 3:
                bv = bv.reshape(-1, bv.shape[-1])
            prod = lax.dot_general(a_ref[...].astype(MXU_DTYPE), bv.astype(MXU_DTYPE), dn, preferred_element_type=F32)

        def finish(acc):
            res = (acc,) if epi is None else epi(acc, *[r[...] for r in epi_refs])
            for o_ref, r in zip(o_refs, res):
                o_ref[...] = r.astype(o_ref.dtype)

        if nk == 1:
            finish(prod)
            return
        acc_ref = refs[-1]

        @pl.when(k == 0)
        def _():
            acc_ref[...] = prod

        @pl.when(jnp.logical_and(k > 0, k < nk - 1))
        def _():
            acc_ref[...] += prod

        @pl.when(k == nk - 1)
        def _():
            finish(acc_ref[...] + prod)

    if a_spec is None:
        a_spec = pl.BlockSpec((tk, tm), lambda i, j, k: (k, i)) if ta else pl.BlockSpec((tm, tk), lambda i, j, k: (i, k + ko))
    else:
        a_spec = a_spec(tm, tn, tk)
    if b_spec is None:
        bs = pl.BlockSpec((tn, tk), lambda i, j, k: (j, k)) if tb else pl.BlockSpec((tk, tn), lambda i, j, k: (k, j))
    else:
        bs = b_spec(tm, tn, tk)
    tile = pl.BlockSpec((tm, tn), lambda i, j, k: (i, j))
    os_ = tile if o_spec is None else o_spec(tm, tn, tk)
    ins, in_specs, aliases = [a, b, *epi_in], [a_spec, bs] + [tile] * n_epi, {}
    if prev is not None:
        aliases = {len(ins): 0}
        ins.append(prev)
        in_specs.append(pl.BlockSpec(memory_space=pl.ANY))
    shapes = [SDS((M, N) if o_shape is None else o_shape, dt) for dt in out_dtypes]
    scratch = [pltpu.VMEM((tm, tn), F32)] if nk > 1 else []
    if hosted is not None:
        outs, bufs = call_hosting(body, hosted, name=name, grid=(M // tm, N // tn, nk), in_specs=in_specs,
                                  out_specs=[os_] * n_out, out_shape=shapes, inputs=ins, aliases=aliases, scratch=scratch)
        return (outs[0] if n_out == 1 else outs), bufs
    outs = pl.pallas_call(
        body, name=name, grid=(M // tm, N // tn, nk),
        in_specs=in_specs, out_specs=[os_] * n_out, out_shape=shapes, input_output_aliases=aliases,
        scratch_shapes=scratch, compiler_params=_cp("parallel", "parallel", "arbitrary"),
    )(*ins)
    return outs[0] if n_out == 1 else outs


def _rms_hat(x):
    return x * lax.rsqrt(jnp.mean(x * x, axis=-1, keepdims=True) + EPS)


def rms_fwd(x, g, *, name):
    T, D = x.shape
    tr = _pick(T, 512)

    def body(x_ref, g_ref, o_ref):
        o_ref[...] = (_rms_hat(x_ref[...]) * g_ref[...]).astype(o_ref.dtype)

    return pl.pallas_call(body, name=name, grid=(T // tr,), in_specs=[_row(tr, D), _full((1, D))],
                          out_specs=_row(tr, D), out_shape=SDS((T, D), MXU_DTYPE), compiler_params=_cp("parallel"))(x, g)


def res_norm_fwd(x, y, g_post, g_next, *, name):
    T, D = x.shape
    tr = _pick(T, 512)

    def body(x_ref, y_ref, gp_ref, gn_ref, o_ref, h_ref):
        xn = x_ref[...] + _rms_hat(y_ref[...]) * gp_ref[...]
        o_ref[...] = xn
        h_ref[...] = (_rms_hat(xn) * gn_ref[...]).astype(h_ref.dtype)

    return pl.pallas_call(body, name=name, grid=(T // tr,),
                          in_specs=[_row(tr, D), _row(tr, D), _full((1, D)), _full((1, D))],
                          out_specs=[_row(tr, D), _row(tr, D)], out_shape=[SDS((T, D), F32), SDS((T, D), MXU_DTYPE)],
                          compiler_params=_cp("parallel"))(x, y, g_post, g_next)


def res_norm_loss(x, y, g_post, target, *, name):
    T, D = x.shape
    tr = _pick(T, 512)

    def body(x_ref, y_ref, g_ref, t_ref, l_ref, d_ref):
        err = x_ref[...] + _rms_hat(y_ref[...]) * g_ref[...] - t_ref[...]
        d_ref[...] = err * (1.0 / D)

        @pl.when(pl.program_id(0) == 0)
        def _():
            l_ref[...] = jnp.zeros_like(l_ref)

        l_ref[...] += 0.5 * jnp.sum(jnp.mean(err * err, axis=-1, keepdims=True))

    return pl.pallas_call(body, name=name, grid=(T // tr,),
                          in_specs=[_row(tr, D), _row(tr, D), _full((1, D)), _row(tr, D)],
                          out_specs=[_full((SUBLANES, LANES)), _row(tr, D)],
                          out_shape=[SDS((SUBLANES, LANES), F32), SDS((T, D), F32)],
                          compiler_params=_cp("arbitrary"))(x, y, g_post, target)


def _rms_bwd_rows(x, g, dy):
    r = lax.rsqrt(jnp.mean(x * x, axis=-1, keepdims=True) + EPS)
    xh = x * r
    dxh = dy * g
    return r * (dxh - xh * jnp.mean(dxh * xh, axis=-1, keepdims=True)), jnp.sum(dy * xh, axis=0, keepdims=True)


def norm_res_bwd(x, g_pre, dh, res, y, g_post, *, name):
    T, D = x.shape
    tr = _pick(T, 512)

    def body(x_ref, gp_ref, dh_ref, res_ref, y_ref, gy_ref, dx_ref, dy_ref, dgp_ref, dgy_ref):
        dx, dgp = _rms_bwd_rows(x_ref[...], gp_ref[...], dh_ref[...])
        dx = dx + res_ref[...]
        dx_ref[...] = dx
        dy, dgy = _rms_bwd_rows(y_ref[...], gy_ref[...], dx)
        dy_ref[...] = dy.astype(dy_ref.dtype)

        @pl.when(pl.program_id(0) == 0)
        def _():
            dgp_ref[...] = jnp.zeros_like(dgp_ref)
            dgy_ref[...] = jnp.zeros_like(dgy_ref)

        dgp_ref[...] += dgp
        dgy_ref[...] += dgy

    row, vec = _row(tr, D), _full((1, D))
    return pl.pallas_call(body, name=name, grid=(T // tr,), in_specs=[row, vec, row, row, row, vec],
                          out_specs=[row, row, vec, vec],
                          out_shape=[SDS((T, D), F32), SDS((T, D), MXU_DTYPE), SDS((1, D), F32), SDS((1, D), F32)],
                          compiler_params=_cp("arbitrary"))(x, g_pre, dh, res, y, g_post)


def rms_bwd(x, g, dy, res, *, name, hosted=None):
    T, D = x.shape
    tr = _pick(T, 512)
    has_res = res is not None

    def body(*refs):
        if has_res:
            x_ref, g_ref, dy_ref, res_ref, dx_ref, dg_ref = refs
        else:
            x_ref, g_ref, dy_ref, dx_ref, dg_ref = refs
        dx, dg = _rms_bwd_rows(x_ref[...], g_ref[...], dy_ref[...])
        if has_res:
            dx = dx + res_ref[...]
        dx_ref[...] = dx.astype(dx_ref.dtype)

        @pl.when(pl.program_id(0) == 0)
        def _():
            dg_ref[...] = jnp.zeros_like(dg_ref)

        dg_ref[...] += dg

    ins = [x, g, dy] + ([res] if has_res else [])
    in_specs = [_row(tr, D), _full((1, D)), _row(tr, D)] + ([_row(tr, D)] if has_res else [])
    out_shape = [SDS((T, D), F32 if has_res else MXU_DTYPE), SDS((1, D), F32)]
    out_specs = [_row(tr, D), _full((1, D))]
    if hosted is not None:
        return call_hosting(body, hosted, name=name, grid=(T // tr,), in_specs=in_specs, out_specs=out_specs,
                            out_shape=out_shape, inputs=ins, aliases={})
    return pl.pallas_call(body, name=name, grid=(T // tr,), in_specs=in_specs, out_specs=out_specs,
                          out_shape=out_shape, compiler_params=_cp("arbitrary"))(*ins)


def _s5_disc(lr, li, ldt, btr, bti):
    dt = jnp.exp(ldt)
    k = lax.broadcasted_iota(jnp.int32, (SUBLANES, S5_LANES), 0).astype(F32)
    kf = k + 1.0
    kb = 8.0 - k
    ph = li * dt
    lm = lr * dt
    tf_re = jnp.exp(kf * lm) * jnp.cos(kf * ph)
    tf_im = jnp.exp(kf * lm) * jnp.sin(kf * ph)
    tb_re = jnp.exp(kb * lm) * jnp.cos(kb * ph)
    tb_im = -jnp.exp(kb * lm) * jnp.sin(kb * ph)
    mag = jnp.exp(lm)
    ab_re = mag * jnp.cos(ph)
    ab_im = mag * jnp.sin(ph)
    den = lr * lr + li * li
    nr = ab_re - 1.0
    ni = ab_im
    q_re = (nr * lr + ni * li) / den
    q_im = (ni * lr - nr * li) / den
    bbt_re = q_re * btr - q_im * bti
    bbt_im = q_re * bti + q_im * btr
    return tf_re, tf_im, tb_re, tb_im, bbt_re, bbt_im


def _s5_disc_core(lr, li, ldt, btr, bti):
    dt = jnp.exp(ldt)
    mag = jnp.exp(lr * dt)
    ab_re = mag * jnp.cos(li * dt)
    ab_im = mag * jnp.sin(li * dt)
    den = lr * lr + li * li
    nr = ab_re - 1.0
    ni = ab_im
    q_re = (nr * lr + ni * li) / den
    q_im = (ni * lr - nr * li) / den
    return ab_re, ab_im, q_re * btr - q_im * bti, q_re * bti + q_im * btr


def s5_disc_fwd(lr, li, ldt, btr, bti, *, name):
    def body(lr_ref, li_ref, ldt_ref, btr_ref, bti_ref, *outs):
        vals = _s5_disc(lr_ref[...], li_ref[...], ldt_ref[...], btr_ref[...], bti_ref[...])
        for o, v in zip(outs, vals):
            o[...] = v

    tab = SDS((SUBLANES, S5_LANES), F32)
    bb = SDS((S5_GROUP, S5_LANES), F32)
    return pl.pallas_call(body, name=name, out_shape=[tab, tab, tab, tab, bb, bb])(lr, li, ldt, btr, bti)


def s5_disc_bwd(lr, li, ldt, btr, bti, dab_re, dab_im, dbbt_re, dbbt_im, *, name):
    def body(lr_ref, li_ref, ldt_ref, btr_ref, bti_ref, dar_ref, dai_ref, dbr_ref, dbi_ref,
             dlr_ref, dli_ref, dldt_ref, dbtr_ref, dbti_ref):
        _, vjp = jax.vjp(_s5_disc_core, lr_ref[...], li_ref[...], ldt_ref[...], btr_ref[...], bti_ref[...])
        dlr, dli, dldt, dbtr, dbti = vjp((dar_ref[...], dai_ref[...], dbr_ref[...], dbi_ref[...]))
        dlr_ref[...] = dlr
        dli_ref[...] = dli
        dbtr_ref[...] = dbtr
        dbti_ref[...] = dbti
        lane_group = lax.broadcasted_iota(jnp.int32, (S5_LANES, LANES), 0) // S5_STATE
        col = lax.broadcasted_iota(jnp.int32, (S5_LANES, LANES), 1)
        ind = (lane_group == col).astype(F32)
        dldt_ref[...] = jnp.dot(jnp.broadcast_to(dldt, (SUBLANES, S5_LANES)), ind,
                                precision=lax.Precision.HIGHEST, preferred_element_type=F32)

    row = SDS((1, S5_LANES), F32)
    bb = SDS((S5_GROUP, S5_LANES), F32)
    return pl.pallas_call(body, name=name, out_shape=[row, row, SDS((SUBLANES, LANES), F32), bb, bb])(
        lr, li, ldt, btr, bti, dab_re, dab_im, dbbt_re, dbbt_im)


S5_NB = 1024


S5_CB = S5_WIDTH * S5_NB // S5_LANES


def _chan_rows_t(tm, tn, tk):
    return pl.BlockSpec((tk, S5_CB), lambda i, j, k: (k, j // 2))


def _chan_cols_of_i(tm, tn, tk):
    return pl.BlockSpec((tk, S5_CB), lambda i, j, k: (k, i // 2))


def _lanes_of_chan(tm, tn, tk):
    return pl.BlockSpec((tm, 2 * S5_NB), lambda i, j, k: (i, j))


def _s5_b_block_t(tm, tn, tk):
    return pl.BlockSpec((S5_CB, 2 * S5_NB), lambda i, j, k: (j, j))


def _s5_c_block(tm, tn, tk):
    return pl.BlockSpec((2 * S5_NB, S5_CB), lambda i, j, k: (j, j))


def s5_interleave(re, im, axis):
    parts = []
    for n in range(S5_LANES // S5_NB):
        sl = [slice(None)] * re.ndim
        sl[axis] = slice(n * S5_NB, (n + 1) * S5_NB)
        parts += [re[tuple(sl)], im[tuple(sl)]]
    return jnp.concatenate(parts, axis=axis)


def s5_deinterleave(a, axis):
    re, im = [], []
    for n in range(S5_LANES // S5_NB):
        sl = [slice(None)] * a.ndim
        sl[axis] = slice(2 * n * S5_NB, (2 * n + 1) * S5_NB)
        re.append(a[tuple(sl)])
        sl[axis] = slice((2 * n + 1) * S5_NB, (2 * n + 2) * S5_NB)
        im.append(a[tuple(sl)])
    return jnp.concatenate(re, axis=axis), jnp.concatenate(im, axis=axis)


def s5_scan(src, mat, tab_re, tab_im, *, reverse, name, states=None, hosted=None):
    T = src.shape[0]
    nb = S5_NB
    tc = _pick(T, 256)
    nl = S5_LANES // nb
    nt = T // tc
    ntile = tc // SUBLANES
    with_da = states is not None
    assert reverse or not with_da
    step_rows = ((1, 7), (2, 6), (4, 4)) if reverse else ((1, 0), (2, 1), (4, 3))
    drive_dn = _NT if reverse else (((1,), (0,)), ((), ()))

    def body(*refs):
        if with_da:
            (src_ref, wr_ref, wi_ref, tr_ref, ti_ref, sr_ref, si_ref, hr_ref, hi_ref, xo_ref, dar_ref, dai_ref,
             cr_ref, ci_ref, mr_ref, mi_ref, br_ref, bi_ref, ar_ref, ai_ref) = refs
        else:
            src_ref, wr_ref, wi_ref, tr_ref, ti_ref, xo_ref, cr_ref, ci_ref, mr_ref, mi_ref, br_ref, bi_ref = refs

        @pl.when(pl.program_id(1) == 0)
        def _():
            cr_ref[...] = jnp.zeros_like(cr_ref)
            ci_ref[...] = jnp.zeros_like(ci_ref)
            if with_da:
                ar_ref[...] = jnp.zeros_like(ar_ref)
                ai_ref[...] = jnp.zeros_like(ai_ref)

        lhs = src_ref[...].astype(MXU_DTYPE)
        br_ref[...] = lax.dot_general(lhs, wr_ref[...].astype(MXU_DTYPE), drive_dn, preferred_element_type=F32)
        bi_ref[...] = lax.dot_general(lhs, wi_ref[...].astype(MXU_DTYPE), drive_dn, preferred_element_type=F32)

        seen = jnp.where(pl.program_id(1) < nt - 1, 1.0, 0.0)

        def add_da(lr, li, r0, last_r, last_i):
            first = lax.broadcasted_iota(jnp.int32, (SUBLANES, nb), 0) == 0
            pr = jnp.where(first, last_r, pltpu.roll(sr_ref[pl.ds(r0, SUBLANES), :], 1, 0))
            pi = jnp.where(first, last_i, pltpu.roll(si_ref[pl.ds(r0, SUBLANES), :], 1, 0))
            ar_ref[...] += lr * pr + li * pi
            ai_ref[...] += li * pr - lr * pi

        io = lax.broadcasted_iota(jnp.int32, (SUBLANES, nb), 0)
        for s_, (d, r) in enumerate(step_rows):
            keep = (io < SUBLANES - d) if reverse else (io >= d)
            mr_ref[s_] = jnp.where(keep, tr_ref[r:r + 1, :], 0.0)
            mi_ref[s_] = jnp.where(keep, ti_ref[r:r + 1, :], 0.0)

        def tile(i, carry):
            cr, ci = carry
            j = (ntile - 1 - i) if reverse else i
            r0 = pl.multiple_of(j * SUBLANES, SUBLANES)
            xr = br_ref[pl.ds(r0, SUBLANES), :]
            xi = bi_ref[pl.ds(r0, SUBLANES), :]
            for s_, (d, _) in enumerate(step_rows):
                sh = (SUBLANES - d) if reverse else d
                sr = pltpu.roll(xr, sh, 0)
                si = pltpu.roll(xi, sh, 0)
                pr, pi = mr_ref[s_], mi_ref[s_]
                xr, xi = xr + pr * sr - pi * si, xi + pr * si + pi * sr
            tr, ti = tr_ref[...], ti_ref[...]
            xr, xi = xr + tr * cr - ti * ci, xi + tr * ci + ti * cr
            xo_ref[pl.ds(r0, SUBLANES), 0:nb] = xr
            xo_ref[pl.ds(r0, SUBLANES), nb:2 * nb] = xi
            if with_da:
                @pl.when(j > 0)
                def _():
                    p0 = pl.multiple_of(r0 - SUBLANES, SUBLANES)
                    add_da(xr, xi, r0, sr_ref[pl.ds(p0, SUBLANES), :][SUBLANES - 1:SUBLANES, :],
                           si_ref[pl.ds(p0, SUBLANES), :][SUBLANES - 1:SUBLANES, :])

                @pl.when(j == 0)
                def _():
                    add_da(xr, xi, r0, hr_ref[SUBLANES - 1:SUBLANES, :] * seen, hi_ref[SUBLANES - 1:SUBLANES, :] * seen)
            if reverse:
                return xr[0:1, :], xi[0:1, :]
            return xr[SUBLANES - 1:SUBLANES, :], xi[SUBLANES - 1:SUBLANES, :]

        cr, ci = lax.fori_loop(0, ntile, tile, (cr_ref[0:1, :], ci_ref[0:1, :]))
        cr_ref[0:1, :] = cr
        ci_ref[0:1, :] = ci
        if with_da:
            @pl.when(pl.program_id(1) == nt - 1)
            def _():
                dar_ref[...] = jnp.sum(ar_ref[...], axis=0, keepdims=True)
                dai_ref[...] = jnp.sum(ai_ref[...], axis=0, keepdims=True)

    def tmap(t):
        return (nt - 1 - t) if reverse else t

    hb = tc // SUBLANES
    re_spec = pl.BlockSpec((tc, nb), lambda n, t: (tmap(t), 2 * n))
    im_spec = pl.BlockSpec((tc, nb), lambda n, t: (tmap(t), 2 * n + 1))
    tab_spec = pl.BlockSpec((SUBLANES, nb), lambda n, t: (0, n))
    out_spec = pl.BlockSpec((tc, 2 * nb), lambda n, t: (tmap(t), n))
    out_shape = SDS((T, 2 * S5_LANES), F32)
    scratch = [pltpu.VMEM((SUBLANES, nb), F32), pltpu.VMEM((SUBLANES, nb), F32),
               pltpu.VMEM((len(step_rows), SUBLANES, nb), F32), pltpu.VMEM((len(step_rows), SUBLANES, nb), F32),
               pltpu.VMEM((tc, nb), F32), pltpu.VMEM((tc, nb), F32)]
    src_spec = pl.BlockSpec((tc, S5_CB), lambda n, t: (tmap(t), n))
    if reverse:
        wr_spec = pl.BlockSpec((nb, S5_CB), lambda n, t: (2 * n, n))
        wi_spec = pl.BlockSpec((nb, S5_CB), lambda n, t: (2 * n + 1, n))
    else:
        wr_spec = pl.BlockSpec((S5_CB, nb), lambda n, t: (n, 2 * n))
        wi_spec = pl.BlockSpec((S5_CB, nb), lambda n, t: (n, 2 * n + 1))
    drive_specs = [src_spec, wr_spec, wi_spec, tab_spec, tab_spec]
    drive = [src, mat, mat, tab_re, tab_im]
    if not with_da:
        return pl.pallas_call(body, name=name, grid=(nl, nt), in_specs=drive_specs,
                              out_specs=out_spec, out_shape=out_shape, scratch_shapes=scratch,
                              compiler_params=_cp("parallel", "arbitrary"))(*drive)
    re_halo = pl.BlockSpec((SUBLANES, nb), lambda n, t: (jnp.maximum(tmap(t) * hb - 1, 0), 2 * n))
    im_halo = pl.BlockSpec((SUBLANES, nb), lambda n, t: (jnp.maximum(tmap(t) * hb - 1, 0), 2 * n + 1))
    acc = pl.BlockSpec((1, nb), lambda n, t: (0, n))
    row = SDS((1, S5_LANES), F32)
    return call_hosting(
        body, hosted, name=name, grid=(nl, nt),
        in_specs=drive_specs + [re_spec, im_spec, re_halo, im_halo],
        out_specs=[out_spec, acc, acc], out_shape=[out_shape, row, row],
        inputs=drive + [states, states, states, states], aliases={},
        scratch=scratch + [pltpu.VMEM((SUBLANES, nb), F32), pltpu.VMEM((SUBLANES, nb), F32)])


def s5_out_fwd(yc, u, d, *, name):
    T, C = yc.shape
    tr = _pick(T, 512)

    def body(yc_ref, u_ref, d_ref, yl_ref, yg_ref):
        yl = yc_ref[...] + d_ref[...] * u_ref[...]
        yl_ref[...] = yl
        yg_ref[...] = jax.nn.gelu(yl)

    return pl.pallas_call(body, name=name, grid=(T // tr,), in_specs=[_row(tr, C), _row(tr, C), _full((1, C))],
                          out_specs=[_row(tr, C)] * 2, out_shape=[SDS((T, C), F32)] * 2,
                          compiler_params=_cp("parallel"))(yc, u, d)


def glu_fwd(yg, gl, *, out_cols, name):
    T, C = yg.shape
    tr = _pick(T, 512)

    def body(yg_ref, gl_ref, o_ref):
        o_ref[...] = yg_ref[...] * jax.nn.sigmoid(gl_ref[...])

    return pl.pallas_call(body, name=name, grid=(T // tr,), in_specs=[_row(tr, C)] * 2, out_specs=_row(tr, C),
                          out_shape=SDS((T, out_cols), F32), compiler_params=_cp("parallel"))(yg, gl)


def glu_bwd(yg, gl, dy, *, name):
    T, C = yg.shape
    tr = _pick(T, 512)

    def body(yg_ref, gl_ref, dy_ref, dyg_ref, dgl_ref):
        s = jax.nn.sigmoid(gl_ref[...])
        dyv = dy_ref[...]
        dyg_ref[...] = dyv * s
        dgl_ref[...] = (dyv * yg_ref[...] * s * (1.0 - s)).astype(dgl_ref.dtype)

    return pl.pallas_call(body, name=name, grid=(T // tr,), in_specs=[_row(tr, C)] * 3, out_specs=[_row(tr, C)] * 2,
                          out_shape=[SDS((T, C), F32), SDS((T, C), MXU_DTYPE)],
                          compiler_params=_cp("parallel"))(yg, gl, dy)


def s5_out_bwd(yl, u, d, dyg_a, dyg_b, *, name):
    T, C = yl.shape
    tr = _pick(T, 512)

    def body(yl_ref, u_ref, d_ref, da_ref, db_ref, dyl_ref, du_ref, dd_ref):
        dyl = (da_ref[...] + db_ref[...]) * _gelu_grad(yl_ref[...])
        dyl_ref[...] = dyl.astype(dyl_ref.dtype)
        du_ref[...] = dyl * d_ref[...]

        @pl.when(pl.program_id(0) == 0)
        def _():
            dd_ref[...] = jnp.zeros_like(dd_ref)

        dd_ref[...] += jnp.sum(dyl * u_ref[...], axis=0, keepdims=True)

    return pl.pallas_call(body, name=name, grid=(T // tr,),
                          in_specs=[_row(tr, C), _row(tr, C), _full((1, C)), _row(tr, C), _row(tr, C)],
                          out_specs=[_row(tr, C), _row(tr, C), _full((1, C))],
                          out_shape=[SDS((T, C), MXU_DTYPE), SDS((T, C), F32), SDS((1, C), F32)],
                          compiler_params=_cp("arbitrary"))(yl, u, d, dyg_a, dyg_b)


def add2(a, b, *, name):
    T, C = a.shape
    tr = _pick(T, 512)

    def body(a_ref, b_ref, o_ref):
        o_ref[...] = a_ref[...] + b_ref[...]

    return pl.pallas_call(body, name=name, grid=(T // tr,), in_specs=[_row(tr, C)] * 2, out_specs=_row(tr, C),
                          out_shape=SDS((T, C), F32), compiler_params=_cp("parallel"))(a, b)


def _tri(n, upper):
    r = lax.broadcasted_iota(jnp.int32, (n, n), 0)
    c = lax.broadcasted_iota(jnp.int32, (n, n), 1)
    return ((c >= r) if upper else (c <= r)).astype(F32)


def fox_gate_fwd(fl, bf, *, fl_col, name):
    T = fl.shape[0]
    tb = _pick(T, 256)

    def body(fl_ref, bf_ref, f_ref, c_ref):
        @pl.when(pl.program_id(0) == 0)
        def _():
            c_ref[...] = jnp.zeros_like(c_ref)

        lf = jax.nn.log_sigmoid(fl_ref[...] + bf_ref[...])
        f = jnp.dot(_tri(tb, False), lf, precision=lax.Precision.HIGHEST, preferred_element_type=F32) + c_ref[0:1, :]
        f_ref[...] = f * LOG2E
        c_ref[0:1, :] = f[tb - 1:tb, :]

    fl_spec = pl.BlockSpec((tb, LANES), lambda i: (i, fl_col))
    return pl.pallas_call(body, name=name, grid=(T // tb,), in_specs=[fl_spec, _full((1, LANES))],
                          out_specs=_row(tb, LANES), out_shape=SDS((T, LANES), F32),
                          scratch_shapes=[pltpu.VMEM((SUBLANES, LANES), F32)], compiler_params=_cp("arbitrary"))(fl, bf)


def fox_gate_bwd(fl, bf, df_keys, df_queries, *, fl_col, name):
    T = fl.shape[0]
    tb = _pick(T, 256)
    nt = T // tb

    def body(fl_ref, bf_ref, dfk_ref, dfq_ref, dfl_ref, dbf_ref, c_ref):
        @pl.when(pl.program_id(0) == 0)
        def _():
            c_ref[...] = jnp.zeros_like(c_ref)
            dbf_ref[...] = jnp.zeros_like(dbf_ref)

        dlf = jnp.dot(_tri(tb, True), dfk_ref[...] + dfq_ref[...], precision=lax.Precision.HIGHEST,
                      preferred_element_type=F32) + c_ref[0:1, :]
        c_ref[0:1, :] = dlf[0:1, :]
        dfl = dlf * jax.nn.sigmoid(-(fl_ref[...] + bf_ref[...]))
        dfl_ref[...] = dfl
        dbf_ref[...] += jnp.sum(dfl, axis=0, keepdims=True)

    rev = pl.BlockSpec((tb, LANES), lambda i: (nt - 1 - i, 0))
    fl_rev = pl.BlockSpec((tb, LANES), lambda i: (nt - 1 - i, fl_col))
    return pl.pallas_call(body, name=name, grid=(nt,), in_specs=[fl_rev, _full((1, LANES)), rev, rev],
                          out_specs=[rev, _full((1, LANES))], out_shape=[SDS((T, LANES), F32), SDS((1, LANES), F32)],
                          scratch_shapes=[pltpu.VMEM((SUBLANES, LANES), F32)],
                          compiler_params=_cp("arbitrary"))(fl, bf, df_keys, df_queries)


FOX_BLOCK = 512
FOX_PAIRS = FOX_HEADS // 2
_NT = (((1,), (1,)), ((), ()))


LOG2E = 1.4426950408889634
FOX_FWD_UNROLL = 4
FOX_BWD_UNROLL = 2


def _fox_block(T):
    return _pick(T, FOX_BLOCK)


def _own_lanes(lane, hh):
    return (lane < FOX_HEAD_DIM) if hh == 0 else (lane >= FOX_HEAD_DIM)


def _grouped_steps(step, lo, n, unroll, init):
    def trip(t, c):
        for u in range(unroll):
            c = step(lo + t * unroll + u, c)
        return c

    carry = lax.fori_loop(0, n // unroll, trip, init)
    for u in range(unroll - 1):
        carry = lax.cond(n % unroll > u, lambda c: step(lo + (n // unroll) * unroll + u, c), lambda c: c, carry)
    return carry


Q_TILE0, K_TILE0, V_TILE0, O_TILE0 = 4, 8, 12, 4
FL_TILE = 16
POOL_COL = 2


def fox_fwd(z, f_col, f_row, ycat, hosted, *, name):
    T = z.shape[0]
    blk = _fox_block(T)
    nb = T // blk
    scale = FOX_HEAD_DIM ** -0.5

    def body(q_ref, k_ref, v_ref, fc_ref, fr_ref, prev_ref, o_ref, l_ref):
        i = pl.program_id(1)
        row = lax.broadcasted_iota(jnp.int32, (blk, blk), 0)
        col = lax.broadcasted_iota(jnp.int32, (blk, blk), 1)
        lane = lax.broadcasted_iota(jnp.int32, (blk, LANES), 1)
        qt = q_ref[...] * (scale * LOG2E)
        outs = []
        for hh in range(2):
            qh = jnp.where(_own_lanes(lane, hh), qt, 0.0).astype(MXU_DTYPE)
            fi = fc_ref[0, :, hh:hh + 1]

            def step(j, carry, masked=False):
                m, l, acc = carry
                r0 = pl.multiple_of(j * blk, blk)
                kj = k_ref[pl.ds(r0, blk), :].astype(MXU_DTYPE)
                vj = v_ref[pl.ds(r0, blk), :].astype(MXU_DTYPE)
                s = lax.dot_general(qh, kj, _NT, preferred_element_type=F32) + (fi - fr_ref[0, j, hh:hh + 1, :])
                if masked:
                    s = jnp.where(col <= row, s, -jnp.inf)
                m_new = jnp.maximum(m, jnp.max(s, axis=-1, keepdims=True))
                p = jnp.exp2(s - m_new)
                alpha = jnp.exp2(m - m_new)
                l = alpha * l + jnp.sum(p, axis=-1, keepdims=True)
                acc = alpha * acc + jnp.dot(p.astype(MXU_DTYPE), vj, preferred_element_type=F32)
                return m_new, l, acc

            init = (jnp.full((blk, 1), -jnp.inf, F32), jnp.zeros((blk, 1), F32), jnp.zeros((blk, LANES), F32))
            m, l, acc = step(i, _grouped_steps(step, 0, i, FOX_FWD_UNROLL, init), True)
            outs.append(acc / l)
            l_ref[0, :, hh:hh + 1] = m + jnp.log2(l)
        o_ref[...] = jnp.where(_own_lanes(lane, 0), outs[0], outs[1])

    qspec = pl.BlockSpec((blk, LANES), lambda h, i: (i, Q_TILE0 + h))
    kspec = pl.BlockSpec((T, LANES), lambda h, i: (0, K_TILE0 + h))
    vspec = pl.BlockSpec((T, LANES), lambda h, i: (0, V_TILE0 + h))
    ospec = pl.BlockSpec((blk, LANES), lambda h, i: (i, O_TILE0 + h))
    cspec = pl.BlockSpec((1, blk, 2), lambda h, i: (h, i, 0))
    rspec = pl.BlockSpec((1, nb, 2, blk), lambda h, i: (h, 0, 0, 0))
    return call_hosting(body, hosted, name=name, grid=(FOX_PAIRS, nb),
                        in_specs=[qspec, kspec, vspec, cspec, rspec, ANY], out_specs=[ospec, cspec],
                        out_shape=[SDS(ycat.shape, F32), SDS((FOX_PAIRS, T, 2), F32)],
                        inputs=[z, z, z, f_col, f_row, ycat], aliases={5: 0})


def fox_dd(ycat, dycat, *, name):
    T = ycat.shape[0]
    blk = _fox_block(T)

    def body(o_ref, do_ref, dd_ref):
        lane = lax.broadcasted_iota(jnp.int32, (blk, LANES), 1)
        prod = do_ref[...] * o_ref[...]
        for hh in range(2):
            dd_ref[0, :, hh:hh + 1] = jnp.sum(jnp.where(_own_lanes(lane, hh), prod, 0.0), axis=-1, keepdims=True)

    ospec = pl.BlockSpec((blk, LANES), lambda h, i: (i, O_TILE0 + h))
    return pl.pallas_call(body, name=name, grid=(FOX_PAIRS, T // blk), in_specs=[ospec, ospec],
                          out_specs=pl.BlockSpec((1, blk, 2), lambda h, i: (h, i, 0)),
                          out_shape=SDS((FOX_PAIRS, T, 2), F32), compiler_params=_cp("parallel", "parallel"))(ycat, dycat)


def fox_bwd(z, dycat, f_col, f_row, lse_row, dd_row, hosted, *, name):
    T = z.shape[0]
    blk = _fox_block(T)
    nb = T // blk
    scale = FOX_HEAD_DIM ** -0.5

    def body(q_ref, k_ref, v_ref, do_ref, fc_ref, fr_ref, lr_ref, dr_ref, dk_ref, dv_ref, df_ref, dqt_ref, dfq_ref):
        j = pl.program_id(1)

        @pl.when(j == 0)
        def _():
            dqt_ref[...] = jnp.zeros_like(dqt_ref)
            dfq_ref[...] = jnp.zeros_like(dfq_ref)

        row = lax.broadcasted_iota(jnp.int32, (blk, blk), 0)
        col = lax.broadcasted_iota(jnp.int32, (blk, blk), 1)
        lane = lax.broadcasted_iota(jnp.int32, (blk, LANES), 1)
        kt = k_ref[...]
        vt = v_ref[...]
        dks, dvs = [], []
        for hh in range(2):
            own = _own_lanes(lane, hh)
            kh = jnp.where(own, kt, 0.0).astype(MXU_DTYPE)
            vh = jnp.where(own, vt, 0.0).astype(MXU_DTYPE)
            kht = kh.T
            fj = fc_ref[0, :, hh:hh + 1]

            def step(i, carry, masked=False):
                dk, dv, df = carry
                r0 = pl.multiple_of(i * blk, blk)
                qi = (q_ref[pl.ds(r0, blk), :] * (scale * LOG2E)).astype(MXU_DTYPE)
                doi = do_ref[pl.ds(r0, blk), :].astype(MXU_DTYPE)
                st = lax.dot_general(kh, qi, _NT, preferred_element_type=F32) + (fr_ref[0, i, hh:hh + 1, :] - fj)
                pt = jnp.exp2(st - lr_ref[0, i, hh:hh + 1, :])
                if masked:
                    pt = jnp.where(col >= row, pt, 0.0)
                dv = dv + jnp.dot(pt.astype(MXU_DTYPE), doi, preferred_element_type=F32)
                dpt = lax.dot_general(vh, doi, _NT, preferred_element_type=F32)
                dst = pt * (dpt - dr_ref[0, i, hh:hh + 1, :])
                dsb = dst.astype(MXU_DTYPE)
                dk = dk + jnp.dot(dsb, qi, preferred_element_type=F32)
                df = df - jnp.sum(dst, axis=-1, keepdims=True)
                dqt_ref[0, i] += jnp.dot(kht, dsb, preferred_element_type=F32)
                dfq_ref[0, i, hh:hh + 1, :] += jnp.sum(dst, axis=0, keepdims=True)
                return dk, dv, df

            init = (jnp.zeros((blk, LANES), F32), jnp.zeros((blk, LANES), F32), jnp.zeros((blk, 1), F32))
            dk, dv, df = _grouped_steps(step, j + 1, nb - 1 - j, FOX_BWD_UNROLL, step(j, init, True))
            dks.append(dk * (1.0 / LOG2E))
            dvs.append(dv)
            df_ref[0, :, hh:hh + 1] = df
        dk_ref[...] = jnp.where(_own_lanes(lane, 0), dks[0], dks[1])
        dv_ref[...] = jnp.where(_own_lanes(lane, 0), dvs[0], dvs[1])

    bspec = pl.BlockSpec((blk, LANES), lambda h, j: (j, h))
    qspec = pl.BlockSpec((T, LANES), lambda h, j: (0, Q_TILE0 + h))
    kspec = pl.BlockSpec((blk, LANES), lambda h, j: (j, K_TILE0 + h))
    vspec = pl.BlockSpec((blk, LANES), lambda h, j: (j, V_TILE0 + h))
    dospec = pl.BlockSpec((T, LANES), lambda h, j: (0, O_TILE0 + h))
    cspec = pl.BlockSpec((1, blk, 2), lambda h, j: (h, j, 0))
    rspec = pl.BlockSpec((1, nb, 2, blk), lambda h, j: (h, 0, 0, 0))
    dqspec = pl.BlockSpec((1, nb, LANES, blk), lambda h, j: (h, 0, 0, 0))
    return call_hosting(body, hosted, name=name, grid=(FOX_PAIRS, nb),
                        in_specs=[qspec, kspec, vspec, dospec, cspec, rspec, rspec, rspec],
                        out_specs=[bspec, bspec, cspec, dqspec, rspec],
                        out_shape=[SDS((T, FOX_WIDTH), F32), SDS((T, FOX_WIDTH), F32), SDS((FOX_PAIRS, T, 2), F32),
                                   SDS((FOX_PAIRS, nb, LANES, blk), F32), SDS((FOX_PAIRS, nb, 2, blk), F32)],
                        inputs=[z, z, z, dycat, f_col, f_row, lse_row, dd_row], aliases={})


def _pairs_col(a, T):
    return jnp.transpose(a[:, :FOX_HEADS].reshape(T, FOX_PAIRS, 2), (1, 0, 2))


def _col_to_row(a, T):
    blk = _fox_block(T)
    return jnp.transpose(a.reshape(FOX_PAIRS, T // blk, blk, 2), (0, 1, 3, 2))


def _row_to_col(a, T):
    return jnp.transpose(a, (0, 1, 3, 2)).reshape(FOX_PAIRS, T, 2)


def _pairs_to_lanes(a, T):
    flat = jnp.transpose(a, (1, 0, 2)).reshape(T, FOX_HEADS)
    return jnp.pad(flat, ((0, 0), (0, LANES - FOX_HEADS)))


def _pool_counts(t0, n, w):
    t = (t0 + lax.broadcasted_iota(jnp.int32, (n, 1), 0)).astype(F32)
    return jnp.minimum(t + 1.0, float(w))


def pool_window(x, *, adjoint, name, in_col=0, into=None, out_col=0, out_dtype=F32):
    T, C = x.shape[0], len(POOL_WINDOWS) * POOL_GROUP_DIM
    tr = _pick(T, 512)
    nt = T // tr
    hb = tr // POOL_HALO
    n = tr + POOL_HALO

    def body(x_ref, h_ref, *rest):
        o_ref = rest[-1]
        i = pl.program_id(0)
        cur = x_ref[...]
        if adjoint:
            halo = h_ref[...] * jnp.where(i < nt - 1, 1.0, 0.0)
            ext = jnp.concatenate([cur, halo], axis=0)
            t0 = i * tr
        else:
            halo = h_ref[...] * jnp.where(i > 0, 1.0, 0.0)
            ext = jnp.concatenate([halo, cur], axis=0)
            t0 = i * tr - POOL_HALO
        sums = {}
        for g, w in enumerate(POOL_WINDOWS):
            ls = slice(g * POOL_GROUP_DIM, (g + 1) * POOL_GROUP_DIM)
            s = ext[:, ls]
            if adjoint:
                s = s / _pool_counts(t0, n, w)
            d = 1
            while d < w:
                s = s + pltpu.roll(s, (n - d) if adjoint else d, 0)
                d *= 2
            if adjoint:
                o_ref[:, ls] = (s[0:tr, :] - cur[:, ls]).astype(o_ref.dtype)
            else:
                o_ref[:, ls] = (s[POOL_HALO:n, :] / _pool_counts(i * tr, tr, w) - cur[:, ls]).astype(o_ref.dtype)

    if adjoint:
        halo_spec = pl.BlockSpec((POOL_HALO, C), lambda i: (jnp.minimum((i + 1) * hb, T // POOL_HALO - 1), in_col))
    else:
        halo_spec = pl.BlockSpec((POOL_HALO, C), lambda i: (jnp.maximum(i * hb - 1, 0), in_col))
    x_spec = pl.BlockSpec((tr, C), lambda i: (i, in_col))
    if into is None:
        return pl.pallas_call(body, name=name, grid=(nt,), in_specs=[x_spec, halo_spec], out_specs=_row(tr, C),
                              out_shape=SDS((T, C), out_dtype), compiler_params=_cp("parallel"))(x, x)
    return pl.pallas_call(body, name=name, grid=(nt,), in_specs=[x_spec, halo_spec, ANY],
                          out_specs=pl.BlockSpec((tr, C), lambda i: (i, out_col)), out_shape=SDS(into.shape, into.dtype),
                          input_output_aliases={2: 0}, compiler_params=_cp("parallel"))(x, x, into)


def colscale_fwd(a, s, *, out_cols, name):
    T, C = a.shape
    tr = _pick(T, 512)

    def body(a_ref, s_ref, o_ref):
        o_ref[...] = (a_ref[...] * s_ref[...]).astype(o_ref.dtype)

    return pl.pallas_call(body, name=name, grid=(T // tr,), in_specs=[_row(tr, C), _full((1, C))], out_specs=_row(tr, C),
                          out_shape=SDS((T, out_cols), MXU_DTYPE), compiler_params=_cp("parallel"))(a, s)


def colscale_bwd(a, s, dy, *, name):
    T, C = a.shape
    tr = _pick(T, 512)

    def body(a_ref, s_ref, dy_ref, da_ref, ds_ref):
        dyv = dy_ref[...]
        da_ref[...] = (dyv * s_ref[...]).astype(da_ref.dtype)

        @pl.when(pl.program_id(0) == 0)
        def _():
            ds_ref[...] = jnp.zeros_like(ds_ref)

        ds_ref[...] += jnp.sum(dyv * a_ref[...], axis=0, keepdims=True)

    return pl.pallas_call(body, name=name, grid=(T // tr,), in_specs=[_row(tr, C), _full((1, C)), _row(tr, C)],
                          out_specs=[_row(tr, C), _full((1, C))], out_shape=[SDS((T, C), MXU_DTYPE), SDS((1, C), F32)],
                          compiler_params=_cp("arbitrary"))(a, s, dy)


SGU_ROWS = 512


def _sgu_norm(v, ln_g, ln_b):
    vg = jax.nn.gelu(v)
    xc = vg - jnp.mean(vg, axis=-1, keepdims=True)
    r = lax.rsqrt(jnp.mean(xc * xc, axis=-1, keepdims=True) + EPS)
    xh = xc * r
    return xh * ln_g + ln_b, xh, r


def _rowc(tr, c, cb):
    return pl.BlockSpec((tr, c), lambda i: (i, cb))


def sgu_fwd(z, ln_g, ln_b, ws, bst, ycat, *, name):
    T, C = z.shape[0], SGU_GROUPS * SGU_GROUP_DIM
    tr = _pick(T, SGU_ROWS)

    def body(u_ref, v_ref, g_ref, b_ref, ws_ref, bst_ref, prev_ref, o_ref):
        vn, _, _ = _sgu_norm(v_ref[...], g_ref[...], b_ref[...])
        vn = vn.astype(MXU_DTYPE)
        ug = jax.nn.gelu(u_ref[...])
        for g in range(SGU_GROUPS):
            w = ws_ref[g].astype(MXU_DTYPE)
            bias = bst_ref[:, g:g + 1]
            for c in range(tr // CHUNK):
                rs = slice(c * CHUNK, (c + 1) * CHUNK)
                ls = slice(g * SGU_GROUP_DIM, (g + 1) * SGU_GROUP_DIM)
                mixed = jnp.dot(w, vn[rs, ls], preferred_element_type=F32) + bias
                o_ref[rs, ls] = (ug[rs, ls] * mixed).astype(o_ref.dtype)

    return pl.pallas_call(body, name=name, grid=(T // tr,),
                          in_specs=[_rowc(tr, C, 0), _rowc(tr, C, 1), _full((1, C)), _full((1, C)),
                                    _full((SGU_GROUPS, CHUNK, CHUNK)), _full((CHUNK, SGU_GROUPS)), ANY],
                          out_specs=_rowc(tr, C, 1), out_shape=SDS(ycat.shape, ycat.dtype), input_output_aliases={6: 0},
                          compiler_params=_cp("parallel"))(z, z, ln_g, ln_b, ws, bst, ycat)


def sgu_bwd(z, ln_g, ln_b, ws, wst, bst, dycat, *, out_cols, name):
    T, C = z.shape[0], SGU_GROUPS * SGU_GROUP_DIM
    tr = _pick(T, SGU_ROWS)

    def body(u_ref, v_ref, g_ref, b_ref, ws_ref, wst_ref, bst_ref, dy_ref,
             duv_ref, dg_ref, db_ref, dws_ref, dbst_ref, dvn_ref):
        du_ref = duv_ref.at[:, 0:C]
        dv_ref = duv_ref.at[:, C:2 * C]
        @pl.when(pl.program_id(0) == 0)
        def _():
            dg_ref[...] = jnp.zeros_like(dg_ref)
            db_ref[...] = jnp.zeros_like(db_ref)
            dws_ref[...] = jnp.zeros_like(dws_ref)
            dbst_ref[...] = jnp.zeros_like(dbst_ref)

        uv = u_ref[...]
        vv = v_ref[...]
        vn, xh, r = _sgu_norm(vv, g_ref[...], b_ref[...])
        vn = vn.astype(MXU_DTYPE)
        ug = jax.nn.gelu(uv)
        dyv = dy_ref[...]
        for g in range(SGU_GROUPS):
            w = ws_ref[g].astype(MXU_DTYPE)
            wt = wst_ref[g].astype(MXU_DTYPE)
            bias = bst_ref[:, g:g + 1]
            dw = jnp.zeros((CHUNK, CHUNK), F32)
            dbias = jnp.zeros((CHUNK, 1), F32)
            for c in range(tr // CHUNK):
                rs = slice(c * CHUNK, (c + 1) * CHUNK)
                ls = slice(g * SGU_GROUP_DIM, (g + 1) * SGU_GROUP_DIM)
                vblk = vn[rs, ls]
                mixed = jnp.dot(w, vblk, preferred_element_type=F32) + bias
                dyb = dyv[rs, ls]
                du_ref[rs, ls] = (dyb * mixed * _gelu_grad(uv[rs, ls])).astype(du_ref.dtype)
                dmixed = dyb * ug[rs, ls]
                dbias = dbias + jnp.sum(dmixed, axis=-1, keepdims=True)
                dmb = dmixed.astype(MXU_DTYPE)
                dw = dw + lax.dot_general(dmb, vblk, _NT, preferred_element_type=F32)
                dvn_ref[rs, ls] = jnp.dot(wt, dmb, preferred_element_type=F32)
            dws_ref[g] += dw
            dbst_ref[:, g:g + 1] += dbias
        dvn = dvn_ref[...]
        dg_ref[...] += jnp.sum(dvn * xh, axis=0, keepdims=True)
        db_ref[...] += jnp.sum(dvn, axis=0, keepdims=True)
        dxh = dvn * g_ref[...]
        dvg = r * (dxh - jnp.mean(dxh, axis=-1, keepdims=True) - xh * jnp.mean(dxh * xh, axis=-1, keepdims=True))
        dv_ref[...] = (dvg * _gelu_grad(vv)).astype(dv_ref.dtype)

    wspec = _full((SGU_GROUPS, CHUNK, CHUNK))
    return pl.pallas_call(body, name=name, grid=(T // tr,),
                          in_specs=[_rowc(tr, C, 0), _rowc(tr, C, 1), _full((1, C)), _full((1, C)), wspec, wspec,
                                    _full((CHUNK, SGU_GROUPS)), _rowc(tr, C, 1)],
                          out_specs=[_rowc(tr, 2 * C, 0), _full((1, C)), _full((1, C)), wspec,
                                     _full((CHUNK, SGU_GROUPS))],
                          out_shape=[SDS((T, out_cols), MXU_DTYPE), SDS((1, C), F32), SDS((1, C), F32),
                                     SDS((SGU_GROUPS, CHUNK, CHUNK), F32), SDS((CHUNK, SGU_GROUPS), F32)],
                          scratch_shapes=[pltpu.VMEM((tr, C), F32)],
                          compiler_params=_cp("arbitrary"))(z, z, ln_g, ln_b, ws, wst, bst, dycat)


def adamw(w, g, m, v, *, name):
    R, C = w.shape
    tr = _pick(R, 512)
    c1 = 1.0 - ADAM_B1 ** ADAM_STEP
    c2 = 1.0 - ADAM_B2 ** ADAM_STEP

    def body(w_ref, g_ref, m_ref, v_ref, d_ref, nm_ref, nv_ref):
        gv = g_ref[...]
        nm = ADAM_B1 * m_ref[...] + (1.0 - ADAM_B1) * gv
        nv = ADAM_B2 * v_ref[...] + (1.0 - ADAM_B2) * (gv * gv)
        nm_ref[...] = nm
        nv_ref[...] = nv
        d_ref[...] = -ADAM_LR * ((nm / c1) / (jnp.sqrt(nv / c2) + ADAM_EPS) + ADAM_WD * w_ref[...])

    spec = _row(tr, C)
    return pl.pallas_call(body, name=name, grid=(R // tr,), in_specs=[spec] * 4, out_specs=[spec] * 3,
                          out_shape=[SDS((R, C), F32)] * 3, compiler_params=_cp("parallel"))(w, g, m, v)


ANY = pl.BlockSpec(memory_space=pl.ANY)


def _coords():
    return lax.axis_index("x"), lax.axis_index("y"), lax.axis_index("c")


def _other_chips(x, y):
    return [(1 - x, y), (x, 1 - y), (1 - x, 1 - y)]


def _remote(src, dst, send_sems, recv_sems, k, dev):
    return pltpu.make_async_remote_copy(src_ref=src, dst_ref=dst, send_sem=send_sems.at[k], recv_sem=recv_sems.at[k],
                                        device_id=dev, device_id_type=MESH)


LOCAL_CHUNKS = 8


def allgather_chip_shards(shards, small, *, name):
    na = len(shards)

    def body(*refs):
        s_refs, sm_ref = refs[:na], refs[na]
        o_refs, smo_ref = refs[na + 1:2 * na + 1], refs[2 * na + 1]
        send_sems, recv_sems, local_sems = refs[2 * na + 2:]
        x, y, c = _coords()
        j = 2 * x + y
        sibling = (x, y, 1 - c)
        chips = _other_chips(x, y)
        for a in range(na):
            chunk = shards[a].shape[0] // LOCAL_CHUNKS
            for q in range(LOCAL_CHUNKS):
                rows = pl.ds(q * chunk, chunk)
                pltpu.make_async_copy(s_refs[a].at[rows], o_refs[a].at[j, rows], local_sems.at[a]).start()
        pltpu.make_async_copy(sm_ref, smo_ref.at[j], local_sems.at[na]).start()
        sends = []
        for a in range(na):
            half = shards[a].shape[0] // 2
            mine = pl.ds(c * half, half)
            for k, (px, py) in enumerate(chips):
                sends.append(_remote(s_refs[a].at[mine], o_refs[a].at[j, mine], send_sems, recv_sems, 6 * a + k, (px, py, c)))
        for k, (px, py) in enumerate(chips):
            sends.append(_remote(sm_ref, smo_ref.at[j], send_sems, recv_sems, 6 * na + k, (px, py, c)))
        for cp in sends:
            cp.start()
        for a in range(na):
            half = shards[a].shape[0] // 2
            mine = pl.ds(c * half, half)
            for k, (px, py) in enumerate(chips):
                rows = o_refs[a].at[2 * px + py, mine]
                _remote(rows, rows, send_sems, recv_sems, 6 * a + k, (px, py, c)).wait_recv()
                fw = _remote(rows, rows, send_sems, recv_sems, 6 * a + 3 + k, sibling)
                fw.start()
                sends.append(fw)
        for a in range(na):
            half = shards[a].shape[0] // 2
            theirs = pl.ds((1 - c) * half, half)
            for k, (px, py) in enumerate(chips):
                rows = o_refs[a].at[2 * px + py, theirs]
                _remote(rows, rows, send_sems, recv_sems, 6 * a + 3 + k, sibling).wait_recv()
        for k, (px, py) in enumerate(chips):
            slot = smo_ref.at[2 * px + py]
            _remote(slot, slot, send_sems, recv_sems, 6 * na + k, (px, py, c)).wait_recv()
        for cp in sends:
            cp.wait_send()
        for a in range(na):
            pltpu.make_async_copy(s_refs[a], o_refs[a].at[j], local_sems.at[a]).wait()
        pltpu.make_async_copy(sm_ref, smo_ref.at[j], local_sems.at[na]).wait()

    nsem = 6 * na + 3
    outs = pl.pallas_call(
        body, name=name, in_specs=[ANY] * (na + 1), out_specs=[ANY] * (na + 1),
        out_shape=[SDS((N_CHIPS,) + s.shape, s.dtype) for s in shards] + [SDS((N_CHIPS,) + small.shape, small.dtype)],
        scratch_shapes=[pltpu.SemaphoreType.DMA((nsem,)), pltpu.SemaphoreType.DMA((nsem,)),
                        pltpu.SemaphoreType.DMA((na + 1,))])(*shards, small)
    return outs[:na], outs[na]


class Exchange:
    def __init__(self, ins, out_shapes, scratch, start, wait):
        self.ins, self.out_shapes, self.scratch, self.start, self.wait = list(ins), list(out_shapes), list(scratch), start, wait


def run_exchange(ex, *, name):
    ni, no = len(ex.ins), len(ex.out_shapes)

    def body(*refs):
        parts = refs[:ni], refs[ni:ni + no], refs[ni + no:]
        ex.start(*parts)
        ex.wait(*parts)

    return pl.pallas_call(body, name=name, in_specs=[ANY] * ni, out_specs=[ANY] * no, out_shape=ex.out_shapes,
                          scratch_shapes=ex.scratch)(*ex.ins)


def call_hosting(body, ex, *, name, grid, in_specs, out_specs, out_shape, inputs, aliases, scratch=()):
    n_in, n_out, ni, no, ns = len(inputs), len(out_shape), len(ex.ins), len(ex.out_shapes), len(scratch)
    outs_at = n_in + ni
    scr_at = outs_at + n_out + no

    def wrapped(*refs):
        own = refs[:n_in] + refs[outs_at:outs_at + n_out] + refs[scr_at:scr_at + ns]
        parts = refs[n_in:outs_at], refs[outs_at + n_out:scr_at], refs[scr_at + ns:]
        ids = [pl.program_id(d) for d in range(len(grid))]
        first = functools.reduce(jnp.logical_and, [i == 0 for i in ids])
        last = functools.reduce(jnp.logical_and, [i == g - 1 for i, g in zip(ids, grid)])

        @pl.when(first)
        def _():
            ex.start(*parts)

        body(*own)

        @pl.when(last)
        def _():
            ex.wait(*parts)

    outs = pl.pallas_call(
        wrapped, name=name, grid=grid, in_specs=list(in_specs) + [ANY] * ni, out_specs=list(out_specs) + [ANY] * no,
        out_shape=list(out_shape) + ex.out_shapes, input_output_aliases=aliases,
        scratch_shapes=list(scratch) + ex.scratch,
        compiler_params=_cp(*["arbitrary"] * len(grid)))(*inputs, *ex.ins)
    return outs[:n_out], outs[n_out:]


def allgather_ici_exchange(shards):
    na = len(shards)

    def copies(s_refs, o_refs, sems):
        send_sems, recv_sems, _ = sems
        x, y, c = _coords()
        j = 2 * x + y
        out = []
        for a in range(na):
            half = shards[a].shape[0] // 2
            mine = pl.ds(c * half, half)
            for k, (px, py) in enumerate(_other_chips(x, y)):
                send = _remote(s_refs[a].at[mine], o_refs[a].at[j, mine], send_sems, recv_sems, 3 * a + k, (px, py, c))
                rows = o_refs[a].at[2 * px + py, mine]
                out.append((send, _remote(rows, rows, send_sems, recv_sems, 3 * a + k, (px, py, c))))
        return out

    def start(s_refs, o_refs, sems):
        x, y, c = _coords()
        j = 2 * x + y
        for a in range(na):
            chunk = shards[a].shape[0] // LOCAL_CHUNKS
            for q in range(LOCAL_CHUNKS):
                rows = pl.ds(q * chunk, chunk)
                pltpu.make_async_copy(s_refs[a].at[rows], o_refs[a].at[j, rows], sems[2].at[a]).start()
        for send, _ in copies(s_refs, o_refs, sems):
            send.start()

    def wait(s_refs, o_refs, sems):
        x, y, c = _coords()
        j = 2 * x + y
        for send, arrival in copies(s_refs, o_refs, sems):
            arrival.wait_recv()
            send.wait_send()
        for a in range(na):
            pltpu.make_async_copy(s_refs[a], o_refs[a].at[j], sems[2].at[a]).wait()

    return Exchange(shards, [SDS((N_CHIPS,) + s.shape, s.dtype) for s in shards],
                    [pltpu.SemaphoreType.DMA((3 * na,)), pltpu.SemaphoreType.DMA((3 * na,)), pltpu.SemaphoreType.DMA((na,))],
                    start, wait)


def allgather_forward(gathered, *, name):
    na = len(gathered)

    def body(*refs):
        o_refs = refs[na:2 * na]
        send_sems, recv_sems = refs[2 * na:]
        x, y, c = _coords()
        sibling = (x, y, 1 - c)
        cps = []
        for a in range(na):
            half = gathered[a].shape[1] // 2
            for k, (px, py) in enumerate(_other_chips(x, y)):
                mine = o_refs[a].at[2 * px + py, pl.ds(c * half, half)]
                theirs = o_refs[a].at[2 * px + py, pl.ds((1 - c) * half, half)]
                cps.append((_remote(mine, mine, send_sems, recv_sems, 3 * a + k, sibling),
                            _remote(theirs, theirs, send_sems, recv_sems, 3 * a + k, sibling)))
        for send, _ in cps:
            send.start()
        for send, arrival in cps:
            send.wait_send()
            arrival.wait_recv()

    return pl.pallas_call(body, name=name, in_specs=[ANY] * na, out_specs=[ANY] * na,
                          out_shape=[SDS(g.shape, g.dtype) for g in gathered],
                          input_output_aliases={a: a for a in range(na)},
                          scratch_shapes=[pltpu.SemaphoreType.DMA((3 * na,)), pltpu.SemaphoreType.DMA((3 * na,))])(*gathered)


def swap_halves_exchange(gs):
    na = len(gs)

    def copies(g_refs, o_refs, sems):
        x, y, c = _coords()
        out = []
        for a in range(na):
            half = gs[a].shape[1] // 2
            out.append(_remote(g_refs[a].at[:, pl.ds((1 - c) * half, half), :], o_refs[a], sems[0], sems[1], a,
                               (x, y, 1 - c)))
        return out

    def start(g_refs, o_refs, sems):
        for cp in copies(g_refs, o_refs, sems):
            cp.start()

    def wait(g_refs, o_refs, sems):
        for cp in copies(g_refs, o_refs, sems):
            cp.wait()

    return Exchange(gs, [SDS((g.shape[0], g.shape[1] // 2, g.shape[2]), g.dtype) for g in gs],
                    [pltpu.SemaphoreType.DMA((na,)), pltpu.SemaphoreType.DMA((na,))], start, wait)


def chip_partials_exchange(pbs):
    na = len(pbs)

    def copies(p_refs, o_refs, sems):
        x, y, c = _coords()
        out = []
        for a in range(na):
            for k, (px, py) in enumerate(_other_chips(x, y)):
                out.append(_remote(p_refs[a].at[2 * px + py], o_refs[a].at[k], sems[0], sems[1], 3 * a + k, (px, py, c)))
        return out

    def start(p_refs, o_refs, sems):
        for cp in copies(p_refs, o_refs, sems):
            cp.start()

    def wait(p_refs, o_refs, sems):
        for cp in copies(p_refs, o_refs, sems):
            cp.wait()

    return Exchange(pbs, [SDS((3,) + p.shape[1:], p.dtype) for p in pbs],
                    [pltpu.SemaphoreType.DMA((3 * na,)), pltpu.SemaphoreType.DMA((3 * na,))], start, wait)


def add_sibling_half(g, land, c_idx, *, name):
    n, R, C = g.shape
    half = R // 2
    tr = _pick(half, 256)
    nt = half // tr

    def body(c_ref, g_ref, l_ref, of_ref, ob_ref):
        s = g_ref[...] + l_ref[...].astype(F32)
        of_ref[...] = s
        ob_ref[...] = s.astype(ob_ref.dtype)

    blk = pl.BlockSpec((1, tr, C), lambda s, i, c_ref: (s, i, 0))
    gblk = pl.BlockSpec((1, tr, C), lambda s, i, c_ref: (s, c_ref[0] * nt + i, 0))
    return pl.pallas_call(
        body, name=name,
        grid_spec=pltpu.PrefetchScalarGridSpec(num_scalar_prefetch=1, grid=(n, nt), in_specs=[gblk, blk],
                                               out_specs=[blk, blk]),
        out_shape=[SDS((n, half, C), F32), SDS((n, half, C), WIRE_DTYPE)],
        compiler_params=_cp("parallel", "parallel"))(c_idx, g, land)


def add_chip_partials(pf, rb, jc_idx, *, name):
    n, H, C = pf.shape
    tr = _pick(H, 256)

    def body(jc_ref, p_ref, r_ref, o_ref):
        s = p_ref[0]
        for k in range(3):
            s = s + r_ref[k].astype(F32)
        o_ref[...] = s

    pblk = pl.BlockSpec((1, tr, C), lambda i, jc_ref: (jc_ref[0], i, 0))
    rblk = pl.BlockSpec((3, tr, C), lambda i, jc_ref: (0, i, 0))
    oblk = pl.BlockSpec((None, tr, C), lambda i, jc_ref: (jc_ref[1], i, 0))
    return pl.pallas_call(
        body, name=name,
        grid_spec=pltpu.PrefetchScalarGridSpec(num_scalar_prefetch=1, grid=(H // tr,), in_specs=[pblk, rblk],
                                               out_specs=oblk),
        out_shape=SDS((2, H, C), F32), compiler_params=_cp("parallel"))(jc_idx, pf, rb)


def join_sibling_halves(bufs, *, name):
    na = len(bufs)

    def body(*refs):
        o_refs = refs[na:2 * na]
        send_sems, recv_sems = refs[2 * na:]
        x, y, c = _coords()
        cps = [_remote(o_refs[a].at[c], o_refs[a].at[c], send_sems, recv_sems, a, (x, y, 1 - c)) for a in range(na)]
        for cp in cps:
            cp.start()
        for a in range(na):
            cps[a].wait_send()
            _remote(o_refs[a].at[1 - c], o_refs[a].at[1 - c], send_sems, recv_sems, a, (x, y, 1 - c)).wait_recv()

    return pl.pallas_call(body, name=name, in_specs=[ANY] * na, out_specs=[ANY] * na,
                          out_shape=[SDS(b.shape, b.dtype) for b in bufs],
                          input_output_aliases={a: a for a in range(na)},
                          scratch_shapes=[pltpu.SemaphoreType.DMA((na,)), pltpu.SemaphoreType.DMA((na,))])(*bufs)


def exchange_pieces(v, *, scatter, name):
    P, C = v.shape[-2:]

    def body(v_ref, o_ref, send_sems, recv_sems, local_sem):
        x, y, c = _coords()
        me = 4 * x + 2 * y + c
        local = pltpu.make_async_copy(v_ref.at[me] if scatter else v_ref, o_ref.at[me], local_sem)
        local.start()
        cps = []
        for m in range(1, N_DEV):
            px = (1 - x) if m & 4 else x
            py = (1 - y) if m & 2 else y
            pc = (1 - c) if m & 1 else c
            src = v_ref.at[4 * px + 2 * py + pc] if scatter else v_ref
            cps.append(_remote(src, o_ref.at[me], send_sems, recv_sems, m - 1, (px, py, pc)))
        for cp in cps:
            cp.start()
        for cp in cps:
            cp.wait_send()
        for m in range(1, N_DEV):
            px = (1 - x) if m & 4 else x
            py = (1 - y) if m & 2 else y
            pc = (1 - c) if m & 1 else c
            slot = o_ref.at[4 * px + 2 * py + pc]
            _remote(slot, slot, send_sems, recv_sems, m - 1, (px, py, pc)).wait_recv()
        local.wait()

    return pl.pallas_call(body, name=name, in_specs=[ANY], out_specs=ANY, out_shape=SDS((N_DEV, P, C), v.dtype),
                          scratch_shapes=[pltpu.SemaphoreType.DMA((N_DEV - 1,)), pltpu.SemaphoreType.DMA((N_DEV - 1,)),
                                          pltpu.SemaphoreType.DMA(())])(v)


def sum_pieces(land, *, name):
    n, P, C = land.shape

    def body(l_ref, o_ref):
        s = l_ref[0]
        for d in range(1, n):
            s = s + l_ref[d]
        o_ref[...] = s

    return pl.pallas_call(body, name=name, out_shape=SDS((P, C), F32))(land)


BIG_SEGS = (
    ("w_in_even", (1024, 514), 1),
    ("s5_w_glu", (128, 512), 0),
    ("w_out_even", (256, 1024), 0),
    ("w_in_odd", (1024, 384), 1),
    ("w_out_odd", (256, 1024), 0),
    ("mlp_w1", (2, 1024, 1024), 2),
    ("mlp_w2", (2, 1024, 1024), 1),
)
BIG_NAMES = tuple(n for n, _, _ in BIG_SEGS)
EARLY_NAMES = ("w_in_even", "s5_w_glu")
LATE_NAMES = ("w_out_even", "w_in_odd", "w_out_odd", "mlp_w1", "mlp_w2")
REDUCED_EARLY = ("s5_w_glu", "w_out_even", "w_in_odd", "w_out_odd", "mlp_w1", "mlp_w2")
SHARDED_SMALL = ("pool_scale", "sgu_ln_g", "sgu_ln_b")
SMALL_SEGS = (
    ("mix_pre_g", (2, 1024)), ("mix_post_g", (2, 1024)), ("mlp_pre_g", (2, 1024)), ("mlp_post_g", (2, 1024)),
    ("s5_lam_re", (1, 32, 64)), ("s5_lam_im", (1, 32, 64)), ("s5_log_dt", (1, 32)),
    ("s5_b_re", (1, 32, 64, 16)), ("s5_b_im", (1, 32, 64, 16)), ("s5_c_re", (1, 32, 16, 64)), ("s5_c_im", (1, 32, 16, 64)),
    ("s5_d", (1, 512)), ("fox_b_f", (1, 8)), ("pool_w", (1, 4, 128, 128)), ("sgu_w_s", (1, 4, 128, 128)),
    ("sgu_b_s", (1, 4, 128)),
)
REDUCED_SEGS = SMALL_SEGS + tuple((n, (1, 512)) for n in SHARDED_SMALL) + (("loss", (1, 1)),)


def _cols_from_chips(g):
    n, R, C = g.shape
    return jnp.transpose(g, (1, 0, 2)).reshape(R, n * C)


def _chips_from_cols(m):
    R, C4 = m.shape
    return jnp.transpose(m.reshape(R, N_CHIPS, C4 // N_CHIPS), (1, 0, 2))


MLP_SHARD = 1024


def _w1_cols(l):
    def spec(tm, tn, tk):
        per = MLP_SHARD // tn
        return pl.BlockSpec((None, tk, tn), lambda i, j, k: (j // per, l * (MLP_SHARD // tk) + k, j % per))
    return spec


def _w1_rows_t(l):
    def spec(tm, tn, tk):
        if tk == N_CHIPS * MLP_SHARD:
            return pl.BlockSpec((N_CHIPS, tn, MLP_SHARD), lambda i, j, k: (0, l * (MLP_SHARD // tn) + j, 0))
        per = MLP_SHARD // tk
        return pl.BlockSpec((None, tn, tk), lambda i, j, k: (k // per, l * (MLP_SHARD // tn) + j, k % per))
    return spec


def _w2_rows(l):
    def spec(tm, tn, tk):
        if tk == N_CHIPS * MLP_SHARD:
            return pl.BlockSpec((N_CHIPS, MLP_SHARD, tn), lambda i, j, k: (0, l, j))
        per = MLP_SHARD // tk
        return pl.BlockSpec((None, tk, tn), lambda i, j, k: (k // per, l * per + k % per, j))
    return spec


def _w2_rows_t(l):
    def spec(tm, tn, tk):
        per = MLP_SHARD // tn
        return pl.BlockSpec((None, tn, tk), lambda i, j, k: (j // per, l * per + j % per, k))
    return spec


def _dw1_out(l):
    def spec(tm, tn, tk):
        per = MLP_SHARD // tn
        return pl.BlockSpec((None, tm, tn), lambda i, j, k: (j // per, l * (MLP_SHARD // tm) + i, j % per))
    return spec


def _dw2_out(l):
    def spec(tm, tn, tk):
        per = MLP_SHARD // tm
        return pl.BlockSpec((None, tm, tn), lambda i, j, k: (i // per, l * per + i % per, j))
    return spec


def _pack_vec(d, segs, rows_multiple):
    flat = jnp.concatenate([d[n].reshape(-1) for n, _ in segs])
    rows = -(-flat.shape[0] // LANES)
    rows = -(-rows // rows_multiple) * rows_multiple
    return jnp.pad(flat, (0, rows * LANES - flat.shape[0])).reshape(rows, LANES)


def _unpack_vec(v, segs):
    flat, out, r = v.reshape(-1), {}, 0
    for n, shape in segs:
        k = math.prod(shape)
        out[n] = flat[r:r + k].reshape(shape)
        r += k
    return out


def _block_diag(blocks):
    G, a, b = blocks.shape
    eye = jnp.eye(G, dtype=blocks.dtype)
    return (eye[:, None, :, None] * blocks[:, :, None, :]).reshape(G * a, G * b)


def _diag_blocks(m, G):
    a, b = m.shape[0] // G, m.shape[1] // G
    return jnp.stack([m[g * a:(g + 1) * a, g * b:(g + 1) * b] for g in range(G)])


def _sqrelu_epi(acc):
    r = jnp.maximum(acc, 0.0)
    return (r * r,)


def _sqrelu_bwd_epi(acc, s):
    return (acc * (2.0 * jnp.sqrt(s.astype(F32))),)


def _mlp_fwd(h, g1, g2, l, tag):
    T, D = h.shape
    s = matmul(h, g1, name=f"{tag}_up", mnk=(T, D_FF, D), b_spec=_w1_cols(l), epi=_sqrelu_epi, out_dtype=MXU_DTYPE)
    m = matmul(s, g2, name=f"{tag}_down", mnk=(T, D, D_FF), b_spec=_w2_rows(l))
    return m, (h, s)


def _mlp_bwd(saved, dm, g1, g2, l, dg1, dg2, tag):
    h, s = saved
    T, D = h.shape
    gshape = (N_CHIPS, 2 * MLP_SHARD, MLP_SHARD)
    da = matmul(dm, g2, tb=True, name=f"{tag}_down_dx", mnk=(T, D_FF, D), b_spec=_w2_rows_t(l),
                epi=_sqrelu_bwd_epi, epi_in=(s,), out_dtype=MXU_DTYPE)
    dg2 = matmul(s, dm, ta=True, name=f"{tag}_down_dw", tm=MLP_SHARD, o_spec=_dw2_out(l), o_shape=gshape, prev=dg2)
    dh = matmul(da, g1, tb=True, name=f"{tag}_up_dx", mnk=(T, D, D_FF), b_spec=_w1_rows_t(l))
    dg1 = matmul(h, da, ta=True, name=f"{tag}_up_dw", o_spec=_dw1_out(l), o_shape=gshape, prev=dg1)
    return dh, dg1, dg2


def kernel(x, mix_pre_g, mix_post_g, mlp_pre_g, mlp_post_g, w_in_even, s5_lam_re, s5_lam_im, s5_log_dt, s5_b_re, s5_b_im, s5_c_re, s5_c_im, s5_d, s5_w_glu, fox_b_f, w_out_even, w_in_odd, pool_w, pool_scale, sgu_ln_g, sgu_ln_b, sgu_w_s, sgu_b_s, w_out_odd, mlp_w1, mlp_w2, loss_target, m_mix_pre_g, m_mix_post_g, m_mlp_pre_g, m_mlp_post_g, m_w_in_even, m_s5_lam_re, m_s5_lam_im, m_s5_log_dt, m_s5_b_re, m_s5_b_im, m_s5_c_re, m_s5_c_im, m_s5_d, m_s5_w_glu, m_fox_b_f, m_w_out_even, m_w_in_odd, m_pool_w, m_pool_scale, m_sgu_ln_g, m_sgu_ln_b, m_sgu_w_s, m_sgu_b_s, m_w_out_odd, m_mlp_w1, m_mlp_w2, v_mix_pre_g, v_mix_post_g, v_mlp_pre_g, v_mlp_post_g, v_w_in_even, v_s5_lam_re, v_s5_lam_im, v_s5_log_dt, v_s5_b_re, v_s5_b_im, v_s5_c_re, v_s5_c_im, v_s5_d, v_s5_w_glu, v_fox_b_f, v_w_out_even, v_w_in_odd, v_pool_w, v_pool_scale, v_sgu_ln_g, v_sgu_ln_b, v_sgu_w_s, v_sgu_b_s, v_w_out_odd, v_mlp_w1, v_mlp_w2):
    names = [n for n, _ in SMALL_SEGS] + [n for n, _, _ in BIG_SEGS] + list(SHARDED_SMALL)
    env = dict(locals())
    W = {n: env[n] for n in names}
    M = {n: env["m_" + n] for n in names}
    V = {n: env["v_" + n] for n in names}

    def shard(n):
        return W[n].reshape(-1, W[n].shape[-1]).astype(WIRE_DTYPE)

    small = jnp.pad(jnp.concatenate([W[n] for n in SHARDED_SMALL]), ((0, SUBLANES - len(SHARDED_SMALL)), (0, 0)))
    gathered, small_all = allgather_chip_shards([shard(n) for n in EARLY_NAMES], small, name="allgather_weights")
    Wf = dict(zip(EARLY_NAMES, gathered))
    for i, n in enumerate(SHARDED_SMALL):
        Wf[n] = small_all[:, i, :].reshape(1, N_CHIPS * LANES)
    for n, _ in SMALL_SEGS:
        Wf[n] = W[n]

    loss8, dx0, halves, local_small = _local_step(x[0], loss_target[0], Wf, [shard(n) for n in LATE_NAMES])
    return _reduce_and_update(W, M, V, loss8, dx0, halves, local_small)


def _reduce_to_my_half(gs, names, tag, carry_swap=None, carry_ici=None):
    cx, cy, cc = _coords()
    c_idx = cc.reshape(1).astype(jnp.int32)
    jc_idx = jnp.stack([2 * cx + cy, cc]).astype(jnp.int32)
    swap = swap_halves_exchange(gs)
    from_sibling = carry_swap(swap) if carry_swap else run_exchange(swap, name=f"{tag}_to_sibling")
    sums = [add_sibling_half(g, l, c_idx, name=f"{tag}_chip_sum_{n}") for n, g, l in zip(names, gs, from_sibling)]
    send = chip_partials_exchange([pb for _, pb in sums])
    from_chips = carry_ici(send) if carry_ici else run_exchange(send, name=f"{tag}_to_chips")
    return [add_chip_partials(pf, r, jc_idx, name=f"{tag}_sum_{n}") for n, (pf, _), r in zip(names, sums, from_chips)]


def _local_step(x0, target, P, late_shards):
    T = x0.shape[0]
    mix_pre_g, mix_post_g, mlp_pre_g, mlp_post_g = P["mix_pre_g"], P["mix_post_g"], P["mlp_pre_g"], P["mlp_post_g"]
    s5_lam_re, s5_lam_im, s5_log_dt = P["s5_lam_re"], P["s5_lam_im"], P["s5_log_dt"]
    s5_b_re, s5_b_im, s5_c_re, s5_c_im, s5_d = P["s5_b_re"], P["s5_b_im"], P["s5_c_re"], P["s5_c_im"], P["s5_d"]
    fox_b_f, pool_w, sgu_w_s, sgu_b_s = P["fox_b_f"], P["pool_w"], P["sgu_w_s"], P["sgu_b_s"]
    pool_scale_f, ln_g_f, ln_b_f = P["pool_scale"], P["sgu_ln_g"], P["sgu_ln_b"]
    w_in_e = jnp.pad(_cols_from_chips(P["w_in_even"]), ((0, 0), (0, EVEN_IN_PAD - EVEN_IN)))
    w_glu = P["s5_w_glu"].reshape(S5_WIDTH, S5_WIDTH)

    def gain(a, l):
        return a[l][None, :]

    lr = s5_lam_re[0].reshape(1, S5_LANES)
    li = s5_lam_im[0].reshape(1, S5_LANES)
    ldt = jnp.repeat(s5_log_dt[0], S5_STATE).reshape(1, S5_LANES)
    btr = s5_b_re[0].reshape(S5_LANES, S5_GROUP).T
    bti = s5_b_im[0].reshape(S5_LANES, S5_GROUP).T
    tf_re, tf_im, tb_re, tb_im, bbt_re, bbt_im = s5_disc_fwd(lr, li, ldt, btr, bti, name="s5_disc")
    same_group = (jnp.arange(S5_WIDTH)[:, None] // S5_GROUP) == (jnp.arange(S5_LANES)[None, :] // S5_STATE)
    b_bd = s5_interleave(jnp.where(same_group, jnp.tile(bbt_re, (S5_GROUPS, 1)), 0.0),
                         jnp.where(same_group, jnp.tile(bbt_im, (S5_GROUPS, 1)), 0.0), axis=1)
    cr2 = jnp.transpose(s5_c_re[0], (0, 2, 1)).reshape(S5_LANES, S5_GROUP)
    ci2 = jnp.transpose(s5_c_im[0], (0, 2, 1)).reshape(S5_LANES, S5_GROUP)
    c_bd = s5_interleave(jnp.where(same_group.T, jnp.tile(cr2, (1, S5_GROUPS)), 0.0),
                         -jnp.where(same_group.T, jnp.tile(ci2, (1, S5_GROUPS)), 0.0), axis=0)
    bf_pad = jnp.pad(fox_b_f, ((0, 0), (0, LANES - FOX_HEADS)))

    h1 = rms_fwd(x0, gain(mix_pre_g, 0), name="l0_pre_norm")
    z = matmul(h1, w_in_e, name="l0_in_proj")
    s5_tiles = dict(tm=_pick(T, S5_NB), exact_tiles=True)
    xs = s5_scan(z, b_bd, tf_re, tf_im, reverse=False, name="s5_scan_fwd")
    yc = matmul(xs, c_bd, mnk=(T, S5_WIDTH, 2 * S5_NB), tn=S5_CB, a_spec=_lanes_of_chan, b_spec=_s5_c_block,
                name="s5_cx", **s5_tiles)
    yl, yg = s5_out_fwd(yc, z, s5_d, name="s5_out")
    gl = matmul(yg, w_glu, name="s5_glu_proj")
    ycat = glu_fwd(yg, gl, out_cols=D_MODEL, name="s5_glu")
    fgate = fox_gate_fwd(z, bf_pad, fl_col=FL_TILE, name="fox_gate")
    f_col = _pairs_col(fgate, T)
    f_row = _col_to_row(f_col, T)
    (ycat, lse_col), late = fox_fwd(z, f_col, f_row, ycat, allgather_ici_exchange(late_shards), name="fox_fwd")
    late = dict(zip(LATE_NAMES, allgather_forward(late, name="allgather_late_weights")))
    w_in_o = _cols_from_chips(late["w_in_odd"])
    w_in_o = jnp.concatenate([w_in_o[:, S5_WIDTH:], w_in_o[:, :S5_WIDTH]], axis=1)
    w_out_e = late["w_out_even"].reshape(D_MODEL, D_MODEL)
    w_out_o = late["w_out_odd"].reshape(D_MODEL, D_MODEL)
    g1, g2 = late["mlp_w1"], late["mlp_w2"]
    mo = matmul(ycat, w_out_e, name="l0_out_proj")
    x1, h2 = res_norm_fwd(x0, mo, gain(mix_post_g, 0), gain(mlp_pre_g, 0), name="l0_post_mlp0_pre_norm")
    m0, mlp0 = _mlp_fwd(h2, g1, g2, 0, "mlp0")

    x2, h3 = res_norm_fwd(x1, m0, gain(mlp_post_g, 0), gain(mix_pre_g, 1), name="mlp0_post_l1_pre_norm")
    z2 = matmul(h3, w_in_o, name="l1_in_proj")
    pooled = pool_window(z2, adjoint=False, in_col=POOL_COL, out_dtype=MXU_DTYPE, name="pool_fwd")
    pw_bd = _block_diag(pool_w[0])
    pw = matmul(pooled, pw_bd, name="pool_proj")
    ycat2 = colscale_fwd(pw, pool_scale_f, out_cols=D_MODEL, name="pool_scale")
    causal = jnp.tril(jnp.ones((CHUNK, CHUNK), dtype=bool))
    wsm = jnp.where(causal[None], sgu_w_s[0], 0.0)
    wsmt = jnp.transpose(wsm, (0, 2, 1))
    bst = sgu_b_s[0].T
    ycat2 = sgu_fwd(z2, ln_g_f, ln_b_f, wsm, bst, ycat2, name="sgu_fwd")
    mo2 = matmul(ycat2, w_out_o, name="l1_out_proj")
    x3, h4 = res_norm_fwd(x2, mo2, gain(mix_post_g, 1), gain(mlp_pre_g, 1), name="l1_post_mlp1_pre_norm")
    m1, mlp1 = _mlp_fwd(h4, g1, g2, 1, "mlp1")
    loss8, dx4 = res_norm_loss(x3, m1, gain(mlp_post_g, 1), target, name="mlp1_post_norm_loss")

    dm1, dg_mlp_post1 = rms_bwd(m1, gain(mlp_post_g, 1), dx4, None, name="mlp1_post_norm_bwd")
    dh4, dg1, dg2 = _mlp_bwd(mlp1, dm1, g1, g2, 1, None, None, "mlp1")
    dx3, dmo2, dg_mlp_pre1, dg_mix_post1 = norm_res_bwd(x3, gain(mlp_pre_g, 1), dh4, dx4, mo2, gain(mix_post_g, 1),
                                                        name="mlp1_pre_l1_post_norm_bwd")
    dycat2 = matmul(dmo2, w_out_o, tb=True, name="l1_out_proj_dx")
    dw_out_o = matmul(ycat2, dmo2, ta=True, name="l1_out_proj_dw")
    dpw, dpool_scale = colscale_bwd(pw, pool_scale_f, dycat2, name="pool_scale_bwd")
    dpooled = matmul(dpw, pw_bd, tb=True, name="pool_proj_dx")
    dpw_bd = matmul(pooled, dpw, ta=True, name="pool_proj_dw")
    dz2, dln_g, dln_b, dws, dbst = sgu_bwd(z2, ln_g_f, ln_b_f, wsm, wsmt, bst, dycat2, out_cols=3 * S5_WIDTH,
                                           name="sgu_bwd")
    dz2 = pool_window(dpooled, adjoint=True, into=dz2, out_col=POOL_COL, name="pool_bwd")
    dh3 = matmul(dz2, w_in_o, tb=True, name="l1_in_proj_dx")
    dw_in_o = matmul(h3, dz2, ta=True, name="l1_in_proj_dw")
    dw_in_o = jnp.concatenate([dw_in_o[:, 2 * S5_WIDTH:], dw_in_o[:, :2 * S5_WIDTH]], axis=1)
    dx2, dm0, dg_mix_pre1, dg_mlp_post0 = norm_res_bwd(x2, gain(mix_pre_g, 1), dh3, dx3, m0, gain(mlp_post_g, 0),
                                                       name="l1_pre_mlp0_post_norm_bwd")

    dh2, dg1, dg2 = _mlp_bwd(mlp0, dm0, g1, g2, 0, dg1, dg2, "mlp0")
    dx1, dmo, dg_mlp_pre0, dg_mix_post0 = norm_res_bwd(x1, gain(mlp_pre_g, 0), dh2, dx2, mo, gain(mix_post_g, 0),
                                                       name="mlp0_pre_l0_post_norm_bwd")
    dycat = matmul(dmo, w_out_e, tb=True, name="l0_out_proj_dx")
    dw_out_e = matmul(ycat, dmo, ta=True, name="l0_out_proj_dw")
    dyg_a, dgl = glu_bwd(yg, gl, dycat, name="s5_glu_bwd")
    dyg_b = matmul(dgl, w_glu, tb=True, name="s5_glu_proj_dx")
    dw_glu = matmul(yg, dgl, ta=True, name="s5_glu_proj_dw")
    dyl, du_skip, dd = s5_out_bwd(yl, z, s5_d, dyg_a, dyg_b, name="s5_out_bwd")
    dc_blocks = matmul(xs, dyl, ta=True, mnk=(2 * S5_LANES, S5_CB, T), tm=S5_NB, tn=S5_CB, b_spec=_chan_cols_of_i,
                       exact_tiles=True, name="s5_cx_dw")
    early_grads = {"s5_w_glu": dw_glu.reshape(N_CHIPS, -1, S5_WIDTH), "w_out_even": dw_out_e.reshape(N_CHIPS, -1, D_MODEL),
                   "w_in_odd": _chips_from_cols(dw_in_o), "w_out_odd": dw_out_o.reshape(N_CHIPS, -1, D_MODEL),
                   "mlp_w1": dg1, "mlp_w2": dg2}
    got = {}

    def reverse_scan(exchange):
        (got["lam"], got["dab_re"], got["dab_im"]), bufs = s5_scan(dyl, c_bd, tb_re, tb_im, reverse=True, states=xs,
                                                                   hosted=exchange, name="s5_scan_bwd")
        return bufs

    def attention_bwd(exchange):
        dd_col = fox_dd(ycat, dycat, name="fox_dd")
        (got["dk"], got["dv"], got["dfk"], got["dqt"], got["dfq"]), bufs = fox_bwd(
            z, dycat, f_col, f_row, _col_to_row(lse_col, T), _col_to_row(dd_col, T), exchange, name="fox_bwd")
        return bufs

    halves = _reduce_to_my_half([early_grads[n] for n in REDUCED_EARLY], REDUCED_EARLY, "early_grads",
                                reverse_scan, attention_bwd)
    lam, dab_re, dab_im, dk, dv = got["lam"], got["dab_re"], got["dab_im"], got["dk"], got["dv"]
    db_blocks = matmul(z, lam, ta=True, mnk=(S5_CB, 2 * S5_LANES, T), tm=S5_CB, tn=S5_NB, a_spec=_chan_rows_t,
                       exact_tiles=True, name="s5_bu_dw")
    du_b = matmul(lam, b_bd, tb=True, mnk=(T, S5_WIDTH, 2 * S5_NB), tn=S5_CB, a_spec=_lanes_of_chan,
                  b_spec=_s5_b_block_t, name="s5_bu_dx", **s5_tiles)
    du = add2(du_skip, du_b, name="s5_du")
    dq = jnp.transpose(got["dqt"], (1, 3, 0, 2)).reshape(T, FOX_WIDTH) * (FOX_HEAD_DIM ** -0.5)
    dfl, dbf = fox_gate_bwd(z, bf_pad, _pairs_to_lanes(got["dfk"], T), _pairs_to_lanes(_row_to_col(got["dfq"], T), T),
                            fl_col=FL_TILE, name="fox_gate_bwd")
    dz = jnp.concatenate([du, dq, dk, dv, dfl], axis=1).astype(MXU_DTYPE)
    dw_in_e = matmul(h1, dz, ta=True, name="l0_in_proj_dw")[:, :EVEN_IN]

    def in_proj_dx(exchange):
        got["dh1"], bufs = matmul(dz, w_in_e, tb=True, hosted=exchange, name="l0_in_proj_dx")
        return bufs

    def pre_norm_bwd(exchange):
        (got["dx0"], got["dg_mix_pre0"]), bufs = rms_bwd(x0, gain(mix_pre_g, 0), got["dh1"], dx1, hosted=exchange,
                                                         name="l0_pre_norm_bwd")
        return bufs

    halves = halves + _reduce_to_my_half([_chips_from_cols(dw_in_e)], ["w_in_even"], "late_grads", in_proj_dx, pre_norm_bwd)
    dx0, dg_mix_pre0 = got["dx0"], got["dg_mix_pre0"]

    groups_per_block = S5_CB // S5_GROUP
    own_group = (jnp.arange(S5_CB)[:, None] // S5_GROUP) == ((jnp.arange(S5_LANES)[None, :] // S5_STATE) % groups_per_block)
    db_re, db_im = s5_deinterleave(db_blocks, axis=1)
    dbbt_re = jnp.where(own_group, db_re, 0.0).reshape(groups_per_block, S5_GROUP, S5_LANES).sum(0)
    dbbt_im = jnp.where(own_group, db_im, 0.0).reshape(groups_per_block, S5_GROUP, S5_LANES).sum(0)
    dlr, dli, dldt8, dbtr, dbti = s5_disc_bwd(lr, li, ldt, btr, bti, dab_re, dab_im, dbbt_re, dbbt_im, name="s5_disc_bwd")
    dc_re, dc_im = s5_deinterleave(dc_blocks, axis=0)
    dcr2 = jnp.where(own_group.T, dc_re, 0.0).reshape(S5_LANES, groups_per_block, S5_GROUP).sum(1)
    dci2 = -jnp.where(own_group.T, dc_im, 0.0).reshape(S5_LANES, groups_per_block, S5_GROUP).sum(1)

    def c_layout(a):
        return jnp.transpose(a.reshape(S5_GROUPS, S5_STATE, S5_GROUP), (0, 2, 1))[None]

    def b_layout(a):
        return a.T.reshape(1, S5_GROUPS, S5_STATE, S5_GROUP)

    local_small = {
        "mix_pre_g": jnp.concatenate([dg_mix_pre0, dg_mix_pre1]), "mix_post_g": jnp.concatenate([dg_mix_post0, dg_mix_post1]),
        "mlp_pre_g": jnp.concatenate([dg_mlp_pre0, dg_mlp_pre1]), "mlp_post_g": jnp.concatenate([dg_mlp_post0, dg_mlp_post1]),
        "s5_lam_re": dlr.reshape(1, S5_GROUPS, S5_STATE), "s5_lam_im": dli.reshape(1, S5_GROUPS, S5_STATE),
        "s5_log_dt": dldt8[0:1, 0:S5_GROUPS],
        "s5_b_re": b_layout(dbtr), "s5_b_im": b_layout(dbti), "s5_c_re": c_layout(dcr2), "s5_c_im": c_layout(dci2),
        "s5_d": dd, "fox_b_f": dbf[:, 0:FOX_HEADS],
        "pool_w": _diag_blocks(dpw_bd, len(POOL_WINDOWS))[None],
        "sgu_w_s": jnp.where(causal[None], dws, 0.0)[None], "sgu_b_s": dbst.T[None],
        "pool_scale": dpool_scale, "sgu_ln_g": dln_g, "sgu_ln_b": dln_b,
    }
    return loss8, dx0, dict(zip(REDUCED_EARLY + ("w_in_even",), halves)), local_small


def _reduce_and_update(W, M, V, loss8, dx0, halves, local_small):
    cx, cy, cc = _coords()
    chip = 2 * cx + cy

    summed = dict(local_small, loss=loss8[0:1, 0:1])
    vec = _pack_vec(summed, REDUCED_SEGS, N_DEV * SUBLANES)
    piece = vec.shape[0] // N_DEV
    landed = exchange_pieces(vec.reshape(N_DEV, piece, LANES), scatter=True, name="small_grads_scatter")
    mine = sum_pieces(landed, name="small_grads_sum")
    everyone = exchange_pieces(mine, scatter=False, name="small_grads_gather")
    G = _unpack_vec(everyone, REDUCED_SEGS)
    loss = G["loss"].reshape(())
    for n in SHARDED_SMALL:
        G[n] = lax.dynamic_slice_in_dim(G[n], chip * LANES, LANES, axis=1)

    reduced = join_sibling_halves([halves[n] for n in BIG_NAMES], name="big_grads_join")
    for n, r in zip(BIG_NAMES, reduced):
        G[n] = r.reshape(W[n].shape)

    def two_d(a):
        return a.reshape(-1, a.shape[-1])

    delta, new_m, new_v = {}, {}, {}
    for n in BIG_NAMES:
        d_, m_, v_ = adamw(two_d(W[n]), two_d(G[n]), two_d(M[n]), two_d(V[n]), name=f"adamw_{n}")
        delta[n], new_m[n], new_v[n] = (t.reshape(W[n].shape) for t in (d_, m_, v_))
    packed = [_pack_vec(src, SMALL_SEGS, SUBLANES) for src in (W, G, M, V)]
    outs = adamw(*packed, name="adamw_replicated")
    for dst, t in zip((delta, new_m, new_v), outs):
        dst.update(_unpack_vec(t, SMALL_SEGS))
    sharded_segs = tuple((n, (1, LANES)) for n in SHARDED_SMALL)
    packed = [_pack_vec(src, sharded_segs, 1) for src in (W, G, M, V)]
    outs = adamw(*packed, name="adamw_sharded_vectors")
    for dst, t in zip((delta, new_m, new_v), outs):
        dst.update(_unpack_vec(t, sharded_segs))

    order = ["mix_pre_g", "mix_post_g", "mlp_pre_g", "mlp_post_g", "w_in_even", "s5_lam_re", "s5_lam_im", "s5_log_dt",
             "s5_b_re", "s5_b_im", "s5_c_re", "s5_c_im", "s5_d", "s5_w_glu", "fox_b_f", "w_out_even", "w_in_odd",
             "pool_w", "pool_scale", "sgu_ln_g", "sgu_ln_b", "sgu_w_s", "sgu_b_s", "w_out_odd", "mlp_w1", "mlp_w2"]
    return (loss, dx0[None], *[G[n] for n in order], *[delta[n] for n in order],
            *[new_m[n] for n in order], *[new_v[n] for n in order])
```

```python
import functools
import math

import jax
import jax.numpy as jnp
from jax import lax
from jax.experimental import pallas as pl
from jax.experimental.pallas import tpu as pltpu

F32 = jnp.float32
MXU_DTYPE = jnp.bfloat16
WIRE_DTYPE = jnp.bfloat16
EPS = 1e-6
VMEM_LIMIT_BYTES = 48 * 1024 * 1024
LANES = 128
SUBLANES = 8

D_MODEL = 1024
S5_WIDTH = 512
S5_GROUP = 16
S5_GROUPS = 32
S5_STATE = 64
S5_LANES = S5_GROUPS * S5_STATE
FOX_HEADS = 8
FOX_HEAD_DIM = 64
FOX_WIDTH = 512
EVEN_IN = S5_WIDTH + 3 * FOX_WIDTH + FOX_HEADS
EVEN_IN_PAD = 2176
POOL_WINDOWS = (2, 4, 8, 16)
POOL_HALO = 16
POOL_GROUP_DIM = 128
SGU_GROUPS = 4
SGU_GROUP_DIM = 128
CHUNK = 128
D_FF = 4096

ADAM_LR = 0.001
ADAM_B1 = 0.9
ADAM_B2 = 0.999
ADAM_EPS = 1e-08
ADAM_WD = 0.01
ADAM_STEP = 10

MESH_AXES = ("x", "y", "c")
MESH = pl.DeviceIdType.MESH
N_CHIPS = 4
N_DEV = 8

SDS = jax.ShapeDtypeStruct


def _cp(*sem):
    return pltpu.CompilerParams(dimension_semantics=sem, vmem_limit_bytes=VMEM_LIMIT_BYTES)


def _pick(dim, pref):
    if dim <= pref:
        return dim
    t = pref
    while t >= 256:
        if dim % t == 0:
            return t
        t //= 2
    return dim


def _row(tr, c):
    return pl.BlockSpec((tr, c), lambda i: (i, 0))


def _full(shape):
    nd = len(shape)
    return pl.BlockSpec(shape, lambda *_: (0,) * nd)


def _gelu_grad(x):
    c = math.sqrt(2.0 / math.pi)
    t = jnp.tanh(c * (x + 0.044715 * x * x * x))
    return 0.5 * (1.0 + t) + 0.5 * x * (1.0 - t * t) * c * (1.0 + 3.0 * 0.044715 * x * x)


MATMUL_VMEM_BYTES = 36 * 1024 * 1024


def matmul(a, b, *, name, ta=False, tb=False, out_dtype=F32, tm=2048, tn=1024, tk=4096, mnk=None, a_koff=0,
           a_spec=None, b_spec=None, o_spec=None, o_shape=None, prev=None, epi=None, epi_in=(), out_dtypes=None,
           exact_tiles=False, hosted=None):
    if mnk is None:
        M, K = (a.shape[1], a.shape[0]) if ta else a.shape
        K2, N = (b.shape[1], b.shape[0]) if tb else b.shape
        assert K == K2, (a.shape, b.shape, ta, tb)
    else:
        M, N, K = mnk
    out_dtypes = tuple(out_dtypes) if out_dtypes is not None else (out_dtype,)
    n_out, n_epi = len(out_dtypes), len(epi_in)
    tm, tn, tk = _pick(M, tm), _pick(N, tn), _pick(K, tk)

    def vmem_bytes(tm_, tn_, tk_):
        tiles = tm_ * tk_ * a.dtype.itemsize + tk_ * tn_ * b.dtype.itemsize
        tiles += tm_ * tn_ * (sum(jnp.dtype(d).itemsize for d in out_dtypes) + sum(e.dtype.itemsize for e in epi_in))
        return 2 * tiles + tm_ * tn_ * 4 * (tk_ < K)

    def halves(t, dim):
        return [t] + ([t // 2] if t % (2 * LANES) == 0 and t // 2 >= 512 and dim % (t // 2) == 0 else [])

    if exact_tiles:
        halves = lambda t, dim: [t]
    fits = [(m_, n_) for m_ in halves(tm, M) for n_ in halves(tn, N) if vmem_bytes(m_, n_, tk) <= MATMUL_VMEM_BYTES]
    if fits:
        tm, tn = max(fits, key=lambda t: (t[0] * t[1], t[0]))
    else:
        tm, tn = halves(tm, M)[-1], halves(tn, N)[-1]
        while vmem_bytes(tm, tn, tk) > MATMUL_VMEM_BYTES and tk % 2 == 0 and tk > 512:
            tk //= 2
    nk = K // tk
    assert a_koff % tk == 0 and not (ta and a_koff)
    ko = a_koff // tk
    dn = (((0 if ta else 1,), (1 if tb else 0,)), ((), ()))

    def body(*refs):
        a_ref, b_ref = refs[0], refs[1]
        epi_refs = refs[2:2 + n_epi]
        o_refs = refs[len(refs) - n_out - (nk > 1):len(refs) - (nk > 1)]
        k = pl.program_id(2)
        bv = b_ref[...]
        if bv.ndim == 3 and tb:
            cw = bv.shape[-1]
            prod = sum(lax.dot_general(a_ref[:, c * cw:(c + 1) * cw].astype(MXU_DTYPE), bv[c].astype(MXU_DTYPE), dn,
                                       preferred_element_type=F32) for c in range(bv.shape[0]))
        else:
            if bv.ndim == 3:
                bv = bv.reshape(-1, bv.shape[-1])
            prod = lax.dot_general(a_ref[...].astype(MXU_DTYPE), bv.astype(MXU_DTYPE), dn, preferred_element_type=F32)

        def finish(acc):
            res = (acc,) if epi is None else epi(acc, *[r[...] for r in epi_refs])
            for o_ref, r in zip(o_refs, res):
                o_ref[...] = r.astype(o_ref.dtype)

        if nk == 1:
            finish(prod)
            return
        acc_ref = refs[-1]

        @pl.when(k == 0)
        def _():
            acc_ref[...] = prod

        @pl.when(jnp.logical_and(k > 0, k < nk - 1))
        def _():
            acc_ref[...] += prod

        @pl.when(k == nk - 1)
        def _():
            finish(acc_ref[...] + prod)

    if a_spec is None:
        a_spec = pl.BlockSpec((tk, tm), lambda i, j, k: (k, i)) if ta else pl.BlockSpec((tm, tk), lambda i, j, k: (i, k + ko))
    else:
        a_spec = a_spec(tm, tn, tk)
    if b_spec is None:
        bs = pl.BlockSpec((tn, tk), lambda i, j, k: (j, k)) if tb else pl.BlockSpec((tk, tn), lambda i, j, k: (k, j))
    else:
        bs = b_spec(tm, tn, tk)
    tile = pl.BlockSpec((tm, tn), lambda i, j, k: (i, j))
    os_ = tile if o_spec is None else o_spec(tm, tn, tk)
    ins, in_specs, aliases = [a, b, *epi_in], [a_spec, bs] + [tile] * n_epi, {}
    if prev is not None:
        aliases = {len(ins): 0}
        ins.append(prev)
        in_specs.append(pl.BlockSpec(memory_space=pl.ANY))
    shapes = [SDS((M, N) if o_shape is None else o_shape, dt) for dt in out_dtypes]
    scratch = [pltpu.VMEM((tm, tn), F32)] if nk > 1 else []
    if hosted is not None:
        outs, bufs = call_hosting(body, hosted, name=name, grid=(M // tm, N // tn, nk), in_specs=in_specs,
                                  out_specs=[os_] * n_out, out_shape=shapes, inputs=ins, aliases=aliases, scratch=scratch)
        return (outs[0] if n_out == 1 else outs), bufs
    outs = pl.pallas_call(
        body, name=name, grid=(M // tm, N // tn, nk),
        in_specs=in_specs, out_specs=[os_] * n_out, out_shape=shapes, input_output_aliases=aliases,
        scratch_shapes=scratch, compiler_params=_cp("parallel", "parallel", "arbitrary"),
    )(*ins)
    return outs[0] if n_out == 1 else outs


def _rms_hat(x):
    return x * lax.rsqrt(jnp.mean(x * x, axis=-1, keepdims=True) + EPS)


def rms_fwd(x, g, *, name):
    T, D = x.shape
    tr = _pick(T, 512)

    def body(x_ref, g_ref, o_ref):
        o_ref[...] = (_rms_hat(x_ref[...]) * g_ref[...]).astype(o_ref.dtype)

    return pl.pallas_call(body, name=name, grid=(T // tr,), in_specs=[_row(tr, D), _full((1, D))],
                          out_specs=_row(tr, D), out_shape=SDS((T, D), MXU_DTYPE), compiler_params=_cp("parallel"))(x, g)


def res_norm_fwd(x, y, g_post, g_next, *, name):
    T, D = x.shape
    tr = _pick(T, 512)

    def body(x_ref, y_ref, gp_ref, gn_ref, o_ref, h_ref):
        xn = x_ref[...] + _rms_hat(y_ref[...]) * gp_ref[...]
        o_ref[...] = xn
        h_ref[...] = (_rms_hat(xn) * gn_ref[...]).astype(h_ref.dtype)

    return pl.pallas_call(body, name=name, grid=(T // tr,),
                          in_specs=[_row(tr, D), _row(tr, D), _full((1, D)), _full((1, D))],
                          out_specs=[_row(tr, D), _row(tr, D)], out_shape=[SDS((T, D), F32), SDS((T, D), MXU_DTYPE)],
                          compiler_params=_cp("parallel"))(x, y, g_post, g_next)


def res_norm_loss(x, y, g_post, target, *, name):
    T, D = x.shape
    tr = _pick(T, 512)

    def body(x_ref, y_ref, g_ref, t_ref, l_ref, d_ref):
        err = x_ref[...] + _rms_hat(y_ref[...]) * g_ref[...] - t_ref[...]
        d_ref[...] = err * (1.0 / D)

        @pl.when(pl.program_id(0) == 0)
        def _():
            l_ref[...] = jnp.zeros_like(l_ref)

        l_ref[...] += 0.5 * jnp.sum(jnp.mean(err * err, axis=-1, keepdims=True))

    return pl.pallas_call(body, name=name, grid=(T // tr,),
                          in_specs=[_row(tr, D), _row(tr, D), _full((1, D)), _row(tr, D)],
                          out_specs=[_full((SUBLANES, LANES)), _row(tr, D)],
                          out_shape=[SDS((SUBLANES, LANES), F32), SDS((T, D), F32)],
                          compiler_params=_cp("arbitrary"))(x, y, g_post, target)


def _rms_bwd_rows(x, g, dy):
    r = lax.rsqrt(jnp.mean(x * x, axis=-1, keepdims=True) + EPS)
    xh = x * r
    dxh = dy * g
    return r * (dxh - xh * jnp.mean(dxh * xh, axis=-1, keepdims=True)), jnp.sum(dy * xh, axis=0, keepdims=True)


def norm_res_bwd(x, g_pre, dh, res, y, g_post, *, name):
    T, D = x.shape
    tr = _pick(T, 512)

    def body(x_ref, gp_ref, dh_ref, res_ref, y_ref, gy_ref, dx_ref, dy_ref, dgp_ref, dgy_ref):
        dx, dgp = _rms_bwd_rows(x_ref[...], gp_ref[...], dh_ref[...])
        dx = dx + res_ref[...]
        dx_ref[...] = dx
        dy, dgy = _rms_bwd_rows(y_ref[...], gy_ref[...], dx)
        dy_ref[...] = dy.astype(dy_ref.dtype)

        @pl.when(pl.program_id(0) == 0)
        def _():
            dgp_ref[...] = jnp.zeros_like(dgp_ref)
            dgy_ref[...] = jnp.zeros_like(dgy_ref)

        dgp_ref[...] += dgp
        dgy_ref[...] += dgy

    row, vec = _row(tr, D), _full((1, D))
    return pl.pallas_call(body, name=name, grid=(T // tr,), in_specs=[row, vec, row, row, row, vec],
                          out_specs=[row, row, vec, vec],
                          out_shape=[SDS((T, D), F32), SDS((T, D), MXU_DTYPE), SDS((1, D), F32), SDS((1, D), F32)],
                          compiler_params=_cp("arbitrary"))(x, g_pre, dh, res, y, g_post)


def rms_bwd(x, g, dy, res, *, name, hosted=None):
    T, D = x.shape
    tr = _pick(T, 512)
    has_res = res is not None

    def body(*refs):
        if has_res:
            x_ref, g_ref, dy_ref, res_ref, dx_ref, dg_ref = refs
        else:
            x_ref, g_ref, dy_ref, dx_ref, dg_ref = refs
        dx, dg = _rms_bwd_rows(x_ref[...], g_ref[...], dy_ref[...])
        if has_res:
            dx = dx + res_ref[...]
        dx_ref[...] = dx.astype(dx_ref.dtype)

        @pl.when(pl.program_id(0) == 0)
        def _():
            dg_ref[...] = jnp.zeros_like(dg_ref)

        dg_ref[...] += dg

    ins = [x, g, dy] + ([res] if has_res else [])
    in_specs = [_row(tr, D), _full((1, D)), _row(tr, D)] + ([_row(tr, D)] if has_res else [])
    out_shape = [SDS((T, D), F32 if has_res else MXU_DTYPE), SDS((1, D), F32)]
    out_specs = [_row(tr, D), _full((1, D))]
    if hosted is not None:
        return call_hosting(body, hosted, name=name, grid=(T // tr,), in_specs=in_specs, out_specs=out_specs,
                            out_shape=out_shape, inputs=ins, aliases={})
    return pl.pallas_call(body, name=name, grid=(T // tr,), in_specs=in_specs, out_specs=out_specs,
                          out_shape=out_shape, compiler_params=_cp("arbitrary"))(*ins)


def _s5_disc(lr, li, ldt, btr, bti):
    dt = jnp.exp(ldt)
    k = lax.broadcasted_iota(jnp.int32, (SUBLANES, S5_LANES), 0).astype(F32)
    kf = k + 1.0
    kb = 8.0 - k
    ph = li * dt
    lm = lr * dt
    tf_re = jnp.exp(kf * lm) * jnp.cos(kf * ph)
    tf_im = jnp.exp(kf * lm) * jnp.sin(kf * ph)
    tb_re = jnp.exp(kb * lm) * jnp.cos(kb * ph)
    tb_im = -jnp.exp(kb * lm) * jnp.sin(kb * ph)
    mag = jnp.exp(lm)
    ab_re = mag * jnp.cos(ph)
    ab_im = mag * jnp.sin(ph)
    den = lr * lr + li * li
    nr = ab_re - 1.0
    ni = ab_im
    q_re = (nr * lr + ni * li) / den
    q_im = (ni * lr - nr * li) / den
    bbt_re = q_re * btr - q_im * bti
    bbt_im = q_re * bti + q_im * btr
    return tf_re, tf_im, tb_re, tb_im, bbt_re, bbt_im


def _s5_disc_core(lr, li, ldt, btr, bti):
    dt = jnp.exp(ldt)
    mag = jnp.exp(lr * dt)
    ab_re = mag * jnp.cos(li * dt)
    ab_im = mag * jnp.sin(li * dt)
    den = lr * lr + li * li
    nr = ab_re - 1.0
    ni = ab_im
    q_re = (nr * lr + ni * li) / den
    q_im = (ni * lr - nr * li) / den
    return ab_re, ab_im, q_re * btr - q_im * bti, q_re * bti + q_im * btr


def s5_disc_fwd(lr, li, ldt, btr, bti, *, name):
    def body(lr_ref, li_ref, ldt_ref, btr_ref, bti_ref, *outs):
        vals = _s5_disc(lr_ref[...], li_ref[...], ldt_ref[...], btr_ref[...], bti_ref[...])
        for o, v in zip(outs, vals):
            o[...] = v

    tab = SDS((SUBLANES, S5_LANES), F32)
    bb = SDS((S5_GROUP, S5_LANES), F32)
    return pl.pallas_call(body, name=name, out_shape=[tab, tab, tab, tab, bb, bb])(lr, li, ldt, btr, bti)


def s5_disc_bwd(lr, li, ldt, btr, bti, dab_re, dab_im, dbbt_re, dbbt_im, *, name):
    def body(lr_ref, li_ref, ldt_ref, btr_ref, bti_ref, dar_ref, dai_ref, dbr_ref, dbi_ref,
             dlr_ref, dli_ref, dldt_ref, dbtr_ref, dbti_ref):
        _, vjp = jax.vjp(_s5_disc_core, lr_ref[...], li_ref[...], ldt_ref[...], btr_ref[...], bti_ref[...])
        dlr, dli, dldt, dbtr, dbti = vjp((dar_ref[...], dai_ref[...], dbr_ref[...], dbi_ref[...]))
        dlr_ref[...] = dlr
        dli_ref[...] = dli
        dbtr_ref[...] = dbtr
        dbti_ref[...] = dbti
        lane_group = lax.broadcasted_iota(jnp.int32, (S5_LANES, LANES), 0) // S5_STATE
        col = lax.broadcasted_iota(jnp.int32, (S5_LANES, LANES), 1)
        ind = (lane_group == col).astype(F32)
        dldt_ref[...] = jnp.dot(jnp.broadcast_to(dldt, (SUBLANES, S5_LANES)), ind,
                                precision=lax.Precision.HIGHEST, preferred_element_type=F32)

    row = SDS((1, S5_LANES), F32)
    bb = SDS((S5_GROUP, S5_LANES), F32)
    return pl.pallas_call(body, name=name, out_shape=[row, row, SDS((SUBLANES, LANES), F32), bb, bb])(
        lr, li, ldt, btr, bti, dab_re, dab_im, dbbt_re, dbbt_im)


S5_NB = 1024


S5_CB = S5_WIDTH * S5_NB // S5_LANES


def _chan_rows_t(tm, tn, tk):
    return pl.BlockSpec((tk, S5_CB), lambda i, j, k: (k, j // 2))


def _chan_cols_of_i(tm, tn, tk):
    return pl.BlockSpec((tk, S5_CB), lambda i, j, k: (k, i // 2))


def _lanes_of_chan(tm, tn, tk):
    return pl.BlockSpec((tm, 2 * S5_NB), lambda i, j, k: (i, j))


def _s5_b_block_t(tm, tn, tk):
    return pl.BlockSpec((S5_CB, 2 * S5_NB), lambda i, j, k: (j, j))


def _s5_c_block(tm, tn, tk):
    return pl.BlockSpec((2 * S5_NB, S5_CB), lambda i, j, k: (j, j))


def s5_interleave(re, im, axis):
    parts = []
    for n in range(S5_LANES // S5_NB):
        sl = [slice(None)] * re.ndim
        sl[axis] = slice(n * S5_NB, (n + 1) * S5_NB)
        parts += [re[tuple(sl)], im[tuple(sl)]]
    return jnp.concatenate(parts, axis=axis)


def s5_deinterleave(a, axis):
    re, im = [], []
    for n in range(S5_LANES // S5_NB):
        sl = [slice(None)] * a.ndim
        sl[axis] = slice(2 * n * S5_NB, (2 * n + 1) * S5_NB)
        re.append(a[tuple(sl)])
        sl[axis] = slice((2 * n + 1) * S5_NB, (2 * n + 2) * S5_NB)
        im.append(a[tuple(sl)])
    return jnp.concatenate(re, axis=axis), jnp.concatenate(im, axis=axis)


def s5_scan(src, mat, tab_re, tab_im, *, reverse, name, states=None, hosted=None):
    T = src.shape[0]
    nb = S5_NB
    tc = _pick(T, 256)
    nl = S5_LANES // nb
    nt = T // tc
    ntile = tc // SUBLANES
    with_da = states is not None
    assert reverse or not with_da
    step_rows = ((1, 7), (2, 6), (4, 4)) if reverse else ((1, 0), (2, 1), (4, 3))
    drive_dn = _NT if reverse else (((1,), (0,)), ((), ()))

    def body(*refs):
        if with_da:
            (src_ref, wr_ref, wi_ref, tr_ref, ti_ref, sr_ref, si_ref, hr_ref, hi_ref, xo_ref, dar_ref, dai_ref,
             cr_ref, ci_ref, mr_ref, mi_ref, br_ref, bi_ref, ar_ref, ai_ref) = refs
        else:
            src_ref, wr_ref, wi_ref, tr_ref, ti_ref, xo_ref, cr_ref, ci_ref, mr_ref, mi_ref, br_ref, bi_ref = refs

        @pl.when(pl.program_id(1) == 0)
        def _():
            cr_ref[...] = jnp.zeros_like(cr_ref)
            ci_ref[...] = jnp.zeros_like(ci_ref)
            if with_da:
                ar_ref[...] = jnp.zeros_like(ar_ref)
                ai_ref[...] = jnp.zeros_like(ai_ref)

        lhs = src_ref[...].astype(MXU_DTYPE)
        br_ref[...] = lax.dot_general(lhs, wr_ref[...].astype(MXU_DTYPE), drive_dn, preferred_element_type=F32)
        bi_ref[...] = lax.dot_general(lhs, wi_ref[...].astype(MXU_DTYPE), drive_dn, preferred_element_type=F32)

        seen = jnp.where(pl.program_id(1) < nt - 1, 1.0, 0.0)

        def add_da(lr, li, r0, last_r, last_i):
            first = lax.broadcasted_iota(jnp.int32, (SUBLANES, nb), 0) == 0
            pr = jnp.where(first, last_r, pltpu.roll(sr_ref[pl.ds(r0, SUBLANES), :], 1, 0))
            pi = jnp.where(first, last_i, pltpu.roll(si_ref[pl.ds(r0, SUBLANES), :], 1, 0))
            ar_ref[...] += lr * pr + li * pi
            ai_ref[...] += li * pr - lr * pi

        io = lax.broadcasted_iota(jnp.int32, (SUBLANES, nb), 0)
        for s_, (d, r) in enumerate(step_rows):
            keep = (io < SUBLANES - d) if reverse else (io >= d)
            mr_ref[s_] = jnp.where(keep, tr_ref[r:r + 1, :], 0.0)
            mi_ref[s_] = jnp.where(keep, ti_ref[r:r + 1, :], 0.0)

        def tile(i, carry):
            cr, ci = carry
            j = (ntile - 1 - i) if reverse else i
            r0 = pl.multiple_of(j * SUBLANES, SUBLANES)
            xr = br_ref[pl.ds(r0, SUBLANES), :]
            xi = bi_ref[pl.ds(r0, SUBLANES), :]
            for s_, (d, _) in enumerate(step_rows):
                sh = (SUBLANES - d) if reverse else d
                sr = pltpu.roll(xr, sh, 0)
                si = pltpu.roll(xi, sh, 0)
                pr, pi = mr_ref[s_], mi_ref[s_]
                xr, xi = xr + pr * sr - pi * si, xi + pr * si + pi * sr
            tr, ti = tr_ref[...], ti_ref[...]
            xr, xi = xr + tr * cr - ti * ci, xi + tr * ci + ti * cr
            xo_ref[pl.ds(r0, SUBLANES), 0:nb] = xr
            xo_ref[pl.ds(r0, SUBLANES), nb:2 * nb] = xi
            if with_da:
                @pl.when(j > 0)
                def _():
                    p0 = pl.multiple_of(r0 - SUBLANES, SUBLANES)
                    add_da(xr, xi, r0, sr_ref[pl.ds(p0, SUBLANES), :][SUBLANES - 1:SUBLANES, :],
                           si_ref[pl.ds(p0, SUBLANES), :][SUBLANES - 1:SUBLANES, :])

                @pl.when(j == 0)
                def _():
                    add_da(xr, xi, r0, hr_ref[SUBLANES - 1:SUBLANES, :] * seen, hi_ref[SUBLANES - 1:SUBLANES, :] * seen)
            if reverse:
                return xr[0:1, :], xi[0:1, :]
            return xr[SUBLANES - 1:SUBLANES, :], xi[SUBLANES - 1:SUBLANES, :]

        cr, ci = lax.fori_loop(0, ntile, tile, (cr_ref[0:1, :], ci_ref[0:1, :]))
        cr_ref[0:1, :] = cr
        ci_ref[0:1, :] = ci
        if with_da:
            @pl.when(pl.program_id(1) == nt - 1)
            def _():
                dar_ref[...] = jnp.sum(ar_ref[...], axis=0, keepdims=True)
                dai_ref[...] = jnp.sum(ai_ref[...], axis=0, keepdims=True)

    def tmap(t):
        return (nt - 1 - t) if reverse else t

    hb = tc // SUBLANES
    re_spec = pl.BlockSpec((tc, nb), lambda n, t: (tmap(t), 2 * n))
    im_spec = pl.BlockSpec((tc, nb), lambda n, t: (tmap(t), 2 * n + 1))
    tab_spec = pl.BlockSpec((SUBLANES, nb), lambda n, t: (0, n))
    out_spec = pl.BlockSpec((tc, 2 * nb), lambda n, t: (tmap(t), n))
    out_shape = SDS((T, 2 * S5_LANES), F32)
    scratch = [pltpu.VMEM((SUBLANES, nb), F32), pltpu.VMEM((SUBLANES, nb), F32),
               pltpu.VMEM((len(step_rows), SUBLANES, nb), F32), pltpu.VMEM((len(step_rows), SUBLANES, nb), F32),
               pltpu.VMEM((tc, nb), F32), pltpu.VMEM((tc, nb), F32)]
    src_spec = pl.BlockSpec((tc, S5_CB), lambda n, t: (tmap(t), n))
    if reverse:
        wr_spec = pl.BlockSpec((nb, S5_CB), lambda n, t: (2 * n, n))
        wi_spec = pl.BlockSpec((nb, S5_CB), lambda n, t: (2 * n + 1, n))
    else:
        wr_spec = pl.BlockSpec((S5_CB, nb), lambda n, t: (n, 2 * n))
        wi_spec = pl.BlockSpec((S5_CB, nb), lambda n, t: (n, 2 * n + 1))
    drive_specs = [src_spec, wr_spec, wi_spec, tab_spec, tab_spec]
    drive = [src, mat, mat, tab_re, tab_im]
    if not with_da:
        return pl.pallas_call(body, name=name, grid=(nl, nt), in_specs=drive_specs,
                              out_specs=out_spec, out_shape=out_shape, scratch_shapes=scratch,
                              compiler_params=_cp("parallel", "arbitrary"))(*drive)
    re_halo = pl.BlockSpec((SUBLANES, nb), lambda n, t: (jnp.maximum(tmap(t) * hb - 1, 0), 2 * n))
    im_halo = pl.BlockSpec((SUBLANES, nb), lambda n, t: (jnp.maximum(tmap(t) * hb - 1, 0), 2 * n + 1))
    acc = pl.BlockSpec((1, nb), lambda n, t: (0, n))
    row = SDS((1, S5_LANES), F32)
    return call_hosting(
        body, hosted, name=name, grid=(nl, nt),
        in_specs=drive_specs + [re_spec, im_spec, re_halo, im_halo],
        out_specs=[out_spec, acc, acc], out_shape=[out_shape, row, row],
        inputs=drive + [states, states, states, states], aliases={},
        scratch=scratch + [pltpu.VMEM((SUBLANES, nb), F32), pltpu.VMEM((SUBLANES, nb), F32)])


def s5_out_fwd(yc, u, d, *, name):
    T, C = yc.shape
    tr = _pick(T, 512)

    def body(yc_ref, u_ref, d_ref, yl_ref, yg_ref):
        yl = yc_ref[...] + d_ref[...] * u_ref[...]
        yl_ref[...] = yl
        yg_ref[...] = jax.nn.gelu(yl)

    return pl.pallas_call(body, name=name, grid=(T // tr,), in_specs=[_row(tr, C), _row(tr, C), _full((1, C))],
                          out_specs=[_row(tr, C)] * 2, out_shape=[SDS((T, C), F32)] * 2,
                          compiler_params=_cp("parallel"))(yc, u, d)


def glu_fwd(yg, gl, *, out_cols, name):
    T, C = yg.shape
    tr = _pick(T, 512)

    def body(yg_ref, gl_ref, o_ref):
        o_ref[...] = yg_ref[...] * jax.nn.sigmoid(gl_ref[...])

    return pl.pallas_call(body, name=name, grid=(T // tr,), in_specs=[_row(tr, C)] * 2, out_specs=_row(tr, C),
                          out_shape=SDS((T, out_cols), F32), compiler_params=_cp("parallel"))(yg, gl)


def glu_bwd(yg, gl, dy, *, name):
    T, C = yg.shape
    tr = _pick(T, 512)

    def body(yg_ref, gl_ref, dy_ref, dyg_ref, dgl_ref):
        s = jax.nn.sigmoid(gl_ref[...])
        dyv = dy_ref[...]
        dyg_ref[...] = dyv * s
        dgl_ref[...] = (dyv * yg_ref[...] * s * (1.0 - s)).astype(dgl_ref.dtype)

    return pl.pallas_call(body, name=name, grid=(T // tr,), in_specs=[_row(tr, C)] * 3, out_specs=[_row(tr, C)] * 2,
                          out_shape=[SDS((T, C), F32), SDS((T, C), MXU_DTYPE)],
                          compiler_params=_cp("parallel"))(yg, gl, dy)


def s5_out_bwd(yl, u, d, dyg_a, dyg_b, *, name):
    T, C = yl.shape
    tr = _pick(T, 512)

    def body(yl_ref, u_ref, d_ref, da_ref, db_ref, dyl_ref, du_ref, dd_ref):
        dyl = (da_ref[...] + db_ref[...]) * _gelu_grad(yl_ref[...])
        dyl_ref[...] = dyl.astype(dyl_ref.dtype)
        du_ref[...] = dyl * d_ref[...]

        @pl.when(pl.program_id(0) == 0)
        def _():
            dd_ref[...] = jnp.zeros_like(dd_ref)

        dd_ref[...] += jnp.sum(dyl * u_ref[...], axis=0, keepdims=True)

    return pl.pallas_call(body, name=name, grid=(T // tr,),
                          in_specs=[_row(tr, C), _row(tr, C), _full((1, C)), _row(tr, C), _row(tr, C)],
                          out_specs=[_row(tr, C), _row(tr, C), _full((1, C))],
                          out_shape=[SDS((T, C), MXU_DTYPE), SDS((T, C), F32), SDS((1, C), F32)],
                          compiler_params=_cp("arbitrary"))(yl, u, d, dyg_a, dyg_b)


def add2(a, b, *, name):
    T, C = a.shape
    tr = _pick(T, 512)

    def body(a_ref, b_ref, o_ref):
        o_ref[...] = a_ref[...] + b_ref[...]

    return pl.pallas_call(body, name=name, grid=(T // tr,), in_specs=[_row(tr, C)] * 2, out_specs=_row(tr, C),
                          out_shape=SDS((T, C), F32), compiler_params=_cp("parallel"))(a, b)


def _tri(n, upper):
    r = lax.broadcasted_iota(jnp.int32, (n, n), 0)
    c = lax.broadcasted_iota(jnp.int32, (n, n), 1)
    return ((c >= r) if upper else (c <= r)).astype(F32)


def fox_gate_fwd(fl, bf, *, fl_col, name):
    T = fl.shape[0]
    tb = _pick(T, 256)

    def body(fl_ref, bf_ref, f_ref, c_ref):
        @pl.when(pl.program_id(0) == 0)
        def _():
            c_ref[...] = jnp.zeros_like(c_ref)

        lf = jax.nn.log_sigmoid(fl_ref[...] + bf_ref[...])
        f = jnp.dot(_tri(tb, False), lf, precision=lax.Precision.HIGHEST, preferred_element_type=F32) + c_ref[0:1, :]
        f_ref[...] = f * LOG2E
        c_ref[0:1, :] = f[tb - 1:tb, :]

    fl_spec = pl.BlockSpec((tb, LANES), lambda i: (i, fl_col))
    return pl.pallas_call(body, name=name, grid=(T // tb,), in_specs=[fl_spec, _full((1, LANES))],
                          out_specs=_row(tb, LANES), out_shape=SDS((T, LANES), F32),
                          scratch_shapes=[pltpu.VMEM((SUBLANES, LANES), F32)], compiler_params=_cp("arbitrary"))(fl, bf)


def fox_gate_bwd(fl, bf, df_keys, df_queries, *, fl_col, name):
    T = fl.shape[0]
    tb = _pick(T, 256)
    nt = T // tb

    def body(fl_ref, bf_ref, dfk_ref, dfq_ref, dfl_ref, dbf_ref, c_ref):
        @pl.when(pl.program_id(0) == 0)
        def _():
            c_ref[...] = jnp.zeros_like(c_ref)
            dbf_ref[...] = jnp.zeros_like(dbf_ref)

        dlf = jnp.dot(_tri(tb, True), dfk_ref[...] + dfq_ref[...], precision=lax.Precision.HIGHEST,
                      preferred_element_type=F32) + c_ref[0:1, :]
        c_ref[0:1, :] = dlf[0:1, :]
        dfl = dlf * jax.nn.sigmoid(-(fl_ref[...] + bf_ref[...]))
        dfl_ref[...] = dfl
        dbf_ref[...] += jnp.sum(dfl, axis=0, keepdims=True)

    rev = pl.BlockSpec((tb, LANES), lambda i: (nt - 1 - i, 0))
    fl_rev = pl.BlockSpec((tb, LANES), lambda i: (nt - 1 - i, fl_col))
    return pl.pallas_call(body, name=name, grid=(nt,), in_specs=[fl_rev, _full((1, LANES)), rev, rev],
                          out_specs=[rev, _full((1, LANES))], out_shape=[SDS((T, LANES), F32), SDS((1, LANES), F32)],
                          scratch_shapes=[pltpu.VMEM((SUBLANES, LANES), F32)],
                          compiler_params=_cp("arbitrary"))(fl, bf, df_keys, df_queries)


FOX_BLOCK = 512
FOX_PAIRS = FOX_HEADS // 2
_NT = (((1,), (1,)), ((), ()))


LOG2E = 1.4426950408889634
FOX_FWD_UNROLL = 4
FOX_BWD_UNROLL = 2


def _fox_block(T):
    return _pick(T, FOX_BLOCK)


def _own_lanes(lane, hh):
    return (lane < FOX_HEAD_DIM) if hh == 0 else (lane >= FOX_HEAD_DIM)


def _grouped_steps(step, lo, n, unroll, init):
    def trip(t, c):
        for u in range(unroll):
            c = step(lo + t * unroll + u, c)
        return c

    carry = lax.fori_loop(0, n // unroll, trip, init)
    for u in range(unroll - 1):
        carry = lax.cond(n % unroll > u, lambda c: step(lo + (n // unroll) * unroll + u, c), lambda c: c, carry)
    return carry


Q_TILE0, K_TILE0, V_TILE0, O_TILE0 = 4, 8, 12, 4
FL_TILE = 16
POOL_COL = 2


def fox_fwd(z, f_col, f_row, ycat, hosted, *, name):
    T = z.shape[0]
    blk = _fox_block(T)
    nb = T // blk
    scale = FOX_HEAD_DIM ** -0.5

    def body(q_ref, k_ref, v_ref, fc_ref, fr_ref, prev_ref, o_ref, l_ref):
        i = pl.program_id(1)
        row = lax.broadcasted_iota(jnp.int32, (blk, blk), 0)
        col = lax.broadcasted_iota(jnp.int32, (blk, blk), 1)
        lane = lax.broadcasted_iota(jnp.int32, (blk, LANES), 1)
        qt = q_ref[...] * (scale * LOG2E)
        outs = []
        for hh in range(2):
            qh = jnp.where(_own_lanes(lane, hh), qt, 0.0).astype(MXU_DTYPE)
            fi = fc_ref[0, :, hh:hh + 1]

            def step(j, carry, masked=False):
                m, l, acc = carry
                r0 = pl.multiple_of(j * blk, blk)
                kj = k_ref[pl.ds(r0, blk), :].astype(MXU_DTYPE)
                vj = v_ref[pl.ds(r0, blk), :].astype(MXU_DTYPE)
                s = lax.dot_general(qh, kj, _NT, preferred_element_type=F32) + (fi - fr_ref[0, j, hh:hh + 1, :])
                if masked:
                    s = jnp.where(col <= row, s, -jnp.inf)
                m_new = jnp.maximum(m, jnp.max(s, axis=-1, keepdims=True))
                p = jnp.exp2(s - m_new)
                alpha = jnp.exp2(m - m_new)
                l = alpha * l + jnp.sum(p, axis=-1, keepdims=True)
                acc = alpha * acc + jnp.dot(p.astype(MXU_DTYPE), vj, preferred_element_type=F32)
                return m_new, l, acc

            init = (jnp.full((blk, 1), -jnp.inf, F32), jnp.zeros((blk, 1), F32), jnp.zeros((blk, LANES), F32))
            m, l, acc = step(i, _grouped_steps(step, 0, i, FOX_FWD_UNROLL, init), True)
            outs.append(acc / l)
            l_ref[0, :, hh:hh + 1] = m + jnp.log2(l)
        o_ref[...] = jnp.where(_own_lanes(lane, 0), outs[0], outs[1])

    qspec = pl.BlockSpec((blk, LANES), lambda h, i: (i, Q_TILE0 + h))
    kspec = pl.BlockSpec((T, LANES), lambda h, i: (0, K_TILE0 + h))
    vspec = pl.BlockSpec((T, LANES), lambda h, i: (0, V_TILE0 + h))
    ospec = pl.BlockSpec((blk, LANES), lambda h, i: (i, O_TILE0 + h))
    cspec = pl.BlockSpec((1, blk, 2), lambda h, i: (h, i, 0))
    rspec = pl.BlockSpec((1, nb, 2, blk), lambda h, i: (h, 0, 0, 0))
    return call_hosting(body, hosted, name=name, grid=(FOX_PAIRS, nb),
                        in_specs=[qspec, kspec, vspec, cspec, rspec, ANY], out_specs=[ospec, cspec],
                        out_shape=[SDS(ycat.shape, F32), SDS((FOX_PAIRS, T, 2), F32)],
                        inputs=[z, z, z, f_col, f_row, ycat], aliases={5: 0})


def fox_dd(ycat, dycat, *, name):
    T = ycat.shape[0]
    blk = _fox_block(T)

    def body(o_ref, do_ref, dd_ref):
        lane = lax.broadcasted_iota(jnp.int32, (blk, LANES), 1)
        prod = do_ref[...] * o_ref[...]
        for hh in range(2):
            dd_ref[0, :, hh:hh + 1] = jnp.sum(jnp.where(_own_lanes(lane, hh), prod, 0.0), axis=-1, keepdims=True)

    ospec = pl.BlockSpec((blk, LANES), lambda h, i: (i, O_TILE0 + h))
    return pl.pallas_call(body, name=name, grid=(FOX_PAIRS, T // blk), in_specs=[ospec, ospec],
                          out_specs=pl.BlockSpec((1, blk, 2), lambda h, i: (h, i, 0)),
                          out_shape=SDS((FOX_PAIRS, T, 2), F32), compiler_params=_cp("parallel", "parallel"))(ycat, dycat)


def fox_bwd(z, dycat, f_col, f_row, lse_row, dd_row, hosted, *, name):
    T = z.shape[0]
    blk = _fox_block(T)
    nb = T // blk
    scale = FOX_HEAD_DIM ** -0.5

    def body(q_ref, k_ref, v_ref, do_ref, fc_ref, fr_ref, lr_ref, dr_ref, dk_ref, dv_ref, df_ref, dqt_ref, dfq_ref):
        j = pl.program_id(1)

        @pl.when(j == 0)
        def _():
            dqt_ref[...] = jnp.zeros_like(dqt_ref)
            dfq_ref[...] = jnp.zeros_like(dfq_ref)

        row = lax.broadcasted_iota(jnp.int32, (blk, blk), 0)
        col = lax.broadcasted_iota(jnp.int32, (blk, blk), 1)
        lane = lax.broadcasted_iota(jnp.int32, (blk, LANES), 1)
        kt = k_ref[...]
        vt = v_ref[...]
        dks, dvs = [], []
        for hh in range(2):
            own = _own_lanes(lane, hh)
            kh = jnp.where(own, kt, 0.0).astype(MXU_DTYPE)
            vh = jnp.where(own, vt, 0.0).astype(MXU_DTYPE)
            kht = kh.T
            fj = fc_ref[0, :, hh:hh + 1]

            def step(i, carry, masked=False):
                dk, dv, df = carry
                r0 = pl.multiple_of(i * blk, blk)
                qi = (q_ref[pl.ds(r0, blk), :] * (scale * LOG2E)).astype(MXU_DTYPE)
                doi = do_ref[pl.ds(r0, blk), :].astype(MXU_DTYPE)
                st = lax.dot_general(kh, qi, _NT, preferred_element_type=F32) + (fr_ref[0, i, hh:hh + 1, :] - fj)
                pt = jnp.exp2(st - lr_ref[0, i, hh:hh + 1, :])
                if masked:
                    pt = jnp.where(col >= row, pt, 0.0)
                dv = dv + jnp.dot(pt.astype(MXU_DTYPE), doi, preferred_element_type=F32)
                dpt = lax.dot_general(vh, doi, _NT, preferred_element_type=F32)
                dst = pt * (dpt - dr_ref[0, i, hh:hh + 1, :])
                dsb = dst.astype(MXU_DTYPE)
                dk = dk + jnp.dot(dsb, qi, preferred_element_type=F32)
                df = df - jnp.sum(dst, axis=-1, keepdims=True)
                dqt_ref[0, i] += jnp.dot(kht, dsb, preferred_element_type=F32)
                dfq_ref[0, i, hh:hh + 1, :] += jnp.sum(dst, axis=0, keepdims=True)
                return dk, dv, df

            init = (jnp.zeros((blk, LANES), F32), jnp.zeros((blk, LANES), F32), jnp.zeros((blk, 1), F32))
            dk, dv, df = _grouped_steps(step, j + 1, nb - 1 - j, FOX_BWD_UNROLL, step(j, init, True))
            dks.append(dk * (1.0 / LOG2E))
            dvs.append(dv)
            df_ref[0, :, hh:hh + 1] = df
        dk_ref[...] = jnp.where(_own_lanes(lane, 0), dks[0], dks[1])
        dv_ref[...] = jnp.where(_own_lanes(lane, 0), dvs[0], dvs[1])

    bspec = pl.BlockSpec((blk, LANES), lambda h, j: (j, h))
    qspec = pl.BlockSpec((T, LANES), lambda h, j: (0, Q_TILE0 + h))
    kspec = pl.BlockSpec((blk, LANES), lambda h, j: (j, K_TILE0 + h))
    vspec = pl.BlockSpec((blk, LANES), lambda h, j: (j, V_TILE0 + h))
    dospec = pl.BlockSpec((T, LANES), lambda h, j: (0, O_TILE0 + h))
    cspec = pl.BlockSpec((1, blk, 2), lambda h, j: (h, j, 0))
    rspec = pl.BlockSpec((1, nb, 2, blk), lambda h, j: (h, 0, 0, 0))
    dqspec = pl.BlockSpec((1, nb, LANES, blk), lambda h, j: (h, 0, 0, 0))
    return call_hosting(body, hosted, name=name, grid=(FOX_PAIRS, nb),
                        in_specs=[qspec, kspec, vspec, dospec, cspec, rspec, rspec, rspec],
                        out_specs=[bspec, bspec, cspec, dqspec, rspec],
                        out_shape=[SDS((T, FOX_WIDTH), F32), SDS((T, FOX_WIDTH), F32), SDS((FOX_PAIRS, T, 2), F32),
                                   SDS((FOX_PAIRS, nb, LANES, blk), F32), SDS((FOX_PAIRS, nb, 2, blk), F32)],
                        inputs=[z, z, z, dycat, f_col, f_row, lse_row, dd_row], aliases={})


def _pairs_col(a, T):
    return jnp.transpose(a[:, :FOX_HEADS].reshape(T, FOX_PAIRS, 2), (1, 0, 2))


def _col_to_row(a, T):
    blk = _fox_block(T)
    return jnp.transpose(a.reshape(FOX_PAIRS, T // blk, blk, 2), (0, 1, 3, 2))


def _row_to_col(a, T):
    return jnp.transpose(a, (0, 1, 3, 2)).reshape(FOX_PAIRS, T, 2)


def _pairs_to_lanes(a, T):
    flat = jnp.transpose(a, (1, 0, 2)).reshape(T, FOX_HEADS)
    return jnp.pad(flat, ((0, 0), (0, LANES - FOX_HEADS)))


def _pool_counts(t0, n, w):
    t = (t0 + lax.broadcasted_iota(jnp.int32, (n, 1), 0)).astype(F32)
    return jnp.minimum(t + 1.0, float(w))


def pool_window(x, *, adjoint, name, in_col=0, into=None, out_col=0, out_dtype=F32):
    T, C = x.shape[0], len(POOL_WINDOWS) * POOL_GROUP_DIM
    tr = _pick(T, 512)
    nt = T // tr
    hb = tr // POOL_HALO
    n = tr + POOL_HALO

    def body(x_ref, h_ref, *rest):
        o_ref = rest[-1]
        i = pl.program_id(0)
        cur = x_ref[...]
        if adjoint:
            halo = h_ref[...] * jnp.where(i < nt - 1, 1.0, 0.0)
            ext = jnp.concatenate([cur, halo], axis=0)
            t0 = i * tr
        else:
            halo = h_ref[...] * jnp.where(i > 0, 1.0, 0.0)
            ext = jnp.concatenate([halo, cur], axis=0)
            t0 = i * tr - POOL_HALO
        sums = {}
        for g, w in enumerate(POOL_WINDOWS):
            ls = slice(g * POOL_GROUP_DIM, (g + 1) * POOL_GROUP_DIM)
            s = ext[:, ls]
            if adjoint:
                s = s / _pool_counts(t0, n, w)
            d = 1
            while d < w:
                s = s + pltpu.roll(s, (n - d) if adjoint else d, 0)
                d *= 2
            if adjoint:
                o_ref[:, ls] = (s[0:tr, :] - cur[:, ls]).astype(o_ref.dtype)
            else:
                o_ref[:, ls] = (s[POOL_HALO:n, :] / _pool_counts(i * tr, tr, w) - cur[:, ls]).astype(o_ref.dtype)

    if adjoint:
        halo_spec = pl.BlockSpec((POOL_HALO, C), lambda i: (jnp.minimum((i + 1) * hb, T // POOL_HALO - 1), in_col))
    else:
        halo_spec = pl.BlockSpec((POOL_HALO, C), lambda i: (jnp.maximum(i * hb - 1, 0), in_col))
    x_spec = pl.BlockSpec((tr, C), lambda i: (i, in_col))
    if into is None:
        return pl.pallas_call(body, name=name, grid=(nt,), in_specs=[x_spec, halo_spec], out_specs=_row(tr, C),
                              out_shape=SDS((T, C), out_dtype), compiler_params=_cp("parallel"))(x, x)
    return pl.pallas_call(body, name=name, grid=(nt,), in_specs=[x_spec, halo_spec, ANY],
                          out_specs=pl.BlockSpec((tr, C), lambda i: (i, out_col)), out_shape=SDS(into.shape, into.dtype),
                          input_output_aliases={2: 0}, compiler_params=_cp("parallel"))(x, x, into)


def colscale_fwd(a, s, *, out_cols, name):
    T, C = a.shape
    tr = _pick(T, 512)

    def body(a_ref, s_ref, o_ref):
        o_ref[...] = (a_ref[...] * s_ref[...]).astype(o_ref.dtype)

    return pl.pallas_call(body, name=name, grid=(T // tr,), in_specs=[_row(tr, C), _full((1, C))], out_specs=_row(tr, C),
                          out_shape=SDS((T, out_cols), MXU_DTYPE), compiler_params=_cp("parallel"))(a, s)


def colscale_bwd(a, s, dy, *, name):
    T, C = a.shape
    tr = _pick(T, 512)

    def body(a_ref, s_ref, dy_ref, da_ref, ds_ref):
        dyv = dy_ref[...]
        da_ref[...] = (dyv * s_ref[...]).astype(da_ref.dtype)

        @pl.when(pl.program_id(0) == 0)
        def _():
            ds_ref[...] = jnp.zeros_like(ds_ref)

        ds_ref[...] += jnp.sum(dyv * a_ref[...], axis=0, keepdims=True)

    return pl.pallas_call(body, name=name, grid=(T // tr,), in_specs=[_row(tr, C), _full((1, C)), _row(tr, C)],
                          out_specs=[_row(tr, C), _full((1, C))], out_shape=[SDS((T, C), MXU_DTYPE), SDS((1, C), F32)],
                          compiler_params=_cp("arbitrary"))(a, s, dy)


SGU_ROWS = 512


def _sgu_norm(v, ln_g, ln_b):
    vg = jax.nn.gelu(v)
    xc = vg - jnp.mean(vg, axis=-1, keepdims=True)
    r = lax.rsqrt(jnp.mean(xc * xc, axis=-1, keepdims=True) + EPS)
    xh = xc * r
    return xh * ln_g + ln_b, xh, r


def _rowc(tr, c, cb):
    return pl.BlockSpec((tr, c), lambda i: (i, cb))


def sgu_fwd(z, ln_g, ln_b, ws, bst, ycat, *, name):
    T, C = z.shape[0], SGU_GROUPS * SGU_GROUP_DIM
    tr = _pick(T, SGU_ROWS)

    def body(u_ref, v_ref, g_ref, b_ref, ws_ref, bst_ref, prev_ref, o_ref):
        vn, _, _ = _sgu_norm(v_ref[...], g_ref[...], b_ref[...])
        vn = vn.astype(MXU_DTYPE)
        ug = jax.nn.gelu(u_ref[...])
        for g in range(SGU_GROUPS):
            w = ws_ref[g].astype(MXU_DTYPE)
            bias = bst_ref[:, g:g + 1]
            for c in range(tr // CHUNK):
                rs = slice(c * CHUNK, (c + 1) * CHUNK)
                ls = slice(g * SGU_GROUP_DIM, (g + 1) * SGU_GROUP_DIM)
                mixed = jnp.dot(w, vn[rs, ls], preferred_element_type=F32) + bias
                o_ref[rs, ls] = (ug[rs, ls] * mixed).astype(o_ref.dtype)

    return pl.pallas_call(body, name=name, grid=(T // tr,),
                          in_specs=[_rowc(tr, C, 0), _rowc(tr, C, 1), _full((1, C)), _full((1, C)),
                                    _full((SGU_GROUPS, CHUNK, CHUNK)), _full((CHUNK, SGU_GROUPS)), ANY],
                          out_specs=_rowc(tr, C, 1), out_shape=SDS(ycat.shape, ycat.dtype), input_output_aliases={6: 0},
                          compiler_params=_cp("parallel"))(z, z, ln_g, ln_b, ws, bst, ycat)


def sgu_bwd(z, ln_g, ln_b, ws, wst, bst, dycat, *, out_cols, name):
    T, C = z.shape[0], SGU_GROUPS * SGU_GROUP_DIM
    tr = _pick(T, SGU_ROWS)

    def body(u_ref, v_ref, g_ref, b_ref, ws_ref, wst_ref, bst_ref, dy_ref,
             duv_ref, dg_ref, db_ref, dws_ref, dbst_ref, dvn_ref):
        du_ref = duv_ref.at[:, 0:C]
        dv_ref = duv_ref.at[:, C:2 * C]
        @pl.when(pl.program_id(0) == 0)
        def _():
            dg_ref[...] = jnp.zeros_like(dg_ref)
            db_ref[...] = jnp.zeros_like(db_ref)
            dws_ref[...] = jnp.zeros_like(dws_ref)
            dbst_ref[...] = jnp.zeros_like(dbst_ref)

        uv = u_ref[...]
        vv = v_ref[...]
        vn, xh, r = _sgu_norm(vv, g_ref[...], b_ref[...])
        vn = vn.astype(MXU_DTYPE)
        ug = jax.nn.gelu(uv)
        dyv = dy_ref[...]
        for g in range(SGU_GROUPS):
            w = ws_ref[g].astype(MXU_DTYPE)
            wt = wst_ref[g].astype(MXU_DTYPE)
            bias = bst_ref[:, g:g + 1]
            dw = jnp.zeros((CHUNK, CHUNK), F32)
            dbias = jnp.zeros((CHUNK, 1), F32)
            for c in range(tr // CHUNK):
                rs = slice(c * CHUNK, (c + 1) * CHUNK)
                ls = slice(g * SGU_GROUP_DIM, (g + 1) * SGU_GROUP_DIM)
                vblk = vn[rs, ls]
                mixed = jnp.dot(w, vblk, preferred_element_type=F32) + bias
                dyb = dyv[rs, ls]
                du_ref[rs, ls] = (dyb * mixed * _gelu_grad(uv[rs, ls])).astype(du_ref.dtype)
                dmixed = dyb * ug[rs, ls]
                dbias = dbias + jnp.sum(dmixed, axis=-1, keepdims=True)
                dmb = dmixed.astype(MXU_DTYPE)
                dw = dw + lax.dot_general(dmb, vblk, _NT, preferred_element_type=F32)
                dvn_ref[rs, ls] = jnp.dot(wt, dmb, preferred_element_type=F32)
            dws_ref[g] += dw
            dbst_ref[:, g:g + 1] += dbias
        dvn = dvn_ref[...]
        dg_ref[...] += jnp.sum(dvn * xh, axis=0, keepdims=True)
        db_ref[...] += jnp.sum(dvn, axis=0, keepdims=True)
        dxh = dvn * g_ref[...]
        dvg = r * (dxh - jnp.mean(dxh, axis=-1, keepdims=True) - xh * jnp.mean(dxh * xh, axis=-1, keepdims=True))
        dv_ref[...] = (dvg * _gelu_grad(vv)).astype(dv_ref.dtype)

    wspec = _full((SGU_GROUPS, CHUNK, CHUNK))
    return pl.pallas_call(body, name=name, grid=(T // tr,),
                          in_specs=[_rowc(tr, C, 0), _rowc(tr, C, 1), _full((1, C)), _full((1, C)), wspec, wspec,
                                    _full((CHUNK, SGU_GROUPS)), _rowc(tr, C, 1)],
                          out_specs=[_rowc(tr, 2 * C, 0), _full((1, C)), _full((1, C)), wspec,
                                     _full((CHUNK, SGU_GROUPS))],
                          out_shape=[SDS((T, out_cols), MXU_DTYPE), SDS((1, C), F32), SDS((1, C), F32),
                                     SDS((SGU_GROUPS, CHUNK, CHUNK), F32), SDS((CHUNK, SGU_GROUPS), F32)],
                          scratch_shapes=[pltpu.VMEM((tr, C), F32)],
                          compiler_params=_cp("arbitrary"))(z, z, ln_g, ln_b, ws, wst, bst, dycat)


def adamw(w, g, m, v, *, name):
    R, C = w.shape
    tr = _pick(R, 512)
    c1 = 1.0 - ADAM_B1 ** ADAM_STEP
    c2 = 1.0 - ADAM_B2 ** ADAM_STEP

    def body(w_ref, g_ref, m_ref, v_ref, d_ref, nm_ref, nv_ref):
        gv = g_ref[...]
        nm = ADAM_B1 * m_ref[...] + (1.0 - ADAM_B1) * gv
        nv = ADAM_B2 * v_ref[...] + (1.0 - ADAM_B2) * (gv * gv)
        nm_ref[...] = nm
        nv_ref[...] = nv
        d_ref[...] = -ADAM_LR * ((nm / c1) / (jnp.sqrt(nv / c2) + ADAM_EPS) + ADAM_WD * w_ref[...])

    spec = _row(tr, C)
    return pl.pallas_call(body, name=name, grid=(R // tr,), in_specs=[spec] * 4, out_specs=[spec] * 3,
                          out_shape=[SDS((R, C), F32)] * 3, compiler_params=_cp("parallel"))(w, g, m, v)


ANY = pl.BlockSpec(memory_space=pl.ANY)


def _coords():
    return lax.axis_index("x"), lax.axis_index("y"), lax.axis_index("c")


def _other_chips(x, y):
    return [(1 - x, y), (x, 1 - y), (1 - x, 1 - y)]


def _remote(src, dst, send_sems, recv_sems, k, dev):
    return pltpu.make_async_remote_copy(src_ref=src, dst_ref=dst, send_sem=send_sems.at[k], recv_sem=recv_sems.at[k],
                                        device_id=dev, device_id_type=MESH)


LOCAL_CHUNKS = 8


def allgather_chip_shards(shards, small, *, name):
    na = len(shards)

    def body(*refs):
        s_refs, sm_ref = refs[:na], refs[na]
        o_refs, smo_ref = refs[na + 1:2 * na + 1], refs[2 * na + 1]
        send_sems, recv_sems, local_sems = refs[2 * na + 2:]
        x, y, c = _coords()
        j = 2 * x + y
        sibling = (x, y, 1 - c)
        chips = _other_chips(x, y)
        for a in range(na):
            chunk = shards[a].shape[0] // LOCAL_CHUNKS
            for q in range(LOCAL_CHUNKS):
                rows = pl.ds(q * chunk, chunk)
                pltpu.make_async_copy(s_refs[a].at[rows], o_refs[a].at[j, rows], local_sems.at[a]).start()
        pltpu.make_async_copy(sm_ref, smo_ref.at[j], local_sems.at[na]).start()
        sends = []
        for a in range(na):
            half = shards[a].shape[0] // 2
            mine = pl.ds(c * half, half)
            for k, (px, py) in enumerate(chips):
                sends.append(_remote(s_refs[a].at[mine], o_refs[a].at[j, mine], send_sems, recv_sems, 6 * a + k, (px, py, c)))
        for k, (px, py) in enumerate(chips):
            sends.append(_remote(sm_ref, smo_ref.at[j], send_sems, recv_sems, 6 * na + k, (px, py, c)))
        for cp in sends:
            cp.start()
        for a in range(na):
            half = shards[a].shape[0] // 2
            mine = pl.ds(c * half, half)
            for k, (px, py) in enumerate(chips):
                rows = o_refs[a].at[2 * px + py, mine]
                _remote(rows, rows, send_sems, recv_sems, 6 * a + k, (px, py, c)).wait_recv()
                fw = _remote(rows, rows, send_sems, recv_sems, 6 * a + 3 + k, sibling)
                fw.start()
                sends.append(fw)
        for a in range(na):
            half = shards[a].shape[0] // 2
            theirs = pl.ds((1 - c) * half, half)
            for k, (px, py) in enumerate(chips):
                rows = o_refs[a].at[2 * px + py, theirs]
                _remote(rows, rows, send_sems, recv_sems, 6 * a + 3 + k, sibling).wait_recv()
        for k, (px, py) in enumerate(chips):
            slot = smo_ref.at[2 * px + py]
            _remote(slot, slot, send_sems, recv_sems, 6 * na + k, (px, py, c)).wait_recv()
        for cp in sends:
            cp.wait_send()
        for a in range(na):
            pltpu.make_async_copy(s_refs[a], o_refs[a].at[j], local_sems.at[a]).wait()
        pltpu.make_async_copy(sm_ref, smo_ref.at[j], local_sems.at[na]).wait()

    nsem = 6 * na + 3
    outs = pl.pallas_call(
        body, name=name, in_specs=[ANY] * (na + 1), out_specs=[ANY] * (na + 1),
        out_shape=[SDS((N_CHIPS,) + s.shape, s.dtype) for s in shards] + [SDS((N_CHIPS,) + small.shape, small.dtype)],
        scratch_shapes=[pltpu.SemaphoreType.DMA((nsem,)), pltpu.SemaphoreType.DMA((nsem,)),
                        pltpu.SemaphoreType.DMA((na + 1,))])(*shards, small)
    return outs[:na], outs[na]


class Exchange:
    def __init__(self, ins, out_shapes, scratch, start, wait):
        self.ins, self.out_shapes, self.scratch, self.start, self.wait = list(ins), list(out_shapes), list(scratch), start, wait


def run_exchange(ex, *, name):
    ni, no = len(ex.ins), len(ex.out_shapes)

    def body(*refs):
        parts = refs[:ni], refs[ni:ni + no], refs[ni + no:]
        ex.start(*parts)
        ex.wait(*parts)

    return pl.pallas_call(body, name=name, in_specs=[ANY] * ni, out_specs=[ANY] * no, out_shape=ex.out_shapes,
                          scratch_shapes=ex.scratch)(*ex.ins)


def call_hosting(body, ex, *, name, grid, in_specs, out_specs, out_shape, inputs, aliases, scratch=()):
    n_in, n_out, ni, no, ns = len(inputs), len(out_shape), len(ex.ins), len(ex.out_shapes), len(scratch)
    outs_at = n_in + ni
    scr_at = outs_at + n_out + no

    def wrapped(*refs):
        own = refs[:n_in] + refs[outs_at:outs_at + n_out] + refs[scr_at:scr_at + ns]
        parts = refs[n_in:outs_at], refs[outs_at + n_out:scr_at], refs[scr_at + ns:]
        ids = [pl.program_id(d) for d in range(len(grid))]
        first = functools.reduce(jnp.logical_and, [i == 0 for i in ids])
        last = functools.reduce(jnp.logical_and, [i == g - 1 for i, g in zip(ids, grid)])

        @pl.when(first)
        def _():
            ex.start(*parts)

        body(*own)

        @pl.when(last)
        def _():
            ex.wait(*parts)

    outs = pl.pallas_call(
        wrapped, name=name, grid=grid, in_specs=list(in_specs) + [ANY] * ni, out_specs=list(out_specs) + [ANY] * no,
        out_shape=list(out_shape) + ex.out_shapes, input_output_aliases=aliases,
        scratch_shapes=list(scratch) + ex.scratch,
        compiler_params=_cp(*["arbitrary"] * len(grid)))(*inputs, *ex.ins)
    return outs[:n_out], outs[n_out:]


def allgather_ici_exchange(shards):
    na = len(shards)

    def copies(s_refs, o_refs, sems):
        send_sems, recv_sems, _ = sems
        x, y, c = _coords()
        j = 2 * x + y
        out = []
        for a in range(na):
            half = shards[a].shape[0] // 2
            mine = pl.ds(c * half, half)
            for k, (px, py) in enumerate(_other_chips(x, y)):
                send = _remote(s_refs[a].at[mine], o_refs[a].at[j, mine], send_sems, recv_sems, 3 * a + k, (px, py, c))
                rows = o_refs[a].at[2 * px + py, mine]
                out.append((send, _remote(rows, rows, send_sems, recv_sems, 3 * a + k, (px, py, c))))
        return out

    def start(s_refs, o_refs, sems):
        x, y, c = _coords()
        j = 2 * x + y
        for a in range(na):
            chunk = shards[a].shape[0] // LOCAL_CHUNKS
            for q in range(LOCAL_CHUNKS):
                rows = pl.ds(q * chunk, chunk)
                pltpu.make_async_copy(s_refs[a].at[rows], o_refs[a].at[j, rows], sems[2].at[a]).start()
        for send, _ in copies(s_refs, o_refs, sems):
            send.start()

    def wait(s_refs, o_refs, sems):
        x, y, c = _coords()
        j = 2 * x + y
        for send, arrival in copies(s_refs, o_refs, sems):
            arrival.wait_recv()
            send.wait_send()
        for a in range(na):
            pltpu.make_async_copy(s_refs[a], o_refs[a].at[j], sems[2].at[a]).wait()

    return Exchange(shards, [SDS((N_CHIPS,) + s.shape, s.dtype) for s in shards],
                    [pltpu.SemaphoreType.DMA((3 * na,)), pltpu.SemaphoreType.DMA((3 * na,)), pltpu.SemaphoreType.DMA((na,))],
                    start, wait)


def allgather_forward(gathered, *, name):
    na = len(gathered)

    def body(*refs):
        o_refs = refs[na:2 * na]
        send_sems, recv_sems = refs[2 * na:]
        x, y, c = _coords()
        sibling = (x, y, 1 - c)
        cps = []
        for a in range(na):
            half = gathered[a].shape[1] // 2
            for k, (px, py) in enumerate(_other_chips(x, y)):
                mine = o_refs[a].at[2 * px + py, pl.ds(c * half, half)]
                theirs = o_refs[a].at[2 * px + py, pl.ds((1 - c) * half, half)]
                cps.append((_remote(mine, mine, send_sems, recv_sems, 3 * a + k, sibling),
                            _remote(theirs, theirs, send_sems, recv_sems, 3 * a + k, sibling)))
        for send, _ in cps:
            send.start()
        for send, arrival in cps:
            send.wait_send()
            arrival.wait_recv()

    return pl.pallas_call(body, name=name, in_specs=[ANY] * na, out_specs=[ANY] * na,
                          out_shape=[SDS(g.shape, g.dtype) for g in gathered],
                          input_output_aliases={a: a for a in range(na)},
                          scratch_shapes=[pltpu.SemaphoreType.DMA((3 * na,)), pltpu.SemaphoreType.DMA((3 * na,))])(*gathered)


def swap_halves_exchange(gs):
    na = len(gs)

    def copies(g_refs, o_refs, sems):
        x, y, c = _coords()
        out = []
        for a in range(na):
            half = gs[a].shape[1] // 2
            out.append(_remote(g_refs[a].at[:, pl.ds((1 - c) * half, half), :], o_refs[a], sems[0], sems[1], a,
                               (x, y, 1 - c)))
        return out

    def start(g_refs, o_refs, sems):
        for cp in copies(g_refs, o_refs, sems):
            cp.start()

    def wait(g_refs, o_refs, sems):
        for cp in copies(g_refs, o_refs, sems):
            cp.wait()

    return Exchange(gs, [SDS((g.shape[0], g.shape[1] // 2, g.shape[2]), g.dtype) for g in gs],
                    [pltpu.SemaphoreType.DMA((na,)), pltpu.SemaphoreType.DMA((na,))], start, wait)


def chip_partials_exchange(pbs):
    na = len(pbs)

    def copies(p_refs, o_refs, sems):
        x, y, c = _coords()
        out = []
        for a in range(na):
            for k, (px, py) in enumerate(_other_chips(x, y)):
                out.append(_remote(p_refs[a].at[2 * px + py], o_refs[a].at[k], sems[0], sems[1], 3 * a + k, (px, py, c)))
        return out

    def start(p_refs, o_refs, sems):
        for cp in copies(p_refs, o_refs, sems):
            cp.start()

    def wait(p_refs, o_refs, sems):
        for cp in copies(p_refs, o_refs, sems):
            cp.wait()

    return Exchange(pbs, [SDS((3,) + p.shape[1:], p.dtype) for p in pbs],
                    [pltpu.SemaphoreType.DMA((3 * na,)), pltpu.SemaphoreType.DMA((3 * na,))], start, wait)


def add_sibling_half(g, land, c_idx, *, name):
    n, R, C = g.shape
    half = R // 2
    tr = _pick(half, 256)
    nt = half // tr

    def body(c_ref, g_ref, l_ref, of_ref, ob_ref):
        s = g_ref[...] + l_ref[...].astype(F32)
        of_ref[...] = s
        ob_ref[...] = s.astype(ob_ref.dtype)

    blk = pl.BlockSpec((1, tr, C), lambda s, i, c_ref: (s, i, 0))
    gblk = pl.BlockSpec((1, tr, C), lambda s, i, c_ref: (s, c_ref[0] * nt + i, 0))
    return pl.pallas_call(
        body, name=name,
        grid_spec=pltpu.PrefetchScalarGridSpec(num_scalar_prefetch=1, grid=(n, nt), in_specs=[gblk, blk],
                                               out_specs=[blk, blk]),
        out_shape=[SDS((n, half, C), F32), SDS((n, half, C), WIRE_DTYPE)],
        compiler_params=_cp("parallel", "parallel"))(c_idx, g, land)


def add_chip_partials(pf, rb, jc_idx, *, name):
    n, H, C = pf.shape
    tr = _pick(H, 256)

    def body(jc_ref, p_ref, r_ref, o_ref):
        s = p_ref[0]
        for k in range(3):
            s = s + r_ref[k].astype(F32)
        o_ref[...] = s

    pblk = pl.BlockSpec((1, tr, C), lambda i, jc_ref: (jc_ref[0], i, 0))
    rblk = pl.BlockSpec((3, tr, C), lambda i, jc_ref: (0, i, 0))
    oblk = pl.BlockSpec((None, tr, C), lambda i, jc_ref: (jc_ref[1], i, 0))
    return pl.pallas_call(
        body, name=name,
        grid_spec=pltpu.PrefetchScalarGridSpec(num_scalar_prefetch=1, grid=(H // tr,), in_specs=[pblk, rblk],
                                               out_specs=oblk),
        out_shape=SDS((2, H, C), F32), compiler_params=_cp("parallel"))(jc_idx, pf, rb)


def join_sibling_halves(bufs, *, name):
    na = len(bufs)

    def body(*refs):
        o_refs = refs[na:2 * na]
        send_sems, recv_sems = refs[2 * na:]
        x, y, c = _coords()
        cps = [_remote(o_refs[a].at[c], o_refs[a].at[c], send_sems, recv_sems, a, (x, y, 1 - c)) for a in range(na)]
        for cp in cps:
            cp.start()
        for a in range(na):
            cps[a].wait_send()
            _remote(o_refs[a].at[1 - c], o_refs[a].at[1 - c], send_sems, recv_sems, a, (x, y, 1 - c)).wait_recv()

    return pl.pallas_call(body, name=name, in_specs=[ANY] * na, out_specs=[ANY] * na,
                          out_shape=[SDS(b.shape, b.dtype) for b in bufs],
                          input_output_aliases={a: a for a in range(na)},
                          scratch_shapes=[pltpu.SemaphoreType.DMA((na,)), pltpu.SemaphoreType.DMA((na,))])(*bufs)


def exchange_pieces(v, *, scatter, name):
    P, C = v.shape[-2:]

    def body(v_ref, o_ref, send_sems, recv_sems, local_sem):
        x, y, c = _coords()
        me = 4 * x + 2 * y + c
        local = pltpu.make_async_copy(v_ref.at[me] if scatter else v_ref, o_ref.at[me], local_sem)
        local.start()
        cps = []
        for m in range(1, N_DEV):
            px = (1 - x) if m & 4 else x
            py = (1 - y) if m & 2 else y
            pc = (1 - c) if m & 1 else c
            src = v_ref.at[4 * px + 2 * py + pc] if scatter else v_ref
            cps.append(_remote(src, o_ref.at[me], send_sems, recv_sems, m - 1, (px, py, pc)))
        for cp in cps:
            cp.start()
        for cp in cps:
            cp.wait_send()
        for m in range(1, N_DEV):
            px = (1 - x) if m & 4 else x
            py = (1 - y) if m & 2 else y
            pc = (1 - c) if m & 1 else c
            slot = o_ref.at[4 * px + 2 * py + pc]
            _remote(slot, slot, send_sems, recv_sems, m - 1, (px, py, pc)).wait_recv()
        local.wait()

    return pl.pallas_call(body, name=name, in_specs=[ANY], out_specs=ANY, out_shape=SDS((N_DEV, P, C), v.dtype),
                          scratch_shapes=[pltpu.SemaphoreType.DMA((N_DEV - 1,)), pltpu.SemaphoreType.DMA((N_DEV - 1,)),
                                          pltpu.SemaphoreType.DMA(())])(v)


def sum_pieces(land, *, name):
    n, P, C = land.shape

    def body(l_ref, o_ref):
        s = l_ref[0]
        for d in range(1, n):
            s = s + l_ref[d]
        o_ref[...] = s

    return pl.pallas_call(body, name=name, out_shape=SDS((P, C), F32))(land)


BIG_SEGS = (
    ("w_in_even", (1024, 514), 1),
    ("s5_w_glu", (128, 512), 0),
    ("w_out_even", (256, 1024), 0),
    ("w_in_odd", (1024, 384), 1),
    ("w_out_odd", (256, 1024), 0),
    ("mlp_w1", (2, 1024, 1024), 2),
    ("mlp_w2", (2, 1024, 1024), 1),
)
BIG_NAMES = tuple(n for n, _, _ in BIG_SEGS)
EARLY_NAMES = ("w_in_even", "s5_w_glu")
LATE_NAMES = ("w_out_even", "w_in_odd", "w_out_odd", "mlp_w1", "mlp_w2")
REDUCED_EARLY = ("s5_w_glu", "w_out_even", "w_in_odd", "w_out_odd", "mlp_w1", "mlp_w2")
SHARDED_SMALL = ("pool_scale", "sgu_ln_g", "sgu_ln_b")
SMALL_SEGS = (
    ("mix_pre_g", (2, 1024)), ("mix_post_g", (2, 1024)), ("mlp_pre_g", (2, 1024)), ("mlp_post_g", (2, 1024)),
    ("s5_lam_re", (1, 32, 64)), ("s5_lam_im", (1, 32, 64)), ("s5_log_dt", (1, 32)),
    ("s5_b_re", (1, 32, 64, 16)), ("s5_b_im", (1, 32, 64, 16)), ("s5_c_re", (1, 32, 16, 64)), ("s5_c_im", (1, 32, 16, 64)),
    ("s5_d", (1, 512)), ("fox_b_f", (1, 8)), ("pool_w", (1, 4, 128, 128)), ("sgu_w_s", (1, 4, 128, 128)),
    ("sgu_b_s", (1, 4, 128)),
)
REDUCED_SEGS = SMALL_SEGS + tuple((n, (1, 512)) for n in SHARDED_SMALL) + (("loss", (1, 1)),)


def _cols_from_chips(g):
    n, R, C = g.shape
    return jnp.transpose(g, (1, 0, 2)).reshape(R, n * C)


def _chips_from_cols(m):
    R, C4 = m.shape
    return jnp.transpose(m.reshape(R, N_CHIPS, C4 // N_CHIPS), (1, 0, 2))


MLP_SHARD = 1024


def _w1_cols(l):
    def spec(tm, tn, tk):
        per = MLP_SHARD // tn
        return pl.BlockSpec((None, tk, tn), lambda i, j, k: (j // per, l * (MLP_SHARD // tk) + k, j % per))
    return spec


def _w1_rows_t(l):
    def spec(tm, tn, tk):
        if tk == N_CHIPS * MLP_SHARD:
            return pl.BlockSpec((N_CHIPS, tn, MLP_SHARD), lambda i, j, k: (0, l * (MLP_SHARD // tn) + j, 0))
        per = MLP_SHARD // tk
        return pl.BlockSpec((None, tn, tk), lambda i, j, k: (k // per, l * (MLP_SHARD // tn) + j, k % per))
    return spec


def _w2_rows(l):
    def spec(tm, tn, tk):
        if tk == N_CHIPS * MLP_SHARD:
            return pl.BlockSpec((N_CHIPS, MLP_SHARD, tn), lambda i, j, k: (0, l, j))
        per = MLP_SHARD // tk
        return pl.BlockSpec((None, tk, tn), lambda i, j, k: (k // per, l * per + k % per, j))
    return spec


def _w2_rows_t(l):
    def spec(tm, tn, tk):
        per = MLP_SHARD // tn
        return pl.BlockSpec((None, tn, tk), lambda i, j, k: (j // per, l * per + j % per, k))
    return spec


def _dw1_out(l):
    def spec(tm, tn, tk):
        per = MLP_SHARD // tn
        return pl.BlockSpec((None, tm, tn), lambda i, j, k: (j // per, l * (MLP_SHARD // tm) + i, j % per))
    return spec


def _dw2_out(l):
    def spec(tm, tn, tk):
        per = MLP_SHARD // tm
        return pl.BlockSpec((None, tm, tn), lambda i, j, k: (i // per, l * per + i % per, j))
    return spec


def _pack_vec(d, segs, rows_multiple):
    flat = jnp.concatenate([d[n].reshape(-1) for n, _ in segs])
    rows = -(-flat.shape[0] // LANES)
    rows = -(-rows // rows_multiple) * rows_multiple
    return jnp.pad(flat, (0, rows * LANES - flat.shape[0])).reshape(rows, LANES)


def _unpack_vec(v, segs):
    flat, out, r = v.reshape(-1), {}, 0
    for n, shape in segs:
        k = math.prod(shape)
        out[n] = flat[r:r + k].reshape(shape)
        r += k
    return out


def _block_diag(blocks):
    G, a, b = blocks.shape
    eye = jnp.eye(G, dtype=blocks.dtype)
    return (eye[:, None, :, None] * blocks[:, :, None, :]).reshape(G * a, G * b)


def _diag_blocks(m, G):
    a, b = m.shape[0] // G, m.shape[1] // G
    return jnp.stack([m[g * a:(g + 1) * a, g * b:(g + 1) * b] for g in range(G)])


def _sqrelu_epi(acc):
    r = jnp.maximum(acc, 0.0)
    return acc, r * r


def _sqrelu_bwd_epi(acc, a):
    return (acc * (2.0 * jnp.maximum(a.astype(F32), 0.0)),)


def _mlp_fwd(h, g1, g2, l, tag):
    T, D = h.shape
    a, s = matmul(h, g1, name=f"{tag}_up", mnk=(T, D_FF, D), b_spec=_w1_cols(l), epi=_sqrelu_epi,
                  out_dtypes=(MXU_DTYPE, MXU_DTYPE))
    m = matmul(s, g2, name=f"{tag}_down", mnk=(T, D, D_FF), b_spec=_w2_rows(l))
    return m, (h, a, s)


def _mlp_bwd(saved, dm, g1, g2, l, dg1, dg2, tag):
    h, a, s = saved
    T, D = h.shape
    gshape = (N_CHIPS, 2 * MLP_SHARD, MLP_SHARD)
    da = matmul(dm, g2, tb=True, name=f"{tag}_down_dx", mnk=(T, D_FF, D), b_spec=_w2_rows_t(l),
                epi=_sqrelu_bwd_epi, epi_in=(a,), out_dtype=MXU_DTYPE)
    dg2 = matmul(s, dm, ta=True, name=f"{tag}_down_dw", tm=MLP_SHARD, o_spec=_dw2_out(l), o_shape=gshape, prev=dg2)
    dh = matmul(da, g1, tb=True, name=f"{tag}_up_dx", mnk=(T, D, D_FF), b_spec=_w1_rows_t(l))
    dg1 = matmul(h, da, ta=True, name=f"{tag}_up_dw", o_spec=_dw1_out(l), o_shape=gshape, prev=dg1)
    return dh, dg1, dg2


def kernel(x, mix_pre_g, mix_post_g, mlp_pre_g, mlp_post_g, w_in_even, s5_lam_re, s5_lam_im, s5_log_dt, s5_b_re, s5_b_im, s5_c_re, s5_c_im, s5_d, s5_w_glu, fox_b_f, w_out_even, w_in_odd, pool_w, pool_scale, sgu_ln_g, sgu_ln_b, sgu_w_s, sgu_b_s, w_out_odd, mlp_w1, mlp_w2, loss_target, m_mix_pre_g, m_mix_post_g, m_mlp_pre_g, m_mlp_post_g, m_w_in_even, m_s5_lam_re, m_s5_lam_im, m_s5_log_dt, m_s5_b_re, m_s5_b_im, m_s5_c_re, m_s5_c_im, m_s5_d, m_s5_w_glu, m_fox_b_f, m_w_out_even, m_w_in_odd, m_pool_w, m_pool_scale, m_sgu_ln_g, m_sgu_ln_b, m_sgu_w_s, m_sgu_b_s, m_w_out_odd, m_mlp_w1, m_mlp_w2, v_mix_pre_g, v_mix_post_g, v_mlp_pre_g, v_mlp_post_g, v_w_in_even, v_s5_lam_re, v_s5_lam_im, v_s5_log_dt, v_s5_b_re, v_s5_b_im, v_s5_c_re, v_s5_c_im, v_s5_d, v_s5_w_glu, v_fox_b_f, v_w_out_even, v_w_in_odd, v_pool_w, v_pool_scale, v_sgu_ln_g, v_sgu_ln_b, v_sgu_w_s, v_sgu_b_s, v_w_out_odd, v_mlp_w1, v_mlp_w2):
    names = [n for n, _ in SMALL_SEGS] + [n for n, _, _ in BIG_SEGS] + list(SHARDED_SMALL)
    env = dict(locals())
    W = {n: env[n] for n in names}
    M = {n: env["m_" + n] for n in names}
    V = {n: env["v_" + n] for n in names}

    def shard(n):
        return W[n].reshape(-1, W[n].shape[-1]).astype(WIRE_DTYPE)

    small = jnp.pad(jnp.concatenate([W[n] for n in SHARDED_SMALL]), ((0, SUBLANES - len(SHARDED_SMALL)), (0, 0)))
    gathered, small_all = allgather_chip_shards([shard(n) for n in EARLY_NAMES], small, name="allgather_weights")
    Wf = dict(zip(EARLY_NAMES, gathered))
    for i, n in enumerate(SHARDED_SMALL):
        Wf[n] = small_all[:, i, :].reshape(1, N_CHIPS * LANES)
    for n, _ in SMALL_SEGS:
        Wf[n] = W[n]

    loss8, dx0, halves, local_small = _local_step(x[0], loss_target[0], Wf, [shard(n) for n in LATE_NAMES])
    return _reduce_and_update(W, M, V, loss8, dx0, halves, local_small)


def _reduce_to_my_half(gs, names, tag, carry_swap=None, carry_ici=None):
    cx, cy, cc = _coords()
    c_idx = cc.reshape(1).astype(jnp.int32)
    jc_idx = jnp.stack([2 * cx + cy, cc]).astype(jnp.int32)
    swap = swap_halves_exchange(gs)
    from_sibling = carry_swap(swap) if carry_swap else run_exchange(swap, name=f"{tag}_to_sibling")
    sums = [add_sibling_half(g, l, c_idx, name=f"{tag}_chip_sum_{n}") for n, g, l in zip(names, gs, from_sibling)]
    send = chip_partials_exchange([pb for _, pb in sums])
    from_chips = carry_ici(send) if carry_ici else run_exchange(send, name=f"{tag}_to_chips")
    return [add_chip_partials(pf, r, jc_idx, name=f"{tag}_sum_{n}") for n, (pf, _), r in zip(names, sums, from_chips)]


def _local_step(x0, target, P, late_shards):
    T = x0.shape[0]
    mix_pre_g, mix_post_g, mlp_pre_g, mlp_post_g = P["mix_pre_g"], P["mix_post_g"], P["mlp_pre_g"], P["mlp_post_g"]
    s5_lam_re, s5_lam_im, s5_log_dt = P["s5_lam_re"], P["s5_lam_im"], P["s5_log_dt"]
    s5_b_re, s5_b_im, s5_c_re, s5_c_im, s5_d = P["s5_b_re"], P["s5_b_im"], P["s5_c_re"], P["s5_c_im"], P["s5_d"]
    fox_b_f, pool_w, sgu_w_s, sgu_b_s = P["fox_b_f"], P["pool_w"], P["sgu_w_s"], P["sgu_b_s"]
    pool_scale_f, ln_g_f, ln_b_f = P["pool_scale"], P["sgu_ln_g"], P["sgu_ln_b"]
    w_in_e = jnp.pad(_cols_from_chips(P["w_in_even"]), ((0, 0), (0, EVEN_IN_PAD - EVEN_IN)))
    w_glu = P["s5_w_glu"].reshape(S5_WIDTH, S5_WIDTH)

    def gain(a, l):
        return a[l][None, :]

    lr = s5_lam_re[0].reshape(1, S5_LANES)
    li = s5_lam_im[0].reshape(1, S5_LANES)
    ldt = jnp.repeat(s5_log_dt[0], S5_STATE).reshape(1, S5_LANES)
    btr = s5_b_re[0].reshape(S5_LANES, S5_GROUP).T
    bti = s5_b_im[0].reshape(S5_LANES, S5_GROUP).T
    tf_re, tf_im, tb_re, tb_im, bbt_re, bbt_im = s5_disc_fwd(lr, li, ldt, btr, bti, name="s5_disc")
    same_group = (jnp.arange(S5_WIDTH)[:, None] // S5_GROUP) == (jnp.arange(S5_LANES)[None, :] // S5_STATE)
    b_bd = s5_interleave(jnp.where(same_group, jnp.tile(bbt_re, (S5_GROUPS, 1)), 0.0),
                         jnp.where(same_group, jnp.tile(bbt_im, (S5_GROUPS, 1)), 0.0), axis=1)
    cr2 = jnp.transpose(s5_c_re[0], (0, 2, 1)).reshape(S5_LANES, S5_GROUP)
    ci2 = jnp.transpose(s5_c_im[0], (0, 2, 1)).reshape(S5_LANES, S5_GROUP)
    c_bd = s5_interleave(jnp.where(same_group.T, jnp.tile(cr2, (1, S5_GROUPS)), 0.0),
                         -jnp.where(same_group.T, jnp.tile(ci2, (1, S5_GROUPS)), 0.0), axis=0)
    bf_pad = jnp.pad(fox_b_f, ((0, 0), (0, LANES - FOX_HEADS)))

    h1 = rms_fwd(x0, gain(mix_pre_g, 0), name="l0_pre_norm")
    z = matmul(h1, w_in_e, name="l0_in_proj")
    s5_tiles = dict(tm=_pick(T, S5_NB), exact_tiles=True)
    xs = s5_scan(z, b_bd, tf_re, tf_im, reverse=False, name="s5_scan_fwd")
    yc = matmul(xs, c_bd, mnk=(T, S5_WIDTH, 2 * S5_NB), tn=S5_CB, a_spec=_lanes_of_chan, b_spec=_s5_c_block,
                name="s5_cx", **s5_tiles)
    yl, yg = s5_out_fwd(yc, z, s5_d, name="s5_out")
    gl = matmul(yg, w_glu, name="s5_glu_proj")
    ycat = glu_fwd(yg, gl, out_cols=D_MODEL, name="s5_glu")
    fgate = fox_gate_fwd(z, bf_pad, fl_col=FL_TILE, name="fox_gate")
    f_col = _pairs_col(fgate, T)
    f_row = _col_to_row(f_col, T)
    (ycat, lse_col), late = fox_fwd(z, f_col, f_row, ycat, allgather_ici_exchange(late_shards), name="fox_fwd")
    late = dict(zip(LATE_NAMES, allgather_forward(late, name="allgather_late_weights")))
    w_in_o = _cols_from_chips(late["w_in_odd"])
    w_in_o = jnp.concatenate([w_in_o[:, S5_WIDTH:], w_in_o[:, :S5_WIDTH]], axis=1)
    w_out_e = late["w_out_even"].reshape(D_MODEL, D_MODEL)
    w_out_o = late["w_out_odd"].reshape(D_MODEL, D_MODEL)
    g1, g2 = late["mlp_w1"], late["mlp_w2"]
    mo = matmul(ycat, w_out_e, name="l0_out_proj")
    x1, h2 = res_norm_fwd(x0, mo, gain(mix_post_g, 0), gain(mlp_pre_g, 0), name="l0_post_mlp0_pre_norm")
    m0, mlp0 = _mlp_fwd(h2, g1, g2, 0, "mlp0")

    x2, h3 = res_norm_fwd(x1, m0, gain(mlp_post_g, 0), gain(mix_pre_g, 1), name="mlp0_post_l1_pre_norm")
    z2 = matmul(h3, w_in_o, name="l1_in_proj")
    pooled = pool_window(z2, adjoint=False, in_col=POOL_COL, out_dtype=MXU_DTYPE, name="pool_fwd")
    pw_bd = _block_diag(pool_w[0])
    pw = matmul(pooled, pw_bd, name="pool_proj")
    ycat2 = colscale_fwd(pw, pool_scale_f, out_cols=D_MODEL, name="pool_scale")
    causal = jnp.tril(jnp.ones((CHUNK, CHUNK), dtype=bool))
    wsm = jnp.where(causal[None], sgu_w_s[0], 0.0)
    wsmt = jnp.transpose(wsm, (0, 2, 1))
    bst = sgu_b_s[0].T
    ycat2 = sgu_fwd(z2, ln_g_f, ln_b_f, wsm, bst, ycat2, name="sgu_fwd")
    mo2 = matmul(ycat2, w_out_o, name="l1_out_proj")
    x3, h4 = res_norm_fwd(x2, mo2, gain(mix_post_g, 1), gain(mlp_pre_g, 1), name="l1_post_mlp1_pre_norm")
    m1, mlp1 = _mlp_fwd(h4, g1, g2, 1, "mlp1")
    loss8, dx4 = res_norm_loss(x3, m1, gain(mlp_post_g, 1), target, name="mlp1_post_norm_loss")

    dm1, dg_mlp_post1 = rms_bwd(m1, gain(mlp_post_g, 1), dx4, None, name="mlp1_post_norm_bwd")
    dh4, dg1, dg2 = _mlp_bwd(mlp1, dm1, g1, g2, 1, None, None, "mlp1")
    dx3, dmo2, dg_mlp_pre1, dg_mix_post1 = norm_res_bwd(x3, gain(mlp_pre_g, 1), dh4, dx4, mo2, gain(mix_post_g, 1),
                                                        name="mlp1_pre_l1_post_norm_bwd")
    dycat2 = matmul(dmo2, w_out_o, tb=True, name="l1_out_proj_dx")
    dw_out_o = matmul(ycat2, dmo2, ta=True, name="l1_out_proj_dw")
    dpw, dpool_scale = colscale_bwd(pw, pool_scale_f, dycat2, name="pool_scale_bwd")
    dpooled = matmul(dpw, pw_bd, tb=True, name="pool_proj_dx")
    dpw_bd = matmul(pooled, dpw, ta=True, name="pool_proj_dw")
    dz2, dln_g, dln_b, dws, dbst = sgu_bwd(z2, ln_g_f, ln_b_f, wsm, wsmt, bst, dycat2, out_cols=3 * S5_WIDTH,
                                           name="sgu_bwd")
    dz2 = pool_window(dpooled, adjoint=True, into=dz2, out_col=POOL_COL, name="pool_bwd")
    dh3 = matmul(dz2, w_in_o, tb=True, name="l1_in_proj_dx")
    dw_in_o = matmul(h3, dz2, ta=True, name="l1_in_proj_dw")
    dw_in_o = jnp.concatenate([dw_in_o[:, 2 * S5_WIDTH:], dw_in_o[:, :2 * S5_WIDTH]], axis=1)
    dx2, dm0, dg_mix_pre1, dg_mlp_post0 = norm_res_bwd(x2, gain(mix_pre_g, 1), dh3, dx3, m0, gain(mlp_post_g, 0),
                                                       name="l1_pre_mlp0_post_norm_bwd")

    dh2, dg1, dg2 = _mlp_bwd(mlp0, dm0, g1, g2, 0, dg1, dg2, "mlp0")
    dx1, dmo, dg_mlp_pre0, dg_mix_post0 = norm_res_bwd(x1, gain(mlp_pre_g, 0), dh2, dx2, mo, gain(mix_post_g, 0),
                                                       name="mlp0_pre_l0_post_norm_bwd")
    dycat = matmul(dmo, w_out_e, tb=True, name="l0_out_proj_dx")
    dw_out_e = matmul(ycat, dmo, ta=True, name="l0_out_proj_dw")
    dyg_a, dgl = glu_bwd(yg, gl, dycat, name="s5_glu_bwd")
    dyg_b = matmul(dgl, w_glu, tb=True, name="s5_glu_proj_dx")
    dw_glu = matmul(yg, dgl, ta=True, name="s5_glu_proj_dw")
    dyl, du_skip, dd = s5_out_bwd(yl, z, s5_d, dyg_a, dyg_b, name="s5_out_bwd")
    dc_blocks = matmul(xs, dyl, ta=True, mnk=(2 * S5_LANES, S5_CB, T), tm=S5_NB, tn=S5_CB, b_spec=_chan_cols_of_i,
                       exact_tiles=True, name="s5_cx_dw")
    early_grads = {"s5_w_glu": dw_glu.reshape(N_CHIPS, -1, S5_WIDTH), "w_out_even": dw_out_e.reshape(N_CHIPS, -1, D_MODEL),
                   "w_in_odd": _chips_from_cols(dw_in_o), "w_out_odd": dw_out_o.reshape(N_CHIPS, -1, D_MODEL),
                   "mlp_w1": dg1, "mlp_w2": dg2}
    got = {}

    def reverse_scan(exchange):
        (got["lam"], got["dab_re"], got["dab_im"]), bufs = s5_scan(dyl, c_bd, tb_re, tb_im, reverse=True, states=xs,
                                                                   hosted=exchange, name="s5_scan_bwd")
        return bufs

    def attention_bwd(exchange):
        dd_col = fox_dd(ycat, dycat, name="fox_dd")
        (got["dk"], got["dv"], got["dfk"], got["dqt"], got["dfq"]), bufs = fox_bwd(
            z, dycat, f_col, f_row, _col_to_row(lse_col, T), _col_to_row(dd_col, T), exchange, name="fox_bwd")
        return bufs

    halves = _reduce_to_my_half([early_grads[n] for n in REDUCED_EARLY], REDUCED_EARLY, "early_grads",
                                reverse_scan, attention_bwd)
    lam, dab_re, dab_im, dk, dv = got["lam"], got["dab_re"], got["dab_im"], got["dk"], got["dv"]
    db_blocks = matmul(z, lam, ta=True, mnk=(S5_CB, 2 * S5_LANES, T), tm=S5_CB, tn=S5_NB, a_spec=_chan_rows_t,
                       exact_tiles=True, name="s5_bu_dw")
    du_b = matmul(lam, b_bd, tb=True, mnk=(T, S5_WIDTH, 2 * S5_NB), tn=S5_CB, a_spec=_lanes_of_chan,
                  b_spec=_s5_b_block_t, name="s5_bu_dx", **s5_tiles)
    du = add2(du_skip, du_b, name="s5_du")
    dq = jnp.transpose(got["dqt"], (1, 3, 0, 2)).reshape(T, FOX_WIDTH) * (FOX_HEAD_DIM ** -0.5)
    dfl, dbf = fox_gate_bwd(z, bf_pad, _pairs_to_lanes(got["dfk"], T), _pairs_to_lanes(_row_to_col(got["dfq"], T), T),
                            fl_col=FL_TILE, name="fox_gate_bwd")
    dz = jnp.concatenate([du, dq, dk, dv, dfl], axis=1).astype(MXU_DTYPE)
    dw_in_e = matmul(h1, dz, ta=True, name="l0_in_proj_dw")[:, :EVEN_IN]

    def in_proj_dx(exchange):
        got["dh1"], bufs = matmul(dz, w_in_e, tb=True, hosted=exchange, name="l0_in_proj_dx")
        return bufs

    def pre_norm_bwd(exchange):
        (got["dx0"], got["dg_mix_pre0"]), bufs = rms_bwd(x0, gain(mix_pre_g, 0), got["dh1"], dx1, hosted=exchange,
                                                         name="l0_pre_norm_bwd")
        return bufs

    halves = halves + _reduce_to_my_half([_chips_from_cols(dw_in_e)], ["w_in_even"], "late_grads", in_proj_dx, pre_norm_bwd)
    dx0, dg_mix_pre0 = got["dx0"], got["dg_mix_pre0"]

    groups_per_block = S5_CB // S5_GROUP
    own_group = (jnp.arange(S5_CB)[:, None] // S5_GROUP) == ((jnp.arange(S5_LANES)[None, :] // S5_STATE) % groups_per_block)
    db_re, db_im = s5_deinterleave(db_blocks, axis=1)
    dbbt_re = jnp.where(own_group, db_re, 0.0).reshape(groups_per_block, S5_GROUP, S5_LANES).sum(0)
    dbbt_im = jnp.where(own_group, db_im, 0.0).reshape(groups_per_block, S5_GROUP, S5_LANES).sum(0)
    dlr, dli, dldt8, dbtr, dbti = s5_disc_bwd(lr, li, ldt, btr, bti, dab_re, dab_im, dbbt_re, dbbt_im, name="s5_disc_bwd")
    dc_re, dc_im = s5_deinterleave(dc_blocks, axis=0)
    dcr2 = jnp.where(own_group.T, dc_re, 0.0).reshape(S5_LANES, groups_per_block, S5_GROUP).sum(1)
    dci2 = -jnp.where(own_group.T, dc_im, 0.0).reshape(S5_LANES, groups_per_block, S5_GROUP).sum(1)

    def c_layout(a):
        return jnp.transpose(a.reshape(S5_GROUPS, S5_STATE, S5_GROUP), (0, 2, 1))[None]

    def b_layout(a):
        return a.T.reshape(1, S5_GROUPS, S5_STATE, S5_GROUP)

    local_small = {
        "mix_pre_g": jnp.concatenate([dg_mix_pre0, dg_mix_pre1]), "mix_post_g": jnp.concatenate([dg_mix_post0, dg_mix_post1]),
        "mlp_pre_g": jnp.concatenate([dg_mlp_pre0, dg_mlp_pre1]), "mlp_post_g": jnp.concatenate([dg_mlp_post0, dg_mlp_post1]),
        "s5_lam_re": dlr.reshape(1, S5_GROUPS, S5_STATE), "s5_lam_im": dli.reshape(1, S5_GROUPS, S5_STATE),
        "s5_log_dt": dldt8[0:1, 0:S5_GROUPS],
        "s5_b_re": b_layout(dbtr), "s5_b_im": b_layout(dbti), "s5_c_re": c_layout(dcr2), "s5_c_im": c_layout(dci2),
        "s5_d": dd, "fox_b_f": dbf[:, 0:FOX_HEADS],
        "pool_w": _diag_blocks(dpw_bd, len(POOL_WINDOWS))[None],
        "sgu_w_s": jnp.where(causal[None], dws, 0.0)[None], "sgu_b_s": dbst.T[None],
        "pool_scale": dpool_scale, "sgu_ln_g": dln_g, "sgu_ln_b": dln_b,
    }
    return loss8, dx0, dict(zip(REDUCED_EARLY + ("w_in_even",), halves)), local_small


def _reduce_and_update(W, M, V, loss8, dx0, halves, local_small):
    cx, cy, cc = _coords()
    chip = 2 * cx + cy

    summed = dict(local_small, loss=loss8[0:1, 0:1])
    vec = _pack_vec(summed, REDUCED_SEGS, N_DEV * SUBLANES)
    piece = vec.shape[0] // N_DEV
    landed = exchange_pieces(vec.reshape(N_DEV, piece, LANES), scatter=True, name="small_grads_scatter")
    mine = sum_pieces(landed, name="small_grads_sum")
    everyone = exchange_pieces(mine, scatter=False, name="small_grads_gather")
    G = _unpack_vec(everyone, REDUCED_SEGS)
    loss = G["loss"].reshape(())
    for n in SHARDED_SMALL:
        G[n] = lax.dynamic_slice_in_dim(G[n], chip * LANES, LANES, axis=1)

    reduced = join_sibling_halves([halves[n] for n in BIG_NAMES], name="big_grads_join")
    for n, r in zip(BIG_NAMES, reduced):
        G[n] = r.reshape(W[n].shape)

    def two_d(a):
        return a.reshape(-1, a.shape[-1])

    delta, new_m, new_v = {}, {}, {}
    for n in BIG_NAMES:
        d_, m_, v_ = adamw(two_d(W[n]), two_d(G[n]), two_d(M[n]), two_d(V[n]), name=f"adamw_{n}")
        delta[n], new_m[n], new_v[n] = (t.reshape(W[n].shape) for t in (d_, m_, v_))
    packed = [_pack_vec(src, SMALL_SEGS, SUBLANES) for src in (W, G, M, V)]
    outs = adamw(*packed, name="adamw_replicated")
    for dst, t in zip((delta, new_m, new_v), outs):
        dst.update(_unpack_vec(t, SMALL_SEGS))
    sharded_segs = tuple((n, (1, LANES)) for n in SHARDED_SMALL)
    packed = [_pack_vec(src, sharded_segs, 1) for src in (W, G, M, V)]
    outs = adamw(*packed, name="adamw_sharded_vectors")
    for dst, t in zip((delta, new_m, new_v), outs):
        dst.update(_unpack_vec(t, sharded_segs))

    order = ["mix_pre_g", "mix_post_g", "mlp_pre_g", "mlp_post_g", "w_in_even", "s5_lam_re", "s5_lam_im", "s5_log_dt",
             "s5_b_re", "s5_b_im", "s5_c_re", "s5_c_im", "s5_d", "s5_w_glu", "fox_b_f", "w_out_even", "w_in_odd",
             "pool_w", "pool_scale", "sgu_ln_g", "sgu_ln_b", "sgu_w_s", "sgu_b_s", "w_out_odd", "mlp_w1", "mlp_w2"]
    return (loss, dx0[None], *[G[n] for n in order], *[delta[n] for n in order],
            *[new_m[n] for n in order], *[new_v[n] for n in order])
```

```python
import functools
import math

import jax
import jax.numpy as jnp
from jax import lax
from jax.experimental import pallas as pl
from jax.experimental.pallas import tpu as pltpu

F32 = jnp.float32
MXU_DTYPE = jnp.bfloat16
WIRE_DTYPE = jnp.bfloat16
EPS = 1e-6
VMEM_LIMIT_BYTES = 48 * 1024 * 1024
LANES = 128
SUBLANES = 8

D_MODEL = 1024
S5_WIDTH = 512
S5_GROUP = 16
S5_GROUPS = 32
S5_STATE = 64
S5_LANES = S5_GROUPS * S5_STATE
FOX_HEADS = 8
FOX_HEAD_DIM = 64
FOX_WIDTH = 512
EVEN_IN = S5_WIDTH + 3 * FOX_WIDTH + FOX_HEADS
EVEN_IN_PAD = 2176
POOL_WINDOWS = (2, 4, 8, 16)
POOL_HALO = 16
POOL_GROUP_DIM = 128
SGU_GROUPS = 4
SGU_GROUP_DIM = 128
CHUNK = 128
D_FF = 4096

ADAM_LR = 0.001
ADAM_B1 = 0.9
ADAM_B2 = 0.999
ADAM_EPS = 1e-08
ADAM_WD = 0.01
ADAM_STEP = 10

MESH_AXES = ("x", "y", "c")
MESH = pl.DeviceIdType.MESH
N_CHIPS = 4
N_DEV = 8

SDS = jax.ShapeDtypeStruct


def _cp(*sem):
    return pltpu.CompilerParams(dimension_semantics=sem, vmem_limit_bytes=VMEM_LIMIT_BYTES)


def _pick(dim, pref):
    if dim <= pref:
        return dim
    t = pref
    while t >= 256:
        if dim % t == 0:
            return t
        t //= 2
    return dim


def _row(tr, c):
    return pl.BlockSpec((tr, c), lambda i: (i, 0))


def _full(shape):
    nd = len(shape)
    return pl.BlockSpec(shape, lambda *_: (0,) * nd)


def _gelu_grad(x):
    c = math.sqrt(2.0 / math.pi)
    t = jnp.tanh(c * (x + 0.044715 * x * x * x))
    return 0.5 * (1.0 + t) + 0.5 * x * (1.0 - t * t) * c * (1.0 + 3.0 * 0.044715 * x * x)


MATMUL_VMEM_BYTES = 36 * 1024 * 1024


def matmul(a, b, *, name, ta=False, tb=False, out_dtype=F32, tm=2048, tn=1024, tk=4096, mnk=None, a_koff=0,
           a_spec=None, b_spec=None, o_spec=None, o_shape=None, prev=None, epi=None, epi_in=(), out_dtypes=None,
           exact_tiles=False, hosted=None):
    if mnk is None:
        M, K = (a.shape[1], a.shape[0]) if ta else a.shape
        K2, N = (b.shape[1], b.shape[0]) if tb else b.shape
        assert K == K2, (a.shape, b.shape, ta, tb)
    else:
        M, N, K = mnk
    out_dtypes = tuple(out_dtypes) if out_dtypes is not None else (out_dtype,)
    n_out, n_epi = len(out_dtypes), len(epi_in)
    tm, tn, tk = _pick(M, tm), _pick(N, tn), _pick(K, tk)

    def vmem_bytes(tm_, tn_, tk_):
        tiles = tm_ * tk_ * a.dtype.itemsize + tk_ * tn_ * b.dtype.itemsize
        tiles += tm_ * tn_ * (sum(jnp.dtype(d).itemsize for d in out_dtypes) + sum(e.dtype.itemsize for e in epi_in))
        return 2 * tiles + tm_ * tn_ * 4 * (tk_ < K)

    def halves(t, dim):
        return [t] + ([t // 2] if t % (2 * LANES) == 0 and t // 2 >= 512 and dim % (t // 2) == 0 else [])

    if exact_tiles:
        halves = lambda t, dim: [t]
    fits = [(m_, n_) for m_ in halves(tm, M) for n_ in halves(tn, N) if vmem_bytes(m_, n_, tk) <= MATMUL_VMEM_BYTES]
    if fits:
        tm, tn = max(fits, key=lambda t: (t[0] * t[1], t[0]))
    else:
        tm, tn = halves(tm, M)[-1], halves(tn, N)[-1]
        while vmem_bytes(tm, tn, tk) > MATMUL_VMEM_BYTES and tk % 2 == 0 and tk > 512:
            tk //= 2
    nk = K // tk
    assert a_koff % tk == 0 and not (ta and a_koff)
    ko = a_koff // tk
    dn = (((0 if ta else 1,), (1 if tb else 0,)), ((), ()))

    def body(*refs):
        a_ref, b_ref = refs[0], refs[1]
        epi_refs = refs[2:2 + n_epi]
        o_refs = refs[len(refs) - n_out - (nk > 1):len(refs) - (nk > 1)]
        k = pl.program_id(2)
        bv = b_ref[...]
        if bv.ndim == 3 and tb:
            cw = bv.shape[-1]
            prod = sum(lax.dot_general(a_ref[:, c * cw:(c + 1) * cw].astype(MXU_DTYPE), bv[c].astype(MXU_DTYPE), dn,
                                       preferred_element_type=F32) for c in range(bv.shape[0]))
        else:
            if bv.ndim == 3:
                bv = bv.reshape(-1, bv.shape[-1])
            prod = lax.dot_general(a_ref[...].astype(MXU_DTYPE), bv.astype(MXU_DTYPE), dn, preferred_element_type=F32)

        def finish(acc):
            res = (acc,) if epi is None else epi(acc, *[r[...] for r in epi_refs])
            for o_ref, r in zip(o_refs, res):
                o_ref[...] = r.astype(o_ref.dtype)

        if nk == 1:
            finish(prod)
            return
        acc_ref = refs[-1]

        @pl.when(k == 0)
        def _():
            acc_ref[...] = prod

        @pl.when(jnp.logical_and(k > 0, k < nk - 1))
        def _():
            acc_ref[...] += prod

        @pl.when(k == nk - 1)
        def _():
            finish(acc_ref[...] + prod)

    if a_spec is None:
        a_spec = pl.BlockSpec((tk, tm), lambda i, j, k: (k, i)) if ta else pl.BlockSpec((tm, tk), lambda i, j, k: (i, k + ko))
    else:
        a_spec = a_spec(tm, tn, tk)
    if b_spec is None:
        bs = pl.BlockSpec((tn, tk), lambda i, j, k: (j, k)) if tb else pl.BlockSpec((tk, tn), lambda i, j, k: (k, j))
    else:
        bs = b_spec(tm, tn, tk)
    tile = pl.BlockSpec((tm, tn), lambda i, j, k: (i, j))
    os_ = tile if o_spec is None else o_spec(tm, tn, tk)
    ins, in_specs, aliases = [a, b, *epi_in], [a_spec, bs] + [tile] * n_epi, {}
    if prev is not None:
        aliases = {len(ins): 0}
        ins.append(prev)
        in_specs.append(pl.BlockSpec(memory_space=pl.ANY))
    shapes = [SDS((M, N) if o_shape is None else o_shape, dt) for dt in out_dtypes]
    scratch = [pltpu.VMEM((tm, tn), F32)] if nk > 1 else []
    if hosted is not None:
        outs, bufs = call_hosting(body, hosted, name=name, grid=(M // tm, N // tn, nk), in_specs=in_specs,
                                  out_specs=[os_] * n_out, out_shape=shapes, inputs=ins, aliases=aliases, scratch=scratch)
        return (outs[0] if n_out == 1 else outs), bufs
    outs = pl.pallas_call(
        body, name=name, grid=(M // tm, N // tn, nk),
        in_specs=in_specs, out_specs=[os_] * n_out, out_shape=shapes, input_output_aliases=aliases,
        scratch_shapes=scratch, compiler_params=_cp("parallel", "parallel", "arbitrary"),
    )(*ins)
    return outs[0] if n_out == 1 else outs


def _rms_hat(x):
    return x * lax.rsqrt(jnp.mean(x * x, axis=-1, keepdims=True) + EPS)


def rms_fwd(x, g, hosted, *, name):
    T, D = x.shape
    tr = _pick(T, 512)

    def body(x_ref, g_ref, o_ref):
        o_ref[...] = (_rms_hat(x_ref[...]) * g_ref[...]).astype(o_ref.dtype)

    (h,), bufs = call_hosting(body, hosted, name=name, grid=(T // tr,), in_specs=[_row(tr, D), _full((1, D))],
                              out_specs=[_row(tr, D)], out_shape=[SDS((T, D), MXU_DTYPE)], inputs=[x, g], aliases={})
    return h, bufs


def res_norm_fwd(x, y, g_post, g_next, *, name):
    T, D = x.shape
    tr = _pick(T, 512)

    def body(x_ref, y_ref, gp_ref, gn_ref, o_ref, h_ref):
        xn = x_ref[...] + _rms_hat(y_ref[...]) * gp_ref[...]
        o_ref[...] = xn
        h_ref[...] = (_rms_hat(xn) * gn_ref[...]).astype(h_ref.dtype)

    return pl.pallas_call(body, name=name, grid=(T // tr,),
                          in_specs=[_row(tr, D), _row(tr, D), _full((1, D)), _full((1, D))],
                          out_specs=[_row(tr, D), _row(tr, D)], out_shape=[SDS((T, D), F32), SDS((T, D), MXU_DTYPE)],
                          compiler_params=_cp("parallel"))(x, y, g_post, g_next)


def res_norm_loss(x, y, g_post, target, *, name):
    T, D = x.shape
    tr = _pick(T, 512)

    def body(x_ref, y_ref, g_ref, t_ref, l_ref, d_ref):
        err = x_ref[...] + _rms_hat(y_ref[...]) * g_ref[...] - t_ref[...]
        d_ref[...] = err * (1.0 / D)

        @pl.when(pl.program_id(0) == 0)
        def _():
            l_ref[...] = jnp.zeros_like(l_ref)

        l_ref[...] += 0.5 * jnp.sum(jnp.mean(err * err, axis=-1, keepdims=True))

    return pl.pallas_call(body, name=name, grid=(T // tr,),
                          in_specs=[_row(tr, D), _row(tr, D), _full((1, D)), _row(tr, D)],
                          out_specs=[_full((SUBLANES, LANES)), _row(tr, D)],
                          out_shape=[SDS((SUBLANES, LANES), F32), SDS((T, D), F32)],
                          compiler_params=_cp("arbitrary"))(x, y, g_post, target)


def _rms_bwd_rows(x, g, dy):
    r = lax.rsqrt(jnp.mean(x * x, axis=-1, keepdims=True) + EPS)
    xh = x * r
    dxh = dy * g
    return r * (dxh - xh * jnp.mean(dxh * xh, axis=-1, keepdims=True)), jnp.sum(dy * xh, axis=0, keepdims=True)


def norm_res_bwd(x, g_pre, dh, res, y, g_post, *, name):
    T, D = x.shape
    tr = _pick(T, 512)

    def body(x_ref, gp_ref, dh_ref, res_ref, y_ref, gy_ref, dx_ref, dy_ref, dgp_ref, dgy_ref):
        dx, dgp = _rms_bwd_rows(x_ref[...], gp_ref[...], dh_ref[...])
        dx = dx + res_ref[...]
        dx_ref[...] = dx
        dy, dgy = _rms_bwd_rows(y_ref[...], gy_ref[...], dx)
        dy_ref[...] = dy.astype(dy_ref.dtype)

        @pl.when(pl.program_id(0) == 0)
        def _():
            dgp_ref[...] = jnp.zeros_like(dgp_ref)
            dgy_ref[...] = jnp.zeros_like(dgy_ref)

        dgp_ref[...] += dgp
        dgy_ref[...] += dgy

    row, vec = _row(tr, D), _full((1, D))
    return pl.pallas_call(body, name=name, grid=(T // tr,), in_specs=[row, vec, row, row, row, vec],
                          out_specs=[row, row, vec, vec],
                          out_shape=[SDS((T, D), F32), SDS((T, D), MXU_DTYPE), SDS((1, D), F32), SDS((1, D), F32)],
                          compiler_params=_cp("arbitrary"))(x, g_pre, dh, res, y, g_post)


def rms_bwd(x, g, dy, res, *, name, hosted=None):
    T, D = x.shape
    tr = _pick(T, 512)
    has_res = res is not None

    def body(*refs):
        if has_res:
            x_ref, g_ref, dy_ref, res_ref, dx_ref, dg_ref = refs
        else:
            x_ref, g_ref, dy_ref, dx_ref, dg_ref = refs
        dx, dg = _rms_bwd_rows(x_ref[...], g_ref[...], dy_ref[...])
        if has_res:
            dx = dx + res_ref[...]
        dx_ref[...] = dx.astype(dx_ref.dtype)

        @pl.when(pl.program_id(0) == 0)
        def _():
            dg_ref[...] = jnp.zeros_like(dg_ref)

        dg_ref[...] += dg

    ins = [x, g, dy] + ([res] if has_res else [])
    in_specs = [_row(tr, D), _full((1, D)), _row(tr, D)] + ([_row(tr, D)] if has_res else [])
    out_shape = [SDS((T, D), F32 if has_res else MXU_DTYPE), SDS((1, D), F32)]
    out_specs = [_row(tr, D), _full((1, D))]
    if hosted is not None:
        return call_hosting(body, hosted, name=name, grid=(T // tr,), in_specs=in_specs, out_specs=out_specs,
                            out_shape=out_shape, inputs=ins, aliases={})
    return pl.pallas_call(body, name=name, grid=(T // tr,), in_specs=in_specs, out_specs=out_specs,
                          out_shape=out_shape, compiler_params=_cp("arbitrary"))(*ins)


def _s5_disc(lr, li, ldt, btr, bti):
    dt = jnp.exp(ldt)
    k = lax.broadcasted_iota(jnp.int32, (SUBLANES, S5_LANES), 0).astype(F32)
    kf = k + 1.0
    kb = 8.0 - k
    ph = li * dt
    lm = lr * dt
    tf_re = jnp.exp(kf * lm) * jnp.cos(kf * ph)
    tf_im = jnp.exp(kf * lm) * jnp.sin(kf * ph)
    tb_re = jnp.exp(kb * lm) * jnp.cos(kb * ph)
    tb_im = -jnp.exp(kb * lm) * jnp.sin(kb * ph)
    mag = jnp.exp(lm)
    ab_re = mag * jnp.cos(ph)
    ab_im = mag * jnp.sin(ph)
    den = lr * lr + li * li
    nr = ab_re - 1.0
    ni = ab_im
    q_re = (nr * lr + ni * li) / den
    q_im = (ni * lr - nr * li) / den
    bbt_re = q_re * btr - q_im * bti
    bbt_im = q_re * bti + q_im * btr
    return tf_re, tf_im, tb_re, tb_im, bbt_re, bbt_im


def _s5_disc_core(lr, li, ldt, btr, bti):
    dt = jnp.exp(ldt)
    mag = jnp.exp(lr * dt)
    ab_re = mag * jnp.cos(li * dt)
    ab_im = mag * jnp.sin(li * dt)
    den = lr * lr + li * li
    nr = ab_re - 1.0
    ni = ab_im
    q_re = (nr * lr + ni * li) / den
    q_im = (ni * lr - nr * li) / den
    return ab_re, ab_im, q_re * btr - q_im * bti, q_re * bti + q_im * btr


def s5_disc_fwd(lr, li, ldt, btr, bti, *, name):
    def body(lr_ref, li_ref, ldt_ref, btr_ref, bti_ref, *outs):
        vals = _s5_disc(lr_ref[...], li_ref[...], ldt_ref[...], btr_ref[...], bti_ref[...])
        for o, v in zip(outs, vals):
            o[...] = v

    tab = SDS((SUBLANES, S5_LANES), F32)
    bb = SDS((S5_GROUP, S5_LANES), F32)
    return pl.pallas_call(body, name=name, out_shape=[tab, tab, tab, tab, bb, bb])(lr, li, ldt, btr, bti)


def s5_disc_bwd(lr, li, ldt, btr, bti, dab_re, dab_im, dbbt_re, dbbt_im, *, name):
    def body(lr_ref, li_ref, ldt_ref, btr_ref, bti_ref, dar_ref, dai_ref, dbr_ref, dbi_ref,
             dlr_ref, dli_ref, dldt_ref, dbtr_ref, dbti_ref):
        _, vjp = jax.vjp(_s5_disc_core, lr_ref[...], li_ref[...], ldt_ref[...], btr_ref[...], bti_ref[...])
        dlr, dli, dldt, dbtr, dbti = vjp((dar_ref[...], dai_ref[...], dbr_ref[...], dbi_ref[...]))
        dlr_ref[...] = dlr
        dli_ref[...] = dli
        dbtr_ref[...] = dbtr
        dbti_ref[...] = dbti
        lane_group = lax.broadcasted_iota(jnp.int32, (S5_LANES, LANES), 0) // S5_STATE
        col = lax.broadcasted_iota(jnp.int32, (S5_LANES, LANES), 1)
        ind = (lane_group == col).astype(F32)
        dldt_ref[...] = jnp.dot(jnp.broadcast_to(dldt, (SUBLANES, S5_LANES)), ind,
                                precision=lax.Precision.HIGHEST, preferred_element_type=F32)

    row = SDS((1, S5_LANES), F32)
    bb = SDS((S5_GROUP, S5_LANES), F32)
    return pl.pallas_call(body, name=name, out_shape=[row, row, SDS((SUBLANES, LANES), F32), bb, bb])(
        lr, li, ldt, btr, bti, dab_re, dab_im, dbbt_re, dbbt_im)


S5_NB = 1024


S5_CB = S5_WIDTH * S5_NB // S5_LANES


def _chan_rows_t(tm, tn, tk):
    return pl.BlockSpec((tk, S5_CB), lambda i, j, k: (k, j // 2))


def _chan_cols_of_i(tm, tn, tk):
    return pl.BlockSpec((tk, S5_CB), lambda i, j, k: (k, i // 2))


def _lanes_of_chan(tm, tn, tk):
    return pl.BlockSpec((tm, 2 * S5_NB), lambda i, j, k: (i, j))


def _s5_b_block_t(tm, tn, tk):
    return pl.BlockSpec((S5_CB, 2 * S5_NB), lambda i, j, k: (j, j))


def _s5_c_block(tm, tn, tk):
    return pl.BlockSpec((2 * S5_NB, S5_CB), lambda i, j, k: (j, j))


def s5_interleave(re, im, axis):
    parts = []
    for n in range(S5_LANES // S5_NB):
        sl = [slice(None)] * re.ndim
        sl[axis] = slice(n * S5_NB, (n + 1) * S5_NB)
        parts += [re[tuple(sl)], im[tuple(sl)]]
    return jnp.concatenate(parts, axis=axis)


def s5_deinterleave(a, axis):
    re, im = [], []
    for n in range(S5_LANES // S5_NB):
        sl = [slice(None)] * a.ndim
        sl[axis] = slice(2 * n * S5_NB, (2 * n + 1) * S5_NB)
        re.append(a[tuple(sl)])
        sl[axis] = slice((2 * n + 1) * S5_NB, (2 * n + 2) * S5_NB)
        im.append(a[tuple(sl)])
    return jnp.concatenate(re, axis=axis), jnp.concatenate(im, axis=axis)


def s5_scan(src, mat, tab_re, tab_im, *, reverse, name, states=None, hosted=None):
    T = src.shape[0]
    nb = S5_NB
    tc = _pick(T, 256)
    nl = S5_LANES // nb
    nt = T // tc
    ntile = tc // SUBLANES
    with_da = states is not None
    assert reverse or not with_da
    step_rows = ((1, 7), (2, 6), (4, 4)) if reverse else ((1, 0), (2, 1), (4, 3))
    drive_dn = _NT if reverse else (((1,), (0,)), ((), ()))

    def body(*refs):
        if with_da:
            (src_ref, wr_ref, wi_ref, tr_ref, ti_ref, sr_ref, si_ref, hr_ref, hi_ref, xo_ref, dar_ref, dai_ref,
             cr_ref, ci_ref, mr_ref, mi_ref, br_ref, bi_ref, ar_ref, ai_ref) = refs
        else:
            src_ref, wr_ref, wi_ref, tr_ref, ti_ref, xo_ref, cr_ref, ci_ref, mr_ref, mi_ref, br_ref, bi_ref = refs

        @pl.when(pl.program_id(1) == 0)
        def _():
            cr_ref[...] = jnp.zeros_like(cr_ref)
            ci_ref[...] = jnp.zeros_like(ci_ref)
            if with_da:
                ar_ref[...] = jnp.zeros_like(ar_ref)
                ai_ref[...] = jnp.zeros_like(ai_ref)

        lhs = src_ref[...].astype(MXU_DTYPE)
        br_ref[...] = lax.dot_general(lhs, wr_ref[...].astype(MXU_DTYPE), drive_dn, preferred_element_type=F32)
        bi_ref[...] = lax.dot_general(lhs, wi_ref[...].astype(MXU_DTYPE), drive_dn, preferred_element_type=F32)

        seen = jnp.where(pl.program_id(1) < nt - 1, 1.0, 0.0)

        def add_da(lr, li, r0, last_r, last_i):
            first = lax.broadcasted_iota(jnp.int32, (SUBLANES, nb), 0) == 0
            pr = jnp.where(first, last_r, pltpu.roll(sr_ref[pl.ds(r0, SUBLANES), :], 1, 0))
            pi = jnp.where(first, last_i, pltpu.roll(si_ref[pl.ds(r0, SUBLANES), :], 1, 0))
            ar_ref[...] += lr * pr + li * pi
            ai_ref[...] += li * pr - lr * pi

        io = lax.broadcasted_iota(jnp.int32, (SUBLANES, nb), 0)
        for s_, (d, r) in enumerate(step_rows):
            keep = (io < SUBLANES - d) if reverse else (io >= d)
            mr_ref[s_] = jnp.where(keep, tr_ref[r:r + 1, :], 0.0)
            mi_ref[s_] = jnp.where(keep, ti_ref[r:r + 1, :], 0.0)

        def tile(i, carry):
            cr, ci = carry
            j = (ntile - 1 - i) if reverse else i
            r0 = pl.multiple_of(j * SUBLANES, SUBLANES)
            xr = br_ref[pl.ds(r0, SUBLANES), :]
            xi = bi_ref[pl.ds(r0, SUBLANES), :]
            for s_, (d, _) in enumerate(step_rows):
                sh = (SUBLANES - d) if reverse else d
                sr = pltpu.roll(xr, sh, 0)
                si = pltpu.roll(xi, sh, 0)
                pr, pi = mr_ref[s_], mi_ref[s_]
                xr, xi = xr + pr * sr - pi * si, xi + pr * si + pi * sr
            tr, ti = tr_ref[...], ti_ref[...]
            xr, xi = xr + tr * cr - ti * ci, xi + tr * ci + ti * cr
            xo_ref[pl.ds(r0, SUBLANES), 0:nb] = xr
            xo_ref[pl.ds(r0, SUBLANES), nb:2 * nb] = xi
            if with_da:
                @pl.when(j > 0)
                def _():
                    p0 = pl.multiple_of(r0 - SUBLANES, SUBLANES)
                    add_da(xr, xi, r0, sr_ref[pl.ds(p0, SUBLANES), :][SUBLANES - 1:SUBLANES, :],
                           si_ref[pl.ds(p0, SUBLANES), :][SUBLANES - 1:SUBLANES, :])

                @pl.when(j == 0)
                def _():
                    add_da(xr, xi, r0, hr_ref[SUBLANES - 1:SUBLANES, :] * seen, hi_ref[SUBLANES - 1:SUBLANES, :] * seen)
            if reverse:
                return xr[0:1, :], xi[0:1, :]
            return xr[SUBLANES - 1:SUBLANES, :], xi[SUBLANES - 1:SUBLANES, :]

        cr, ci = lax.fori_loop(0, ntile, tile, (cr_ref[0:1, :], ci_ref[0:1, :]))
        cr_ref[0:1, :] = cr
        ci_ref[0:1, :] = ci
        if with_da:
            @pl.when(pl.program_id(1) == nt - 1)
            def _():
                dar_ref[...] = jnp.sum(ar_ref[...], axis=0, keepdims=True)
                dai_ref[...] = jnp.sum(ai_ref[...], axis=0, keepdims=True)

    def tmap(t):
        return (nt - 1 - t) if reverse else t

    hb = tc // SUBLANES
    re_spec = pl.BlockSpec((tc, nb), lambda n, t: (tmap(t), 2 * n))
    im_spec = pl.BlockSpec((tc, nb), lambda n, t: (tmap(t), 2 * n + 1))
    tab_spec = pl.BlockSpec((SUBLANES, nb), lambda n, t: (0, n))
    out_spec = pl.BlockSpec((tc, 2 * nb), lambda n, t: (tmap(t), n))
    out_shape = SDS((T, 2 * S5_LANES), F32)
    scratch = [pltpu.VMEM((SUBLANES, nb), F32), pltpu.VMEM((SUBLANES, nb), F32),
               pltpu.VMEM((len(step_rows), SUBLANES, nb), F32), pltpu.VMEM((len(step_rows), SUBLANES, nb), F32),
               pltpu.VMEM((tc, nb), F32), pltpu.VMEM((tc, nb), F32)]
    src_spec = pl.BlockSpec((tc, S5_CB), lambda n, t: (tmap(t), n))
    if reverse:
        wr_spec = pl.BlockSpec((nb, S5_CB), lambda n, t: (2 * n, n))
        wi_spec = pl.BlockSpec((nb, S5_CB), lambda n, t: (2 * n + 1, n))
    else:
        wr_spec = pl.BlockSpec((S5_CB, nb), lambda n, t: (n, 2 * n))
        wi_spec = pl.BlockSpec((S5_CB, nb), lambda n, t: (n, 2 * n + 1))
    drive_specs = [src_spec, wr_spec, wi_spec, tab_spec, tab_spec]
    drive = [src, mat, mat, tab_re, tab_im]
    if not with_da:
        return pl.pallas_call(body, name=name, grid=(nl, nt), in_specs=drive_specs,
                              out_specs=out_spec, out_shape=out_shape, scratch_shapes=scratch,
                              compiler_params=_cp("parallel", "arbitrary"))(*drive)
    re_halo = pl.BlockSpec((SUBLANES, nb), lambda n, t: (jnp.maximum(tmap(t) * hb - 1, 0), 2 * n))
    im_halo = pl.BlockSpec((SUBLANES, nb), lambda n, t: (jnp.maximum(tmap(t) * hb - 1, 0), 2 * n + 1))
    acc = pl.BlockSpec((1, nb), lambda n, t: (0, n))
    row = SDS((1, S5_LANES), F32)
    return call_hosting(
        body, hosted, name=name, grid=(nl, nt),
        in_specs=drive_specs + [re_spec, im_spec, re_halo, im_halo],
        out_specs=[out_spec, acc, acc], out_shape=[out_shape, row, row],
        inputs=drive + [states, states, states, states], aliases={},
        scratch=scratch + [pltpu.VMEM((SUBLANES, nb), F32), pltpu.VMEM((SUBLANES, nb), F32)])


def s5_out_fwd(yc, u, d, *, name):
    T, C = yc.shape
    tr = _pick(T, 512)

    def body(yc_ref, u_ref, d_ref, yl_ref, yg_ref):
        yl = yc_ref[...] + d_ref[...] * u_ref[...]
        yl_ref[...] = yl
        yg_ref[...] = jax.nn.gelu(yl)

    return pl.pallas_call(body, name=name, grid=(T // tr,), in_specs=[_row(tr, C), _row(tr, C), _full((1, C))],
                          out_specs=[_row(tr, C)] * 2, out_shape=[SDS((T, C), F32)] * 2,
                          compiler_params=_cp("parallel"))(yc, u, d)


def glu_fwd(yg, gl, *, out_cols, name):
    T, C = yg.shape
    tr = _pick(T, 512)

    def body(yg_ref, gl_ref, o_ref):
        o_ref[...] = yg_ref[...] * jax.nn.sigmoid(gl_ref[...])

    return pl.pallas_call(body, name=name, grid=(T // tr,), in_specs=[_row(tr, C)] * 2, out_specs=_row(tr, C),
                          out_shape=SDS((T, out_cols), F32), compiler_params=_cp("parallel"))(yg, gl)


def glu_bwd(yg, gl, dy, *, name):
    T, C = yg.shape
    tr = _pick(T, 512)

    def body(yg_ref, gl_ref, dy_ref, dyg_ref, dgl_ref):
        s = jax.nn.sigmoid(gl_ref[...])
        dyv = dy_ref[...]
        dyg_ref[...] = dyv * s
        dgl_ref[...] = (dyv * yg_ref[...] * s * (1.0 - s)).astype(dgl_ref.dtype)

    return pl.pallas_call(body, name=name, grid=(T // tr,), in_specs=[_row(tr, C)] * 3, out_specs=[_row(tr, C)] * 2,
                          out_shape=[SDS((T, C), F32), SDS((T, C), MXU_DTYPE)],
                          compiler_params=_cp("parallel"))(yg, gl, dy)


def s5_out_bwd(yl, u, d, dyg_a, dyg_b, *, name):
    T, C = yl.shape
    tr = _pick(T, 512)

    def body(yl_ref, u_ref, d_ref, da_ref, db_ref, dyl_ref, du_ref, dd_ref):
        dyl = (da_ref[...] + db_ref[...]) * _gelu_grad(yl_ref[...])
        dyl_ref[...] = dyl.astype(dyl_ref.dtype)
        du_ref[...] = dyl * d_ref[...]

        @pl.when(pl.program_id(0) == 0)
        def _():
            dd_ref[...] = jnp.zeros_like(dd_ref)

        dd_ref[...] += jnp.sum(dyl * u_ref[...], axis=0, keepdims=True)

    return pl.pallas_call(body, name=name, grid=(T // tr,),
                          in_specs=[_row(tr, C), _row(tr, C), _full((1, C)), _row(tr, C), _row(tr, C)],
                          out_specs=[_row(tr, C), _row(tr, C), _full((1, C))],
                          out_shape=[SDS((T, C), MXU_DTYPE), SDS((T, C), F32), SDS((1, C), F32)],
                          compiler_params=_cp("arbitrary"))(yl, u, d, dyg_a, dyg_b)


def add2(a, b, *, name):
    T, C = a.shape
    tr = _pick(T, 512)

    def body(a_ref, b_ref, o_ref):
        o_ref[...] = a_ref[...] + b_ref[...]

    return pl.pallas_call(body, name=name, grid=(T // tr,), in_specs=[_row(tr, C)] * 2, out_specs=_row(tr, C),
                          out_shape=SDS((T, C), F32), compiler_params=_cp("parallel"))(a, b)


def _tri(n, upper):
    r = lax.broadcasted_iota(jnp.int32, (n, n), 0)
    c = lax.broadcasted_iota(jnp.int32, (n, n), 1)
    return ((c >= r) if upper else (c <= r)).astype(F32)


def fox_gate_fwd(fl, bf, *, fl_col, name):
    T = fl.shape[0]
    tb = _pick(T, 256)

    def body(fl_ref, bf_ref, f_ref, c_ref):
        @pl.when(pl.program_id(0) == 0)
        def _():
            c_ref[...] = jnp.zeros_like(c_ref)

        lf = jax.nn.log_sigmoid(fl_ref[...] + bf_ref[...])
        f = jnp.dot(_tri(tb, False), lf, precision=lax.Precision.HIGHEST, preferred_element_type=F32) + c_ref[0:1, :]
        f_ref[...] = f * LOG2E
        c_ref[0:1, :] = f[tb - 1:tb, :]

    fl_spec = pl.BlockSpec((tb, LANES), lambda i: (i, fl_col))
    return pl.pallas_call(body, name=name, grid=(T // tb,), in_specs=[fl_spec, _full((1, LANES))],
                          out_specs=_row(tb, LANES), out_shape=SDS((T, LANES), F32),
                          scratch_shapes=[pltpu.VMEM((SUBLANES, LANES), F32)], compiler_params=_cp("arbitrary"))(fl, bf)


def fox_gate_bwd(fl, bf, df_keys, df_queries, *, fl_col, name):
    T = fl.shape[0]
    tb = _pick(T, 256)
    nt = T // tb

    def body(fl_ref, bf_ref, dfk_ref, dfq_ref, dfl_ref, dbf_ref, c_ref):
        @pl.when(pl.program_id(0) == 0)
        def _():
            c_ref[...] = jnp.zeros_like(c_ref)
            dbf_ref[...] = jnp.zeros_like(dbf_ref)

        dlf = jnp.dot(_tri(tb, True), dfk_ref[...] + dfq_ref[...], precision=lax.Precision.HIGHEST,
                      preferred_element_type=F32) + c_ref[0:1, :]
        c_ref[0:1, :] = dlf[0:1, :]
        dfl = dlf * jax.nn.sigmoid(-(fl_ref[...] + bf_ref[...]))
        dfl_ref[...] = dfl
        dbf_ref[...] += jnp.sum(dfl, axis=0, keepdims=True)

    rev = pl.BlockSpec((tb, LANES), lambda i: (nt - 1 - i, 0))
    fl_rev = pl.BlockSpec((tb, LANES), lambda i: (nt - 1 - i, fl_col))
    return pl.pallas_call(body, name=name, grid=(nt,), in_specs=[fl_rev, _full((1, LANES)), rev, rev],
                          out_specs=[rev, _full((1, LANES))], out_shape=[SDS((T, LANES), F32), SDS((1, LANES), F32)],
                          scratch_shapes=[pltpu.VMEM((SUBLANES, LANES), F32)],
                          compiler_params=_cp("arbitrary"))(fl, bf, df_keys, df_queries)


FOX_BLOCK = 512
FOX_PAIRS = FOX_HEADS // 2
_NT = (((1,), (1,)), ((), ()))


LOG2E = 1.4426950408889634
FOX_FWD_UNROLL = 4
FOX_BWD_UNROLL = 2


def _fox_block(T):
    return _pick(T, FOX_BLOCK)


def _own_lanes(lane, hh):
    return (lane < FOX_HEAD_DIM) if hh == 0 else (lane >= FOX_HEAD_DIM)


def _grouped_steps(step, lo, n, unroll, init):
    def trip(t, c):
        for u in range(unroll):
            c = step(lo + t * unroll + u, c)
        return c

    carry = lax.fori_loop(0, n // unroll, trip, init)
    for u in range(unroll - 1):
        carry = lax.cond(n % unroll > u, lambda c: step(lo + (n // unroll) * unroll + u, c), lambda c: c, carry)
    return carry


Q_TILE0, K_TILE0, V_TILE0, O_TILE0 = 4, 8, 12, 4
FL_TILE = 16
POOL_COL = 2


def fox_fwd(z, f_col, f_row, ycat, hosted, *, name):
    T = z.shape[0]
    blk = _fox_block(T)
    nb = T // blk
    scale = FOX_HEAD_DIM ** -0.5

    def body(q_ref, k_ref, v_ref, fc_ref, fr_ref, prev_ref, o_ref, l_ref):
        i = pl.program_id(1)
        row = lax.broadcasted_iota(jnp.int32, (blk, blk), 0)
        col = lax.broadcasted_iota(jnp.int32, (blk, blk), 1)
        lane = lax.broadcasted_iota(jnp.int32, (blk, LANES), 1)
        qt = q_ref[...] * (scale * LOG2E)
        outs = []
        for hh in range(2):
            qh = jnp.where(_own_lanes(lane, hh), qt, 0.0).astype(MXU_DTYPE)
            fi = fc_ref[0, :, hh:hh + 1]

            def step(j, carry, masked=False):
                m, l, acc = carry
                r0 = pl.multiple_of(j * blk, blk)
                kj = k_ref[pl.ds(r0, blk), :].astype(MXU_DTYPE)
                vj = v_ref[pl.ds(r0, blk), :].astype(MXU_DTYPE)
                s = lax.dot_general(qh, kj, _NT, preferred_element_type=F32) + (fi - fr_ref[0, j, hh:hh + 1, :])
                if masked:
                    s = jnp.where(col <= row, s, -jnp.inf)
                m_new = jnp.maximum(m, jnp.max(s, axis=-1, keepdims=True))
                p = jnp.exp2(s - m_new)
                alpha = jnp.exp2(m - m_new)
                l = alpha * l + jnp.sum(p, axis=-1, keepdims=True)
                acc = alpha * acc + jnp.dot(p.astype(MXU_DTYPE), vj, preferred_element_type=F32)
                return m_new, l, acc

            init = (jnp.full((blk, 1), -jnp.inf, F32), jnp.zeros((blk, 1), F32), jnp.zeros((blk, LANES), F32))
            m, l, acc = step(i, _grouped_steps(step, 0, i, FOX_FWD_UNROLL, init), True)
            outs.append(acc / l)
            l_ref[0, :, hh:hh + 1] = m + jnp.log2(l)
        o_ref[...] = jnp.where(_own_lanes(lane, 0), outs[0], outs[1])

    qspec = pl.BlockSpec((blk, LANES), lambda h, i: (i, Q_TILE0 + h))
    kspec = pl.BlockSpec((T, LANES), lambda h, i: (0, K_TILE0 + h))
    vspec = pl.BlockSpec((T, LANES), lambda h, i: (0, V_TILE0 + h))
    ospec = pl.BlockSpec((blk, LANES), lambda h, i: (i, O_TILE0 + h))
    cspec = pl.BlockSpec((1, blk, 2), lambda h, i: (h, i, 0))
    rspec = pl.BlockSpec((1, nb, 2, blk), lambda h, i: (h, 0, 0, 0))
    return call_hosting(body, hosted, name=name, grid=(FOX_PAIRS, nb),
                        in_specs=[qspec, kspec, vspec, cspec, rspec, ANY], out_specs=[ospec, cspec],
                        out_shape=[SDS(ycat.shape, F32), SDS((FOX_PAIRS, T, 2), F32)],
                        inputs=[z, z, z, f_col, f_row, ycat], aliases={5: 0})


def fox_dd(ycat, dycat, *, name):
    T = ycat.shape[0]
    blk = _fox_block(T)

    def body(o_ref, do_ref, dd_ref):
        lane = lax.broadcasted_iota(jnp.int32, (blk, LANES), 1)
        prod = do_ref[...] * o_ref[...]
        for hh in range(2):
            dd_ref[0, :, hh:hh + 1] = jnp.sum(jnp.where(_own_lanes(lane, hh), prod, 0.0), axis=-1, keepdims=True)

    ospec = pl.BlockSpec((blk, LANES), lambda h, i: (i, O_TILE0 + h))
    return pl.pallas_call(body, name=name, grid=(FOX_PAIRS, T // blk), in_specs=[ospec, ospec],
                          out_specs=pl.BlockSpec((1, blk, 2), lambda h, i: (h, i, 0)),
                          out_shape=SDS((FOX_PAIRS, T, 2), F32), compiler_params=_cp("parallel", "parallel"))(ycat, dycat)


def fox_bwd(z, dycat, f_col, f_row, lse_row, dd_row, hosted, *, name):
    T = z.shape[0]
    blk = _fox_block(T)
    nb = T // blk
    scale = FOX_HEAD_DIM ** -0.5

    def body(q_ref, k_ref, v_ref, do_ref, fc_ref, fr_ref, lr_ref, dr_ref, dk_ref, dv_ref, df_ref, dqt_ref, dfq_ref):
        j = pl.program_id(1)

        @pl.when(j == 0)
        def _():
            dqt_ref[...] = jnp.zeros_like(dqt_ref)
            dfq_ref[...] = jnp.zeros_like(dfq_ref)

        row = lax.broadcasted_iota(jnp.int32, (blk, blk), 0)
        col = lax.broadcasted_iota(jnp.int32, (blk, blk), 1)
        lane = lax.broadcasted_iota(jnp.int32, (blk, LANES), 1)
        kt = k_ref[...]
        vt = v_ref[...]
        dks, dvs = [], []
        for hh in range(2):
            own = _own_lanes(lane, hh)
            kh = jnp.where(own, kt, 0.0).astype(MXU_DTYPE)
            vh = jnp.where(own, vt, 0.0).astype(MXU_DTYPE)
            kht = kh.T
            fj = fc_ref[0, :, hh:hh + 1]

            def step(i, carry, masked=False):
                dk, dv, df = carry
                r0 = pl.multiple_of(i * blk, blk)
                qi = (q_ref[pl.ds(r0, blk), :] * (scale * LOG2E)).astype(MXU_DTYPE)
                doi = do_ref[pl.ds(r0, blk), :].astype(MXU_DTYPE)
                st = lax.dot_general(kh, qi, _NT, preferred_element_type=F32) + (fr_ref[0, i, hh:hh + 1, :] - fj)
                pt = jnp.exp2(st - lr_ref[0, i, hh:hh + 1, :])
                if masked:
                    pt = jnp.where(col >= row, pt, 0.0)
                dv = dv + jnp.dot(pt.astype(MXU_DTYPE), doi, preferred_element_type=F32)
                dpt = lax.dot_general(vh, doi, _NT, preferred_element_type=F32)
                dst = pt * (dpt - dr_ref[0, i, hh:hh + 1, :])
                dsb = dst.astype(MXU_DTYPE)
                dk = dk + jnp.dot(dsb, qi, preferred_element_type=F32)
                df = df - jnp.sum(dst, axis=-1, keepdims=True)
                dqt_ref[0, i] += jnp.dot(kht, dsb, preferred_element_type=F32)
                dfq_ref[0, i, hh:hh + 1, :] += jnp.sum(dst, axis=0, keepdims=True)
                return dk, dv, df

            init = (jnp.zeros((blk, LANES), F32), jnp.zeros((blk, LANES), F32), jnp.zeros((blk, 1), F32))
            dk, dv, df = _grouped_steps(step, j + 1, nb - 1 - j, FOX_BWD_UNROLL, step(j, init, True))
            dks.append(dk * (1.0 / LOG2E))
            dvs.append(dv)
            df_ref[0, :, hh:hh + 1] = df
        dk_ref[...] = jnp.where(_own_lanes(lane, 0), dks[0], dks[1])
        dv_ref[...] = jnp.where(_own_lanes(lane, 0), dvs[0], dvs[1])

    bspec = pl.BlockSpec((blk, LANES), lambda h, j: (j, h))
    qspec = pl.BlockSpec((T, LANES), lambda h, j: (0, Q_TILE0 + h))
    kspec = pl.BlockSpec((blk, LANES), lambda h, j: (j, K_TILE0 + h))
    vspec = pl.BlockSpec((blk, LANES), lambda h, j: (j, V_TILE0 + h))
    dospec = pl.BlockSpec((T, LANES), lambda h, j: (0, O_TILE0 + h))
    cspec = pl.BlockSpec((1, blk, 2), lambda h, j: (h, j, 0))
    rspec = pl.BlockSpec((1, nb, 2, blk), lambda h, j: (h, 0, 0, 0))
    dqspec = pl.BlockSpec((1, nb, LANES, blk), lambda h, j: (h, 0, 0, 0))
    return call_hosting(body, hosted, name=name, grid=(FOX_PAIRS, nb),
                        in_specs=[qspec, kspec, vspec, dospec, cspec, rspec, rspec, rspec],
                        out_specs=[bspec, bspec, cspec, dqspec, rspec],
                        out_shape=[SDS((T, FOX_WIDTH), F32), SDS((T, FOX_WIDTH), F32), SDS((FOX_PAIRS, T, 2), F32),
                                   SDS((FOX_PAIRS, nb, LANES, blk), F32), SDS((FOX_PAIRS, nb, 2, blk), F32)],
                        inputs=[z, z, z, dycat, f_col, f_row, lse_row, dd_row], aliases={})


def _pairs_col(a, T):
    return jnp.transpose(a[:, :FOX_HEADS].reshape(T, FOX_PAIRS, 2), (1, 0, 2))


def _col_to_row(a, T):
    blk = _fox_block(T)
    return jnp.transpose(a.reshape(FOX_PAIRS, T // blk, blk, 2), (0, 1, 3, 2))


def _row_to_col(a, T):
    return jnp.transpose(a, (0, 1, 3, 2)).reshape(FOX_PAIRS, T, 2)


def _pairs_to_lanes(a, T):
    flat = jnp.transpose(a, (1, 0, 2)).reshape(T, FOX_HEADS)
    return jnp.pad(flat, ((0, 0), (0, LANES - FOX_HEADS)))


def _pool_counts(t0, n, w):
    t = (t0 + lax.broadcasted_iota(jnp.int32, (n, 1), 0)).astype(F32)
    return jnp.minimum(t + 1.0, float(w))


def pool_window(x, *, adjoint, name, in_col=0, into=None, out_col=0, out_dtype=F32):
    T, C = x.shape[0], len(POOL_WINDOWS) * POOL_GROUP_DIM
    tr = _pick(T, 512)
    nt = T // tr
    hb = tr // POOL_HALO
    n = tr + POOL_HALO

    def body(x_ref, h_ref, *rest):
        o_ref = rest[-1]
        i = pl.program_id(0)
        cur = x_ref[...]
        if adjoint:
            halo = h_ref[...] * jnp.where(i < nt - 1, 1.0, 0.0)
            ext = jnp.concatenate([cur, halo], axis=0)
            t0 = i * tr
        else:
            halo = h_ref[...] * jnp.where(i > 0, 1.0, 0.0)
            ext = jnp.concatenate([halo, cur], axis=0)
            t0 = i * tr - POOL_HALO
        sums = {}
        for g, w in enumerate(POOL_WINDOWS):
            ls = slice(g * POOL_GROUP_DIM, (g + 1) * POOL_GROUP_DIM)
            s = ext[:, ls]
            if adjoint:
                s = s / _pool_counts(t0, n, w)
            d = 1
            while d < w:
                s = s + pltpu.roll(s, (n - d) if adjoint else d, 0)
                d *= 2
            if adjoint:
                o_ref[:, ls] = (s[0:tr, :] - cur[:, ls]).astype(o_ref.dtype)
            else:
                o_ref[:, ls] = (s[POOL_HALO:n, :] / _pool_counts(i * tr, tr, w) - cur[:, ls]).astype(o_ref.dtype)

    if adjoint:
        halo_spec = pl.BlockSpec((POOL_HALO, C), lambda i: (jnp.minimum((i + 1) * hb, T // POOL_HALO - 1), in_col))
    else:
        halo_spec = pl.BlockSpec((POOL_HALO, C), lambda i: (jnp.maximum(i * hb - 1, 0), in_col))
    x_spec = pl.BlockSpec((tr, C), lambda i: (i, in_col))
    if into is None:
        return pl.pallas_call(body, name=name, grid=(nt,), in_specs=[x_spec, halo_spec], out_specs=_row(tr, C),
                              out_shape=SDS((T, C), out_dtype), compiler_params=_cp("parallel"))(x, x)
    return pl.pallas_call(body, name=name, grid=(nt,), in_specs=[x_spec, halo_spec, ANY],
                          out_specs=pl.BlockSpec((tr, C), lambda i: (i, out_col)), out_shape=SDS(into.shape, into.dtype),
                          input_output_aliases={2: 0}, compiler_params=_cp("parallel"))(x, x, into)


def colscale_fwd(a, s, *, out_cols, name):
    T, C = a.shape
    tr = _pick(T, 512)

    def body(a_ref, s_ref, o_ref):
        o_ref[...] = (a_ref[...] * s_ref[...]).astype(o_ref.dtype)

    return pl.pallas_call(body, name=name, grid=(T // tr,), in_specs=[_row(tr, C), _full((1, C))], out_specs=_row(tr, C),
                          out_shape=SDS((T, out_cols), MXU_DTYPE), compiler_params=_cp("parallel"))(a, s)


def colscale_bwd(a, s, dy, *, name):
    T, C = a.shape
    tr = _pick(T, 512)

    def body(a_ref, s_ref, dy_ref, da_ref, ds_ref):
        dyv = dy_ref[...]
        da_ref[...] = (dyv * s_ref[...]).astype(da_ref.dtype)

        @pl.when(pl.program_id(0) == 0)
        def _():
            ds_ref[...] = jnp.zeros_like(ds_ref)

        ds_ref[...] += jnp.sum(dyv * a_ref[...], axis=0, keepdims=True)

    return pl.pallas_call(body, name=name, grid=(T // tr,), in_specs=[_row(tr, C), _full((1, C)), _row(tr, C)],
                          out_specs=[_row(tr, C), _full((1, C))], out_shape=[SDS((T, C), MXU_DTYPE), SDS((1, C), F32)],
                          compiler_params=_cp("arbitrary"))(a, s, dy)


SGU_ROWS = 512


def _sgu_norm(v, ln_g, ln_b):
    vg = jax.nn.gelu(v)
    xc = vg - jnp.mean(vg, axis=-1, keepdims=True)
    r = lax.rsqrt(jnp.mean(xc * xc, axis=-1, keepdims=True) + EPS)
    xh = xc * r
    return xh * ln_g + ln_b, xh, r


def _rowc(tr, c, cb):
    return pl.BlockSpec((tr, c), lambda i: (i, cb))


def sgu_fwd(z, ln_g, ln_b, ws, bst, ycat, *, name):
    T, C = z.shape[0], SGU_GROUPS * SGU_GROUP_DIM
    tr = _pick(T, SGU_ROWS)

    def body(u_ref, v_ref, g_ref, b_ref, ws_ref, bst_ref, prev_ref, o_ref):
        vn, _, _ = _sgu_norm(v_ref[...], g_ref[...], b_ref[...])
        vn = vn.astype(MXU_DTYPE)
        ug = jax.nn.gelu(u_ref[...])
        for g in range(SGU_GROUPS):
            w = ws_ref[g].astype(MXU_DTYPE)
            bias = bst_ref[:, g:g + 1]
            for c in range(tr // CHUNK):
                rs = slice(c * CHUNK, (c + 1) * CHUNK)
                ls = slice(g * SGU_GROUP_DIM, (g + 1) * SGU_GROUP_DIM)
                mixed = jnp.dot(w, vn[rs, ls], preferred_element_type=F32) + bias
                o_ref[rs, ls] = (ug[rs, ls] * mixed).astype(o_ref.dtype)

    return pl.pallas_call(body, name=name, grid=(T // tr,),
                          in_specs=[_rowc(tr, C, 0), _rowc(tr, C, 1), _full((1, C)), _full((1, C)),
                                    _full((SGU_GROUPS, CHUNK, CHUNK)), _full((CHUNK, SGU_GROUPS)), ANY],
                          out_specs=_rowc(tr, C, 1), out_shape=SDS(ycat.shape, ycat.dtype), input_output_aliases={6: 0},
                          compiler_params=_cp("parallel"))(z, z, ln_g, ln_b, ws, bst, ycat)


def sgu_bwd(z, ln_g, ln_b, ws, wst, bst, dycat, *, out_cols, name):
    T, C = z.shape[0], SGU_GROUPS * SGU_GROUP_DIM
    tr = _pick(T, SGU_ROWS)

    def body(u_ref, v_ref, g_ref, b_ref, ws_ref, wst_ref, bst_ref, dy_ref,
             duv_ref, dg_ref, db_ref, dws_ref, dbst_ref, dvn_ref):
        du_ref = duv_ref.at[:, 0:C]
        dv_ref = duv_ref.at[:, C:2 * C]
        @pl.when(pl.program_id(0) == 0)
        def _():
            dg_ref[...] = jnp.zeros_like(dg_ref)
            db_ref[...] = jnp.zeros_like(db_ref)
            dws_ref[...] = jnp.zeros_like(dws_ref)
            dbst_ref[...] = jnp.zeros_like(dbst_ref)

        uv = u_ref[...]
        vv = v_ref[...]
        vn, xh, r = _sgu_norm(vv, g_ref[...], b_ref[...])
        vn = vn.astype(MXU_DTYPE)
        ug = jax.nn.gelu(uv)
        dyv = dy_ref[...]
        for g in range(SGU_GROUPS):
            w = ws_ref[g].astype(MXU_DTYPE)
            wt = wst_ref[g].astype(MXU_DTYPE)
            bias = bst_ref[:, g:g + 1]
            dw = jnp.zeros((CHUNK, CHUNK), F32)
            dbias = jnp.zeros((CHUNK, 1), F32)
            for c in range(tr // CHUNK):
                rs = slice(c * CHUNK, (c + 1) * CHUNK)
                ls = slice(g * SGU_GROUP_DIM, (g + 1) * SGU_GROUP_DIM)
                vblk = vn[rs, ls]
                mixed = jnp.dot(w, vblk, preferred_element_type=F32) + bias
                dyb = dyv[rs, ls]
                du_ref[rs, ls] = (dyb * mixed * _gelu_grad(uv[rs, ls])).astype(du_ref.dtype)
                dmixed = dyb * ug[rs, ls]
                dbias = dbias + jnp.sum(dmixed, axis=-1, keepdims=True)
                dmb = dmixed.astype(MXU_DTYPE)
                dw = dw + lax.dot_general(dmb, vblk, _NT, preferred_element_type=F32)
                dvn_ref[rs, ls] = jnp.dot(wt, dmb, preferred_element_type=F32)
            dws_ref[g] += dw
            dbst_ref[:, g:g + 1] += dbias
        dvn = dvn_ref[...]
        dg_ref[...] += jnp.sum(dvn * xh, axis=0, keepdims=True)
        db_ref[...] += jnp.sum(dvn, axis=0, keepdims=True)
        dxh = dvn * g_ref[...]
        dvg = r * (dxh - jnp.mean(dxh, axis=-1, keepdims=True) - xh * jnp.mean(dxh * xh, axis=-1, keepdims=True))
        dv_ref[...] = (dvg * _gelu_grad(vv)).astype(dv_ref.dtype)

    wspec = _full((SGU_GROUPS, CHUNK, CHUNK))
    return pl.pallas_call(body, name=name, grid=(T // tr,),
                          in_specs=[_rowc(tr, C, 0), _rowc(tr, C, 1), _full((1, C)), _full((1, C)), wspec, wspec,
                                    _full((CHUNK, SGU_GROUPS)), _rowc(tr, C, 1)],
                          out_specs=[_rowc(tr, 2 * C, 0), _full((1, C)), _full((1, C)), wspec,
                                     _full((CHUNK, SGU_GROUPS))],
                          out_shape=[SDS((T, out_cols), MXU_DTYPE), SDS((1, C), F32), SDS((1, C), F32),
                                     SDS((SGU_GROUPS, CHUNK, CHUNK), F32), SDS((CHUNK, SGU_GROUPS), F32)],
                          scratch_shapes=[pltpu.VMEM((tr, C), F32)],
                          compiler_params=_cp("arbitrary"))(z, z, ln_g, ln_b, ws, wst, bst, dycat)


def adamw(w, g, m, v, *, name):
    R, C = w.shape
    tr = _pick(R, 512)
    c1 = 1.0 - ADAM_B1 ** ADAM_STEP
    c2 = 1.0 - ADAM_B2 ** ADAM_STEP

    def body(w_ref, g_ref, m_ref, v_ref, d_ref, nm_ref, nv_ref):
        gv = g_ref[...]
        nm = ADAM_B1 * m_ref[...] + (1.0 - ADAM_B1) * gv
        nv = ADAM_B2 * v_ref[...] + (1.0 - ADAM_B2) * (gv * gv)
        nm_ref[...] = nm
        nv_ref[...] = nv
        d_ref[...] = -ADAM_LR * ((nm / c1) / (jnp.sqrt(nv / c2) + ADAM_EPS) + ADAM_WD * w_ref[...])

    spec = _row(tr, C)
    return pl.pallas_call(body, name=name, grid=(R // tr,), in_specs=[spec] * 4, out_specs=[spec] * 3,
                          out_shape=[SDS((R, C), F32)] * 3, compiler_params=_cp("parallel"))(w, g, m, v)


ANY = pl.BlockSpec(memory_space=pl.ANY)


def _coords():
    return lax.axis_index("x"), lax.axis_index("y"), lax.axis_index("c")


def _other_chips(x, y):
    return [(1 - x, y), (x, 1 - y), (1 - x, 1 - y)]


def _remote(src, dst, send_sems, recv_sems, k, dev):
    return pltpu.make_async_remote_copy(src_ref=src, dst_ref=dst, send_sem=send_sems.at[k], recv_sem=recv_sems.at[k],
                                        device_id=dev, device_id_type=MESH)


LOCAL_CHUNKS = 8


class Exchange:
    def __init__(self, ins, out_shapes, scratch, start, wait):
        self.ins, self.out_shapes, self.scratch, self.start, self.wait = list(ins), list(out_shapes), list(scratch), start, wait


def run_exchange(ex, *, name):
    ni, no = len(ex.ins), len(ex.out_shapes)

    def body(*refs):
        parts = refs[:ni], refs[ni:ni + no], refs[ni + no:]
        ex.start(*parts)
        ex.wait(*parts)

    return pl.pallas_call(body, name=name, in_specs=[ANY] * ni, out_specs=[ANY] * no, out_shape=ex.out_shapes,
                          scratch_shapes=ex.scratch)(*ex.ins)


def call_hosting(body, ex, *, name, grid, in_specs, out_specs, out_shape, inputs, aliases, scratch=()):
    n_in, n_out, ni, no, ns = len(inputs), len(out_shape), len(ex.ins), len(ex.out_shapes), len(scratch)
    outs_at = n_in + ni
    scr_at = outs_at + n_out + no

    def wrapped(*refs):
        own = refs[:n_in] + refs[outs_at:outs_at + n_out] + refs[scr_at:scr_at + ns]
        parts = refs[n_in:outs_at], refs[outs_at + n_out:scr_at], refs[scr_at + ns:]
        ids = [pl.program_id(d) for d in range(len(grid))]
        first = functools.reduce(jnp.logical_and, [i == 0 for i in ids])
        last = functools.reduce(jnp.logical_and, [i == g - 1 for i, g in zip(ids, grid)])

        @pl.when(first)
        def _():
            ex.start(*parts)

        body(*own)

        @pl.when(last)
        def _():
            ex.wait(*parts)

    outs = pl.pallas_call(
        wrapped, name=name, grid=grid, in_specs=list(in_specs) + [ANY] * ni, out_specs=list(out_specs) + [ANY] * no,
        out_shape=list(out_shape) + ex.out_shapes, input_output_aliases=aliases,
        scratch_shapes=list(scratch) + ex.scratch,
        compiler_params=_cp(*["arbitrary"] * len(grid)))(*inputs, *ex.ins)
    return outs[:n_out], outs[n_out:]


def allgather_ici_exchange(shards, whole=()):
    na, n_all = len(shards), len(shards) + len(whole)
    arrays = list(shards) + list(whole)

    def copies(s_refs, o_refs, sems):
        send_sems, recv_sems, _ = sems
        x, y, c = _coords()
        j = 2 * x + y
        out = []
        for a in range(n_all):
            if a < na:
                half = shards[a].shape[0] // 2
                part = (pl.ds(c * half, half),)
            else:
                part = ()
            for k, (px, py) in enumerate(_other_chips(x, y)):
                send = _remote(s_refs[a].at[part] if part else s_refs[a], o_refs[a].at[(j,) + part], send_sems, recv_sems,
                               3 * a + k, (px, py, c))
                rows = o_refs[a].at[(2 * px + py,) + part]
                out.append((send, _remote(rows, rows, send_sems, recv_sems, 3 * a + k, (px, py, c))))
        return out

    def start(s_refs, o_refs, sems):
        x, y, c = _coords()
        j = 2 * x + y
        for a in range(n_all):
            chunks = LOCAL_CHUNKS if a < na else 1
            chunk = arrays[a].shape[0] // chunks
            for q in range(chunks):
                rows = pl.ds(q * chunk, chunk)
                pltpu.make_async_copy(s_refs[a].at[rows], o_refs[a].at[j, rows], sems[2].at[a]).start()
        for send, _ in copies(s_refs, o_refs, sems):
            send.start()

    def wait(s_refs, o_refs, sems):
        x, y, c = _coords()
        j = 2 * x + y
        for send, arrival in copies(s_refs, o_refs, sems):
            arrival.wait_recv()
            send.wait_send()
        for a in range(n_all):
            pltpu.make_async_copy(s_refs[a], o_refs[a].at[j], sems[2].at[a]).wait()

    return Exchange(arrays, [SDS((N_CHIPS,) + s.shape, s.dtype) for s in arrays],
                    [pltpu.SemaphoreType.DMA((3 * n_all,)), pltpu.SemaphoreType.DMA((3 * n_all,)),
                     pltpu.SemaphoreType.DMA((n_all,))], start, wait)


def allgather_forward(gathered, *, name):
    na = len(gathered)

    def body(*refs):
        o_refs = refs[na:2 * na]
        send_sems, recv_sems = refs[2 * na:]
        x, y, c = _coords()
        sibling = (x, y, 1 - c)
        cps = []
        for a in range(na):
            half = gathered[a].shape[1] // 2
            for k, (px, py) in enumerate(_other_chips(x, y)):
                mine = o_refs[a].at[2 * px + py, pl.ds(c * half, half)]
                theirs = o_refs[a].at[2 * px + py, pl.ds((1 - c) * half, half)]
                cps.append((_remote(mine, mine, send_sems, recv_sems, 3 * a + k, sibling),
                            _remote(theirs, theirs, send_sems, recv_sems, 3 * a + k, sibling)))
        for send, _ in cps:
            send.start()
        for send, arrival in cps:
            send.wait_send()
            arrival.wait_recv()

    return pl.pallas_call(body, name=name, in_specs=[ANY] * na, out_specs=[ANY] * na,
                          out_shape=[SDS(g.shape, g.dtype) for g in gathered],
                          input_output_aliases={a: a for a in range(na)},
                          scratch_shapes=[pltpu.SemaphoreType.DMA((3 * na,)), pltpu.SemaphoreType.DMA((3 * na,))])(*gathered)


def swap_halves_exchange(gs):
    na = len(gs)

    def copies(g_refs, o_refs, sems):
        x, y, c = _coords()
        out = []
        for a in range(na):
            half = gs[a].shape[1] // 2
            out.append(_remote(g_refs[a].at[:, pl.ds((1 - c) * half, half), :], o_refs[a], sems[0], sems[1], a,
                               (x, y, 1 - c)))
        return out

    def start(g_refs, o_refs, sems):
        for cp in copies(g_refs, o_refs, sems):
            cp.start()

    def wait(g_refs, o_refs, sems):
        for cp in copies(g_refs, o_refs, sems):
            cp.wait()

    return Exchange(gs, [SDS((g.shape[0], g.shape[1] // 2, g.shape[2]), g.dtype) for g in gs],
                    [pltpu.SemaphoreType.DMA((na,)), pltpu.SemaphoreType.DMA((na,))], start, wait)


def chip_partials_exchange(pbs):
    na = len(pbs)

    def copies(p_refs, o_refs, sems):
        x, y, c = _coords()
        out = []
        for a in range(na):
            for k, (px, py) in enumerate(_other_chips(x, y)):
                out.append(_remote(p_refs[a].at[2 * px + py], o_refs[a].at[k], sems[0], sems[1], 3 * a + k, (px, py, c)))
        return out

    def start(p_refs, o_refs, sems):
        for cp in copies(p_refs, o_refs, sems):
            cp.start()

    def wait(p_refs, o_refs, sems):
        for cp in copies(p_refs, o_refs, sems):
            cp.wait()

    return Exchange(pbs, [SDS((3,) + p.shape[1:], p.dtype) for p in pbs],
                    [pltpu.SemaphoreType.DMA((3 * na,)), pltpu.SemaphoreType.DMA((3 * na,))], start, wait)


def add_sibling_half(g, land, c_idx, *, name):
    n, R, C = g.shape
    half = R // 2
    tr = _pick(half, 256)
    nt = half // tr

    def body(c_ref, g_ref, l_ref, of_ref, ob_ref):
        s = g_ref[...] + l_ref[...].astype(F32)
        of_ref[...] = s
        ob_ref[...] = s.astype(ob_ref.dtype)

    blk = pl.BlockSpec((1, tr, C), lambda s, i, c_ref: (s, i, 0))
    gblk = pl.BlockSpec((1, tr, C), lambda s, i, c_ref: (s, c_ref[0] * nt + i, 0))
    return pl.pallas_call(
        body, name=name,
        grid_spec=pltpu.PrefetchScalarGridSpec(num_scalar_prefetch=1, grid=(n, nt), in_specs=[gblk, blk],
                                               out_specs=[blk, blk]),
        out_shape=[SDS((n, half, C), F32), SDS((n, half, C), WIRE_DTYPE)],
        compiler_params=_cp("parallel", "parallel"))(c_idx, g, land)


def add_chip_partials(pf, rb, jc_idx, *, name):
    n, H, C = pf.shape
    tr = _pick(H, 256)

    def body(jc_ref, p_ref, r_ref, o_ref):
        s = p_ref[0]
        for k in range(3):
            s = s + r_ref[k].astype(F32)
        o_ref[...] = s

    pblk = pl.BlockSpec((1, tr, C), lambda i, jc_ref: (jc_ref[0], i, 0))
    rblk = pl.BlockSpec((3, tr, C), lambda i, jc_ref: (0, i, 0))
    oblk = pl.BlockSpec((None, tr, C), lambda i, jc_ref: (jc_ref[1], i, 0))
    return pl.pallas_call(
        body, name=name,
        grid_spec=pltpu.PrefetchScalarGridSpec(num_scalar_prefetch=1, grid=(H // tr,), in_specs=[pblk, rblk],
                                               out_specs=oblk),
        out_shape=SDS((2, H, C), F32), compiler_params=_cp("parallel"))(jc_idx, pf, rb)


def join_sibling_halves(bufs, *, name):
    na = len(bufs)

    def body(*refs):
        o_refs = refs[na:2 * na]
        send_sems, recv_sems = refs[2 * na:]
        x, y, c = _coords()
        cps = [_remote(o_refs[a].at[c], o_refs[a].at[c], send_sems, recv_sems, a, (x, y, 1 - c)) for a in range(na)]
        for cp in cps:
            cp.start()
        for a in range(na):
            cps[a].wait_send()
            _remote(o_refs[a].at[1 - c], o_refs[a].at[1 - c], send_sems, recv_sems, a, (x, y, 1 - c)).wait_recv()

    return pl.pallas_call(body, name=name, in_specs=[ANY] * na, out_specs=[ANY] * na,
                          out_shape=[SDS(b.shape, b.dtype) for b in bufs],
                          input_output_aliases={a: a for a in range(na)},
                          scratch_shapes=[pltpu.SemaphoreType.DMA((na,)), pltpu.SemaphoreType.DMA((na,))])(*bufs)


def exchange_pieces(v, *, scatter, name):
    P, C = v.shape[-2:]

    def body(v_ref, o_ref, send_sems, recv_sems, local_sem):
        x, y, c = _coords()
        me = 4 * x + 2 * y + c
        local = pltpu.make_async_copy(v_ref.at[me] if scatter else v_ref, o_ref.at[me], local_sem)
        local.start()
        cps = []
        for m in range(1, N_DEV):
            px = (1 - x) if m & 4 else x
            py = (1 - y) if m & 2 else y
            pc = (1 - c) if m & 1 else c
            src = v_ref.at[4 * px + 2 * py + pc] if scatter else v_ref
            cps.append(_remote(src, o_ref.at[me], send_sems, recv_sems, m - 1, (px, py, pc)))
        for cp in cps:
            cp.start()
        for cp in cps:
            cp.wait_send()
        for m in range(1, N_DEV):
            px = (1 - x) if m & 4 else x
            py = (1 - y) if m & 2 else y
            pc = (1 - c) if m & 1 else c
            slot = o_ref.at[4 * px + 2 * py + pc]
            _remote(slot, slot, send_sems, recv_sems, m - 1, (px, py, pc)).wait_recv()
        local.wait()

    return pl.pallas_call(body, name=name, in_specs=[ANY], out_specs=ANY, out_shape=SDS((N_DEV, P, C), v.dtype),
                          scratch_shapes=[pltpu.SemaphoreType.DMA((N_DEV - 1,)), pltpu.SemaphoreType.DMA((N_DEV - 1,)),
                                          pltpu.SemaphoreType.DMA(())])(v)


def sum_pieces(land, *, name):
    n, P, C = land.shape

    def body(l_ref, o_ref):
        s = l_ref[0]
        for d in range(1, n):
            s = s + l_ref[d]
        o_ref[...] = s

    return pl.pallas_call(body, name=name, out_shape=SDS((P, C), F32))(land)


BIG_SEGS = (
    ("w_in_even", (1024, 514), 1),
    ("s5_w_glu", (128, 512), 0),
    ("w_out_even", (256, 1024), 0),
    ("w_in_odd", (1024, 384), 1),
    ("w_out_odd", (256, 1024), 0),
    ("mlp_w1", (2, 1024, 1024), 2),
    ("mlp_w2", (2, 1024, 1024), 1),
)
BIG_NAMES = tuple(n for n, _, _ in BIG_SEGS)
EARLY_NAMES = ("w_in_even", "s5_w_glu")
LATE_NAMES = ("w_out_even", "w_in_odd", "w_out_odd", "mlp_w1", "mlp_w2")
REDUCED_EARLY = ("s5_w_glu", "w_out_even", "w_in_odd", "w_out_odd", "mlp_w1", "mlp_w2")
SHARDED_SMALL = ("pool_scale", "sgu_ln_g", "sgu_ln_b")
SMALL_SEGS = (
    ("mix_pre_g", (2, 1024)), ("mix_post_g", (2, 1024)), ("mlp_pre_g", (2, 1024)), ("mlp_post_g", (2, 1024)),
    ("s5_lam_re", (1, 32, 64)), ("s5_lam_im", (1, 32, 64)), ("s5_log_dt", (1, 32)),
    ("s5_b_re", (1, 32, 64, 16)), ("s5_b_im", (1, 32, 64, 16)), ("s5_c_re", (1, 32, 16, 64)), ("s5_c_im", (1, 32, 16, 64)),
    ("s5_d", (1, 512)), ("fox_b_f", (1, 8)), ("pool_w", (1, 4, 128, 128)), ("sgu_w_s", (1, 4, 128, 128)),
    ("sgu_b_s", (1, 4, 128)),
)
REDUCED_SEGS = SMALL_SEGS + tuple((n, (1, 512)) for n in SHARDED_SMALL) + (("loss", (1, 1)),)


def _cols_from_chips(g):
    n, R, C = g.shape
    return jnp.transpose(g, (1, 0, 2)).reshape(R, n * C)


def _chips_from_cols(m):
    R, C4 = m.shape
    return jnp.transpose(m.reshape(R, N_CHIPS, C4 // N_CHIPS), (1, 0, 2))


MLP_SHARD = 1024


def _w1_cols(l):
    def spec(tm, tn, tk):
        per = MLP_SHARD // tn
        return pl.BlockSpec((None, tk, tn), lambda i, j, k: (j // per, l * (MLP_SHARD // tk) + k, j % per))
    return spec


def _w1_rows_t(l):
    def spec(tm, tn, tk):
        if tk == N_CHIPS * MLP_SHARD:
            return pl.BlockSpec((N_CHIPS, tn, MLP_SHARD), lambda i, j, k: (0, l * (MLP_SHARD // tn) + j, 0))
        per = MLP_SHARD // tk
        return pl.BlockSpec((None, tn, tk), lambda i, j, k: (k // per, l * (MLP_SHARD // tn) + j, k % per))
    return spec


def _w2_rows(l):
    def spec(tm, tn, tk):
        if tk == N_CHIPS * MLP_SHARD:
            return pl.BlockSpec((N_CHIPS, MLP_SHARD, tn), lambda i, j, k: (0, l, j))
        per = MLP_SHARD // tk
        return pl.BlockSpec((None, tk, tn), lambda i, j, k: (k // per, l * per + k % per, j))
    return spec


def _w2_rows_t(l):
    def spec(tm, tn, tk):
        per = MLP_SHARD // tn
        return pl.BlockSpec((None, tn, tk), lambda i, j, k: (j // per, l * per + j % per, k))
    return spec


def _dw1_out(l):
    def spec(tm, tn, tk):
        per = MLP_SHARD // tn
        return pl.BlockSpec((None, tm, tn), lambda i, j, k: (j // per, l * (MLP_SHARD // tm) + i, j % per))
    return spec


def _dw2_out(l):
    def spec(tm, tn, tk):
        per = MLP_SHARD // tm
        return pl.BlockSpec((None, tm, tn), lambda i, j, k: (i // per, l * per + i % per, j))
    return spec


def _pack_vec(d, segs, rows_multiple):
    flat = jnp.concatenate([d[n].reshape(-1) for n, _ in segs])
    rows = -(-flat.shape[0] // LANES)
    rows = -(-rows // rows_multiple) * rows_multiple
    return jnp.pad(flat, (0, rows * LANES - flat.shape[0])).reshape(rows, LANES)


def _unpack_vec(v, segs):
    flat, out, r = v.reshape(-1), {}, 0
    for n, shape in segs:
        k = math.prod(shape)
        out[n] = flat[r:r + k].reshape(shape)
        r += k
    return out


def _block_diag(blocks):
    G, a, b = blocks.shape
    eye = jnp.eye(G, dtype=blocks.dtype)
    return (eye[:, None, :, None] * blocks[:, :, None, :]).reshape(G * a, G * b)


def _diag_blocks(m, G):
    a, b = m.shape[0] // G, m.shape[1] // G
    return jnp.stack([m[g * a:(g + 1) * a, g * b:(g + 1) * b] for g in range(G)])


def _sqrelu_epi(acc):
    r = jnp.maximum(acc, 0.0)
    return acc, r * r


def _sqrelu_bwd_epi(acc, a):
    return (acc * (2.0 * jnp.maximum(a.astype(F32), 0.0)),)


def _mlp_fwd(h, g1, g2, l, tag):
    T, D = h.shape
    a, s = matmul(h, g1, name=f"{tag}_up", mnk=(T, D_FF, D), b_spec=_w1_cols(l), epi=_sqrelu_epi,
                  out_dtypes=(MXU_DTYPE, MXU_DTYPE))
    m = matmul(s, g2, name=f"{tag}_down", mnk=(T, D, D_FF), b_spec=_w2_rows(l))
    return m, (h, a, s)


def _mlp_bwd(saved, dm, g1, g2, l, dg1, dg2, tag):
    h, a, s = saved
    T, D = h.shape
    gshape = (N_CHIPS, 2 * MLP_SHARD, MLP_SHARD)
    da = matmul(dm, g2, tb=True, name=f"{tag}_down_dx", mnk=(T, D_FF, D), b_spec=_w2_rows_t(l),
                epi=_sqrelu_bwd_epi, epi_in=(a,), out_dtype=MXU_DTYPE)
    dg2 = matmul(s, dm, ta=True, name=f"{tag}_down_dw", tm=MLP_SHARD, o_spec=_dw2_out(l), o_shape=gshape, prev=dg2)
    dh = matmul(da, g1, tb=True, name=f"{tag}_up_dx", mnk=(T, D, D_FF), b_spec=_w1_rows_t(l))
    dg1 = matmul(h, da, ta=True, name=f"{tag}_up_dw", o_spec=_dw1_out(l), o_shape=gshape, prev=dg1)
    return dh, dg1, dg2


def kernel(x, mix_pre_g, mix_post_g, mlp_pre_g, mlp_post_g, w_in_even, s5_lam_re, s5_lam_im, s5_log_dt, s5_b_re, s5_b_im, s5_c_re, s5_c_im, s5_d, s5_w_glu, fox_b_f, w_out_even, w_in_odd, pool_w, pool_scale, sgu_ln_g, sgu_ln_b, sgu_w_s, sgu_b_s, w_out_odd, mlp_w1, mlp_w2, loss_target, m_mix_pre_g, m_mix_post_g, m_mlp_pre_g, m_mlp_post_g, m_w_in_even, m_s5_lam_re, m_s5_lam_im, m_s5_log_dt, m_s5_b_re, m_s5_b_im, m_s5_c_re, m_s5_c_im, m_s5_d, m_s5_w_glu, m_fox_b_f, m_w_out_even, m_w_in_odd, m_pool_w, m_pool_scale, m_sgu_ln_g, m_sgu_ln_b, m_sgu_w_s, m_sgu_b_s, m_w_out_odd, m_mlp_w1, m_mlp_w2, v_mix_pre_g, v_mix_post_g, v_mlp_pre_g, v_mlp_post_g, v_w_in_even, v_s5_lam_re, v_s5_lam_im, v_s5_log_dt, v_s5_b_re, v_s5_b_im, v_s5_c_re, v_s5_c_im, v_s5_d, v_s5_w_glu, v_fox_b_f, v_w_out_even, v_w_in_odd, v_pool_w, v_pool_scale, v_sgu_ln_g, v_sgu_ln_b, v_sgu_w_s, v_sgu_b_s, v_w_out_odd, v_mlp_w1, v_mlp_w2):
    names = [n for n, _ in SMALL_SEGS] + [n for n, _, _ in BIG_SEGS] + list(SHARDED_SMALL)
    env = dict(locals())
    W = {n: env[n] for n in names}
    M = {n: env["m_" + n] for n in names}
    V = {n: env["v_" + n] for n in names}

    def shard(n):
        return W[n].reshape(-1, W[n].shape[-1]).astype(WIRE_DTYPE)

    small = jnp.pad(jnp.concatenate([W[n] for n in SHARDED_SMALL]), ((0, SUBLANES - len(SHARDED_SMALL)), (0, 0)))
    loss8, dx0, halves, local_small = _local_step(x[0], loss_target[0], {n: W[n] for n, _ in SMALL_SEGS},
                                                  [shard(n) for n in EARLY_NAMES], [shard(n) for n in LATE_NAMES], small)
    return _reduce_and_update(W, M, V, loss8, dx0, halves, local_small)


def _reduce_to_my_half(gs, names, tag, carry_swap=None, carry_ici=None):
    cx, cy, cc = _coords()
    c_idx = cc.reshape(1).astype(jnp.int32)
    jc_idx = jnp.stack([2 * cx + cy, cc]).astype(jnp.int32)
    swap = swap_halves_exchange(gs)
    from_sibling = carry_swap(swap) if carry_swap else run_exchange(swap, name=f"{tag}_to_sibling")
    sums = [add_sibling_half(g, l, c_idx, name=f"{tag}_chip_sum_{n}") for n, g, l in zip(names, gs, from_sibling)]
    send = chip_partials_exchange([pb for _, pb in sums])
    from_chips = carry_ici(send) if carry_ici else run_exchange(send, name=f"{tag}_to_chips")
    return [add_chip_partials(pf, r, jc_idx, name=f"{tag}_sum_{n}") for n, (pf, _), r in zip(names, sums, from_chips)]


def _local_step(x0, target, P, early_shards, late_shards, small_shard):
    T = x0.shape[0]
    mix_pre_g, mix_post_g, mlp_pre_g, mlp_post_g = P["mix_pre_g"], P["mix_post_g"], P["mlp_pre_g"], P["mlp_post_g"]
    s5_lam_re, s5_lam_im, s5_log_dt = P["s5_lam_re"], P["s5_lam_im"], P["s5_log_dt"]
    s5_b_re, s5_b_im, s5_c_re, s5_c_im, s5_d = P["s5_b_re"], P["s5_b_im"], P["s5_c_re"], P["s5_c_im"], P["s5_d"]
    fox_b_f, pool_w, sgu_w_s, sgu_b_s = P["fox_b_f"], P["pool_w"], P["sgu_w_s"], P["sgu_b_s"]

    def gain(a, l):
        return a[l][None, :]

    lr = s5_lam_re[0].reshape(1, S5_LANES)
    li = s5_lam_im[0].reshape(1, S5_LANES)
    ldt = jnp.repeat(s5_log_dt[0], S5_STATE).reshape(1, S5_LANES)
    btr = s5_b_re[0].reshape(S5_LANES, S5_GROUP).T
    bti = s5_b_im[0].reshape(S5_LANES, S5_GROUP).T
    tf_re, tf_im, tb_re, tb_im, bbt_re, bbt_im = s5_disc_fwd(lr, li, ldt, btr, bti, name="s5_disc")
    same_group = (jnp.arange(S5_WIDTH)[:, None] // S5_GROUP) == (jnp.arange(S5_LANES)[None, :] // S5_STATE)
    b_bd = s5_interleave(jnp.where(same_group, jnp.tile(bbt_re, (S5_GROUPS, 1)), 0.0),
                         jnp.where(same_group, jnp.tile(bbt_im, (S5_GROUPS, 1)), 0.0), axis=1)
    cr2 = jnp.transpose(s5_c_re[0], (0, 2, 1)).reshape(S5_LANES, S5_GROUP)
    ci2 = jnp.transpose(s5_c_im[0], (0, 2, 1)).reshape(S5_LANES, S5_GROUP)
    c_bd = s5_interleave(jnp.where(same_group.T, jnp.tile(cr2, (1, S5_GROUPS)), 0.0),
                         -jnp.where(same_group.T, jnp.tile(ci2, (1, S5_GROUPS)), 0.0), axis=0)
    bf_pad = jnp.pad(fox_b_f, ((0, 0), (0, LANES - FOX_HEADS)))

    h1, early = rms_fwd(x0, gain(mix_pre_g, 0), allgather_ici_exchange(early_shards), name="l0_pre_norm")
    early = dict(zip(EARLY_NAMES, allgather_forward(early, name="allgather_early_weights")))
    w_in_e = jnp.pad(_cols_from_chips(early["w_in_even"]), ((0, 0), (0, EVEN_IN_PAD - EVEN_IN)))
    w_glu = early["s5_w_glu"].reshape(S5_WIDTH, S5_WIDTH)
    z = matmul(h1, w_in_e, name="l0_in_proj")
    s5_tiles = dict(tm=_pick(T, S5_NB), exact_tiles=True)
    xs = s5_scan(z, b_bd, tf_re, tf_im, reverse=False, name="s5_scan_fwd")
    yc = matmul(xs, c_bd, mnk=(T, S5_WIDTH, 2 * S5_NB), tn=S5_CB, a_spec=_lanes_of_chan, b_spec=_s5_c_block,
                name="s5_cx", **s5_tiles)
    yl, yg = s5_out_fwd(yc, z, s5_d, name="s5_out")
    gl = matmul(yg, w_glu, name="s5_glu_proj")
    ycat = glu_fwd(yg, gl, out_cols=D_MODEL, name="s5_glu")
    fgate = fox_gate_fwd(z, bf_pad, fl_col=FL_TILE, name="fox_gate")
    f_col = _pairs_col(fgate, T)
    f_row = _col_to_row(f_col, T)
    (ycat, lse_col), late = fox_fwd(z, f_col, f_row, ycat, allgather_ici_exchange(late_shards, [small_shard]),
                                    name="fox_fwd")
    small_all = late[-1]
    pool_scale_f, ln_g_f, ln_b_f = (small_all[:, i, :].reshape(1, N_CHIPS * LANES) for i in range(len(SHARDED_SMALL)))
    late = dict(zip(LATE_NAMES, allgather_forward(late[:-1], name="allgather_late_weights")))
    w_in_o = _cols_from_chips(late["w_in_odd"])
    w_in_o = jnp.concatenate([w_in_o[:, S5_WIDTH:], w_in_o[:, :S5_WIDTH]], axis=1)
    w_out_e = late["w_out_even"].reshape(D_MODEL, D_MODEL)
    w_out_o = late["w_out_odd"].reshape(D_MODEL, D_MODEL)
    g1, g2 = late["mlp_w1"], late["mlp_w2"]
    mo = matmul(ycat, w_out_e, name="l0_out_proj")
    x1, h2 = res_norm_fwd(x0, mo, gain(mix_post_g, 0), gain(mlp_pre_g, 0), name="l0_post_mlp0_pre_norm")
    m0, mlp0 = _mlp_fwd(h2, g1, g2, 0, "mlp0")

    x2, h3 = res_norm_fwd(x1, m0, gain(mlp_post_g, 0), gain(mix_pre_g, 1), name="mlp0_post_l1_pre_norm")
    z2 = matmul(h3, w_in_o, name="l1_in_proj")
    pooled = pool_window(z2, adjoint=False, in_col=POOL_COL, out_dtype=MXU_DTYPE, name="pool_fwd")
    pw_bd = _block_diag(pool_w[0])
    pw = matmul(pooled, pw_bd, name="pool_proj")
    ycat2 = colscale_fwd(pw, pool_scale_f, out_cols=D_MODEL, name="pool_scale")
    causal = jnp.tril(jnp.ones((CHUNK, CHUNK), dtype=bool))
    wsm = jnp.where(causal[None], sgu_w_s[0], 0.0)
    wsmt = jnp.transpose(wsm, (0, 2, 1))
    bst = sgu_b_s[0].T
    ycat2 = sgu_fwd(z2, ln_g_f, ln_b_f, wsm, bst, ycat2, name="sgu_fwd")
    mo2 = matmul(ycat2, w_out_o, name="l1_out_proj")
    x3, h4 = res_norm_fwd(x2, mo2, gain(mix_post_g, 1), gain(mlp_pre_g, 1), name="l1_post_mlp1_pre_norm")
    m1, mlp1 = _mlp_fwd(h4, g1, g2, 1, "mlp1")
    loss8, dx4 = res_norm_loss(x3, m1, gain(mlp_post_g, 1), target, name="mlp1_post_norm_loss")

    dm1, dg_mlp_post1 = rms_bwd(m1, gain(mlp_post_g, 1), dx4, None, name="mlp1_post_norm_bwd")
    dh4, dg1, dg2 = _mlp_bwd(mlp1, dm1, g1, g2, 1, None, None, "mlp1")
    dx3, dmo2, dg_mlp_pre1, dg_mix_post1 = norm_res_bwd(x3, gain(mlp_pre_g, 1), dh4, dx4, mo2, gain(mix_post_g, 1),
                                                        name="mlp1_pre_l1_post_norm_bwd")
    dycat2 = matmul(dmo2, w_out_o, tb=True, name="l1_out_proj_dx")
    dw_out_o = matmul(ycat2, dmo2, ta=True, name="l1_out_proj_dw")
    dpw, dpool_scale = colscale_bwd(pw, pool_scale_f, dycat2, name="pool_scale_bwd")
    dpooled = matmul(dpw, pw_bd, tb=True, name="pool_proj_dx")
    dpw_bd = matmul(pooled, dpw, ta=True, name="pool_proj_dw")
    dz2, dln_g, dln_b, dws, dbst = sgu_bwd(z2, ln_g_f, ln_b_f, wsm, wsmt, bst, dycat2, out_cols=3 * S5_WIDTH,
                                           name="sgu_bwd")
    dz2 = pool_window(dpooled, adjoint=True, into=dz2, out_col=POOL_COL, name="pool_bwd")
    dh3 = matmul(dz2, w_in_o, tb=True, name="l1_in_proj_dx")
    dw_in_o = matmul(h3, dz2, ta=True, name="l1_in_proj_dw")
    dw_in_o = jnp.concatenate([dw_in_o[:, 2 * S5_WIDTH:], dw_in_o[:, :2 * S5_WIDTH]], axis=1)
    dx2, dm0, dg_mix_pre1, dg_mlp_post0 = norm_res_bwd(x2, gain(mix_pre_g, 1), dh3, dx3, m0, gain(mlp_post_g, 0),
                                                       name="l1_pre_mlp0_post_norm_bwd")

    dh2, dg1, dg2 = _mlp_bwd(mlp0, dm0, g1, g2, 0, dg1, dg2, "mlp0")
    dx1, dmo, dg_mlp_pre0, dg_mix_post0 = norm_res_bwd(x1, gain(mlp_pre_g, 0), dh2, dx2, mo, gain(mix_post_g, 0),
                                                       name="mlp0_pre_l0_post_norm_bwd")
    dycat = matmul(dmo, w_out_e, tb=True, name="l0_out_proj_dx")
    dw_out_e = matmul(ycat, dmo, ta=True, name="l0_out_proj_dw")
    dyg_a, dgl = glu_bwd(yg, gl, dycat, name="s5_glu_bwd")
    dyg_b = matmul(dgl, w_glu, tb=True, name="s5_glu_proj_dx")
    dw_glu = matmul(yg, dgl, ta=True, name="s5_glu_proj_dw")
    dyl, du_skip, dd = s5_out_bwd(yl, z, s5_d, dyg_a, dyg_b, name="s5_out_bwd")
    dc_blocks = matmul(xs, dyl, ta=True, mnk=(2 * S5_LANES, S5_CB, T), tm=S5_NB, tn=S5_CB, b_spec=_chan_cols_of_i,
                       exact_tiles=True, name="s5_cx_dw")
    early_grads = {"s5_w_glu": dw_glu.reshape(N_CHIPS, -1, S5_WIDTH), "w_out_even": dw_out_e.reshape(N_CHIPS, -1, D_MODEL),
                   "w_in_odd": _chips_from_cols(dw_in_o), "w_out_odd": dw_out_o.reshape(N_CHIPS, -1, D_MODEL),
                   "mlp_w1": dg1, "mlp_w2": dg2}
    got = {}

    def reverse_scan(exchange):
        (got["lam"], got["dab_re"], got["dab_im"]), bufs = s5_scan(dyl, c_bd, tb_re, tb_im, reverse=True, states=xs,
                                                                   hosted=exchange, name="s5_scan_bwd")
        return bufs

    def attention_bwd(exchange):
        dd_col = fox_dd(ycat, dycat, name="fox_dd")
        (got["dk"], got["dv"], got["dfk"], got["dqt"], got["dfq"]), bufs = fox_bwd(
            z, dycat, f_col, f_row, _col_to_row(lse_col, T), _col_to_row(dd_col, T), exchange, name="fox_bwd")
        return bufs

    halves = _reduce_to_my_half([early_grads[n] for n in REDUCED_EARLY], REDUCED_EARLY, "early_grads",
                                reverse_scan, attention_bwd)
    lam, dab_re, dab_im, dk, dv = got["lam"], got["dab_re"], got["dab_im"], got["dk"], got["dv"]
    db_blocks = matmul(z, lam, ta=True, mnk=(S5_CB, 2 * S5_LANES, T), tm=S5_CB, tn=S5_NB, a_spec=_chan_rows_t,
                       exact_tiles=True, name="s5_bu_dw")
    du_b = matmul(lam, b_bd, tb=True, mnk=(T, S5_WIDTH, 2 * S5_NB), tn=S5_CB, a_spec=_lanes_of_chan,
                  b_spec=_s5_b_block_t, name="s5_bu_dx", **s5_tiles)
    du = add2(du_skip, du_b, name="s5_du")
    dq = jnp.transpose(got["dqt"], (1, 3, 0, 2)).reshape(T, FOX_WIDTH) * (FOX_HEAD_DIM ** -0.5)
    dfl, dbf = fox_gate_bwd(z, bf_pad, _pairs_to_lanes(got["dfk"], T), _pairs_to_lanes(_row_to_col(got["dfq"], T), T),
                            fl_col=FL_TILE, name="fox_gate_bwd")
    dz = jnp.concatenate([du, dq, dk, dv, dfl], axis=1).astype(MXU_DTYPE)
    dw_in_e = matmul(h1, dz, ta=True, name="l0_in_proj_dw")[:, :EVEN_IN]

    def in_proj_dx(exchange):
        got["dh1"], bufs = matmul(dz, w_in_e, tb=True, hosted=exchange, name="l0_in_proj_dx")
        return bufs

    def pre_norm_bwd(exchange):
        (got["dx0"], got["dg_mix_pre0"]), bufs = rms_bwd(x0, gain(mix_pre_g, 0), got["dh1"], dx1, hosted=exchange,
                                                         name="l0_pre_norm_bwd")
        return bufs

    halves = halves + _reduce_to_my_half([_chips_from_cols(dw_in_e)], ["w_in_even"], "late_grads", in_proj_dx, pre_norm_bwd)
    dx0, dg_mix_pre0 = got["dx0"], got["dg_mix_pre0"]

    groups_per_block = S5_CB // S5_GROUP
    own_group = (jnp.arange(S5_CB)[:, None] // S5_GROUP) == ((jnp.arange(S5_LANES)[None, :] // S5_STATE) % groups_per_block)
    db_re, db_im = s5_deinterleave(db_blocks, axis=1)
    dbbt_re = jnp.where(own_group, db_re, 0.0).reshape(groups_per_block, S5_GROUP, S5_LANES).sum(0)
    dbbt_im = jnp.where(own_group, db_im, 0.0).reshape(groups_per_block, S5_GROUP, S5_LANES).sum(0)
    dlr, dli, dldt8, dbtr, dbti = s5_disc_bwd(lr, li, ldt, btr, bti, dab_re, dab_im, dbbt_re, dbbt_im, name="s5_disc_bwd")
    dc_re, dc_im = s5_deinterleave(dc_blocks, axis=0)
    dcr2 = jnp.where(own_group.T, dc_re, 0.0).reshape(S5_LANES, groups_per_block, S5_GROUP).sum(1)
    dci2 = -jnp.where(own_group.T, dc_im, 0.0).reshape(S5_LANES, groups_per_block, S5_GROUP).sum(1)

    def c_layout(a):
        return jnp.transpose(a.reshape(S5_GROUPS, S5_STATE, S5_GROUP), (0, 2, 1))[None]

    def b_layout(a):
        return a.T.reshape(1, S5_GROUPS, S5_STATE, S5_GROUP)

    local_small = {
        "mix_pre_g": jnp.concatenate([dg_mix_pre0, dg_mix_pre1]), "mix_post_g": jnp.concatenate([dg_mix_post0, dg_mix_post1]),
        "mlp_pre_g": jnp.concatenate([dg_mlp_pre0, dg_mlp_pre1]), "mlp_post_g": jnp.concatenate([dg_mlp_post0, dg_mlp_post1]),
        "s5_lam_re": dlr.reshape(1, S5_GROUPS, S5_STATE), "s5_lam_im": dli.reshape(1, S5_GROUPS, S5_STATE),
        "s5_log_dt": dldt8[0:1, 0:S5_GROUPS],
        "s5_b_re": b_layout(dbtr), "s5_b_im": b_layout(dbti), "s5_c_re": c_layout(dcr2), "s5_c_im": c_layout(dci2),
        "s5_d": dd, "fox_b_f": dbf[:, 0:FOX_HEADS],
        "pool_w": _diag_blocks(dpw_bd, len(POOL_WINDOWS))[None],
        "sgu_w_s": jnp.where(causal[None], dws, 0.0)[None], "sgu_b_s": dbst.T[None],
        "pool_scale": dpool_scale, "sgu_ln_g": dln_g, "sgu_ln_b": dln_b,
    }
    return loss8, dx0, dict(zip(REDUCED_EARLY + ("w_in_even",), halves)), local_small


def _reduce_and_update(W, M, V, loss8, dx0, halves, local_small):
    cx, cy, cc = _coords()
    chip = 2 * cx + cy

    summed = dict(local_small, loss=loss8[0:1, 0:1])
    vec = _pack_vec(summed, REDUCED_SEGS, N_DEV * SUBLANES)
    piece = vec.shape[0] // N_DEV
    landed = exchange_pieces(vec.reshape(N_DEV, piece, LANES), scatter=True, name="small_grads_scatter")
    mine = sum_pieces(landed, name="small_grads_sum")
    everyone = exchange_pieces(mine, scatter=False, name="small_grads_gather")
    G = _unpack_vec(everyone, REDUCED_SEGS)
    loss = G["loss"].reshape(())
    for n in SHARDED_SMALL:
        G[n] = lax.dynamic_slice_in_dim(G[n], chip * LANES, LANES, axis=1)

    reduced = join_sibling_halves([halves[n] for n in BIG_NAMES], name="big_grads_join")
    for n, r in zip(BIG_NAMES, reduced):
        G[n] = r.reshape(W[n].shape)

    def two_d(a):
        return a.reshape(-1, a.shape[-1])

    delta, new_m, new_v = {}, {}, {}
    for n in BIG_NAMES:
        d_, m_, v_ = adamw(two_d(W[n]), two_d(G[n]), two_d(M[n]), two_d(V[n]), name=f"adamw_{n}")
        delta[n], new_m[n], new_v[n] = (t.reshape(W[n].shape) for t in (d_, m_, v_))
    packed = [_pack_vec(src, SMALL_SEGS, SUBLANES) for src in (W, G, M, V)]
    outs = adamw(*packed, name="adamw_replicated")
    for dst, t in zip((delta, new_m, new_v), outs):
        dst.update(_unpack_vec(t, SMALL_SEGS))
    sharded_segs = tuple((n, (1, LANES)) for n in SHARDED_SMALL)
    packed = [_pack_vec(src, sharded_segs, 1) for src in (W, G, M, V)]
    outs = adamw(*packed, name="adamw_sharded_vectors")
    for dst, t in zip((delta, new_m, new_v), outs):
        dst.update(_unpack_vec(t, sharded_segs))

    order = ["mix_pre_g", "mix_post_g", "mlp_pre_g", "mlp_post_g", "w_in_even", "s5_lam_re", "s5_lam_im", "s5_log_dt",
             "s5_b_re", "s5_b_im", "s5_c_re", "s5_c_im", "s5_d", "s5_w_glu", "fox_b_f", "w_out_even", "w_in_odd",
             "pool_w", "pool_scale", "sgu_ln_g", "sgu_ln_b", "sgu_w_s", "sgu_b_s", "w_out_odd", "mlp_w1", "mlp_w2"]
    return (loss, dx0[None], *[G[n] for n in order], *[delta[n] for n in order],
            *[new_m[n] for n in order], *[new_v[n] for n in order])
```

```python
import functools
import math

import jax
import jax.numpy as jnp
from jax import lax
from jax.experimental import pallas as pl
from jax.experimental.pallas import tpu as pltpu

F32 = jnp.float32
MXU_DTYPE = jnp.bfloat16
WIRE_DTYPE = jnp.bfloat16
EPS = 1e-6
VMEM_LIMIT_BYTES = 48 * 1024 * 1024
LANES = 128
SUBLANES = 8

D_MODEL = 1024
S5_WIDTH = 512
S5_GROUP = 16
S5_GROUPS = 32
S5_STATE = 64
S5_LANES = S5_GROUPS * S5_STATE
FOX_HEADS = 8
FOX_HEAD_DIM = 64
FOX_WIDTH = 512
EVEN_IN = S5_WIDTH + 3 * FOX_WIDTH + FOX_HEADS
EVEN_IN_PAD = 2176
POOL_WINDOWS = (2, 4, 8, 16)
POOL_HALO = 16
POOL_GROUP_DIM = 128
SGU_GROUPS = 4
SGU_GROUP_DIM = 128
CHUNK = 128
D_FF = 4096

ADAM_LR = 0.001
ADAM_B1 = 0.9
ADAM_B2 = 0.999
ADAM_EPS = 1e-08
ADAM_WD = 0.01
ADAM_STEP = 10

MESH_AXES = ("x", "y", "c")
MESH = pl.DeviceIdType.MESH
N_CHIPS = 4
N_DEV = 8

SDS = jax.ShapeDtypeStruct


def _cp(*sem):
    return pltpu.CompilerParams(dimension_semantics=sem, vmem_limit_bytes=VMEM_LIMIT_BYTES)


def _pick(dim, pref):
    if dim <= pref:
        return dim
    t = pref
    while t >= 256:
        if dim % t == 0:
            return t
        t //= 2
    return dim


def _row(tr, c):
    return pl.BlockSpec((tr, c), lambda i: (i, 0))


def _full(shape):
    nd = len(shape)
    return pl.BlockSpec(shape, lambda *_: (0,) * nd)


def _gelu_grad(x):
    c = math.sqrt(2.0 / math.pi)
    t = jnp.tanh(c * (x + 0.044715 * x * x * x))
    return 0.5 * (1.0 + t) + 0.5 * x * (1.0 - t * t) * c * (1.0 + 3.0 * 0.044715 * x * x)


MATMUL_VMEM_BYTES = 36 * 1024 * 1024


def matmul(a, b, *, name, ta=False, tb=False, out_dtype=F32, tm=2048, tn=1024, tk=4096, mnk=None, a_koff=0,
           a_spec=None, b_spec=None, o_spec=None, o_shape=None, prev=None, epi=None, epi_in=(), out_dtypes=None,
           exact_tiles=False, hosted=None):
    if mnk is None:
        M, K = (a.shape[1], a.shape[0]) if ta else a.shape
        K2, N = (b.shape[1], b.shape[0]) if tb else b.shape
        assert K == K2, (a.shape, b.shape, ta, tb)
    else:
        M, N, K = mnk
    out_dtypes = tuple(out_dtypes) if out_dtypes is not None else (out_dtype,)
    n_out, n_epi = len(out_dtypes), len(epi_in)
    tm, tn, tk = _pick(M, tm), _pick(N, tn), _pick(K, tk)

    def vmem_bytes(tm_, tn_, tk_):
        tiles = tm_ * tk_ * a.dtype.itemsize + tk_ * tn_ * b.dtype.itemsize
        tiles += tm_ * tn_ * (sum(jnp.dtype(d).itemsize for d in out_dtypes) + sum(e.dtype.itemsize for e in epi_in))
        return 2 * tiles + tm_ * tn_ * 4 * (tk_ < K)

    def halves(t, dim):
        return [t] + ([t // 2] if t % (2 * LANES) == 0 and t // 2 >= 512 and dim % (t // 2) == 0 else [])

    if exact_tiles:
        halves = lambda t, dim: [t]
    fits = [(m_, n_) for m_ in halves(tm, M) for n_ in halves(tn, N) if vmem_bytes(m_, n_, tk) <= MATMUL_VMEM_BYTES]
    if fits:
        tm, tn = max(fits, key=lambda t: (t[0] * t[1], t[0]))
    else:
        tm, tn = halves(tm, M)[-1], halves(tn, N)[-1]
        while vmem_bytes(tm, tn, tk) > MATMUL_VMEM_BYTES and tk % 2 == 0 and tk > 512:
            tk //= 2
    nk = K // tk
    assert a_koff % tk == 0 and not (ta and a_koff)
    ko = a_koff // tk
    dn = (((0 if ta else 1,), (1 if tb else 0,)), ((), ()))

    def body(*refs):
        a_ref, b_ref = refs[0], refs[1]
        epi_refs = refs[2:2 + n_epi]
        o_refs = refs[len(refs) - n_out - (nk > 1):len(refs) - (nk > 1)]
        k = pl.program_id(2)
        bv = b_ref[...]
        if bv.ndim == 3 and tb:
            cw = bv.shape[-1]
            prod = sum(lax.dot_general(a_ref[:, c * cw:(c + 1) * cw].astype(MXU_DTYPE), bv[c].astype(MXU_DTYPE), dn,
                                       preferred_element_type=F32) for c in range(bv.shape[0]))
        else:
            if bv.ndim == 3:
                bv = bv.reshape(-1, bv.shape[-1])
            prod = lax.dot_general(a_ref[...].astype(MXU_DTYPE), bv.astype(MXU_DTYPE), dn, preferred_element_type=F32)

        def finish(acc):
            res = (acc,) if epi is None else epi(acc, *[r[...] for r in epi_refs])
            for o_ref, r in zip(o_refs, res):
                o_ref[...] = r.astype(o_ref.dtype)

        if nk == 1:
            finish(prod)
            return
        acc_ref = refs[-1]

        @pl.when(k == 0)
        def _():
            acc_ref[...] = prod

        @pl.when(jnp.logical_and(k > 0, k < nk - 1))
        def _():
            acc_ref[...] += prod

        @pl.when(k == nk - 1)
        def _():
            finish(acc_ref[...] + prod)

    if a_spec is None:
        a_spec = pl.BlockSpec((tk, tm), lambda i, j, k: (k, i)) if ta else pl.BlockSpec((tm, tk), lambda i, j, k: (i, k + ko))
    else:
        a_spec = a_spec(tm, tn, tk)
    if b_spec is None:
        bs = pl.BlockSpec((tn, tk), lambda i, j, k: (j, k)) if tb else pl.BlockSpec((tk, tn), lambda i, j, k: (k, j))
    else:
        bs = b_spec(tm, tn, tk)
    tile = pl.BlockSpec((tm, tn), lambda i, j, k: (i, j))
    os_ = tile if o_spec is None else o_spec(tm, tn, tk)
    ins, in_specs, aliases = [a, b, *epi_in], [a_spec, bs] + [tile] * n_epi, {}
    if prev is not None:
        aliases = {len(ins): 0}
        ins.append(prev)
        in_specs.append(pl.BlockSpec(memory_space=pl.ANY))
    shapes = [SDS((M, N) if o_shape is None else o_shape, dt) for dt in out_dtypes]
    scratch = [pltpu.VMEM((tm, tn), F32)] if nk > 1 else []
    if hosted is not None:
        outs, bufs = call_hosting(body, hosted, name=name, grid=(M // tm, N // tn, nk), in_specs=in_specs,
                                  out_specs=[os_] * n_out, out_shape=shapes, inputs=ins, aliases=aliases, scratch=scratch)
        return (outs[0] if n_out == 1 else outs), bufs
    outs = pl.pallas_call(
        body, name=name, grid=(M // tm, N // tn, nk),
        in_specs=in_specs, out_specs=[os_] * n_out, out_shape=shapes, input_output_aliases=aliases,
        scratch_shapes=scratch, compiler_params=_cp("parallel", "parallel", "arbitrary"),
    )(*ins)
    return outs[0] if n_out == 1 else outs


def _rms_hat(x):
    return x * lax.rsqrt(jnp.mean(x * x, axis=-1, keepdims=True) + EPS)


def rms_fwd(x, g, hosted, *, name):
    T, D = x.shape
    tr = _pick(T, 512)

    def body(x_ref, g_ref, o_ref):
        o_ref[...] = (_rms_hat(x_ref[...]) * g_ref[...]).astype(o_ref.dtype)

    (h,), bufs = call_hosting(body, hosted, name=name, grid=(T // tr,), in_specs=[_row(tr, D), _full((1, D))],
                              out_specs=[_row(tr, D)], out_shape=[SDS((T, D), MXU_DTYPE)], inputs=[x, g], aliases={})
    return h, bufs


def res_norm_fwd(x, y, g_post, g_next, *, name):
    T, D = x.shape
    tr = _pick(T, 512)

    def body(x_ref, y_ref, gp_ref, gn_ref, o_ref, h_ref):
        xn = x_ref[...] + _rms_hat(y_ref[...]) * gp_ref[...]
        o_ref[...] = xn
        h_ref[...] = (_rms_hat(xn) * gn_ref[...]).astype(h_ref.dtype)

    return pl.pallas_call(body, name=name, grid=(T // tr,),
                          in_specs=[_row(tr, D), _row(tr, D), _full((1, D)), _full((1, D))],
                          out_specs=[_row(tr, D), _row(tr, D)], out_shape=[SDS((T, D), F32), SDS((T, D), MXU_DTYPE)],
                          compiler_params=_cp("parallel"))(x, y, g_post, g_next)


def res_norm_loss(x, y, g_post, target, *, name):
    T, D = x.shape
    tr = _pick(T, 512)

    def body(x_ref, y_ref, g_ref, t_ref, l_ref, d_ref):
        err = x_ref[...] + _rms_hat(y_ref[...]) * g_ref[...] - t_ref[...]
        d_ref[...] = err * (1.0 / D)

        @pl.when(pl.program_id(0) == 0)
        def _():
            l_ref[...] = jnp.zeros_like(l_ref)

        l_ref[...] += 0.5 * jnp.sum(jnp.mean(err * err, axis=-1, keepdims=True))

    return pl.pallas_call(body, name=name, grid=(T // tr,),
                          in_specs=[_row(tr, D), _row(tr, D), _full((1, D)), _row(tr, D)],
                          out_specs=[_full((SUBLANES, LANES)), _row(tr, D)],
                          out_shape=[SDS((SUBLANES, LANES), F32), SDS((T, D), F32)],
                          compiler_params=_cp("arbitrary"))(x, y, g_post, target)


def _rms_bwd_rows(x, g, dy):
    r = lax.rsqrt(jnp.mean(x * x, axis=-1, keepdims=True) + EPS)
    xh = x * r
    dxh = dy * g
    return r * (dxh - xh * jnp.mean(dxh * xh, axis=-1, keepdims=True)), jnp.sum(dy * xh, axis=0, keepdims=True)


def norm_res_bwd(x, g_pre, dh, res, y, g_post, *, name):
    T, D = x.shape
    tr = _pick(T, 512)

    def body(x_ref, gp_ref, dh_ref, res_ref, y_ref, gy_ref, dx_ref, dy_ref, dgp_ref, dgy_ref):
        dx, dgp = _rms_bwd_rows(x_ref[...], gp_ref[...], dh_ref[...])
        dx = dx + res_ref[...]
        dx_ref[...] = dx
        dy, dgy = _rms_bwd_rows(y_ref[...], gy_ref[...], dx)
        dy_ref[...] = dy.astype(dy_ref.dtype)

        @pl.when(pl.program_id(0) == 0)
        def _():
            dgp_ref[...] = jnp.zeros_like(dgp_ref)
            dgy_ref[...] = jnp.zeros_like(dgy_ref)

        dgp_ref[...] += dgp
        dgy_ref[...] += dgy

    row, vec = _row(tr, D), _full((1, D))
    return pl.pallas_call(body, name=name, grid=(T // tr,), in_specs=[row, vec, row, row, row, vec],
                          out_specs=[row, row, vec, vec],
                          out_shape=[SDS((T, D), F32), SDS((T, D), MXU_DTYPE), SDS((1, D), F32), SDS((1, D), F32)],
                          compiler_params=_cp("arbitrary"))(x, g_pre, dh, res, y, g_post)


def rms_bwd(x, g, dy, res, *, name, hosted=None):
    T, D = x.shape
    tr = _pick(T, 512)
    has_res = res is not None

    def body(*refs):
        if has_res:
            x_ref, g_ref, dy_ref, res_ref, dx_ref, dg_ref = refs
        else:
            x_ref, g_ref, dy_ref, dx_ref, dg_ref = refs
        dx, dg = _rms_bwd_rows(x_ref[...], g_ref[...], dy_ref[...])
        if has_res:
            dx = dx + res_ref[...]
        dx_ref[...] = dx.astype(dx_ref.dtype)

        @pl.when(pl.program_id(0) == 0)
        def _():
            dg_ref[...] = jnp.zeros_like(dg_ref)

        dg_ref[...] += dg

    ins = [x, g, dy] + ([res] if has_res else [])
    in_specs = [_row(tr, D), _full((1, D)), _row(tr, D)] + ([_row(tr, D)] if has_res else [])
    out_shape = [SDS((T, D), F32 if has_res else MXU_DTYPE), SDS((1, D), F32)]
    out_specs = [_row(tr, D), _full((1, D))]
    if hosted is not None:
        return call_hosting(body, hosted, name=name, grid=(T // tr,), in_specs=in_specs, out_specs=out_specs,
                            out_shape=out_shape, inputs=ins, aliases={})
    return pl.pallas_call(body, name=name, grid=(T // tr,), in_specs=in_specs, out_specs=out_specs,
                          out_shape=out_shape, compiler_params=_cp("arbitrary"))(*ins)


def _s5_disc(lr, li, ldt, btr, bti):
    dt = jnp.exp(ldt)
    k = lax.broadcasted_iota(jnp.int32, (SUBLANES, S5_LANES), 0).astype(F32)
    kf = k + 1.0
    kb = 8.0 - k
    ph = li * dt
    lm = lr * dt
    tf_re = jnp.exp(kf * lm) * jnp.cos(kf * ph)
    tf_im = jnp.exp(kf * lm) * jnp.sin(kf * ph)
    tb_re = jnp.exp(kb * lm) * jnp.cos(kb * ph)
    tb_im = -jnp.exp(kb * lm) * jnp.sin(kb * ph)
    mag = jnp.exp(lm)
    ab_re = mag * jnp.cos(ph)
    ab_im = mag * jnp.sin(ph)
    den = lr * lr + li * li
    nr = ab_re - 1.0
    ni = ab_im
    q_re = (nr * lr + ni * li) / den
    q_im = (ni * lr - nr * li) / den
    bbt_re = q_re * btr - q_im * bti
    bbt_im = q_re * bti + q_im * btr
    return tf_re, tf_im, tb_re, tb_im, bbt_re, bbt_im


def _s5_disc_core(lr, li, ldt, btr, bti):
    dt = jnp.exp(ldt)
    mag = jnp.exp(lr * dt)
    ab_re = mag * jnp.cos(li * dt)
    ab_im = mag * jnp.sin(li * dt)
    den = lr * lr + li * li
    nr = ab_re - 1.0
    ni = ab_im
    q_re = (nr * lr + ni * li) / den
    q_im = (ni * lr - nr * li) / den
    return ab_re, ab_im, q_re * btr - q_im * bti, q_re * bti + q_im * btr


def s5_disc_fwd(lr, li, ldt, btr, bti, *, name):
    def body(lr_ref, li_ref, ldt_ref, btr_ref, bti_ref, *outs):
        vals = _s5_disc(lr_ref[...], li_ref[...], ldt_ref[...], btr_ref[...], bti_ref[...])
        for o, v in zip(outs, vals):
            o[...] = v

    tab = SDS((SUBLANES, S5_LANES), F32)
    bb = SDS((S5_GROUP, S5_LANES), F32)
    return pl.pallas_call(body, name=name, out_shape=[tab, tab, tab, tab, bb, bb])(lr, li, ldt, btr, bti)


def s5_disc_bwd(lr, li, ldt, btr, bti, dab_re, dab_im, dbbt_re, dbbt_im, *, name):
    def body(lr_ref, li_ref, ldt_ref, btr_ref, bti_ref, dar_ref, dai_ref, dbr_ref, dbi_ref,
             dlr_ref, dli_ref, dldt_ref, dbtr_ref, dbti_ref):
        _, vjp = jax.vjp(_s5_disc_core, lr_ref[...], li_ref[...], ldt_ref[...], btr_ref[...], bti_ref[...])
        dlr, dli, dldt, dbtr, dbti = vjp((dar_ref[...], dai_ref[...], dbr_ref[...], dbi_ref[...]))
        dlr_ref[...] = dlr
        dli_ref[...] = dli
        dbtr_ref[...] = dbtr
        dbti_ref[...] = dbti
        lane_group = lax.broadcasted_iota(jnp.int32, (S5_LANES, LANES), 0) // S5_STATE
        col = lax.broadcasted_iota(jnp.int32, (S5_LANES, LANES), 1)
        ind = (lane_group == col).astype(F32)
        dldt_ref[...] = jnp.dot(jnp.broadcast_to(dldt, (SUBLANES, S5_LANES)), ind,
                                precision=lax.Precision.HIGHEST, preferred_element_type=F32)

    row = SDS((1, S5_LANES), F32)
    bb = SDS((S5_GROUP, S5_LANES), F32)
    return pl.pallas_call(body, name=name, out_shape=[row, row, SDS((SUBLANES, LANES), F32), bb, bb])(
        lr, li, ldt, btr, bti, dab_re, dab_im, dbbt_re, dbbt_im)


S5_NB = 1024


S5_CB = S5_WIDTH * S5_NB // S5_LANES


def _chan_rows_t(tm, tn, tk):
    return pl.BlockSpec((tk, S5_CB), lambda i, j, k: (k, j // 2))


def _chan_cols_of_i(tm, tn, tk):
    return pl.BlockSpec((tk, S5_CB), lambda i, j, k: (k, i // 2))


def _lanes_of_chan(tm, tn, tk):
    return pl.BlockSpec((tm, 2 * S5_NB), lambda i, j, k: (i, j))


def _s5_b_block_t(tm, tn, tk):
    return pl.BlockSpec((S5_CB, 2 * S5_NB), lambda i, j, k: (j, j))


def _s5_c_block(tm, tn, tk):
    return pl.BlockSpec((2 * S5_NB, S5_CB), lambda i, j, k: (j, j))


def s5_interleave(re, im, axis):
    parts = []
    for n in range(S5_LANES // S5_NB):
        sl = [slice(None)] * re.ndim
        sl[axis] = slice(n * S5_NB, (n + 1) * S5_NB)
        parts += [re[tuple(sl)], im[tuple(sl)]]
    return jnp.concatenate(parts, axis=axis)


def s5_deinterleave(a, axis):
    re, im = [], []
    for n in range(S5_LANES // S5_NB):
        sl = [slice(None)] * a.ndim
        sl[axis] = slice(2 * n * S5_NB, (2 * n + 1) * S5_NB)
        re.append(a[tuple(sl)])
        sl[axis] = slice((2 * n + 1) * S5_NB, (2 * n + 2) * S5_NB)
        im.append(a[tuple(sl)])
    return jnp.concatenate(re, axis=axis), jnp.concatenate(im, axis=axis)


def s5_scan(src, mat, tab_re, tab_im, *, reverse, name, states=None, hosted=None):
    T = src.shape[0]
    nb = S5_NB
    tc = _pick(T, 256)
    nl = S5_LANES // nb
    nt = T // tc
    ntile = tc // SUBLANES
    with_da = states is not None
    assert reverse or not with_da
    step_rows = ((1, 7), (2, 6), (4, 4)) if reverse else ((1, 0), (2, 1), (4, 3))
    drive_dn = _NT if reverse else (((1,), (0,)), ((), ()))

    def body(*refs):
        if with_da:
            (src_ref, wr_ref, wi_ref, tr_ref, ti_ref, sr_ref, si_ref, hr_ref, hi_ref, xo_ref, dar_ref, dai_ref,
             cr_ref, ci_ref, mr_ref, mi_ref, br_ref, bi_ref, ar_ref, ai_ref) = refs
        else:
            src_ref, wr_ref, wi_ref, tr_ref, ti_ref, xo_ref, cr_ref, ci_ref, mr_ref, mi_ref, br_ref, bi_ref = refs

        @pl.when(pl.program_id(1) == 0)
        def _():
            cr_ref[...] = jnp.zeros_like(cr_ref)
            ci_ref[...] = jnp.zeros_like(ci_ref)
            if with_da:
                ar_ref[...] = jnp.zeros_like(ar_ref)
                ai_ref[...] = jnp.zeros_like(ai_ref)

        lhs = src_ref[...].astype(MXU_DTYPE)
        br_ref[...] = lax.dot_general(lhs, wr_ref[...].astype(MXU_DTYPE), drive_dn, preferred_element_type=F32)
        bi_ref[...] = lax.dot_general(lhs, wi_ref[...].astype(MXU_DTYPE), drive_dn, preferred_element_type=F32)

        seen = jnp.where(pl.program_id(1) < nt - 1, 1.0, 0.0)

        def add_da(lr, li, r0, last_r, last_i):
            first = lax.broadcasted_iota(jnp.int32, (SUBLANES, nb), 0) == 0
            pr = jnp.where(first, last_r, pltpu.roll(sr_ref[pl.ds(r0, SUBLANES), :], 1, 0))
            pi = jnp.where(first, last_i, pltpu.roll(si_ref[pl.ds(r0, SUBLANES), :], 1, 0))
            ar_ref[...] += lr * pr + li * pi
            ai_ref[...] += li * pr - lr * pi

        io = lax.broadcasted_iota(jnp.int32, (SUBLANES, nb), 0)
        for s_, (d, r) in enumerate(step_rows):
            keep = (io < SUBLANES - d) if reverse else (io >= d)
            mr_ref[s_] = jnp.where(keep, tr_ref[r:r + 1, :], 0.0)
            mi_ref[s_] = jnp.where(keep, ti_ref[r:r + 1, :], 0.0)

        def tile(i, carry):
            cr, ci = carry
            j = (ntile - 1 - i) if reverse else i
            r0 = pl.multiple_of(j * SUBLANES, SUBLANES)
            xr = br_ref[pl.ds(r0, SUBLANES), :]
            xi = bi_ref[pl.ds(r0, SUBLANES), :]
            for s_, (d, _) in enumerate(step_rows):
                sh = (SUBLANES - d) if reverse else d
                sr = pltpu.roll(xr, sh, 0)
                si = pltpu.roll(xi, sh, 0)
                pr, pi = mr_ref[s_], mi_ref[s_]
                xr, xi = xr + pr * sr - pi * si, xi + pr * si + pi * sr
            tr, ti = tr_ref[...], ti_ref[...]
            xr, xi = xr + tr * cr - ti * ci, xi + tr * ci + ti * cr
            xo_ref[pl.ds(r0, SUBLANES), 0:nb] = xr
            xo_ref[pl.ds(r0, SUBLANES), nb:2 * nb] = xi
            if with_da:
                @pl.when(j > 0)
                def _():
                    p0 = pl.multiple_of(r0 - SUBLANES, SUBLANES)
                    add_da(xr, xi, r0, sr_ref[pl.ds(p0, SUBLANES), :][SUBLANES - 1:SUBLANES, :],
                           si_ref[pl.ds(p0, SUBLANES), :][SUBLANES - 1:SUBLANES, :])

                @pl.when(j == 0)
                def _():
                    add_da(xr, xi, r0, hr_ref[SUBLANES - 1:SUBLANES, :] * seen, hi_ref[SUBLANES - 1:SUBLANES, :] * seen)
            if reverse:
                return xr[0:1, :], xi[0:1, :]
            return xr[SUBLANES - 1:SUBLANES, :], xi[SUBLANES - 1:SUBLANES, :]

        cr, ci = lax.fori_loop(0, ntile, tile, (cr_ref[0:1, :], ci_ref[0:1, :]))
        cr_ref[0:1, :] = cr
        ci_ref[0:1, :] = ci
        if with_da:
            @pl.when(pl.program_id(1) == nt - 1)
            def _():
                dar_ref[...] = jnp.sum(ar_ref[...], axis=0, keepdims=True)
                dai_ref[...] = jnp.sum(ai_ref[...], axis=0, keepdims=True)

    def tmap(t):
        return (nt - 1 - t) if reverse else t

    hb = tc // SUBLANES
    re_spec = pl.BlockSpec((tc, nb), lambda n, t: (tmap(t), 2 * n))
    im_spec = pl.BlockSpec((tc, nb), lambda n, t: (tmap(t), 2 * n + 1))
    tab_spec = pl.BlockSpec((SUBLANES, nb), lambda n, t: (0, n))
    out_spec = pl.BlockSpec((tc, 2 * nb), lambda n, t: (tmap(t), n))
    out_shape = SDS((T, 2 * S5_LANES), F32)
    scratch = [pltpu.VMEM((SUBLANES, nb), F32), pltpu.VMEM((SUBLANES, nb), F32),
               pltpu.VMEM((len(step_rows), SUBLANES, nb), F32), pltpu.VMEM((len(step_rows), SUBLANES, nb), F32),
               pltpu.VMEM((tc, nb), F32), pltpu.VMEM((tc, nb), F32)]
    src_spec = pl.BlockSpec((tc, S5_CB), lambda n, t: (tmap(t), n))
    if reverse:
        wr_spec = pl.BlockSpec((nb, S5_CB), lambda n, t: (2 * n, n))
        wi_spec = pl.BlockSpec((nb, S5_CB), lambda n, t: (2 * n + 1, n))
    else:
        wr_spec = pl.BlockSpec((S5_CB, nb), lambda n, t: (n, 2 * n))
        wi_spec = pl.BlockSpec((S5_CB, nb), lambda n, t: (n, 2 * n + 1))
    drive_specs = [src_spec, wr_spec, wi_spec, tab_spec, tab_spec]
    drive = [src, mat, mat, tab_re, tab_im]
    if not with_da:
        return pl.pallas_call(body, name=name, grid=(nl, nt), in_specs=drive_specs,
                              out_specs=out_spec, out_shape=out_shape, scratch_shapes=scratch,
                              compiler_params=_cp("parallel", "arbitrary"))(*drive)
    re_halo = pl.BlockSpec((SUBLANES, nb), lambda n, t: (jnp.maximum(tmap(t) * hb - 1, 0), 2 * n))
    im_halo = pl.BlockSpec((SUBLANES, nb), lambda n, t: (jnp.maximum(tmap(t) * hb - 1, 0), 2 * n + 1))
    acc = pl.BlockSpec((1, nb), lambda n, t: (0, n))
    row = SDS((1, S5_LANES), F32)
    return call_hosting(
        body, hosted, name=name, grid=(nl, nt),
        in_specs=drive_specs + [re_spec, im_spec, re_halo, im_halo],
        out_specs=[out_spec, acc, acc], out_shape=[out_shape, row, row],
        inputs=drive + [states, states, states, states], aliases={},
        scratch=scratch + [pltpu.VMEM((SUBLANES, nb), F32), pltpu.VMEM((SUBLANES, nb), F32)])


def s5_out_fwd(yc, u, d, *, name):
    T, C = yc.shape
    tr = _pick(T, 512)

    def body(yc_ref, u_ref, d_ref, yl_ref, yg_ref):
        yl = yc_ref[...] + d_ref[...] * u_ref[...]
        yl_ref[...] = yl
        yg_ref[...] = jax.nn.gelu(yl)

    return pl.pallas_call(body, name=name, grid=(T // tr,), in_specs=[_row(tr, C), _row(tr, C), _full((1, C))],
                          out_specs=[_row(tr, C)] * 2, out_shape=[SDS((T, C), F32)] * 2,
                          compiler_params=_cp("parallel"))(yc, u, d)


def glu_fwd(yg, gl, *, out_cols, name):
    T, C = yg.shape
    tr = _pick(T, 512)

    def body(yg_ref, gl_ref, o_ref):
        o_ref[...] = yg_ref[...] * jax.nn.sigmoid(gl_ref[...])

    return pl.pallas_call(body, name=name, grid=(T // tr,), in_specs=[_row(tr, C)] * 2, out_specs=_row(tr, C),
                          out_shape=SDS((T, out_cols), F32), compiler_params=_cp("parallel"))(yg, gl)


def glu_bwd(yg, gl, dy, *, name):
    T, C = yg.shape
    tr = _pick(T, 512)

    def body(yg_ref, gl_ref, dy_ref, dyg_ref, dgl_ref):
        s = jax.nn.sigmoid(gl_ref[...])
        dyv = dy_ref[...]
        dyg_ref[...] = dyv * s
        dgl_ref[...] = (dyv * yg_ref[...] * s * (1.0 - s)).astype(dgl_ref.dtype)

    return pl.pallas_call(body, name=name, grid=(T // tr,), in_specs=[_row(tr, C)] * 3, out_specs=[_row(tr, C)] * 2,
                          out_shape=[SDS((T, C), F32), SDS((T, C), MXU_DTYPE)],
                          compiler_params=_cp("parallel"))(yg, gl, dy)


def s5_out_bwd(yl, u, d, dyg_a, dyg_b, *, name):
    T, C = yl.shape
    tr = _pick(T, 512)

    def body(yl_ref, u_ref, d_ref, da_ref, db_ref, dyl_ref, du_ref, dd_ref):
        dyl = (da_ref[...] + db_ref[...]) * _gelu_grad(yl_ref[...])
        dyl_ref[...] = dyl.astype(dyl_ref.dtype)
        du_ref[...] = dyl * d_ref[...]

        @pl.when(pl.program_id(0) == 0)
        def _():
            dd_ref[...] = jnp.zeros_like(dd_ref)

        dd_ref[...] += jnp.sum(dyl * u_ref[...], axis=0, keepdims=True)

    return pl.pallas_call(body, name=name, grid=(T // tr,),
                          in_specs=[_row(tr, C), _row(tr, C), _full((1, C)), _row(tr, C), _row(tr, C)],
                          out_specs=[_row(tr, C), _row(tr, C), _full((1, C))],
                          out_shape=[SDS((T, C), MXU_DTYPE), SDS((T, C), F32), SDS((1, C), F32)],
                          compiler_params=_cp("arbitrary"))(yl, u, d, dyg_a, dyg_b)


def add2(a, b, *, name):
    T, C = a.shape
    tr = _pick(T, 512)

    def body(a_ref, b_ref, o_ref):
        o_ref[...] = a_ref[...] + b_ref[...]

    return pl.pallas_call(body, name=name, grid=(T // tr,), in_specs=[_row(tr, C)] * 2, out_specs=_row(tr, C),
                          out_shape=SDS((T, C), F32), compiler_params=_cp("parallel"))(a, b)


def _tri(n, upper):
    r = lax.broadcasted_iota(jnp.int32, (n, n), 0)
    c = lax.broadcasted_iota(jnp.int32, (n, n), 1)
    return ((c >= r) if upper else (c <= r)).astype(F32)


def fox_gate_fwd(fl, bf, *, fl_col, name):
    T = fl.shape[0]
    tb = _pick(T, 256)

    def body(fl_ref, bf_ref, f_ref, c_ref):
        @pl.when(pl.program_id(0) == 0)
        def _():
            c_ref[...] = jnp.zeros_like(c_ref)

        lf = jax.nn.log_sigmoid(fl_ref[...] + bf_ref[...])
        f = jnp.dot(_tri(tb, False), lf, precision=lax.Precision.HIGHEST, preferred_element_type=F32) + c_ref[0:1, :]
        f_ref[...] = f * LOG2E
        c_ref[0:1, :] = f[tb - 1:tb, :]

    fl_spec = pl.BlockSpec((tb, LANES), lambda i: (i, fl_col))
    return pl.pallas_call(body, name=name, grid=(T // tb,), in_specs=[fl_spec, _full((1, LANES))],
                          out_specs=_row(tb, LANES), out_shape=SDS((T, LANES), F32),
                          scratch_shapes=[pltpu.VMEM((SUBLANES, LANES), F32)], compiler_params=_cp("arbitrary"))(fl, bf)


def fox_gate_bwd(fl, bf, df_keys, df_queries, *, fl_col, name):
    T = fl.shape[0]
    tb = _pick(T, 256)
    nt = T // tb

    def body(fl_ref, bf_ref, dfk_ref, dfq_ref, dfl_ref, dbf_ref, c_ref):
        @pl.when(pl.program_id(0) == 0)
        def _():
            c_ref[...] = jnp.zeros_like(c_ref)
            dbf_ref[...] = jnp.zeros_like(dbf_ref)

        dlf = jnp.dot(_tri(tb, True), dfk_ref[...] + dfq_ref[...], precision=lax.Precision.HIGHEST,
                      preferred_element_type=F32) + c_ref[0:1, :]
        c_ref[0:1, :] = dlf[0:1, :]
        dfl = dlf * jax.nn.sigmoid(-(fl_ref[...] + bf_ref[...]))
        dfl_ref[...] = dfl
        dbf_ref[...] += jnp.sum(dfl, axis=0, keepdims=True)

    rev = pl.BlockSpec((tb, LANES), lambda i: (nt - 1 - i, 0))
    fl_rev = pl.BlockSpec((tb, LANES), lambda i: (nt - 1 - i, fl_col))
    return pl.pallas_call(body, name=name, grid=(nt,), in_specs=[fl_rev, _full((1, LANES)), rev, rev],
                          out_specs=[rev, _full((1, LANES))], out_shape=[SDS((T, LANES), F32), SDS((1, LANES), F32)],
                          scratch_shapes=[pltpu.VMEM((SUBLANES, LANES), F32)],
                          compiler_params=_cp("arbitrary"))(fl, bf, df_keys, df_queries)


FOX_BLOCK = 512
FOX_PAIRS = FOX_HEADS // 2
_NT = (((1,), (1,)), ((), ()))


LOG2E = 1.4426950408889634
FOX_FWD_UNROLL = 4
FOX_BWD_UNROLL = 2


def _fox_block(T):
    return _pick(T, FOX_BLOCK)


def _own_lanes(lane, hh):
    return (lane < FOX_HEAD_DIM) if hh == 0 else (lane >= FOX_HEAD_DIM)


def _grouped_steps(step, lo, n, unroll, init):
    def trip(t, c):
        for u in range(unroll):
            c = step(lo + t * unroll + u, c)
        return c

    carry = lax.fori_loop(0, n // unroll, trip, init)
    for u in range(unroll - 1):
        carry = lax.cond(n % unroll > u, lambda c: step(lo + (n // unroll) * unroll + u, c), lambda c: c, carry)
    return carry


Q_TILE0, K_TILE0, V_TILE0, O_TILE0 = 4, 8, 12, 4
FL_TILE = 16
POOL_COL = 2


def fox_fwd(z, f_col, f_row, ycat, hosted, *, name):
    T = z.shape[0]
    blk = _fox_block(T)
    nb = T // blk
    scale = FOX_HEAD_DIM ** -0.5

    def body(q_ref, k_ref, v_ref, fc_ref, fr_ref, prev_ref, o_ref, l_ref):
        i = pl.program_id(1)
        row = lax.broadcasted_iota(jnp.int32, (blk, blk), 0)
        col = lax.broadcasted_iota(jnp.int32, (blk, blk), 1)
        lane = lax.broadcasted_iota(jnp.int32, (blk, LANES), 1)
        qt = q_ref[...] * (scale * LOG2E)
        outs = []
        for hh in range(2):
            qh = jnp.where(_own_lanes(lane, hh), qt, 0.0).astype(MXU_DTYPE)
            fi = fc_ref[0, :, hh:hh + 1]

            def step(j, carry, masked=False):
                m, l, acc = carry
                r0 = pl.multiple_of(j * blk, blk)
                kj = k_ref[pl.ds(r0, blk), :].astype(MXU_DTYPE)
                vj = v_ref[pl.ds(r0, blk), :].astype(MXU_DTYPE)
                s = lax.dot_general(qh, kj, _NT, preferred_element_type=F32) + (fi - fr_ref[0, j, hh:hh + 1, :])
                if masked:
                    s = jnp.where(col <= row, s, -jnp.inf)
                m_new = jnp.maximum(m, jnp.max(s, axis=-1, keepdims=True))
                p = jnp.exp2(s - m_new)
                alpha = jnp.exp2(m - m_new)
                l = alpha * l + jnp.sum(p, axis=-1, keepdims=True)
                acc = alpha * acc + jnp.dot(p.astype(MXU_DTYPE), vj, preferred_element_type=F32)
                return m_new, l, acc

            init = (jnp.full((blk, 1), -jnp.inf, F32), jnp.zeros((blk, 1), F32), jnp.zeros((blk, LANES), F32))
            m, l, acc = step(i, _grouped_steps(step, 0, i, FOX_FWD_UNROLL, init), True)
            outs.append(acc / l)
            l_ref[0, :, hh:hh + 1] = m + jnp.log2(l)
        o_ref[...] = jnp.where(_own_lanes(lane, 0), outs[0], outs[1])

    qspec = pl.BlockSpec((blk, LANES), lambda h, i: (i, Q_TILE0 + h))
    kspec = pl.BlockSpec((T, LANES), lambda h, i: (0, K_TILE0 + h))
    vspec = pl.BlockSpec((T, LANES), lambda h, i: (0, V_TILE0 + h))
    ospec = pl.BlockSpec((blk, LANES), lambda h, i: (i, O_TILE0 + h))
    cspec = pl.BlockSpec((1, blk, 2), lambda h, i: (h, i, 0))
    rspec = pl.BlockSpec((1, nb, 2, blk), lambda h, i: (h, 0, 0, 0))
    return call_hosting(body, hosted, name=name, grid=(FOX_PAIRS, nb),
                        in_specs=[qspec, kspec, vspec, cspec, rspec, ANY], out_specs=[ospec, cspec],
                        out_shape=[SDS(ycat.shape, F32), SDS((FOX_PAIRS, T, 2), F32)],
                        inputs=[z, z, z, f_col, f_row, ycat], aliases={5: 0})


def fox_dd(ycat, dycat, *, name):
    T = ycat.shape[0]
    blk = _fox_block(T)

    def body(o_ref, do_ref, dd_ref):
        lane = lax.broadcasted_iota(jnp.int32, (blk, LANES), 1)
        prod = do_ref[...] * o_ref[...]
        for hh in range(2):
            dd_ref[0, :, hh:hh + 1] = jnp.sum(jnp.where(_own_lanes(lane, hh), prod, 0.0), axis=-1, keepdims=True)

    ospec = pl.BlockSpec((blk, LANES), lambda h, i: (i, O_TILE0 + h))
    return pl.pallas_call(body, name=name, grid=(FOX_PAIRS, T // blk), in_specs=[ospec, ospec],
                          out_specs=pl.BlockSpec((1, blk, 2), lambda h, i: (h, i, 0)),
                          out_shape=SDS((FOX_PAIRS, T, 2), F32), compiler_params=_cp("parallel", "parallel"))(ycat, dycat)


def fox_bwd(z, dycat, f_col, f_row, lse_row, dd_row, hosted, *, name):
    T = z.shape[0]
    blk = _fox_block(T)
    nb = T // blk
    scale = FOX_HEAD_DIM ** -0.5

    def body(q_ref, k_ref, v_ref, do_ref, fc_ref, fr_ref, lr_ref, dr_ref, dk_ref, dv_ref, df_ref, dqt_ref, dfq_ref):
        j = pl.program_id(1)

        @pl.when(j == 0)
        def _():
            dqt_ref[...] = jnp.zeros_like(dqt_ref)
            dfq_ref[...] = jnp.zeros_like(dfq_ref)

        row = lax.broadcasted_iota(jnp.int32, (blk, blk), 0)
        col = lax.broadcasted_iota(jnp.int32, (blk, blk), 1)
        lane = lax.broadcasted_iota(jnp.int32, (blk, LANES), 1)
        kt = k_ref[...]
        vt = v_ref[...]
        dks, dvs = [], []
        for hh in range(2):
            own = _own_lanes(lane, hh)
            kh = jnp.where(own, kt, 0.0).astype(MXU_DTYPE)
            vh = jnp.where(own, vt, 0.0).astype(MXU_DTYPE)
            kht = kh.T
            fj = fc_ref[0, :, hh:hh + 1]

            def step(i, carry, masked=False):
                dk, dv, df = carry
                r0 = pl.multiple_of(i * blk, blk)
                qi = (q_ref[pl.ds(r0, blk), :] * (scale * LOG2E)).astype(MXU_DTYPE)
                doi = do_ref[pl.ds(r0, blk), :].astype(MXU_DTYPE)
                st = lax.dot_general(kh, qi, _NT, preferred_element_type=F32) + (fr_ref[0, i, hh:hh + 1, :] - fj)
                pt = jnp.exp2(st - lr_ref[0, i, hh:hh + 1, :])
                if masked:
                    pt = jnp.where(col >= row, pt, 0.0)
                dv = dv + jnp.dot(pt.astype(MXU_DTYPE), doi, preferred_element_type=F32)
                dpt = lax.dot_general(vh, doi, _NT, preferred_element_type=F32)
                dst = pt * (dpt - dr_ref[0, i, hh:hh + 1, :])
                dsb = dst.astype(MXU_DTYPE)
                dk = dk + jnp.dot(dsb, qi, preferred_element_type=F32)
                df = df - jnp.sum(dst, axis=-1, keepdims=True)
                dqt_ref[0, i] += jnp.dot(kht, dsb, preferred_element_type=F32)
                dfq_ref[0, i, hh:hh + 1, :] += jnp.sum(dst, axis=0, keepdims=True)
                return dk, dv, df

            init = (jnp.zeros((blk, LANES), F32), jnp.zeros((blk, LANES), F32), jnp.zeros((blk, 1), F32))
            dk, dv, df = _grouped_steps(step, j + 1, nb - 1 - j, FOX_BWD_UNROLL, step(j, init, True))
            dks.append(dk * (1.0 / LOG2E))
            dvs.append(dv)
            df_ref[0, :, hh:hh + 1] = df
        dk_ref[...] = jnp.where(_own_lanes(lane, 0), dks[0], dks[1])
        dv_ref[...] = jnp.where(_own_lanes(lane, 0), dvs[0], dvs[1])

    bspec = pl.BlockSpec((blk, LANES), lambda h, j: (j, h))
    qspec = pl.BlockSpec((T, LANES), lambda h, j: (0, Q_TILE0 + h))
    kspec = pl.BlockSpec((blk, LANES), lambda h, j: (j, K_TILE0 + h))
    vspec = pl.BlockSpec((blk, LANES), lambda h, j: (j, V_TILE0 + h))
    dospec = pl.BlockSpec((T, LANES), lambda h, j: (0, O_TILE0 + h))
    cspec = pl.BlockSpec((1, blk, 2), lambda h, j: (h, j, 0))
    rspec = pl.BlockSpec((1, nb, 2, blk), lambda h, j: (h, 0, 0, 0))
    dqspec = pl.BlockSpec((1, nb, LANES, blk), lambda h, j: (h, 0, 0, 0))
    return call_hosting(body, hosted, name=name, grid=(FOX_PAIRS, nb),
                        in_specs=[qspec, kspec, vspec, dospec, cspec, rspec, rspec, rspec],
                        out_specs=[bspec, bspec, cspec, dqspec, rspec],
                        out_shape=[SDS((T, FOX_WIDTH), F32), SDS((T, FOX_WIDTH), F32), SDS((FOX_PAIRS, T, 2), F32),
                                   SDS((FOX_PAIRS, nb, LANES, blk), F32), SDS((FOX_PAIRS, nb, 2, blk), F32)],
                        inputs=[z, z, z, dycat, f_col, f_row, lse_row, dd_row], aliases={})


def _pairs_col(a, T):
    return jnp.transpose(a[:, :FOX_HEADS].reshape(T, FOX_PAIRS, 2), (1, 0, 2))


def _col_to_row(a, T):
    blk = _fox_block(T)
    return jnp.transpose(a.reshape(FOX_PAIRS, T // blk, blk, 2), (0, 1, 3, 2))


def _row_to_col(a, T):
    return jnp.transpose(a, (0, 1, 3, 2)).reshape(FOX_PAIRS, T, 2)


def _pairs_to_lanes(a, T):
    flat = jnp.transpose(a, (1, 0, 2)).reshape(T, FOX_HEADS)
    return jnp.pad(flat, ((0, 0), (0, LANES - FOX_HEADS)))


def _pool_counts(t0, n, w):
    t = (t0 + lax.broadcasted_iota(jnp.int32, (n, 1), 0)).astype(F32)
    return jnp.minimum(t + 1.0, float(w))


def pool_window(x, *, adjoint, name, in_col=0, into=None, out_col=0, out_dtype=F32):
    T, C = x.shape[0], len(POOL_WINDOWS) * POOL_GROUP_DIM
    tr = _pick(T, 512)
    nt = T // tr
    hb = tr // POOL_HALO
    n = tr + POOL_HALO

    def body(x_ref, h_ref, *rest):
        o_ref = rest[-1]
        i = pl.program_id(0)
        cur = x_ref[...]
        if adjoint:
            halo = h_ref[...] * jnp.where(i < nt - 1, 1.0, 0.0)
            ext = jnp.concatenate([cur, halo], axis=0)
            t0 = i * tr
        else:
            halo = h_ref[...] * jnp.where(i > 0, 1.0, 0.0)
            ext = jnp.concatenate([halo, cur], axis=0)
            t0 = i * tr - POOL_HALO
        sums = {}
        for g, w in enumerate(POOL_WINDOWS):
            ls = slice(g * POOL_GROUP_DIM, (g + 1) * POOL_GROUP_DIM)
            s = ext[:, ls]
            if adjoint:
                s = s / _pool_counts(t0, n, w)
            d = 1
            while d < w:
                s = s + pltpu.roll(s, (n - d) if adjoint else d, 0)
                d *= 2
            if adjoint:
                o_ref[:, ls] = (s[0:tr, :] - cur[:, ls]).astype(o_ref.dtype)
            else:
                o_ref[:, ls] = (s[POOL_HALO:n, :] / _pool_counts(i * tr, tr, w) - cur[:, ls]).astype(o_ref.dtype)

    if adjoint:
        halo_spec = pl.BlockSpec((POOL_HALO, C), lambda i: (jnp.minimum((i + 1) * hb, T // POOL_HALO - 1), in_col))
    else:
        halo_spec = pl.BlockSpec((POOL_HALO, C), lambda i: (jnp.maximum(i * hb - 1, 0), in_col))
    x_spec = pl.BlockSpec((tr, C), lambda i: (i, in_col))
    if into is None:
        return pl.pallas_call(body, name=name, grid=(nt,), in_specs=[x_spec, halo_spec], out_specs=_row(tr, C),
                              out_shape=SDS((T, C), out_dtype), compiler_params=_cp("parallel"))(x, x)
    return pl.pallas_call(body, name=name, grid=(nt,), in_specs=[x_spec, halo_spec, ANY],
                          out_specs=pl.BlockSpec((tr, C), lambda i: (i, out_col)), out_shape=SDS(into.shape, into.dtype),
                          input_output_aliases={2: 0}, compiler_params=_cp("parallel"))(x, x, into)


def colscale_fwd(a, s, *, out_cols, name):
    T, C = a.shape
    tr = _pick(T, 512)

    def body(a_ref, s_ref, o_ref):
        o_ref[...] = (a_ref[...] * s_ref[...]).astype(o_ref.dtype)

    return pl.pallas_call(body, name=name, grid=(T // tr,), in_specs=[_row(tr, C), _full((1, C))], out_specs=_row(tr, C),
                          out_shape=SDS((T, out_cols), MXU_DTYPE), compiler_params=_cp("parallel"))(a, s)


def colscale_bwd(a, s, dy, *, name):
    T, C = a.shape
    tr = _pick(T, 512)

    def body(a_ref, s_ref, dy_ref, da_ref, ds_ref):
        dyv = dy_ref[...]
        da_ref[...] = (dyv * s_ref[...]).astype(da_ref.dtype)

        @pl.when(pl.program_id(0) == 0)
        def _():
            ds_ref[...] = jnp.zeros_like(ds_ref)

        ds_ref[...] += jnp.sum(dyv * a_ref[...], axis=0, keepdims=True)

    return pl.pallas_call(body, name=name, grid=(T // tr,), in_specs=[_row(tr, C), _full((1, C)), _row(tr, C)],
                          out_specs=[_row(tr, C), _full((1, C))], out_shape=[SDS((T, C), MXU_DTYPE), SDS((1, C), F32)],
                          compiler_params=_cp("arbitrary"))(a, s, dy)


SGU_ROWS = 512


def _sgu_norm(v, ln_g, ln_b):
    vg = jax.nn.gelu(v)
    xc = vg - jnp.mean(vg, axis=-1, keepdims=True)
    r = lax.rsqrt(jnp.mean(xc * xc, axis=-1, keepdims=True) + EPS)
    xh = xc * r
    return xh * ln_g + ln_b, xh, r


def _rowc(tr, c, cb):
    return pl.BlockSpec((tr, c), lambda i: (i, cb))


def sgu_fwd(z, ln_g, ln_b, ws, bst, ycat, *, name):
    T, C = z.shape[0], SGU_GROUPS * SGU_GROUP_DIM
    tr = _pick(T, SGU_ROWS)

    def body(u_ref, v_ref, g_ref, b_ref, ws_ref, bst_ref, prev_ref, o_ref):
        vn, _, _ = _sgu_norm(v_ref[...], g_ref[...], b_ref[...])
        vn = vn.astype(MXU_DTYPE)
        ug = jax.nn.gelu(u_ref[...])
        for g in range(SGU_GROUPS):
            w = ws_ref[g].astype(MXU_DTYPE)
            bias = bst_ref[:, g:g + 1]
            for c in range(tr // CHUNK):
                rs = slice(c * CHUNK, (c + 1) * CHUNK)
                ls = slice(g * SGU_GROUP_DIM, (g + 1) * SGU_GROUP_DIM)
                mixed = jnp.dot(w, vn[rs, ls], preferred_element_type=F32) + bias
                o_ref[rs, ls] = (ug[rs, ls] * mixed).astype(o_ref.dtype)

    return pl.pallas_call(body, name=name, grid=(T // tr,),
                          in_specs=[_rowc(tr, C, 0), _rowc(tr, C, 1), _full((1, C)), _full((1, C)),
                                    _full((SGU_GROUPS, CHUNK, CHUNK)), _full((CHUNK, SGU_GROUPS)), ANY],
                          out_specs=_rowc(tr, C, 1), out_shape=SDS(ycat.shape, ycat.dtype), input_output_aliases={6: 0},
                          compiler_params=_cp("parallel"))(z, z, ln_g, ln_b, ws, bst, ycat)


def sgu_bwd(z, ln_g, ln_b, ws, wst, bst, dycat, *, out_cols, name):
    T, C = z.shape[0], SGU_GROUPS * SGU_GROUP_DIM
    tr = _pick(T, SGU_ROWS)

    def body(u_ref, v_ref, g_ref, b_ref, ws_ref, wst_ref, bst_ref, dy_ref,
             duv_ref, dg_ref, db_ref, dws_ref, dbst_ref, dvn_ref):
        du_ref = duv_ref.at[:, 0:C]
        dv_ref = duv_ref.at[:, C:2 * C]
        @pl.when(pl.program_id(0) == 0)
        def _():
            dg_ref[...] = jnp.zeros_like(dg_ref)
            db_ref[...] = jnp.zeros_like(db_ref)
            dws_ref[...] = jnp.zeros_like(dws_ref)
            dbst_ref[...] = jnp.zeros_like(dbst_ref)

        uv = u_ref[...]
        vv = v_ref[...]
        vn, xh, r = _sgu_norm(vv, g_ref[...], b_ref[...])
        vn = vn.astype(MXU_DTYPE)
        ug = jax.nn.gelu(uv)
        dyv = dy_ref[...]
        for g in range(SGU_GROUPS):
            w = ws_ref[g].astype(MXU_DTYPE)
            wt = wst_ref[g].astype(MXU_DTYPE)
            bias = bst_ref[:, g:g + 1]
            dw = jnp.zeros((CHUNK, CHUNK), F32)
            dbias = jnp.zeros((CHUNK, 1), F32)
            for c in range(tr // CHUNK):
                rs = slice(c * CHUNK, (c + 1) * CHUNK)
                ls = slice(g * SGU_GROUP_DIM, (g + 1) * SGU_GROUP_DIM)
                vblk = vn[rs, ls]
                mixed = jnp.dot(w, vblk, preferred_element_type=F32) + bias
                dyb = dyv[rs, ls]
                du_ref[rs, ls] = (dyb * mixed * _gelu_grad(uv[rs, ls])).astype(du_ref.dtype)
                dmixed = dyb * ug[rs, ls]
                dbias = dbias + jnp.sum(dmixed, axis=-1, keepdims=True)
                dmb = dmixed.astype(MXU_DTYPE)
                dw = dw + lax.dot_general(dmb, vblk, _NT, preferred_element_type=F32)
                dvn_ref[rs, ls] = jnp.dot(wt, dmb, preferred_element_type=F32)
            dws_ref[g] += dw
            dbst_ref[:, g:g + 1] += dbias
        dvn = dvn_ref[...]
        dg_ref[...] += jnp.sum(dvn * xh, axis=0, keepdims=True)
        db_ref[...] += jnp.sum(dvn, axis=0, keepdims=True)
        dxh = dvn * g_ref[...]
        dvg = r * (dxh - jnp.mean(dxh, axis=-1, keepdims=True) - xh * jnp.mean(dxh * xh, axis=-1, keepdims=True))
        dv_ref[...] = (dvg * _gelu_grad(vv)).astype(dv_ref.dtype)

    wspec = _full((SGU_GROUPS, CHUNK, CHUNK))
    return pl.pallas_call(body, name=name, grid=(T // tr,),
                          in_specs=[_rowc(tr, C, 0), _rowc(tr, C, 1), _full((1, C)), _full((1, C)), wspec, wspec,
                                    _full((CHUNK, SGU_GROUPS)), _rowc(tr, C, 1)],
                          out_specs=[_rowc(tr, 2 * C, 0), _full((1, C)), _full((1, C)), wspec,
                                     _full((CHUNK, SGU_GROUPS))],
                          out_shape=[SDS((T, out_cols), MXU_DTYPE), SDS((1, C), F32), SDS((1, C), F32),
                                     SDS((SGU_GROUPS, CHUNK, CHUNK), F32), SDS((CHUNK, SGU_GROUPS), F32)],
                          scratch_shapes=[pltpu.VMEM((tr, C), F32)],
                          compiler_params=_cp("arbitrary"))(z, z, ln_g, ln_b, ws, wst, bst, dycat)


def adamw(w, g, m, v, *, name):
    R, C = w.shape
    tr = _pick(R, 512)
    c1 = 1.0 - ADAM_B1 ** ADAM_STEP
    c2 = 1.0 - ADAM_B2 ** ADAM_STEP

    def body(w_ref, g_ref, m_ref, v_ref, d_ref, nm_ref, nv_ref):
        gv = g_ref[...]
        nm = ADAM_B1 * m_ref[...] + (1.0 - ADAM_B1) * gv
        nv = ADAM_B2 * v_ref[...] + (1.0 - ADAM_B2) * (gv * gv)
        nm_ref[...] = nm
        nv_ref[...] = nv
        d_ref[...] = -ADAM_LR * ((nm / c1) / (jnp.sqrt(nv / c2) + ADAM_EPS) + ADAM_WD * w_ref[...])

    spec = _row(tr, C)
    return pl.pallas_call(body, name=name, grid=(R // tr,), in_specs=[spec] * 4, out_specs=[spec] * 3,
                          out_shape=[SDS((R, C), F32)] * 3, compiler_params=_cp("parallel"))(w, g, m, v)


ANY = pl.BlockSpec(memory_space=pl.ANY)


def _coords():
    return lax.axis_index("x"), lax.axis_index("y"), lax.axis_index("c")


def _other_chips(x, y):
    return [(1 - x, y), (x, 1 - y), (1 - x, 1 - y)]


def _remote(src, dst, send_sems, recv_sems, k, dev):
    return pltpu.make_async_remote_copy(src_ref=src, dst_ref=dst, send_sem=send_sems.at[k], recv_sem=recv_sems.at[k],
                                        device_id=dev, device_id_type=MESH)


LOCAL_CHUNKS = 8


class Exchange:
    def __init__(self, ins, out_shapes, scratch, start, wait):
        self.ins, self.out_shapes, self.scratch, self.start, self.wait = list(ins), list(out_shapes), list(scratch), start, wait


def run_exchange(ex, *, name):
    ni, no = len(ex.ins), len(ex.out_shapes)

    def body(*refs):
        parts = refs[:ni], refs[ni:ni + no], refs[ni + no:]
        ex.start(*parts)
        ex.wait(*parts)

    return pl.pallas_call(body, name=name, in_specs=[ANY] * ni, out_specs=[ANY] * no, out_shape=ex.out_shapes,
                          scratch_shapes=ex.scratch)(*ex.ins)


def call_hosting(body, ex, *, name, grid, in_specs, out_specs, out_shape, inputs, aliases, scratch=()):
    n_in, n_out, ni, no, ns = len(inputs), len(out_shape), len(ex.ins), len(ex.out_shapes), len(scratch)
    outs_at = n_in + ni
    scr_at = outs_at + n_out + no

    def wrapped(*refs):
        own = refs[:n_in] + refs[outs_at:outs_at + n_out] + refs[scr_at:scr_at + ns]
        parts = refs[n_in:outs_at], refs[outs_at + n_out:scr_at], refs[scr_at + ns:]
        ids = [pl.program_id(d) for d in range(len(grid))]
        first = functools.reduce(jnp.logical_and, [i == 0 for i in ids])
        last = functools.reduce(jnp.logical_and, [i == g - 1 for i, g in zip(ids, grid)])

        @pl.when(first)
        def _():
            ex.start(*parts)

        body(*own)

        @pl.when(last)
        def _():
            ex.wait(*parts)

    outs = pl.pallas_call(
        wrapped, name=name, grid=grid, in_specs=list(in_specs) + [ANY] * ni, out_specs=list(out_specs) + [ANY] * no,
        out_shape=list(out_shape) + ex.out_shapes, input_output_aliases=aliases,
        scratch_shapes=list(scratch) + ex.scratch,
        compiler_params=_cp(*["arbitrary"] * len(grid)))(*inputs, *ex.ins)
    return outs[:n_out], outs[n_out:]


def allgather_ici_exchange(shards, whole=()):
    na, n_all = len(shards), len(shards) + len(whole)
    arrays = list(shards) + list(whole)

    def copies(s_refs, o_refs, sems):
        send_sems, recv_sems, _ = sems
        x, y, c = _coords()
        j = 2 * x + y
        out = []
        for a in range(n_all):
            if a < na:
                half = shards[a].shape[0] // 2
                part = (pl.ds(c * half, half),)
            else:
                part = ()
            for k, (px, py) in enumerate(_other_chips(x, y)):
                send = _remote(s_refs[a].at[part] if part else s_refs[a], o_refs[a].at[(j,) + part], send_sems, recv_sems,
                               3 * a + k, (px, py, c))
                rows = o_refs[a].at[(2 * px + py,) + part]
                out.append((send, _remote(rows, rows, send_sems, recv_sems, 3 * a + k, (px, py, c))))
        return out

    def start(s_refs, o_refs, sems):
        x, y, c = _coords()
        j = 2 * x + y
        for a in range(n_all):
            chunks = LOCAL_CHUNKS if a < na else 1
            chunk = arrays[a].shape[0] // chunks
            for q in range(chunks):
                rows = pl.ds(q * chunk, chunk)
                pltpu.make_async_copy(s_refs[a].at[rows], o_refs[a].at[j, rows], sems[2].at[a]).start()
        for send, _ in copies(s_refs, o_refs, sems):
            send.start()

    def wait(s_refs, o_refs, sems):
        x, y, c = _coords()
        j = 2 * x + y
        for send, arrival in copies(s_refs, o_refs, sems):
            arrival.wait_recv()
            send.wait_send()
        for a in range(n_all):
            pltpu.make_async_copy(s_refs[a], o_refs[a].at[j], sems[2].at[a]).wait()

    return Exchange(arrays, [SDS((N_CHIPS,) + s.shape, s.dtype) for s in arrays],
                    [pltpu.SemaphoreType.DMA((3 * n_all,)), pltpu.SemaphoreType.DMA((3 * n_all,)),
                     pltpu.SemaphoreType.DMA((n_all,))], start, wait)


def allgather_forward(gathered, *, name):
    na = len(gathered)

    def body(*refs):
        o_refs = refs[na:2 * na]
        send_sems, recv_sems = refs[2 * na:]
        x, y, c = _coords()
        sibling = (x, y, 1 - c)
        cps = []
        for a in range(na):
            half = gathered[a].shape[1] // 2
            for k, (px, py) in enumerate(_other_chips(x, y)):
                mine = o_refs[a].at[2 * px + py, pl.ds(c * half, half)]
                theirs = o_refs[a].at[2 * px + py, pl.ds((1 - c) * half, half)]
                cps.append((_remote(mine, mine, send_sems, recv_sems, 3 * a + k, sibling),
                            _remote(theirs, theirs, send_sems, recv_sems, 3 * a + k, sibling)))
        for send, _ in cps:
            send.start()
        for send, arrival in cps:
            send.wait_send()
            arrival.wait_recv()

    return pl.pallas_call(body, name=name, in_specs=[ANY] * na, out_specs=[ANY] * na,
                          out_shape=[SDS(g.shape, g.dtype) for g in gathered],
                          input_output_aliases={a: a for a in range(na)},
                          scratch_shapes=[pltpu.SemaphoreType.DMA((3 * na,)), pltpu.SemaphoreType.DMA((3 * na,))])(*gathered)


def swap_halves_exchange(gs):
    na = len(gs)

    def copies(g_refs, o_refs, sems):
        x, y, c = _coords()
        out = []
        for a in range(na):
            half = gs[a].shape[1] // 2
            out.append(_remote(g_refs[a].at[:, pl.ds((1 - c) * half, half), :], o_refs[a], sems[0], sems[1], a,
                               (x, y, 1 - c)))
        return out

    def start(g_refs, o_refs, sems):
        for cp in copies(g_refs, o_refs, sems):
            cp.start()

    def wait(g_refs, o_refs, sems):
        for cp in copies(g_refs, o_refs, sems):
            cp.wait()

    return Exchange(gs, [SDS((g.shape[0], g.shape[1] // 2, g.shape[2]), g.dtype) for g in gs],
                    [pltpu.SemaphoreType.DMA((na,)), pltpu.SemaphoreType.DMA((na,))], start, wait)


def chip_partials_exchange(pbs):
    na = len(pbs)

    def copies(p_refs, o_refs, sems):
        x, y, c = _coords()
        out = []
        for a in range(na):
            for k, (px, py) in enumerate(_other_chips(x, y)):
                out.append(_remote(p_refs[a].at[2 * px + py], o_refs[a].at[k], sems[0], sems[1], 3 * a + k, (px, py, c)))
        return out

    def start(p_refs, o_refs, sems):
        for cp in copies(p_refs, o_refs, sems):
            cp.start()

    def wait(p_refs, o_refs, sems):
        for cp in copies(p_refs, o_refs, sems):
            cp.wait()

    return Exchange(pbs, [SDS((3,) + p.shape[1:], p.dtype) for p in pbs],
                    [pltpu.SemaphoreType.DMA((3 * na,)), pltpu.SemaphoreType.DMA((3 * na,))], start, wait)


def add_sibling_half(g, land, jc_idx, *, name):
    n, R, C = g.shape
    half = R // 2
    tr = _pick(half, 256)
    nt = half // tr

    def body(jc_ref, g_ref, l_ref, of_ref, ob_ref):
        s = g_ref[...] + l_ref[...].astype(F32)
        ob_ref[...] = s.astype(ob_ref.dtype)

        @pl.when(pl.program_id(1) == jc_ref[0])
        def _():
            of_ref[...] = s[0]

    blk = pl.BlockSpec((1, tr, C), lambda i, s, jc_ref: (s, i, 0))
    gblk = pl.BlockSpec((1, tr, C), lambda i, s, jc_ref: (s, jc_ref[1] * nt + i, 0))
    own = pl.BlockSpec((tr, C), lambda i, s, jc_ref: (i, 0))
    return pl.pallas_call(
        body, name=name,
        grid_spec=pltpu.PrefetchScalarGridSpec(num_scalar_prefetch=1, grid=(nt, n), in_specs=[gblk, blk],
                                               out_specs=[own, blk]),
        out_shape=[SDS((half, C), F32), SDS((n, half, C), WIRE_DTYPE)],
        compiler_params=_cp("parallel", "arbitrary"))(jc_idx, g, land)


def add_chip_partials(pf, rb, jc_idx, *, name):
    H, C = pf.shape
    tr = _pick(H, 256)

    def body(jc_ref, p_ref, r_ref, o_ref):
        s = p_ref[...]
        for k in range(3):
            s = s + r_ref[k].astype(F32)
        o_ref[...] = s

    pblk = pl.BlockSpec((tr, C), lambda i, jc_ref: (i, 0))
    rblk = pl.BlockSpec((3, tr, C), lambda i, jc_ref: (0, i, 0))
    oblk = pl.BlockSpec((None, tr, C), lambda i, jc_ref: (jc_ref[1], i, 0))
    return pl.pallas_call(
        body, name=name,
        grid_spec=pltpu.PrefetchScalarGridSpec(num_scalar_prefetch=1, grid=(H // tr,), in_specs=[pblk, rblk],
                                               out_specs=oblk),
        out_shape=SDS((2, H, C), F32), compiler_params=_cp("parallel"))(jc_idx, pf, rb)


def join_sibling_halves(bufs, *, name):
    na = len(bufs)

    def body(*refs):
        o_refs = refs[na:2 * na]
        send_sems, recv_sems = refs[2 * na:]
        x, y, c = _coords()
        cps = [_remote(o_refs[a].at[c], o_refs[a].at[c], send_sems, recv_sems, a, (x, y, 1 - c)) for a in range(na)]
        for cp in cps:
            cp.start()
        for a in range(na):
            cps[a].wait_send()
            _remote(o_refs[a].at[1 - c], o_refs[a].at[1 - c], send_sems, recv_sems, a, (x, y, 1 - c)).wait_recv()

    return pl.pallas_call(body, name=name, in_specs=[ANY] * na, out_specs=[ANY] * na,
                          out_shape=[SDS(b.shape, b.dtype) for b in bufs],
                          input_output_aliases={a: a for a in range(na)},
                          scratch_shapes=[pltpu.SemaphoreType.DMA((na,)), pltpu.SemaphoreType.DMA((na,))])(*bufs)


def exchange_pieces(v, *, scatter, name):
    P, C = v.shape[-2:]

    def body(v_ref, o_ref, send_sems, recv_sems, local_sem):
        x, y, c = _coords()
        me = 4 * x + 2 * y + c
        local = pltpu.make_async_copy(v_ref.at[me] if scatter else v_ref, o_ref.at[me], local_sem)
        local.start()
        cps = []
        for m in range(1, N_DEV):
            px = (1 - x) if m & 4 else x
            py = (1 - y) if m & 2 else y
            pc = (1 - c) if m & 1 else c
            src = v_ref.at[4 * px + 2 * py + pc] if scatter else v_ref
            cps.append(_remote(src, o_ref.at[me], send_sems, recv_sems, m - 1, (px, py, pc)))
        for cp in cps:
            cp.start()
        for cp in cps:
            cp.wait_send()
        for m in range(1, N_DEV):
            px = (1 - x) if m & 4 else x
            py = (1 - y) if m & 2 else y
            pc = (1 - c) if m & 1 else c
            slot = o_ref.at[4 * px + 2 * py + pc]
            _remote(slot, slot, send_sems, recv_sems, m - 1, (px, py, pc)).wait_recv()
        local.wait()

    return pl.pallas_call(body, name=name, in_specs=[ANY], out_specs=ANY, out_shape=SDS((N_DEV, P, C), v.dtype),
                          scratch_shapes=[pltpu.SemaphoreType.DMA((N_DEV - 1,)), pltpu.SemaphoreType.DMA((N_DEV - 1,)),
                                          pltpu.SemaphoreType.DMA(())])(v)


def sum_pieces(land, *, name):
    n, P, C = land.shape

    def body(l_ref, o_ref):
        s = l_ref[0]
        for d in range(1, n):
            s = s + l_ref[d]
        o_ref[...] = s

    return pl.pallas_call(body, name=name, out_shape=SDS((P, C), F32))(land)


BIG_SEGS = (
    ("w_in_even", (1024, 514), 1),
    ("s5_w_glu", (128, 512), 0),
    ("w_out_even", (256, 1024), 0),
    ("w_in_odd", (1024, 384), 1),
    ("w_out_odd", (256, 1024), 0),
    ("mlp_w1", (2, 1024, 1024), 2),
    ("mlp_w2", (2, 1024, 1024), 1),
)
BIG_NAMES = tuple(n for n, _, _ in BIG_SEGS)
EARLY_NAMES = ("w_in_even", "s5_w_glu")
LATE_NAMES = ("w_out_even", "w_in_odd", "w_out_odd", "mlp_w1", "mlp_w2")
REDUCED_EARLY = ("s5_w_glu", "w_out_even", "w_in_odd", "w_out_odd", "mlp_w1", "mlp_w2")
SHARDED_SMALL = ("pool_scale", "sgu_ln_g", "sgu_ln_b")
SMALL_SEGS = (
    ("mix_pre_g", (2, 1024)), ("mix_post_g", (2, 1024)), ("mlp_pre_g", (2, 1024)), ("mlp_post_g", (2, 1024)),
    ("s5_lam_re", (1, 32, 64)), ("s5_lam_im", (1, 32, 64)), ("s5_log_dt", (1, 32)),
    ("s5_b_re", (1, 32, 64, 16)), ("s5_b_im", (1, 32, 64, 16)), ("s5_c_re", (1, 32, 16, 64)), ("s5_c_im", (1, 32, 16, 64)),
    ("s5_d", (1, 512)), ("fox_b_f", (1, 8)), ("pool_w", (1, 4, 128, 128)), ("sgu_w_s", (1, 4, 128, 128)),
    ("sgu_b_s", (1, 4, 128)),
)
REDUCED_SEGS = SMALL_SEGS + tuple((n, (1, 512)) for n in SHARDED_SMALL) + (("loss", (1, 1)),)


def _cols_from_chips(g):
    n, R, C = g.shape
    return jnp.transpose(g, (1, 0, 2)).reshape(R, n * C)


def _chips_from_cols(m):
    R, C4 = m.shape
    return jnp.transpose(m.reshape(R, N_CHIPS, C4 // N_CHIPS), (1, 0, 2))


MLP_SHARD = 1024


def _w1_cols(l):
    def spec(tm, tn, tk):
        per = MLP_SHARD // tn
        return pl.BlockSpec((None, tk, tn), lambda i, j, k: (j // per, l * (MLP_SHARD // tk) + k, j % per))
    return spec


def _w1_rows_t(l):
    def spec(tm, tn, tk):
        if tk == N_CHIPS * MLP_SHARD:
            return pl.BlockSpec((N_CHIPS, tn, MLP_SHARD), lambda i, j, k: (0, l * (MLP_SHARD // tn) + j, 0))
        per = MLP_SHARD // tk
        return pl.BlockSpec((None, tn, tk), lambda i, j, k: (k // per, l * (MLP_SHARD // tn) + j, k % per))
    return spec


def _w2_rows(l):
    def spec(tm, tn, tk):
        if tk == N_CHIPS * MLP_SHARD:
            return pl.BlockSpec((N_CHIPS, MLP_SHARD, tn), lambda i, j, k: (0, l, j))
        per = MLP_SHARD // tk
        return pl.BlockSpec((None, tk, tn), lambda i, j, k: (k // per, l * per + k % per, j))
    return spec


def _w2_rows_t(l):
    def spec(tm, tn, tk):
        per = MLP_SHARD // tn
        return pl.BlockSpec((None, tn, tk), lambda i, j, k: (j // per, l * per + j % per, k))
    return spec


def _dw1_out(l):
    def spec(tm, tn, tk):
        per = MLP_SHARD // tn
        return pl.BlockSpec((None, tm, tn), lambda i, j, k: (j // per, l * (MLP_SHARD // tm) + i, j % per))
    return spec


def _dw2_out(l):
    def spec(tm, tn, tk):
        per = MLP_SHARD // tm
        return pl.BlockSpec((None, tm, tn), lambda i, j, k: (i // per, l * per + i % per, j))
    return spec


def _pack_vec(d, segs, rows_multiple):
    flat = jnp.concatenate([d[n].reshape(-1) for n, _ in segs])
    rows = -(-flat.shape[0] // LANES)
    rows = -(-rows // rows_multiple) * rows_multiple
    return jnp.pad(flat, (0, rows * LANES - flat.shape[0])).reshape(rows, LANES)


def _unpack_vec(v, segs):
    flat, out, r = v.reshape(-1), {}, 0
    for n, shape in segs:
        k = math.prod(shape)
        out[n] = flat[r:r + k].reshape(shape)
        r += k
    return out


def _block_diag(blocks):
    G, a, b = blocks.shape
    eye = jnp.eye(G, dtype=blocks.dtype)
    return (eye[:, None, :, None] * blocks[:, :, None, :]).reshape(G * a, G * b)


def _diag_blocks(m, G):
    a, b = m.shape[0] // G, m.shape[1] // G
    return jnp.stack([m[g * a:(g + 1) * a, g * b:(g + 1) * b] for g in range(G)])


def _sqrelu_epi(acc):
    r = jnp.maximum(acc, 0.0)
    return acc, r * r


def _sqrelu_bwd_epi(acc, a):
    return (acc * (2.0 * jnp.maximum(a.astype(F32), 0.0)),)


def _mlp_fwd(h, g1, g2, l, tag):
    T, D = h.shape
    a, s = matmul(h, g1, name=f"{tag}_up", mnk=(T, D_FF, D), b_spec=_w1_cols(l), epi=_sqrelu_epi,
                  out_dtypes=(MXU_DTYPE, MXU_DTYPE))
    m = matmul(s, g2, name=f"{tag}_down", mnk=(T, D, D_FF), b_spec=_w2_rows(l))
    return m, (h, a, s)


def _mlp_bwd(saved, dm, g1, g2, l, dg1, dg2, tag):
    h, a, s = saved
    T, D = h.shape
    gshape = (N_CHIPS, 2 * MLP_SHARD, MLP_SHARD)
    da = matmul(dm, g2, tb=True, name=f"{tag}_down_dx", mnk=(T, D_FF, D), b_spec=_w2_rows_t(l),
                epi=_sqrelu_bwd_epi, epi_in=(a,), out_dtype=MXU_DTYPE)
    dg2 = matmul(s, dm, ta=True, name=f"{tag}_down_dw", tm=MLP_SHARD, o_spec=_dw2_out(l), o_shape=gshape, prev=dg2)
    dh = matmul(da, g1, tb=True, name=f"{tag}_up_dx", mnk=(T, D, D_FF), b_spec=_w1_rows_t(l))
    dg1 = matmul(h, da, ta=True, name=f"{tag}_up_dw", o_spec=_dw1_out(l), o_shape=gshape, prev=dg1)
    return dh, dg1, dg2


def kernel(x, mix_pre_g, mix_post_g, mlp_pre_g, mlp_post_g, w_in_even, s5_lam_re, s5_lam_im, s5_log_dt, s5_b_re, s5_b_im, s5_c_re, s5_c_im, s5_d, s5_w_glu, fox_b_f, w_out_even, w_in_odd, pool_w, pool_scale, sgu_ln_g, sgu_ln_b, sgu_w_s, sgu_b_s, w_out_odd, mlp_w1, mlp_w2, loss_target, m_mix_pre_g, m_mix_post_g, m_mlp_pre_g, m_mlp_post_g, m_w_in_even, m_s5_lam_re, m_s5_lam_im, m_s5_log_dt, m_s5_b_re, m_s5_b_im, m_s5_c_re, m_s5_c_im, m_s5_d, m_s5_w_glu, m_fox_b_f, m_w_out_even, m_w_in_odd, m_pool_w, m_pool_scale, m_sgu_ln_g, m_sgu_ln_b, m_sgu_w_s, m_sgu_b_s, m_w_out_odd, m_mlp_w1, m_mlp_w2, v_mix_pre_g, v_mix_post_g, v_mlp_pre_g, v_mlp_post_g, v_w_in_even, v_s5_lam_re, v_s5_lam_im, v_s5_log_dt, v_s5_b_re, v_s5_b_im, v_s5_c_re, v_s5_c_im, v_s5_d, v_s5_w_glu, v_fox_b_f, v_w_out_even, v_w_in_odd, v_pool_w, v_pool_scale, v_sgu_ln_g, v_sgu_ln_b, v_sgu_w_s, v_sgu_b_s, v_w_out_odd, v_mlp_w1, v_mlp_w2):
    names = [n for n, _ in SMALL_SEGS] + [n for n, _, _ in BIG_SEGS] + list(SHARDED_SMALL)
    env = dict(locals())
    W = {n: env[n] for n in names}
    M = {n: env["m_" + n] for n in names}
    V = {n: env["v_" + n] for n in names}

    def shard(n):
        return W[n].reshape(-1, W[n].shape[-1]).astype(WIRE_DTYPE)

    small = jnp.pad(jnp.concatenate([W[n] for n in SHARDED_SMALL]), ((0, SUBLANES - len(SHARDED_SMALL)), (0, 0)))
    loss8, dx0, halves, local_small = _local_step(x[0], loss_target[0], {n: W[n] for n, _ in SMALL_SEGS},
                                                  [shard(n) for n in EARLY_NAMES], [shard(n) for n in LATE_NAMES], small)
    return _reduce_and_update(W, M, V, loss8, dx0, halves, local_small)


def _reduce_to_my_half(gs, names, tag, carry_swap=None, carry_ici=None):
    cx, cy, cc = _coords()
    jc_idx = jnp.stack([2 * cx + cy, cc]).astype(jnp.int32)
    swap = swap_halves_exchange(gs)
    from_sibling = carry_swap(swap) if carry_swap else run_exchange(swap, name=f"{tag}_to_sibling")
    sums = [add_sibling_half(g, l, jc_idx, name=f"{tag}_chip_sum_{n}") for n, g, l in zip(names, gs, from_sibling)]
    send = chip_partials_exchange([pb for _, pb in sums])
    from_chips = carry_ici(send) if carry_ici else run_exchange(send, name=f"{tag}_to_chips")
    return [add_chip_partials(pf, r, jc_idx, name=f"{tag}_sum_{n}") for n, (pf, _), r in zip(names, sums, from_chips)]


def _local_step(x0, target, P, early_shards, late_shards, small_shard):
    T = x0.shape[0]
    mix_pre_g, mix_post_g, mlp_pre_g, mlp_post_g = P["mix_pre_g"], P["mix_post_g"], P["mlp_pre_g"], P["mlp_post_g"]
    s5_lam_re, s5_lam_im, s5_log_dt = P["s5_lam_re"], P["s5_lam_im"], P["s5_log_dt"]
    s5_b_re, s5_b_im, s5_c_re, s5_c_im, s5_d = P["s5_b_re"], P["s5_b_im"], P["s5_c_re"], P["s5_c_im"], P["s5_d"]
    fox_b_f, pool_w, sgu_w_s, sgu_b_s = P["fox_b_f"], P["pool_w"], P["sgu_w_s"], P["sgu_b_s"]

    def gain(a, l):
        return a[l][None, :]

    lr = s5_lam_re[0].reshape(1, S5_LANES)
    li = s5_lam_im[0].reshape(1, S5_LANES)
    ldt = jnp.repeat(s5_log_dt[0], S5_STATE).reshape(1, S5_LANES)
    btr = s5_b_re[0].reshape(S5_LANES, S5_GROUP).T
    bti = s5_b_im[0].reshape(S5_LANES, S5_GROUP).T
    tf_re, tf_im, tb_re, tb_im, bbt_re, bbt_im = s5_disc_fwd(lr, li, ldt, btr, bti, name="s5_disc")
    same_group = (jnp.arange(S5_WIDTH)[:, None] // S5_GROUP) == (jnp.arange(S5_LANES)[None, :] // S5_STATE)
    b_bd = s5_interleave(jnp.where(same_group, jnp.tile(bbt_re, (S5_GROUPS, 1)), 0.0),
                         jnp.where(same_group, jnp.tile(bbt_im, (S5_GROUPS, 1)), 0.0), axis=1)
    cr2 = jnp.transpose(s5_c_re[0], (0, 2, 1)).reshape(S5_LANES, S5_GROUP)
    ci2 = jnp.transpose(s5_c_im[0], (0, 2, 1)).reshape(S5_LANES, S5_GROUP)
    c_bd = s5_interleave(jnp.where(same_group.T, jnp.tile(cr2, (1, S5_GROUPS)), 0.0),
                         -jnp.where(same_group.T, jnp.tile(ci2, (1, S5_GROUPS)), 0.0), axis=0)
    bf_pad = jnp.pad(fox_b_f, ((0, 0), (0, LANES - FOX_HEADS)))

    h1, early = rms_fwd(x0, gain(mix_pre_g, 0), allgather_ici_exchange(early_shards), name="l0_pre_norm")
    early = dict(zip(EARLY_NAMES, allgather_forward(early, name="allgather_early_weights")))
    w_in_e = jnp.pad(_cols_from_chips(early["w_in_even"]), ((0, 0), (0, EVEN_IN_PAD - EVEN_IN)))
    w_glu = early["s5_w_glu"].reshape(S5_WIDTH, S5_WIDTH)
    z = matmul(h1, w_in_e, name="l0_in_proj")
    s5_tiles = dict(tm=_pick(T, S5_NB), exact_tiles=True)
    xs = s5_scan(z, b_bd, tf_re, tf_im, reverse=False, name="s5_scan_fwd")
    yc = matmul(xs, c_bd, mnk=(T, S5_WIDTH, 2 * S5_NB), tn=S5_CB, a_spec=_lanes_of_chan, b_spec=_s5_c_block,
                name="s5_cx", **s5_tiles)
    yl, yg = s5_out_fwd(yc, z, s5_d, name="s5_out")
    gl = matmul(yg, w_glu, name="s5_glu_proj")
    ycat = glu_fwd(yg, gl, out_cols=D_MODEL, name="s5_glu")
    fgate = fox_gate_fwd(z, bf_pad, fl_col=FL_TILE, name="fox_gate")
    f_col = _pairs_col(fgate, T)
    f_row = _col_to_row(f_col, T)
    (ycat, lse_col), late = fox_fwd(z, f_col, f_row, ycat, allgather_ici_exchange(late_shards, [small_shard]),
                                    name="fox_fwd")
    small_all = late[-1]
    pool_scale_f, ln_g_f, ln_b_f = (small_all[:, i, :].reshape(1, N_CHIPS * LANES) for i in range(len(SHARDED_SMALL)))
    late = dict(zip(LATE_NAMES, allgather_forward(late[:-1], name="allgather_late_weights")))
    w_in_o = _cols_from_chips(late["w_in_odd"])
    w_in_o = jnp.concatenate([w_in_o[:, S5_WIDTH:], w_in_o[:, :S5_WIDTH]], axis=1)
    w_out_e = late["w_out_even"].reshape(D_MODEL, D_MODEL)
    w_out_o = late["w_out_odd"].reshape(D_MODEL, D_MODEL)
    g1, g2 = late["mlp_w1"], late["mlp_w2"]
    mo = matmul(ycat, w_out_e, name="l0_out_proj")
    x1, h2 = res_norm_fwd(x0, mo, gain(mix_post_g, 0), gain(mlp_pre_g, 0), name="l0_post_mlp0_pre_norm")
    m0, mlp0 = _mlp_fwd(h2, g1, g2, 0, "mlp0")

    x2, h3 = res_norm_fwd(x1, m0, gain(mlp_post_g, 0), gain(mix_pre_g, 1), name="mlp0_post_l1_pre_norm")
    z2 = matmul(h3, w_in_o, name="l1_in_proj")
    pooled = pool_window(z2, adjoint=False, in_col=POOL_COL, out_dtype=MXU_DTYPE, name="pool_fwd")
    pw_bd = _block_diag(pool_w[0])
    pw = matmul(pooled, pw_bd, name="pool_proj")
    ycat2 = colscale_fwd(pw, pool_scale_f, out_cols=D_MODEL, name="pool_scale")
    causal = jnp.tril(jnp.ones((CHUNK, CHUNK), dtype=bool))
    wsm = jnp.where(causal[None], sgu_w_s[0], 0.0)
    wsmt = jnp.transpose(wsm, (0, 2, 1))
    bst = sgu_b_s[0].T
    ycat2 = sgu_fwd(z2, ln_g_f, ln_b_f, wsm, bst, ycat2, name="sgu_fwd")
    mo2 = matmul(ycat2, w_out_o, name="l1_out_proj")
    x3, h4 = res_norm_fwd(x2, mo2, gain(mix_post_g, 1), gain(mlp_pre_g, 1), name="l1_post_mlp1_pre_norm")
    m1, mlp1 = _mlp_fwd(h4, g1, g2, 1, "mlp1")
    loss8, dx4 = res_norm_loss(x3, m1, gain(mlp_post_g, 1), target, name="mlp1_post_norm_loss")

    dm1, dg_mlp_post1 = rms_bwd(m1, gain(mlp_post_g, 1), dx4, None, name="mlp1_post_norm_bwd")
    dh4, dg1, dg2 = _mlp_bwd(mlp1, dm1, g1, g2, 1, None, None, "mlp1")
    dx3, dmo2, dg_mlp_pre1, dg_mix_post1 = norm_res_bwd(x3, gain(mlp_pre_g, 1), dh4, dx4, mo2, gain(mix_post_g, 1),
                                                        name="mlp1_pre_l1_post_norm_bwd")
    dycat2 = matmul(dmo2, w_out_o, tb=True, name="l1_out_proj_dx")
    dw_out_o = matmul(ycat2, dmo2, ta=True, name="l1_out_proj_dw")
    dpw, dpool_scale = colscale_bwd(pw, pool_scale_f, dycat2, name="pool_scale_bwd")
    dpooled = matmul(dpw, pw_bd, tb=True, name="pool_proj_dx")
    dpw_bd = matmul(pooled, dpw, ta=True, name="pool_proj_dw")
    dz2, dln_g, dln_b, dws, dbst = sgu_bwd(z2, ln_g_f, ln_b_f, wsm, wsmt, bst, dycat2, out_cols=3 * S5_WIDTH,
                                           name="sgu_bwd")
    dz2 = pool_window(dpooled, adjoint=True, into=dz2, out_col=POOL_COL, name="pool_bwd")
    dh3 = matmul(dz2, w_in_o, tb=True, name="l1_in_proj_dx")
    dw_in_o = matmul(h3, dz2, ta=True, name="l1_in_proj_dw")
    dw_in_o = jnp.concatenate([dw_in_o[:, 2 * S5_WIDTH:], dw_in_o[:, :2 * S5_WIDTH]], axis=1)
    dx2, dm0, dg_mix_pre1, dg_mlp_post0 = norm_res_bwd(x2, gain(mix_pre_g, 1), dh3, dx3, m0, gain(mlp_post_g, 0),
                                                       name="l1_pre_mlp0_post_norm_bwd")

    dh2, dg1, dg2 = _mlp_bwd(mlp0, dm0, g1, g2, 0, dg1, dg2, "mlp0")
    dx1, dmo, dg_mlp_pre0, dg_mix_post0 = norm_res_bwd(x1, gain(mlp_pre_g, 0), dh2, dx2, mo, gain(mix_post_g, 0),
                                                       name="mlp0_pre_l0_post_norm_bwd")
    dycat = matmul(dmo, w_out_e, tb=True, name="l0_out_proj_dx")
    dw_out_e = matmul(ycat, dmo, ta=True, name="l0_out_proj_dw")
    dyg_a, dgl = glu_bwd(yg, gl, dycat, name="s5_glu_bwd")
    dyg_b = matmul(dgl, w_glu, tb=True, name="s5_glu_proj_dx")
    dw_glu = matmul(yg, dgl, ta=True, name="s5_glu_proj_dw")
    dyl, du_skip, dd = s5_out_bwd(yl, z, s5_d, dyg_a, dyg_b, name="s5_out_bwd")
    dc_blocks = matmul(xs, dyl, ta=True, mnk=(2 * S5_LANES, S5_CB, T), tm=S5_NB, tn=S5_CB, b_spec=_chan_cols_of_i,
                       exact_tiles=True, name="s5_cx_dw")
    early_grads = {"s5_w_glu": dw_glu.reshape(N_CHIPS, -1, S5_WIDTH), "w_out_even": dw_out_e.reshape(N_CHIPS, -1, D_MODEL),
                   "w_in_odd": _chips_from_cols(dw_in_o), "w_out_odd": dw_out_o.reshape(N_CHIPS, -1, D_MODEL),
                   "mlp_w1": dg1, "mlp_w2": dg2}
    got = {}

    def reverse_scan(exchange):
        (got["lam"], got["dab_re"], got["dab_im"]), bufs = s5_scan(dyl, c_bd, tb_re, tb_im, reverse=True, states=xs,
                                                                   hosted=exchange, name="s5_scan_bwd")
        return bufs

    def attention_bwd(exchange):
        dd_col = fox_dd(ycat, dycat, name="fox_dd")
        (got["dk"], got["dv"], got["dfk"], got["dqt"], got["dfq"]), bufs = fox_bwd(
            z, dycat, f_col, f_row, _col_to_row(lse_col, T), _col_to_row(dd_col, T), exchange, name="fox_bwd")
        return bufs

    halves = _reduce_to_my_half([early_grads[n] for n in REDUCED_EARLY], REDUCED_EARLY, "early_grads",
                                reverse_scan, attention_bwd)
    lam, dab_re, dab_im, dk, dv = got["lam"], got["dab_re"], got["dab_im"], got["dk"], got["dv"]
    db_blocks = matmul(z, lam, ta=True, mnk=(S5_CB, 2 * S5_LANES, T), tm=S5_CB, tn=S5_NB, a_spec=_chan_rows_t,
                       exact_tiles=True, name="s5_bu_dw")
    du_b = matmul(lam, b_bd, tb=True, mnk=(T, S5_WIDTH, 2 * S5_NB), tn=S5_CB, a_spec=_lanes_of_chan,
                  b_spec=_s5_b_block_t, name="s5_bu_dx", **s5_tiles)
    du = add2(du_skip, du_b, name="s5_du")
    dq = jnp.transpose(got["dqt"], (1, 3, 0, 2)).reshape(T, FOX_WIDTH) * (FOX_HEAD_DIM ** -0.5)
    dfl, dbf = fox_gate_bwd(z, bf_pad, _pairs_to_lanes(got["dfk"], T), _pairs_to_lanes(_row_to_col(got["dfq"], T), T),
                            fl_col=FL_TILE, name="fox_gate_bwd")
    dz = jnp.concatenate([du, dq, dk, dv, dfl], axis=1).astype(MXU_DTYPE)
    dw_in_e = matmul(h1, dz, ta=True, name="l0_in_proj_dw")[:, :EVEN_IN]

    def in_proj_dx(exchange):
        got["dh1"], bufs = matmul(dz, w_in_e, tb=True, hosted=exchange, name="l0_in_proj_dx")
        return bufs

    def pre_norm_bwd(exchange):
        (got["dx0"], got["dg_mix_pre0"]), bufs = rms_bwd(x0, gain(mix_pre_g, 0), got["dh1"], dx1, hosted=exchange,
                                                         name="l0_pre_norm_bwd")
        return bufs

    halves = halves + _reduce_to_my_half([_chips_from_cols(dw_in_e)], ["w_in_even"], "late_grads", in_proj_dx, pre_norm_bwd)
    dx0, dg_mix_pre0 = got["dx0"], got["dg_mix_pre0"]

    groups_per_block = S5_CB // S5_GROUP
    own_group = (jnp.arange(S5_CB)[:, None] // S5_GROUP) == ((jnp.arange(S5_LANES)[None, :] // S5_STATE) % groups_per_block)
    db_re, db_im = s5_deinterleave(db_blocks, axis=1)
    dbbt_re = jnp.where(own_group, db_re, 0.0).reshape(groups_per_block, S5_GROUP, S5_LANES).sum(0)
    dbbt_im = jnp.where(own_group, db_im, 0.0).reshape(groups_per_block, S5_GROUP, S5_LANES).sum(0)
    dlr, dli, dldt8, dbtr, dbti = s5_disc_bwd(lr, li, ldt, btr, bti, dab_re, dab_im, dbbt_re, dbbt_im, name="s5_disc_bwd")
    dc_re, dc_im = s5_deinterleave(dc_blocks, axis=0)
    dcr2 = jnp.where(own_group.T, dc_re, 0.0).reshape(S5_LANES, groups_per_block, S5_GROUP).sum(1)
    dci2 = -jnp.where(own_group.T, dc_im, 0.0).reshape(S5_LANES, groups_per_block, S5_GROUP).sum(1)

    def c_layout(a):
        return jnp.transpose(a.reshape(S5_GROUPS, S5_STATE, S5_GROUP), (0, 2, 1))[None]

    def b_layout(a):
        return a.T.reshape(1, S5_GROUPS, S5_STATE, S5_GROUP)

    local_small = {
        "mix_pre_g": jnp.concatenate([dg_mix_pre0, dg_mix_pre1]), "mix_post_g": jnp.concatenate([dg_mix_post0, dg_mix_post1]),
        "mlp_pre_g": jnp.concatenate([dg_mlp_pre0, dg_mlp_pre1]), "mlp_post_g": jnp.concatenate([dg_mlp_post0, dg_mlp_post1]),
        "s5_lam_re": dlr.reshape(1, S5_GROUPS, S5_STATE), "s5_lam_im": dli.reshape(1, S5_GROUPS, S5_STATE),
        "s5_log_dt": dldt8[0:1, 0:S5_GROUPS],
        "s5_b_re": b_layout(dbtr), "s5_b_im": b_layout(dbti), "s5_c_re": c_layout(dcr2), "s5_c_im": c_layout(dci2),
        "s5_d": dd, "fox_b_f": dbf[:, 0:FOX_HEADS],
        "pool_w": _diag_blocks(dpw_bd, len(POOL_WINDOWS))[None],
        "sgu_w_s": jnp.where(causal[None], dws, 0.0)[None], "sgu_b_s": dbst.T[None],
        "pool_scale": dpool_scale, "sgu_ln_g": dln_g, "sgu_ln_b": dln_b,
    }
    return loss8, dx0, dict(zip(REDUCED_EARLY + ("w_in_even",), halves)), local_small


def _reduce_and_update(W, M, V, loss8, dx0, halves, local_small):
    cx, cy, cc = _coords()
    chip = 2 * cx + cy

    summed = dict(local_small, loss=loss8[0:1, 0:1])
    vec = _pack_vec(summed, REDUCED_SEGS, N_DEV * SUBLANES)
    piece = vec.shape[0] // N_DEV
    landed = exchange_pieces(vec.reshape(N_DEV, piece, LANES), scatter=True, name="small_grads_scatter")
    mine = sum_pieces(landed, name="small_grads_sum")
    everyone = exchange_pieces(mine, scatter=False, name="small_grads_gather")
    G = _unpack_vec(everyone, REDUCED_SEGS)
    loss = G["loss"].reshape(())
    for n in SHARDED_SMALL:
        G[n] = lax.dynamic_slice_in_dim(G[n], chip * LANES, LANES, axis=1)

    reduced = join_sibling_halves([halves[n] for n in BIG_NAMES], name="big_grads_join")
    for n, r in zip(BIG_NAMES, reduced):
        G[n] = r.reshape(W[n].shape)

    def two_d(a):
        return a.reshape(-1, a.shape[-1])

    delta, new_m, new_v = {}, {}, {}
    for n in BIG_NAMES:
        d_, m_, v_ = adamw(two_d(W[n]), two_d(G[n]), two_d(M[n]), two_d(V[n]), name=f"adamw_{n}")
        delta[n], new_m[n], new_v[n] = (t.reshape(W[n].shape) for t in (d_, m_, v_))
    packed = [_pack_vec(src, SMALL_SEGS, SUBLANES) for src in (W, G, M, V)]
    outs = adamw(*packed, name="adamw_replicated")
    for dst, t in zip((delta, new_m, new_v), outs):
        dst.update(_unpack_vec(t, SMALL_SEGS))
    sharded_segs = tuple((n, (1, LANES)) for n in SHARDED_SMALL)
    packed = [_pack_vec(src, sharded_segs, 1) for src in (W, G, M, V)]
    outs = adamw(*packed, name="adamw_sharded_vectors")
    for dst, t in zip((delta, new_m, new_v), outs):
        dst.update(_unpack_vec(t, sharded_segs))

    order = ["mix_pre_g", "mix_post_g", "mlp_pre_g", "mlp_post_g", "w_in_even", "s5_lam_re", "s5_lam_im", "s5_log_dt",
             "s5_b_re", "s5_b_im", "s5_c_re", "s5_c_im", "s5_d", "s5_w_glu", "fox_b_f", "w_out_even", "w_in_odd",
             "pool_w", "pool_scale", "sgu_ln_g", "sgu_ln_b", "sgu_w_s", "sgu_b_s", "w_out_odd", "mlp_w1", "mlp_w2"]
    return (loss, dx0[None], *[G[n] for n in order], *[delta[n] for n in order],
            *[new_m[n] for n in order], *[new_v[n] for n in order])
```

```python
import functools
import math

import jax
import jax.numpy as jnp
from jax import lax
from jax.experimental import pallas as pl
from jax.experimental.pallas import tpu as pltpu

F32 = jnp.float32
MXU_DTYPE = jnp.bfloat16
WIRE_DTYPE = jnp.bfloat16
EPS = 1e-6
VMEM_LIMIT_BYTES = 48 * 1024 * 1024
LANES = 128
SUBLANES = 8

D_MODEL = 1024
S5_WIDTH = 512
S5_GROUP = 16
S5_GROUPS = 32
S5_STATE = 64
S5_LANES = S5_GROUPS * S5_STATE
FOX_HEADS = 8
FOX_HEAD_DIM = 64
FOX_WIDTH = 512
EVEN_IN = S5_WIDTH + 3 * FOX_WIDTH + FOX_HEADS
EVEN_IN_PAD = 2176
POOL_WINDOWS = (2, 4, 8, 16)
POOL_HALO = 16
POOL_GROUP_DIM = 128
SGU_GROUPS = 4
SGU_GROUP_DIM = 128
CHUNK = 128
D_FF = 4096

ADAM_LR = 0.001
ADAM_B1 = 0.9
ADAM_B2 = 0.999
ADAM_EPS = 1e-08
ADAM_WD = 0.01
ADAM_STEP = 10

MESH_AXES = ("x", "y", "c")
MESH = pl.DeviceIdType.MESH
N_CHIPS = 4
N_DEV = 8

SDS = jax.ShapeDtypeStruct


def _cp(*sem):
    return pltpu.CompilerParams(dimension_semantics=sem, vmem_limit_bytes=VMEM_LIMIT_BYTES)


def _pick(dim, pref):
    if dim <= pref:
        return dim
    t = pref
    while t >= 256:
        if dim % t == 0:
            return t
        t //= 2
    return dim


def _row(tr, c):
    return pl.BlockSpec((tr, c), lambda i: (i, 0))


def _full(shape):
    nd = len(shape)
    return pl.BlockSpec(shape, lambda *_: (0,) * nd)


def _gelu_grad(x):
    c = math.sqrt(2.0 / math.pi)
    t = jnp.tanh(c * (x + 0.044715 * x * x * x))
    return 0.5 * (1.0 + t) + 0.5 * x * (1.0 - t * t) * c * (1.0 + 3.0 * 0.044715 * x * x)


MATMUL_VMEM_BYTES = 36 * 1024 * 1024


def matmul(a, b, *, name, ta=False, tb=False, out_dtype=F32, tm=2048, tn=1024, tk=4096, mnk=None, a_koff=0,
           a_spec=None, b_spec=None, o_spec=None, o_shape=None, prev=None, epi=None, epi_in=(), out_dtypes=None,
           exact_tiles=False, hosted=None):
    if mnk is None:
        M, K = (a.shape[1], a.shape[0]) if ta else a.shape
        K2, N = (b.shape[1], b.shape[0]) if tb else b.shape
        assert K == K2, (a.shape, b.shape, ta, tb)
    else:
        M, N, K = mnk
    out_dtypes = tuple(out_dtypes) if out_dtypes is not None else (out_dtype,)
    n_out, n_epi = len(out_dtypes), len(epi_in)
    tm, tn, tk = _pick(M, tm), _pick(N, tn), _pick(K, tk)

    def vmem_bytes(tm_, tn_, tk_):
        tiles = tm_ * tk_ * a.dtype.itemsize + tk_ * tn_ * b.dtype.itemsize
        tiles += tm_ * tn_ * (sum(jnp.dtype(d).itemsize for d in out_dtypes) + sum(e.dtype.itemsize for e in epi_in))
        return 2 * tiles + tm_ * tn_ * 4 * (tk_ < K)

    def halves(t, dim):
        return [t] + ([t // 2] if t % (2 * LANES) == 0 and t // 2 >= 512 and dim % (t // 2) == 0 else [])

    if exact_tiles:
        halves = lambda t, dim: [t]
    fits = [(m_, n_) for m_ in halves(tm, M) for n_ in halves(tn, N) if vmem_bytes(m_, n_, tk) <= MATMUL_VMEM_BYTES]
    if fits:
        tm, tn = max(fits, key=lambda t: (t[0] * t[1], t[0]))
    else:
        tm, tn = halves(tm, M)[-1], halves(tn, N)[-1]
        while vmem_bytes(tm, tn, tk) > MATMUL_VMEM_BYTES and tk % 2 == 0 and tk > 512:
            tk //= 2
    nk = K // tk
    assert a_koff % tk == 0 and not (ta and a_koff)
    ko = a_koff // tk
    dn = (((0 if ta else 1,), (1 if tb else 0,)), ((), ()))

    def body(*refs):
        a_ref, b_ref = refs[0], refs[1]
        epi_refs = refs[2:2 + n_epi]
        o_refs = refs[len(refs) - n_out - (nk > 1):len(refs) - (nk > 1)]
        k = pl.program_id(2)
        bv = b_ref[...]
        if bv.ndim == 3 and tb:
            cw = bv.shape[-1]
            prod = sum(lax.dot_general(a_ref[:, c * cw:(c + 1) * cw].astype(MXU_DTYPE), bv[c].astype(MXU_DTYPE), dn,
                                       preferred_element_type=F32) for c in range(bv.shape[0]))
        else:
            if bv.ndim == 3:
                bv = bv.reshape(-1, bv.shape[-1])
            prod = lax.dot_general(a_ref[...].astype(MXU_DTYPE), bv.astype(MXU_DTYPE), dn, preferred_element_type=F32)

        def finish(acc):
            res = (acc,) if epi is None else epi(acc, *[r[...] for r in epi_refs])
            for o_ref, r in zip(o_refs, res):
                o_ref[...] = r.astype(o_ref.dtype)

        if nk == 1:
            finish(prod)
            return
        acc_ref = refs[-1]

        @pl.when(k == 0)
        def _():
            acc_ref[...] = prod

        @pl.when(jnp.logical_and(k > 0, k < nk - 1))
        def _():
            acc_ref[...] += prod

        @pl.when(k == nk - 1)
        def _():
            finish(acc_ref[...] + prod)

    if a_spec is None:
        a_spec = pl.BlockSpec((tk, tm), lambda i, j, k: (k, i)) if ta else pl.BlockSpec((tm, tk), lambda i, j, k: (i, k + ko))
    else:
        a_spec = a_spec(tm, tn, tk)
    if b_spec is None:
        bs = pl.BlockSpec((tn, tk), lambda i, j, k: (j, k)) if tb else pl.BlockSpec((tk, tn), lambda i, j, k: (k, j))
    else:
        bs = b_spec(tm, tn, tk)
    tile = pl.BlockSpec((tm, tn), lambda i, j, k: (i, j))
    os_ = tile if o_spec is None else o_spec(tm, tn, tk)
    ins, in_specs, aliases = [a, b, *epi_in], [a_spec, bs] + [tile] * n_epi, {}
    if prev is not None:
        aliases = {len(ins): 0}
        ins.append(prev)
        in_specs.append(pl.BlockSpec(memory_space=pl.ANY))
    shapes = [SDS((M, N) if o_shape is None else o_shape, dt) for dt in out_dtypes]
    scratch = [pltpu.VMEM((tm, tn), F32)] if nk > 1 else []
    if hosted is not None:
        outs, bufs = call_hosting(body, hosted, name=name, grid=(M // tm, N // tn, nk), in_specs=in_specs,
                                  out_specs=[os_] * n_out, out_shape=shapes, inputs=ins, aliases=aliases, scratch=scratch)
        return (outs[0] if n_out == 1 else outs), bufs
    outs = pl.pallas_call(
        body, name=name, grid=(M // tm, N // tn, nk),
        in_specs=in_specs, out_specs=[os_] * n_out, out_shape=shapes, input_output_aliases=aliases,
        scratch_shapes=scratch, compiler_params=_cp("parallel", "parallel", "arbitrary"),
    )(*ins)
    return outs[0] if n_out == 1 else outs


def _rms_hat(x):
    return x * lax.rsqrt(jnp.mean(x * x, axis=-1, keepdims=True) + EPS)


def rms_fwd(x, g, hosted, *, name):
    T, D = x.shape
    tr = _pick(T, 512)

    def body(x_ref, g_ref, o_ref):
        o_ref[...] = (_rms_hat(x_ref[...]) * g_ref[...]).astype(o_ref.dtype)

    (h,), bufs = call_hosting(body, hosted, name=name, grid=(T // tr,), in_specs=[_row(tr, D), _full((1, D))],
                              out_specs=[_row(tr, D)], out_shape=[SDS((T, D), MXU_DTYPE)], inputs=[x, g], aliases={})
    return h, bufs


def res_norm_fwd(x, y, g_post, g_next, *, name):
    T, D = x.shape
    tr = _pick(T, 512)

    def body(x_ref, y_ref, gp_ref, gn_ref, o_ref, h_ref):
        xn = x_ref[...] + _rms_hat(y_ref[...]) * gp_ref[...]
        o_ref[...] = xn
        h_ref[...] = (_rms_hat(xn) * gn_ref[...]).astype(h_ref.dtype)

    return pl.pallas_call(body, name=name, grid=(T // tr,),
                          in_specs=[_row(tr, D), _row(tr, D), _full((1, D)), _full((1, D))],
                          out_specs=[_row(tr, D), _row(tr, D)], out_shape=[SDS((T, D), F32), SDS((T, D), MXU_DTYPE)],
                          compiler_params=_cp("parallel"))(x, y, g_post, g_next)


def res_norm_loss(x, y, g_post, target, *, name):
    T, D = x.shape
    tr = _pick(T, 512)

    def body(x_ref, y_ref, g_ref, t_ref, l_ref, d_ref):
        err = x_ref[...] + _rms_hat(y_ref[...]) * g_ref[...] - t_ref[...]
        d_ref[...] = err * (1.0 / D)

        @pl.when(pl.program_id(0) == 0)
        def _():
            l_ref[...] = jnp.zeros_like(l_ref)

        l_ref[...] += 0.5 * jnp.sum(jnp.mean(err * err, axis=-1, keepdims=True))

    return pl.pallas_call(body, name=name, grid=(T // tr,),
                          in_specs=[_row(tr, D), _row(tr, D), _full((1, D)), _row(tr, D)],
                          out_specs=[_full((SUBLANES, LANES)), _row(tr, D)],
                          out_shape=[SDS((SUBLANES, LANES), F32), SDS((T, D), F32)],
                          compiler_params=_cp("arbitrary"))(x, y, g_post, target)


def _rms_bwd_rows(x, g, dy):
    r = lax.rsqrt(jnp.mean(x * x, axis=-1, keepdims=True) + EPS)
    xh = x * r
    dxh = dy * g
    return r * (dxh - xh * jnp.mean(dxh * xh, axis=-1, keepdims=True)), jnp.sum(dy * xh, axis=0, keepdims=True)


def norm_res_bwd(x, g_pre, dh, res, y, g_post, *, name):
    T, D = x.shape
    tr = _pick(T, 512)

    def body(x_ref, gp_ref, dh_ref, res_ref, y_ref, gy_ref, dx_ref, dy_ref, dgp_ref, dgy_ref):
        dx, dgp = _rms_bwd_rows(x_ref[...], gp_ref[...], dh_ref[...])
        dx = dx + res_ref[...]
        dx_ref[...] = dx
        dy, dgy = _rms_bwd_rows(y_ref[...], gy_ref[...], dx)
        dy_ref[...] = dy.astype(dy_ref.dtype)

        @pl.when(pl.program_id(0) == 0)
        def _():
            dgp_ref[...] = jnp.zeros_like(dgp_ref)
            dgy_ref[...] = jnp.zeros_like(dgy_ref)

        dgp_ref[...] += dgp
        dgy_ref[...] += dgy

    row, vec = _row(tr, D), _full((1, D))
    return pl.pallas_call(body, name=name, grid=(T // tr,), in_specs=[row, vec, row, row, row, vec],
                          out_specs=[row, row, vec, vec],
                          out_shape=[SDS((T, D), F32), SDS((T, D), MXU_DTYPE), SDS((1, D), F32), SDS((1, D), F32)],
                          compiler_params=_cp("arbitrary"))(x, g_pre, dh, res, y, g_post)


def rms_bwd(x, g, dy, res, *, name, hosted=None):
    T, D = x.shape
    tr = _pick(T, 512)
    has_res = res is not None

    def body(*refs):
        if has_res:
            x_ref, g_ref, dy_ref, res_ref, dx_ref, dg_ref = refs
        else:
            x_ref, g_ref, dy_ref, dx_ref, dg_ref = refs
        dx, dg = _rms_bwd_rows(x_ref[...], g_ref[...], dy_ref[...])
        if has_res:
            dx = dx + res_ref[...]
        dx_ref[...] = dx.astype(dx_ref.dtype)

        @pl.when(pl.program_id(0) == 0)
        def _():
            dg_ref[...] = jnp.zeros_like(dg_ref)

        dg_ref[...] += dg

    ins = [x, g, dy] + ([res] if has_res else [])
    in_specs = [_row(tr, D), _full((1, D)), _row(tr, D)] + ([_row(tr, D)] if has_res else [])
    out_shape = [SDS((T, D), F32 if has_res else MXU_DTYPE), SDS((1, D), F32)]
    out_specs = [_row(tr, D), _full((1, D))]
    if hosted is not None:
        return call_hosting(body, hosted, name=name, grid=(T // tr,), in_specs=in_specs, out_specs=out_specs,
                            out_shape=out_shape, inputs=ins, aliases={})
    return pl.pallas_call(body, name=name, grid=(T // tr,), in_specs=in_specs, out_specs=out_specs,
                          out_shape=out_shape, compiler_params=_cp("arbitrary"))(*ins)


def _s5_disc(lr, li, ldt, btr, bti):
    dt = jnp.exp(ldt)
    k = lax.broadcasted_iota(jnp.int32, (SUBLANES, S5_LANES), 0).astype(F32)
    kf = k + 1.0
    kb = 8.0 - k
    ph = li * dt
    lm = lr * dt
    tf_re = jnp.exp(kf * lm) * jnp.cos(kf * ph)
    tf_im = jnp.exp(kf * lm) * jnp.sin(kf * ph)
    tb_re = jnp.exp(kb * lm) * jnp.cos(kb * ph)
    tb_im = -jnp.exp(kb * lm) * jnp.sin(kb * ph)
    mag = jnp.exp(lm)
    ab_re = mag * jnp.cos(ph)
    ab_im = mag * jnp.sin(ph)
    den = lr * lr + li * li
    nr = ab_re - 1.0
    ni = ab_im
    q_re = (nr * lr + ni * li) / den
    q_im = (ni * lr - nr * li) / den
    bbt_re = q_re * btr - q_im * bti
    bbt_im = q_re * bti + q_im * btr
    return tf_re, tf_im, tb_re, tb_im, bbt_re, bbt_im


def _s5_disc_core(lr, li, ldt, btr, bti):
    dt = jnp.exp(ldt)
    mag = jnp.exp(lr * dt)
    ab_re = mag * jnp.cos(li * dt)
    ab_im = mag * jnp.sin(li * dt)
    den = lr * lr + li * li
    nr = ab_re - 1.0
    ni = ab_im
    q_re = (nr * lr + ni * li) / den
    q_im = (ni * lr - nr * li) / den
    return ab_re, ab_im, q_re * btr - q_im * bti, q_re * bti + q_im * btr


def s5_disc_fwd(lr, li, ldt, btr, bti, *, name):
    def body(lr_ref, li_ref, ldt_ref, btr_ref, bti_ref, *outs):
        vals = _s5_disc(lr_ref[...], li_ref[...], ldt_ref[...], btr_ref[...], bti_ref[...])
        for o, v in zip(outs, vals):
            o[...] = v

    tab = SDS((SUBLANES, S5_LANES), F32)
    bb = SDS((S5_GROUP, S5_LANES), F32)
    return pl.pallas_call(body, name=name, out_shape=[tab, tab, tab, tab, bb, bb])(lr, li, ldt, btr, bti)


def s5_disc_bwd(lr, li, ldt, btr, bti, dab_re, dab_im, dbbt_re, dbbt_im, *, name):
    def body(lr_ref, li_ref, ldt_ref, btr_ref, bti_ref, dar_ref, dai_ref, dbr_ref, dbi_ref,
             dlr_ref, dli_ref, dldt_ref, dbtr_ref, dbti_ref):
        _, vjp = jax.vjp(_s5_disc_core, lr_ref[...], li_ref[...], ldt_ref[...], btr_ref[...], bti_ref[...])
        dlr, dli, dldt, dbtr, dbti = vjp((dar_ref[...], dai_ref[...], dbr_ref[...], dbi_ref[...]))
        dlr_ref[...] = dlr
        dli_ref[...] = dli
        dbtr_ref[...] = dbtr
        dbti_ref[...] = dbti
        lane_group = lax.broadcasted_iota(jnp.int32, (S5_LANES, LANES), 0) // S5_STATE
        col = lax.broadcasted_iota(jnp.int32, (S5_LANES, LANES), 1)
        ind = (lane_group == col).astype(F32)
        dldt_ref[...] = jnp.dot(jnp.broadcast_to(dldt, (SUBLANES, S5_LANES)), ind,
                                precision=lax.Precision.HIGHEST, preferred_element_type=F32)

    row = SDS((1, S5_LANES), F32)
    bb = SDS((S5_GROUP, S5_LANES), F32)
    return pl.pallas_call(body, name=name, out_shape=[row, row, SDS((SUBLANES, LANES), F32), bb, bb])(
        lr, li, ldt, btr, bti, dab_re, dab_im, dbbt_re, dbbt_im)


S5_NB = 1024


S5_CB = S5_WIDTH * S5_NB // S5_LANES


def _chan_rows_t(tm, tn, tk):
    return pl.BlockSpec((tk, S5_CB), lambda i, j, k: (k, j // 2))


def _chan_cols_of_i(tm, tn, tk):
    return pl.BlockSpec((tk, S5_CB), lambda i, j, k: (k, i // 2))


def _lanes_of_chan(tm, tn, tk):
    return pl.BlockSpec((tm, 2 * S5_NB), lambda i, j, k: (i, j))


def _s5_b_block_t(tm, tn, tk):
    return pl.BlockSpec((S5_CB, 2 * S5_NB), lambda i, j, k: (j, j))


def _s5_c_block(tm, tn, tk):
    return pl.BlockSpec((2 * S5_NB, S5_CB), lambda i, j, k: (j, j))


def s5_interleave(re, im, axis):
    parts = []
    for n in range(S5_LANES // S5_NB):
        sl = [slice(None)] * re.ndim
        sl[axis] = slice(n * S5_NB, (n + 1) * S5_NB)
        parts += [re[tuple(sl)], im[tuple(sl)]]
    return jnp.concatenate(parts, axis=axis)


def s5_deinterleave(a, axis):
    re, im = [], []
    for n in range(S5_LANES // S5_NB):
        sl = [slice(None)] * a.ndim
        sl[axis] = slice(2 * n * S5_NB, (2 * n + 1) * S5_NB)
        re.append(a[tuple(sl)])
        sl[axis] = slice((2 * n + 1) * S5_NB, (2 * n + 2) * S5_NB)
        im.append(a[tuple(sl)])
    return jnp.concatenate(re, axis=axis), jnp.concatenate(im, axis=axis)


def s5_scan(src, mat, tab_re, tab_im, *, reverse, name, states=None, hosted=None):
    T = src.shape[0]
    nb = S5_NB
    tc = _pick(T, 256)
    nl = S5_LANES // nb
    nt = T // tc
    ntile = tc // SUBLANES
    with_da = states is not None
    assert reverse or not with_da
    step_rows = ((1, 7), (2, 6), (4, 4)) if reverse else ((1, 0), (2, 1), (4, 3))
    drive_dn = _NT if reverse else (((1,), (0,)), ((), ()))

    def body(*refs):
        if with_da:
            (src_ref, wr_ref, wi_ref, tr_ref, ti_ref, sr_ref, si_ref, hr_ref, hi_ref, xo_ref, dar_ref, dai_ref,
             cr_ref, ci_ref, mr_ref, mi_ref, br_ref, bi_ref, ar_ref, ai_ref) = refs
        else:
            src_ref, wr_ref, wi_ref, tr_ref, ti_ref, xo_ref, cr_ref, ci_ref, mr_ref, mi_ref, br_ref, bi_ref = refs

        @pl.when(pl.program_id(1) == 0)
        def _():
            cr_ref[...] = jnp.zeros_like(cr_ref)
            ci_ref[...] = jnp.zeros_like(ci_ref)
            if with_da:
                ar_ref[...] = jnp.zeros_like(ar_ref)
                ai_ref[...] = jnp.zeros_like(ai_ref)

        lhs = src_ref[...].astype(MXU_DTYPE)
        br_ref[...] = lax.dot_general(lhs, wr_ref[...].astype(MXU_DTYPE), drive_dn, preferred_element_type=F32)
        bi_ref[...] = lax.dot_general(lhs, wi_ref[...].astype(MXU_DTYPE), drive_dn, preferred_element_type=F32)

        seen = jnp.where(pl.program_id(1) < nt - 1, 1.0, 0.0)

        def add_da(lr, li, r0, last_r, last_i):
            first = lax.broadcasted_iota(jnp.int32, (SUBLANES, nb), 0) == 0
            pr = jnp.where(first, last_r, pltpu.roll(sr_ref[pl.ds(r0, SUBLANES), :], 1, 0))
            pi = jnp.where(first, last_i, pltpu.roll(si_ref[pl.ds(r0, SUBLANES), :], 1, 0))
            ar_ref[...] += lr * pr + li * pi
            ai_ref[...] += li * pr - lr * pi

        io = lax.broadcasted_iota(jnp.int32, (SUBLANES, nb), 0)
        for s_, (d, r) in enumerate(step_rows):
            keep = (io < SUBLANES - d) if reverse else (io >= d)
            mr_ref[s_] = jnp.where(keep, tr_ref[r:r + 1, :], 0.0)
            mi_ref[s_] = jnp.where(keep, ti_ref[r:r + 1, :], 0.0)

        def tile(i, carry):
            cr, ci = carry
            j = (ntile - 1 - i) if reverse else i
            r0 = pl.multiple_of(j * SUBLANES, SUBLANES)
            xr = br_ref[pl.ds(r0, SUBLANES), :]
            xi = bi_ref[pl.ds(r0, SUBLANES), :]
            for s_, (d, _) in enumerate(step_rows):
                sh = (SUBLANES - d) if reverse else d
                sr = pltpu.roll(xr, sh, 0)
                si = pltpu.roll(xi, sh, 0)
                pr, pi = mr_ref[s_], mi_ref[s_]
                xr, xi = xr + pr * sr - pi * si, xi + pr * si + pi * sr
            tr, ti = tr_ref[...], ti_ref[...]
            xr, xi = xr + tr * cr - ti * ci, xi + tr * ci + ti * cr
            xo_ref[pl.ds(r0, SUBLANES), 0:nb] = xr
            xo_ref[pl.ds(r0, SUBLANES), nb:2 * nb] = xi
            if with_da:
                @pl.when(j > 0)
                def _():
                    p0 = pl.multiple_of(r0 - SUBLANES, SUBLANES)
                    add_da(xr, xi, r0, sr_ref[pl.ds(p0, SUBLANES), :][SUBLANES - 1:SUBLANES, :],
                           si_ref[pl.ds(p0, SUBLANES), :][SUBLANES - 1:SUBLANES, :])

                @pl.when(j == 0)
                def _():
                    add_da(xr, xi, r0, hr_ref[SUBLANES - 1:SUBLANES, :] * seen, hi_ref[SUBLANES - 1:SUBLANES, :] * seen)
            if reverse:
                return xr[0:1, :], xi[0:1, :]
            return xr[SUBLANES - 1:SUBLANES, :], xi[SUBLANES - 1:SUBLANES, :]

        cr, ci = lax.fori_loop(0, ntile, tile, (cr_ref[0:1, :], ci_ref[0:1, :]))
        cr_ref[0:1, :] = cr
        ci_ref[0:1, :] = ci
        if with_da:
            @pl.when(pl.program_id(1) == nt - 1)
            def _():
                dar_ref[...] = jnp.sum(ar_ref[...], axis=0, keepdims=True)
                dai_ref[...] = jnp.sum(ai_ref[...], axis=0, keepdims=True)

    def tmap(t):
        return (nt - 1 - t) if reverse else t

    hb = tc // SUBLANES
    re_spec = pl.BlockSpec((tc, nb), lambda n, t: (tmap(t), 2 * n))
    im_spec = pl.BlockSpec((tc, nb), lambda n, t: (tmap(t), 2 * n + 1))
    tab_spec = pl.BlockSpec((SUBLANES, nb), lambda n, t: (0, n))
    out_spec = pl.BlockSpec((tc, 2 * nb), lambda n, t: (tmap(t), n))
    out_shape = SDS((T, 2 * S5_LANES), F32)
    scratch = [pltpu.VMEM((SUBLANES, nb), F32), pltpu.VMEM((SUBLANES, nb), F32),
               pltpu.VMEM((len(step_rows), SUBLANES, nb), F32), pltpu.VMEM((len(step_rows), SUBLANES, nb), F32),
               pltpu.VMEM((tc, nb), F32), pltpu.VMEM((tc, nb), F32)]
    src_spec = pl.BlockSpec((tc, S5_CB), lambda n, t: (tmap(t), n))
    if reverse:
        wr_spec = pl.BlockSpec((nb, S5_CB), lambda n, t: (2 * n, n))
        wi_spec = pl.BlockSpec((nb, S5_CB), lambda n, t: (2 * n + 1, n))
    else:
        wr_spec = pl.BlockSpec((S5_CB, nb), lambda n, t: (n, 2 * n))
        wi_spec = pl.BlockSpec((S5_CB, nb), lambda n, t: (n, 2 * n + 1))
    drive_specs = [src_spec, wr_spec, wi_spec, tab_spec, tab_spec]
    drive = [src, mat, mat, tab_re, tab_im]
    if not with_da:
        return pl.pallas_call(body, name=name, grid=(nl, nt), in_specs=drive_specs,
                              out_specs=out_spec, out_shape=out_shape, scratch_shapes=scratch,
                              compiler_params=_cp("parallel", "arbitrary"))(*drive)
    re_halo = pl.BlockSpec((SUBLANES, nb), lambda n, t: (jnp.maximum(tmap(t) * hb - 1, 0), 2 * n))
    im_halo = pl.BlockSpec((SUBLANES, nb), lambda n, t: (jnp.maximum(tmap(t) * hb - 1, 0), 2 * n + 1))
    acc = pl.BlockSpec((1, nb), lambda n, t: (0, n))
    row = SDS((1, S5_LANES), F32)
    return call_hosting(
        body, hosted, name=name, grid=(nl, nt),
        in_specs=drive_specs + [re_spec, im_spec, re_halo, im_halo],
        out_specs=[out_spec, acc, acc], out_shape=[out_shape, row, row],
        inputs=drive + [states, states, states, states], aliases={},
        scratch=scratch + [pltpu.VMEM((SUBLANES, nb), F32), pltpu.VMEM((SUBLANES, nb), F32)])


def s5_out_fwd(yc, u, d, *, name):
    T, C = yc.shape
    tr = _pick(T, 512)

    def body(yc_ref, u_ref, d_ref, yl_ref, yg_ref):
        yl = yc_ref[...] + d_ref[...] * u_ref[...]
        yl_ref[...] = yl
        yg_ref[...] = jax.nn.gelu(yl)

    return pl.pallas_call(body, name=name, grid=(T // tr,), in_specs=[_row(tr, C), _row(tr, C), _full((1, C))],
                          out_specs=[_row(tr, C)] * 2, out_shape=[SDS((T, C), F32)] * 2,
                          compiler_params=_cp("parallel"))(yc, u, d)


def glu_fwd(yg, gl, *, out_cols, name):
    T, C = yg.shape
    tr = _pick(T, 512)

    def body(yg_ref, gl_ref, o_ref):
        o_ref[...] = yg_ref[...] * jax.nn.sigmoid(gl_ref[...])

    return pl.pallas_call(body, name=name, grid=(T // tr,), in_specs=[_row(tr, C)] * 2, out_specs=_row(tr, C),
                          out_shape=SDS((T, out_cols), F32), compiler_params=_cp("parallel"))(yg, gl)


def glu_bwd(yg, gl, dy, *, name):
    T, C = yg.shape
    tr = _pick(T, 512)

    def body(yg_ref, gl_ref, dy_ref, dyg_ref, dgl_ref):
        s = jax.nn.sigmoid(gl_ref[...])
        dyv = dy_ref[...]
        dyg_ref[...] = dyv * s
        dgl_ref[...] = (dyv * yg_ref[...] * s * (1.0 - s)).astype(dgl_ref.dtype)

    return pl.pallas_call(body, name=name, grid=(T // tr,), in_specs=[_row(tr, C)] * 3, out_specs=[_row(tr, C)] * 2,
                          out_shape=[SDS((T, C), F32), SDS((T, C), MXU_DTYPE)],
                          compiler_params=_cp("parallel"))(yg, gl, dy)


def s5_out_bwd(yl, u, d, dyg_a, dyg_b, *, name):
    T, C = yl.shape
    tr = _pick(T, 512)

    def body(yl_ref, u_ref, d_ref, da_ref, db_ref, dyl_ref, du_ref, dd_ref):
        dyl = (da_ref[...] + db_ref[...]) * _gelu_grad(yl_ref[...])
        dyl_ref[...] = dyl.astype(dyl_ref.dtype)
        du_ref[...] = dyl * d_ref[...]

        @pl.when(pl.program_id(0) == 0)
        def _():
            dd_ref[...] = jnp.zeros_like(dd_ref)

        dd_ref[...] += jnp.sum(dyl * u_ref[...], axis=0, keepdims=True)

    return pl.pallas_call(body, name=name, grid=(T // tr,),
                          in_specs=[_row(tr, C), _row(tr, C), _full((1, C)), _row(tr, C), _row(tr, C)],
                          out_specs=[_row(tr, C), _row(tr, C), _full((1, C))],
                          out_shape=[SDS((T, C), MXU_DTYPE), SDS((T, C), F32), SDS((1, C), F32)],
                          compiler_params=_cp("arbitrary"))(yl, u, d, dyg_a, dyg_b)


def add2(a, b, *, name):
    T, C = a.shape
    tr = _pick(T, 512)

    def body(a_ref, b_ref, o_ref):
        o_ref[...] = a_ref[...] + b_ref[...]

    return pl.pallas_call(body, name=name, grid=(T // tr,), in_specs=[_row(tr, C)] * 2, out_specs=_row(tr, C),
                          out_shape=SDS((T, C), F32), compiler_params=_cp("parallel"))(a, b)


def _tri(n, upper):
    r = lax.broadcasted_iota(jnp.int32, (n, n), 0)
    c = lax.broadcasted_iota(jnp.int32, (n, n), 1)
    return ((c >= r) if upper else (c <= r)).astype(F32)


def fox_gate_fwd(fl, bf, *, fl_col, name):
    T = fl.shape[0]
    tb = _pick(T, 256)

    def body(fl_ref, bf_ref, f_ref, c_ref):
        @pl.when(pl.program_id(0) == 0)
        def _():
            c_ref[...] = jnp.zeros_like(c_ref)

        lf = jax.nn.log_sigmoid(fl_ref[...] + bf_ref[...])
        f = jnp.dot(_tri(tb, False), lf, precision=lax.Precision.HIGHEST, preferred_element_type=F32) + c_ref[0:1, :]
        f_ref[...] = f * LOG2E
        c_ref[0:1, :] = f[tb - 1:tb, :]

    fl_spec = pl.BlockSpec((tb, LANES), lambda i: (i, fl_col))
    return pl.pallas_call(body, name=name, grid=(T // tb,), in_specs=[fl_spec, _full((1, LANES))],
                          out_specs=_row(tb, LANES), out_shape=SDS((T, LANES), F32),
                          scratch_shapes=[pltpu.VMEM((SUBLANES, LANES), F32)], compiler_params=_cp("arbitrary"))(fl, bf)


def fox_gate_bwd(fl, bf, df_keys, df_queries, *, fl_col, name):
    T = fl.shape[0]
    tb = _pick(T, 256)
    nt = T // tb

    def body(fl_ref, bf_ref, dfk_ref, dfq_ref, dfl_ref, dbf_ref, c_ref):
        @pl.when(pl.program_id(0) == 0)
        def _():
            c_ref[...] = jnp.zeros_like(c_ref)
            dbf_ref[...] = jnp.zeros_like(dbf_ref)

        dlf = jnp.dot(_tri(tb, True), dfk_ref[...] + dfq_ref[...], precision=lax.Precision.HIGHEST,
                      preferred_element_type=F32) + c_ref[0:1, :]
        c_ref[0:1, :] = dlf[0:1, :]
        dfl = dlf * jax.nn.sigmoid(-(fl_ref[...] + bf_ref[...]))
        dfl_ref[...] = dfl
        dbf_ref[...] += jnp.sum(dfl, axis=0, keepdims=True)

    rev = pl.BlockSpec((tb, LANES), lambda i: (nt - 1 - i, 0))
    fl_rev = pl.BlockSpec((tb, LANES), lambda i: (nt - 1 - i, fl_col))
    return pl.pallas_call(body, name=name, grid=(nt,), in_specs=[fl_rev, _full((1, LANES)), rev, rev],
                          out_specs=[rev, _full((1, LANES))], out_shape=[SDS((T, LANES), F32), SDS((1, LANES), F32)],
                          scratch_shapes=[pltpu.VMEM((SUBLANES, LANES), F32)],
                          compiler_params=_cp("arbitrary"))(fl, bf, df_keys, df_queries)


FOX_BLOCK = 512
FOX_PAIRS = FOX_HEADS // 2
_NT = (((1,), (1,)), ((), ()))


LOG2E = 1.4426950408889634
FOX_FWD_UNROLL = 4
FOX_BWD_UNROLL = 2


def _fox_block(T):
    return _pick(T, FOX_BLOCK)


def _own_lanes(lane, hh):
    return (lane < FOX_HEAD_DIM) if hh == 0 else (lane >= FOX_HEAD_DIM)


def _grouped_steps(step, lo, n, unroll, init):
    def trip(t, c):
        for u in range(unroll):
            c = step(lo + t * unroll + u, c)
        return c

    carry = lax.fori_loop(0, n // unroll, trip, init)
    for u in range(unroll - 1):
        carry = lax.cond(n % unroll > u, lambda c: step(lo + (n // unroll) * unroll + u, c), lambda c: c, carry)
    return carry


Q_TILE0, K_TILE0, V_TILE0, O_TILE0 = 4, 8, 12, 4
FL_TILE = 16
POOL_COL = 2


def fox_fwd(z, f_col, f_row, ycat, hosted, *, name):
    T = z.shape[0]
    blk = _fox_block(T)
    nb = T // blk
    scale = FOX_HEAD_DIM ** -0.5

    def body(q_ref, k_ref, v_ref, fc_ref, fr_ref, prev_ref, o_ref, l_ref):
        i = pl.program_id(1)
        row = lax.broadcasted_iota(jnp.int32, (blk, blk), 0)
        col = lax.broadcasted_iota(jnp.int32, (blk, blk), 1)
        lane = lax.broadcasted_iota(jnp.int32, (blk, LANES), 1)
        qt = q_ref[...] * (scale * LOG2E)
        outs = []
        for hh in range(2):
            qh = jnp.where(_own_lanes(lane, hh), qt, 0.0).astype(MXU_DTYPE)
            fi = fc_ref[0, :, hh:hh + 1]

            def step(j, carry, masked=False):
                m, l, acc = carry
                r0 = pl.multiple_of(j * blk, blk)
                kj = k_ref[pl.ds(r0, blk), :].astype(MXU_DTYPE)
                vj = v_ref[pl.ds(r0, blk), :].astype(MXU_DTYPE)
                s = lax.dot_general(qh, kj, _NT, preferred_element_type=F32) + (fi - fr_ref[0, j, hh:hh + 1, :])
                if masked:
                    s = jnp.where(col <= row, s, -jnp.inf)
                m_new = jnp.maximum(m, jnp.max(s, axis=-1, keepdims=True))
                p = jnp.exp2(s - m_new)
                alpha = jnp.exp2(m - m_new)
                l = alpha * l + jnp.sum(p, axis=-1, keepdims=True)
                acc = alpha * acc + jnp.dot(p.astype(MXU_DTYPE), vj, preferred_element_type=F32)
                return m_new, l, acc

            init = (jnp.full((blk, 1), -jnp.inf, F32), jnp.zeros((blk, 1), F32), jnp.zeros((blk, LANES), F32))
            m, l, acc = step(i, _grouped_steps(step, 0, i, FOX_FWD_UNROLL, init), True)
            outs.append(acc / l)
            l_ref[0, :, hh:hh + 1] = m + jnp.log2(l)
        o_ref[...] = jnp.where(_own_lanes(lane, 0), outs[0], outs[1])

    qspec = pl.BlockSpec((blk, LANES), lambda h, i: (i, Q_TILE0 + h))
    kspec = pl.BlockSpec((T, LANES), lambda h, i: (0, K_TILE0 + h))
    vspec = pl.BlockSpec((T, LANES), lambda h, i: (0, V_TILE0 + h))
    ospec = pl.BlockSpec((blk, LANES), lambda h, i: (i, O_TILE0 + h))
    cspec = pl.BlockSpec((1, blk, 2), lambda h, i: (h, i, 0))
    rspec = pl.BlockSpec((1, nb, 2, blk), lambda h, i: (h, 0, 0, 0))
    return call_hosting(body, hosted, name=name, grid=(FOX_PAIRS, nb),
                        in_specs=[qspec, kspec, vspec, cspec, rspec, ANY], out_specs=[ospec, cspec],
                        out_shape=[SDS(ycat.shape, F32), SDS((FOX_PAIRS, T, 2), F32)],
                        inputs=[z, z, z, f_col, f_row, ycat], aliases={5: 0})


def fox_dd(ycat, dycat, *, name):
    T = ycat.shape[0]
    blk = _fox_block(T)

    def body(o_ref, do_ref, dd_ref):
        lane = lax.broadcasted_iota(jnp.int32, (blk, LANES), 1)
        prod = do_ref[...] * o_ref[...]
        for hh in range(2):
            dd_ref[0, :, hh:hh + 1] = jnp.sum(jnp.where(_own_lanes(lane, hh), prod, 0.0), axis=-1, keepdims=True)

    ospec = pl.BlockSpec((blk, LANES), lambda h, i: (i, O_TILE0 + h))
    return pl.pallas_call(body, name=name, grid=(FOX_PAIRS, T // blk), in_specs=[ospec, ospec],
                          out_specs=pl.BlockSpec((1, blk, 2), lambda h, i: (h, i, 0)),
                          out_shape=SDS((FOX_PAIRS, T, 2), F32), compiler_params=_cp("parallel", "parallel"))(ycat, dycat)


def fox_bwd(z, dycat, f_col, f_row, lse_row, dd_row, hosted, *, name):
    T = z.shape[0]
    blk = _fox_block(T)
    nb = T // blk
    scale = FOX_HEAD_DIM ** -0.5

    def body(q_ref, k_ref, v_ref, do_ref, fc_ref, fr_ref, lr_ref, dr_ref, dk_ref, dv_ref, df_ref, dqt_ref, dfq_ref):
        j = pl.program_id(1)

        @pl.when(j == 0)
        def _():
            dqt_ref[...] = jnp.zeros_like(dqt_ref)
            dfq_ref[...] = jnp.zeros_like(dfq_ref)

        row = lax.broadcasted_iota(jnp.int32, (blk, blk), 0)
        col = lax.broadcasted_iota(jnp.int32, (blk, blk), 1)
        lane = lax.broadcasted_iota(jnp.int32, (blk, LANES), 1)
        kt = k_ref[...]
        vt = v_ref[...]
        dks, dvs = [], []
        for hh in range(2):
            own = _own_lanes(lane, hh)
            kh = jnp.where(own, kt, 0.0).astype(MXU_DTYPE)
            vh = jnp.where(own, vt, 0.0).astype(MXU_DTYPE)
            kht = kh.T
            fj = fc_ref[0, :, hh:hh + 1]

            def step(i, carry, masked=False):
                dk, dv, df = carry
                r0 = pl.multiple_of(i * blk, blk)
                qi = (q_ref[pl.ds(r0, blk), :] * (scale * LOG2E)).astype(MXU_DTYPE)
                doi = do_ref[pl.ds(r0, blk), :].astype(MXU_DTYPE)
                st = lax.dot_general(kh, qi, _NT, preferred_element_type=F32) + (fr_ref[0, i, hh:hh + 1, :] - fj)
                pt = jnp.exp2(st - lr_ref[0, i, hh:hh + 1, :])
                if masked:
                    pt = jnp.where(col >= row, pt, 0.0)
                dv = dv + jnp.dot(pt.astype(MXU_DTYPE), doi, preferred_element_type=F32)
                dpt = lax.dot_general(vh, doi, _NT, preferred_element_type=F32)
                dst = pt * (dpt - dr_ref[0, i, hh:hh + 1, :])
                dsb = dst.astype(MXU_DTYPE)
                dk = dk + jnp.dot(dsb, qi, preferred_element_type=F32)
                df = df - jnp.sum(dst, axis=-1, keepdims=True)
                dqt_ref[0, i] += jnp.dot(kht, dsb, preferred_element_type=F32)
                dfq_ref[0, i, hh:hh + 1, :] += jnp.sum(dst, axis=0, keepdims=True)
                return dk, dv, df

            init = (jnp.zeros((blk, LANES), F32), jnp.zeros((blk, LANES), F32), jnp.zeros((blk, 1), F32))
            dk, dv, df = _grouped_steps(step, j + 1, nb - 1 - j, FOX_BWD_UNROLL, step(j, init, True))
            dks.append(dk * (1.0 / LOG2E))
            dvs.append(dv)
            df_ref[0, :, hh:hh + 1] = df
        dk_ref[...] = jnp.where(_own_lanes(lane, 0), dks[0], dks[1])
        dv_ref[...] = jnp.where(_own_lanes(lane, 0), dvs[0], dvs[1])

    bspec = pl.BlockSpec((blk, LANES), lambda h, j: (j, h))
    qspec = pl.BlockSpec((T, LANES), lambda h, j: (0, Q_TILE0 + h))
    kspec = pl.BlockSpec((blk, LANES), lambda h, j: (j, K_TILE0 + h))
    vspec = pl.BlockSpec((blk, LANES), lambda h, j: (j, V_TILE0 + h))
    dospec = pl.BlockSpec((T, LANES), lambda h, j: (0, O_TILE0 + h))
    cspec = pl.BlockSpec((1, blk, 2), lambda h, j: (h, j, 0))
    rspec = pl.BlockSpec((1, nb, 2, blk), lambda h, j: (h, 0, 0, 0))
    dqspec = pl.BlockSpec((1, nb, LANES, blk), lambda h, j: (h, 0, 0, 0))
    return call_hosting(body, hosted, name=name, grid=(FOX_PAIRS, nb),
                        in_specs=[qspec, kspec, vspec, dospec, cspec, rspec, rspec, rspec],
                        out_specs=[bspec, bspec, cspec, dqspec, rspec],
                        out_shape=[SDS((T, FOX_WIDTH), F32), SDS((T, FOX_WIDTH), F32), SDS((FOX_PAIRS, T, 2), F32),
                                   SDS((FOX_PAIRS, nb, LANES, blk), F32), SDS((FOX_PAIRS, nb, 2, blk), F32)],
                        inputs=[z, z, z, dycat, f_col, f_row, lse_row, dd_row], aliases={})


def _pairs_col(a, T):
    return jnp.transpose(a[:, :FOX_HEADS].reshape(T, FOX_PAIRS, 2), (1, 0, 2))


def _col_to_row(a, T):
    blk = _fox_block(T)
    return jnp.transpose(a.reshape(FOX_PAIRS, T // blk, blk, 2), (0, 1, 3, 2))


def _row_to_col(a, T):
    return jnp.transpose(a, (0, 1, 3, 2)).reshape(FOX_PAIRS, T, 2)


def _pairs_to_lanes(a, T):
    flat = jnp.transpose(a, (1, 0, 2)).reshape(T, FOX_HEADS)
    return jnp.pad(flat, ((0, 0), (0, LANES - FOX_HEADS)))


def _pool_counts(t0, n, w):
    t = (t0 + lax.broadcasted_iota(jnp.int32, (n, 1), 0)).astype(F32)
    return jnp.minimum(t + 1.0, float(w))


def pool_window(x, *, adjoint, name, in_col=0, into=None, out_col=0, out_dtype=F32):
    T, C = x.shape[0], len(POOL_WINDOWS) * POOL_GROUP_DIM
    tr = _pick(T, 512)
    nt = T // tr
    hb = tr // POOL_HALO
    n = tr + POOL_HALO

    def body(x_ref, h_ref, *rest):
        o_ref = rest[-1]
        i = pl.program_id(0)
        cur = x_ref[...]
        if adjoint:
            halo = h_ref[...] * jnp.where(i < nt - 1, 1.0, 0.0)
            ext = jnp.concatenate([cur, halo], axis=0)
            t0 = i * tr
        else:
            halo = h_ref[...] * jnp.where(i > 0, 1.0, 0.0)
            ext = jnp.concatenate([halo, cur], axis=0)
            t0 = i * tr - POOL_HALO
        sums = {}
        for g, w in enumerate(POOL_WINDOWS):
            ls = slice(g * POOL_GROUP_DIM, (g + 1) * POOL_GROUP_DIM)
            s = ext[:, ls]
            if adjoint:
                s = s / _pool_counts(t0, n, w)
            d = 1
            while d < w:
                s = s + pltpu.roll(s, (n - d) if adjoint else d, 0)
                d *= 2
            if adjoint:
                o_ref[:, ls] = (s[0:tr, :] - cur[:, ls]).astype(o_ref.dtype)
            else:
                o_ref[:, ls] = (s[POOL_HALO:n, :] / _pool_counts(i * tr, tr, w) - cur[:, ls]).astype(o_ref.dtype)

    if adjoint:
        halo_spec = pl.BlockSpec((POOL_HALO, C), lambda i: (jnp.minimum((i + 1) * hb, T // POOL_HALO - 1), in_col))
    else:
        halo_spec = pl.BlockSpec((POOL_HALO, C), lambda i: (jnp.maximum(i * hb - 1, 0), in_col))
    x_spec = pl.BlockSpec((tr, C), lambda i: (i, in_col))
    if into is None:
        return pl.pallas_call(body, name=name, grid=(nt,), in_specs=[x_spec, halo_spec], out_specs=_row(tr, C),
                              out_shape=SDS((T, C), out_dtype), compiler_params=_cp("parallel"))(x, x)
    return pl.pallas_call(body, name=name, grid=(nt,), in_specs=[x_spec, halo_spec, ANY],
                          out_specs=pl.BlockSpec((tr, C), lambda i: (i, out_col)), out_shape=SDS(into.shape, into.dtype),
                          input_output_aliases={2: 0}, compiler_params=_cp("parallel"))(x, x, into)


def colscale_fwd(a, s, *, out_cols, name):
    T, C = a.shape
    tr = _pick(T, 512)

    def body(a_ref, s_ref, o_ref):
        o_ref[...] = (a_ref[...] * s_ref[...]).astype(o_ref.dtype)

    return pl.pallas_call(body, name=name, grid=(T // tr,), in_specs=[_row(tr, C), _full((1, C))], out_specs=_row(tr, C),
                          out_shape=SDS((T, out_cols), MXU_DTYPE), compiler_params=_cp("parallel"))(a, s)


def colscale_bwd(a, s, dy, *, name):
    T, C = a.shape
    tr = _pick(T, 512)

    def body(a_ref, s_ref, dy_ref, da_ref, ds_ref):
        dyv = dy_ref[...]
        da_ref[...] = (dyv * s_ref[...]).astype(da_ref.dtype)

        @pl.when(pl.program_id(0) == 0)
        def _():
            ds_ref[...] = jnp.zeros_like(ds_ref)

        ds_ref[...] += jnp.sum(dyv * a_ref[...], axis=0, keepdims=True)

    return pl.pallas_call(body, name=name, grid=(T // tr,), in_specs=[_row(tr, C), _full((1, C)), _row(tr, C)],
                          out_specs=[_row(tr, C), _full((1, C))], out_shape=[SDS((T, C), MXU_DTYPE), SDS((1, C), F32)],
                          compiler_params=_cp("arbitrary"))(a, s, dy)


SGU_ROWS = 512


def _sgu_norm(v, ln_g, ln_b):
    vg = jax.nn.gelu(v)
    xc = vg - jnp.mean(vg, axis=-1, keepdims=True)
    r = lax.rsqrt(jnp.mean(xc * xc, axis=-1, keepdims=True) + EPS)
    xh = xc * r
    return xh * ln_g + ln_b, xh, r


def _rowc(tr, c, cb):
    return pl.BlockSpec((tr, c), lambda i: (i, cb))


def sgu_fwd(z, ln_g, ln_b, ws, bst, ycat, *, name):
    T, C = z.shape[0], SGU_GROUPS * SGU_GROUP_DIM
    tr = _pick(T, SGU_ROWS)

    def body(u_ref, v_ref, g_ref, b_ref, ws_ref, bst_ref, prev_ref, o_ref):
        vn, _, _ = _sgu_norm(v_ref[...], g_ref[...], b_ref[...])
        vn = vn.astype(MXU_DTYPE)
        ug = jax.nn.gelu(u_ref[...])
        for g in range(SGU_GROUPS):
            w = ws_ref[g].astype(MXU_DTYPE)
            bias = bst_ref[:, g:g + 1]
            for c in range(tr // CHUNK):
                rs = slice(c * CHUNK, (c + 1) * CHUNK)
                ls = slice(g * SGU_GROUP_DIM, (g + 1) * SGU_GROUP_DIM)
                mixed = jnp.dot(w, vn[rs, ls], preferred_element_type=F32) + bias
                o_ref[rs, ls] = (ug[rs, ls] * mixed).astype(o_ref.dtype)

    return pl.pallas_call(body, name=name, grid=(T // tr,),
                          in_specs=[_rowc(tr, C, 0), _rowc(tr, C, 1), _full((1, C)), _full((1, C)),
                                    _full((SGU_GROUPS, CHUNK, CHUNK)), _full((CHUNK, SGU_GROUPS)), ANY],
                          out_specs=_rowc(tr, C, 1), out_shape=SDS(ycat.shape, ycat.dtype), input_output_aliases={6: 0},
                          compiler_params=_cp("parallel"))(z, z, ln_g, ln_b, ws, bst, ycat)


def sgu_bwd(z, ln_g, ln_b, ws, wst, bst, dycat, *, out_cols, name):
    T, C = z.shape[0], SGU_GROUPS * SGU_GROUP_DIM
    tr = _pick(T, SGU_ROWS)

    def body(u_ref, v_ref, g_ref, b_ref, ws_ref, wst_ref, bst_ref, dy_ref,
             duv_ref, dg_ref, db_ref, dws_ref, dbst_ref, dvn_ref):
        du_ref = duv_ref.at[:, 0:C]
        dv_ref = duv_ref.at[:, C:2 * C]
        @pl.when(pl.program_id(0) == 0)
        def _():
            dg_ref[...] = jnp.zeros_like(dg_ref)
            db_ref[...] = jnp.zeros_like(db_ref)
            dws_ref[...] = jnp.zeros_like(dws_ref)
            dbst_ref[...] = jnp.zeros_like(dbst_ref)

        uv = u_ref[...]
        vv = v_ref[...]
        vn, xh, r = _sgu_norm(vv, g_ref[...], b_ref[...])
        vn = vn.astype(MXU_DTYPE)
        ug = jax.nn.gelu(uv)
        dyv = dy_ref[...]
        for g in range(SGU_GROUPS):
            w = ws_ref[g].astype(MXU_DTYPE)
            wt = wst_ref[g].astype(MXU_DTYPE)
            bias = bst_ref[:, g:g + 1]
            dw = jnp.zeros((CHUNK, CHUNK), F32)
            dbias = jnp.zeros((CHUNK, 1), F32)
            for c in range(tr // CHUNK):
                rs = slice(c * CHUNK, (c + 1) * CHUNK)
                ls = slice(g * SGU_GROUP_DIM, (g + 1) * SGU_GROUP_DIM)
                vblk = vn[rs, ls]
                mixed = jnp.dot(w, vblk, preferred_element_type=F32) + bias
                dyb = dyv[rs, ls]
                du_ref[rs, ls] = (dyb * mixed * _gelu_grad(uv[rs, ls])).astype(du_ref.dtype)
                dmixed = dyb * ug[rs, ls]
                dbias = dbias + jnp.sum(dmixed, axis=-1, keepdims=True)
                dmb = dmixed.astype(MXU_DTYPE)
                dw = dw + lax.dot_general(dmb, vblk, _NT, preferred_element_type=F32)
                dvn_ref[rs, ls] = jnp.dot(wt, dmb, preferred_element_type=F32)
            dws_ref[g] += dw
            dbst_ref[:, g:g + 1] += dbias
        dvn = dvn_ref[...]
        dg_ref[...] += jnp.sum(dvn * xh, axis=0, keepdims=True)
        db_ref[...] += jnp.sum(dvn, axis=0, keepdims=True)
        dxh = dvn * g_ref[...]
        dvg = r * (dxh - jnp.mean(dxh, axis=-1, keepdims=True) - xh * jnp.mean(dxh * xh, axis=-1, keepdims=True))
        dv_ref[...] = (dvg * _gelu_grad(vv)).astype(dv_ref.dtype)

    wspec = _full((SGU_GROUPS, CHUNK, CHUNK))
    return pl.pallas_call(body, name=name, grid=(T // tr,),
                          in_specs=[_rowc(tr, C, 0), _rowc(tr, C, 1), _full((1, C)), _full((1, C)), wspec, wspec,
                                    _full((CHUNK, SGU_GROUPS)), _rowc(tr, C, 1)],
                          out_specs=[_rowc(tr, 2 * C, 0), _full((1, C)), _full((1, C)), wspec,
                                     _full((CHUNK, SGU_GROUPS))],
                          out_shape=[SDS((T, out_cols), MXU_DTYPE), SDS((1, C), F32), SDS((1, C), F32),
                                     SDS((SGU_GROUPS, CHUNK, CHUNK), F32), SDS((CHUNK, SGU_GROUPS), F32)],
                          scratch_shapes=[pltpu.VMEM((tr, C), F32)],
                          compiler_params=_cp("arbitrary"))(z, z, ln_g, ln_b, ws, wst, bst, dycat)


def adamw(w, g, m, v, *, name):
    R, C = w.shape
    tr = _pick(R, 512)
    c1 = 1.0 - ADAM_B1 ** ADAM_STEP
    c2 = 1.0 - ADAM_B2 ** ADAM_STEP

    def body(w_ref, g_ref, m_ref, v_ref, d_ref, nm_ref, nv_ref):
        gv = g_ref[...]
        nm = ADAM_B1 * m_ref[...] + (1.0 - ADAM_B1) * gv
        nv = ADAM_B2 * v_ref[...] + (1.0 - ADAM_B2) * (gv * gv)
        nm_ref[...] = nm
        nv_ref[...] = nv
        d_ref[...] = -ADAM_LR * ((nm / c1) / (jnp.sqrt(nv / c2) + ADAM_EPS) + ADAM_WD * w_ref[...])

    spec = _row(tr, C)
    return pl.pallas_call(body, name=name, grid=(R // tr,), in_specs=[spec] * 4, out_specs=[spec] * 3,
                          out_shape=[SDS((R, C), F32)] * 3, compiler_params=_cp("parallel"))(w, g, m, v)


ANY = pl.BlockSpec(memory_space=pl.ANY)


def _coords():
    return lax.axis_index("x"), lax.axis_index("y"), lax.axis_index("c")


def _other_chips(x, y):
    return [(1 - x, y), (x, 1 - y), (1 - x, 1 - y)]


def _remote(src, dst, send_sems, recv_sems, k, dev):
    return pltpu.make_async_remote_copy(src_ref=src, dst_ref=dst, send_sem=send_sems.at[k], recv_sem=recv_sems.at[k],
                                        device_id=dev, device_id_type=MESH)


LOCAL_CHUNKS = 8


class Exchange:
    def __init__(self, ins, out_shapes, scratch, start, wait):
        self.ins, self.out_shapes, self.scratch, self.start, self.wait = list(ins), list(out_shapes), list(scratch), start, wait


def run_exchange(ex, *, name):
    ni, no = len(ex.ins), len(ex.out_shapes)

    def body(*refs):
        parts = refs[:ni], refs[ni:ni + no], refs[ni + no:]
        ex.start(*parts)
        ex.wait(*parts)

    return pl.pallas_call(body, name=name, in_specs=[ANY] * ni, out_specs=[ANY] * no, out_shape=ex.out_shapes,
                          scratch_shapes=ex.scratch)(*ex.ins)


def call_hosting(body, ex, *, name, grid, in_specs, out_specs, out_shape, inputs, aliases, scratch=()):
    n_in, n_out, ni, no, ns = len(inputs), len(out_shape), len(ex.ins), len(ex.out_shapes), len(scratch)
    outs_at = n_in + ni
    scr_at = outs_at + n_out + no

    def wrapped(*refs):
        own = refs[:n_in] + refs[outs_at:outs_at + n_out] + refs[scr_at:scr_at + ns]
        parts = refs[n_in:outs_at], refs[outs_at + n_out:scr_at], refs[scr_at + ns:]
        ids = [pl.program_id(d) for d in range(len(grid))]
        first = functools.reduce(jnp.logical_and, [i == 0 for i in ids])
        last = functools.reduce(jnp.logical_and, [i == g - 1 for i, g in zip(ids, grid)])

        @pl.when(first)
        def _():
            ex.start(*parts)

        body(*own)

        @pl.when(last)
        def _():
            ex.wait(*parts)

    outs = pl.pallas_call(
        wrapped, name=name, grid=grid, in_specs=list(in_specs) + [ANY] * ni, out_specs=list(out_specs) + [ANY] * no,
        out_shape=list(out_shape) + ex.out_shapes, input_output_aliases=aliases,
        scratch_shapes=list(scratch) + ex.scratch,
        compiler_params=_cp(*["arbitrary"] * len(grid)))(*inputs, *ex.ins)
    return outs[:n_out], outs[n_out:]


def allgather_ici_exchange(shards, whole=()):
    na, n_all = len(shards), len(shards) + len(whole)
    arrays = list(shards) + list(whole)

    def copies(s_refs, o_refs, sems):
        send_sems, recv_sems, _ = sems
        x, y, c = _coords()
        j = 2 * x + y
        out = []
        for a in range(n_all):
            if a < na:
                half = shards[a].shape[0] // 2
                part = (pl.ds(c * half, half),)
            else:
                part = ()
            for k, (px, py) in enumerate(_other_chips(x, y)):
                send = _remote(s_refs[a].at[part] if part else s_refs[a], o_refs[a].at[(j,) + part], send_sems, recv_sems,
                               3 * a + k, (px, py, c))
                rows = o_refs[a].at[(2 * px + py,) + part]
                out.append((send, _remote(rows, rows, send_sems, recv_sems, 3 * a + k, (px, py, c))))
        return out

    def start(s_refs, o_refs, sems):
        x, y, c = _coords()
        j = 2 * x + y
        for a in range(n_all):
            chunks = LOCAL_CHUNKS if a < na else 1
            chunk = arrays[a].shape[0] // chunks
            for q in range(chunks):
                rows = pl.ds(q * chunk, chunk)
                pltpu.make_async_copy(s_refs[a].at[rows], o_refs[a].at[j, rows], sems[2].at[a]).start()
        for send, _ in copies(s_refs, o_refs, sems):
            send.start()

    def wait(s_refs, o_refs, sems):
        x, y, c = _coords()
        j = 2 * x + y
        for send, arrival in copies(s_refs, o_refs, sems):
            arrival.wait_recv()
            send.wait_send()
        for a in range(n_all):
            pltpu.make_async_copy(s_refs[a], o_refs[a].at[j], sems[2].at[a]).wait()

    return Exchange(arrays, [SDS((N_CHIPS,) + s.shape, s.dtype) for s in arrays],
                    [pltpu.SemaphoreType.DMA((3 * n_all,)), pltpu.SemaphoreType.DMA((3 * n_all,)),
                     pltpu.SemaphoreType.DMA((n_all,))], start, wait)


def allgather_forward(gathered, *, name):
    na = len(gathered)

    def body(*refs):
        o_refs = refs[na:2 * na]
        send_sems, recv_sems = refs[2 * na:]
        x, y, c = _coords()
        sibling = (x, y, 1 - c)
        cps = []
        for a in range(na):
            half = gathered[a].shape[1] // 2
            for k, (px, py) in enumerate(_other_chips(x, y)):
                mine = o_refs[a].at[2 * px + py, pl.ds(c * half, half)]
                theirs = o_refs[a].at[2 * px + py, pl.ds((1 - c) * half, half)]
                cps.append((_remote(mine, mine, send_sems, recv_sems, 3 * a + k, sibling),
                            _remote(theirs, theirs, send_sems, recv_sems, 3 * a + k, sibling)))
        for send, _ in cps:
            send.start()
        for send, arrival in cps:
            send.wait_send()
            arrival.wait_recv()

    return pl.pallas_call(body, name=name, in_specs=[ANY] * na, out_specs=[ANY] * na,
                          out_shape=[SDS(g.shape, g.dtype) for g in gathered],
                          input_output_aliases={a: a for a in range(na)},
                          scratch_shapes=[pltpu.SemaphoreType.DMA((3 * na,)), pltpu.SemaphoreType.DMA((3 * na,))])(*gathered)


def swap_halves_exchange(gs):
    na = len(gs)

    def copies(g_refs, o_refs, sems):
        x, y, c = _coords()
        out = []
        for a in range(na):
            half = gs[a].shape[1] // 2
            out.append(_remote(g_refs[a].at[:, pl.ds((1 - c) * half, half), :], o_refs[a], sems[0], sems[1], a,
                               (x, y, 1 - c)))
        return out

    def start(g_refs, o_refs, sems):
        for cp in copies(g_refs, o_refs, sems):
            cp.start()

    def wait(g_refs, o_refs, sems):
        for cp in copies(g_refs, o_refs, sems):
            cp.wait()

    return Exchange(gs, [SDS((g.shape[0], g.shape[1] // 2, g.shape[2]), g.dtype) for g in gs],
                    [pltpu.SemaphoreType.DMA((na,)), pltpu.SemaphoreType.DMA((na,))], start, wait)


def chip_partials_exchange(pbs):
    na = len(pbs)

    def copies(p_refs, o_refs, sems):
        x, y, c = _coords()
        out = []
        for a in range(na):
            for k, (px, py) in enumerate(_other_chips(x, y)):
                out.append(_remote(p_refs[a].at[2 * px + py], o_refs[a].at[k], sems[0], sems[1], 3 * a + k, (px, py, c)))
        return out

    def start(p_refs, o_refs, sems):
        for cp in copies(p_refs, o_refs, sems):
            cp.start()

    def wait(p_refs, o_refs, sems):
        for cp in copies(p_refs, o_refs, sems):
            cp.wait()

    return Exchange(pbs, [SDS((3,) + p.shape[1:], p.dtype) for p in pbs],
                    [pltpu.SemaphoreType.DMA((3 * na,)), pltpu.SemaphoreType.DMA((3 * na,))], start, wait)


def add_sibling_half(g, land, c_idx, *, name):
    n, R, C = g.shape
    half = R // 2
    tr = _pick(half, 256)
    nt = half // tr

    def body(c_ref, g_ref, l_ref, of_ref, ob_ref):
        s = g_ref[...] + l_ref[...].astype(F32)
        of_ref[...] = s
        ob_ref[...] = s.astype(ob_ref.dtype)

    blk = pl.BlockSpec((1, tr, C), lambda s, i, c_ref: (s, i, 0))
    gblk = pl.BlockSpec((1, tr, C), lambda s, i, c_ref: (s, c_ref[0] * nt + i, 0))
    return pl.pallas_call(
        body, name=name,
        grid_spec=pltpu.PrefetchScalarGridSpec(num_scalar_prefetch=1, grid=(n, nt), in_specs=[gblk, blk],
                                               out_specs=[blk, blk]),
        out_shape=[SDS((n, half, C), F32), SDS((n, half, C), WIRE_DTYPE)],
        compiler_params=_cp("parallel", "parallel"))(c_idx, g, land)


def add_chip_partials(pf, rb, jc_idx, *, name):
    n, H, C = pf.shape
    tr = _pick(H, 256)

    def body(jc_ref, p_ref, r_ref, o_ref):
        s = p_ref[0]
        for k in range(3):
            s = s + r_ref[k].astype(F32)
        o_ref[...] = s

    pblk = pl.BlockSpec((1, tr, C), lambda i, jc_ref: (jc_ref[0], i, 0))
    rblk = pl.BlockSpec((3, tr, C), lambda i, jc_ref: (0, i, 0))
    oblk = pl.BlockSpec((None, tr, C), lambda i, jc_ref: (jc_ref[1], i, 0))
    return pl.pallas_call(
        body, name=name,
        grid_spec=pltpu.PrefetchScalarGridSpec(num_scalar_prefetch=1, grid=(H // tr,), in_specs=[pblk, rblk],
                                               out_specs=oblk),
        out_shape=SDS((2, H, C), F32), compiler_params=_cp("parallel"))(jc_idx, pf, rb)


def join_sibling_halves(bufs, *, name):
    na = len(bufs)

    def body(*refs):
        o_refs = refs[na:2 * na]
        send_sems, recv_sems = refs[2 * na:]
        x, y, c = _coords()
        cps = [_remote(o_refs[a].at[c], o_refs[a].at[c], send_sems, recv_sems, a, (x, y, 1 - c)) for a in range(na)]
        for cp in cps:
            cp.start()
        for a in range(na):
            cps[a].wait_send()
            _remote(o_refs[a].at[1 - c], o_refs[a].at[1 - c], send_sems, recv_sems, a, (x, y, 1 - c)).wait_recv()

    return pl.pallas_call(body, name=name, in_specs=[ANY] * na, out_specs=[ANY] * na,
                          out_shape=[SDS(b.shape, b.dtype) for b in bufs],
                          input_output_aliases={a: a for a in range(na)},
                          scratch_shapes=[pltpu.SemaphoreType.DMA((na,)), pltpu.SemaphoreType.DMA((na,))])(*bufs)


def allreduce_pieces(v, *, name):
    n, P, C = v.shape
    assert n == N_DEV

    def body(v_ref, o_ref, land_ref, mine_ref, send1, recv1, send2, recv2):
        x, y, c = _coords()
        me = 4 * x + 2 * y + c
        peers = []
        for m in range(1, N_DEV):
            px = (1 - x) if m & 4 else x
            py = (1 - y) if m & 2 else y
            pc = (1 - c) if m & 1 else c
            peers.append((m - 1, 4 * px + 2 * py + pc, (px, py, pc)))
        scatter = [(_remote(v_ref.at[lin], land_ref.at[me], send1, recv1, k, dev),
                    _remote(land_ref.at[lin], land_ref.at[lin], send1, recv1, k, dev)) for k, lin, dev in peers]
        for send, _ in scatter:
            send.start()
        land_ref[pl.ds(me, 1)] = v_ref[pl.ds(me, 1)]
        for _, arrival in scatter:
            arrival.wait_recv()
        s = land_ref[0]
        for d in range(1, N_DEV):
            s = s + land_ref[d]
        mine_ref[...] = s
        gather = [(_remote(mine_ref, o_ref.at[me], send2, recv2, k, dev),
                   _remote(o_ref.at[lin], o_ref.at[lin], send2, recv2, k, dev)) for k, lin, dev in peers]
        for send, _ in gather:
            send.start()
        o_ref[pl.ds(me, 1)] = s[None]
        for send, arrival in scatter + gather:
            send.wait_send()
        for _, arrival in gather:
            arrival.wait_recv()

    vmem = pl.BlockSpec(memory_space=pltpu.VMEM)
    sems = [pltpu.SemaphoreType.DMA((N_DEV - 1,))] * 4
    return pl.pallas_call(body, name=name, in_specs=[vmem], out_specs=vmem, out_shape=SDS((n, P, C), F32),
                          scratch_shapes=[pltpu.VMEM((n, P, C), F32), pltpu.VMEM((P, C), F32)] + sems)(v)


BIG_SEGS = (
    ("w_in_even", (1024, 514), 1),
    ("s5_w_glu", (128, 512), 0),
    ("w_out_even", (256, 1024), 0),
    ("w_in_odd", (1024, 384), 1),
    ("w_out_odd", (256, 1024), 0),
    ("mlp_w1", (2, 1024, 1024), 2),
    ("mlp_w2", (2, 1024, 1024), 1),
)
BIG_NAMES = tuple(n for n, _, _ in BIG_SEGS)
EARLY_NAMES = ("w_in_even", "s5_w_glu")
LATE_NAMES = ("w_out_even", "w_in_odd", "w_out_odd", "mlp_w1", "mlp_w2")
REDUCED_EARLY = ("s5_w_glu", "w_out_even", "w_in_odd", "w_out_odd", "mlp_w1", "mlp_w2")
SHARDED_SMALL = ("pool_scale", "sgu_ln_g", "sgu_ln_b")
SMALL_SEGS = (
    ("mix_pre_g", (2, 1024)), ("mix_post_g", (2, 1024)), ("mlp_pre_g", (2, 1024)), ("mlp_post_g", (2, 1024)),
    ("s5_lam_re", (1, 32, 64)), ("s5_lam_im", (1, 32, 64)), ("s5_log_dt", (1, 32)),
    ("s5_b_re", (1, 32, 64, 16)), ("s5_b_im", (1, 32, 64, 16)), ("s5_c_re", (1, 32, 16, 64)), ("s5_c_im", (1, 32, 16, 64)),
    ("s5_d", (1, 512)), ("fox_b_f", (1, 8)), ("pool_w", (1, 4, 128, 128)), ("sgu_w_s", (1, 4, 128, 128)),
    ("sgu_b_s", (1, 4, 128)),
)
REDUCED_SEGS = SMALL_SEGS + tuple((n, (1, 512)) for n in SHARDED_SMALL) + (("loss", (1, 1)),)


def _cols_from_chips(g):
    n, R, C = g.shape
    return jnp.transpose(g, (1, 0, 2)).reshape(R, n * C)


def _chips_from_cols(m):
    R, C4 = m.shape
    return jnp.transpose(m.reshape(R, N_CHIPS, C4 // N_CHIPS), (1, 0, 2))


MLP_SHARD = 1024


def _w1_cols(l):
    def spec(tm, tn, tk):
        per = MLP_SHARD // tn
        return pl.BlockSpec((None, tk, tn), lambda i, j, k: (j // per, l * (MLP_SHARD // tk) + k, j % per))
    return spec


def _w1_rows_t(l):
    def spec(tm, tn, tk):
        if tk == N_CHIPS * MLP_SHARD:
            return pl.BlockSpec((N_CHIPS, tn, MLP_SHARD), lambda i, j, k: (0, l * (MLP_SHARD // tn) + j, 0))
        per = MLP_SHARD // tk
        return pl.BlockSpec((None, tn, tk), lambda i, j, k: (k // per, l * (MLP_SHARD // tn) + j, k % per))
    return spec


def _w2_rows(l):
    def spec(tm, tn, tk):
        if tk == N_CHIPS * MLP_SHARD:
            return pl.BlockSpec((N_CHIPS, MLP_SHARD, tn), lambda i, j, k: (0, l, j))
        per = MLP_SHARD // tk
        return pl.BlockSpec((None, tk, tn), lambda i, j, k: (k // per, l * per + k % per, j))
    return spec


def _w2_rows_t(l):
    def spec(tm, tn, tk):
        per = MLP_SHARD // tn
        return pl.BlockSpec((None, tn, tk), lambda i, j, k: (j // per, l * per + j % per, k))
    return spec


def _dw1_out(l):
    def spec(tm, tn, tk):
        per = MLP_SHARD // tn
        return pl.BlockSpec((None, tm, tn), lambda i, j, k: (j // per, l * (MLP_SHARD // tm) + i, j % per))
    return spec


def _dw2_out(l):
    def spec(tm, tn, tk):
        per = MLP_SHARD // tm
        return pl.BlockSpec((None, tm, tn), lambda i, j, k: (i // per, l * per + i % per, j))
    return spec


def _pack_vec(d, segs, rows_multiple):
    flat = jnp.concatenate([d[n].reshape(-1) for n, _ in segs])
    rows = -(-flat.shape[0] // LANES)
    rows = -(-rows // rows_multiple) * rows_multiple
    return jnp.pad(flat, (0, rows * LANES - flat.shape[0])).reshape(rows, LANES)


def _unpack_vec(v, segs):
    flat, out, r = v.reshape(-1), {}, 0
    for n, shape in segs:
        k = math.prod(shape)
        out[n] = flat[r:r + k].reshape(shape)
        r += k
    return out


def _block_diag(blocks):
    G, a, b = blocks.shape
    eye = jnp.eye(G, dtype=blocks.dtype)
    return (eye[:, None, :, None] * blocks[:, :, None, :]).reshape(G * a, G * b)


def _diag_blocks(m, G):
    a, b = m.shape[0] // G, m.shape[1] // G
    return jnp.stack([m[g * a:(g + 1) * a, g * b:(g + 1) * b] for g in range(G)])


def _sqrelu_epi(acc):
    r = jnp.maximum(acc, 0.0)
    return acc, r * r


def _sqrelu_bwd_epi(acc, a):
    return (acc * (2.0 * jnp.maximum(a.astype(F32), 0.0)),)


def _mlp_fwd(h, g1, g2, l, tag):
    T, D = h.shape
    a, s = matmul(h, g1, name=f"{tag}_up", mnk=(T, D_FF, D), b_spec=_w1_cols(l), epi=_sqrelu_epi,
                  out_dtypes=(MXU_DTYPE, MXU_DTYPE))
    m = matmul(s, g2, name=f"{tag}_down", mnk=(T, D, D_FF), b_spec=_w2_rows(l))
    return m, (h, a, s)


def _mlp_bwd(saved, dm, g1, g2, l, dg1, dg2, tag):
    h, a, s = saved
    T, D = h.shape
    gshape = (N_CHIPS, 2 * MLP_SHARD, MLP_SHARD)
    da = matmul(dm, g2, tb=True, name=f"{tag}_down_dx", mnk=(T, D_FF, D), b_spec=_w2_rows_t(l),
                epi=_sqrelu_bwd_epi, epi_in=(a,), out_dtype=MXU_DTYPE)
    dg2 = matmul(s, dm, ta=True, name=f"{tag}_down_dw", tm=MLP_SHARD, o_spec=_dw2_out(l), o_shape=gshape, prev=dg2)
    dh = matmul(da, g1, tb=True, name=f"{tag}_up_dx", mnk=(T, D, D_FF), b_spec=_w1_rows_t(l))
    dg1 = matmul(h, da, ta=True, name=f"{tag}_up_dw", o_spec=_dw1_out(l), o_shape=gshape, prev=dg1)
    return dh, dg1, dg2


def kernel(x, mix_pre_g, mix_post_g, mlp_pre_g, mlp_post_g, w_in_even, s5_lam_re, s5_lam_im, s5_log_dt, s5_b_re, s5_b_im, s5_c_re, s5_c_im, s5_d, s5_w_glu, fox_b_f, w_out_even, w_in_odd, pool_w, pool_scale, sgu_ln_g, sgu_ln_b, sgu_w_s, sgu_b_s, w_out_odd, mlp_w1, mlp_w2, loss_target, m_mix_pre_g, m_mix_post_g, m_mlp_pre_g, m_mlp_post_g, m_w_in_even, m_s5_lam_re, m_s5_lam_im, m_s5_log_dt, m_s5_b_re, m_s5_b_im, m_s5_c_re, m_s5_c_im, m_s5_d, m_s5_w_glu, m_fox_b_f, m_w_out_even, m_w_in_odd, m_pool_w, m_pool_scale, m_sgu_ln_g, m_sgu_ln_b, m_sgu_w_s, m_sgu_b_s, m_w_out_odd, m_mlp_w1, m_mlp_w2, v_mix_pre_g, v_mix_post_g, v_mlp_pre_g, v_mlp_post_g, v_w_in_even, v_s5_lam_re, v_s5_lam_im, v_s5_log_dt, v_s5_b_re, v_s5_b_im, v_s5_c_re, v_s5_c_im, v_s5_d, v_s5_w_glu, v_fox_b_f, v_w_out_even, v_w_in_odd, v_pool_w, v_pool_scale, v_sgu_ln_g, v_sgu_ln_b, v_sgu_w_s, v_sgu_b_s, v_w_out_odd, v_mlp_w1, v_mlp_w2):
    names = [n for n, _ in SMALL_SEGS] + [n for n, _, _ in BIG_SEGS] + list(SHARDED_SMALL)
    env = dict(locals())
    W = {n: env[n] for n in names}
    M = {n: env["m_" + n] for n in names}
    V = {n: env["v_" + n] for n in names}

    def shard(n):
        return W[n].reshape(-1, W[n].shape[-1]).astype(WIRE_DTYPE)

    small = jnp.pad(jnp.concatenate([W[n] for n in SHARDED_SMALL]), ((0, SUBLANES - len(SHARDED_SMALL)), (0, 0)))
    loss8, dx0, halves, local_small = _local_step(x[0], loss_target[0], {n: W[n] for n, _ in SMALL_SEGS},
                                                  [shard(n) for n in EARLY_NAMES], [shard(n) for n in LATE_NAMES], small)
    return _reduce_and_update(W, M, V, loss8, dx0, halves, local_small)


def _reduce_to_my_half(gs, names, tag, carry_swap=None, carry_ici=None):
    cx, cy, cc = _coords()
    c_idx = cc.reshape(1).astype(jnp.int32)
    jc_idx = jnp.stack([2 * cx + cy, cc]).astype(jnp.int32)
    swap = swap_halves_exchange(gs)
    from_sibling = carry_swap(swap) if carry_swap else run_exchange(swap, name=f"{tag}_to_sibling")
    sums = [add_sibling_half(g, l, c_idx, name=f"{tag}_chip_sum_{n}") for n, g, l in zip(names, gs, from_sibling)]
    send = chip_partials_exchange([pb for _, pb in sums])
    from_chips = carry_ici(send) if carry_ici else run_exchange(send, name=f"{tag}_to_chips")
    return [add_chip_partials(pf, r, jc_idx, name=f"{tag}_sum_{n}") for n, (pf, _), r in zip(names, sums, from_chips)]


def _local_step(x0, target, P, early_shards, late_shards, small_shard):
    T = x0.shape[0]
    mix_pre_g, mix_post_g, mlp_pre_g, mlp_post_g = P["mix_pre_g"], P["mix_post_g"], P["mlp_pre_g"], P["mlp_post_g"]
    s5_lam_re, s5_lam_im, s5_log_dt = P["s5_lam_re"], P["s5_lam_im"], P["s5_log_dt"]
    s5_b_re, s5_b_im, s5_c_re, s5_c_im, s5_d = P["s5_b_re"], P["s5_b_im"], P["s5_c_re"], P["s5_c_im"], P["s5_d"]
    fox_b_f, pool_w, sgu_w_s, sgu_b_s = P["fox_b_f"], P["pool_w"], P["sgu_w_s"], P["sgu_b_s"]

    def gain(a, l):
        return a[l][None, :]

    lr = s5_lam_re[0].reshape(1, S5_LANES)
    li = s5_lam_im[0].reshape(1, S5_LANES)
    ldt = jnp.repeat(s5_log_dt[0], S5_STATE).reshape(1, S5_LANES)
    btr = s5_b_re[0].reshape(S5_LANES, S5_GROUP).T
    bti = s5_b_im[0].reshape(S5_LANES, S5_GROUP).T
    tf_re, tf_im, tb_re, tb_im, bbt_re, bbt_im = s5_disc_fwd(lr, li, ldt, btr, bti, name="s5_disc")
    same_group = (jnp.arange(S5_WIDTH)[:, None] // S5_GROUP) == (jnp.arange(S5_LANES)[None, :] // S5_STATE)
    b_bd = s5_interleave(jnp.where(same_group, jnp.tile(bbt_re, (S5_GROUPS, 1)), 0.0),
                         jnp.where(same_group, jnp.tile(bbt_im, (S5_GROUPS, 1)), 0.0), axis=1)
    cr2 = jnp.transpose(s5_c_re[0], (0, 2, 1)).reshape(S5_LANES, S5_GROUP)
    ci2 = jnp.transpose(s5_c_im[0], (0, 2, 1)).reshape(S5_LANES, S5_GROUP)
    c_bd = s5_interleave(jnp.where(same_group.T, jnp.tile(cr2, (1, S5_GROUPS)), 0.0),
                         -jnp.where(same_group.T, jnp.tile(ci2, (1, S5_GROUPS)), 0.0), axis=0)
    bf_pad = jnp.pad(fox_b_f, ((0, 0), (0, LANES - FOX_HEADS)))

    h1, early = rms_fwd(x0, gain(mix_pre_g, 0), allgather_ici_exchange(early_shards), name="l0_pre_norm")
    early = dict(zip(EARLY_NAMES, allgather_forward(early, name="allgather_early_weights")))
    w_in_e = jnp.pad(_cols_from_chips(early["w_in_even"]), ((0, 0), (0, EVEN_IN_PAD - EVEN_IN)))
    w_glu = early["s5_w_glu"].reshape(S5_WIDTH, S5_WIDTH)
    z = matmul(h1, w_in_e, name="l0_in_proj")
    s5_tiles = dict(tm=_pick(T, S5_NB), exact_tiles=True)
    xs = s5_scan(z, b_bd, tf_re, tf_im, reverse=False, name="s5_scan_fwd")
    yc = matmul(xs, c_bd, mnk=(T, S5_WIDTH, 2 * S5_NB), tn=S5_CB, a_spec=_lanes_of_chan, b_spec=_s5_c_block,
                name="s5_cx", **s5_tiles)
    yl, yg = s5_out_fwd(yc, z, s5_d, name="s5_out")
    gl = matmul(yg, w_glu, name="s5_glu_proj")
    ycat = glu_fwd(yg, gl, out_cols=D_MODEL, name="s5_glu")
    fgate = fox_gate_fwd(z, bf_pad, fl_col=FL_TILE, name="fox_gate")
    f_col = _pairs_col(fgate, T)
    f_row = _col_to_row(f_col, T)
    (ycat, lse_col), late = fox_fwd(z, f_col, f_row, ycat, allgather_ici_exchange(late_shards, [small_shard]),
                                    name="fox_fwd")
    small_all = late[-1]
    pool_scale_f, ln_g_f, ln_b_f = (small_all[:, i, :].reshape(1, N_CHIPS * LANES) for i in range(len(SHARDED_SMALL)))
    late = dict(zip(LATE_NAMES, allgather_forward(late[:-1], name="allgather_late_weights")))
    w_in_o = _cols_from_chips(late["w_in_odd"])
    w_in_o = jnp.concatenate([w_in_o[:, S5_WIDTH:], w_in_o[:, :S5_WIDTH]], axis=1)
    w_out_e = late["w_out_even"].reshape(D_MODEL, D_MODEL)
    w_out_o = late["w_out_odd"].reshape(D_MODEL, D_MODEL)
    g1, g2 = late["mlp_w1"], late["mlp_w2"]
    mo = matmul(ycat, w_out_e, name="l0_out_proj")
    x1, h2 = res_norm_fwd(x0, mo, gain(mix_post_g, 0), gain(mlp_pre_g, 0), name="l0_post_mlp0_pre_norm")
    m0, mlp0 = _mlp_fwd(h2, g1, g2, 0, "mlp0")

    x2, h3 = res_norm_fwd(x1, m0, gain(mlp_post_g, 0), gain(mix_pre_g, 1), name="mlp0_post_l1_pre_norm")
    z2 = matmul(h3, w_in_o, name="l1_in_proj")
    pooled = pool_window(z2, adjoint=False, in_col=POOL_COL, out_dtype=MXU_DTYPE, name="pool_fwd")
    pw_bd = _block_diag(pool_w[0])
    pw = matmul(pooled, pw_bd, name="pool_proj")
    ycat2 = colscale_fwd(pw, pool_scale_f, out_cols=D_MODEL, name="pool_scale")
    causal = jnp.tril(jnp.ones((CHUNK, CHUNK), dtype=bool))
    wsm = jnp.where(causal[None], sgu_w_s[0], 0.0)
    wsmt = jnp.transpose(wsm, (0, 2, 1))
    bst = sgu_b_s[0].T
    ycat2 = sgu_fwd(z2, ln_g_f, ln_b_f, wsm, bst, ycat2, name="sgu_fwd")
    mo2 = matmul(ycat2, w_out_o, name="l1_out_proj")
    x3, h4 = res_norm_fwd(x2, mo2, gain(mix_post_g, 1), gain(mlp_pre_g, 1), name="l1_post_mlp1_pre_norm")
    m1, mlp1 = _mlp_fwd(h4, g1, g2, 1, "mlp1")
    loss8, dx4 = res_norm_loss(x3, m1, gain(mlp_post_g, 1), target, name="mlp1_post_norm_loss")

    dm1, dg_mlp_post1 = rms_bwd(m1, gain(mlp_post_g, 1), dx4, None, name="mlp1_post_norm_bwd")
    dh4, dg1, dg2 = _mlp_bwd(mlp1, dm1, g1, g2, 1, None, None, "mlp1")
    dx3, dmo2, dg_mlp_pre1, dg_mix_post1 = norm_res_bwd(x3, gain(mlp_pre_g, 1), dh4, dx4, mo2, gain(mix_post_g, 1),
                                                        name="mlp1_pre_l1_post_norm_bwd")
    dycat2 = matmul(dmo2, w_out_o, tb=True, name="l1_out_proj_dx")
    dw_out_o = matmul(ycat2, dmo2, ta=True, name="l1_out_proj_dw")
    dpw, dpool_scale = colscale_bwd(pw, pool_scale_f, dycat2, name="pool_scale_bwd")
    dpooled = matmul(dpw, pw_bd, tb=True, name="pool_proj_dx")
    dpw_bd = matmul(pooled, dpw, ta=True, name="pool_proj_dw")
    dz2, dln_g, dln_b, dws, dbst = sgu_bwd(z2, ln_g_f, ln_b_f, wsm, wsmt, bst, dycat2, out_cols=3 * S5_WIDTH,
                                           name="sgu_bwd")
    dz2 = pool_window(dpooled, adjoint=True, into=dz2, out_col=POOL_COL, name="pool_bwd")
    dh3 = matmul(dz2, w_in_o, tb=True, name="l1_in_proj_dx")
    dw_in_o = matmul(h3, dz2, ta=True, name="l1_in_proj_dw")
    dw_in_o = jnp.concatenate([dw_in_o[:, 2 * S5_WIDTH:], dw_in_o[:, :2 * S5_WIDTH]], axis=1)
    dx2, dm0, dg_mix_pre1, dg_mlp_post0 = norm_res_bwd(x2, gain(mix_pre_g, 1), dh3, dx3, m0, gain(mlp_post_g, 0),
                                                       name="l1_pre_mlp0_post_norm_bwd")

    dh2, dg1, dg2 = _mlp_bwd(mlp0, dm0, g1, g2, 0, dg1, dg2, "mlp0")
    dx1, dmo, dg_mlp_pre0, dg_mix_post0 = norm_res_bwd(x1, gain(mlp_pre_g, 0), dh2, dx2, mo, gain(mix_post_g, 0),
                                                       name="mlp0_pre_l0_post_norm_bwd")
    dycat = matmul(dmo, w_out_e, tb=True, name="l0_out_proj_dx")
    dw_out_e = matmul(ycat, dmo, ta=True, name="l0_out_proj_dw")
    dyg_a, dgl = glu_bwd(yg, gl, dycat, name="s5_glu_bwd")
    dyg_b = matmul(dgl, w_glu, tb=True, name="s5_glu_proj_dx")
    dw_glu = matmul(yg, dgl, ta=True, name="s5_glu_proj_dw")
    dyl, du_skip, dd = s5_out_bwd(yl, z, s5_d, dyg_a, dyg_b, name="s5_out_bwd")
    dc_blocks = matmul(xs, dyl, ta=True, mnk=(2 * S5_LANES, S5_CB, T), tm=S5_NB, tn=S5_CB, b_spec=_chan_cols_of_i,
                       exact_tiles=True, name="s5_cx_dw")
    early_grads = {"s5_w_glu": dw_glu.reshape(N_CHIPS, -1, S5_WIDTH), "w_out_even": dw_out_e.reshape(N_CHIPS, -1, D_MODEL),
                   "w_in_odd": _chips_from_cols(dw_in_o), "w_out_odd": dw_out_o.reshape(N_CHIPS, -1, D_MODEL),
                   "mlp_w1": dg1, "mlp_w2": dg2}
    got = {}

    def reverse_scan(exchange):
        (got["lam"], got["dab_re"], got["dab_im"]), bufs = s5_scan(dyl, c_bd, tb_re, tb_im, reverse=True, states=xs,
                                                                   hosted=exchange, name="s5_scan_bwd")
        return bufs

    def attention_bwd(exchange):
        dd_col = fox_dd(ycat, dycat, name="fox_dd")
        (got["dk"], got["dv"], got["dfk"], got["dqt"], got["dfq"]), bufs = fox_bwd(
            z, dycat, f_col, f_row, _col_to_row(lse_col, T), _col_to_row(dd_col, T), exchange, name="fox_bwd")
        return bufs

    halves = _reduce_to_my_half([early_grads[n] for n in REDUCED_EARLY], REDUCED_EARLY, "early_grads",
                                reverse_scan, attention_bwd)
    lam, dab_re, dab_im, dk, dv = got["lam"], got["dab_re"], got["dab_im"], got["dk"], got["dv"]
    db_blocks = matmul(z, lam, ta=True, mnk=(S5_CB, 2 * S5_LANES, T), tm=S5_CB, tn=S5_NB, a_spec=_chan_rows_t,
                       exact_tiles=True, name="s5_bu_dw")
    du_b = matmul(lam, b_bd, tb=True, mnk=(T, S5_WIDTH, 2 * S5_NB), tn=S5_CB, a_spec=_lanes_of_chan,
                  b_spec=_s5_b_block_t, name="s5_bu_dx", **s5_tiles)
    du = add2(du_skip, du_b, name="s5_du")
    dq = jnp.transpose(got["dqt"], (1, 3, 0, 2)).reshape(T, FOX_WIDTH) * (FOX_HEAD_DIM ** -0.5)
    dfl, dbf = fox_gate_bwd(z, bf_pad, _pairs_to_lanes(got["dfk"], T), _pairs_to_lanes(_row_to_col(got["dfq"], T), T),
                            fl_col=FL_TILE, name="fox_gate_bwd")
    dz = jnp.concatenate([du, dq, dk, dv, dfl], axis=1).astype(MXU_DTYPE)
    dw_in_e = matmul(h1, dz, ta=True, name="l0_in_proj_dw")[:, :EVEN_IN]

    def in_proj_dx(exchange):
        got["dh1"], bufs = matmul(dz, w_in_e, tb=True, hosted=exchange, name="l0_in_proj_dx")
        return bufs

    def pre_norm_bwd(exchange):
        (got["dx0"], got["dg_mix_pre0"]), bufs = rms_bwd(x0, gain(mix_pre_g, 0), got["dh1"], dx1, hosted=exchange,
                                                         name="l0_pre_norm_bwd")
        return bufs

    halves = halves + _reduce_to_my_half([_chips_from_cols(dw_in_e)], ["w_in_even"], "late_grads", in_proj_dx, pre_norm_bwd)
    dx0, dg_mix_pre0 = got["dx0"], got["dg_mix_pre0"]

    groups_per_block = S5_CB // S5_GROUP
    own_group = (jnp.arange(S5_CB)[:, None] // S5_GROUP) == ((jnp.arange(S5_LANES)[None, :] // S5_STATE) % groups_per_block)
    db_re, db_im = s5_deinterleave(db_blocks, axis=1)
    dbbt_re = jnp.where(own_group, db_re, 0.0).reshape(groups_per_block, S5_GROUP, S5_LANES).sum(0)
    dbbt_im = jnp.where(own_group, db_im, 0.0).reshape(groups_per_block, S5_GROUP, S5_LANES).sum(0)
    dlr, dli, dldt8, dbtr, dbti = s5_disc_bwd(lr, li, ldt, btr, bti, dab_re, dab_im, dbbt_re, dbbt_im, name="s5_disc_bwd")
    dc_re, dc_im = s5_deinterleave(dc_blocks, axis=0)
    dcr2 = jnp.where(own_group.T, dc_re, 0.0).reshape(S5_LANES, groups_per_block, S5_GROUP).sum(1)
    dci2 = -jnp.where(own_group.T, dc_im, 0.0).reshape(S5_LANES, groups_per_block, S5_GROUP).sum(1)

    def c_layout(a):
        return jnp.transpose(a.reshape(S5_GROUPS, S5_STATE, S5_GROUP), (0, 2, 1))[None]

    def b_layout(a):
        return a.T.reshape(1, S5_GROUPS, S5_STATE, S5_GROUP)

    local_small = {
        "mix_pre_g": jnp.concatenate([dg_mix_pre0, dg_mix_pre1]), "mix_post_g": jnp.concatenate([dg_mix_post0, dg_mix_post1]),
        "mlp_pre_g": jnp.concatenate([dg_mlp_pre0, dg_mlp_pre1]), "mlp_post_g": jnp.concatenate([dg_mlp_post0, dg_mlp_post1]),
        "s5_lam_re": dlr.reshape(1, S5_GROUPS, S5_STATE), "s5_lam_im": dli.reshape(1, S5_GROUPS, S5_STATE),
        "s5_log_dt": dldt8[0:1, 0:S5_GROUPS],
        "s5_b_re": b_layout(dbtr), "s5_b_im": b_layout(dbti), "s5_c_re": c_layout(dcr2), "s5_c_im": c_layout(dci2),
        "s5_d": dd, "fox_b_f": dbf[:, 0:FOX_HEADS],
        "pool_w": _diag_blocks(dpw_bd, len(POOL_WINDOWS))[None],
        "sgu_w_s": jnp.where(causal[None], dws, 0.0)[None], "sgu_b_s": dbst.T[None],
        "pool_scale": dpool_scale, "sgu_ln_g": dln_g, "sgu_ln_b": dln_b,
    }
    return loss8, dx0, dict(zip(REDUCED_EARLY + ("w_in_even",), halves)), local_small


def _reduce_and_update(W, M, V, loss8, dx0, halves, local_small):
    cx, cy, cc = _coords()
    chip = 2 * cx + cy

    summed = dict(local_small, loss=loss8[0:1, 0:1])
    vec = _pack_vec(summed, REDUCED_SEGS, N_DEV * SUBLANES)
    piece = vec.shape[0] // N_DEV
    everyone = allreduce_pieces(vec.reshape(N_DEV, piece, LANES), name="small_grads_allreduce")
    G = _unpack_vec(everyone, REDUCED_SEGS)
    loss = G["loss"].reshape(())
    for n in SHARDED_SMALL:
        G[n] = lax.dynamic_slice_in_dim(G[n], chip * LANES, LANES, axis=1)

    reduced = join_sibling_halves([halves[n] for n in BIG_NAMES], name="big_grads_join")
    for n, r in zip(BIG_NAMES, reduced):
        G[n] = r.reshape(W[n].shape)

    def two_d(a):
        return a.reshape(-1, a.shape[-1])

    delta, new_m, new_v = {}, {}, {}
    for n in BIG_NAMES:
        d_, m_, v_ = adamw(two_d(W[n]), two_d(G[n]), two_d(M[n]), two_d(V[n]), name=f"adamw_{n}")
        delta[n], new_m[n], new_v[n] = (t.reshape(W[n].shape) for t in (d_, m_, v_))
    packed = [_pack_vec(src, SMALL_SEGS, SUBLANES) for src in (W, G, M, V)]
    outs = adamw(*packed, name="adamw_replicated")
    for dst, t in zip((delta, new_m, new_v), outs):
        dst.update(_unpack_vec(t, SMALL_SEGS))
    sharded_segs = tuple((n, (1, LANES)) for n in SHARDED_SMALL)
    packed = [_pack_vec(src, sharded_segs, 1) for src in (W, G, M, V)]
    outs = adamw(*packed, name="adamw_sharded_vectors")
    for dst, t in zip((delta, new_m, new_v), outs):
        dst.update(_unpack_vec(t, sharded_segs))

    order = ["mix_pre_g", "mix_post_g", "mlp_pre_g", "mlp_post_g", "w_in_even", "s5_lam_re", "s5_lam_im", "s5_log_dt",
             "s5_b_re", "s5_b_im", "s5_c_re", "s5_c_im", "s5_d", "s5_w_glu", "fox_b_f", "w_out_even", "w_in_odd",
             "pool_w", "pool_scale", "sgu_ln_g", "sgu_ln_b", "sgu_w_s", "sgu_b_s", "w_out_odd", "mlp_w1", "mlp_w2"]
    return (loss, dx0[None], *[G[n] for n in order], *[delta[n] for n in order],
            *[new_m[n] for n in order], *[new_v[n] for n in order])
```

```python
import functools
import math

import jax
import jax.numpy as jnp
from jax import lax
from jax.experimental import pallas as pl
from jax.experimental.pallas import tpu as pltpu

F32 = jnp.float32
MXU_DTYPE = jnp.bfloat16
WIRE_DTYPE = jnp.bfloat16
EPS = 1e-6
VMEM_LIMIT_BYTES = 48 * 1024 * 1024
LANES = 128
SUBLANES = 8

D_MODEL = 1024
S5_WIDTH = 512
S5_GROUP = 16
S5_GROUPS = 32
S5_STATE = 64
S5_LANES = S5_GROUPS * S5_STATE
FOX_HEADS = 8
FOX_HEAD_DIM = 64
FOX_WIDTH = 512
EVEN_IN = S5_WIDTH + 3 * FOX_WIDTH + FOX_HEADS
EVEN_IN_PAD = 2176
POOL_WINDOWS = (2, 4, 8, 16)
POOL_HALO = 16
POOL_GROUP_DIM = 128
SGU_GROUPS = 4
SGU_GROUP_DIM = 128
CHUNK = 128
D_FF = 4096

ADAM_LR = 0.001
ADAM_B1 = 0.9
ADAM_B2 = 0.999
ADAM_EPS = 1e-08
ADAM_WD = 0.01
ADAM_STEP = 10

MESH_AXES = ("x", "y", "c")
MESH = pl.DeviceIdType.MESH
N_CHIPS = 4
N_DEV = 8

SDS = jax.ShapeDtypeStruct


def _cp(*sem):
    return pltpu.CompilerParams(dimension_semantics=sem, vmem_limit_bytes=VMEM_LIMIT_BYTES)


def _pick(dim, pref):
    if dim <= pref:
        return dim
    t = pref
    while t >= 256:
        if dim % t == 0:
            return t
        t //= 2
    return dim


def _row(tr, c):
    return pl.BlockSpec((tr, c), lambda i: (i, 0))


def _full(shape):
    nd = len(shape)
    return pl.BlockSpec(shape, lambda *_: (0,) * nd)


def _gelu_grad(x):
    c = math.sqrt(2.0 / math.pi)
    t = jnp.tanh(c * (x + 0.044715 * x * x * x))
    return 0.5 * (1.0 + t) + 0.5 * x * (1.0 - t * t) * c * (1.0 + 3.0 * 0.044715 * x * x)


MATMUL_VMEM_BYTES = 36 * 1024 * 1024


def matmul(a, b, *, name, ta=False, tb=False, out_dtype=F32, tm=2048, tn=1024, tk=4096, mnk=None, a_koff=0,
           a_spec=None, b_spec=None, o_spec=None, o_shape=None, prev=None, epi=None, epi_in=(), out_dtypes=None,
           exact_tiles=False, hosted=None):
    if mnk is None:
        M, K = (a.shape[1], a.shape[0]) if ta else a.shape
        K2, N = (b.shape[1], b.shape[0]) if tb else b.shape
        assert K == K2, (a.shape, b.shape, ta, tb)
    else:
        M, N, K = mnk
    out_dtypes = tuple(out_dtypes) if out_dtypes is not None else (out_dtype,)
    n_out, n_epi = len(out_dtypes), len(epi_in)
    tm, tn, tk = _pick(M, tm), _pick(N, tn), _pick(K, tk)

    def vmem_bytes(tm_, tn_, tk_):
        tiles = tm_ * tk_ * a.dtype.itemsize + tk_ * tn_ * b.dtype.itemsize
        tiles += tm_ * tn_ * (sum(jnp.dtype(d).itemsize for d in out_dtypes) + sum(e.dtype.itemsize for e in epi_in))
        return 2 * tiles + tm_ * tn_ * 4 * (tk_ < K)

    def halves(t, dim):
        return [t] + ([t // 2] if t % (2 * LANES) == 0 and t // 2 >= 512 and dim % (t // 2) == 0 else [])

    if exact_tiles:
        halves = lambda t, dim: [t]
    fits = [(m_, n_) for m_ in halves(tm, M) for n_ in halves(tn, N) if vmem_bytes(m_, n_, tk) <= MATMUL_VMEM_BYTES]
    if fits:
        tm, tn = max(fits, key=lambda t: (t[0] * t[1], t[0]))
    else:
        tm, tn = halves(tm, M)[-1], halves(tn, N)[-1]
        while vmem_bytes(tm, tn, tk) > MATMUL_VMEM_BYTES and tk % 2 == 0 and tk > 512:
            tk //= 2
    nk = K // tk
    assert a_koff % tk == 0 and not (ta and a_koff)
    ko = a_koff // tk
    dn = (((0 if ta else 1,), (1 if tb else 0,)), ((), ()))

    def body(*refs):
        a_ref, b_ref = refs[0], refs[1]
        epi_refs = refs[2:2 + n_epi]
        o_refs = refs[len(refs) - n_out - (nk > 1):len(refs) - (nk > 1)]
        k = pl.program_id(2)
        bv = b_ref[...]
        if bv.ndim == 3 and tb:
            cw = bv.shape[-1]
            prod = sum(lax.dot_general(a_ref[:, c * cw:(c + 1) * cw].astype(MXU_DTYPE), bv[c].astype(MXU_DTYPE), dn,
                                       preferred_element_type=F32) for c in range(bv.shape[0]))
        else:
            if bv.ndim == 3:
                bv = bv.reshape(-1, bv.shape[-1])
            prod = lax.dot_general(a_ref[...].astype(MXU_DTYPE), bv.astype(MXU_DTYPE), dn, preferred_element_type=F32)

        def finish(acc):
            res = (acc,) if epi is None else epi(acc, *[r[...] for r in epi_refs])
            for o_ref, r in zip(o_refs, res):
                o_ref[...] = r.astype(o_ref.dtype)

        if nk == 1:
            finish(prod)
            return
        acc_ref = refs[-1]

        @pl.when(k == 0)
        def _():
            acc_ref[...] = prod

        @pl.when(jnp.logical_and(k > 0, k < nk - 1))
        def _():
            acc_ref[...] += prod

        @pl.when(k == nk - 1)
        def _():
            finish(acc_ref[...] + prod)

    if a_spec is None:
        a_spec = pl.BlockSpec((tk, tm), lambda i, j, k: (k, i)) if ta else pl.BlockSpec((tm, tk), lambda i, j, k: (i, k + ko))
    else:
        a_spec = a_spec(tm, tn, tk)
    if b_spec is None:
        bs = pl.BlockSpec((tn, tk), lambda i, j, k: (j, k)) if tb else pl.BlockSpec((tk, tn), lambda i, j, k: (k, j))
    else:
        bs = b_spec(tm, tn, tk)
    tile = pl.BlockSpec((tm, tn), lambda i, j, k: (i, j))
    os_ = tile if o_spec is None else o_spec(tm, tn, tk)
    ins, in_specs, aliases = [a, b, *epi_in], [a_spec, bs] + [tile] * n_epi, {}
    if prev is not None:
        aliases = {len(ins): 0}
        ins.append(prev)
        in_specs.append(pl.BlockSpec(memory_space=pl.ANY))
    shapes = [SDS((M, N) if o_shape is None else o_shape, dt) for dt in out_dtypes]
    scratch = [pltpu.VMEM((tm, tn), F32)] if nk > 1 else []
    if hosted is not None:
        outs, bufs = call_hosting(body, hosted, name=name, grid=(M // tm, N // tn, nk), in_specs=in_specs,
                                  out_specs=[os_] * n_out, out_shape=shapes, inputs=ins, aliases=aliases, scratch=scratch)
        return (outs[0] if n_out == 1 else outs), bufs
    outs = pl.pallas_call(
        body, name=name, grid=(M // tm, N // tn, nk),
        in_specs=in_specs, out_specs=[os_] * n_out, out_shape=shapes, input_output_aliases=aliases,
        scratch_shapes=scratch, compiler_params=_cp("parallel", "parallel", "arbitrary"),
    )(*ins)
    return outs[0] if n_out == 1 else outs


def _rms_hat(x):
    return x * lax.rsqrt(jnp.mean(x * x, axis=-1, keepdims=True) + EPS)


def rms_fwd(x, g, hosted, *, name):
    T, D = x.shape
    tr = _pick(T, 512)

    def body(x_ref, g_ref, o_ref):
        o_ref[...] = (_rms_hat(x_ref[...]) * g_ref[...]).astype(o_ref.dtype)

    (h,), bufs = call_hosting(body, hosted, name=name, grid=(T // tr,), in_specs=[_row(tr, D), _full((1, D))],
                              out_specs=[_row(tr, D)], out_shape=[SDS((T, D), MXU_DTYPE)], inputs=[x, g], aliases={})
    return h, bufs


def res_norm_fwd(x, y, g_post, g_next, *, name):
    T, D = x.shape
    tr = _pick(T, 512)

    def body(x_ref, y_ref, gp_ref, gn_ref, o_ref, h_ref):
        xn = x_ref[...] + _rms_hat(y_ref[...]) * gp_ref[...]
        o_ref[...] = xn
        h_ref[...] = (_rms_hat(xn) * gn_ref[...]).astype(h_ref.dtype)

    return pl.pallas_call(body, name=name, grid=(T // tr,),
                          in_specs=[_row(tr, D), _row(tr, D), _full((1, D)), _full((1, D))],
                          out_specs=[_row(tr, D), _row(tr, D)], out_shape=[SDS((T, D), F32), SDS((T, D), MXU_DTYPE)],
                          compiler_params=_cp("parallel"))(x, y, g_post, g_next)


def res_norm_loss(x, y, g_post, target, *, name):
    T, D = x.shape
    tr = _pick(T, 512)

    def body(x_ref, y_ref, g_ref, t_ref, l_ref, d_ref):
        err = x_ref[...] + _rms_hat(y_ref[...]) * g_ref[...] - t_ref[...]
        d_ref[...] = err * (1.0 / D)

        @pl.when(pl.program_id(0) == 0)
        def _():
            l_ref[...] = jnp.zeros_like(l_ref)

        l_ref[...] += 0.5 * jnp.sum(jnp.mean(err * err, axis=-1, keepdims=True))

    return pl.pallas_call(body, name=name, grid=(T // tr,),
                          in_specs=[_row(tr, D), _row(tr, D), _full((1, D)), _row(tr, D)],
                          out_specs=[_full((SUBLANES, LANES)), _row(tr, D)],
                          out_shape=[SDS((SUBLANES, LANES), F32), SDS((T, D), F32)],
                          compiler_params=_cp("arbitrary"))(x, y, g_post, target)


def _rms_bwd_rows(x, g, dy):
    r = lax.rsqrt(jnp.mean(x * x, axis=-1, keepdims=True) + EPS)
    xh = x * r
    dxh = dy * g
    return r * (dxh - xh * jnp.mean(dxh * xh, axis=-1, keepdims=True)), jnp.sum(dy * xh, axis=0, keepdims=True)


def norm_res_bwd(x, g_pre, dh, res, y, g_post, *, name):
    T, D = x.shape
    tr = _pick(T, 512)

    def body(x_ref, gp_ref, dh_ref, res_ref, y_ref, gy_ref, dx_ref, dy_ref, dgp_ref, dgy_ref):
        dx, dgp = _rms_bwd_rows(x_ref[...], gp_ref[...], dh_ref[...])
        dx = dx + res_ref[...]
        dx_ref[...] = dx
        dy, dgy = _rms_bwd_rows(y_ref[...], gy_ref[...], dx)
        dy_ref[...] = dy.astype(dy_ref.dtype)

        @pl.when(pl.program_id(0) == 0)
        def _():
            dgp_ref[...] = jnp.zeros_like(dgp_ref)
            dgy_ref[...] = jnp.zeros_like(dgy_ref)

        dgp_ref[...] += dgp
        dgy_ref[...] += dgy

    row, vec = _row(tr, D), _full((1, D))
    return pl.pallas_call(body, name=name, grid=(T // tr,), in_specs=[row, vec, row, row, row, vec],
                          out_specs=[row, row, vec, vec],
                          out_shape=[SDS((T, D), F32), SDS((T, D), MXU_DTYPE), SDS((1, D), F32), SDS((1, D), F32)],
                          compiler_params=_cp("arbitrary"))(x, g_pre, dh, res, y, g_post)


def rms_bwd(x, g, dy, res, *, name, hosted=None):
    T, D = x.shape
    tr = _pick(T, 512)
    has_res = res is not None

    def body(*refs):
        if has_res:
            x_ref, g_ref, dy_ref, res_ref, dx_ref, dg_ref = refs
        else:
            x_ref, g_ref, dy_ref, dx_ref, dg_ref = refs
        dx, dg = _rms_bwd_rows(x_ref[...], g_ref[...], dy_ref[...])
        if has_res:
            dx = dx + res_ref[...]
        dx_ref[...] = dx.astype(dx_ref.dtype)

        @pl.when(pl.program_id(0) == 0)
        def _():
            dg_ref[...] = jnp.zeros_like(dg_ref)

        dg_ref[...] += dg

    ins = [x, g, dy] + ([res] if has_res else [])
    in_specs = [_row(tr, D), _full((1, D)), _row(tr, D)] + ([_row(tr, D)] if has_res else [])
    out_shape = [SDS((T, D), F32 if has_res else MXU_DTYPE), SDS((1, D), F32)]
    out_specs = [_row(tr, D), _full((1, D))]
    if hosted is not None:
        return call_hosting(body, hosted, name=name, grid=(T // tr,), in_specs=in_specs, out_specs=out_specs,
                            out_shape=out_shape, inputs=ins, aliases={})
    return pl.pallas_call(body, name=name, grid=(T // tr,), in_specs=in_specs, out_specs=out_specs,
                          out_shape=out_shape, compiler_params=_cp("arbitrary"))(*ins)


def _s5_disc(lr, li, ldt, btr, bti):
    dt = jnp.exp(ldt)
    k = lax.broadcasted_iota(jnp.int32, (SUBLANES, S5_LANES), 0).astype(F32)
    kf = k + 1.0
    kb = 8.0 - k
    ph = li * dt
    lm = lr * dt
    tf_re = jnp.exp(kf * lm) * jnp.cos(kf * ph)
    tf_im = jnp.exp(kf * lm) * jnp.sin(kf * ph)
    tb_re = jnp.exp(kb * lm) * jnp.cos(kb * ph)
    tb_im = -jnp.exp(kb * lm) * jnp.sin(kb * ph)
    mag = jnp.exp(lm)
    ab_re = mag * jnp.cos(ph)
    ab_im = mag * jnp.sin(ph)
    den = lr * lr + li * li
    nr = ab_re - 1.0
    ni = ab_im
    q_re = (nr * lr + ni * li) / den
    q_im = (ni * lr - nr * li) / den
    bbt_re = q_re * btr - q_im * bti
    bbt_im = q_re * bti + q_im * btr
    return tf_re, tf_im, tb_re, tb_im, bbt_re, bbt_im


def _s5_disc_core(lr, li, ldt, btr, bti):
    dt = jnp.exp(ldt)
    mag = jnp.exp(lr * dt)
    ab_re = mag * jnp.cos(li * dt)
    ab_im = mag * jnp.sin(li * dt)
    den = lr * lr + li * li
    nr = ab_re - 1.0
    ni = ab_im
    q_re = (nr * lr + ni * li) / den
    q_im = (ni * lr - nr * li) / den
    return ab_re, ab_im, q_re * btr - q_im * bti, q_re * bti + q_im * btr


def s5_disc_fwd(lr, li, ldt, btr, bti, *, name):
    def body(lr_ref, li_ref, ldt_ref, btr_ref, bti_ref, *outs):
        vals = _s5_disc(lr_ref[...], li_ref[...], ldt_ref[...], btr_ref[...], bti_ref[...])
        for o, v in zip(outs, vals):
            o[...] = v

    tab = SDS((SUBLANES, S5_LANES), F32)
    bb = SDS((S5_GROUP, S5_LANES), F32)
    return pl.pallas_call(body, name=name, out_shape=[tab, tab, tab, tab, bb, bb])(lr, li, ldt, btr, bti)


def s5_disc_bwd(lr, li, ldt, btr, bti, dab_re, dab_im, dbbt_re, dbbt_im, *, name):
    def body(lr_ref, li_ref, ldt_ref, btr_ref, bti_ref, dar_ref, dai_ref, dbr_ref, dbi_ref,
             dlr_ref, dli_ref, dldt_ref, dbtr_ref, dbti_ref):
        _, vjp = jax.vjp(_s5_disc_core, lr_ref[...], li_ref[...], ldt_ref[...], btr_ref[...], bti_ref[...])
        dlr, dli, dldt, dbtr, dbti = vjp((dar_ref[...], dai_ref[...], dbr_ref[...], dbi_ref[...]))
        dlr_ref[...] = dlr
        dli_ref[...] = dli
        dbtr_ref[...] = dbtr
        dbti_ref[...] = dbti
        lane_group = lax.broadcasted_iota(jnp.int32, (S5_LANES, LANES), 0) // S5_STATE
        col = lax.broadcasted_iota(jnp.int32, (S5_LANES, LANES), 1)
        ind = (lane_group == col).astype(F32)
        dldt_ref[...] = jnp.dot(jnp.broadcast_to(dldt, (SUBLANES, S5_LANES)), ind,
                                precision=lax.Precision.HIGHEST, preferred_element_type=F32)

    row = SDS((1, S5_LANES), F32)
    bb = SDS((S5_GROUP, S5_LANES), F32)
    return pl.pallas_call(body, name=name, out_shape=[row, row, SDS((SUBLANES, LANES), F32), bb, bb])(
        lr, li, ldt, btr, bti, dab_re, dab_im, dbbt_re, dbbt_im)


S5_NB = 1024


S5_CB = S5_WIDTH * S5_NB // S5_LANES


def _chan_rows_t(tm, tn, tk):
    return pl.BlockSpec((tk, S5_CB), lambda i, j, k: (k, j // 2))


def _chan_cols_of_i(tm, tn, tk):
    return pl.BlockSpec((tk, S5_CB), lambda i, j, k: (k, i // 2))


def _lanes_of_chan(tm, tn, tk):
    return pl.BlockSpec((tm, 2 * S5_NB), lambda i, j, k: (i, j))


def _s5_b_block_t(tm, tn, tk):
    return pl.BlockSpec((S5_CB, 2 * S5_NB), lambda i, j, k: (j, j))


def _s5_c_block(tm, tn, tk):
    return pl.BlockSpec((2 * S5_NB, S5_CB), lambda i, j, k: (j, j))


def s5_interleave(re, im, axis):
    parts = []
    for n in range(S5_LANES // S5_NB):
        sl = [slice(None)] * re.ndim
        sl[axis] = slice(n * S5_NB, (n + 1) * S5_NB)
        parts += [re[tuple(sl)], im[tuple(sl)]]
    return jnp.concatenate(parts, axis=axis)


def s5_deinterleave(a, axis):
    re, im = [], []
    for n in range(S5_LANES // S5_NB):
        sl = [slice(None)] * a.ndim
        sl[axis] = slice(2 * n * S5_NB, (2 * n + 1) * S5_NB)
        re.append(a[tuple(sl)])
        sl[axis] = slice((2 * n + 1) * S5_NB, (2 * n + 2) * S5_NB)
        im.append(a[tuple(sl)])
    return jnp.concatenate(re, axis=axis), jnp.concatenate(im, axis=axis)


def s5_scan(src, mat, tab_re, tab_im, *, reverse, name, states=None, hosted=None):
    T = src.shape[0]
    nb = S5_NB
    tc = _pick(T, 256)
    nl = S5_LANES // nb
    nt = T // tc
    ntile = tc // SUBLANES
    with_da = states is not None
    assert reverse or not with_da
    step_rows = ((1, 7), (2, 6), (4, 4)) if reverse else ((1, 0), (2, 1), (4, 3))
    drive_dn = _NT if reverse else (((1,), (0,)), ((), ()))

    def body(*refs):
        if with_da:
            (src_ref, wr_ref, wi_ref, tr_ref, ti_ref, sr_ref, si_ref, hr_ref, hi_ref, xo_ref, dar_ref, dai_ref,
             cr_ref, ci_ref, mr_ref, mi_ref, br_ref, bi_ref, ar_ref, ai_ref) = refs
        else:
            src_ref, wr_ref, wi_ref, tr_ref, ti_ref, xo_ref, cr_ref, ci_ref, mr_ref, mi_ref, br_ref, bi_ref = refs

        @pl.when(pl.program_id(1) == 0)
        def _():
            cr_ref[...] = jnp.zeros_like(cr_ref)
            ci_ref[...] = jnp.zeros_like(ci_ref)
            if with_da:
                ar_ref[...] = jnp.zeros_like(ar_ref)
                ai_ref[...] = jnp.zeros_like(ai_ref)

        lhs = src_ref[...].astype(MXU_DTYPE)
        br_ref[...] = lax.dot_general(lhs, wr_ref[...].astype(MXU_DTYPE), drive_dn, preferred_element_type=F32)
        bi_ref[...] = lax.dot_general(lhs, wi_ref[...].astype(MXU_DTYPE), drive_dn, preferred_element_type=F32)

        seen = jnp.where(pl.program_id(1) < nt - 1, 1.0, 0.0)

        def add_da(lr, li, r0, last_r, last_i):
            first = lax.broadcasted_iota(jnp.int32, (SUBLANES, nb), 0) == 0
            pr = jnp.where(first, last_r, pltpu.roll(sr_ref[pl.ds(r0, SUBLANES), :], 1, 0))
            pi = jnp.where(first, last_i, pltpu.roll(si_ref[pl.ds(r0, SUBLANES), :], 1, 0))
            ar_ref[...] += lr * pr + li * pi
            ai_ref[...] += li * pr - lr * pi

        io = lax.broadcasted_iota(jnp.int32, (SUBLANES, nb), 0)
        for s_, (d, r) in enumerate(step_rows):
            keep = (io < SUBLANES - d) if reverse else (io >= d)
            mr_ref[s_] = jnp.where(keep, tr_ref[r:r + 1, :], 0.0)
            mi_ref[s_] = jnp.where(keep, ti_ref[r:r + 1, :], 0.0)

        def tile(i, carry):
            cr, ci = carry
            j = (ntile - 1 - i) if reverse else i
            r0 = pl.multiple_of(j * SUBLANES, SUBLANES)
            xr = br_ref[pl.ds(r0, SUBLANES), :]
            xi = bi_ref[pl.ds(r0, SUBLANES), :]
            for s_, (d, _) in enumerate(step_rows):
                sh = (SUBLANES - d) if reverse else d
                sr = pltpu.roll(xr, sh, 0)
                si = pltpu.roll(xi, sh, 0)
                pr, pi = mr_ref[s_], mi_ref[s_]
                xr, xi = xr + pr * sr - pi * si, xi + pr * si + pi * sr
            tr, ti = tr_ref[...], ti_ref[...]
            xr, xi = xr + tr * cr - ti * ci, xi + tr * ci + ti * cr
            xo_ref[pl.ds(r0, SUBLANES), 0:nb] = xr
            xo_ref[pl.ds(r0, SUBLANES), nb:2 * nb] = xi
            if with_da:
                @pl.when(j > 0)
                def _():
                    p0 = pl.multiple_of(r0 - SUBLANES, SUBLANES)
                    add_da(xr, xi, r0, sr_ref[pl.ds(p0, SUBLANES), :][SUBLANES - 1:SUBLANES, :],
                           si_ref[pl.ds(p0, SUBLANES), :][SUBLANES - 1:SUBLANES, :])

                @pl.when(j == 0)
                def _():
                    add_da(xr, xi, r0, hr_ref[SUBLANES - 1:SUBLANES, :] * seen, hi_ref[SUBLANES - 1:SUBLANES, :] * seen)
            if reverse:
                return xr[0:1, :], xi[0:1, :]
            return xr[SUBLANES - 1:SUBLANES, :], xi[SUBLANES - 1:SUBLANES, :]

        cr, ci = lax.fori_loop(0, ntile, tile, (cr_ref[0:1, :], ci_ref[0:1, :]))
        cr_ref[0:1, :] = cr
        ci_ref[0:1, :] = ci
        if with_da:
            @pl.when(pl.program_id(1) == nt - 1)
            def _():
                dar_ref[...] = jnp.sum(ar_ref[...], axis=0, keepdims=True)
                dai_ref[...] = jnp.sum(ai_ref[...], axis=0, keepdims=True)

    def tmap(t):
        return (nt - 1 - t) if reverse else t

    hb = tc // SUBLANES
    re_spec = pl.BlockSpec((tc, nb), lambda n, t: (tmap(t), 2 * n))
    im_spec = pl.BlockSpec((tc, nb), lambda n, t: (tmap(t), 2 * n + 1))
    tab_spec = pl.BlockSpec((SUBLANES, nb), lambda n, t: (0, n))
    out_spec = pl.BlockSpec((tc, 2 * nb), lambda n, t: (tmap(t), n))
    out_shape = SDS((T, 2 * S5_LANES), F32)
    scratch = [pltpu.VMEM((SUBLANES, nb), F32), pltpu.VMEM((SUBLANES, nb), F32),
               pltpu.VMEM((len(step_rows), SUBLANES, nb), F32), pltpu.VMEM((len(step_rows), SUBLANES, nb), F32),
               pltpu.VMEM((tc, nb), F32), pltpu.VMEM((tc, nb), F32)]
    src_spec = pl.BlockSpec((tc, S5_CB), lambda n, t: (tmap(t), n))
    if reverse:
        wr_spec = pl.BlockSpec((nb, S5_CB), lambda n, t: (2 * n, n))
        wi_spec = pl.BlockSpec((nb, S5_CB), lambda n, t: (2 * n + 1, n))
    else:
        wr_spec = pl.BlockSpec((S5_CB, nb), lambda n, t: (n, 2 * n))
        wi_spec = pl.BlockSpec((S5_CB, nb), lambda n, t: (n, 2 * n + 1))
    drive_specs = [src_spec, wr_spec, wi_spec, tab_spec, tab_spec]
    drive = [src, mat, mat, tab_re, tab_im]
    if not with_da:
        return pl.pallas_call(body, name=name, grid=(nl, nt), in_specs=drive_specs,
                              out_specs=out_spec, out_shape=out_shape, scratch_shapes=scratch,
                              compiler_params=_cp("parallel", "arbitrary"))(*drive)
    re_halo = pl.BlockSpec((SUBLANES, nb), lambda n, t: (jnp.maximum(tmap(t) * hb - 1, 0), 2 * n))
    im_halo = pl.BlockSpec((SUBLANES, nb), lambda n, t: (jnp.maximum(tmap(t) * hb - 1, 0), 2 * n + 1))
    acc = pl.BlockSpec((1, nb), lambda n, t: (0, n))
    row = SDS((1, S5_LANES), F32)
    return call_hosting(
        body, hosted, name=name, grid=(nl, nt),
        in_specs=drive_specs + [re_spec, im_spec, re_halo, im_halo],
        out_specs=[out_spec, acc, acc], out_shape=[out_shape, row, row],
        inputs=drive + [states, states, states, states], aliases={},
        scratch=scratch + [pltpu.VMEM((SUBLANES, nb), F32), pltpu.VMEM((SUBLANES, nb), F32)])


def s5_out_fwd(yc, u, d, *, name):
    T, C = yc.shape
    tr = _pick(T, 512)

    def body(yc_ref, u_ref, d_ref, yl_ref, yg_ref):
        yl = yc_ref[...] + d_ref[...] * u_ref[...]
        yl_ref[...] = yl
        yg_ref[...] = jax.nn.gelu(yl)

    return pl.pallas_call(body, name=name, grid=(T // tr,), in_specs=[_row(tr, C), _row(tr, C), _full((1, C))],
                          out_specs=[_row(tr, C)] * 2, out_shape=[SDS((T, C), F32)] * 2,
                          compiler_params=_cp("parallel"))(yc, u, d)


def glu_fwd(yg, gl, *, out_cols, name):
    T, C = yg.shape
    tr = _pick(T, 512)

    def body(yg_ref, gl_ref, o_ref):
        o_ref[...] = yg_ref[...] * jax.nn.sigmoid(gl_ref[...])

    return pl.pallas_call(body, name=name, grid=(T // tr,), in_specs=[_row(tr, C)] * 2, out_specs=_row(tr, C),
                          out_shape=SDS((T, out_cols), F32), compiler_params=_cp("parallel"))(yg, gl)


def glu_bwd(yg, gl, dy, *, name):
    T, C = yg.shape
    tr = _pick(T, 512)

    def body(yg_ref, gl_ref, dy_ref, dyg_ref, dgl_ref):
        s = jax.nn.sigmoid(gl_ref[...])
        dyv = dy_ref[...]
        dyg_ref[...] = dyv * s
        dgl_ref[...] = (dyv * yg_ref[...] * s * (1.0 - s)).astype(dgl_ref.dtype)

    return pl.pallas_call(body, name=name, grid=(T // tr,), in_specs=[_row(tr, C)] * 3, out_specs=[_row(tr, C)] * 2,
                          out_shape=[SDS((T, C), F32), SDS((T, C), MXU_DTYPE)],
                          compiler_params=_cp("parallel"))(yg, gl, dy)


def s5_out_bwd(yl, u, d, dyg_a, dyg_b, *, name):
    T, C = yl.shape
    tr = _pick(T, 512)

    def body(yl_ref, u_ref, d_ref, da_ref, db_ref, dyl_ref, du_ref, dd_ref):
        dyl = (da_ref[...] + db_ref[...]) * _gelu_grad(yl_ref[...])
        dyl_ref[...] = dyl.astype(dyl_ref.dtype)
        du_ref[...] = dyl * d_ref[...]

        @pl.when(pl.program_id(0) == 0)
        def _():
            dd_ref[...] = jnp.zeros_like(dd_ref)

        dd_ref[...] += jnp.sum(dyl * u_ref[...], axis=0, keepdims=True)

    return pl.pallas_call(body, name=name, grid=(T // tr,),
                          in_specs=[_row(tr, C), _row(tr, C), _full((1, C)), _row(tr, C), _row(tr, C)],
                          out_specs=[_row(tr, C), _row(tr, C), _full((1, C))],
                          out_shape=[SDS((T, C), MXU_DTYPE), SDS((T, C), F32), SDS((1, C), F32)],
                          compiler_params=_cp("arbitrary"))(yl, u, d, dyg_a, dyg_b)


def add2(a, b, *, name):
    T, C = a.shape
    tr = _pick(T, 512)

    def body(a_ref, b_ref, o_ref):
        o_ref[...] = a_ref[...] + b_ref[...]

    return pl.pallas_call(body, name=name, grid=(T // tr,), in_specs=[_row(tr, C)] * 2, out_specs=_row(tr, C),
                          out_shape=SDS((T, C), F32), compiler_params=_cp("parallel"))(a, b)


def _tri(n, upper):
    r = lax.broadcasted_iota(jnp.int32, (n, n), 0)
    c = lax.broadcasted_iota(jnp.int32, (n, n), 1)
    return ((c >= r) if upper else (c <= r)).astype(F32)


def fox_gate_fwd(fl, bf, *, fl_col, name):
    T = fl.shape[0]
    tb = _pick(T, 256)

    def body(fl_ref, bf_ref, f_ref, c_ref):
        @pl.when(pl.program_id(0) == 0)
        def _():
            c_ref[...] = jnp.zeros_like(c_ref)

        lf = jax.nn.log_sigmoid(fl_ref[...] + bf_ref[...])
        f = jnp.dot(_tri(tb, False), lf, precision=lax.Precision.HIGHEST, preferred_element_type=F32) + c_ref[0:1, :]
        f_ref[...] = f * LOG2E
        c_ref[0:1, :] = f[tb - 1:tb, :]

    fl_spec = pl.BlockSpec((tb, LANES), lambda i: (i, fl_col))
    return pl.pallas_call(body, name=name, grid=(T // tb,), in_specs=[fl_spec, _full((1, LANES))],
                          out_specs=_row(tb, LANES), out_shape=SDS((T, LANES), F32),
                          scratch_shapes=[pltpu.VMEM((SUBLANES, LANES), F32)], compiler_params=_cp("arbitrary"))(fl, bf)


def fox_gate_bwd(fl, bf, df_keys, df_queries, *, fl_col, name):
    T = fl.shape[0]
    tb = _pick(T, 256)
    nt = T // tb

    def body(fl_ref, bf_ref, dfk_ref, dfq_ref, dfl_ref, dbf_ref, c_ref):
        @pl.when(pl.program_id(0) == 0)
        def _():
            c_ref[...] = jnp.zeros_like(c_ref)
            dbf_ref[...] = jnp.zeros_like(dbf_ref)

        dlf = jnp.dot(_tri(tb, True), dfk_ref[...] + dfq_ref[...], precision=lax.Precision.HIGHEST,
                      preferred_element_type=F32) + c_ref[0:1, :]
        c_ref[0:1, :] = dlf[0:1, :]
        dfl = dlf * jax.nn.sigmoid(-(fl_ref[...] + bf_ref[...]))
        dfl_ref[...] = dfl
        dbf_ref[...] += jnp.sum(dfl, axis=0, keepdims=True)

    rev = pl.BlockSpec((tb, LANES), lambda i: (nt - 1 - i, 0))
    fl_rev = pl.BlockSpec((tb, LANES), lambda i: (nt - 1 - i, fl_col))
    return pl.pallas_call(body, name=name, grid=(nt,), in_specs=[fl_rev, _full((1, LANES)), rev, rev],
                          out_specs=[rev, _full((1, LANES))], out_shape=[SDS((T, LANES), F32), SDS((1, LANES), F32)],
                          scratch_shapes=[pltpu.VMEM((SUBLANES, LANES), F32)],
                          compiler_params=_cp("arbitrary"))(fl, bf, df_keys, df_queries)


FOX_BLOCK = 512
FOX_PAIRS = FOX_HEADS // 2
_NT = (((1,), (1,)), ((), ()))


LOG2E = 1.4426950408889634
FOX_FWD_UNROLL = 4
FOX_BWD_UNROLL = 3


def _fox_block(T):
    return _pick(T, FOX_BLOCK)


def _own_lanes(lane, hh):
    return (lane < FOX_HEAD_DIM) if hh == 0 else (lane >= FOX_HEAD_DIM)


def _grouped_steps(step, lo, n, unroll, init):
    def trip(t, c):
        for u in range(unroll):
            c = step(lo + t * unroll + u, c)
        return c

    carry = lax.fori_loop(0, n // unroll, trip, init)
    for u in range(unroll - 1):
        carry = lax.cond(n % unroll > u, lambda c: step(lo + (n // unroll) * unroll + u, c), lambda c: c, carry)
    return carry


Q_TILE0, K_TILE0, V_TILE0, O_TILE0 = 4, 8, 12, 4
FL_TILE = 16
POOL_COL = 2


def fox_fwd(z, f_col, f_row, ycat, hosted, *, name):
    T = z.shape[0]
    blk = _fox_block(T)
    nb = T // blk
    scale = FOX_HEAD_DIM ** -0.5

    def body(q_ref, k_ref, v_ref, fc_ref, fr_ref, prev_ref, o_ref, l_ref):
        i = pl.program_id(1)
        row = lax.broadcasted_iota(jnp.int32, (blk, blk), 0)
        col = lax.broadcasted_iota(jnp.int32, (blk, blk), 1)
        lane = lax.broadcasted_iota(jnp.int32, (blk, LANES), 1)
        qt = q_ref[...] * (scale * LOG2E)
        outs = []
        for hh in range(2):
            qh = jnp.where(_own_lanes(lane, hh), qt, 0.0).astype(MXU_DTYPE)
            fi = fc_ref[0, :, hh:hh + 1]

            def step(j, carry, masked=False):
                m, l, acc = carry
                r0 = pl.multiple_of(j * blk, blk)
                kj = k_ref[pl.ds(r0, blk), :].astype(MXU_DTYPE)
                vj = v_ref[pl.ds(r0, blk), :].astype(MXU_DTYPE)
                s = lax.dot_general(qh, kj, _NT, preferred_element_type=F32) + (fi - fr_ref[0, j, hh:hh + 1, :])
                if masked:
                    s = jnp.where(col <= row, s, -jnp.inf)
                m_new = jnp.maximum(m, jnp.max(s, axis=-1, keepdims=True))
                p = jnp.exp2(s - m_new)
                alpha = jnp.exp2(m - m_new)
                l = alpha * l + jnp.sum(p, axis=-1, keepdims=True)
                acc = alpha * acc + jnp.dot(p.astype(MXU_DTYPE), vj, preferred_element_type=F32)
                return m_new, l, acc

            init = (jnp.full((blk, 1), -jnp.inf, F32), jnp.zeros((blk, 1), F32), jnp.zeros((blk, LANES), F32))
            m, l, acc = step(i, _grouped_steps(step, 0, i, FOX_FWD_UNROLL, init), True)
            outs.append(acc / l)
            l_ref[0, :, hh:hh + 1] = m + jnp.log2(l)
        o_ref[...] = jnp.where(_own_lanes(lane, 0), outs[0], outs[1])

    qspec = pl.BlockSpec((blk, LANES), lambda h, i: (i, Q_TILE0 + h))
    kspec = pl.BlockSpec((T, LANES), lambda h, i: (0, K_TILE0 + h))
    vspec = pl.BlockSpec((T, LANES), lambda h, i: (0, V_TILE0 + h))
    ospec = pl.BlockSpec((blk, LANES), lambda h, i: (i, O_TILE0 + h))
    cspec = pl.BlockSpec((1, blk, 2), lambda h, i: (h, i, 0))
    rspec = pl.BlockSpec((1, nb, 2, blk), lambda h, i: (h, 0, 0, 0))
    return call_hosting(body, hosted, name=name, grid=(FOX_PAIRS, nb),
                        in_specs=[qspec, kspec, vspec, cspec, rspec, ANY], out_specs=[ospec, cspec],
                        out_shape=[SDS(ycat.shape, F32), SDS((FOX_PAIRS, T, 2), F32)],
                        inputs=[z, z, z, f_col, f_row, ycat], aliases={5: 0})


def fox_dd(ycat, dycat, *, name):
    T = ycat.shape[0]
    blk = _fox_block(T)

    def body(o_ref, do_ref, dd_ref):
        lane = lax.broadcasted_iota(jnp.int32, (blk, LANES), 1)
        prod = do_ref[...] * o_ref[...]
        for hh in range(2):
            dd_ref[0, :, hh:hh + 1] = jnp.sum(jnp.where(_own_lanes(lane, hh), prod, 0.0), axis=-1, keepdims=True)

    ospec = pl.BlockSpec((blk, LANES), lambda h, i: (i, O_TILE0 + h))
    return pl.pallas_call(body, name=name, grid=(FOX_PAIRS, T // blk), in_specs=[ospec, ospec],
                          out_specs=pl.BlockSpec((1, blk, 2), lambda h, i: (h, i, 0)),
                          out_shape=SDS((FOX_PAIRS, T, 2), F32), compiler_params=_cp("parallel", "parallel"))(ycat, dycat)


def fox_bwd(z, dycat, f_col, f_row, lse_row, dd_row, hosted, *, name):
    T = z.shape[0]
    blk = _fox_block(T)
    nb = T // blk
    scale = FOX_HEAD_DIM ** -0.5

    def body(q_ref, k_ref, v_ref, do_ref, fc_ref, fr_ref, lr_ref, dr_ref, dk_ref, dv_ref, df_ref, dqt_ref, dfq_ref):
        j = pl.program_id(1)

        @pl.when(j == 0)
        def _():
            dqt_ref[...] = jnp.zeros_like(dqt_ref)
            dfq_ref[...] = jnp.zeros_like(dfq_ref)

        row = lax.broadcasted_iota(jnp.int32, (blk, blk), 0)
        col = lax.broadcasted_iota(jnp.int32, (blk, blk), 1)
        lane = lax.broadcasted_iota(jnp.int32, (blk, LANES), 1)
        kt = k_ref[...]
        vt = v_ref[...]
        dks, dvs = [], []
        for hh in range(2):
            own = _own_lanes(lane, hh)
            kh = jnp.where(own, kt, 0.0).astype(MXU_DTYPE)
            vh = jnp.where(own, vt, 0.0).astype(MXU_DTYPE)
            kht = kh.T
            fj = fc_ref[0, :, hh:hh + 1]

            def step(i, carry, masked=False):
                dk, dv, df = carry
                r0 = pl.multiple_of(i * blk, blk)
                qi = (q_ref[pl.ds(r0, blk), :] * (scale * LOG2E)).astype(MXU_DTYPE)
                doi = do_ref[pl.ds(r0, blk), :].astype(MXU_DTYPE)
                st = lax.dot_general(kh, qi, _NT, preferred_element_type=F32) + (fr_ref[0, i, hh:hh + 1, :] - fj)
                pt = jnp.exp2(st - lr_ref[0, i, hh:hh + 1, :])
                if masked:
                    pt = jnp.where(col >= row, pt, 0.0)
                dv = dv + jnp.dot(pt.astype(MXU_DTYPE), doi, preferred_element_type=F32)
                dpt = lax.dot_general(vh, doi, _NT, preferred_element_type=F32)
                dst = pt * (dpt - dr_ref[0, i, hh:hh + 1, :])
                dsb = dst.astype(MXU_DTYPE)
                dk = dk + jnp.dot(dsb, qi, preferred_element_type=F32)
                df = df - jnp.sum(dst, axis=-1, keepdims=True)
                dqt_ref[0, i] += jnp.dot(kht, dsb, preferred_element_type=F32)
                dfq_ref[0, i, hh:hh + 1, :] += jnp.sum(dst, axis=0, keepdims=True)
                return dk, dv, df

            init = (jnp.zeros((blk, LANES), F32), jnp.zeros((blk, LANES), F32), jnp.zeros((blk, 1), F32))
            dk, dv, df = _grouped_steps(step, j + 1, nb - 1 - j, FOX_BWD_UNROLL, step(j, init, True))
            dks.append(dk * (1.0 / LOG2E))
            dvs.append(dv)
            df_ref[0, :, hh:hh + 1] = df
        dk_ref[...] = jnp.where(_own_lanes(lane, 0), dks[0], dks[1])
        dv_ref[...] = jnp.where(_own_lanes(lane, 0), dvs[0], dvs[1])

    bspec = pl.BlockSpec((blk, LANES), lambda h, j: (j, h))
    qspec = pl.BlockSpec((T, LANES), lambda h, j: (0, Q_TILE0 + h))
    kspec = pl.BlockSpec((blk, LANES), lambda h, j: (j, K_TILE0 + h))
    vspec = pl.BlockSpec((blk, LANES), lambda h, j: (j, V_TILE0 + h))
    dospec = pl.BlockSpec((T, LANES), lambda h, j: (0, O_TILE0 + h))
    cspec = pl.BlockSpec((1, blk, 2), lambda h, j: (h, j, 0))
    rspec = pl.BlockSpec((1, nb, 2, blk), lambda h, j: (h, 0, 0, 0))
    dqspec = pl.BlockSpec((1, nb, LANES, blk), lambda h, j: (h, 0, 0, 0))
    return call_hosting(body, hosted, name=name, grid=(FOX_PAIRS, nb),
                        in_specs=[qspec, kspec, vspec, dospec, cspec, rspec, rspec, rspec],
                        out_specs=[bspec, bspec, cspec, dqspec, rspec],
                        out_shape=[SDS((T, FOX_WIDTH), F32), SDS((T, FOX_WIDTH), F32), SDS((FOX_PAIRS, T, 2), F32),
                                   SDS((FOX_PAIRS, nb, LANES, blk), F32), SDS((FOX_PAIRS, nb, 2, blk), F32)],
                        inputs=[z, z, z, dycat, f_col, f_row, lse_row, dd_row], aliases={})


def _pairs_col(a, T):
    return jnp.transpose(a[:, :FOX_HEADS].reshape(T, FOX_PAIRS, 2), (1, 0, 2))


def _col_to_row(a, T):
    blk = _fox_block(T)
    return jnp.transpose(a.reshape(FOX_PAIRS, T // blk, blk, 2), (0, 1, 3, 2))


def _row_to_col(a, T):
    return jnp.transpose(a, (0, 1, 3, 2)).reshape(FOX_PAIRS, T, 2)


def _pairs_to_lanes(a, T):
    flat = jnp.transpose(a, (1, 0, 2)).reshape(T, FOX_HEADS)
    return jnp.pad(flat, ((0, 0), (0, LANES - FOX_HEADS)))


def _pool_counts(t0, n, w):
    t = (t0 + lax.broadcasted_iota(jnp.int32, (n, 1), 0)).astype(F32)
    return jnp.minimum(t + 1.0, float(w))


def pool_window(x, *, adjoint, name, in_col=0, into=None, out_col=0, out_dtype=F32):
    T, C = x.shape[0], len(POOL_WINDOWS) * POOL_GROUP_DIM
    tr = _pick(T, 512)
    nt = T // tr
    hb = tr // POOL_HALO
    n = tr + POOL_HALO

    def body(x_ref, h_ref, *rest):
        o_ref = rest[-1]
        i = pl.program_id(0)
        cur = x_ref[...]
        if adjoint:
            halo = h_ref[...] * jnp.where(i < nt - 1, 1.0, 0.0)
            ext = jnp.concatenate([cur, halo], axis=0)
            t0 = i * tr
        else:
            halo = h_ref[...] * jnp.where(i > 0, 1.0, 0.0)
            ext = jnp.concatenate([halo, cur], axis=0)
            t0 = i * tr - POOL_HALO
        sums = {}
        for g, w in enumerate(POOL_WINDOWS):
            ls = slice(g * POOL_GROUP_DIM, (g + 1) * POOL_GROUP_DIM)
            s = ext[:, ls]
            if adjoint:
                s = s / _pool_counts(t0, n, w)
            d = 1
            while d < w:
                s = s + pltpu.roll(s, (n - d) if adjoint else d, 0)
                d *= 2
            if adjoint:
                o_ref[:, ls] = (s[0:tr, :] - cur[:, ls]).astype(o_ref.dtype)
            else:
                o_ref[:, ls] = (s[POOL_HALO:n, :] / _pool_counts(i * tr, tr, w) - cur[:, ls]).astype(o_ref.dtype)

    if adjoint:
        halo_spec = pl.BlockSpec((POOL_HALO, C), lambda i: (jnp.minimum((i + 1) * hb, T // POOL_HALO - 1), in_col))
    else:
        halo_spec = pl.BlockSpec((POOL_HALO, C), lambda i: (jnp.maximum(i * hb - 1, 0), in_col))
    x_spec = pl.BlockSpec((tr, C), lambda i: (i, in_col))
    if into is None:
        return pl.pallas_call(body, name=name, grid=(nt,), in_specs=[x_spec, halo_spec], out_specs=_row(tr, C),
                              out_shape=SDS((T, C), out_dtype), compiler_params=_cp("parallel"))(x, x)
    return pl.pallas_call(body, name=name, grid=(nt,), in_specs=[x_spec, halo_spec, ANY],
                          out_specs=pl.BlockSpec((tr, C), lambda i: (i, out_col)), out_shape=SDS(into.shape, into.dtype),
                          input_output_aliases={2: 0}, compiler_params=_cp("parallel"))(x, x, into)


def colscale_fwd(a, s, *, out_cols, name):
    T, C = a.shape
    tr = _pick(T, 512)

    def body(a_ref, s_ref, o_ref):
        o_ref[...] = (a_ref[...] * s_ref[...]).astype(o_ref.dtype)

    return pl.pallas_call(body, name=name, grid=(T // tr,), in_specs=[_row(tr, C), _full((1, C))], out_specs=_row(tr, C),
                          out_shape=SDS((T, out_cols), MXU_DTYPE), compiler_params=_cp("parallel"))(a, s)


def colscale_bwd(a, s, dy, *, name):
    T, C = a.shape
    tr = _pick(T, 512)

    def body(a_ref, s_ref, dy_ref, da_ref, ds_ref):
        dyv = dy_ref[...]
        da_ref[...] = (dyv * s_ref[...]).astype(da_ref.dtype)

        @pl.when(pl.program_id(0) == 0)
        def _():
            ds_ref[...] = jnp.zeros_like(ds_ref)

        ds_ref[...] += jnp.sum(dyv * a_ref[...], axis=0, keepdims=True)

    return pl.pallas_call(body, name=name, grid=(T // tr,), in_specs=[_row(tr, C), _full((1, C)), _row(tr, C)],
                          out_specs=[_row(tr, C), _full((1, C))], out_shape=[SDS((T, C), MXU_DTYPE), SDS((1, C), F32)],
                          compiler_params=_cp("arbitrary"))(a, s, dy)


SGU_ROWS = 512


def _sgu_norm(v, ln_g, ln_b):
    vg = jax.nn.gelu(v)
    xc = vg - jnp.mean(vg, axis=-1, keepdims=True)
    r = lax.rsqrt(jnp.mean(xc * xc, axis=-1, keepdims=True) + EPS)
    xh = xc * r
    return xh * ln_g + ln_b, xh, r


def _rowc(tr, c, cb):
    return pl.BlockSpec((tr, c), lambda i: (i, cb))


def sgu_fwd(z, ln_g, ln_b, ws, bst, ycat, *, name):
    T, C = z.shape[0], SGU_GROUPS * SGU_GROUP_DIM
    tr = _pick(T, SGU_ROWS)

    def body(u_ref, v_ref, g_ref, b_ref, ws_ref, bst_ref, prev_ref, o_ref):
        vn, _, _ = _sgu_norm(v_ref[...], g_ref[...], b_ref[...])
        vn = vn.astype(MXU_DTYPE)
        ug = jax.nn.gelu(u_ref[...])
        for g in range(SGU_GROUPS):
            w = ws_ref[g].astype(MXU_DTYPE)
            bias = bst_ref[:, g:g + 1]
            for c in range(tr // CHUNK):
                rs = slice(c * CHUNK, (c + 1) * CHUNK)
                ls = slice(g * SGU_GROUP_DIM, (g + 1) * SGU_GROUP_DIM)
                mixed = jnp.dot(w, vn[rs, ls], preferred_element_type=F32) + bias
                o_ref[rs, ls] = (ug[rs, ls] * mixed).astype(o_ref.dtype)

    return pl.pallas_call(body, name=name, grid=(T // tr,),
                          in_specs=[_rowc(tr, C, 0), _rowc(tr, C, 1), _full((1, C)), _full((1, C)),
                                    _full((SGU_GROUPS, CHUNK, CHUNK)), _full((CHUNK, SGU_GROUPS)), ANY],
                          out_specs=_rowc(tr, C, 1), out_shape=SDS(ycat.shape, ycat.dtype), input_output_aliases={6: 0},
                          compiler_params=_cp("parallel"))(z, z, ln_g, ln_b, ws, bst, ycat)


def sgu_bwd(z, ln_g, ln_b, ws, wst, bst, dycat, *, out_cols, name):
    T, C = z.shape[0], SGU_GROUPS * SGU_GROUP_DIM
    tr = _pick(T, SGU_ROWS)

    def body(u_ref, v_ref, g_ref, b_ref, ws_ref, wst_ref, bst_ref, dy_ref,
             duv_ref, dg_ref, db_ref, dws_ref, dbst_ref, dvn_ref):
        du_ref = duv_ref.at[:, 0:C]
        dv_ref = duv_ref.at[:, C:2 * C]
        @pl.when(pl.program_id(0) == 0)
        def _():
            dg_ref[...] = jnp.zeros_like(dg_ref)
            db_ref[...] = jnp.zeros_like(db_ref)
            dws_ref[...] = jnp.zeros_like(dws_ref)
            dbst_ref[...] = jnp.zeros_like(dbst_ref)

        uv = u_ref[...]
        vv = v_ref[...]
        vn, xh, r = _sgu_norm(vv, g_ref[...], b_ref[...])
        vn = vn.astype(MXU_DTYPE)
        ug = jax.nn.gelu(uv)
        dyv = dy_ref[...]
        for g in range(SGU_GROUPS):
            w = ws_ref[g].astype(MXU_DTYPE)
            wt = wst_ref[g].astype(MXU_DTYPE)
            bias = bst_ref[:, g:g + 1]
            dw = jnp.zeros((CHUNK, CHUNK), F32)
            dbias = jnp.zeros((CHUNK, 1), F32)
            for c in range(tr // CHUNK):
                rs = slice(c * CHUNK, (c + 1) * CHUNK)
                ls = slice(g * SGU_GROUP_DIM, (g + 1) * SGU_GROUP_DIM)
                vblk = vn[rs, ls]
                mixed = jnp.dot(w, vblk, preferred_element_type=F32) + bias
                dyb = dyv[rs, ls]
                du_ref[rs, ls] = (dyb * mixed * _gelu_grad(uv[rs, ls])).astype(du_ref.dtype)
                dmixed = dyb * ug[rs, ls]
                dbias = dbias + jnp.sum(dmixed, axis=-1, keepdims=True)
                dmb = dmixed.astype(MXU_DTYPE)
                dw = dw + lax.dot_general(dmb, vblk, _NT, preferred_element_type=F32)
                dvn_ref[rs, ls] = jnp.dot(wt, dmb, preferred_element_type=F32)
            dws_ref[g] += dw
            dbst_ref[:, g:g + 1] += dbias
        dvn = dvn_ref[...]
        dg_ref[...] += jnp.sum(dvn * xh, axis=0, keepdims=True)
        db_ref[...] += jnp.sum(dvn, axis=0, keepdims=True)
        dxh = dvn * g_ref[...]
        dvg = r * (dxh - jnp.mean(dxh, axis=-1, keepdims=True) - xh * jnp.mean(dxh * xh, axis=-1, keepdims=True))
        dv_ref[...] = (dvg * _gelu_grad(vv)).astype(dv_ref.dtype)

    wspec = _full((SGU_GROUPS, CHUNK, CHUNK))
    return pl.pallas_call(body, name=name, grid=(T // tr,),
                          in_specs=[_rowc(tr, C, 0), _rowc(tr, C, 1), _full((1, C)), _full((1, C)), wspec, wspec,
                                    _full((CHUNK, SGU_GROUPS)), _rowc(tr, C, 1)],
                          out_specs=[_rowc(tr, 2 * C, 0), _full((1, C)), _full((1, C)), wspec,
                                     _full((CHUNK, SGU_GROUPS))],
                          out_shape=[SDS((T, out_cols), MXU_DTYPE), SDS((1, C), F32), SDS((1, C), F32),
                                     SDS((SGU_GROUPS, CHUNK, CHUNK), F32), SDS((CHUNK, SGU_GROUPS), F32)],
                          scratch_shapes=[pltpu.VMEM((tr, C), F32)],
                          compiler_params=_cp("arbitrary"))(z, z, ln_g, ln_b, ws, wst, bst, dycat)


def adamw(w, g, m, v, *, name):
    R, C = w.shape
    tr = _pick(R, 512)
    c1 = 1.0 - ADAM_B1 ** ADAM_STEP
    c2 = 1.0 - ADAM_B2 ** ADAM_STEP

    def body(w_ref, g_ref, m_ref, v_ref, d_ref, nm_ref, nv_ref):
        gv = g_ref[...]
        nm = ADAM_B1 * m_ref[...] + (1.0 - ADAM_B1) * gv
        nv = ADAM_B2 * v_ref[...] + (1.0 - ADAM_B2) * (gv * gv)
        nm_ref[...] = nm
        nv_ref[...] = nv
        d_ref[...] = -ADAM_LR * ((nm / c1) / (jnp.sqrt(nv / c2) + ADAM_EPS) + ADAM_WD * w_ref[...])

    spec = _row(tr, C)
    return pl.pallas_call(body, name=name, grid=(R // tr,), in_specs=[spec] * 4, out_specs=[spec] * 3,
                          out_shape=[SDS((R, C), F32)] * 3, compiler_params=_cp("parallel"))(w, g, m, v)


ANY = pl.BlockSpec(memory_space=pl.ANY)


def _coords():
    return lax.axis_index("x"), lax.axis_index("y"), lax.axis_index("c")


def _other_chips(x, y):
    return [(1 - x, y), (x, 1 - y), (1 - x, 1 - y)]


def _remote(src, dst, send_sems, recv_sems, k, dev):
    return pltpu.make_async_remote_copy(src_ref=src, dst_ref=dst, send_sem=send_sems.at[k], recv_sem=recv_sems.at[k],
                                        device_id=dev, device_id_type=MESH)


LOCAL_CHUNKS = 8


class Exchange:
    def __init__(self, ins, out_shapes, scratch, start, wait):
        self.ins, self.out_shapes, self.scratch, self.start, self.wait = list(ins), list(out_shapes), list(scratch), start, wait


def run_exchange(ex, *, name):
    ni, no = len(ex.ins), len(ex.out_shapes)

    def body(*refs):
        parts = refs[:ni], refs[ni:ni + no], refs[ni + no:]
        ex.start(*parts)
        ex.wait(*parts)

    return pl.pallas_call(body, name=name, in_specs=[ANY] * ni, out_specs=[ANY] * no, out_shape=ex.out_shapes,
                          scratch_shapes=ex.scratch)(*ex.ins)


def call_hosting(body, ex, *, name, grid, in_specs, out_specs, out_shape, inputs, aliases, scratch=()):
    n_in, n_out, ni, no, ns = len(inputs), len(out_shape), len(ex.ins), len(ex.out_shapes), len(scratch)
    outs_at = n_in + ni
    scr_at = outs_at + n_out + no

    def wrapped(*refs):
        own = refs[:n_in] + refs[outs_at:outs_at + n_out] + refs[scr_at:scr_at + ns]
        parts = refs[n_in:outs_at], refs[outs_at + n_out:scr_at], refs[scr_at + ns:]
        ids = [pl.program_id(d) for d in range(len(grid))]
        first = functools.reduce(jnp.logical_and, [i == 0 for i in ids])
        last = functools.reduce(jnp.logical_and, [i == g - 1 for i, g in zip(ids, grid)])

        @pl.when(first)
        def _():
            ex.start(*parts)

        body(*own)

        @pl.when(last)
        def _():
            ex.wait(*parts)

    outs = pl.pallas_call(
        wrapped, name=name, grid=grid, in_specs=list(in_specs) + [ANY] * ni, out_specs=list(out_specs) + [ANY] * no,
        out_shape=list(out_shape) + ex.out_shapes, input_output_aliases=aliases,
        scratch_shapes=list(scratch) + ex.scratch,
        compiler_params=_cp(*["arbitrary"] * len(grid)))(*inputs, *ex.ins)
    return outs[:n_out], outs[n_out:]


def allgather_ici_exchange(shards, whole=()):
    na, n_all = len(shards), len(shards) + len(whole)
    arrays = list(shards) + list(whole)

    def copies(s_refs, o_refs, sems):
        send_sems, recv_sems, _ = sems
        x, y, c = _coords()
        j = 2 * x + y
        out = []
        for a in range(n_all):
            if a < na:
                half = shards[a].shape[0] // 2
                part = (pl.ds(c * half, half),)
            else:
                part = ()
            for k, (px, py) in enumerate(_other_chips(x, y)):
                send = _remote(s_refs[a].at[part] if part else s_refs[a], o_refs[a].at[(j,) + part], send_sems, recv_sems,
                               3 * a + k, (px, py, c))
                rows = o_refs[a].at[(2 * px + py,) + part]
                out.append((send, _remote(rows, rows, send_sems, recv_sems, 3 * a + k, (px, py, c))))
        return out

    def start(s_refs, o_refs, sems):
        x, y, c = _coords()
        j = 2 * x + y
        for a in range(n_all):
            chunks = LOCAL_CHUNKS if a < na else 1
            chunk = arrays[a].shape[0] // chunks
            for q in range(chunks):
                rows = pl.ds(q * chunk, chunk)
                pltpu.make_async_copy(s_refs[a].at[rows], o_refs[a].at[j, rows], sems[2].at[a]).start()
        for send, _ in copies(s_refs, o_refs, sems):
            send.start()

    def wait(s_refs, o_refs, sems):
        x, y, c = _coords()
        j = 2 * x + y
        for send, arrival in copies(s_refs, o_refs, sems):
            arrival.wait_recv()
            send.wait_send()
        for a in range(n_all):
            pltpu.make_async_copy(s_refs[a], o_refs[a].at[j], sems[2].at[a]).wait()

    return Exchange(arrays, [SDS((N_CHIPS,) + s.shape, s.dtype) for s in arrays],
                    [pltpu.SemaphoreType.DMA((3 * n_all,)), pltpu.SemaphoreType.DMA((3 * n_all,)),
                     pltpu.SemaphoreType.DMA((n_all,))], start, wait)


def allgather_forward(gathered, *, name):
    na = len(gathered)

    def body(*refs):
        o_refs = refs[na:2 * na]
        send_sems, recv_sems = refs[2 * na:]
        x, y, c = _coords()
        sibling = (x, y, 1 - c)
        cps = []
        for a in range(na):
            half = gathered[a].shape[1] // 2
            for k, (px, py) in enumerate(_other_chips(x, y)):
                mine = o_refs[a].at[2 * px + py, pl.ds(c * half, half)]
                theirs = o_refs[a].at[2 * px + py, pl.ds((1 - c) * half, half)]
                cps.append((_remote(mine, mine, send_sems, recv_sems, 3 * a + k, sibling),
                            _remote(theirs, theirs, send_sems, recv_sems, 3 * a + k, sibling)))
        for send, _ in cps:
            send.start()
        for send, arrival in cps:
            send.wait_send()
            arrival.wait_recv()

    return pl.pallas_call(body, name=name, in_specs=[ANY] * na, out_specs=[ANY] * na,
                          out_shape=[SDS(g.shape, g.dtype) for g in gathered],
                          input_output_aliases={a: a for a in range(na)},
                          scratch_shapes=[pltpu.SemaphoreType.DMA((3 * na,)), pltpu.SemaphoreType.DMA((3 * na,))])(*gathered)


def swap_halves_exchange(gs):
    na = len(gs)

    def copies(g_refs, o_refs, sems):
        x, y, c = _coords()
        out = []
        for a in range(na):
            half = gs[a].shape[1] // 2
            out.append(_remote(g_refs[a].at[:, pl.ds((1 - c) * half, half), :], o_refs[a], sems[0], sems[1], a,
                               (x, y, 1 - c)))
        return out

    def start(g_refs, o_refs, sems):
        for cp in copies(g_refs, o_refs, sems):
            cp.start()

    def wait(g_refs, o_refs, sems):
        for cp in copies(g_refs, o_refs, sems):
            cp.wait()

    return Exchange(gs, [SDS((g.shape[0], g.shape[1] // 2, g.shape[2]), g.dtype) for g in gs],
                    [pltpu.SemaphoreType.DMA((na,)), pltpu.SemaphoreType.DMA((na,))], start, wait)


def chip_partials_exchange(pbs):
    na = len(pbs)

    def copies(p_refs, o_refs, sems):
        x, y, c = _coords()
        out = []
        for a in range(na):
            for k, (px, py) in enumerate(_other_chips(x, y)):
                out.append(_remote(p_refs[a].at[2 * px + py], o_refs[a].at[k], sems[0], sems[1], 3 * a + k, (px, py, c)))
        return out

    def start(p_refs, o_refs, sems):
        for cp in copies(p_refs, o_refs, sems):
            cp.start()

    def wait(p_refs, o_refs, sems):
        for cp in copies(p_refs, o_refs, sems):
            cp.wait()

    return Exchange(pbs, [SDS((3,) + p.shape[1:], p.dtype) for p in pbs],
                    [pltpu.SemaphoreType.DMA((3 * na,)), pltpu.SemaphoreType.DMA((3 * na,))], start, wait)


def add_sibling_half(g, land, c_idx, *, name):
    n, R, C = g.shape
    half = R // 2
    tr = _pick(half, 256)
    nt = half // tr

    def body(c_ref, g_ref, l_ref, of_ref, ob_ref):
        s = g_ref[...] + l_ref[...].astype(F32)
        of_ref[...] = s
        ob_ref[...] = s.astype(ob_ref.dtype)

    blk = pl.BlockSpec((1, tr, C), lambda s, i, c_ref: (s, i, 0))
    gblk = pl.BlockSpec((1, tr, C), lambda s, i, c_ref: (s, c_ref[0] * nt + i, 0))
    return pl.pallas_call(
        body, name=name,
        grid_spec=pltpu.PrefetchScalarGridSpec(num_scalar_prefetch=1, grid=(n, nt), in_specs=[gblk, blk],
                                               out_specs=[blk, blk]),
        out_shape=[SDS((n, half, C), F32), SDS((n, half, C), WIRE_DTYPE)],
        compiler_params=_cp("parallel", "parallel"))(c_idx, g, land)


def add_chip_partials(pf, rb, jc_idx, *, name):
    n, H, C = pf.shape
    tr = _pick(H, 256)

    def body(jc_ref, p_ref, r_ref, o_ref):
        s = p_ref[0]
        for k in range(3):
            s = s + r_ref[k].astype(F32)
        o_ref[...] = s

    pblk = pl.BlockSpec((1, tr, C), lambda i, jc_ref: (jc_ref[0], i, 0))
    rblk = pl.BlockSpec((3, tr, C), lambda i, jc_ref: (0, i, 0))
    oblk = pl.BlockSpec((None, tr, C), lambda i, jc_ref: (jc_ref[1], i, 0))
    return pl.pallas_call(
        body, name=name,
        grid_spec=pltpu.PrefetchScalarGridSpec(num_scalar_prefetch=1, grid=(H // tr,), in_specs=[pblk, rblk],
                                               out_specs=oblk),
        out_shape=SDS((2, H, C), F32), compiler_params=_cp("parallel"))(jc_idx, pf, rb)


def join_sibling_halves(bufs, *, name):
    na = len(bufs)

    def body(*refs):
        o_refs = refs[na:2 * na]
        send_sems, recv_sems = refs[2 * na:]
        x, y, c = _coords()
        cps = [_remote(o_refs[a].at[c], o_refs[a].at[c], send_sems, recv_sems, a, (x, y, 1 - c)) for a in range(na)]
        for cp in cps:
            cp.start()
        for a in range(na):
            cps[a].wait_send()
            _remote(o_refs[a].at[1 - c], o_refs[a].at[1 - c], send_sems, recv_sems, a, (x, y, 1 - c)).wait_recv()

    return pl.pallas_call(body, name=name, in_specs=[ANY] * na, out_specs=[ANY] * na,
                          out_shape=[SDS(b.shape, b.dtype) for b in bufs],
                          input_output_aliases={a: a for a in range(na)},
                          scratch_shapes=[pltpu.SemaphoreType.DMA((na,)), pltpu.SemaphoreType.DMA((na,))])(*bufs)


def allreduce_pieces(v, *, name):
    n, P, C = v.shape
    assert n == N_DEV

    def body(v_ref, o_ref, land_ref, mine_ref, send1, recv1, send2, recv2):
        x, y, c = _coords()
        me = 4 * x + 2 * y + c
        peers = []
        for m in range(1, N_DEV):
            px = (1 - x) if m & 4 else x
            py = (1 - y) if m & 2 else y
            pc = (1 - c) if m & 1 else c
            peers.append((m - 1, 4 * px + 2 * py + pc, (px, py, pc)))
        scatter = [(_remote(v_ref.at[lin], land_ref.at[me], send1, recv1, k, dev),
                    _remote(land_ref.at[lin], land_ref.at[lin], send1, recv1, k, dev)) for k, lin, dev in peers]
        for send, _ in scatter:
            send.start()
        land_ref[pl.ds(me, 1)] = v_ref[pl.ds(me, 1)]
        for _, arrival in scatter:
            arrival.wait_recv()
        s = land_ref[0]
        for d in range(1, N_DEV):
            s = s + land_ref[d]
        mine_ref[...] = s
        gather = [(_remote(mine_ref, o_ref.at[me], send2, recv2, k, dev),
                   _remote(o_ref.at[lin], o_ref.at[lin], send2, recv2, k, dev)) for k, lin, dev in peers]
        for send, _ in gather:
            send.start()
        o_ref[pl.ds(me, 1)] = s[None]
        for send, arrival in scatter + gather:
            send.wait_send()
        for _, arrival in gather:
            arrival.wait_recv()

    vmem = pl.BlockSpec(memory_space=pltpu.VMEM)
    sems = [pltpu.SemaphoreType.DMA((N_DEV - 1,))] * 4
    return pl.pallas_call(body, name=name, in_specs=[vmem], out_specs=vmem, out_shape=SDS((n, P, C), F32),
                          scratch_shapes=[pltpu.VMEM((n, P, C), F32), pltpu.VMEM((P, C), F32)] + sems)(v)


BIG_SEGS = (
    ("w_in_even", (1024, 514), 1),
    ("s5_w_glu", (128, 512), 0),
    ("w_out_even", (256, 1024), 0),
    ("w_in_odd", (1024, 384), 1),
    ("w_out_odd", (256, 1024), 0),
    ("mlp_w1", (2, 1024, 1024), 2),
    ("mlp_w2", (2, 1024, 1024), 1),
)
BIG_NAMES = tuple(n for n, _, _ in BIG_SEGS)
EARLY_NAMES = ("w_in_even", "s5_w_glu")
LATE_NAMES = ("w_out_even", "w_in_odd", "w_out_odd", "mlp_w1", "mlp_w2")
REDUCED_EARLY = ("s5_w_glu", "w_out_even", "w_in_odd", "w_out_odd", "mlp_w1", "mlp_w2")
SHARDED_SMALL = ("pool_scale", "sgu_ln_g", "sgu_ln_b")
SMALL_SEGS = (
    ("mix_pre_g", (2, 1024)), ("mix_post_g", (2, 1024)), ("mlp_pre_g", (2, 1024)), ("mlp_post_g", (2, 1024)),
    ("s5_lam_re", (1, 32, 64)), ("s5_lam_im", (1, 32, 64)), ("s5_log_dt", (1, 32)),
    ("s5_b_re", (1, 32, 64, 16)), ("s5_b_im", (1, 32, 64, 16)), ("s5_c_re", (1, 32, 16, 64)), ("s5_c_im", (1, 32, 16, 64)),
    ("s5_d", (1, 512)), ("fox_b_f", (1, 8)), ("pool_w", (1, 4, 128, 128)), ("sgu_w_s", (1, 4, 128, 128)),
    ("sgu_b_s", (1, 4, 128)),
)
REDUCED_SEGS = SMALL_SEGS + tuple((n, (1, 512)) for n in SHARDED_SMALL) + (("loss", (1, 1)),)


def _cols_from_chips(g):
    n, R, C = g.shape
    return jnp.transpose(g, (1, 0, 2)).reshape(R, n * C)


def _chips_from_cols(m):
    R, C4 = m.shape
    return jnp.transpose(m.reshape(R, N_CHIPS, C4 // N_CHIPS), (1, 0, 2))


MLP_SHARD = 1024


def _w1_cols(l):
    def spec(tm, tn, tk):
        per = MLP_SHARD // tn
        return pl.BlockSpec((None, tk, tn), lambda i, j, k: (j // per, l * (MLP_SHARD // tk) + k, j % per))
    return spec


def _w1_rows_t(l):
    def spec(tm, tn, tk):
        if tk == N_CHIPS * MLP_SHARD:
            return pl.BlockSpec((N_CHIPS, tn, MLP_SHARD), lambda i, j, k: (0, l * (MLP_SHARD // tn) + j, 0))
        per = MLP_SHARD // tk
        return pl.BlockSpec((None, tn, tk), lambda i, j, k: (k // per, l * (MLP_SHARD // tn) + j, k % per))
    return spec


def _w2_rows(l):
    def spec(tm, tn, tk):
        if tk == N_CHIPS * MLP_SHARD:
            return pl.BlockSpec((N_CHIPS, MLP_SHARD, tn), lambda i, j, k: (0, l, j))
        per = MLP_SHARD // tk
        return pl.BlockSpec((None, tk, tn), lambda i, j, k: (k // per, l * per + k % per, j))
    return spec


def _w2_rows_t(l):
    def spec(tm, tn, tk):
        per = MLP_SHARD // tn
        return pl.BlockSpec((None, tn, tk), lambda i, j, k: (j // per, l * per + j % per, k))
    return spec


def _dw1_out(l):
    def spec(tm, tn, tk):
        per = MLP_SHARD // tn
        return pl.BlockSpec((None, tm, tn), lambda i, j, k: (j // per, l * (MLP_SHARD // tm) + i, j % per))
    return spec


def _dw2_out(l):
    def spec(tm, tn, tk):
        per = MLP_SHARD // tm
        return pl.BlockSpec((None, tm, tn), lambda i, j, k: (i // per, l * per + i % per, j))
    return spec


def _pack_vec(d, segs, rows_multiple):
    flat = jnp.concatenate([d[n].reshape(-1) for n, _ in segs])
    rows = -(-flat.shape[0] // LANES)
    rows = -(-rows // rows_multiple) * rows_multiple
    return jnp.pad(flat, (0, rows * LANES - flat.shape[0])).reshape(rows, LANES)


def _unpack_vec(v, segs):
    flat, out, r = v.reshape(-1), {}, 0
    for n, shape in segs:
        k = math.prod(shape)
        out[n] = flat[r:r + k].reshape(shape)
        r += k
    return out


def _block_diag(blocks):
    G, a, b = blocks.shape
    eye = jnp.eye(G, dtype=blocks.dtype)
    return (eye[:, None, :, None] * blocks[:, :, None, :]).reshape(G * a, G * b)


def _diag_blocks(m, G):
    a, b = m.shape[0] // G, m.shape[1] // G
    return jnp.stack([m[g * a:(g + 1) * a, g * b:(g + 1) * b] for g in range(G)])


def _sqrelu_epi(acc):
    r = jnp.maximum(acc, 0.0)
    return acc, r * r


def _sqrelu_bwd_epi(acc, a):
    return (acc * (2.0 * jnp.maximum(a.astype(F32), 0.0)),)


def _mlp_fwd(h, g1, g2, l, tag):
    T, D = h.shape
    a, s = matmul(h, g1, name=f"{tag}_up", mnk=(T, D_FF, D), b_spec=_w1_cols(l), epi=_sqrelu_epi,
                  out_dtypes=(MXU_DTYPE, MXU_DTYPE))
    m = matmul(s, g2, name=f"{tag}_down", mnk=(T, D, D_FF), b_spec=_w2_rows(l))
    return m, (h, a, s)


def _mlp_bwd(saved, dm, g1, g2, l, dg1, dg2, tag):
    h, a, s = saved
    T, D = h.shape
    gshape = (N_CHIPS, 2 * MLP_SHARD, MLP_SHARD)
    da = matmul(dm, g2, tb=True, name=f"{tag}_down_dx", mnk=(T, D_FF, D), b_spec=_w2_rows_t(l),
                epi=_sqrelu_bwd_epi, epi_in=(a,), out_dtype=MXU_DTYPE)
    dg2 = matmul(s, dm, ta=True, name=f"{tag}_down_dw", tm=MLP_SHARD, o_spec=_dw2_out(l), o_shape=gshape, prev=dg2)
    dh = matmul(da, g1, tb=True, name=f"{tag}_up_dx", mnk=(T, D, D_FF), b_spec=_w1_rows_t(l))
    dg1 = matmul(h, da, ta=True, name=f"{tag}_up_dw", o_spec=_dw1_out(l), o_shape=gshape, prev=dg1)
    return dh, dg1, dg2


def kernel(x, mix_pre_g, mix_post_g, mlp_pre_g, mlp_post_g, w_in_even, s5_lam_re, s5_lam_im, s5_log_dt, s5_b_re, s5_b_im, s5_c_re, s5_c_im, s5_d, s5_w_glu, fox_b_f, w_out_even, w_in_odd, pool_w, pool_scale, sgu_ln_g, sgu_ln_b, sgu_w_s, sgu_b_s, w_out_odd, mlp_w1, mlp_w2, loss_target, m_mix_pre_g, m_mix_post_g, m_mlp_pre_g, m_mlp_post_g, m_w_in_even, m_s5_lam_re, m_s5_lam_im, m_s5_log_dt, m_s5_b_re, m_s5_b_im, m_s5_c_re, m_s5_c_im, m_s5_d, m_s5_w_glu, m_fox_b_f, m_w_out_even, m_w_in_odd, m_pool_w, m_pool_scale, m_sgu_ln_g, m_sgu_ln_b, m_sgu_w_s, m_sgu_b_s, m_w_out_odd, m_mlp_w1, m_mlp_w2, v_mix_pre_g, v_mix_post_g, v_mlp_pre_g, v_mlp_post_g, v_w_in_even, v_s5_lam_re, v_s5_lam_im, v_s5_log_dt, v_s5_b_re, v_s5_b_im, v_s5_c_re, v_s5_c_im, v_s5_d, v_s5_w_glu, v_fox_b_f, v_w_out_even, v_w_in_odd, v_pool_w, v_pool_scale, v_sgu_ln_g, v_sgu_ln_b, v_sgu_w_s, v_sgu_b_s, v_w_out_odd, v_mlp_w1, v_mlp_w2):
    names = [n for n, _ in SMALL_SEGS] + [n for n, _, _ in BIG_SEGS] + list(SHARDED_SMALL)
    env = dict(locals())
    W = {n: env[n] for n in names}
    M = {n: env["m_" + n] for n in names}
    V = {n: env["v_" + n] for n in names}

    def shard(n):
        return W[n].reshape(-1, W[n].shape[-1]).astype(WIRE_DTYPE)

    small = jnp.pad(jnp.concatenate([W[n] for n in SHARDED_SMALL]), ((0, SUBLANES - len(SHARDED_SMALL)), (0, 0)))
    loss8, dx0, halves, local_small = _local_step(x[0], loss_target[0], {n: W[n] for n, _ in SMALL_SEGS},
                                                  [shard(n) for n in EARLY_NAMES], [shard(n) for n in LATE_NAMES], small)
    return _reduce_and_update(W, M, V, loss8, dx0, halves, local_small)


def _reduce_to_my_half(gs, names, tag, carry_swap=None, carry_ici=None):
    cx, cy, cc = _coords()
    c_idx = cc.reshape(1).astype(jnp.int32)
    jc_idx = jnp.stack([2 * cx + cy, cc]).astype(jnp.int32)
    swap = swap_halves_exchange(gs)
    from_sibling = carry_swap(swap) if carry_swap else run_exchange(swap, name=f"{tag}_to_sibling")
    sums = [add_sibling_half(g, l, c_idx, name=f"{tag}_chip_sum_{n}") for n, g, l in zip(names, gs, from_sibling)]
    send = chip_partials_exchange([pb for _, pb in sums])
    from_chips = carry_ici(send) if carry_ici else run_exchange(send, name=f"{tag}_to_chips")
    return [add_chip_partials(pf, r, jc_idx, name=f"{tag}_sum_{n}") for n, (pf, _), r in zip(names, sums, from_chips)]


def _local_step(x0, target, P, early_shards, late_shards, small_shard):
    T = x0.shape[0]
    mix_pre_g, mix_post_g, mlp_pre_g, mlp_post_g = P["mix_pre_g"], P["mix_post_g"], P["mlp_pre_g"], P["mlp_post_g"]
    s5_lam_re, s5_lam_im, s5_log_dt = P["s5_lam_re"], P["s5_lam_im"], P["s5_log_dt"]
    s5_b_re, s5_b_im, s5_c_re, s5_c_im, s5_d = P["s5_b_re"], P["s5_b_im"], P["s5_c_re"], P["s5_c_im"], P["s5_d"]
    fox_b_f, pool_w, sgu_w_s, sgu_b_s = P["fox_b_f"], P["pool_w"], P["sgu_w_s"], P["sgu_b_s"]

    def gain(a, l):
        return a[l][None, :]

    lr = s5_lam_re[0].reshape(1, S5_LANES)
    li = s5_lam_im[0].reshape(1, S5_LANES)
    ldt = jnp.repeat(s5_log_dt[0], S5_STATE).reshape(1, S5_LANES)
    btr = s5_b_re[0].reshape(S5_LANES, S5_GROUP).T
    bti = s5_b_im[0].reshape(S5_LANES, S5_GROUP).T
    tf_re, tf_im, tb_re, tb_im, bbt_re, bbt_im = s5_disc_fwd(lr, li, ldt, btr, bti, name="s5_disc")
    same_group = (jnp.arange(S5_WIDTH)[:, None] // S5_GROUP) == (jnp.arange(S5_LANES)[None, :] // S5_STATE)
    b_bd = s5_interleave(jnp.where(same_group, jnp.tile(bbt_re, (S5_GROUPS, 1)), 0.0),
                         jnp.where(same_group, jnp.tile(bbt_im, (S5_GROUPS, 1)), 0.0), axis=1)
    cr2 = jnp.transpose(s5_c_re[0], (0, 2, 1)).reshape(S5_LANES, S5_GROUP)
    ci2 = jnp.transpose(s5_c_im[0], (0, 2, 1)).reshape(S5_LANES, S5_GROUP)
    c_bd = s5_interleave(jnp.where(same_group.T, jnp.tile(cr2, (1, S5_GROUPS)), 0.0),
                         -jnp.where(same_group.T, jnp.tile(ci2, (1, S5_GROUPS)), 0.0), axis=0)
    bf_pad = jnp.pad(fox_b_f, ((0, 0), (0, LANES - FOX_HEADS)))

    h1, early = rms_fwd(x0, gain(mix_pre_g, 0), allgather_ici_exchange(early_shards), name="l0_pre_norm")
    early = dict(zip(EARLY_NAMES, allgather_forward(early, name="allgather_early_weights")))
    w_in_e = jnp.pad(_cols_from_chips(early["w_in_even"]), ((0, 0), (0, EVEN_IN_PAD - EVEN_IN)))
    w_glu = early["s5_w_glu"].reshape(S5_WIDTH, S5_WIDTH)
    z = matmul(h1, w_in_e, name="l0_in_proj")
    s5_tiles = dict(tm=_pick(T, S5_NB), exact_tiles=True)
    xs = s5_scan(z, b_bd, tf_re, tf_im, reverse=False, name="s5_scan_fwd")
    yc = matmul(xs, c_bd, mnk=(T, S5_WIDTH, 2 * S5_NB), tn=S5_CB, a_spec=_lanes_of_chan, b_spec=_s5_c_block,
                name="s5_cx", **s5_tiles)
    yl, yg = s5_out_fwd(yc, z, s5_d, name="s5_out")
    gl = matmul(yg, w_glu, name="s5_glu_proj")
    ycat = glu_fwd(yg, gl, out_cols=D_MODEL, name="s5_glu")
    fgate = fox_gate_fwd(z, bf_pad, fl_col=FL_TILE, name="fox_gate")
    f_col = _pairs_col(fgate, T)
    f_row = _col_to_row(f_col, T)
    (ycat, lse_col), late = fox_fwd(z, f_col, f_row, ycat, allgather_ici_exchange(late_shards, [small_shard]),
                                    name="fox_fwd")
    small_all = late[-1]
    pool_scale_f, ln_g_f, ln_b_f = (small_all[:, i, :].reshape(1, N_CHIPS * LANES) for i in range(len(SHARDED_SMALL)))
    late = dict(zip(LATE_NAMES, allgather_forward(late[:-1], name="allgather_late_weights")))
    w_in_o = _cols_from_chips(late["w_in_odd"])
    w_in_o = jnp.concatenate([w_in_o[:, S5_WIDTH:], w_in_o[:, :S5_WIDTH]], axis=1)
    w_out_e = late["w_out_even"].reshape(D_MODEL, D_MODEL)
    w_out_o = late["w_out_odd"].reshape(D_MODEL, D_MODEL)
    g1, g2 = late["mlp_w1"], late["mlp_w2"]
    mo = matmul(ycat, w_out_e, name="l0_out_proj")
    x1, h2 = res_norm_fwd(x0, mo, gain(mix_post_g, 0), gain(mlp_pre_g, 0), name="l0_post_mlp0_pre_norm")
    m0, mlp0 = _mlp_fwd(h2, g1, g2, 0, "mlp0")

    x2, h3 = res_norm_fwd(x1, m0, gain(mlp_post_g, 0), gain(mix_pre_g, 1), name="mlp0_post_l1_pre_norm")
    z2 = matmul(h3, w_in_o, name="l1_in_proj")
    pooled = pool_window(z2, adjoint=False, in_col=POOL_COL, out_dtype=MXU_DTYPE, name="pool_fwd")
    pw_bd = _block_diag(pool_w[0])
    pw = matmul(pooled, pw_bd, name="pool_proj")
    ycat2 = colscale_fwd(pw, pool_scale_f, out_cols=D_MODEL, name="pool_scale")
    causal = jnp.tril(jnp.ones((CHUNK, CHUNK), dtype=bool))
    wsm = jnp.where(causal[None], sgu_w_s[0], 0.0)
    wsmt = jnp.transpose(wsm, (0, 2, 1))
    bst = sgu_b_s[0].T
    ycat2 = sgu_fwd(z2, ln_g_f, ln_b_f, wsm, bst, ycat2, name="sgu_fwd")
    mo2 = matmul(ycat2, w_out_o, name="l1_out_proj")
    x3, h4 = res_norm_fwd(x2, mo2, gain(mix_post_g, 1), gain(mlp_pre_g, 1), name="l1_post_mlp1_pre_norm")
    m1, mlp1 = _mlp_fwd(h4, g1, g2, 1, "mlp1")
    loss8, dx4 = res_norm_loss(x3, m1, gain(mlp_post_g, 1), target, name="mlp1_post_norm_loss")

    dm1, dg_mlp_post1 = rms_bwd(m1, gain(mlp_post_g, 1), dx4, None, name="mlp1_post_norm_bwd")
    dh4, dg1, dg2 = _mlp_bwd(mlp1, dm1, g1, g2, 1, None, None, "mlp1")
    dx3, dmo2, dg_mlp_pre1, dg_mix_post1 = norm_res_bwd(x3, gain(mlp_pre_g, 1), dh4, dx4, mo2, gain(mix_post_g, 1),
                                                        name="mlp1_pre_l1_post_norm_bwd")
    dycat2 = matmul(dmo2, w_out_o, tb=True, name="l1_out_proj_dx")
    dw_out_o = matmul(ycat2, dmo2, ta=True, name="l1_out_proj_dw")
    dpw, dpool_scale = colscale_bwd(pw, pool_scale_f, dycat2, name="pool_scale_bwd")
    dpooled = matmul(dpw, pw_bd, tb=True, name="pool_proj_dx")
    dpw_bd = matmul(pooled, dpw, ta=True, name="pool_proj_dw")
    dz2, dln_g, dln_b, dws, dbst = sgu_bwd(z2, ln_g_f, ln_b_f, wsm, wsmt, bst, dycat2, out_cols=3 * S5_WIDTH,
                                           name="sgu_bwd")
    dz2 = pool_window(dpooled, adjoint=True, into=dz2, out_col=POOL_COL, name="pool_bwd")
    dh3 = matmul(dz2, w_in_o, tb=True, name="l1_in_proj_dx")
    dw_in_o = matmul(h3, dz2, ta=True, name="l1_in_proj_dw")
    dw_in_o = jnp.concatenate([dw_in_o[:, 2 * S5_WIDTH:], dw_in_o[:, :2 * S5_WIDTH]], axis=1)
    dx2, dm0, dg_mix_pre1, dg_mlp_post0 = norm_res_bwd(x2, gain(mix_pre_g, 1), dh3, dx3, m0, gain(mlp_post_g, 0),
                                                       name="l1_pre_mlp0_post_norm_bwd")

    dh2, dg1, dg2 = _mlp_bwd(mlp0, dm0, g1, g2, 0, dg1, dg2, "mlp0")
    dx1, dmo, dg_mlp_pre0, dg_mix_post0 = norm_res_bwd(x1, gain(mlp_pre_g, 0), dh2, dx2, mo, gain(mix_post_g, 0),
                                                       name="mlp0_pre_l0_post_norm_bwd")
    dycat = matmul(dmo, w_out_e, tb=True, name="l0_out_proj_dx")
    dw_out_e = matmul(ycat, dmo, ta=True, name="l0_out_proj_dw")
    dyg_a, dgl = glu_bwd(yg, gl, dycat, name="s5_glu_bwd")
    dyg_b = matmul(dgl, w_glu, tb=True, name="s5_glu_proj_dx")
    dw_glu = matmul(yg, dgl, ta=True, name="s5_glu_proj_dw")
    dyl, du_skip, dd = s5_out_bwd(yl, z, s5_d, dyg_a, dyg_b, name="s5_out_bwd")
    dc_blocks = matmul(xs, dyl, ta=True, mnk=(2 * S5_LANES, S5_CB, T), tm=S5_NB, tn=S5_CB, b_spec=_chan_cols_of_i,
                       exact_tiles=True, name="s5_cx_dw")
    early_grads = {"s5_w_glu": dw_glu.reshape(N_CHIPS, -1, S5_WIDTH), "w_out_even": dw_out_e.reshape(N_CHIPS, -1, D_MODEL),
                   "w_in_odd": _chips_from_cols(dw_in_o), "w_out_odd": dw_out_o.reshape(N_CHIPS, -1, D_MODEL),
                   "mlp_w1": dg1, "mlp_w2": dg2}
    got = {}

    def reverse_scan(exchange):
        (got["lam"], got["dab_re"], got["dab_im"]), bufs = s5_scan(dyl, c_bd, tb_re, tb_im, reverse=True, states=xs,
                                                                   hosted=exchange, name="s5_scan_bwd")
        return bufs

    def attention_bwd(exchange):
        dd_col = fox_dd(ycat, dycat, name="fox_dd")
        (got["dk"], got["dv"], got["dfk"], got["dqt"], got["dfq"]), bufs = fox_bwd(
            z, dycat, f_col, f_row, _col_to_row(lse_col, T), _col_to_row(dd_col, T), exchange, name="fox_bwd")
        return bufs

    halves = _reduce_to_my_half([early_grads[n] for n in REDUCED_EARLY], REDUCED_EARLY, "early_grads",
                                reverse_scan, attention_bwd)
    lam, dab_re, dab_im, dk, dv = got["lam"], got["dab_re"], got["dab_im"], got["dk"], got["dv"]
    db_blocks = matmul(z, lam, ta=True, mnk=(S5_CB, 2 * S5_LANES, T), tm=S5_CB, tn=S5_NB, a_spec=_chan_rows_t,
                       exact_tiles=True, name="s5_bu_dw")
    du_b = matmul(lam, b_bd, tb=True, mnk=(T, S5_WIDTH, 2 * S5_NB), tn=S5_CB, a_spec=_lanes_of_chan,
                  b_spec=_s5_b_block_t, name="s5_bu_dx", **s5_tiles)
    du = add2(du_skip, du_b, name="s5_du")
    dq = jnp.transpose(got["dqt"], (1, 3, 0, 2)).reshape(T, FOX_WIDTH) * (FOX_HEAD_DIM ** -0.5)
    dfl, dbf = fox_gate_bwd(z, bf_pad, _pairs_to_lanes(got["dfk"], T), _pairs_to_lanes(_row_to_col(got["dfq"], T), T),
                            fl_col=FL_TILE, name="fox_gate_bwd")
    dz = jnp.concatenate([du, dq, dk, dv, dfl], axis=1).astype(MXU_DTYPE)
    dw_in_e = matmul(h1, dz, ta=True, name="l0_in_proj_dw")[:, :EVEN_IN]

    def in_proj_dx(exchange):
        got["dh1"], bufs = matmul(dz, w_in_e, tb=True, hosted=exchange, name="l0_in_proj_dx")
        return bufs

    def pre_norm_bwd(exchange):
        (got["dx0"], got["dg_mix_pre0"]), bufs = rms_bwd(x0, gain(mix_pre_g, 0), got["dh1"], dx1, hosted=exchange,
                                                         name="l0_pre_norm_bwd")
        return bufs

    halves = halves + _reduce_to_my_half([_chips_from_cols(dw_in_e)], ["w_in_even"], "late_grads", in_proj_dx, pre_norm_bwd)
    dx0, dg_mix_pre0 = got["dx0"], got["dg_mix_pre0"]

    groups_per_block = S5_CB // S5_GROUP
    own_group = (jnp.arange(S5_CB)[:, None] // S5_GROUP) == ((jnp.arange(S5_LANES)[None, :] // S5_STATE) % groups_per_block)
    db_re, db_im = s5_deinterleave(db_blocks, axis=1)
    dbbt_re = jnp.where(own_group, db_re, 0.0).reshape(groups_per_block, S5_GROUP, S5_LANES).sum(0)
    dbbt_im = jnp.where(own_group, db_im, 0.0).reshape(groups_per_block, S5_GROUP, S5_LANES).sum(0)
    dlr, dli, dldt8, dbtr, dbti = s5_disc_bwd(lr, li, ldt, btr, bti, dab_re, dab_im, dbbt_re, dbbt_im, name="s5_disc_bwd")
    dc_re, dc_im = s5_deinterleave(dc_blocks, axis=0)
    dcr2 = jnp.where(own_group.T, dc_re, 0.0).reshape(S5_LANES, groups_per_block, S5_GROUP).sum(1)
    dci2 = -jnp.where(own_group.T, dc_im, 0.0).reshape(S5_LANES, groups_per_block, S5_GROUP).sum(1)

    def c_layout(a):
        return jnp.transpose(a.reshape(S5_GROUPS, S5_STATE, S5_GROUP), (0, 2, 1))[None]

    def b_layout(a):
        return a.T.reshape(1, S5_GROUPS, S5_STATE, S5_GROUP)

    local_small = {
        "mix_pre_g": jnp.concatenate([dg_mix_pre0, dg_mix_pre1]), "mix_post_g": jnp.concatenate([dg_mix_post0, dg_mix_post1]),
        "mlp_pre_g": jnp.concatenate([dg_mlp_pre0, dg_mlp_pre1]), "mlp_post_g": jnp.concatenate([dg_mlp_post0, dg_mlp_post1]),
        "s5_lam_re": dlr.reshape(1, S5_GROUPS, S5_STATE), "s5_lam_im": dli.reshape(1, S5_GROUPS, S5_STATE),
        "s5_log_dt": dldt8[0:1, 0:S5_GROUPS],
        "s5_b_re": b_layout(dbtr), "s5_b_im": b_layout(dbti), "s5_c_re": c_layout(dcr2), "s5_c_im": c_layout(dci2),
        "s5_d": dd, "fox_b_f": dbf[:, 0:FOX_HEADS],
        "pool_w": _diag_blocks(dpw_bd, len(POOL_WINDOWS))[None],
        "sgu_w_s": jnp.where(causal[None], dws, 0.0)[None], "sgu_b_s": dbst.T[None],
        "pool_scale": dpool_scale, "sgu_ln_g": dln_g, "sgu_ln_b": dln_b,
    }
    return loss8, dx0, dict(zip(REDUCED_EARLY + ("w_in_even",), halves)), local_small


def _reduce_and_update(W, M, V, loss8, dx0, halves, local_small):
    cx, cy, cc = _coords()
    chip = 2 * cx + cy

    summed = dict(local_small, loss=loss8[0:1, 0:1])
    vec = _pack_vec(summed, REDUCED_SEGS, N_DEV * SUBLANES)
    piece = vec.shape[0] // N_DEV
    everyone = allreduce_pieces(vec.reshape(N_DEV, piece, LANES), name="small_grads_allreduce")
    G = _unpack_vec(everyone, REDUCED_SEGS)
    loss = G["loss"].reshape(())
    for n in SHARDED_SMALL:
        G[n] = lax.dynamic_slice_in_dim(G[n], chip * LANES, LANES, axis=1)

    reduced = join_sibling_halves([halves[n] for n in BIG_NAMES], name="big_grads_join")
    for n, r in zip(BIG_NAMES, reduced):
        G[n] = r.reshape(W[n].shape)

    def two_d(a):
        return a.reshape(-1, a.shape[-1])

    delta, new_m, new_v = {}, {}, {}
    for n in BIG_NAMES:
        d_, m_, v_ = adamw(two_d(W[n]), two_d(G[n]), two_d(M[n]), two_d(V[n]), name=f"adamw_{n}")
        delta[n], new_m[n], new_v[n] = (t.reshape(W[n].shape) for t in (d_, m_, v_))
    packed = [_pack_vec(src, SMALL_SEGS, SUBLANES) for src in (W, G, M, V)]
    outs = adamw(*packed, name="adamw_replicated")
    for dst, t in zip((delta, new_m, new_v), outs):
        dst.update(_unpack_vec(t, SMALL_SEGS))
    sharded_segs = tuple((n, (1, LANES)) for n in SHARDED_SMALL)
    packed = [_pack_vec(src, sharded_segs, 1) for src in (W, G, M, V)]
    outs = adamw(*packed, name="adamw_sharded_vectors")
    for dst, t in zip((delta, new_m, new_v), outs):
        dst.update(_unpack_vec(t, sharded_segs))

    order = ["mix_pre_g", "mix_post_g", "mlp_pre_g", "mlp_post_g", "w_in_even", "s5_lam_re", "s5_lam_im", "s5_log_dt",
             "s5_b_re", "s5_b_im", "s5_c_re", "s5_c_im", "s5_d", "s5_w_glu", "fox_b_f", "w_out_even", "w_in_odd",
             "pool_w", "pool_scale", "sgu_ln_g", "sgu_ln_b", "sgu_w_s", "sgu_b_s", "w_out_odd", "mlp_w1", "mlp_w2"]
    return (loss, dx0[None], *[G[n] for n in order], *[delta[n] for n in order],
            *[new_m[n] for n in order], *[new_v[n] for n in order])
```

```python
import functools
import math

import jax
import jax.numpy as jnp
from jax import lax
from jax.experimental import pallas as pl
from jax.experimental.pallas import tpu as pltpu

F32 = jnp.float32
MXU_DTYPE = jnp.bfloat16
WIRE_DTYPE = jnp.bfloat16
EPS = 1e-6
VMEM_LIMIT_BYTES = 48 * 1024 * 1024
LANES = 128
SUBLANES = 8

D_MODEL = 1024
S5_WIDTH = 512
S5_GROUP = 16
S5_GROUPS = 32
S5_STATE = 64
S5_LANES = S5_GROUPS * S5_STATE
FOX_HEADS = 8
FOX_HEAD_DIM = 64
FOX_WIDTH = 512
EVEN_IN = S5_WIDTH + 3 * FOX_WIDTH + FOX_HEADS
EVEN_IN_PAD = 2176
POOL_WINDOWS = (2, 4, 8, 16)
POOL_HALO = 16
POOL_GROUP_DIM = 128
SGU_GROUPS = 4
SGU_GROUP_DIM = 128
CHUNK = 128
D_FF = 4096

ADAM_LR = 0.001
ADAM_B1 = 0.9
ADAM_B2 = 0.999
ADAM_EPS = 1e-08
ADAM_WD = 0.01
ADAM_STEP = 10

MESH_AXES = ("x", "y", "c")
MESH = pl.DeviceIdType.MESH
N_CHIPS = 4
N_DEV = 8

SDS = jax.ShapeDtypeStruct


def _cp(*sem):
    return pltpu.CompilerParams(dimension_semantics=sem, vmem_limit_bytes=VMEM_LIMIT_BYTES)


def _pick(dim, pref):
    if dim <= pref:
        return dim
    t = pref
    while t >= 256:
        if dim % t == 0:
            return t
        t //= 2
    return dim


def _row(tr, c):
    return pl.BlockSpec((tr, c), lambda i: (i, 0))


def _full(shape):
    nd = len(shape)
    return pl.BlockSpec(shape, lambda *_: (0,) * nd)


def _gelu_grad(x):
    c = math.sqrt(2.0 / math.pi)
    t = jnp.tanh(c * (x + 0.044715 * x * x * x))
    return 0.5 * (1.0 + t) + 0.5 * x * (1.0 - t * t) * c * (1.0 + 3.0 * 0.044715 * x * x)


MATMUL_VMEM_BYTES = 36 * 1024 * 1024


def matmul(a, b, *, name, ta=False, tb=False, out_dtype=F32, tm=2048, tn=1024, tk=4096, mnk=None, a_koff=0,
           a_spec=None, b_spec=None, o_spec=None, o_shape=None, prev=None, epi=None, epi_in=(), out_dtypes=None,
           exact_tiles=False, hosted=None):
    if mnk is None:
        M, K = (a.shape[1], a.shape[0]) if ta else a.shape
        K2, N = (b.shape[1], b.shape[0]) if tb else b.shape
        assert K == K2, (a.shape, b.shape, ta, tb)
    else:
        M, N, K = mnk
    out_dtypes = tuple(out_dtypes) if out_dtypes is not None else (out_dtype,)
    n_out, n_epi = len(out_dtypes), len(epi_in)
    tm, tn, tk = _pick(M, tm), _pick(N, tn), _pick(K, tk)

    def vmem_bytes(tm_, tn_, tk_):
        tiles = tm_ * tk_ * a.dtype.itemsize + tk_ * tn_ * b.dtype.itemsize
        tiles += tm_ * tn_ * (sum(jnp.dtype(d).itemsize for d in out_dtypes) + sum(e.dtype.itemsize for e in epi_in))
        return 2 * tiles + tm_ * tn_ * 4 * (tk_ < K)

    def halves(t, dim):
        return [t] + ([t // 2] if t % (2 * LANES) == 0 and t // 2 >= 512 and dim % (t // 2) == 0 else [])

    if exact_tiles:
        halves = lambda t, dim: [t]
    fits = [(m_, n_) for m_ in halves(tm, M) for n_ in halves(tn, N) if vmem_bytes(m_, n_, tk) <= MATMUL_VMEM_BYTES]
    if fits:
        tm, tn = max(fits, key=lambda t: (t[0] * t[1], t[0]))
    else:
        tm, tn = halves(tm, M)[-1], halves(tn, N)[-1]
        while vmem_bytes(tm, tn, tk) > MATMUL_VMEM_BYTES and tk % 2 == 0 and tk > 512:
            tk //= 2
    nk = K // tk
    assert a_koff % tk == 0 and not (ta and a_koff)
    ko = a_koff // tk
    dn = (((0 if ta else 1,), (1 if tb else 0,)), ((), ()))

    def body(*refs):
        a_ref, b_ref = refs[0], refs[1]
        epi_refs = refs[2:2 + n_epi]
        o_refs = refs[len(refs) - n_out - (nk > 1):len(refs) - (nk > 1)]
        k = pl.program_id(2)
        bv = b_ref[...]
        if bv.ndim == 3 and tb:
            cw = bv.shape[-1]
            prod = sum(lax.dot_general(a_ref[:, c * cw:(c + 1) * cw].astype(MXU_DTYPE), bv[c].astype(MXU_DTYPE), dn,
                                       preferred_element_type=F32) for c in range(bv.shape[0]))
        else:
            if bv.ndim == 3:
                bv = bv.reshape(-1, bv.shape[-1])
            prod = lax.dot_general(a_ref[...].astype(MXU_DTYPE), bv.astype(MXU_DTYPE), dn, preferred_element_type=F32)

        def finish(acc):
            res = (acc,) if epi is None else epi(acc, *[r[...] for r in epi_refs])
            for o_ref, r in zip(o_refs, res):
                o_ref[...] = r.astype(o_ref.dtype)

        if nk == 1:
            finish(prod)
            return
        acc_ref = refs[-1]

        @pl.when(k == 0)
        def _():
            acc_ref[...] = prod

        @pl.when(jnp.logical_and(k > 0, k < nk - 1))
        def _():
            acc_ref[...] += prod

        @pl.when(k == nk - 1)
        def _():
            finish(acc_ref[...] + prod)

    if a_spec is None:
        a_spec = pl.BlockSpec((tk, tm), lambda i, j, k: (k, i)) if ta else pl.BlockSpec((tm, tk), lambda i, j, k: (i, k + ko))
    else:
        a_spec = a_spec(tm, tn, tk)
    if b_spec is None:
        bs = pl.BlockSpec((tn, tk), lambda i, j, k: (j, k)) if tb else pl.BlockSpec((tk, tn), lambda i, j, k: (k, j))
    else:
        bs = b_spec(tm, tn, tk)
    tile = pl.BlockSpec((tm, tn), lambda i, j, k: (i, j))
    os_ = tile if o_spec is None else o_spec(tm, tn, tk)
    ins, in_specs, aliases = [a, b, *epi_in], [a_spec, bs] + [tile] * n_epi, {}
    if prev is not None:
        aliases = {len(ins): 0}
        ins.append(prev)
        in_specs.append(pl.BlockSpec(memory_space=pl.ANY))
    shapes = [SDS((M, N) if o_shape is None else o_shape, dt) for dt in out_dtypes]
    scratch = [pltpu.VMEM((tm, tn), F32)] if nk > 1 else []
    if hosted is not None:
        outs, bufs = call_hosting(body, hosted, name=name, grid=(M // tm, N // tn, nk), in_specs=in_specs,
                                  out_specs=[os_] * n_out, out_shape=shapes, inputs=ins, aliases=aliases, scratch=scratch)
        return (outs[0] if n_out == 1 else outs), bufs
    outs = pl.pallas_call(
        body, name=name, grid=(M // tm, N // tn, nk),
        in_specs=in_specs, out_specs=[os_] * n_out, out_shape=shapes, input_output_aliases=aliases,
        scratch_shapes=scratch, compiler_params=_cp("parallel", "parallel", "arbitrary"),
    )(*ins)
    return outs[0] if n_out == 1 else outs


def _rms_hat(x):
    return x * lax.rsqrt(jnp.mean(x * x, axis=-1, keepdims=True) + EPS)


def rms_fwd(x, g, hosted, *, name):
    T, D = x.shape
    tr = _pick(T, 512)

    def body(x_ref, g_ref, o_ref):
        o_ref[...] = (_rms_hat(x_ref[...]) * g_ref[...]).astype(o_ref.dtype)

    (h,), bufs = call_hosting(body, hosted, name=name, grid=(T // tr,), in_specs=[_row(tr, D), _full((1, D))],
                              out_specs=[_row(tr, D)], out_shape=[SDS((T, D), MXU_DTYPE)], inputs=[x, g], aliases={})
    return h, bufs


def res_norm_fwd(x, y, g_post, g_next, *, name):
    T, D = x.shape
    tr = _pick(T, 512)

    def body(x_ref, y_ref, gp_ref, gn_ref, o_ref, h_ref):
        xn = x_ref[...] + _rms_hat(y_ref[...]) * gp_ref[...]
        o_ref[...] = xn
        h_ref[...] = (_rms_hat(xn) * gn_ref[...]).astype(h_ref.dtype)

    return pl.pallas_call(body, name=name, grid=(T // tr,),
                          in_specs=[_row(tr, D), _row(tr, D), _full((1, D)), _full((1, D))],
                          out_specs=[_row(tr, D), _row(tr, D)], out_shape=[SDS((T, D), F32), SDS((T, D), MXU_DTYPE)],
                          compiler_params=_cp("parallel"))(x, y, g_post, g_next)


def res_norm_loss(x, y, g_post, target, *, name):
    T, D = x.shape
    tr = _pick(T, 512)

    def body(x_ref, y_ref, g_ref, t_ref, l_ref, d_ref):
        err = x_ref[...] + _rms_hat(y_ref[...]) * g_ref[...] - t_ref[...]
        d_ref[...] = err * (1.0 / D)

        @pl.when(pl.program_id(0) == 0)
        def _():
            l_ref[...] = jnp.zeros_like(l_ref)

        l_ref[...] += 0.5 * jnp.sum(jnp.mean(err * err, axis=-1, keepdims=True))

    return pl.pallas_call(body, name=name, grid=(T // tr,),
                          in_specs=[_row(tr, D), _row(tr, D), _full((1, D)), _row(tr, D)],
                          out_specs=[_full((SUBLANES, LANES)), _row(tr, D)],
                          out_shape=[SDS((SUBLANES, LANES), F32), SDS((T, D), F32)],
                          compiler_params=_cp("arbitrary"))(x, y, g_post, target)


def _rms_bwd_rows(x, g, dy):
    r = lax.rsqrt(jnp.mean(x * x, axis=-1, keepdims=True) + EPS)
    xh = x * r
    dxh = dy * g
    return r * (dxh - xh * jnp.mean(dxh * xh, axis=-1, keepdims=True)), jnp.sum(dy * xh, axis=0, keepdims=True)


def norm_res_bwd(x, g_pre, dh, res, y, g_post, *, name):
    T, D = x.shape
    tr = _pick(T, 512)

    def body(x_ref, gp_ref, dh_ref, res_ref, y_ref, gy_ref, dx_ref, dy_ref, dgp_ref, dgy_ref):
        dx, dgp = _rms_bwd_rows(x_ref[...], gp_ref[...], dh_ref[...])
        dx = dx + res_ref[...]
        dx_ref[...] = dx
        dy, dgy = _rms_bwd_rows(y_ref[...], gy_ref[...], dx)
        dy_ref[...] = dy.astype(dy_ref.dtype)

        @pl.when(pl.program_id(0) == 0)
        def _():
            dgp_ref[...] = jnp.zeros_like(dgp_ref)
            dgy_ref[...] = jnp.zeros_like(dgy_ref)

        dgp_ref[...] += dgp
        dgy_ref[...] += dgy

    row, vec = _row(tr, D), _full((1, D))
    return pl.pallas_call(body, name=name, grid=(T // tr,), in_specs=[row, vec, row, row, row, vec],
                          out_specs=[row, row, vec, vec],
                          out_shape=[SDS((T, D), F32), SDS((T, D), MXU_DTYPE), SDS((1, D), F32), SDS((1, D), F32)],
                          compiler_params=_cp("arbitrary"))(x, g_pre, dh, res, y, g_post)


def rms_bwd(x, g, dy, res, *, name, hosted=None):
    T, D = x.shape
    tr = _pick(T, 512)
    has_res = res is not None

    def body(*refs):
        if has_res:
            x_ref, g_ref, dy_ref, res_ref, dx_ref, dg_ref = refs
        else:
            x_ref, g_ref, dy_ref, dx_ref, dg_ref = refs
        dx, dg = _rms_bwd_rows(x_ref[...], g_ref[...], dy_ref[...])
        if has_res:
            dx = dx + res_ref[...]
        dx_ref[...] = dx.astype(dx_ref.dtype)

        @pl.when(pl.program_id(0) == 0)
        def _():
            dg_ref[...] = jnp.zeros_like(dg_ref)

        dg_ref[...] += dg

    ins = [x, g, dy] + ([res] if has_res else [])
    in_specs = [_row(tr, D), _full((1, D)), _row(tr, D)] + ([_row(tr, D)] if has_res else [])
    out_shape = [SDS((T, D), F32 if has_res else MXU_DTYPE), SDS((1, D), F32)]
    out_specs = [_row(tr, D), _full((1, D))]
    if hosted is not None:
        return call_hosting(body, hosted, name=name, grid=(T // tr,), in_specs=in_specs, out_specs=out_specs,
                            out_shape=out_shape, inputs=ins, aliases={})
    return pl.pallas_call(body, name=name, grid=(T // tr,), in_specs=in_specs, out_specs=out_specs,
                          out_shape=out_shape, compiler_params=_cp("arbitrary"))(*ins)


def _s5_disc(lr, li, ldt, btr, bti):
    dt = jnp.exp(ldt)
    k = lax.broadcasted_iota(jnp.int32, (SUBLANES, S5_LANES), 0).astype(F32)
    kf = k + 1.0
    kb = 8.0 - k
    ph = li * dt
    lm = lr * dt
    tf_re = jnp.exp(kf * lm) * jnp.cos(kf * ph)
    tf_im = jnp.exp(kf * lm) * jnp.sin(kf * ph)
    tb_re = jnp.exp(kb * lm) * jnp.cos(kb * ph)
    tb_im = -jnp.exp(kb * lm) * jnp.sin(kb * ph)
    mag = jnp.exp(lm)
    ab_re = mag * jnp.cos(ph)
    ab_im = mag * jnp.sin(ph)
    den = lr * lr + li * li
    nr = ab_re - 1.0
    ni = ab_im
    q_re = (nr * lr + ni * li) / den
    q_im = (ni * lr - nr * li) / den
    bbt_re = q_re * btr - q_im * bti
    bbt_im = q_re * bti + q_im * btr
    return tf_re, tf_im, tb_re, tb_im, bbt_re, bbt_im


def _s5_disc_core(lr, li, ldt, btr, bti):
    dt = jnp.exp(ldt)
    mag = jnp.exp(lr * dt)
    ab_re = mag * jnp.cos(li * dt)
    ab_im = mag * jnp.sin(li * dt)
    den = lr * lr + li * li
    nr = ab_re - 1.0
    ni = ab_im
    q_re = (nr * lr + ni * li) / den
    q_im = (ni * lr - nr * li) / den
    return ab_re, ab_im, q_re * btr - q_im * bti, q_re * bti + q_im * btr


def s5_disc_fwd(lr, li, ldt, btr, bti, *, name):
    def body(lr_ref, li_ref, ldt_ref, btr_ref, bti_ref, *outs):
        vals = _s5_disc(lr_ref[...], li_ref[...], ldt_ref[...], btr_ref[...], bti_ref[...])
        for o, v in zip(outs, vals):
            o[...] = v

    tab = SDS((SUBLANES, S5_LANES), F32)
    bb = SDS((S5_GROUP, S5_LANES), F32)
    return pl.pallas_call(body, name=name, out_shape=[tab, tab, tab, tab, bb, bb])(lr, li, ldt, btr, bti)


def s5_disc_bwd(lr, li, ldt, btr, bti, dab_re, dab_im, dbbt_re, dbbt_im, *, name):
    def body(lr_ref, li_ref, ldt_ref, btr_ref, bti_ref, dar_ref, dai_ref, dbr_ref, dbi_ref,
             dlr_ref, dli_ref, dldt_ref, dbtr_ref, dbti_ref):
        _, vjp = jax.vjp(_s5_disc_core, lr_ref[...], li_ref[...], ldt_ref[...], btr_ref[...], bti_ref[...])
        dlr, dli, dldt, dbtr, dbti = vjp((dar_ref[...], dai_ref[...], dbr_ref[...], dbi_ref[...]))
        dlr_ref[...] = dlr
        dli_ref[...] = dli
        dbtr_ref[...] = dbtr
        dbti_ref[...] = dbti
        lane_group = lax.broadcasted_iota(jnp.int32, (S5_LANES, LANES), 0) // S5_STATE
        col = lax.broadcasted_iota(jnp.int32, (S5_LANES, LANES), 1)
        ind = (lane_group == col).astype(F32)
        dldt_ref[...] = jnp.dot(jnp.broadcast_to(dldt, (SUBLANES, S5_LANES)), ind,
                                precision=lax.Precision.HIGHEST, preferred_element_type=F32)

    row = SDS((1, S5_LANES), F32)
    bb = SDS((S5_GROUP, S5_LANES), F32)
    return pl.pallas_call(body, name=name, out_shape=[row, row, SDS((SUBLANES, LANES), F32), bb, bb])(
        lr, li, ldt, btr, bti, dab_re, dab_im, dbbt_re, dbbt_im)


S5_NB = 1024


S5_CB = S5_WIDTH * S5_NB // S5_LANES


def _chan_rows_t(tm, tn, tk):
    return pl.BlockSpec((tk, S5_CB), lambda i, j, k: (k, j // 2))


def _chan_cols_of_i(tm, tn, tk):
    return pl.BlockSpec((tk, S5_CB), lambda i, j, k: (k, i // 2))


def _lanes_of_chan(tm, tn, tk):
    return pl.BlockSpec((tm, 2 * S5_NB), lambda i, j, k: (i, j))


def _s5_b_block_t(tm, tn, tk):
    return pl.BlockSpec((S5_CB, 2 * S5_NB), lambda i, j, k: (j, j))


def _s5_c_block(tm, tn, tk):
    return pl.BlockSpec((2 * S5_NB, S5_CB), lambda i, j, k: (j, j))


def s5_interleave(re, im, axis):
    parts = []
    for n in range(S5_LANES // S5_NB):
        sl = [slice(None)] * re.ndim
        sl[axis] = slice(n * S5_NB, (n + 1) * S5_NB)
        parts += [re[tuple(sl)], im[tuple(sl)]]
    return jnp.concatenate(parts, axis=axis)


def s5_deinterleave(a, axis):
    re, im = [], []
    for n in range(S5_LANES // S5_NB):
        sl = [slice(None)] * a.ndim
        sl[axis] = slice(2 * n * S5_NB, (2 * n + 1) * S5_NB)
        re.append(a[tuple(sl)])
        sl[axis] = slice((2 * n + 1) * S5_NB, (2 * n + 2) * S5_NB)
        im.append(a[tuple(sl)])
    return jnp.concatenate(re, axis=axis), jnp.concatenate(im, axis=axis)


def s5_scan(src, mat, tab_re, tab_im, *, reverse, name, states=None, hosted=None):
    T = src.shape[0]
    nb = S5_NB
    tc = _pick(T, 256)
    nl = S5_LANES // nb
    nt = T // tc
    ntile = tc // SUBLANES
    with_da = states is not None
    assert reverse or not with_da
    step_rows = ((1, 7), (2, 6), (4, 4)) if reverse else ((1, 0), (2, 1), (4, 3))
    drive_dn = _NT if reverse else (((1,), (0,)), ((), ()))

    def body(*refs):
        if with_da:
            (src_ref, wr_ref, wi_ref, tr_ref, ti_ref, sr_ref, si_ref, hr_ref, hi_ref, xo_ref, dar_ref, dai_ref,
             cr_ref, ci_ref, mr_ref, mi_ref, br_ref, bi_ref, ar_ref, ai_ref) = refs
        else:
            src_ref, wr_ref, wi_ref, tr_ref, ti_ref, xo_ref, cr_ref, ci_ref, mr_ref, mi_ref, br_ref, bi_ref = refs

        @pl.when(pl.program_id(1) == 0)
        def _():
            cr_ref[...] = jnp.zeros_like(cr_ref)
            ci_ref[...] = jnp.zeros_like(ci_ref)
            if with_da:
                ar_ref[...] = jnp.zeros_like(ar_ref)
                ai_ref[...] = jnp.zeros_like(ai_ref)

        lhs = src_ref[...].astype(MXU_DTYPE)
        br_ref[...] = lax.dot_general(lhs, wr_ref[...].astype(MXU_DTYPE), drive_dn, preferred_element_type=F32)
        bi_ref[...] = lax.dot_general(lhs, wi_ref[...].astype(MXU_DTYPE), drive_dn, preferred_element_type=F32)

        seen = jnp.where(pl.program_id(1) < nt - 1, 1.0, 0.0)

        def add_da(lr, li, r0, last_r, last_i):
            first = lax.broadcasted_iota(jnp.int32, (SUBLANES, nb), 0) == 0
            pr = jnp.where(first, last_r, pltpu.roll(sr_ref[pl.ds(r0, SUBLANES), :], 1, 0))
            pi = jnp.where(first, last_i, pltpu.roll(si_ref[pl.ds(r0, SUBLANES), :], 1, 0))
            ar_ref[...] += lr * pr + li * pi
            ai_ref[...] += li * pr - lr * pi

        io = lax.broadcasted_iota(jnp.int32, (SUBLANES, nb), 0)
        for s_, (d, r) in enumerate(step_rows):
            keep = (io < SUBLANES - d) if reverse else (io >= d)
            mr_ref[s_] = jnp.where(keep, tr_ref[r:r + 1, :], 0.0)
            mi_ref[s_] = jnp.where(keep, ti_ref[r:r + 1, :], 0.0)

        def tile(i, carry):
            cr, ci = carry
            j = (ntile - 1 - i) if reverse else i
            r0 = pl.multiple_of(j * SUBLANES, SUBLANES)
            xr = br_ref[pl.ds(r0, SUBLANES), :]
            xi = bi_ref[pl.ds(r0, SUBLANES), :]
            for s_, (d, _) in enumerate(step_rows):
                sh = (SUBLANES - d) if reverse else d
                sr = pltpu.roll(xr, sh, 0)
                si = pltpu.roll(xi, sh, 0)
                pr, pi = mr_ref[s_], mi_ref[s_]
                xr, xi = xr + pr * sr - pi * si, xi + pr * si + pi * sr
            tr, ti = tr_ref[...], ti_ref[...]
            xr, xi = xr + tr * cr - ti * ci, xi + tr * ci + ti * cr
            xo_ref[pl.ds(r0, SUBLANES), 0:nb] = xr
            xo_ref[pl.ds(r0, SUBLANES), nb:2 * nb] = xi
            if with_da:
                @pl.when(j > 0)
                def _():
                    p0 = pl.multiple_of(r0 - SUBLANES, SUBLANES)
                    add_da(xr, xi, r0, sr_ref[pl.ds(p0, SUBLANES), :][SUBLANES - 1:SUBLANES, :],
                           si_ref[pl.ds(p0, SUBLANES), :][SUBLANES - 1:SUBLANES, :])

                @pl.when(j == 0)
                def _():
                    add_da(xr, xi, r0, hr_ref[SUBLANES - 1:SUBLANES, :] * seen, hi_ref[SUBLANES - 1:SUBLANES, :] * seen)
            if reverse:
                return xr[0:1, :], xi[0:1, :]
            return xr[SUBLANES - 1:SUBLANES, :], xi[SUBLANES - 1:SUBLANES, :]

        cr, ci = lax.fori_loop(0, ntile, tile, (cr_ref[0:1, :], ci_ref[0:1, :]))
        cr_ref[0:1, :] = cr
        ci_ref[0:1, :] = ci
        if with_da:
            @pl.when(pl.program_id(1) == nt - 1)
            def _():
                dar_ref[...] = jnp.sum(ar_ref[...], axis=0, keepdims=True)
                dai_ref[...] = jnp.sum(ai_ref[...], axis=0, keepdims=True)

    def tmap(t):
        return (nt - 1 - t) if reverse else t

    hb = tc // SUBLANES
    re_spec = pl.BlockSpec((tc, nb), lambda n, t: (tmap(t), 2 * n))
    im_spec = pl.BlockSpec((tc, nb), lambda n, t: (tmap(t), 2 * n + 1))
    tab_spec = pl.BlockSpec((SUBLANES, nb), lambda n, t: (0, n))
    out_spec = pl.BlockSpec((tc, 2 * nb), lambda n, t: (tmap(t), n))
    out_shape = SDS((T, 2 * S5_LANES), F32)
    scratch = [pltpu.VMEM((SUBLANES, nb), F32), pltpu.VMEM((SUBLANES, nb), F32),
               pltpu.VMEM((len(step_rows), SUBLANES, nb), F32), pltpu.VMEM((len(step_rows), SUBLANES, nb), F32),
               pltpu.VMEM((tc, nb), F32), pltpu.VMEM((tc, nb), F32)]
    src_spec = pl.BlockSpec((tc, S5_CB), lambda n, t: (tmap(t), n))
    if reverse:
        wr_spec = pl.BlockSpec((nb, S5_CB), lambda n, t: (2 * n, n))
        wi_spec = pl.BlockSpec((nb, S5_CB), lambda n, t: (2 * n + 1, n))
    else:
        wr_spec = pl.BlockSpec((S5_CB, nb), lambda n, t: (n, 2 * n))
        wi_spec = pl.BlockSpec((S5_CB, nb), lambda n, t: (n, 2 * n + 1))
    drive_specs = [src_spec, wr_spec, wi_spec, tab_spec, tab_spec]
    drive = [src, mat, mat, tab_re, tab_im]
    if not with_da:
        return pl.pallas_call(body, name=name, grid=(nl, nt), in_specs=drive_specs,
                              out_specs=out_spec, out_shape=out_shape, scratch_shapes=scratch,
                              compiler_params=_cp("parallel", "arbitrary"))(*drive)
    re_halo = pl.BlockSpec((SUBLANES, nb), lambda n, t: (jnp.maximum(tmap(t) * hb - 1, 0), 2 * n))
    im_halo = pl.BlockSpec((SUBLANES, nb), lambda n, t: (jnp.maximum(tmap(t) * hb - 1, 0), 2 * n + 1))
    acc = pl.BlockSpec((1, nb), lambda n, t: (0, n))
    row = SDS((1, S5_LANES), F32)
    return call_hosting(
        body, hosted, name=name, grid=(nl, nt),
        in_specs=drive_specs + [re_spec, im_spec, re_halo, im_halo],
        out_specs=[out_spec, acc, acc], out_shape=[out_shape, row, row],
        inputs=drive + [states, states, states, states], aliases={},
        scratch=scratch + [pltpu.VMEM((SUBLANES, nb), F32), pltpu.VMEM((SUBLANES, nb), F32)])


def s5_out_fwd(yc, u, d, *, name):
    T, C = yc.shape
    tr = _pick(T, 512)

    def body(yc_ref, u_ref, d_ref, yl_ref, yg_ref):
        yl = yc_ref[...] + d_ref[...] * u_ref[...]
        yl_ref[...] = yl
        yg_ref[...] = jax.nn.gelu(yl)

    return pl.pallas_call(body, name=name, grid=(T // tr,), in_specs=[_row(tr, C), _row(tr, C), _full((1, C))],
                          out_specs=[_row(tr, C)] * 2, out_shape=[SDS((T, C), F32)] * 2,
                          compiler_params=_cp("parallel"))(yc, u, d)


def glu_fwd(yg, gl, *, out_cols, name):
    T, C = yg.shape
    tr = _pick(T, 512)

    def body(yg_ref, gl_ref, o_ref):
        o_ref[...] = yg_ref[...] * jax.nn.sigmoid(gl_ref[...])

    return pl.pallas_call(body, name=name, grid=(T // tr,), in_specs=[_row(tr, C)] * 2, out_specs=_row(tr, C),
                          out_shape=SDS((T, out_cols), F32), compiler_params=_cp("parallel"))(yg, gl)


def glu_bwd(yg, gl, dy, *, name):
    T, C = yg.shape
    tr = _pick(T, 512)

    def body(yg_ref, gl_ref, dy_ref, dyg_ref, dgl_ref):
        s = jax.nn.sigmoid(gl_ref[...])
        dyv = dy_ref[...]
        dyg_ref[...] = dyv * s
        dgl_ref[...] = (dyv * yg_ref[...] * s * (1.0 - s)).astype(dgl_ref.dtype)

    return pl.pallas_call(body, name=name, grid=(T // tr,), in_specs=[_row(tr, C)] * 3, out_specs=[_row(tr, C)] * 2,
                          out_shape=[SDS((T, C), F32), SDS((T, C), MXU_DTYPE)],
                          compiler_params=_cp("parallel"))(yg, gl, dy)


def s5_out_bwd(yl, u, d, dyg_a, dyg_b, *, name):
    T, C = yl.shape
    tr = _pick(T, 512)

    def body(yl_ref, u_ref, d_ref, da_ref, db_ref, dyl_ref, du_ref, dd_ref):
        dyl = (da_ref[...] + db_ref[...]) * _gelu_grad(yl_ref[...])
        dyl_ref[...] = dyl.astype(dyl_ref.dtype)
        du_ref[...] = dyl * d_ref[...]

        @pl.when(pl.program_id(0) == 0)
        def _():
            dd_ref[...] = jnp.zeros_like(dd_ref)

        dd_ref[...] += jnp.sum(dyl * u_ref[...], axis=0, keepdims=True)

    return pl.pallas_call(body, name=name, grid=(T // tr,),
                          in_specs=[_row(tr, C), _row(tr, C), _full((1, C)), _row(tr, C), _row(tr, C)],
                          out_specs=[_row(tr, C), _row(tr, C), _full((1, C))],
                          out_shape=[SDS((T, C), MXU_DTYPE), SDS((T, C), F32), SDS((1, C), F32)],
                          compiler_params=_cp("arbitrary"))(yl, u, d, dyg_a, dyg_b)


def add2(a, b, *, name):
    T, C = a.shape
    tr = _pick(T, 512)

    def body(a_ref, b_ref, o_ref):
        o_ref[...] = a_ref[...] + b_ref[...]

    return pl.pallas_call(body, name=name, grid=(T // tr,), in_specs=[_row(tr, C)] * 2, out_specs=_row(tr, C),
                          out_shape=SDS((T, C), F32), compiler_params=_cp("parallel"))(a, b)


def _tri(n, upper):
    r = lax.broadcasted_iota(jnp.int32, (n, n), 0)
    c = lax.broadcasted_iota(jnp.int32, (n, n), 1)
    return ((c >= r) if upper else (c <= r)).astype(F32)


def fox_gate_fwd(fl, bf, *, fl_col, name):
    T = fl.shape[0]
    tb = _pick(T, 256)

    def body(fl_ref, bf_ref, f_ref, c_ref):
        @pl.when(pl.program_id(0) == 0)
        def _():
            c_ref[...] = jnp.zeros_like(c_ref)

        lf = jax.nn.log_sigmoid(fl_ref[...] + bf_ref[...])
        f = jnp.dot(_tri(tb, False), lf, precision=lax.Precision.HIGHEST, preferred_element_type=F32) + c_ref[0:1, :]
        f_ref[...] = f * LOG2E
        c_ref[0:1, :] = f[tb - 1:tb, :]

    fl_spec = pl.BlockSpec((tb, LANES), lambda i: (i, fl_col))
    return pl.pallas_call(body, name=name, grid=(T // tb,), in_specs=[fl_spec, _full((1, LANES))],
                          out_specs=_row(tb, LANES), out_shape=SDS((T, LANES), F32),
                          scratch_shapes=[pltpu.VMEM((SUBLANES, LANES), F32)], compiler_params=_cp("arbitrary"))(fl, bf)


def fox_gate_bwd(fl, bf, df_keys, df_queries, *, fl_col, name):
    T = fl.shape[0]
    tb = _pick(T, 256)
    nt = T // tb

    def body(fl_ref, bf_ref, dfk_ref, dfq_ref, dfl_ref, dbf_ref, c_ref):
        @pl.when(pl.program_id(0) == 0)
        def _():
            c_ref[...] = jnp.zeros_like(c_ref)
            dbf_ref[...] = jnp.zeros_like(dbf_ref)

        dlf = jnp.dot(_tri(tb, True), dfk_ref[...] + dfq_ref[...], precision=lax.Precision.HIGHEST,
                      preferred_element_type=F32) + c_ref[0:1, :]
        c_ref[0:1, :] = dlf[0:1, :]
        dfl = dlf * jax.nn.sigmoid(-(fl_ref[...] + bf_ref[...]))
        dfl_ref[...] = dfl
        dbf_ref[...] += jnp.sum(dfl, axis=0, keepdims=True)

    rev = pl.BlockSpec((tb, LANES), lambda i: (nt - 1 - i, 0))
    fl_rev = pl.BlockSpec((tb, LANES), lambda i: (nt - 1 - i, fl_col))
    return pl.pallas_call(body, name=name, grid=(nt,), in_specs=[fl_rev, _full((1, LANES)), rev, rev],
                          out_specs=[rev, _full((1, LANES))], out_shape=[SDS((T, LANES), F32), SDS((1, LANES), F32)],
                          scratch_shapes=[pltpu.VMEM((SUBLANES, LANES), F32)],
                          compiler_params=_cp("arbitrary"))(fl, bf, df_keys, df_queries)


FOX_BLOCK = 512
FOX_PAIRS = FOX_HEADS // 2
_NT = (((1,), (1,)), ((), ()))


LOG2E = 1.4426950408889634
FOX_FWD_UNROLL = 4
FOX_BWD_UNROLL = 3


def _fox_block(T):
    return _pick(T, FOX_BLOCK)


def _own_lanes(lane, hh):
    return (lane < FOX_HEAD_DIM) if hh == 0 else (lane >= FOX_HEAD_DIM)


def _grouped_steps(step, lo, n, unroll, init):
    def trip(t, c):
        for u in range(unroll):
            c = step(lo + t * unroll + u, c)
        return c

    carry = lax.fori_loop(0, n // unroll, trip, init)
    for u in range(unroll - 1):
        carry = lax.cond(n % unroll > u, lambda c: step(lo + (n // unroll) * unroll + u, c), lambda c: c, carry)
    return carry


Q_TILE0, K_TILE0, V_TILE0, O_TILE0 = 4, 8, 12, 4
FL_TILE = 16
POOL_COL = 2


def fox_fwd(z, f_col, f_row, ycat, hosted, *, name):
    T = z.shape[0]
    blk = _fox_block(T)
    nb = T // blk
    scale = FOX_HEAD_DIM ** -0.5

    def body(q_ref, k_ref, v_ref, fc_ref, fr_ref, prev_ref, o_ref, l_ref):
        i = pl.program_id(1)
        row = lax.broadcasted_iota(jnp.int32, (blk, blk), 0)
        col = lax.broadcasted_iota(jnp.int32, (blk, blk), 1)
        lane = lax.broadcasted_iota(jnp.int32, (blk, LANES), 1)
        qt = q_ref[...] * (scale * LOG2E)
        outs = []
        for hh in range(2):
            qh = jnp.where(_own_lanes(lane, hh), qt, 0.0).astype(MXU_DTYPE)
            fi = fc_ref[0, :, hh:hh + 1]

            def step(j, carry, masked=False):
                m, l, acc = carry
                r0 = pl.multiple_of(j * blk, blk)
                kj = k_ref[pl.ds(r0, blk), :].astype(MXU_DTYPE)
                vj = v_ref[pl.ds(r0, blk), :].astype(MXU_DTYPE)
                s = lax.dot_general(qh, kj, _NT, preferred_element_type=F32) + (fi - fr_ref[0, j, hh:hh + 1, :])
                if masked:
                    s = jnp.where(col <= row, s, -jnp.inf)
                m_new = jnp.maximum(m, jnp.max(s, axis=-1, keepdims=True))
                p = jnp.exp2(s - m_new)
                alpha = jnp.exp2(m - m_new)
                l = alpha * l + jnp.sum(p, axis=-1, keepdims=True)
                acc = alpha * acc + jnp.dot(p.astype(MXU_DTYPE), vj, preferred_element_type=F32)
                return m_new, l, acc

            init = (jnp.full((blk, 1), -jnp.inf, F32), jnp.zeros((blk, 1), F32), jnp.zeros((blk, LANES), F32))
            m, l, acc = step(i, _grouped_steps(step, 0, i, FOX_FWD_UNROLL, init), True)
            outs.append(acc / l)
            l_ref[0, :, hh:hh + 1] = m + jnp.log2(l)
        o_ref[...] = jnp.where(_own_lanes(lane, 0), outs[0], outs[1])

    qspec = pl.BlockSpec((blk, LANES), lambda h, i: (i, Q_TILE0 + h))
    kspec = pl.BlockSpec((T, LANES), lambda h, i: (0, K_TILE0 + h))
    vspec = pl.BlockSpec((T, LANES), lambda h, i: (0, V_TILE0 + h))
    ospec = pl.BlockSpec((blk, LANES), lambda h, i: (i, O_TILE0 + h))
    cspec = pl.BlockSpec((1, blk, 2), lambda h, i: (h, i, 0))
    rspec = pl.BlockSpec((1, nb, 2, blk), lambda h, i: (h, 0, 0, 0))
    return call_hosting(body, hosted, name=name, grid=(FOX_PAIRS, nb),
                        in_specs=[qspec, kspec, vspec, cspec, rspec, ANY], out_specs=[ospec, cspec],
                        out_shape=[SDS(ycat.shape, F32), SDS((FOX_PAIRS, T, 2), F32)],
                        inputs=[z, z, z, f_col, f_row, ycat], aliases={5: 0})


def fox_dd(ycat, dycat, *, name):
    T = ycat.shape[0]
    blk = _fox_block(T)

    def body(o_ref, do_ref, dd_ref):
        lane = lax.broadcasted_iota(jnp.int32, (blk, LANES), 1)
        prod = do_ref[...] * o_ref[...]
        for hh in range(2):
            dd_ref[0, :, hh:hh + 1] = jnp.sum(jnp.where(_own_lanes(lane, hh), prod, 0.0), axis=-1, keepdims=True)

    ospec = pl.BlockSpec((blk, LANES), lambda h, i: (i, O_TILE0 + h))
    return pl.pallas_call(body, name=name, grid=(FOX_PAIRS, T // blk), in_specs=[ospec, ospec],
                          out_specs=pl.BlockSpec((1, blk, 2), lambda h, i: (h, i, 0)),
                          out_shape=SDS((FOX_PAIRS, T, 2), F32), compiler_params=_cp("parallel", "parallel"))(ycat, dycat)


def fox_bwd(z, dycat, f_col, f_row, lse_row, dd_row, hosted, *, name):
    T = z.shape[0]
    blk = _fox_block(T)
    nb = T // blk
    scale = FOX_HEAD_DIM ** -0.5

    def body(q_ref, k_ref, v_ref, do_ref, fc_ref, fr_ref, lr_ref, dr_ref, dk_ref, dv_ref, df_ref, dqt_ref, dfq_ref):
        j = pl.program_id(1)

        @pl.when(j == 0)
        def _():
            dqt_ref[...] = jnp.zeros_like(dqt_ref)
            dfq_ref[...] = jnp.zeros_like(dfq_ref)

        row = lax.broadcasted_iota(jnp.int32, (blk, blk), 0)
        col = lax.broadcasted_iota(jnp.int32, (blk, blk), 1)
        lane = lax.broadcasted_iota(jnp.int32, (blk, LANES), 1)
        kt = k_ref[...]
        vt = v_ref[...]
        dks, dvs = [], []
        for hh in range(2):
            own = _own_lanes(lane, hh)
            kh = jnp.where(own, kt, 0.0).astype(MXU_DTYPE)
            vh = jnp.where(own, vt, 0.0).astype(MXU_DTYPE)
            kht = kh.T
            fj = fc_ref[0, :, hh:hh + 1]

            def step(i, carry, masked=False):
                dk, dv, df = carry
                r0 = pl.multiple_of(i * blk, blk)
                qi = (q_ref[pl.ds(r0, blk), :] * (scale * LOG2E)).astype(MXU_DTYPE)
                doi = do_ref[pl.ds(r0, blk), :].astype(MXU_DTYPE)
                st = lax.dot_general(kh, qi, _NT, preferred_element_type=F32) + (fr_ref[0, i, hh:hh + 1, :] - fj)
                pt = jnp.exp2(st - lr_ref[0, i, hh:hh + 1, :])
                if masked:
                    pt = jnp.where(col >= row, pt, 0.0)
                dv = dv + jnp.dot(pt.astype(MXU_DTYPE), doi, preferred_element_type=F32)
                dpt = lax.dot_general(vh, doi, _NT, preferred_element_type=F32)
                dst = pt * (dpt - dr_ref[0, i, hh:hh + 1, :])
                dsb = dst.astype(MXU_DTYPE)
                dk = dk + jnp.dot(dsb, qi, preferred_element_type=F32)
                df = df - jnp.sum(dst, axis=-1, keepdims=True)
                dqt_ref[0, i] += jnp.dot(kht, dsb, preferred_element_type=F32)
                dfq_ref[0, i, hh:hh + 1, :] += jnp.sum(dst, axis=0, keepdims=True)
                return dk, dv, df

            init = (jnp.zeros((blk, LANES), F32), jnp.zeros((blk, LANES), F32), jnp.zeros((blk, 1), F32))
            dk, dv, df = _grouped_steps(step, j + 1, nb - 1 - j, FOX_BWD_UNROLL, step(j, init, True))
            dks.append(dk * (1.0 / LOG2E))
            dvs.append(dv)
            df_ref[0, :, hh:hh + 1] = df
        dk_ref[...] = jnp.where(_own_lanes(lane, 0), dks[0], dks[1])
        dv_ref[...] = jnp.where(_own_lanes(lane, 0), dvs[0], dvs[1])

    bspec = pl.BlockSpec((blk, LANES), lambda h, j: (j, h))
    qspec = pl.BlockSpec((T, LANES), lambda h, j: (0, Q_TILE0 + h))
    kspec = pl.BlockSpec((blk, LANES), lambda h, j: (j, K_TILE0 + h))
    vspec = pl.BlockSpec((blk, LANES), lambda h, j: (j, V_TILE0 + h))
    dospec = pl.BlockSpec((T, LANES), lambda h, j: (0, O_TILE0 + h))
    cspec = pl.BlockSpec((1, blk, 2), lambda h, j: (h, j, 0))
    rspec = pl.BlockSpec((1, nb, 2, blk), lambda h, j: (h, 0, 0, 0))
    dqspec = pl.BlockSpec((1, nb, LANES, blk), lambda h, j: (h, 0, 0, 0))
    return call_hosting(body, hosted, name=name, grid=(FOX_PAIRS, nb),
                        in_specs=[qspec, kspec, vspec, dospec, cspec, rspec, rspec, rspec],
                        out_specs=[bspec, bspec, cspec, dqspec, rspec],
                        out_shape=[SDS((T, FOX_WIDTH), F32), SDS((T, FOX_WIDTH), F32), SDS((FOX_PAIRS, T, 2), F32),
                                   SDS((FOX_PAIRS, nb, LANES, blk), F32), SDS((FOX_PAIRS, nb, 2, blk), F32)],
                        inputs=[z, z, z, dycat, f_col, f_row, lse_row, dd_row], aliases={})


def _pairs_col(a, T):
    return jnp.transpose(a[:, :FOX_HEADS].reshape(T, FOX_PAIRS, 2), (1, 0, 2))


def _col_to_row(a, T):
    blk = _fox_block(T)
    return jnp.transpose(a.reshape(FOX_PAIRS, T // blk, blk, 2), (0, 1, 3, 2))


def _row_to_col(a, T):
    return jnp.transpose(a, (0, 1, 3, 2)).reshape(FOX_PAIRS, T, 2)


def _pairs_to_lanes(a, T):
    flat = jnp.transpose(a, (1, 0, 2)).reshape(T, FOX_HEADS)
    return jnp.pad(flat, ((0, 0), (0, LANES - FOX_HEADS)))


def _pool_counts(t0, n, w):
    t = (t0 + lax.broadcasted_iota(jnp.int32, (n, 1), 0)).astype(F32)
    return jnp.minimum(t + 1.0, float(w))


def pool_window(x, *, adjoint, name, in_col=0, into=None, out_col=0, out_dtype=F32):
    T, C = x.shape[0], len(POOL_WINDOWS) * POOL_GROUP_DIM
    tr = _pick(T, 512)
    nt = T // tr
    hb = tr // POOL_HALO
    n = tr + POOL_HALO

    def body(x_ref, h_ref, *rest):
        o_ref = rest[-1]
        i = pl.program_id(0)
        cur = x_ref[...]
        if adjoint:
            halo = h_ref[...] * jnp.where(i < nt - 1, 1.0, 0.0)
            ext = jnp.concatenate([cur, halo], axis=0)
            t0 = i * tr
        else:
            halo = h_ref[...] * jnp.where(i > 0, 1.0, 0.0)
            ext = jnp.concatenate([halo, cur], axis=0)
            t0 = i * tr - POOL_HALO
        sums = {}
        for g, w in enumerate(POOL_WINDOWS):
            ls = slice(g * POOL_GROUP_DIM, (g + 1) * POOL_GROUP_DIM)
            s = ext[:, ls]
            if adjoint:
                s = s / _pool_counts(t0, n, w)
            d = 1
            while d < w:
                s = s + pltpu.roll(s, (n - d) if adjoint else d, 0)
                d *= 2
            if adjoint:
                o_ref[:, ls] = (s[0:tr, :] - cur[:, ls]).astype(o_ref.dtype)
            else:
                o_ref[:, ls] = (s[POOL_HALO:n, :] / _pool_counts(i * tr, tr, w) - cur[:, ls]).astype(o_ref.dtype)

    if adjoint:
        halo_spec = pl.BlockSpec((POOL_HALO, C), lambda i: (jnp.minimum((i + 1) * hb, T // POOL_HALO - 1), in_col))
    else:
        halo_spec = pl.BlockSpec((POOL_HALO, C), lambda i: (jnp.maximum(i * hb - 1, 0), in_col))
    x_spec = pl.BlockSpec((tr, C), lambda i: (i, in_col))
    if into is None:
        return pl.pallas_call(body, name=name, grid=(nt,), in_specs=[x_spec, halo_spec], out_specs=_row(tr, C),
                              out_shape=SDS((T, C), out_dtype), compiler_params=_cp("parallel"))(x, x)
    return pl.pallas_call(body, name=name, grid=(nt,), in_specs=[x_spec, halo_spec, ANY],
                          out_specs=pl.BlockSpec((tr, C), lambda i: (i, out_col)), out_shape=SDS(into.shape, into.dtype),
                          input_output_aliases={2: 0}, compiler_params=_cp("parallel"))(x, x, into)


def colscale_fwd(a, s, *, out_cols, name):
    T, C = a.shape
    tr = _pick(T, 512)

    def body(a_ref, s_ref, o_ref):
        o_ref[...] = (a_ref[...] * s_ref[...]).astype(o_ref.dtype)

    return pl.pallas_call(body, name=name, grid=(T // tr,), in_specs=[_row(tr, C), _full((1, C))], out_specs=_row(tr, C),
                          out_shape=SDS((T, out_cols), MXU_DTYPE), compiler_params=_cp("parallel"))(a, s)


def colscale_bwd(a, s, dy, *, name):
    T, C = a.shape
    tr = _pick(T, 512)

    def body(a_ref, s_ref, dy_ref, da_ref, ds_ref):
        dyv = dy_ref[...]
        da_ref[...] = (dyv * s_ref[...]).astype(da_ref.dtype)

        @pl.when(pl.program_id(0) == 0)
        def _():
            ds_ref[...] = jnp.zeros_like(ds_ref)

        ds_ref[...] += jnp.sum(dyv * a_ref[...], axis=0, keepdims=True)

    return pl.pallas_call(body, name=name, grid=(T // tr,), in_specs=[_row(tr, C), _full((1, C)), _row(tr, C)],
                          out_specs=[_row(tr, C), _full((1, C))], out_shape=[SDS((T, C), MXU_DTYPE), SDS((1, C), F32)],
                          compiler_params=_cp("arbitrary"))(a, s, dy)


SGU_ROWS = 512


def _sgu_norm(v, ln_g, ln_b):
    vg = jax.nn.gelu(v)
    xc = vg - jnp.mean(vg, axis=-1, keepdims=True)
    r = lax.rsqrt(jnp.mean(xc * xc, axis=-1, keepdims=True) + EPS)
    xh = xc * r
    return xh * ln_g + ln_b, xh, r


def _rowc(tr, c, cb):
    return pl.BlockSpec((tr, c), lambda i: (i, cb))


def sgu_fwd(z, ln_g, ln_b, ws, bst, ycat, *, name):
    T, C = z.shape[0], SGU_GROUPS * SGU_GROUP_DIM
    tr = _pick(T, SGU_ROWS)

    def body(u_ref, v_ref, g_ref, b_ref, ws_ref, bst_ref, prev_ref, o_ref):
        vn, _, _ = _sgu_norm(v_ref[...], g_ref[...], b_ref[...])
        vn = vn.astype(MXU_DTYPE)
        ug = jax.nn.gelu(u_ref[...])
        for g in range(SGU_GROUPS):
            w = ws_ref[g].astype(MXU_DTYPE)
            bias = bst_ref[:, g:g + 1]
            for c in range(tr // CHUNK):
                rs = slice(c * CHUNK, (c + 1) * CHUNK)
                ls = slice(g * SGU_GROUP_DIM, (g + 1) * SGU_GROUP_DIM)
                mixed = jnp.dot(w, vn[rs, ls], preferred_element_type=F32) + bias
                o_ref[rs, ls] = (ug[rs, ls] * mixed).astype(o_ref.dtype)

    return pl.pallas_call(body, name=name, grid=(T // tr,),
                          in_specs=[_rowc(tr, C, 0), _rowc(tr, C, 1), _full((1, C)), _full((1, C)),
                                    _full((SGU_GROUPS, CHUNK, CHUNK)), _full((CHUNK, SGU_GROUPS)), ANY],
                          out_specs=_rowc(tr, C, 1), out_shape=SDS(ycat.shape, ycat.dtype), input_output_aliases={6: 0},
                          compiler_params=_cp("parallel"))(z, z, ln_g, ln_b, ws, bst, ycat)


def sgu_bwd(z, ln_g, ln_b, ws, wst, bst, dycat, *, out_cols, name):
    T, C = z.shape[0], SGU_GROUPS * SGU_GROUP_DIM
    tr = _pick(T, SGU_ROWS)

    def body(u_ref, v_ref, g_ref, b_ref, ws_ref, wst_ref, bst_ref, dy_ref,
             duv_ref, dg_ref, db_ref, dws_ref, dbst_ref, dvn_ref):
        du_ref = duv_ref.at[:, 0:C]
        dv_ref = duv_ref.at[:, C:2 * C]
        @pl.when(pl.program_id(0) == 0)
        def _():
            dg_ref[...] = jnp.zeros_like(dg_ref)
            db_ref[...] = jnp.zeros_like(db_ref)
            dws_ref[...] = jnp.zeros_like(dws_ref)
            dbst_ref[...] = jnp.zeros_like(dbst_ref)

        uv = u_ref[...]
        vv = v_ref[...]
        vn, xh, r = _sgu_norm(vv, g_ref[...], b_ref[...])
        vn = vn.astype(MXU_DTYPE)
        ug = jax.nn.gelu(uv)
        dyv = dy_ref[...]
        for g in range(SGU_GROUPS):
            w = ws_ref[g].astype(MXU_DTYPE)
            wt = wst_ref[g].astype(MXU_DTYPE)
            bias = bst_ref[:, g:g + 1]
            dw = jnp.zeros((CHUNK, CHUNK), F32)
            dbias = jnp.zeros((CHUNK, 1), F32)
            for c in range(tr // CHUNK):
                rs = slice(c * CHUNK, (c + 1) * CHUNK)
                ls = slice(g * SGU_GROUP_DIM, (g + 1) * SGU_GROUP_DIM)
                vblk = vn[rs, ls]
                mixed = jnp.dot(w, vblk, preferred_element_type=F32) + bias
                dyb = dyv[rs, ls]
                du_ref[rs, ls] = (dyb * mixed * _gelu_grad(uv[rs, ls])).astype(du_ref.dtype)
                dmixed = dyb * ug[rs, ls]
                dbias = dbias + jnp.sum(dmixed, axis=-1, keepdims=True)
                dmb = dmixed.astype(MXU_DTYPE)
                dw = dw + lax.dot_general(dmb, vblk, _NT, preferred_element_type=F32)
                dvn_ref[rs, ls] = jnp.dot(wt, dmb, preferred_element_type=F32)
            dws_ref[g] += dw
            dbst_ref[:, g:g + 1] += dbias
        dvn = dvn_ref[...]
        dg_ref[...] += jnp.sum(dvn * xh, axis=0, keepdims=True)
        db_ref[...] += jnp.sum(dvn, axis=0, keepdims=True)
        dxh = dvn * g_ref[...]
        dvg = r * (dxh - jnp.mean(dxh, axis=-1, keepdims=True) - xh * jnp.mean(dxh * xh, axis=-1, keepdims=True))
        dv_ref[...] = (dvg * _gelu_grad(vv)).astype(dv_ref.dtype)

    wspec = _full((SGU_GROUPS, CHUNK, CHUNK))
    return pl.pallas_call(body, name=name, grid=(T // tr,),
                          in_specs=[_rowc(tr, C, 0), _rowc(tr, C, 1), _full((1, C)), _full((1, C)), wspec, wspec,
                                    _full((CHUNK, SGU_GROUPS)), _rowc(tr, C, 1)],
                          out_specs=[_rowc(tr, 2 * C, 0), _full((1, C)), _full((1, C)), wspec,
                                     _full((CHUNK, SGU_GROUPS))],
                          out_shape=[SDS((T, out_cols), MXU_DTYPE), SDS((1, C), F32), SDS((1, C), F32),
                                     SDS((SGU_GROUPS, CHUNK, CHUNK), F32), SDS((CHUNK, SGU_GROUPS), F32)],
                          scratch_shapes=[pltpu.VMEM((tr, C), F32)],
                          compiler_params=_cp("arbitrary"))(z, z, ln_g, ln_b, ws, wst, bst, dycat)


def adamw(w, g, m, v, *, name):
    R, C = w.shape
    tr = _pick(R, 512)
    c1 = 1.0 - ADAM_B1 ** ADAM_STEP
    c2 = 1.0 - ADAM_B2 ** ADAM_STEP

    def body(w_ref, g_ref, m_ref, v_ref, d_ref, nm_ref, nv_ref):
        gv = g_ref[...]
        nm = ADAM_B1 * m_ref[...] + (1.0 - ADAM_B1) * gv
        nv = ADAM_B2 * v_ref[...] + (1.0 - ADAM_B2) * (gv * gv)
        nm_ref[...] = nm
        nv_ref[...] = nv
        d_ref[...] = -ADAM_LR * ((nm / c1) / (jnp.sqrt(nv / c2) + ADAM_EPS) + ADAM_WD * w_ref[...])

    spec = _row(tr, C)
    return pl.pallas_call(body, name=name, grid=(R // tr,), in_specs=[spec] * 4, out_specs=[spec] * 3,
                          out_shape=[SDS((R, C), F32)] * 3, compiler_params=_cp("parallel"))(w, g, m, v)


ANY = pl.BlockSpec(memory_space=pl.ANY)


def _coords():
    return lax.axis_index("x"), lax.axis_index("y"), lax.axis_index("c")


def _other_chips(x, y):
    return [(1 - x, y), (x, 1 - y), (1 - x, 1 - y)]


def _remote(src, dst, send_sems, recv_sems, k, dev):
    return pltpu.make_async_remote_copy(src_ref=src, dst_ref=dst, send_sem=send_sems.at[k], recv_sem=recv_sems.at[k],
                                        device_id=dev, device_id_type=MESH)


LOCAL_CHUNKS = 8


class Exchange:
    def __init__(self, ins, out_shapes, scratch, start, wait):
        self.ins, self.out_shapes, self.scratch, self.start, self.wait = list(ins), list(out_shapes), list(scratch), start, wait


def run_exchange(ex, *, name):
    ni, no = len(ex.ins), len(ex.out_shapes)

    def body(*refs):
        parts = refs[:ni], refs[ni:ni + no], refs[ni + no:]
        ex.start(*parts)
        ex.wait(*parts)

    return pl.pallas_call(body, name=name, in_specs=[ANY] * ni, out_specs=[ANY] * no, out_shape=ex.out_shapes,
                          scratch_shapes=ex.scratch)(*ex.ins)


def call_hosting(body, ex, *, name, grid, in_specs, out_specs, out_shape, inputs, aliases, scratch=()):
    n_in, n_out, ni, no, ns = len(inputs), len(out_shape), len(ex.ins), len(ex.out_shapes), len(scratch)
    outs_at = n_in + ni
    scr_at = outs_at + n_out + no

    def wrapped(*refs):
        own = refs[:n_in] + refs[outs_at:outs_at + n_out] + refs[scr_at:scr_at + ns]
        parts = refs[n_in:outs_at], refs[outs_at + n_out:scr_at], refs[scr_at + ns:]
        ids = [pl.program_id(d) for d in range(len(grid))]
        first = functools.reduce(jnp.logical_and, [i == 0 for i in ids])
        last = functools.reduce(jnp.logical_and, [i == g - 1 for i, g in zip(ids, grid)])

        @pl.when(first)
        def _():
            ex.start(*parts)

        body(*own)

        @pl.when(last)
        def _():
            ex.wait(*parts)

    outs = pl.pallas_call(
        wrapped, name=name, grid=grid, in_specs=list(in_specs) + [ANY] * ni, out_specs=list(out_specs) + [ANY] * no,
        out_shape=list(out_shape) + ex.out_shapes, input_output_aliases=aliases,
        scratch_shapes=list(scratch) + ex.scratch,
        compiler_params=_cp(*["arbitrary"] * len(grid)))(*inputs, *ex.ins)
    return outs[:n_out], outs[n_out:]


def allgather_ici_exchange(shards, whole=()):
    na, n_all = len(shards), len(shards) + len(whole)
    arrays = list(shards) + list(whole)

    def copies(s_refs, o_refs, sems):
        send_sems, recv_sems, _ = sems
        x, y, c = _coords()
        j = 2 * x + y
        out = []
        for a in range(n_all):
            if a < na:
                half = shards[a].shape[0] // 2
                part = (pl.ds(c * half, half),)
            else:
                part = ()
            for k, (px, py) in enumerate(_other_chips(x, y)):
                send = _remote(s_refs[a].at[part] if part else s_refs[a], o_refs[a].at[(j,) + part], send_sems, recv_sems,
                               3 * a + k, (px, py, c))
                rows = o_refs[a].at[(2 * px + py,) + part]
                out.append((send, _remote(rows, rows, send_sems, recv_sems, 3 * a + k, (px, py, c))))
        return out

    def start(s_refs, o_refs, sems):
        x, y, c = _coords()
        j = 2 * x + y
        for a in range(n_all):
            chunks = LOCAL_CHUNKS if a < na else 1
            chunk = arrays[a].shape[0] // chunks
            for q in range(chunks):
                rows = pl.ds(q * chunk, chunk)
                pltpu.make_async_copy(s_refs[a].at[rows], o_refs[a].at[j, rows], sems[2].at[a]).start()
        for send, _ in copies(s_refs, o_refs, sems):
            send.start()

    def wait(s_refs, o_refs, sems):
        x, y, c = _coords()
        j = 2 * x + y
        for send, arrival in copies(s_refs, o_refs, sems):
            arrival.wait_recv()
            send.wait_send()
        for a in range(n_all):
            pltpu.make_async_copy(s_refs[a], o_refs[a].at[j], sems[2].at[a]).wait()

    return Exchange(arrays, [SDS((N_CHIPS,) + s.shape, s.dtype) for s in arrays],
                    [pltpu.SemaphoreType.DMA((3 * n_all,)), pltpu.SemaphoreType.DMA((3 * n_all,)),
                     pltpu.SemaphoreType.DMA((n_all,))], start, wait)


def allgather_forward(gathered, *, name):
    na = len(gathered)

    def body(*refs):
        o_refs = refs[na:2 * na]
        send_sems, recv_sems = refs[2 * na:]
        x, y, c = _coords()
        sibling = (x, y, 1 - c)
        cps = []
        for a in range(na):
            half = gathered[a].shape[1] // 2
            for k, (px, py) in enumerate(_other_chips(x, y)):
                mine = o_refs[a].at[2 * px + py, pl.ds(c * half, half)]
                theirs = o_refs[a].at[2 * px + py, pl.ds((1 - c) * half, half)]
                cps.append((_remote(mine, mine, send_sems, recv_sems, 3 * a + k, sibling),
                            _remote(theirs, theirs, send_sems, recv_sems, 3 * a + k, sibling)))
        for send, _ in cps:
            send.start()
        for send, arrival in cps:
            send.wait_send()
            arrival.wait_recv()

    return pl.pallas_call(body, name=name, in_specs=[ANY] * na, out_specs=[ANY] * na,
                          out_shape=[SDS(g.shape, g.dtype) for g in gathered],
                          input_output_aliases={a: a for a in range(na)},
                          scratch_shapes=[pltpu.SemaphoreType.DMA((3 * na,)), pltpu.SemaphoreType.DMA((3 * na,))])(*gathered)


def swap_halves_exchange(gs):
    na = len(gs)

    def copies(g_refs, o_refs, sems):
        x, y, c = _coords()
        out = []
        for a in range(na):
            half = gs[a].shape[1] // 2
            out.append(_remote(g_refs[a].at[:, pl.ds((1 - c) * half, half), :], o_refs[a], sems[0], sems[1], a,
                               (x, y, 1 - c)))
        return out

    def start(g_refs, o_refs, sems):
        for cp in copies(g_refs, o_refs, sems):
            cp.start()

    def wait(g_refs, o_refs, sems):
        for cp in copies(g_refs, o_refs, sems):
            cp.wait()

    return Exchange(gs, [SDS((g.shape[0], g.shape[1] // 2, g.shape[2]), g.dtype) for g in gs],
                    [pltpu.SemaphoreType.DMA((na,)), pltpu.SemaphoreType.DMA((na,))], start, wait)


def chip_partials_exchange(pbs):
    na = len(pbs)

    def copies(p_refs, o_refs, sems):
        x, y, c = _coords()
        out = []
        for a in range(na):
            for k, (px, py) in enumerate(_other_chips(x, y)):
                out.append(_remote(p_refs[a].at[2 * px + py], o_refs[a].at[k], sems[0], sems[1], 3 * a + k, (px, py, c)))
        return out

    def start(p_refs, o_refs, sems):
        for cp in copies(p_refs, o_refs, sems):
            cp.start()

    def wait(p_refs, o_refs, sems):
        for cp in copies(p_refs, o_refs, sems):
            cp.wait()

    return Exchange(pbs, [SDS((3,) + p.shape[1:], p.dtype) for p in pbs],
                    [pltpu.SemaphoreType.DMA((3 * na,)), pltpu.SemaphoreType.DMA((3 * na,))], start, wait)


def add_sibling_half(g, land, c_idx, *, name):
    n, R, C = g.shape
    half = R // 2
    tr = _pick(half, 256)
    nt = half // tr

    def body(c_ref, g_ref, l_ref, of_ref, ob_ref):
        s = g_ref[...] + l_ref[...].astype(F32)
        of_ref[...] = s
        ob_ref[...] = s.astype(ob_ref.dtype)

    blk = pl.BlockSpec((1, tr, C), lambda s, i, c_ref: (s, i, 0))
    gblk = pl.BlockSpec((1, tr, C), lambda s, i, c_ref: (s, c_ref[0] * nt + i, 0))
    return pl.pallas_call(
        body, name=name,
        grid_spec=pltpu.PrefetchScalarGridSpec(num_scalar_prefetch=1, grid=(n, nt), in_specs=[gblk, blk],
                                               out_specs=[blk, blk]),
        out_shape=[SDS((n, half, C), F32), SDS((n, half, C), WIRE_DTYPE)],
        compiler_params=_cp("parallel", "parallel"))(c_idx, g, land)


def add_chip_partials(pf, rb, jc_idx, *, name):
    n, H, C = pf.shape
    tr = _pick(H, 256)

    def body(jc_ref, p_ref, r_ref, o_ref):
        s = p_ref[0]
        for k in range(3):
            s = s + r_ref[k].astype(F32)
        o_ref[...] = s

    pblk = pl.BlockSpec((1, tr, C), lambda i, jc_ref: (jc_ref[0], i, 0))
    rblk = pl.BlockSpec((3, tr, C), lambda i, jc_ref: (0, i, 0))
    oblk = pl.BlockSpec((None, tr, C), lambda i, jc_ref: (jc_ref[1], i, 0))
    return pl.pallas_call(
        body, name=name,
        grid_spec=pltpu.PrefetchScalarGridSpec(num_scalar_prefetch=1, grid=(H // tr,), in_specs=[pblk, rblk],
                                               out_specs=oblk),
        out_shape=SDS((2, H, C), F32), compiler_params=_cp("parallel"))(jc_idx, pf, rb)


def join_sibling_halves(bufs, *, name):
    na = len(bufs)

    def body(*refs):
        o_refs = refs[na:2 * na]
        send_sems, recv_sems = refs[2 * na:]
        x, y, c = _coords()
        cps = [_remote(o_refs[a].at[c], o_refs[a].at[c], send_sems, recv_sems, a, (x, y, 1 - c)) for a in range(na)]
        for cp in cps:
            cp.start()
        for a in range(na):
            cps[a].wait_send()
            _remote(o_refs[a].at[1 - c], o_refs[a].at[1 - c], send_sems, recv_sems, a, (x, y, 1 - c)).wait_recv()

    return pl.pallas_call(body, name=name, in_specs=[ANY] * na, out_specs=[ANY] * na,
                          out_shape=[SDS(b.shape, b.dtype) for b in bufs],
                          input_output_aliases={a: a for a in range(na)},
                          scratch_shapes=[pltpu.SemaphoreType.DMA((na,)), pltpu.SemaphoreType.DMA((na,))])(*bufs)


def allreduce_pieces(v, *, name):
    n, P, C = v.shape
    assert n == N_DEV

    def body(v_ref, o_ref, land_ref, mine_ref, send1, recv1, send2, recv2):
        x, y, c = _coords()
        me = 4 * x + 2 * y + c
        peers = []
        for m in range(1, N_DEV):
            px = (1 - x) if m & 4 else x
            py = (1 - y) if m & 2 else y
            pc = (1 - c) if m & 1 else c
            peers.append((m - 1, 4 * px + 2 * py + pc, (px, py, pc)))
        scatter = [(_remote(v_ref.at[lin], land_ref.at[me], send1, recv1, k, dev),
                    _remote(land_ref.at[lin], land_ref.at[lin], send1, recv1, k, dev)) for k, lin, dev in peers]
        for send, _ in scatter:
            send.start()
        land_ref[pl.ds(me, 1)] = v_ref[pl.ds(me, 1)]
        for _, arrival in scatter:
            arrival.wait_recv()
        s = land_ref[0]
        for d in range(1, N_DEV):
            s = s + land_ref[d]
        mine_ref[...] = s
        gather = [(_remote(mine_ref, o_ref.at[me], send2, recv2, k, dev),
                   _remote(o_ref.at[lin], o_ref.at[lin], send2, recv2, k, dev)) for k, lin, dev in peers]
        for send, _ in gather:
            send.start()
        o_ref[pl.ds(me, 1)] = s[None]
        for send, arrival in scatter + gather:
            send.wait_send()
        for _, arrival in gather:
            arrival.wait_recv()

    vmem = pl.BlockSpec(memory_space=pltpu.VMEM)
    sems = [pltpu.SemaphoreType.DMA((N_DEV - 1,))] * 4
    return pl.pallas_call(body, name=name, in_specs=[vmem], out_specs=vmem, out_shape=SDS((n, P, C), F32),
                          scratch_shapes=[pltpu.VMEM((n, P, C), F32), pltpu.VMEM((P, C), F32)] + sems)(v)


BIG_SEGS = (
    ("w_in_even", (1024, 514), 1),
    ("s5_w_glu", (128, 512), 0),
    ("w_out_even", (256, 1024), 0),
    ("w_in_odd", (1024, 384), 1),
    ("w_out_odd", (256, 1024), 0),
    ("mlp_w1", (2, 1024, 1024), 2),
    ("mlp_w2", (2, 1024, 1024), 1),
)
BIG_NAMES = tuple(n for n, _, _ in BIG_SEGS)
EARLY_NAMES = ("w_in_even", "s5_w_glu")
LATE_NAMES = ("w_out_even", "w_in_odd", "w_out_odd", "mlp_w1", "mlp_w2")
REDUCED_EARLY = ("s5_w_glu", "w_out_even", "w_in_odd", "w_out_odd", "mlp_w1", "mlp_w2")
SHARDED_SMALL = ("pool_scale", "sgu_ln_g", "sgu_ln_b")
SMALL_SEGS = (
    ("mix_pre_g", (2, 1024)), ("mix_post_g", (2, 1024)), ("mlp_pre_g", (2, 1024)), ("mlp_post_g", (2, 1024)),
    ("s5_lam_re", (1, 32, 64)), ("s5_lam_im", (1, 32, 64)), ("s5_log_dt", (1, 32)),
    ("s5_b_re", (1, 32, 64, 16)), ("s5_b_im", (1, 32, 64, 16)), ("s5_c_re", (1, 32, 16, 64)), ("s5_c_im", (1, 32, 16, 64)),
    ("s5_d", (1, 512)), ("fox_b_f", (1, 8)), ("pool_w", (1, 4, 128, 128)), ("sgu_w_s", (1, 4, 128, 128)),
    ("sgu_b_s", (1, 4, 128)),
)
REDUCED_SEGS = SMALL_SEGS + tuple((n, (1, 512)) for n in SHARDED_SMALL) + (("loss", (1, 1)),)


def _cols_from_chips(g):
    n, R, C = g.shape
    return jnp.transpose(g, (1, 0, 2)).reshape(R, n * C)


def _chips_from_cols(m):
    R, C4 = m.shape
    return jnp.transpose(m.reshape(R, N_CHIPS, C4 // N_CHIPS), (1, 0, 2))


MLP_SHARD = 1024


def _w1_cols(l):
    def spec(tm, tn, tk):
        per = MLP_SHARD // tn
        return pl.BlockSpec((None, tk, tn), lambda i, j, k: (j // per, l * (MLP_SHARD // tk) + k, j % per))
    return spec


def _w1_rows_t(l):
    def spec(tm, tn, tk):
        if tk == N_CHIPS * MLP_SHARD:
            return pl.BlockSpec((N_CHIPS, tn, MLP_SHARD), lambda i, j, k: (0, l * (MLP_SHARD // tn) + j, 0))
        per = MLP_SHARD // tk
        return pl.BlockSpec((None, tn, tk), lambda i, j, k: (k // per, l * (MLP_SHARD // tn) + j, k % per))
    return spec


def _w2_rows(l):
    def spec(tm, tn, tk):
        if tk == N_CHIPS * MLP_SHARD:
            return pl.BlockSpec((N_CHIPS, MLP_SHARD, tn), lambda i, j, k: (0, l, j))
        per = MLP_SHARD // tk
        return pl.BlockSpec((None, tk, tn), lambda i, j, k: (k // per, l * per + k % per, j))
    return spec


def _w2_rows_t(l):
    def spec(tm, tn, tk):
        per = MLP_SHARD // tn
        return pl.BlockSpec((None, tn, tk), lambda i, j, k: (j // per, l * per + j % per, k))
    return spec


def _dw1_out(l):
    def spec(tm, tn, tk):
        per = MLP_SHARD // tn
        return pl.BlockSpec((None, tm, tn), lambda i, j, k: (j // per, l * (MLP_SHARD // tm) + i, j % per))
    return spec


def _dw2_out(l):
    def spec(tm, tn, tk):
        per = MLP_SHARD // tm
        return pl.BlockSpec((None, tm, tn), lambda i, j, k: (i // per, l * per + i % per, j))
    return spec


def _pack_vec(d, segs, rows_multiple):
    flat = jnp.concatenate([d[n].reshape(-1) for n, _ in segs])
    rows = -(-flat.shape[0] // LANES)
    rows = -(-rows // rows_multiple) * rows_multiple
    return jnp.pad(flat, (0, rows * LANES - flat.shape[0])).reshape(rows, LANES)


def _unpack_vec(v, segs):
    flat, out, r = v.reshape(-1), {}, 0
    for n, shape in segs:
        k = math.prod(shape)
        out[n] = flat[r:r + k].reshape(shape)
        r += k
    return out


def _block_diag(blocks):
    G, a, b = blocks.shape
    eye = jnp.eye(G, dtype=blocks.dtype)
    return (eye[:, None, :, None] * blocks[:, :, None, :]).reshape(G * a, G * b)


def _diag_blocks(m, G):
    a, b = m.shape[0] // G, m.shape[1] // G
    return jnp.stack([m[g * a:(g + 1) * a, g * b:(g + 1) * b] for g in range(G)])


def _sqrelu_epi(acc):
    r = jnp.maximum(acc, 0.0)
    return acc, r * r


def _sqrelu_bwd_epi(acc, a):
    return (acc * (2.0 * jnp.maximum(a.astype(F32), 0.0)),)


def _mlp_fwd(h, g1, g2, l, tag):
    T, D = h.shape
    a, s = matmul(h, g1, name=f"{tag}_up", mnk=(T, D_FF, D), b_spec=_w1_cols(l), epi=_sqrelu_epi,
                  out_dtypes=(MXU_DTYPE, MXU_DTYPE))
    m = matmul(s, g2, name=f"{tag}_down", mnk=(T, D, D_FF), b_spec=_w2_rows(l))
    return m, (h, a, s)


def _mlp_bwd(saved, dm, g1, g2, l, dg1, dg2, tag):
    h, a, s = saved
    T, D = h.shape
    gshape = (N_CHIPS, 2 * MLP_SHARD, MLP_SHARD)
    da = matmul(dm, g2, tb=True, name=f"{tag}_down_dx", mnk=(T, D_FF, D), b_spec=_w2_rows_t(l),
                epi=_sqrelu_bwd_epi, epi_in=(a,), out_dtype=MXU_DTYPE)
    dg2 = matmul(s, dm, ta=True, name=f"{tag}_down_dw", tm=MLP_SHARD, o_spec=_dw2_out(l), o_shape=gshape, prev=dg2)
    dh = matmul(da, g1, tb=True, name=f"{tag}_up_dx", mnk=(T, D, D_FF), b_spec=_w1_rows_t(l))
    dg1 = matmul(h, da, ta=True, name=f"{tag}_up_dw", o_spec=_dw1_out(l), o_shape=gshape, prev=dg1)
    return dh, dg1, dg2


def kernel(x, mix_pre_g, mix_post_g, mlp_pre_g, mlp_post_g, w_in_even, s5_lam_re, s5_lam_im, s5_log_dt, s5_b_re, s5_b_im, s5_c_re, s5_c_im, s5_d, s5_w_glu, fox_b_f, w_out_even, w_in_odd, pool_w, pool_scale, sgu_ln_g, sgu_ln_b, sgu_w_s, sgu_b_s, w_out_odd, mlp_w1, mlp_w2, loss_target, m_mix_pre_g, m_mix_post_g, m_mlp_pre_g, m_mlp_post_g, m_w_in_even, m_s5_lam_re, m_s5_lam_im, m_s5_log_dt, m_s5_b_re, m_s5_b_im, m_s5_c_re, m_s5_c_im, m_s5_d, m_s5_w_glu, m_fox_b_f, m_w_out_even, m_w_in_odd, m_pool_w, m_pool_scale, m_sgu_ln_g, m_sgu_ln_b, m_sgu_w_s, m_sgu_b_s, m_w_out_odd, m_mlp_w1, m_mlp_w2, v_mix_pre_g, v_mix_post_g, v_mlp_pre_g, v_mlp_post_g, v_w_in_even, v_s5_lam_re, v_s5_lam_im, v_s5_log_dt, v_s5_b_re, v_s5_b_im, v_s5_c_re, v_s5_c_im, v_s5_d, v_s5_w_glu, v_fox_b_f, v_w_out_even, v_w_in_odd, v_pool_w, v_pool_scale, v_sgu_ln_g, v_sgu_ln_b, v_sgu_w_s, v_sgu_b_s, v_w_out_odd, v_mlp_w1, v_mlp_w2):
    names = [n for n, _ in SMALL_SEGS] + [n for n, _, _ in BIG_SEGS] + list(SHARDED_SMALL)
    env = dict(locals())
    W = {n: env[n] for n in names}
    M = {n: env["m_" + n] for n in names}
    V = {n: env["v_" + n] for n in names}

    def shard(n):
        return W[n].reshape(-1, W[n].shape[-1]).astype(WIRE_DTYPE)

    small = jnp.pad(jnp.concatenate([W[n] for n in SHARDED_SMALL]), ((0, SUBLANES - len(SHARDED_SMALL)), (0, 0)))
    loss8, dx0, halves, local_small = _local_step(x[0], loss_target[0], {n: W[n] for n, _ in SMALL_SEGS},
                                                  [shard(n) for n in EARLY_NAMES], [shard(n) for n in LATE_NAMES], small)
    return _reduce_and_update(W, M, V, loss8, dx0, halves, local_small)


def _reduce_to_my_half(gs, names, tag, carry_swap=None, carry_ici=None):
    cx, cy, cc = _coords()
    c_idx = cc.reshape(1).astype(jnp.int32)
    jc_idx = jnp.stack([2 * cx + cy, cc]).astype(jnp.int32)
    swap = swap_halves_exchange(gs)
    from_sibling = carry_swap(swap) if carry_swap else run_exchange(swap, name=f"{tag}_to_sibling")
    sums = [add_sibling_half(g, l, c_idx, name=f"{tag}_chip_sum_{n}") for n, g, l in zip(names, gs, from_sibling)]
    shares = carry_ici if isinstance(carry_ici, list) else [(carry_ici, list(range(len(gs))))]
    from_chips = [None] * len(gs)
    for carry, idxs in shares:
        send = chip_partials_exchange([sums[i][1] for i in idxs])
        bufs = carry(send) if carry else run_exchange(send, name=f"{tag}_to_chips")
        for i, b in zip(idxs, bufs):
            from_chips[i] = b
    return [add_chip_partials(pf, r, jc_idx, name=f"{tag}_sum_{n}") for n, (pf, _), r in zip(names, sums, from_chips)]


def _local_step(x0, target, P, early_shards, late_shards, small_shard):
    T = x0.shape[0]
    mix_pre_g, mix_post_g, mlp_pre_g, mlp_post_g = P["mix_pre_g"], P["mix_post_g"], P["mlp_pre_g"], P["mlp_post_g"]
    s5_lam_re, s5_lam_im, s5_log_dt = P["s5_lam_re"], P["s5_lam_im"], P["s5_log_dt"]
    s5_b_re, s5_b_im, s5_c_re, s5_c_im, s5_d = P["s5_b_re"], P["s5_b_im"], P["s5_c_re"], P["s5_c_im"], P["s5_d"]
    fox_b_f, pool_w, sgu_w_s, sgu_b_s = P["fox_b_f"], P["pool_w"], P["sgu_w_s"], P["sgu_b_s"]

    def gain(a, l):
        return a[l][None, :]

    lr = s5_lam_re[0].reshape(1, S5_LANES)
    li = s5_lam_im[0].reshape(1, S5_LANES)
    ldt = jnp.repeat(s5_log_dt[0], S5_STATE).reshape(1, S5_LANES)
    btr = s5_b_re[0].reshape(S5_LANES, S5_GROUP).T
    bti = s5_b_im[0].reshape(S5_LANES, S5_GROUP).T
    tf_re, tf_im, tb_re, tb_im, bbt_re, bbt_im = s5_disc_fwd(lr, li, ldt, btr, bti, name="s5_disc")
    same_group = (jnp.arange(S5_WIDTH)[:, None] // S5_GROUP) == (jnp.arange(S5_LANES)[None, :] // S5_STATE)
    b_bd = s5_interleave(jnp.where(same_group, jnp.tile(bbt_re, (S5_GROUPS, 1)), 0.0),
                         jnp.where(same_group, jnp.tile(bbt_im, (S5_GROUPS, 1)), 0.0), axis=1)
    cr2 = jnp.transpose(s5_c_re[0], (0, 2, 1)).reshape(S5_LANES, S5_GROUP)
    ci2 = jnp.transpose(s5_c_im[0], (0, 2, 1)).reshape(S5_LANES, S5_GROUP)
    c_bd = s5_interleave(jnp.where(same_group.T, jnp.tile(cr2, (1, S5_GROUPS)), 0.0),
                         -jnp.where(same_group.T, jnp.tile(ci2, (1, S5_GROUPS)), 0.0), axis=0)
    bf_pad = jnp.pad(fox_b_f, ((0, 0), (0, LANES - FOX_HEADS)))

    h1, early = rms_fwd(x0, gain(mix_pre_g, 0), allgather_ici_exchange(early_shards), name="l0_pre_norm")
    early = dict(zip(EARLY_NAMES, allgather_forward(early, name="allgather_early_weights")))
    w_in_e = jnp.pad(_cols_from_chips(early["w_in_even"]), ((0, 0), (0, EVEN_IN_PAD - EVEN_IN)))
    w_glu = early["s5_w_glu"].reshape(S5_WIDTH, S5_WIDTH)
    z = matmul(h1, w_in_e, name="l0_in_proj")
    s5_tiles = dict(tm=_pick(T, S5_NB), exact_tiles=True)
    xs = s5_scan(z, b_bd, tf_re, tf_im, reverse=False, name="s5_scan_fwd")
    yc = matmul(xs, c_bd, mnk=(T, S5_WIDTH, 2 * S5_NB), tn=S5_CB, a_spec=_lanes_of_chan, b_spec=_s5_c_block,
                name="s5_cx", **s5_tiles)
    yl, yg = s5_out_fwd(yc, z, s5_d, name="s5_out")
    gl = matmul(yg, w_glu, name="s5_glu_proj")
    ycat = glu_fwd(yg, gl, out_cols=D_MODEL, name="s5_glu")
    fgate = fox_gate_fwd(z, bf_pad, fl_col=FL_TILE, name="fox_gate")
    f_col = _pairs_col(fgate, T)
    f_row = _col_to_row(f_col, T)
    (ycat, lse_col), late = fox_fwd(z, f_col, f_row, ycat, allgather_ici_exchange(late_shards, [small_shard]),
                                    name="fox_fwd")
    small_all = late[-1]
    pool_scale_f, ln_g_f, ln_b_f = (small_all[:, i, :].reshape(1, N_CHIPS * LANES) for i in range(len(SHARDED_SMALL)))
    late = dict(zip(LATE_NAMES, allgather_forward(late[:-1], name="allgather_late_weights")))
    w_in_o = _cols_from_chips(late["w_in_odd"])
    w_in_o = jnp.concatenate([w_in_o[:, S5_WIDTH:], w_in_o[:, :S5_WIDTH]], axis=1)
    w_out_e = late["w_out_even"].reshape(D_MODEL, D_MODEL)
    w_out_o = late["w_out_odd"].reshape(D_MODEL, D_MODEL)
    g1, g2 = late["mlp_w1"], late["mlp_w2"]
    mo = matmul(ycat, w_out_e, name="l0_out_proj")
    x1, h2 = res_norm_fwd(x0, mo, gain(mix_post_g, 0), gain(mlp_pre_g, 0), name="l0_post_mlp0_pre_norm")
    m0, mlp0 = _mlp_fwd(h2, g1, g2, 0, "mlp0")

    x2, h3 = res_norm_fwd(x1, m0, gain(mlp_post_g, 0), gain(mix_pre_g, 1), name="mlp0_post_l1_pre_norm")
    z2 = matmul(h3, w_in_o, name="l1_in_proj")
    pooled = pool_window(z2, adjoint=False, in_col=POOL_COL, out_dtype=MXU_DTYPE, name="pool_fwd")
    pw_bd = _block_diag(pool_w[0])
    pw = matmul(pooled, pw_bd, name="pool_proj")
    ycat2 = colscale_fwd(pw, pool_scale_f, out_cols=D_MODEL, name="pool_scale")
    causal = jnp.tril(jnp.ones((CHUNK, CHUNK), dtype=bool))
    wsm = jnp.where(causal[None], sgu_w_s[0], 0.0)
    wsmt = jnp.transpose(wsm, (0, 2, 1))
    bst = sgu_b_s[0].T
    ycat2 = sgu_fwd(z2, ln_g_f, ln_b_f, wsm, bst, ycat2, name="sgu_fwd")
    mo2 = matmul(ycat2, w_out_o, name="l1_out_proj")
    x3, h4 = res_norm_fwd(x2, mo2, gain(mix_post_g, 1), gain(mlp_pre_g, 1), name="l1_post_mlp1_pre_norm")
    m1, mlp1 = _mlp_fwd(h4, g1, g2, 1, "mlp1")
    loss8, dx4 = res_norm_loss(x3, m1, gain(mlp_post_g, 1), target, name="mlp1_post_norm_loss")

    dm1, dg_mlp_post1 = rms_bwd(m1, gain(mlp_post_g, 1), dx4, None, name="mlp1_post_norm_bwd")
    dh4, dg1, dg2 = _mlp_bwd(mlp1, dm1, g1, g2, 1, None, None, "mlp1")
    dx3, dmo2, dg_mlp_pre1, dg_mix_post1 = norm_res_bwd(x3, gain(mlp_pre_g, 1), dh4, dx4, mo2, gain(mix_post_g, 1),
                                                        name="mlp1_pre_l1_post_norm_bwd")
    dycat2 = matmul(dmo2, w_out_o, tb=True, name="l1_out_proj_dx")
    dw_out_o = matmul(ycat2, dmo2, ta=True, name="l1_out_proj_dw")
    dpw, dpool_scale = colscale_bwd(pw, pool_scale_f, dycat2, name="pool_scale_bwd")
    dpooled = matmul(dpw, pw_bd, tb=True, name="pool_proj_dx")
    dpw_bd = matmul(pooled, dpw, ta=True, name="pool_proj_dw")
    dz2, dln_g, dln_b, dws, dbst = sgu_bwd(z2, ln_g_f, ln_b_f, wsm, wsmt, bst, dycat2, out_cols=3 * S5_WIDTH,
                                           name="sgu_bwd")
    dz2 = pool_window(dpooled, adjoint=True, into=dz2, out_col=POOL_COL, name="pool_bwd")
    dh3 = matmul(dz2, w_in_o, tb=True, name="l1_in_proj_dx")
    dw_in_o = matmul(h3, dz2, ta=True, name="l1_in_proj_dw")
    dw_in_o = jnp.concatenate([dw_in_o[:, 2 * S5_WIDTH:], dw_in_o[:, :2 * S5_WIDTH]], axis=1)
    dx2, dm0, dg_mix_pre1, dg_mlp_post0 = norm_res_bwd(x2, gain(mix_pre_g, 1), dh3, dx3, m0, gain(mlp_post_g, 0),
                                                       name="l1_pre_mlp0_post_norm_bwd")

    dh2, dg1, dg2 = _mlp_bwd(mlp0, dm0, g1, g2, 0, dg1, dg2, "mlp0")
    dx1, dmo, dg_mlp_pre0, dg_mix_post0 = norm_res_bwd(x1, gain(mlp_pre_g, 0), dh2, dx2, mo, gain(mix_post_g, 0),
                                                       name="mlp0_pre_l0_post_norm_bwd")
    dycat = matmul(dmo, w_out_e, tb=True, name="l0_out_proj_dx")
    dw_out_e = matmul(ycat, dmo, ta=True, name="l0_out_proj_dw")
    dyg_a, dgl = glu_bwd(yg, gl, dycat, name="s5_glu_bwd")
    dyg_b = matmul(dgl, w_glu, tb=True, name="s5_glu_proj_dx")
    dw_glu = matmul(yg, dgl, ta=True, name="s5_glu_proj_dw")
    dyl, du_skip, dd = s5_out_bwd(yl, z, s5_d, dyg_a, dyg_b, name="s5_out_bwd")
    dc_blocks = matmul(xs, dyl, ta=True, mnk=(2 * S5_LANES, S5_CB, T), tm=S5_NB, tn=S5_CB, b_spec=_chan_cols_of_i,
                       exact_tiles=True, name="s5_cx_dw")
    early_grads = {"s5_w_glu": dw_glu.reshape(N_CHIPS, -1, S5_WIDTH), "w_out_even": dw_out_e.reshape(N_CHIPS, -1, D_MODEL),
                   "w_in_odd": _chips_from_cols(dw_in_o), "w_out_odd": dw_out_o.reshape(N_CHIPS, -1, D_MODEL),
                   "mlp_w1": dg1, "mlp_w2": dg2}
    got = {}

    def reverse_scan(exchange):
        (got["lam"], got["dab_re"], got["dab_im"]), bufs = s5_scan(dyl, c_bd, tb_re, tb_im, reverse=True, states=xs,
                                                                   hosted=exchange, name="s5_scan_bwd")
        return bufs

    def attention_bwd(exchange):
        dd_col = fox_dd(ycat, dycat, name="fox_dd")
        (got["dk"], got["dv"], got["dfk"], got["dqt"], got["dfq"]), bufs = fox_bwd(
            z, dycat, f_col, f_row, _col_to_row(lse_col, T), _col_to_row(dd_col, T), exchange, name="fox_bwd")
        return bufs

    def input_matrix_grad(exchange):
        got["db_blocks"], bufs = matmul(z, got["lam"], ta=True, mnk=(S5_CB, 2 * S5_LANES, T), tm=S5_CB, tn=S5_NB,
                                        a_spec=_chan_rows_t, exact_tiles=True, hosted=exchange, name="s5_bu_dw")
        return bufs

    ici_small = [REDUCED_EARLY.index(n) for n in ("w_out_even", "w_in_odd")]
    ici_rest = [i for i in range(len(REDUCED_EARLY)) if i not in ici_small]
    halves = _reduce_to_my_half([early_grads[n] for n in REDUCED_EARLY], REDUCED_EARLY, "early_grads",
                                reverse_scan, [(input_matrix_grad, ici_small), (attention_bwd, ici_rest)])
    lam, dab_re, dab_im, dk, dv = got["lam"], got["dab_re"], got["dab_im"], got["dk"], got["dv"]
    db_blocks = got["db_blocks"]
    du_b = matmul(lam, b_bd, tb=True, mnk=(T, S5_WIDTH, 2 * S5_NB), tn=S5_CB, a_spec=_lanes_of_chan,
                  b_spec=_s5_b_block_t, name="s5_bu_dx", **s5_tiles)
    du = add2(du_skip, du_b, name="s5_du")
    dq = jnp.transpose(got["dqt"], (1, 3, 0, 2)).reshape(T, FOX_WIDTH) * (FOX_HEAD_DIM ** -0.5)
    dfl, dbf = fox_gate_bwd(z, bf_pad, _pairs_to_lanes(got["dfk"], T), _pairs_to_lanes(_row_to_col(got["dfq"], T), T),
                            fl_col=FL_TILE, name="fox_gate_bwd")
    dz = jnp.concatenate([du, dq, dk, dv, dfl], axis=1).astype(MXU_DTYPE)
    dw_in_e = matmul(h1, dz, ta=True, name="l0_in_proj_dw")[:, :EVEN_IN]

    def in_proj_dx(exchange):
        got["dh1"], bufs = matmul(dz, w_in_e, tb=True, hosted=exchange, name="l0_in_proj_dx")
        return bufs

    def pre_norm_bwd(exchange):
        (got["dx0"], got["dg_mix_pre0"]), bufs = rms_bwd(x0, gain(mix_pre_g, 0), got["dh1"], dx1, hosted=exchange,
                                                         name="l0_pre_norm_bwd")
        return bufs

    halves = halves + _reduce_to_my_half([_chips_from_cols(dw_in_e)], ["w_in_even"], "late_grads", in_proj_dx, pre_norm_bwd)
    dx0, dg_mix_pre0 = got["dx0"], got["dg_mix_pre0"]

    groups_per_block = S5_CB // S5_GROUP
    own_group = (jnp.arange(S5_CB)[:, None] // S5_GROUP) == ((jnp.arange(S5_LANES)[None, :] // S5_STATE) % groups_per_block)
    db_re, db_im = s5_deinterleave(db_blocks, axis=1)
    dbbt_re = jnp.where(own_group, db_re, 0.0).reshape(groups_per_block, S5_GROUP, S5_LANES).sum(0)
    dbbt_im = jnp.where(own_group, db_im, 0.0).reshape(groups_per_block, S5_GROUP, S5_LANES).sum(0)
    dlr, dli, dldt8, dbtr, dbti = s5_disc_bwd(lr, li, ldt, btr, bti, dab_re, dab_im, dbbt_re, dbbt_im, name="s5_disc_bwd")
    dc_re, dc_im = s5_deinterleave(dc_blocks, axis=0)
    dcr2 = jnp.where(own_group.T, dc_re, 0.0).reshape(S5_LANES, groups_per_block, S5_GROUP).sum(1)
    dci2 = -jnp.where(own_group.T, dc_im, 0.0).reshape(S5_LANES, groups_per_block, S5_GROUP).sum(1)

    def c_layout(a):
        return jnp.transpose(a.reshape(S5_GROUPS, S5_STATE, S5_GROUP), (0, 2, 1))[None]

    def b_layout(a):
        return a.T.reshape(1, S5_GROUPS, S5_STATE, S5_GROUP)

    local_small = {
        "mix_pre_g": jnp.concatenate([dg_mix_pre0, dg_mix_pre1]), "mix_post_g": jnp.concatenate([dg_mix_post0, dg_mix_post1]),
        "mlp_pre_g": jnp.concatenate([dg_mlp_pre0, dg_mlp_pre1]), "mlp_post_g": jnp.concatenate([dg_mlp_post0, dg_mlp_post1]),
        "s5_lam_re": dlr.reshape(1, S5_GROUPS, S5_STATE), "s5_lam_im": dli.reshape(1, S5_GROUPS, S5_STATE),
        "s5_log_dt": dldt8[0:1, 0:S5_GROUPS],
        "s5_b_re": b_layout(dbtr), "s5_b_im": b_layout(dbti), "s5_c_re": c_layout(dcr2), "s5_c_im": c_layout(dci2),
        "s5_d": dd, "fox_b_f": dbf[:, 0:FOX_HEADS],
        "pool_w": _diag_blocks(dpw_bd, len(POOL_WINDOWS))[None],
        "sgu_w_s": jnp.where(causal[None], dws, 0.0)[None], "sgu_b_s": dbst.T[None],
        "pool_scale": dpool_scale, "sgu_ln_g": dln_g, "sgu_ln_b": dln_b,
    }
    return loss8, dx0, dict(zip(REDUCED_EARLY + ("w_in_even",), halves)), local_small


def _reduce_and_update(W, M, V, loss8, dx0, halves, local_small):
    cx, cy, cc = _coords()
    chip = 2 * cx + cy

    summed = dict(local_small, loss=loss8[0:1, 0:1])
    vec = _pack_vec(summed, REDUCED_SEGS, N_DEV * SUBLANES)
    piece = vec.shape[0] // N_DEV
    everyone = allreduce_pieces(vec.reshape(N_DEV, piece, LANES), name="small_grads_allreduce")
    G = _unpack_vec(everyone, REDUCED_SEGS)
    loss = G["loss"].reshape(())
    for n in SHARDED_SMALL:
        G[n] = lax.dynamic_slice_in_dim(G[n], chip * LANES, LANES, axis=1)

    reduced = join_sibling_halves([halves[n] for n in BIG_NAMES], name="big_grads_join")
    for n, r in zip(BIG_NAMES, reduced):
        G[n] = r.reshape(W[n].shape)

    def two_d(a):
        return a.reshape(-1, a.shape[-1])

    delta, new_m, new_v = {}, {}, {}
    for n in BIG_NAMES:
        d_, m_, v_ = adamw(two_d(W[n]), two_d(G[n]), two_d(M[n]), two_d(V[n]), name=f"adamw_{n}")
        delta[n], new_m[n], new_v[n] = (t.reshape(W[n].shape) for t in (d_, m_, v_))
    packed = [_pack_vec(src, SMALL_SEGS, SUBLANES) for src in (W, G, M, V)]
    outs = adamw(*packed, name="adamw_replicated")
    for dst, t in zip((delta, new_m, new_v), outs):
        dst.update(_unpack_vec(t, SMALL_SEGS))
    sharded_segs = tuple((n, (1, LANES)) for n in SHARDED_SMALL)
    packed = [_pack_vec(src, sharded_segs, 1) for src in (W, G, M, V)]
    outs = adamw(*packed, name="adamw_sharded_vectors")
    for dst, t in zip((delta, new_m, new_v), outs):
        dst.update(_unpack_vec(t, sharded_segs))

    order = ["mix_pre_g", "mix_post_g", "mlp_pre_g", "mlp_post_g", "w_in_even", "s5_lam_re", "s5_lam_im", "s5_log_dt",
             "s5_b_re", "s5_b_im", "s5_c_re", "s5_c_im", "s5_d", "s5_w_glu", "fox_b_f", "w_out_even", "w_in_odd",
             "pool_w", "pool_scale", "sgu_ln_g", "sgu_ln_b", "sgu_w_s", "sgu_b_s", "w_out_odd", "mlp_w1", "mlp_w2"]
    return (loss, dx0[None], *[G[n] for n in order], *[delta[n] for n in order],
            *[new_m[n] for n in order], *[new_v[n] for n in order])
```

```python
import functools
import math

import jax
import jax.numpy as jnp
from jax import lax
from jax.experimental import pallas as pl
from jax.experimental.pallas import tpu as pltpu

F32 = jnp.float32
MXU_DTYPE = jnp.bfloat16
WIRE_DTYPE = jnp.bfloat16
EPS = 1e-6
VMEM_LIMIT_BYTES = 48 * 1024 * 1024
LANES = 128
SUBLANES = 8
ROW_TILE = 1024

D_MODEL = 1024
S5_WIDTH = 512
S5_GROUP = 16
S5_GROUPS = 32
S5_STATE = 64
S5_LANES = S5_GROUPS * S5_STATE
FOX_HEADS = 8
FOX_HEAD_DIM = 64
FOX_WIDTH = 512
EVEN_IN = S5_WIDTH + 3 * FOX_WIDTH + FOX_HEADS
EVEN_IN_PAD = 2176
POOL_WINDOWS = (2, 4, 8, 16)
POOL_HALO = 16
POOL_GROUP_DIM = 128
SGU_GROUPS = 4
SGU_GROUP_DIM = 128
CHUNK = 128
D_FF = 4096

ADAM_LR = 0.001
ADAM_B1 = 0.9
ADAM_B2 = 0.999
ADAM_EPS = 1e-08
ADAM_WD = 0.01
ADAM_STEP = 10

MESH_AXES = ("x", "y", "c")
MESH = pl.DeviceIdType.MESH
N_CHIPS = 4
N_DEV = 8

SDS = jax.ShapeDtypeStruct


def _cp(*sem):
    return pltpu.CompilerParams(dimension_semantics=sem, vmem_limit_bytes=VMEM_LIMIT_BYTES)


def _pick(dim, pref):
    if dim <= pref:
        return dim
    t = pref
    while t >= 256:
        if dim % t == 0:
            return t
        t //= 2
    return dim


def _row(tr, c):
    return pl.BlockSpec((tr, c), lambda i: (i, 0))


def _full(shape):
    nd = len(shape)
    return pl.BlockSpec(shape, lambda *_: (0,) * nd)


def _gelu_grad(x):
    c = math.sqrt(2.0 / math.pi)
    t = jnp.tanh(c * (x + 0.044715 * x * x * x))
    return 0.5 * (1.0 + t) + 0.5 * x * (1.0 - t * t) * c * (1.0 + 3.0 * 0.044715 * x * x)


MATMUL_VMEM_BYTES = 36 * 1024 * 1024


def matmul(a, b, *, name, ta=False, tb=False, out_dtype=F32, tm=2048, tn=1024, tk=4096, mnk=None, a_koff=0,
           a_spec=None, b_spec=None, o_spec=None, o_shape=None, prev=None, epi=None, epi_in=(), out_dtypes=None,
           exact_tiles=False, hosted=None):
    if mnk is None:
        M, K = (a.shape[1], a.shape[0]) if ta else a.shape
        K2, N = (b.shape[1], b.shape[0]) if tb else b.shape
        assert K == K2, (a.shape, b.shape, ta, tb)
    else:
        M, N, K = mnk
    out_dtypes = tuple(out_dtypes) if out_dtypes is not None else (out_dtype,)
    n_out, n_epi = len(out_dtypes), len(epi_in)
    tm, tn, tk = _pick(M, tm), _pick(N, tn), _pick(K, tk)

    def vmem_bytes(tm_, tn_, tk_):
        tiles = tm_ * tk_ * a.dtype.itemsize + tk_ * tn_ * b.dtype.itemsize
        tiles += tm_ * tn_ * (sum(jnp.dtype(d).itemsize for d in out_dtypes) + sum(e.dtype.itemsize for e in epi_in))
        return 2 * tiles + tm_ * tn_ * 4 * (tk_ < K)

    def halves(t, dim):
        return [t] + ([t // 2] if t % (2 * LANES) == 0 and t // 2 >= 512 and dim % (t // 2) == 0 else [])

    if exact_tiles:
        halves = lambda t, dim: [t]
    fits = [(m_, n_) for m_ in halves(tm, M) for n_ in halves(tn, N) if vmem_bytes(m_, n_, tk) <= MATMUL_VMEM_BYTES]
    if fits:
        tm, tn = max(fits, key=lambda t: (t[0] * t[1], t[0]))
    else:
        tm, tn = halves(tm, M)[-1], halves(tn, N)[-1]
        while vmem_bytes(tm, tn, tk) > MATMUL_VMEM_BYTES and tk % 2 == 0 and tk > 512:
            tk //= 2
    nk = K // tk
    assert a_koff % tk == 0 and not (ta and a_koff)
    ko = a_koff // tk
    dn = (((0 if ta else 1,), (1 if tb else 0,)), ((), ()))

    def body(*refs):
        a_ref, b_ref = refs[0], refs[1]
        epi_refs = refs[2:2 + n_epi]
        o_refs = refs[len(refs) - n_out - (nk > 1):len(refs) - (nk > 1)]
        k = pl.program_id(2)
        bv = b_ref[...]
        if bv.ndim == 3 and tb:
            cw = bv.shape[-1]
            prod = sum(lax.dot_general(a_ref[:, c * cw:(c + 1) * cw].astype(MXU_DTYPE), bv[c].astype(MXU_DTYPE), dn,
                                       preferred_element_type=F32) for c in range(bv.shape[0]))
        else:
            if bv.ndim == 3:
                bv = bv.reshape(-1, bv.shape[-1])
            prod = lax.dot_general(a_ref[...].astype(MXU_DTYPE), bv.astype(MXU_DTYPE), dn, preferred_element_type=F32)

        def finish(acc):
            res = (acc,) if epi is None else epi(acc, *[r[...] for r in epi_refs])
            for o_ref, r in zip(o_refs, res):
                o_ref[...] = r.astype(o_ref.dtype)

        if nk == 1:
            finish(prod)
            return
        acc_ref = refs[-1]

        @pl.when(k == 0)
        def _():
            acc_ref[...] = prod

        @pl.when(jnp.logical_and(k > 0, k < nk - 1))
        def _():
            acc_ref[...] += prod

        @pl.when(k == nk - 1)
        def _():
            finish(acc_ref[...] + prod)

    if a_spec is None:
        a_spec = pl.BlockSpec((tk, tm), lambda i, j, k: (k, i)) if ta else pl.BlockSpec((tm, tk), lambda i, j, k: (i, k + ko))
    else:
        a_spec = a_spec(tm, tn, tk)
    if b_spec is None:
        bs = pl.BlockSpec((tn, tk), lambda i, j, k: (j, k)) if tb else pl.BlockSpec((tk, tn), lambda i, j, k: (k, j))
    else:
        bs = b_spec(tm, tn, tk)
    tile = pl.BlockSpec((tm, tn), lambda i, j, k: (i, j))
    os_ = tile if o_spec is None else o_spec(tm, tn, tk)
    ins, in_specs, aliases = [a, b, *epi_in], [a_spec, bs] + [tile] * n_epi, {}
    if prev is not None:
        aliases = {len(ins): 0}
        ins.append(prev)
        in_specs.append(pl.BlockSpec(memory_space=pl.ANY))
    shapes = [SDS((M, N) if o_shape is None else o_shape, dt) for dt in out_dtypes]
    scratch = [pltpu.VMEM((tm, tn), F32)] if nk > 1 else []
    if hosted is not None:
        outs, bufs = call_hosting(body, hosted, name=name, grid=(M // tm, N // tn, nk), in_specs=in_specs,
                                  out_specs=[os_] * n_out, out_shape=shapes, inputs=ins, aliases=aliases, scratch=scratch)
        return (outs[0] if n_out == 1 else outs), bufs
    outs = pl.pallas_call(
        body, name=name, grid=(M // tm, N // tn, nk),
        in_specs=in_specs, out_specs=[os_] * n_out, out_shape=shapes, input_output_aliases=aliases,
        scratch_shapes=scratch, compiler_params=_cp("parallel", "parallel", "arbitrary"),
    )(*ins)
    return outs[0] if n_out == 1 else outs


def _rms_hat(x):
    return x * lax.rsqrt(jnp.mean(x * x, axis=-1, keepdims=True) + EPS)


def rms_fwd(x, g, hosted, *, name):
    T, D = x.shape
    tr = _pick(T, ROW_TILE)

    def body(x_ref, g_ref, o_ref):
        o_ref[...] = (_rms_hat(x_ref[...]) * g_ref[...]).astype(o_ref.dtype)

    (h,), bufs = call_hosting(body, hosted, name=name, grid=(T // tr,), in_specs=[_row(tr, D), _full((1, D))],
                              out_specs=[_row(tr, D)], out_shape=[SDS((T, D), MXU_DTYPE)], inputs=[x, g], aliases={})
    return h, bufs


def res_norm_fwd(x, y, g_post, g_next, *, name):
    T, D = x.shape
    tr = _pick(T, ROW_TILE)

    def body(x_ref, y_ref, gp_ref, gn_ref, o_ref, h_ref):
        xn = x_ref[...] + _rms_hat(y_ref[...]) * gp_ref[...]
        o_ref[...] = xn
        h_ref[...] = (_rms_hat(xn) * gn_ref[...]).astype(h_ref.dtype)

    return pl.pallas_call(body, name=name, grid=(T // tr,),
                          in_specs=[_row(tr, D), _row(tr, D), _full((1, D)), _full((1, D))],
                          out_specs=[_row(tr, D), _row(tr, D)], out_shape=[SDS((T, D), F32), SDS((T, D), MXU_DTYPE)],
                          compiler_params=_cp("parallel"))(x, y, g_post, g_next)


def res_norm_loss(x, y, g_post, target, *, name):
    T, D = x.shape
    tr = _pick(T, ROW_TILE)

    def body(x_ref, y_ref, g_ref, t_ref, l_ref, d_ref):
        err = x_ref[...] + _rms_hat(y_ref[...]) * g_ref[...] - t_ref[...]
        d_ref[...] = err * (1.0 / D)

        @pl.when(pl.program_id(0) == 0)
        def _():
            l_ref[...] = jnp.zeros_like(l_ref)

        l_ref[...] += 0.5 * jnp.sum(jnp.mean(err * err, axis=-1, keepdims=True))

    return pl.pallas_call(body, name=name, grid=(T // tr,),
                          in_specs=[_row(tr, D), _row(tr, D), _full((1, D)), _row(tr, D)],
                          out_specs=[_full((SUBLANES, LANES)), _row(tr, D)],
                          out_shape=[SDS((SUBLANES, LANES), F32), SDS((T, D), F32)],
                          compiler_params=_cp("arbitrary"))(x, y, g_post, target)


def _rms_bwd_rows(x, g, dy):
    r = lax.rsqrt(jnp.mean(x * x, axis=-1, keepdims=True) + EPS)
    xh = x * r
    dxh = dy * g
    return r * (dxh - xh * jnp.mean(dxh * xh, axis=-1, keepdims=True)), jnp.sum(dy * xh, axis=0, keepdims=True)


def norm_res_bwd(x, g_pre, dh, res, y, g_post, *, name):
    T, D = x.shape
    tr = _pick(T, ROW_TILE // 2)

    def body(x_ref, gp_ref, dh_ref, res_ref, y_ref, gy_ref, dx_ref, dy_ref, dgp_ref, dgy_ref):
        dx, dgp = _rms_bwd_rows(x_ref[...], gp_ref[...], dh_ref[...])
        dx = dx + res_ref[...]
        dx_ref[...] = dx
        dy, dgy = _rms_bwd_rows(y_ref[...], gy_ref[...], dx)
        dy_ref[...] = dy.astype(dy_ref.dtype)

        @pl.when(pl.program_id(0) == 0)
        def _():
            dgp_ref[...] = jnp.zeros_like(dgp_ref)
            dgy_ref[...] = jnp.zeros_like(dgy_ref)

        dgp_ref[...] += dgp
        dgy_ref[...] += dgy

    row, vec = _row(tr, D), _full((1, D))
    return pl.pallas_call(body, name=name, grid=(T // tr,), in_specs=[row, vec, row, row, row, vec],
                          out_specs=[row, row, vec, vec],
                          out_shape=[SDS((T, D), F32), SDS((T, D), MXU_DTYPE), SDS((1, D), F32), SDS((1, D), F32)],
                          compiler_params=_cp("arbitrary"))(x, g_pre, dh, res, y, g_post)


def rms_bwd(x, g, dy, res, *, name, hosted=None):
    T, D = x.shape
    tr = _pick(T, ROW_TILE)
    has_res = res is not None

    def body(*refs):
        if has_res:
            x_ref, g_ref, dy_ref, res_ref, dx_ref, dg_ref = refs
        else:
            x_ref, g_ref, dy_ref, dx_ref, dg_ref = refs
        dx, dg = _rms_bwd_rows(x_ref[...], g_ref[...], dy_ref[...])
        if has_res:
            dx = dx + res_ref[...]
        dx_ref[...] = dx.astype(dx_ref.dtype)

        @pl.when(pl.program_id(0) == 0)
        def _():
            dg_ref[...] = jnp.zeros_like(dg_ref)

        dg_ref[...] += dg

    ins = [x, g, dy] + ([res] if has_res else [])
    in_specs = [_row(tr, D), _full((1, D)), _row(tr, D)] + ([_row(tr, D)] if has_res else [])
    out_shape = [SDS((T, D), F32 if has_res else MXU_DTYPE), SDS((1, D), F32)]
    out_specs = [_row(tr, D), _full((1, D))]
    if hosted is not None:
        return call_hosting(body, hosted, name=name, grid=(T // tr,), in_specs=in_specs, out_specs=out_specs,
                            out_shape=out_shape, inputs=ins, aliases={})
    return pl.pallas_call(body, name=name, grid=(T // tr,), in_specs=in_specs, out_specs=out_specs,
                          out_shape=out_shape, compiler_params=_cp("arbitrary"))(*ins)


def _s5_disc(lr, li, ldt, btr, bti):
    dt = jnp.exp(ldt)
    k = lax.broadcasted_iota(jnp.int32, (SUBLANES, S5_LANES), 0).astype(F32)
    kf = k + 1.0
    kb = 8.0 - k
    ph = li * dt
    lm = lr * dt
    tf_re = jnp.exp(kf * lm) * jnp.cos(kf * ph)
    tf_im = jnp.exp(kf * lm) * jnp.sin(kf * ph)
    tb_re = jnp.exp(kb * lm) * jnp.cos(kb * ph)
    tb_im = -jnp.exp(kb * lm) * jnp.sin(kb * ph)
    mag = jnp.exp(lm)
    ab_re = mag * jnp.cos(ph)
    ab_im = mag * jnp.sin(ph)
    den = lr * lr + li * li
    nr = ab_re - 1.0
    ni = ab_im
    q_re = (nr * lr + ni * li) / den
    q_im = (ni * lr - nr * li) / den
    bbt_re = q_re * btr - q_im * bti
    bbt_im = q_re * bti + q_im * btr
    return tf_re, tf_im, tb_re, tb_im, bbt_re, bbt_im


def _s5_disc_core(lr, li, ldt, btr, bti):
    dt = jnp.exp(ldt)
    mag = jnp.exp(lr * dt)
    ab_re = mag * jnp.cos(li * dt)
    ab_im = mag * jnp.sin(li * dt)
    den = lr * lr + li * li
    nr = ab_re - 1.0
    ni = ab_im
    q_re = (nr * lr + ni * li) / den
    q_im = (ni * lr - nr * li) / den
    return ab_re, ab_im, q_re * btr - q_im * bti, q_re * bti + q_im * btr


def s5_disc_fwd(lr, li, ldt, btr, bti, *, name):
    def body(lr_ref, li_ref, ldt_ref, btr_ref, bti_ref, *outs):
        vals = _s5_disc(lr_ref[...], li_ref[...], ldt_ref[...], btr_ref[...], bti_ref[...])
        for o, v in zip(outs, vals):
            o[...] = v

    tab = SDS((SUBLANES, S5_LANES), F32)
    bb = SDS((S5_GROUP, S5_LANES), F32)
    return pl.pallas_call(body, name=name, out_shape=[tab, tab, tab, tab, bb, bb])(lr, li, ldt, btr, bti)


def s5_disc_bwd(lr, li, ldt, btr, bti, dab_re, dab_im, dbbt_re, dbbt_im, *, name):
    def body(lr_ref, li_ref, ldt_ref, btr_ref, bti_ref, dar_ref, dai_ref, dbr_ref, dbi_ref,
             dlr_ref, dli_ref, dldt_ref, dbtr_ref, dbti_ref):
        _, vjp = jax.vjp(_s5_disc_core, lr_ref[...], li_ref[...], ldt_ref[...], btr_ref[...], bti_ref[...])
        dlr, dli, dldt, dbtr, dbti = vjp((dar_ref[...], dai_ref[...], dbr_ref[...], dbi_ref[...]))
        dlr_ref[...] = dlr
        dli_ref[...] = dli
        dbtr_ref[...] = dbtr
        dbti_ref[...] = dbti
        lane_group = lax.broadcasted_iota(jnp.int32, (S5_LANES, LANES), 0) // S5_STATE
        col = lax.broadcasted_iota(jnp.int32, (S5_LANES, LANES), 1)
        ind = (lane_group == col).astype(F32)
        dldt_ref[...] = jnp.dot(jnp.broadcast_to(dldt, (SUBLANES, S5_LANES)), ind,
                                precision=lax.Precision.HIGHEST, preferred_element_type=F32)

    row = SDS((1, S5_LANES), F32)
    bb = SDS((S5_GROUP, S5_LANES), F32)
    return pl.pallas_call(body, name=name, out_shape=[row, row, SDS((SUBLANES, LANES), F32), bb, bb])(
        lr, li, ldt, btr, bti, dab_re, dab_im, dbbt_re, dbbt_im)


S5_NB = 1024


S5_CB = S5_WIDTH * S5_NB // S5_LANES


def _chan_rows_t(tm, tn, tk):
    return pl.BlockSpec((tk, S5_CB), lambda i, j, k: (k, j // 2))


def _chan_cols_of_i(tm, tn, tk):
    return pl.BlockSpec((tk, S5_CB), lambda i, j, k: (k, i // 2))


def _lanes_of_chan(tm, tn, tk):
    return pl.BlockSpec((tm, 2 * S5_NB), lambda i, j, k: (i, j))


def _s5_b_block_t(tm, tn, tk):
    return pl.BlockSpec((S5_CB, 2 * S5_NB), lambda i, j, k: (j, j))


def _s5_c_block(tm, tn, tk):
    return pl.BlockSpec((2 * S5_NB, S5_CB), lambda i, j, k: (j, j))


def s5_interleave(re, im, axis):
    parts = []
    for n in range(S5_LANES // S5_NB):
        sl = [slice(None)] * re.ndim
        sl[axis] = slice(n * S5_NB, (n + 1) * S5_NB)
        parts += [re[tuple(sl)], im[tuple(sl)]]
    return jnp.concatenate(parts, axis=axis)


def s5_deinterleave(a, axis):
    re, im = [], []
    for n in range(S5_LANES // S5_NB):
        sl = [slice(None)] * a.ndim
        sl[axis] = slice(2 * n * S5_NB, (2 * n + 1) * S5_NB)
        re.append(a[tuple(sl)])
        sl[axis] = slice((2 * n + 1) * S5_NB, (2 * n + 2) * S5_NB)
        im.append(a[tuple(sl)])
    return jnp.concatenate(re, axis=axis), jnp.concatenate(im, axis=axis)


def s5_scan(src, mat, tab_re, tab_im, *, reverse, name, states=None, hosted=None):
    T = src.shape[0]
    nb = S5_NB
    tc = _pick(T, 256)
    nl = S5_LANES // nb
    nt = T // tc
    ntile = tc // SUBLANES
    with_da = states is not None
    assert reverse or not with_da
    step_rows = ((1, 7), (2, 6), (4, 4)) if reverse else ((1, 0), (2, 1), (4, 3))
    drive_dn = _NT if reverse else (((1,), (0,)), ((), ()))

    def body(*refs):
        if with_da:
            (src_ref, wr_ref, wi_ref, tr_ref, ti_ref, sr_ref, si_ref, hr_ref, hi_ref, xo_ref, dar_ref, dai_ref,
             cr_ref, ci_ref, mr_ref, mi_ref, br_ref, bi_ref, ar_ref, ai_ref) = refs
        else:
            src_ref, wr_ref, wi_ref, tr_ref, ti_ref, xo_ref, cr_ref, ci_ref, mr_ref, mi_ref, br_ref, bi_ref = refs

        @pl.when(pl.program_id(1) == 0)
        def _():
            cr_ref[...] = jnp.zeros_like(cr_ref)
            ci_ref[...] = jnp.zeros_like(ci_ref)
            if with_da:
                ar_ref[...] = jnp.zeros_like(ar_ref)
                ai_ref[...] = jnp.zeros_like(ai_ref)

        lhs = src_ref[...].astype(MXU_DTYPE)
        br_ref[...] = lax.dot_general(lhs, wr_ref[...].astype(MXU_DTYPE), drive_dn, preferred_element_type=F32)
        bi_ref[...] = lax.dot_general(lhs, wi_ref[...].astype(MXU_DTYPE), drive_dn, preferred_element_type=F32)

        seen = jnp.where(pl.program_id(1) < nt - 1, 1.0, 0.0)

        def add_da(lr, li, r0, last_r, last_i):
            first = lax.broadcasted_iota(jnp.int32, (SUBLANES, nb), 0) == 0
            pr = jnp.where(first, last_r, pltpu.roll(sr_ref[pl.ds(r0, SUBLANES), :], 1, 0))
            pi = jnp.where(first, last_i, pltpu.roll(si_ref[pl.ds(r0, SUBLANES), :], 1, 0))
            ar_ref[...] += lr * pr + li * pi
            ai_ref[...] += li * pr - lr * pi

        io = lax.broadcasted_iota(jnp.int32, (SUBLANES, nb), 0)
        for s_, (d, r) in enumerate(step_rows):
            keep = (io < SUBLANES - d) if reverse else (io >= d)
            mr_ref[s_] = jnp.where(keep, tr_ref[r:r + 1, :], 0.0)
            mi_ref[s_] = jnp.where(keep, ti_ref[r:r + 1, :], 0.0)

        def tile(i, carry):
            cr, ci = carry
            j = (ntile - 1 - i) if reverse else i
            r0 = pl.multiple_of(j * SUBLANES, SUBLANES)
            xr = br_ref[pl.ds(r0, SUBLANES), :]
            xi = bi_ref[pl.ds(r0, SUBLANES), :]
            for s_, (d, _) in enumerate(step_rows):
                sh = (SUBLANES - d) if reverse else d
                sr = pltpu.roll(xr, sh, 0)
                si = pltpu.roll(xi, sh, 0)
                pr, pi = mr_ref[s_], mi_ref[s_]
                xr, xi = xr + pr * sr - pi * si, xi + pr * si + pi * sr
            tr, ti = tr_ref[...], ti_ref[...]
            xr, xi = xr + tr * cr - ti * ci, xi + tr * ci + ti * cr
            xo_ref[pl.ds(r0, SUBLANES), 0:nb] = xr
            xo_ref[pl.ds(r0, SUBLANES), nb:2 * nb] = xi
            if with_da:
                @pl.when(j > 0)
                def _():
                    p0 = pl.multiple_of(r0 - SUBLANES, SUBLANES)
                    add_da(xr, xi, r0, sr_ref[pl.ds(p0, SUBLANES), :][SUBLANES - 1:SUBLANES, :],
                           si_ref[pl.ds(p0, SUBLANES), :][SUBLANES - 1:SUBLANES, :])

                @pl.when(j == 0)
                def _():
                    add_da(xr, xi, r0, hr_ref[SUBLANES - 1:SUBLANES, :] * seen, hi_ref[SUBLANES - 1:SUBLANES, :] * seen)
            if reverse:
                return xr[0:1, :], xi[0:1, :]
            return xr[SUBLANES - 1:SUBLANES, :], xi[SUBLANES - 1:SUBLANES, :]

        cr, ci = lax.fori_loop(0, ntile, tile, (cr_ref[0:1, :], ci_ref[0:1, :]))
        cr_ref[0:1, :] = cr
        ci_ref[0:1, :] = ci
        if with_da:
            @pl.when(pl.program_id(1) == nt - 1)
            def _():
                dar_ref[...] = jnp.sum(ar_ref[...], axis=0, keepdims=True)
                dai_ref[...] = jnp.sum(ai_ref[...], axis=0, keepdims=True)

    def tmap(t):
        return (nt - 1 - t) if reverse else t

    hb = tc // SUBLANES
    re_spec = pl.BlockSpec((tc, nb), lambda n, t: (tmap(t), 2 * n))
    im_spec = pl.BlockSpec((tc, nb), lambda n, t: (tmap(t), 2 * n + 1))
    tab_spec = pl.BlockSpec((SUBLANES, nb), lambda n, t: (0, n))
    out_spec = pl.BlockSpec((tc, 2 * nb), lambda n, t: (tmap(t), n))
    out_shape = SDS((T, 2 * S5_LANES), F32)
    scratch = [pltpu.VMEM((SUBLANES, nb), F32), pltpu.VMEM((SUBLANES, nb), F32),
               pltpu.VMEM((len(step_rows), SUBLANES, nb), F32), pltpu.VMEM((len(step_rows), SUBLANES, nb), F32),
               pltpu.VMEM((tc, nb), F32), pltpu.VMEM((tc, nb), F32)]
    src_spec = pl.BlockSpec((tc, S5_CB), lambda n, t: (tmap(t), n))
    if reverse:
        wr_spec = pl.BlockSpec((nb, S5_CB), lambda n, t: (2 * n, n))
        wi_spec = pl.BlockSpec((nb, S5_CB), lambda n, t: (2 * n + 1, n))
    else:
        wr_spec = pl.BlockSpec((S5_CB, nb), lambda n, t: (n, 2 * n))
        wi_spec = pl.BlockSpec((S5_CB, nb), lambda n, t: (n, 2 * n + 1))
    drive_specs = [src_spec, wr_spec, wi_spec, tab_spec, tab_spec]
    drive = [src, mat, mat, tab_re, tab_im]
    if not with_da:
        return pl.pallas_call(body, name=name, grid=(nl, nt), in_specs=drive_specs,
                              out_specs=out_spec, out_shape=out_shape, scratch_shapes=scratch,
                              compiler_params=_cp("parallel", "arbitrary"))(*drive)
    re_halo = pl.BlockSpec((SUBLANES, nb), lambda n, t: (jnp.maximum(tmap(t) * hb - 1, 0), 2 * n))
    im_halo = pl.BlockSpec((SUBLANES, nb), lambda n, t: (jnp.maximum(tmap(t) * hb - 1, 0), 2 * n + 1))
    acc = pl.BlockSpec((1, nb), lambda n, t: (0, n))
    row = SDS((1, S5_LANES), F32)
    return call_hosting(
        body, hosted, name=name, grid=(nl, nt),
        in_specs=drive_specs + [re_spec, im_spec, re_halo, im_halo],
        out_specs=[out_spec, acc, acc], out_shape=[out_shape, row, row],
        inputs=drive + [states, states, states, states], aliases={},
        scratch=scratch + [pltpu.VMEM((SUBLANES, nb), F32), pltpu.VMEM((SUBLANES, nb), F32)])


def s5_out_fwd(yc, u, d, *, name):
    T, C = yc.shape
    tr = _pick(T, ROW_TILE)

    def body(yc_ref, u_ref, d_ref, yl_ref, yg_ref):
        yl = yc_ref[...] + d_ref[...] * u_ref[...]
        yl_ref[...] = yl
        yg_ref[...] = jax.nn.gelu(yl)

    return pl.pallas_call(body, name=name, grid=(T // tr,), in_specs=[_row(tr, C), _row(tr, C), _full((1, C))],
                          out_specs=[_row(tr, C)] * 2, out_shape=[SDS((T, C), F32)] * 2,
                          compiler_params=_cp("parallel"))(yc, u, d)


def glu_fwd(yg, gl, *, out_cols, name):
    T, C = yg.shape
    tr = _pick(T, ROW_TILE)

    def body(yg_ref, gl_ref, o_ref):
        o_ref[...] = yg_ref[...] * jax.nn.sigmoid(gl_ref[...])

    return pl.pallas_call(body, name=name, grid=(T // tr,), in_specs=[_row(tr, C)] * 2, out_specs=_row(tr, C),
                          out_shape=SDS((T, out_cols), F32), compiler_params=_cp("parallel"))(yg, gl)


def glu_bwd(yg, gl, dy, *, name):
    T, C = yg.shape
    tr = _pick(T, ROW_TILE)

    def body(yg_ref, gl_ref, dy_ref, dyg_ref, dgl_ref):
        s = jax.nn.sigmoid(gl_ref[...])
        dyv = dy_ref[...]
        dyg_ref[...] = dyv * s
        dgl_ref[...] = (dyv * yg_ref[...] * s * (1.0 - s)).astype(dgl_ref.dtype)

    return pl.pallas_call(body, name=name, grid=(T // tr,), in_specs=[_row(tr, C)] * 3, out_specs=[_row(tr, C)] * 2,
                          out_shape=[SDS((T, C), F32), SDS((T, C), MXU_DTYPE)],
                          compiler_params=_cp("parallel"))(yg, gl, dy)


def s5_out_bwd(yl, u, d, dyg_a, dyg_b, *, name):
    T, C = yl.shape
    tr = _pick(T, ROW_TILE)

    def body(yl_ref, u_ref, d_ref, da_ref, db_ref, dyl_ref, du_ref, dd_ref):
        dyl = (da_ref[...] + db_ref[...]) * _gelu_grad(yl_ref[...])
        dyl_ref[...] = dyl.astype(dyl_ref.dtype)
        du_ref[...] = dyl * d_ref[...]

        @pl.when(pl.program_id(0) == 0)
        def _():
            dd_ref[...] = jnp.zeros_like(dd_ref)

        dd_ref[...] += jnp.sum(dyl * u_ref[...], axis=0, keepdims=True)

    return pl.pallas_call(body, name=name, grid=(T // tr,),
                          in_specs=[_row(tr, C), _row(tr, C), _full((1, C)), _row(tr, C), _row(tr, C)],
                          out_specs=[_row(tr, C), _row(tr, C), _full((1, C))],
                          out_shape=[SDS((T, C), MXU_DTYPE), SDS((T, C), F32), SDS((1, C), F32)],
                          compiler_params=_cp("arbitrary"))(yl, u, d, dyg_a, dyg_b)


def add2(a, b, *, name):
    T, C = a.shape
    tr = _pick(T, ROW_TILE)

    def body(a_ref, b_ref, o_ref):
        o_ref[...] = a_ref[...] + b_ref[...]

    return pl.pallas_call(body, name=name, grid=(T // tr,), in_specs=[_row(tr, C)] * 2, out_specs=_row(tr, C),
                          out_shape=SDS((T, C), F32), compiler_params=_cp("parallel"))(a, b)


def _tri(n, upper):
    r = lax.broadcasted_iota(jnp.int32, (n, n), 0)
    c = lax.broadcasted_iota(jnp.int32, (n, n), 1)
    return ((c >= r) if upper else (c <= r)).astype(F32)


def fox_gate_fwd(fl, bf, *, fl_col, name):
    T = fl.shape[0]
    tb = _pick(T, 256)

    def body(fl_ref, bf_ref, f_ref, c_ref):
        @pl.when(pl.program_id(0) == 0)
        def _():
            c_ref[...] = jnp.zeros_like(c_ref)

        lf = jax.nn.log_sigmoid(fl_ref[...] + bf_ref[...])
        f = jnp.dot(_tri(tb, False), lf, precision=lax.Precision.HIGHEST, preferred_element_type=F32) + c_ref[0:1, :]
        f_ref[...] = f * LOG2E
        c_ref[0:1, :] = f[tb - 1:tb, :]

    fl_spec = pl.BlockSpec((tb, LANES), lambda i: (i, fl_col))
    return pl.pallas_call(body, name=name, grid=(T // tb,), in_specs=[fl_spec, _full((1, LANES))],
                          out_specs=_row(tb, LANES), out_shape=SDS((T, LANES), F32),
                          scratch_shapes=[pltpu.VMEM((SUBLANES, LANES), F32)], compiler_params=_cp("arbitrary"))(fl, bf)


def fox_gate_bwd(fl, bf, df_keys, df_queries, *, fl_col, name):
    T = fl.shape[0]
    tb = _pick(T, 256)
    nt = T // tb

    def body(fl_ref, bf_ref, dfk_ref, dfq_ref, dfl_ref, dbf_ref, c_ref):
        @pl.when(pl.program_id(0) == 0)
        def _():
            c_ref[...] = jnp.zeros_like(c_ref)
            dbf_ref[...] = jnp.zeros_like(dbf_ref)

        dlf = jnp.dot(_tri(tb, True), dfk_ref[...] + dfq_ref[...], precision=lax.Precision.HIGHEST,
                      preferred_element_type=F32) + c_ref[0:1, :]
        c_ref[0:1, :] = dlf[0:1, :]
        dfl = dlf * jax.nn.sigmoid(-(fl_ref[...] + bf_ref[...]))
        dfl_ref[...] = dfl
        dbf_ref[...] += jnp.sum(dfl, axis=0, keepdims=True)

    rev = pl.BlockSpec((tb, LANES), lambda i: (nt - 1 - i, 0))
    fl_rev = pl.BlockSpec((tb, LANES), lambda i: (nt - 1 - i, fl_col))
    return pl.pallas_call(body, name=name, grid=(nt,), in_specs=[fl_rev, _full((1, LANES)), rev, rev],
                          out_specs=[rev, _full((1, LANES))], out_shape=[SDS((T, LANES), F32), SDS((1, LANES), F32)],
                          scratch_shapes=[pltpu.VMEM((SUBLANES, LANES), F32)],
                          compiler_params=_cp("arbitrary"))(fl, bf, df_keys, df_queries)


FOX_BLOCK = 512
FOX_PAIRS = FOX_HEADS // 2
_NT = (((1,), (1,)), ((), ()))


LOG2E = 1.4426950408889634
FOX_FWD_UNROLL = 4
FOX_BWD_UNROLL = 3


def _fox_block(T):
    return _pick(T, FOX_BLOCK)


def _own_lanes(lane, hh):
    return (lane < FOX_HEAD_DIM) if hh == 0 else (lane >= FOX_HEAD_DIM)


def _grouped_steps(step, lo, n, unroll, init):
    def trip(t, c):
        for u in range(unroll):
            c = step(lo + t * unroll + u, c)
        return c

    carry = lax.fori_loop(0, n // unroll, trip, init)
    for u in range(unroll - 1):
        carry = lax.cond(n % unroll > u, lambda c: step(lo + (n // unroll) * unroll + u, c), lambda c: c, carry)
    return carry


Q_TILE0, K_TILE0, V_TILE0, O_TILE0 = 4, 8, 12, 4
FL_TILE = 16
POOL_COL = 2


def fox_fwd(z, f_col, f_row, ycat, hosted, *, name):
    T = z.shape[0]
    blk = _fox_block(T)
    nb = T // blk
    scale = FOX_HEAD_DIM ** -0.5

    def body(q_ref, k_ref, v_ref, fc_ref, fr_ref, prev_ref, o_ref, l_ref):
        i = pl.program_id(1)
        row = lax.broadcasted_iota(jnp.int32, (blk, blk), 0)
        col = lax.broadcasted_iota(jnp.int32, (blk, blk), 1)
        lane = lax.broadcasted_iota(jnp.int32, (blk, LANES), 1)
        qt = q_ref[...] * (scale * LOG2E)
        outs = []
        for hh in range(2):
            qh = jnp.where(_own_lanes(lane, hh), qt, 0.0).astype(MXU_DTYPE)
            fi = fc_ref[0, :, hh:hh + 1]

            def step(j, carry, masked=False):
                m, l, acc = carry
                r0 = pl.multiple_of(j * blk, blk)
                kj = k_ref[pl.ds(r0, blk), :].astype(MXU_DTYPE)
                vj = v_ref[pl.ds(r0, blk), :].astype(MXU_DTYPE)
                s = lax.dot_general(qh, kj, _NT, preferred_element_type=F32) + (fi - fr_ref[0, j, hh:hh + 1, :])
                if masked:
                    s = jnp.where(col <= row, s, -jnp.inf)
                m_new = jnp.maximum(m, jnp.max(s, axis=-1, keepdims=True))
                p = jnp.exp2(s - m_new)
                alpha = jnp.exp2(m - m_new)
                l = alpha * l + jnp.sum(p, axis=-1, keepdims=True)
                acc = alpha * acc + jnp.dot(p.astype(MXU_DTYPE), vj, preferred_element_type=F32)
                return m_new, l, acc

            init = (jnp.full((blk, 1), -jnp.inf, F32), jnp.zeros((blk, 1), F32), jnp.zeros((blk, LANES), F32))
            m, l, acc = step(i, _grouped_steps(step, 0, i, FOX_FWD_UNROLL, init), True)
            outs.append(acc / l)
            l_ref[0, :, hh:hh + 1] = m + jnp.log2(l)
        o_ref[...] = jnp.where(_own_lanes(lane, 0), outs[0], outs[1])

    qspec = pl.BlockSpec((blk, LANES), lambda h, i: (i, Q_TILE0 + h))
    kspec = pl.BlockSpec((T, LANES), lambda h, i: (0, K_TILE0 + h))
    vspec = pl.BlockSpec((T, LANES), lambda h, i: (0, V_TILE0 + h))
    ospec = pl.BlockSpec((blk, LANES), lambda h, i: (i, O_TILE0 + h))
    cspec = pl.BlockSpec((1, blk, 2), lambda h, i: (h, i, 0))
    rspec = pl.BlockSpec((1, nb, 2, blk), lambda h, i: (h, 0, 0, 0))
    return call_hosting(body, hosted, name=name, grid=(FOX_PAIRS, nb),
                        in_specs=[qspec, kspec, vspec, cspec, rspec, ANY], out_specs=[ospec, cspec],
                        out_shape=[SDS(ycat.shape, F32), SDS((FOX_PAIRS, T, 2), F32)],
                        inputs=[z, z, z, f_col, f_row, ycat], aliases={5: 0})


def fox_dd(ycat, dycat, *, name):
    T = ycat.shape[0]
    blk = _fox_block(T)

    def body(o_ref, do_ref, dd_ref):
        lane = lax.broadcasted_iota(jnp.int32, (blk, LANES), 1)
        prod = do_ref[...] * o_ref[...]
        for hh in range(2):
            dd_ref[0, :, hh:hh + 1] = jnp.sum(jnp.where(_own_lanes(lane, hh), prod, 0.0), axis=-1, keepdims=True)

    ospec = pl.BlockSpec((blk, LANES), lambda h, i: (i, O_TILE0 + h))
    return pl.pallas_call(body, name=name, grid=(FOX_PAIRS, T // blk), in_specs=[ospec, ospec],
                          out_specs=pl.BlockSpec((1, blk, 2), lambda h, i: (h, i, 0)),
                          out_shape=SDS((FOX_PAIRS, T, 2), F32), compiler_params=_cp("parallel", "parallel"))(ycat, dycat)


def fox_bwd(z, dycat, f_col, f_row, lse_row, dd_row, hosted, *, name):
    T = z.shape[0]
    blk = _fox_block(T)
    nb = T // blk
    scale = FOX_HEAD_DIM ** -0.5

    def body(q_ref, k_ref, v_ref, do_ref, fc_ref, fr_ref, lr_ref, dr_ref, dk_ref, dv_ref, df_ref, dqt_ref, dfq_ref):
        j = pl.program_id(1)

        @pl.when(j == 0)
        def _():
            dqt_ref[...] = jnp.zeros_like(dqt_ref)
            dfq_ref[...] = jnp.zeros_like(dfq_ref)

        row = lax.broadcasted_iota(jnp.int32, (blk, blk), 0)
        col = lax.broadcasted_iota(jnp.int32, (blk, blk), 1)
        lane = lax.broadcasted_iota(jnp.int32, (blk, LANES), 1)
        kt = k_ref[...]
        vt = v_ref[...]
        dks, dvs = [], []
        for hh in range(2):
            own = _own_lanes(lane, hh)
            kh = jnp.where(own, kt, 0.0).astype(MXU_DTYPE)
            vh = jnp.where(own, vt, 0.0).astype(MXU_DTYPE)
            kht = kh.T
            fj = fc_ref[0, :, hh:hh + 1]

            def step(i, carry, masked=False):
                dk, dv, df = carry
                r0 = pl.multiple_of(i * blk, blk)
                qi = (q_ref[pl.ds(r0, blk), :] * (scale * LOG2E)).astype(MXU_DTYPE)
                doi = do_ref[pl.ds(r0, blk), :].astype(MXU_DTYPE)
                st = lax.dot_general(kh, qi, _NT, preferred_element_type=F32) + (fr_ref[0, i, hh:hh + 1, :] - fj)
                pt = jnp.exp2(st - lr_ref[0, i, hh:hh + 1, :])
                if masked:
                    pt = jnp.where(col >= row, pt, 0.0)
                dv = dv + jnp.dot(pt.astype(MXU_DTYPE), doi, preferred_element_type=F32)
                dpt = lax.dot_general(vh, doi, _NT, preferred_element_type=F32)
                dst = pt * (dpt - dr_ref[0, i, hh:hh + 1, :])
                dsb = dst.astype(MXU_DTYPE)
                dk = dk + jnp.dot(dsb, qi, preferred_element_type=F32)
                df = df - jnp.sum(dst, axis=-1, keepdims=True)
                dqt_ref[0, i] += jnp.dot(kht, dsb, preferred_element_type=F32)
                dfq_ref[0, i, hh:hh + 1, :] += jnp.sum(dst, axis=0, keepdims=True)
                return dk, dv, df

            init = (jnp.zeros((blk, LANES), F32), jnp.zeros((blk, LANES), F32), jnp.zeros((blk, 1), F32))
            dk, dv, df = _grouped_steps(step, j + 1, nb - 1 - j, FOX_BWD_UNROLL, step(j, init, True))
            dks.append(dk * (1.0 / LOG2E))
            dvs.append(dv)
            df_ref[0, :, hh:hh + 1] = df
        dk_ref[...] = jnp.where(_own_lanes(lane, 0), dks[0], dks[1])
        dv_ref[...] = jnp.where(_own_lanes(lane, 0), dvs[0], dvs[1])

    bspec = pl.BlockSpec((blk, LANES), lambda h, j: (j, h))
    qspec = pl.BlockSpec((T, LANES), lambda h, j: (0, Q_TILE0 + h))
    kspec = pl.BlockSpec((blk, LANES), lambda h, j: (j, K_TILE0 + h))
    vspec = pl.BlockSpec((blk, LANES), lambda h, j: (j, V_TILE0 + h))
    dospec = pl.BlockSpec((T, LANES), lambda h, j: (0, O_TILE0 + h))
    cspec = pl.BlockSpec((1, blk, 2), lambda h, j: (h, j, 0))
    rspec = pl.BlockSpec((1, nb, 2, blk), lambda h, j: (h, 0, 0, 0))
    dqspec = pl.BlockSpec((1, nb, LANES, blk), lambda h, j: (h, 0, 0, 0))
    return call_hosting(body, hosted, name=name, grid=(FOX_PAIRS, nb),
                        in_specs=[qspec, kspec, vspec, dospec, cspec, rspec, rspec, rspec],
                        out_specs=[bspec, bspec, cspec, dqspec, rspec],
                        out_shape=[SDS((T, FOX_WIDTH), F32), SDS((T, FOX_WIDTH), F32), SDS((FOX_PAIRS, T, 2), F32),
                                   SDS((FOX_PAIRS, nb, LANES, blk), F32), SDS((FOX_PAIRS, nb, 2, blk), F32)],
                        inputs=[z, z, z, dycat, f_col, f_row, lse_row, dd_row], aliases={})


def _pairs_col(a, T):
    return jnp.transpose(a[:, :FOX_HEADS].reshape(T, FOX_PAIRS, 2), (1, 0, 2))


def _col_to_row(a, T):
    blk = _fox_block(T)
    return jnp.transpose(a.reshape(FOX_PAIRS, T // blk, blk, 2), (0, 1, 3, 2))


def _row_to_col(a, T):
    return jnp.transpose(a, (0, 1, 3, 2)).reshape(FOX_PAIRS, T, 2)


def _pairs_to_lanes(a, T):
    flat = jnp.transpose(a, (1, 0, 2)).reshape(T, FOX_HEADS)
    return jnp.pad(flat, ((0, 0), (0, LANES - FOX_HEADS)))


def _pool_counts(t0, n, w):
    t = (t0 + lax.broadcasted_iota(jnp.int32, (n, 1), 0)).astype(F32)
    return jnp.minimum(t + 1.0, float(w))


def pool_window(x, *, adjoint, name, in_col=0, into=None, out_col=0, out_dtype=F32):
    T, C = x.shape[0], len(POOL_WINDOWS) * POOL_GROUP_DIM
    tr = _pick(T, ROW_TILE)
    nt = T // tr
    hb = tr // POOL_HALO
    n = tr + POOL_HALO

    def body(x_ref, h_ref, *rest):
        o_ref = rest[-1]
        i = pl.program_id(0)
        cur = x_ref[...]
        if adjoint:
            halo = h_ref[...] * jnp.where(i < nt - 1, 1.0, 0.0)
            ext = jnp.concatenate([cur, halo], axis=0)
            t0 = i * tr
        else:
            halo = h_ref[...] * jnp.where(i > 0, 1.0, 0.0)
            ext = jnp.concatenate([halo, cur], axis=0)
            t0 = i * tr - POOL_HALO
        sums = {}
        for g, w in enumerate(POOL_WINDOWS):
            ls = slice(g * POOL_GROUP_DIM, (g + 1) * POOL_GROUP_DIM)
            s = ext[:, ls]
            if adjoint:
                s = s / _pool_counts(t0, n, w)
            d = 1
            while d < w:
                s = s + pltpu.roll(s, (n - d) if adjoint else d, 0)
                d *= 2
            if adjoint:
                o_ref[:, ls] = (s[0:tr, :] - cur[:, ls]).astype(o_ref.dtype)
            else:
                o_ref[:, ls] = (s[POOL_HALO:n, :] / _pool_counts(i * tr, tr, w) - cur[:, ls]).astype(o_ref.dtype)

    if adjoint:
        halo_spec = pl.BlockSpec((POOL_HALO, C), lambda i: (jnp.minimum((i + 1) * hb, T // POOL_HALO - 1), in_col))
    else:
        halo_spec = pl.BlockSpec((POOL_HALO, C), lambda i: (jnp.maximum(i * hb - 1, 0), in_col))
    x_spec = pl.BlockSpec((tr, C), lambda i: (i, in_col))
    if into is None:
        return pl.pallas_call(body, name=name, grid=(nt,), in_specs=[x_spec, halo_spec], out_specs=_row(tr, C),
                              out_shape=SDS((T, C), out_dtype), compiler_params=_cp("parallel"))(x, x)
    return pl.pallas_call(body, name=name, grid=(nt,), in_specs=[x_spec, halo_spec, ANY],
                          out_specs=pl.BlockSpec((tr, C), lambda i: (i, out_col)), out_shape=SDS(into.shape, into.dtype),
                          input_output_aliases={2: 0}, compiler_params=_cp("parallel"))(x, x, into)


def colscale_fwd(a, s, *, out_cols, name):
    T, C = a.shape
    tr = _pick(T, ROW_TILE)

    def body(a_ref, s_ref, o_ref):
        o_ref[...] = (a_ref[...] * s_ref[...]).astype(o_ref.dtype)

    return pl.pallas_call(body, name=name, grid=(T // tr,), in_specs=[_row(tr, C), _full((1, C))], out_specs=_row(tr, C),
                          out_shape=SDS((T, out_cols), MXU_DTYPE), compiler_params=_cp("parallel"))(a, s)


def colscale_bwd(a, s, dy, *, name):
    T, C = a.shape
    tr = _pick(T, ROW_TILE)

    def body(a_ref, s_ref, dy_ref, da_ref, ds_ref):
        dyv = dy_ref[...]
        da_ref[...] = (dyv * s_ref[...]).astype(da_ref.dtype)

        @pl.when(pl.program_id(0) == 0)
        def _():
            ds_ref[...] = jnp.zeros_like(ds_ref)

        ds_ref[...] += jnp.sum(dyv * a_ref[...], axis=0, keepdims=True)

    return pl.pallas_call(body, name=name, grid=(T // tr,), in_specs=[_row(tr, C), _full((1, C)), _row(tr, C)],
                          out_specs=[_row(tr, C), _full((1, C))], out_shape=[SDS((T, C), MXU_DTYPE), SDS((1, C), F32)],
                          compiler_params=_cp("arbitrary"))(a, s, dy)


SGU_ROWS = 512


def _sgu_norm(v, ln_g, ln_b):
    vg = jax.nn.gelu(v)
    xc = vg - jnp.mean(vg, axis=-1, keepdims=True)
    r = lax.rsqrt(jnp.mean(xc * xc, axis=-1, keepdims=True) + EPS)
    xh = xc * r
    return xh * ln_g + ln_b, xh, r


def _rowc(tr, c, cb):
    return pl.BlockSpec((tr, c), lambda i: (i, cb))


def sgu_fwd(z, ln_g, ln_b, ws, bst, ycat, *, name):
    T, C = z.shape[0], SGU_GROUPS * SGU_GROUP_DIM
    tr = _pick(T, SGU_ROWS)

    def body(u_ref, v_ref, g_ref, b_ref, ws_ref, bst_ref, prev_ref, o_ref):
        vn, _, _ = _sgu_norm(v_ref[...], g_ref[...], b_ref[...])
        vn = vn.astype(MXU_DTYPE)
        ug = jax.nn.gelu(u_ref[...])
        for g in range(SGU_GROUPS):
            w = ws_ref[g].astype(MXU_DTYPE)
            bias = bst_ref[:, g:g + 1]
            for c in range(tr // CHUNK):
                rs = slice(c * CHUNK, (c + 1) * CHUNK)
                ls = slice(g * SGU_GROUP_DIM, (g + 1) * SGU_GROUP_DIM)
                mixed = jnp.dot(w, vn[rs, ls], preferred_element_type=F32) + bias
                o_ref[rs, ls] = (ug[rs, ls] * mixed).astype(o_ref.dtype)

    return pl.pallas_call(body, name=name, grid=(T // tr,),
                          in_specs=[_rowc(tr, C, 0), _rowc(tr, C, 1), _full((1, C)), _full((1, C)),
                                    _full((SGU_GROUPS, CHUNK, CHUNK)), _full((CHUNK, SGU_GROUPS)), ANY],
                          out_specs=_rowc(tr, C, 1), out_shape=SDS(ycat.shape, ycat.dtype), input_output_aliases={6: 0},
                          compiler_params=_cp("parallel"))(z, z, ln_g, ln_b, ws, bst, ycat)


def sgu_bwd(z, ln_g, ln_b, ws, wst, bst, dycat, *, out_cols, name):
    T, C = z.shape[0], SGU_GROUPS * SGU_GROUP_DIM
    tr = _pick(T, SGU_ROWS)

    def body(u_ref, v_ref, g_ref, b_ref, ws_ref, wst_ref, bst_ref, dy_ref,
             duv_ref, dg_ref, db_ref, dws_ref, dbst_ref, dvn_ref):
        du_ref = duv_ref.at[:, 0:C]
        dv_ref = duv_ref.at[:, C:2 * C]
        @pl.when(pl.program_id(0) == 0)
        def _():
            dg_ref[...] = jnp.zeros_like(dg_ref)
            db_ref[...] = jnp.zeros_like(db_ref)
            dws_ref[...] = jnp.zeros_like(dws_ref)
            dbst_ref[...] = jnp.zeros_like(dbst_ref)

        uv = u_ref[...]
        vv = v_ref[...]
        vn, xh, r = _sgu_norm(vv, g_ref[...], b_ref[...])
        vn = vn.astype(MXU_DTYPE)
        ug = jax.nn.gelu(uv)
        dyv = dy_ref[...]
        for g in range(SGU_GROUPS):
            w = ws_ref[g].astype(MXU_DTYPE)
            wt = wst_ref[g].astype(MXU_DTYPE)
            bias = bst_ref[:, g:g + 1]
            dw = jnp.zeros((CHUNK, CHUNK), F32)
            dbias = jnp.zeros((CHUNK, 1), F32)
            for c in range(tr // CHUNK):
                rs = slice(c * CHUNK, (c + 1) * CHUNK)
                ls = slice(g * SGU_GROUP_DIM, (g + 1) * SGU_GROUP_DIM)
                vblk = vn[rs, ls]
                mixed = jnp.dot(w, vblk, preferred_element_type=F32) + bias
                dyb = dyv[rs, ls]
                du_ref[rs, ls] = (dyb * mixed * _gelu_grad(uv[rs, ls])).astype(du_ref.dtype)
                dmixed = dyb * ug[rs, ls]
                dbias = dbias + jnp.sum(dmixed, axis=-1, keepdims=True)
                dmb = dmixed.astype(MXU_DTYPE)
                dw = dw + lax.dot_general(dmb, vblk, _NT, preferred_element_type=F32)
                dvn_ref[rs, ls] = jnp.dot(wt, dmb, preferred_element_type=F32)
            dws_ref[g] += dw
            dbst_ref[:, g:g + 1] += dbias
        dvn = dvn_ref[...]
        dg_ref[...] += jnp.sum(dvn * xh, axis=0, keepdims=True)
        db_ref[...] += jnp.sum(dvn, axis=0, keepdims=True)
        dxh = dvn * g_ref[...]
        dvg = r * (dxh - jnp.mean(dxh, axis=-1, keepdims=True) - xh * jnp.mean(dxh * xh, axis=-1, keepdims=True))
        dv_ref[...] = (dvg * _gelu_grad(vv)).astype(dv_ref.dtype)

    wspec = _full((SGU_GROUPS, CHUNK, CHUNK))
    return pl.pallas_call(body, name=name, grid=(T // tr,),
                          in_specs=[_rowc(tr, C, 0), _rowc(tr, C, 1), _full((1, C)), _full((1, C)), wspec, wspec,
                                    _full((CHUNK, SGU_GROUPS)), _rowc(tr, C, 1)],
                          out_specs=[_rowc(tr, 2 * C, 0), _full((1, C)), _full((1, C)), wspec,
                                     _full((CHUNK, SGU_GROUPS))],
                          out_shape=[SDS((T, out_cols), MXU_DTYPE), SDS((1, C), F32), SDS((1, C), F32),
                                     SDS((SGU_GROUPS, CHUNK, CHUNK), F32), SDS((CHUNK, SGU_GROUPS), F32)],
                          scratch_shapes=[pltpu.VMEM((tr, C), F32)],
                          compiler_params=_cp("arbitrary"))(z, z, ln_g, ln_b, ws, wst, bst, dycat)


def adamw(w, g, m, v, *, name):
    R, C = w.shape
    tr = _pick(R, 512)
    c1 = 1.0 - ADAM_B1 ** ADAM_STEP
    c2 = 1.0 - ADAM_B2 ** ADAM_STEP

    def body(w_ref, g_ref, m_ref, v_ref, d_ref, nm_ref, nv_ref):
        gv = g_ref[...]
        nm = ADAM_B1 * m_ref[...] + (1.0 - ADAM_B1) * gv
        nv = ADAM_B2 * v_ref[...] + (1.0 - ADAM_B2) * (gv * gv)
        nm_ref[...] = nm
        nv_ref[...] = nv
        d_ref[...] = -ADAM_LR * ((nm / c1) / (jnp.sqrt(nv / c2) + ADAM_EPS) + ADAM_WD * w_ref[...])

    spec = _row(tr, C)
    return pl.pallas_call(body, name=name, grid=(R // tr,), in_specs=[spec] * 4, out_specs=[spec] * 3,
                          out_shape=[SDS((R, C), F32)] * 3, compiler_params=_cp("parallel"))(w, g, m, v)


ANY = pl.BlockSpec(memory_space=pl.ANY)


def _coords():
    return lax.axis_index("x"), lax.axis_index("y"), lax.axis_index("c")


def _other_chips(x, y):
    return [(1 - x, y), (x, 1 - y), (1 - x, 1 - y)]


def _remote(src, dst, send_sems, recv_sems, k, dev):
    return pltpu.make_async_remote_copy(src_ref=src, dst_ref=dst, send_sem=send_sems.at[k], recv_sem=recv_sems.at[k],
                                        device_id=dev, device_id_type=MESH)


LOCAL_CHUNKS = 8


class Exchange:
    def __init__(self, ins, out_shapes, scratch, start, wait):
        self.ins, self.out_shapes, self.scratch, self.start, self.wait = list(ins), list(out_shapes), list(scratch), start, wait


def run_exchange(ex, *, name):
    ni, no = len(ex.ins), len(ex.out_shapes)

    def body(*refs):
        parts = refs[:ni], refs[ni:ni + no], refs[ni + no:]
        ex.start(*parts)
        ex.wait(*parts)

    return pl.pallas_call(body, name=name, in_specs=[ANY] * ni, out_specs=[ANY] * no, out_shape=ex.out_shapes,
                          scratch_shapes=ex.scratch)(*ex.ins)


def call_hosting(body, ex, *, name, grid, in_specs, out_specs, out_shape, inputs, aliases, scratch=()):
    n_in, n_out, ni, no, ns = len(inputs), len(out_shape), len(ex.ins), len(ex.out_shapes), len(scratch)
    outs_at = n_in + ni
    scr_at = outs_at + n_out + no

    def wrapped(*refs):
        own = refs[:n_in] + refs[outs_at:outs_at + n_out] + refs[scr_at:scr_at + ns]
        parts = refs[n_in:outs_at], refs[outs_at + n_out:scr_at], refs[scr_at + ns:]
        ids = [pl.program_id(d) for d in range(len(grid))]
        first = functools.reduce(jnp.logical_and, [i == 0 for i in ids])
        last = functools.reduce(jnp.logical_and, [i == g - 1 for i, g in zip(ids, grid)])

        @pl.when(first)
        def _():
            ex.start(*parts)

        body(*own)

        @pl.when(last)
        def _():
            ex.wait(*parts)

    outs = pl.pallas_call(
        wrapped, name=name, grid=grid, in_specs=list(in_specs) + [ANY] * ni, out_specs=list(out_specs) + [ANY] * no,
        out_shape=list(out_shape) + ex.out_shapes, input_output_aliases=aliases,
        scratch_shapes=list(scratch) + ex.scratch,
        compiler_params=_cp(*["arbitrary"] * len(grid)))(*inputs, *ex.ins)
    return outs[:n_out], outs[n_out:]


def allgather_ici_exchange(shards, whole=()):
    na, n_all = len(shards), len(shards) + len(whole)
    arrays = list(shards) + list(whole)

    def copies(s_refs, o_refs, sems):
        send_sems, recv_sems, _ = sems
        x, y, c = _coords()
        j = 2 * x + y
        out = []
        for a in range(n_all):
            if a < na:
                half = shards[a].shape[0] // 2
                part = (pl.ds(c * half, half),)
            else:
                part = ()
            for k, (px, py) in enumerate(_other_chips(x, y)):
                send = _remote(s_refs[a].at[part] if part else s_refs[a], o_refs[a].at[(j,) + part], send_sems, recv_sems,
                               3 * a + k, (px, py, c))
                rows = o_refs[a].at[(2 * px + py,) + part]
                out.append((send, _remote(rows, rows, send_sems, recv_sems, 3 * a + k, (px, py, c))))
        return out

    def start(s_refs, o_refs, sems):
        x, y, c = _coords()
        j = 2 * x + y
        for a in range(n_all):
            chunks = LOCAL_CHUNKS if a < na else 1
            chunk = arrays[a].shape[0] // chunks
            for q in range(chunks):
                rows = pl.ds(q * chunk, chunk)
                pltpu.make_async_copy(s_refs[a].at[rows], o_refs[a].at[j, rows], sems[2].at[a]).start()
        for send, _ in copies(s_refs, o_refs, sems):
            send.start()

    def wait(s_refs, o_refs, sems):
        x, y, c = _coords()
        j = 2 * x + y
        for send, arrival in copies(s_refs, o_refs, sems):
            arrival.wait_recv()
            send.wait_send()
        for a in range(n_all):
            pltpu.make_async_copy(s_refs[a], o_refs[a].at[j], sems[2].at[a]).wait()

    return Exchange(arrays, [SDS((N_CHIPS,) + s.shape, s.dtype) for s in arrays],
                    [pltpu.SemaphoreType.DMA((3 * n_all,)), pltpu.SemaphoreType.DMA((3 * n_all,)),
                     pltpu.SemaphoreType.DMA((n_all,))], start, wait)


def allgather_forward(gathered, *, name):
    na = len(gathered)

    def body(*refs):
        o_refs = refs[na:2 * na]
        send_sems, recv_sems = refs[2 * na:]
        x, y, c = _coords()
        sibling = (x, y, 1 - c)
        cps = []
        for a in range(na):
            half = gathered[a].shape[1] // 2
            for k, (px, py) in enumerate(_other_chips(x, y)):
                mine = o_refs[a].at[2 * px + py, pl.ds(c * half, half)]
                theirs = o_refs[a].at[2 * px + py, pl.ds((1 - c) * half, half)]
                cps.append((_remote(mine, mine, send_sems, recv_sems, 3 * a + k, sibling),
                            _remote(theirs, theirs, send_sems, recv_sems, 3 * a + k, sibling)))
        for send, _ in cps:
            send.start()
        for send, arrival in cps:
            send.wait_send()
            arrival.wait_recv()

    return pl.pallas_call(body, name=name, in_specs=[ANY] * na, out_specs=[ANY] * na,
                          out_shape=[SDS(g.shape, g.dtype) for g in gathered],
                          input_output_aliases={a: a for a in range(na)},
                          scratch_shapes=[pltpu.SemaphoreType.DMA((3 * na,)), pltpu.SemaphoreType.DMA((3 * na,))])(*gathered)


def swap_halves_exchange(gs):
    na = len(gs)

    def copies(g_refs, o_refs, sems):
        x, y, c = _coords()
        out = []
        for a in range(na):
            half = gs[a].shape[1] // 2
            out.append(_remote(g_refs[a].at[:, pl.ds((1 - c) * half, half), :], o_refs[a], sems[0], sems[1], a,
                               (x, y, 1 - c)))
        return out

    def start(g_refs, o_refs, sems):
        for cp in copies(g_refs, o_refs, sems):
            cp.start()

    def wait(g_refs, o_refs, sems):
        for cp in copies(g_refs, o_refs, sems):
            cp.wait()

    return Exchange(gs, [SDS((g.shape[0], g.shape[1] // 2, g.shape[2]), g.dtype) for g in gs],
                    [pltpu.SemaphoreType.DMA((na,)), pltpu.SemaphoreType.DMA((na,))], start, wait)


def chip_partials_exchange(pbs):
    na = len(pbs)

    def copies(p_refs, o_refs, sems):
        x, y, c = _coords()
        out = []
        for a in range(na):
            for k, (px, py) in enumerate(_other_chips(x, y)):
                out.append(_remote(p_refs[a].at[2 * px + py], o_refs[a].at[k], sems[0], sems[1], 3 * a + k, (px, py, c)))
        return out

    def start(p_refs, o_refs, sems):
        for cp in copies(p_refs, o_refs, sems):
            cp.start()

    def wait(p_refs, o_refs, sems):
        for cp in copies(p_refs, o_refs, sems):
            cp.wait()

    return Exchange(pbs, [SDS((3,) + p.shape[1:], p.dtype) for p in pbs],
                    [pltpu.SemaphoreType.DMA((3 * na,)), pltpu.SemaphoreType.DMA((3 * na,))], start, wait)


def add_sibling_half(g, land, c_idx, *, name):
    n, R, C = g.shape
    half = R // 2
    tr = _pick(half, 256)
    nt = half // tr

    def body(c_ref, g_ref, l_ref, of_ref, ob_ref):
        s = g_ref[...] + l_ref[...].astype(F32)
        of_ref[...] = s
        ob_ref[...] = s.astype(ob_ref.dtype)

    blk = pl.BlockSpec((1, tr, C), lambda s, i, c_ref: (s, i, 0))
    gblk = pl.BlockSpec((1, tr, C), lambda s, i, c_ref: (s, c_ref[0] * nt + i, 0))
    return pl.pallas_call(
        body, name=name,
        grid_spec=pltpu.PrefetchScalarGridSpec(num_scalar_prefetch=1, grid=(n, nt), in_specs=[gblk, blk],
                                               out_specs=[blk, blk]),
        out_shape=[SDS((n, half, C), F32), SDS((n, half, C), WIRE_DTYPE)],
        compiler_params=_cp("parallel", "parallel"))(c_idx, g, land)


def add_chip_partials(pf, rb, jc_idx, *, name):
    n, H, C = pf.shape
    tr = _pick(H, 256)

    def body(jc_ref, p_ref, r_ref, o_ref):
        s = p_ref[0]
        for k in range(3):
            s = s + r_ref[k].astype(F32)
        o_ref[...] = s

    pblk = pl.BlockSpec((1, tr, C), lambda i, jc_ref: (jc_ref[0], i, 0))
    rblk = pl.BlockSpec((3, tr, C), lambda i, jc_ref: (0, i, 0))
    oblk = pl.BlockSpec((None, tr, C), lambda i, jc_ref: (jc_ref[1], i, 0))
    return pl.pallas_call(
        body, name=name,
        grid_spec=pltpu.PrefetchScalarGridSpec(num_scalar_prefetch=1, grid=(H // tr,), in_specs=[pblk, rblk],
                                               out_specs=oblk),
        out_shape=SDS((2, H, C), F32), compiler_params=_cp("parallel"))(jc_idx, pf, rb)


def join_sibling_halves(bufs, *, name):
    na = len(bufs)

    def body(*refs):
        o_refs = refs[na:2 * na]
        send_sems, recv_sems = refs[2 * na:]
        x, y, c = _coords()
        cps = [_remote(o_refs[a].at[c], o_refs[a].at[c], send_sems, recv_sems, a, (x, y, 1 - c)) for a in range(na)]
        for cp in cps:
            cp.start()
        for a in range(na):
            cps[a].wait_send()
            _remote(o_refs[a].at[1 - c], o_refs[a].at[1 - c], send_sems, recv_sems, a, (x, y, 1 - c)).wait_recv()

    return pl.pallas_call(body, name=name, in_specs=[ANY] * na, out_specs=[ANY] * na,
                          out_shape=[SDS(b.shape, b.dtype) for b in bufs],
                          input_output_aliases={a: a for a in range(na)},
                          scratch_shapes=[pltpu.SemaphoreType.DMA((na,)), pltpu.SemaphoreType.DMA((na,))])(*bufs)


def allreduce_pieces(v, *, name):
    n, P, C = v.shape
    assert n == N_DEV

    def body(v_ref, o_ref, land_ref, mine_ref, send1, recv1, send2, recv2):
        x, y, c = _coords()
        me = 4 * x + 2 * y + c
        peers = []
        for m in range(1, N_DEV):
            px = (1 - x) if m & 4 else x
            py = (1 - y) if m & 2 else y
            pc = (1 - c) if m & 1 else c
            peers.append((m - 1, 4 * px + 2 * py + pc, (px, py, pc)))
        scatter = [(_remote(v_ref.at[lin], land_ref.at[me], send1, recv1, k, dev),
                    _remote(land_ref.at[lin], land_ref.at[lin], send1, recv1, k, dev)) for k, lin, dev in peers]
        for send, _ in scatter:
            send.start()
        land_ref[pl.ds(me, 1)] = v_ref[pl.ds(me, 1)]
        for _, arrival in scatter:
            arrival.wait_recv()
        s = land_ref[0]
        for d in range(1, N_DEV):
            s = s + land_ref[d]
        mine_ref[...] = s
        gather = [(_remote(mine_ref, o_ref.at[me], send2, recv2, k, dev),
                   _remote(o_ref.at[lin], o_ref.at[lin], send2, recv2, k, dev)) for k, lin, dev in peers]
        for send, _ in gather:
            send.start()
        o_ref[pl.ds(me, 1)] = s[None]
        for send, arrival in scatter + gather:
            send.wait_send()
        for _, arrival in gather:
            arrival.wait_recv()

    vmem = pl.BlockSpec(memory_space=pltpu.VMEM)
    sems = [pltpu.SemaphoreType.DMA((N_DEV - 1,))] * 4
    return pl.pallas_call(body, name=name, in_specs=[vmem], out_specs=vmem, out_shape=SDS((n, P, C), F32),
                          scratch_shapes=[pltpu.VMEM((n, P, C), F32), pltpu.VMEM((P, C), F32)] + sems)(v)


BIG_SEGS = (
    ("w_in_even", (1024, 514), 1),
    ("s5_w_glu", (128, 512), 0),
    ("w_out_even", (256, 1024), 0),
    ("w_in_odd", (1024, 384), 1),
    ("w_out_odd", (256, 1024), 0),
    ("mlp_w1", (2, 1024, 1024), 2),
    ("mlp_w2", (2, 1024, 1024), 1),
)
BIG_NAMES = tuple(n for n, _, _ in BIG_SEGS)
EARLY_NAMES = ("w_in_even", "s5_w_glu")
LATE_NAMES = ("w_out_even", "w_in_odd", "w_out_odd", "mlp_w1", "mlp_w2")
REDUCED_EARLY = ("s5_w_glu", "w_out_even", "w_in_odd", "w_out_odd", "mlp_w1", "mlp_w2")
SHARDED_SMALL = ("pool_scale", "sgu_ln_g", "sgu_ln_b")
SMALL_SEGS = (
    ("mix_pre_g", (2, 1024)), ("mix_post_g", (2, 1024)), ("mlp_pre_g", (2, 1024)), ("mlp_post_g", (2, 1024)),
    ("s5_lam_re", (1, 32, 64)), ("s5_lam_im", (1, 32, 64)), ("s5_log_dt", (1, 32)),
    ("s5_b_re", (1, 32, 64, 16)), ("s5_b_im", (1, 32, 64, 16)), ("s5_c_re", (1, 32, 16, 64)), ("s5_c_im", (1, 32, 16, 64)),
    ("s5_d", (1, 512)), ("fox_b_f", (1, 8)), ("pool_w", (1, 4, 128, 128)), ("sgu_w_s", (1, 4, 128, 128)),
    ("sgu_b_s", (1, 4, 128)),
)
REDUCED_SEGS = SMALL_SEGS + tuple((n, (1, 512)) for n in SHARDED_SMALL) + (("loss", (1, 1)),)


def _cols_from_chips(g):
    n, R, C = g.shape
    return jnp.transpose(g, (1, 0, 2)).reshape(R, n * C)


def _chips_from_cols(m):
    R, C4 = m.shape
    return jnp.transpose(m.reshape(R, N_CHIPS, C4 // N_CHIPS), (1, 0, 2))


MLP_SHARD = 1024


def _w1_cols(l):
    def spec(tm, tn, tk):
        per = MLP_SHARD // tn
        return pl.BlockSpec((None, tk, tn), lambda i, j, k: (j // per, l * (MLP_SHARD // tk) + k, j % per))
    return spec


def _w1_rows_t(l):
    def spec(tm, tn, tk):
        if tk == N_CHIPS * MLP_SHARD:
            return pl.BlockSpec((N_CHIPS, tn, MLP_SHARD), lambda i, j, k: (0, l * (MLP_SHARD // tn) + j, 0))
        per = MLP_SHARD // tk
        return pl.BlockSpec((None, tn, tk), lambda i, j, k: (k // per, l * (MLP_SHARD // tn) + j, k % per))
    return spec


def _w2_rows(l):
    def spec(tm, tn, tk):
        if tk == N_CHIPS * MLP_SHARD:
            return pl.BlockSpec((N_CHIPS, MLP_SHARD, tn), lambda i, j, k: (0, l, j))
        per = MLP_SHARD // tk
        return pl.BlockSpec((None, tk, tn), lambda i, j, k: (k // per, l * per + k % per, j))
    return spec


def _w2_rows_t(l):
    def spec(tm, tn, tk):
        per = MLP_SHARD // tn
        return pl.BlockSpec((None, tn, tk), lambda i, j, k: (j // per, l * per + j % per, k))
    return spec


def _dw1_out(l):
    def spec(tm, tn, tk):
        per = MLP_SHARD // tn
        return pl.BlockSpec((None, tm, tn), lambda i, j, k: (j // per, l * (MLP_SHARD // tm) + i, j % per))
    return spec


def _dw2_out(l):
    def spec(tm, tn, tk):
        per = MLP_SHARD // tm
        return pl.BlockSpec((None, tm, tn), lambda i, j, k: (i // per, l * per + i % per, j))
    return spec


def _pack_vec(d, segs, rows_multiple):
    flat = jnp.concatenate([d[n].reshape(-1) for n, _ in segs])
    rows = -(-flat.shape[0] // LANES)
    rows = -(-rows // rows_multiple) * rows_multiple
    return jnp.pad(flat, (0, rows * LANES - flat.shape[0])).reshape(rows, LANES)


def _unpack_vec(v, segs):
    flat, out, r = v.reshape(-1), {}, 0
    for n, shape in segs:
        k = math.prod(shape)
        out[n] = flat[r:r + k].reshape(shape)
        r += k
    return out


def _block_diag(blocks):
    G, a, b = blocks.shape
    eye = jnp.eye(G, dtype=blocks.dtype)
    return (eye[:, None, :, None] * blocks[:, :, None, :]).reshape(G * a, G * b)


def _diag_blocks(m, G):
    a, b = m.shape[0] // G, m.shape[1] // G
    return jnp.stack([m[g * a:(g + 1) * a, g * b:(g + 1) * b] for g in range(G)])


def _sqrelu_epi(acc):
    r = jnp.maximum(acc, 0.0)
    return acc, r * r


def _sqrelu_bwd_epi(acc, a):
    return (acc * (2.0 * jnp.maximum(a.astype(F32), 0.0)),)


def _mlp_fwd(h, g1, g2, l, tag):
    T, D = h.shape
    a, s = matmul(h, g1, name=f"{tag}_up", mnk=(T, D_FF, D), b_spec=_w1_cols(l), epi=_sqrelu_epi,
                  out_dtypes=(MXU_DTYPE, MXU_DTYPE))
    m = matmul(s, g2, name=f"{tag}_down", mnk=(T, D, D_FF), b_spec=_w2_rows(l))
    return m, (h, a, s)


def _mlp_bwd(saved, dm, g1, g2, l, dg1, dg2, tag):
    h, a, s = saved
    T, D = h.shape
    gshape = (N_CHIPS, 2 * MLP_SHARD, MLP_SHARD)
    da = matmul(dm, g2, tb=True, name=f"{tag}_down_dx", mnk=(T, D_FF, D), b_spec=_w2_rows_t(l),
                epi=_sqrelu_bwd_epi, epi_in=(a,), out_dtype=MXU_DTYPE)
    dg2 = matmul(s, dm, ta=True, name=f"{tag}_down_dw", tm=MLP_SHARD, o_spec=_dw2_out(l), o_shape=gshape, prev=dg2)
    dh = matmul(da, g1, tb=True, name=f"{tag}_up_dx", mnk=(T, D, D_FF), b_spec=_w1_rows_t(l))
    dg1 = matmul(h, da, ta=True, name=f"{tag}_up_dw", o_spec=_dw1_out(l), o_shape=gshape, prev=dg1)
    return dh, dg1, dg2


def kernel(x, mix_pre_g, mix_post_g, mlp_pre_g, mlp_post_g, w_in_even, s5_lam_re, s5_lam_im, s5_log_dt, s5_b_re, s5_b_im, s5_c_re, s5_c_im, s5_d, s5_w_glu, fox_b_f, w_out_even, w_in_odd, pool_w, pool_scale, sgu_ln_g, sgu_ln_b, sgu_w_s, sgu_b_s, w_out_odd, mlp_w1, mlp_w2, loss_target, m_mix_pre_g, m_mix_post_g, m_mlp_pre_g, m_mlp_post_g, m_w_in_even, m_s5_lam_re, m_s5_lam_im, m_s5_log_dt, m_s5_b_re, m_s5_b_im, m_s5_c_re, m_s5_c_im, m_s5_d, m_s5_w_glu, m_fox_b_f, m_w_out_even, m_w_in_odd, m_pool_w, m_pool_scale, m_sgu_ln_g, m_sgu_ln_b, m_sgu_w_s, m_sgu_b_s, m_w_out_odd, m_mlp_w1, m_mlp_w2, v_mix_pre_g, v_mix_post_g, v_mlp_pre_g, v_mlp_post_g, v_w_in_even, v_s5_lam_re, v_s5_lam_im, v_s5_log_dt, v_s5_b_re, v_s5_b_im, v_s5_c_re, v_s5_c_im, v_s5_d, v_s5_w_glu, v_fox_b_f, v_w_out_even, v_w_in_odd, v_pool_w, v_pool_scale, v_sgu_ln_g, v_sgu_ln_b, v_sgu_w_s, v_sgu_b_s, v_w_out_odd, v_mlp_w1, v_mlp_w2):
    names = [n for n, _ in SMALL_SEGS] + [n for n, _, _ in BIG_SEGS] + list(SHARDED_SMALL)
    env = dict(locals())
    W = {n: env[n] for n in names}
    M = {n: env["m_" + n] for n in names}
    V = {n: env["v_" + n] for n in names}

    def shard(n):
        return W[n].reshape(-1, W[n].shape[-1]).astype(WIRE_DTYPE)

    small = jnp.pad(jnp.concatenate([W[n] for n in SHARDED_SMALL]), ((0, SUBLANES - len(SHARDED_SMALL)), (0, 0)))
    loss8, dx0, halves, local_small = _local_step(x[0], loss_target[0], {n: W[n] for n, _ in SMALL_SEGS},
                                                  [shard(n) for n in EARLY_NAMES], [shard(n) for n in LATE_NAMES], small)
    return _reduce_and_update(W, M, V, loss8, dx0, halves, local_small)


def _reduce_to_my_half(gs, names, tag, carry_swap=None, carry_ici=None):
    cx, cy, cc = _coords()
    c_idx = cc.reshape(1).astype(jnp.int32)
    jc_idx = jnp.stack([2 * cx + cy, cc]).astype(jnp.int32)
    swap = swap_halves_exchange(gs)
    from_sibling = carry_swap(swap) if carry_swap else run_exchange(swap, name=f"{tag}_to_sibling")
    sums = [add_sibling_half(g, l, c_idx, name=f"{tag}_chip_sum_{n}") for n, g, l in zip(names, gs, from_sibling)]
    send = chip_partials_exchange([pb for _, pb in sums])
    from_chips = carry_ici(send) if carry_ici else run_exchange(send, name=f"{tag}_to_chips")
    return [add_chip_partials(pf, r, jc_idx, name=f"{tag}_sum_{n}") for n, (pf, _), r in zip(names, sums, from_chips)]


def _local_step(x0, target, P, early_shards, late_shards, small_shard):
    T = x0.shape[0]
    mix_pre_g, mix_post_g, mlp_pre_g, mlp_post_g = P["mix_pre_g"], P["mix_post_g"], P["mlp_pre_g"], P["mlp_post_g"]
    s5_lam_re, s5_lam_im, s5_log_dt = P["s5_lam_re"], P["s5_lam_im"], P["s5_log_dt"]
    s5_b_re, s5_b_im, s5_c_re, s5_c_im, s5_d = P["s5_b_re"], P["s5_b_im"], P["s5_c_re"], P["s5_c_im"], P["s5_d"]
    fox_b_f, pool_w, sgu_w_s, sgu_b_s = P["fox_b_f"], P["pool_w"], P["sgu_w_s"], P["sgu_b_s"]

    def gain(a, l):
        return a[l][None, :]

    lr = s5_lam_re[0].reshape(1, S5_LANES)
    li = s5_lam_im[0].reshape(1, S5_LANES)
    ldt = jnp.repeat(s5_log_dt[0], S5_STATE).reshape(1, S5_LANES)
    btr = s5_b_re[0].reshape(S5_LANES, S5_GROUP).T
    bti = s5_b_im[0].reshape(S5_LANES, S5_GROUP).T
    tf_re, tf_im, tb_re, tb_im, bbt_re, bbt_im = s5_disc_fwd(lr, li, ldt, btr, bti, name="s5_disc")
    same_group = (jnp.arange(S5_WIDTH)[:, None] // S5_GROUP) == (jnp.arange(S5_LANES)[None, :] // S5_STATE)
    b_bd = s5_interleave(jnp.where(same_group, jnp.tile(bbt_re, (S5_GROUPS, 1)), 0.0),
                         jnp.where(same_group, jnp.tile(bbt_im, (S5_GROUPS, 1)), 0.0), axis=1)
    cr2 = jnp.transpose(s5_c_re[0], (0, 2, 1)).reshape(S5_LANES, S5_GROUP)
    ci2 = jnp.transpose(s5_c_im[0], (0, 2, 1)).reshape(S5_LANES, S5_GROUP)
    c_bd = s5_interleave(jnp.where(same_group.T, jnp.tile(cr2, (1, S5_GROUPS)), 0.0),
                         -jnp.where(same_group.T, jnp.tile(ci2, (1, S5_GROUPS)), 0.0), axis=0)
    bf_pad = jnp.pad(fox_b_f, ((0, 0), (0, LANES - FOX_HEADS)))

    h1, early = rms_fwd(x0, gain(mix_pre_g, 0), allgather_ici_exchange(early_shards), name="l0_pre_norm")
    early = dict(zip(EARLY_NAMES, allgather_forward(early, name="allgather_early_weights")))
    w_in_e = jnp.pad(_cols_from_chips(early["w_in_even"]), ((0, 0), (0, EVEN_IN_PAD - EVEN_IN)))
    w_glu = early["s5_w_glu"].reshape(S5_WIDTH, S5_WIDTH)
    z = matmul(h1, w_in_e, name="l0_in_proj")
    s5_tiles = dict(tm=_pick(T, S5_NB), exact_tiles=True)
    xs = s5_scan(z, b_bd, tf_re, tf_im, reverse=False, name="s5_scan_fwd")
    yc = matmul(xs, c_bd, mnk=(T, S5_WIDTH, 2 * S5_NB), tn=S5_CB, a_spec=_lanes_of_chan, b_spec=_s5_c_block,
                name="s5_cx", **s5_tiles)
    yl, yg = s5_out_fwd(yc, z, s5_d, name="s5_out")
    gl = matmul(yg, w_glu, name="s5_glu_proj")
    ycat = glu_fwd(yg, gl, out_cols=D_MODEL, name="s5_glu")
    fgate = fox_gate_fwd(z, bf_pad, fl_col=FL_TILE, name="fox_gate")
    f_col = _pairs_col(fgate, T)
    f_row = _col_to_row(f_col, T)
    (ycat, lse_col), late = fox_fwd(z, f_col, f_row, ycat, allgather_ici_exchange(late_shards, [small_shard]),
                                    name="fox_fwd")
    small_all = late[-1]
    pool_scale_f, ln_g_f, ln_b_f = (small_all[:, i, :].reshape(1, N_CHIPS * LANES) for i in range(len(SHARDED_SMALL)))
    late = dict(zip(LATE_NAMES, allgather_forward(late[:-1], name="allgather_late_weights")))
    w_in_o = _cols_from_chips(late["w_in_odd"])
    w_in_o = jnp.concatenate([w_in_o[:, S5_WIDTH:], w_in_o[:, :S5_WIDTH]], axis=1)
    w_out_e = late["w_out_even"].reshape(D_MODEL, D_MODEL)
    w_out_o = late["w_out_odd"].reshape(D_MODEL, D_MODEL)
    g1, g2 = late["mlp_w1"], late["mlp_w2"]
    mo = matmul(ycat, w_out_e, name="l0_out_proj")
    x1, h2 = res_norm_fwd(x0, mo, gain(mix_post_g, 0), gain(mlp_pre_g, 0), name="l0_post_mlp0_pre_norm")
    m0, mlp0 = _mlp_fwd(h2, g1, g2, 0, "mlp0")

    x2, h3 = res_norm_fwd(x1, m0, gain(mlp_post_g, 0), gain(mix_pre_g, 1), name="mlp0_post_l1_pre_norm")
    z2 = matmul(h3, w_in_o, name="l1_in_proj")
    pooled = pool_window(z2, adjoint=False, in_col=POOL_COL, out_dtype=MXU_DTYPE, name="pool_fwd")
    pw_bd = _block_diag(pool_w[0])
    pw = matmul(pooled, pw_bd, name="pool_proj")
    ycat2 = colscale_fwd(pw, pool_scale_f, out_cols=D_MODEL, name="pool_scale")
    causal = jnp.tril(jnp.ones((CHUNK, CHUNK), dtype=bool))
    wsm = jnp.where(causal[None], sgu_w_s[0], 0.0)
    wsmt = jnp.transpose(wsm, (0, 2, 1))
    bst = sgu_b_s[0].T
    ycat2 = sgu_fwd(z2, ln_g_f, ln_b_f, wsm, bst, ycat2, name="sgu_fwd")
    mo2 = matmul(ycat2, w_out_o, name="l1_out_proj")
    x3, h4 = res_norm_fwd(x2, mo2, gain(mix_post_g, 1), gain(mlp_pre_g, 1), name="l1_post_mlp1_pre_norm")
    m1, mlp1 = _mlp_fwd(h4, g1, g2, 1, "mlp1")
    loss8, dx4 = res_norm_loss(x3, m1, gain(mlp_post_g, 1), target, name="mlp1_post_norm_loss")

    dm1, dg_mlp_post1 = rms_bwd(m1, gain(mlp_post_g, 1), dx4, None, name="mlp1_post_norm_bwd")
    dh4, dg1, dg2 = _mlp_bwd(mlp1, dm1, g1, g2, 1, None, None, "mlp1")
    dx3, dmo2, dg_mlp_pre1, dg_mix_post1 = norm_res_bwd(x3, gain(mlp_pre_g, 1), dh4, dx4, mo2, gain(mix_post_g, 1),
                                                        name="mlp1_pre_l1_post_norm_bwd")
    dycat2 = matmul(dmo2, w_out_o, tb=True, name="l1_out_proj_dx")
    dw_out_o = matmul(ycat2, dmo2, ta=True, name="l1_out_proj_dw")
    dpw, dpool_scale = colscale_bwd(pw, pool_scale_f, dycat2, name="pool_scale_bwd")
    dpooled = matmul(dpw, pw_bd, tb=True, name="pool_proj_dx")
    dpw_bd = matmul(pooled, dpw, ta=True, name="pool_proj_dw")
    dz2, dln_g, dln_b, dws, dbst = sgu_bwd(z2, ln_g_f, ln_b_f, wsm, wsmt, bst, dycat2, out_cols=3 * S5_WIDTH,
                                           name="sgu_bwd")
    dz2 = pool_window(dpooled, adjoint=True, into=dz2, out_col=POOL_COL, name="pool_bwd")
    dh3 = matmul(dz2, w_in_o, tb=True, name="l1_in_proj_dx")
    dw_in_o = matmul(h3, dz2, ta=True, name="l1_in_proj_dw")
    dw_in_o = jnp.concatenate([dw_in_o[:, 2 * S5_WIDTH:], dw_in_o[:, :2 * S5_WIDTH]], axis=1)
    dx2, dm0, dg_mix_pre1, dg_mlp_post0 = norm_res_bwd(x2, gain(mix_pre_g, 1), dh3, dx3, m0, gain(mlp_post_g, 0),
                                                       name="l1_pre_mlp0_post_norm_bwd")

    dh2, dg1, dg2 = _mlp_bwd(mlp0, dm0, g1, g2, 0, dg1, dg2, "mlp0")
    dx1, dmo, dg_mlp_pre0, dg_mix_post0 = norm_res_bwd(x1, gain(mlp_pre_g, 0), dh2, dx2, mo, gain(mix_post_g, 0),
                                                       name="mlp0_pre_l0_post_norm_bwd")
    dycat = matmul(dmo, w_out_e, tb=True, name="l0_out_proj_dx")
    dw_out_e = matmul(ycat, dmo, ta=True, name="l0_out_proj_dw")
    dyg_a, dgl = glu_bwd(yg, gl, dycat, name="s5_glu_bwd")
    dyg_b = matmul(dgl, w_glu, tb=True, name="s5_glu_proj_dx")
    dw_glu = matmul(yg, dgl, ta=True, name="s5_glu_proj_dw")
    dyl, du_skip, dd = s5_out_bwd(yl, z, s5_d, dyg_a, dyg_b, name="s5_out_bwd")
    dc_blocks = matmul(xs, dyl, ta=True, mnk=(2 * S5_LANES, S5_CB, T), tm=S5_NB, tn=S5_CB, b_spec=_chan_cols_of_i,
                       exact_tiles=True, name="s5_cx_dw")
    early_grads = {"s5_w_glu": dw_glu.reshape(N_CHIPS, -1, S5_WIDTH), "w_out_even": dw_out_e.reshape(N_CHIPS, -1, D_MODEL),
                   "w_in_odd": _chips_from_cols(dw_in_o), "w_out_odd": dw_out_o.reshape(N_CHIPS, -1, D_MODEL),
                   "mlp_w1": dg1, "mlp_w2": dg2}
    got = {}

    def reverse_scan(exchange):
        (got["lam"], got["dab_re"], got["dab_im"]), bufs = s5_scan(dyl, c_bd, tb_re, tb_im, reverse=True, states=xs,
                                                                   hosted=exchange, name="s5_scan_bwd")
        return bufs

    def attention_bwd(exchange):
        dd_col = fox_dd(ycat, dycat, name="fox_dd")
        (got["dk"], got["dv"], got["dfk"], got["dqt"], got["dfq"]), bufs = fox_bwd(
            z, dycat, f_col, f_row, _col_to_row(lse_col, T), _col_to_row(dd_col, T), exchange, name="fox_bwd")
        return bufs

    halves = _reduce_to_my_half([early_grads[n] for n in REDUCED_EARLY], REDUCED_EARLY, "early_grads",
                                reverse_scan, attention_bwd)
    lam, dab_re, dab_im, dk, dv = got["lam"], got["dab_re"], got["dab_im"], got["dk"], got["dv"]
    db_blocks = matmul(z, lam, ta=True, mnk=(S5_CB, 2 * S5_LANES, T), tm=S5_CB, tn=S5_NB, a_spec=_chan_rows_t,
                       exact_tiles=True, name="s5_bu_dw")
    du_b = matmul(lam, b_bd, tb=True, mnk=(T, S5_WIDTH, 2 * S5_NB), tn=S5_CB, a_spec=_lanes_of_chan,
                  b_spec=_s5_b_block_t, name="s5_bu_dx", **s5_tiles)
    du = add2(du_skip, du_b, name="s5_du")
    dq = jnp.transpose(got["dqt"], (1, 3, 0, 2)).reshape(T, FOX_WIDTH) * (FOX_HEAD_DIM ** -0.5)
    dfl, dbf = fox_gate_bwd(z, bf_pad, _pairs_to_lanes(got["dfk"], T), _pairs_to_lanes(_row_to_col(got["dfq"], T), T),
                            fl_col=FL_TILE, name="fox_gate_bwd")
    dz = jnp.concatenate([du, dq, dk, dv, dfl], axis=1).astype(MXU_DTYPE)
    dw_in_e = matmul(h1, dz, ta=True, name="l0_in_proj_dw")[:, :EVEN_IN]

    def in_proj_dx(exchange):
        got["dh1"], bufs = matmul(dz, w_in_e, tb=True, hosted=exchange, name="l0_in_proj_dx")
        return bufs

    def pre_norm_bwd(exchange):
        (got["dx0"], got["dg_mix_pre0"]), bufs = rms_bwd(x0, gain(mix_pre_g, 0), got["dh1"], dx1, hosted=exchange,
                                                         name="l0_pre_norm_bwd")
        return bufs

    halves = halves + _reduce_to_my_half([_chips_from_cols(dw_in_e)], ["w_in_even"], "late_grads", in_proj_dx, pre_norm_bwd)
    dx0, dg_mix_pre0 = got["dx0"], got["dg_mix_pre0"]

    groups_per_block = S5_CB // S5_GROUP
    own_group = (jnp.arange(S5_CB)[:, None] // S5_GROUP) == ((jnp.arange(S5_LANES)[None, :] // S5_STATE) % groups_per_block)
    db_re, db_im = s5_deinterleave(db_blocks, axis=1)
    dbbt_re = jnp.where(own_group, db_re, 0.0).reshape(groups_per_block, S5_GROUP, S5_LANES).sum(0)
    dbbt_im = jnp.where(own_group, db_im, 0.0).reshape(groups_per_block, S5_GROUP, S5_LANES).sum(0)
    dlr, dli, dldt8, dbtr, dbti = s5_disc_bwd(lr, li, ldt, btr, bti, dab_re, dab_im, dbbt_re, dbbt_im, name="s5_disc_bwd")
    dc_re, dc_im = s5_deinterleave(dc_blocks, axis=0)
    dcr2 = jnp.where(own_group.T, dc_re, 0.0).reshape(S5_LANES, groups_per_block, S5_GROUP).sum(1)
    dci2 = -jnp.where(own_group.T, dc_im, 0.0).reshape(S5_LANES, groups_per_block, S5_GROUP).sum(1)

    def c_layout(a):
        return jnp.transpose(a.reshape(S5_GROUPS, S5_STATE, S5_GROUP), (0, 2, 1))[None]

    def b_layout(a):
        return a.T.reshape(1, S5_GROUPS, S5_STATE, S5_GROUP)

    local_small = {
        "mix_pre_g": jnp.concatenate([dg_mix_pre0, dg_mix_pre1]), "mix_post_g": jnp.concatenate([dg_mix_post0, dg_mix_post1]),
        "mlp_pre_g": jnp.concatenate([dg_mlp_pre0, dg_mlp_pre1]), "mlp_post_g": jnp.concatenate([dg_mlp_post0, dg_mlp_post1]),
        "s5_lam_re": dlr.reshape(1, S5_GROUPS, S5_STATE), "s5_lam_im": dli.reshape(1, S5_GROUPS, S5_STATE),
        "s5_log_dt": dldt8[0:1, 0:S5_GROUPS],
        "s5_b_re": b_layout(dbtr), "s5_b_im": b_layout(dbti), "s5_c_re": c_layout(dcr2), "s5_c_im": c_layout(dci2),
        "s5_d": dd, "fox_b_f": dbf[:, 0:FOX_HEADS],
        "pool_w": _diag_blocks(dpw_bd, len(POOL_WINDOWS))[None],
        "sgu_w_s": jnp.where(causal[None], dws, 0.0)[None], "sgu_b_s": dbst.T[None],
        "pool_scale": dpool_scale, "sgu_ln_g": dln_g, "sgu_ln_b": dln_b,
    }
    return loss8, dx0, dict(zip(REDUCED_EARLY + ("w_in_even",), halves)), local_small


def _reduce_and_update(W, M, V, loss8, dx0, halves, local_small):
    cx, cy, cc = _coords()
    chip = 2 * cx + cy

    summed = dict(local_small, loss=loss8[0:1, 0:1])
    vec = _pack_vec(summed, REDUCED_SEGS, N_DEV * SUBLANES)
    piece = vec.shape[0] // N_DEV
    everyone = allreduce_pieces(vec.reshape(N_DEV, piece, LANES), name="small_grads_allreduce")
    G = _unpack_vec(everyone, REDUCED_SEGS)
    loss = G["loss"].reshape(())
    for n in SHARDED_SMALL:
        G[n] = lax.dynamic_slice_in_dim(G[n], chip * LANES, LANES, axis=1)

    reduced = join_sibling_halves([halves[n] for n in BIG_NAMES], name="big_grads_join")
    for n, r in zip(BIG_NAMES, reduced):
        G[n] = r.reshape(W[n].shape)

    def two_d(a):
        return a.reshape(-1, a.shape[-1])

    delta, new_m, new_v = {}, {}, {}
    for n in BIG_NAMES:
        d_, m_, v_ = adamw(two_d(W[n]), two_d(G[n]), two_d(M[n]), two_d(V[n]), name=f"adamw_{n}")
        delta[n], new_m[n], new_v[n] = (t.reshape(W[n].shape) for t in (d_, m_, v_))
    packed = [_pack_vec(src, SMALL_SEGS, SUBLANES) for src in (W, G, M, V)]
    outs = adamw(*packed, name="adamw_replicated")
    for dst, t in zip((delta, new_m, new_v), outs):
        dst.update(_unpack_vec(t, SMALL_SEGS))
    sharded_segs = tuple((n, (1, LANES)) for n in SHARDED_SMALL)
    packed = [_pack_vec(src, sharded_segs, 1) for src in (W, G, M, V)]
    outs = adamw(*packed, name="adamw_sharded_vectors")
    for dst, t in zip((delta, new_m, new_v), outs):
        dst.update(_unpack_vec(t, sharded_segs))

    order = ["mix_pre_g", "mix_post_g", "mlp_pre_g", "mlp_post_g", "w_in_even", "s5_lam_re", "s5_lam_im", "s5_log_dt",
             "s5_b_re", "s5_b_im", "s5_c_re", "s5_c_im", "s5_d", "s5_w_glu", "fox_b_f", "w_out_even", "w_in_odd",
             "pool_w", "pool_scale", "sgu_ln_g", "sgu_ln_b", "sgu_w_s", "sgu_b_s", "w_out_odd", "mlp_w1", "mlp_w2"]
    return (loss, dx0[None], *[G[n] for n in order], *[delta[n] for n in order],
            *[new_m[n] for n in order], *[new_v[n] for n in order])
```
